```python
import math
import jax, jax.numpy as jnp
from jax import lax
import numpy as np

D_MODEL = 1024
BATCH = 16
SEQ = 2048
DEPTH = 1

MIX_WIDTH = D_MODEL
POOL_WIDTH = MIX_WIDTH // 2
POOL_GROUPS = 4
POOL_GROUP_DIM = POOL_WIDTH // POOL_GROUPS
POOL_WINDOWS = (2, 4, 8, 16)
ATTN_WIDTH = MIX_WIDTH - POOL_WIDTH
HEAD_DIM = 64
N_HEADS = ATTN_WIDTH // HEAD_DIM
D_FF = 2816
Q_BLOCK = 128
IN_COLS = POOL_WIDTH + 3 * ATTN_WIDTH + N_HEADS
EPS = 1e-6

kernel_name = "hymba_pool_fox_macaron_block"


def rmsnorm(x, g):
    xf = x.astype(jnp.float32)
    y = xf * lax.rsqrt(jnp.mean(xf * xf, axis=-1, keepdims=True) + EPS)
    return (y * g.astype(jnp.float32)).astype(x.dtype)


def swiglu(h, w_gate, w_up, w_down):
    return (jax.nn.silu(h @ w_gate) * (h @ w_up)) @ w_down


def causal_window_mean(v, w):
    B, S, C = v.shape
    vf = v.astype(jnp.float32)
    cs = jnp.cumsum(vf, axis=1)
    shifted = jnp.concatenate([jnp.zeros((B, w, C), jnp.float32), cs[:, : S - w]], axis=1)
    count = jnp.minimum(jnp.arange(1, S + 1, dtype=jnp.float32), float(w))
    return ((cs - shifted) / count[None, :, None]).astype(v.dtype)


def pool_mixer(pv, pool_w, pool_scale):
    B, S, _ = pv.shape
    groups = pv.reshape(B, S, POOL_GROUPS, POOL_GROUP_DIM)
    pooled = jnp.stack(
        [causal_window_mean(groups[:, :, g], POOL_WINDOWS[g]) for g in range(POOL_GROUPS)], axis=2
    ) - groups
    mixed = jnp.einsum("bsgc,gcd->bsgd", pooled, pool_w)
    return mixed.reshape(B, S, POOL_WIDTH) * pool_scale


def forgetting_attention(q, k, v, f_logit, b_forget, q_norm, k_norm):
    B, S, H, Dh = q.shape
    q = rmsnorm(q, q_norm).transpose(0, 2, 1, 3)
    k = rmsnorm(k, k_norm).transpose(0, 2, 1, 3)
    v = v.transpose(0, 2, 1, 3)
    log_f = jax.nn.log_sigmoid((f_logit + b_forget).astype(jnp.float32))
    F = jnp.cumsum(log_f, axis=1).transpose(0, 2, 1)
    scale = 1.0 / math.sqrt(Dh)
    outs = []
    for i in range(S // Q_BLOCK):
        q0, end = i * Q_BLOCK, (i + 1) * Q_BLOCK
        qb = q[:, :, q0:end]
        kb, vb = k[:, :, :end], v[:, :, :end]
        logits = jnp.einsum("bhqd,bhkd->bhqk", qb, kb).astype(jnp.float32) * scale
        logits = logits + F[:, :, q0:end, None] - F[:, :, None, :end]
        q_pos = jnp.arange(q0, end)[:, None]
        k_pos = jnp.arange(end)[None, :]
        logits = jnp.where(q_pos >= k_pos, logits, -jnp.inf)
        p = jax.nn.softmax(logits, axis=-1).astype(vb.dtype)
        outs.append(jnp.einsum("bhqk,bhkd->bhqd", p, vb))
    o = jnp.concatenate(outs, axis=2)
    return o.transpose(0, 2, 1, 3).reshape(B, S, H * Dh)


def _fwd_setup_inputs(seed: int = 0) -> dict:
    key = jax.random.key(seed)
    ks = jax.random.split(key, 24)
    f32 = jnp.float32

    def nrm(k, shape, fan_in):
        return jax.random.normal(k, shape, f32) * fan_in ** -0.5

    def gain(k, shape):
        return 1.0 + 0.02 * jax.random.normal(k, shape, f32)

    return {
        "x": jax.random.normal(ks[0], (BATCH, SEQ, D_MODEL), f32),
        "ffn1_norm": gain(ks[1], (D_MODEL,)),
        "ffn1_w_gate": nrm(ks[2], (D_MODEL, D_FF), D_MODEL),
        "ffn1_w_up": nrm(ks[3], (D_MODEL, D_FF), D_MODEL),
        "ffn1_w_down": nrm(ks[4], (D_FF, D_MODEL), D_FF),
        "mix_norm": gain(ks[5], (D_MODEL,)),
        "w_in": nrm(ks[6], (D_MODEL, IN_COLS), D_MODEL),
        "b_forget": jax.random.uniform(ks[7], (N_HEADS,), f32, 1.0, 4.0),
        "pool_w": nrm(ks[8], (POOL_GROUPS, POOL_GROUP_DIM, POOL_GROUP_DIM), POOL_GROUP_DIM),
        "pool_scale": gain(ks[9], (POOL_WIDTH,)),
        "q_norm": gain(ks[10], (HEAD_DIM,)),
        "k_norm": gain(ks[11], (HEAD_DIM,)),
        "out_norm_pool": gain(ks[12], (POOL_WIDTH,)),
        "out_norm_attn": gain(ks[13], (ATTN_WIDTH,)),
        "w_out": nrm(ks[14], (MIX_WIDTH, D_MODEL), MIX_WIDTH),
        "ffn2_norm": gain(ks[15], (D_MODEL,)),
        "ffn2_w_gate": nrm(ks[16], (D_MODEL, D_FF), D_MODEL),
        "ffn2_w_up": nrm(ks[17], (D_MODEL, D_FF), D_MODEL),
        "ffn2_w_down": nrm(ks[18], (D_FF, D_MODEL), D_FF),
    }


def _fwd_reference(x, ffn1_norm, ffn1_w_gate, ffn1_w_up, ffn1_w_down, mix_norm, w_in, b_forget,
              pool_w, pool_scale, q_norm, k_norm, out_norm_pool, out_norm_attn, w_out,
              ffn2_norm, ffn2_w_gate, ffn2_w_up, ffn2_w_down):
    B, S, _ = x.shape
    for _layer in range(DEPTH):
        x = x + 0.5 * swiglu(rmsnorm(x, ffn1_norm), ffn1_w_gate, ffn1_w_up, ffn1_w_down)

        h = rmsnorm(x, mix_norm) @ w_in
        c0 = POOL_WIDTH
        pv = h[..., :c0]
        q = h[..., c0:c0 + ATTN_WIDTH].reshape(B, S, N_HEADS, HEAD_DIM)
        k = h[..., c0 + ATTN_WIDTH:c0 + 2 * ATTN_WIDTH].reshape(B, S, N_HEADS, HEAD_DIM)
        v = h[..., c0 + 2 * ATTN_WIDTH:c0 + 3 * ATTN_WIDTH].reshape(B, S, N_HEADS, HEAD_DIM)
        f_logit = h[..., c0 + 3 * ATTN_WIDTH:]

        y_pool = rmsnorm(pool_mixer(pv, pool_w, pool_scale), out_norm_pool)
        y_attn = rmsnorm(forgetting_attention(q, k, v, f_logit, b_forget, q_norm, k_norm), out_norm_attn)
        x = x + jnp.concatenate([y_pool, y_attn], axis=-1) @ w_out

        x = x + 0.5 * swiglu(rmsnorm(x, ffn2_norm), ffn2_w_gate, ffn2_w_up, ffn2_w_down)
    return x


import jax as _jax
import jax.numpy as _jnp

TWIN_FORMAT = 'train_step'
FWD_PARAMS = ['x', 'ffn1_norm', 'ffn1_w_gate', 'ffn1_w_up', 'ffn1_w_down', 'mix_norm', 'w_in', 'b_forget', 'pool_w', 'pool_scale', 'q_norm', 'k_norm', 'out_norm_pool', 'out_norm_attn', 'w_out', 'ffn2_norm', 'ffn2_w_gate', 'ffn2_w_up', 'ffn2_w_down']
TWIN_WEIGHTS = ['ffn1_norm', 'ffn1_w_gate', 'ffn1_w_up', 'ffn1_w_down', 'mix_norm', 'w_in', 'b_forget', 'pool_w', 'pool_scale', 'q_norm', 'k_norm', 'out_norm_pool', 'out_norm_attn', 'w_out', 'ffn2_norm', 'ffn2_w_gate', 'ffn2_w_up', 'ffn2_w_down']
TWIN_DIFF_INPUT = 'x'
TWIN_INPUTS = ['x', 'ffn1_norm', 'ffn1_w_gate', 'ffn1_w_up', 'ffn1_w_down', 'mix_norm', 'w_in', 'b_forget', 'pool_w', 'pool_scale', 'q_norm', 'k_norm', 'out_norm_pool', 'out_norm_attn', 'w_out', 'ffn2_norm', 'ffn2_w_gate', 'ffn2_w_up', 'ffn2_w_down', 'loss_target', 'm_ffn1_norm', 'm_ffn1_w_gate', 'm_ffn1_w_up', 'm_ffn1_w_down', 'm_mix_norm', 'm_w_in', 'm_b_forget', 'm_pool_w', 'm_pool_scale', 'm_q_norm', 'm_k_norm', 'm_out_norm_pool', 'm_out_norm_attn', 'm_w_out', 'm_ffn2_norm', 'm_ffn2_w_gate', 'm_ffn2_w_up', 'm_ffn2_w_down', 'v_ffn1_norm', 'v_ffn1_w_gate', 'v_ffn1_w_up', 'v_ffn1_w_down', 'v_mix_norm', 'v_w_in', 'v_b_forget', 'v_pool_w', 'v_pool_scale', 'v_q_norm', 'v_k_norm', 'v_out_norm_pool', 'v_out_norm_attn', 'v_w_out', 'v_ffn2_norm', 'v_ffn2_w_gate', 'v_ffn2_w_up', 'v_ffn2_w_down']
TWIN_OUTPUTS = ['loss', 'grad_x', 'grad_ffn1_norm', 'grad_ffn1_w_gate', 'grad_ffn1_w_up', 'grad_ffn1_w_down', 'grad_mix_norm', 'grad_w_in', 'grad_b_forget', 'grad_pool_w', 'grad_pool_scale', 'grad_q_norm', 'grad_k_norm', 'grad_out_norm_pool', 'grad_out_norm_attn', 'grad_w_out', 'grad_ffn2_norm', 'grad_ffn2_w_gate', 'grad_ffn2_w_up', 'grad_ffn2_w_down', 'delta_ffn1_norm', 'delta_ffn1_w_gate', 'delta_ffn1_w_up', 'delta_ffn1_w_down', 'delta_mix_norm', 'delta_w_in', 'delta_b_forget', 'delta_pool_w', 'delta_pool_scale', 'delta_q_norm', 'delta_k_norm', 'delta_out_norm_pool', 'delta_out_norm_attn', 'delta_w_out', 'delta_ffn2_norm', 'delta_ffn2_w_gate', 'delta_ffn2_w_up', 'delta_ffn2_w_down', 'new_m_ffn1_norm', 'new_m_ffn1_w_gate', 'new_m_ffn1_w_up', 'new_m_ffn1_w_down', 'new_m_mix_norm', 'new_m_w_in', 'new_m_b_forget', 'new_m_pool_w', 'new_m_pool_scale', 'new_m_q_norm', 'new_m_k_norm', 'new_m_out_norm_pool', 'new_m_out_norm_attn', 'new_m_w_out', 'new_m_ffn2_norm', 'new_m_ffn2_w_gate', 'new_m_ffn2_w_up', 'new_m_ffn2_w_down', 'new_v_ffn1_norm', 'new_v_ffn1_w_gate', 'new_v_ffn1_w_up', 'new_v_ffn1_w_down', 'new_v_mix_norm', 'new_v_w_in', 'new_v_b_forget', 'new_v_pool_w', 'new_v_pool_scale', 'new_v_q_norm', 'new_v_k_norm', 'new_v_out_norm_pool', 'new_v_out_norm_attn', 'new_v_w_out', 'new_v_ffn2_norm', 'new_v_ffn2_w_gate', 'new_v_ffn2_w_up', 'new_v_ffn2_w_down']
TWIN_LEAF_KINDS = {'loss': 'loss', 'grad_x': 'grad_x', 'grad_ffn1_norm': 'grad_w', 'grad_ffn1_w_gate': 'grad_w', 'grad_ffn1_w_up': 'grad_w', 'grad_ffn1_w_down': 'grad_w', 'grad_mix_norm': 'grad_w', 'grad_w_in': 'grad_w', 'grad_b_forget': 'grad_w', 'grad_pool_w': 'grad_w', 'grad_pool_scale': 'grad_w', 'grad_q_norm': 'grad_w', 'grad_k_norm': 'grad_w', 'grad_out_norm_pool': 'grad_w', 'grad_out_norm_attn': 'grad_w', 'grad_w_out': 'grad_w', 'grad_ffn2_norm': 'grad_w', 'grad_ffn2_w_gate': 'grad_w', 'grad_ffn2_w_up': 'grad_w', 'grad_ffn2_w_down': 'grad_w', 'delta_ffn1_norm': 'delta_w', 'delta_ffn1_w_gate': 'delta_w', 'delta_ffn1_w_up': 'delta_w', 'delta_ffn1_w_down': 'delta_w', 'delta_mix_norm': 'delta_w', 'delta_w_in': 'delta_w', 'delta_b_forget': 'delta_w', 'delta_pool_w': 'delta_w', 'delta_pool_scale': 'delta_w', 'delta_q_norm': 'delta_w', 'delta_k_norm': 'delta_w', 'delta_out_norm_pool': 'delta_w', 'delta_out_norm_attn': 'delta_w', 'delta_w_out': 'delta_w', 'delta_ffn2_norm': 'delta_w', 'delta_ffn2_w_gate': 'delta_w', 'delta_ffn2_w_up': 'delta_w', 'delta_ffn2_w_down': 'delta_w', 'new_m_ffn1_norm': 'new_m', 'new_m_ffn1_w_gate': 'new_m', 'new_m_ffn1_w_up': 'new_m', 'new_m_ffn1_w_down': 'new_m', 'new_m_mix_norm': 'new_m', 'new_m_w_in': 'new_m', 'new_m_b_forget': 'new_m', 'new_m_pool_w': 'new_m', 'new_m_pool_scale': 'new_m', 'new_m_q_norm': 'new_m', 'new_m_k_norm': 'new_m', 'new_m_out_norm_pool': 'new_m', 'new_m_out_norm_attn': 'new_m', 'new_m_w_out': 'new_m', 'new_m_ffn2_norm': 'new_m', 'new_m_ffn2_w_gate': 'new_m', 'new_m_ffn2_w_up': 'new_m', 'new_m_ffn2_w_down': 'new_m', 'new_v_ffn1_norm': 'new_v', 'new_v_ffn1_w_gate': 'new_v', 'new_v_ffn1_w_up': 'new_v', 'new_v_ffn1_w_down': 'new_v', 'new_v_mix_norm': 'new_v', 'new_v_w_in': 'new_v', 'new_v_b_forget': 'new_v', 'new_v_pool_w': 'new_v', 'new_v_pool_scale': 'new_v', 'new_v_q_norm': 'new_v', 'new_v_k_norm': 'new_v', 'new_v_out_norm_pool': 'new_v', 'new_v_out_norm_attn': 'new_v', 'new_v_w_out': 'new_v', 'new_v_ffn2_norm': 'new_v', 'new_v_ffn2_w_gate': 'new_v', 'new_v_ffn2_w_up': 'new_v', 'new_v_ffn2_w_down': 'new_v'}


def _forward(args):
    return _fwd_reference(*[args[k] for k in FWD_PARAMS])


def _output_shape():
    out = _jax.eval_shape(lambda: _forward(_fwd_setup_inputs(0)))
    return out.shape, out.dtype

N_MICROBATCH = 1
ADAM_LR = 0.001
ADAM_B1 = 0.9
ADAM_B2 = 0.999
ADAM_EPS = 1e-08
ADAM_WD = 0.01
ADAM_STEP = 10
PER_EXAMPLE_BATCH_AXIS = {'x': 0, 'loss_target': 0}
SHARED_INPUTS = []
_WEIGHT_DTYPES = {'ffn1_norm': _jnp.float32, 'ffn1_w_gate': _jnp.float32, 'ffn1_w_up': _jnp.float32, 'ffn1_w_down': _jnp.float32, 'mix_norm': _jnp.float32, 'w_in': _jnp.float32, 'b_forget': _jnp.float32, 'pool_w': _jnp.float32, 'pool_scale': _jnp.float32, 'q_norm': _jnp.float32, 'k_norm': _jnp.float32, 'out_norm_pool': _jnp.float32, 'out_norm_attn': _jnp.float32, 'w_out': _jnp.float32, 'ffn2_norm': _jnp.float32, 'ffn2_w_gate': _jnp.float32, 'ffn2_w_up': _jnp.float32, 'ffn2_w_down': _jnp.float32}
MOMENT_SCALE = {'ffn1_norm': 6.213019e+00, 'ffn1_w_gate': 1.009186e-01, 'ffn1_w_up': 1.092816e-01, 'ffn1_w_down': 1.815877e-01, 'mix_norm': 7.645516e-01, 'w_in': 4.955763e-01, 'b_forget': 7.646352e+00, 'pool_w': 1.185735e+00, 'pool_scale': 2.165540e+00, 'q_norm': 7.920406e-01, 'k_norm': 7.946430e-01, 'out_norm_pool': 3.304823e+01, 'out_norm_attn': 3.483089e+01, 'w_out': 1.206391e+00, 'ffn2_norm': 6.159452e+00, 'ffn2_w_gate': 6.747446e-02, 'ffn2_w_up': 9.460434e-02, 'ffn2_w_down': 1.519471e-01}


def _to_microbatches(a, axis):
    t = _jnp.moveaxis(a, axis, 0)
    t = t.reshape((N_MICROBATCH, t.shape[0] // N_MICROBATCH) + t.shape[1:])
    return _jnp.moveaxis(t, 1, axis + 1)


def setup_inputs(seed: int = 0) -> dict:
    inp = _fwd_setup_inputs(seed)
    key = _jax.random.fold_in(_jax.random.key(seed), 7919)
    shape, _ = _output_shape()
    out = dict(inp)
    out["loss_target"] = _jax.random.normal(_jax.random.fold_in(key, 0), shape, _jnp.float32)
    for i, name in enumerate(TWIN_WEIGHTS):
        w = inp[name].astype(_jnp.float32)
        if MOMENT_SCALE is None:
            s = _jnp.sqrt(_jnp.mean(_jnp.square(w)) + 1e-30)
        else:
            s = MOMENT_SCALE[name]
        km, kv = _jax.random.split(_jax.random.fold_in(key, i + 1))
        out[name] = w
        out["m_" + name] = s * _jax.random.normal(km, w.shape, _jnp.float32)
        out["v_" + name] = (s * s) * _jax.random.uniform(kv, w.shape, _jnp.float32, 0.5, 1.5)
    if N_MICROBATCH > 1:
        for name, axis in PER_EXAMPLE_BATCH_AXIS.items():
            out[name] = _to_microbatches(out[name], axis)
    return {'x': out['x'], 'ffn1_norm': out['ffn1_norm'], 'ffn1_w_gate': out['ffn1_w_gate'], 'ffn1_w_up': out['ffn1_w_up'], 'ffn1_w_down': out['ffn1_w_down'], 'mix_norm': out['mix_norm'], 'w_in': out['w_in'], 'b_forget': out['b_forget'], 'pool_w': out['pool_w'], 'pool_scale': out['pool_scale'], 'q_norm': out['q_norm'], 'k_norm': out['k_norm'], 'out_norm_pool': out['out_norm_pool'], 'out_norm_attn': out['out_norm_attn'], 'w_out': out['w_out'], 'ffn2_norm': out['ffn2_norm'], 'ffn2_w_gate': out['ffn2_w_gate'], 'ffn2_w_up': out['ffn2_w_up'], 'ffn2_w_down': out['ffn2_w_down'], 'loss_target': out['loss_target'], 'm_ffn1_norm': out['m_ffn1_norm'], 'm_ffn1_w_gate': out['m_ffn1_w_gate'], 'm_ffn1_w_up': out['m_ffn1_w_up'], 'm_ffn1_w_down': out['m_ffn1_w_down'], 'm_mix_norm': out['m_mix_norm'], 'm_w_in': out['m_w_in'], 'm_b_forget': out['m_b_forget'], 'm_pool_w': out['m_pool_w'], 'm_pool_scale': out['m_pool_scale'], 'm_q_norm': out['m_q_norm'], 'm_k_norm': out['m_k_norm'], 'm_out_norm_pool': out['m_out_norm_pool'], 'm_out_norm_attn': out['m_out_norm_attn'], 'm_w_out': out['m_w_out'], 'm_ffn2_norm': out['m_ffn2_norm'], 'm_ffn2_w_gate': out['m_ffn2_w_gate'], 'm_ffn2_w_up': out['m_ffn2_w_up'], 'm_ffn2_w_down': out['m_ffn2_w_down'], 'v_ffn1_norm': out['v_ffn1_norm'], 'v_ffn1_w_gate': out['v_ffn1_w_gate'], 'v_ffn1_w_up': out['v_ffn1_w_up'], 'v_ffn1_w_down': out['v_ffn1_w_down'], 'v_mix_norm': out['v_mix_norm'], 'v_w_in': out['v_w_in'], 'v_b_forget': out['v_b_forget'], 'v_pool_w': out['v_pool_w'], 'v_pool_scale': out['v_pool_scale'], 'v_q_norm': out['v_q_norm'], 'v_k_norm': out['v_k_norm'], 'v_out_norm_pool': out['v_out_norm_pool'], 'v_out_norm_attn': out['v_out_norm_attn'], 'v_w_out': out['v_w_out'], 'v_ffn2_norm': out['v_ffn2_norm'], 'v_ffn2_w_gate': out['v_ffn2_w_gate'], 'v_ffn2_w_up': out['v_ffn2_w_up'], 'v_ffn2_w_down': out['v_ffn2_w_down']}


def _loss(weights, diff, rest, loss_target):
    with _jax.named_scope("forward"):
        args = {**rest, TWIN_DIFF_INPUT: diff, **{k: w.astype(_WEIGHT_DTYPES[k]) for k, w in weights.items()}}
        y = _forward(args)
    with _jax.named_scope("loss_head"):
        err = _jnp.square(y.astype(_jnp.float32) - loss_target)
        return 0.5 * _jnp.sum(_jnp.mean(err, axis=-1)) if err.ndim else 0.5 * err


def _adamw(w, g, m, v):
    m = ADAM_B1 * m + (1.0 - ADAM_B1) * g
    v = ADAM_B2 * v + (1.0 - ADAM_B2) * _jnp.square(g)
    m_hat = m / (1.0 - ADAM_B1 ** ADAM_STEP)
    v_hat = v / (1.0 - ADAM_B2 ** ADAM_STEP)
    delta = -ADAM_LR * (m_hat / (_jnp.sqrt(v_hat) + ADAM_EPS) + ADAM_WD * w)
    return delta, m, v


def reference(x, ffn1_norm, ffn1_w_gate, ffn1_w_up, ffn1_w_down, mix_norm, w_in, b_forget, pool_w, pool_scale, q_norm, k_norm, out_norm_pool, out_norm_attn, w_out, ffn2_norm, ffn2_w_gate, ffn2_w_up, ffn2_w_down, loss_target, m_ffn1_norm, m_ffn1_w_gate, m_ffn1_w_up, m_ffn1_w_down, m_mix_norm, m_w_in, m_b_forget, m_pool_w, m_pool_scale, m_q_norm, m_k_norm, m_out_norm_pool, m_out_norm_attn, m_w_out, m_ffn2_norm, m_ffn2_w_gate, m_ffn2_w_up, m_ffn2_w_down, v_ffn1_norm, v_ffn1_w_gate, v_ffn1_w_up, v_ffn1_w_down, v_mix_norm, v_w_in, v_b_forget, v_pool_w, v_pool_scale, v_q_norm, v_k_norm, v_out_norm_pool, v_out_norm_attn, v_w_out, v_ffn2_norm, v_ffn2_w_gate, v_ffn2_w_up, v_ffn2_w_down):
    given = dict(x=x, ffn1_norm=ffn1_norm, ffn1_w_gate=ffn1_w_gate, ffn1_w_up=ffn1_w_up, ffn1_w_down=ffn1_w_down, mix_norm=mix_norm, w_in=w_in, b_forget=b_forget, pool_w=pool_w, pool_scale=pool_scale, q_norm=q_norm, k_norm=k_norm, out_norm_pool=out_norm_pool, out_norm_attn=out_norm_attn, w_out=w_out, ffn2_norm=ffn2_norm, ffn2_w_gate=ffn2_w_gate, ffn2_w_up=ffn2_w_up, ffn2_w_down=ffn2_w_down, loss_target=loss_target, m_ffn1_norm=m_ffn1_norm, m_ffn1_w_gate=m_ffn1_w_gate, m_ffn1_w_up=m_ffn1_w_up, m_ffn1_w_down=m_ffn1_w_down, m_mix_norm=m_mix_norm, m_w_in=m_w_in, m_b_forget=m_b_forget, m_pool_w=m_pool_w, m_pool_scale=m_pool_scale, m_q_norm=m_q_norm, m_k_norm=m_k_norm, m_out_norm_pool=m_out_norm_pool, m_out_norm_attn=m_out_norm_attn, m_w_out=m_w_out, m_ffn2_norm=m_ffn2_norm, m_ffn2_w_gate=m_ffn2_w_gate, m_ffn2_w_up=m_ffn2_w_up, m_ffn2_w_down=m_ffn2_w_down, v_ffn1_norm=v_ffn1_norm, v_ffn1_w_gate=v_ffn1_w_gate, v_ffn1_w_up=v_ffn1_w_up, v_ffn1_w_down=v_ffn1_w_down, v_mix_norm=v_mix_norm, v_w_in=v_w_in, v_b_forget=v_b_forget, v_pool_w=v_pool_w, v_pool_scale=v_pool_scale, v_q_norm=v_q_norm, v_k_norm=v_k_norm, v_out_norm_pool=v_out_norm_pool, v_out_norm_attn=v_out_norm_attn, v_w_out=v_w_out, v_ffn2_norm=v_ffn2_norm, v_ffn2_w_gate=v_ffn2_w_gate, v_ffn2_w_up=v_ffn2_w_up, v_ffn2_w_down=v_ffn2_w_down)
    weights = {n: given[n] for n in TWIN_WEIGHTS}
    shared = {n: given[n] for n in SHARED_INPUTS}
    per_example = {n: given[n] for n in ['x']}
    grad_fn = _jax.value_and_grad(_loss, argnums=(0, 1))

    def one_microbatch(ex, loss_target):
        ex = dict(ex)
        diff = ex.pop(TWIN_DIFF_INPUT)
        return grad_fn(weights, diff, {**shared, **ex}, loss_target)

    if N_MICROBATCH == 1:
        loss, (grad_w, grad_x) = one_microbatch(per_example, given["loss_target"])
    else:
        def body(carry, xs):
            loss_sum, grad_sum = carry
            l_k, (gw_k, gx_k) = one_microbatch(xs[0], xs[1])
            with _jax.named_scope("update"):
                return (loss_sum + l_k, _jax.tree.map(_jnp.add, grad_sum, gw_k)), gx_k

        init = (_jnp.zeros((), _jnp.float32), _jax.tree.map(_jnp.zeros_like, weights))
        (loss, grad_w), grad_x = _jax.lax.scan(body, init, (per_example, given["loss_target"]))
    with _jax.named_scope("update"):
        delta_w, new_m, new_v = {}, {}, {}
        for n in TWIN_WEIGHTS:
            delta_w[n], new_m[n], new_v[n] = _adamw(weights[n], grad_w[n], given["m_" + n], given["v_" + n])
    return (loss, grad_x, *[grad_w[n] for n in TWIN_WEIGHTS], *[delta_w[n] for n in TWIN_WEIGHTS],
            *[new_m[n] for n in TWIN_WEIGHTS], *[new_v[n] for n in TWIN_WEIGHTS])
```

```python
import functools

import jax
import jax.numpy as jnp
from jax import lax
from jax.experimental import pallas as pl
from jax.experimental.pallas import tpu as pltpu

F32 = jnp.float32
BF16 = jnp.bfloat16

EPS = 1e-6
D_MODEL = 1024
D_FF = 2816
N_HEADS = 8
HEAD_DIM = 64
POOL_WIDTH = 512
ATTN_WIDTH = 512
POOL_GROUPS = 4
POOL_GROUP_DIM = 128
POOL_WINDOWS = (2, 4, 8, 16)
POOL_HALO = 16
MIX_COLS = POOL_WIDTH + 3 * ATTN_WIDTH + N_HEADS
MIX_PAD = POOL_WIDTH + 3 * ATTN_WIDTH + 128
N_DEV = 8
BF16_ROWS = 16
LANES = 128
VMEM_LIMIT = 56 * 1024 * 1024

ADAM_LR = 0.001
ADAM_B1 = 0.9
ADAM_B2 = 0.999
ADAM_EPS = 1e-08
ADAM_WD = 0.01
ADAM_STEP = 10

SMALL_ROWS = 80


def _params(*sem):
    return pltpu.CompilerParams(dimension_semantics=sem, vmem_limit_bytes=VMEM_LIMIT)


def _dot(a, b):
    return jnp.dot(a, b, preferred_element_type=F32)


def _dot_nt(a, b):
    return lax.dot_general(a, b, (((1,), (1,)), ((), ())), preferred_element_type=F32)


def _dot_tn(a, b):
    return lax.dot_general(a, b, (((0,), (0,)), ((), ())), preferred_element_type=F32)


def _resident(shape):
    return pl.BlockSpec(shape, lambda *_: (0,) * len(shape), pipeline_mode=pl.Buffered(1))


def _rows(tm, width):
    return pl.BlockSpec((tm, width), lambda i: (i, 0))


def _rms_scale(x):
    return lax.rsqrt(jnp.mean(x * x, axis=-1, keepdims=True) + EPS)


def _rms_bwd(dh, x, gain):
    r = _rms_scale(x)
    n = x * r
    dgain = jnp.sum(dh * n, axis=0, keepdims=True)
    dn = dh * gain
    dx = r * (dn - n * jnp.mean(dn * n, axis=-1, keepdims=True))
    return dx, dgain


def _split3(x):
    hi = x.astype(BF16)
    r1 = x - hi.astype(F32)
    mid = r1.astype(BF16)
    lo = (r1 - mid.astype(F32)).astype(BF16)
    return hi, mid, lo


def _split2(x):
    hi = x.astype(BF16)
    return hi, (x - hi.astype(F32)).astype(BF16)


FF_CHUNK = 256


def _ffn_fwd(x, gain, wg_t, wu_t, wd, target, name):
    t, d = x.shape
    f = wd.shape[0]
    tm = 256
    with_loss = target is not None

    def body(*refs):
        if with_loss:
            x_ref, g_ref, wg_ref, wu_ref, wd_ref, t_ref, h_ref, a_ref, b_ref, s_ref, dy_ref, loss_ref = refs
        else:
            x_ref, g_ref, wg_ref, wu_ref, wd_ref, h_ref, a_ref, b_ref, s_ref, y_ref = refs
        xv = x_ref[...]
        h = (xv * _rms_scale(xv) * g_ref[...]).astype(BF16)
        h_ref[...] = h
        acc = jnp.zeros((tm, d), F32)
        for c in range(f // FF_CHUNK):
            sl = pl.ds(c * FF_CHUNK, FF_CHUNK)
            a = _dot_nt(h, wg_ref[sl, :])
            b = _dot_nt(h, wu_ref[sl, :])
            s = (a * jax.nn.sigmoid(a) * b).astype(BF16)
            a_ref[:, sl] = a.astype(BF16)
            b_ref[:, sl] = b.astype(BF16)
            s_ref[:, sl] = s
            acc = acc + _dot(s, wd_ref[sl, :])
        y = xv + 0.5 * acc
        if with_loss:
            e = y - t_ref[...]
            dy_ref[...] = e * (1.0 / d)

            @pl.when(pl.program_id(0) == 0)
            def _():
                loss_ref[...] = jnp.zeros_like(loss_ref)

            part = jnp.sum(jnp.sum(e * e, axis=0, keepdims=True), axis=1, keepdims=True)
            loss_ref[...] += part * (0.5 / d)
        else:
            y_ref[...] = y

    saved_shapes = [
        jax.ShapeDtypeStruct((t, d), BF16),
        jax.ShapeDtypeStruct((t, f), BF16),
        jax.ShapeDtypeStruct((t, f), BF16),
        jax.ShapeDtypeStruct((t, f), BF16),
    ]
    saved_specs = [_rows(tm, d), _rows(tm, f), _rows(tm, f), _rows(tm, f)]
    in_specs = [_rows(tm, d), _resident((1, d)), _resident((f, d)), _resident((f, d)), _resident((f, d))]
    args = [x, gain, wg_t, wu_t, wd]
    if with_loss:
        in_specs.append(_rows(tm, d))
        args.append(target)
        out_shape = saved_shapes + [jax.ShapeDtypeStruct((t, d), F32), jax.ShapeDtypeStruct((1, 1), F32)]
        out_specs = saved_specs + [_rows(tm, d), pl.BlockSpec((1, 1), lambda i: (0, 0))]
    else:
        out_shape = saved_shapes + [jax.ShapeDtypeStruct((t, d), F32)]
        out_specs = saved_specs + [_rows(tm, d)]
    return pl.pallas_call(
        body, name=name, grid=(t // tm,), in_specs=in_specs, out_specs=out_specs, out_shape=out_shape,
        compiler_params=_params("arbitrary"),
    )(*args)


def _ffn_bwd(dy, x, gain, a, b, wg_t, wu_t, wd, name):
    t, d = x.shape
    f = wd.shape[0]
    tm = 256

    def body(dy_ref, x_ref, g_ref, a_ref, b_ref, wg_ref, wu_ref, wd_ref, da_ref, db_ref, dyh_ref, dx_ref, dg_ref):
        dyv = dy_ref[...]
        dyh = (0.5 * dyv).astype(BF16)
        dyh_ref[...] = dyh
        dh = jnp.zeros((tm, d), F32)
        for c in range(f // FF_CHUNK):
            sl = pl.ds(c * FF_CHUNK, FF_CHUNK)
            ds = _dot_nt(dyh, wd_ref[sl, :])
            av = a_ref[:, sl].astype(F32)
            bv = b_ref[:, sl].astype(F32)
            sig = jax.nn.sigmoid(av)
            da = (ds * bv * (sig * (1.0 + av * (1.0 - sig)))).astype(BF16)
            db = (ds * (av * sig)).astype(BF16)
            da_ref[:, sl] = da
            db_ref[:, sl] = db
            dh = dh + _dot(da, wg_ref[sl, :]) + _dot(db, wu_ref[sl, :])
        dx, dgain = _rms_bwd(dh, x_ref[...], g_ref[...])
        dx_ref[...] = dyv + dx

        @pl.when(pl.program_id(0) == 0)
        def _():
            dg_ref[...] = jnp.zeros_like(dg_ref)

        dg_ref[...] += dgain

    return pl.pallas_call(
        body, name=name, grid=(t // tm,),
        in_specs=[_rows(tm, d), _rows(tm, d), _resident((1, d)), _rows(tm, f), _rows(tm, f),
                  _resident((f, d)), _resident((f, d)), _resident((f, d))],
        out_specs=[_rows(tm, f), _rows(tm, f), _rows(tm, d), _rows(tm, d), pl.BlockSpec((1, d), lambda i: (0, 0))],
        out_shape=[jax.ShapeDtypeStruct((t, f), BF16), jax.ShapeDtypeStruct((t, f), BF16),
                   jax.ShapeDtypeStruct((t, d), BF16), jax.ShapeDtypeStruct((t, d), F32),
                   jax.ShapeDtypeStruct((1, d), F32)],
        compiler_params=_params("arbitrary"),
    )(dy, x, gain, a, b, wg_t, wu_t, wd)


def _ffn_wgrad(da, db, s, h, dyh, name):
    t, f = da.shape
    d = h.shape[1]
    tf = f // 2
    tk = 256
    nk = t // tk

    def body(da_ref, db_ref, s_ref, h_ref, dy_ref, og_ref, ou_ref, od_ref, acc_g, acc_u, acc_d):
        k = pl.program_id(1)

        @pl.when(k == 0)
        def _():
            acc_g[...] = jnp.zeros_like(acc_g)
            acc_u[...] = jnp.zeros_like(acc_u)
            acc_d[...] = jnp.zeros_like(acc_d)

        hv = h_ref[...]
        acc_g[...] += _dot_tn(da_ref[...], hv)
        acc_u[...] += _dot_tn(db_ref[...], hv)
        acc_d[...] += _dot_tn(s_ref[...], dy_ref[...])

        @pl.when(k == nk - 1)
        def _():
            og_ref[...] = acc_g[...].astype(BF16)
            ou_ref[...] = acc_u[...].astype(BF16)
            od_ref[...] = acc_d[...].astype(BF16)

    a_spec = pl.BlockSpec((tk, tf), lambda j, k: (k, j))
    b_spec = pl.BlockSpec((tk, d), lambda j, k: (k, 0))
    o_spec = pl.BlockSpec((tf, d), lambda j, k: (j, 0))
    o_shape = jax.ShapeDtypeStruct((f, d), BF16)
    return pl.pallas_call(
        body, name=name, grid=(f // tf, nk),
        in_specs=[a_spec, a_spec, a_spec, b_spec, b_spec],
        out_specs=[o_spec, o_spec, o_spec], out_shape=[o_shape, o_shape, o_shape],
        scratch_shapes=[pltpu.VMEM((tf, d), F32)] * 3,
        compiler_params=_params("arbitrary", "arbitrary"),
    )(da, db, s, h, dyh)


def _wgrad(a, b, name):
    t, n = a.shape
    d = b.shape[1]
    tk = 256
    nk = t // tk

    def body(a_ref, b_ref, o_ref, acc):
        k = pl.program_id(0)

        @pl.when(k == 0)
        def _():
            acc[...] = jnp.zeros_like(acc)

        acc[...] += _dot_tn(a_ref[...], b_ref[...])

        @pl.when(k == nk - 1)
        def _():
            o_ref[...] = acc[...].astype(BF16)

    return pl.pallas_call(
        body, name=name, grid=(nk,),
        in_specs=[_rows(tk, n), _rows(tk, d)],
        out_specs=pl.BlockSpec((n, d), lambda k: (0, 0)), out_shape=jax.ShapeDtypeStruct((n, d), BF16),
        scratch_shapes=[pltpu.VMEM((n, d), F32)],
        compiler_params=_params("arbitrary"),
    )(a, b)


def _mix_in_fwd(x, gain, w_in_t):
    t, d = x.shape
    tm = 512
    pw, aw = POOL_WIDTH, ATTN_WIDTH

    def body(x_ref, g_ref, w_ref, hm_ref, pv_ref, q_ref, k_ref, v_ref, f_ref):
        xv = x_ref[...]
        hm = (xv * _rms_scale(xv) * g_ref[...]).astype(BF16)
        hm_ref[...] = hm
        pv_ref[...] = _dot_nt(hm, w_ref[pl.ds(0, pw), :])
        q_ref[...] = _dot_nt(hm, w_ref[pl.ds(pw, aw), :])
        k_ref[...] = _dot_nt(hm, w_ref[pl.ds(pw + aw, aw), :])
        v_ref[...] = _dot_nt(hm, w_ref[pl.ds(pw + 2 * aw, aw), :]).astype(BF16)
        f_ref[...] = _dot_nt(hm, w_ref[pl.ds(pw + 3 * aw, LANES), :])

    return pl.pallas_call(
        body, name="mix_in_fwd", grid=(t // tm,),
        in_specs=[_rows(tm, d), _resident((1, d)), _resident((MIX_PAD, d))],
        out_specs=[_rows(tm, d), _rows(tm, pw), _rows(tm, aw), _rows(tm, aw), _rows(tm, aw), _rows(tm, LANES)],
        out_shape=[jax.ShapeDtypeStruct((t, d), BF16), jax.ShapeDtypeStruct((t, pw), F32),
                   jax.ShapeDtypeStruct((t, aw), F32), jax.ShapeDtypeStruct((t, aw), F32),
                   jax.ShapeDtypeStruct((t, aw), BF16), jax.ShapeDtypeStruct((t, LANES), F32)],
        compiler_params=_params("arbitrary"),
    )(x, gain, w_in_t)


def _pool_fwd(pv, pool_w, pool_scale, gain, bsz, seq):
    ts = 512
    ns = seq // ts
    pw = POOL_WIDTH

    def body(pv_ref, w_ref, sc_ref, g_ref, pooled_ref, mixed_ref, y_ref, ext):
        s = pl.program_id(1)

        @pl.when(s == 0)
        def _():
            ext[pl.ds(0, POOL_HALO), :] = jnp.zeros((POOL_HALO, pw), F32)

        p = pv_ref[...]
        ext[pl.ds(POOL_HALO, ts), :] = p
        pos = s * ts + lax.broadcasted_iota(jnp.int32, (ts, 1), 0)
        parts = []
        for g, w in enumerate(POOL_WINDOWS):
            lanes = pl.ds(g * POOL_GROUP_DIM, POOL_GROUP_DIM)
            win = ext[pl.ds(POOL_HALO, ts), lanes]
            for i in range(1, w):
                win = win + ext[pl.ds(POOL_HALO - i, ts), lanes]
            cnt = jnp.minimum(pos + 1, w).astype(F32)
            pooled = (win / cnt - ext[pl.ds(POOL_HALO, ts), lanes]).astype(BF16)
            pooled_ref[:, lanes] = pooled
            parts.append(_dot(pooled, w_ref[g].astype(BF16)))
        mixed = jnp.concatenate(parts, axis=1)
        mixed_ref[...] = mixed
        pm = mixed * sc_ref[...]
        y_ref[...] = (pm * _rms_scale(pm) * g_ref[...]).astype(BF16)
        ext[pl.ds(0, POOL_HALO), :] = p[ts - POOL_HALO:, :]

    blk = pl.BlockSpec((ts, pw), lambda b, s: (b * ns + s, 0))
    t = bsz * seq
    return pl.pallas_call(
        body, name="pool_fwd", grid=(bsz, ns),
        in_specs=[blk, pl.BlockSpec((POOL_GROUPS, POOL_GROUP_DIM, POOL_GROUP_DIM), lambda b, s: (0, 0, 0)),
                  pl.BlockSpec((1, pw), lambda b, s: (0, 0)), pl.BlockSpec((1, pw), lambda b, s: (0, 0))],
        out_specs=[blk, blk, blk],
        out_shape=[jax.ShapeDtypeStruct((t, pw), BF16), jax.ShapeDtypeStruct((t, pw), F32),
                   jax.ShapeDtypeStruct((t, pw), BF16)],
        scratch_shapes=[pltpu.VMEM((POOL_HALO + ts, pw), F32)],
        compiler_params=_params("arbitrary", "arbitrary"),
    )(pv, pool_w, pool_scale, gain)


def _pool_bwd(dy, mixed, pooled, pool_w, pool_scale, gain, bsz, seq):
    ts = 512
    ns = seq // ts
    pw = POOL_WIDTH

    def body(dy_ref, mixed_ref, pooled_ref, w_ref, sc_ref, g_ref, dpv_ref, dw_ref, dsc_ref, dg_ref, ext):
        b = pl.program_id(0)
        sr = pl.program_id(1)
        s = ns - 1 - sr

        @pl.when(jnp.logical_and(b == 0, sr == 0))
        def _():
            dw_ref[...] = jnp.zeros_like(dw_ref)
            dsc_ref[...] = jnp.zeros_like(dsc_ref)
            dg_ref[...] = jnp.zeros_like(dg_ref)

        @pl.when(sr == 0)
        def _():
            ext[pl.ds(ts, POOL_HALO), :] = jnp.zeros((POOL_HALO, pw), F32)

        mixed = mixed_ref[...]
        sc = sc_ref[...]
        dpm, dgain = _rms_bwd(dy_ref[...], mixed * sc, g_ref[...])
        dg_ref[...] += dgain
        dsc_ref[...] += jnp.sum(dpm * mixed, axis=0, keepdims=True)
        dmixed = (dpm * sc).astype(BF16)
        pos = s * ts + lax.broadcasted_iota(jnp.int32, (ts, 1), 0)
        dpooled = []
        for g, w in enumerate(POOL_WINDOWS):
            lanes = pl.ds(g * POOL_GROUP_DIM, POOL_GROUP_DIM)
            dm = dmixed[:, g * POOL_GROUP_DIM:(g + 1) * POOL_GROUP_DIM]
            dw_ref[g] += _dot_tn(pooled_ref[:, lanes], dm)
            dp = _dot_nt(dm, w_ref[g].astype(BF16))
            dpooled.append(dp)
            cnt = jnp.minimum(pos + 1, w).astype(F32)
            ext[pl.ds(0, ts), lanes] = dp / cnt
        for g, w in enumerate(POOL_WINDOWS):
            lanes = pl.ds(g * POOL_GROUP_DIM, POOL_GROUP_DIM)
            win = ext[pl.ds(0, ts), lanes]
            for i in range(1, w):
                win = win + ext[pl.ds(i, ts), lanes]
            dpv_ref[:, lanes] = (win - dpooled[g]).astype(BF16)
        head = ext[pl.ds(0, POOL_HALO), :]
        ext[pl.ds(ts, POOL_HALO), :] = head

    blk = pl.BlockSpec((ts, pw), lambda b, s: (b * ns + (ns - 1 - s), 0))
    vec = pl.BlockSpec((1, pw), lambda b, s: (0, 0))
    wspec = pl.BlockSpec((POOL_GROUPS, POOL_GROUP_DIM, POOL_GROUP_DIM), lambda b, s: (0, 0, 0))
    t = bsz * seq
    return pl.pallas_call(
        body, name="pool_bwd", grid=(bsz, ns),
        in_specs=[blk, blk, blk, wspec, vec, vec],
        out_specs=[blk, wspec, vec, vec],
        out_shape=[jax.ShapeDtypeStruct((t, pw), BF16),
                   jax.ShapeDtypeStruct((POOL_GROUPS, POOL_GROUP_DIM, POOL_GROUP_DIM), F32),
                   jax.ShapeDtypeStruct((1, pw), F32), jax.ShapeDtypeStruct((1, pw), F32)],
        scratch_shapes=[pltpu.VMEM((ts + POOL_HALO, pw), F32)],
        compiler_params=_params("arbitrary", "arbitrary"),
    )(dy, mixed, pooled, pool_w, pool_scale, gain)


AUX_ONE = 64
AUX_F = 67

ATTN_PREP_ROWS = 256


def _seg_ones(width, seg):
    r = lax.broadcasted_iota(jnp.int32, (width, width), 0) // seg
    c = lax.broadcasted_iota(jnp.int32, (width, width), 1) // seg
    return (r == c).astype(BF16)


def _tri_ones(n, lower):
    r = lax.broadcasted_iota(jnp.int32, (n, n), 0)
    c = lax.broadcasted_iota(jnp.int32, (n, n), 1)
    return ((r >= c) if lower else (r <= c)).astype(BF16)


def _place_pieces(first_lane):
    r = lax.broadcasted_iota(jnp.int32, (3 * LANES, N_HEADS * LANES), 0)
    c = lax.broadcasted_iota(jnp.int32, (3 * LANES, N_HEADS * LANES), 1)
    piece, head = r // LANES, r % LANES
    return jnp.logical_and(head < N_HEADS, c == head * LANES + first_lane + piece).astype(BF16)


def _head_sums(x, seg_ones):
    hi, lo = _split2(x)
    return _dot(hi, seg_ones) + _dot(lo, seg_ones)


def _log_sigmoid(x):
    return jnp.minimum(x, 0.0) - jnp.log(1.0 + jnp.exp(-jnp.abs(x)))


def _attn_prep_fwd(q, k, f, b_forget, q_gain, k_gain, bsz, seq):
    ts = ATTN_PREP_ROWS
    ns = seq // ts
    aw = ATTN_WIDTH
    t = bsz * seq
    seg = _seg_ones(aw, HEAD_DIM)
    tri = _tri_ones(ts, True)

    def body(q_ref, k_ref, f_ref, bf_ref, gq_ref, gk_ref, seg_ref, tri_ref, pq_ref, pk_ref, qp_ref, kp_ref, carry):
        s = pl.program_id(1)

        @pl.when(s == 0)
        def _():
            carry[...] = jnp.zeros_like(carry)

        logf = _log_sigmoid(f_ref[...] + bf_ref[...])
        hi, mid, lo = _split3(logf)
        tri_v = tri_ref[...]
        fc = _dot(tri_v, hi) + _dot(tri_v, mid) + _dot(tri_v, lo) + carry[pl.ds(0, 1), :]
        carry[pl.ds(0, 1), :] = fc[ts - 1:, :]
        pcs = jnp.concatenate(_split3(fc), axis=1)
        lane = lax.broadcasted_iota(jnp.int32, (1, LANES), 1)
        ones_q = jnp.logical_and(lane >= AUX_ONE, lane < AUX_ONE + 3).astype(F32)
        ones_k = jnp.logical_and(lane >= AUX_F, lane < AUX_F + 3).astype(F32)
        seg_v = seg_ref[...]

        def build(x_ref, g_ref, scale, out_ref, ones, place_ref, f_sign):
            xv = x_ref[...]
            r = lax.rsqrt(_head_sums(xv * xv, seg_v) * (1.0 / HEAD_DIM) + EPS)
            xn = xv * r * g_ref[...] * scale
            aux = _dot(pcs, place_ref[...]) * f_sign
            for h in range(N_HEADS):
                pair = xn[:, (h // 2) * LANES:(h // 2 + 1) * LANES]
                feat = pair if h % 2 == 0 else pltpu.roll(pair, HEAD_DIM, 1)
                aux_h = aux[:, h * LANES:(h + 1) * LANES] + ones
                out_ref[:, h * LANES:(h + 1) * LANES] = jnp.where(lane < HEAD_DIM, feat, aux_h).astype(BF16)

        build(q_ref, gq_ref, 0.125, qp_ref, ones_q, pq_ref, 1.0)
        build(k_ref, gk_ref, 1.0, kp_ref, ones_k, pk_ref, -1.0)

    blk = pl.BlockSpec((ts, aw), lambda b, s: (b * ns + s, 0))
    fblk = pl.BlockSpec((ts, LANES), lambda b, s: (b * ns + s, 0))
    oblk = pl.BlockSpec((ts, N_HEADS * LANES), lambda b, s: (b * ns + s, 0))
    const = lambda shape: pl.BlockSpec(shape, lambda b, s: (0, 0))
    return pl.pallas_call(
        body, name="attn_prep_fwd", grid=(bsz, ns),
        in_specs=[blk, blk, fblk, const((1, LANES)), const((1, aw)), const((1, aw)), const((aw, aw)), const((ts, ts)),
                  const((3 * LANES, N_HEADS * LANES)), const((3 * LANES, N_HEADS * LANES))],
        out_specs=[oblk, oblk],
        out_shape=[jax.ShapeDtypeStruct((t, N_HEADS * LANES), BF16)] * 2,
        scratch_shapes=[pltpu.VMEM((8, LANES), F32)],
        compiler_params=_params("arbitrary", "arbitrary"),
    )(q, k, f, b_forget, q_gain, k_gain, seg, tri, _place_pieces(AUX_F), _place_pieces(AUX_ONE))


def _attn_prep_bwd(dqp, dkp, q, k, f, b_forget, q_gain, k_gain, bsz, seq):
    ts = ATTN_PREP_ROWS
    ns = seq // ts
    aw = ATTN_WIDTH
    t = bsz * seq
    seg = _seg_ones(aw, HEAD_DIM)
    tri = _tri_ones(ts, False)

    def body(dqp_ref, dkp_ref, q_ref, k_ref, f_ref, bf_ref, gq_ref, gk_ref, seg_ref, tri_ref,
             dq_ref, dk_ref, df_ref, dgq_ref, dgk_ref, dbf_ref, carry):
        b = pl.program_id(0)
        sr = pl.program_id(1)

        @pl.when(jnp.logical_and(b == 0, sr == 0))
        def _():
            dgq_ref[...] = jnp.zeros_like(dgq_ref)
            dgk_ref[...] = jnp.zeros_like(dgk_ref)
            dbf_ref[...] = jnp.zeros_like(dbf_ref)

        @pl.when(sr == 0)
        def _():
            carry[...] = jnp.zeros_like(carry)

        lane = lax.broadcasted_iota(jnp.int32, (1, LANES), 1)
        seg_v = seg_ref[...]

        def norm_bwd(dp_ref, x_ref, g_ref, scale, dx_ref, dgain_ref):
            parts = []
            for j in range(N_HEADS // 2):
                even = dp_ref[:, (2 * j) * LANES:(2 * j + 1) * LANES]
                odd = dp_ref[:, (2 * j + 1) * LANES:(2 * j + 2) * LANES]
                parts.append(jnp.where(lane < HEAD_DIM, even, pltpu.roll(odd, HEAD_DIM, 1)))
            dxn = jnp.concatenate(parts, axis=1) * scale
            xv = x_ref[...]
            r = lax.rsqrt(_head_sums(xv * xv, seg_v) * (1.0 / HEAD_DIM) + EPS)
            n = xv * r
            dgain_ref[...] += jnp.sum(dxn * n, axis=0, keepdims=True)
            dn = dxn * g_ref[...]
            m = _head_sums(dn * n, seg_v) * (1.0 / HEAD_DIM)
            dx_ref[...] = (r * (dn - n * m)).astype(BF16)

        norm_bwd(dqp_ref, q_ref, gq_ref, 0.125, dq_ref, dgq_ref)
        norm_bwd(dkp_ref, k_ref, gk_ref, 1.0, dk_ref, dgk_ref)

        dfc = jnp.zeros((ts, LANES), F32)
        for h in range(N_HEADS):
            cols = pl.ds(h * LANES, LANES)
            both = jnp.where(lane == AUX_F, dqp_ref[:, cols], 0.0) - jnp.where(lane == AUX_ONE, dkp_ref[:, cols], 0.0)
            dfc = jnp.where(lane == h, jnp.sum(both, axis=1, keepdims=True), dfc)
        hi, mid, lo = _split3(dfc)
        tri_v = tri_ref[...]
        dlogf = _dot(tri_v, hi) + _dot(tri_v, mid) + _dot(tri_v, lo) + carry[pl.ds(0, 1), :]
        carry[pl.ds(0, 1), :] = dlogf[0:1, :]
        df = jnp.where(lane < N_HEADS, dlogf * jax.nn.sigmoid(-(f_ref[...] + bf_ref[...])), 0.0)
        df_ref[...] = df.astype(BF16)
        dbf_ref[...] += jnp.sum(df, axis=0, keepdims=True)

    rev = lambda b, s: (b * ns + (ns - 1 - s), 0)
    blk = pl.BlockSpec((ts, aw), rev)
    fblk = pl.BlockSpec((ts, LANES), rev)
    pblk = pl.BlockSpec((ts, N_HEADS * LANES), rev)
    const = lambda shape: pl.BlockSpec(shape, lambda b, s: (0, 0))
    return pl.pallas_call(
        body, name="attn_prep_bwd", grid=(bsz, ns),
        in_specs=[pblk, pblk, blk, blk, fblk, const((1, LANES)), const((1, aw)), const((1, aw)), const((aw, aw)),
                  const((ts, ts))],
        out_specs=[blk, blk, fblk, const((1, aw)), const((1, aw)), const((1, LANES))],
        out_shape=[jax.ShapeDtypeStruct((t, aw), BF16), jax.ShapeDtypeStruct((t, aw), BF16),
                   jax.ShapeDtypeStruct((t, LANES), BF16), jax.ShapeDtypeStruct((1, aw), F32),
                   jax.ShapeDtypeStruct((1, aw), F32), jax.ShapeDtypeStruct((1, LANES), F32)],
        scratch_shapes=[pltpu.VMEM((8, LANES), F32)],
        compiler_params=_params("arbitrary", "arbitrary"),
    )(dqp, dkp, q, k, f, b_forget, q_gain, k_gain, seg, tri)


ATTN_BLOCK = 512
HEAD_PAIRS = N_HEADS // 2


def _flash_fwd(qp, kp, v, bsz, seq):
    tq = ATTN_BLOCK
    nq = seq // tq
    t = bsz * seq

    def body(q_ref, k_ref, v_ref, o_ref, lse_ref, m_sc, l_sc, acc_sc):
        i = pl.program_id(2)
        j = pl.program_id(3)

        @pl.when(j == 0)
        def _():
            m_sc[...] = jnp.full(m_sc.shape, -jnp.inf, F32)
            l_sc[...] = jnp.zeros_like(l_sc)
            acc_sc[...] = jnp.zeros_like(acc_sc)

        lane = lax.broadcasted_iota(jnp.int32, (1, LANES), 1)
        low = lane < HEAD_DIM

        def step(masked):
            vv = v_ref[...]
            for h in range(2):
                mine = low if h == 0 else jnp.logical_not(low)
                s = _dot_nt(q_ref[:, h * LANES:(h + 1) * LANES], k_ref[:, h * LANES:(h + 1) * LANES])
                if masked:
                    row = lax.broadcasted_iota(jnp.int32, (tq, tq), 0)
                    col = lax.broadcasted_iota(jnp.int32, (tq, tq), 1)
                    s = jnp.where(row >= col, s, -jnp.inf)
                m_prev = m_sc[h]
                m_new = jnp.maximum(m_prev, jnp.max(s, axis=1, keepdims=True))
                p = jnp.exp(s - jnp.tile(m_new, (1, tq // LANES)))
                alpha = jnp.exp(m_prev - m_new)
                l_sc[h] = alpha * l_sc[h] + jnp.sum(p, axis=1, keepdims=True)
                m_sc[h] = m_new
                pv = _dot(p.astype(BF16), jnp.where(mine, vv, jnp.zeros_like(vv)))
                acc_sc[...] = acc_sc[...] * jnp.where(mine, alpha, 1.0) + pv

        @pl.when(j < i)
        def _():
            step(False)

        @pl.when(j == i)
        def _():
            step(True)
            l = jnp.where(low, l_sc[0], l_sc[1])
            m = jnp.where(low, m_sc[0], m_sc[1])
            o_ref[...] = acc_sc[...] / l
            lse_ref[...] = m + jnp.log(l)

    qspec = pl.BlockSpec((tq, 2 * LANES), lambda b, hp, i, j: (b * nq + i, hp))
    kspec = pl.BlockSpec((tq, 2 * LANES), lambda b, hp, i, j: (b * nq + jnp.minimum(i, j), hp))
    vspec = pl.BlockSpec((tq, LANES), lambda b, hp, i, j: (b * nq + jnp.minimum(i, j), hp))
    ospec = pl.BlockSpec((tq, LANES), lambda b, hp, i, j: (b * nq + i, hp))
    return pl.pallas_call(
        body, name="flash_fwd", grid=(bsz, HEAD_PAIRS, nq, nq),
        in_specs=[qspec, kspec, vspec], out_specs=[ospec, ospec],
        out_shape=[jax.ShapeDtypeStruct((t, ATTN_WIDTH), F32), jax.ShapeDtypeStruct((t, ATTN_WIDTH), F32)],
        scratch_shapes=[pltpu.VMEM((2, tq, LANES), F32), pltpu.VMEM((2, tq, LANES), F32), pltpu.VMEM((tq, LANES), F32)],
        compiler_params=_params("arbitrary", "arbitrary", "arbitrary", "arbitrary"),
    )(qp, kp, v)


def _flash_bwd(qp, kp, v, o, do, lse, bsz, seq):
    tq = ATTN_BLOCK
    nq = seq // tq
    t = bsz * seq

    def body(q_ref, k_ref, v_ref, o_ref, do_ref, lse_ref, dq_ref, dk_ref, dv_ref, dk_acc, dv_acc):
        j = pl.program_id(2)
        i = pl.program_id(3)

        @pl.when(jnp.logical_and(j == 0, i == 0))
        def _():
            dq_ref[...] = jnp.zeros_like(dq_ref)

        @pl.when(i == 0)
        def _():
            dk_acc[...] = jnp.zeros_like(dk_acc)
            dv_acc[...] = jnp.zeros_like(dv_acc)

        lane = lax.broadcasted_iota(jnp.int32, (1, LANES), 1)
        low = lane < HEAD_DIM

        def step(masked):
            dov = do_ref[...]
            dd = dov * o_ref[...]
            dob = dov.astype(BF16)
            vv = v_ref[...]
            lse_v = lse_ref[...]
            rows = pl.ds(pl.multiple_of(i * tq, tq), tq)
            for h in range(2):
                mine = low if h == 0 else jnp.logical_not(low)
                cols = pl.ds(h * LANES, LANES)
                qh = q_ref[:, cols]
                kh = k_ref[:, cols]
                s = _dot_nt(qh, kh)
                lse_h = jnp.where(mine, lse_v, pltpu.roll(lse_v, HEAD_DIM, 1))
                p = jnp.exp(s - jnp.tile(lse_h, (1, tq // LANES)))
                if masked:
                    row = lax.broadcasted_iota(jnp.int32, (tq, tq), 0)
                    col = lax.broadcasted_iota(jnp.int32, (tq, tq), 1)
                    p = jnp.where(row >= col, p, 0.0)
                delta = jnp.sum(jnp.where(mine, dd, 0.0), axis=1, keepdims=True)
                dp = _dot_nt(dob, jnp.where(mine, vv, jnp.zeros_like(vv)))
                ds = (p * (dp - delta)).astype(BF16)
                dv_acc[...] += jnp.where(mine, _dot_tn(p.astype(BF16), dob), 0.0)
                dk_acc[:, cols] += _dot_tn(ds, qh)
                dq_ref[rows, cols] += _dot(ds, kh)

        @pl.when(i > j)
        def _():
            step(False)

        @pl.when(i == j)
        def _():
            step(True)

        @pl.when(i == nq - 1)
        def _():
            dk_ref[...] = dk_acc[...]
            dv_ref[...] = dv_acc[...].astype(BF16)

    qspec = pl.BlockSpec((tq, 2 * LANES), lambda b, hp, j, i: (b * nq + jnp.maximum(i, j), hp))
    kspec = pl.BlockSpec((tq, 2 * LANES), lambda b, hp, j, i: (b * nq + j, hp))
    vspec = pl.BlockSpec((tq, LANES), lambda b, hp, j, i: (b * nq + j, hp))
    ospec = pl.BlockSpec((tq, LANES), lambda b, hp, j, i: (b * nq + jnp.maximum(i, j), hp))
    dqspec = pl.BlockSpec((seq, 2 * LANES), lambda b, hp, j, i: (b, hp))
    return pl.pallas_call(
        body, name="flash_bwd", grid=(bsz, HEAD_PAIRS, nq, nq),
        in_specs=[qspec, kspec, vspec, ospec, ospec, ospec], out_specs=[dqspec, kspec, vspec],
        out_shape=[jax.ShapeDtypeStruct((t, N_HEADS * LANES), F32), jax.ShapeDtypeStruct((t, N_HEADS * LANES), F32),
                   jax.ShapeDtypeStruct((t, ATTN_WIDTH), BF16)],
        scratch_shapes=[pltpu.VMEM((tq, 2 * LANES), F32), pltpu.VMEM((tq, LANES), F32)],
        compiler_params=_params("arbitrary", "arbitrary", "arbitrary", "arbitrary"),
    )(qp, kp, v, o, do, lse)


def _mix_out_fwd(o, y_pool, x, gain, w_out):
    t, d = x.shape
    tm = 512
    pw, aw = POOL_WIDTH, ATTN_WIDTH

    def body(o_ref, yp_ref, x_ref, g_ref, w_ref, ycat_ref, y_ref):
        ov = o_ref[...]
        ya = (ov * _rms_scale(ov) * g_ref[...]).astype(BF16)
        ycat = jnp.concatenate([yp_ref[...], ya], axis=1)
        ycat_ref[...] = ycat
        y_ref[...] = x_ref[...] + _dot(ycat, w_ref[...])

    return pl.pallas_call(
        body, name="mix_out_fwd", grid=(t // tm,),
        in_specs=[_rows(tm, aw), _rows(tm, pw), _rows(tm, d), _resident((1, aw)), _resident((pw + aw, d))],
        out_specs=[_rows(tm, pw + aw), _rows(tm, d)],
        out_shape=[jax.ShapeDtypeStruct((t, pw + aw), BF16), jax.ShapeDtypeStruct((t, d), F32)],
        compiler_params=_params("arbitrary"),
    )(o, y_pool, x, gain, w_out)


def _mix_out_bwd(dx, o, gain, w_out):
    t, d = dx.shape
    tm = 512
    pw, aw = POOL_WIDTH, ATTN_WIDTH

    def body(dx_ref, o_ref, g_ref, w_ref, dxb_ref, dyp_ref, do_ref, dg_ref):
        dxb = dx_ref[...].astype(BF16)
        dxb_ref[...] = dxb
        dyp_ref[...] = _dot_nt(dxb, w_ref[pl.ds(0, pw), :])
        dya = _dot_nt(dxb, w_ref[pl.ds(pw, aw), :])
        do, dgain = _rms_bwd(dya, o_ref[...], g_ref[...])
        do_ref[...] = do

        @pl.when(pl.program_id(0) == 0)
        def _():
            dg_ref[...] = jnp.zeros_like(dg_ref)

        dg_ref[...] += dgain

    return pl.pallas_call(
        body, name="mix_out_bwd", grid=(t // tm,),
        in_specs=[_rows(tm, d), _rows(tm, aw), _resident((1, aw)), _resident((pw + aw, d))],
        out_specs=[_rows(tm, d), _rows(tm, pw), _rows(tm, aw), pl.BlockSpec((1, aw), lambda i: (0, 0))],
        out_shape=[jax.ShapeDtypeStruct((t, d), BF16), jax.ShapeDtypeStruct((t, pw), F32),
                   jax.ShapeDtypeStruct((t, aw), F32), jax.ShapeDtypeStruct((1, aw), F32)],
        compiler_params=_params("arbitrary"),
    )(dx, o, gain, w_out)


def _mix_in_bwd(dpv, dq, dk, dv, df, x, dx_res, gain, w_in_t):
    t, d = x.shape
    tm = 512
    pw, aw = POOL_WIDTH, ATTN_WIDTH

    def body(dpv_ref, dq_ref, dk_ref, dv_ref, df_ref, x_ref, dxr_ref, g_ref, w_ref, dh_ref, dx_ref, dg_ref):
        dh = jnp.concatenate([dpv_ref[...], dq_ref[...], dk_ref[...], dv_ref[...], df_ref[...]], axis=1)
        dh_ref[...] = dh
        dhm = _dot(dh, w_ref[...])
        dx, dgain = _rms_bwd(dhm, x_ref[...], g_ref[...])
        dx_ref[...] = dxr_ref[...] + dx

        @pl.when(pl.program_id(0) == 0)
        def _():
            dg_ref[...] = jnp.zeros_like(dg_ref)

        dg_ref[...] += dgain

    return pl.pallas_call(
        body, name="mix_in_bwd", grid=(t // tm,),
        in_specs=[_rows(tm, pw), _rows(tm, aw), _rows(tm, aw), _rows(tm, aw), _rows(tm, LANES), _rows(tm, d),
                  _rows(tm, d), _resident((1, d)), _resident((MIX_PAD, d))],
        out_specs=[_rows(tm, MIX_PAD), _rows(tm, d), pl.BlockSpec((1, d), lambda i: (0, 0))],
        out_shape=[jax.ShapeDtypeStruct((t, MIX_PAD), BF16), jax.ShapeDtypeStruct((t, d), F32),
                   jax.ShapeDtypeStruct((1, d), F32)],
        compiler_params=_params("arbitrary"),
    )(dpv, dq, dk, dv, df, x, dx_res, gain, w_in_t)


MESH_IDS = pl.DeviceIdType.MESH


def _me():
    return lax.axis_index("x"), lax.axis_index("y"), lax.axis_index("c")


def _peer(x, y, c, p):
    px = 1 - x if p & 4 else x
    py = 1 - y if p & 2 else y
    pc = 1 - c if p & 1 else c
    return (px, py, pc), 4 * px + 2 * py + pc


def _all_gather_rows(slabs):
    n = len(slabs)

    def body(*refs):
        ins, outs = refs[:n], refs[n:2 * n]
        send_sems, recv_sems, local_sems = refs[2 * n:]
        x, y, c = _me()
        me = 4 * x + 2 * y + c

        def mine(w):
            rows = ins[w].shape[0]
            return outs[w].at[pl.ds(pl.multiple_of(me * rows, BF16_ROWS), rows)]

        local = [pltpu.make_async_copy(ins[w], mine(w), local_sems.at[w]) for w in range(n)]
        for cp in local:
            cp.start()
        for p in range(1, N_DEV):
            peer, _ = _peer(x, y, c, p)
            for w in range(n):
                pltpu.make_async_remote_copy(ins[w], mine(w), send_sems.at[w, p - 1], recv_sems.at[w, p - 1],
                                             device_id=peer, device_id_type=MESH_IDS).start()
        for cp in local:
            cp.wait()
        for p in range(1, N_DEV):
            peer, peer_id = _peer(x, y, c, p)
            for w in range(n):
                rows = ins[w].shape[0]
                theirs = outs[w].at[pl.ds(pl.multiple_of(peer_id * rows, BF16_ROWS), rows)]
                cp = pltpu.make_async_remote_copy(ins[w], theirs, send_sems.at[w, p - 1], recv_sems.at[w, p - 1],
                                                  device_id=peer, device_id_type=MESH_IDS)
                cp.wait_send()
                cp.wait_recv()

    hbm = pl.BlockSpec(memory_space=pl.ANY)
    return pl.pallas_call(
        body, name="gather_weights",
        in_specs=[hbm] * n, out_specs=[hbm] * n,
        out_shape=[jax.ShapeDtypeStruct((N_DEV * s.shape[0], s.shape[1]), s.dtype) for s in slabs],
        scratch_shapes=[pltpu.SemaphoreType.DMA((n, N_DEV - 1)), pltpu.SemaphoreType.DMA((n, N_DEV - 1)),
                        pltpu.SemaphoreType.DMA((n,))],
        compiler_params=pltpu.CompilerParams(has_side_effects=True),
    )(*slabs)


def _exchange_row_blocks(grads):
    n = len(grads)

    def body(*refs):
        ins, outs = refs[:n], refs[n:2 * n]
        send_sems, recv_sems, local_sems = refs[2 * n:]
        x, y, c = _me()
        me = 4 * x + 2 * y + c

        def block(w, dev):
            rows = ins[w].shape[0] // N_DEV
            return ins[w].at[pl.ds(pl.multiple_of(dev * rows, BF16_ROWS), rows)]

        local = [pltpu.make_async_copy(block(w, me), outs[w].at[me], local_sems.at[w]) for w in range(n)]
        for cp in local:
            cp.start()
        for p in range(1, N_DEV):
            peer, peer_id = _peer(x, y, c, p)
            for w in range(n):
                pltpu.make_async_remote_copy(block(w, peer_id), outs[w].at[me], send_sems.at[w, p - 1],
                                             recv_sems.at[w, p - 1], device_id=peer, device_id_type=MESH_IDS).start()
        for cp in local:
            cp.wait()
        for p in range(1, N_DEV):
            peer, peer_id = _peer(x, y, c, p)
            for w in range(n):
                cp = pltpu.make_async_remote_copy(block(w, peer_id), outs[w].at[peer_id], send_sems.at[w, p - 1],
                                                  recv_sems.at[w, p - 1], device_id=peer, device_id_type=MESH_IDS)
                cp.wait_send()
                cp.wait_recv()

    hbm = pl.BlockSpec(memory_space=pl.ANY)
    return pl.pallas_call(
        body, name="exchange_grads",
        in_specs=[hbm] * n, out_specs=[hbm] * n,
        out_shape=[jax.ShapeDtypeStruct((N_DEV, g.shape[0] // N_DEV, g.shape[1]), g.dtype) for g in grads],
        scratch_shapes=[pltpu.SemaphoreType.DMA((n, N_DEV - 1)), pltpu.SemaphoreType.DMA((n, N_DEV - 1)),
                        pltpu.SemaphoreType.DMA((n,))],
        compiler_params=pltpu.CompilerParams(has_side_effects=True),
    )(*grads)


def _sum_slots(parts, name):
    _, rows, d = parts.shape

    def body(p_ref, o_ref):
        acc = p_ref[0].astype(F32)
        for dev in range(1, N_DEV):
            acc = acc + p_ref[dev].astype(F32)
        o_ref[...] = acc

    return pl.pallas_call(
        body, name=name, grid=(1,),
        in_specs=[pl.BlockSpec((N_DEV, rows, d), lambda i: (0, 0, 0))],
        out_specs=pl.BlockSpec((rows, d), lambda i: (0, 0)), out_shape=jax.ShapeDtypeStruct((rows, d), F32),
        compiler_params=_params("arbitrary"),
    )(parts)


def _all_reduce_small(part):
    rows, d = part.shape

    def body(p_ref, o_ref, slots, send_sems, recv_sems):
        x, y, c = _me()
        me = 4 * x + 2 * y + c
        slots[me] = p_ref[...]
        for p in range(1, N_DEV):
            peer, _ = _peer(x, y, c, p)
            pltpu.make_async_remote_copy(p_ref, slots.at[me], send_sems.at[p - 1], recv_sems.at[p - 1],
                                         device_id=peer, device_id_type=MESH_IDS).start()
        for p in range(1, N_DEV):
            peer, peer_id = _peer(x, y, c, p)
            cp = pltpu.make_async_remote_copy(p_ref, slots.at[peer_id], send_sems.at[p - 1], recv_sems.at[p - 1],
                                              device_id=peer, device_id_type=MESH_IDS)
            cp.wait_send()
            cp.wait_recv()
        acc = slots[0]
        for dev in range(1, N_DEV):
            acc = acc + slots[dev]
        o_ref[...] = acc

    vmem = pl.BlockSpec(memory_space=pltpu.VMEM)
    return pl.pallas_call(
        body, name="reduce_small_grads", in_specs=[vmem], out_specs=vmem,
        out_shape=jax.ShapeDtypeStruct((rows, d), F32),
        scratch_shapes=[pltpu.VMEM((N_DEV, rows, d), F32), pltpu.SemaphoreType.DMA((N_DEV - 1,)),
                        pltpu.SemaphoreType.DMA((N_DEV - 1,))],
        compiler_params=pltpu.CompilerParams(has_side_effects=True, vmem_limit_bytes=VMEM_LIMIT),
    )(part)


def _adamw(w, g, m, v, name):
    r, c = w.shape

    def body(w_ref, g_ref, m_ref, v_ref, d_ref, nm_ref, nv_ref):
        gv = g_ref[...]
        nm = ADAM_B1 * m_ref[...] + (1.0 - ADAM_B1) * gv
        nv = ADAM_B2 * v_ref[...] + (1.0 - ADAM_B2) * (gv * gv)
        m_hat = nm / (1.0 - ADAM_B1 ** ADAM_STEP)
        v_hat = nv / (1.0 - ADAM_B2 ** ADAM_STEP)
        d_ref[...] = -ADAM_LR * (m_hat / (jnp.sqrt(v_hat) + ADAM_EPS) + ADAM_WD * w_ref[...])
        nm_ref[...] = nm
        nv_ref[...] = nv

    tr = r // 4 if r % 32 == 0 else r
    spec = pl.BlockSpec((tr, c), lambda i: (i, 0))
    shape = jax.ShapeDtypeStruct((r, c), F32)
    return pl.pallas_call(
        body, name=name, grid=(r // tr,), in_specs=[spec] * 4, out_specs=[spec] * 3, out_shape=[shape] * 3,
        compiler_params=_params("arbitrary"),
    )(w, g, m, v)


def _pad_rows(a, rows):
    return jnp.pad(a, ((0, rows - a.shape[0]), (0, 0)))


def _row1(vec, width=D_MODEL):
    return jnp.pad(vec.reshape(1, -1), ((0, 0), (0, width - vec.shape[-1])))


SMALL_NAMES = ("ffn1_norm", "mix_norm", "ffn2_norm", "b_forget", "pool_scale", "q_norm", "k_norm", "out_norm_pool",
               "out_norm_attn", "pool_w")


def _pack_small(vals):
    rows = [_row1(vals[n].reshape(-1)) for n in SMALL_NAMES[:-1]]
    rows.append(vals["pool_w"].reshape(-1, D_MODEL))
    packed = jnp.concatenate(rows, axis=0)
    return _pad_rows(packed, SMALL_ROWS)


def _unpack_small(packed, like):
    out = {}
    for i, n in enumerate(SMALL_NAMES[:-1]):
        size = like[n].size
        out[n] = packed[i, :size].reshape(like[n].shape)
    first = len(SMALL_NAMES) - 1
    out["pool_w"] = packed[first:first + like["pool_w"].size // D_MODEL].reshape(like["pool_w"].shape)
    return out


def kernel(x, ffn1_norm, ffn1_w_gate, ffn1_w_up, ffn1_w_down, mix_norm, w_in, b_forget, pool_w, pool_scale, q_norm, k_norm, out_norm_pool, out_norm_attn, w_out, ffn2_norm, ffn2_w_gate, ffn2_w_up, ffn2_w_down, loss_target, m_ffn1_norm, m_ffn1_w_gate, m_ffn1_w_up, m_ffn1_w_down, m_mix_norm, m_w_in, m_b_forget, m_pool_w, m_pool_scale, m_q_norm, m_k_norm, m_out_norm_pool, m_out_norm_attn, m_w_out, m_ffn2_norm, m_ffn2_w_gate, m_ffn2_w_up, m_ffn2_w_down, v_ffn1_norm, v_ffn1_w_gate, v_ffn1_w_up, v_ffn1_w_down, v_mix_norm, v_w_in, v_b_forget, v_pool_w, v_pool_scale, v_q_norm, v_k_norm, v_out_norm_pool, v_out_norm_attn, v_w_out, v_ffn2_norm, v_ffn2_w_gate, v_ffn2_w_up, v_ffn2_w_down):
    bsz, seq, d = x.shape
    t = bsz * seq
    x0 = x.reshape(t, d)
    target = loss_target.reshape(t, d)
    in_rows = -(-w_in.shape[1] // BF16_ROWS) * BF16_ROWS

    slabs = [ffn1_w_gate.T, ffn1_w_up.T, ffn1_w_down, _pad_rows(w_in.T, in_rows), w_out,
             ffn2_w_gate.T, ffn2_w_up.T, ffn2_w_down]
    wg1, wu1, wd1, win_g, wout, wg2, wu2, wd2 = _all_gather_rows([s.astype(BF16) for s in slabs])
    win_cols = win_g.reshape(N_DEV, in_rows, d)[:, :w_in.shape[1]].reshape(MIX_COLS, d)
    win_t = _pad_rows(win_cols, MIX_PAD)

    g1, gm, g2 = ffn1_norm.reshape(1, d), mix_norm.reshape(1, d), ffn2_norm.reshape(1, d)
    bf_row = _row1(b_forget, LANES)
    gq = jnp.tile(q_norm, N_HEADS).reshape(1, ATTN_WIDTH)
    gk = jnp.tile(k_norm, N_HEADS).reshape(1, ATTN_WIDTH)
    scale_row = pool_scale.reshape(1, POOL_WIDTH)
    gp, ga = out_norm_pool.reshape(1, POOL_WIDTH), out_norm_attn.reshape(1, ATTN_WIDTH)

    h1, a1, b1, s1, x1 = _ffn_fwd(x0, g1, wg1, wu1, wd1, None, "ffn1_fwd")
    hm, pv, q, k, v, f = _mix_in_fwd(x1, gm, win_t)
    pooled, mixed, y_pool = _pool_fwd(pv, pool_w, scale_row, gp, bsz, seq)
    qp, kp = _attn_prep_fwd(q, k, f, bf_row, gq, gk, bsz, seq)
    o, lse = _flash_fwd(qp, kp, v, bsz, seq)
    ycat, x2 = _mix_out_fwd(o, y_pool, x1, ga, wout)
    h2, a2, b2, s2, dx3, loss_part = _ffn_fwd(x2, g2, wg2, wu2, wd2, target, "ffn2_fwd")

    da2, db2, dyh2, dx2, dg2 = _ffn_bwd(dx3, x2, g2, a2, b2, wg2, wu2, wd2, "ffn2_bwd")
    dwg2, dwu2, dwd2 = _ffn_wgrad(da2, db2, s2, h2, dyh2, "ffn2_wgrad")
    dx2b, dy_pool, do, dga = _mix_out_bwd(dx2, o, ga, wout)
    dwout = _wgrad(ycat, dx2b, "w_out_wgrad")
    dqp, dkp, dv = _flash_bwd(qp, kp, v, o, do, lse, bsz, seq)
    dq, dk, df, dgq, dgk, dbf = _attn_prep_bwd(dqp, dkp, q, k, f, bf_row, gq, gk, bsz, seq)
    dpv, dpool_w, dscale, dgp = _pool_bwd(dy_pool, mixed, pooled, pool_w, scale_row, gp, bsz, seq)
    dhcat, dx1, dgm = _mix_in_bwd(dpv, dq, dk, dv, df, x1, dx2, gm, win_t)
    dwin = _wgrad(dhcat, hm, "w_in_wgrad")
    da1, db1, dyh1, dx0, dg1 = _ffn_bwd(dx1, x0, g1, a1, b1, wg1, wu1, wd1, "ffn1_bwd")
    dwg1, dwu1, dwd1 = _ffn_wgrad(da1, db1, s1, h1, dyh1, "ffn1_wgrad")

    dwin_blocks = jnp.pad(dwin[:MIX_COLS].reshape(N_DEV, w_in.shape[1], d), ((0, 0), (0, in_rows - w_in.shape[1]), (0, 0)))
    parts = _exchange_row_blocks([dwg1, dwu1, dwd1, dwin_blocks.reshape(N_DEV * in_rows, d), dwout, dwg2, dwu2, dwd2])
    sums = [_sum_slots(p, f"sum_grads_{i}") for i, p in enumerate(parts)]
    grads = {
        "ffn1_w_gate": sums[0].T, "ffn1_w_up": sums[1].T, "ffn1_w_down": sums[2],
        "w_in": sums[3][:w_in.shape[1]].T, "w_out": sums[4],
        "ffn2_w_gate": sums[5].T, "ffn2_w_up": sums[6].T, "ffn2_w_down": sums[7],
    }

    fold = lambda g: g.reshape(N_HEADS, HEAD_DIM).sum(axis=0)
    small_like = dict(ffn1_norm=ffn1_norm, mix_norm=mix_norm, ffn2_norm=ffn2_norm, b_forget=b_forget,
                      pool_scale=pool_scale, q_norm=q_norm, k_norm=k_norm, out_norm_pool=out_norm_pool,
                      out_norm_attn=out_norm_attn, pool_w=pool_w)
    small_part = dict(ffn1_norm=dg1, mix_norm=dgm, ffn2_norm=dg2, b_forget=dbf[0, :N_HEADS], pool_scale=dscale,
                      q_norm=fold(dgq), k_norm=fold(dgk), out_norm_pool=dgp, out_norm_attn=dga, pool_w=dpool_w)
    small_sum = _all_reduce_small(_pack_small(small_part))
    grads.update(_unpack_small(small_sum, small_like))
    loss = lax.psum(loss_part[0, 0], ("x", "y", "c"))

    weights = dict(ffn1_norm=ffn1_norm, ffn1_w_gate=ffn1_w_gate, ffn1_w_up=ffn1_w_up, ffn1_w_down=ffn1_w_down,
                   mix_norm=mix_norm, w_in=w_in, b_forget=b_forget, pool_w=pool_w, pool_scale=pool_scale,
                   q_norm=q_norm, k_norm=k_norm, out_norm_pool=out_norm_pool, out_norm_attn=out_norm_attn,
                   w_out=w_out, ffn2_norm=ffn2_norm, ffn2_w_gate=ffn2_w_gate, ffn2_w_up=ffn2_w_up,
                   ffn2_w_down=ffn2_w_down)
    m_in = dict(ffn1_norm=m_ffn1_norm, ffn1_w_gate=m_ffn1_w_gate, ffn1_w_up=m_ffn1_w_up, ffn1_w_down=m_ffn1_w_down,
                mix_norm=m_mix_norm, w_in=m_w_in, b_forget=m_b_forget, pool_w=m_pool_w, pool_scale=m_pool_scale,
                q_norm=m_q_norm, k_norm=m_k_norm, out_norm_pool=m_out_norm_pool, out_norm_attn=m_out_norm_attn,
                w_out=m_w_out, ffn2_norm=m_ffn2_norm, ffn2_w_gate=m_ffn2_w_gate, ffn2_w_up=m_ffn2_w_up,
                ffn2_w_down=m_ffn2_w_down)
    v_in = dict(ffn1_norm=v_ffn1_norm, ffn1_w_gate=v_ffn1_w_gate, ffn1_w_up=v_ffn1_w_up, ffn1_w_down=v_ffn1_w_down,
                mix_norm=v_mix_norm, w_in=v_w_in, b_forget=v_b_forget, pool_w=v_pool_w, pool_scale=v_pool_scale,
                q_norm=v_q_norm, k_norm=v_k_norm, out_norm_pool=v_out_norm_pool, out_norm_attn=v_out_norm_attn,
                w_out=v_w_out, ffn2_norm=v_ffn2_norm, ffn2_w_gate=v_ffn2_w_gate, ffn2_w_up=v_ffn2_w_up,
                ffn2_w_down=v_ffn2_w_down)
    delta, new_m, new_v = {}, {}, {}
    for n in ("ffn1_w_gate", "ffn1_w_up", "ffn1_w_down", "w_in", "w_out", "ffn2_w_gate", "ffn2_w_up", "ffn2_w_down"):
        delta[n], new_m[n], new_v[n] = _adamw(weights[n], grads[n], m_in[n], v_in[n], f"adamw_{n}")
    small_d, small_m, small_v = _adamw(_pack_small(weights), small_sum, _pack_small(m_in), _pack_small(v_in),
                                       "adamw_small")
    delta.update(_unpack_small(small_d, small_like))
    new_m.update(_unpack_small(small_m, small_like))
    new_v.update(_unpack_small(small_v, small_like))

    order = ("ffn1_norm", "ffn1_w_gate", "ffn1_w_up", "ffn1_w_down", "mix_norm", "w_in", "b_forget", "pool_w",
             "pool_scale", "q_norm", "k_norm", "out_norm_pool", "out_norm_attn", "w_out", "ffn2_norm", "ffn2_w_gate",
             "ffn2_w_up", "ffn2_w_down")
    return (loss, dx0.reshape(bsz, seq, d), *[grads[n] for n in order], *[delta[n] for n in order],
            *[new_m[n] for n in order], *[new_v[n] for n in order])
```

```python
import functools

import jax
import jax.numpy as jnp
from jax import lax
from jax.experimental import pallas as pl
from jax.experimental.pallas import tpu as pltpu

F32 = jnp.float32
BF16 = jnp.bfloat16

EPS = 1e-6
D_MODEL = 1024
D_FF = 2816
N_HEADS = 8
HEAD_DIM = 64
POOL_WIDTH = 512
ATTN_WIDTH = 512
POOL_GROUPS = 4
POOL_GROUP_DIM = 128
POOL_WINDOWS = (2, 4, 8, 16)
POOL_HALO = 16
MIX_COLS = POOL_WIDTH + 3 * ATTN_WIDTH + N_HEADS
MIX_PAD = POOL_WIDTH + 3 * ATTN_WIDTH + 128
N_DEV = 8
BF16_ROWS = 16
LANES = 128
VMEM_LIMIT = 56 * 1024 * 1024

ADAM_LR = 0.001
ADAM_B1 = 0.9
ADAM_B2 = 0.999
ADAM_EPS = 1e-08
ADAM_WD = 0.01
ADAM_STEP = 10

SMALL_ROWS = 80


def _params(*sem):
    return pltpu.CompilerParams(dimension_semantics=sem, vmem_limit_bytes=VMEM_LIMIT)


def _dot(a, b):
    return jnp.dot(a, b, preferred_element_type=F32)


def _dot_nt(a, b):
    return lax.dot_general(a, b, (((1,), (1,)), ((), ())), preferred_element_type=F32)


def _dot_tn(a, b):
    return lax.dot_general(a, b, (((0,), (0,)), ((), ())), preferred_element_type=F32)


def _resident(shape):
    return pl.BlockSpec(shape, lambda *_: (0,) * len(shape), pipeline_mode=pl.Buffered(1))


def _rows(tm, width):
    return pl.BlockSpec((tm, width), lambda i: (i, 0))


def _rms_scale(x):
    return lax.rsqrt(jnp.mean(x * x, axis=-1, keepdims=True) + EPS)


def _rms_bwd(dh, x, gain):
    r = _rms_scale(x)
    n = x * r
    dgain = jnp.sum(dh * n, axis=0, keepdims=True)
    dn = dh * gain
    dx = r * (dn - n * jnp.mean(dn * n, axis=-1, keepdims=True))
    return dx, dgain


def _split3(x):
    hi = x.astype(BF16)
    r1 = x - hi.astype(F32)
    mid = r1.astype(BF16)
    lo = (r1 - mid.astype(F32)).astype(BF16)
    return hi, mid, lo


def _split2(x):
    hi = x.astype(BF16)
    return hi, (x - hi.astype(F32)).astype(BF16)


FF_CHUNK = 256


def _ffn_fwd(x, gain, wg_t, wu_t, wd, target, name):
    t, d = x.shape
    f = wd.shape[0]
    tm = 256
    with_loss = target is not None

    def body(*refs):
        if with_loss:
            x_ref, g_ref, wg_ref, wu_ref, wd_ref, t_ref, h_ref, a_ref, b_ref, s_ref, dy_ref, loss_ref = refs
        else:
            x_ref, g_ref, wg_ref, wu_ref, wd_ref, h_ref, a_ref, b_ref, s_ref, y_ref = refs
        xv = x_ref[...]
        h = (xv * _rms_scale(xv) * g_ref[...]).astype(BF16)
        h_ref[...] = h
        acc = jnp.zeros((tm, d), F32)
        for c in range(f // FF_CHUNK):
            sl = pl.ds(c * FF_CHUNK, FF_CHUNK)
            a = _dot_nt(h, wg_ref[sl, :])
            b = _dot_nt(h, wu_ref[sl, :])
            s = (a * jax.nn.sigmoid(a) * b).astype(BF16)
            a_ref[:, sl] = a.astype(BF16)
            b_ref[:, sl] = b.astype(BF16)
            s_ref[:, sl] = s
            acc = acc + _dot(s, wd_ref[sl, :])
        y = xv + 0.5 * acc
        if with_loss:
            e = y - t_ref[...]
            dy_ref[...] = e * (1.0 / d)

            @pl.when(pl.program_id(0) == 0)
            def _():
                loss_ref[...] = jnp.zeros_like(loss_ref)

            part = jnp.sum(jnp.sum(e * e, axis=0, keepdims=True), axis=1, keepdims=True)
            loss_ref[...] += part * (0.5 / d)
        else:
            y_ref[...] = y

    saved_shapes = [
        jax.ShapeDtypeStruct((t, d), BF16),
        jax.ShapeDtypeStruct((t, f), BF16),
        jax.ShapeDtypeStruct((t, f), BF16),
        jax.ShapeDtypeStruct((t, f), BF16),
    ]
    saved_specs = [_rows(tm, d), _rows(tm, f), _rows(tm, f), _rows(tm, f)]
    in_specs = [_rows(tm, d), _resident((1, d)), _resident((f, d)), _resident((f, d)), _resident((f, d))]
    args = [x, gain, wg_t, wu_t, wd]
    if with_loss:
        in_specs.append(_rows(tm, d))
        args.append(target)
        out_shape = saved_shapes + [jax.ShapeDtypeStruct((t, d), F32), jax.ShapeDtypeStruct((1, 1), F32)]
        out_specs = saved_specs + [_rows(tm, d), pl.BlockSpec((1, 1), lambda i: (0, 0))]
    else:
        out_shape = saved_shapes + [jax.ShapeDtypeStruct((t, d), F32)]
        out_specs = saved_specs + [_rows(tm, d)]
    return pl.pallas_call(
        body, name=name, grid=(t // tm,), in_specs=in_specs, out_specs=out_specs, out_shape=out_shape,
        compiler_params=_params("arbitrary"),
    )(*args)


def _ffn_bwd(dy, x, gain, a, b, wg_t, wu_t, wd, name):
    t, d = x.shape
    f = wd.shape[0]
    tm = 256

    def body(dy_ref, x_ref, g_ref, a_ref, b_ref, wg_ref, wu_ref, wd_ref, da_ref, db_ref, dyh_ref, dx_ref, dg_ref):
        dyv = dy_ref[...]
        dyh = (0.5 * dyv).astype(BF16)
        dyh_ref[...] = dyh
        dh = jnp.zeros((tm, d), F32)
        for c in range(f // FF_CHUNK):
            sl = pl.ds(c * FF_CHUNK, FF_CHUNK)
            ds = _dot_nt(dyh, wd_ref[sl, :])
            av = a_ref[:, sl].astype(F32)
            bv = b_ref[:, sl].astype(F32)
            sig = jax.nn.sigmoid(av)
            da = (ds * bv * (sig * (1.0 + av * (1.0 - sig)))).astype(BF16)
            db = (ds * (av * sig)).astype(BF16)
            da_ref[:, sl] = da
            db_ref[:, sl] = db
            dh = dh + _dot(da, wg_ref[sl, :]) + _dot(db, wu_ref[sl, :])
        dx, dgain = _rms_bwd(dh, x_ref[...], g_ref[...])
        dx_ref[...] = dyv + dx

        @pl.when(pl.program_id(0) == 0)
        def _():
            dg_ref[...] = jnp.zeros_like(dg_ref)

        dg_ref[...] += dgain

    return pl.pallas_call(
        body, name=name, grid=(t // tm,),
        in_specs=[_rows(tm, d), _rows(tm, d), _resident((1, d)), _rows(tm, f), _rows(tm, f),
                  _resident((f, d)), _resident((f, d)), _resident((f, d))],
        out_specs=[_rows(tm, f), _rows(tm, f), _rows(tm, d), _rows(tm, d), pl.BlockSpec((1, d), lambda i: (0, 0))],
        out_shape=[jax.ShapeDtypeStruct((t, f), BF16), jax.ShapeDtypeStruct((t, f), BF16),
                   jax.ShapeDtypeStruct((t, d), BF16), jax.ShapeDtypeStruct((t, d), F32),
                   jax.ShapeDtypeStruct((1, d), F32)],
        compiler_params=_params("arbitrary"),
    )(dy, x, gain, a, b, wg_t, wu_t, wd)


def _ffn_wgrad(da, db, s, h, dyh, name):
    t, f = da.shape
    d = h.shape[1]
    tf = f // 2
    tk = 256
    nk = t // tk

    def body(da_ref, db_ref, s_ref, h_ref, dy_ref, og_ref, ou_ref, od_ref, acc_g, acc_u, acc_d):
        k = pl.program_id(1)

        @pl.when(k == 0)
        def _():
            acc_g[...] = jnp.zeros_like(acc_g)
            acc_u[...] = jnp.zeros_like(acc_u)
            acc_d[...] = jnp.zeros_like(acc_d)

        hv = h_ref[...]
        acc_g[...] += _dot_tn(da_ref[...], hv)
        acc_u[...] += _dot_tn(db_ref[...], hv)
        acc_d[...] += _dot_tn(s_ref[...], dy_ref[...])

        @pl.when(k == nk - 1)
        def _():
            og_ref[...] = acc_g[...].astype(BF16)
            ou_ref[...] = acc_u[...].astype(BF16)
            od_ref[...] = acc_d[...].astype(BF16)

    a_spec = pl.BlockSpec((tk, tf), lambda j, k: (k, j))
    b_spec = pl.BlockSpec((tk, d), lambda j, k: (k, 0))
    o_spec = pl.BlockSpec((tf, d), lambda j, k: (j, 0))
    o_shape = jax.ShapeDtypeStruct((f, d), BF16)
    return pl.pallas_call(
        body, name=name, grid=(f // tf, nk),
        in_specs=[a_spec, a_spec, a_spec, b_spec, b_spec],
        out_specs=[o_spec, o_spec, o_spec], out_shape=[o_shape, o_shape, o_shape],
        scratch_shapes=[pltpu.VMEM((tf, d), F32)] * 3,
        compiler_params=_params("arbitrary", "arbitrary"),
    )(da, db, s, h, dyh)


def _wgrad(a, b, name):
    t, n = a.shape
    d = b.shape[1]
    tk = 256
    nk = t // tk

    def body(a_ref, b_ref, o_ref, acc):
        k = pl.program_id(0)

        @pl.when(k == 0)
        def _():
            acc[...] = jnp.zeros_like(acc)

        acc[...] += _dot_tn(a_ref[...], b_ref[...])

        @pl.when(k == nk - 1)
        def _():
            o_ref[...] = acc[...].astype(BF16)

    return pl.pallas_call(
        body, name=name, grid=(nk,),
        in_specs=[_rows(tk, n), _rows(tk, d)],
        out_specs=pl.BlockSpec((n, d), lambda k: (0, 0)), out_shape=jax.ShapeDtypeStruct((n, d), BF16),
        scratch_shapes=[pltpu.VMEM((n, d), F32)],
        compiler_params=_params("arbitrary"),
    )(a, b)


def _mix_in_fwd(x, gain, w_in_t):
    t, d = x.shape
    tm = 512
    pw, aw = POOL_WIDTH, ATTN_WIDTH

    def body(x_ref, g_ref, w_ref, hm_ref, pv_ref, q_ref, k_ref, v_ref, f_ref):
        xv = x_ref[...]
        hm = (xv * _rms_scale(xv) * g_ref[...]).astype(BF16)
        hm_ref[...] = hm
        pv_ref[...] = _dot_nt(hm, w_ref[pl.ds(0, pw), :])
        q_ref[...] = _dot_nt(hm, w_ref[pl.ds(pw, aw), :])
        k_ref[...] = _dot_nt(hm, w_ref[pl.ds(pw + aw, aw), :])
        v_ref[...] = _dot_nt(hm, w_ref[pl.ds(pw + 2 * aw, aw), :]).astype(BF16)
        f_ref[...] = _dot_nt(hm, w_ref[pl.ds(pw + 3 * aw, LANES), :])

    return pl.pallas_call(
        body, name="mix_in_fwd", grid=(t // tm,),
        in_specs=[_rows(tm, d), _resident((1, d)), _resident((MIX_PAD, d))],
        out_specs=[_rows(tm, d), _rows(tm, pw), _rows(tm, aw), _rows(tm, aw), _rows(tm, aw), _rows(tm, LANES)],
        out_shape=[jax.ShapeDtypeStruct((t, d), BF16), jax.ShapeDtypeStruct((t, pw), F32),
                   jax.ShapeDtypeStruct((t, aw), F32), jax.ShapeDtypeStruct((t, aw), F32),
                   jax.ShapeDtypeStruct((t, aw), BF16), jax.ShapeDtypeStruct((t, LANES), F32)],
        compiler_params=_params("arbitrary"),
    )(x, gain, w_in_t)


def _pool_fwd(pv, pool_w, pool_scale, gain, bsz, seq):
    ts = 512
    ns = seq // ts
    pw = POOL_WIDTH

    def body(pv_ref, w_ref, sc_ref, g_ref, pooled_ref, mixed_ref, y_ref, ext):
        s = pl.program_id(1)

        @pl.when(s == 0)
        def _():
            ext[pl.ds(0, POOL_HALO), :] = jnp.zeros((POOL_HALO, pw), F32)

        p = pv_ref[...]
        ext[pl.ds(POOL_HALO, ts), :] = p
        pos = s * ts + lax.broadcasted_iota(jnp.int32, (ts, 1), 0)
        parts = []
        for g, w in enumerate(POOL_WINDOWS):
            lanes = pl.ds(g * POOL_GROUP_DIM, POOL_GROUP_DIM)
            win = ext[pl.ds(POOL_HALO, ts), lanes]
            for i in range(1, w):
                win = win + ext[pl.ds(POOL_HALO - i, ts), lanes]
            cnt = jnp.minimum(pos + 1, w).astype(F32)
            pooled = (win / cnt - ext[pl.ds(POOL_HALO, ts), lanes]).astype(BF16)
            pooled_ref[:, lanes] = pooled
            parts.append(_dot(pooled, w_ref[g].astype(BF16)))
        mixed = jnp.concatenate(parts, axis=1)
        mixed_ref[...] = mixed
        pm = mixed * sc_ref[...]
        y_ref[...] = (pm * _rms_scale(pm) * g_ref[...]).astype(BF16)
        ext[pl.ds(0, POOL_HALO), :] = p[ts - POOL_HALO:, :]

    blk = pl.BlockSpec((ts, pw), lambda b, s: (b * ns + s, 0))
    t = bsz * seq
    return pl.pallas_call(
        body, name="pool_fwd", grid=(bsz, ns),
        in_specs=[blk, pl.BlockSpec((POOL_GROUPS, POOL_GROUP_DIM, POOL_GROUP_DIM), lambda b, s: (0, 0, 0)),
                  pl.BlockSpec((1, pw), lambda b, s: (0, 0)), pl.BlockSpec((1, pw), lambda b, s: (0, 0))],
        out_specs=[blk, blk, blk],
        out_shape=[jax.ShapeDtypeStruct((t, pw), BF16), jax.ShapeDtypeStruct((t, pw), F32),
                   jax.ShapeDtypeStruct((t, pw), BF16)],
        scratch_shapes=[pltpu.VMEM((POOL_HALO + ts, pw), F32)],
        compiler_params=_params("arbitrary", "arbitrary"),
    )(pv, pool_w, pool_scale, gain)


def _pool_bwd(dy, mixed, pooled, pool_w, pool_scale, gain, bsz, seq):
    ts = 512
    ns = seq // ts
    pw = POOL_WIDTH

    def body(dy_ref, mixed_ref, pooled_ref, w_ref, sc_ref, g_ref, dpv_ref, dw_ref, dsc_ref, dg_ref, ext):
        b = pl.program_id(0)
        sr = pl.program_id(1)
        s = ns - 1 - sr

        @pl.when(jnp.logical_and(b == 0, sr == 0))
        def _():
            dw_ref[...] = jnp.zeros_like(dw_ref)
            dsc_ref[...] = jnp.zeros_like(dsc_ref)
            dg_ref[...] = jnp.zeros_like(dg_ref)

        @pl.when(sr == 0)
        def _():
            ext[pl.ds(ts, POOL_HALO), :] = jnp.zeros((POOL_HALO, pw), F32)

        mixed = mixed_ref[...]
        sc = sc_ref[...]
        dpm, dgain = _rms_bwd(dy_ref[...], mixed * sc, g_ref[...])
        dg_ref[...] += dgain
        dsc_ref[...] += jnp.sum(dpm * mixed, axis=0, keepdims=True)
        dmixed = (dpm * sc).astype(BF16)
        pos = s * ts + lax.broadcasted_iota(jnp.int32, (ts, 1), 0)
        dpooled = []
        for g, w in enumerate(POOL_WINDOWS):
            lanes = pl.ds(g * POOL_GROUP_DIM, POOL_GROUP_DIM)
            dm = dmixed[:, g * POOL_GROUP_DIM:(g + 1) * POOL_GROUP_DIM]
            dw_ref[g] += _dot_tn(pooled_ref[:, lanes], dm)
            dp = _dot_nt(dm, w_ref[g].astype(BF16))
            dpooled.append(dp)
            cnt = jnp.minimum(pos + 1, w).astype(F32)
            ext[pl.ds(0, ts), lanes] = dp / cnt
        for g, w in enumerate(POOL_WINDOWS):
            lanes = pl.ds(g * POOL_GROUP_DIM, POOL_GROUP_DIM)
            win = ext[pl.ds(0, ts), lanes]
            for i in range(1, w):
                win = win + ext[pl.ds(i, ts), lanes]
            dpv_ref[:, lanes] = (win - dpooled[g]).astype(BF16)
        head = ext[pl.ds(0, POOL_HALO), :]
        ext[pl.ds(ts, POOL_HALO), :] = head

    blk = pl.BlockSpec((ts, pw), lambda b, s: (b * ns + (ns - 1 - s), 0))
    vec = pl.BlockSpec((1, pw), lambda b, s: (0, 0))
    wspec = pl.BlockSpec((POOL_GROUPS, POOL_GROUP_DIM, POOL_GROUP_DIM), lambda b, s: (0, 0, 0))
    t = bsz * seq
    return pl.pallas_call(
        body, name="pool_bwd", grid=(bsz, ns),
        in_specs=[blk, blk, blk, wspec, vec, vec],
        out_specs=[blk, wspec, vec, vec],
        out_shape=[jax.ShapeDtypeStruct((t, pw), BF16),
                   jax.ShapeDtypeStruct((POOL_GROUPS, POOL_GROUP_DIM, POOL_GROUP_DIM), F32),
                   jax.ShapeDtypeStruct((1, pw), F32), jax.ShapeDtypeStruct((1, pw), F32)],
        scratch_shapes=[pltpu.VMEM((ts + POOL_HALO, pw), F32)],
        compiler_params=_params("arbitrary", "arbitrary"),
    )(dy, mixed, pooled, pool_w, pool_scale, gain)


AUX_ONE = 64
AUX_F = 67

ATTN_PREP_ROWS = 256


def _seg_ones(width, seg):
    r = lax.broadcasted_iota(jnp.int32, (width, width), 0) // seg
    c = lax.broadcasted_iota(jnp.int32, (width, width), 1) // seg
    return (r == c).astype(BF16)


def _tri_ones(n, lower):
    r = lax.broadcasted_iota(jnp.int32, (n, n), 0)
    c = lax.broadcasted_iota(jnp.int32, (n, n), 1)
    return ((r >= c) if lower else (r <= c)).astype(BF16)


def _place_pieces(first_lane):
    r = lax.broadcasted_iota(jnp.int32, (3 * LANES, N_HEADS * LANES), 0)
    c = lax.broadcasted_iota(jnp.int32, (3 * LANES, N_HEADS * LANES), 1)
    piece, head = r // LANES, r % LANES
    return jnp.logical_and(head < N_HEADS, c == head * LANES + first_lane + piece).astype(BF16)


def _head_sums(x, seg_ones):
    hi, lo = _split2(x)
    return _dot(hi, seg_ones) + _dot(lo, seg_ones)


def _log_sigmoid(x):
    return jnp.minimum(x, 0.0) - jnp.log(1.0 + jnp.exp(-jnp.abs(x)))


def _attn_prep_fwd(q, k, f, b_forget, q_gain, k_gain, bsz, seq):
    ts = ATTN_PREP_ROWS
    ns = seq // ts
    aw = ATTN_WIDTH
    t = bsz * seq
    seg = _seg_ones(aw, HEAD_DIM)
    tri = _tri_ones(ts, True)

    def body(q_ref, k_ref, f_ref, bf_ref, gq_ref, gk_ref, seg_ref, tri_ref, pq_ref, pk_ref, qp_ref, kp_ref, carry):
        s = pl.program_id(1)

        @pl.when(s == 0)
        def _():
            carry[...] = jnp.zeros_like(carry)

        logf = _log_sigmoid(f_ref[...] + bf_ref[...])
        hi, mid, lo = _split3(logf)
        tri_v = tri_ref[...]
        fc = _dot(tri_v, hi) + _dot(tri_v, mid) + _dot(tri_v, lo) + carry[pl.ds(0, 1), :]
        carry[pl.ds(0, 1), :] = fc[ts - 1:, :]
        pcs = jnp.concatenate(_split3(fc), axis=1)
        lane = lax.broadcasted_iota(jnp.int32, (1, LANES), 1)
        ones_q = jnp.logical_and(lane >= AUX_ONE, lane < AUX_ONE + 3).astype(F32)
        ones_k = jnp.logical_and(lane >= AUX_F, lane < AUX_F + 3).astype(F32)
        seg_v = seg_ref[...]

        def build(x_ref, g_ref, scale, out_ref, ones, place_ref, f_sign):
            xv = x_ref[...]
            r = lax.rsqrt(_head_sums(xv * xv, seg_v) * (1.0 / HEAD_DIM) + EPS)
            xn = xv * r * g_ref[...] * scale
            aux = _dot(pcs, place_ref[...]) * f_sign
            for h in range(N_HEADS):
                pair = xn[:, (h // 2) * LANES:(h // 2 + 1) * LANES]
                feat = pair if h % 2 == 0 else pltpu.roll(pair, HEAD_DIM, 1)
                aux_h = aux[:, h * LANES:(h + 1) * LANES] + ones
                out_ref[:, h * LANES:(h + 1) * LANES] = jnp.where(lane < HEAD_DIM, feat, aux_h).astype(BF16)

        build(q_ref, gq_ref, 0.125, qp_ref, ones_q, pq_ref, 1.0)
        build(k_ref, gk_ref, 1.0, kp_ref, ones_k, pk_ref, -1.0)

    blk = pl.BlockSpec((ts, aw), lambda b, s: (b * ns + s, 0))
    fblk = pl.BlockSpec((ts, LANES), lambda b, s: (b * ns + s, 0))
    oblk = pl.BlockSpec((ts, N_HEADS * LANES), lambda b, s: (b * ns + s, 0))
    const = lambda shape: pl.BlockSpec(shape, lambda b, s: (0, 0))
    return pl.pallas_call(
        body, name="attn_prep_fwd", grid=(bsz, ns),
        in_specs=[blk, blk, fblk, const((1, LANES)), const((1, aw)), const((1, aw)), const((aw, aw)), const((ts, ts)),
                  const((3 * LANES, N_HEADS * LANES)), const((3 * LANES, N_HEADS * LANES))],
        out_specs=[oblk, oblk],
        out_shape=[jax.ShapeDtypeStruct((t, N_HEADS * LANES), BF16)] * 2,
        scratch_shapes=[pltpu.VMEM((8, LANES), F32)],
        compiler_params=_params("arbitrary", "arbitrary"),
    )(q, k, f, b_forget, q_gain, k_gain, seg, tri, _place_pieces(AUX_F), _place_pieces(AUX_ONE))


def _attn_prep_bwd(dqp, dkp, q, k, f, b_forget, q_gain, k_gain, bsz, seq):
    ts = ATTN_PREP_ROWS
    ns = seq // ts
    aw = ATTN_WIDTH
    t = bsz * seq
    seg = _seg_ones(aw, HEAD_DIM)
    tri = _tri_ones(ts, False)

    def body(dqp_ref, dkp_ref, q_ref, k_ref, f_ref, bf_ref, gq_ref, gk_ref, seg_ref, tri_ref,
             dq_ref, dk_ref, df_ref, dgq_ref, dgk_ref, dbf_ref, carry):
        b = pl.program_id(0)
        sr = pl.program_id(1)

        @pl.when(jnp.logical_and(b == 0, sr == 0))
        def _():
            dgq_ref[...] = jnp.zeros_like(dgq_ref)
            dgk_ref[...] = jnp.zeros_like(dgk_ref)
            dbf_ref[...] = jnp.zeros_like(dbf_ref)

        @pl.when(sr == 0)
        def _():
            carry[...] = jnp.zeros_like(carry)

        lane = lax.broadcasted_iota(jnp.int32, (1, LANES), 1)
        seg_v = seg_ref[...]

        def norm_bwd(dp_ref, x_ref, g_ref, scale, dx_ref, dgain_ref):
            parts = []
            for j in range(N_HEADS // 2):
                even = dp_ref[:, (2 * j) * LANES:(2 * j + 1) * LANES]
                odd = dp_ref[:, (2 * j + 1) * LANES:(2 * j + 2) * LANES]
                parts.append(jnp.where(lane < HEAD_DIM, even, pltpu.roll(odd, HEAD_DIM, 1)))
            dxn = jnp.concatenate(parts, axis=1) * scale
            xv = x_ref[...]
            r = lax.rsqrt(_head_sums(xv * xv, seg_v) * (1.0 / HEAD_DIM) + EPS)
            n = xv * r
            dgain_ref[...] += jnp.sum(dxn * n, axis=0, keepdims=True)
            dn = dxn * g_ref[...]
            m = _head_sums(dn * n, seg_v) * (1.0 / HEAD_DIM)
            dx_ref[...] = (r * (dn - n * m)).astype(BF16)

        norm_bwd(dqp_ref, q_ref, gq_ref, 0.125, dq_ref, dgq_ref)
        norm_bwd(dkp_ref, k_ref, gk_ref, 1.0, dk_ref, dgk_ref)

        dfc = jnp.zeros((ts, LANES), F32)
        for h in range(N_HEADS):
            cols = pl.ds(h * LANES, LANES)
            both = jnp.where(lane == AUX_F, dqp_ref[:, cols], 0.0) - jnp.where(lane == AUX_ONE, dkp_ref[:, cols], 0.0)
            dfc = jnp.where(lane == h, jnp.sum(both, axis=1, keepdims=True), dfc)
        hi, mid, lo = _split3(dfc)
        tri_v = tri_ref[...]
        dlogf = _dot(tri_v, hi) + _dot(tri_v, mid) + _dot(tri_v, lo) + carry[pl.ds(0, 1), :]
        carry[pl.ds(0, 1), :] = dlogf[0:1, :]
        df = jnp.where(lane < N_HEADS, dlogf * jax.nn.sigmoid(-(f_ref[...] + bf_ref[...])), 0.0)
        df_ref[...] = df.astype(BF16)
        dbf_ref[...] += jnp.sum(df, axis=0, keepdims=True)

    rev = lambda b, s: (b * ns + (ns - 1 - s), 0)
    blk = pl.BlockSpec((ts, aw), rev)
    fblk = pl.BlockSpec((ts, LANES), rev)
    pblk = pl.BlockSpec((ts, N_HEADS * LANES), rev)
    const = lambda shape: pl.BlockSpec(shape, lambda b, s: (0, 0))
    return pl.pallas_call(
        body, name="attn_prep_bwd", grid=(bsz, ns),
        in_specs=[pblk, pblk, blk, blk, fblk, const((1, LANES)), const((1, aw)), const((1, aw)), const((aw, aw)),
                  const((ts, ts))],
        out_specs=[blk, blk, fblk, const((1, aw)), const((1, aw)), const((1, LANES))],
        out_shape=[jax.ShapeDtypeStruct((t, aw), BF16), jax.ShapeDtypeStruct((t, aw), BF16),
                   jax.ShapeDtypeStruct((t, LANES), BF16), jax.ShapeDtypeStruct((1, aw), F32),
                   jax.ShapeDtypeStruct((1, aw), F32), jax.ShapeDtypeStruct((1, LANES), F32)],
        scratch_shapes=[pltpu.VMEM((8, LANES), F32)],
        compiler_params=_params("arbitrary", "arbitrary"),
    )(dqp, dkp, q, k, f, b_forget, q_gain, k_gain, seg, tri)


ATTN_BLOCK = 512
HEAD_PAIRS = N_HEADS // 2


def _flash_fwd(qp, kp, v, bsz, seq):
    tq = ATTN_BLOCK
    nq = seq // tq
    t = bsz * seq

    def body(q_ref, k_ref, v_ref, o_ref, lse_ref, m_sc, l_sc, acc_sc):
        i = pl.program_id(2)
        j = pl.program_id(3)

        @pl.when(j == 0)
        def _():
            m_sc[...] = jnp.full(m_sc.shape, -jnp.inf, F32)
            l_sc[...] = jnp.zeros_like(l_sc)
            acc_sc[...] = jnp.zeros_like(acc_sc)

        lane = lax.broadcasted_iota(jnp.int32, (1, LANES), 1)
        low = lane < HEAD_DIM

        def step(masked):
            vv = v_ref[...]
            for h in range(2):
                mine = low if h == 0 else jnp.logical_not(low)
                s = _dot_nt(q_ref[:, h * LANES:(h + 1) * LANES], k_ref[:, h * LANES:(h + 1) * LANES])
                if masked:
                    row = lax.broadcasted_iota(jnp.int32, (tq, tq), 0)
                    col = lax.broadcasted_iota(jnp.int32, (tq, tq), 1)
                    s = jnp.where(row >= col, s, -jnp.inf)
                m_prev = m_sc[h]
                m_new = jnp.maximum(m_prev, jnp.max(s, axis=1, keepdims=True))
                p = jnp.exp(s - jnp.tile(m_new, (1, tq // LANES)))
                alpha = jnp.exp(m_prev - m_new)
                l_sc[h] = alpha * l_sc[h] + jnp.sum(p, axis=1, keepdims=True)
                m_sc[h] = m_new
                pv = _dot(p.astype(BF16), jnp.where(mine, vv, jnp.zeros_like(vv)))
                acc_sc[...] = acc_sc[...] * jnp.where(mine, alpha, 1.0) + pv

        @pl.when(j < i)
        def _():
            step(False)

        @pl.when(j == i)
        def _():
            step(True)
            l = jnp.where(low, l_sc[0], l_sc[1])
            m = jnp.where(low, m_sc[0], m_sc[1])
            o_ref[...] = acc_sc[...] / l
            lse_ref[...] = m + jnp.log(l)

    qspec = pl.BlockSpec((tq, 2 * LANES), lambda b, hp, i, j: (b * nq + i, hp))
    kspec = pl.BlockSpec((tq, 2 * LANES), lambda b, hp, i, j: (b * nq + jnp.minimum(i, j), hp))
    vspec = pl.BlockSpec((tq, LANES), lambda b, hp, i, j: (b * nq + jnp.minimum(i, j), hp))
    ospec = pl.BlockSpec((tq, LANES), lambda b, hp, i, j: (b * nq + i, hp))
    return pl.pallas_call(
        body, name="flash_fwd", grid=(bsz, HEAD_PAIRS, nq, nq),
        in_specs=[qspec, kspec, vspec], out_specs=[ospec, ospec],
        out_shape=[jax.ShapeDtypeStruct((t, ATTN_WIDTH), F32), jax.ShapeDtypeStruct((t, ATTN_WIDTH), F32)],
        scratch_shapes=[pltpu.VMEM((2, tq, LANES), F32), pltpu.VMEM((2, tq, LANES), F32), pltpu.VMEM((tq, LANES), F32)],
        compiler_params=_params("arbitrary", "arbitrary", "arbitrary", "arbitrary"),
    )(qp, kp, v)


def _flash_bwd(qp, kp, v, o, do, lse, bsz, seq):
    tq = ATTN_BLOCK
    nq = seq // tq
    t = bsz * seq

    def body(q_ref, k_ref, v_ref, o_ref, do_ref, lse_ref, dq_ref, dk_ref, dv_ref, dk_acc, dv_acc):
        j = pl.program_id(2)
        i = pl.program_id(3)

        @pl.when(jnp.logical_and(j == 0, i == 0))
        def _():
            dq_ref[...] = jnp.zeros_like(dq_ref)

        @pl.when(i == 0)
        def _():
            dk_acc[...] = jnp.zeros_like(dk_acc)
            dv_acc[...] = jnp.zeros_like(dv_acc)

        lane = lax.broadcasted_iota(jnp.int32, (1, LANES), 1)
        low = lane < HEAD_DIM

        def step(masked):
            dov = do_ref[...]
            dd = dov * o_ref[...]
            dob = dov.astype(BF16)
            vv = v_ref[...]
            lse_v = lse_ref[...]
            rows = pl.ds(pl.multiple_of(i * tq, tq), tq)
            for h in range(2):
                mine = low if h == 0 else jnp.logical_not(low)
                cols = pl.ds(h * LANES, LANES)
                qh = q_ref[:, cols]
                kh = k_ref[:, cols]
                s = _dot_nt(qh, kh)
                lse_h = jnp.where(mine, lse_v, pltpu.roll(lse_v, HEAD_DIM, 1))
                p = jnp.exp(s - jnp.tile(lse_h, (1, tq // LANES)))
                if masked:
                    row = lax.broadcasted_iota(jnp.int32, (tq, tq), 0)
                    col = lax.broadcasted_iota(jnp.int32, (tq, tq), 1)
                    p = jnp.where(row >= col, p, 0.0)
                delta = jnp.sum(jnp.where(mine, dd, 0.0), axis=1, keepdims=True)
                dp = _dot_nt(dob, jnp.where(mine, vv, jnp.zeros_like(vv)))
                ds = (p * (dp - delta)).astype(BF16)
                dv_acc[...] += jnp.where(mine, _dot_tn(p.astype(BF16), dob), 0.0)
                dk_acc[:, cols] += _dot_tn(ds, qh)
                dq_ref[rows, cols] += _dot(ds, kh)

        @pl.when(i > j)
        def _():
            step(False)

        @pl.when(i == j)
        def _():
            step(True)

        @pl.when(i == nq - 1)
        def _():
            dk_ref[...] = dk_acc[...]
            dv_ref[...] = dv_acc[...].astype(BF16)

    qspec = pl.BlockSpec((tq, 2 * LANES), lambda b, hp, j, i: (b * nq + jnp.maximum(i, j), hp))
    kspec = pl.BlockSpec((tq, 2 * LANES), lambda b, hp, j, i: (b * nq + j, hp))
    vspec = pl.BlockSpec((tq, LANES), lambda b, hp, j, i: (b * nq + j, hp))
    ospec = pl.BlockSpec((tq, LANES), lambda b, hp, j, i: (b * nq + jnp.maximum(i, j), hp))
    dqspec = pl.BlockSpec((seq, 2 * LANES), lambda b, hp, j, i: (b, hp))
    return pl.pallas_call(
        body, name="flash_bwd", grid=(bsz, HEAD_PAIRS, nq, nq),
        in_specs=[qspec, kspec, vspec, ospec, ospec, ospec], out_specs=[dqspec, kspec, vspec],
        out_shape=[jax.ShapeDtypeStruct((t, N_HEADS * LANES), F32), jax.ShapeDtypeStruct((t, N_HEADS * LANES), F32),
                   jax.ShapeDtypeStruct((t, ATTN_WIDTH), BF16)],
        scratch_shapes=[pltpu.VMEM((tq, 2 * LANES), F32), pltpu.VMEM((tq, LANES), F32)],
        compiler_params=_params("arbitrary", "arbitrary", "arbitrary", "arbitrary"),
    )(qp, kp, v, o, do, lse)


def _mix_out_fwd(o, y_pool, x, gain, w_out):
    t, d = x.shape
    tm = 512
    pw, aw = POOL_WIDTH, ATTN_WIDTH

    def body(o_ref, yp_ref, x_ref, g_ref, w_ref, ycat_ref, y_ref):
        ov = o_ref[...]
        ya = (ov * _rms_scale(ov) * g_ref[...]).astype(BF16)
        ycat = jnp.concatenate([yp_ref[...], ya], axis=1)
        ycat_ref[...] = ycat
        y_ref[...] = x_ref[...] + _dot(ycat, w_ref[...])

    return pl.pallas_call(
        body, name="mix_out_fwd", grid=(t // tm,),
        in_specs=[_rows(tm, aw), _rows(tm, pw), _rows(tm, d), _resident((1, aw)), _resident((pw + aw, d))],
        out_specs=[_rows(tm, pw + aw), _rows(tm, d)],
        out_shape=[jax.ShapeDtypeStruct((t, pw + aw), BF16), jax.ShapeDtypeStruct((t, d), F32)],
        compiler_params=_params("arbitrary"),
    )(o, y_pool, x, gain, w_out)


def _mix_out_bwd(dx, o, gain, w_out):
    t, d = dx.shape
    tm = 512
    pw, aw = POOL_WIDTH, ATTN_WIDTH

    def body(dx_ref, o_ref, g_ref, w_ref, dxb_ref, dyp_ref, do_ref, dg_ref):
        dxb = dx_ref[...].astype(BF16)
        dxb_ref[...] = dxb
        dyp_ref[...] = _dot_nt(dxb, w_ref[pl.ds(0, pw), :])
        dya = _dot_nt(dxb, w_ref[pl.ds(pw, aw), :])
        do, dgain = _rms_bwd(dya, o_ref[...], g_ref[...])
        do_ref[...] = do

        @pl.when(pl.program_id(0) == 0)
        def _():
            dg_ref[...] = jnp.zeros_like(dg_ref)

        dg_ref[...] += dgain

    return pl.pallas_call(
        body, name="mix_out_bwd", grid=(t // tm,),
        in_specs=[_rows(tm, d), _rows(tm, aw), _resident((1, aw)), _resident((pw + aw, d))],
        out_specs=[_rows(tm, d), _rows(tm, pw), _rows(tm, aw), pl.BlockSpec((1, aw), lambda i: (0, 0))],
        out_shape=[jax.ShapeDtypeStruct((t, d), BF16), jax.ShapeDtypeStruct((t, pw), F32),
                   jax.ShapeDtypeStruct((t, aw), F32), jax.ShapeDtypeStruct((1, aw), F32)],
        compiler_params=_params("arbitrary"),
    )(dx, o, gain, w_out)


def _mix_in_bwd(dpv, dq, dk, dv, df, x, dx_res, gain, w_in_t):
    t, d = x.shape
    tm = 512
    pw, aw = POOL_WIDTH, ATTN_WIDTH

    def body(dpv_ref, dq_ref, dk_ref, dv_ref, df_ref, x_ref, dxr_ref, g_ref, w_ref, dh_ref, dx_ref, dg_ref):
        dh = jnp.concatenate([dpv_ref[...], dq_ref[...], dk_ref[...], dv_ref[...], df_ref[...]], axis=1)
        dh_ref[...] = dh
        dhm = _dot(dh, w_ref[...])
        dx, dgain = _rms_bwd(dhm, x_ref[...], g_ref[...])
        dx_ref[...] = dxr_ref[...] + dx

        @pl.when(pl.program_id(0) == 0)
        def _():
            dg_ref[...] = jnp.zeros_like(dg_ref)

        dg_ref[...] += dgain

    return pl.pallas_call(
        body, name="mix_in_bwd", grid=(t // tm,),
        in_specs=[_rows(tm, pw), _rows(tm, aw), _rows(tm, aw), _rows(tm, aw), _rows(tm, LANES), _rows(tm, d),
                  _rows(tm, d), _resident((1, d)), _resident((MIX_PAD, d))],
        out_specs=[_rows(tm, MIX_PAD), _rows(tm, d), pl.BlockSpec((1, d), lambda i: (0, 0))],
        out_shape=[jax.ShapeDtypeStruct((t, MIX_PAD), BF16), jax.ShapeDtypeStruct((t, d), F32),
                   jax.ShapeDtypeStruct((1, d), F32)],
        compiler_params=_params("arbitrary"),
    )(dpv, dq, dk, dv, df, x, dx_res, gain, w_in_t)


MESH_IDS = pl.DeviceIdType.MESH


def _me():
    return lax.axis_index("x"), lax.axis_index("y"), lax.axis_index("c")


def _peer(x, y, c, p):
    px = 1 - x if p & 4 else x
    py = 1 - y if p & 2 else y
    pc = 1 - c if p & 1 else c
    return (px, py, pc), 4 * px + 2 * py + pc


HBM_SPEC = pl.BlockSpec(memory_space=pltpu.HBM)
SEM_SPEC = pl.BlockSpec(memory_space=pltpu.SEMAPHORE)
SPLIT_COPY = pltpu.CompilerParams(has_side_effects=pltpu.SideEffectType.DATAFLOW_SIDE_EFFECTING)
PEERS = N_DEV - 1


def _hbm(a):
    return pltpu.with_memory_space_constraint(a, pltpu.HBM)


def _row_block(ref, dev, rows):
    return ref.at[pl.ds(pl.multiple_of(dev * rows, BF16_ROWS), rows)]


def _copy_ends(gather, src, land, me, peer_id):
    if gather:
        rows = src.shape[0]
        return src, _row_block(land, me, rows), _row_block(land, peer_id, rows), src, _row_block(land, me, rows)
    rows = src.shape[0] // N_DEV
    return (_row_block(src, peer_id, rows), land.at[me], land.at[peer_id], _row_block(src, me, rows), land.at[me])


def _land_shape(gather, s):
    return (N_DEV * s.shape[0], s.shape[1]) if gather else (N_DEV, s.shape[0] // N_DEV, s.shape[1])


def _copies_start(groups, gather, name):
    flat = [s for g in groups for s in g]
    n, ng = len(flat), len(groups)
    lands = [lax.empty(_land_shape(gather, s), s.dtype) for s in flat]

    def body(*refs):
        ins, lnd = refs[:n], refs[n:2 * n]
        sems = refs[2 * n:2 * n + 2 * ng]
        token = refs[-1]
        x, y, c = _me()
        me = 4 * x + 2 * y + c
        w = 0
        for gi, g in enumerate(groups):
            for k in range(len(g)):
                for p in range(1, N_DEV):
                    peer, peer_id = _peer(x, y, c, p)
                    src, dst, _, _, _ = _copy_ends(gather, ins[w], lnd[w], me, peer_id)
                    pltpu.make_async_remote_copy(src, dst, sems[2 * gi].at[k * PEERS + p - 1],
                                                 sems[2 * gi + 1].at[k * PEERS + p - 1], device_id=peer,
                                                 device_id_type=MESH_IDS).start()
                w += 1
        token[...] = jnp.zeros_like(token)

    sem_shapes = []
    for g in groups:
        sem_shapes += [pltpu.SemaphoreType.DMA((len(g) * PEERS,))] * 2
    out = pl.pallas_call(
        body, name=name,
        out_shape=(*sem_shapes, *[pltpu.HBM(s.shape, s.dtype) for s in flat],
                   *[pltpu.HBM(l.shape, l.dtype) for l in lands], jax.ShapeDtypeStruct((8, LANES), F32)),
        in_specs=[HBM_SPEC] * (2 * n),
        out_specs=(*[SEM_SPEC] * (2 * ng), *[HBM_SPEC] * (2 * n), pl.BlockSpec(memory_space=pltpu.VMEM)),
        input_output_aliases={i: 2 * ng + i for i in range(2 * n)},
        compiler_params=SPLIT_COPY,
    )(*[_hbm(s) for s in flat], *[_hbm(l) for l in lands])
    sems, thru, token = out[:2 * ng], out[2 * ng:2 * ng + 2 * n], out[-1]
    res, w = [], 0
    for gi, g in enumerate(groups):
        res.append((sems[2 * gi], sems[2 * gi + 1], list(thru[w:w + len(g)]), list(thru[n + w:n + w + len(g)])))
        w += len(g)
    return res, token


def _copies_wait(started, gather, after, name):
    send, recv, srcs, lands = started
    n = len(srcs)

    def body(*refs):
        ins, lnd = refs[:n], refs[n:2 * n]
        send_sems, recv_sems = refs[2 * n], refs[2 * n + 1]
        local_sems = refs[-1]
        x, y, c = _me()
        me = 4 * x + 2 * y + c
        local = []
        for w in range(n):
            _, _, _, own, place = _copy_ends(gather, ins[w], lnd[w], me, me)
            local.append(pltpu.make_async_copy(own, place, local_sems.at[w]))
            local[-1].start()
        for w in range(n):
            for p in range(1, N_DEV):
                peer, peer_id = _peer(x, y, c, p)
                src, _, arrival, _, _ = _copy_ends(gather, ins[w], lnd[w], me, peer_id)
                cp = pltpu.make_async_remote_copy(src, arrival, send_sems.at[w * PEERS + p - 1],
                                                  recv_sems.at[w * PEERS + p - 1], device_id=peer,
                                                  device_id_type=MESH_IDS)
                cp.wait_send()
                cp.wait_recv()
        for cp in local:
            cp.wait()

    out = pl.pallas_call(
        body, name=name,
        out_shape=(*[pltpu.HBM(s.shape, s.dtype) for s in srcs], *[pltpu.HBM(l.shape, l.dtype) for l in lands]),
        in_specs=[HBM_SPEC] * (2 * n) + [SEM_SPEC, SEM_SPEC, pl.BlockSpec(memory_space=pl.ANY)],
        out_specs=[HBM_SPEC] * (2 * n),
        input_output_aliases={i: i for i in range(2 * n)},
        scratch_shapes=[pltpu.SemaphoreType.DMA((n,))],
        compiler_params=SPLIT_COPY,
    )(*srcs, *lands, send, recv, after)
    return list(out[n:])


def _sum_slots(parts, name):
    _, rows, d = parts.shape

    def body(p_ref, o_ref):
        acc = p_ref[0].astype(F32)
        for dev in range(1, N_DEV):
            acc = acc + p_ref[dev].astype(F32)
        o_ref[...] = acc

    return pl.pallas_call(
        body, name=name, grid=(1,),
        in_specs=[pl.BlockSpec((N_DEV, rows, d), lambda i: (0, 0, 0))],
        out_specs=pl.BlockSpec((rows, d), lambda i: (0, 0)), out_shape=jax.ShapeDtypeStruct((rows, d), F32),
        compiler_params=_params("arbitrary"),
    )(parts)


def _all_reduce_small(part):
    rows, d = part.shape

    def body(p_ref, o_ref, slots, send_sems, recv_sems):
        x, y, c = _me()
        me = 4 * x + 2 * y + c
        slots[me] = p_ref[...]
        for p in range(1, N_DEV):
            peer, _ = _peer(x, y, c, p)
            pltpu.make_async_remote_copy(p_ref, slots.at[me], send_sems.at[p - 1], recv_sems.at[p - 1],
                                         device_id=peer, device_id_type=MESH_IDS).start()
        for p in range(1, N_DEV):
            peer, peer_id = _peer(x, y, c, p)
            cp = pltpu.make_async_remote_copy(p_ref, slots.at[peer_id], send_sems.at[p - 1], recv_sems.at[p - 1],
                                              device_id=peer, device_id_type=MESH_IDS)
            cp.wait_send()
            cp.wait_recv()
        acc = slots[0]
        for dev in range(1, N_DEV):
            acc = acc + slots[dev]
        o_ref[...] = acc

    vmem = pl.BlockSpec(memory_space=pltpu.VMEM)
    return pl.pallas_call(
        body, name="reduce_small_grads", in_specs=[vmem], out_specs=vmem,
        out_shape=jax.ShapeDtypeStruct((rows, d), F32),
        scratch_shapes=[pltpu.VMEM((N_DEV, rows, d), F32), pltpu.SemaphoreType.DMA((N_DEV - 1,)),
                        pltpu.SemaphoreType.DMA((N_DEV - 1,))],
        compiler_params=pltpu.CompilerParams(has_side_effects=True, vmem_limit_bytes=VMEM_LIMIT),
    )(part)


def _adamw(w, g, m, v, name):
    r, c = w.shape

    def body(w_ref, g_ref, m_ref, v_ref, d_ref, nm_ref, nv_ref):
        gv = g_ref[...]
        nm = ADAM_B1 * m_ref[...] + (1.0 - ADAM_B1) * gv
        nv = ADAM_B2 * v_ref[...] + (1.0 - ADAM_B2) * (gv * gv)
        m_hat = nm / (1.0 - ADAM_B1 ** ADAM_STEP)
        v_hat = nv / (1.0 - ADAM_B2 ** ADAM_STEP)
        d_ref[...] = -ADAM_LR * (m_hat / (jnp.sqrt(v_hat) + ADAM_EPS) + ADAM_WD * w_ref[...])
        nm_ref[...] = nm
        nv_ref[...] = nv

    tr = r // 4 if r % 32 == 0 else r
    spec = pl.BlockSpec((tr, c), lambda i: (i, 0))
    shape = jax.ShapeDtypeStruct((r, c), F32)
    return pl.pallas_call(
        body, name=name, grid=(r // tr,), in_specs=[spec] * 4, out_specs=[spec] * 3, out_shape=[shape] * 3,
        compiler_params=_params("arbitrary"),
    )(w, g, m, v)


def _pad_rows(a, rows):
    return jnp.pad(a, ((0, rows - a.shape[0]), (0, 0)))


def _row1(vec, width=D_MODEL):
    return jnp.pad(vec.reshape(1, -1), ((0, 0), (0, width - vec.shape[-1])))


SMALL_NAMES = ("ffn1_norm", "mix_norm", "ffn2_norm", "b_forget", "pool_scale", "q_norm", "k_norm", "out_norm_pool",
               "out_norm_attn", "pool_w")


def _pack_small(vals):
    rows = [_row1(vals[n].reshape(-1)) for n in SMALL_NAMES[:-1]]
    rows.append(vals["pool_w"].reshape(-1, D_MODEL))
    packed = jnp.concatenate(rows, axis=0)
    return _pad_rows(packed, SMALL_ROWS)


def _unpack_small(packed, like):
    out = {}
    for i, n in enumerate(SMALL_NAMES[:-1]):
        size = like[n].size
        out[n] = packed[i, :size].reshape(like[n].shape)
    first = len(SMALL_NAMES) - 1
    out["pool_w"] = packed[first:first + like["pool_w"].size // D_MODEL].reshape(like["pool_w"].shape)
    return out


def kernel(x, ffn1_norm, ffn1_w_gate, ffn1_w_up, ffn1_w_down, mix_norm, w_in, b_forget, pool_w, pool_scale, q_norm, k_norm, out_norm_pool, out_norm_attn, w_out, ffn2_norm, ffn2_w_gate, ffn2_w_up, ffn2_w_down, loss_target, m_ffn1_norm, m_ffn1_w_gate, m_ffn1_w_up, m_ffn1_w_down, m_mix_norm, m_w_in, m_b_forget, m_pool_w, m_pool_scale, m_q_norm, m_k_norm, m_out_norm_pool, m_out_norm_attn, m_w_out, m_ffn2_norm, m_ffn2_w_gate, m_ffn2_w_up, m_ffn2_w_down, v_ffn1_norm, v_ffn1_w_gate, v_ffn1_w_up, v_ffn1_w_down, v_mix_norm, v_w_in, v_b_forget, v_pool_w, v_pool_scale, v_q_norm, v_k_norm, v_out_norm_pool, v_out_norm_attn, v_w_out, v_ffn2_norm, v_ffn2_w_gate, v_ffn2_w_up, v_ffn2_w_down):
    bsz, seq, d = x.shape
    t = bsz * seq
    x0 = x.reshape(t, d)
    target = loss_target.reshape(t, d)
    in_rows = -(-w_in.shape[1] // BF16_ROWS) * BF16_ROWS

    slabs = [s.astype(BF16) for s in (ffn1_w_gate.T, ffn1_w_up.T, ffn1_w_down, _pad_rows(w_in.T, in_rows), w_out,
                                       ffn2_w_gate.T, ffn2_w_up.T, ffn2_w_down)]
    gathers, started = _copies_start([slabs[0:3], slabs[3:4], slabs[4:5], slabs[5:8]], True, "gather_start")

    g1, gm, g2 = ffn1_norm.reshape(1, d), mix_norm.reshape(1, d), ffn2_norm.reshape(1, d)
    bf_row = _row1(b_forget, LANES)
    gq = jnp.tile(q_norm, N_HEADS).reshape(1, ATTN_WIDTH)
    gk = jnp.tile(k_norm, N_HEADS).reshape(1, ATTN_WIDTH)
    scale_row = pool_scale.reshape(1, POOL_WIDTH)
    gp, ga = out_norm_pool.reshape(1, POOL_WIDTH), out_norm_attn.reshape(1, ATTN_WIDTH)

    wg1, wu1, wd1 = _copies_wait(gathers[0], True, started, "gather_wait_ffn1")
    h1, a1, b1, s1, x1 = _ffn_fwd(x0, g1, wg1, wu1, wd1, None, "ffn1_fwd")
    (win_g,) = _copies_wait(gathers[1], True, x1, "gather_wait_w_in")
    win_cols = win_g.reshape(N_DEV, in_rows, d)[:, :w_in.shape[1]].reshape(MIX_COLS, d)
    win_t = _pad_rows(win_cols, MIX_PAD)
    hm, pv, q, k, v, f = _mix_in_fwd(x1, gm, win_t)
    pooled, mixed, y_pool = _pool_fwd(pv, pool_w, scale_row, gp, bsz, seq)
    qp, kp = _attn_prep_fwd(q, k, f, bf_row, gq, gk, bsz, seq)
    o, lse = _flash_fwd(qp, kp, v, bsz, seq)
    (wout,) = _copies_wait(gathers[2], True, o, "gather_wait_w_out")
    ycat, x2 = _mix_out_fwd(o, y_pool, x1, ga, wout)
    wg2, wu2, wd2 = _copies_wait(gathers[3], True, x2, "gather_wait_ffn2")
    h2, a2, b2, s2, dx3, loss_part = _ffn_fwd(x2, g2, wg2, wu2, wd2, target, "ffn2_fwd")

    da2, db2, dyh2, dx2, dg2 = _ffn_bwd(dx3, x2, g2, a2, b2, wg2, wu2, wd2, "ffn2_bwd")
    dwg2, dwu2, dwd2 = _ffn_wgrad(da2, db2, s2, h2, dyh2, "ffn2_wgrad")
    (sent_ffn2,), tok = _copies_start([[dwg2, dwu2, dwd2]], False, "exchange_start_ffn2")
    dx2b, dy_pool, do, dga = _mix_out_bwd(dx2, o, ga + tok[0, 0], wout)
    dwout = _wgrad(ycat, dx2b, "w_out_wgrad")
    (sent_out,), tok = _copies_start([[dwout]], False, "exchange_start_w_out")
    dqp, dkp, dv = _flash_bwd(qp, kp, v, o, do, lse, bsz, seq)
    dq, dk, df, dgq, dgk, dbf = _attn_prep_bwd(dqp, dkp, q, k, f, bf_row + tok[0, 0], gq, gk, bsz, seq)
    dpv, dpool_w, dscale, dgp = _pool_bwd(dy_pool, mixed, pooled, pool_w, scale_row, gp, bsz, seq)
    dhcat, dx1, dgm = _mix_in_bwd(dpv, dq, dk, dv, df, x1, dx2, gm, win_t)
    dwin = _wgrad(dhcat, hm, "w_in_wgrad")
    dwin_blocks = jnp.pad(dwin[:MIX_COLS].reshape(N_DEV, w_in.shape[1], d), ((0, 0), (0, in_rows - w_in.shape[1]), (0, 0)))
    (sent_in,), tok = _copies_start([[dwin_blocks.reshape(N_DEV * in_rows, d)]], False, "exchange_start_w_in")
    da1, db1, dyh1, dx0, dg1 = _ffn_bwd(dx1, x0, g1 + tok[0, 0], a1, b1, wg1, wu1, wd1, "ffn1_bwd")
    dwg1, dwu1, dwd1 = _ffn_wgrad(da1, db1, s1, h1, dyh1, "ffn1_wgrad")
    (sent_ffn1,), tok = _copies_start([[dwg1, dwu1, dwd1]], False, "exchange_start_ffn1")

    fold = lambda g: g.reshape(N_HEADS, HEAD_DIM).sum(axis=0)
    small_like = dict(ffn1_norm=ffn1_norm, mix_norm=mix_norm, ffn2_norm=ffn2_norm, b_forget=b_forget,
                      pool_scale=pool_scale, q_norm=q_norm, k_norm=k_norm, out_norm_pool=out_norm_pool,
                      out_norm_attn=out_norm_attn, pool_w=pool_w)
    small_part = dict(ffn1_norm=dg1, mix_norm=dgm, ffn2_norm=dg2, b_forget=dbf[0, :N_HEADS], pool_scale=dscale,
                      q_norm=fold(dgq), k_norm=fold(dgk), out_norm_pool=dgp, out_norm_attn=dga, pool_w=dpool_w)
    small_sum = _all_reduce_small(_pack_small(small_part) + tok[0, 0])
    grads = _unpack_small(small_sum, small_like)
    loss = lax.psum(loss_part[0, 0], ("x", "y", "c"))

    weights = dict(ffn1_norm=ffn1_norm, ffn1_w_gate=ffn1_w_gate, ffn1_w_up=ffn1_w_up, ffn1_w_down=ffn1_w_down,
                   mix_norm=mix_norm, w_in=w_in, b_forget=b_forget, pool_w=pool_w, pool_scale=pool_scale,
                   q_norm=q_norm, k_norm=k_norm, out_norm_pool=out_norm_pool, out_norm_attn=out_norm_attn,
                   w_out=w_out, ffn2_norm=ffn2_norm, ffn2_w_gate=ffn2_w_gate, ffn2_w_up=ffn2_w_up,
                   ffn2_w_down=ffn2_w_down)
    m_in = dict(ffn1_norm=m_ffn1_norm, ffn1_w_gate=m_ffn1_w_gate, ffn1_w_up=m_ffn1_w_up, ffn1_w_down=m_ffn1_w_down,
                mix_norm=m_mix_norm, w_in=m_w_in, b_forget=m_b_forget, pool_w=m_pool_w, pool_scale=m_pool_scale,
                q_norm=m_q_norm, k_norm=m_k_norm, out_norm_pool=m_out_norm_pool, out_norm_attn=m_out_norm_attn,
                w_out=m_w_out, ffn2_norm=m_ffn2_norm, ffn2_w_gate=m_ffn2_w_gate, ffn2_w_up=m_ffn2_w_up,
                ffn2_w_down=m_ffn2_w_down)
    v_in = dict(ffn1_norm=v_ffn1_norm, ffn1_w_gate=v_ffn1_w_gate, ffn1_w_up=v_ffn1_w_up, ffn1_w_down=v_ffn1_w_down,
                mix_norm=v_mix_norm, w_in=v_w_in, b_forget=v_b_forget, pool_w=v_pool_w, pool_scale=v_pool_scale,
                q_norm=v_q_norm, k_norm=v_k_norm, out_norm_pool=v_out_norm_pool, out_norm_attn=v_out_norm_attn,
                w_out=v_w_out, ffn2_norm=v_ffn2_norm, ffn2_w_gate=v_ffn2_w_gate, ffn2_w_up=v_ffn2_w_up,
                ffn2_w_down=v_ffn2_w_down)
    delta, new_m, new_v = {}, {}, {}
    small_d, small_m, small_v = _adamw(_pack_small(weights), small_sum, _pack_small(m_in), _pack_small(v_in),
                                       "adamw_small")
    delta.update(_unpack_small(small_d, small_like))
    new_m.update(_unpack_small(small_m, small_like))
    new_v.update(_unpack_small(small_v, small_like))

    after = small_v
    plan = ((sent_ffn2, "ffn2", ("ffn2_w_gate", "ffn2_w_up", "ffn2_w_down")), (sent_out, "w_out", ("w_out",)),
            (sent_in, "w_in", ("w_in",)), (sent_ffn1, "ffn1", ("ffn1_w_gate", "ffn1_w_up", "ffn1_w_down")))
    for sent, tag, names in plan:
        parts = _copies_wait(sent, False, after, f"exchange_wait_{tag}")
        for n, part in zip(names, parts):
            total = _sum_slots(part, f"sum_grads_{n}")
            if n == "w_in":
                total = total[:w_in.shape[1]]
            grads[n] = total if n in ("ffn1_w_down", "ffn2_w_down", "w_out") else total.T
            delta[n], new_m[n], new_v[n] = _adamw(weights[n], grads[n], m_in[n], v_in[n], f"adamw_{n}")
            after = new_v[n]

    order = ("ffn1_norm", "ffn1_w_gate", "ffn1_w_up", "ffn1_w_down", "mix_norm", "w_in", "b_forget", "pool_w",
             "pool_scale", "q_norm", "k_norm", "out_norm_pool", "out_norm_attn", "w_out", "ffn2_norm", "ffn2_w_gate",
             "ffn2_w_up", "ffn2_w_down")
    return (loss, dx0.reshape(bsz, seq, d), *[grads[n] for n in order], *[delta[n] for n in order],
            *[new_m[n] for n in order], *[new_v[n] for n in order])
```

```python
import functools

import jax
import jax.numpy as jnp
from jax import lax
from jax.experimental import pallas as pl
from jax.experimental.pallas import tpu as pltpu

F32 = jnp.float32
BF16 = jnp.bfloat16

EPS = 1e-6
D_MODEL = 1024
D_FF = 2816
N_HEADS = 8
HEAD_DIM = 64
POOL_WIDTH = 512
ATTN_WIDTH = 512
POOL_GROUPS = 4
POOL_GROUP_DIM = 128
POOL_WINDOWS = (2, 4, 8, 16)
POOL_HALO = 16
MIX_COLS = POOL_WIDTH + 3 * ATTN_WIDTH + N_HEADS
MIX_PAD = POOL_WIDTH + 3 * ATTN_WIDTH + 128
N_DEV = 8
BF16_ROWS = 16
LANES = 128
VMEM_LIMIT = 56 * 1024 * 1024

ADAM_LR = 0.001
ADAM_B1 = 0.9
ADAM_B2 = 0.999
ADAM_EPS = 1e-08
ADAM_WD = 0.01
ADAM_STEP = 10

SMALL_ROWS = 80


def _params(*sem):
    return pltpu.CompilerParams(dimension_semantics=sem, vmem_limit_bytes=VMEM_LIMIT)


def _dot(a, b):
    return jnp.dot(a, b, preferred_element_type=F32)


def _dot_nt(a, b):
    return lax.dot_general(a, b, (((1,), (1,)), ((), ())), preferred_element_type=F32)


def _dot_tn(a, b):
    return lax.dot_general(a, b, (((0,), (0,)), ((), ())), preferred_element_type=F32)


def _resident(shape):
    return pl.BlockSpec(shape, lambda *_: (0,) * len(shape), pipeline_mode=pl.Buffered(1))


def _rows(tm, width):
    return pl.BlockSpec((tm, width), lambda i: (i, 0))


def _rms_scale(x):
    return lax.rsqrt(jnp.mean(x * x, axis=-1, keepdims=True) + EPS)


def _rms_bwd(dh, x, gain):
    r = _rms_scale(x)
    n = x * r
    dgain = jnp.sum(dh * n, axis=0, keepdims=True)
    dn = dh * gain
    dx = r * (dn - n * jnp.mean(dn * n, axis=-1, keepdims=True))
    return dx, dgain


def _split3(x):
    hi = x.astype(BF16)
    r1 = x - hi.astype(F32)
    mid = r1.astype(BF16)
    lo = (r1 - mid.astype(F32)).astype(BF16)
    return hi, mid, lo


def _split2(x):
    hi = x.astype(BF16)
    return hi, (x - hi.astype(F32)).astype(BF16)


FF_CHUNK = 256


def _ffn_fwd(x, gain, wg_t, wu_t, wd, target, name):
    t, d = x.shape
    f = wd.shape[0]
    tm = 256
    with_loss = target is not None

    def body(*refs):
        if with_loss:
            x_ref, g_ref, wg_ref, wu_ref, wd_ref, t_ref, h_ref, a_ref, b_ref, s_ref, dy_ref, loss_ref = refs
        else:
            x_ref, g_ref, wg_ref, wu_ref, wd_ref, h_ref, a_ref, b_ref, s_ref, y_ref = refs
        xv = x_ref[...]
        h = (xv * _rms_scale(xv) * g_ref[...]).astype(BF16)
        h_ref[...] = h
        acc = jnp.zeros((tm, d), F32)
        for c in range(f // FF_CHUNK):
            sl = pl.ds(c * FF_CHUNK, FF_CHUNK)
            a = _dot_nt(h, wg_ref[sl, :])
            b = _dot_nt(h, wu_ref[sl, :])
            s = (a * jax.nn.sigmoid(a) * b).astype(BF16)
            a_ref[:, sl] = a.astype(BF16)
            b_ref[:, sl] = b.astype(BF16)
            s_ref[:, sl] = s
            acc = acc + _dot(s, wd_ref[sl, :])
        y = xv + 0.5 * acc
        if with_loss:
            e = y - t_ref[...]
            dy_ref[...] = e * (1.0 / d)

            @pl.when(pl.program_id(0) == 0)
            def _():
                loss_ref[...] = jnp.zeros_like(loss_ref)

            part = jnp.sum(jnp.sum(e * e, axis=0, keepdims=True), axis=1, keepdims=True)
            loss_ref[...] += part * (0.5 / d)
        else:
            y_ref[...] = y

    saved_shapes = [
        jax.ShapeDtypeStruct((t, d), BF16),
        jax.ShapeDtypeStruct((t, f), BF16),
        jax.ShapeDtypeStruct((t, f), BF16),
        jax.ShapeDtypeStruct((t, f), BF16),
    ]
    saved_specs = [_rows(tm, d), _rows(tm, f), _rows(tm, f), _rows(tm, f)]
    in_specs = [_rows(tm, d), _resident((1, d)), _resident((f, d)), _resident((f, d)), _resident((f, d))]
    args = [x, gain, wg_t, wu_t, wd]
    if with_loss:
        in_specs.append(_rows(tm, d))
        args.append(target)
        out_shape = saved_shapes + [jax.ShapeDtypeStruct((t, d), F32), jax.ShapeDtypeStruct((1, 1), F32)]
        out_specs = saved_specs + [_rows(tm, d), pl.BlockSpec((1, 1), lambda i: (0, 0))]
    else:
        out_shape = saved_shapes + [jax.ShapeDtypeStruct((t, d), F32)]
        out_specs = saved_specs + [_rows(tm, d)]
    return pl.pallas_call(
        body, name=name, grid=(t // tm,), in_specs=in_specs, out_specs=out_specs, out_shape=out_shape,
        compiler_params=_params("arbitrary"),
    )(*args)


def _ffn_bwd(dy, x, gain, a, b, wg_t, wu_t, wd, name):
    t, d = x.shape
    f = wd.shape[0]
    tm = 256

    def body(dy_ref, x_ref, g_ref, a_ref, b_ref, wg_ref, wu_ref, wd_ref, da_ref, db_ref, dyh_ref, dx_ref, dg_ref):
        dyv = dy_ref[...]
        dyh = (0.5 * dyv).astype(BF16)
        dyh_ref[...] = dyh
        dh = jnp.zeros((tm, d), F32)
        for c in range(f // FF_CHUNK):
            sl = pl.ds(c * FF_CHUNK, FF_CHUNK)
            ds = _dot_nt(dyh, wd_ref[sl, :])
            av = a_ref[:, sl].astype(F32)
            bv = b_ref[:, sl].astype(F32)
            sig = jax.nn.sigmoid(av)
            da = (ds * bv * (sig * (1.0 + av * (1.0 - sig)))).astype(BF16)
            db = (ds * (av * sig)).astype(BF16)
            da_ref[:, sl] = da
            db_ref[:, sl] = db
            dh = dh + _dot(da, wg_ref[sl, :]) + _dot(db, wu_ref[sl, :])
        dx, dgain = _rms_bwd(dh, x_ref[...], g_ref[...])
        dx_ref[...] = dyv + dx

        @pl.when(pl.program_id(0) == 0)
        def _():
            dg_ref[...] = jnp.zeros_like(dg_ref)

        dg_ref[...] += dgain

    return pl.pallas_call(
        body, name=name, grid=(t // tm,),
        in_specs=[_rows(tm, d), _rows(tm, d), _resident((1, d)), _rows(tm, f), _rows(tm, f),
                  _resident((f, d)), _resident((f, d)), _resident((f, d))],
        out_specs=[_rows(tm, f), _rows(tm, f), _rows(tm, d), _rows(tm, d), pl.BlockSpec((1, d), lambda i: (0, 0))],
        out_shape=[jax.ShapeDtypeStruct((t, f), BF16), jax.ShapeDtypeStruct((t, f), BF16),
                   jax.ShapeDtypeStruct((t, d), BF16), jax.ShapeDtypeStruct((t, d), F32),
                   jax.ShapeDtypeStruct((1, d), F32)],
        compiler_params=_params("arbitrary"),
    )(dy, x, gain, a, b, wg_t, wu_t, wd)


def _ffn_wgrad(da, db, s, h, dyh, name):
    t, f = da.shape
    d = h.shape[1]
    tf = f // 2
    tk = 256
    nk = t // tk

    def body(da_ref, db_ref, s_ref, h_ref, dy_ref, og_ref, ou_ref, od_ref, acc_g, acc_u, acc_d):
        k = pl.program_id(1)

        @pl.when(k == 0)
        def _():
            acc_g[...] = jnp.zeros_like(acc_g)
            acc_u[...] = jnp.zeros_like(acc_u)
            acc_d[...] = jnp.zeros_like(acc_d)

        hv = h_ref[...]
        acc_g[...] += _dot_tn(da_ref[...], hv)
        acc_u[...] += _dot_tn(db_ref[...], hv)
        acc_d[...] += _dot_tn(s_ref[...], dy_ref[...])

        @pl.when(k == nk - 1)
        def _():
            og_ref[...] = acc_g[...].astype(BF16)
            ou_ref[...] = acc_u[...].astype(BF16)
            od_ref[...] = acc_d[...].astype(BF16)

    a_spec = pl.BlockSpec((tk, tf), lambda j, k: (k, j))
    b_spec = pl.BlockSpec((tk, d), lambda j, k: (k, 0))
    o_spec = pl.BlockSpec((tf, d), lambda j, k: (j, 0))
    o_shape = jax.ShapeDtypeStruct((f, d), BF16)
    return pl.pallas_call(
        body, name=name, grid=(f // tf, nk),
        in_specs=[a_spec, a_spec, a_spec, b_spec, b_spec],
        out_specs=[o_spec, o_spec, o_spec], out_shape=[o_shape, o_shape, o_shape],
        scratch_shapes=[pltpu.VMEM((tf, d), F32)] * 3,
        compiler_params=_params("arbitrary", "arbitrary"),
    )(da, db, s, h, dyh)


def _wgrad(a, b, name):
    t, n = a.shape
    d = b.shape[1]
    tk = 256
    nk = t // tk

    def body(a_ref, b_ref, o_ref, acc):
        k = pl.program_id(0)

        @pl.when(k == 0)
        def _():
            acc[...] = jnp.zeros_like(acc)

        acc[...] += _dot_tn(a_ref[...], b_ref[...])

        @pl.when(k == nk - 1)
        def _():
            o_ref[...] = acc[...].astype(BF16)

    return pl.pallas_call(
        body, name=name, grid=(nk,),
        in_specs=[_rows(tk, n), _rows(tk, d)],
        out_specs=pl.BlockSpec((n, d), lambda k: (0, 0)), out_shape=jax.ShapeDtypeStruct((n, d), BF16),
        scratch_shapes=[pltpu.VMEM((n, d), F32)],
        compiler_params=_params("arbitrary"),
    )(a, b)


def _mix_in_fwd(x, gain, w_in_t):
    t, d = x.shape
    tm = 512
    pw, aw = POOL_WIDTH, ATTN_WIDTH

    def body(x_ref, g_ref, w_ref, hm_ref, pv_ref, q_ref, k_ref, v_ref, f_ref):
        xv = x_ref[...]
        hm = (xv * _rms_scale(xv) * g_ref[...]).astype(BF16)
        hm_ref[...] = hm
        pv_ref[...] = _dot_nt(hm, w_ref[pl.ds(0, pw), :])
        q_ref[...] = _dot_nt(hm, w_ref[pl.ds(pw, aw), :])
        k_ref[...] = _dot_nt(hm, w_ref[pl.ds(pw + aw, aw), :])
        v_ref[...] = _dot_nt(hm, w_ref[pl.ds(pw + 2 * aw, aw), :]).astype(BF16)
        f_ref[...] = _dot_nt(hm, w_ref[pl.ds(pw + 3 * aw, LANES), :])

    return pl.pallas_call(
        body, name="mix_in_fwd", grid=(t // tm,),
        in_specs=[_rows(tm, d), _resident((1, d)), _resident((MIX_PAD, d))],
        out_specs=[_rows(tm, d), _rows(tm, pw), _rows(tm, aw), _rows(tm, aw), _rows(tm, aw), _rows(tm, LANES)],
        out_shape=[jax.ShapeDtypeStruct((t, d), BF16), jax.ShapeDtypeStruct((t, pw), F32),
                   jax.ShapeDtypeStruct((t, aw), F32), jax.ShapeDtypeStruct((t, aw), F32),
                   jax.ShapeDtypeStruct((t, aw), BF16), jax.ShapeDtypeStruct((t, LANES), F32)],
        compiler_params=_params("arbitrary"),
    )(x, gain, w_in_t)


def _pool_fwd(pv, pool_w, pool_scale, gain, bsz, seq):
    ts = 512
    ns = seq // ts
    pw = POOL_WIDTH

    def body(pv_ref, w_ref, sc_ref, g_ref, pooled_ref, mixed_ref, y_ref, ext):
        s = pl.program_id(1)

        @pl.when(s == 0)
        def _():
            ext[pl.ds(0, POOL_HALO), :] = jnp.zeros((POOL_HALO, pw), F32)

        p = pv_ref[...]
        ext[pl.ds(POOL_HALO, ts), :] = p
        pos = s * ts + lax.broadcasted_iota(jnp.int32, (ts, 1), 0)
        parts = []
        for g, w in enumerate(POOL_WINDOWS):
            lanes = pl.ds(g * POOL_GROUP_DIM, POOL_GROUP_DIM)
            win = ext[pl.ds(POOL_HALO, ts), lanes]
            for i in range(1, w):
                win = win + ext[pl.ds(POOL_HALO - i, ts), lanes]
            cnt = jnp.minimum(pos + 1, w).astype(F32)
            pooled = (win / cnt - ext[pl.ds(POOL_HALO, ts), lanes]).astype(BF16)
            pooled_ref[:, lanes] = pooled
            parts.append(_dot(pooled, w_ref[g].astype(BF16)))
        mixed = jnp.concatenate(parts, axis=1)
        mixed_ref[...] = mixed
        pm = mixed * sc_ref[...]
        y_ref[...] = (pm * _rms_scale(pm) * g_ref[...]).astype(BF16)
        ext[pl.ds(0, POOL_HALO), :] = p[ts - POOL_HALO:, :]

    blk = pl.BlockSpec((ts, pw), lambda b, s: (b * ns + s, 0))
    t = bsz * seq
    return pl.pallas_call(
        body, name="pool_fwd", grid=(bsz, ns),
        in_specs=[blk, pl.BlockSpec((POOL_GROUPS, POOL_GROUP_DIM, POOL_GROUP_DIM), lambda b, s: (0, 0, 0)),
                  pl.BlockSpec((1, pw), lambda b, s: (0, 0)), pl.BlockSpec((1, pw), lambda b, s: (0, 0))],
        out_specs=[blk, blk, blk],
        out_shape=[jax.ShapeDtypeStruct((t, pw), BF16), jax.ShapeDtypeStruct((t, pw), F32),
                   jax.ShapeDtypeStruct((t, pw), BF16)],
        scratch_shapes=[pltpu.VMEM((POOL_HALO + ts, pw), F32)],
        compiler_params=_params("arbitrary", "arbitrary"),
    )(pv, pool_w, pool_scale, gain)


def _pool_bwd(dy, mixed, pooled, pool_w, pool_scale, gain, bsz, seq):
    ts = 512
    ns = seq // ts
    pw = POOL_WIDTH

    def body(dy_ref, mixed_ref, pooled_ref, w_ref, sc_ref, g_ref, dpv_ref, dw_ref, dsc_ref, dg_ref, ext):
        b = pl.program_id(0)
        sr = pl.program_id(1)
        s = ns - 1 - sr

        @pl.when(jnp.logical_and(b == 0, sr == 0))
        def _():
            dw_ref[...] = jnp.zeros_like(dw_ref)
            dsc_ref[...] = jnp.zeros_like(dsc_ref)
            dg_ref[...] = jnp.zeros_like(dg_ref)

        @pl.when(sr == 0)
        def _():
            ext[pl.ds(ts, POOL_HALO), :] = jnp.zeros((POOL_HALO, pw), F32)

        mixed = mixed_ref[...]
        sc = sc_ref[...]
        dpm, dgain = _rms_bwd(dy_ref[...], mixed * sc, g_ref[...])
        dg_ref[...] += dgain
        dsc_ref[...] += jnp.sum(dpm * mixed, axis=0, keepdims=True)
        dmixed = (dpm * sc).astype(BF16)
        pos = s * ts + lax.broadcasted_iota(jnp.int32, (ts, 1), 0)
        dpooled = []
        for g, w in enumerate(POOL_WINDOWS):
            lanes = pl.ds(g * POOL_GROUP_DIM, POOL_GROUP_DIM)
            dm = dmixed[:, g * POOL_GROUP_DIM:(g + 1) * POOL_GROUP_DIM]
            dw_ref[g] += _dot_tn(pooled_ref[:, lanes], dm)
            dp = _dot_nt(dm, w_ref[g].astype(BF16))
            dpooled.append(dp)
            cnt = jnp.minimum(pos + 1, w).astype(F32)
            ext[pl.ds(0, ts), lanes] = dp / cnt
        for g, w in enumerate(POOL_WINDOWS):
            lanes = pl.ds(g * POOL_GROUP_DIM, POOL_GROUP_DIM)
            win = ext[pl.ds(0, ts), lanes]
            for i in range(1, w):
                win = win + ext[pl.ds(i, ts), lanes]
            dpv_ref[:, lanes] = (win - dpooled[g]).astype(BF16)
        head = ext[pl.ds(0, POOL_HALO), :]
        ext[pl.ds(ts, POOL_HALO), :] = head

    blk = pl.BlockSpec((ts, pw), lambda b, s: (b * ns + (ns - 1 - s), 0))
    vec = pl.BlockSpec((1, pw), lambda b, s: (0, 0))
    wspec = pl.BlockSpec((POOL_GROUPS, POOL_GROUP_DIM, POOL_GROUP_DIM), lambda b, s: (0, 0, 0))
    t = bsz * seq
    return pl.pallas_call(
        body, name="pool_bwd", grid=(bsz, ns),
        in_specs=[blk, blk, blk, wspec, vec, vec],
        out_specs=[blk, wspec, vec, vec],
        out_shape=[jax.ShapeDtypeStruct((t, pw), BF16),
                   jax.ShapeDtypeStruct((POOL_GROUPS, POOL_GROUP_DIM, POOL_GROUP_DIM), F32),
                   jax.ShapeDtypeStruct((1, pw), F32), jax.ShapeDtypeStruct((1, pw), F32)],
        scratch_shapes=[pltpu.VMEM((ts + POOL_HALO, pw), F32)],
        compiler_params=_params("arbitrary", "arbitrary"),
    )(dy, mixed, pooled, pool_w, pool_scale, gain)


AUX_ONE = 64
AUX_F = 67

ATTN_PREP_ROWS = 256


def _seg_ones(width, seg):
    r = lax.broadcasted_iota(jnp.int32, (width, width), 0) // seg
    c = lax.broadcasted_iota(jnp.int32, (width, width), 1) // seg
    return (r == c).astype(BF16)


def _tri_ones(n, lower):
    r = lax.broadcasted_iota(jnp.int32, (n, n), 0)
    c = lax.broadcasted_iota(jnp.int32, (n, n), 1)
    return ((r >= c) if lower else (r <= c)).astype(BF16)


def _place_pieces(first_lane):
    r = lax.broadcasted_iota(jnp.int32, (3 * LANES, N_HEADS * LANES), 0)
    c = lax.broadcasted_iota(jnp.int32, (3 * LANES, N_HEADS * LANES), 1)
    piece, head = r // LANES, r % LANES
    return jnp.logical_and(head < N_HEADS, c == head * LANES + first_lane + piece).astype(BF16)


def _head_sums(x, seg_ones):
    hi, lo = _split2(x)
    return _dot(hi, seg_ones) + _dot(lo, seg_ones)


def _log_sigmoid(x):
    return jnp.minimum(x, 0.0) - jnp.log(1.0 + jnp.exp(-jnp.abs(x)))


def _attn_prep_fwd(q, k, f, b_forget, q_gain, k_gain, bsz, seq):
    ts = ATTN_PREP_ROWS
    ns = seq // ts
    aw = ATTN_WIDTH
    t = bsz * seq
    seg = _seg_ones(aw, HEAD_DIM)
    tri = _tri_ones(ts, True)

    def body(q_ref, k_ref, f_ref, bf_ref, gq_ref, gk_ref, seg_ref, tri_ref, pq_ref, pk_ref, qp_ref, kp_ref, carry):
        s = pl.program_id(1)

        @pl.when(s == 0)
        def _():
            carry[...] = jnp.zeros_like(carry)

        logf = _log_sigmoid(f_ref[...] + bf_ref[...])
        hi, mid, lo = _split3(logf)
        tri_v = tri_ref[...]
        fc = _dot(tri_v, hi) + _dot(tri_v, mid) + _dot(tri_v, lo) + carry[pl.ds(0, 1), :]
        carry[pl.ds(0, 1), :] = fc[ts - 1:, :]
        pcs = jnp.concatenate(_split3(fc), axis=1)
        lane = lax.broadcasted_iota(jnp.int32, (1, LANES), 1)
        ones_q = jnp.logical_and(lane >= AUX_ONE, lane < AUX_ONE + 3).astype(F32)
        ones_k = jnp.logical_and(lane >= AUX_F, lane < AUX_F + 3).astype(F32)
        seg_v = seg_ref[...]

        def build(x_ref, g_ref, scale, out_ref, ones, place_ref, f_sign):
            xv = x_ref[...]
            r = lax.rsqrt(_head_sums(xv * xv, seg_v) * (1.0 / HEAD_DIM) + EPS)
            xn = xv * r * g_ref[...] * scale
            aux = _dot(pcs, place_ref[...]) * f_sign
            for h in range(N_HEADS):
                pair = xn[:, (h // 2) * LANES:(h // 2 + 1) * LANES]
                feat = pair if h % 2 == 0 else pltpu.roll(pair, HEAD_DIM, 1)
                aux_h = aux[:, h * LANES:(h + 1) * LANES] + ones
                out_ref[:, h * LANES:(h + 1) * LANES] = jnp.where(lane < HEAD_DIM, feat, aux_h).astype(BF16)

        build(q_ref, gq_ref, 0.125, qp_ref, ones_q, pq_ref, 1.0)
        build(k_ref, gk_ref, 1.0, kp_ref, ones_k, pk_ref, -1.0)

    blk = pl.BlockSpec((ts, aw), lambda b, s: (b * ns + s, 0))
    fblk = pl.BlockSpec((ts, LANES), lambda b, s: (b * ns + s, 0))
    oblk = pl.BlockSpec((ts, N_HEADS * LANES), lambda b, s: (b * ns + s, 0))
    const = lambda shape: pl.BlockSpec(shape, lambda b, s: (0, 0))
    return pl.pallas_call(
        body, name="attn_prep_fwd", grid=(bsz, ns),
        in_specs=[blk, blk, fblk, const((1, LANES)), const((1, aw)), const((1, aw)), const((aw, aw)), const((ts, ts)),
                  const((3 * LANES, N_HEADS * LANES)), const((3 * LANES, N_HEADS * LANES))],
        out_specs=[oblk, oblk],
        out_shape=[jax.ShapeDtypeStruct((t, N_HEADS * LANES), BF16)] * 2,
        scratch_shapes=[pltpu.VMEM((8, LANES), F32)],
        compiler_params=_params("arbitrary", "arbitrary"),
    )(q, k, f, b_forget, q_gain, k_gain, seg, tri, _place_pieces(AUX_F), _place_pieces(AUX_ONE))


def _attn_prep_bwd(dqp, dkp, q, k, f, b_forget, q_gain, k_gain, bsz, seq):
    ts = ATTN_PREP_ROWS
    ns = seq // ts
    aw = ATTN_WIDTH
    t = bsz * seq
    seg = _seg_ones(aw, HEAD_DIM)
    tri = _tri_ones(ts, False)

    def body(dqp_ref, dkp_ref, q_ref, k_ref, f_ref, bf_ref, gq_ref, gk_ref, seg_ref, tri_ref,
             dq_ref, dk_ref, df_ref, dgq_ref, dgk_ref, dbf_ref, carry):
        b = pl.program_id(0)
        sr = pl.program_id(1)

        @pl.when(jnp.logical_and(b == 0, sr == 0))
        def _():
            dgq_ref[...] = jnp.zeros_like(dgq_ref)
            dgk_ref[...] = jnp.zeros_like(dgk_ref)
            dbf_ref[...] = jnp.zeros_like(dbf_ref)

        @pl.when(sr == 0)
        def _():
            carry[...] = jnp.zeros_like(carry)

        lane = lax.broadcasted_iota(jnp.int32, (1, LANES), 1)
        seg_v = seg_ref[...]

        def norm_bwd(dp_ref, x_ref, g_ref, scale, dx_ref, dgain_ref):
            parts = []
            for j in range(N_HEADS // 2):
                even = dp_ref[:, (2 * j) * LANES:(2 * j + 1) * LANES]
                odd = dp_ref[:, (2 * j + 1) * LANES:(2 * j + 2) * LANES]
                parts.append(jnp.where(lane < HEAD_DIM, even, pltpu.roll(odd, HEAD_DIM, 1)))
            dxn = jnp.concatenate(parts, axis=1) * scale
            xv = x_ref[...]
            r = lax.rsqrt(_head_sums(xv * xv, seg_v) * (1.0 / HEAD_DIM) + EPS)
            n = xv * r
            dgain_ref[...] += jnp.sum(dxn * n, axis=0, keepdims=True)
            dn = dxn * g_ref[...]
            m = _head_sums(dn * n, seg_v) * (1.0 / HEAD_DIM)
            dx_ref[...] = (r * (dn - n * m)).astype(BF16)

        norm_bwd(dqp_ref, q_ref, gq_ref, 0.125, dq_ref, dgq_ref)
        norm_bwd(dkp_ref, k_ref, gk_ref, 1.0, dk_ref, dgk_ref)

        dfc = jnp.zeros((ts, LANES), F32)
        for h in range(N_HEADS):
            cols = pl.ds(h * LANES, LANES)
            both = jnp.where(lane == AUX_F, dqp_ref[:, cols], 0.0) - jnp.where(lane == AUX_ONE, dkp_ref[:, cols], 0.0)
            dfc = jnp.where(lane == h, jnp.sum(both, axis=1, keepdims=True), dfc)
        hi, mid, lo = _split3(dfc)
        tri_v = tri_ref[...]
        dlogf = _dot(tri_v, hi) + _dot(tri_v, mid) + _dot(tri_v, lo) + carry[pl.ds(0, 1), :]
        carry[pl.ds(0, 1), :] = dlogf[0:1, :]
        df = jnp.where(lane < N_HEADS, dlogf * jax.nn.sigmoid(-(f_ref[...] + bf_ref[...])), 0.0)
        df_ref[...] = df.astype(BF16)
        dbf_ref[...] += jnp.sum(df, axis=0, keepdims=True)

    rev = lambda b, s: (b * ns + (ns - 1 - s), 0)
    blk = pl.BlockSpec((ts, aw), rev)
    fblk = pl.BlockSpec((ts, LANES), rev)
    pblk = pl.BlockSpec((ts, N_HEADS * LANES), rev)
    const = lambda shape: pl.BlockSpec(shape, lambda b, s: (0, 0))
    return pl.pallas_call(
        body, name="attn_prep_bwd", grid=(bsz, ns),
        in_specs=[pblk, pblk, blk, blk, fblk, const((1, LANES)), const((1, aw)), const((1, aw)), const((aw, aw)),
                  const((ts, ts))],
        out_specs=[blk, blk, fblk, const((1, aw)), const((1, aw)), const((1, LANES))],
        out_shape=[jax.ShapeDtypeStruct((t, aw), BF16), jax.ShapeDtypeStruct((t, aw), BF16),
                   jax.ShapeDtypeStruct((t, LANES), BF16), jax.ShapeDtypeStruct((1, aw), F32),
                   jax.ShapeDtypeStruct((1, aw), F32), jax.ShapeDtypeStruct((1, LANES), F32)],
        scratch_shapes=[pltpu.VMEM((8, LANES), F32)],
        compiler_params=_params("arbitrary", "arbitrary"),
    )(dqp, dkp, q, k, f, b_forget, q_gain, k_gain, seg, tri)


ATTN_BLOCK = 512
HEAD_PAIRS = N_HEADS // 2


def _flash_fwd(qp, kp, v, bsz, seq):
    tq = ATTN_BLOCK
    nq = seq // tq
    t = bsz * seq

    def body(q_ref, k_ref, v_ref, o_ref, lse_ref, m_sc, l_sc, acc_sc):
        i = pl.program_id(2)
        j = pl.program_id(3)

        @pl.when(j == 0)
        def _():
            m_sc[...] = jnp.full(m_sc.shape, -jnp.inf, F32)
            l_sc[...] = jnp.zeros_like(l_sc)
            acc_sc[...] = jnp.zeros_like(acc_sc)

        lane = lax.broadcasted_iota(jnp.int32, (1, LANES), 1)
        low = lane < HEAD_DIM

        def step(masked):
            vv = v_ref[...]
            for h in range(2):
                mine = low if h == 0 else jnp.logical_not(low)
                s = _dot_nt(q_ref[:, h * LANES:(h + 1) * LANES], k_ref[:, h * LANES:(h + 1) * LANES])
                if masked:
                    row = lax.broadcasted_iota(jnp.int32, (tq, tq), 0)
                    col = lax.broadcasted_iota(jnp.int32, (tq, tq), 1)
                    s = jnp.where(row >= col, s, -jnp.inf)
                m_prev = m_sc[h]
                m_new = jnp.maximum(m_prev, jnp.max(s, axis=1, keepdims=True))
                p = jnp.exp(s - jnp.tile(m_new, (1, tq // LANES)))
                alpha = jnp.exp(m_prev - m_new)
                l_sc[h] = alpha * l_sc[h] + jnp.sum(p, axis=1, keepdims=True)
                m_sc[h] = m_new
                pv = _dot(p.astype(BF16), jnp.where(mine, vv, jnp.zeros_like(vv)))
                acc_sc[...] = acc_sc[...] * jnp.where(mine, alpha, 1.0) + pv

        @pl.when(j < i)
        def _():
            step(False)

        @pl.when(j == i)
        def _():
            step(True)
            l = jnp.where(low, l_sc[0], l_sc[1])
            m = jnp.where(low, m_sc[0], m_sc[1])
            o_ref[...] = acc_sc[...] / l
            lse_ref[...] = m + jnp.log(l)

    qspec = pl.BlockSpec((tq, 2 * LANES), lambda b, hp, i, j: (b * nq + i, hp))
    kspec = pl.BlockSpec((tq, 2 * LANES), lambda b, hp, i, j: (b * nq + jnp.minimum(i, j), hp))
    vspec = pl.BlockSpec((tq, LANES), lambda b, hp, i, j: (b * nq + jnp.minimum(i, j), hp))
    ospec = pl.BlockSpec((tq, LANES), lambda b, hp, i, j: (b * nq + i, hp))
    return pl.pallas_call(
        body, name="flash_fwd", grid=(bsz, HEAD_PAIRS, nq, nq),
        in_specs=[qspec, kspec, vspec], out_specs=[ospec, ospec],
        out_shape=[jax.ShapeDtypeStruct((t, ATTN_WIDTH), F32), jax.ShapeDtypeStruct((t, ATTN_WIDTH), F32)],
        scratch_shapes=[pltpu.VMEM((2, tq, LANES), F32), pltpu.VMEM((2, tq, LANES), F32), pltpu.VMEM((tq, LANES), F32)],
        compiler_params=_params("arbitrary", "arbitrary", "arbitrary", "arbitrary"),
    )(qp, kp, v)


def _flash_bwd(qp, kp, v, o, do, lse, bsz, seq):
    tq = ATTN_BLOCK
    nq = seq // tq
    t = bsz * seq

    def body(q_ref, k_ref, v_ref, o_ref, do_ref, lse_ref, dq_ref, dk_ref, dv_ref, dk_acc, dv_acc):
        j = pl.program_id(2)
        i = pl.program_id(3)

        @pl.when(jnp.logical_and(j == 0, i == 0))
        def _():
            dq_ref[...] = jnp.zeros_like(dq_ref)

        @pl.when(i == 0)
        def _():
            dk_acc[...] = jnp.zeros_like(dk_acc)
            dv_acc[...] = jnp.zeros_like(dv_acc)

        lane = lax.broadcasted_iota(jnp.int32, (1, LANES), 1)
        low = lane < HEAD_DIM

        def step(masked):
            dov = do_ref[...]
            dd = dov * o_ref[...]
            dob = dov.astype(BF16)
            vv = v_ref[...]
            lse_v = lse_ref[...]
            rows = pl.ds(pl.multiple_of(i * tq, tq), tq)
            for h in range(2):
                mine = low if h == 0 else jnp.logical_not(low)
                cols = pl.ds(h * LANES, LANES)
                qh = q_ref[:, cols]
                kh = k_ref[:, cols]
                s = _dot_nt(qh, kh)
                lse_h = jnp.where(mine, lse_v, pltpu.roll(lse_v, HEAD_DIM, 1))
                p = jnp.exp(s - jnp.tile(lse_h, (1, tq // LANES)))
                if masked:
                    row = lax.broadcasted_iota(jnp.int32, (tq, tq), 0)
                    col = lax.broadcasted_iota(jnp.int32, (tq, tq), 1)
                    p = jnp.where(row >= col, p, 0.0)
                delta = jnp.sum(jnp.where(mine, dd, 0.0), axis=1, keepdims=True)
                dp = _dot_nt(dob, jnp.where(mine, vv, jnp.zeros_like(vv)))
                ds = (p * (dp - delta)).astype(BF16)
                dv_acc[...] += jnp.where(mine, _dot_tn(p.astype(BF16), dob), 0.0)
                dk_acc[:, cols] += _dot_tn(ds, qh)
                dq_ref[rows, cols] += _dot(ds, kh)

        @pl.when(i > j)
        def _():
            step(False)

        @pl.when(i == j)
        def _():
            step(True)

        @pl.when(i == nq - 1)
        def _():
            dk_ref[...] = dk_acc[...]
            dv_ref[...] = dv_acc[...].astype(BF16)

    qspec = pl.BlockSpec((tq, 2 * LANES), lambda b, hp, j, i: (b * nq + jnp.maximum(i, j), hp))
    kspec = pl.BlockSpec((tq, 2 * LANES), lambda b, hp, j, i: (b * nq + j, hp))
    vspec = pl.BlockSpec((tq, LANES), lambda b, hp, j, i: (b * nq + j, hp))
    ospec = pl.BlockSpec((tq, LANES), lambda b, hp, j, i: (b * nq + jnp.maximum(i, j), hp))
    dqspec = pl.BlockSpec((seq, 2 * LANES), lambda b, hp, j, i: (b, hp))
    return pl.pallas_call(
        body, name="flash_bwd", grid=(bsz, HEAD_PAIRS, nq, nq),
        in_specs=[qspec, kspec, vspec, ospec, ospec, ospec], out_specs=[dqspec, kspec, vspec],
        out_shape=[jax.ShapeDtypeStruct((t, N_HEADS * LANES), F32), jax.ShapeDtypeStruct((t, N_HEADS * LANES), F32),
                   jax.ShapeDtypeStruct((t, ATTN_WIDTH), BF16)],
        scratch_shapes=[pltpu.VMEM((tq, 2 * LANES), F32), pltpu.VMEM((tq, LANES), F32)],
        compiler_params=_params("arbitrary", "arbitrary", "arbitrary", "arbitrary"),
    )(qp, kp, v, o, do, lse)


def _mix_out_fwd(o, y_pool, x, gain, w_out):
    t, d = x.shape
    tm = 512
    pw, aw = POOL_WIDTH, ATTN_WIDTH

    def body(o_ref, yp_ref, x_ref, g_ref, w_ref, ycat_ref, y_ref):
        ov = o_ref[...]
        ya = (ov * _rms_scale(ov) * g_ref[...]).astype(BF16)
        ycat = jnp.concatenate([yp_ref[...], ya], axis=1)
        ycat_ref[...] = ycat
        y_ref[...] = x_ref[...] + _dot(ycat, w_ref[...])

    return pl.pallas_call(
        body, name="mix_out_fwd", grid=(t // tm,),
        in_specs=[_rows(tm, aw), _rows(tm, pw), _rows(tm, d), _resident((1, aw)), _resident((pw + aw, d))],
        out_specs=[_rows(tm, pw + aw), _rows(tm, d)],
        out_shape=[jax.ShapeDtypeStruct((t, pw + aw), BF16), jax.ShapeDtypeStruct((t, d), F32)],
        compiler_params=_params("arbitrary"),
    )(o, y_pool, x, gain, w_out)


def _mix_out_bwd(dx, o, gain, w_out):
    t, d = dx.shape
    tm = 512
    pw, aw = POOL_WIDTH, ATTN_WIDTH

    def body(dx_ref, o_ref, g_ref, w_ref, dxb_ref, dyp_ref, do_ref, dg_ref):
        dxb = dx_ref[...].astype(BF16)
        dxb_ref[...] = dxb
        dyp_ref[...] = _dot_nt(dxb, w_ref[pl.ds(0, pw), :])
        dya = _dot_nt(dxb, w_ref[pl.ds(pw, aw), :])
        do, dgain = _rms_bwd(dya, o_ref[...], g_ref[...])
        do_ref[...] = do

        @pl.when(pl.program_id(0) == 0)
        def _():
            dg_ref[...] = jnp.zeros_like(dg_ref)

        dg_ref[...] += dgain

    return pl.pallas_call(
        body, name="mix_out_bwd", grid=(t // tm,),
        in_specs=[_rows(tm, d), _rows(tm, aw), _resident((1, aw)), _resident((pw + aw, d))],
        out_specs=[_rows(tm, d), _rows(tm, pw), _rows(tm, aw), pl.BlockSpec((1, aw), lambda i: (0, 0))],
        out_shape=[jax.ShapeDtypeStruct((t, d), BF16), jax.ShapeDtypeStruct((t, pw), F32),
                   jax.ShapeDtypeStruct((t, aw), F32), jax.ShapeDtypeStruct((1, aw), F32)],
        compiler_params=_params("arbitrary"),
    )(dx, o, gain, w_out)


def _mix_in_bwd(dpv, dq, dk, dv, df, x, dx_res, gain, w_in_t):
    t, d = x.shape
    tm = 512
    pw, aw = POOL_WIDTH, ATTN_WIDTH

    def body(dpv_ref, dq_ref, dk_ref, dv_ref, df_ref, x_ref, dxr_ref, g_ref, w_ref, dh_ref, dx_ref, dg_ref):
        dh = jnp.concatenate([dpv_ref[...], dq_ref[...], dk_ref[...], dv_ref[...], df_ref[...]], axis=1)
        dh_ref[...] = dh
        dhm = _dot(dh, w_ref[...])
        dx, dgain = _rms_bwd(dhm, x_ref[...], g_ref[...])
        dx_ref[...] = dxr_ref[...] + dx

        @pl.when(pl.program_id(0) == 0)
        def _():
            dg_ref[...] = jnp.zeros_like(dg_ref)

        dg_ref[...] += dgain

    return pl.pallas_call(
        body, name="mix_in_bwd", grid=(t // tm,),
        in_specs=[_rows(tm, pw), _rows(tm, aw), _rows(tm, aw), _rows(tm, aw), _rows(tm, LANES), _rows(tm, d),
                  _rows(tm, d), _resident((1, d)), _resident((MIX_PAD, d))],
        out_specs=[_rows(tm, MIX_PAD), _rows(tm, d), pl.BlockSpec((1, d), lambda i: (0, 0))],
        out_shape=[jax.ShapeDtypeStruct((t, MIX_PAD), BF16), jax.ShapeDtypeStruct((t, d), F32),
                   jax.ShapeDtypeStruct((1, d), F32)],
        compiler_params=_params("arbitrary"),
    )(dpv, dq, dk, dv, df, x, dx_res, gain, w_in_t)


MESH_IDS = pl.DeviceIdType.MESH


def _me():
    return lax.axis_index("x"), lax.axis_index("y"), lax.axis_index("c")


def _peer(x, y, c, p):
    px = 1 - x if p & 4 else x
    py = 1 - y if p & 2 else y
    pc = 1 - c if p & 1 else c
    return (px, py, pc), 4 * px + 2 * py + pc


HBM_SPEC = pl.BlockSpec(memory_space=pltpu.HBM)
SEM_SPEC = pl.BlockSpec(memory_space=pltpu.SEMAPHORE)
SPLIT_COPY = pltpu.CompilerParams(has_side_effects=pltpu.SideEffectType.DATAFLOW_SIDE_EFFECTING)
PEERS = N_DEV - 1


def _hbm(a):
    return pltpu.with_memory_space_constraint(a, pltpu.HBM)


def _row_block(ref, dev, rows):
    return ref.at[pl.ds(pl.multiple_of(dev * rows, BF16_ROWS), rows)]


def _copy_ends(gather, src, land, me, peer_id):
    if gather:
        rows = src.shape[0]
        return src, _row_block(land, me, rows), _row_block(land, peer_id, rows), src, _row_block(land, me, rows)
    rows = src.shape[0] // N_DEV
    return (_row_block(src, peer_id, rows), land.at[me], land.at[peer_id], _row_block(src, me, rows), land.at[me])


def _land_shape(gather, s):
    return (N_DEV * s.shape[0], s.shape[1]) if gather else (N_DEV, s.shape[0] // N_DEV, s.shape[1])


def _copies_start(groups, gather, name, after=None):
    flat = [s for g in groups for s in g]
    n, ng = len(flat), len(groups)
    lands = [lax.empty(_land_shape(gather, s), s.dtype) for s in flat]
    n_in = 2 * n + (after is not None)

    def body(*refs):
        ins, lnd = refs[:n], refs[n:2 * n]
        sems = refs[n_in:n_in + 2 * ng]
        token = refs[-1]
        x, y, c = _me()
        me = 4 * x + 2 * y + c
        w = 0
        for gi, g in enumerate(groups):
            for k in range(len(g)):
                for p in range(1, N_DEV):
                    peer, peer_id = _peer(x, y, c, p)
                    src, dst, _, _, _ = _copy_ends(gather, ins[w], lnd[w], me, peer_id)
                    pltpu.make_async_remote_copy(src, dst, sems[2 * gi].at[k * PEERS + p - 1],
                                                 sems[2 * gi + 1].at[k * PEERS + p - 1], device_id=peer,
                                                 device_id_type=MESH_IDS).start()
                w += 1
        token[...] = jnp.zeros_like(token)

    sem_shapes = []
    for g in groups:
        sem_shapes += [pltpu.SemaphoreType.DMA((len(g) * PEERS,))] * 2
    out = pl.pallas_call(
        body, name=name,
        out_shape=(*sem_shapes, *[pltpu.HBM(s.shape, s.dtype) for s in flat],
                   *[pltpu.HBM(l.shape, l.dtype) for l in lands], jax.ShapeDtypeStruct((8, LANES), F32)),
        in_specs=[HBM_SPEC] * (2 * n) + [pl.BlockSpec(memory_space=pl.ANY)] * (after is not None),
        out_specs=(*[SEM_SPEC] * (2 * ng), *[HBM_SPEC] * (2 * n), pl.BlockSpec(memory_space=pltpu.VMEM)),
        input_output_aliases={i: 2 * ng + i for i in range(2 * n)},
        compiler_params=SPLIT_COPY,
    )(*[_hbm(s) for s in flat], *[_hbm(l) for l in lands], *([after] if after is not None else []))
    sems, thru, token = out[:2 * ng], out[2 * ng:2 * ng + 2 * n], out[-1]
    res, w = [], 0
    for gi, g in enumerate(groups):
        res.append((sems[2 * gi], sems[2 * gi + 1], list(thru[w:w + len(g)]), list(thru[n + w:n + w + len(g)])))
        w += len(g)
    return res, token


def _copies_wait(started, gather, after, name):
    send, recv, srcs, lands = started
    n = len(srcs)

    own_shapes = [s.shape if gather else (s.shape[0] // N_DEV, s.shape[1]) for s in srcs]

    def body(*refs):
        ins, lnd = refs[:n], refs[n:2 * n]
        send_sems, recv_sems = refs[2 * n], refs[2 * n + 1]
        bounce, in_sems, out_sems = refs[-n - 2:-2], refs[-2], refs[-1]
        x, y, c = _me()
        me = 4 * x + 2 * y + c
        ends = [_copy_ends(gather, ins[w], lnd[w], me, me)[3:] for w in range(n)]
        loads = [pltpu.make_async_copy(ends[w][0], bounce[w], in_sems.at[w]) for w in range(n)]
        stores = [pltpu.make_async_copy(bounce[w], ends[w][1], out_sems.at[w]) for w in range(n)]
        for cp in loads:
            cp.start()
        for w in range(n):
            loads[w].wait()
            stores[w].start()
        for w in range(n):
            for p in range(1, N_DEV):
                peer, peer_id = _peer(x, y, c, p)
                src, _, arrival, _, _ = _copy_ends(gather, ins[w], lnd[w], me, peer_id)
                cp = pltpu.make_async_remote_copy(src, arrival, send_sems.at[w * PEERS + p - 1],
                                                  recv_sems.at[w * PEERS + p - 1], device_id=peer,
                                                  device_id_type=MESH_IDS)
                cp.wait_send()
                cp.wait_recv()
        for cp in stores:
            cp.wait()

    out = pl.pallas_call(
        body, name=name,
        out_shape=(*[pltpu.HBM(s.shape, s.dtype) for s in srcs], *[pltpu.HBM(l.shape, l.dtype) for l in lands]),
        in_specs=[HBM_SPEC] * (2 * n) + [SEM_SPEC, SEM_SPEC, pl.BlockSpec(memory_space=pl.ANY)],
        out_specs=[HBM_SPEC] * (2 * n),
        input_output_aliases={i: i for i in range(2 * n)},
        scratch_shapes=[*[pltpu.VMEM(shape, s.dtype) for shape, s in zip(own_shapes, srcs)],
                        pltpu.SemaphoreType.DMA((n,)), pltpu.SemaphoreType.DMA((n,))],
        compiler_params=SPLIT_COPY,
    )(*srcs, *lands, send, recv, after)
    return list(out[n:])


def _sum_slots(parts, name):
    _, rows, d = parts.shape

    def body(p_ref, o_ref):
        acc = p_ref[0].astype(F32)
        for dev in range(1, N_DEV):
            acc = acc + p_ref[dev].astype(F32)
        o_ref[...] = acc

    return pl.pallas_call(
        body, name=name, grid=(1,),
        in_specs=[pl.BlockSpec((N_DEV, rows, d), lambda i: (0, 0, 0))],
        out_specs=pl.BlockSpec((rows, d), lambda i: (0, 0)), out_shape=jax.ShapeDtypeStruct((rows, d), F32),
        compiler_params=_params("arbitrary"),
    )(parts)


def _adamw(w, g, m, v, name):
    r, c = w.shape

    def body(w_ref, g_ref, m_ref, v_ref, d_ref, nm_ref, nv_ref):
        gv = g_ref[...]
        nm = ADAM_B1 * m_ref[...] + (1.0 - ADAM_B1) * gv
        nv = ADAM_B2 * v_ref[...] + (1.0 - ADAM_B2) * (gv * gv)
        m_hat = nm / (1.0 - ADAM_B1 ** ADAM_STEP)
        v_hat = nv / (1.0 - ADAM_B2 ** ADAM_STEP)
        d_ref[...] = -ADAM_LR * (m_hat / (jnp.sqrt(v_hat) + ADAM_EPS) + ADAM_WD * w_ref[...])
        nm_ref[...] = nm
        nv_ref[...] = nv

    tr = r // 4 if r % 32 == 0 else r
    spec = pl.BlockSpec((tr, c), lambda i: (i, 0))
    shape = jax.ShapeDtypeStruct((r, c), F32)
    return pl.pallas_call(
        body, name=name, grid=(r // tr,), in_specs=[spec] * 4, out_specs=[spec] * 3, out_shape=[shape] * 3,
        compiler_params=_params("arbitrary"),
    )(w, g, m, v)


def _pad_rows(a, rows):
    return jnp.pad(a, ((0, rows - a.shape[0]), (0, 0)))


def _row1(vec, width=D_MODEL):
    return jnp.pad(vec.reshape(1, -1), ((0, 0), (0, width - vec.shape[-1])))


SMALL_NAMES = ("ffn1_norm", "mix_norm", "ffn2_norm", "b_forget", "pool_scale", "q_norm", "k_norm", "out_norm_pool",
               "out_norm_attn", "pool_w")


def _pack_small(vals):
    rows = [_row1(vals[n].reshape(-1)) for n in SMALL_NAMES[:-1]]
    rows.append(vals["pool_w"].reshape(-1, D_MODEL))
    packed = jnp.concatenate(rows, axis=0)
    return _pad_rows(packed, SMALL_ROWS)


def _unpack_small(packed, like):
    out = {}
    for i, n in enumerate(SMALL_NAMES[:-1]):
        size = like[n].size
        out[n] = packed[i, :size].reshape(like[n].shape)
    first = len(SMALL_NAMES) - 1
    out["pool_w"] = packed[first:first + like["pool_w"].size // D_MODEL].reshape(like["pool_w"].shape)
    return out


def kernel(x, ffn1_norm, ffn1_w_gate, ffn1_w_up, ffn1_w_down, mix_norm, w_in, b_forget, pool_w, pool_scale, q_norm, k_norm, out_norm_pool, out_norm_attn, w_out, ffn2_norm, ffn2_w_gate, ffn2_w_up, ffn2_w_down, loss_target, m_ffn1_norm, m_ffn1_w_gate, m_ffn1_w_up, m_ffn1_w_down, m_mix_norm, m_w_in, m_b_forget, m_pool_w, m_pool_scale, m_q_norm, m_k_norm, m_out_norm_pool, m_out_norm_attn, m_w_out, m_ffn2_norm, m_ffn2_w_gate, m_ffn2_w_up, m_ffn2_w_down, v_ffn1_norm, v_ffn1_w_gate, v_ffn1_w_up, v_ffn1_w_down, v_mix_norm, v_w_in, v_b_forget, v_pool_w, v_pool_scale, v_q_norm, v_k_norm, v_out_norm_pool, v_out_norm_attn, v_w_out, v_ffn2_norm, v_ffn2_w_gate, v_ffn2_w_up, v_ffn2_w_down):
    bsz, seq, d = x.shape
    t = bsz * seq
    x0 = x.reshape(t, d)
    target = loss_target.reshape(t, d)
    in_rows = -(-w_in.shape[1] // BF16_ROWS) * BF16_ROWS

    slabs = [s.astype(BF16) for s in (ffn1_w_gate.T, ffn1_w_up.T, ffn1_w_down, _pad_rows(w_in.T, in_rows), w_out,
                                       ffn2_w_gate.T, ffn2_w_up.T, ffn2_w_down)]
    gathers, started = _copies_start([slabs[0:3], slabs[3:4], slabs[4:5], slabs[5:8]], True, "gather_start")

    g1, gm, g2 = ffn1_norm.reshape(1, d), mix_norm.reshape(1, d), ffn2_norm.reshape(1, d)
    bf_row = _row1(b_forget, LANES)
    gq = jnp.tile(q_norm, N_HEADS).reshape(1, ATTN_WIDTH)
    gk = jnp.tile(k_norm, N_HEADS).reshape(1, ATTN_WIDTH)
    scale_row = pool_scale.reshape(1, POOL_WIDTH)
    gp, ga = out_norm_pool.reshape(1, POOL_WIDTH), out_norm_attn.reshape(1, ATTN_WIDTH)

    wg1, wu1, wd1 = _copies_wait(gathers[0], True, started, "gather_wait_ffn1")
    h1, a1, b1, s1, x1 = _ffn_fwd(x0, g1, wg1, wu1, wd1, None, "ffn1_fwd")
    (win_g,) = _copies_wait(gathers[1], True, x1, "gather_wait_w_in")
    win_cols = win_g.reshape(N_DEV, in_rows, d)[:, :w_in.shape[1]].reshape(MIX_COLS, d)
    win_t = _pad_rows(win_cols, MIX_PAD)
    hm, pv, q, k, v, f = _mix_in_fwd(x1, gm, win_t)
    pooled, mixed, y_pool = _pool_fwd(pv, pool_w, scale_row, gp, bsz, seq)
    qp, kp = _attn_prep_fwd(q, k, f, bf_row, gq, gk, bsz, seq)
    o, lse = _flash_fwd(qp, kp, v, bsz, seq)
    (wout,) = _copies_wait(gathers[2], True, o, "gather_wait_w_out")
    ycat, x2 = _mix_out_fwd(o, y_pool, x1, ga, wout)
    wg2, wu2, wd2 = _copies_wait(gathers[3], True, x2, "gather_wait_ffn2")
    h2, a2, b2, s2, dx3, loss_part = _ffn_fwd(x2, g2, wg2, wu2, wd2, target, "ffn2_fwd")

    da2, db2, dyh2, dx2, dg2 = _ffn_bwd(dx3, x2, g2, a2, b2, wg2, wu2, wd2, "ffn2_bwd")
    dwg2, dwu2, dwd2 = _ffn_wgrad(da2, db2, s2, h2, dyh2, "ffn2_wgrad")
    (sent_ffn2,), tok = _copies_start([[dwg2, dwu2, dwd2]], False, "exchange_start_ffn2")
    dx2b, dy_pool, do, dga = _mix_out_bwd(dx2, o, ga + tok[0, 0], wout)
    dwout = _wgrad(ycat, dx2b, "w_out_wgrad")
    (sent_out,), tok = _copies_start([[dwout]], False, "exchange_start_w_out")
    dqp, dkp, dv = _flash_bwd(qp, kp, v, o, do, lse, bsz, seq)
    dq, dk, df, dgq, dgk, dbf = _attn_prep_bwd(dqp, dkp, q, k, f, bf_row + tok[0, 0], gq, gk, bsz, seq)
    dpv, dpool_w, dscale, dgp = _pool_bwd(dy_pool, mixed, pooled, pool_w, scale_row, gp, bsz, seq)
    dhcat, dx1, dgm = _mix_in_bwd(dpv, dq, dk, dv, df, x1, dx2, gm, win_t)
    dwin = _wgrad(dhcat, hm, "w_in_wgrad")
    dwin_blocks = jnp.pad(dwin[:MIX_COLS].reshape(N_DEV, w_in.shape[1], d), ((0, 0), (0, in_rows - w_in.shape[1]), (0, 0)))
    (sent_in,), tok = _copies_start([[dwin_blocks.reshape(N_DEV * in_rows, d)]], False, "exchange_start_w_in")
    da1, db1, dyh1, dx0, dg1 = _ffn_bwd(dx1, x0, g1 + tok[0, 0], a1, b1, wg1, wu1, wd1, "ffn1_bwd")

    fold = lambda g: g.reshape(N_HEADS, HEAD_DIM).sum(axis=0)
    small_like = dict(ffn1_norm=ffn1_norm, mix_norm=mix_norm, ffn2_norm=ffn2_norm, b_forget=b_forget,
                      pool_scale=pool_scale, q_norm=q_norm, k_norm=k_norm, out_norm_pool=out_norm_pool,
                      out_norm_attn=out_norm_attn, pool_w=pool_w)
    small_part = dict(ffn1_norm=dg1, mix_norm=dgm, ffn2_norm=dg2, b_forget=dbf[0, :N_HEADS], pool_scale=dscale,
                      q_norm=fold(dgq), k_norm=fold(dgk), out_norm_pool=dgp, out_norm_attn=dga, pool_w=dpool_w)
    (sent_small,), tok = _copies_start([[_pack_small(small_part)]], True, "small_grads_start")
    dwg1, dwu1, dwd1 = _ffn_wgrad(da1, db1, s1, h1, dyh1, "ffn1_wgrad")
    (sent_ffn1,), tok = _copies_start([[dwg1, dwu1, dwd1]], False, "exchange_start_ffn1", after=tok)
    (small_all,) = _copies_wait(sent_small, True, tok, "small_grads_wait")
    small_sum = _sum_slots(small_all.reshape(N_DEV, SMALL_ROWS, d), "sum_small_grads")
    grads = _unpack_small(small_sum, small_like)
    loss = lax.psum(loss_part[0, 0], ("x", "y", "c"))

    weights = dict(ffn1_norm=ffn1_norm, ffn1_w_gate=ffn1_w_gate, ffn1_w_up=ffn1_w_up, ffn1_w_down=ffn1_w_down,
                   mix_norm=mix_norm, w_in=w_in, b_forget=b_forget, pool_w=pool_w, pool_scale=pool_scale,
                   q_norm=q_norm, k_norm=k_norm, out_norm_pool=out_norm_pool, out_norm_attn=out_norm_attn,
                   w_out=w_out, ffn2_norm=ffn2_norm, ffn2_w_gate=ffn2_w_gate, ffn2_w_up=ffn2_w_up,
                   ffn2_w_down=ffn2_w_down)
    m_in = dict(ffn1_norm=m_ffn1_norm, ffn1_w_gate=m_ffn1_w_gate, ffn1_w_up=m_ffn1_w_up, ffn1_w_down=m_ffn1_w_down,
                mix_norm=m_mix_norm, w_in=m_w_in, b_forget=m_b_forget, pool_w=m_pool_w, pool_scale=m_pool_scale,
                q_norm=m_q_norm, k_norm=m_k_norm, out_norm_pool=m_out_norm_pool, out_norm_attn=m_out_norm_attn,
                w_out=m_w_out, ffn2_norm=m_ffn2_norm, ffn2_w_gate=m_ffn2_w_gate, ffn2_w_up=m_ffn2_w_up,
                ffn2_w_down=m_ffn2_w_down)
    v_in = dict(ffn1_norm=v_ffn1_norm, ffn1_w_gate=v_ffn1_w_gate, ffn1_w_up=v_ffn1_w_up, ffn1_w_down=v_ffn1_w_down,
                mix_norm=v_mix_norm, w_in=v_w_in, b_forget=v_b_forget, pool_w=v_pool_w, pool_scale=v_pool_scale,
                q_norm=v_q_norm, k_norm=v_k_norm, out_norm_pool=v_out_norm_pool, out_norm_attn=v_out_norm_attn,
                w_out=v_w_out, ffn2_norm=v_ffn2_norm, ffn2_w_gate=v_ffn2_w_gate, ffn2_w_up=v_ffn2_w_up,
                ffn2_w_down=v_ffn2_w_down)
    delta, new_m, new_v = {}, {}, {}
    small_d, small_m, small_v = _adamw(_pack_small(weights), small_sum, _pack_small(m_in), _pack_small(v_in),
                                       "adamw_small")
    delta.update(_unpack_small(small_d, small_like))
    new_m.update(_unpack_small(small_m, small_like))
    new_v.update(_unpack_small(small_v, small_like))

    after = small_v
    plan = ((sent_ffn2, "ffn2", ("ffn2_w_gate", "ffn2_w_up", "ffn2_w_down")), (sent_out, "w_out", ("w_out",)),
            (sent_in, "w_in", ("w_in",)), (sent_ffn1, "ffn1", ("ffn1_w_gate", "ffn1_w_up", "ffn1_w_down")))
    for sent, tag, names in plan:
        parts = _copies_wait(sent, False, after, f"exchange_wait_{tag}")
        for n, part in zip(names, parts):
            total = _sum_slots(part, f"sum_grads_{n}")
            if n == "w_in":
                total = total[:w_in.shape[1]]
            grads[n] = total if n in ("ffn1_w_down", "ffn2_w_down", "w_out") else total.T
            delta[n], new_m[n], new_v[n] = _adamw(weights[n], grads[n], m_in[n], v_in[n], f"adamw_{n}")
            after = new_v[n]

    order = ("ffn1_norm", "ffn1_w_gate", "ffn1_w_up", "ffn1_w_down", "mix_norm", "w_in", "b_forget", "pool_w",
             "pool_scale", "q_norm", "k_norm", "out_norm_pool", "out_norm_attn", "w_out", "ffn2_norm", "ffn2_w_gate",
             "ffn2_w_up", "ffn2_w_down")
    return (loss, dx0.reshape(bsz, seq, d), *[grads[n] for n in order], *[delta[n] for n in order],
            *[new_m[n] for n in order], *[new_v[n] for n in order])
```

```python
import functools

import jax
import jax.numpy as jnp
from jax import lax
from jax.experimental import pallas as pl
from jax.experimental.pallas import tpu as pltpu

F32 = jnp.float32
BF16 = jnp.bfloat16

EPS = 1e-6
D_MODEL = 1024
D_FF = 2816
N_HEADS = 8
HEAD_DIM = 64
POOL_WIDTH = 512
ATTN_WIDTH = 512
POOL_GROUPS = 4
POOL_GROUP_DIM = 128
POOL_WINDOWS = (2, 4, 8, 16)
POOL_HALO = 16
MIX_COLS = POOL_WIDTH + 3 * ATTN_WIDTH + N_HEADS
MIX_PAD = POOL_WIDTH + 3 * ATTN_WIDTH + 128
N_DEV = 8
BF16_ROWS = 16
LANES = 128
VMEM_LIMIT = 56 * 1024 * 1024

ADAM_LR = 0.001
ADAM_B1 = 0.9
ADAM_B2 = 0.999
ADAM_EPS = 1e-08
ADAM_WD = 0.01
ADAM_STEP = 10

SMALL_ROWS = 80


def _params(*sem):
    return pltpu.CompilerParams(dimension_semantics=sem, vmem_limit_bytes=VMEM_LIMIT)


def _dot(a, b):
    return jnp.dot(a, b, preferred_element_type=F32)


def _dot_nt(a, b):
    return lax.dot_general(a, b, (((1,), (1,)), ((), ())), preferred_element_type=F32)


def _dot_tn(a, b):
    return lax.dot_general(a, b, (((0,), (0,)), ((), ())), preferred_element_type=F32)


def _resident(shape):
    return pl.BlockSpec(shape, lambda *_: (0,) * len(shape), pipeline_mode=pl.Buffered(1))


def _rows(tm, width):
    return pl.BlockSpec((tm, width), lambda i: (i, 0))


def _rms_scale(x):
    return lax.rsqrt(jnp.mean(x * x, axis=-1, keepdims=True) + EPS)


def _rms_bwd(dh, x, gain):
    r = _rms_scale(x)
    n = x * r
    dgain = jnp.sum(dh * n, axis=0, keepdims=True)
    dn = dh * gain
    dx = r * (dn - n * jnp.mean(dn * n, axis=-1, keepdims=True))
    return dx, dgain


def _split3(x):
    hi = x.astype(BF16)
    r1 = x - hi.astype(F32)
    mid = r1.astype(BF16)
    lo = (r1 - mid.astype(F32)).astype(BF16)
    return hi, mid, lo


def _split2(x):
    hi = x.astype(BF16)
    return hi, (x - hi.astype(F32)).astype(BF16)


FF_CHUNK = 256


def _ffn_up(x, gain, wg_t, wu_t, name):
    t, d = x.shape
    f = wg_t.shape[0]
    tm = 256

    def body(x_ref, g_ref, wg_ref, wu_ref, h_ref, a_ref, b_ref, s_ref):
        xv = x_ref[...]
        h = (xv * _rms_scale(xv) * g_ref[...]).astype(BF16)
        h_ref[...] = h
        for c in range(f // FF_CHUNK):
            sl = pl.ds(c * FF_CHUNK, FF_CHUNK)
            a = _dot_nt(h, wg_ref[sl, :])
            b = _dot_nt(h, wu_ref[sl, :])
            a_ref[:, sl] = a.astype(BF16)
            b_ref[:, sl] = b.astype(BF16)
            s_ref[:, sl] = (a * jax.nn.sigmoid(a) * b).astype(BF16)

    wide = jax.ShapeDtypeStruct((t, f), BF16)
    return pl.pallas_call(
        body, name=name, grid=(t // tm,),
        in_specs=[_rows(tm, d), _resident((1, d)), _resident((f, d)), _resident((f, d))],
        out_specs=[_rows(tm, d), _rows(tm, f), _rows(tm, f), _rows(tm, f)],
        out_shape=[jax.ShapeDtypeStruct((t, d), BF16), wide, wide, wide],
        compiler_params=_params("arbitrary"),
    )(x, gain, wg_t, wu_t)


def _ffn_down(s, wd, x, target, name):
    t, d = x.shape
    f = wd.shape[0]
    tm = 512
    with_loss = target is not None

    def body(*refs):
        if with_loss:
            s_ref, w_ref, x_ref, t_ref, dy_ref, dyh_ref, loss_ref = refs
        else:
            s_ref, w_ref, x_ref, y_ref = refs
        y = x_ref[...] + 0.5 * _dot(s_ref[...], w_ref[...])
        if with_loss:
            e = y - t_ref[...]
            dy = e * (1.0 / d)
            dy_ref[...] = dy
            dyh_ref[...] = (0.5 * dy).astype(BF16)

            @pl.when(pl.program_id(0) == 0)
            def _():
                loss_ref[...] = jnp.zeros_like(loss_ref)

            part = jnp.sum(jnp.sum(e * e, axis=0, keepdims=True), axis=1, keepdims=True)
            loss_ref[...] += part * (0.5 / d)
        else:
            y_ref[...] = y

    in_specs = [_rows(tm, f), _resident((f, d)), _rows(tm, d)]
    args = [s, wd, x]
    if with_loss:
        in_specs.append(_rows(tm, d))
        args.append(target)
        out_shape = [jax.ShapeDtypeStruct((t, d), F32), jax.ShapeDtypeStruct((t, d), BF16),
                     jax.ShapeDtypeStruct((1, 1), F32)]
        out_specs = [_rows(tm, d), _rows(tm, d), pl.BlockSpec((1, 1), lambda i: (0, 0))]
    else:
        out_shape = [jax.ShapeDtypeStruct((t, d), F32)]
        out_specs = [_rows(tm, d)]
    return pl.pallas_call(
        body, name=name, grid=(t // tm,), in_specs=in_specs, out_specs=out_specs, out_shape=out_shape,
        compiler_params=_params("arbitrary"),
    )(*args)


def _ffn_bwd_act(dyh, a, b, wd, name):
    t, d = dyh.shape
    f = wd.shape[0]
    tm = 256

    def body(dy_ref, a_ref, b_ref, wd_ref, da_ref, db_ref):
        dyh_v = dy_ref[...]
        for c in range(f // FF_CHUNK):
            sl = pl.ds(c * FF_CHUNK, FF_CHUNK)
            ds = _dot_nt(dyh_v, wd_ref[sl, :])
            av = a_ref[:, sl].astype(F32)
            bv = b_ref[:, sl].astype(F32)
            sig = jax.nn.sigmoid(av)
            da_ref[:, sl] = (ds * bv * (sig * (1.0 + av * (1.0 - sig)))).astype(BF16)
            db_ref[:, sl] = (ds * (av * sig)).astype(BF16)

    wide = jax.ShapeDtypeStruct((t, f), BF16)
    return pl.pallas_call(
        body, name=name, grid=(t // tm,),
        in_specs=[_rows(tm, d), _rows(tm, f), _rows(tm, f), _resident((f, d))],
        out_specs=[_rows(tm, f), _rows(tm, f)], out_shape=[wide, wide],
        compiler_params=_params("arbitrary"),
    )(dyh, a, b, wd)


def _ffn_bwd_dx(da, db, dy, x, gain, wg_t, wu_t, name):
    t, d = x.shape
    f = wg_t.shape[0]
    tm = 512

    def body(da_ref, db_ref, dy_ref, x_ref, g_ref, wg_ref, wu_ref, dx_ref, dg_ref):
        dh = _dot(da_ref[...], wg_ref[...]) + _dot(db_ref[...], wu_ref[...])
        dx, dgain = _rms_bwd(dh, x_ref[...], g_ref[...])
        dx_ref[...] = dy_ref[...] + dx

        @pl.when(pl.program_id(0) == 0)
        def _():
            dg_ref[...] = jnp.zeros_like(dg_ref)

        dg_ref[...] += dgain

    return pl.pallas_call(
        body, name=name, grid=(t // tm,),
        in_specs=[_rows(tm, f), _rows(tm, f), _rows(tm, d), _rows(tm, d), _resident((1, d)), _resident((f, d)),
                  _resident((f, d))],
        out_specs=[_rows(tm, d), pl.BlockSpec((1, d), lambda i: (0, 0))],
        out_shape=[jax.ShapeDtypeStruct((t, d), F32), jax.ShapeDtypeStruct((1, d), F32)],
        compiler_params=_params("arbitrary"),
    )(da, db, dy, x, gain, wg_t, wu_t)


def _wgrad(lhs, b, name):
    t, n = lhs[0].shape
    d = b.shape[1]
    m = len(lhs)
    tn = n // 2 if n * d * m > (4 << 20) else n
    tk = 256
    nk = t // tk

    def body(*refs):
        a_refs, b_ref, o_refs, accs = refs[:m], refs[m], refs[m + 1:2 * m + 1], refs[2 * m + 1:]
        k = pl.program_id(1)

        @pl.when(k == 0)
        def _():
            for acc in accs:
                acc[...] = jnp.zeros_like(acc)

        bv = b_ref[...]
        for a_ref, acc in zip(a_refs, accs):
            acc[...] += _dot_tn(a_ref[...], bv)

        @pl.when(k == nk - 1)
        def _():
            for o_ref, acc in zip(o_refs, accs):
                o_ref[...] = acc[...].astype(BF16)

    return pl.pallas_call(
        body, name=name, grid=(n // tn, nk),
        in_specs=[pl.BlockSpec((tk, tn), lambda j, k: (k, j))] * m + [pl.BlockSpec((tk, d), lambda j, k: (k, 0))],
        out_specs=[pl.BlockSpec((tn, d), lambda j, k: (j, 0))] * m,
        out_shape=[jax.ShapeDtypeStruct((n, d), BF16)] * m,
        scratch_shapes=[pltpu.VMEM((tn, d), F32)] * m,
        compiler_params=_params("arbitrary", "arbitrary"),
    )(*lhs, b)


def _mix_in_fwd(x, gain, w_in_t):
    t, d = x.shape
    tm = 512
    pw, aw = POOL_WIDTH, ATTN_WIDTH

    def body(x_ref, g_ref, w_ref, hm_ref, pv_ref, q_ref, k_ref, v_ref, f_ref):
        xv = x_ref[...]
        hm = (xv * _rms_scale(xv) * g_ref[...]).astype(BF16)
        hm_ref[...] = hm
        pv_ref[...] = _dot_nt(hm, w_ref[pl.ds(0, pw), :])
        q_ref[...] = _dot_nt(hm, w_ref[pl.ds(pw, aw), :])
        k_ref[...] = _dot_nt(hm, w_ref[pl.ds(pw + aw, aw), :])
        v_ref[...] = _dot_nt(hm, w_ref[pl.ds(pw + 2 * aw, aw), :]).astype(BF16)
        f_ref[...] = _dot_nt(hm, w_ref[pl.ds(pw + 3 * aw, LANES), :])

    return pl.pallas_call(
        body, name="mix_in_fwd", grid=(t // tm,),
        in_specs=[_rows(tm, d), _resident((1, d)), _resident((MIX_PAD, d))],
        out_specs=[_rows(tm, d), _rows(tm, pw), _rows(tm, aw), _rows(tm, aw), _rows(tm, aw), _rows(tm, LANES)],
        out_shape=[jax.ShapeDtypeStruct((t, d), BF16), jax.ShapeDtypeStruct((t, pw), F32),
                   jax.ShapeDtypeStruct((t, aw), F32), jax.ShapeDtypeStruct((t, aw), F32),
                   jax.ShapeDtypeStruct((t, aw), BF16), jax.ShapeDtypeStruct((t, LANES), F32)],
        compiler_params=_params("arbitrary"),
    )(x, gain, w_in_t)


def _pool_fwd(pv, pool_w, pool_scale, gain, bsz, seq):
    ts = 512
    ns = seq // ts
    pw = POOL_WIDTH

    def body(pv_ref, w_ref, sc_ref, g_ref, pooled_ref, mixed_ref, y_ref, ext):
        s = pl.program_id(1)

        @pl.when(s == 0)
        def _():
            ext[pl.ds(0, POOL_HALO), :] = jnp.zeros((POOL_HALO, pw), F32)

        p = pv_ref[...]
        ext[pl.ds(POOL_HALO, ts), :] = p
        pos = s * ts + lax.broadcasted_iota(jnp.int32, (ts, 1), 0)
        parts = []
        for g, w in enumerate(POOL_WINDOWS):
            lanes = pl.ds(g * POOL_GROUP_DIM, POOL_GROUP_DIM)
            win = ext[pl.ds(POOL_HALO, ts), lanes]
            for i in range(1, w):
                win = win + ext[pl.ds(POOL_HALO - i, ts), lanes]
            cnt = jnp.minimum(pos + 1, w).astype(F32)
            pooled = (win / cnt - ext[pl.ds(POOL_HALO, ts), lanes]).astype(BF16)
            pooled_ref[:, lanes] = pooled
            parts.append(_dot(pooled, w_ref[g].astype(BF16)))
        mixed = jnp.concatenate(parts, axis=1)
        mixed_ref[...] = mixed
        pm = mixed * sc_ref[...]
        y_ref[...] = (pm * _rms_scale(pm) * g_ref[...]).astype(BF16)
        ext[pl.ds(0, POOL_HALO), :] = p[ts - POOL_HALO:, :]

    blk = pl.BlockSpec((ts, pw), lambda b, s: (b * ns + s, 0))
    t = bsz * seq
    return pl.pallas_call(
        body, name="pool_fwd", grid=(bsz, ns),
        in_specs=[blk, pl.BlockSpec((POOL_GROUPS, POOL_GROUP_DIM, POOL_GROUP_DIM), lambda b, s: (0, 0, 0)),
                  pl.BlockSpec((1, pw), lambda b, s: (0, 0)), pl.BlockSpec((1, pw), lambda b, s: (0, 0))],
        out_specs=[blk, blk, blk],
        out_shape=[jax.ShapeDtypeStruct((t, pw), BF16), jax.ShapeDtypeStruct((t, pw), F32),
                   jax.ShapeDtypeStruct((t, pw), BF16)],
        scratch_shapes=[pltpu.VMEM((POOL_HALO + ts, pw), F32)],
        compiler_params=_params("arbitrary", "arbitrary"),
    )(pv, pool_w, pool_scale, gain)


def _pool_bwd(dy, mixed, pooled, pool_w, pool_scale, gain, bsz, seq):
    ts = 512
    ns = seq // ts
    pw = POOL_WIDTH

    def body(dy_ref, mixed_ref, pooled_ref, w_ref, sc_ref, g_ref, dpv_ref, dw_ref, dsc_ref, dg_ref, ext):
        b = pl.program_id(0)
        sr = pl.program_id(1)
        s = ns - 1 - sr

        @pl.when(jnp.logical_and(b == 0, sr == 0))
        def _():
            dw_ref[...] = jnp.zeros_like(dw_ref)
            dsc_ref[...] = jnp.zeros_like(dsc_ref)
            dg_ref[...] = jnp.zeros_like(dg_ref)

        @pl.when(sr == 0)
        def _():
            ext[pl.ds(ts, POOL_HALO), :] = jnp.zeros((POOL_HALO, pw), F32)

        mixed = mixed_ref[...]
        sc = sc_ref[...]
        dpm, dgain = _rms_bwd(dy_ref[...], mixed * sc, g_ref[...])
        dg_ref[...] += dgain
        dsc_ref[...] += jnp.sum(dpm * mixed, axis=0, keepdims=True)
        dmixed = (dpm * sc).astype(BF16)
        pos = s * ts + lax.broadcasted_iota(jnp.int32, (ts, 1), 0)
        dpooled = []
        for g, w in enumerate(POOL_WINDOWS):
            lanes = pl.ds(g * POOL_GROUP_DIM, POOL_GROUP_DIM)
            dm = dmixed[:, g * POOL_GROUP_DIM:(g + 1) * POOL_GROUP_DIM]
            dw_ref[g] += _dot_tn(pooled_ref[:, lanes], dm)
            dp = _dot_nt(dm, w_ref[g].astype(BF16))
            dpooled.append(dp)
            cnt = jnp.minimum(pos + 1, w).astype(F32)
            ext[pl.ds(0, ts), lanes] = dp / cnt
        for g, w in enumerate(POOL_WINDOWS):
            lanes = pl.ds(g * POOL_GROUP_DIM, POOL_GROUP_DIM)
            win = ext[pl.ds(0, ts), lanes]
            for i in range(1, w):
                win = win + ext[pl.ds(i, ts), lanes]
            dpv_ref[:, lanes] = (win - dpooled[g]).astype(BF16)
        head = ext[pl.ds(0, POOL_HALO), :]
        ext[pl.ds(ts, POOL_HALO), :] = head

    blk = pl.BlockSpec((ts, pw), lambda b, s: (b * ns + (ns - 1 - s), 0))
    vec = pl.BlockSpec((1, pw), lambda b, s: (0, 0))
    wspec = pl.BlockSpec((POOL_GROUPS, POOL_GROUP_DIM, POOL_GROUP_DIM), lambda b, s: (0, 0, 0))
    t = bsz * seq
    return pl.pallas_call(
        body, name="pool_bwd", grid=(bsz, ns),
        in_specs=[blk, blk, blk, wspec, vec, vec],
        out_specs=[blk, wspec, vec, vec],
        out_shape=[jax.ShapeDtypeStruct((t, pw), BF16),
                   jax.ShapeDtypeStruct((POOL_GROUPS, POOL_GROUP_DIM, POOL_GROUP_DIM), F32),
                   jax.ShapeDtypeStruct((1, pw), F32), jax.ShapeDtypeStruct((1, pw), F32)],
        scratch_shapes=[pltpu.VMEM((ts + POOL_HALO, pw), F32)],
        compiler_params=_params("arbitrary", "arbitrary"),
    )(dy, mixed, pooled, pool_w, pool_scale, gain)


AUX_ONE = 64
AUX_F = 67

ATTN_PREP_ROWS = 256


def _seg_ones(width, seg):
    r = lax.broadcasted_iota(jnp.int32, (width, width), 0) // seg
    c = lax.broadcasted_iota(jnp.int32, (width, width), 1) // seg
    return (r == c).astype(BF16)


def _tri_ones(n, lower):
    r = lax.broadcasted_iota(jnp.int32, (n, n), 0)
    c = lax.broadcasted_iota(jnp.int32, (n, n), 1)
    return ((r >= c) if lower else (r <= c)).astype(BF16)


def _place_pieces(first_lane):
    r = lax.broadcasted_iota(jnp.int32, (3 * LANES, N_HEADS * LANES), 0)
    c = lax.broadcasted_iota(jnp.int32, (3 * LANES, N_HEADS * LANES), 1)
    piece, head = r // LANES, r % LANES
    return jnp.logical_and(head < N_HEADS, c == head * LANES + first_lane + piece).astype(BF16)


def _head_sums(x, seg_ones):
    hi, lo = _split2(x)
    return _dot(hi, seg_ones) + _dot(lo, seg_ones)


def _log_sigmoid(x):
    return jnp.minimum(x, 0.0) - jnp.log(1.0 + jnp.exp(-jnp.abs(x)))


def _attn_prep_fwd(q, k, f, b_forget, q_gain, k_gain, bsz, seq):
    ts = ATTN_PREP_ROWS
    ns = seq // ts
    aw = ATTN_WIDTH
    t = bsz * seq
    seg = _seg_ones(aw, HEAD_DIM)
    tri = _tri_ones(ts, True)

    def body(q_ref, k_ref, f_ref, bf_ref, gq_ref, gk_ref, seg_ref, tri_ref, pq_ref, pk_ref, qp_ref, kp_ref, carry):
        s = pl.program_id(1)

        @pl.when(s == 0)
        def _():
            carry[...] = jnp.zeros_like(carry)

        logf = _log_sigmoid(f_ref[...] + bf_ref[...])
        hi, mid, lo = _split3(logf)
        tri_v = tri_ref[...]
        fc = _dot(tri_v, hi) + _dot(tri_v, mid) + _dot(tri_v, lo) + carry[pl.ds(0, 1), :]
        carry[pl.ds(0, 1), :] = fc[ts - 1:, :]
        pcs = jnp.concatenate(_split3(fc), axis=1)
        lane = lax.broadcasted_iota(jnp.int32, (1, LANES), 1)
        ones_q = jnp.logical_and(lane >= AUX_ONE, lane < AUX_ONE + 3).astype(F32)
        ones_k = jnp.logical_and(lane >= AUX_F, lane < AUX_F + 3).astype(F32)
        seg_v = seg_ref[...]

        def build(x_ref, g_ref, scale, out_ref, ones, place_ref, f_sign):
            xv = x_ref[...]
            r = lax.rsqrt(_head_sums(xv * xv, seg_v) * (1.0 / HEAD_DIM) + EPS)
            xn = xv * r * g_ref[...] * scale
            aux = _dot(pcs, place_ref[...]) * f_sign
            for h in range(N_HEADS):
                pair = xn[:, (h // 2) * LANES:(h // 2 + 1) * LANES]
                feat = pair if h % 2 == 0 else pltpu.roll(pair, HEAD_DIM, 1)
                aux_h = aux[:, h * LANES:(h + 1) * LANES] + ones
                out_ref[:, h * LANES:(h + 1) * LANES] = jnp.where(lane < HEAD_DIM, feat, aux_h).astype(BF16)

        build(q_ref, gq_ref, 0.125, qp_ref, ones_q, pq_ref, 1.0)
        build(k_ref, gk_ref, 1.0, kp_ref, ones_k, pk_ref, -1.0)

    blk = pl.BlockSpec((ts, aw), lambda b, s: (b * ns + s, 0))
    fblk = pl.BlockSpec((ts, LANES), lambda b, s: (b * ns + s, 0))
    oblk = pl.BlockSpec((ts, N_HEADS * LANES), lambda b, s: (b * ns + s, 0))
    const = lambda shape: pl.BlockSpec(shape, lambda b, s: (0, 0))
    return pl.pallas_call(
        body, name="attn_prep_fwd", grid=(bsz, ns),
        in_specs=[blk, blk, fblk, const((1, LANES)), const((1, aw)), const((1, aw)), const((aw, aw)), const((ts, ts)),
                  const((3 * LANES, N_HEADS * LANES)), const((3 * LANES, N_HEADS * LANES))],
        out_specs=[oblk, oblk],
        out_shape=[jax.ShapeDtypeStruct((t, N_HEADS * LANES), BF16)] * 2,
        scratch_shapes=[pltpu.VMEM((8, LANES), F32)],
        compiler_params=_params("arbitrary", "arbitrary"),
    )(q, k, f, b_forget, q_gain, k_gain, seg, tri, _place_pieces(AUX_F), _place_pieces(AUX_ONE))


def _attn_prep_bwd(dqp, dkp, q, k, f, b_forget, q_gain, k_gain, bsz, seq):
    ts = ATTN_PREP_ROWS
    ns = seq // ts
    aw = ATTN_WIDTH
    t = bsz * seq
    seg = _seg_ones(aw, HEAD_DIM)
    tri = _tri_ones(ts, False)

    def body(dqp_ref, dkp_ref, q_ref, k_ref, f_ref, bf_ref, gq_ref, gk_ref, seg_ref, tri_ref,
             dq_ref, dk_ref, df_ref, dgq_ref, dgk_ref, dbf_ref, carry):
        b = pl.program_id(0)
        sr = pl.program_id(1)

        @pl.when(jnp.logical_and(b == 0, sr == 0))
        def _():
            dgq_ref[...] = jnp.zeros_like(dgq_ref)
            dgk_ref[...] = jnp.zeros_like(dgk_ref)
            dbf_ref[...] = jnp.zeros_like(dbf_ref)

        @pl.when(sr == 0)
        def _():
            carry[...] = jnp.zeros_like(carry)

        lane = lax.broadcasted_iota(jnp.int32, (1, LANES), 1)
        seg_v = seg_ref[...]

        def norm_bwd(dp_ref, x_ref, g_ref, scale, dx_ref, dgain_ref):
            parts = []
            for j in range(N_HEADS // 2):
                even = dp_ref[:, (2 * j) * LANES:(2 * j + 1) * LANES]
                odd = dp_ref[:, (2 * j + 1) * LANES:(2 * j + 2) * LANES]
                parts.append(jnp.where(lane < HEAD_DIM, even, pltpu.roll(odd, HEAD_DIM, 1)))
            dxn = jnp.concatenate(parts, axis=1) * scale
            xv = x_ref[...]
            r = lax.rsqrt(_head_sums(xv * xv, seg_v) * (1.0 / HEAD_DIM) + EPS)
            n = xv * r
            dgain_ref[...] += jnp.sum(dxn * n, axis=0, keepdims=True)
            dn = dxn * g_ref[...]
            m = _head_sums(dn * n, seg_v) * (1.0 / HEAD_DIM)
            dx_ref[...] = (r * (dn - n * m)).astype(BF16)

        norm_bwd(dqp_ref, q_ref, gq_ref, 0.125, dq_ref, dgq_ref)
        norm_bwd(dkp_ref, k_ref, gk_ref, 1.0, dk_ref, dgk_ref)

        dfc = jnp.zeros((ts, LANES), F32)
        for h in range(N_HEADS):
            cols = pl.ds(h * LANES, LANES)
            both = jnp.where(lane == AUX_F, dqp_ref[:, cols], 0.0) - jnp.where(lane == AUX_ONE, dkp_ref[:, cols], 0.0)
            dfc = jnp.where(lane == h, jnp.sum(both, axis=1, keepdims=True), dfc)
        hi, mid, lo = _split3(dfc)
        tri_v = tri_ref[...]
        dlogf = _dot(tri_v, hi) + _dot(tri_v, mid) + _dot(tri_v, lo) + carry[pl.ds(0, 1), :]
        carry[pl.ds(0, 1), :] = dlogf[0:1, :]
        df = jnp.where(lane < N_HEADS, dlogf * jax.nn.sigmoid(-(f_ref[...] + bf_ref[...])), 0.0)
        df_ref[...] = df.astype(BF16)
        dbf_ref[...] += jnp.sum(df, axis=0, keepdims=True)

    rev = lambda b, s: (b * ns + (ns - 1 - s), 0)
    blk = pl.BlockSpec((ts, aw), rev)
    fblk = pl.BlockSpec((ts, LANES), rev)
    pblk = pl.BlockSpec((ts, N_HEADS * LANES), rev)
    const = lambda shape: pl.BlockSpec(shape, lambda b, s: (0, 0))
    return pl.pallas_call(
        body, name="attn_prep_bwd", grid=(bsz, ns),
        in_specs=[pblk, pblk, blk, blk, fblk, const((1, LANES)), const((1, aw)), const((1, aw)), const((aw, aw)),
                  const((ts, ts))],
        out_specs=[blk, blk, fblk, const((1, aw)), const((1, aw)), const((1, LANES))],
        out_shape=[jax.ShapeDtypeStruct((t, aw), BF16), jax.ShapeDtypeStruct((t, aw), BF16),
                   jax.ShapeDtypeStruct((t, LANES), BF16), jax.ShapeDtypeStruct((1, aw), F32),
                   jax.ShapeDtypeStruct((1, aw), F32), jax.ShapeDtypeStruct((1, LANES), F32)],
        scratch_shapes=[pltpu.VMEM((8, LANES), F32)],
        compiler_params=_params("arbitrary", "arbitrary"),
    )(dqp, dkp, q, k, f, b_forget, q_gain, k_gain, seg, tri)


ATTN_BLOCK = 512
HEAD_PAIRS = N_HEADS // 2


def _flash_fwd(qp, kp, v, bsz, seq):
    tq = ATTN_BLOCK
    nq = seq // tq
    t = bsz * seq

    def body(q_ref, k_ref, v_ref, o_ref, lse_ref, m_sc, l_sc, acc_sc):
        i = pl.program_id(2)
        j = pl.program_id(3)

        @pl.when(j == 0)
        def _():
            m_sc[...] = jnp.full(m_sc.shape, -jnp.inf, F32)
            l_sc[...] = jnp.zeros_like(l_sc)
            acc_sc[...] = jnp.zeros_like(acc_sc)

        lane = lax.broadcasted_iota(jnp.int32, (1, LANES), 1)
        low = lane < HEAD_DIM

        def step(masked):
            vv = v_ref[...]
            for h in range(2):
                mine = low if h == 0 else jnp.logical_not(low)
                s = _dot_nt(q_ref[:, h * LANES:(h + 1) * LANES], k_ref[:, h * LANES:(h + 1) * LANES])
                if masked:
                    row = lax.broadcasted_iota(jnp.int32, (tq, tq), 0)
                    col = lax.broadcasted_iota(jnp.int32, (tq, tq), 1)
                    s = jnp.where(row >= col, s, -jnp.inf)
                m_prev = m_sc[h]
                m_new = jnp.maximum(m_prev, jnp.max(s, axis=1, keepdims=True))
                p = jnp.exp(s - jnp.tile(m_new, (1, tq // LANES)))
                alpha = jnp.exp(m_prev - m_new)
                l_sc[h] = alpha * l_sc[h] + jnp.sum(p, axis=1, keepdims=True)
                m_sc[h] = m_new
                pv = _dot(p.astype(BF16), jnp.where(mine, vv, jnp.zeros_like(vv)))
                acc_sc[...] = acc_sc[...] * jnp.where(mine, alpha, 1.0) + pv

        @pl.when(j < i)
        def _():
            step(False)

        @pl.when(j == i)
        def _():
            step(True)
            l = jnp.where(low, l_sc[0], l_sc[1])
            m = jnp.where(low, m_sc[0], m_sc[1])
            o_ref[...] = acc_sc[...] / l
            lse_ref[...] = m + jnp.log(l)

    qspec = pl.BlockSpec((tq, 2 * LANES), lambda b, hp, i, j: (b * nq + i, hp))
    kspec = pl.BlockSpec((tq, 2 * LANES), lambda b, hp, i, j: (b * nq + jnp.minimum(i, j), hp))
    vspec = pl.BlockSpec((tq, LANES), lambda b, hp, i, j: (b * nq + jnp.minimum(i, j), hp))
    ospec = pl.BlockSpec((tq, LANES), lambda b, hp, i, j: (b * nq + i, hp))
    return pl.pallas_call(
        body, name="flash_fwd", grid=(bsz, HEAD_PAIRS, nq, nq),
        in_specs=[qspec, kspec, vspec], out_specs=[ospec, ospec],
        out_shape=[jax.ShapeDtypeStruct((t, ATTN_WIDTH), F32), jax.ShapeDtypeStruct((t, ATTN_WIDTH), F32)],
        scratch_shapes=[pltpu.VMEM((2, tq, LANES), F32), pltpu.VMEM((2, tq, LANES), F32), pltpu.VMEM((tq, LANES), F32)],
        compiler_params=_params("arbitrary", "arbitrary", "arbitrary", "arbitrary"),
    )(qp, kp, v)


def _flash_bwd(qp, kp, v, o, do, lse, bsz, seq):
    tq = ATTN_BLOCK
    nq = seq // tq
    t = bsz * seq

    def body(q_ref, k_ref, v_ref, o_ref, do_ref, lse_ref, dq_ref, dk_ref, dv_ref, dk_acc, dv_acc):
        j = pl.program_id(2)
        i = pl.program_id(3)

        @pl.when(jnp.logical_and(j == 0, i == 0))
        def _():
            dq_ref[...] = jnp.zeros_like(dq_ref)

        @pl.when(i == 0)
        def _():
            dk_acc[...] = jnp.zeros_like(dk_acc)
            dv_acc[...] = jnp.zeros_like(dv_acc)

        lane = lax.broadcasted_iota(jnp.int32, (1, LANES), 1)
        low = lane < HEAD_DIM

        def step(masked):
            dov = do_ref[...]
            dd = dov * o_ref[...]
            dob = dov.astype(BF16)
            vv = v_ref[...]
            lse_v = lse_ref[...]
            rows = pl.ds(pl.multiple_of(i * tq, tq), tq)
            for h in range(2):
                mine = low if h == 0 else jnp.logical_not(low)
                cols = pl.ds(h * LANES, LANES)
                qh = q_ref[:, cols]
                kh = k_ref[:, cols]
                s = _dot_nt(qh, kh)
                lse_h = jnp.where(mine, lse_v, pltpu.roll(lse_v, HEAD_DIM, 1))
                p = jnp.exp(s - jnp.tile(lse_h, (1, tq // LANES)))
                if masked:
                    row = lax.broadcasted_iota(jnp.int32, (tq, tq), 0)
                    col = lax.broadcasted_iota(jnp.int32, (tq, tq), 1)
                    p = jnp.where(row >= col, p, 0.0)
                delta = jnp.sum(jnp.where(mine, dd, 0.0), axis=1, keepdims=True)
                dp = _dot_nt(dob, jnp.where(mine, vv, jnp.zeros_like(vv)))
                ds = (p * (dp - delta)).astype(BF16)
                dv_acc[...] += jnp.where(mine, _dot_tn(p.astype(BF16), dob), 0.0)
                dk_acc[:, cols] += _dot_tn(ds, qh)
                dq_ref[rows, cols] += _dot(ds, kh)

        @pl.when(i > j)
        def _():
            step(False)

        @pl.when(i == j)
        def _():
            step(True)

        @pl.when(i == nq - 1)
        def _():
            dk_ref[...] = dk_acc[...]
            dv_ref[...] = dv_acc[...].astype(BF16)

    qspec = pl.BlockSpec((tq, 2 * LANES), lambda b, hp, j, i: (b * nq + jnp.maximum(i, j), hp))
    kspec = pl.BlockSpec((tq, 2 * LANES), lambda b, hp, j, i: (b * nq + j, hp))
    vspec = pl.BlockSpec((tq, LANES), lambda b, hp, j, i: (b * nq + j, hp))
    ospec = pl.BlockSpec((tq, LANES), lambda b, hp, j, i: (b * nq + jnp.maximum(i, j), hp))
    dqspec = pl.BlockSpec((seq, 2 * LANES), lambda b, hp, j, i: (b, hp))
    return pl.pallas_call(
        body, name="flash_bwd", grid=(bsz, HEAD_PAIRS, nq, nq),
        in_specs=[qspec, kspec, vspec, ospec, ospec, ospec], out_specs=[dqspec, kspec, vspec],
        out_shape=[jax.ShapeDtypeStruct((t, N_HEADS * LANES), F32), jax.ShapeDtypeStruct((t, N_HEADS * LANES), F32),
                   jax.ShapeDtypeStruct((t, ATTN_WIDTH), BF16)],
        scratch_shapes=[pltpu.VMEM((tq, 2 * LANES), F32), pltpu.VMEM((tq, LANES), F32)],
        compiler_params=_params("arbitrary", "arbitrary", "arbitrary", "arbitrary"),
    )(qp, kp, v, o, do, lse)


def _mix_out_fwd(o, y_pool, x, gain, w_out):
    t, d = x.shape
    tm = 512
    pw, aw = POOL_WIDTH, ATTN_WIDTH

    def body(o_ref, yp_ref, x_ref, g_ref, w_ref, ycat_ref, y_ref):
        ov = o_ref[...]
        ya = (ov * _rms_scale(ov) * g_ref[...]).astype(BF16)
        ycat = jnp.concatenate([yp_ref[...], ya], axis=1)
        ycat_ref[...] = ycat
        y_ref[...] = x_ref[...] + _dot(ycat, w_ref[...])

    return pl.pallas_call(
        body, name="mix_out_fwd", grid=(t // tm,),
        in_specs=[_rows(tm, aw), _rows(tm, pw), _rows(tm, d), _resident((1, aw)), _resident((pw + aw, d))],
        out_specs=[_rows(tm, pw + aw), _rows(tm, d)],
        out_shape=[jax.ShapeDtypeStruct((t, pw + aw), BF16), jax.ShapeDtypeStruct((t, d), F32)],
        compiler_params=_params("arbitrary"),
    )(o, y_pool, x, gain, w_out)


def _mix_out_bwd(dx, o, gain, w_out):
    t, d = dx.shape
    tm = 512
    pw, aw = POOL_WIDTH, ATTN_WIDTH

    def body(dx_ref, o_ref, g_ref, w_ref, dxb_ref, dyp_ref, do_ref, dg_ref):
        dxb = dx_ref[...].astype(BF16)
        dxb_ref[...] = dxb
        dyp_ref[...] = _dot_nt(dxb, w_ref[pl.ds(0, pw), :])
        dya = _dot_nt(dxb, w_ref[pl.ds(pw, aw), :])
        do, dgain = _rms_bwd(dya, o_ref[...], g_ref[...])
        do_ref[...] = do

        @pl.when(pl.program_id(0) == 0)
        def _():
            dg_ref[...] = jnp.zeros_like(dg_ref)

        dg_ref[...] += dgain

    return pl.pallas_call(
        body, name="mix_out_bwd", grid=(t // tm,),
        in_specs=[_rows(tm, d), _rows(tm, aw), _resident((1, aw)), _resident((pw + aw, d))],
        out_specs=[_rows(tm, d), _rows(tm, pw), _rows(tm, aw), pl.BlockSpec((1, aw), lambda i: (0, 0))],
        out_shape=[jax.ShapeDtypeStruct((t, d), BF16), jax.ShapeDtypeStruct((t, pw), F32),
                   jax.ShapeDtypeStruct((t, aw), F32), jax.ShapeDtypeStruct((1, aw), F32)],
        compiler_params=_params("arbitrary"),
    )(dx, o, gain, w_out)


def _mix_in_bwd(dpv, dq, dk, dv, df, x, dx_res, gain, w_in_t):
    t, d = x.shape
    tm = 512
    pw, aw = POOL_WIDTH, ATTN_WIDTH

    def body(dpv_ref, dq_ref, dk_ref, dv_ref, df_ref, x_ref, dxr_ref, g_ref, w_ref, dh_ref, dx_ref, dxh_ref, dg_ref):
        dh = jnp.concatenate([dpv_ref[...], dq_ref[...], dk_ref[...], dv_ref[...], df_ref[...]], axis=1)
        dh_ref[...] = dh
        dhm = _dot(dh, w_ref[...])
        dx, dgain = _rms_bwd(dhm, x_ref[...], g_ref[...])
        dx = dxr_ref[...] + dx
        dx_ref[...] = dx
        dxh_ref[...] = (0.5 * dx).astype(BF16)

        @pl.when(pl.program_id(0) == 0)
        def _():
            dg_ref[...] = jnp.zeros_like(dg_ref)

        dg_ref[...] += dgain

    return pl.pallas_call(
        body, name="mix_in_bwd", grid=(t // tm,),
        in_specs=[_rows(tm, pw), _rows(tm, aw), _rows(tm, aw), _rows(tm, aw), _rows(tm, LANES), _rows(tm, d),
                  _rows(tm, d), _resident((1, d)), _resident((MIX_PAD, d))],
        out_specs=[_rows(tm, MIX_PAD), _rows(tm, d), _rows(tm, d), pl.BlockSpec((1, d), lambda i: (0, 0))],
        out_shape=[jax.ShapeDtypeStruct((t, MIX_PAD), BF16), jax.ShapeDtypeStruct((t, d), F32),
                   jax.ShapeDtypeStruct((t, d), BF16), jax.ShapeDtypeStruct((1, d), F32)],
        compiler_params=_params("arbitrary"),
    )(dpv, dq, dk, dv, df, x, dx_res, gain, w_in_t)


MESH_IDS = pl.DeviceIdType.MESH


def _me():
    return lax.axis_index("x"), lax.axis_index("y"), lax.axis_index("c")


def _peer(x, y, c, p):
    px = 1 - x if p & 4 else x
    py = 1 - y if p & 2 else y
    pc = 1 - c if p & 1 else c
    return (px, py, pc), 4 * px + 2 * py + pc


HBM_SPEC = pl.BlockSpec(memory_space=pltpu.HBM)
SEM_SPEC = pl.BlockSpec(memory_space=pltpu.SEMAPHORE)
SPLIT_COPY = pltpu.CompilerParams(has_side_effects=pltpu.SideEffectType.DATAFLOW_SIDE_EFFECTING)
PEERS = N_DEV - 1


def _hbm(a):
    return pltpu.with_memory_space_constraint(a, pltpu.HBM)


def _row_block(ref, dev, rows):
    return ref.at[pl.ds(pl.multiple_of(dev * rows, BF16_ROWS), rows)]


def _copy_ends(gather, src, land, me, peer_id):
    if gather:
        rows = src.shape[0]
        return src, _row_block(land, me, rows), _row_block(land, peer_id, rows), src, _row_block(land, me, rows)
    rows = src.shape[0] // N_DEV
    return (_row_block(src, peer_id, rows), land.at[me], land.at[peer_id], _row_block(src, me, rows), land.at[me])


def _land_shape(gather, s):
    return (N_DEV * s.shape[0], s.shape[1]) if gather else (N_DEV, s.shape[0] // N_DEV, s.shape[1])


def _copies_start(groups, gather, name, after=None):
    flat = [s for g in groups for s in g]
    n, ng = len(flat), len(groups)
    lands = [lax.empty(_land_shape(gather, s), s.dtype) for s in flat]
    n_in = 2 * n + (after is not None)

    def body(*refs):
        ins, lnd = refs[:n], refs[n:2 * n]
        sems = refs[n_in:n_in + 2 * ng]
        token = refs[-1]
        x, y, c = _me()
        me = 4 * x + 2 * y + c
        w = 0
        for gi, g in enumerate(groups):
            for k in range(len(g)):
                for p in range(1, N_DEV):
                    peer, peer_id = _peer(x, y, c, p)
                    src, dst, _, _, _ = _copy_ends(gather, ins[w], lnd[w], me, peer_id)
                    pltpu.make_async_remote_copy(src, dst, sems[2 * gi].at[k * PEERS + p - 1],
                                                 sems[2 * gi + 1].at[k * PEERS + p - 1], device_id=peer,
                                                 device_id_type=MESH_IDS).start()
                w += 1
        token[...] = jnp.zeros_like(token)

    sem_shapes = []
    for g in groups:
        sem_shapes += [pltpu.SemaphoreType.DMA((len(g) * PEERS,))] * 2
    out = pl.pallas_call(
        body, name=name,
        out_shape=(*sem_shapes, *[pltpu.HBM(s.shape, s.dtype) for s in flat],
                   *[pltpu.HBM(l.shape, l.dtype) for l in lands], jax.ShapeDtypeStruct((8, LANES), F32)),
        in_specs=[HBM_SPEC] * (2 * n) + [pl.BlockSpec(memory_space=pl.ANY)] * (after is not None),
        out_specs=(*[SEM_SPEC] * (2 * ng), *[HBM_SPEC] * (2 * n), pl.BlockSpec(memory_space=pltpu.VMEM)),
        input_output_aliases={i: 2 * ng + i for i in range(2 * n)},
        compiler_params=SPLIT_COPY,
    )(*[_hbm(s) for s in flat], *[_hbm(l) for l in lands], *([after] if after is not None else []))
    sems, thru, token = out[:2 * ng], out[2 * ng:2 * ng + 2 * n], out[-1]
    res, w = [], 0
    for gi, g in enumerate(groups):
        res.append((sems[2 * gi], sems[2 * gi + 1], list(thru[w:w + len(g)]), list(thru[n + w:n + w + len(g)])))
        w += len(g)
    return res, token


def _copies_wait(started, gather, after, name):
    send, recv, srcs, lands = started
    n = len(srcs)
    after = list(after) if isinstance(after, (list, tuple)) else [after]

    own_shapes = [s.shape if gather else (s.shape[0] // N_DEV, s.shape[1]) for s in srcs]

    def body(*refs):
        ins, lnd = refs[:n], refs[n:2 * n]
        send_sems, recv_sems = refs[2 * n], refs[2 * n + 1]
        bounce, in_sems, out_sems = refs[-n - 2:-2], refs[-2], refs[-1]
        x, y, c = _me()
        me = 4 * x + 2 * y + c
        ends = [_copy_ends(gather, ins[w], lnd[w], me, me)[3:] for w in range(n)]
        loads = [pltpu.make_async_copy(ends[w][0], bounce[w], in_sems.at[w]) for w in range(n)]
        stores = [pltpu.make_async_copy(bounce[w], ends[w][1], out_sems.at[w]) for w in range(n)]
        for cp in loads:
            cp.start()
        for w in range(n):
            loads[w].wait()
            stores[w].start()
        for w in range(n):
            for p in range(1, N_DEV):
                peer, peer_id = _peer(x, y, c, p)
                src, _, arrival, _, _ = _copy_ends(gather, ins[w], lnd[w], me, peer_id)
                cp = pltpu.make_async_remote_copy(src, arrival, send_sems.at[w * PEERS + p - 1],
                                                  recv_sems.at[w * PEERS + p - 1], device_id=peer,
                                                  device_id_type=MESH_IDS)
                cp.wait_send()
                cp.wait_recv()
        for cp in stores:
            cp.wait()

    out = pl.pallas_call(
        body, name=name,
        out_shape=(*[pltpu.HBM(s.shape, s.dtype) for s in srcs], *[pltpu.HBM(l.shape, l.dtype) for l in lands]),
        in_specs=[HBM_SPEC] * (2 * n) + [SEM_SPEC, SEM_SPEC] + [pl.BlockSpec(memory_space=pl.ANY)] * len(after),
        out_specs=[HBM_SPEC] * (2 * n),
        input_output_aliases={i: i for i in range(2 * n)},
        scratch_shapes=[*[pltpu.VMEM(shape, s.dtype) for shape, s in zip(own_shapes, srcs)],
                        pltpu.SemaphoreType.DMA((n,)), pltpu.SemaphoreType.DMA((n,))],
        compiler_params=SPLIT_COPY,
    )(*srcs, *lands, send, recv, *after)
    return list(out[n:])


def _sum_slots(parts, name):
    _, rows, d = parts.shape

    def body(p_ref, o_ref):
        acc = p_ref[0].astype(F32)
        for dev in range(1, N_DEV):
            acc = acc + p_ref[dev].astype(F32)
        o_ref[...] = acc

    return pl.pallas_call(
        body, name=name, grid=(1,),
        in_specs=[pl.BlockSpec((N_DEV, rows, d), lambda i: (0, 0, 0))],
        out_specs=pl.BlockSpec((rows, d), lambda i: (0, 0)), out_shape=jax.ShapeDtypeStruct((rows, d), F32),
        compiler_params=_params("arbitrary"),
    )(parts)


def _adamw_update(w, g, m, v):
    nm = ADAM_B1 * m + (1.0 - ADAM_B1) * g
    nv = ADAM_B2 * v + (1.0 - ADAM_B2) * (g * g)
    m_hat = nm / (1.0 - ADAM_B1 ** ADAM_STEP)
    v_hat = nv / (1.0 - ADAM_B2 ** ADAM_STEP)
    return -ADAM_LR * (m_hat / (jnp.sqrt(v_hat) + ADAM_EPS) + ADAM_WD * w), nm, nv


def _adamw(w, g, m, v, name):
    r, c = w.shape

    def body(w_ref, g_ref, m_ref, v_ref, d_ref, nm_ref, nv_ref):
        d_ref[...], nm_ref[...], nv_ref[...] = _adamw_update(w_ref[...], g_ref[...], m_ref[...], v_ref[...])

    tr = r // 4 if r % 32 == 0 else r
    spec = pl.BlockSpec((tr, c), lambda i: (i, 0))
    shape = jax.ShapeDtypeStruct((r, c), F32)
    return pl.pallas_call(
        body, name=name, grid=(r // tr,), in_specs=[spec] * 4, out_specs=[spec] * 3, out_shape=[shape] * 3,
        compiler_params=_params("arbitrary"),
    )(w, g, m, v)


SUM_ADAMW_COLS = 256


def _sum_adamw(parts, w, m, v, name):
    _, rows, d = parts.shape
    transposed = w.shape != (rows, d)
    tc = SUM_ADAMW_COLS

    def body(p_ref, w_ref, m_ref, v_ref, g_ref, d_ref, nm_ref, nv_ref):
        g = p_ref[0].astype(F32)
        for dev in range(1, N_DEV):
            g = g + p_ref[dev].astype(F32)
        if transposed:
            r = lax.broadcasted_iota(jnp.int32, (rows, rows), 0)
            c = lax.broadcasted_iota(jnp.int32, (rows, rows), 1)
            eye = (r == c).astype(BF16)
            hi, mid, lo = _split3(g)
            g = _dot_tn(hi, eye) + _dot_tn(mid, eye) + _dot_tn(lo, eye)
        g_ref[...] = g
        d_ref[...], nm_ref[...], nv_ref[...] = _adamw_update(w_ref[...], g, m_ref[...], v_ref[...])

    spec = pl.BlockSpec((tc, rows), lambda j: (j, 0)) if transposed else pl.BlockSpec((rows, tc), lambda j: (0, j))
    shape = jax.ShapeDtypeStruct(w.shape, F32)
    return pl.pallas_call(
        body, name=name, grid=(d // tc,),
        in_specs=[pl.BlockSpec((N_DEV, rows, tc), lambda j: (0, 0, j)), spec, spec, spec],
        out_specs=[spec] * 4, out_shape=[shape] * 4,
        compiler_params=_params("arbitrary"),
    )(parts, w, m, v)


def _pad_rows(a, rows):
    return jnp.pad(a, ((0, rows - a.shape[0]), (0, 0)))


def _row1(vec, width=D_MODEL):
    return jnp.pad(vec.reshape(1, -1), ((0, 0), (0, width - vec.shape[-1])))


SMALL_NAMES = ("ffn1_norm", "mix_norm", "ffn2_norm", "b_forget", "pool_scale", "q_norm", "k_norm", "out_norm_pool",
               "out_norm_attn", "pool_w")


def _pack_small(vals):
    rows = [_row1(vals[n].reshape(-1)) for n in SMALL_NAMES[:-1]]
    rows.append(vals["pool_w"].reshape(-1, D_MODEL))
    packed = jnp.concatenate(rows, axis=0)
    return _pad_rows(packed, SMALL_ROWS)


def _unpack_small(packed, like):
    out = {}
    for i, n in enumerate(SMALL_NAMES[:-1]):
        size = like[n].size
        out[n] = packed[i, :size].reshape(like[n].shape)
    first = len(SMALL_NAMES) - 1
    out["pool_w"] = packed[first:first + like["pool_w"].size // D_MODEL].reshape(like["pool_w"].shape)
    return out


def kernel(x, ffn1_norm, ffn1_w_gate, ffn1_w_up, ffn1_w_down, mix_norm, w_in, b_forget, pool_w, pool_scale, q_norm, k_norm, out_norm_pool, out_norm_attn, w_out, ffn2_norm, ffn2_w_gate, ffn2_w_up, ffn2_w_down, loss_target, m_ffn1_norm, m_ffn1_w_gate, m_ffn1_w_up, m_ffn1_w_down, m_mix_norm, m_w_in, m_b_forget, m_pool_w, m_pool_scale, m_q_norm, m_k_norm, m_out_norm_pool, m_out_norm_attn, m_w_out, m_ffn2_norm, m_ffn2_w_gate, m_ffn2_w_up, m_ffn2_w_down, v_ffn1_norm, v_ffn1_w_gate, v_ffn1_w_up, v_ffn1_w_down, v_mix_norm, v_w_in, v_b_forget, v_pool_w, v_pool_scale, v_q_norm, v_k_norm, v_out_norm_pool, v_out_norm_attn, v_w_out, v_ffn2_norm, v_ffn2_w_gate, v_ffn2_w_up, v_ffn2_w_down):
    bsz, seq, d = x.shape
    t = bsz * seq
    x0 = x.reshape(t, d)
    target = loss_target.reshape(t, d)
    in_rows = -(-w_in.shape[1] // BF16_ROWS) * BF16_ROWS

    slabs = [s.astype(BF16) for s in (ffn1_w_gate.T, ffn1_w_up.T, ffn1_w_down, _pad_rows(w_in.T, in_rows), w_out,
                                       ffn2_w_gate.T, ffn2_w_up.T, ffn2_w_down)]
    gathers, started = _copies_start([slabs[0:2], slabs[2:3], slabs[3:4], slabs[4:5], slabs[5:8]], True, "gather_start")

    g1, gm, g2 = ffn1_norm.reshape(1, d), mix_norm.reshape(1, d), ffn2_norm.reshape(1, d)
    bf_row = _row1(b_forget, LANES)
    gq = jnp.tile(q_norm, N_HEADS).reshape(1, ATTN_WIDTH)
    gk = jnp.tile(k_norm, N_HEADS).reshape(1, ATTN_WIDTH)
    scale_row = pool_scale.reshape(1, POOL_WIDTH)
    gp, ga = out_norm_pool.reshape(1, POOL_WIDTH), out_norm_attn.reshape(1, ATTN_WIDTH)

    wg1, wu1 = _copies_wait(gathers[0], True, started, "gather_wait_ffn1_up")
    h1, a1, b1, s1 = _ffn_up(x0, g1, wg1, wu1, "ffn1_up")
    (wd1,) = _copies_wait(gathers[1], True, s1, "gather_wait_ffn1_down")
    (x1,) = _ffn_down(s1, wd1, x0, None, "ffn1_down")
    (win_g,) = _copies_wait(gathers[2], True, x1, "gather_wait_w_in")
    win_cols = win_g.reshape(N_DEV, in_rows, d)[:, :w_in.shape[1]].reshape(MIX_COLS, d)
    win_t = _pad_rows(win_cols, MIX_PAD)
    hm, pv, q, k, v, f = _mix_in_fwd(x1, gm, win_t)
    pooled, mixed, y_pool = _pool_fwd(pv, pool_w, scale_row, gp, bsz, seq)
    qp, kp = _attn_prep_fwd(q, k, f, bf_row, gq, gk, bsz, seq)
    o, lse = _flash_fwd(qp, kp, v, bsz, seq)
    (wout,) = _copies_wait(gathers[3], True, o, "gather_wait_w_out")
    ycat, x2 = _mix_out_fwd(o, y_pool, x1, ga, wout)
    wg2, wu2, wd2 = _copies_wait(gathers[4], True, x2, "gather_wait_ffn2")
    h2, a2, b2, s2 = _ffn_up(x2, g2, wg2, wu2, "ffn2_up")
    dx3, dyh2, loss_part = _ffn_down(s2, wd2, x2, target, "ffn2_down")

    da2, db2 = _ffn_bwd_act(dyh2, a2, b2, wd2, "ffn2_bwd_act")
    dwg2, dwu2 = _wgrad([da2, db2], h2, "ffn2_up_wgrad")
    (dwd2,) = _wgrad([s2], dyh2, "ffn2_down_wgrad")
    (sent_ffn2,), tok = _copies_start([[dwg2, dwu2, dwd2]], False, "exchange_start_ffn2")
    dx2, dg2 = _ffn_bwd_dx(da2, db2, dx3, x2, g2 + tok[0, 0], wg2, wu2, "ffn2_bwd_dx")
    dx2b, dy_pool, do, dga = _mix_out_bwd(dx2, o, ga, wout)
    (dwout,) = _wgrad([ycat], dx2b, "w_out_wgrad")
    (sent_out,), tok = _copies_start([[dwout]], False, "exchange_start_w_out")
    dqp, dkp, dv = _flash_bwd(qp, kp, v, o, do, lse, bsz, seq)
    dq, dk, df, dgq, dgk, dbf = _attn_prep_bwd(dqp, dkp, q, k, f, bf_row + tok[0, 0], gq, gk, bsz, seq)
    dpv, dpool_w, dscale, dgp = _pool_bwd(dy_pool, mixed, pooled, pool_w, scale_row, gp, bsz, seq)
    dhcat, dx1, dyh1, dgm = _mix_in_bwd(dpv, dq, dk, dv, df, x1, dx2, gm, win_t)
    (dwin,) = _wgrad([dhcat], hm, "w_in_wgrad")
    dwin_blocks = jnp.pad(dwin[:MIX_COLS].reshape(N_DEV, w_in.shape[1], d), ((0, 0), (0, in_rows - w_in.shape[1]), (0, 0)))
    (sent_in,), tok = _copies_start([[dwin_blocks.reshape(N_DEV * in_rows, d)]], False, "exchange_start_w_in")
    (dwd1,) = _wgrad([s1], dyh1, "ffn1_down_wgrad")
    (sent_down1,), tok = _copies_start([[dwd1]], False, "exchange_start_ffn1_down", after=tok)
    da1, db1 = _ffn_bwd_act(dyh1, a1, b1, wd1, "ffn1_bwd_act")
    dwg1, dwu1 = _wgrad([da1, db1], h1, "ffn1_up_wgrad")
    (sent_up1,), tok = _copies_start([[dwg1, dwu1]], False, "exchange_start_ffn1_up", after=tok)
    dx0, dg1 = _ffn_bwd_dx(da1, db1, dx1, x0, g1 + tok[0, 0], wg1, wu1, "ffn1_bwd_dx")

    fold = lambda g: g.reshape(N_HEADS, HEAD_DIM).sum(axis=0)
    small_like = dict(ffn1_norm=ffn1_norm, mix_norm=mix_norm, ffn2_norm=ffn2_norm, b_forget=b_forget,
                      pool_scale=pool_scale, q_norm=q_norm, k_norm=k_norm, out_norm_pool=out_norm_pool,
                      out_norm_attn=out_norm_attn, pool_w=pool_w)
    small_part = dict(ffn1_norm=dg1, mix_norm=dgm, ffn2_norm=dg2, b_forget=dbf[0, :N_HEADS], pool_scale=dscale,
                      q_norm=fold(dgq), k_norm=fold(dgk), out_norm_pool=dgp, out_norm_attn=dga, pool_w=dpool_w)
    (sent_small,), tok = _copies_start([[_pack_small(small_part)]], True, "small_grads_start")
    loss = lax.psum(loss_part[0, 0], ("x", "y", "c"))

    weights = dict(ffn1_norm=ffn1_norm, ffn1_w_gate=ffn1_w_gate, ffn1_w_up=ffn1_w_up, ffn1_w_down=ffn1_w_down,
                   mix_norm=mix_norm, w_in=w_in, b_forget=b_forget, pool_w=pool_w, pool_scale=pool_scale,
                   q_norm=q_norm, k_norm=k_norm, out_norm_pool=out_norm_pool, out_norm_attn=out_norm_attn,
                   w_out=w_out, ffn2_norm=ffn2_norm, ffn2_w_gate=ffn2_w_gate, ffn2_w_up=ffn2_w_up,
                   ffn2_w_down=ffn2_w_down)
    m_in = dict(ffn1_norm=m_ffn1_norm, ffn1_w_gate=m_ffn1_w_gate, ffn1_w_up=m_ffn1_w_up, ffn1_w_down=m_ffn1_w_down,
                mix_norm=m_mix_norm, w_in=m_w_in, b_forget=m_b_forget, pool_w=m_pool_w, pool_scale=m_pool_scale,
                q_norm=m_q_norm, k_norm=m_k_norm, out_norm_pool=m_out_norm_pool, out_norm_attn=m_out_norm_attn,
                w_out=m_w_out, ffn2_norm=m_ffn2_norm, ffn2_w_gate=m_ffn2_w_gate, ffn2_w_up=m_ffn2_w_up,
                ffn2_w_down=m_ffn2_w_down)
    v_in = dict(ffn1_norm=v_ffn1_norm, ffn1_w_gate=v_ffn1_w_gate, ffn1_w_up=v_ffn1_w_up, ffn1_w_down=v_ffn1_w_down,
                mix_norm=v_mix_norm, w_in=v_w_in, b_forget=v_b_forget, pool_w=v_pool_w, pool_scale=v_pool_scale,
                q_norm=v_q_norm, k_norm=v_k_norm, out_norm_pool=v_out_norm_pool, out_norm_attn=v_out_norm_attn,
                w_out=v_w_out, ffn2_norm=v_ffn2_norm, ffn2_w_gate=v_ffn2_w_gate, ffn2_w_up=v_ffn2_w_up,
                ffn2_w_down=v_ffn2_w_down)
    grads, delta, new_m, new_v = {}, {}, {}, {}
    after = [tok]
    plan = ((sent_ffn2, "ffn2", ("ffn2_w_gate", "ffn2_w_up", "ffn2_w_down")), (sent_out, "w_out", ("w_out",)),
            (sent_in, "w_in", ("w_in",)), (sent_down1, "ffn1_down", ("ffn1_w_down",)),
            (sent_up1, "ffn1_up", ("ffn1_w_gate", "ffn1_w_up")))
    for sent, tag, names in plan:
        parts = _copies_wait(sent, False, after, f"exchange_wait_{tag}")
        for n, part in zip(names, parts):
            if n == "w_in":
                grads[n] = _sum_slots(part, "sum_grads_w_in")[:w_in.shape[1]].T
                delta[n], new_m[n], new_v[n] = _adamw(weights[n], grads[n], m_in[n], v_in[n], "adamw_w_in")
            else:
                grads[n], delta[n], new_m[n], new_v[n] = _sum_adamw(part, weights[n], m_in[n], v_in[n], f"adamw_{n}")
        after = [new_v[n] for n in names]
    (small_all,) = _copies_wait(sent_small, True, after, "small_grads_wait")
    small_sum = _sum_slots(small_all.reshape(N_DEV, SMALL_ROWS, d), "sum_small_grads")
    grads.update(_unpack_small(small_sum, small_like))
    small_d, small_m, small_v = _adamw(_pack_small(weights), small_sum, _pack_small(m_in), _pack_small(v_in),
                                       "adamw_small")
    delta.update(_unpack_small(small_d, small_like))
    new_m.update(_unpack_small(small_m, small_like))
    new_v.update(_unpack_small(small_v, small_like))

    order = ("ffn1_norm", "ffn1_w_gate", "ffn1_w_up", "ffn1_w_down", "mix_norm", "w_in", "b_forget", "pool_w",
             "pool_scale", "q_norm", "k_norm", "out_norm_pool", "out_norm_attn", "w_out", "ffn2_norm", "ffn2_w_gate",
             "ffn2_w_up", "ffn2_w_down")
    return (loss, dx0.reshape(bsz, seq, d), *[grads[n] for n in order], *[delta[n] for n in order],
            *[new_m[n] for n in order], *[new_v[n] for n in order])
```

```python
import functools

import jax
import jax.numpy as jnp
from jax import lax
from jax.experimental import pallas as pl
from jax.experimental.pallas import tpu as pltpu

F32 = jnp.float32
BF16 = jnp.bfloat16

EPS = 1e-6
D_MODEL = 1024
D_FF = 2816
N_HEADS = 8
HEAD_DIM = 64
POOL_WIDTH = 512
ATTN_WIDTH = 512
POOL_GROUPS = 4
POOL_GROUP_DIM = 128
POOL_WINDOWS = (2, 4, 8, 16)
POOL_HALO = 16
MIX_COLS = POOL_WIDTH + 3 * ATTN_WIDTH + N_HEADS
MIX_PAD = POOL_WIDTH + 3 * ATTN_WIDTH + 128
N_DEV = 8
BF16_ROWS = 16
LANES = 128
VMEM_LIMIT = 56 * 1024 * 1024

ADAM_LR = 0.001
ADAM_B1 = 0.9
ADAM_B2 = 0.999
ADAM_EPS = 1e-08
ADAM_WD = 0.01
ADAM_STEP = 10


def _params(*sem):
    return pltpu.CompilerParams(dimension_semantics=sem, vmem_limit_bytes=VMEM_LIMIT)


def _dot(a, b):
    return jnp.dot(a, b, preferred_element_type=F32)


def _dot_nt(a, b):
    return lax.dot_general(a, b, (((1,), (1,)), ((), ())), preferred_element_type=F32)


def _dot_tn(a, b):
    return lax.dot_general(a, b, (((0,), (0,)), ((), ())), preferred_element_type=F32)


def _resident(shape):
    return pl.BlockSpec(shape, lambda *_: (0,) * len(shape), pipeline_mode=pl.Buffered(1))


def _rows(tm, width):
    return pl.BlockSpec((tm, width), lambda i: (i, 0))


def _rms_scale(x):
    return lax.rsqrt(jnp.mean(x * x, axis=-1, keepdims=True) + EPS)


def _rms_bwd(dh, x, gain):
    r = _rms_scale(x)
    n = x * r
    dgain = jnp.sum(dh * n, axis=0, keepdims=True)
    dn = dh * gain
    dx = r * (dn - n * jnp.mean(dn * n, axis=-1, keepdims=True))
    return dx, dgain


def _split3(x):
    hi = x.astype(BF16)
    r1 = x - hi.astype(F32)
    mid = r1.astype(BF16)
    lo = (r1 - mid.astype(F32)).astype(BF16)
    return hi, mid, lo


def _split2(x):
    hi = x.astype(BF16)
    return hi, (x - hi.astype(F32)).astype(BF16)


FF_CHUNK = 256


def _ffn_up(x, gain, wg_t, wu_t, name):
    t, d = x.shape
    f = wg_t.shape[0]
    tm = 256

    def body(x_ref, g_ref, wg_ref, wu_ref, h_ref, a_ref, b_ref, s_ref):
        xv = x_ref[...]
        h = (xv * _rms_scale(xv) * g_ref[...]).astype(BF16)
        h_ref[...] = h
        for c in range(f // FF_CHUNK):
            sl = pl.ds(c * FF_CHUNK, FF_CHUNK)
            a = _dot_nt(h, wg_ref[sl, :])
            b = _dot_nt(h, wu_ref[sl, :])
            a_ref[:, sl] = a.astype(BF16)
            b_ref[:, sl] = b.astype(BF16)
            s_ref[:, sl] = (a * jax.nn.sigmoid(a) * b).astype(BF16)

    wide = jax.ShapeDtypeStruct((t, f), BF16)
    return pl.pallas_call(
        body, name=name, grid=(t // tm,),
        in_specs=[_rows(tm, d), _resident((1, d)), _resident((f, d)), _resident((f, d))],
        out_specs=[_rows(tm, d), _rows(tm, f), _rows(tm, f), _rows(tm, f)],
        out_shape=[jax.ShapeDtypeStruct((t, d), BF16), wide, wide, wide],
        compiler_params=_params("arbitrary"),
    )(x, gain, wg_t, wu_t)


def _ffn_down(s, wd, x, target, name):
    t, d = x.shape
    f = wd.shape[0]
    tm = 512
    with_loss = target is not None

    def body(*refs):
        if with_loss:
            s_ref, w_ref, x_ref, t_ref, dy_ref, dyh_ref, loss_ref = refs
        else:
            s_ref, w_ref, x_ref, y_ref = refs
        y = x_ref[...] + 0.5 * _dot(s_ref[...], w_ref[...])
        if with_loss:
            e = y - t_ref[...]
            dy = e * (1.0 / d)
            dy_ref[...] = dy
            dyh_ref[...] = (0.5 * dy).astype(BF16)

            @pl.when(pl.program_id(0) == 0)
            def _():
                loss_ref[...] = jnp.zeros_like(loss_ref)

            part = jnp.sum(jnp.sum(e * e, axis=0, keepdims=True), axis=1, keepdims=True)
            loss_ref[...] += part * (0.5 / d)
        else:
            y_ref[...] = y

    in_specs = [_rows(tm, f), _resident((f, d)), _rows(tm, d)]
    args = [s, wd, x]
    if with_loss:
        in_specs.append(_rows(tm, d))
        args.append(target)
        out_shape = [jax.ShapeDtypeStruct((t, d), F32), jax.ShapeDtypeStruct((t, d), BF16),
                     jax.ShapeDtypeStruct((1, 1), F32)]
        out_specs = [_rows(tm, d), _rows(tm, d), pl.BlockSpec((1, 1), lambda i: (0, 0))]
    else:
        out_shape = [jax.ShapeDtypeStruct((t, d), F32)]
        out_specs = [_rows(tm, d)]
    return pl.pallas_call(
        body, name=name, grid=(t // tm,), in_specs=in_specs, out_specs=out_specs, out_shape=out_shape,
        compiler_params=_params("arbitrary"),
    )(*args)


def _ffn_bwd_act(dyh, a, b, wd, name):
    t, d = dyh.shape
    f = wd.shape[0]
    tm = 256

    def body(dy_ref, a_ref, b_ref, wd_ref, da_ref, db_ref):
        dyh_v = dy_ref[...]
        for c in range(f // FF_CHUNK):
            sl = pl.ds(c * FF_CHUNK, FF_CHUNK)
            ds = _dot_nt(dyh_v, wd_ref[sl, :])
            av = a_ref[:, sl].astype(F32)
            bv = b_ref[:, sl].astype(F32)
            sig = jax.nn.sigmoid(av)
            da_ref[:, sl] = (ds * bv * (sig * (1.0 + av * (1.0 - sig)))).astype(BF16)
            db_ref[:, sl] = (ds * (av * sig)).astype(BF16)

    wide = jax.ShapeDtypeStruct((t, f), BF16)
    return pl.pallas_call(
        body, name=name, grid=(t // tm,),
        in_specs=[_rows(tm, d), _rows(tm, f), _rows(tm, f), _resident((f, d))],
        out_specs=[_rows(tm, f), _rows(tm, f)], out_shape=[wide, wide],
        compiler_params=_params("arbitrary"),
    )(dyh, a, b, wd)


def _ffn_bwd_dx(da, db, dy, x, gain, wg_t, wu_t, name):
    t, d = x.shape
    f = wg_t.shape[0]
    tm = 512

    def body(da_ref, db_ref, dy_ref, x_ref, g_ref, wg_ref, wu_ref, dx_ref, dg_ref):
        dh = _dot(da_ref[...], wg_ref[...]) + _dot(db_ref[...], wu_ref[...])
        dx, dgain = _rms_bwd(dh, x_ref[...], g_ref[...])
        dx_ref[...] = dy_ref[...] + dx

        @pl.when(pl.program_id(0) == 0)
        def _():
            dg_ref[...] = jnp.zeros_like(dg_ref)

        dg_ref[...] += dgain

    return pl.pallas_call(
        body, name=name, grid=(t // tm,),
        in_specs=[_rows(tm, f), _rows(tm, f), _rows(tm, d), _rows(tm, d), _resident((1, d)), _resident((f, d)),
                  _resident((f, d))],
        out_specs=[_rows(tm, d), pl.BlockSpec((1, d), lambda i: (0, 0))],
        out_shape=[jax.ShapeDtypeStruct((t, d), F32), jax.ShapeDtypeStruct((1, d), F32)],
        compiler_params=_params("arbitrary"),
    )(da, db, dy, x, gain, wg_t, wu_t)


def _wgrad(lhs, b, name):
    t, n = lhs[0].shape
    d = b.shape[1]
    m = len(lhs)
    tn = n // 2 if n * d * m > (4 << 20) else n
    tk = 256
    nk = t // tk

    def body(*refs):
        a_refs, b_ref, o_refs, accs = refs[:m], refs[m], refs[m + 1:2 * m + 1], refs[2 * m + 1:]
        k = pl.program_id(1)

        @pl.when(k == 0)
        def _():
            for acc in accs:
                acc[...] = jnp.zeros_like(acc)

        bv = b_ref[...]
        for a_ref, acc in zip(a_refs, accs):
            acc[...] += _dot_tn(a_ref[...], bv)

        @pl.when(k == nk - 1)
        def _():
            for o_ref, acc in zip(o_refs, accs):
                o_ref[...] = acc[...].astype(BF16)

    return pl.pallas_call(
        body, name=name, grid=(n // tn, nk),
        in_specs=[pl.BlockSpec((tk, tn), lambda j, k: (k, j))] * m + [pl.BlockSpec((tk, d), lambda j, k: (k, 0))],
        out_specs=[pl.BlockSpec((tn, d), lambda j, k: (j, 0))] * m,
        out_shape=[jax.ShapeDtypeStruct((n, d), BF16)] * m,
        scratch_shapes=[pltpu.VMEM((tn, d), F32)] * m,
        compiler_params=_params("arbitrary", "arbitrary"),
    )(*lhs, b)


def _mix_in_fwd(x, gain, w_in_t):
    t, d = x.shape
    tm = 512
    pw, aw = POOL_WIDTH, ATTN_WIDTH

    def body(x_ref, g_ref, w_ref, hm_ref, pv_ref, q_ref, k_ref, v_ref, f_ref):
        xv = x_ref[...]
        hm = (xv * _rms_scale(xv) * g_ref[...]).astype(BF16)
        hm_ref[...] = hm
        pv_ref[...] = _dot_nt(hm, w_ref[pl.ds(0, pw), :])
        q_ref[...] = _dot_nt(hm, w_ref[pl.ds(pw, aw), :])
        k_ref[...] = _dot_nt(hm, w_ref[pl.ds(pw + aw, aw), :])
        v_ref[...] = _dot_nt(hm, w_ref[pl.ds(pw + 2 * aw, aw), :]).astype(BF16)
        f_ref[...] = _dot_nt(hm, w_ref[pl.ds(pw + 3 * aw, LANES), :])

    return pl.pallas_call(
        body, name="mix_in_fwd", grid=(t // tm,),
        in_specs=[_rows(tm, d), _resident((1, d)), _resident((MIX_PAD, d))],
        out_specs=[_rows(tm, d), _rows(tm, pw), _rows(tm, aw), _rows(tm, aw), _rows(tm, aw), _rows(tm, LANES)],
        out_shape=[jax.ShapeDtypeStruct((t, d), BF16), jax.ShapeDtypeStruct((t, pw), F32),
                   jax.ShapeDtypeStruct((t, aw), F32), jax.ShapeDtypeStruct((t, aw), F32),
                   jax.ShapeDtypeStruct((t, aw), BF16), jax.ShapeDtypeStruct((t, LANES), F32)],
        compiler_params=_params("arbitrary"),
    )(x, gain, w_in_t)


def _pool_fwd(pv, pool_w, pool_scale, gain, bsz, seq):
    ts = 512
    ns = seq // ts
    pw = POOL_WIDTH

    def body(pv_ref, w_ref, sc_ref, g_ref, pooled_ref, mixed_ref, y_ref, ext):
        s = pl.program_id(1)

        @pl.when(s == 0)
        def _():
            ext[pl.ds(0, POOL_HALO), :] = jnp.zeros((POOL_HALO, pw), F32)

        p = pv_ref[...]
        ext[pl.ds(POOL_HALO, ts), :] = p
        pos = s * ts + lax.broadcasted_iota(jnp.int32, (ts, 1), 0)
        parts = []
        for g, w in enumerate(POOL_WINDOWS):
            lanes = pl.ds(g * POOL_GROUP_DIM, POOL_GROUP_DIM)
            win = ext[pl.ds(POOL_HALO, ts), lanes]
            for i in range(1, w):
                win = win + ext[pl.ds(POOL_HALO - i, ts), lanes]
            cnt = jnp.minimum(pos + 1, w).astype(F32)
            pooled = (win / cnt - ext[pl.ds(POOL_HALO, ts), lanes]).astype(BF16)
            pooled_ref[:, lanes] = pooled
            parts.append(_dot(pooled, w_ref[g].astype(BF16)))
        mixed = jnp.concatenate(parts, axis=1)
        mixed_ref[...] = mixed
        pm = mixed * sc_ref[...]
        y_ref[...] = (pm * _rms_scale(pm) * g_ref[...]).astype(BF16)
        ext[pl.ds(0, POOL_HALO), :] = p[ts - POOL_HALO:, :]

    blk = pl.BlockSpec((ts, pw), lambda b, s: (b * ns + s, 0))
    t = bsz * seq
    return pl.pallas_call(
        body, name="pool_fwd", grid=(bsz, ns),
        in_specs=[blk, pl.BlockSpec((POOL_GROUPS, POOL_GROUP_DIM, POOL_GROUP_DIM), lambda b, s: (0, 0, 0)),
                  pl.BlockSpec((1, pw), lambda b, s: (0, 0)), pl.BlockSpec((1, pw), lambda b, s: (0, 0))],
        out_specs=[blk, blk, blk],
        out_shape=[jax.ShapeDtypeStruct((t, pw), BF16), jax.ShapeDtypeStruct((t, pw), F32),
                   jax.ShapeDtypeStruct((t, pw), BF16)],
        scratch_shapes=[pltpu.VMEM((POOL_HALO + ts, pw), F32)],
        compiler_params=_params("arbitrary", "arbitrary"),
    )(pv, pool_w, pool_scale, gain)


def _pool_bwd(dy, mixed, pooled, pool_w, pool_scale, gain, bsz, seq):
    ts = 512
    ns = seq // ts
    pw = POOL_WIDTH

    def body(dy_ref, mixed_ref, pooled_ref, w_ref, sc_ref, g_ref, dpv_ref, dw_ref, dsc_ref, dg_ref, ext):
        b = pl.program_id(0)
        sr = pl.program_id(1)
        s = ns - 1 - sr

        @pl.when(jnp.logical_and(b == 0, sr == 0))
        def _():
            dw_ref[...] = jnp.zeros_like(dw_ref)
            dsc_ref[...] = jnp.zeros_like(dsc_ref)
            dg_ref[...] = jnp.zeros_like(dg_ref)

        @pl.when(sr == 0)
        def _():
            ext[pl.ds(ts, POOL_HALO), :] = jnp.zeros((POOL_HALO, pw), F32)

        mixed = mixed_ref[...]
        sc = sc_ref[...]
        dpm, dgain = _rms_bwd(dy_ref[...], mixed * sc, g_ref[...])
        dg_ref[...] += dgain
        dsc_ref[...] += jnp.sum(dpm * mixed, axis=0, keepdims=True)
        dmixed = (dpm * sc).astype(BF16)
        pos = s * ts + lax.broadcasted_iota(jnp.int32, (ts, 1), 0)
        dpooled = []
        for g, w in enumerate(POOL_WINDOWS):
            lanes = pl.ds(g * POOL_GROUP_DIM, POOL_GROUP_DIM)
            dm = dmixed[:, g * POOL_GROUP_DIM:(g + 1) * POOL_GROUP_DIM]
            dw_ref[g] += _dot_tn(pooled_ref[:, lanes], dm)
            dp = _dot_nt(dm, w_ref[g].astype(BF16))
            dpooled.append(dp)
            cnt = jnp.minimum(pos + 1, w).astype(F32)
            ext[pl.ds(0, ts), lanes] = dp / cnt
        for g, w in enumerate(POOL_WINDOWS):
            lanes = pl.ds(g * POOL_GROUP_DIM, POOL_GROUP_DIM)
            win = ext[pl.ds(0, ts), lanes]
            for i in range(1, w):
                win = win + ext[pl.ds(i, ts), lanes]
            dpv_ref[:, lanes] = (win - dpooled[g]).astype(BF16)
        head = ext[pl.ds(0, POOL_HALO), :]
        ext[pl.ds(ts, POOL_HALO), :] = head

    blk = pl.BlockSpec((ts, pw), lambda b, s: (b * ns + (ns - 1 - s), 0))
    vec = pl.BlockSpec((1, pw), lambda b, s: (0, 0))
    wspec = pl.BlockSpec((POOL_GROUPS, POOL_GROUP_DIM, POOL_GROUP_DIM), lambda b, s: (0, 0, 0))
    t = bsz * seq
    return pl.pallas_call(
        body, name="pool_bwd", grid=(bsz, ns),
        in_specs=[blk, blk, blk, wspec, vec, vec],
        out_specs=[blk, wspec, vec, vec],
        out_shape=[jax.ShapeDtypeStruct((t, pw), BF16),
                   jax.ShapeDtypeStruct((POOL_GROUPS, POOL_GROUP_DIM, POOL_GROUP_DIM), F32),
                   jax.ShapeDtypeStruct((1, pw), F32), jax.ShapeDtypeStruct((1, pw), F32)],
        scratch_shapes=[pltpu.VMEM((ts + POOL_HALO, pw), F32)],
        compiler_params=_params("arbitrary", "arbitrary"),
    )(dy, mixed, pooled, pool_w, pool_scale, gain)


AUX_ONE = 64
AUX_F = 67

ATTN_PREP_ROWS = 256


def _seg_ones(width, seg):
    r = lax.broadcasted_iota(jnp.int32, (width, width), 0) // seg
    c = lax.broadcasted_iota(jnp.int32, (width, width), 1) // seg
    return (r == c).astype(BF16)


def _tri_ones(n, lower):
    r = lax.broadcasted_iota(jnp.int32, (n, n), 0)
    c = lax.broadcasted_iota(jnp.int32, (n, n), 1)
    return ((r >= c) if lower else (r <= c)).astype(BF16)


def _place_pieces(first_lane):
    r = lax.broadcasted_iota(jnp.int32, (3 * LANES, N_HEADS * LANES), 0)
    c = lax.broadcasted_iota(jnp.int32, (3 * LANES, N_HEADS * LANES), 1)
    piece, head = r // LANES, r % LANES
    return jnp.logical_and(head < N_HEADS, c == head * LANES + first_lane + piece).astype(BF16)


def _head_sums(x, seg_ones):
    hi, lo = _split2(x)
    return _dot(hi, seg_ones) + _dot(lo, seg_ones)


def _log_sigmoid(x):
    return jnp.minimum(x, 0.0) - jnp.log(1.0 + jnp.exp(-jnp.abs(x)))


def _attn_prep_fwd(q, k, f, b_forget, q_gain, k_gain, bsz, seq):
    ts = ATTN_PREP_ROWS
    ns = seq // ts
    aw = ATTN_WIDTH
    t = bsz * seq
    seg = _seg_ones(aw, HEAD_DIM)
    tri = _tri_ones(ts, True)

    def body(q_ref, k_ref, f_ref, bf_ref, gq_ref, gk_ref, seg_ref, tri_ref, pq_ref, pk_ref, qp_ref, kp_ref, carry):
        s = pl.program_id(1)

        @pl.when(s == 0)
        def _():
            carry[...] = jnp.zeros_like(carry)

        logf = _log_sigmoid(f_ref[...] + bf_ref[...])
        hi, mid, lo = _split3(logf)
        tri_v = tri_ref[...]
        fc = _dot(tri_v, hi) + _dot(tri_v, mid) + _dot(tri_v, lo) + carry[pl.ds(0, 1), :]
        carry[pl.ds(0, 1), :] = fc[ts - 1:, :]
        pcs = jnp.concatenate(_split3(fc), axis=1)
        lane = lax.broadcasted_iota(jnp.int32, (1, LANES), 1)
        ones_q = jnp.logical_and(lane >= AUX_ONE, lane < AUX_ONE + 3).astype(F32)
        ones_k = jnp.logical_and(lane >= AUX_F, lane < AUX_F + 3).astype(F32)
        seg_v = seg_ref[...]

        def build(x_ref, g_ref, scale, out_ref, ones, place_ref, f_sign):
            xv = x_ref[...]
            r = lax.rsqrt(_head_sums(xv * xv, seg_v) * (1.0 / HEAD_DIM) + EPS)
            xn = xv * r * g_ref[...] * scale
            aux = _dot(pcs, place_ref[...]) * f_sign
            for h in range(N_HEADS):
                pair = xn[:, (h // 2) * LANES:(h // 2 + 1) * LANES]
                feat = pair if h % 2 == 0 else pltpu.roll(pair, HEAD_DIM, 1)
                aux_h = aux[:, h * LANES:(h + 1) * LANES] + ones
                out_ref[:, h * LANES:(h + 1) * LANES] = jnp.where(lane < HEAD_DIM, feat, aux_h).astype(BF16)

        build(q_ref, gq_ref, 0.125, qp_ref, ones_q, pq_ref, 1.0)
        build(k_ref, gk_ref, 1.0, kp_ref, ones_k, pk_ref, -1.0)

    blk = pl.BlockSpec((ts, aw), lambda b, s: (b * ns + s, 0))
    fblk = pl.BlockSpec((ts, LANES), lambda b, s: (b * ns + s, 0))
    oblk = pl.BlockSpec((ts, N_HEADS * LANES), lambda b, s: (b * ns + s, 0))
    const = lambda shape: pl.BlockSpec(shape, lambda b, s: (0, 0))
    return pl.pallas_call(
        body, name="attn_prep_fwd", grid=(bsz, ns),
        in_specs=[blk, blk, fblk, const((1, LANES)), const((1, aw)), const((1, aw)), const((aw, aw)), const((ts, ts)),
                  const((3 * LANES, N_HEADS * LANES)), const((3 * LANES, N_HEADS * LANES))],
        out_specs=[oblk, oblk],
        out_shape=[jax.ShapeDtypeStruct((t, N_HEADS * LANES), BF16)] * 2,
        scratch_shapes=[pltpu.VMEM((8, LANES), F32)],
        compiler_params=_params("arbitrary", "arbitrary"),
    )(q, k, f, b_forget, q_gain, k_gain, seg, tri, _place_pieces(AUX_F), _place_pieces(AUX_ONE))


def _attn_prep_bwd(dqp, dkp, q, k, f, b_forget, q_gain, k_gain, bsz, seq):
    ts = ATTN_PREP_ROWS
    ns = seq // ts
    aw = ATTN_WIDTH
    t = bsz * seq
    seg = _seg_ones(aw, HEAD_DIM)
    tri = _tri_ones(ts, False)

    def body(dqp_ref, dkp_ref, q_ref, k_ref, f_ref, bf_ref, gq_ref, gk_ref, seg_ref, tri_ref,
             dq_ref, dk_ref, df_ref, dgq_ref, dgk_ref, dbf_ref, carry):
        b = pl.program_id(0)
        sr = pl.program_id(1)

        @pl.when(jnp.logical_and(b == 0, sr == 0))
        def _():
            dgq_ref[...] = jnp.zeros_like(dgq_ref)
            dgk_ref[...] = jnp.zeros_like(dgk_ref)
            dbf_ref[...] = jnp.zeros_like(dbf_ref)

        @pl.when(sr == 0)
        def _():
            carry[...] = jnp.zeros_like(carry)

        lane = lax.broadcasted_iota(jnp.int32, (1, LANES), 1)
        seg_v = seg_ref[...]

        def norm_bwd(dp_ref, x_ref, g_ref, scale, dx_ref, dgain_ref):
            parts = []
            for j in range(N_HEADS // 2):
                even = dp_ref[:, (2 * j) * LANES:(2 * j + 1) * LANES]
                odd = dp_ref[:, (2 * j + 1) * LANES:(2 * j + 2) * LANES]
                parts.append(jnp.where(lane < HEAD_DIM, even, pltpu.roll(odd, HEAD_DIM, 1)))
            dxn = jnp.concatenate(parts, axis=1) * scale
            xv = x_ref[...]
            r = lax.rsqrt(_head_sums(xv * xv, seg_v) * (1.0 / HEAD_DIM) + EPS)
            n = xv * r
            dgain_ref[...] += jnp.sum(dxn * n, axis=0, keepdims=True)
            dn = dxn * g_ref[...]
            m = _head_sums(dn * n, seg_v) * (1.0 / HEAD_DIM)
            dx_ref[...] = (r * (dn - n * m)).astype(BF16)

        norm_bwd(dqp_ref, q_ref, gq_ref, 0.125, dq_ref, dgq_ref)
        norm_bwd(dkp_ref, k_ref, gk_ref, 1.0, dk_ref, dgk_ref)

        dfc = jnp.zeros((ts, LANES), F32)
        for h in range(N_HEADS):
            cols = pl.ds(h * LANES, LANES)
            both = jnp.where(lane == AUX_F, dqp_ref[:, cols], 0.0) - jnp.where(lane == AUX_ONE, dkp_ref[:, cols], 0.0)
            dfc = jnp.where(lane == h, jnp.sum(both, axis=1, keepdims=True), dfc)
        hi, mid, lo = _split3(dfc)
        tri_v = tri_ref[...]
        dlogf = _dot(tri_v, hi) + _dot(tri_v, mid) + _dot(tri_v, lo) + carry[pl.ds(0, 1), :]
        carry[pl.ds(0, 1), :] = dlogf[0:1, :]
        df = jnp.where(lane < N_HEADS, dlogf * jax.nn.sigmoid(-(f_ref[...] + bf_ref[...])), 0.0)
        df_ref[...] = df.astype(BF16)
        dbf_ref[...] += jnp.sum(df, axis=0, keepdims=True)

    rev = lambda b, s: (b * ns + (ns - 1 - s), 0)
    blk = pl.BlockSpec((ts, aw), rev)
    fblk = pl.BlockSpec((ts, LANES), rev)
    pblk = pl.BlockSpec((ts, N_HEADS * LANES), rev)
    const = lambda shape: pl.BlockSpec(shape, lambda b, s: (0, 0))
    return pl.pallas_call(
        body, name="attn_prep_bwd", grid=(bsz, ns),
        in_specs=[pblk, pblk, blk, blk, fblk, const((1, LANES)), const((1, aw)), const((1, aw)), const((aw, aw)),
                  const((ts, ts))],
        out_specs=[blk, blk, fblk, const((1, aw)), const((1, aw)), const((1, LANES))],
        out_shape=[jax.ShapeDtypeStruct((t, aw), BF16), jax.ShapeDtypeStruct((t, aw), BF16),
                   jax.ShapeDtypeStruct((t, LANES), BF16), jax.ShapeDtypeStruct((1, aw), F32),
                   jax.ShapeDtypeStruct((1, aw), F32), jax.ShapeDtypeStruct((1, LANES), F32)],
        scratch_shapes=[pltpu.VMEM((8, LANES), F32)],
        compiler_params=_params("arbitrary", "arbitrary"),
    )(dqp, dkp, q, k, f, b_forget, q_gain, k_gain, seg, tri)


ATTN_BLOCK = 512
HEAD_PAIRS = N_HEADS // 2


def _flash_fwd(qp, kp, v, bsz, seq):
    tq = ATTN_BLOCK
    nq = seq // tq
    t = bsz * seq

    def body(q_ref, k_ref, v_ref, o_ref, lse_ref, m_sc, l_sc, acc_sc):
        i = pl.program_id(2)
        m_sc[...] = jnp.full(m_sc.shape, -jnp.inf, F32)
        l_sc[...] = jnp.zeros_like(l_sc)
        acc_sc[...] = jnp.zeros_like(acc_sc)
        lane = lax.broadcasted_iota(jnp.int32, (1, LANES), 1)
        low = lane < HEAD_DIM

        def key_block(j, masked):
            rows = pl.ds(pl.multiple_of(j * tq, tq), tq)
            vv = v_ref[rows, :]
            for h in range(2):
                mine = low if h == 0 else jnp.logical_not(low)
                s = _dot_nt(q_ref[:, h * LANES:(h + 1) * LANES], k_ref[rows, pl.ds(h * LANES, LANES)])
                if masked:
                    row = lax.broadcasted_iota(jnp.int32, (tq, tq), 0)
                    col = lax.broadcasted_iota(jnp.int32, (tq, tq), 1)
                    s = jnp.where(row >= col, s, -jnp.inf)
                m_prev = m_sc[h]
                m_new = jnp.maximum(m_prev, jnp.max(s, axis=1, keepdims=True))
                p = jnp.exp(s - jnp.tile(m_new, (1, tq // LANES)))
                alpha = jnp.exp(m_prev - m_new)
                l_sc[h] = alpha * l_sc[h] + jnp.sum(p, axis=1, keepdims=True)
                m_sc[h] = m_new
                pv = _dot(p.astype(BF16), jnp.where(mine, vv, jnp.zeros_like(vv)))
                acc_sc[...] = acc_sc[...] * jnp.where(mine, alpha, 1.0) + pv

        def below_diagonal(j, carry):
            key_block(j, False)
            return carry

        lax.fori_loop(0, i, below_diagonal, 0)
        key_block(i, True)
        l = jnp.where(low, l_sc[0], l_sc[1])
        m = jnp.where(low, m_sc[0], m_sc[1])
        o_ref[...] = acc_sc[...] / l
        lse_ref[...] = m + jnp.log(l)

    qspec = pl.BlockSpec((tq, 2 * LANES), lambda b, hp, i: (b * nq + i, hp))
    kspec = pl.BlockSpec((seq, 2 * LANES), lambda b, hp, i: (b, hp))
    vspec = pl.BlockSpec((seq, LANES), lambda b, hp, i: (b, hp))
    ospec = pl.BlockSpec((tq, LANES), lambda b, hp, i: (b * nq + i, hp))
    return pl.pallas_call(
        body, name="flash_fwd", grid=(bsz, HEAD_PAIRS, nq),
        in_specs=[qspec, kspec, vspec], out_specs=[ospec, ospec],
        out_shape=[jax.ShapeDtypeStruct((t, ATTN_WIDTH), F32), jax.ShapeDtypeStruct((t, ATTN_WIDTH), F32)],
        scratch_shapes=[pltpu.VMEM((2, tq, LANES), F32), pltpu.VMEM((2, tq, LANES), F32), pltpu.VMEM((tq, LANES), F32)],
        compiler_params=_params("arbitrary", "arbitrary", "arbitrary"),
    )(qp, kp, v)


def _flash_bwd(qp, kp, v, o, do, lse, bsz, seq):
    tq = ATTN_BLOCK
    nq = seq // tq
    t = bsz * seq

    def body(q_ref, k_ref, v_ref, o_ref, do_ref, lse_ref, dq_ref, dk_ref, dv_ref, dk_acc, dv_acc):
        j = pl.program_id(2)

        @pl.when(j == 0)
        def _():
            dq_ref[...] = jnp.zeros_like(dq_ref)

        dk_acc[...] = jnp.zeros_like(dk_acc)
        dv_acc[...] = jnp.zeros_like(dv_acc)
        lane = lax.broadcasted_iota(jnp.int32, (1, LANES), 1)
        low = lane < HEAD_DIM

        def query_block(i, masked):
            rows = pl.ds(pl.multiple_of(i * tq, tq), tq)
            dov = do_ref[rows, :]
            dd = dov * o_ref[rows, :]
            dob = dov.astype(BF16)
            vv = v_ref[...]
            lse_v = lse_ref[rows, :]
            for h in range(2):
                mine = low if h == 0 else jnp.logical_not(low)
                cols = pl.ds(h * LANES, LANES)
                qh = q_ref[rows, cols]
                kh = k_ref[:, cols]
                s = _dot_nt(qh, kh)
                lse_h = jnp.where(mine, lse_v, pltpu.roll(lse_v, HEAD_DIM, 1))
                p = jnp.exp(s - jnp.tile(lse_h, (1, tq // LANES)))
                if masked:
                    row = lax.broadcasted_iota(jnp.int32, (tq, tq), 0)
                    col = lax.broadcasted_iota(jnp.int32, (tq, tq), 1)
                    p = jnp.where(row >= col, p, 0.0)
                delta = jnp.sum(jnp.where(mine, dd, 0.0), axis=1, keepdims=True)
                dp = _dot_nt(dob, jnp.where(mine, vv, jnp.zeros_like(vv)))
                ds = (p * (dp - delta)).astype(BF16)
                dv_acc[...] += jnp.where(mine, _dot_tn(p.astype(BF16), dob), 0.0)
                dk_acc[:, cols] += _dot_tn(ds, qh)
                dq_ref[rows, cols] += _dot(ds, kh)

        def above_diagonal(i, carry):
            query_block(i, False)
            return carry

        query_block(j, True)
        lax.fori_loop(j + 1, nq, above_diagonal, 0)
        dk_ref[...] = dk_acc[...]
        dv_ref[...] = dv_acc[...].astype(BF16)

    qspec = pl.BlockSpec((seq, 2 * LANES), lambda b, hp, j: (b, hp))
    kspec = pl.BlockSpec((tq, 2 * LANES), lambda b, hp, j: (b * nq + j, hp))
    vspec = pl.BlockSpec((tq, LANES), lambda b, hp, j: (b * nq + j, hp))
    ospec = pl.BlockSpec((seq, LANES), lambda b, hp, j: (b, hp))
    return pl.pallas_call(
        body, name="flash_bwd", grid=(bsz, HEAD_PAIRS, nq),
        in_specs=[qspec, kspec, vspec, ospec, ospec, ospec], out_specs=[qspec, kspec, vspec],
        out_shape=[jax.ShapeDtypeStruct((t, N_HEADS * LANES), F32), jax.ShapeDtypeStruct((t, N_HEADS * LANES), F32),
                   jax.ShapeDtypeStruct((t, ATTN_WIDTH), BF16)],
        scratch_shapes=[pltpu.VMEM((tq, 2 * LANES), F32), pltpu.VMEM((tq, LANES), F32)],
        compiler_params=_params("arbitrary", "arbitrary", "arbitrary"),
    )(qp, kp, v, o, do, lse)


def _mix_out_fwd(o, y_pool, x, gain, w_out):
    t, d = x.shape
    tm = 512
    pw, aw = POOL_WIDTH, ATTN_WIDTH

    def body(o_ref, yp_ref, x_ref, g_ref, w_ref, ycat_ref, y_ref):
        ov = o_ref[...]
        ya = (ov * _rms_scale(ov) * g_ref[...]).astype(BF16)
        ycat = jnp.concatenate([yp_ref[...], ya], axis=1)
        ycat_ref[...] = ycat
        y_ref[...] = x_ref[...] + _dot(ycat, w_ref[...])

    return pl.pallas_call(
        body, name="mix_out_fwd", grid=(t // tm,),
        in_specs=[_rows(tm, aw), _rows(tm, pw), _rows(tm, d), _resident((1, aw)), _resident((pw + aw, d))],
        out_specs=[_rows(tm, pw + aw), _rows(tm, d)],
        out_shape=[jax.ShapeDtypeStruct((t, pw + aw), BF16), jax.ShapeDtypeStruct((t, d), F32)],
        compiler_params=_params("arbitrary"),
    )(o, y_pool, x, gain, w_out)


def _mix_out_bwd(dx, o, gain, w_out):
    t, d = dx.shape
    tm = 512
    pw, aw = POOL_WIDTH, ATTN_WIDTH

    def body(dx_ref, o_ref, g_ref, w_ref, dxb_ref, dyp_ref, do_ref, dg_ref):
        dxb = dx_ref[...].astype(BF16)
        dxb_ref[...] = dxb
        dyp_ref[...] = _dot_nt(dxb, w_ref[pl.ds(0, pw), :])
        dya = _dot_nt(dxb, w_ref[pl.ds(pw, aw), :])
        do, dgain = _rms_bwd(dya, o_ref[...], g_ref[...])
        do_ref[...] = do

        @pl.when(pl.program_id(0) == 0)
        def _():
            dg_ref[...] = jnp.zeros_like(dg_ref)

        dg_ref[...] += dgain

    return pl.pallas_call(
        body, name="mix_out_bwd", grid=(t // tm,),
        in_specs=[_rows(tm, d), _rows(tm, aw), _resident((1, aw)), _resident((pw + aw, d))],
        out_specs=[_rows(tm, d), _rows(tm, pw), _rows(tm, aw), pl.BlockSpec((1, aw), lambda i: (0, 0))],
        out_shape=[jax.ShapeDtypeStruct((t, d), BF16), jax.ShapeDtypeStruct((t, pw), F32),
                   jax.ShapeDtypeStruct((t, aw), F32), jax.ShapeDtypeStruct((1, aw), F32)],
        compiler_params=_params("arbitrary"),
    )(dx, o, gain, w_out)


def _mix_in_bwd(dpv, dq, dk, dv, df, x, dx_res, gain, w_in_t):
    t, d = x.shape
    tm = 512
    pw, aw = POOL_WIDTH, ATTN_WIDTH

    def body(dpv_ref, dq_ref, dk_ref, dv_ref, df_ref, x_ref, dxr_ref, g_ref, w_ref, dh_ref, dx_ref, dxh_ref, dg_ref):
        dh = jnp.concatenate([dpv_ref[...], dq_ref[...], dk_ref[...], dv_ref[...], df_ref[...]], axis=1)
        dh_ref[...] = dh
        dhm = _dot(dh, w_ref[...])
        dx, dgain = _rms_bwd(dhm, x_ref[...], g_ref[...])
        dx = dxr_ref[...] + dx
        dx_ref[...] = dx
        dxh_ref[...] = (0.5 * dx).astype(BF16)

        @pl.when(pl.program_id(0) == 0)
        def _():
            dg_ref[...] = jnp.zeros_like(dg_ref)

        dg_ref[...] += dgain

    return pl.pallas_call(
        body, name="mix_in_bwd", grid=(t // tm,),
        in_specs=[_rows(tm, pw), _rows(tm, aw), _rows(tm, aw), _rows(tm, aw), _rows(tm, LANES), _rows(tm, d),
                  _rows(tm, d), _resident((1, d)), _resident((MIX_PAD, d))],
        out_specs=[_rows(tm, MIX_PAD), _rows(tm, d), _rows(tm, d), pl.BlockSpec((1, d), lambda i: (0, 0))],
        out_shape=[jax.ShapeDtypeStruct((t, MIX_PAD), BF16), jax.ShapeDtypeStruct((t, d), F32),
                   jax.ShapeDtypeStruct((t, d), BF16), jax.ShapeDtypeStruct((1, d), F32)],
        compiler_params=_params("arbitrary"),
    )(dpv, dq, dk, dv, df, x, dx_res, gain, w_in_t)


MESH_IDS = pl.DeviceIdType.MESH


def _me():
    return lax.axis_index("x"), lax.axis_index("y"), lax.axis_index("c")


def _peer(x, y, c, p):
    px = 1 - x if p & 4 else x
    py = 1 - y if p & 2 else y
    pc = 1 - c if p & 1 else c
    return (px, py, pc), 4 * px + 2 * py + pc


HBM_SPEC = pl.BlockSpec(memory_space=pltpu.HBM)
SEM_SPEC = pl.BlockSpec(memory_space=pltpu.SEMAPHORE)
SPLIT_COPY = pltpu.CompilerParams(has_side_effects=pltpu.SideEffectType.DATAFLOW_SIDE_EFFECTING)
PEERS = N_DEV - 1


def _hbm(a):
    return pltpu.with_memory_space_constraint(a, pltpu.HBM)


def _row_block(ref, dev, rows):
    return ref.at[pl.ds(pl.multiple_of(dev * rows, BF16_ROWS), rows)]


def _copy_ends(gather, src, land, me, peer_id):
    if gather:
        rows = src.shape[0]
        return src, _row_block(land, me, rows), _row_block(land, peer_id, rows), src, _row_block(land, me, rows)
    rows = src.shape[0] // N_DEV
    return (_row_block(src, peer_id, rows), land.at[me], land.at[peer_id], _row_block(src, me, rows), land.at[me])


def _land_shape(gather, s):
    return (N_DEV * s.shape[0], s.shape[1]) if gather else (N_DEV, s.shape[0] // N_DEV, s.shape[1])


def _copies_start(groups, gather, name, after=None):
    flat = [s for g in groups for s in g]
    n, ng = len(flat), len(groups)
    lands = [lax.empty(_land_shape(gather, s), s.dtype) for s in flat]
    n_in = 2 * n + (after is not None)

    def body(*refs):
        ins, lnd = refs[:n], refs[n:2 * n]
        sems = refs[n_in:n_in + 2 * ng]
        token = refs[-1]
        x, y, c = _me()
        me = 4 * x + 2 * y + c
        w = 0
        for gi, g in enumerate(groups):
            for k in range(len(g)):
                for p in range(1, N_DEV):
                    peer, peer_id = _peer(x, y, c, p)
                    src, dst, _, _, _ = _copy_ends(gather, ins[w], lnd[w], me, peer_id)
                    pltpu.make_async_remote_copy(src, dst, sems[2 * gi].at[k * PEERS + p - 1],
                                                 sems[2 * gi + 1].at[k * PEERS + p - 1], device_id=peer,
                                                 device_id_type=MESH_IDS).start()
                w += 1
        token[...] = jnp.zeros_like(token)

    sem_shapes = []
    for g in groups:
        sem_shapes += [pltpu.SemaphoreType.DMA((len(g) * PEERS,))] * 2
    out = pl.pallas_call(
        body, name=name,
        out_shape=(*sem_shapes, *[pltpu.HBM(s.shape, s.dtype) for s in flat],
                   *[pltpu.HBM(l.shape, l.dtype) for l in lands], jax.ShapeDtypeStruct((8, LANES), F32)),
        in_specs=[HBM_SPEC] * (2 * n) + [pl.BlockSpec(memory_space=pl.ANY)] * (after is not None),
        out_specs=(*[SEM_SPEC] * (2 * ng), *[HBM_SPEC] * (2 * n), pl.BlockSpec(memory_space=pltpu.VMEM)),
        input_output_aliases={i: 2 * ng + i for i in range(2 * n)},
        compiler_params=SPLIT_COPY,
    )(*[_hbm(s) for s in flat], *[_hbm(l) for l in lands], *([after] if after is not None else []))
    sems, thru, token = out[:2 * ng], out[2 * ng:2 * ng + 2 * n], out[-1]
    res, w = [], 0
    for gi, g in enumerate(groups):
        res.append((sems[2 * gi], sems[2 * gi + 1], list(thru[w:w + len(g)]), list(thru[n + w:n + w + len(g)])))
        w += len(g)
    return res, token


def _copies_wait(started, gather, after, name):
    send, recv, srcs, lands = started
    n = len(srcs)
    after = list(after) if isinstance(after, (list, tuple)) else [after]

    own_shapes = [s.shape if gather else (s.shape[0] // N_DEV, s.shape[1]) for s in srcs]

    def body(*refs):
        ins, lnd = refs[:n], refs[n:2 * n]
        send_sems, recv_sems = refs[2 * n], refs[2 * n + 1]
        bounce, in_sems, out_sems = refs[-n - 2:-2], refs[-2], refs[-1]
        x, y, c = _me()
        me = 4 * x + 2 * y + c
        ends = [_copy_ends(gather, ins[w], lnd[w], me, me)[3:] for w in range(n)]
        loads = [pltpu.make_async_copy(ends[w][0], bounce[w], in_sems.at[w]) for w in range(n)]
        stores = [pltpu.make_async_copy(bounce[w], ends[w][1], out_sems.at[w]) for w in range(n)]
        for cp in loads:
            cp.start()
        for w in range(n):
            loads[w].wait()
            stores[w].start()
        for w in range(n):
            for p in range(1, N_DEV):
                peer, peer_id = _peer(x, y, c, p)
                src, _, arrival, _, _ = _copy_ends(gather, ins[w], lnd[w], me, peer_id)
                cp = pltpu.make_async_remote_copy(src, arrival, send_sems.at[w * PEERS + p - 1],
                                                  recv_sems.at[w * PEERS + p - 1], device_id=peer,
                                                  device_id_type=MESH_IDS)
                cp.wait_send()
                cp.wait_recv()
        for cp in stores:
            cp.wait()

    out = pl.pallas_call(
        body, name=name,
        out_shape=(*[pltpu.HBM(s.shape, s.dtype) for s in srcs], *[pltpu.HBM(l.shape, l.dtype) for l in lands]),
        in_specs=[HBM_SPEC] * (2 * n) + [SEM_SPEC, SEM_SPEC] + [pl.BlockSpec(memory_space=pl.ANY)] * len(after),
        out_specs=[HBM_SPEC] * (2 * n),
        input_output_aliases={i: i for i in range(2 * n)},
        scratch_shapes=[*[pltpu.VMEM(shape, s.dtype) for shape, s in zip(own_shapes, srcs)],
                        pltpu.SemaphoreType.DMA((n,)), pltpu.SemaphoreType.DMA((n,))],
        compiler_params=SPLIT_COPY,
    )(*srcs, *lands, send, recv, *after)
    return list(out[n:])


def _sum_slots(parts, name):
    _, rows, d = parts.shape

    def body(p_ref, o_ref):
        acc = p_ref[0].astype(F32)
        for dev in range(1, N_DEV):
            acc = acc + p_ref[dev].astype(F32)
        o_ref[...] = acc

    return pl.pallas_call(
        body, name=name, grid=(1,),
        in_specs=[pl.BlockSpec((N_DEV, rows, d), lambda i: (0, 0, 0))],
        out_specs=pl.BlockSpec((rows, d), lambda i: (0, 0)), out_shape=jax.ShapeDtypeStruct((rows, d), F32),
        compiler_params=_params("arbitrary"),
    )(parts)


def _adamw_update(w, g, m, v):
    nm = ADAM_B1 * m + (1.0 - ADAM_B1) * g
    nv = ADAM_B2 * v + (1.0 - ADAM_B2) * (g * g)
    m_hat = nm / (1.0 - ADAM_B1 ** ADAM_STEP)
    v_hat = nv / (1.0 - ADAM_B2 ** ADAM_STEP)
    return -ADAM_LR * (m_hat / (jnp.sqrt(v_hat) + ADAM_EPS) + ADAM_WD * w), nm, nv


def _adamw(w, g, m, v, name):
    r, c = w.shape

    def body(w_ref, g_ref, m_ref, v_ref, d_ref, nm_ref, nv_ref):
        d_ref[...], nm_ref[...], nv_ref[...] = _adamw_update(w_ref[...], g_ref[...], m_ref[...], v_ref[...])

    tr = r // 4 if r % 32 == 0 else r
    spec = pl.BlockSpec((tr, c), lambda i: (i, 0))
    shape = jax.ShapeDtypeStruct((r, c), F32)
    return pl.pallas_call(
        body, name=name, grid=(r // tr,), in_specs=[spec] * 4, out_specs=[spec] * 3, out_shape=[shape] * 3,
        compiler_params=_params("arbitrary"),
    )(w, g, m, v)


SUM_ADAMW_COLS = 256


def _sum_adamw(parts, w, m, v, name):
    _, rows, d = parts.shape
    transposed = w.shape != (rows, d)
    tc = SUM_ADAMW_COLS

    def body(p_ref, w_ref, m_ref, v_ref, g_ref, d_ref, nm_ref, nv_ref):
        g = p_ref[0].astype(F32)
        for dev in range(1, N_DEV):
            g = g + p_ref[dev].astype(F32)
        if transposed:
            r = lax.broadcasted_iota(jnp.int32, (rows, rows), 0)
            c = lax.broadcasted_iota(jnp.int32, (rows, rows), 1)
            eye = (r == c).astype(BF16)
            hi, mid, lo = _split3(g)
            g = _dot_tn(hi, eye) + _dot_tn(mid, eye) + _dot_tn(lo, eye)
        g_ref[...] = g
        d_ref[...], nm_ref[...], nv_ref[...] = _adamw_update(w_ref[...], g, m_ref[...], v_ref[...])

    spec = pl.BlockSpec((tc, rows), lambda j: (j, 0)) if transposed else pl.BlockSpec((rows, tc), lambda j: (0, j))
    shape = jax.ShapeDtypeStruct(w.shape, F32)
    return pl.pallas_call(
        body, name=name, grid=(d // tc,),
        in_specs=[pl.BlockSpec((N_DEV, rows, tc), lambda j: (0, 0, j)), spec, spec, spec],
        out_specs=[spec] * 4, out_shape=[shape] * 4,
        compiler_params=_params("arbitrary"),
    )(parts, w, m, v)


def _pad_rows(a, rows):
    return jnp.pad(a, ((0, rows - a.shape[0]), (0, 0)))


def _row1(vec, width=D_MODEL):
    return jnp.pad(vec.reshape(1, -1), ((0, 0), (0, width - vec.shape[-1])))


VEC_NAMES = ("ffn1_norm", "mix_norm", "ffn2_norm", "b_forget", "pool_scale", "q_norm", "k_norm", "out_norm_pool",
             "out_norm_attn")
VEC_ROWS = 16


def _pack_vector_grads(parts):
    def body(*refs):
        out_ref = refs[-1]
        out_ref[...] = jnp.zeros_like(out_ref)
        lane = lax.broadcasted_iota(jnp.int32, (1, LANES), 1)
        for i, (name, ref) in enumerate(zip(VEC_NAMES, refs[:-1])):
            val = ref[...]
            if name in ("q_norm", "k_norm"):
                val = val[:, 0:LANES] + val[:, LANES:2 * LANES] + val[:, 2 * LANES:3 * LANES] + val[:, 3 * LANES:]
                val = jnp.where(lane < HEAD_DIM, val + pltpu.roll(val, HEAD_DIM, 1), 0.0)
            out_ref[pl.ds(i, 1), pl.ds(0, val.shape[1])] = val

    vmem = pl.BlockSpec(memory_space=pltpu.VMEM)
    return pl.pallas_call(
        body, name="pack_vector_grads", in_specs=[vmem] * len(parts), out_specs=vmem,
        out_shape=jax.ShapeDtypeStruct((VEC_ROWS, D_MODEL), F32),
    )(*parts)


def _small_adamw(vec_all, pool_all, vec_params, pool_params):
    nv = len(vec_params)
    pool_rows = pool_params[0].shape[0]

    def body(*refs):
        vec_ref, pool_ref = refs[0], refs[1]
        ins = refs[2:2 + 3 * nv + 3]
        outs = refs[2 + 3 * nv + 3:-1]
        rows = refs[-1]
        total = vec_ref[pl.ds(0, VEC_ROWS), :]
        for dev in range(1, N_DEV):
            total = total + vec_ref[pl.ds(dev * VEC_ROWS, VEC_ROWS), :]
        rows[...] = total
        for i in range(nv):
            w_ref, m_ref, v_ref = ins[3 * i:3 * i + 3]
            g = rows[pl.ds(i, 1), pl.ds(0, w_ref.shape[1])]
            outs[4 * i][...] = g
            outs[4 * i + 1][...], outs[4 * i + 2][...], outs[4 * i + 3][...] = _adamw_update(
                w_ref[...], g, m_ref[...], v_ref[...])
        g = pool_ref[pl.ds(0, pool_rows), :]
        for dev in range(1, N_DEV):
            g = g + pool_ref[pl.ds(dev * pool_rows, pool_rows), :]
        w_ref, m_ref, v_ref = ins[3 * nv:]
        outs[4 * nv][...] = g
        outs[4 * nv + 1][...], outs[4 * nv + 2][...], outs[4 * nv + 3][...] = _adamw_update(
            w_ref[...], g, m_ref[...], v_ref[...])

    vmem = pl.BlockSpec(memory_space=pltpu.VMEM)
    flat = [a for trio in vec_params for a in trio] + list(pool_params)
    out_shape = []
    for trio in list(vec_params) + [pool_params]:
        out_shape += [jax.ShapeDtypeStruct(trio[0].shape, F32)] * 4
    return pl.pallas_call(
        body, name="adamw_small", in_specs=[vmem] * (2 + len(flat)), out_specs=[vmem] * len(out_shape),
        out_shape=out_shape, scratch_shapes=[pltpu.VMEM((VEC_ROWS, D_MODEL), F32)],
    )(vec_all, pool_all, *flat)


def kernel(x, ffn1_norm, ffn1_w_gate, ffn1_w_up, ffn1_w_down, mix_norm, w_in, b_forget, pool_w, pool_scale, q_norm, k_norm, out_norm_pool, out_norm_attn, w_out, ffn2_norm, ffn2_w_gate, ffn2_w_up, ffn2_w_down, loss_target, m_ffn1_norm, m_ffn1_w_gate, m_ffn1_w_up, m_ffn1_w_down, m_mix_norm, m_w_in, m_b_forget, m_pool_w, m_pool_scale, m_q_norm, m_k_norm, m_out_norm_pool, m_out_norm_attn, m_w_out, m_ffn2_norm, m_ffn2_w_gate, m_ffn2_w_up, m_ffn2_w_down, v_ffn1_norm, v_ffn1_w_gate, v_ffn1_w_up, v_ffn1_w_down, v_mix_norm, v_w_in, v_b_forget, v_pool_w, v_pool_scale, v_q_norm, v_k_norm, v_out_norm_pool, v_out_norm_attn, v_w_out, v_ffn2_norm, v_ffn2_w_gate, v_ffn2_w_up, v_ffn2_w_down):
    bsz, seq, d = x.shape
    t = bsz * seq
    x0 = x.reshape(t, d)
    target = loss_target.reshape(t, d)
    in_rows = -(-w_in.shape[1] // BF16_ROWS) * BF16_ROWS

    slabs = [s.astype(BF16) for s in (ffn1_w_gate.T, ffn1_w_up.T, ffn1_w_down, _pad_rows(w_in.T, in_rows), w_out,
                                       ffn2_w_gate.T, ffn2_w_up.T, ffn2_w_down)]
    gathers, started = _copies_start([slabs[0:2], slabs[2:3], slabs[3:4], slabs[4:5], slabs[5:8]], True, "gather_start")

    g1, gm, g2 = ffn1_norm.reshape(1, d), mix_norm.reshape(1, d), ffn2_norm.reshape(1, d)
    bf_row = _row1(b_forget, LANES)
    gq = jnp.tile(q_norm, N_HEADS).reshape(1, ATTN_WIDTH)
    gk = jnp.tile(k_norm, N_HEADS).reshape(1, ATTN_WIDTH)
    scale_row = pool_scale.reshape(1, POOL_WIDTH)
    gp, ga = out_norm_pool.reshape(1, POOL_WIDTH), out_norm_attn.reshape(1, ATTN_WIDTH)

    wg1, wu1 = _copies_wait(gathers[0], True, started, "gather_wait_ffn1_up")
    h1, a1, b1, s1 = _ffn_up(x0, g1, wg1, wu1, "ffn1_up")
    (wd1,) = _copies_wait(gathers[1], True, s1, "gather_wait_ffn1_down")
    (x1,) = _ffn_down(s1, wd1, x0, None, "ffn1_down")
    (win_g,) = _copies_wait(gathers[2], True, x1, "gather_wait_w_in")
    win_cols = win_g.reshape(N_DEV, in_rows, d)[:, :w_in.shape[1]].reshape(MIX_COLS, d)
    win_t = _pad_rows(win_cols, MIX_PAD)
    hm, pv, q, k, v, f = _mix_in_fwd(x1, gm, win_t)
    pooled, mixed, y_pool = _pool_fwd(pv, pool_w, scale_row, gp, bsz, seq)
    qp, kp = _attn_prep_fwd(q, k, f, bf_row, gq, gk, bsz, seq)
    o, lse = _flash_fwd(qp, kp, v, bsz, seq)
    (wout,) = _copies_wait(gathers[3], True, o, "gather_wait_w_out")
    ycat, x2 = _mix_out_fwd(o, y_pool, x1, ga, wout)
    wg2, wu2, wd2 = _copies_wait(gathers[4], True, x2, "gather_wait_ffn2")
    h2, a2, b2, s2 = _ffn_up(x2, g2, wg2, wu2, "ffn2_up")
    dx3, dyh2, loss_part = _ffn_down(s2, wd2, x2, target, "ffn2_down")

    da2, db2 = _ffn_bwd_act(dyh2, a2, b2, wd2, "ffn2_bwd_act")
    dwg2, dwu2 = _wgrad([da2, db2], h2, "ffn2_up_wgrad")
    (dwd2,) = _wgrad([s2], dyh2, "ffn2_down_wgrad")
    (sent_ffn2,), tok = _copies_start([[dwg2, dwu2, dwd2]], False, "exchange_start_ffn2")
    dx2, dg2 = _ffn_bwd_dx(da2, db2, dx3, x2, g2 + tok[0, 0], wg2, wu2, "ffn2_bwd_dx")
    dx2b, dy_pool, do, dga = _mix_out_bwd(dx2, o, ga, wout)
    (dwout,) = _wgrad([ycat], dx2b, "w_out_wgrad")
    (sent_out,), tok = _copies_start([[dwout]], False, "exchange_start_w_out")
    dqp, dkp, dv = _flash_bwd(qp, kp, v, o, do, lse, bsz, seq)
    dq, dk, df, dgq, dgk, dbf = _attn_prep_bwd(dqp, dkp, q, k, f, bf_row + tok[0, 0], gq, gk, bsz, seq)
    dpv, dpool_w, dscale, dgp = _pool_bwd(dy_pool, mixed, pooled, pool_w, scale_row, gp, bsz, seq)
    dhcat, dx1, dyh1, dgm = _mix_in_bwd(dpv, dq, dk, dv, df, x1, dx2, gm, win_t)
    (dwin,) = _wgrad([dhcat], hm, "w_in_wgrad")
    dwin_blocks = jnp.pad(dwin[:MIX_COLS].reshape(N_DEV, w_in.shape[1], d), ((0, 0), (0, in_rows - w_in.shape[1]), (0, 0)))
    (sent_in,), tok = _copies_start([[dwin_blocks.reshape(N_DEV * in_rows, d)]], False, "exchange_start_w_in")
    (dwd1,) = _wgrad([s1], dyh1, "ffn1_down_wgrad")
    (sent_down1,), tok = _copies_start([[dwd1]], False, "exchange_start_ffn1_down", after=tok)
    da1, db1 = _ffn_bwd_act(dyh1, a1, b1, wd1, "ffn1_bwd_act")
    dwg1, dwu1 = _wgrad([da1, db1], h1, "ffn1_up_wgrad")
    (sent_up1,), tok = _copies_start([[dwg1, dwu1]], False, "exchange_start_ffn1_up", after=tok)
    dx0, dg1 = _ffn_bwd_dx(da1, db1, dx1, x0, g1 + tok[0, 0], wg1, wu1, "ffn1_bwd_dx")

    pool_rows = POOL_GROUPS * POOL_GROUP_DIM
    packed = _pack_vector_grads([dg1, dgm, dg2, dbf, dscale, dgq, dgk, dgp, dga])
    (sent_small,), tok = _copies_start([[packed, dpool_w.reshape(pool_rows, POOL_GROUP_DIM)]], True, "small_grads_start")
    loss = lax.psum(loss_part[0, 0], ("x", "y", "c"))

    weights = dict(ffn1_norm=ffn1_norm, ffn1_w_gate=ffn1_w_gate, ffn1_w_up=ffn1_w_up, ffn1_w_down=ffn1_w_down,
                   mix_norm=mix_norm, w_in=w_in, b_forget=b_forget, pool_w=pool_w, pool_scale=pool_scale,
                   q_norm=q_norm, k_norm=k_norm, out_norm_pool=out_norm_pool, out_norm_attn=out_norm_attn,
                   w_out=w_out, ffn2_norm=ffn2_norm, ffn2_w_gate=ffn2_w_gate, ffn2_w_up=ffn2_w_up,
                   ffn2_w_down=ffn2_w_down)
    m_in = dict(ffn1_norm=m_ffn1_norm, ffn1_w_gate=m_ffn1_w_gate, ffn1_w_up=m_ffn1_w_up, ffn1_w_down=m_ffn1_w_down,
                mix_norm=m_mix_norm, w_in=m_w_in, b_forget=m_b_forget, pool_w=m_pool_w, pool_scale=m_pool_scale,
                q_norm=m_q_norm, k_norm=m_k_norm, out_norm_pool=m_out_norm_pool, out_norm_attn=m_out_norm_attn,
                w_out=m_w_out, ffn2_norm=m_ffn2_norm, ffn2_w_gate=m_ffn2_w_gate, ffn2_w_up=m_ffn2_w_up,
                ffn2_w_down=m_ffn2_w_down)
    v_in = dict(ffn1_norm=v_ffn1_norm, ffn1_w_gate=v_ffn1_w_gate, ffn1_w_up=v_ffn1_w_up, ffn1_w_down=v_ffn1_w_down,
                mix_norm=v_mix_norm, w_in=v_w_in, b_forget=v_b_forget, pool_w=v_pool_w, pool_scale=v_pool_scale,
                q_norm=v_q_norm, k_norm=v_k_norm, out_norm_pool=v_out_norm_pool, out_norm_attn=v_out_norm_attn,
                w_out=v_w_out, ffn2_norm=v_ffn2_norm, ffn2_w_gate=v_ffn2_w_gate, ffn2_w_up=v_ffn2_w_up,
                ffn2_w_down=v_ffn2_w_down)
    grads, delta, new_m, new_v = {}, {}, {}, {}
    after = [tok]
    plan = ((sent_ffn2, "ffn2", ("ffn2_w_gate", "ffn2_w_up", "ffn2_w_down")), (sent_out, "w_out", ("w_out",)),
            (sent_in, "w_in", ("w_in",)), (sent_down1, "ffn1_down", ("ffn1_w_down",)),
            (sent_up1, "ffn1_up", ("ffn1_w_gate", "ffn1_w_up")))
    for sent, tag, names in plan:
        parts = _copies_wait(sent, False, after, f"exchange_wait_{tag}")
        for n, part in zip(names, parts):
            if n == "w_in":
                grads[n] = _sum_slots(part, "sum_grads_w_in")[:w_in.shape[1]].T
                delta[n], new_m[n], new_v[n] = _adamw(weights[n], grads[n], m_in[n], v_in[n], "adamw_w_in")
            else:
                grads[n], delta[n], new_m[n], new_v[n] = _sum_adamw(part, weights[n], m_in[n], v_in[n], f"adamw_{n}")
        after = [new_v[n] for n in names]
    vec_all, pool_all = _copies_wait(sent_small, True, after, "small_grads_wait")
    as_row = lambda a: a.reshape(1, -1)
    as_pool = lambda a: a.reshape(pool_rows, POOL_GROUP_DIM)
    small = _small_adamw(vec_all, pool_all, [tuple(as_row(z[n]) for z in (weights, m_in, v_in)) for n in VEC_NAMES],
                         tuple(as_pool(z["pool_w"]) for z in (weights, m_in, v_in)))
    for i, n in enumerate(VEC_NAMES + ("pool_w",)):
        grads[n], delta[n], new_m[n], new_v[n] = (a.reshape(weights[n].shape) for a in small[4 * i:4 * i + 4])

    order = ("ffn1_norm", "ffn1_w_gate", "ffn1_w_up", "ffn1_w_down", "mix_norm", "w_in", "b_forget", "pool_w",
             "pool_scale", "q_norm", "k_norm", "out_norm_pool", "out_norm_attn", "w_out", "ffn2_norm", "ffn2_w_gate",
             "ffn2_w_up", "ffn2_w_down")
    return (loss, dx0.reshape(bsz, seq, d), *[grads[n] for n in order], *[delta[n] for n in order],
            *[new_m[n] for n in order], *[new_v[n] for n in order])
```

```python
import functools

import jax
import jax.numpy as jnp
from jax import lax
from jax.experimental import pallas as pl
from jax.experimental.pallas import tpu as pltpu

F32 = jnp.float32
BF16 = jnp.bfloat16

EPS = 1e-6
D_MODEL = 1024
D_FF = 2816
N_HEADS = 8
HEAD_DIM = 64
POOL_WIDTH = 512
ATTN_WIDTH = 512
POOL_GROUPS = 4
POOL_GROUP_DIM = 128
POOL_WINDOWS = (2, 4, 8, 16)
POOL_HALO = 16
MIX_COLS = POOL_WIDTH + 3 * ATTN_WIDTH + N_HEADS
MIX_PAD = POOL_WIDTH + 3 * ATTN_WIDTH + 128
N_DEV = 8
BF16_ROWS = 16
LANES = 128
VMEM_LIMIT = 56 * 1024 * 1024

ADAM_LR = 0.001
ADAM_B1 = 0.9
ADAM_B2 = 0.999
ADAM_EPS = 1e-08
ADAM_WD = 0.01
ADAM_STEP = 10


def _params(*sem):
    return pltpu.CompilerParams(dimension_semantics=sem, vmem_limit_bytes=VMEM_LIMIT)


def _dot(a, b):
    return jnp.dot(a, b, preferred_element_type=F32)


def _dot_nt(a, b):
    return lax.dot_general(a, b, (((1,), (1,)), ((), ())), preferred_element_type=F32)


def _dot_tn(a, b):
    return lax.dot_general(a, b, (((0,), (0,)), ((), ())), preferred_element_type=F32)


def _resident(shape):
    return pl.BlockSpec(shape, lambda *_: (0,) * len(shape), pipeline_mode=pl.Buffered(1))


def _rows(tm, width):
    return pl.BlockSpec((tm, width), lambda i: (i, 0))


def _rms_scale(x):
    return lax.rsqrt(jnp.mean(x * x, axis=-1, keepdims=True) + EPS)


def _rms_bwd(dh, x, gain):
    r = _rms_scale(x)
    n = x * r
    dgain = jnp.sum(dh * n, axis=0, keepdims=True)
    dn = dh * gain
    dx = r * (dn - n * jnp.mean(dn * n, axis=-1, keepdims=True))
    return dx, dgain


def _split3(x):
    hi = x.astype(BF16)
    r1 = x - hi.astype(F32)
    mid = r1.astype(BF16)
    lo = (r1 - mid.astype(F32)).astype(BF16)
    return hi, mid, lo


def _split2(x):
    hi = x.astype(BF16)
    return hi, (x - hi.astype(F32)).astype(BF16)


FF_CHUNK = 256


def _ffn_up(x, gain, wg_t, wu_t, name):
    t, d = x.shape
    f = wg_t.shape[0]
    tm = 256

    def body(x_ref, g_ref, wg_ref, wu_ref, h_ref, a_ref, b_ref, s_ref):
        xv = x_ref[...]
        h = (xv * _rms_scale(xv) * g_ref[...]).astype(BF16)
        h_ref[...] = h
        for c in range(f // FF_CHUNK):
            sl = pl.ds(c * FF_CHUNK, FF_CHUNK)
            a = _dot_nt(h, wg_ref[sl, :])
            b = _dot_nt(h, wu_ref[sl, :])
            a_ref[:, sl] = a.astype(BF16)
            b_ref[:, sl] = b.astype(BF16)
            s_ref[:, sl] = (a * jax.nn.sigmoid(a) * b).astype(BF16)

    wide = jax.ShapeDtypeStruct((t, f), BF16)
    return pl.pallas_call(
        body, name=name, grid=(t // tm,),
        in_specs=[_rows(tm, d), _resident((1, d)), _resident((f, d)), _resident((f, d))],
        out_specs=[_rows(tm, d), _rows(tm, f), _rows(tm, f), _rows(tm, f)],
        out_shape=[jax.ShapeDtypeStruct((t, d), BF16), wide, wide, wide],
        compiler_params=_params("arbitrary"),
    )(x, gain, wg_t, wu_t)


def _ffn_down(s, wd, x, target, name):
    t, d = x.shape
    f = wd.shape[0]
    tm = 512
    with_loss = target is not None

    def body(*refs):
        if with_loss:
            s_ref, w_ref, x_ref, t_ref, dy_ref, dyh_ref, loss_ref = refs
        else:
            s_ref, w_ref, x_ref, y_ref = refs
        y = x_ref[...] + 0.5 * _dot(s_ref[...], w_ref[...])
        if with_loss:
            e = y - t_ref[...]
            dy = e * (1.0 / d)
            dy_ref[...] = dy
            dyh_ref[...] = (0.5 * dy).astype(BF16)

            @pl.when(pl.program_id(0) == 0)
            def _():
                loss_ref[...] = jnp.zeros_like(loss_ref)

            part = jnp.sum(jnp.sum(e * e, axis=0, keepdims=True), axis=1, keepdims=True)
            loss_ref[...] += part * (0.5 / d)
        else:
            y_ref[...] = y

    in_specs = [_rows(tm, f), _resident((f, d)), _rows(tm, d)]
    args = [s, wd, x]
    if with_loss:
        in_specs.append(_rows(tm, d))
        args.append(target)
        out_shape = [jax.ShapeDtypeStruct((t, d), F32), jax.ShapeDtypeStruct((t, d), BF16),
                     jax.ShapeDtypeStruct((1, 1), F32)]
        out_specs = [_rows(tm, d), _rows(tm, d), pl.BlockSpec((1, 1), lambda i: (0, 0))]
    else:
        out_shape = [jax.ShapeDtypeStruct((t, d), F32)]
        out_specs = [_rows(tm, d)]
    return pl.pallas_call(
        body, name=name, grid=(t // tm,), in_specs=in_specs, out_specs=out_specs, out_shape=out_shape,
        compiler_params=_params("arbitrary"),
    )(*args)


def _ffn_bwd_act(dyh, a, b, wd, name):
    t, d = dyh.shape
    f = wd.shape[0]
    tm = 256

    def body(dy_ref, a_ref, b_ref, wd_ref, da_ref, db_ref):
        dyh_v = dy_ref[...]
        for c in range(f // FF_CHUNK):
            sl = pl.ds(c * FF_CHUNK, FF_CHUNK)
            ds = _dot_nt(dyh_v, wd_ref[sl, :])
            av = a_ref[:, sl].astype(F32)
            bv = b_ref[:, sl].astype(F32)
            sig = jax.nn.sigmoid(av)
            da_ref[:, sl] = (ds * bv * (sig * (1.0 + av * (1.0 - sig)))).astype(BF16)
            db_ref[:, sl] = (ds * (av * sig)).astype(BF16)

    wide = jax.ShapeDtypeStruct((t, f), BF16)
    return pl.pallas_call(
        body, name=name, grid=(t // tm,),
        in_specs=[_rows(tm, d), _rows(tm, f), _rows(tm, f), _resident((f, d))],
        out_specs=[_rows(tm, f), _rows(tm, f)], out_shape=[wide, wide],
        compiler_params=_params("arbitrary"),
    )(dyh, a, b, wd)


def _ffn_bwd_dx(da, db, dy, x, gain, wg_t, wu_t, name):
    t, d = x.shape
    f = wg_t.shape[0]
    tm = 512

    def body(da_ref, db_ref, dy_ref, x_ref, g_ref, wg_ref, wu_ref, dx_ref, dg_ref):
        dh = _dot(da_ref[...], wg_ref[...]) + _dot(db_ref[...], wu_ref[...])
        dx, dgain = _rms_bwd(dh, x_ref[...], g_ref[...])
        dx_ref[...] = dy_ref[...] + dx

        @pl.when(pl.program_id(0) == 0)
        def _():
            dg_ref[...] = jnp.zeros_like(dg_ref)

        dg_ref[...] += dgain

    return pl.pallas_call(
        body, name=name, grid=(t // tm,),
        in_specs=[_rows(tm, f), _rows(tm, f), _rows(tm, d), _rows(tm, d), _resident((1, d)), _resident((f, d)),
                  _resident((f, d))],
        out_specs=[_rows(tm, d), pl.BlockSpec((1, d), lambda i: (0, 0))],
        out_shape=[jax.ShapeDtypeStruct((t, d), F32), jax.ShapeDtypeStruct((1, d), F32)],
        compiler_params=_params("arbitrary"),
    )(da, db, dy, x, gain, wg_t, wu_t)


def _wgrad(lhs, b, name):
    t, n = lhs[0].shape
    d = b.shape[1]
    m = len(lhs)
    tn = n // 2 if n * d * m > (4 << 20) else n
    tk = 256
    nk = t // tk

    def body(*refs):
        a_refs, b_ref, o_refs, accs = refs[:m], refs[m], refs[m + 1:2 * m + 1], refs[2 * m + 1:]
        k = pl.program_id(1)

        @pl.when(k == 0)
        def _():
            for acc in accs:
                acc[...] = jnp.zeros_like(acc)

        bv = b_ref[...]
        for a_ref, acc in zip(a_refs, accs):
            acc[...] += _dot_tn(a_ref[...], bv)

        @pl.when(k == nk - 1)
        def _():
            for o_ref, acc in zip(o_refs, accs):
                o_ref[...] = acc[...].astype(BF16)

    return pl.pallas_call(
        body, name=name, grid=(n // tn, nk),
        in_specs=[pl.BlockSpec((tk, tn), lambda j, k: (k, j))] * m + [pl.BlockSpec((tk, d), lambda j, k: (k, 0))],
        out_specs=[pl.BlockSpec((tn, d), lambda j, k: (j, 0))] * m,
        out_shape=[jax.ShapeDtypeStruct((n, d), BF16)] * m,
        scratch_shapes=[pltpu.VMEM((tn, d), F32)] * m,
        compiler_params=_params("arbitrary", "arbitrary"),
    )(*lhs, b)


def _mix_in_fwd(x, gain, w_in_t):
    t, d = x.shape
    tm = 512
    pw, aw = POOL_WIDTH, ATTN_WIDTH

    def body(x_ref, g_ref, w_ref, hm_ref, pv_ref, q_ref, k_ref, v_ref, f_ref):
        xv = x_ref[...]
        hm = (xv * _rms_scale(xv) * g_ref[...]).astype(BF16)
        hm_ref[...] = hm
        pv_ref[...] = _dot_nt(hm, w_ref[pl.ds(0, pw), :])
        q_ref[...] = _dot_nt(hm, w_ref[pl.ds(pw, aw), :])
        k_ref[...] = _dot_nt(hm, w_ref[pl.ds(pw + aw, aw), :])
        v_ref[...] = _dot_nt(hm, w_ref[pl.ds(pw + 2 * aw, aw), :]).astype(BF16)
        f_ref[...] = _dot_nt(hm, w_ref[pl.ds(pw + 3 * aw, LANES), :])

    return pl.pallas_call(
        body, name="mix_in_fwd", grid=(t // tm,),
        in_specs=[_rows(tm, d), _resident((1, d)), _resident((MIX_PAD, d))],
        out_specs=[_rows(tm, d), _rows(tm, pw), _rows(tm, aw), _rows(tm, aw), _rows(tm, aw), _rows(tm, LANES)],
        out_shape=[jax.ShapeDtypeStruct((t, d), BF16), jax.ShapeDtypeStruct((t, pw), F32),
                   jax.ShapeDtypeStruct((t, aw), F32), jax.ShapeDtypeStruct((t, aw), F32),
                   jax.ShapeDtypeStruct((t, aw), BF16), jax.ShapeDtypeStruct((t, LANES), F32)],
        compiler_params=_params("arbitrary"),
    )(x, gain, w_in_t)


def _pool_fwd(pv, pool_w, pool_scale, gain, bsz, seq):
    ts = 512
    ns = seq // ts
    pw = POOL_WIDTH

    def body(pv_ref, w_ref, sc_ref, g_ref, pooled_ref, mixed_ref, y_ref, ext):
        s = pl.program_id(1)

        @pl.when(s == 0)
        def _():
            ext[pl.ds(0, POOL_HALO), :] = jnp.zeros((POOL_HALO, pw), F32)

        p = pv_ref[...]
        ext[pl.ds(POOL_HALO, ts), :] = p
        pos = s * ts + lax.broadcasted_iota(jnp.int32, (ts, 1), 0)
        parts = []
        for g, w in enumerate(POOL_WINDOWS):
            lanes = pl.ds(g * POOL_GROUP_DIM, POOL_GROUP_DIM)
            win = ext[pl.ds(POOL_HALO, ts), lanes]
            for i in range(1, w):
                win = win + ext[pl.ds(POOL_HALO - i, ts), lanes]
            cnt = jnp.minimum(pos + 1, w).astype(F32)
            pooled = (win / cnt - ext[pl.ds(POOL_HALO, ts), lanes]).astype(BF16)
            pooled_ref[:, lanes] = pooled
            parts.append(_dot(pooled, w_ref[g].astype(BF16)))
        mixed = jnp.concatenate(parts, axis=1)
        mixed_ref[...] = mixed
        pm = mixed * sc_ref[...]
        y_ref[...] = (pm * _rms_scale(pm) * g_ref[...]).astype(BF16)
        ext[pl.ds(0, POOL_HALO), :] = p[ts - POOL_HALO:, :]

    blk = pl.BlockSpec((ts, pw), lambda b, s: (b * ns + s, 0))
    t = bsz * seq
    return pl.pallas_call(
        body, name="pool_fwd", grid=(bsz, ns),
        in_specs=[blk, pl.BlockSpec((POOL_GROUPS, POOL_GROUP_DIM, POOL_GROUP_DIM), lambda b, s: (0, 0, 0)),
                  pl.BlockSpec((1, pw), lambda b, s: (0, 0)), pl.BlockSpec((1, pw), lambda b, s: (0, 0))],
        out_specs=[blk, blk, blk],
        out_shape=[jax.ShapeDtypeStruct((t, pw), BF16), jax.ShapeDtypeStruct((t, pw), F32),
                   jax.ShapeDtypeStruct((t, pw), BF16)],
        scratch_shapes=[pltpu.VMEM((POOL_HALO + ts, pw), F32)],
        compiler_params=_params("arbitrary", "arbitrary"),
    )(pv, pool_w, pool_scale, gain)


def _pool_bwd(dy, mixed, pooled, pool_w, pool_scale, gain, bsz, seq):
    ts = 512
    ns = seq // ts
    pw = POOL_WIDTH

    def body(dy_ref, mixed_ref, pooled_ref, w_ref, sc_ref, g_ref, dpv_ref, dw_ref, dsc_ref, dg_ref, ext):
        b = pl.program_id(0)
        sr = pl.program_id(1)
        s = ns - 1 - sr

        @pl.when(jnp.logical_and(b == 0, sr == 0))
        def _():
            dw_ref[...] = jnp.zeros_like(dw_ref)
            dsc_ref[...] = jnp.zeros_like(dsc_ref)
            dg_ref[...] = jnp.zeros_like(dg_ref)

        @pl.when(sr == 0)
        def _():
            ext[pl.ds(ts, POOL_HALO), :] = jnp.zeros((POOL_HALO, pw), F32)

        mixed = mixed_ref[...]
        sc = sc_ref[...]
        dpm, dgain = _rms_bwd(dy_ref[...], mixed * sc, g_ref[...])
        dg_ref[...] += dgain
        dsc_ref[...] += jnp.sum(dpm * mixed, axis=0, keepdims=True)
        dmixed = (dpm * sc).astype(BF16)
        pos = s * ts + lax.broadcasted_iota(jnp.int32, (ts, 1), 0)
        dpooled = []
        for g, w in enumerate(POOL_WINDOWS):
            lanes = pl.ds(g * POOL_GROUP_DIM, POOL_GROUP_DIM)
            dm = dmixed[:, g * POOL_GROUP_DIM:(g + 1) * POOL_GROUP_DIM]
            dw_ref[g] += _dot_tn(pooled_ref[:, lanes], dm)
            dp = _dot_nt(dm, w_ref[g].astype(BF16))
            dpooled.append(dp)
            cnt = jnp.minimum(pos + 1, w).astype(F32)
            ext[pl.ds(0, ts), lanes] = dp / cnt
        for g, w in enumerate(POOL_WINDOWS):
            lanes = pl.ds(g * POOL_GROUP_DIM, POOL_GROUP_DIM)
            win = ext[pl.ds(0, ts), lanes]
            for i in range(1, w):
                win = win + ext[pl.ds(i, ts), lanes]
            dpv_ref[:, lanes] = (win - dpooled[g]).astype(BF16)
        head = ext[pl.ds(0, POOL_HALO), :]
        ext[pl.ds(ts, POOL_HALO), :] = head

    blk = pl.BlockSpec((ts, pw), lambda b, s: (b * ns + (ns - 1 - s), 0))
    vec = pl.BlockSpec((1, pw), lambda b, s: (0, 0))
    wspec = pl.BlockSpec((POOL_GROUPS, POOL_GROUP_DIM, POOL_GROUP_DIM), lambda b, s: (0, 0, 0))
    t = bsz * seq
    return pl.pallas_call(
        body, name="pool_bwd", grid=(bsz, ns),
        in_specs=[blk, blk, blk, wspec, vec, vec],
        out_specs=[blk, wspec, vec, vec],
        out_shape=[jax.ShapeDtypeStruct((t, pw), BF16),
                   jax.ShapeDtypeStruct((POOL_GROUPS, POOL_GROUP_DIM, POOL_GROUP_DIM), F32),
                   jax.ShapeDtypeStruct((1, pw), F32), jax.ShapeDtypeStruct((1, pw), F32)],
        scratch_shapes=[pltpu.VMEM((ts + POOL_HALO, pw), F32)],
        compiler_params=_params("arbitrary", "arbitrary"),
    )(dy, mixed, pooled, pool_w, pool_scale, gain)


AUX_ONE = 64
AUX_F = 67

ATTN_PREP_ROWS = 256


def _seg_ones(width, seg):
    r = lax.broadcasted_iota(jnp.int32, (width, width), 0) // seg
    c = lax.broadcasted_iota(jnp.int32, (width, width), 1) // seg
    return (r == c).astype(BF16)


def _tri_ones(n, lower):
    r = lax.broadcasted_iota(jnp.int32, (n, n), 0)
    c = lax.broadcasted_iota(jnp.int32, (n, n), 1)
    return ((r >= c) if lower else (r <= c)).astype(BF16)


def _place_pieces(first_lane):
    r = lax.broadcasted_iota(jnp.int32, (3 * LANES, N_HEADS * LANES), 0)
    c = lax.broadcasted_iota(jnp.int32, (3 * LANES, N_HEADS * LANES), 1)
    piece, head = r // LANES, r % LANES
    return jnp.logical_and(head < N_HEADS, c == head * LANES + first_lane + piece).astype(BF16)


def _head_sums(x, seg_ones):
    hi, lo = _split2(x)
    return _dot(hi, seg_ones) + _dot(lo, seg_ones)


def _log_sigmoid(x):
    return jnp.minimum(x, 0.0) - jnp.log(1.0 + jnp.exp(-jnp.abs(x)))


def _attn_prep_fwd(q, k, f, b_forget, q_gain, k_gain, bsz, seq):
    ts = ATTN_PREP_ROWS
    ns = seq // ts
    aw = ATTN_WIDTH
    t = bsz * seq
    seg = _seg_ones(aw, HEAD_DIM)
    tri = _tri_ones(ts, True)

    def body(q_ref, k_ref, f_ref, bf_ref, gq_ref, gk_ref, seg_ref, tri_ref, pq_ref, pk_ref, qp_ref, kp_ref, carry):
        s = pl.program_id(1)

        @pl.when(s == 0)
        def _():
            carry[...] = jnp.zeros_like(carry)

        logf = _log_sigmoid(f_ref[...] + bf_ref[...])
        hi, mid, lo = _split3(logf)
        tri_v = tri_ref[...]
        fc = _dot(tri_v, hi) + _dot(tri_v, mid) + _dot(tri_v, lo) + carry[pl.ds(0, 1), :]
        carry[pl.ds(0, 1), :] = fc[ts - 1:, :]
        pcs = jnp.concatenate(_split3(fc), axis=1)
        lane = lax.broadcasted_iota(jnp.int32, (1, LANES), 1)
        ones_q = jnp.logical_and(lane >= AUX_ONE, lane < AUX_ONE + 3).astype(F32)
        ones_k = jnp.logical_and(lane >= AUX_F, lane < AUX_F + 3).astype(F32)
        seg_v = seg_ref[...]

        def build(x_ref, g_ref, scale, out_ref, ones, place_ref, f_sign):
            xv = x_ref[...]
            r = lax.rsqrt(_head_sums(xv * xv, seg_v) * (1.0 / HEAD_DIM) + EPS)
            xn = xv * r * g_ref[...] * scale
            aux = _dot(pcs, place_ref[...]) * f_sign
            for h in range(N_HEADS):
                pair = xn[:, (h // 2) * LANES:(h // 2 + 1) * LANES]
                feat = pair if h % 2 == 0 else pltpu.roll(pair, HEAD_DIM, 1)
                aux_h = aux[:, h * LANES:(h + 1) * LANES] + ones
                out_ref[:, h * LANES:(h + 1) * LANES] = jnp.where(lane < HEAD_DIM, feat, aux_h).astype(BF16)

        build(q_ref, gq_ref, 0.125, qp_ref, ones_q, pq_ref, 1.0)
        build(k_ref, gk_ref, 1.0, kp_ref, ones_k, pk_ref, -1.0)

    blk = pl.BlockSpec((ts, aw), lambda b, s: (b * ns + s, 0))
    fblk = pl.BlockSpec((ts, LANES), lambda b, s: (b * ns + s, 0))
    oblk = pl.BlockSpec((ts, N_HEADS * LANES), lambda b, s: (b * ns + s, 0))
    const = lambda shape: pl.BlockSpec(shape, lambda b, s: (0, 0))
    return pl.pallas_call(
        body, name="attn_prep_fwd", grid=(bsz, ns),
        in_specs=[blk, blk, fblk, const((1, LANES)), const((1, aw)), const((1, aw)), const((aw, aw)), const((ts, ts)),
                  const((3 * LANES, N_HEADS * LANES)), const((3 * LANES, N_HEADS * LANES))],
        out_specs=[oblk, oblk],
        out_shape=[jax.ShapeDtypeStruct((t, N_HEADS * LANES), BF16)] * 2,
        scratch_shapes=[pltpu.VMEM((8, LANES), F32)],
        compiler_params=_params("arbitrary", "arbitrary"),
    )(q, k, f, b_forget, q_gain, k_gain, seg, tri, _place_pieces(AUX_F), _place_pieces(AUX_ONE))


def _attn_prep_bwd(dqp, dkp, q, k, f, b_forget, q_gain, k_gain, bsz, seq):
    ts = ATTN_PREP_ROWS
    ns = seq // ts
    aw = ATTN_WIDTH
    t = bsz * seq
    seg = _seg_ones(aw, HEAD_DIM)
    tri = _tri_ones(ts, False)

    def body(dqp_ref, dkp_ref, q_ref, k_ref, f_ref, bf_ref, gq_ref, gk_ref, seg_ref, tri_ref,
             dq_ref, dk_ref, df_ref, dgq_ref, dgk_ref, dbf_ref, carry):
        b = pl.program_id(0)
        sr = pl.program_id(1)

        @pl.when(jnp.logical_and(b == 0, sr == 0))
        def _():
            dgq_ref[...] = jnp.zeros_like(dgq_ref)
            dgk_ref[...] = jnp.zeros_like(dgk_ref)
            dbf_ref[...] = jnp.zeros_like(dbf_ref)

        @pl.when(sr == 0)
        def _():
            carry[...] = jnp.zeros_like(carry)

        lane = lax.broadcasted_iota(jnp.int32, (1, LANES), 1)
        seg_v = seg_ref[...]

        def norm_bwd(dp_ref, x_ref, g_ref, scale, dx_ref, dgain_ref):
            parts = []
            for j in range(N_HEADS // 2):
                even = dp_ref[:, (2 * j) * LANES:(2 * j + 1) * LANES]
                odd = dp_ref[:, (2 * j + 1) * LANES:(2 * j + 2) * LANES]
                parts.append(jnp.where(lane < HEAD_DIM, even, pltpu.roll(odd, HEAD_DIM, 1)))
            dxn = jnp.concatenate(parts, axis=1) * scale
            xv = x_ref[...]
            r = lax.rsqrt(_head_sums(xv * xv, seg_v) * (1.0 / HEAD_DIM) + EPS)
            n = xv * r
            dgain_ref[...] += jnp.sum(dxn * n, axis=0, keepdims=True)
            dn = dxn * g_ref[...]
            m = _head_sums(dn * n, seg_v) * (1.0 / HEAD_DIM)
            dx_ref[...] = (r * (dn - n * m)).astype(BF16)

        norm_bwd(dqp_ref, q_ref, gq_ref, 0.125, dq_ref, dgq_ref)
        norm_bwd(dkp_ref, k_ref, gk_ref, 1.0, dk_ref, dgk_ref)

        dfc = jnp.zeros((ts, LANES), F32)
        for h in range(N_HEADS):
            cols = pl.ds(h * LANES, LANES)
            both = jnp.where(lane == AUX_F, dqp_ref[:, cols], 0.0) - jnp.where(lane == AUX_ONE, dkp_ref[:, cols], 0.0)
            dfc = jnp.where(lane == h, jnp.sum(both, axis=1, keepdims=True), dfc)
        hi, mid, lo = _split3(dfc)
        tri_v = tri_ref[...]
        dlogf = _dot(tri_v, hi) + _dot(tri_v, mid) + _dot(tri_v, lo) + carry[pl.ds(0, 1), :]
        carry[pl.ds(0, 1), :] = dlogf[0:1, :]
        df = jnp.where(lane < N_HEADS, dlogf * jax.nn.sigmoid(-(f_ref[...] + bf_ref[...])), 0.0)
        df_ref[...] = df.astype(BF16)
        dbf_ref[...] += jnp.sum(df, axis=0, keepdims=True)

    rev = lambda b, s: (b * ns + (ns - 1 - s), 0)
    blk = pl.BlockSpec((ts, aw), rev)
    fblk = pl.BlockSpec((ts, LANES), rev)
    pblk = pl.BlockSpec((ts, N_HEADS * LANES), rev)
    const = lambda shape: pl.BlockSpec(shape, lambda b, s: (0, 0))
    return pl.pallas_call(
        body, name="attn_prep_bwd", grid=(bsz, ns),
        in_specs=[pblk, pblk, blk, blk, fblk, const((1, LANES)), const((1, aw)), const((1, aw)), const((aw, aw)),
                  const((ts, ts))],
        out_specs=[blk, blk, fblk, const((1, aw)), const((1, aw)), const((1, LANES))],
        out_shape=[jax.ShapeDtypeStruct((t, aw), BF16), jax.ShapeDtypeStruct((t, aw), BF16),
                   jax.ShapeDtypeStruct((t, LANES), BF16), jax.ShapeDtypeStruct((1, aw), F32),
                   jax.ShapeDtypeStruct((1, aw), F32), jax.ShapeDtypeStruct((1, LANES), F32)],
        scratch_shapes=[pltpu.VMEM((8, LANES), F32)],
        compiler_params=_params("arbitrary", "arbitrary"),
    )(dqp, dkp, q, k, f, b_forget, q_gain, k_gain, seg, tri)


ATTN_BLOCK = 512
HEAD_PAIRS = N_HEADS // 2


def _flash_fwd(qp, kp, v, bsz, seq):
    tq = ATTN_BLOCK
    nq = seq // tq
    t = bsz * seq

    def body(q_ref, k_ref, v_ref, o_ref, lse_ref, m_sc, l_sc, acc_sc):
        i = pl.program_id(2)
        m_sc[...] = jnp.full(m_sc.shape, -jnp.inf, F32)
        l_sc[...] = jnp.zeros_like(l_sc)
        acc_sc[...] = jnp.zeros_like(acc_sc)
        lane = lax.broadcasted_iota(jnp.int32, (1, LANES), 1)
        low = lane < HEAD_DIM

        def key_block(j, masked):
            rows = pl.ds(pl.multiple_of(j * tq, tq), tq)
            vv = v_ref[rows, :]
            for h in range(2):
                mine = low if h == 0 else jnp.logical_not(low)
                s = _dot_nt(q_ref[:, h * LANES:(h + 1) * LANES], k_ref[rows, pl.ds(h * LANES, LANES)])
                if masked:
                    row = lax.broadcasted_iota(jnp.int32, (tq, tq), 0)
                    col = lax.broadcasted_iota(jnp.int32, (tq, tq), 1)
                    s = jnp.where(row >= col, s, -jnp.inf)
                m_prev = m_sc[h]
                m_new = jnp.maximum(m_prev, jnp.max(s, axis=1, keepdims=True))
                p = jnp.exp(s - jnp.tile(m_new, (1, tq // LANES)))
                alpha = jnp.exp(m_prev - m_new)
                l_sc[h] = alpha * l_sc[h] + jnp.sum(p, axis=1, keepdims=True)
                m_sc[h] = m_new
                pv = _dot(p.astype(BF16), jnp.where(mine, vv, jnp.zeros_like(vv)))
                acc_sc[...] = acc_sc[...] * jnp.where(mine, alpha, 1.0) + pv

        def below_diagonal(j, carry):
            key_block(j, False)
            return carry

        lax.fori_loop(0, i, below_diagonal, 0)
        key_block(i, True)
        l = jnp.where(low, l_sc[0], l_sc[1])
        m = jnp.where(low, m_sc[0], m_sc[1])
        o_ref[...] = acc_sc[...] / l
        lse_ref[...] = m + jnp.log(l)

    qspec = pl.BlockSpec((tq, 2 * LANES), lambda b, hp, i: (b * nq + i, hp))
    kspec = pl.BlockSpec((seq, 2 * LANES), lambda b, hp, i: (b, hp))
    vspec = pl.BlockSpec((seq, LANES), lambda b, hp, i: (b, hp))
    ospec = pl.BlockSpec((tq, LANES), lambda b, hp, i: (b * nq + i, hp))
    return pl.pallas_call(
        body, name="flash_fwd", grid=(bsz, HEAD_PAIRS, nq),
        in_specs=[qspec, kspec, vspec], out_specs=[ospec, ospec],
        out_shape=[jax.ShapeDtypeStruct((t, ATTN_WIDTH), F32), jax.ShapeDtypeStruct((t, ATTN_WIDTH), F32)],
        scratch_shapes=[pltpu.VMEM((2, tq, LANES), F32), pltpu.VMEM((2, tq, LANES), F32), pltpu.VMEM((tq, LANES), F32)],
        compiler_params=_params("arbitrary", "arbitrary", "arbitrary"),
    )(qp, kp, v)


def _flash_bwd(qp, kp, v, o, do, lse, bsz, seq):
    tq = ATTN_BLOCK
    nq = seq // tq
    t = bsz * seq

    def body(q_ref, k_ref, v_ref, o_ref, do_ref, lse_ref, dq_ref, dk_ref, dv_ref, dk_acc, dv_acc):
        j = pl.program_id(2)

        @pl.when(j == 0)
        def _():
            dq_ref[...] = jnp.zeros_like(dq_ref)

        dk_acc[...] = jnp.zeros_like(dk_acc)
        dv_acc[...] = jnp.zeros_like(dv_acc)
        lane = lax.broadcasted_iota(jnp.int32, (1, LANES), 1)
        low = lane < HEAD_DIM

        def query_block(i, masked):
            rows = pl.ds(pl.multiple_of(i * tq, tq), tq)
            dov = do_ref[rows, :]
            dd = dov * o_ref[rows, :]
            dob = dov.astype(BF16)
            vv = v_ref[...]
            lse_v = lse_ref[rows, :]
            for h in range(2):
                mine = low if h == 0 else jnp.logical_not(low)
                cols = pl.ds(h * LANES, LANES)
                qh = q_ref[rows, cols]
                kh = k_ref[:, cols]
                s = _dot_nt(qh, kh)
                lse_h = jnp.where(mine, lse_v, pltpu.roll(lse_v, HEAD_DIM, 1))
                p = jnp.exp(s - jnp.tile(lse_h, (1, tq // LANES)))
                if masked:
                    row = lax.broadcasted_iota(jnp.int32, (tq, tq), 0)
                    col = lax.broadcasted_iota(jnp.int32, (tq, tq), 1)
                    p = jnp.where(row >= col, p, 0.0)
                delta = jnp.sum(jnp.where(mine, dd, 0.0), axis=1, keepdims=True)
                dp = _dot_nt(dob, jnp.where(mine, vv, jnp.zeros_like(vv)))
                ds = (p * (dp - delta)).astype(BF16)
                dv_acc[...] += jnp.where(mine, _dot_tn(p.astype(BF16), dob), 0.0)
                dk_acc[:, cols] += _dot_tn(ds, qh)
                dq_ref[rows, cols] += _dot(ds, kh)

        def above_diagonal(i, carry):
            query_block(i, False)
            return carry

        query_block(j, True)
        lax.fori_loop(j + 1, nq, above_diagonal, 0)
        dk_ref[...] = dk_acc[...]
        dv_ref[...] = dv_acc[...].astype(BF16)

    qspec = pl.BlockSpec((seq, 2 * LANES), lambda b, hp, j: (b, hp))
    kspec = pl.BlockSpec((tq, 2 * LANES), lambda b, hp, j: (b * nq + j, hp))
    vspec = pl.BlockSpec((tq, LANES), lambda b, hp, j: (b * nq + j, hp))
    ospec = pl.BlockSpec((seq, LANES), lambda b, hp, j: (b, hp))
    return pl.pallas_call(
        body, name="flash_bwd", grid=(bsz, HEAD_PAIRS, nq),
        in_specs=[qspec, kspec, vspec, ospec, ospec, ospec], out_specs=[qspec, kspec, vspec],
        out_shape=[jax.ShapeDtypeStruct((t, N_HEADS * LANES), F32), jax.ShapeDtypeStruct((t, N_HEADS * LANES), F32),
                   jax.ShapeDtypeStruct((t, ATTN_WIDTH), BF16)],
        scratch_shapes=[pltpu.VMEM((tq, 2 * LANES), F32), pltpu.VMEM((tq, LANES), F32)],
        compiler_params=_params("arbitrary", "arbitrary", "arbitrary"),
    )(qp, kp, v, o, do, lse)


def _mix_out_fwd(o, y_pool, x, gain, w_out):
    t, d = x.shape
    tm = 512
    pw, aw = POOL_WIDTH, ATTN_WIDTH

    def body(o_ref, yp_ref, x_ref, g_ref, w_ref, ycat_ref, y_ref):
        ov = o_ref[...]
        ya = (ov * _rms_scale(ov) * g_ref[...]).astype(BF16)
        ycat = jnp.concatenate([yp_ref[...], ya], axis=1)
        ycat_ref[...] = ycat
        y_ref[...] = x_ref[...] + _dot(ycat, w_ref[...])

    return pl.pallas_call(
        body, name="mix_out_fwd", grid=(t // tm,),
        in_specs=[_rows(tm, aw), _rows(tm, pw), _rows(tm, d), _resident((1, aw)), _resident((pw + aw, d))],
        out_specs=[_rows(tm, pw + aw), _rows(tm, d)],
        out_shape=[jax.ShapeDtypeStruct((t, pw + aw), BF16), jax.ShapeDtypeStruct((t, d), F32)],
        compiler_params=_params("arbitrary"),
    )(o, y_pool, x, gain, w_out)


def _mix_out_bwd(dx, o, gain, w_out):
    t, d = dx.shape
    tm = 512
    pw, aw = POOL_WIDTH, ATTN_WIDTH

    def body(dx_ref, o_ref, g_ref, w_ref, dxb_ref, dyp_ref, do_ref, dg_ref):
        dxb = dx_ref[...].astype(BF16)
        dxb_ref[...] = dxb
        dyp_ref[...] = _dot_nt(dxb, w_ref[pl.ds(0, pw), :])
        dya = _dot_nt(dxb, w_ref[pl.ds(pw, aw), :])
        do, dgain = _rms_bwd(dya, o_ref[...], g_ref[...])
        do_ref[...] = do

        @pl.when(pl.program_id(0) == 0)
        def _():
            dg_ref[...] = jnp.zeros_like(dg_ref)

        dg_ref[...] += dgain

    return pl.pallas_call(
        body, name="mix_out_bwd", grid=(t // tm,),
        in_specs=[_rows(tm, d), _rows(tm, aw), _resident((1, aw)), _resident((pw + aw, d))],
        out_specs=[_rows(tm, d), _rows(tm, pw), _rows(tm, aw), pl.BlockSpec((1, aw), lambda i: (0, 0))],
        out_shape=[jax.ShapeDtypeStruct((t, d), BF16), jax.ShapeDtypeStruct((t, pw), F32),
                   jax.ShapeDtypeStruct((t, aw), F32), jax.ShapeDtypeStruct((1, aw), F32)],
        compiler_params=_params("arbitrary"),
    )(dx, o, gain, w_out)


def _mix_in_bwd(dpv, dq, dk, dv, df, x, dx_res, gain, w_in_t):
    t, d = x.shape
    tm = 512
    pw, aw = POOL_WIDTH, ATTN_WIDTH

    def body(dpv_ref, dq_ref, dk_ref, dv_ref, df_ref, x_ref, dxr_ref, g_ref, w_ref, dh_ref, dx_ref, dxh_ref, dg_ref):
        dh = jnp.concatenate([dpv_ref[...], dq_ref[...], dk_ref[...], dv_ref[...], df_ref[...]], axis=1)
        dh_ref[...] = dh
        dhm = _dot(dh, w_ref[...])
        dx, dgain = _rms_bwd(dhm, x_ref[...], g_ref[...])
        dx = dxr_ref[...] + dx
        dx_ref[...] = dx
        dxh_ref[...] = (0.5 * dx).astype(BF16)

        @pl.when(pl.program_id(0) == 0)
        def _():
            dg_ref[...] = jnp.zeros_like(dg_ref)

        dg_ref[...] += dgain

    return pl.pallas_call(
        body, name="mix_in_bwd", grid=(t // tm,),
        in_specs=[_rows(tm, pw), _rows(tm, aw), _rows(tm, aw), _rows(tm, aw), _rows(tm, LANES), _rows(tm, d),
                  _rows(tm, d), _resident((1, d)), _resident((MIX_PAD, d))],
        out_specs=[_rows(tm, MIX_PAD), _rows(tm, d), _rows(tm, d), pl.BlockSpec((1, d), lambda i: (0, 0))],
        out_shape=[jax.ShapeDtypeStruct((t, MIX_PAD), BF16), jax.ShapeDtypeStruct((t, d), F32),
                   jax.ShapeDtypeStruct((t, d), BF16), jax.ShapeDtypeStruct((1, d), F32)],
        compiler_params=_params("arbitrary"),
    )(dpv, dq, dk, dv, df, x, dx_res, gain, w_in_t)


MESH_IDS = pl.DeviceIdType.MESH


def _me():
    return lax.axis_index("x"), lax.axis_index("y"), lax.axis_index("c")


def _peer(x, y, c, p):
    px = 1 - x if p & 4 else x
    py = 1 - y if p & 2 else y
    pc = 1 - c if p & 1 else c
    return (px, py, pc), 4 * px + 2 * py + pc


HBM_SPEC = pl.BlockSpec(memory_space=pltpu.HBM)
SEM_SPEC = pl.BlockSpec(memory_space=pltpu.SEMAPHORE)
SPLIT_COPY = pltpu.CompilerParams(has_side_effects=pltpu.SideEffectType.DATAFLOW_SIDE_EFFECTING)
PEERS = N_DEV - 1


def _hbm(a):
    return pltpu.with_memory_space_constraint(a, pltpu.HBM)


def _row_block(ref, dev, rows):
    return ref.at[pl.ds(pl.multiple_of(dev * rows, BF16_ROWS), rows)]


def _copy_ends(gather, src, land, me, peer_id):
    if gather:
        rows = src.shape[0]
        return src, _row_block(land, me, rows), _row_block(land, peer_id, rows), src, _row_block(land, me, rows)
    rows = src.shape[0] // N_DEV
    return (_row_block(src, peer_id, rows), land.at[me], land.at[peer_id], _row_block(src, me, rows), land.at[me])


def _land_shape(gather, s):
    return (N_DEV * s.shape[0], s.shape[1]) if gather else (N_DEV, s.shape[0] // N_DEV, s.shape[1])


def _copies_start(groups, gather, name, after=None):
    flat = [s for g in groups for s in g]
    n, ng = len(flat), len(groups)
    lands = [lax.empty(_land_shape(gather, s), s.dtype) for s in flat]
    n_in = 2 * n + (after is not None)

    def body(*refs):
        ins, lnd = refs[:n], refs[n:2 * n]
        sems = refs[n_in:n_in + 2 * ng]
        token = refs[-1]
        x, y, c = _me()
        me = 4 * x + 2 * y + c
        w = 0
        for gi, g in enumerate(groups):
            for k in range(len(g)):
                for p in range(1, N_DEV):
                    peer, peer_id = _peer(x, y, c, p)
                    src, dst, _, _, _ = _copy_ends(gather, ins[w], lnd[w], me, peer_id)
                    pltpu.make_async_remote_copy(src, dst, sems[2 * gi].at[k * PEERS + p - 1],
                                                 sems[2 * gi + 1].at[k * PEERS + p - 1], device_id=peer,
                                                 device_id_type=MESH_IDS).start()
                w += 1
        token[...] = jnp.zeros_like(token)

    sem_shapes = []
    for g in groups:
        sem_shapes += [pltpu.SemaphoreType.DMA((len(g) * PEERS,))] * 2
    out = pl.pallas_call(
        body, name=name,
        out_shape=(*sem_shapes, *[pltpu.HBM(s.shape, s.dtype) for s in flat],
                   *[pltpu.HBM(l.shape, l.dtype) for l in lands], jax.ShapeDtypeStruct((8, LANES), F32)),
        in_specs=[HBM_SPEC] * (2 * n) + [pl.BlockSpec(memory_space=pl.ANY)] * (after is not None),
        out_specs=(*[SEM_SPEC] * (2 * ng), *[HBM_SPEC] * (2 * n), pl.BlockSpec(memory_space=pltpu.VMEM)),
        input_output_aliases={i: 2 * ng + i for i in range(2 * n)},
        compiler_params=SPLIT_COPY,
    )(*[_hbm(s) for s in flat], *[_hbm(l) for l in lands], *([after] if after is not None else []))
    sems, thru, token = out[:2 * ng], out[2 * ng:2 * ng + 2 * n], out[-1]
    res, w = [], 0
    for gi, g in enumerate(groups):
        res.append((sems[2 * gi], sems[2 * gi + 1], list(thru[w:w + len(g)]), list(thru[n + w:n + w + len(g)])))
        w += len(g)
    return res, token


def _copies_wait(started, gather, after, name):
    send, recv, srcs, lands = started
    n = len(srcs)
    after = list(after) if isinstance(after, (list, tuple)) else [after]

    own_shapes = [s.shape if gather else (s.shape[0] // N_DEV, s.shape[1]) for s in srcs]

    def body(*refs):
        ins, lnd = refs[:n], refs[n:2 * n]
        send_sems, recv_sems = refs[2 * n], refs[2 * n + 1]
        bounce, in_sems, out_sems = refs[-n - 2:-2], refs[-2], refs[-1]
        x, y, c = _me()
        me = 4 * x + 2 * y + c
        ends = [_copy_ends(gather, ins[w], lnd[w], me, me)[3:] for w in range(n)]
        loads = [pltpu.make_async_copy(ends[w][0], bounce[w], in_sems.at[w]) for w in range(n)]
        stores = [pltpu.make_async_copy(bounce[w], ends[w][1], out_sems.at[w]) for w in range(n)]
        for cp in loads:
            cp.start()
        for w in range(n):
            loads[w].wait()
            stores[w].start()
        for w in range(n):
            for p in range(1, N_DEV):
                peer, peer_id = _peer(x, y, c, p)
                src, _, arrival, _, _ = _copy_ends(gather, ins[w], lnd[w], me, peer_id)
                cp = pltpu.make_async_remote_copy(src, arrival, send_sems.at[w * PEERS + p - 1],
                                                  recv_sems.at[w * PEERS + p - 1], device_id=peer,
                                                  device_id_type=MESH_IDS)
                cp.wait_send()
                cp.wait_recv()
        for cp in stores:
            cp.wait()

    out = pl.pallas_call(
        body, name=name,
        out_shape=(*[pltpu.HBM(s.shape, s.dtype) for s in srcs], *[pltpu.HBM(l.shape, l.dtype) for l in lands]),
        in_specs=[HBM_SPEC] * (2 * n) + [SEM_SPEC, SEM_SPEC] + [pl.BlockSpec(memory_space=pl.ANY)] * len(after),
        out_specs=[HBM_SPEC] * (2 * n),
        input_output_aliases={i: i for i in range(2 * n)},
        scratch_shapes=[*[pltpu.VMEM(shape, s.dtype) for shape, s in zip(own_shapes, srcs)],
                        pltpu.SemaphoreType.DMA((n,)), pltpu.SemaphoreType.DMA((n,))],
        compiler_params=SPLIT_COPY,
    )(*srcs, *lands, send, recv, *after)
    return list(out[n:])


def _adamw_update(w, g, m, v):
    nm = ADAM_B1 * m + (1.0 - ADAM_B1) * g
    nv = ADAM_B2 * v + (1.0 - ADAM_B2) * (g * g)
    m_hat = nm / (1.0 - ADAM_B1 ** ADAM_STEP)
    v_hat = nv / (1.0 - ADAM_B2 ** ADAM_STEP)
    return -ADAM_LR * (m_hat / (jnp.sqrt(v_hat) + ADAM_EPS) + ADAM_WD * w), nm, nv


SUM_ADAMW_COLS = 256


def _sum_adamw(parts, w, m, v, name):
    _, rows, d = parts.shape
    n = w.shape[0]
    tc = SUM_ADAMW_COLS

    def body(p_ref, w_ref, m_ref, v_ref, g_ref, d_ref, nm_ref, nv_ref):
        g = p_ref[0].astype(F32)
        for dev in range(1, N_DEV):
            g = g + p_ref[dev].astype(F32)
        g = g[:n]
        g_ref[...] = g
        d_ref[...], nm_ref[...], nv_ref[...] = _adamw_update(w_ref[...], g, m_ref[...], v_ref[...])

    spec = pl.BlockSpec((n, tc), lambda j: (0, j))
    shape = jax.ShapeDtypeStruct((n, d), F32)
    return pl.pallas_call(
        body, name=name, grid=(d // tc,),
        in_specs=[pl.BlockSpec((N_DEV, rows, tc), lambda j: (0, 0, j)), spec, spec, spec],
        out_specs=[spec] * 4, out_shape=[shape] * 4,
        compiler_params=_params("arbitrary"),
    )(parts, w, m, v)


def _pad_rows(a, rows):
    return jnp.pad(a, ((0, rows - a.shape[0]), (0, 0)))


def _row1(vec, width=D_MODEL):
    return jnp.pad(vec.reshape(1, -1), ((0, 0), (0, width - vec.shape[-1])))


COLUMN_SHARDED = ("ffn1_w_gate", "ffn1_w_up", "w_in", "ffn2_w_gate", "ffn2_w_up")
VEC_NAMES = ("ffn1_norm", "mix_norm", "ffn2_norm", "b_forget", "pool_scale", "q_norm", "k_norm", "out_norm_pool",
             "out_norm_attn")
VEC_ROWS = 16
LOSS_ROW = len(VEC_NAMES)


def _pack_vector_grads(parts, loss_part):
    def body(*refs):
        loss_ref, out_ref = refs[-2], refs[-1]
        out_ref[...] = jnp.zeros_like(out_ref)
        lane = lax.broadcasted_iota(jnp.int32, (1, LANES), 1)
        for i, (name, ref) in enumerate(zip(VEC_NAMES, refs[:-2])):
            val = ref[...]
            if name in ("q_norm", "k_norm"):
                val = val[:, 0:LANES] + val[:, LANES:2 * LANES] + val[:, 2 * LANES:3 * LANES] + val[:, 3 * LANES:]
                val = jnp.where(lane < HEAD_DIM, val + pltpu.roll(val, HEAD_DIM, 1), 0.0)
            out_ref[pl.ds(i, 1), pl.ds(0, val.shape[1])] = val
        out_ref[pl.ds(LOSS_ROW, 1), pl.ds(0, 1)] = loss_ref[...]

    vmem = pl.BlockSpec(memory_space=pltpu.VMEM)
    return pl.pallas_call(
        body, name="pack_vector_grads", in_specs=[vmem] * (len(parts) + 1), out_specs=vmem,
        out_shape=jax.ShapeDtypeStruct((VEC_ROWS, D_MODEL), F32),
    )(*parts, loss_part)


def _small_adamw(vec_all, pool_all, vec_params, pool_params):
    nv = len(vec_params)
    pool_rows = pool_params[0].shape[0]

    def body(*refs):
        vec_ref, pool_ref = refs[0], refs[1]
        ins = refs[2:2 + 3 * nv + 3]
        outs = refs[2 + 3 * nv + 3:-1]
        rows = refs[-1]
        total = vec_ref[pl.ds(0, VEC_ROWS), :]
        for dev in range(1, N_DEV):
            total = total + vec_ref[pl.ds(dev * VEC_ROWS, VEC_ROWS), :]
        rows[...] = total
        outs[4 * nv + 4][...] = rows[pl.ds(LOSS_ROW, 1), pl.ds(0, 1)]
        for i in range(nv):
            w_ref, m_ref, v_ref = ins[3 * i:3 * i + 3]
            g = rows[pl.ds(i, 1), pl.ds(0, w_ref.shape[1])]
            outs[4 * i][...] = g
            outs[4 * i + 1][...], outs[4 * i + 2][...], outs[4 * i + 3][...] = _adamw_update(
                w_ref[...], g, m_ref[...], v_ref[...])
        g = pool_ref[pl.ds(0, pool_rows), :]
        for dev in range(1, N_DEV):
            g = g + pool_ref[pl.ds(dev * pool_rows, pool_rows), :]
        w_ref, m_ref, v_ref = ins[3 * nv:]
        outs[4 * nv][...] = g
        outs[4 * nv + 1][...], outs[4 * nv + 2][...], outs[4 * nv + 3][...] = _adamw_update(
            w_ref[...], g, m_ref[...], v_ref[...])

    vmem = pl.BlockSpec(memory_space=pltpu.VMEM)
    flat = [a for trio in vec_params for a in trio] + list(pool_params)
    out_shape = []
    for trio in list(vec_params) + [pool_params]:
        out_shape += [jax.ShapeDtypeStruct(trio[0].shape, F32)] * 4
    out_shape.append(jax.ShapeDtypeStruct((1, 1), F32))
    return pl.pallas_call(
        body, name="adamw_small", in_specs=[vmem] * (2 + len(flat)), out_specs=[vmem] * len(out_shape),
        out_shape=out_shape, scratch_shapes=[pltpu.VMEM((VEC_ROWS, D_MODEL), F32)],
    )(vec_all, pool_all, *flat)


def kernel(x, ffn1_norm, ffn1_w_gate, ffn1_w_up, ffn1_w_down, mix_norm, w_in, b_forget, pool_w, pool_scale, q_norm, k_norm, out_norm_pool, out_norm_attn, w_out, ffn2_norm, ffn2_w_gate, ffn2_w_up, ffn2_w_down, loss_target, m_ffn1_norm, m_ffn1_w_gate, m_ffn1_w_up, m_ffn1_w_down, m_mix_norm, m_w_in, m_b_forget, m_pool_w, m_pool_scale, m_q_norm, m_k_norm, m_out_norm_pool, m_out_norm_attn, m_w_out, m_ffn2_norm, m_ffn2_w_gate, m_ffn2_w_up, m_ffn2_w_down, v_ffn1_norm, v_ffn1_w_gate, v_ffn1_w_up, v_ffn1_w_down, v_mix_norm, v_w_in, v_b_forget, v_pool_w, v_pool_scale, v_q_norm, v_k_norm, v_out_norm_pool, v_out_norm_attn, v_w_out, v_ffn2_norm, v_ffn2_w_gate, v_ffn2_w_up, v_ffn2_w_down):
    bsz, seq, d = x.shape
    t = bsz * seq
    x0 = x.reshape(t, d)
    target = loss_target.reshape(t, d)
    in_rows = -(-w_in.shape[1] // BF16_ROWS) * BF16_ROWS

    slabs = [s.astype(BF16) for s in (ffn1_w_gate.T, ffn1_w_up.T, ffn1_w_down, _pad_rows(w_in.T, in_rows), w_out,
                                       ffn2_w_gate.T, ffn2_w_up.T, ffn2_w_down)]
    gathers, started = _copies_start([slabs[0:2], slabs[2:3], slabs[3:4], slabs[4:5], slabs[5:8]], True, "gather_start")

    g1, gm, g2 = ffn1_norm.reshape(1, d), mix_norm.reshape(1, d), ffn2_norm.reshape(1, d)
    bf_row = _row1(b_forget, LANES)
    gq = jnp.tile(q_norm, N_HEADS).reshape(1, ATTN_WIDTH)
    gk = jnp.tile(k_norm, N_HEADS).reshape(1, ATTN_WIDTH)
    scale_row = pool_scale.reshape(1, POOL_WIDTH)
    gp, ga = out_norm_pool.reshape(1, POOL_WIDTH), out_norm_attn.reshape(1, ATTN_WIDTH)

    wg1, wu1 = _copies_wait(gathers[0], True, started, "gather_wait_ffn1_up")
    h1, a1, b1, s1 = _ffn_up(x0, g1, wg1, wu1, "ffn1_up")
    (wd1,) = _copies_wait(gathers[1], True, s1, "gather_wait_ffn1_down")
    (x1,) = _ffn_down(s1, wd1, x0, None, "ffn1_down")
    (win_g,) = _copies_wait(gathers[2], True, x1, "gather_wait_w_in")
    win_cols = win_g.reshape(N_DEV, in_rows, d)[:, :w_in.shape[1]].reshape(MIX_COLS, d)
    win_t = _pad_rows(win_cols, MIX_PAD)
    hm, pv, q, k, v, f = _mix_in_fwd(x1, gm, win_t)
    pooled, mixed, y_pool = _pool_fwd(pv, pool_w, scale_row, gp, bsz, seq)
    qp, kp = _attn_prep_fwd(q, k, f, bf_row, gq, gk, bsz, seq)
    o, lse = _flash_fwd(qp, kp, v, bsz, seq)
    (wout,) = _copies_wait(gathers[3], True, o, "gather_wait_w_out")
    ycat, x2 = _mix_out_fwd(o, y_pool, x1, ga, wout)
    wg2, wu2, wd2 = _copies_wait(gathers[4], True, x2, "gather_wait_ffn2")
    h2, a2, b2, s2 = _ffn_up(x2, g2, wg2, wu2, "ffn2_up")
    dx3, dyh2, loss_part = _ffn_down(s2, wd2, x2, target, "ffn2_down")

    da2, db2 = _ffn_bwd_act(dyh2, a2, b2, wd2, "ffn2_bwd_act")
    dwg2, dwu2 = _wgrad([da2, db2], h2, "ffn2_up_wgrad")
    (dwd2,) = _wgrad([s2], dyh2, "ffn2_down_wgrad")
    (sent_ffn2,), tok = _copies_start([[dwg2, dwu2, dwd2]], False, "exchange_start_ffn2")
    dx2, dg2 = _ffn_bwd_dx(da2, db2, dx3, x2, g2 + tok[0, 0], wg2, wu2, "ffn2_bwd_dx")
    dx2b, dy_pool, do, dga = _mix_out_bwd(dx2, o, ga, wout)
    (dwout,) = _wgrad([ycat], dx2b, "w_out_wgrad")
    (sent_out,), tok = _copies_start([[dwout]], False, "exchange_start_w_out")
    dqp, dkp, dv = _flash_bwd(qp, kp, v, o, do, lse, bsz, seq)
    dq, dk, df, dgq, dgk, dbf = _attn_prep_bwd(dqp, dkp, q, k, f, bf_row + tok[0, 0], gq, gk, bsz, seq)
    dpv, dpool_w, dscale, dgp = _pool_bwd(dy_pool, mixed, pooled, pool_w, scale_row, gp, bsz, seq)
    dhcat, dx1, dyh1, dgm = _mix_in_bwd(dpv, dq, dk, dv, df, x1, dx2, gm, win_t)
    (dwin,) = _wgrad([dhcat], hm, "w_in_wgrad")
    dwin_blocks = jnp.pad(dwin[:MIX_COLS].reshape(N_DEV, w_in.shape[1], d), ((0, 0), (0, in_rows - w_in.shape[1]), (0, 0)))
    (sent_in,), tok = _copies_start([[dwin_blocks.reshape(N_DEV * in_rows, d)]], False, "exchange_start_w_in")
    (dwd1,) = _wgrad([s1], dyh1, "ffn1_down_wgrad")
    (sent_down1,), tok = _copies_start([[dwd1]], False, "exchange_start_ffn1_down", after=tok)
    da1, db1 = _ffn_bwd_act(dyh1, a1, b1, wd1, "ffn1_bwd_act")
    dwg1, dwu1 = _wgrad([da1, db1], h1, "ffn1_up_wgrad")
    (sent_up1,), tok = _copies_start([[dwg1, dwu1]], False, "exchange_start_ffn1_up", after=tok)
    dx0, dg1 = _ffn_bwd_dx(da1, db1, dx1, x0, g1 + tok[0, 0], wg1, wu1, "ffn1_bwd_dx")

    pool_rows = POOL_GROUPS * POOL_GROUP_DIM
    packed = _pack_vector_grads([dg1, dgm, dg2, dbf, dscale, dgq, dgk, dgp, dga], loss_part)
    (sent_small,), tok = _copies_start([[packed, dpool_w.reshape(pool_rows, POOL_GROUP_DIM)]], True, "small_grads_start")

    weights = dict(ffn1_norm=ffn1_norm, ffn1_w_gate=ffn1_w_gate, ffn1_w_up=ffn1_w_up, ffn1_w_down=ffn1_w_down,
                   mix_norm=mix_norm, w_in=w_in, b_forget=b_forget, pool_w=pool_w, pool_scale=pool_scale,
                   q_norm=q_norm, k_norm=k_norm, out_norm_pool=out_norm_pool, out_norm_attn=out_norm_attn,
                   w_out=w_out, ffn2_norm=ffn2_norm, ffn2_w_gate=ffn2_w_gate, ffn2_w_up=ffn2_w_up,
                   ffn2_w_down=ffn2_w_down)
    m_in = dict(ffn1_norm=m_ffn1_norm, ffn1_w_gate=m_ffn1_w_gate, ffn1_w_up=m_ffn1_w_up, ffn1_w_down=m_ffn1_w_down,
                mix_norm=m_mix_norm, w_in=m_w_in, b_forget=m_b_forget, pool_w=m_pool_w, pool_scale=m_pool_scale,
                q_norm=m_q_norm, k_norm=m_k_norm, out_norm_pool=m_out_norm_pool, out_norm_attn=m_out_norm_attn,
                w_out=m_w_out, ffn2_norm=m_ffn2_norm, ffn2_w_gate=m_ffn2_w_gate, ffn2_w_up=m_ffn2_w_up,
                ffn2_w_down=m_ffn2_w_down)
    v_in = dict(ffn1_norm=v_ffn1_norm, ffn1_w_gate=v_ffn1_w_gate, ffn1_w_up=v_ffn1_w_up, ffn1_w_down=v_ffn1_w_down,
                mix_norm=v_mix_norm, w_in=v_w_in, b_forget=v_b_forget, pool_w=v_pool_w, pool_scale=v_pool_scale,
                q_norm=v_q_norm, k_norm=v_k_norm, out_norm_pool=v_out_norm_pool, out_norm_attn=v_out_norm_attn,
                w_out=v_w_out, ffn2_norm=v_ffn2_norm, ffn2_w_gate=v_ffn2_w_gate, ffn2_w_up=v_ffn2_w_up,
                ffn2_w_down=v_ffn2_w_down)
    grads, delta, new_m, new_v = {}, {}, {}, {}
    after = [tok]
    plan = ((sent_ffn2, "ffn2", ("ffn2_w_gate", "ffn2_w_up", "ffn2_w_down")), (sent_out, "w_out", ("w_out",)),
            (sent_in, "w_in", ("w_in",)), (sent_down1, "ffn1_down", ("ffn1_w_down",)),
            (sent_up1, "ffn1_up", ("ffn1_w_gate", "ffn1_w_up")))
    for sent, tag, names in plan:
        parts = _copies_wait(sent, False, after, f"exchange_wait_{tag}")
        for n, part in zip(names, parts):
            turn = (lambda a: a.T) if n in COLUMN_SHARDED else (lambda a: a)
            done = _sum_adamw(part, turn(weights[n]), turn(m_in[n]), turn(v_in[n]), f"adamw_{n}")
            grads[n], delta[n], new_m[n], new_v[n] = (turn(a) for a in done)
        after = [new_v[n] for n in names]
    vec_all, pool_all = _copies_wait(sent_small, True, after, "small_grads_wait")
    as_row = lambda a: a.reshape(1, -1)
    as_pool = lambda a: a.reshape(pool_rows, POOL_GROUP_DIM)
    small = _small_adamw(vec_all, pool_all, [tuple(as_row(z[n]) for z in (weights, m_in, v_in)) for n in VEC_NAMES],
                         tuple(as_pool(z["pool_w"]) for z in (weights, m_in, v_in)))
    for i, n in enumerate(VEC_NAMES + ("pool_w",)):
        grads[n], delta[n], new_m[n], new_v[n] = (a.reshape(weights[n].shape) for a in small[4 * i:4 * i + 4])
    loss = small[-1].reshape(())

    order = ("ffn1_norm", "ffn1_w_gate", "ffn1_w_up", "ffn1_w_down", "mix_norm", "w_in", "b_forget", "pool_w",
             "pool_scale", "q_norm", "k_norm", "out_norm_pool", "out_norm_attn", "w_out", "ffn2_norm", "ffn2_w_gate",
             "ffn2_w_up", "ffn2_w_down")
    return (loss, dx0.reshape(bsz, seq, d), *[grads[n] for n in order], *[delta[n] for n in order],
            *[new_m[n] for n in order], *[new_v[n] for n in order])
```

```python
import functools

import jax
import jax.numpy as jnp
from jax import lax
from jax.experimental import pallas as pl
from jax.experimental.pallas import tpu as pltpu

F32 = jnp.float32
BF16 = jnp.bfloat16

EPS = 1e-6
D_MODEL = 1024
D_FF = 2816
N_HEADS = 8
HEAD_DIM = 64
POOL_WIDTH = 512
ATTN_WIDTH = 512
POOL_GROUPS = 4
POOL_GROUP_DIM = 128
POOL_WINDOWS = (2, 4, 8, 16)
POOL_HALO = 16
MIX_COLS = POOL_WIDTH + 3 * ATTN_WIDTH + N_HEADS
MIX_PAD = POOL_WIDTH + 3 * ATTN_WIDTH + 128
N_DEV = 8
BF16_ROWS = 16
LANES = 128
VMEM_LIMIT = 56 * 1024 * 1024

ADAM_LR = 0.001
ADAM_B1 = 0.9
ADAM_B2 = 0.999
ADAM_EPS = 1e-08
ADAM_WD = 0.01
ADAM_STEP = 10


def _params(*sem):
    return pltpu.CompilerParams(dimension_semantics=sem, vmem_limit_bytes=VMEM_LIMIT)


def _dot(a, b):
    return jnp.dot(a, b, preferred_element_type=F32)


def _dot_nt(a, b):
    return lax.dot_general(a, b, (((1,), (1,)), ((), ())), preferred_element_type=F32)


def _dot_tn(a, b):
    return lax.dot_general(a, b, (((0,), (0,)), ((), ())), preferred_element_type=F32)


def _resident(shape):
    return pl.BlockSpec(shape, lambda *_: (0,) * len(shape), pipeline_mode=pl.Buffered(1))


def _rows(tm, width):
    return pl.BlockSpec((tm, width), lambda i: (i, 0))


def _rms_scale(x):
    return lax.rsqrt(jnp.mean(x * x, axis=-1, keepdims=True) + EPS)


def _rms_bwd(dh, x, gain):
    r = _rms_scale(x)
    n = x * r
    dgain = jnp.sum(dh * n, axis=0, keepdims=True)
    dn = dh * gain
    dx = r * (dn - n * jnp.mean(dn * n, axis=-1, keepdims=True))
    return dx, dgain


def _split3(x):
    hi = x.astype(BF16)
    r1 = x - hi.astype(F32)
    mid = r1.astype(BF16)
    lo = (r1 - mid.astype(F32)).astype(BF16)
    return hi, mid, lo


def _split2(x):
    hi = x.astype(BF16)
    return hi, (x - hi.astype(F32)).astype(BF16)


FF_CHUNK = 256


def _ffn_up(x, gain, wg_t, wu_t, name):
    t, d = x.shape
    f = wg_t.shape[0]
    tm = 512

    def body(x_ref, g_ref, wg_ref, wu_ref, h_ref, a_ref, b_ref, s_ref):
        xv = x_ref[...]
        h = (xv * _rms_scale(xv) * g_ref[...]).astype(BF16)
        h_ref[...] = h
        for c in range(f // FF_CHUNK):
            sl = pl.ds(c * FF_CHUNK, FF_CHUNK)
            a = _dot_nt(h, wg_ref[sl, :])
            b = _dot_nt(h, wu_ref[sl, :])
            a_ref[:, sl] = a.astype(BF16)
            b_ref[:, sl] = b.astype(BF16)
            s_ref[:, sl] = (a * jax.nn.sigmoid(a) * b).astype(BF16)

    wide = jax.ShapeDtypeStruct((t, f), BF16)
    return pl.pallas_call(
        body, name=name, grid=(t // tm,),
        in_specs=[_rows(tm, d), _resident((1, d)), _resident((f, d)), _resident((f, d))],
        out_specs=[_rows(tm, d), _rows(tm, f), _rows(tm, f), _rows(tm, f)],
        out_shape=[jax.ShapeDtypeStruct((t, d), BF16), wide, wide, wide],
        compiler_params=_params("arbitrary"),
    )(x, gain, wg_t, wu_t)


def _ffn_down(s, wd, x, target, name):
    t, d = x.shape
    f = wd.shape[0]
    tm = 512
    with_loss = target is not None

    def body(*refs):
        if with_loss:
            s_ref, w_ref, x_ref, t_ref, dy_ref, dyh_ref, loss_ref = refs
        else:
            s_ref, w_ref, x_ref, y_ref = refs
        y = x_ref[...] + 0.5 * _dot(s_ref[...], w_ref[...])
        if with_loss:
            e = y - t_ref[...]
            dy = e * (1.0 / d)
            dy_ref[...] = dy
            dyh_ref[...] = (0.5 * dy).astype(BF16)

            @pl.when(pl.program_id(0) == 0)
            def _():
                loss_ref[...] = jnp.zeros_like(loss_ref)

            part = jnp.sum(jnp.sum(e * e, axis=0, keepdims=True), axis=1, keepdims=True)
            loss_ref[...] += part * (0.5 / d)
        else:
            y_ref[...] = y

    in_specs = [_rows(tm, f), _resident((f, d)), _rows(tm, d)]
    args = [s, wd, x]
    if with_loss:
        in_specs.append(_rows(tm, d))
        args.append(target)
        out_shape = [jax.ShapeDtypeStruct((t, d), F32), jax.ShapeDtypeStruct((t, d), BF16),
                     jax.ShapeDtypeStruct((1, 1), F32)]
        out_specs = [_rows(tm, d), _rows(tm, d), pl.BlockSpec((1, 1), lambda i: (0, 0))]
    else:
        out_shape = [jax.ShapeDtypeStruct((t, d), F32)]
        out_specs = [_rows(tm, d)]
    return pl.pallas_call(
        body, name=name, grid=(t // tm,), in_specs=in_specs, out_specs=out_specs, out_shape=out_shape,
        compiler_params=_params("arbitrary"),
    )(*args)


def _ffn_bwd_act(dyh, a, b, wd, name):
    t, d = dyh.shape
    f = wd.shape[0]
    tm = 512

    def body(dy_ref, a_ref, b_ref, wd_ref, da_ref, db_ref):
        dyh_v = dy_ref[...]
        for c in range(f // FF_CHUNK):
            sl = pl.ds(c * FF_CHUNK, FF_CHUNK)
            ds = _dot_nt(dyh_v, wd_ref[sl, :])
            av = a_ref[:, sl].astype(F32)
            bv = b_ref[:, sl].astype(F32)
            sig = jax.nn.sigmoid(av)
            da_ref[:, sl] = (ds * bv * (sig * (1.0 + av * (1.0 - sig)))).astype(BF16)
            db_ref[:, sl] = (ds * (av * sig)).astype(BF16)

    wide = jax.ShapeDtypeStruct((t, f), BF16)
    return pl.pallas_call(
        body, name=name, grid=(t // tm,),
        in_specs=[_rows(tm, d), _rows(tm, f), _rows(tm, f), _resident((f, d))],
        out_specs=[_rows(tm, f), _rows(tm, f)], out_shape=[wide, wide],
        compiler_params=_params("arbitrary"),
    )(dyh, a, b, wd)


def _ffn_bwd_dx(da, db, dy, x, gain, wg_t, wu_t, name):
    t, d = x.shape
    f = wg_t.shape[0]
    tm = 512

    def body(da_ref, db_ref, dy_ref, x_ref, g_ref, wg_ref, wu_ref, dx_ref, dg_ref):
        dh = _dot(da_ref[...], wg_ref[...]) + _dot(db_ref[...], wu_ref[...])
        dx, dgain = _rms_bwd(dh, x_ref[...], g_ref[...])
        dx_ref[...] = dy_ref[...] + dx

        @pl.when(pl.program_id(0) == 0)
        def _():
            dg_ref[...] = jnp.zeros_like(dg_ref)

        dg_ref[...] += dgain

    return pl.pallas_call(
        body, name=name, grid=(t // tm,),
        in_specs=[_rows(tm, f), _rows(tm, f), _rows(tm, d), _rows(tm, d), _resident((1, d)), _resident((f, d)),
                  _resident((f, d))],
        out_specs=[_rows(tm, d), pl.BlockSpec((1, d), lambda i: (0, 0))],
        out_shape=[jax.ShapeDtypeStruct((t, d), F32), jax.ShapeDtypeStruct((1, d), F32)],
        compiler_params=_params("arbitrary"),
    )(da, db, dy, x, gain, wg_t, wu_t)


def _wgrad(lhs, b, name):
    t, n = lhs[0].shape
    d = b.shape[1]
    m = len(lhs)
    tn = n // 2 if n * d * m > (4 << 20) else n
    tk = 1024
    nk = t // tk

    def body(*refs):
        a_refs, b_ref, o_refs, accs = refs[:m], refs[m], refs[m + 1:2 * m + 1], refs[2 * m + 1:]
        k = pl.program_id(1)

        @pl.when(k == 0)
        def _():
            for acc in accs:
                acc[...] = jnp.zeros_like(acc)

        bv = b_ref[...]
        for a_ref, acc in zip(a_refs, accs):
            acc[...] += _dot_tn(a_ref[...], bv)

        @pl.when(k == nk - 1)
        def _():
            for o_ref, acc in zip(o_refs, accs):
                o_ref[...] = acc[...].astype(BF16)

    return pl.pallas_call(
        body, name=name, grid=(n // tn, nk),
        in_specs=[pl.BlockSpec((tk, tn), lambda j, k: (k, j))] * m + [pl.BlockSpec((tk, d), lambda j, k: (k, 0))],
        out_specs=[pl.BlockSpec((tn, d), lambda j, k: (j, 0))] * m,
        out_shape=[jax.ShapeDtypeStruct((n, d), BF16)] * m,
        scratch_shapes=[pltpu.VMEM((tn, d), F32)] * m,
        compiler_params=_params("arbitrary", "arbitrary"),
    )(*lhs, b)


def _mix_in_fwd(x, gain, w_in_t):
    t, d = x.shape
    tm = 1024
    pw, aw = POOL_WIDTH, ATTN_WIDTH

    def body(x_ref, g_ref, w_ref, hm_ref, pv_ref, q_ref, k_ref, v_ref, f_ref):
        xv = x_ref[...]
        hm = (xv * _rms_scale(xv) * g_ref[...]).astype(BF16)
        hm_ref[...] = hm
        pv_ref[...] = _dot_nt(hm, w_ref[pl.ds(0, pw), :])
        q_ref[...] = _dot_nt(hm, w_ref[pl.ds(pw, aw), :])
        k_ref[...] = _dot_nt(hm, w_ref[pl.ds(pw + aw, aw), :])
        v_ref[...] = _dot_nt(hm, w_ref[pl.ds(pw + 2 * aw, aw), :]).astype(BF16)
        f_ref[...] = _dot_nt(hm, w_ref[pl.ds(pw + 3 * aw, LANES), :])

    return pl.pallas_call(
        body, name="mix_in_fwd", grid=(t // tm,),
        in_specs=[_rows(tm, d), _resident((1, d)), _resident((MIX_PAD, d))],
        out_specs=[_rows(tm, d), _rows(tm, pw), _rows(tm, aw), _rows(tm, aw), _rows(tm, aw), _rows(tm, LANES)],
        out_shape=[jax.ShapeDtypeStruct((t, d), BF16), jax.ShapeDtypeStruct((t, pw), F32),
                   jax.ShapeDtypeStruct((t, aw), F32), jax.ShapeDtypeStruct((t, aw), F32),
                   jax.ShapeDtypeStruct((t, aw), BF16), jax.ShapeDtypeStruct((t, LANES), F32)],
        compiler_params=_params("arbitrary"),
    )(x, gain, w_in_t)


def _pool_fwd(pv, pool_w, pool_scale, gain, bsz, seq):
    ts = 512
    ns = seq // ts
    pw = POOL_WIDTH

    def body(pv_ref, w_ref, sc_ref, g_ref, pooled_ref, mixed_ref, y_ref, ext):
        s = pl.program_id(1)

        @pl.when(s == 0)
        def _():
            ext[pl.ds(0, POOL_HALO), :] = jnp.zeros((POOL_HALO, pw), F32)

        p = pv_ref[...]
        ext[pl.ds(POOL_HALO, ts), :] = p
        pos = s * ts + lax.broadcasted_iota(jnp.int32, (ts, 1), 0)
        parts = []
        for g, w in enumerate(POOL_WINDOWS):
            lanes = pl.ds(g * POOL_GROUP_DIM, POOL_GROUP_DIM)
            win = ext[pl.ds(POOL_HALO, ts), lanes]
            for i in range(1, w):
                win = win + ext[pl.ds(POOL_HALO - i, ts), lanes]
            cnt = jnp.minimum(pos + 1, w).astype(F32)
            pooled = (win / cnt - ext[pl.ds(POOL_HALO, ts), lanes]).astype(BF16)
            pooled_ref[:, lanes] = pooled
            parts.append(_dot(pooled, w_ref[g].astype(BF16)))
        mixed = jnp.concatenate(parts, axis=1)
        mixed_ref[...] = mixed
        pm = mixed * sc_ref[...]
        y_ref[...] = (pm * _rms_scale(pm) * g_ref[...]).astype(BF16)
        ext[pl.ds(0, POOL_HALO), :] = p[ts - POOL_HALO:, :]

    blk = pl.BlockSpec((ts, pw), lambda b, s: (b * ns + s, 0))
    t = bsz * seq
    return pl.pallas_call(
        body, name="pool_fwd", grid=(bsz, ns),
        in_specs=[blk, pl.BlockSpec((POOL_GROUPS, POOL_GROUP_DIM, POOL_GROUP_DIM), lambda b, s: (0, 0, 0)),
                  pl.BlockSpec((1, pw), lambda b, s: (0, 0)), pl.BlockSpec((1, pw), lambda b, s: (0, 0))],
        out_specs=[blk, blk, blk],
        out_shape=[jax.ShapeDtypeStruct((t, pw), BF16), jax.ShapeDtypeStruct((t, pw), F32),
                   jax.ShapeDtypeStruct((t, pw), BF16)],
        scratch_shapes=[pltpu.VMEM((POOL_HALO + ts, pw), F32)],
        compiler_params=_params("arbitrary", "arbitrary"),
    )(pv, pool_w, pool_scale, gain)


def _pool_bwd(dy, mixed, pooled, pool_w, pool_scale, gain, bsz, seq):
    ts = 512
    ns = seq // ts
    pw = POOL_WIDTH

    def body(dy_ref, mixed_ref, pooled_ref, w_ref, sc_ref, g_ref, dpv_ref, dw_ref, dsc_ref, dg_ref, ext):
        b = pl.program_id(0)
        sr = pl.program_id(1)
        s = ns - 1 - sr

        @pl.when(jnp.logical_and(b == 0, sr == 0))
        def _():
            dw_ref[...] = jnp.zeros_like(dw_ref)
            dsc_ref[...] = jnp.zeros_like(dsc_ref)
            dg_ref[...] = jnp.zeros_like(dg_ref)

        @pl.when(sr == 0)
        def _():
            ext[pl.ds(ts, POOL_HALO), :] = jnp.zeros((POOL_HALO, pw), F32)

        mixed = mixed_ref[...]
        sc = sc_ref[...]
        dpm, dgain = _rms_bwd(dy_ref[...], mixed * sc, g_ref[...])
        dg_ref[...] += dgain
        dsc_ref[...] += jnp.sum(dpm * mixed, axis=0, keepdims=True)
        dmixed = (dpm * sc).astype(BF16)
        pos = s * ts + lax.broadcasted_iota(jnp.int32, (ts, 1), 0)
        dpooled = []
        for g, w in enumerate(POOL_WINDOWS):
            lanes = pl.ds(g * POOL_GROUP_DIM, POOL_GROUP_DIM)
            dm = dmixed[:, g * POOL_GROUP_DIM:(g + 1) * POOL_GROUP_DIM]
            dw_ref[g] += _dot_tn(pooled_ref[:, lanes], dm)
            dp = _dot_nt(dm, w_ref[g].astype(BF16))
            dpooled.append(dp)
            cnt = jnp.minimum(pos + 1, w).astype(F32)
            ext[pl.ds(0, ts), lanes] = dp / cnt
        for g, w in enumerate(POOL_WINDOWS):
            lanes = pl.ds(g * POOL_GROUP_DIM, POOL_GROUP_DIM)
            win = ext[pl.ds(0, ts), lanes]
            for i in range(1, w):
                win = win + ext[pl.ds(i, ts), lanes]
            dpv_ref[:, lanes] = (win - dpooled[g]).astype(BF16)
        head = ext[pl.ds(0, POOL_HALO), :]
        ext[pl.ds(ts, POOL_HALO), :] = head

    blk = pl.BlockSpec((ts, pw), lambda b, s: (b * ns + (ns - 1 - s), 0))
    vec = pl.BlockSpec((1, pw), lambda b, s: (0, 0))
    wspec = pl.BlockSpec((POOL_GROUPS, POOL_GROUP_DIM, POOL_GROUP_DIM), lambda b, s: (0, 0, 0))
    t = bsz * seq
    return pl.pallas_call(
        body, name="pool_bwd", grid=(bsz, ns),
        in_specs=[blk, blk, blk, wspec, vec, vec],
        out_specs=[blk, wspec, vec, vec],
        out_shape=[jax.ShapeDtypeStruct((t, pw), BF16),
                   jax.ShapeDtypeStruct((POOL_GROUPS, POOL_GROUP_DIM, POOL_GROUP_DIM), F32),
                   jax.ShapeDtypeStruct((1, pw), F32), jax.ShapeDtypeStruct((1, pw), F32)],
        scratch_shapes=[pltpu.VMEM((ts + POOL_HALO, pw), F32)],
        compiler_params=_params("arbitrary", "arbitrary"),
    )(dy, mixed, pooled, pool_w, pool_scale, gain)


AUX_ONE = 64
AUX_F = 67

ATTN_PREP_ROWS = 512


def _seg_ones(width, seg):
    r = lax.broadcasted_iota(jnp.int32, (width, width), 0) // seg
    c = lax.broadcasted_iota(jnp.int32, (width, width), 1) // seg
    return (r == c).astype(BF16)


def _tri_ones(n, lower):
    r = lax.broadcasted_iota(jnp.int32, (n, n), 0)
    c = lax.broadcasted_iota(jnp.int32, (n, n), 1)
    return ((r >= c) if lower else (r <= c)).astype(BF16)


def _place_pieces(first_lane):
    r = lax.broadcasted_iota(jnp.int32, (3 * LANES, N_HEADS * LANES), 0)
    c = lax.broadcasted_iota(jnp.int32, (3 * LANES, N_HEADS * LANES), 1)
    piece, head = r // LANES, r % LANES
    return jnp.logical_and(head < N_HEADS, c == head * LANES + first_lane + piece).astype(BF16)


def _head_sums(x, seg_ones):
    hi, lo = _split2(x)
    return _dot(hi, seg_ones) + _dot(lo, seg_ones)


def _log_sigmoid(x):
    return jnp.minimum(x, 0.0) - jnp.log(1.0 + jnp.exp(-jnp.abs(x)))


def _attn_prep_fwd(q, k, f, b_forget, q_gain, k_gain, bsz, seq):
    ts = ATTN_PREP_ROWS
    ns = seq // ts
    aw = ATTN_WIDTH
    t = bsz * seq
    seg = _seg_ones(aw, HEAD_DIM)
    tri = _tri_ones(ts, True)

    def body(q_ref, k_ref, f_ref, bf_ref, gq_ref, gk_ref, seg_ref, tri_ref, pq_ref, pk_ref, qp_ref, kp_ref, carry):
        s = pl.program_id(1)

        @pl.when(s == 0)
        def _():
            carry[...] = jnp.zeros_like(carry)

        logf = _log_sigmoid(f_ref[...] + bf_ref[...])
        hi, mid, lo = _split3(logf)
        tri_v = tri_ref[...]
        fc = _dot(tri_v, hi) + _dot(tri_v, mid) + _dot(tri_v, lo) + carry[pl.ds(0, 1), :]
        carry[pl.ds(0, 1), :] = fc[ts - 1:, :]
        pcs = jnp.concatenate(_split3(fc), axis=1)
        lane = lax.broadcasted_iota(jnp.int32, (1, LANES), 1)
        ones_q = jnp.logical_and(lane >= AUX_ONE, lane < AUX_ONE + 3).astype(F32)
        ones_k = jnp.logical_and(lane >= AUX_F, lane < AUX_F + 3).astype(F32)
        seg_v = seg_ref[...]

        def build(x_ref, g_ref, scale, out_ref, ones, place_ref, f_sign):
            xv = x_ref[...]
            r = lax.rsqrt(_head_sums(xv * xv, seg_v) * (1.0 / HEAD_DIM) + EPS)
            xn = xv * r * g_ref[...] * scale
            aux = _dot(pcs, place_ref[...]) * f_sign
            for h in range(N_HEADS):
                pair = xn[:, (h // 2) * LANES:(h // 2 + 1) * LANES]
                feat = pair if h % 2 == 0 else pltpu.roll(pair, HEAD_DIM, 1)
                aux_h = aux[:, h * LANES:(h + 1) * LANES] + ones
                out_ref[:, h * LANES:(h + 1) * LANES] = jnp.where(lane < HEAD_DIM, feat, aux_h).astype(BF16)

        build(q_ref, gq_ref, 0.125, qp_ref, ones_q, pq_ref, 1.0)
        build(k_ref, gk_ref, 1.0, kp_ref, ones_k, pk_ref, -1.0)

    blk = pl.BlockSpec((ts, aw), lambda b, s: (b * ns + s, 0))
    fblk = pl.BlockSpec((ts, LANES), lambda b, s: (b * ns + s, 0))
    oblk = pl.BlockSpec((ts, N_HEADS * LANES), lambda b, s: (b * ns + s, 0))
    const = lambda shape: pl.BlockSpec(shape, lambda b, s: (0, 0))
    return pl.pallas_call(
        body, name="attn_prep_fwd", grid=(bsz, ns),
        in_specs=[blk, blk, fblk, const((1, LANES)), const((1, aw)), const((1, aw)), const((aw, aw)), const((ts, ts)),
                  const((3 * LANES, N_HEADS * LANES)), const((3 * LANES, N_HEADS * LANES))],
        out_specs=[oblk, oblk],
        out_shape=[jax.ShapeDtypeStruct((t, N_HEADS * LANES), BF16)] * 2,
        scratch_shapes=[pltpu.VMEM((8, LANES), F32)],
        compiler_params=_params("arbitrary", "arbitrary"),
    )(q, k, f, b_forget, q_gain, k_gain, seg, tri, _place_pieces(AUX_F), _place_pieces(AUX_ONE))


def _attn_prep_bwd(dqp, dkp, q, k, f, b_forget, q_gain, k_gain, bsz, seq):
    ts = ATTN_PREP_ROWS
    ns = seq // ts
    aw = ATTN_WIDTH
    t = bsz * seq
    seg = _seg_ones(aw, HEAD_DIM)
    tri = _tri_ones(ts, False)

    def body(dqp_ref, dkp_ref, q_ref, k_ref, f_ref, bf_ref, gq_ref, gk_ref, seg_ref, tri_ref,
             dq_ref, dk_ref, df_ref, dgq_ref, dgk_ref, dbf_ref, carry):
        b = pl.program_id(0)
        sr = pl.program_id(1)

        @pl.when(jnp.logical_and(b == 0, sr == 0))
        def _():
            dgq_ref[...] = jnp.zeros_like(dgq_ref)
            dgk_ref[...] = jnp.zeros_like(dgk_ref)
            dbf_ref[...] = jnp.zeros_like(dbf_ref)

        @pl.when(sr == 0)
        def _():
            carry[...] = jnp.zeros_like(carry)

        lane = lax.broadcasted_iota(jnp.int32, (1, LANES), 1)
        seg_v = seg_ref[...]

        def norm_bwd(dp_ref, x_ref, g_ref, scale, dx_ref, dgain_ref):
            parts = []
            for j in range(N_HEADS // 2):
                even = dp_ref[:, (2 * j) * LANES:(2 * j + 1) * LANES]
                odd = dp_ref[:, (2 * j + 1) * LANES:(2 * j + 2) * LANES]
                parts.append(jnp.where(lane < HEAD_DIM, even, pltpu.roll(odd, HEAD_DIM, 1)))
            dxn = jnp.concatenate(parts, axis=1) * scale
            xv = x_ref[...]
            r = lax.rsqrt(_head_sums(xv * xv, seg_v) * (1.0 / HEAD_DIM) + EPS)
            n = xv * r
            dgain_ref[...] += jnp.sum(dxn * n, axis=0, keepdims=True)
            dn = dxn * g_ref[...]
            m = _head_sums(dn * n, seg_v) * (1.0 / HEAD_DIM)
            dx_ref[...] = (r * (dn - n * m)).astype(BF16)

        norm_bwd(dqp_ref, q_ref, gq_ref, 0.125, dq_ref, dgq_ref)
        norm_bwd(dkp_ref, k_ref, gk_ref, 1.0, dk_ref, dgk_ref)

        dfc = jnp.zeros((ts, LANES), F32)
        for h in range(N_HEADS):
            cols = pl.ds(h * LANES, LANES)
            both = jnp.where(lane == AUX_F, dqp_ref[:, cols], 0.0) - jnp.where(lane == AUX_ONE, dkp_ref[:, cols], 0.0)
            dfc = jnp.where(lane == h, jnp.sum(both, axis=1, keepdims=True), dfc)
        hi, mid, lo = _split3(dfc)
        tri_v = tri_ref[...]
        dlogf = _dot(tri_v, hi) + _dot(tri_v, mid) + _dot(tri_v, lo) + carry[pl.ds(0, 1), :]
        carry[pl.ds(0, 1), :] = dlogf[0:1, :]
        df = jnp.where(lane < N_HEADS, dlogf * jax.nn.sigmoid(-(f_ref[...] + bf_ref[...])), 0.0)
        df_ref[...] = df.astype(BF16)
        dbf_ref[...] += jnp.sum(df, axis=0, keepdims=True)

    rev = lambda b, s: (b * ns + (ns - 1 - s), 0)
    blk = pl.BlockSpec((ts, aw), rev)
    fblk = pl.BlockSpec((ts, LANES), rev)
    pblk = pl.BlockSpec((ts, N_HEADS * LANES), rev)
    const = lambda shape: pl.BlockSpec(shape, lambda b, s: (0, 0))
    return pl.pallas_call(
        body, name="attn_prep_bwd", grid=(bsz, ns),
        in_specs=[pblk, pblk, blk, blk, fblk, const((1, LANES)), const((1, aw)), const((1, aw)), const((aw, aw)),
                  const((ts, ts))],
        out_specs=[blk, blk, fblk, const((1, aw)), const((1, aw)), const((1, LANES))],
        out_shape=[jax.ShapeDtypeStruct((t, aw), BF16), jax.ShapeDtypeStruct((t, aw), BF16),
                   jax.ShapeDtypeStruct((t, LANES), BF16), jax.ShapeDtypeStruct((1, aw), F32),
                   jax.ShapeDtypeStruct((1, aw), F32), jax.ShapeDtypeStruct((1, LANES), F32)],
        scratch_shapes=[pltpu.VMEM((8, LANES), F32)],
        compiler_params=_params("arbitrary", "arbitrary"),
    )(dqp, dkp, q, k, f, b_forget, q_gain, k_gain, seg, tri)


ATTN_BLOCK = 512
HEAD_PAIRS = N_HEADS // 2


def _flash_fwd(qp, kp, v, bsz, seq):
    tq = ATTN_BLOCK
    nq = seq // tq
    t = bsz * seq

    def body(q_ref, k_ref, v_ref, o_ref, lse_ref, m_sc, l_sc, acc_sc):
        i = pl.program_id(2)
        m_sc[...] = jnp.full(m_sc.shape, -jnp.inf, F32)
        l_sc[...] = jnp.zeros_like(l_sc)
        acc_sc[...] = jnp.zeros_like(acc_sc)
        lane = lax.broadcasted_iota(jnp.int32, (1, LANES), 1)
        low = lane < HEAD_DIM

        def key_block(j, masked):
            rows = pl.ds(pl.multiple_of(j * tq, tq), tq)
            vv = v_ref[rows, :]
            for h in range(2):
                mine = low if h == 0 else jnp.logical_not(low)
                s = _dot_nt(q_ref[:, h * LANES:(h + 1) * LANES], k_ref[rows, pl.ds(h * LANES, LANES)])
                if masked:
                    row = lax.broadcasted_iota(jnp.int32, (tq, tq), 0)
                    col = lax.broadcasted_iota(jnp.int32, (tq, tq), 1)
                    s = jnp.where(row >= col, s, -jnp.inf)
                m_prev = m_sc[h]
                m_new = jnp.maximum(m_prev, jnp.max(s, axis=1, keepdims=True))
                p = jnp.exp(s - jnp.tile(m_new, (1, tq // LANES)))
                alpha = jnp.exp(m_prev - m_new)
                l_sc[h] = alpha * l_sc[h] + jnp.sum(p, axis=1, keepdims=True)
                m_sc[h] = m_new
                pv = _dot(p.astype(BF16), jnp.where(mine, vv, jnp.zeros_like(vv)))
                acc_sc[...] = acc_sc[...] * jnp.where(mine, alpha, 1.0) + pv

        def below_diagonal(j, carry):
            key_block(j, False)
            return carry

        lax.fori_loop(0, i, below_diagonal, 0)
        key_block(i, True)
        l = jnp.where(low, l_sc[0], l_sc[1])
        m = jnp.where(low, m_sc[0], m_sc[1])
        o_ref[...] = acc_sc[...] / l
        lse_ref[...] = m + jnp.log(l)

    qspec = pl.BlockSpec((tq, 2 * LANES), lambda b, hp, i: (b * nq + i, hp))
    kspec = pl.BlockSpec((seq, 2 * LANES), lambda b, hp, i: (b, hp))
    vspec = pl.BlockSpec((seq, LANES), lambda b, hp, i: (b, hp))
    ospec = pl.BlockSpec((tq, LANES), lambda b, hp, i: (b * nq + i, hp))
    return pl.pallas_call(
        body, name="flash_fwd", grid=(bsz, HEAD_PAIRS, nq),
        in_specs=[qspec, kspec, vspec], out_specs=[ospec, ospec],
        out_shape=[jax.ShapeDtypeStruct((t, ATTN_WIDTH), F32), jax.ShapeDtypeStruct((t, ATTN_WIDTH), F32)],
        scratch_shapes=[pltpu.VMEM((2, tq, LANES), F32), pltpu.VMEM((2, tq, LANES), F32), pltpu.VMEM((tq, LANES), F32)],
        compiler_params=_params("arbitrary", "arbitrary", "arbitrary"),
    )(qp, kp, v)


def _flash_bwd(qp, kp, v, o, do, lse, bsz, seq):
    tq = ATTN_BLOCK
    nq = seq // tq
    t = bsz * seq

    def body(q_ref, k_ref, v_ref, o_ref, do_ref, lse_ref, dq_ref, dk_ref, dv_ref, dk_acc, dv_acc):
        j = pl.program_id(2)

        @pl.when(j == 0)
        def _():
            dq_ref[...] = jnp.zeros_like(dq_ref)

        dk_acc[...] = jnp.zeros_like(dk_acc)
        dv_acc[...] = jnp.zeros_like(dv_acc)
        lane = lax.broadcasted_iota(jnp.int32, (1, LANES), 1)
        low = lane < HEAD_DIM

        def query_block(i, masked):
            rows = pl.ds(pl.multiple_of(i * tq, tq), tq)
            dov = do_ref[rows, :]
            dd = dov * o_ref[rows, :]
            dob = dov.astype(BF16)
            vv = v_ref[...]
            lse_v = lse_ref[rows, :]
            for h in range(2):
                mine = low if h == 0 else jnp.logical_not(low)
                cols = pl.ds(h * LANES, LANES)
                qh = q_ref[rows, cols]
                kh = k_ref[:, cols]
                s = _dot_nt(qh, kh)
                lse_h = jnp.where(mine, lse_v, pltpu.roll(lse_v, HEAD_DIM, 1))
                p = jnp.exp(s - jnp.tile(lse_h, (1, tq // LANES)))
                if masked:
                    row = lax.broadcasted_iota(jnp.int32, (tq, tq), 0)
                    col = lax.broadcasted_iota(jnp.int32, (tq, tq), 1)
                    p = jnp.where(row >= col, p, 0.0)
                delta = jnp.sum(jnp.where(mine, dd, 0.0), axis=1, keepdims=True)
                dp = _dot_nt(dob, jnp.where(mine, vv, jnp.zeros_like(vv)))
                ds = (p * (dp - delta)).astype(BF16)
                dv_acc[...] += jnp.where(mine, _dot_tn(p.astype(BF16), dob), 0.0)
                dk_acc[:, cols] += _dot_tn(ds, qh)
                dq_ref[rows, cols] += _dot(ds, kh)

        def above_diagonal(i, carry):
            query_block(i, False)
            return carry

        query_block(j, True)
        lax.fori_loop(j + 1, nq, above_diagonal, 0)
        dk_ref[...] = dk_acc[...]
        dv_ref[...] = dv_acc[...].astype(BF16)

    qspec = pl.BlockSpec((seq, 2 * LANES), lambda b, hp, j: (b, hp))
    kspec = pl.BlockSpec((tq, 2 * LANES), lambda b, hp, j: (b * nq + j, hp))
    vspec = pl.BlockSpec((tq, LANES), lambda b, hp, j: (b * nq + j, hp))
    ospec = pl.BlockSpec((seq, LANES), lambda b, hp, j: (b, hp))
    return pl.pallas_call(
        body, name="flash_bwd", grid=(bsz, HEAD_PAIRS, nq),
        in_specs=[qspec, kspec, vspec, ospec, ospec, ospec], out_specs=[qspec, kspec, vspec],
        out_shape=[jax.ShapeDtypeStruct((t, N_HEADS * LANES), F32), jax.ShapeDtypeStruct((t, N_HEADS * LANES), F32),
                   jax.ShapeDtypeStruct((t, ATTN_WIDTH), BF16)],
        scratch_shapes=[pltpu.VMEM((tq, 2 * LANES), F32), pltpu.VMEM((tq, LANES), F32)],
        compiler_params=_params("arbitrary", "arbitrary", "arbitrary"),
    )(qp, kp, v, o, do, lse)


def _mix_out_fwd(o, y_pool, x, gain, w_out):
    t, d = x.shape
    tm = 1024
    pw, aw = POOL_WIDTH, ATTN_WIDTH

    def body(o_ref, yp_ref, x_ref, g_ref, w_ref, ycat_ref, y_ref):
        ov = o_ref[...]
        ya = (ov * _rms_scale(ov) * g_ref[...]).astype(BF16)
        ycat = jnp.concatenate([yp_ref[...], ya], axis=1)
        ycat_ref[...] = ycat
        y_ref[...] = x_ref[...] + _dot(ycat, w_ref[...])

    return pl.pallas_call(
        body, name="mix_out_fwd", grid=(t // tm,),
        in_specs=[_rows(tm, aw), _rows(tm, pw), _rows(tm, d), _resident((1, aw)), _resident((pw + aw, d))],
        out_specs=[_rows(tm, pw + aw), _rows(tm, d)],
        out_shape=[jax.ShapeDtypeStruct((t, pw + aw), BF16), jax.ShapeDtypeStruct((t, d), F32)],
        compiler_params=_params("arbitrary"),
    )(o, y_pool, x, gain, w_out)


def _mix_out_bwd(dx, o, gain, w_out):
    t, d = dx.shape
    tm = 1024
    pw, aw = POOL_WIDTH, ATTN_WIDTH

    def body(dx_ref, o_ref, g_ref, w_ref, dxb_ref, dyp_ref, do_ref, dg_ref):
        dxb = dx_ref[...].astype(BF16)
        dxb_ref[...] = dxb
        dyp_ref[...] = _dot_nt(dxb, w_ref[pl.ds(0, pw), :])
        dya = _dot_nt(dxb, w_ref[pl.ds(pw, aw), :])
        do, dgain = _rms_bwd(dya, o_ref[...], g_ref[...])
        do_ref[...] = do

        @pl.when(pl.program_id(0) == 0)
        def _():
            dg_ref[...] = jnp.zeros_like(dg_ref)

        dg_ref[...] += dgain

    return pl.pallas_call(
        body, name="mix_out_bwd", grid=(t // tm,),
        in_specs=[_rows(tm, d), _rows(tm, aw), _resident((1, aw)), _resident((pw + aw, d))],
        out_specs=[_rows(tm, d), _rows(tm, pw), _rows(tm, aw), pl.BlockSpec((1, aw), lambda i: (0, 0))],
        out_shape=[jax.ShapeDtypeStruct((t, d), BF16), jax.ShapeDtypeStruct((t, pw), F32),
                   jax.ShapeDtypeStruct((t, aw), F32), jax.ShapeDtypeStruct((1, aw), F32)],
        compiler_params=_params("arbitrary"),
    )(dx, o, gain, w_out)


def _mix_in_bwd(dpv, dq, dk, dv, df, x, dx_res, gain, w_in_t):
    t, d = x.shape
    tm = 512
    pw, aw = POOL_WIDTH, ATTN_WIDTH

    def body(dpv_ref, dq_ref, dk_ref, dv_ref, df_ref, x_ref, dxr_ref, g_ref, w_ref, dh_ref, dx_ref, dxh_ref, dg_ref):
        dh = jnp.concatenate([dpv_ref[...], dq_ref[...], dk_ref[...], dv_ref[...], df_ref[...]], axis=1)
        dh_ref[...] = dh
        dhm = _dot(dh, w_ref[...])
        dx, dgain = _rms_bwd(dhm, x_ref[...], g_ref[...])
        dx = dxr_ref[...] + dx
        dx_ref[...] = dx
        dxh_ref[...] = (0.5 * dx).astype(BF16)

        @pl.when(pl.program_id(0) == 0)
        def _():
            dg_ref[...] = jnp.zeros_like(dg_ref)

        dg_ref[...] += dgain

    return pl.pallas_call(
        body, name="mix_in_bwd", grid=(t // tm,),
        in_specs=[_rows(tm, pw), _rows(tm, aw), _rows(tm, aw), _rows(tm, aw), _rows(tm, LANES), _rows(tm, d),
                  _rows(tm, d), _resident((1, d)), _resident((MIX_PAD, d))],
        out_specs=[_rows(tm, MIX_PAD), _rows(tm, d), _rows(tm, d), pl.BlockSpec((1, d), lambda i: (0, 0))],
        out_shape=[jax.ShapeDtypeStruct((t, MIX_PAD), BF16), jax.ShapeDtypeStruct((t, d), F32),
                   jax.ShapeDtypeStruct((t, d), BF16), jax.ShapeDtypeStruct((1, d), F32)],
        compiler_params=_params("arbitrary"),
    )(dpv, dq, dk, dv, df, x, dx_res, gain, w_in_t)


MESH_IDS = pl.DeviceIdType.MESH


def _me():
    return lax.axis_index("x"), lax.axis_index("y"), lax.axis_index("c")


def _peer(x, y, c, p):
    px = 1 - x if p & 4 else x
    py = 1 - y if p & 2 else y
    pc = 1 - c if p & 1 else c
    return (px, py, pc), 4 * px + 2 * py + pc


HBM_SPEC = pl.BlockSpec(memory_space=pltpu.HBM)
SEM_SPEC = pl.BlockSpec(memory_space=pltpu.SEMAPHORE)
SPLIT_COPY = pltpu.CompilerParams(has_side_effects=pltpu.SideEffectType.DATAFLOW_SIDE_EFFECTING)
PEERS = N_DEV - 1


def _hbm(a):
    return pltpu.with_memory_space_constraint(a, pltpu.HBM)


def _row_block(ref, dev, rows):
    return ref.at[pl.ds(pl.multiple_of(dev * rows, BF16_ROWS), rows)]


def _copy_ends(gather, src, land, me, peer_id):
    if gather:
        rows = src.shape[0]
        return src, _row_block(land, me, rows), _row_block(land, peer_id, rows), src, _row_block(land, me, rows)
    rows = src.shape[0] // N_DEV
    return (_row_block(src, peer_id, rows), land.at[me], land.at[peer_id], _row_block(src, me, rows), land.at[me])


def _land_shape(gather, s):
    return (N_DEV * s.shape[0], s.shape[1]) if gather else (N_DEV, s.shape[0] // N_DEV, s.shape[1])


def _copies_start(groups, gather, name, after=None):
    flat = [s for g in groups for s in g]
    n, ng = len(flat), len(groups)
    lands = [lax.empty(_land_shape(gather, s), s.dtype) for s in flat]
    n_in = 2 * n + (after is not None)

    def body(*refs):
        ins, lnd = refs[:n], refs[n:2 * n]
        sems = refs[n_in:n_in + 2 * ng]
        token = refs[-1]
        x, y, c = _me()
        me = 4 * x + 2 * y + c
        w = 0
        for gi, g in enumerate(groups):
            for k in range(len(g)):
                for p in range(1, N_DEV):
                    peer, peer_id = _peer(x, y, c, p)
                    src, dst, _, _, _ = _copy_ends(gather, ins[w], lnd[w], me, peer_id)
                    pltpu.make_async_remote_copy(src, dst, sems[2 * gi].at[k * PEERS + p - 1],
                                                 sems[2 * gi + 1].at[k * PEERS + p - 1], device_id=peer,
                                                 device_id_type=MESH_IDS).start()
                w += 1
        token[...] = jnp.zeros_like(token)

    sem_shapes = []
    for g in groups:
        sem_shapes += [pltpu.SemaphoreType.DMA((len(g) * PEERS,))] * 2
    out = pl.pallas_call(
        body, name=name,
        out_shape=(*sem_shapes, *[pltpu.HBM(s.shape, s.dtype) for s in flat],
                   *[pltpu.HBM(l.shape, l.dtype) for l in lands], jax.ShapeDtypeStruct((8, LANES), F32)),
        in_specs=[HBM_SPEC] * (2 * n) + [pl.BlockSpec(memory_space=pl.ANY)] * (after is not None),
        out_specs=(*[SEM_SPEC] * (2 * ng), *[HBM_SPEC] * (2 * n), pl.BlockSpec(memory_space=pltpu.VMEM)),
        input_output_aliases={i: 2 * ng + i for i in range(2 * n)},
        compiler_params=SPLIT_COPY,
    )(*[_hbm(s) for s in flat], *[_hbm(l) for l in lands], *([after] if after is not None else []))
    sems, thru, token = out[:2 * ng], out[2 * ng:2 * ng + 2 * n], out[-1]
    res, w = [], 0
    for gi, g in enumerate(groups):
        res.append((sems[2 * gi], sems[2 * gi + 1], list(thru[w:w + len(g)]), list(thru[n + w:n + w + len(g)])))
        w += len(g)
    return res, token


def _copies_wait(started, gather, after, name):
    send, recv, srcs, lands = started
    n = len(srcs)
    after = list(after) if isinstance(after, (list, tuple)) else [after]

    own_shapes = [s.shape if gather else (s.shape[0] // N_DEV, s.shape[1]) for s in srcs]

    def body(*refs):
        ins, lnd = refs[:n], refs[n:2 * n]
        send_sems, recv_sems = refs[2 * n], refs[2 * n + 1]
        bounce, in_sems, out_sems = refs[-n - 2:-2], refs[-2], refs[-1]
        x, y, c = _me()
        me = 4 * x + 2 * y + c
        ends = [_copy_ends(gather, ins[w], lnd[w], me, me)[3:] for w in range(n)]
        loads = [pltpu.make_async_copy(ends[w][0], bounce[w], in_sems.at[w]) for w in range(n)]
        stores = [pltpu.make_async_copy(bounce[w], ends[w][1], out_sems.at[w]) for w in range(n)]
        for cp in loads:
            cp.start()
        for w in range(n):
            loads[w].wait()
            stores[w].start()
        for w in range(n):
            for p in range(1, N_DEV):
                peer, peer_id = _peer(x, y, c, p)
                src, _, arrival, _, _ = _copy_ends(gather, ins[w], lnd[w], me, peer_id)
                cp = pltpu.make_async_remote_copy(src, arrival, send_sems.at[w * PEERS + p - 1],
                                                  recv_sems.at[w * PEERS + p - 1], device_id=peer,
                                                  device_id_type=MESH_IDS)
                cp.wait_send()
                cp.wait_recv()
        for cp in stores:
            cp.wait()

    out = pl.pallas_call(
        body, name=name,
        out_shape=(*[pltpu.HBM(s.shape, s.dtype) for s in srcs], *[pltpu.HBM(l.shape, l.dtype) for l in lands]),
        in_specs=[HBM_SPEC] * (2 * n) + [SEM_SPEC, SEM_SPEC] + [pl.BlockSpec(memory_space=pl.ANY)] * len(after),
        out_specs=[HBM_SPEC] * (2 * n),
        input_output_aliases={i: i for i in range(2 * n)},
        scratch_shapes=[*[pltpu.VMEM(shape, s.dtype) for shape, s in zip(own_shapes, srcs)],
                        pltpu.SemaphoreType.DMA((n,)), pltpu.SemaphoreType.DMA((n,))],
        compiler_params=SPLIT_COPY,
    )(*srcs, *lands, send, recv, *after)
    return list(out[n:])


def _adamw_update(w, g, m, v):
    nm = ADAM_B1 * m + (1.0 - ADAM_B1) * g
    nv = ADAM_B2 * v + (1.0 - ADAM_B2) * (g * g)
    m_hat = nm / (1.0 - ADAM_B1 ** ADAM_STEP)
    v_hat = nv / (1.0 - ADAM_B2 ** ADAM_STEP)
    return -ADAM_LR * (m_hat / (jnp.sqrt(v_hat) + ADAM_EPS) + ADAM_WD * w), nm, nv


SUM_ADAMW_COLS = 256


def _sum_adamw(parts, w, m, v, name):
    _, rows, d = parts.shape
    n = w.shape[0]
    tc = SUM_ADAMW_COLS

    def body(p_ref, w_ref, m_ref, v_ref, g_ref, d_ref, nm_ref, nv_ref):
        g = p_ref[0].astype(F32)
        for dev in range(1, N_DEV):
            g = g + p_ref[dev].astype(F32)
        g = g[:n]
        g_ref[...] = g
        d_ref[...], nm_ref[...], nv_ref[...] = _adamw_update(w_ref[...], g, m_ref[...], v_ref[...])

    spec = pl.BlockSpec((n, tc), lambda j: (0, j))
    shape = jax.ShapeDtypeStruct((n, d), F32)
    return pl.pallas_call(
        body, name=name, grid=(d // tc,),
        in_specs=[pl.BlockSpec((N_DEV, rows, tc), lambda j: (0, 0, j)), spec, spec, spec],
        out_specs=[spec] * 4, out_shape=[shape] * 4,
        compiler_params=_params("arbitrary"),
    )(parts, w, m, v)


def _pad_rows(a, rows):
    return jnp.pad(a, ((0, rows - a.shape[0]), (0, 0)))


def _row1(vec, width=D_MODEL):
    return jnp.pad(vec.reshape(1, -1), ((0, 0), (0, width - vec.shape[-1])))


COLUMN_SHARDED = ("ffn1_w_gate", "ffn1_w_up", "w_in", "ffn2_w_gate", "ffn2_w_up")
VEC_NAMES = ("ffn1_norm", "mix_norm", "ffn2_norm", "b_forget", "pool_scale", "q_norm", "k_norm", "out_norm_pool",
             "out_norm_attn")
VEC_ROWS = 16
LOSS_ROW = len(VEC_NAMES)


def _pack_vector_grads(parts, loss_part):
    def body(*refs):
        loss_ref, out_ref = refs[-2], refs[-1]
        out_ref[...] = jnp.zeros_like(out_ref)
        lane = lax.broadcasted_iota(jnp.int32, (1, LANES), 1)
        for i, (name, ref) in enumerate(zip(VEC_NAMES, refs[:-2])):
            val = ref[...]
            if name in ("q_norm", "k_norm"):
                val = val[:, 0:LANES] + val[:, LANES:2 * LANES] + val[:, 2 * LANES:3 * LANES] + val[:, 3 * LANES:]
                val = jnp.where(lane < HEAD_DIM, val + pltpu.roll(val, HEAD_DIM, 1), 0.0)
            out_ref[pl.ds(i, 1), pl.ds(0, val.shape[1])] = val
        out_ref[pl.ds(LOSS_ROW, 1), pl.ds(0, 1)] = loss_ref[...]

    vmem = pl.BlockSpec(memory_space=pltpu.VMEM)
    return pl.pallas_call(
        body, name="pack_vector_grads", in_specs=[vmem] * (len(parts) + 1), out_specs=vmem,
        out_shape=jax.ShapeDtypeStruct((VEC_ROWS, D_MODEL), F32),
    )(*parts, loss_part)


def _small_adamw(vec_all, pool_all, vec_params, pool_params):
    nv = len(vec_params)
    pool_rows = pool_params[0].shape[0]

    def body(*refs):
        vec_ref, pool_ref = refs[0], refs[1]
        ins = refs[2:2 + 3 * nv + 3]
        outs = refs[2 + 3 * nv + 3:-1]
        rows = refs[-1]
        total = vec_ref[pl.ds(0, VEC_ROWS), :]
        for dev in range(1, N_DEV):
            total = total + vec_ref[pl.ds(dev * VEC_ROWS, VEC_ROWS), :]
        rows[...] = total
        outs[4 * nv + 4][...] = rows[pl.ds(LOSS_ROW, 1), pl.ds(0, 1)]
        for i in range(nv):
            w_ref, m_ref, v_ref = ins[3 * i:3 * i + 3]
            g = rows[pl.ds(i, 1), pl.ds(0, w_ref.shape[1])]
            outs[4 * i][...] = g
            outs[4 * i + 1][...], outs[4 * i + 2][...], outs[4 * i + 3][...] = _adamw_update(
                w_ref[...], g, m_ref[...], v_ref[...])
        g = pool_ref[pl.ds(0, pool_rows), :]
        for dev in range(1, N_DEV):
            g = g + pool_ref[pl.ds(dev * pool_rows, pool_rows), :]
        w_ref, m_ref, v_ref = ins[3 * nv:]
        outs[4 * nv][...] = g
        outs[4 * nv + 1][...], outs[4 * nv + 2][...], outs[4 * nv + 3][...] = _adamw_update(
            w_ref[...], g, m_ref[...], v_ref[...])

    vmem = pl.BlockSpec(memory_space=pltpu.VMEM)
    flat = [a for trio in vec_params for a in trio] + list(pool_params)
    out_shape = []
    for trio in list(vec_params) + [pool_params]:
        out_shape += [jax.ShapeDtypeStruct(trio[0].shape, F32)] * 4
    out_shape.append(jax.ShapeDtypeStruct((1, 1), F32))
    return pl.pallas_call(
        body, name="adamw_small", in_specs=[vmem] * (2 + len(flat)), out_specs=[vmem] * len(out_shape),
        out_shape=out_shape, scratch_shapes=[pltpu.VMEM((VEC_ROWS, D_MODEL), F32)],
    )(vec_all, pool_all, *flat)


def kernel(x, ffn1_norm, ffn1_w_gate, ffn1_w_up, ffn1_w_down, mix_norm, w_in, b_forget, pool_w, pool_scale, q_norm, k_norm, out_norm_pool, out_norm_attn, w_out, ffn2_norm, ffn2_w_gate, ffn2_w_up, ffn2_w_down, loss_target, m_ffn1_norm, m_ffn1_w_gate, m_ffn1_w_up, m_ffn1_w_down, m_mix_norm, m_w_in, m_b_forget, m_pool_w, m_pool_scale, m_q_norm, m_k_norm, m_out_norm_pool, m_out_norm_attn, m_w_out, m_ffn2_norm, m_ffn2_w_gate, m_ffn2_w_up, m_ffn2_w_down, v_ffn1_norm, v_ffn1_w_gate, v_ffn1_w_up, v_ffn1_w_down, v_mix_norm, v_w_in, v_b_forget, v_pool_w, v_pool_scale, v_q_norm, v_k_norm, v_out_norm_pool, v_out_norm_attn, v_w_out, v_ffn2_norm, v_ffn2_w_gate, v_ffn2_w_up, v_ffn2_w_down):
    bsz, seq, d = x.shape
    t = bsz * seq
    x0 = x.reshape(t, d)
    target = loss_target.reshape(t, d)
    in_rows = -(-w_in.shape[1] // BF16_ROWS) * BF16_ROWS

    slabs = [s.astype(BF16) for s in (ffn1_w_gate.T, ffn1_w_up.T, ffn1_w_down, _pad_rows(w_in.T, in_rows), w_out,
                                       ffn2_w_gate.T, ffn2_w_up.T, ffn2_w_down)]
    gathers, started = _copies_start([slabs[0:2], slabs[2:3], slabs[3:4], slabs[4:5], slabs[5:8]], True, "gather_start")

    g1, gm, g2 = ffn1_norm.reshape(1, d), mix_norm.reshape(1, d), ffn2_norm.reshape(1, d)
    bf_row = _row1(b_forget, LANES)
    gq = jnp.tile(q_norm, N_HEADS).reshape(1, ATTN_WIDTH)
    gk = jnp.tile(k_norm, N_HEADS).reshape(1, ATTN_WIDTH)
    scale_row = pool_scale.reshape(1, POOL_WIDTH)
    gp, ga = out_norm_pool.reshape(1, POOL_WIDTH), out_norm_attn.reshape(1, ATTN_WIDTH)

    wg1, wu1 = _copies_wait(gathers[0], True, started, "gather_wait_ffn1_up")
    h1, a1, b1, s1 = _ffn_up(x0, g1, wg1, wu1, "ffn1_up")
    (wd1,) = _copies_wait(gathers[1], True, s1, "gather_wait_ffn1_down")
    (x1,) = _ffn_down(s1, wd1, x0, None, "ffn1_down")
    (win_g,) = _copies_wait(gathers[2], True, x1, "gather_wait_w_in")
    win_cols = win_g.reshape(N_DEV, in_rows, d)[:, :w_in.shape[1]].reshape(MIX_COLS, d)
    win_t = _pad_rows(win_cols, MIX_PAD)
    hm, pv, q, k, v, f = _mix_in_fwd(x1, gm, win_t)
    pooled, mixed, y_pool = _pool_fwd(pv, pool_w, scale_row, gp, bsz, seq)
    qp, kp = _attn_prep_fwd(q, k, f, bf_row, gq, gk, bsz, seq)
    o, lse = _flash_fwd(qp, kp, v, bsz, seq)
    (wout,) = _copies_wait(gathers[3], True, o, "gather_wait_w_out")
    ycat, x2 = _mix_out_fwd(o, y_pool, x1, ga, wout)
    wg2, wu2, wd2 = _copies_wait(gathers[4], True, x2, "gather_wait_ffn2")
    h2, a2, b2, s2 = _ffn_up(x2, g2, wg2, wu2, "ffn2_up")
    dx3, dyh2, loss_part = _ffn_down(s2, wd2, x2, target, "ffn2_down")

    da2, db2 = _ffn_bwd_act(dyh2, a2, b2, wd2, "ffn2_bwd_act")
    dwg2, dwu2 = _wgrad([da2, db2], h2, "ffn2_up_wgrad")
    (dwd2,) = _wgrad([s2], dyh2, "ffn2_down_wgrad")
    (sent_ffn2,), tok = _copies_start([[dwg2, dwu2, dwd2]], False, "exchange_start_ffn2")
    dx2, dg2 = _ffn_bwd_dx(da2, db2, dx3, x2, g2 + tok[0, 0], wg2, wu2, "ffn2_bwd_dx")
    dx2b, dy_pool, do, dga = _mix_out_bwd(dx2, o, ga, wout)
    (dwout,) = _wgrad([ycat], dx2b, "w_out_wgrad")
    (sent_out,), tok = _copies_start([[dwout]], False, "exchange_start_w_out")
    dqp, dkp, dv = _flash_bwd(qp, kp, v, o, do, lse, bsz, seq)
    dq, dk, df, dgq, dgk, dbf = _attn_prep_bwd(dqp, dkp, q, k, f, bf_row + tok[0, 0], gq, gk, bsz, seq)
    dpv, dpool_w, dscale, dgp = _pool_bwd(dy_pool, mixed, pooled, pool_w, scale_row, gp, bsz, seq)
    dhcat, dx1, dyh1, dgm = _mix_in_bwd(dpv, dq, dk, dv, df, x1, dx2, gm, win_t)
    (dwin,) = _wgrad([dhcat], hm, "w_in_wgrad")
    dwin_blocks = jnp.pad(dwin[:MIX_COLS].reshape(N_DEV, w_in.shape[1], d), ((0, 0), (0, in_rows - w_in.shape[1]), (0, 0)))
    (sent_in,), tok = _copies_start([[dwin_blocks.reshape(N_DEV * in_rows, d)]], False, "exchange_start_w_in")
    (dwd1,) = _wgrad([s1], dyh1, "ffn1_down_wgrad")
    (sent_down1,), tok = _copies_start([[dwd1]], False, "exchange_start_ffn1_down", after=tok)
    da1, db1 = _ffn_bwd_act(dyh1, a1, b1, wd1, "ffn1_bwd_act")
    dwg1, dwu1 = _wgrad([da1, db1], h1, "ffn1_up_wgrad")
    (sent_up1,), tok = _copies_start([[dwg1, dwu1]], False, "exchange_start_ffn1_up", after=tok)
    dx0, dg1 = _ffn_bwd_dx(da1, db1, dx1, x0, g1 + tok[0, 0], wg1, wu1, "ffn1_bwd_dx")

    pool_rows = POOL_GROUPS * POOL_GROUP_DIM
    packed = _pack_vector_grads([dg1, dgm, dg2, dbf, dscale, dgq, dgk, dgp, dga], loss_part)
    (sent_small,), tok = _copies_start([[packed, dpool_w.reshape(pool_rows, POOL_GROUP_DIM)]], True, "small_grads_start")

    weights = dict(ffn1_norm=ffn1_norm, ffn1_w_gate=ffn1_w_gate, ffn1_w_up=ffn1_w_up, ffn1_w_down=ffn1_w_down,
                   mix_norm=mix_norm, w_in=w_in, b_forget=b_forget, pool_w=pool_w, pool_scale=pool_scale,
                   q_norm=q_norm, k_norm=k_norm, out_norm_pool=out_norm_pool, out_norm_attn=out_norm_attn,
                   w_out=w_out, ffn2_norm=ffn2_norm, ffn2_w_gate=ffn2_w_gate, ffn2_w_up=ffn2_w_up,
                   ffn2_w_down=ffn2_w_down)
    m_in = dict(ffn1_norm=m_ffn1_norm, ffn1_w_gate=m_ffn1_w_gate, ffn1_w_up=m_ffn1_w_up, ffn1_w_down=m_ffn1_w_down,
                mix_norm=m_mix_norm, w_in=m_w_in, b_forget=m_b_forget, pool_w=m_pool_w, pool_scale=m_pool_scale,
                q_norm=m_q_norm, k_norm=m_k_norm, out_norm_pool=m_out_norm_pool, out_norm_attn=m_out_norm_attn,
                w_out=m_w_out, ffn2_norm=m_ffn2_norm, ffn2_w_gate=m_ffn2_w_gate, ffn2_w_up=m_ffn2_w_up,
                ffn2_w_down=m_ffn2_w_down)
    v_in = dict(ffn1_norm=v_ffn1_norm, ffn1_w_gate=v_ffn1_w_gate, ffn1_w_up=v_ffn1_w_up, ffn1_w_down=v_ffn1_w_down,
                mix_norm=v_mix_norm, w_in=v_w_in, b_forget=v_b_forget, pool_w=v_pool_w, pool_scale=v_pool_scale,
                q_norm=v_q_norm, k_norm=v_k_norm, out_norm_pool=v_out_norm_pool, out_norm_attn=v_out_norm_attn,
                w_out=v_w_out, ffn2_norm=v_ffn2_norm, ffn2_w_gate=v_ffn2_w_gate, ffn2_w_up=v_ffn2_w_up,
                ffn2_w_down=v_ffn2_w_down)
    grads, delta, new_m, new_v = {}, {}, {}, {}
    after = [tok]
    plan = ((sent_ffn2, "ffn2", ("ffn2_w_gate", "ffn2_w_up", "ffn2_w_down")), (sent_out, "w_out", ("w_out",)),
            (sent_in, "w_in", ("w_in",)), (sent_down1, "ffn1_down", ("ffn1_w_down",)),
            (sent_up1, "ffn1_up", ("ffn1_w_gate", "ffn1_w_up")))
    for sent, tag, names in plan:
        parts = _copies_wait(sent, False, after, f"exchange_wait_{tag}")
        for n, part in zip(names, parts):
            turn = (lambda a: a.T) if n in COLUMN_SHARDED else (lambda a: a)
            done = _sum_adamw(part, turn(weights[n]), turn(m_in[n]), turn(v_in[n]), f"adamw_{n}")
            grads[n], delta[n], new_m[n], new_v[n] = (turn(a) for a in done)
        after = [new_v[n] for n in names]
    vec_all, pool_all = _copies_wait(sent_small, True, after, "small_grads_wait")
    as_row = lambda a: a.reshape(1, -1)
    as_pool = lambda a: a.reshape(pool_rows, POOL_GROUP_DIM)
    small = _small_adamw(vec_all, pool_all, [tuple(as_row(z[n]) for z in (weights, m_in, v_in)) for n in VEC_NAMES],
                         tuple(as_pool(z["pool_w"]) for z in (weights, m_in, v_in)))
    for i, n in enumerate(VEC_NAMES + ("pool_w",)):
        grads[n], delta[n], new_m[n], new_v[n] = (a.reshape(weights[n].shape) for a in small[4 * i:4 * i + 4])
    loss = small[-1].reshape(())

    order = ("ffn1_norm", "ffn1_w_gate", "ffn1_w_up", "ffn1_w_down", "mix_norm", "w_in", "b_forget", "pool_w",
             "pool_scale", "q_norm", "k_norm", "out_norm_pool", "out_norm_attn", "w_out", "ffn2_norm", "ffn2_w_gate",
             "ffn2_w_up", "ffn2_w_down")
    return (loss, dx0.reshape(bsz, seq, d), *[grads[n] for n in order], *[delta[n] for n in order],
            *[new_m[n] for n in order], *[new_v[n] for n in order])
```

```python
import functools

import jax
import jax.numpy as jnp
from jax import lax
from jax.experimental import pallas as pl
from jax.experimental.pallas import tpu as pltpu

F32 = jnp.float32
BF16 = jnp.bfloat16

EPS = 1e-6
D_MODEL = 1024
D_FF = 2816
N_HEADS = 8
HEAD_DIM = 64
POOL_WIDTH = 512
ATTN_WIDTH = 512
POOL_GROUPS = 4
POOL_GROUP_DIM = 128
POOL_WINDOWS = (2, 4, 8, 16)
POOL_HALO = 16
MIX_COLS = POOL_WIDTH + 3 * ATTN_WIDTH + N_HEADS
MIX_PAD = POOL_WIDTH + 3 * ATTN_WIDTH + 128
N_DEV = 8
BF16_ROWS = 16
LANES = 128
VMEM_LIMIT = 56 * 1024 * 1024

ADAM_LR = 0.001
ADAM_B1 = 0.9
ADAM_B2 = 0.999
ADAM_EPS = 1e-08
ADAM_WD = 0.01
ADAM_STEP = 10


def _params(*sem):
    return pltpu.CompilerParams(dimension_semantics=sem, vmem_limit_bytes=VMEM_LIMIT)


def _dot(a, b):
    return jnp.dot(a, b, preferred_element_type=F32)


def _dot_nt(a, b):
    return lax.dot_general(a, b, (((1,), (1,)), ((), ())), preferred_element_type=F32)


def _dot_tn(a, b):
    return lax.dot_general(a, b, (((0,), (0,)), ((), ())), preferred_element_type=F32)


def _resident(shape):
    return pl.BlockSpec(shape, lambda *_: (0,) * len(shape), pipeline_mode=pl.Buffered(1))


def _rows(tm, width):
    return pl.BlockSpec((tm, width), lambda i: (i, 0))


def _rms_scale(x):
    return lax.rsqrt(jnp.mean(x * x, axis=-1, keepdims=True) + EPS)


def _rms_bwd(dh, x, gain):
    r = _rms_scale(x)
    n = x * r
    dgain = jnp.sum(dh * n, axis=0, keepdims=True)
    dn = dh * gain
    dx = r * (dn - n * jnp.mean(dn * n, axis=-1, keepdims=True))
    return dx, dgain


def _split3(x):
    hi = x.astype(BF16)
    r1 = x - hi.astype(F32)
    mid = r1.astype(BF16)
    lo = (r1 - mid.astype(F32)).astype(BF16)
    return hi, mid, lo


def _split2(x):
    hi = x.astype(BF16)
    return hi, (x - hi.astype(F32)).astype(BF16)


FF_CHUNK = 256


def _ffn_up(x, gain, wg_t, wu_t, name):
    t, d = x.shape
    f = wg_t.shape[0]
    tm = 512

    def body(x_ref, g_ref, wg_ref, wu_ref, h_ref, a_ref, b_ref, s_ref):
        xv = x_ref[...]
        h = (xv * _rms_scale(xv) * g_ref[...]).astype(BF16)
        h_ref[...] = h
        for c in range(f // FF_CHUNK):
            sl = pl.ds(c * FF_CHUNK, FF_CHUNK)
            a = _dot_nt(h, wg_ref[sl, :])
            b = _dot_nt(h, wu_ref[sl, :])
            a_ref[:, sl] = a.astype(BF16)
            b_ref[:, sl] = b.astype(BF16)
            s_ref[:, sl] = (a * jax.nn.sigmoid(a) * b).astype(BF16)

    wide = jax.ShapeDtypeStruct((t, f), BF16)
    return pl.pallas_call(
        body, name=name, grid=(t // tm,),
        in_specs=[_rows(tm, d), _resident((1, d)), _resident((f, d)), _resident((f, d))],
        out_specs=[_rows(tm, d), _rows(tm, f), _rows(tm, f), _rows(tm, f)],
        out_shape=[jax.ShapeDtypeStruct((t, d), BF16), wide, wide, wide],
        compiler_params=_params("arbitrary"),
    )(x, gain, wg_t, wu_t)


def _ffn_down(s, wd, x, target, name):
    t, d = x.shape
    f = wd.shape[0]
    tm = 512
    with_loss = target is not None

    def body(*refs):
        if with_loss:
            s_ref, w_ref, x_ref, t_ref, dy_ref, dyh_ref, loss_ref = refs
        else:
            s_ref, w_ref, x_ref, y_ref = refs
        y = x_ref[...] + 0.5 * _dot(s_ref[...], w_ref[...])
        if with_loss:
            e = y - t_ref[...]
            dy = e * (1.0 / d)
            dy_ref[...] = dy
            dyh_ref[...] = (0.5 * dy).astype(BF16)

            @pl.when(pl.program_id(0) == 0)
            def _():
                loss_ref[...] = jnp.zeros_like(loss_ref)

            part = jnp.sum(jnp.sum(e * e, axis=0, keepdims=True), axis=1, keepdims=True)
            loss_ref[...] += part * (0.5 / d)
        else:
            y_ref[...] = y

    in_specs = [_rows(tm, f), _resident((f, d)), _rows(tm, d)]
    args = [s, wd, x]
    if with_loss:
        in_specs.append(_rows(tm, d))
        args.append(target)
        out_shape = [jax.ShapeDtypeStruct((t, d), F32), jax.ShapeDtypeStruct((t, d), BF16),
                     jax.ShapeDtypeStruct((1, 1), F32)]
        out_specs = [_rows(tm, d), _rows(tm, d), pl.BlockSpec((1, 1), lambda i: (0, 0))]
    else:
        out_shape = [jax.ShapeDtypeStruct((t, d), F32)]
        out_specs = [_rows(tm, d)]
    return pl.pallas_call(
        body, name=name, grid=(t // tm,), in_specs=in_specs, out_specs=out_specs, out_shape=out_shape,
        compiler_params=_params("arbitrary"),
    )(*args)


def _ffn_bwd_act(dyh, a, b, wd, name):
    t, d = dyh.shape
    f = wd.shape[0]
    tm = 512

    def body(dy_ref, a_ref, b_ref, wd_ref, da_ref, db_ref):
        dyh_v = dy_ref[...]
        for c in range(f // FF_CHUNK):
            sl = pl.ds(c * FF_CHUNK, FF_CHUNK)
            ds = _dot_nt(dyh_v, wd_ref[sl, :])
            av = a_ref[:, sl].astype(F32)
            bv = b_ref[:, sl].astype(F32)
            sig = jax.nn.sigmoid(av)
            da_ref[:, sl] = (ds * bv * (sig * (1.0 + av * (1.0 - sig)))).astype(BF16)
            db_ref[:, sl] = (ds * (av * sig)).astype(BF16)

    wide = jax.ShapeDtypeStruct((t, f), BF16)
    return pl.pallas_call(
        body, name=name, grid=(t // tm,),
        in_specs=[_rows(tm, d), _rows(tm, f), _rows(tm, f), _resident((f, d))],
        out_specs=[_rows(tm, f), _rows(tm, f)], out_shape=[wide, wide],
        compiler_params=_params("arbitrary"),
    )(dyh, a, b, wd)


def _ffn_bwd_dx(da, db, dy, x, gain, wg_t, wu_t, name):
    t, d = x.shape
    f = wg_t.shape[0]
    tm = 512

    def body(da_ref, db_ref, dy_ref, x_ref, g_ref, wg_ref, wu_ref, dx_ref, dg_ref):
        dh = _dot(da_ref[...], wg_ref[...]) + _dot(db_ref[...], wu_ref[...])
        dx, dgain = _rms_bwd(dh, x_ref[...], g_ref[...])
        dx_ref[...] = dy_ref[...] + dx

        @pl.when(pl.program_id(0) == 0)
        def _():
            dg_ref[...] = jnp.zeros_like(dg_ref)

        dg_ref[...] += dgain

    return pl.pallas_call(
        body, name=name, grid=(t // tm,),
        in_specs=[_rows(tm, f), _rows(tm, f), _rows(tm, d), _rows(tm, d), _resident((1, d)), _resident((f, d)),
                  _resident((f, d))],
        out_specs=[_rows(tm, d), pl.BlockSpec((1, d), lambda i: (0, 0))],
        out_shape=[jax.ShapeDtypeStruct((t, d), F32), jax.ShapeDtypeStruct((1, d), F32)],
        compiler_params=_params("arbitrary"),
    )(da, db, dy, x, gain, wg_t, wu_t)


def _wgrad(lhs, b, name):
    t, n = lhs[0].shape
    d = b.shape[1]
    m = len(lhs)
    tn = n // 2 if n * d * m > (4 << 20) else n
    tk = 1024
    nk = t // tk

    def body(*refs):
        a_refs, b_ref, o_refs, accs = refs[:m], refs[m], refs[m + 1:2 * m + 1], refs[2 * m + 1:]
        k = pl.program_id(1)

        @pl.when(k == 0)
        def _():
            for acc in accs:
                acc[...] = jnp.zeros_like(acc)

        bv = b_ref[...]
        for a_ref, acc in zip(a_refs, accs):
            acc[...] += _dot_tn(a_ref[...], bv)

        @pl.when(k == nk - 1)
        def _():
            for o_ref, acc in zip(o_refs, accs):
                o_ref[...] = acc[...].astype(BF16)

    return pl.pallas_call(
        body, name=name, grid=(n // tn, nk),
        in_specs=[pl.BlockSpec((tk, tn), lambda j, k: (k, j))] * m + [pl.BlockSpec((tk, d), lambda j, k: (k, 0))],
        out_specs=[pl.BlockSpec((tn, d), lambda j, k: (j, 0))] * m,
        out_shape=[jax.ShapeDtypeStruct((n, d), BF16)] * m,
        scratch_shapes=[pltpu.VMEM((tn, d), F32)] * m,
        compiler_params=_params("arbitrary", "arbitrary"),
    )(*lhs, b)


def _mix_in_fwd(x, gain, w_in_t):
    t, d = x.shape
    tm = 1024
    pw, aw = POOL_WIDTH, ATTN_WIDTH

    def body(x_ref, g_ref, w_ref, hm_ref, pv_ref, q_ref, k_ref, v_ref, f_ref):
        xv = x_ref[...]
        hm = (xv * _rms_scale(xv) * g_ref[...]).astype(BF16)
        hm_ref[...] = hm
        pv_ref[...] = _dot_nt(hm, w_ref[pl.ds(0, pw), :])
        q_ref[...] = _dot_nt(hm, w_ref[pl.ds(pw, aw), :])
        k_ref[...] = _dot_nt(hm, w_ref[pl.ds(pw + aw, aw), :])
        v_ref[...] = _dot_nt(hm, w_ref[pl.ds(pw + 2 * aw, aw), :]).astype(BF16)
        f_ref[...] = _dot_nt(hm, w_ref[pl.ds(pw + 3 * aw, LANES), :])

    return pl.pallas_call(
        body, name="mix_in_fwd", grid=(t // tm,),
        in_specs=[_rows(tm, d), _resident((1, d)), _resident((MIX_PAD, d))],
        out_specs=[_rows(tm, d), _rows(tm, pw), _rows(tm, aw), _rows(tm, aw), _rows(tm, aw), _rows(tm, LANES)],
        out_shape=[jax.ShapeDtypeStruct((t, d), BF16), jax.ShapeDtypeStruct((t, pw), F32),
                   jax.ShapeDtypeStruct((t, aw), F32), jax.ShapeDtypeStruct((t, aw), F32),
                   jax.ShapeDtypeStruct((t, aw), BF16), jax.ShapeDtypeStruct((t, LANES), F32)],
        compiler_params=_params("arbitrary"),
    )(x, gain, w_in_t)


def _pool_fwd(pv, pool_w, pool_scale, gain, bsz, seq):
    ts = 512
    ns = seq // ts
    pw = POOL_WIDTH

    def body(pv_ref, w_ref, sc_ref, g_ref, pooled_ref, mixed_ref, y_ref, ext):
        s = pl.program_id(1)

        @pl.when(s == 0)
        def _():
            ext[pl.ds(0, POOL_HALO), :] = jnp.zeros((POOL_HALO, pw), F32)

        p = pv_ref[...]
        ext[pl.ds(POOL_HALO, ts), :] = p
        pos = s * ts + lax.broadcasted_iota(jnp.int32, (ts, 1), 0)
        parts = []
        for g, w in enumerate(POOL_WINDOWS):
            lanes = pl.ds(g * POOL_GROUP_DIM, POOL_GROUP_DIM)
            win = ext[pl.ds(POOL_HALO, ts), lanes]
            for i in range(1, w):
                win = win + ext[pl.ds(POOL_HALO - i, ts), lanes]
            cnt = jnp.minimum(pos + 1, w).astype(F32)
            pooled = (win / cnt - ext[pl.ds(POOL_HALO, ts), lanes]).astype(BF16)
            pooled_ref[:, lanes] = pooled
            parts.append(_dot(pooled, w_ref[g].astype(BF16)))
        mixed = jnp.concatenate(parts, axis=1)
        mixed_ref[...] = mixed
        pm = mixed * sc_ref[...]
        y_ref[...] = (pm * _rms_scale(pm) * g_ref[...]).astype(BF16)
        ext[pl.ds(0, POOL_HALO), :] = p[ts - POOL_HALO:, :]

    blk = pl.BlockSpec((ts, pw), lambda b, s: (b * ns + s, 0))
    t = bsz * seq
    return pl.pallas_call(
        body, name="pool_fwd", grid=(bsz, ns),
        in_specs=[blk, pl.BlockSpec((POOL_GROUPS, POOL_GROUP_DIM, POOL_GROUP_DIM), lambda b, s: (0, 0, 0)),
                  pl.BlockSpec((1, pw), lambda b, s: (0, 0)), pl.BlockSpec((1, pw), lambda b, s: (0, 0))],
        out_specs=[blk, blk, blk],
        out_shape=[jax.ShapeDtypeStruct((t, pw), BF16), jax.ShapeDtypeStruct((t, pw), F32),
                   jax.ShapeDtypeStruct((t, pw), BF16)],
        scratch_shapes=[pltpu.VMEM((POOL_HALO + ts, pw), F32)],
        compiler_params=_params("arbitrary", "arbitrary"),
    )(pv, pool_w, pool_scale, gain)


def _pool_bwd(dy, mixed, pooled, pool_w, pool_scale, gain, bsz, seq):
    ts = 512
    ns = seq // ts
    pw = POOL_WIDTH

    def body(dy_ref, mixed_ref, pooled_ref, w_ref, sc_ref, g_ref, dpv_ref, dw_ref, dsc_ref, dg_ref, ext):
        b = pl.program_id(0)
        sr = pl.program_id(1)
        s = ns - 1 - sr

        @pl.when(jnp.logical_and(b == 0, sr == 0))
        def _():
            dw_ref[...] = jnp.zeros_like(dw_ref)
            dsc_ref[...] = jnp.zeros_like(dsc_ref)
            dg_ref[...] = jnp.zeros_like(dg_ref)

        @pl.when(sr == 0)
        def _():
            ext[pl.ds(ts, POOL_HALO), :] = jnp.zeros((POOL_HALO, pw), F32)

        mixed = mixed_ref[...]
        sc = sc_ref[...]
        dpm, dgain = _rms_bwd(dy_ref[...], mixed * sc, g_ref[...])
        dg_ref[...] += dgain
        dsc_ref[...] += jnp.sum(dpm * mixed, axis=0, keepdims=True)
        dmixed = (dpm * sc).astype(BF16)
        pos = s * ts + lax.broadcasted_iota(jnp.int32, (ts, 1), 0)
        dpooled = []
        for g, w in enumerate(POOL_WINDOWS):
            lanes = pl.ds(g * POOL_GROUP_DIM, POOL_GROUP_DIM)
            dm = dmixed[:, g * POOL_GROUP_DIM:(g + 1) * POOL_GROUP_DIM]
            dw_ref[g] += _dot_tn(pooled_ref[:, lanes], dm)
            dp = _dot_nt(dm, w_ref[g].astype(BF16))
            dpooled.append(dp)
            cnt = jnp.minimum(pos + 1, w).astype(F32)
            ext[pl.ds(0, ts), lanes] = dp / cnt
        for g, w in enumerate(POOL_WINDOWS):
            lanes = pl.ds(g * POOL_GROUP_DIM, POOL_GROUP_DIM)
            win = ext[pl.ds(0, ts), lanes]
            for i in range(1, w):
                win = win + ext[pl.ds(i, ts), lanes]
            dpv_ref[:, lanes] = (win - dpooled[g]).astype(BF16)
        head = ext[pl.ds(0, POOL_HALO), :]
        ext[pl.ds(ts, POOL_HALO), :] = head

    blk = pl.BlockSpec((ts, pw), lambda b, s: (b * ns + (ns - 1 - s), 0))
    vec = pl.BlockSpec((1, pw), lambda b, s: (0, 0))
    wspec = pl.BlockSpec((POOL_GROUPS, POOL_GROUP_DIM, POOL_GROUP_DIM), lambda b, s: (0, 0, 0))
    t = bsz * seq
    return pl.pallas_call(
        body, name="pool_bwd", grid=(bsz, ns),
        in_specs=[blk, blk, blk, wspec, vec, vec],
        out_specs=[blk, wspec, vec, vec],
        out_shape=[jax.ShapeDtypeStruct((t, pw), BF16),
                   jax.ShapeDtypeStruct((POOL_GROUPS, POOL_GROUP_DIM, POOL_GROUP_DIM), F32),
                   jax.ShapeDtypeStruct((1, pw), F32), jax.ShapeDtypeStruct((1, pw), F32)],
        scratch_shapes=[pltpu.VMEM((ts + POOL_HALO, pw), F32)],
        compiler_params=_params("arbitrary", "arbitrary"),
    )(dy, mixed, pooled, pool_w, pool_scale, gain)


AUX_ONE = 64
AUX_F = 67

ATTN_PREP_ROWS = 512


def _seg_ones(width, seg):
    r = lax.broadcasted_iota(jnp.int32, (width, width), 0) // seg
    c = lax.broadcasted_iota(jnp.int32, (width, width), 1) // seg
    return (r == c).astype(BF16)


def _tri_ones(n, lower):
    r = lax.broadcasted_iota(jnp.int32, (n, n), 0)
    c = lax.broadcasted_iota(jnp.int32, (n, n), 1)
    return ((r >= c) if lower else (r <= c)).astype(BF16)


def _place_pieces(first_lane):
    r = lax.broadcasted_iota(jnp.int32, (3 * LANES, N_HEADS * LANES), 0)
    c = lax.broadcasted_iota(jnp.int32, (3 * LANES, N_HEADS * LANES), 1)
    piece, head = r // LANES, r % LANES
    return jnp.logical_and(head < N_HEADS, c == head * LANES + first_lane + piece).astype(BF16)


def _head_sums(x, seg_ones):
    hi, lo = _split2(x)
    return _dot(hi, seg_ones) + _dot(lo, seg_ones)


def _log_sigmoid(x):
    return jnp.minimum(x, 0.0) - jnp.log(1.0 + jnp.exp(-jnp.abs(x)))


def _attn_prep_fwd(q, k, f, b_forget, q_gain, k_gain, bsz, seq):
    ts = ATTN_PREP_ROWS
    ns = seq // ts
    aw = ATTN_WIDTH
    t = bsz * seq
    seg = _seg_ones(aw, HEAD_DIM)
    tri = _tri_ones(ts, True)

    def body(q_ref, k_ref, f_ref, bf_ref, gq_ref, gk_ref, seg_ref, tri_ref, pq_ref, pk_ref, qp_ref, kp_ref, carry):
        s = pl.program_id(1)

        @pl.when(s == 0)
        def _():
            carry[...] = jnp.zeros_like(carry)

        logf = _log_sigmoid(f_ref[...] + bf_ref[...])
        hi, mid, lo = _split3(logf)
        tri_v = tri_ref[...]
        fc = _dot(tri_v, hi) + _dot(tri_v, mid) + _dot(tri_v, lo) + carry[pl.ds(0, 1), :]
        carry[pl.ds(0, 1), :] = fc[ts - 1:, :]
        pcs = jnp.concatenate(_split3(fc), axis=1)
        lane = lax.broadcasted_iota(jnp.int32, (1, LANES), 1)
        ones_q = jnp.logical_and(lane >= AUX_ONE, lane < AUX_ONE + 3).astype(F32)
        ones_k = jnp.logical_and(lane >= AUX_F, lane < AUX_F + 3).astype(F32)
        seg_v = seg_ref[...]

        def build(x_ref, g_ref, scale, out_ref, ones, place_ref, f_sign):
            xv = x_ref[...]
            r = lax.rsqrt(_head_sums(xv * xv, seg_v) * (1.0 / HEAD_DIM) + EPS)
            xn = xv * r * g_ref[...] * scale
            aux = _dot(pcs, place_ref[...]) * f_sign
            for h in range(N_HEADS):
                pair = xn[:, (h // 2) * LANES:(h // 2 + 1) * LANES]
                feat = pair if h % 2 == 0 else pltpu.roll(pair, HEAD_DIM, 1)
                aux_h = aux[:, h * LANES:(h + 1) * LANES] + ones
                out_ref[:, h * LANES:(h + 1) * LANES] = jnp.where(lane < HEAD_DIM, feat, aux_h).astype(BF16)

        build(q_ref, gq_ref, 0.125, qp_ref, ones_q, pq_ref, 1.0)
        build(k_ref, gk_ref, 1.0, kp_ref, ones_k, pk_ref, -1.0)

    blk = pl.BlockSpec((ts, aw), lambda b, s: (b * ns + s, 0))
    fblk = pl.BlockSpec((ts, LANES), lambda b, s: (b * ns + s, 0))
    oblk = pl.BlockSpec((ts, N_HEADS * LANES), lambda b, s: (b * ns + s, 0))
    const = lambda shape: pl.BlockSpec(shape, lambda b, s: (0, 0))
    return pl.pallas_call(
        body, name="attn_prep_fwd", grid=(bsz, ns),
        in_specs=[blk, blk, fblk, const((1, LANES)), const((1, aw)), const((1, aw)), const((aw, aw)), const((ts, ts)),
                  const((3 * LANES, N_HEADS * LANES)), const((3 * LANES, N_HEADS * LANES))],
        out_specs=[oblk, oblk],
        out_shape=[jax.ShapeDtypeStruct((t, N_HEADS * LANES), BF16)] * 2,
        scratch_shapes=[pltpu.VMEM((8, LANES), F32)],
        compiler_params=_params("arbitrary", "arbitrary"),
    )(q, k, f, b_forget, q_gain, k_gain, seg, tri, _place_pieces(AUX_F), _place_pieces(AUX_ONE))


def _attn_prep_bwd(dqp, dkp, q, k, f, b_forget, q_gain, k_gain, bsz, seq):
    ts = ATTN_PREP_ROWS
    ns = seq // ts
    aw = ATTN_WIDTH
    t = bsz * seq
    seg = _seg_ones(aw, HEAD_DIM)
    tri = _tri_ones(ts, False)

    def body(dqp_ref, dkp_ref, q_ref, k_ref, f_ref, bf_ref, gq_ref, gk_ref, seg_ref, tri_ref,
             dq_ref, dk_ref, df_ref, dgq_ref, dgk_ref, dbf_ref, carry):
        b = pl.program_id(0)
        sr = pl.program_id(1)

        @pl.when(jnp.logical_and(b == 0, sr == 0))
        def _():
            dgq_ref[...] = jnp.zeros_like(dgq_ref)
            dgk_ref[...] = jnp.zeros_like(dgk_ref)
            dbf_ref[...] = jnp.zeros_like(dbf_ref)

        @pl.when(sr == 0)
        def _():
            carry[...] = jnp.zeros_like(carry)

        lane = lax.broadcasted_iota(jnp.int32, (1, LANES), 1)
        seg_v = seg_ref[...]

        def norm_bwd(dp_ref, x_ref, g_ref, scale, dx_ref, dgain_ref):
            parts = []
            for j in range(N_HEADS // 2):
                even = dp_ref[:, (2 * j) * LANES:(2 * j + 1) * LANES]
                odd = dp_ref[:, (2 * j + 1) * LANES:(2 * j + 2) * LANES]
                parts.append(jnp.where(lane < HEAD_DIM, even, pltpu.roll(odd, HEAD_DIM, 1)))
            dxn = jnp.concatenate(parts, axis=1) * scale
            xv = x_ref[...]
            r = lax.rsqrt(_head_sums(xv * xv, seg_v) * (1.0 / HEAD_DIM) + EPS)
            n = xv * r
            dgain_ref[...] += jnp.sum(dxn * n, axis=0, keepdims=True)
            dn = dxn * g_ref[...]
            m = _head_sums(dn * n, seg_v) * (1.0 / HEAD_DIM)
            dx_ref[...] = (r * (dn - n * m)).astype(BF16)

        norm_bwd(dqp_ref, q_ref, gq_ref, 0.125, dq_ref, dgq_ref)
        norm_bwd(dkp_ref, k_ref, gk_ref, 1.0, dk_ref, dgk_ref)

        dfc = jnp.zeros((ts, LANES), F32)
        for h in range(N_HEADS):
            cols = pl.ds(h * LANES, LANES)
            both = jnp.where(lane == AUX_F, dqp_ref[:, cols], 0.0) - jnp.where(lane == AUX_ONE, dkp_ref[:, cols], 0.0)
            dfc = jnp.where(lane == h, jnp.sum(both, axis=1, keepdims=True), dfc)
        hi, mid, lo = _split3(dfc)
        tri_v = tri_ref[...]
        dlogf = _dot(tri_v, hi) + _dot(tri_v, mid) + _dot(tri_v, lo) + carry[pl.ds(0, 1), :]
        carry[pl.ds(0, 1), :] = dlogf[0:1, :]
        df = jnp.where(lane < N_HEADS, dlogf * jax.nn.sigmoid(-(f_ref[...] + bf_ref[...])), 0.0)
        df_ref[...] = df.astype(BF16)
        dbf_ref[...] += jnp.sum(df, axis=0, keepdims=True)

    rev = lambda b, s: (b * ns + (ns - 1 - s), 0)
    blk = pl.BlockSpec((ts, aw), rev)
    fblk = pl.BlockSpec((ts, LANES), rev)
    pblk = pl.BlockSpec((ts, N_HEADS * LANES), rev)
    const = lambda shape: pl.BlockSpec(shape, lambda b, s: (0, 0))
    return pl.pallas_call(
        body, name="attn_prep_bwd", grid=(bsz, ns),
        in_specs=[pblk, pblk, blk, blk, fblk, const((1, LANES)), const((1, aw)), const((1, aw)), const((aw, aw)),
                  const((ts, ts))],
        out_specs=[blk, blk, fblk, const((1, aw)), const((1, aw)), const((1, LANES))],
        out_shape=[jax.ShapeDtypeStruct((t, aw), BF16), jax.ShapeDtypeStruct((t, aw), BF16),
                   jax.ShapeDtypeStruct((t, LANES), BF16), jax.ShapeDtypeStruct((1, aw), F32),
                   jax.ShapeDtypeStruct((1, aw), F32), jax.ShapeDtypeStruct((1, LANES), F32)],
        scratch_shapes=[pltpu.VMEM((8, LANES), F32)],
        compiler_params=_params("arbitrary", "arbitrary"),
    )(dqp, dkp, q, k, f, b_forget, q_gain, k_gain, seg, tri)


ATTN_BLOCK = 512
HEAD_PAIRS = N_HEADS // 2


def _flash_fwd(qp, kp, v, bsz, seq):
    tq = ATTN_BLOCK
    nq = seq // tq
    t = bsz * seq

    def body(q_ref, k_ref, v_ref, o_ref, lse_ref, m_sc, l_sc, acc_sc):
        i = pl.program_id(2)
        m_sc[...] = jnp.full(m_sc.shape, -jnp.inf, F32)
        l_sc[...] = jnp.zeros_like(l_sc)
        acc_sc[...] = jnp.zeros_like(acc_sc)
        lane = lax.broadcasted_iota(jnp.int32, (1, LANES), 1)
        low = lane < HEAD_DIM

        def key_block(j, masked):
            rows = pl.ds(pl.multiple_of(j * tq, tq), tq)
            vv = v_ref[rows, :]
            for h in range(2):
                mine = low if h == 0 else jnp.logical_not(low)
                s = _dot_nt(q_ref[:, h * LANES:(h + 1) * LANES], k_ref[rows, pl.ds(h * LANES, LANES)])
                if masked:
                    row = lax.broadcasted_iota(jnp.int32, (tq, tq), 0)
                    col = lax.broadcasted_iota(jnp.int32, (tq, tq), 1)
                    s = jnp.where(row >= col, s, -jnp.inf)
                m_prev = m_sc[h]
                m_new = jnp.maximum(m_prev, jnp.max(s, axis=1, keepdims=True))
                p = jnp.exp(s - jnp.tile(m_new, (1, tq // LANES)))
                alpha = jnp.exp(m_prev - m_new)
                l_sc[h] = alpha * l_sc[h] + jnp.sum(p, axis=1, keepdims=True)
                m_sc[h] = m_new
                pv = _dot(p.astype(BF16), jnp.where(mine, vv, jnp.zeros_like(vv)))
                acc_sc[...] = acc_sc[...] * jnp.where(mine, alpha, 1.0) + pv

        def below_diagonal(j, carry):
            key_block(j, False)
            return carry

        lax.fori_loop(0, i, below_diagonal, 0)
        key_block(i, True)
        l = jnp.where(low, l_sc[0], l_sc[1])
        m = jnp.where(low, m_sc[0], m_sc[1])
        o_ref[...] = acc_sc[...] / l
        lse_ref[...] = m + jnp.log(l)

    qspec = pl.BlockSpec((tq, 2 * LANES), lambda b, hp, i: (b * nq + i, hp))
    kspec = pl.BlockSpec((seq, 2 * LANES), lambda b, hp, i: (b, hp))
    vspec = pl.BlockSpec((seq, LANES), lambda b, hp, i: (b, hp))
    ospec = pl.BlockSpec((tq, LANES), lambda b, hp, i: (b * nq + i, hp))
    return pl.pallas_call(
        body, name="flash_fwd", grid=(bsz, HEAD_PAIRS, nq),
        in_specs=[qspec, kspec, vspec], out_specs=[ospec, ospec],
        out_shape=[jax.ShapeDtypeStruct((t, ATTN_WIDTH), F32), jax.ShapeDtypeStruct((t, ATTN_WIDTH), F32)],
        scratch_shapes=[pltpu.VMEM((2, tq, LANES), F32), pltpu.VMEM((2, tq, LANES), F32), pltpu.VMEM((tq, LANES), F32)],
        compiler_params=_params("arbitrary", "arbitrary", "arbitrary"),
    )(qp, kp, v)


def _flash_bwd(qp, kp, v, o, do, lse, bsz, seq):
    tq = ATTN_BLOCK
    nq = seq // tq
    t = bsz * seq

    def body(q_ref, k_ref, v_ref, o_ref, do_ref, lse_ref, dq_ref, dk_ref, dv_ref, dk_acc, dv_acc):
        j = pl.program_id(2)

        @pl.when(j == 0)
        def _():
            dq_ref[...] = jnp.zeros_like(dq_ref)

        dk_acc[...] = jnp.zeros_like(dk_acc)
        dv_acc[...] = jnp.zeros_like(dv_acc)
        lane = lax.broadcasted_iota(jnp.int32, (1, LANES), 1)
        low = lane < HEAD_DIM

        def query_block(i, masked):
            rows = pl.ds(pl.multiple_of(i * tq, tq), tq)
            dov = do_ref[rows, :]
            dd = dov * o_ref[rows, :]
            dob = dov.astype(BF16)
            vv = v_ref[...]
            lse_v = lse_ref[rows, :]
            for h in range(2):
                mine = low if h == 0 else jnp.logical_not(low)
                cols = pl.ds(h * LANES, LANES)
                qh = q_ref[rows, cols]
                kh = k_ref[:, cols]
                s = _dot_nt(qh, kh)
                lse_h = jnp.where(mine, lse_v, pltpu.roll(lse_v, HEAD_DIM, 1))
                p = jnp.exp(s - jnp.tile(lse_h, (1, tq // LANES)))
                if masked:
                    row = lax.broadcasted_iota(jnp.int32, (tq, tq), 0)
                    col = lax.broadcasted_iota(jnp.int32, (tq, tq), 1)
                    p = jnp.where(row >= col, p, 0.0)
                delta = jnp.sum(jnp.where(mine, dd, 0.0), axis=1, keepdims=True)
                dp = _dot_nt(dob, jnp.where(mine, vv, jnp.zeros_like(vv)))
                ds = (p * (dp - delta)).astype(BF16)
                dv_acc[...] += jnp.where(mine, _dot_tn(p.astype(BF16), dob), 0.0)
                dk_acc[:, cols] += _dot_tn(ds, qh)
                dq_ref[rows, cols] += _dot(ds, kh)

        def above_diagonal(i, carry):
            query_block(i, False)
            return carry

        query_block(j, True)
        lax.fori_loop(j + 1, nq, above_diagonal, 0)
        dk_ref[...] = dk_acc[...]
        dv_ref[...] = dv_acc[...].astype(BF16)

    qspec = pl.BlockSpec((seq, 2 * LANES), lambda b, hp, j: (b, hp))
    kspec = pl.BlockSpec((tq, 2 * LANES), lambda b, hp, j: (b * nq + j, hp))
    vspec = pl.BlockSpec((tq, LANES), lambda b, hp, j: (b * nq + j, hp))
    ospec = pl.BlockSpec((seq, LANES), lambda b, hp, j: (b, hp))
    return pl.pallas_call(
        body, name="flash_bwd", grid=(bsz, HEAD_PAIRS, nq),
        in_specs=[qspec, kspec, vspec, ospec, ospec, ospec], out_specs=[qspec, kspec, vspec],
        out_shape=[jax.ShapeDtypeStruct((t, N_HEADS * LANES), F32), jax.ShapeDtypeStruct((t, N_HEADS * LANES), F32),
                   jax.ShapeDtypeStruct((t, ATTN_WIDTH), BF16)],
        scratch_shapes=[pltpu.VMEM((tq, 2 * LANES), F32), pltpu.VMEM((tq, LANES), F32)],
        compiler_params=_params("arbitrary", "arbitrary", "arbitrary"),
    )(qp, kp, v, o, do, lse)


def _mix_out_fwd(o, y_pool, x, gain, w_out):
    t, d = x.shape
    tm = 1024
    pw, aw = POOL_WIDTH, ATTN_WIDTH

    def body(o_ref, yp_ref, x_ref, g_ref, w_ref, ycat_ref, y_ref):
        ov = o_ref[...]
        ya = (ov * _rms_scale(ov) * g_ref[...]).astype(BF16)
        ycat = jnp.concatenate([yp_ref[...], ya], axis=1)
        ycat_ref[...] = ycat
        y_ref[...] = x_ref[...] + _dot(ycat, w_ref[...])

    return pl.pallas_call(
        body, name="mix_out_fwd", grid=(t // tm,),
        in_specs=[_rows(tm, aw), _rows(tm, pw), _rows(tm, d), _resident((1, aw)), _resident((pw + aw, d))],
        out_specs=[_rows(tm, pw + aw), _rows(tm, d)],
        out_shape=[jax.ShapeDtypeStruct((t, pw + aw), BF16), jax.ShapeDtypeStruct((t, d), F32)],
        compiler_params=_params("arbitrary"),
    )(o, y_pool, x, gain, w_out)


def _mix_out_bwd(dx, o, gain, w_out):
    t, d = dx.shape
    tm = 1024
    pw, aw = POOL_WIDTH, ATTN_WIDTH

    def body(dx_ref, o_ref, g_ref, w_ref, dxb_ref, dyp_ref, do_ref, dg_ref):
        dxb = dx_ref[...].astype(BF16)
        dxb_ref[...] = dxb
        dyp_ref[...] = _dot_nt(dxb, w_ref[pl.ds(0, pw), :])
        dya = _dot_nt(dxb, w_ref[pl.ds(pw, aw), :])
        do, dgain = _rms_bwd(dya, o_ref[...], g_ref[...])
        do_ref[...] = do

        @pl.when(pl.program_id(0) == 0)
        def _():
            dg_ref[...] = jnp.zeros_like(dg_ref)

        dg_ref[...] += dgain

    return pl.pallas_call(
        body, name="mix_out_bwd", grid=(t // tm,),
        in_specs=[_rows(tm, d), _rows(tm, aw), _resident((1, aw)), _resident((pw + aw, d))],
        out_specs=[_rows(tm, d), _rows(tm, pw), _rows(tm, aw), pl.BlockSpec((1, aw), lambda i: (0, 0))],
        out_shape=[jax.ShapeDtypeStruct((t, d), BF16), jax.ShapeDtypeStruct((t, pw), F32),
                   jax.ShapeDtypeStruct((t, aw), F32), jax.ShapeDtypeStruct((1, aw), F32)],
        compiler_params=_params("arbitrary"),
    )(dx, o, gain, w_out)


def _mix_in_bwd(dpv, dq, dk, dv, df, x, dx_res, gain, w_in_t):
    t, d = x.shape
    tm = 512
    pw, aw = POOL_WIDTH, ATTN_WIDTH

    def body(dpv_ref, dq_ref, dk_ref, dv_ref, df_ref, x_ref, dxr_ref, g_ref, w_ref, dh_ref, dx_ref, dxh_ref, dg_ref):
        dh = jnp.concatenate([dpv_ref[...], dq_ref[...], dk_ref[...], dv_ref[...], df_ref[...]], axis=1)
        dh_ref[...] = dh
        dhm = _dot(dh, w_ref[...])
        dx, dgain = _rms_bwd(dhm, x_ref[...], g_ref[...])
        dx = dxr_ref[...] + dx
        dx_ref[...] = dx
        dxh_ref[...] = (0.5 * dx).astype(BF16)

        @pl.when(pl.program_id(0) == 0)
        def _():
            dg_ref[...] = jnp.zeros_like(dg_ref)

        dg_ref[...] += dgain

    return pl.pallas_call(
        body, name="mix_in_bwd", grid=(t // tm,),
        in_specs=[_rows(tm, pw), _rows(tm, aw), _rows(tm, aw), _rows(tm, aw), _rows(tm, LANES), _rows(tm, d),
                  _rows(tm, d), _resident((1, d)), _resident((MIX_PAD, d))],
        out_specs=[_rows(tm, MIX_PAD), _rows(tm, d), _rows(tm, d), pl.BlockSpec((1, d), lambda i: (0, 0))],
        out_shape=[jax.ShapeDtypeStruct((t, MIX_PAD), BF16), jax.ShapeDtypeStruct((t, d), F32),
                   jax.ShapeDtypeStruct((t, d), BF16), jax.ShapeDtypeStruct((1, d), F32)],
        compiler_params=_params("arbitrary"),
    )(dpv, dq, dk, dv, df, x, dx_res, gain, w_in_t)


MESH_IDS = pl.DeviceIdType.MESH


def _me():
    return lax.axis_index("x"), lax.axis_index("y"), lax.axis_index("c")


def _peer(x, y, c, p):
    px = 1 - x if p & 4 else x
    py = 1 - y if p & 2 else y
    pc = 1 - c if p & 1 else c
    return (px, py, pc), 4 * px + 2 * py + pc


HBM_SPEC = pl.BlockSpec(memory_space=pltpu.HBM)
SEM_SPEC = pl.BlockSpec(memory_space=pltpu.SEMAPHORE)
SPLIT_COPY = pltpu.CompilerParams(has_side_effects=pltpu.SideEffectType.DATAFLOW_SIDE_EFFECTING)
PEERS = N_DEV - 1


def _hbm(a):
    return pltpu.with_memory_space_constraint(a, pltpu.HBM)


def _row_block(ref, dev, rows):
    return ref.at[pl.ds(pl.multiple_of(dev * rows, BF16_ROWS), rows)]


def _copy_ends(gather, src, land, me, peer_id):
    if gather:
        rows = src.shape[0]
        return src, _row_block(land, me, rows), _row_block(land, peer_id, rows), src, _row_block(land, me, rows)
    rows = src.shape[0] // N_DEV
    return (_row_block(src, peer_id, rows), land.at[me], land.at[peer_id], _row_block(src, me, rows), land.at[me])


def _land_shape(gather, s):
    return (N_DEV * s.shape[0], s.shape[1]) if gather else (N_DEV, s.shape[0] // N_DEV, s.shape[1])


SIBLING = 1
SAME_CORE_PEERS = (2, 4, 6)
RELAYS = len(SAME_CORE_PEERS)


def _copies_start(groups, gather, name, after=None, relayed=()):
    flat = [s for g in groups for s in g]
    n, ng = len(flat), len(groups)
    lands = [lax.empty(_land_shape(gather, s), s.dtype) for s in flat]
    n_in = 2 * n + (after is not None)

    def body(*refs):
        ins, lnd = refs[:n], refs[n:2 * n]
        sems = refs[n_in:n_in + 2 * ng]
        token = refs[-1]
        x, y, c = _me()
        me = 4 * x + 2 * y + c
        w = 0
        for gi, g in enumerate(groups):
            for k in range(len(g)):
                for p in ((SIBLING,) + SAME_CORE_PEERS if gi in relayed else range(1, N_DEV)):
                    peer, peer_id = _peer(x, y, c, p)
                    src, dst, _, _, _ = _copy_ends(gather, ins[w], lnd[w], me, peer_id)
                    pltpu.make_async_remote_copy(src, dst, sems[2 * gi].at[k * PEERS + p - 1],
                                                 sems[2 * gi + 1].at[k * PEERS + p - 1], device_id=peer,
                                                 device_id_type=MESH_IDS).start()
                w += 1
        token[...] = jnp.zeros_like(token)

    sem_shapes = []
    for g in groups:
        sem_shapes += [pltpu.SemaphoreType.DMA((len(g) * PEERS,))] * 2
    out = pl.pallas_call(
        body, name=name,
        out_shape=(*sem_shapes, *[pltpu.HBM(s.shape, s.dtype) for s in flat],
                   *[pltpu.HBM(l.shape, l.dtype) for l in lands], jax.ShapeDtypeStruct((8, LANES), F32)),
        in_specs=[HBM_SPEC] * (2 * n) + [pl.BlockSpec(memory_space=pl.ANY)] * (after is not None),
        out_specs=(*[SEM_SPEC] * (2 * ng), *[HBM_SPEC] * (2 * n), pl.BlockSpec(memory_space=pltpu.VMEM)),
        input_output_aliases={i: 2 * ng + i for i in range(2 * n)},
        compiler_params=SPLIT_COPY,
    )(*[_hbm(s) for s in flat], *[_hbm(l) for l in lands], *([after] if after is not None else []))
    sems, thru, token = out[:2 * ng], out[2 * ng:2 * ng + 2 * n], out[-1]
    res, w = [], 0
    for gi, g in enumerate(groups):
        res.append((sems[2 * gi], sems[2 * gi + 1], list(thru[w:w + len(g)]), list(thru[n + w:n + w + len(g)])))
        w += len(g)
    return res, token


def _copies_wait(started, gather, after, name):
    send, recv, srcs, lands = started
    n = len(srcs)
    after = list(after) if isinstance(after, (list, tuple)) else [after]

    own_shapes = [s.shape if gather else (s.shape[0] // N_DEV, s.shape[1]) for s in srcs]

    def body(*refs):
        ins, lnd = refs[:n], refs[n:2 * n]
        send_sems, recv_sems = refs[2 * n], refs[2 * n + 1]
        bounce, in_sems, out_sems = refs[-n - 2:-2], refs[-2], refs[-1]
        x, y, c = _me()
        me = 4 * x + 2 * y + c
        ends = [_copy_ends(gather, ins[w], lnd[w], me, me)[3:] for w in range(n)]
        loads = [pltpu.make_async_copy(ends[w][0], bounce[w], in_sems.at[w]) for w in range(n)]
        stores = [pltpu.make_async_copy(bounce[w], ends[w][1], out_sems.at[w]) for w in range(n)]
        for cp in loads:
            cp.start()
        for w in range(n):
            loads[w].wait()
            stores[w].start()
        for w in range(n):
            for p in range(1, N_DEV):
                peer, peer_id = _peer(x, y, c, p)
                src, _, arrival, _, _ = _copy_ends(gather, ins[w], lnd[w], me, peer_id)
                cp = pltpu.make_async_remote_copy(src, arrival, send_sems.at[w * PEERS + p - 1],
                                                  recv_sems.at[w * PEERS + p - 1], device_id=peer,
                                                  device_id_type=MESH_IDS)
                cp.wait_send()
                cp.wait_recv()
        for cp in stores:
            cp.wait()

    out = pl.pallas_call(
        body, name=name,
        out_shape=(*[pltpu.HBM(s.shape, s.dtype) for s in srcs], *[pltpu.HBM(l.shape, l.dtype) for l in lands]),
        in_specs=[HBM_SPEC] * (2 * n) + [SEM_SPEC, SEM_SPEC] + [pl.BlockSpec(memory_space=pl.ANY)] * len(after),
        out_specs=[HBM_SPEC] * (2 * n),
        input_output_aliases={i: i for i in range(2 * n)},
        scratch_shapes=[*[pltpu.VMEM(shape, s.dtype) for shape, s in zip(own_shapes, srcs)],
                        pltpu.SemaphoreType.DMA((n,)), pltpu.SemaphoreType.DMA((n,))],
        compiler_params=SPLIT_COPY,
    )(*srcs, *lands, send, recv, *after)
    return list(out[n:])


def _relay_to_sibling(started, name):
    send, recv, srcs, lands = started
    n = len(srcs)

    def body(*refs):
        ins, lnd = refs[:n], refs[n:2 * n]
        send_sems, recv_sems = refs[2 * n], refs[2 * n + 1]
        relay_send, relay_recv = refs[2 * n + 2], refs[2 * n + 3]
        x, y, c = _me()
        sibling, _ = _peer(x, y, c, SIBLING)
        for w in range(n):
            rows = ins[w].shape[0]
            for k, p in enumerate(SAME_CORE_PEERS):
                peer, peer_id = _peer(x, y, c, p)
                arrived = _row_block(lnd[w], peer_id, rows)
                first = pltpu.make_async_remote_copy(ins[w], arrived, send_sems.at[w * PEERS + p - 1],
                                                     recv_sems.at[w * PEERS + p - 1], device_id=peer,
                                                     device_id_type=MESH_IDS)
                first.wait_recv()
                pltpu.make_async_remote_copy(arrived, arrived, relay_send.at[w * RELAYS + k],
                                             relay_recv.at[w * RELAYS + k], device_id=sibling,
                                             device_id_type=MESH_IDS).start()
                first.wait_send()

    sems = pltpu.SemaphoreType.DMA((n * RELAYS,))
    out = pl.pallas_call(
        body, name=name,
        out_shape=(sems, sems, *[pltpu.HBM(s.shape, s.dtype) for s in srcs], *[pltpu.HBM(l.shape, l.dtype) for l in lands]),
        in_specs=[HBM_SPEC] * (2 * n) + [SEM_SPEC, SEM_SPEC],
        out_specs=(SEM_SPEC, SEM_SPEC, *[HBM_SPEC] * (2 * n)),
        input_output_aliases={i: 2 + i for i in range(2 * n)},
        compiler_params=SPLIT_COPY,
    )(*srcs, *lands, send, recv)
    return send, recv, out[0], out[1], list(out[2:2 + n]), list(out[2 + n:])


def _relayed_wait(relayed, after, name):
    send, recv, relay_send, relay_recv, srcs, lands = relayed
    n = len(srcs)
    after = list(after) if isinstance(after, (list, tuple)) else [after]

    def body(*refs):
        ins, lnd = refs[:n], refs[n:2 * n]
        send_sems, recv_sems, relay_send_sems, relay_recv_sems = refs[2 * n:2 * n + 4]
        bounce, in_sems, out_sems = refs[-n - 2:-2], refs[-2], refs[-1]
        x, y, c = _me()
        me = 4 * x + 2 * y + c
        sibling, sibling_id = _peer(x, y, c, SIBLING)
        loads = [pltpu.make_async_copy(ins[w], bounce[w], in_sems.at[w]) for w in range(n)]
        stores = [pltpu.make_async_copy(bounce[w], _row_block(lnd[w], me, ins[w].shape[0]), out_sems.at[w])
                  for w in range(n)]
        for cp in loads:
            cp.start()
        for w in range(n):
            loads[w].wait()
            stores[w].start()
        for w in range(n):
            rows = ins[w].shape[0]
            direct = pltpu.make_async_remote_copy(ins[w], _row_block(lnd[w], sibling_id, rows),
                                                  send_sems.at[w * PEERS + SIBLING - 1],
                                                  recv_sems.at[w * PEERS + SIBLING - 1], device_id=sibling,
                                                  device_id_type=MESH_IDS)
            direct.wait_send()
            direct.wait_recv()
            for k, p in enumerate(SAME_CORE_PEERS):
                _, sent_id = _peer(x, y, c, p)
                _, got_id = _peer(x, y, c, p + SIBLING)
                relay = pltpu.make_async_remote_copy(_row_block(lnd[w], sent_id, rows), _row_block(lnd[w], got_id, rows),
                                                     relay_send_sems.at[w * RELAYS + k],
                                                     relay_recv_sems.at[w * RELAYS + k], device_id=sibling,
                                                     device_id_type=MESH_IDS)
                relay.wait_send()
                relay.wait_recv()
        for cp in stores:
            cp.wait()

    out = pl.pallas_call(
        body, name=name,
        out_shape=(*[pltpu.HBM(s.shape, s.dtype) for s in srcs], *[pltpu.HBM(l.shape, l.dtype) for l in lands]),
        in_specs=[HBM_SPEC] * (2 * n) + [SEM_SPEC] * 4 + [pl.BlockSpec(memory_space=pl.ANY)] * len(after),
        out_specs=[HBM_SPEC] * (2 * n),
        input_output_aliases={i: i for i in range(2 * n)},
        scratch_shapes=[*[pltpu.VMEM(s.shape, s.dtype) for s in srcs],
                        pltpu.SemaphoreType.DMA((n,)), pltpu.SemaphoreType.DMA((n,))],
        compiler_params=SPLIT_COPY,
    )(*srcs, *lands, send, recv, relay_send, relay_recv, *after)
    return list(out[n:])


def _adamw_update(w, g, m, v):
    nm = ADAM_B1 * m + (1.0 - ADAM_B1) * g
    nv = ADAM_B2 * v + (1.0 - ADAM_B2) * (g * g)
    m_hat = nm / (1.0 - ADAM_B1 ** ADAM_STEP)
    v_hat = nv / (1.0 - ADAM_B2 ** ADAM_STEP)
    return -ADAM_LR * (m_hat / (jnp.sqrt(v_hat) + ADAM_EPS) + ADAM_WD * w), nm, nv


SUM_ADAMW_COLS = 256


def _sum_adamw(parts, w, m, v, name):
    _, rows, d = parts.shape
    n = w.shape[0]
    tc = SUM_ADAMW_COLS

    def body(p_ref, w_ref, m_ref, v_ref, g_ref, d_ref, nm_ref, nv_ref):
        g = p_ref[0].astype(F32)
        for dev in range(1, N_DEV):
            g = g + p_ref[dev].astype(F32)
        g = g[:n]
        g_ref[...] = g
        d_ref[...], nm_ref[...], nv_ref[...] = _adamw_update(w_ref[...], g, m_ref[...], v_ref[...])

    spec = pl.BlockSpec((n, tc), lambda j: (0, j))
    shape = jax.ShapeDtypeStruct((n, d), F32)
    return pl.pallas_call(
        body, name=name, grid=(d // tc,),
        in_specs=[pl.BlockSpec((N_DEV, rows, tc), lambda j: (0, 0, j)), spec, spec, spec],
        out_specs=[spec] * 4, out_shape=[shape] * 4,
        compiler_params=_params("arbitrary"),
    )(parts, w, m, v)


def _pad_rows(a, rows):
    return jnp.pad(a, ((0, rows - a.shape[0]), (0, 0)))


def _row1(vec, width=D_MODEL):
    return jnp.pad(vec.reshape(1, -1), ((0, 0), (0, width - vec.shape[-1])))


COLUMN_SHARDED = ("ffn1_w_gate", "ffn1_w_up", "w_in", "ffn2_w_gate", "ffn2_w_up")
VEC_NAMES = ("ffn1_norm", "mix_norm", "ffn2_norm", "b_forget", "pool_scale", "q_norm", "k_norm", "out_norm_pool",
             "out_norm_attn")
VEC_ROWS = 16
LOSS_ROW = len(VEC_NAMES)


def _pack_vector_grads(parts, loss_part):
    def body(*refs):
        loss_ref, out_ref = refs[-2], refs[-1]
        out_ref[...] = jnp.zeros_like(out_ref)
        lane = lax.broadcasted_iota(jnp.int32, (1, LANES), 1)
        for i, (name, ref) in enumerate(zip(VEC_NAMES, refs[:-2])):
            val = ref[...]
            if name in ("q_norm", "k_norm"):
                val = val[:, 0:LANES] + val[:, LANES:2 * LANES] + val[:, 2 * LANES:3 * LANES] + val[:, 3 * LANES:]
                val = jnp.where(lane < HEAD_DIM, val + pltpu.roll(val, HEAD_DIM, 1), 0.0)
            out_ref[pl.ds(i, 1), pl.ds(0, val.shape[1])] = val
        out_ref[pl.ds(LOSS_ROW, 1), pl.ds(0, 1)] = loss_ref[...]

    vmem = pl.BlockSpec(memory_space=pltpu.VMEM)
    return pl.pallas_call(
        body, name="pack_vector_grads", in_specs=[vmem] * (len(parts) + 1), out_specs=vmem,
        out_shape=jax.ShapeDtypeStruct((VEC_ROWS, D_MODEL), F32),
    )(*parts, loss_part)


def _small_adamw(vec_all, pool_all, vec_params, pool_params):
    nv = len(vec_params)
    pool_rows = pool_params[0].shape[0]

    def body(*refs):
        vec_ref, pool_ref = refs[0], refs[1]
        ins = refs[2:2 + 3 * nv + 3]
        outs = refs[2 + 3 * nv + 3:-1]
        rows = refs[-1]
        total = vec_ref[pl.ds(0, VEC_ROWS), :]
        for dev in range(1, N_DEV):
            total = total + vec_ref[pl.ds(dev * VEC_ROWS, VEC_ROWS), :]
        rows[...] = total
        outs[4 * nv + 4][...] = rows[pl.ds(LOSS_ROW, 1), pl.ds(0, 1)]
        for i in range(nv):
            w_ref, m_ref, v_ref = ins[3 * i:3 * i + 3]
            g = rows[pl.ds(i, 1), pl.ds(0, w_ref.shape[1])]
            outs[4 * i][...] = g
            outs[4 * i + 1][...], outs[4 * i + 2][...], outs[4 * i + 3][...] = _adamw_update(
                w_ref[...], g, m_ref[...], v_ref[...])
        g = pool_ref[pl.ds(0, pool_rows), :]
        for dev in range(1, N_DEV):
            g = g + pool_ref[pl.ds(dev * pool_rows, pool_rows), :]
        w_ref, m_ref, v_ref = ins[3 * nv:]
        outs[4 * nv][...] = g
        outs[4 * nv + 1][...], outs[4 * nv + 2][...], outs[4 * nv + 3][...] = _adamw_update(
            w_ref[...], g, m_ref[...], v_ref[...])

    vmem = pl.BlockSpec(memory_space=pltpu.VMEM)
    flat = [a for trio in vec_params for a in trio] + list(pool_params)
    out_shape = []
    for trio in list(vec_params) + [pool_params]:
        out_shape += [jax.ShapeDtypeStruct(trio[0].shape, F32)] * 4
    out_shape.append(jax.ShapeDtypeStruct((1, 1), F32))
    return pl.pallas_call(
        body, name="adamw_small", in_specs=[vmem] * (2 + len(flat)), out_specs=[vmem] * len(out_shape),
        out_shape=out_shape, scratch_shapes=[pltpu.VMEM((VEC_ROWS, D_MODEL), F32)],
    )(vec_all, pool_all, *flat)


def kernel(x, ffn1_norm, ffn1_w_gate, ffn1_w_up, ffn1_w_down, mix_norm, w_in, b_forget, pool_w, pool_scale, q_norm, k_norm, out_norm_pool, out_norm_attn, w_out, ffn2_norm, ffn2_w_gate, ffn2_w_up, ffn2_w_down, loss_target, m_ffn1_norm, m_ffn1_w_gate, m_ffn1_w_up, m_ffn1_w_down, m_mix_norm, m_w_in, m_b_forget, m_pool_w, m_pool_scale, m_q_norm, m_k_norm, m_out_norm_pool, m_out_norm_attn, m_w_out, m_ffn2_norm, m_ffn2_w_gate, m_ffn2_w_up, m_ffn2_w_down, v_ffn1_norm, v_ffn1_w_gate, v_ffn1_w_up, v_ffn1_w_down, v_mix_norm, v_w_in, v_b_forget, v_pool_w, v_pool_scale, v_q_norm, v_k_norm, v_out_norm_pool, v_out_norm_attn, v_w_out, v_ffn2_norm, v_ffn2_w_gate, v_ffn2_w_up, v_ffn2_w_down):
    bsz, seq, d = x.shape
    t = bsz * seq
    x0 = x.reshape(t, d)
    target = loss_target.reshape(t, d)
    in_rows = -(-w_in.shape[1] // BF16_ROWS) * BF16_ROWS

    slabs = [s.astype(BF16) for s in (ffn1_w_gate.T, ffn1_w_up.T, ffn1_w_down, _pad_rows(w_in.T, in_rows), w_out,
                                       ffn2_w_gate.T, ffn2_w_up.T, ffn2_w_down)]
    gathers, started = _copies_start([slabs[0:2], slabs[2:3], slabs[3:4], slabs[4:5], slabs[5:8]], True, "gather_start",
                                     relayed=(0,))

    g1, gm, g2 = ffn1_norm.reshape(1, d), mix_norm.reshape(1, d), ffn2_norm.reshape(1, d)
    bf_row = _row1(b_forget, LANES)
    gq = jnp.tile(q_norm, N_HEADS).reshape(1, ATTN_WIDTH)
    gk = jnp.tile(k_norm, N_HEADS).reshape(1, ATTN_WIDTH)
    scale_row = pool_scale.reshape(1, POOL_WIDTH)
    gp, ga = out_norm_pool.reshape(1, POOL_WIDTH), out_norm_attn.reshape(1, ATTN_WIDTH)

    wg1, wu1 = _relayed_wait(_relay_to_sibling(gathers[0], "gather_relay_ffn1_up"), started, "gather_wait_ffn1_up")
    h1, a1, b1, s1 = _ffn_up(x0, g1, wg1, wu1, "ffn1_up")
    (wd1,) = _copies_wait(gathers[1], True, s1, "gather_wait_ffn1_down")
    (x1,) = _ffn_down(s1, wd1, x0, None, "ffn1_down")
    (win_g,) = _copies_wait(gathers[2], True, x1, "gather_wait_w_in")
    win_cols = win_g.reshape(N_DEV, in_rows, d)[:, :w_in.shape[1]].reshape(MIX_COLS, d)
    win_t = _pad_rows(win_cols, MIX_PAD)
    hm, pv, q, k, v, f = _mix_in_fwd(x1, gm, win_t)
    pooled, mixed, y_pool = _pool_fwd(pv, pool_w, scale_row, gp, bsz, seq)
    qp, kp = _attn_prep_fwd(q, k, f, bf_row, gq, gk, bsz, seq)
    o, lse = _flash_fwd(qp, kp, v, bsz, seq)
    (wout,) = _copies_wait(gathers[3], True, o, "gather_wait_w_out")
    ycat, x2 = _mix_out_fwd(o, y_pool, x1, ga, wout)
    wg2, wu2, wd2 = _copies_wait(gathers[4], True, x2, "gather_wait_ffn2")
    h2, a2, b2, s2 = _ffn_up(x2, g2, wg2, wu2, "ffn2_up")
    dx3, dyh2, loss_part = _ffn_down(s2, wd2, x2, target, "ffn2_down")

    da2, db2 = _ffn_bwd_act(dyh2, a2, b2, wd2, "ffn2_bwd_act")
    dwg2, dwu2 = _wgrad([da2, db2], h2, "ffn2_up_wgrad")
    (dwd2,) = _wgrad([s2], dyh2, "ffn2_down_wgrad")
    (sent_ffn2,), tok = _copies_start([[dwg2, dwu2, dwd2]], False, "exchange_start_ffn2")
    dx2, dg2 = _ffn_bwd_dx(da2, db2, dx3, x2, g2 + tok[0, 0], wg2, wu2, "ffn2_bwd_dx")
    dx2b, dy_pool, do, dga = _mix_out_bwd(dx2, o, ga, wout)
    (dwout,) = _wgrad([ycat], dx2b, "w_out_wgrad")
    (sent_out,), tok = _copies_start([[dwout]], False, "exchange_start_w_out")
    dqp, dkp, dv = _flash_bwd(qp, kp, v, o, do, lse, bsz, seq)
    dq, dk, df, dgq, dgk, dbf = _attn_prep_bwd(dqp, dkp, q, k, f, bf_row + tok[0, 0], gq, gk, bsz, seq)
    dpv, dpool_w, dscale, dgp = _pool_bwd(dy_pool, mixed, pooled, pool_w, scale_row, gp, bsz, seq)
    dhcat, dx1, dyh1, dgm = _mix_in_bwd(dpv, dq, dk, dv, df, x1, dx2, gm, win_t)
    (dwin,) = _wgrad([dhcat], hm, "w_in_wgrad")
    dwin_blocks = jnp.pad(dwin[:MIX_COLS].reshape(N_DEV, w_in.shape[1], d), ((0, 0), (0, in_rows - w_in.shape[1]), (0, 0)))
    (sent_in,), tok = _copies_start([[dwin_blocks.reshape(N_DEV * in_rows, d)]], False, "exchange_start_w_in")
    (dwd1,) = _wgrad([s1], dyh1, "ffn1_down_wgrad")
    (sent_down1,), tok = _copies_start([[dwd1]], False, "exchange_start_ffn1_down", after=tok)
    da1, db1 = _ffn_bwd_act(dyh1, a1, b1, wd1, "ffn1_bwd_act")
    dwg1, dwu1 = _wgrad([da1, db1], h1, "ffn1_up_wgrad")
    (sent_up1,), tok = _copies_start([[dwg1, dwu1]], False, "exchange_start_ffn1_up", after=tok)
    dx0, dg1 = _ffn_bwd_dx(da1, db1, dx1, x0, g1 + tok[0, 0], wg1, wu1, "ffn1_bwd_dx")

    pool_rows = POOL_GROUPS * POOL_GROUP_DIM
    packed = _pack_vector_grads([dg1, dgm, dg2, dbf, dscale, dgq, dgk, dgp, dga], loss_part)
    (sent_small,), tok = _copies_start([[packed, dpool_w.reshape(pool_rows, POOL_GROUP_DIM)]], True, "small_grads_start")

    weights = dict(ffn1_norm=ffn1_norm, ffn1_w_gate=ffn1_w_gate, ffn1_w_up=ffn1_w_up, ffn1_w_down=ffn1_w_down,
                   mix_norm=mix_norm, w_in=w_in, b_forget=b_forget, pool_w=pool_w, pool_scale=pool_scale,
                   q_norm=q_norm, k_norm=k_norm, out_norm_pool=out_norm_pool, out_norm_attn=out_norm_attn,
                   w_out=w_out, ffn2_norm=ffn2_norm, ffn2_w_gate=ffn2_w_gate, ffn2_w_up=ffn2_w_up,
                   ffn2_w_down=ffn2_w_down)
    m_in = dict(ffn1_norm=m_ffn1_norm, ffn1_w_gate=m_ffn1_w_gate, ffn1_w_up=m_ffn1_w_up, ffn1_w_down=m_ffn1_w_down,
                mix_norm=m_mix_norm, w_in=m_w_in, b_forget=m_b_forget, pool_w=m_pool_w, pool_scale=m_pool_scale,
                q_norm=m_q_norm, k_norm=m_k_norm, out_norm_pool=m_out_norm_pool, out_norm_attn=m_out_norm_attn,
                w_out=m_w_out, ffn2_norm=m_ffn2_norm, ffn2_w_gate=m_ffn2_w_gate, ffn2_w_up=m_ffn2_w_up,
                ffn2_w_down=m_ffn2_w_down)
    v_in = dict(ffn1_norm=v_ffn1_norm, ffn1_w_gate=v_ffn1_w_gate, ffn1_w_up=v_ffn1_w_up, ffn1_w_down=v_ffn1_w_down,
                mix_norm=v_mix_norm, w_in=v_w_in, b_forget=v_b_forget, pool_w=v_pool_w, pool_scale=v_pool_scale,
                q_norm=v_q_norm, k_norm=v_k_norm, out_norm_pool=v_out_norm_pool, out_norm_attn=v_out_norm_attn,
                w_out=v_w_out, ffn2_norm=v_ffn2_norm, ffn2_w_gate=v_ffn2_w_gate, ffn2_w_up=v_ffn2_w_up,
                ffn2_w_down=v_ffn2_w_down)
    grads, delta, new_m, new_v = {}, {}, {}, {}
    after = [tok]
    plan = ((sent_ffn2, "ffn2", ("ffn2_w_gate", "ffn2_w_up", "ffn2_w_down")), (sent_out, "w_out", ("w_out",)),
            (sent_in, "w_in", ("w_in",)), (sent_down1, "ffn1_down", ("ffn1_w_down",)),
            (sent_up1, "ffn1_up", ("ffn1_w_gate", "ffn1_w_up")))
    for sent, tag, names in plan:
        parts = _copies_wait(sent, False, after, f"exchange_wait_{tag}")
        for n, part in zip(names, parts):
            turn = (lambda a: a.T) if n in COLUMN_SHARDED else (lambda a: a)
            done = _sum_adamw(part, turn(weights[n]), turn(m_in[n]), turn(v_in[n]), f"adamw_{n}")
            grads[n], delta[n], new_m[n], new_v[n] = (turn(a) for a in done)
        after = [new_v[n] for n in names]
    vec_all, pool_all = _copies_wait(sent_small, True, after, "small_grads_wait")
    as_row = lambda a: a.reshape(1, -1)
    as_pool = lambda a: a.reshape(pool_rows, POOL_GROUP_DIM)
    small = _small_adamw(vec_all, pool_all, [tuple(as_row(z[n]) for z in (weights, m_in, v_in)) for n in VEC_NAMES],
                         tuple(as_pool(z["pool_w"]) for z in (weights, m_in, v_in)))
    for i, n in enumerate(VEC_NAMES + ("pool_w",)):
        grads[n], delta[n], new_m[n], new_v[n] = (a.reshape(weights[n].shape) for a in small[4 * i:4 * i + 4])
    loss = small[-1].reshape(())

    order = ("ffn1_norm", "ffn1_w_gate", "ffn1_w_up", "ffn1_w_down", "mix_norm", "w_in", "b_forget", "pool_w",
             "pool_scale", "q_norm", "k_norm", "out_norm_pool", "out_norm_attn", "w_out", "ffn2_norm", "ffn2_w_gate",
             "ffn2_w_up", "ffn2_w_down")
    return (loss, dx0.reshape(bsz, seq, d), *[grads[n] for n in order], *[delta[n] for n in order],
            *[new_m[n] for n in order], *[new_v[n] for n in order])
```

```python
import functools

import jax
import jax.numpy as jnp
from jax import lax
from jax.experimental import pallas as pl
from jax.experimental.pallas import tpu as pltpu

F32 = jnp.float32
BF16 = jnp.bfloat16

EPS = 1e-6
D_MODEL = 1024
D_FF = 2816
N_HEADS = 8
HEAD_DIM = 64
POOL_WIDTH = 512
ATTN_WIDTH = 512
POOL_GROUPS = 4
POOL_GROUP_DIM = 128
POOL_WINDOWS = (2, 4, 8, 16)
POOL_HALO = 16
MIX_COLS = POOL_WIDTH + 3 * ATTN_WIDTH + N_HEADS
MIX_PAD = POOL_WIDTH + 3 * ATTN_WIDTH + 128
N_DEV = 8
BF16_ROWS = 16
LANES = 128
VMEM_LIMIT = 56 * 1024 * 1024

ADAM_LR = 0.001
ADAM_B1 = 0.9
ADAM_B2 = 0.999
ADAM_EPS = 1e-08
ADAM_WD = 0.01
ADAM_STEP = 10


def _params(*sem):
    return pltpu.CompilerParams(dimension_semantics=sem, vmem_limit_bytes=VMEM_LIMIT)


def _dot(a, b):
    return jnp.dot(a, b, preferred_element_type=F32)


def _dot_nt(a, b):
    return lax.dot_general(a, b, (((1,), (1,)), ((), ())), preferred_element_type=F32)


def _dot_tn(a, b):
    return lax.dot_general(a, b, (((0,), (0,)), ((), ())), preferred_element_type=F32)


def _resident(shape):
    return pl.BlockSpec(shape, lambda *_: (0,) * len(shape), pipeline_mode=pl.Buffered(1))


def _rows(tm, width):
    return pl.BlockSpec((tm, width), lambda i: (i, 0))


def _rms_scale(x):
    return lax.rsqrt(jnp.mean(x * x, axis=-1, keepdims=True) + EPS)


def _rms_bwd(dh, x, gain):
    r = _rms_scale(x)
    n = x * r
    dgain = jnp.sum(dh * n, axis=0, keepdims=True)
    dn = dh * gain
    dx = r * (dn - n * jnp.mean(dn * n, axis=-1, keepdims=True))
    return dx, dgain


def _split3(x):
    hi = x.astype(BF16)
    r1 = x - hi.astype(F32)
    mid = r1.astype(BF16)
    lo = (r1 - mid.astype(F32)).astype(BF16)
    return hi, mid, lo


def _split2(x):
    hi = x.astype(BF16)
    return hi, (x - hi.astype(F32)).astype(BF16)


FF_CHUNK = 256


def _ffn_up(x, gain, wg_t, wu_t, name):
    t, d = x.shape
    f = wg_t.shape[0]
    tm = 512

    def body(x_ref, g_ref, wg_ref, wu_ref, h_ref, sa_ref, sb_ref, s_ref):
        xv = x_ref[...]
        h = (xv * _rms_scale(xv) * g_ref[...]).astype(BF16)
        h_ref[...] = h
        for c in range(f // FF_CHUNK):
            sl = pl.ds(c * FF_CHUNK, FF_CHUNK)
            a = _dot_nt(h, wg_ref[sl, :])
            b = _dot_nt(h, wu_ref[sl, :])
            sig = jax.nn.sigmoid(a)
            silu = a * sig
            sa_ref[:, sl] = (b * (sig + silu * (1.0 - sig))).astype(BF16)
            sb_ref[:, sl] = silu.astype(BF16)
            s_ref[:, sl] = (silu * b).astype(BF16)

    wide = jax.ShapeDtypeStruct((t, f), BF16)
    return pl.pallas_call(
        body, name=name, grid=(t // tm,),
        in_specs=[_rows(tm, d), _resident((1, d)), _resident((f, d)), _resident((f, d))],
        out_specs=[_rows(tm, d), _rows(tm, f), _rows(tm, f), _rows(tm, f)],
        out_shape=[jax.ShapeDtypeStruct((t, d), BF16), wide, wide, wide],
        compiler_params=_params("arbitrary"),
    )(x, gain, wg_t, wu_t)


def _ffn_down(s, wd, x, target, name):
    t, d = x.shape
    f = wd.shape[0]
    tm = 512
    with_loss = target is not None

    def body(*refs):
        if with_loss:
            s_ref, w_ref, x_ref, t_ref, dy_ref, dyh_ref, loss_ref = refs
        else:
            s_ref, w_ref, x_ref, y_ref = refs
        y = x_ref[...] + 0.5 * _dot(s_ref[...], w_ref[...])
        if with_loss:
            e = y - t_ref[...]
            dy = e * (1.0 / d)
            dy_ref[...] = dy
            dyh_ref[...] = (0.5 * dy).astype(BF16)

            @pl.when(pl.program_id(0) == 0)
            def _():
                loss_ref[...] = jnp.zeros_like(loss_ref)

            part = jnp.sum(jnp.sum(e * e, axis=0, keepdims=True), axis=1, keepdims=True)
            loss_ref[...] += part * (0.5 / d)
        else:
            y_ref[...] = y

    in_specs = [_rows(tm, f), _resident((f, d)), _rows(tm, d)]
    args = [s, wd, x]
    if with_loss:
        in_specs.append(_rows(tm, d))
        args.append(target)
        out_shape = [jax.ShapeDtypeStruct((t, d), F32), jax.ShapeDtypeStruct((t, d), BF16),
                     jax.ShapeDtypeStruct((1, 1), F32)]
        out_specs = [_rows(tm, d), _rows(tm, d), pl.BlockSpec((1, 1), lambda i: (0, 0))]
    else:
        out_shape = [jax.ShapeDtypeStruct((t, d), F32)]
        out_specs = [_rows(tm, d)]
    return pl.pallas_call(
        body, name=name, grid=(t // tm,), in_specs=in_specs, out_specs=out_specs, out_shape=out_shape,
        compiler_params=_params("arbitrary"),
    )(*args)


def _ffn_bwd_act(dyh, sa, sb, wd, name):
    t, d = dyh.shape
    f = wd.shape[0]
    tm = 512

    def body(dy_ref, sa_ref, sb_ref, wd_ref, da_ref, db_ref):
        dyh_v = dy_ref[...]
        for c in range(f // FF_CHUNK):
            sl = pl.ds(c * FF_CHUNK, FF_CHUNK)
            ds = _dot_nt(dyh_v, wd_ref[sl, :])
            da_ref[:, sl] = (ds * sa_ref[:, sl].astype(F32)).astype(BF16)
            db_ref[:, sl] = (ds * sb_ref[:, sl].astype(F32)).astype(BF16)

    wide = jax.ShapeDtypeStruct((t, f), BF16)
    return pl.pallas_call(
        body, name=name, grid=(t // tm,),
        in_specs=[_rows(tm, d), _rows(tm, f), _rows(tm, f), _resident((f, d))],
        out_specs=[_rows(tm, f), _rows(tm, f)], out_shape=[wide, wide],
        compiler_params=_params("arbitrary"),
    )(dyh, sa, sb, wd)


def _ffn_bwd_dx(da, db, dy, x, gain, wg_t, wu_t, name):
    t, d = x.shape
    f = wg_t.shape[0]
    tm = 512

    def body(da_ref, db_ref, dy_ref, x_ref, g_ref, wg_ref, wu_ref, dx_ref, dg_ref):
        dh = _dot(da_ref[...], wg_ref[...]) + _dot(db_ref[...], wu_ref[...])
        dx, dgain = _rms_bwd(dh, x_ref[...], g_ref[...])
        dx_ref[...] = dy_ref[...] + dx

        @pl.when(pl.program_id(0) == 0)
        def _():
            dg_ref[...] = jnp.zeros_like(dg_ref)

        dg_ref[...] += dgain

    return pl.pallas_call(
        body, name=name, grid=(t // tm,),
        in_specs=[_rows(tm, f), _rows(tm, f), _rows(tm, d), _rows(tm, d), _resident((1, d)), _resident((f, d)),
                  _resident((f, d))],
        out_specs=[_rows(tm, d), pl.BlockSpec((1, d), lambda i: (0, 0))],
        out_shape=[jax.ShapeDtypeStruct((t, d), F32), jax.ShapeDtypeStruct((1, d), F32)],
        compiler_params=_params("arbitrary"),
    )(da, db, dy, x, gain, wg_t, wu_t)


def _wgrad(lhs, b, name):
    t, n = lhs[0].shape
    d = b.shape[1]
    m = len(lhs)
    tn = n // 2 if n * d * m > (4 << 20) else n
    tk = 1024
    nk = t // tk

    def body(*refs):
        a_refs, b_ref, o_refs, accs = refs[:m], refs[m], refs[m + 1:2 * m + 1], refs[2 * m + 1:]
        k = pl.program_id(1)

        @pl.when(k == 0)
        def _():
            for acc in accs:
                acc[...] = jnp.zeros_like(acc)

        bv = b_ref[...]
        for a_ref, acc in zip(a_refs, accs):
            acc[...] += _dot_tn(a_ref[...], bv)

        @pl.when(k == nk - 1)
        def _():
            for o_ref, acc in zip(o_refs, accs):
                o_ref[...] = acc[...].astype(BF16)

    return pl.pallas_call(
        body, name=name, grid=(n // tn, nk),
        in_specs=[pl.BlockSpec((tk, tn), lambda j, k: (k, j))] * m + [pl.BlockSpec((tk, d), lambda j, k: (k, 0))],
        out_specs=[pl.BlockSpec((tn, d), lambda j, k: (j, 0))] * m,
        out_shape=[jax.ShapeDtypeStruct((n, d), BF16)] * m,
        scratch_shapes=[pltpu.VMEM((tn, d), F32)] * m,
        compiler_params=_params("arbitrary", "arbitrary"),
    )(*lhs, b)


def _mix_in_fwd(x, gain, w_in_t):
    t, d = x.shape
    tm = 1024
    pw, aw = POOL_WIDTH, ATTN_WIDTH

    def body(x_ref, g_ref, w_ref, hm_ref, pv_ref, q_ref, k_ref, v_ref, f_ref):
        xv = x_ref[...]
        hm = (xv * _rms_scale(xv) * g_ref[...]).astype(BF16)
        hm_ref[...] = hm
        pv_ref[...] = _dot_nt(hm, w_ref[pl.ds(0, pw), :])
        q_ref[...] = _dot_nt(hm, w_ref[pl.ds(pw, aw), :])
        k_ref[...] = _dot_nt(hm, w_ref[pl.ds(pw + aw, aw), :])
        v_ref[...] = _dot_nt(hm, w_ref[pl.ds(pw + 2 * aw, aw), :]).astype(BF16)
        f_ref[...] = _dot_nt(hm, w_ref[pl.ds(pw + 3 * aw, LANES), :])

    return pl.pallas_call(
        body, name="mix_in_fwd", grid=(t // tm,),
        in_specs=[_rows(tm, d), _resident((1, d)), _resident((MIX_PAD, d))],
        out_specs=[_rows(tm, d), _rows(tm, pw), _rows(tm, aw), _rows(tm, aw), _rows(tm, aw), _rows(tm, LANES)],
        out_shape=[jax.ShapeDtypeStruct((t, d), BF16), jax.ShapeDtypeStruct((t, pw), F32),
                   jax.ShapeDtypeStruct((t, aw), F32), jax.ShapeDtypeStruct((t, aw), F32),
                   jax.ShapeDtypeStruct((t, aw), BF16), jax.ShapeDtypeStruct((t, LANES), F32)],
        compiler_params=_params("arbitrary"),
    )(x, gain, w_in_t)


def _pool_fwd(pv, pool_w, pool_scale, gain, bsz, seq):
    ts = 512
    ns = seq // ts
    pw = POOL_WIDTH

    def body(pv_ref, w_ref, sc_ref, g_ref, pooled_ref, mixed_ref, y_ref, ext):
        s = pl.program_id(1)

        @pl.when(s == 0)
        def _():
            ext[pl.ds(0, POOL_HALO), :] = jnp.zeros((POOL_HALO, pw), F32)

        p = pv_ref[...]
        ext[pl.ds(POOL_HALO, ts), :] = p
        pos = s * ts + lax.broadcasted_iota(jnp.int32, (ts, 1), 0)
        parts = []
        for g, w in enumerate(POOL_WINDOWS):
            lanes = pl.ds(g * POOL_GROUP_DIM, POOL_GROUP_DIM)
            win = ext[pl.ds(POOL_HALO, ts), lanes]
            for i in range(1, w):
                win = win + ext[pl.ds(POOL_HALO - i, ts), lanes]
            cnt = jnp.minimum(pos + 1, w).astype(F32)
            pooled = (win / cnt - ext[pl.ds(POOL_HALO, ts), lanes]).astype(BF16)
            pooled_ref[:, lanes] = pooled
            parts.append(_dot(pooled, w_ref[g].astype(BF16)))
        mixed = jnp.concatenate(parts, axis=1)
        mixed_ref[...] = mixed
        pm = mixed * sc_ref[...]
        y_ref[...] = (pm * _rms_scale(pm) * g_ref[...]).astype(BF16)
        ext[pl.ds(0, POOL_HALO), :] = p[ts - POOL_HALO:, :]

    blk = pl.BlockSpec((ts, pw), lambda b, s: (b * ns + s, 0))
    t = bsz * seq
    return pl.pallas_call(
        body, name="pool_fwd", grid=(bsz, ns),
        in_specs=[blk, pl.BlockSpec((POOL_GROUPS, POOL_GROUP_DIM, POOL_GROUP_DIM), lambda b, s: (0, 0, 0)),
                  pl.BlockSpec((1, pw), lambda b, s: (0, 0)), pl.BlockSpec((1, pw), lambda b, s: (0, 0))],
        out_specs=[blk, blk, blk],
        out_shape=[jax.ShapeDtypeStruct((t, pw), BF16), jax.ShapeDtypeStruct((t, pw), F32),
                   jax.ShapeDtypeStruct((t, pw), BF16)],
        scratch_shapes=[pltpu.VMEM((POOL_HALO + ts, pw), F32)],
        compiler_params=_params("arbitrary", "arbitrary"),
    )(pv, pool_w, pool_scale, gain)


def _pool_bwd(dy, mixed, pooled, pool_w, pool_scale, gain, bsz, seq):
    ts = 512
    ns = seq // ts
    pw = POOL_WIDTH

    def body(dy_ref, mixed_ref, pooled_ref, w_ref, sc_ref, g_ref, dpv_ref, dw_ref, dsc_ref, dg_ref, ext):
        b = pl.program_id(0)
        sr = pl.program_id(1)
        s = ns - 1 - sr

        @pl.when(jnp.logical_and(b == 0, sr == 0))
        def _():
            dw_ref[...] = jnp.zeros_like(dw_ref)
            dsc_ref[...] = jnp.zeros_like(dsc_ref)
            dg_ref[...] = jnp.zeros_like(dg_ref)

        @pl.when(sr == 0)
        def _():
            ext[pl.ds(ts, POOL_HALO), :] = jnp.zeros((POOL_HALO, pw), F32)

        mixed = mixed_ref[...]
        sc = sc_ref[...]
        dpm, dgain = _rms_bwd(dy_ref[...], mixed * sc, g_ref[...])
        dg_ref[...] += dgain
        dsc_ref[...] += jnp.sum(dpm * mixed, axis=0, keepdims=True)
        dmixed = (dpm * sc).astype(BF16)
        pos = s * ts + lax.broadcasted_iota(jnp.int32, (ts, 1), 0)
        dpooled = []
        for g, w in enumerate(POOL_WINDOWS):
            lanes = pl.ds(g * POOL_GROUP_DIM, POOL_GROUP_DIM)
            dm = dmixed[:, g * POOL_GROUP_DIM:(g + 1) * POOL_GROUP_DIM]
            dw_ref[g] += _dot_tn(pooled_ref[:, lanes], dm)
            dp = _dot_nt(dm, w_ref[g].astype(BF16))
            dpooled.append(dp)
            cnt = jnp.minimum(pos + 1, w).astype(F32)
            ext[pl.ds(0, ts), lanes] = dp / cnt
        for g, w in enumerate(POOL_WINDOWS):
            lanes = pl.ds(g * POOL_GROUP_DIM, POOL_GROUP_DIM)
            win = ext[pl.ds(0, ts), lanes]
            for i in range(1, w):
                win = win + ext[pl.ds(i, ts), lanes]
            dpv_ref[:, lanes] = (win - dpooled[g]).astype(BF16)
        head = ext[pl.ds(0, POOL_HALO), :]
        ext[pl.ds(ts, POOL_HALO), :] = head

    blk = pl.BlockSpec((ts, pw), lambda b, s: (b * ns + (ns - 1 - s), 0))
    vec = pl.BlockSpec((1, pw), lambda b, s: (0, 0))
    wspec = pl.BlockSpec((POOL_GROUPS, POOL_GROUP_DIM, POOL_GROUP_DIM), lambda b, s: (0, 0, 0))
    t = bsz * seq
    return pl.pallas_call(
        body, name="pool_bwd", grid=(bsz, ns),
        in_specs=[blk, blk, blk, wspec, vec, vec],
        out_specs=[blk, wspec, vec, vec],
        out_shape=[jax.ShapeDtypeStruct((t, pw), BF16),
                   jax.ShapeDtypeStruct((POOL_GROUPS, POOL_GROUP_DIM, POOL_GROUP_DIM), F32),
                   jax.ShapeDtypeStruct((1, pw), F32), jax.ShapeDtypeStruct((1, pw), F32)],
        scratch_shapes=[pltpu.VMEM((ts + POOL_HALO, pw), F32)],
        compiler_params=_params("arbitrary", "arbitrary"),
    )(dy, mixed, pooled, pool_w, pool_scale, gain)


AUX_ONE = 64
AUX_F = 67

ATTN_PREP_ROWS = 512


def _seg_ones(width, seg):
    r = lax.broadcasted_iota(jnp.int32, (width, width), 0) // seg
    c = lax.broadcasted_iota(jnp.int32, (width, width), 1) // seg
    return (r == c).astype(BF16)


def _tri_ones(n, lower):
    r = lax.broadcasted_iota(jnp.int32, (n, n), 0)
    c = lax.broadcasted_iota(jnp.int32, (n, n), 1)
    return ((r >= c) if lower else (r <= c)).astype(BF16)


def _place_pieces(first_lane):
    r = lax.broadcasted_iota(jnp.int32, (3 * LANES, N_HEADS * LANES), 0)
    c = lax.broadcasted_iota(jnp.int32, (3 * LANES, N_HEADS * LANES), 1)
    piece, head = r // LANES, r % LANES
    return jnp.logical_and(head < N_HEADS, c == head * LANES + first_lane + piece).astype(BF16)


def _head_sums(x, seg_ones):
    hi, lo = _split2(x)
    return _dot(hi, seg_ones) + _dot(lo, seg_ones)


def _log_sigmoid(x):
    return jnp.minimum(x, 0.0) - jnp.log(1.0 + jnp.exp(-jnp.abs(x)))


def _attn_prep_fwd(q, k, f, b_forget, q_gain, k_gain, bsz, seq):
    ts = ATTN_PREP_ROWS
    ns = seq // ts
    aw = ATTN_WIDTH
    t = bsz * seq
    seg = _seg_ones(aw, HEAD_DIM)
    tri = _tri_ones(ts, True)

    def body(q_ref, k_ref, f_ref, bf_ref, gq_ref, gk_ref, seg_ref, tri_ref, pq_ref, pk_ref, qp_ref, kp_ref, carry):
        s = pl.program_id(1)

        @pl.when(s == 0)
        def _():
            carry[...] = jnp.zeros_like(carry)

        logf = _log_sigmoid(f_ref[...] + bf_ref[...])
        hi, mid, lo = _split3(logf)
        tri_v = tri_ref[...]
        fc = _dot(tri_v, hi) + _dot(tri_v, mid) + _dot(tri_v, lo) + carry[pl.ds(0, 1), :]
        carry[pl.ds(0, 1), :] = fc[ts - 1:, :]
        pcs = jnp.concatenate(_split3(fc), axis=1)
        lane = lax.broadcasted_iota(jnp.int32, (1, LANES), 1)
        ones_q = jnp.logical_and(lane >= AUX_ONE, lane < AUX_ONE + 3).astype(F32)
        ones_k = jnp.logical_and(lane >= AUX_F, lane < AUX_F + 3).astype(F32)
        seg_v = seg_ref[...]

        def build(x_ref, g_ref, scale, out_ref, ones, place_ref, f_sign):
            xv = x_ref[...]
            r = lax.rsqrt(_head_sums(xv * xv, seg_v) * (1.0 / HEAD_DIM) + EPS)
            xn = xv * r * g_ref[...] * scale
            aux = _dot(pcs, place_ref[...]) * f_sign
            for h in range(N_HEADS):
                pair = xn[:, (h // 2) * LANES:(h // 2 + 1) * LANES]
                feat = pair if h % 2 == 0 else pltpu.roll(pair, HEAD_DIM, 1)
                aux_h = aux[:, h * LANES:(h + 1) * LANES] + ones
                out_ref[:, h * LANES:(h + 1) * LANES] = jnp.where(lane < HEAD_DIM, feat, aux_h).astype(BF16)

        build(q_ref, gq_ref, 0.125, qp_ref, ones_q, pq_ref, 1.0)
        build(k_ref, gk_ref, 1.0, kp_ref, ones_k, pk_ref, -1.0)

    blk = pl.BlockSpec((ts, aw), lambda b, s: (b * ns + s, 0))
    fblk = pl.BlockSpec((ts, LANES), lambda b, s: (b * ns + s, 0))
    oblk = pl.BlockSpec((ts, N_HEADS * LANES), lambda b, s: (b * ns + s, 0))
    const = lambda shape: pl.BlockSpec(shape, lambda b, s: (0, 0))
    return pl.pallas_call(
        body, name="attn_prep_fwd", grid=(bsz, ns),
        in_specs=[blk, blk, fblk, const((1, LANES)), const((1, aw)), const((1, aw)), const((aw, aw)), const((ts, ts)),
                  const((3 * LANES, N_HEADS * LANES)), const((3 * LANES, N_HEADS * LANES))],
        out_specs=[oblk, oblk],
        out_shape=[jax.ShapeDtypeStruct((t, N_HEADS * LANES), BF16)] * 2,
        scratch_shapes=[pltpu.VMEM((8, LANES), F32)],
        compiler_params=_params("arbitrary", "arbitrary"),
    )(q, k, f, b_forget, q_gain, k_gain, seg, tri, _place_pieces(AUX_F), _place_pieces(AUX_ONE))


def _attn_prep_bwd(dqp, dkp, q, k, f, b_forget, q_gain, k_gain, bsz, seq):
    ts = ATTN_PREP_ROWS
    ns = seq // ts
    aw = ATTN_WIDTH
    t = bsz * seq
    seg = _seg_ones(aw, HEAD_DIM)
    tri = _tri_ones(ts, False)

    def body(dqp_ref, dkp_ref, q_ref, k_ref, f_ref, bf_ref, gq_ref, gk_ref, seg_ref, tri_ref,
             dq_ref, dk_ref, df_ref, dgq_ref, dgk_ref, dbf_ref, carry):
        b = pl.program_id(0)
        sr = pl.program_id(1)

        @pl.when(jnp.logical_and(b == 0, sr == 0))
        def _():
            dgq_ref[...] = jnp.zeros_like(dgq_ref)
            dgk_ref[...] = jnp.zeros_like(dgk_ref)
            dbf_ref[...] = jnp.zeros_like(dbf_ref)

        @pl.when(sr == 0)
        def _():
            carry[...] = jnp.zeros_like(carry)

        lane = lax.broadcasted_iota(jnp.int32, (1, LANES), 1)
        seg_v = seg_ref[...]

        def norm_bwd(dp_ref, x_ref, g_ref, scale, dx_ref, dgain_ref):
            parts = []
            for j in range(N_HEADS // 2):
                even = dp_ref[:, (2 * j) * LANES:(2 * j + 1) * LANES]
                odd = dp_ref[:, (2 * j + 1) * LANES:(2 * j + 2) * LANES]
                parts.append(jnp.where(lane < HEAD_DIM, even, pltpu.roll(odd, HEAD_DIM, 1)))
            dxn = jnp.concatenate(parts, axis=1) * scale
            xv = x_ref[...]
            r = lax.rsqrt(_head_sums(xv * xv, seg_v) * (1.0 / HEAD_DIM) + EPS)
            n = xv * r
            dgain_ref[...] += jnp.sum(dxn * n, axis=0, keepdims=True)
            dn = dxn * g_ref[...]
            m = _head_sums(dn * n, seg_v) * (1.0 / HEAD_DIM)
            dx_ref[...] = (r * (dn - n * m)).astype(BF16)

        norm_bwd(dqp_ref, q_ref, gq_ref, 0.125, dq_ref, dgq_ref)
        norm_bwd(dkp_ref, k_ref, gk_ref, 1.0, dk_ref, dgk_ref)

        dfc = jnp.zeros((ts, LANES), F32)
        for h in range(N_HEADS):
            cols = pl.ds(h * LANES, LANES)
            both = jnp.where(lane == AUX_F, dqp_ref[:, cols], 0.0) - jnp.where(lane == AUX_ONE, dkp_ref[:, cols], 0.0)
            dfc = jnp.where(lane == h, jnp.sum(both, axis=1, keepdims=True), dfc)
        hi, mid, lo = _split3(dfc)
        tri_v = tri_ref[...]
        dlogf = _dot(tri_v, hi) + _dot(tri_v, mid) + _dot(tri_v, lo) + carry[pl.ds(0, 1), :]
        carry[pl.ds(0, 1), :] = dlogf[0:1, :]
        df = jnp.where(lane < N_HEADS, dlogf * jax.nn.sigmoid(-(f_ref[...] + bf_ref[...])), 0.0)
        df_ref[...] = df.astype(BF16)
        dbf_ref[...] += jnp.sum(df, axis=0, keepdims=True)

    rev = lambda b, s: (b * ns + (ns - 1 - s), 0)
    blk = pl.BlockSpec((ts, aw), rev)
    fblk = pl.BlockSpec((ts, LANES), rev)
    pblk = pl.BlockSpec((ts, N_HEADS * LANES), rev)
    const = lambda shape: pl.BlockSpec(shape, lambda b, s: (0, 0))
    return pl.pallas_call(
        body, name="attn_prep_bwd", grid=(bsz, ns),
        in_specs=[pblk, pblk, blk, blk, fblk, const((1, LANES)), const((1, aw)), const((1, aw)), const((aw, aw)),
                  const((ts, ts))],
        out_specs=[blk, blk, fblk, const((1, aw)), const((1, aw)), const((1, LANES))],
        out_shape=[jax.ShapeDtypeStruct((t, aw), BF16), jax.ShapeDtypeStruct((t, aw), BF16),
                   jax.ShapeDtypeStruct((t, LANES), BF16), jax.ShapeDtypeStruct((1, aw), F32),
                   jax.ShapeDtypeStruct((1, aw), F32), jax.ShapeDtypeStruct((1, LANES), F32)],
        scratch_shapes=[pltpu.VMEM((8, LANES), F32)],
        compiler_params=_params("arbitrary", "arbitrary"),
    )(dqp, dkp, q, k, f, b_forget, q_gain, k_gain, seg, tri)


ATTN_BLOCK = 512
HEAD_PAIRS = N_HEADS // 2


def _flash_fwd(qp, kp, v, bsz, seq):
    tq = ATTN_BLOCK
    nq = seq // tq
    t = bsz * seq

    def body(q_ref, k_ref, v_ref, o_ref, lse_ref, m_sc, l_sc, acc_sc):
        i = pl.program_id(2)
        m_sc[...] = jnp.full(m_sc.shape, -jnp.inf, F32)
        l_sc[...] = jnp.zeros_like(l_sc)
        acc_sc[...] = jnp.zeros_like(acc_sc)
        lane = lax.broadcasted_iota(jnp.int32, (1, LANES), 1)
        low = lane < HEAD_DIM

        def key_block(j, masked):
            rows = pl.ds(pl.multiple_of(j * tq, tq), tq)
            vv = v_ref[rows, :]
            for h in range(2):
                mine = low if h == 0 else jnp.logical_not(low)
                s = _dot_nt(q_ref[:, h * LANES:(h + 1) * LANES], k_ref[rows, pl.ds(h * LANES, LANES)])
                if masked:
                    row = lax.broadcasted_iota(jnp.int32, (tq, tq), 0)
                    col = lax.broadcasted_iota(jnp.int32, (tq, tq), 1)
                    s = jnp.where(row >= col, s, -jnp.inf)
                m_prev = m_sc[h]
                m_new = jnp.maximum(m_prev, jnp.max(s, axis=1, keepdims=True))
                p = jnp.exp(s - jnp.tile(m_new, (1, tq // LANES)))
                alpha = jnp.exp(m_prev - m_new)
                l_sc[h] = alpha * l_sc[h] + jnp.sum(p, axis=1, keepdims=True)
                m_sc[h] = m_new
                pv = _dot(p.astype(BF16), jnp.where(mine, vv, jnp.zeros_like(vv)))
                acc_sc[...] = acc_sc[...] * jnp.where(mine, alpha, 1.0) + pv

        def below_diagonal(j, carry):
            key_block(j, False)
            return carry

        lax.fori_loop(0, i, below_diagonal, 0)
        key_block(i, True)
        l = jnp.where(low, l_sc[0], l_sc[1])
        m = jnp.where(low, m_sc[0], m_sc[1])
        o_ref[...] = acc_sc[...] / l
        lse_ref[...] = m + jnp.log(l)

    qspec = pl.BlockSpec((tq, 2 * LANES), lambda b, hp, i: (b * nq + i, hp))
    kspec = pl.BlockSpec((seq, 2 * LANES), lambda b, hp, i: (b, hp))
    vspec = pl.BlockSpec((seq, LANES), lambda b, hp, i: (b, hp))
    ospec = pl.BlockSpec((tq, LANES), lambda b, hp, i: (b * nq + i, hp))
    return pl.pallas_call(
        body, name="flash_fwd", grid=(bsz, HEAD_PAIRS, nq),
        in_specs=[qspec, kspec, vspec], out_specs=[ospec, ospec],
        out_shape=[jax.ShapeDtypeStruct((t, ATTN_WIDTH), F32), jax.ShapeDtypeStruct((t, ATTN_WIDTH), F32)],
        scratch_shapes=[pltpu.VMEM((2, tq, LANES), F32), pltpu.VMEM((2, tq, LANES), F32), pltpu.VMEM((tq, LANES), F32)],
        compiler_params=_params("arbitrary", "arbitrary", "arbitrary"),
    )(qp, kp, v)


def _flash_bwd(qp, kp, v, o, do, lse, bsz, seq):
    tq = ATTN_BLOCK
    nq = seq // tq
    t = bsz * seq

    def body(q_ref, k_ref, v_ref, o_ref, do_ref, lse_ref, dq_ref, dk_ref, dv_ref, dk_acc, dv_acc):
        j = pl.program_id(2)

        @pl.when(j == 0)
        def _():
            dq_ref[...] = jnp.zeros_like(dq_ref)

        dk_acc[...] = jnp.zeros_like(dk_acc)
        dv_acc[...] = jnp.zeros_like(dv_acc)
        lane = lax.broadcasted_iota(jnp.int32, (1, LANES), 1)
        low = lane < HEAD_DIM

        def query_block(i, masked):
            rows = pl.ds(pl.multiple_of(i * tq, tq), tq)
            dov = do_ref[rows, :]
            dd = dov * o_ref[rows, :]
            dob = dov.astype(BF16)
            vv = v_ref[...]
            lse_v = lse_ref[rows, :]
            for h in range(2):
                mine = low if h == 0 else jnp.logical_not(low)
                cols = pl.ds(h * LANES, LANES)
                qh = q_ref[rows, cols]
                kh = k_ref[:, cols]
                s = _dot_nt(qh, kh)
                lse_h = jnp.where(mine, lse_v, pltpu.roll(lse_v, HEAD_DIM, 1))
                p = jnp.exp(s - jnp.tile(lse_h, (1, tq // LANES)))
                if masked:
                    row = lax.broadcasted_iota(jnp.int32, (tq, tq), 0)
                    col = lax.broadcasted_iota(jnp.int32, (tq, tq), 1)
                    p = jnp.where(row >= col, p, 0.0)
                delta = jnp.sum(jnp.where(mine, dd, 0.0), axis=1, keepdims=True)
                dp = _dot_nt(dob, jnp.where(mine, vv, jnp.zeros_like(vv)))
                ds = (p * (dp - delta)).astype(BF16)
                dv_acc[...] += jnp.where(mine, _dot_tn(p.astype(BF16), dob), 0.0)
                dk_acc[:, cols] += _dot_tn(ds, qh)
                dq_ref[rows, cols] += _dot(ds, kh)

        def above_diagonal(i, carry):
            query_block(i, False)
            return carry

        query_block(j, True)
        lax.fori_loop(j + 1, nq, above_diagonal, 0)
        dk_ref[...] = dk_acc[...]
        dv_ref[...] = dv_acc[...].astype(BF16)

    qspec = pl.BlockSpec((seq, 2 * LANES), lambda b, hp, j: (b, hp))
    kspec = pl.BlockSpec((tq, 2 * LANES), lambda b, hp, j: (b * nq + j, hp))
    vspec = pl.BlockSpec((tq, LANES), lambda b, hp, j: (b * nq + j, hp))
    ospec = pl.BlockSpec((seq, LANES), lambda b, hp, j: (b, hp))
    return pl.pallas_call(
        body, name="flash_bwd", grid=(bsz, HEAD_PAIRS, nq),
        in_specs=[qspec, kspec, vspec, ospec, ospec, ospec], out_specs=[qspec, kspec, vspec],
        out_shape=[jax.ShapeDtypeStruct((t, N_HEADS * LANES), F32), jax.ShapeDtypeStruct((t, N_HEADS * LANES), F32),
                   jax.ShapeDtypeStruct((t, ATTN_WIDTH), BF16)],
        scratch_shapes=[pltpu.VMEM((tq, 2 * LANES), F32), pltpu.VMEM((tq, LANES), F32)],
        compiler_params=_params("arbitrary", "arbitrary", "arbitrary"),
    )(qp, kp, v, o, do, lse)


def _mix_out_fwd(o, y_pool, x, gain, w_out):
    t, d = x.shape
    tm = 1024
    pw, aw = POOL_WIDTH, ATTN_WIDTH

    def body(o_ref, yp_ref, x_ref, g_ref, w_ref, ycat_ref, y_ref):
        ov = o_ref[...]
        ya = (ov * _rms_scale(ov) * g_ref[...]).astype(BF16)
        ycat = jnp.concatenate([yp_ref[...], ya], axis=1)
        ycat_ref[...] = ycat
        y_ref[...] = x_ref[...] + _dot(ycat, w_ref[...])

    return pl.pallas_call(
        body, name="mix_out_fwd", grid=(t // tm,),
        in_specs=[_rows(tm, aw), _rows(tm, pw), _rows(tm, d), _resident((1, aw)), _resident((pw + aw, d))],
        out_specs=[_rows(tm, pw + aw), _rows(tm, d)],
        out_shape=[jax.ShapeDtypeStruct((t, pw + aw), BF16), jax.ShapeDtypeStruct((t, d), F32)],
        compiler_params=_params("arbitrary"),
    )(o, y_pool, x, gain, w_out)


def _mix_out_bwd(dx, o, gain, w_out):
    t, d = dx.shape
    tm = 1024
    pw, aw = POOL_WIDTH, ATTN_WIDTH

    def body(dx_ref, o_ref, g_ref, w_ref, dxb_ref, dyp_ref, do_ref, dg_ref):
        dxb = dx_ref[...].astype(BF16)
        dxb_ref[...] = dxb
        dyp_ref[...] = _dot_nt(dxb, w_ref[pl.ds(0, pw), :])
        dya = _dot_nt(dxb, w_ref[pl.ds(pw, aw), :])
        do, dgain = _rms_bwd(dya, o_ref[...], g_ref[...])
        do_ref[...] = do

        @pl.when(pl.program_id(0) == 0)
        def _():
            dg_ref[...] = jnp.zeros_like(dg_ref)

        dg_ref[...] += dgain

    return pl.pallas_call(
        body, name="mix_out_bwd", grid=(t // tm,),
        in_specs=[_rows(tm, d), _rows(tm, aw), _resident((1, aw)), _resident((pw + aw, d))],
        out_specs=[_rows(tm, d), _rows(tm, pw), _rows(tm, aw), pl.BlockSpec((1, aw), lambda i: (0, 0))],
        out_shape=[jax.ShapeDtypeStruct((t, d), BF16), jax.ShapeDtypeStruct((t, pw), F32),
                   jax.ShapeDtypeStruct((t, aw), F32), jax.ShapeDtypeStruct((1, aw), F32)],
        compiler_params=_params("arbitrary"),
    )(dx, o, gain, w_out)


def _mix_in_bwd(dpv, dq, dk, dv, df, x, dx_res, gain, w_in_t):
    t, d = x.shape
    tm = 512
    pw, aw = POOL_WIDTH, ATTN_WIDTH

    def body(dpv_ref, dq_ref, dk_ref, dv_ref, df_ref, x_ref, dxr_ref, g_ref, w_ref, dh_ref, dx_ref, dxh_ref, dg_ref):
        dh = jnp.concatenate([dpv_ref[...], dq_ref[...], dk_ref[...], dv_ref[...], df_ref[...]], axis=1)
        dh_ref[...] = dh
        dhm = _dot(dh, w_ref[...])
        dx, dgain = _rms_bwd(dhm, x_ref[...], g_ref[...])
        dx = dxr_ref[...] + dx
        dx_ref[...] = dx
        dxh_ref[...] = (0.5 * dx).astype(BF16)

        @pl.when(pl.program_id(0) == 0)
        def _():
            dg_ref[...] = jnp.zeros_like(dg_ref)

        dg_ref[...] += dgain

    return pl.pallas_call(
        body, name="mix_in_bwd", grid=(t // tm,),
        in_specs=[_rows(tm, pw), _rows(tm, aw), _rows(tm, aw), _rows(tm, aw), _rows(tm, LANES), _rows(tm, d),
                  _rows(tm, d), _resident((1, d)), _resident((MIX_PAD, d))],
        out_specs=[_rows(tm, MIX_PAD), _rows(tm, d), _rows(tm, d), pl.BlockSpec((1, d), lambda i: (0, 0))],
        out_shape=[jax.ShapeDtypeStruct((t, MIX_PAD), BF16), jax.ShapeDtypeStruct((t, d), F32),
                   jax.ShapeDtypeStruct((t, d), BF16), jax.ShapeDtypeStruct((1, d), F32)],
        compiler_params=_params("arbitrary"),
    )(dpv, dq, dk, dv, df, x, dx_res, gain, w_in_t)


MESH_IDS = pl.DeviceIdType.MESH


def _me():
    return lax.axis_index("x"), lax.axis_index("y"), lax.axis_index("c")


def _peer(x, y, c, p):
    px = 1 - x if p & 4 else x
    py = 1 - y if p & 2 else y
    pc = 1 - c if p & 1 else c
    return (px, py, pc), 4 * px + 2 * py + pc


HBM_SPEC = pl.BlockSpec(memory_space=pltpu.HBM)
SEM_SPEC = pl.BlockSpec(memory_space=pltpu.SEMAPHORE)
SPLIT_COPY = pltpu.CompilerParams(has_side_effects=pltpu.SideEffectType.DATAFLOW_SIDE_EFFECTING)
PEERS = N_DEV - 1


def _hbm(a):
    return pltpu.with_memory_space_constraint(a, pltpu.HBM)


def _row_block(ref, dev, rows):
    return ref.at[pl.ds(pl.multiple_of(dev * rows, BF16_ROWS), rows)]


def _copy_ends(gather, src, land, me, peer_id):
    if gather:
        rows = src.shape[0]
        return src, _row_block(land, me, rows), _row_block(land, peer_id, rows), src, _row_block(land, me, rows)
    rows = src.shape[0] // N_DEV
    return (_row_block(src, peer_id, rows), land.at[me], land.at[peer_id], _row_block(src, me, rows), land.at[me])


def _land_shape(gather, s):
    return (N_DEV * s.shape[0], s.shape[1]) if gather else (N_DEV, s.shape[0] // N_DEV, s.shape[1])


SIBLING = 1
SAME_CORE_PEERS = (2, 4, 6)
RELAYS = len(SAME_CORE_PEERS)


def _copies_start(groups, gather, name, after=None, relayed=()):
    flat = [s for g in groups for s in g]
    n, ng = len(flat), len(groups)
    lands = [lax.empty(_land_shape(gather, s), s.dtype) for s in flat]
    n_in = 2 * n + (after is not None)

    def body(*refs):
        ins, lnd = refs[:n], refs[n:2 * n]
        sems = refs[n_in:n_in + 2 * ng]
        token = refs[-1]
        x, y, c = _me()
        me = 4 * x + 2 * y + c
        w = 0
        for gi, g in enumerate(groups):
            for k in range(len(g)):
                for p in ((SIBLING,) + SAME_CORE_PEERS if gi in relayed else range(1, N_DEV)):
                    peer, peer_id = _peer(x, y, c, p)
                    src, dst, _, _, _ = _copy_ends(gather, ins[w], lnd[w], me, peer_id)
                    pltpu.make_async_remote_copy(src, dst, sems[2 * gi].at[k * PEERS + p - 1],
                                                 sems[2 * gi + 1].at[k * PEERS + p - 1], device_id=peer,
                                                 device_id_type=MESH_IDS).start()
                w += 1
        token[...] = jnp.zeros_like(token)

    sem_shapes = []
    for g in groups:
        sem_shapes += [pltpu.SemaphoreType.DMA((len(g) * PEERS,))] * 2
    out = pl.pallas_call(
        body, name=name,
        out_shape=(*sem_shapes, *[pltpu.HBM(s.shape, s.dtype) for s in flat],
                   *[pltpu.HBM(l.shape, l.dtype) for l in lands], jax.ShapeDtypeStruct((8, LANES), F32)),
        in_specs=[HBM_SPEC] * (2 * n) + [pl.BlockSpec(memory_space=pl.ANY)] * (after is not None),
        out_specs=(*[SEM_SPEC] * (2 * ng), *[HBM_SPEC] * (2 * n), pl.BlockSpec(memory_space=pltpu.VMEM)),
        input_output_aliases={i: 2 * ng + i for i in range(2 * n)},
        compiler_params=SPLIT_COPY,
    )(*[_hbm(s) for s in flat], *[_hbm(l) for l in lands], *([after] if after is not None else []))
    sems, thru, token = out[:2 * ng], out[2 * ng:2 * ng + 2 * n], out[-1]
    res, w = [], 0
    for gi, g in enumerate(groups):
        res.append((sems[2 * gi], sems[2 * gi + 1], list(thru[w:w + len(g)]), list(thru[n + w:n + w + len(g)])))
        w += len(g)
    return res, token


def _copies_wait(started, gather, after, name):
    send, recv, srcs, lands = started
    n = len(srcs)
    after = list(after) if isinstance(after, (list, tuple)) else [after]

    own_shapes = [s.shape if gather else (s.shape[0] // N_DEV, s.shape[1]) for s in srcs]

    def body(*refs):
        ins, lnd = refs[:n], refs[n:2 * n]
        send_sems, recv_sems = refs[2 * n], refs[2 * n + 1]
        bounce, in_sems, out_sems = refs[-n - 2:-2], refs[-2], refs[-1]
        x, y, c = _me()
        me = 4 * x + 2 * y + c
        ends = [_copy_ends(gather, ins[w], lnd[w], me, me)[3:] for w in range(n)]
        loads = [pltpu.make_async_copy(ends[w][0], bounce[w], in_sems.at[w]) for w in range(n)]
        stores = [pltpu.make_async_copy(bounce[w], ends[w][1], out_sems.at[w]) for w in range(n)]
        for cp in loads:
            cp.start()
        for w in range(n):
            loads[w].wait()
            stores[w].start()
        for w in range(n):
            for p in range(1, N_DEV):
                peer, peer_id = _peer(x, y, c, p)
                src, _, arrival, _, _ = _copy_ends(gather, ins[w], lnd[w], me, peer_id)
                cp = pltpu.make_async_remote_copy(src, arrival, send_sems.at[w * PEERS + p - 1],
                                                  recv_sems.at[w * PEERS + p - 1], device_id=peer,
                                                  device_id_type=MESH_IDS)
                cp.wait_send()
                cp.wait_recv()
        for cp in stores:
            cp.wait()

    out = pl.pallas_call(
        body, name=name,
        out_shape=(*[pltpu.HBM(s.shape, s.dtype) for s in srcs], *[pltpu.HBM(l.shape, l.dtype) for l in lands]),
        in_specs=[HBM_SPEC] * (2 * n) + [SEM_SPEC, SEM_SPEC] + [pl.BlockSpec(memory_space=pl.ANY)] * len(after),
        out_specs=[HBM_SPEC] * (2 * n),
        input_output_aliases={i: i for i in range(2 * n)},
        scratch_shapes=[*[pltpu.VMEM(shape, s.dtype) for shape, s in zip(own_shapes, srcs)],
                        pltpu.SemaphoreType.DMA((n,)), pltpu.SemaphoreType.DMA((n,))],
        compiler_params=SPLIT_COPY,
    )(*srcs, *lands, send, recv, *after)
    return list(out[n:])


def _relay_to_sibling(started, name):
    send, recv, srcs, lands = started
    n = len(srcs)

    def body(*refs):
        ins, lnd = refs[:n], refs[n:2 * n]
        send_sems, recv_sems = refs[2 * n], refs[2 * n + 1]
        relay_send, relay_recv = refs[2 * n + 2], refs[2 * n + 3]
        x, y, c = _me()
        sibling, _ = _peer(x, y, c, SIBLING)
        for w in range(n):
            rows = ins[w].shape[0]
            for k, p in enumerate(SAME_CORE_PEERS):
                peer, peer_id = _peer(x, y, c, p)
                arrived = _row_block(lnd[w], peer_id, rows)
                first = pltpu.make_async_remote_copy(ins[w], arrived, send_sems.at[w * PEERS + p - 1],
                                                     recv_sems.at[w * PEERS + p - 1], device_id=peer,
                                                     device_id_type=MESH_IDS)
                first.wait_recv()
                pltpu.make_async_remote_copy(arrived, arrived, relay_send.at[w * RELAYS + k],
                                             relay_recv.at[w * RELAYS + k], device_id=sibling,
                                             device_id_type=MESH_IDS).start()
                first.wait_send()

    sems = pltpu.SemaphoreType.DMA((n * RELAYS,))
    out = pl.pallas_call(
        body, name=name,
        out_shape=(sems, sems, *[pltpu.HBM(s.shape, s.dtype) for s in srcs], *[pltpu.HBM(l.shape, l.dtype) for l in lands]),
        in_specs=[HBM_SPEC] * (2 * n) + [SEM_SPEC, SEM_SPEC],
        out_specs=(SEM_SPEC, SEM_SPEC, *[HBM_SPEC] * (2 * n)),
        input_output_aliases={i: 2 + i for i in range(2 * n)},
        compiler_params=SPLIT_COPY,
    )(*srcs, *lands, send, recv)
    return send, recv, out[0], out[1], list(out[2:2 + n]), list(out[2 + n:])


def _relayed_wait(relayed, after, name):
    send, recv, relay_send, relay_recv, srcs, lands = relayed
    n = len(srcs)
    after = list(after) if isinstance(after, (list, tuple)) else [after]

    def body(*refs):
        ins, lnd = refs[:n], refs[n:2 * n]
        send_sems, recv_sems, relay_send_sems, relay_recv_sems = refs[2 * n:2 * n + 4]
        bounce, in_sems, out_sems = refs[-n - 2:-2], refs[-2], refs[-1]
        x, y, c = _me()
        me = 4 * x + 2 * y + c
        sibling, sibling_id = _peer(x, y, c, SIBLING)
        loads = [pltpu.make_async_copy(ins[w], bounce[w], in_sems.at[w]) for w in range(n)]
        stores = [pltpu.make_async_copy(bounce[w], _row_block(lnd[w], me, ins[w].shape[0]), out_sems.at[w])
                  for w in range(n)]
        for cp in loads:
            cp.start()
        for w in range(n):
            loads[w].wait()
            stores[w].start()
        for w in range(n):
            rows = ins[w].shape[0]
            direct = pltpu.make_async_remote_copy(ins[w], _row_block(lnd[w], sibling_id, rows),
                                                  send_sems.at[w * PEERS + SIBLING - 1],
                                                  recv_sems.at[w * PEERS + SIBLING - 1], device_id=sibling,
                                                  device_id_type=MESH_IDS)
            direct.wait_send()
            direct.wait_recv()
            for k, p in enumerate(SAME_CORE_PEERS):
                _, sent_id = _peer(x, y, c, p)
                _, got_id = _peer(x, y, c, p + SIBLING)
                relay = pltpu.make_async_remote_copy(_row_block(lnd[w], sent_id, rows), _row_block(lnd[w], got_id, rows),
                                                     relay_send_sems.at[w * RELAYS + k],
                                                     relay_recv_sems.at[w * RELAYS + k], device_id=sibling,
                                                     device_id_type=MESH_IDS)
                relay.wait_send()
                relay.wait_recv()
        for cp in stores:
            cp.wait()

    out = pl.pallas_call(
        body, name=name,
        out_shape=(*[pltpu.HBM(s.shape, s.dtype) for s in srcs], *[pltpu.HBM(l.shape, l.dtype) for l in lands]),
        in_specs=[HBM_SPEC] * (2 * n) + [SEM_SPEC] * 4 + [pl.BlockSpec(memory_space=pl.ANY)] * len(after),
        out_specs=[HBM_SPEC] * (2 * n),
        input_output_aliases={i: i for i in range(2 * n)},
        scratch_shapes=[*[pltpu.VMEM(s.shape, s.dtype) for s in srcs],
                        pltpu.SemaphoreType.DMA((n,)), pltpu.SemaphoreType.DMA((n,))],
        compiler_params=SPLIT_COPY,
    )(*srcs, *lands, send, recv, relay_send, relay_recv, *after)
    return list(out[n:])


def _adamw_update(w, g, m, v):
    nm = ADAM_B1 * m + (1.0 - ADAM_B1) * g
    nv = ADAM_B2 * v + (1.0 - ADAM_B2) * (g * g)
    m_hat = nm / (1.0 - ADAM_B1 ** ADAM_STEP)
    v_hat = nv / (1.0 - ADAM_B2 ** ADAM_STEP)
    return -ADAM_LR * (m_hat / (jnp.sqrt(v_hat) + ADAM_EPS) + ADAM_WD * w), nm, nv


SUM_ADAMW_COLS = 256


def _sum_adamw(parts, w, m, v, turn_grad, name):
    _, rows, d = parts.shape
    n = w.shape[0]
    tc = SUM_ADAMW_COLS

    def body(p_ref, w_ref, m_ref, v_ref, g_ref, d_ref, nm_ref, nv_ref):
        g = p_ref[0].astype(F32)
        for dev in range(1, N_DEV):
            g = g + p_ref[dev].astype(F32)
        g = g[:n]
        if turn_grad:
            eye = (lax.broadcasted_iota(jnp.int32, (n, n), 0) == lax.broadcasted_iota(jnp.int32, (n, n), 1)).astype(BF16)
            hi, mid, lo = _split3(g)
            g_ref[...] = _dot_tn(hi, eye) + _dot_tn(mid, eye) + _dot_tn(lo, eye)
        else:
            g_ref[...] = g
        d_ref[...], nm_ref[...], nv_ref[...] = _adamw_update(w_ref[...], g, m_ref[...], v_ref[...])

    spec = pl.BlockSpec((n, tc), lambda j: (0, j))
    shape = jax.ShapeDtypeStruct((n, d), F32)
    g_spec = pl.BlockSpec((tc, n), lambda j: (j, 0)) if turn_grad else spec
    g_shape = jax.ShapeDtypeStruct((d, n), F32) if turn_grad else shape
    return pl.pallas_call(
        body, name=name, grid=(d // tc,),
        in_specs=[pl.BlockSpec((N_DEV, rows, tc), lambda j: (0, 0, j)), spec, spec, spec],
        out_specs=[g_spec, spec, spec, spec], out_shape=[g_shape, shape, shape, shape],
        compiler_params=_params("arbitrary"),
    )(parts, w, m, v)


def _pad_rows(a, rows):
    return jnp.pad(a, ((0, rows - a.shape[0]), (0, 0)))


def _row1(vec, width=D_MODEL):
    return jnp.pad(vec.reshape(1, -1), ((0, 0), (0, width - vec.shape[-1])))


COLUMN_SHARDED = ("ffn1_w_gate", "ffn1_w_up", "w_in", "ffn2_w_gate", "ffn2_w_up")
VEC_NAMES = ("ffn1_norm", "mix_norm", "ffn2_norm", "b_forget", "pool_scale", "q_norm", "k_norm", "out_norm_pool",
             "out_norm_attn")
VEC_ROWS = 16
LOSS_ROW = len(VEC_NAMES)


def _pack_vector_grads(parts, loss_part):
    def body(*refs):
        loss_ref, out_ref = refs[-2], refs[-1]
        out_ref[...] = jnp.zeros_like(out_ref)
        lane = lax.broadcasted_iota(jnp.int32, (1, LANES), 1)
        for i, (name, ref) in enumerate(zip(VEC_NAMES, refs[:-2])):
            val = ref[...]
            if name in ("q_norm", "k_norm"):
                val = val[:, 0:LANES] + val[:, LANES:2 * LANES] + val[:, 2 * LANES:3 * LANES] + val[:, 3 * LANES:]
                val = jnp.where(lane < HEAD_DIM, val + pltpu.roll(val, HEAD_DIM, 1), 0.0)
            out_ref[pl.ds(i, 1), pl.ds(0, val.shape[1])] = val
        out_ref[pl.ds(LOSS_ROW, 1), pl.ds(0, 1)] = loss_ref[...]

    vmem = pl.BlockSpec(memory_space=pltpu.VMEM)
    return pl.pallas_call(
        body, name="pack_vector_grads", in_specs=[vmem] * (len(parts) + 1), out_specs=vmem,
        out_shape=jax.ShapeDtypeStruct((VEC_ROWS, D_MODEL), F32),
    )(*parts, loss_part)


def _small_adamw(vec_all, pool_all, vec_params, pool_params):
    nv = len(vec_params)
    pool_rows = pool_params[0].shape[0]

    def body(*refs):
        vec_ref, pool_ref = refs[0], refs[1]
        ins = refs[2:2 + 3 * nv + 3]
        outs = refs[2 + 3 * nv + 3:-1]
        rows = refs[-1]
        total = vec_ref[pl.ds(0, VEC_ROWS), :]
        for dev in range(1, N_DEV):
            total = total + vec_ref[pl.ds(dev * VEC_ROWS, VEC_ROWS), :]
        rows[...] = total
        outs[4 * nv + 4][...] = rows[pl.ds(LOSS_ROW, 1), pl.ds(0, 1)]
        for i in range(nv):
            w_ref, m_ref, v_ref = ins[3 * i:3 * i + 3]
            g = rows[pl.ds(i, 1), pl.ds(0, w_ref.shape[1])]
            outs[4 * i][...] = g
            outs[4 * i + 1][...], outs[4 * i + 2][...], outs[4 * i + 3][...] = _adamw_update(
                w_ref[...], g, m_ref[...], v_ref[...])
        g = pool_ref[pl.ds(0, pool_rows), :]
        for dev in range(1, N_DEV):
            g = g + pool_ref[pl.ds(dev * pool_rows, pool_rows), :]
        w_ref, m_ref, v_ref = ins[3 * nv:]
        outs[4 * nv][...] = g
        outs[4 * nv + 1][...], outs[4 * nv + 2][...], outs[4 * nv + 3][...] = _adamw_update(
            w_ref[...], g, m_ref[...], v_ref[...])

    vmem = pl.BlockSpec(memory_space=pltpu.VMEM)
    flat = [a for trio in vec_params for a in trio] + list(pool_params)
    out_shape = []
    for trio in list(vec_params) + [pool_params]:
        out_shape += [jax.ShapeDtypeStruct(trio[0].shape, F32)] * 4
    out_shape.append(jax.ShapeDtypeStruct((1, 1), F32))
    return pl.pallas_call(
        body, name="adamw_small", in_specs=[vmem] * (2 + len(flat)), out_specs=[vmem] * len(out_shape),
        out_shape=out_shape, scratch_shapes=[pltpu.VMEM((VEC_ROWS, D_MODEL), F32)],
    )(vec_all, pool_all, *flat)


def kernel(x, ffn1_norm, ffn1_w_gate, ffn1_w_up, ffn1_w_down, mix_norm, w_in, b_forget, pool_w, pool_scale, q_norm, k_norm, out_norm_pool, out_norm_attn, w_out, ffn2_norm, ffn2_w_gate, ffn2_w_up, ffn2_w_down, loss_target, m_ffn1_norm, m_ffn1_w_gate, m_ffn1_w_up, m_ffn1_w_down, m_mix_norm, m_w_in, m_b_forget, m_pool_w, m_pool_scale, m_q_norm, m_k_norm, m_out_norm_pool, m_out_norm_attn, m_w_out, m_ffn2_norm, m_ffn2_w_gate, m_ffn2_w_up, m_ffn2_w_down, v_ffn1_norm, v_ffn1_w_gate, v_ffn1_w_up, v_ffn1_w_down, v_mix_norm, v_w_in, v_b_forget, v_pool_w, v_pool_scale, v_q_norm, v_k_norm, v_out_norm_pool, v_out_norm_attn, v_w_out, v_ffn2_norm, v_ffn2_w_gate, v_ffn2_w_up, v_ffn2_w_down):
    bsz, seq, d = x.shape
    t = bsz * seq
    x0 = x.reshape(t, d)
    target = loss_target.reshape(t, d)
    in_rows = -(-w_in.shape[1] // BF16_ROWS) * BF16_ROWS

    slabs = [s.astype(BF16) for s in (ffn1_w_gate.T, ffn1_w_up.T, ffn1_w_down, _pad_rows(w_in.T, in_rows), w_out,
                                       ffn2_w_gate.T, ffn2_w_up.T, ffn2_w_down)]
    gathers, started = _copies_start([slabs[0:2], slabs[2:3], slabs[3:4], slabs[4:5], slabs[5:8]], True, "gather_start",
                                     relayed=(0,))

    g1, gm, g2 = ffn1_norm.reshape(1, d), mix_norm.reshape(1, d), ffn2_norm.reshape(1, d)
    bf_row = _row1(b_forget, LANES)
    gq = jnp.tile(q_norm, N_HEADS).reshape(1, ATTN_WIDTH)
    gk = jnp.tile(k_norm, N_HEADS).reshape(1, ATTN_WIDTH)
    scale_row = pool_scale.reshape(1, POOL_WIDTH)
    gp, ga = out_norm_pool.reshape(1, POOL_WIDTH), out_norm_attn.reshape(1, ATTN_WIDTH)

    wg1, wu1 = _relayed_wait(_relay_to_sibling(gathers[0], "gather_relay_ffn1_up"), started, "gather_wait_ffn1_up")
    h1, sa1, sb1, s1 = _ffn_up(x0, g1, wg1, wu1, "ffn1_up")
    (wd1,) = _copies_wait(gathers[1], True, s1, "gather_wait_ffn1_down")
    (x1,) = _ffn_down(s1, wd1, x0, None, "ffn1_down")
    (win_g,) = _copies_wait(gathers[2], True, x1, "gather_wait_w_in")
    win_cols = win_g.reshape(N_DEV, in_rows, d)[:, :w_in.shape[1]].reshape(MIX_COLS, d)
    win_t = _pad_rows(win_cols, MIX_PAD)
    hm, pv, q, k, v, f = _mix_in_fwd(x1, gm, win_t)
    pooled, mixed, y_pool = _pool_fwd(pv, pool_w, scale_row, gp, bsz, seq)
    qp, kp = _attn_prep_fwd(q, k, f, bf_row, gq, gk, bsz, seq)
    o, lse = _flash_fwd(qp, kp, v, bsz, seq)
    (wout,) = _copies_wait(gathers[3], True, o, "gather_wait_w_out")
    ycat, x2 = _mix_out_fwd(o, y_pool, x1, ga, wout)
    wg2, wu2, wd2 = _copies_wait(gathers[4], True, x2, "gather_wait_ffn2")
    h2, sa2, sb2, s2 = _ffn_up(x2, g2, wg2, wu2, "ffn2_up")
    dx3, dyh2, loss_part = _ffn_down(s2, wd2, x2, target, "ffn2_down")

    da2, db2 = _ffn_bwd_act(dyh2, sa2, sb2, wd2, "ffn2_bwd_act")
    dwg2, dwu2 = _wgrad([da2, db2], h2, "ffn2_up_wgrad")
    (dwd2,) = _wgrad([s2], dyh2, "ffn2_down_wgrad")
    (sent_ffn2,), tok = _copies_start([[dwg2, dwu2, dwd2]], False, "exchange_start_ffn2")
    dx2, dg2 = _ffn_bwd_dx(da2, db2, dx3, x2, g2 + tok[0, 0], wg2, wu2, "ffn2_bwd_dx")
    dx2b, dy_pool, do, dga = _mix_out_bwd(dx2, o, ga, wout)
    (dwout,) = _wgrad([ycat], dx2b, "w_out_wgrad")
    (sent_out,), tok = _copies_start([[dwout]], False, "exchange_start_w_out")
    dqp, dkp, dv = _flash_bwd(qp, kp, v, o, do, lse, bsz, seq)
    dq, dk, df, dgq, dgk, dbf = _attn_prep_bwd(dqp, dkp, q, k, f, bf_row + tok[0, 0], gq, gk, bsz, seq)
    dpv, dpool_w, dscale, dgp = _pool_bwd(dy_pool, mixed, pooled, pool_w, scale_row, gp, bsz, seq)
    dhcat, dx1, dyh1, dgm = _mix_in_bwd(dpv, dq, dk, dv, df, x1, dx2, gm, win_t)
    (dwin,) = _wgrad([dhcat], hm, "w_in_wgrad")
    dwin_blocks = jnp.pad(dwin[:MIX_COLS].reshape(N_DEV, w_in.shape[1], d), ((0, 0), (0, in_rows - w_in.shape[1]), (0, 0)))
    (sent_in,), tok = _copies_start([[dwin_blocks.reshape(N_DEV * in_rows, d)]], False, "exchange_start_w_in")
    (dwd1,) = _wgrad([s1], dyh1, "ffn1_down_wgrad")
    (sent_down1,), tok = _copies_start([[dwd1]], False, "exchange_start_ffn1_down", after=tok)
    da1, db1 = _ffn_bwd_act(dyh1, sa1, sb1, wd1, "ffn1_bwd_act")
    dwg1, dwu1 = _wgrad([da1, db1], h1, "ffn1_up_wgrad")
    (sent_up1,), tok = _copies_start([[dwg1, dwu1]], False, "exchange_start_ffn1_up", after=tok)
    dx0, dg1 = _ffn_bwd_dx(da1, db1, dx1, x0, g1 + tok[0, 0], wg1, wu1, "ffn1_bwd_dx")

    pool_rows = POOL_GROUPS * POOL_GROUP_DIM
    packed = _pack_vector_grads([dg1, dgm, dg2, dbf, dscale, dgq, dgk, dgp, dga], loss_part)
    (sent_small,), tok = _copies_start([[packed, dpool_w.reshape(pool_rows, POOL_GROUP_DIM)]], True, "small_grads_start")

    weights = dict(ffn1_norm=ffn1_norm, ffn1_w_gate=ffn1_w_gate, ffn1_w_up=ffn1_w_up, ffn1_w_down=ffn1_w_down,
                   mix_norm=mix_norm, w_in=w_in, b_forget=b_forget, pool_w=pool_w, pool_scale=pool_scale,
                   q_norm=q_norm, k_norm=k_norm, out_norm_pool=out_norm_pool, out_norm_attn=out_norm_attn,
                   w_out=w_out, ffn2_norm=ffn2_norm, ffn2_w_gate=ffn2_w_gate, ffn2_w_up=ffn2_w_up,
                   ffn2_w_down=ffn2_w_down)
    m_in = dict(ffn1_norm=m_ffn1_norm, ffn1_w_gate=m_ffn1_w_gate, ffn1_w_up=m_ffn1_w_up, ffn1_w_down=m_ffn1_w_down,
                mix_norm=m_mix_norm, w_in=m_w_in, b_forget=m_b_forget, pool_w=m_pool_w, pool_scale=m_pool_scale,
                q_norm=m_q_norm, k_norm=m_k_norm, out_norm_pool=m_out_norm_pool, out_norm_attn=m_out_norm_attn,
                w_out=m_w_out, ffn2_norm=m_ffn2_norm, ffn2_w_gate=m_ffn2_w_gate, ffn2_w_up=m_ffn2_w_up,
                ffn2_w_down=m_ffn2_w_down)
    v_in = dict(ffn1_norm=v_ffn1_norm, ffn1_w_gate=v_ffn1_w_gate, ffn1_w_up=v_ffn1_w_up, ffn1_w_down=v_ffn1_w_down,
                mix_norm=v_mix_norm, w_in=v_w_in, b_forget=v_b_forget, pool_w=v_pool_w, pool_scale=v_pool_scale,
                q_norm=v_q_norm, k_norm=v_k_norm, out_norm_pool=v_out_norm_pool, out_norm_attn=v_out_norm_attn,
                w_out=v_w_out, ffn2_norm=v_ffn2_norm, ffn2_w_gate=v_ffn2_w_gate, ffn2_w_up=v_ffn2_w_up,
                ffn2_w_down=v_ffn2_w_down)
    grads, delta, new_m, new_v = {}, {}, {}, {}
    after = [tok]
    plan = ((sent_ffn2, "ffn2", ("ffn2_w_gate", "ffn2_w_up", "ffn2_w_down")), (sent_out, "w_out", ("w_out",)),
            (sent_in, "w_in", ("w_in",)), (sent_down1, "ffn1_down", ("ffn1_w_down",)),
            (sent_up1, "ffn1_up", ("ffn1_w_gate", "ffn1_w_up")))
    for sent, tag, names in plan:
        parts = _copies_wait(sent, False, after, f"exchange_wait_{tag}")
        for n, part in zip(names, parts):
            turn = (lambda a: a.T) if n in COLUMN_SHARDED else (lambda a: a)
            turn_grad = n in COLUMN_SHARDED and n != "w_in"
            done = _sum_adamw(part, turn(weights[n]), turn(m_in[n]), turn(v_in[n]), turn_grad, f"adamw_{n}")
            grads[n] = done[0] if turn_grad else turn(done[0])
            delta[n], new_m[n], new_v[n] = (turn(a) for a in done[1:])
        after = [new_v[n] for n in names]
    vec_all, pool_all = _copies_wait(sent_small, True, after, "small_grads_wait")
    as_row = lambda a: a.reshape(1, -1)
    as_pool = lambda a: a.reshape(pool_rows, POOL_GROUP_DIM)
    small = _small_adamw(vec_all, pool_all, [tuple(as_row(z[n]) for z in (weights, m_in, v_in)) for n in VEC_NAMES],
                         tuple(as_pool(z["pool_w"]) for z in (weights, m_in, v_in)))
    for i, n in enumerate(VEC_NAMES + ("pool_w",)):
        grads[n], delta[n], new_m[n], new_v[n] = (a.reshape(weights[n].shape) for a in small[4 * i:4 * i + 4])
    loss = small[-1].reshape(())

    order = ("ffn1_norm", "ffn1_w_gate", "ffn1_w_up", "ffn1_w_down", "mix_norm", "w_in", "b_forget", "pool_w",
             "pool_scale", "q_norm", "k_norm", "out_norm_pool", "out_norm_attn", "w_out", "ffn2_norm", "ffn2_w_gate",
             "ffn2_w_up", "ffn2_w_down")
    return (loss, dx0.reshape(bsz, seq, d), *[grads[n] for n in order], *[delta[n] for n in order],
            *[new_m[n] for n in order], *[new_v[n] for n in order])
```

```python
import functools

import jax
import jax.numpy as jnp
from jax import lax
from jax.experimental import pallas as pl
from jax.experimental.pallas import tpu as pltpu

F32 = jnp.float32
BF16 = jnp.bfloat16

EPS = 1e-6
D_MODEL = 1024
D_FF = 2816
N_HEADS = 8
HEAD_DIM = 64
POOL_WIDTH = 512
ATTN_WIDTH = 512
POOL_GROUPS = 4
POOL_GROUP_DIM = 128
POOL_WINDOWS = (2, 4, 8, 16)
POOL_HALO = 16
MIX_COLS = POOL_WIDTH + 3 * ATTN_WIDTH + N_HEADS
MIX_PAD = POOL_WIDTH + 3 * ATTN_WIDTH + 128
N_DEV = 8
BF16_ROWS = 16
LANES = 128
VMEM_LIMIT = 56 * 1024 * 1024

ADAM_LR = 0.001
ADAM_B1 = 0.9
ADAM_B2 = 0.999
ADAM_EPS = 1e-08
ADAM_WD = 0.01
ADAM_STEP = 10


def _params(*sem):
    return pltpu.CompilerParams(dimension_semantics=sem, vmem_limit_bytes=VMEM_LIMIT)


def _dot(a, b):
    return jnp.dot(a, b, preferred_element_type=F32)


def _dot_nt(a, b):
    return lax.dot_general(a, b, (((1,), (1,)), ((), ())), preferred_element_type=F32)


def _dot_tn(a, b):
    return lax.dot_general(a, b, (((0,), (0,)), ((), ())), preferred_element_type=F32)


def _resident(shape):
    return pl.BlockSpec(shape, lambda *_: (0,) * len(shape), pipeline_mode=pl.Buffered(1))


def _rows(tm, width):
    return pl.BlockSpec((tm, width), lambda i: (i, 0))


def _rms_scale(x):
    return lax.rsqrt(jnp.mean(x * x, axis=-1, keepdims=True) + EPS)


def _rms_bwd(dh, x, gain):
    r = _rms_scale(x)
    n = x * r
    dgain = jnp.sum(dh * n, axis=0, keepdims=True)
    dn = dh * gain
    dx = r * (dn - n * jnp.mean(dn * n, axis=-1, keepdims=True))
    return dx, dgain


def _split3(x):
    hi = x.astype(BF16)
    r1 = x - hi.astype(F32)
    mid = r1.astype(BF16)
    lo = (r1 - mid.astype(F32)).astype(BF16)
    return hi, mid, lo


def _split2(x):
    hi = x.astype(BF16)
    return hi, (x - hi.astype(F32)).astype(BF16)


FF_CHUNK = 256


def _ffn_up(x, gain, wg_t, wu_t, name):
    t, d = x.shape
    f = wg_t.shape[0]
    tm = 512

    def body(x_ref, g_ref, wg_ref, wu_ref, h_ref, sa_ref, sb_ref, s_ref):
        xv = x_ref[...]
        h = (xv * _rms_scale(xv) * g_ref[...]).astype(BF16)
        h_ref[...] = h
        for c in range(f // FF_CHUNK):
            sl = pl.ds(c * FF_CHUNK, FF_CHUNK)
            a = _dot_nt(h, wg_ref[sl, :])
            b = _dot_nt(h, wu_ref[sl, :])
            sig = jax.nn.sigmoid(a)
            silu = a * sig
            sa_ref[:, sl] = (b * (sig + silu * (1.0 - sig))).astype(BF16)
            sb_ref[:, sl] = silu.astype(BF16)
            s_ref[:, sl] = (silu * b).astype(BF16)

    wide = jax.ShapeDtypeStruct((t, f), BF16)
    return pl.pallas_call(
        body, name=name, grid=(t // tm,),
        in_specs=[_rows(tm, d), _resident((1, d)), _resident((f, d)), _resident((f, d))],
        out_specs=[_rows(tm, d), _rows(tm, f), _rows(tm, f), _rows(tm, f)],
        out_shape=[jax.ShapeDtypeStruct((t, d), BF16), wide, wide, wide],
        compiler_params=_params("arbitrary"),
    )(x, gain, wg_t, wu_t)


def _ffn_down(s, wd, x, target, name):
    t, d = x.shape
    f = wd.shape[0]
    tm = 512
    with_loss = target is not None

    def body(*refs):
        if with_loss:
            s_ref, w_ref, x_ref, t_ref, dy_ref, dyh_ref, loss_ref = refs
        else:
            s_ref, w_ref, x_ref, y_ref = refs
        y = x_ref[...] + 0.5 * _dot(s_ref[...], w_ref[...])
        if with_loss:
            e = y - t_ref[...]
            dy = e * (1.0 / d)
            dy_ref[...] = dy
            dyh_ref[...] = (0.5 * dy).astype(BF16)

            @pl.when(pl.program_id(0) == 0)
            def _():
                loss_ref[...] = jnp.zeros_like(loss_ref)

            part = jnp.sum(jnp.sum(e * e, axis=0, keepdims=True), axis=1, keepdims=True)
            loss_ref[...] += part * (0.5 / d)
        else:
            y_ref[...] = y

    in_specs = [_rows(tm, f), _resident((f, d)), _rows(tm, d)]
    args = [s, wd, x]
    if with_loss:
        in_specs.append(_rows(tm, d))
        args.append(target)
        out_shape = [jax.ShapeDtypeStruct((t, d), F32), jax.ShapeDtypeStruct((t, d), BF16),
                     jax.ShapeDtypeStruct((1, 1), F32)]
        out_specs = [_rows(tm, d), _rows(tm, d), pl.BlockSpec((1, 1), lambda i: (0, 0))]
    else:
        out_shape = [jax.ShapeDtypeStruct((t, d), F32)]
        out_specs = [_rows(tm, d)]
    return pl.pallas_call(
        body, name=name, grid=(t // tm,), in_specs=in_specs, out_specs=out_specs, out_shape=out_shape,
        compiler_params=_params("arbitrary"),
    )(*args)


def _ffn_bwd_act(dyh, sa, sb, wd, name):
    t, d = dyh.shape
    f = wd.shape[0]
    tm = 512

    def body(dy_ref, sa_ref, sb_ref, wd_ref, da_ref, db_ref):
        dyh_v = dy_ref[...]
        for c in range(f // FF_CHUNK):
            sl = pl.ds(c * FF_CHUNK, FF_CHUNK)
            ds = _dot_nt(dyh_v, wd_ref[sl, :])
            da_ref[:, sl] = (ds * sa_ref[:, sl].astype(F32)).astype(BF16)
            db_ref[:, sl] = (ds * sb_ref[:, sl].astype(F32)).astype(BF16)

    wide = jax.ShapeDtypeStruct((t, f), BF16)
    return pl.pallas_call(
        body, name=name, grid=(t // tm,),
        in_specs=[_rows(tm, d), _rows(tm, f), _rows(tm, f), _resident((f, d))],
        out_specs=[_rows(tm, f), _rows(tm, f)], out_shape=[wide, wide],
        compiler_params=_params("arbitrary"),
    )(dyh, sa, sb, wd)


def _ffn_bwd_dx(da, db, dy, x, gain, wg_t, wu_t, name):
    t, d = x.shape
    f = wg_t.shape[0]
    tm = 512

    def body(da_ref, db_ref, dy_ref, x_ref, g_ref, wg_ref, wu_ref, dx_ref, dg_ref):
        dh = _dot(da_ref[...], wg_ref[...]) + _dot(db_ref[...], wu_ref[...])
        dx, dgain = _rms_bwd(dh, x_ref[...], g_ref[...])
        dx_ref[...] = dy_ref[...] + dx

        @pl.when(pl.program_id(0) == 0)
        def _():
            dg_ref[...] = jnp.zeros_like(dg_ref)

        dg_ref[...] += dgain

    return pl.pallas_call(
        body, name=name, grid=(t // tm,),
        in_specs=[_rows(tm, f), _rows(tm, f), _rows(tm, d), _rows(tm, d), _resident((1, d)), _resident((f, d)),
                  _resident((f, d))],
        out_specs=[_rows(tm, d), pl.BlockSpec((1, d), lambda i: (0, 0))],
        out_shape=[jax.ShapeDtypeStruct((t, d), F32), jax.ShapeDtypeStruct((1, d), F32)],
        compiler_params=_params("arbitrary"),
    )(da, db, dy, x, gain, wg_t, wu_t)


def _wgrad(lhs, b, name):
    t, n = lhs[0].shape
    d = b.shape[1]
    m = len(lhs)
    tn = n // 2 if n * d * m > (4 << 20) else n
    tk = 1024
    nk = t // tk

    def body(*refs):
        a_refs, b_ref, o_refs, accs = refs[:m], refs[m], refs[m + 1:2 * m + 1], refs[2 * m + 1:]
        k = pl.program_id(1)

        @pl.when(k == 0)
        def _():
            for acc in accs:
                acc[...] = jnp.zeros_like(acc)

        bv = b_ref[...]
        for a_ref, acc in zip(a_refs, accs):
            acc[...] += _dot_tn(a_ref[...], bv)

        @pl.when(k == nk - 1)
        def _():
            for o_ref, acc in zip(o_refs, accs):
                o_ref[...] = acc[...].astype(BF16)

    return pl.pallas_call(
        body, name=name, grid=(n // tn, nk),
        in_specs=[pl.BlockSpec((tk, tn), lambda j, k: (k, j))] * m + [pl.BlockSpec((tk, d), lambda j, k: (k, 0))],
        out_specs=[pl.BlockSpec((tn, d), lambda j, k: (j, 0))] * m,
        out_shape=[jax.ShapeDtypeStruct((n, d), BF16)] * m,
        scratch_shapes=[pltpu.VMEM((tn, d), F32)] * m,
        compiler_params=_params("arbitrary", "arbitrary"),
    )(*lhs, b)


def _mix_in_fwd(x, gain, w_in_t):
    t, d = x.shape
    tm = 1024
    pw, aw = POOL_WIDTH, ATTN_WIDTH

    def body(x_ref, g_ref, w_ref, hm_ref, pv_ref, q_ref, k_ref, v_ref, f_ref):
        xv = x_ref[...]
        hm = (xv * _rms_scale(xv) * g_ref[...]).astype(BF16)
        hm_ref[...] = hm
        pv_ref[...] = _dot_nt(hm, w_ref[pl.ds(0, pw), :])
        q_ref[...] = _dot_nt(hm, w_ref[pl.ds(pw, aw), :])
        k_ref[...] = _dot_nt(hm, w_ref[pl.ds(pw + aw, aw), :])
        v_ref[...] = _dot_nt(hm, w_ref[pl.ds(pw + 2 * aw, aw), :]).astype(BF16)
        f_ref[...] = _dot_nt(hm, w_ref[pl.ds(pw + 3 * aw, LANES), :])

    return pl.pallas_call(
        body, name="mix_in_fwd", grid=(t // tm,),
        in_specs=[_rows(tm, d), _resident((1, d)), _resident((MIX_PAD, d))],
        out_specs=[_rows(tm, d), _rows(tm, pw), _rows(tm, aw), _rows(tm, aw), _rows(tm, aw), _rows(tm, LANES)],
        out_shape=[jax.ShapeDtypeStruct((t, d), BF16), jax.ShapeDtypeStruct((t, pw), F32),
                   jax.ShapeDtypeStruct((t, aw), F32), jax.ShapeDtypeStruct((t, aw), F32),
                   jax.ShapeDtypeStruct((t, aw), BF16), jax.ShapeDtypeStruct((t, LANES), F32)],
        compiler_params=_params("arbitrary"),
    )(x, gain, w_in_t)


def _pool_fwd(pv, pool_w, pool_scale, gain, bsz, seq):
    ts = 512
    ns = seq // ts
    pw = POOL_WIDTH

    def body(pv_ref, w_ref, sc_ref, g_ref, pooled_ref, mixed_ref, y_ref, ext):
        s = pl.program_id(1)

        @pl.when(s == 0)
        def _():
            ext[pl.ds(0, POOL_HALO), :] = jnp.zeros((POOL_HALO, pw), F32)

        p = pv_ref[...]
        ext[pl.ds(POOL_HALO, ts), :] = p
        pos = s * ts + lax.broadcasted_iota(jnp.int32, (ts, 1), 0)
        parts = []
        for g, w in enumerate(POOL_WINDOWS):
            lanes = pl.ds(g * POOL_GROUP_DIM, POOL_GROUP_DIM)
            win = ext[pl.ds(POOL_HALO, ts), lanes]
            for i in range(1, w):
                win = win + ext[pl.ds(POOL_HALO - i, ts), lanes]
            cnt = jnp.minimum(pos + 1, w).astype(F32)
            pooled = (win / cnt - ext[pl.ds(POOL_HALO, ts), lanes]).astype(BF16)
            pooled_ref[:, lanes] = pooled
            parts.append(_dot(pooled, w_ref[g].astype(BF16)))
        mixed = jnp.concatenate(parts, axis=1)
        mixed_ref[...] = mixed
        pm = mixed * sc_ref[...]
        y_ref[...] = (pm * _rms_scale(pm) * g_ref[...]).astype(BF16)
        ext[pl.ds(0, POOL_HALO), :] = p[ts - POOL_HALO:, :]

    blk = pl.BlockSpec((ts, pw), lambda b, s: (b * ns + s, 0))
    t = bsz * seq
    return pl.pallas_call(
        body, name="pool_fwd", grid=(bsz, ns),
        in_specs=[blk, pl.BlockSpec((POOL_GROUPS, POOL_GROUP_DIM, POOL_GROUP_DIM), lambda b, s: (0, 0, 0)),
                  pl.BlockSpec((1, pw), lambda b, s: (0, 0)), pl.BlockSpec((1, pw), lambda b, s: (0, 0))],
        out_specs=[blk, blk, blk],
        out_shape=[jax.ShapeDtypeStruct((t, pw), BF16), jax.ShapeDtypeStruct((t, pw), F32),
                   jax.ShapeDtypeStruct((t, pw), BF16)],
        scratch_shapes=[pltpu.VMEM((POOL_HALO + ts, pw), F32)],
        compiler_params=_params("arbitrary", "arbitrary"),
    )(pv, pool_w, pool_scale, gain)


def _pool_bwd(dy, mixed, pooled, pool_w, pool_scale, gain, bsz, seq):
    ts = 512
    ns = seq // ts
    pw = POOL_WIDTH

    def body(dy_ref, mixed_ref, pooled_ref, w_ref, sc_ref, g_ref, dpv_ref, dw_ref, dsc_ref, dg_ref, ext):
        b = pl.program_id(0)
        sr = pl.program_id(1)
        s = ns - 1 - sr

        @pl.when(jnp.logical_and(b == 0, sr == 0))
        def _():
            dw_ref[...] = jnp.zeros_like(dw_ref)
            dsc_ref[...] = jnp.zeros_like(dsc_ref)
            dg_ref[...] = jnp.zeros_like(dg_ref)

        @pl.when(sr == 0)
        def _():
            ext[pl.ds(ts, POOL_HALO), :] = jnp.zeros((POOL_HALO, pw), F32)

        mixed = mixed_ref[...]
        sc = sc_ref[...]
        dpm, dgain = _rms_bwd(dy_ref[...], mixed * sc, g_ref[...])
        dg_ref[...] += dgain
        dsc_ref[...] += jnp.sum(dpm * mixed, axis=0, keepdims=True)
        dmixed = (dpm * sc).astype(BF16)
        pos = s * ts + lax.broadcasted_iota(jnp.int32, (ts, 1), 0)
        dpooled = []
        for g, w in enumerate(POOL_WINDOWS):
            lanes = pl.ds(g * POOL_GROUP_DIM, POOL_GROUP_DIM)
            dm = dmixed[:, g * POOL_GROUP_DIM:(g + 1) * POOL_GROUP_DIM]
            dw_ref[g] += _dot_tn(pooled_ref[:, lanes], dm)
            dp = _dot_nt(dm, w_ref[g].astype(BF16))
            dpooled.append(dp)
            cnt = jnp.minimum(pos + 1, w).astype(F32)
            ext[pl.ds(0, ts), lanes] = dp / cnt
        for g, w in enumerate(POOL_WINDOWS):
            lanes = pl.ds(g * POOL_GROUP_DIM, POOL_GROUP_DIM)
            win = ext[pl.ds(0, ts), lanes]
            for i in range(1, w):
                win = win + ext[pl.ds(i, ts), lanes]
            dpv_ref[:, lanes] = (win - dpooled[g]).astype(BF16)
        head = ext[pl.ds(0, POOL_HALO), :]
        ext[pl.ds(ts, POOL_HALO), :] = head

    blk = pl.BlockSpec((ts, pw), lambda b, s: (b * ns + (ns - 1 - s), 0))
    vec = pl.BlockSpec((1, pw), lambda b, s: (0, 0))
    wspec = pl.BlockSpec((POOL_GROUPS, POOL_GROUP_DIM, POOL_GROUP_DIM), lambda b, s: (0, 0, 0))
    t = bsz * seq
    return pl.pallas_call(
        body, name="pool_bwd", grid=(bsz, ns),
        in_specs=[blk, blk, blk, wspec, vec, vec],
        out_specs=[blk, wspec, vec, vec],
        out_shape=[jax.ShapeDtypeStruct((t, pw), BF16),
                   jax.ShapeDtypeStruct((POOL_GROUPS, POOL_GROUP_DIM, POOL_GROUP_DIM), F32),
                   jax.ShapeDtypeStruct((1, pw), F32), jax.ShapeDtypeStruct((1, pw), F32)],
        scratch_shapes=[pltpu.VMEM((ts + POOL_HALO, pw), F32)],
        compiler_params=_params("arbitrary", "arbitrary"),
    )(dy, mixed, pooled, pool_w, pool_scale, gain)


AUX_ONE = 64
AUX_F = 67

ATTN_PREP_ROWS = 512


def _seg_ones(width, seg):
    r = lax.broadcasted_iota(jnp.int32, (width, width), 0) // seg
    c = lax.broadcasted_iota(jnp.int32, (width, width), 1) // seg
    return (r == c).astype(BF16)


def _tri_ones(n, lower):
    r = lax.broadcasted_iota(jnp.int32, (n, n), 0)
    c = lax.broadcasted_iota(jnp.int32, (n, n), 1)
    return ((r >= c) if lower else (r <= c)).astype(BF16)


def _place_pieces(first_lane):
    r = lax.broadcasted_iota(jnp.int32, (3 * LANES, N_HEADS * LANES), 0)
    c = lax.broadcasted_iota(jnp.int32, (3 * LANES, N_HEADS * LANES), 1)
    piece, head = r // LANES, r % LANES
    return jnp.logical_and(head < N_HEADS, c == head * LANES + first_lane + piece).astype(BF16)


def _head_sums(x, seg_ones):
    hi, lo = _split2(x)
    return _dot(hi, seg_ones) + _dot(lo, seg_ones)


def _log_sigmoid(x):
    return jnp.minimum(x, 0.0) - jnp.log(1.0 + jnp.exp(-jnp.abs(x)))


def _attn_prep_fwd(q, k, f, b_forget, q_gain, k_gain, bsz, seq):
    ts = ATTN_PREP_ROWS
    ns = seq // ts
    aw = ATTN_WIDTH
    t = bsz * seq
    seg = _seg_ones(aw, HEAD_DIM)
    tri = _tri_ones(ts, True)

    def body(q_ref, k_ref, f_ref, bf_ref, gq_ref, gk_ref, seg_ref, tri_ref, pq_ref, pk_ref, qp_ref, kp_ref, carry):
        s = pl.program_id(1)

        @pl.when(s == 0)
        def _():
            carry[...] = jnp.zeros_like(carry)

        logf = _log_sigmoid(f_ref[...] + bf_ref[...])
        hi, mid, lo = _split3(logf)
        tri_v = tri_ref[...]
        fc = _dot(tri_v, hi) + _dot(tri_v, mid) + _dot(tri_v, lo) + carry[pl.ds(0, 1), :]
        carry[pl.ds(0, 1), :] = fc[ts - 1:, :]
        pcs = jnp.concatenate(_split3(fc), axis=1)
        lane = lax.broadcasted_iota(jnp.int32, (1, LANES), 1)
        ones_q = jnp.logical_and(lane >= AUX_ONE, lane < AUX_ONE + 3).astype(F32)
        ones_k = jnp.logical_and(lane >= AUX_F, lane < AUX_F + 3).astype(F32)
        seg_v = seg_ref[...]

        def build(x_ref, g_ref, scale, out_ref, ones, place_ref, f_sign):
            xv = x_ref[...]
            r = lax.rsqrt(_head_sums(xv * xv, seg_v) * (1.0 / HEAD_DIM) + EPS)
            xn = xv * r * g_ref[...] * scale
            aux = _dot(pcs, place_ref[...]) * f_sign
            for h in range(N_HEADS):
                pair = xn[:, (h // 2) * LANES:(h // 2 + 1) * LANES]
                feat = pair if h % 2 == 0 else pltpu.roll(pair, HEAD_DIM, 1)
                aux_h = aux[:, h * LANES:(h + 1) * LANES] + ones
                out_ref[:, h * LANES:(h + 1) * LANES] = jnp.where(lane < HEAD_DIM, feat, aux_h).astype(BF16)

        build(q_ref, gq_ref, 0.125, qp_ref, ones_q, pq_ref, 1.0)
        build(k_ref, gk_ref, 1.0, kp_ref, ones_k, pk_ref, -1.0)

    blk = pl.BlockSpec((ts, aw), lambda b, s: (b * ns + s, 0))
    fblk = pl.BlockSpec((ts, LANES), lambda b, s: (b * ns + s, 0))
    oblk = pl.BlockSpec((ts, N_HEADS * LANES), lambda b, s: (b * ns + s, 0))
    const = lambda shape: pl.BlockSpec(shape, lambda b, s: (0, 0))
    return pl.pallas_call(
        body, name="attn_prep_fwd", grid=(bsz, ns),
        in_specs=[blk, blk, fblk, const((1, LANES)), const((1, aw)), const((1, aw)), const((aw, aw)), const((ts, ts)),
                  const((3 * LANES, N_HEADS * LANES)), const((3 * LANES, N_HEADS * LANES))],
        out_specs=[oblk, oblk],
        out_shape=[jax.ShapeDtypeStruct((t, N_HEADS * LANES), BF16)] * 2,
        scratch_shapes=[pltpu.VMEM((8, LANES), F32)],
        compiler_params=_params("arbitrary", "arbitrary"),
    )(q, k, f, b_forget, q_gain, k_gain, seg, tri, _place_pieces(AUX_F), _place_pieces(AUX_ONE))


def _attn_prep_bwd(dqp, dkp, q, k, f, b_forget, q_gain, k_gain, bsz, seq):
    ts = ATTN_PREP_ROWS
    ns = seq // ts
    aw = ATTN_WIDTH
    t = bsz * seq
    seg = _seg_ones(aw, HEAD_DIM)
    tri = _tri_ones(ts, False)

    def body(dqp_ref, dkp_ref, q_ref, k_ref, f_ref, bf_ref, gq_ref, gk_ref, seg_ref, tri_ref,
             dq_ref, dk_ref, df_ref, dgq_ref, dgk_ref, dbf_ref, carry):
        b = pl.program_id(0)
        sr = pl.program_id(1)

        @pl.when(jnp.logical_and(b == 0, sr == 0))
        def _():
            dgq_ref[...] = jnp.zeros_like(dgq_ref)
            dgk_ref[...] = jnp.zeros_like(dgk_ref)
            dbf_ref[...] = jnp.zeros_like(dbf_ref)

        @pl.when(sr == 0)
        def _():
            carry[...] = jnp.zeros_like(carry)

        lane = lax.broadcasted_iota(jnp.int32, (1, LANES), 1)
        seg_v = seg_ref[...]

        def norm_bwd(dp_ref, x_ref, g_ref, scale, dx_ref, dgain_ref):
            parts = []
            for j in range(N_HEADS // 2):
                even = dp_ref[:, (2 * j) * LANES:(2 * j + 1) * LANES]
                odd = dp_ref[:, (2 * j + 1) * LANES:(2 * j + 2) * LANES]
                parts.append(jnp.where(lane < HEAD_DIM, even, pltpu.roll(odd, HEAD_DIM, 1)))
            dxn = jnp.concatenate(parts, axis=1) * scale
            xv = x_ref[...]
            r = lax.rsqrt(_head_sums(xv * xv, seg_v) * (1.0 / HEAD_DIM) + EPS)
            n = xv * r
            dgain_ref[...] += jnp.sum(dxn * n, axis=0, keepdims=True)
            dn = dxn * g_ref[...]
            m = _head_sums(dn * n, seg_v) * (1.0 / HEAD_DIM)
            dx_ref[...] = (r * (dn - n * m)).astype(BF16)

        norm_bwd(dqp_ref, q_ref, gq_ref, 0.125, dq_ref, dgq_ref)
        norm_bwd(dkp_ref, k_ref, gk_ref, 1.0, dk_ref, dgk_ref)

        dfc = jnp.zeros((ts, LANES), F32)
        for h in range(N_HEADS):
            cols = pl.ds(h * LANES, LANES)
            both = jnp.where(lane == AUX_F, dqp_ref[:, cols], 0.0) - jnp.where(lane == AUX_ONE, dkp_ref[:, cols], 0.0)
            dfc = jnp.where(lane == h, jnp.sum(both, axis=1, keepdims=True), dfc)
        hi, mid, lo = _split3(dfc)
        tri_v = tri_ref[...]
        dlogf = _dot(tri_v, hi) + _dot(tri_v, mid) + _dot(tri_v, lo) + carry[pl.ds(0, 1), :]
        carry[pl.ds(0, 1), :] = dlogf[0:1, :]
        df = jnp.where(lane < N_HEADS, dlogf * jax.nn.sigmoid(-(f_ref[...] + bf_ref[...])), 0.0)
        df_ref[...] = df.astype(BF16)
        dbf_ref[...] += jnp.sum(df, axis=0, keepdims=True)

    rev = lambda b, s: (b * ns + (ns - 1 - s), 0)
    blk = pl.BlockSpec((ts, aw), rev)
    fblk = pl.BlockSpec((ts, LANES), rev)
    pblk = pl.BlockSpec((ts, N_HEADS * LANES), rev)
    const = lambda shape: pl.BlockSpec(shape, lambda b, s: (0, 0))
    return pl.pallas_call(
        body, name="attn_prep_bwd", grid=(bsz, ns),
        in_specs=[pblk, pblk, blk, blk, fblk, const((1, LANES)), const((1, aw)), const((1, aw)), const((aw, aw)),
                  const((ts, ts))],
        out_specs=[blk, blk, fblk, const((1, aw)), const((1, aw)), const((1, LANES))],
        out_shape=[jax.ShapeDtypeStruct((t, aw), BF16), jax.ShapeDtypeStruct((t, aw), BF16),
                   jax.ShapeDtypeStruct((t, LANES), BF16), jax.ShapeDtypeStruct((1, aw), F32),
                   jax.ShapeDtypeStruct((1, aw), F32), jax.ShapeDtypeStruct((1, LANES), F32)],
        scratch_shapes=[pltpu.VMEM((8, LANES), F32)],
        compiler_params=_params("arbitrary", "arbitrary"),
    )(dqp, dkp, q, k, f, b_forget, q_gain, k_gain, seg, tri)


ATTN_BLOCK = 512
HEAD_PAIRS = N_HEADS // 2


def _flash_fwd(qp, kp, v, bsz, seq):
    tq = ATTN_BLOCK
    nq = seq // tq
    t = bsz * seq

    def body(q_ref, k_ref, v_ref, o_ref, lse_ref, m_sc, l_sc, acc_sc):
        i = pl.program_id(2)
        m_sc[...] = jnp.full(m_sc.shape, -jnp.inf, F32)
        l_sc[...] = jnp.zeros_like(l_sc)
        acc_sc[...] = jnp.zeros_like(acc_sc)
        lane = lax.broadcasted_iota(jnp.int32, (1, LANES), 1)
        low = lane < HEAD_DIM

        def key_block(j, masked):
            rows = pl.ds(pl.multiple_of(j * tq, tq), tq)
            vv = v_ref[rows, :]
            for h in range(2):
                mine = low if h == 0 else jnp.logical_not(low)
                s = _dot_nt(q_ref[:, h * LANES:(h + 1) * LANES], k_ref[rows, pl.ds(h * LANES, LANES)])
                if masked:
                    row = lax.broadcasted_iota(jnp.int32, (tq, tq), 0)
                    col = lax.broadcasted_iota(jnp.int32, (tq, tq), 1)
                    s = jnp.where(row >= col, s, -jnp.inf)
                m_prev = m_sc[h]
                m_new = jnp.maximum(m_prev, jnp.max(s, axis=1, keepdims=True))
                p = jnp.exp(s - jnp.tile(m_new, (1, tq // LANES)))
                alpha = jnp.exp(m_prev - m_new)
                l_sc[h] = alpha * l_sc[h] + jnp.sum(p, axis=1, keepdims=True)
                m_sc[h] = m_new
                pv = _dot(p.astype(BF16), jnp.where(mine, vv, jnp.zeros_like(vv)))
                acc_sc[...] = acc_sc[...] * jnp.where(mine, alpha, 1.0) + pv

        def below_diagonal(j, carry):
            key_block(j, False)
            return carry

        lax.fori_loop(0, i, below_diagonal, 0)
        key_block(i, True)
        l = jnp.where(low, l_sc[0], l_sc[1])
        m = jnp.where(low, m_sc[0], m_sc[1])
        o_ref[...] = acc_sc[...] / l
        lse_ref[...] = m + jnp.log(l)

    qspec = pl.BlockSpec((tq, 2 * LANES), lambda b, hp, i: (b * nq + i, hp))
    kspec = pl.BlockSpec((seq, 2 * LANES), lambda b, hp, i: (b, hp))
    vspec = pl.BlockSpec((seq, LANES), lambda b, hp, i: (b, hp))
    ospec = pl.BlockSpec((tq, LANES), lambda b, hp, i: (b * nq + i, hp))
    return pl.pallas_call(
        body, name="flash_fwd", grid=(bsz, HEAD_PAIRS, nq),
        in_specs=[qspec, kspec, vspec], out_specs=[ospec, ospec],
        out_shape=[jax.ShapeDtypeStruct((t, ATTN_WIDTH), F32), jax.ShapeDtypeStruct((t, ATTN_WIDTH), F32)],
        scratch_shapes=[pltpu.VMEM((2, tq, LANES), F32), pltpu.VMEM((2, tq, LANES), F32), pltpu.VMEM((tq, LANES), F32)],
        compiler_params=_params("arbitrary", "arbitrary", "arbitrary"),
    )(qp, kp, v)


def _flash_bwd(qp, kp, v, o, do, lse, bsz, seq):
    tq = ATTN_BLOCK
    nq = seq // tq
    t = bsz * seq

    def body(q_ref, k_ref, v_ref, o_ref, do_ref, lse_ref, dq_ref, dk_ref, dv_ref, dk_acc, dv_acc):
        j = pl.program_id(2)

        @pl.when(j == 0)
        def _():
            dq_ref[...] = jnp.zeros_like(dq_ref)

        dk_acc[...] = jnp.zeros_like(dk_acc)
        dv_acc[...] = jnp.zeros_like(dv_acc)
        lane = lax.broadcasted_iota(jnp.int32, (1, LANES), 1)
        low = lane < HEAD_DIM

        def query_block(i, masked):
            rows = pl.ds(pl.multiple_of(i * tq, tq), tq)
            dov = do_ref[rows, :]
            dd = dov * o_ref[rows, :]
            dob = dov.astype(BF16)
            vv = v_ref[...]
            lse_v = lse_ref[rows, :]
            for h in range(2):
                mine = low if h == 0 else jnp.logical_not(low)
                cols = pl.ds(h * LANES, LANES)
                qh = q_ref[rows, cols]
                kh = k_ref[:, cols]
                s = _dot_nt(qh, kh)
                lse_h = jnp.where(mine, lse_v, pltpu.roll(lse_v, HEAD_DIM, 1))
                p = jnp.exp(s - jnp.tile(lse_h, (1, tq // LANES)))
                if masked:
                    row = lax.broadcasted_iota(jnp.int32, (tq, tq), 0)
                    col = lax.broadcasted_iota(jnp.int32, (tq, tq), 1)
                    p = jnp.where(row >= col, p, 0.0)
                delta = jnp.sum(jnp.where(mine, dd, 0.0), axis=1, keepdims=True)
                dp = _dot_nt(dob, jnp.where(mine, vv, jnp.zeros_like(vv)))
                ds = (p * (dp - delta)).astype(BF16)
                dv_acc[...] += jnp.where(mine, _dot_tn(p.astype(BF16), dob), 0.0)
                dk_acc[:, cols] += _dot_tn(ds, qh)
                dq_ref[rows, cols] += _dot(ds, kh)

        def above_diagonal(i, carry):
            query_block(i, False)
            return carry

        query_block(j, True)
        lax.fori_loop(j + 1, nq, above_diagonal, 0)
        dk_ref[...] = dk_acc[...]
        dv_ref[...] = dv_acc[...].astype(BF16)

    qspec = pl.BlockSpec((seq, 2 * LANES), lambda b, hp, j: (b, hp))
    kspec = pl.BlockSpec((tq, 2 * LANES), lambda b, hp, j: (b * nq + j, hp))
    vspec = pl.BlockSpec((tq, LANES), lambda b, hp, j: (b * nq + j, hp))
    ospec = pl.BlockSpec((seq, LANES), lambda b, hp, j: (b, hp))
    return pl.pallas_call(
        body, name="flash_bwd", grid=(bsz, HEAD_PAIRS, nq),
        in_specs=[qspec, kspec, vspec, ospec, ospec, ospec], out_specs=[qspec, kspec, vspec],
        out_shape=[jax.ShapeDtypeStruct((t, N_HEADS * LANES), F32), jax.ShapeDtypeStruct((t, N_HEADS * LANES), F32),
                   jax.ShapeDtypeStruct((t, ATTN_WIDTH), BF16)],
        scratch_shapes=[pltpu.VMEM((tq, 2 * LANES), F32), pltpu.VMEM((tq, LANES), F32)],
        compiler_params=_params("arbitrary", "arbitrary", "arbitrary"),
    )(qp, kp, v, o, do, lse)


def _mix_out_fwd(o, y_pool, x, gain, w_out):
    t, d = x.shape
    tm = 1024
    pw, aw = POOL_WIDTH, ATTN_WIDTH

    def body(o_ref, yp_ref, x_ref, g_ref, w_ref, ycat_ref, y_ref):
        ov = o_ref[...]
        ya = (ov * _rms_scale(ov) * g_ref[...]).astype(BF16)
        ycat = jnp.concatenate([yp_ref[...], ya], axis=1)
        ycat_ref[...] = ycat
        y_ref[...] = x_ref[...] + _dot(ycat, w_ref[...])

    return pl.pallas_call(
        body, name="mix_out_fwd", grid=(t // tm,),
        in_specs=[_rows(tm, aw), _rows(tm, pw), _rows(tm, d), _resident((1, aw)), _resident((pw + aw, d))],
        out_specs=[_rows(tm, pw + aw), _rows(tm, d)],
        out_shape=[jax.ShapeDtypeStruct((t, pw + aw), BF16), jax.ShapeDtypeStruct((t, d), F32)],
        compiler_params=_params("arbitrary"),
    )(o, y_pool, x, gain, w_out)


def _mix_out_bwd(dx, o, gain, w_out):
    t, d = dx.shape
    tm = 1024
    pw, aw = POOL_WIDTH, ATTN_WIDTH

    def body(dx_ref, o_ref, g_ref, w_ref, dxb_ref, dyp_ref, do_ref, dg_ref):
        dxb = dx_ref[...].astype(BF16)
        dxb_ref[...] = dxb
        dyp_ref[...] = _dot_nt(dxb, w_ref[pl.ds(0, pw), :])
        dya = _dot_nt(dxb, w_ref[pl.ds(pw, aw), :])
        do, dgain = _rms_bwd(dya, o_ref[...], g_ref[...])
        do_ref[...] = do

        @pl.when(pl.program_id(0) == 0)
        def _():
            dg_ref[...] = jnp.zeros_like(dg_ref)

        dg_ref[...] += dgain

    return pl.pallas_call(
        body, name="mix_out_bwd", grid=(t // tm,),
        in_specs=[_rows(tm, d), _rows(tm, aw), _resident((1, aw)), _resident((pw + aw, d))],
        out_specs=[_rows(tm, d), _rows(tm, pw), _rows(tm, aw), pl.BlockSpec((1, aw), lambda i: (0, 0))],
        out_shape=[jax.ShapeDtypeStruct((t, d), BF16), jax.ShapeDtypeStruct((t, pw), F32),
                   jax.ShapeDtypeStruct((t, aw), F32), jax.ShapeDtypeStruct((1, aw), F32)],
        compiler_params=_params("arbitrary"),
    )(dx, o, gain, w_out)


def _mix_in_bwd(dpv, dq, dk, dv, df, x, dx_res, gain, w_in_t):
    t, d = x.shape
    tm = 512
    pw, aw = POOL_WIDTH, ATTN_WIDTH

    def body(dpv_ref, dq_ref, dk_ref, dv_ref, df_ref, x_ref, dxr_ref, g_ref, w_ref, dh_ref, dx_ref, dxh_ref, dg_ref):
        dh = jnp.concatenate([dpv_ref[...], dq_ref[...], dk_ref[...], dv_ref[...], df_ref[...]], axis=1)
        dh_ref[...] = dh
        dhm = _dot(dh, w_ref[...])
        dx, dgain = _rms_bwd(dhm, x_ref[...], g_ref[...])
        dx = dxr_ref[...] + dx
        dx_ref[...] = dx
        dxh_ref[...] = (0.5 * dx).astype(BF16)

        @pl.when(pl.program_id(0) == 0)
        def _():
            dg_ref[...] = jnp.zeros_like(dg_ref)

        dg_ref[...] += dgain

    return pl.pallas_call(
        body, name="mix_in_bwd", grid=(t // tm,),
        in_specs=[_rows(tm, pw), _rows(tm, aw), _rows(tm, aw), _rows(tm, aw), _rows(tm, LANES), _rows(tm, d),
                  _rows(tm, d), _resident((1, d)), _resident((MIX_PAD, d))],
        out_specs=[_rows(tm, MIX_PAD), _rows(tm, d), _rows(tm, d), pl.BlockSpec((1, d), lambda i: (0, 0))],
        out_shape=[jax.ShapeDtypeStruct((t, MIX_PAD), BF16), jax.ShapeDtypeStruct((t, d), F32),
                   jax.ShapeDtypeStruct((t, d), BF16), jax.ShapeDtypeStruct((1, d), F32)],
        compiler_params=_params("arbitrary"),
    )(dpv, dq, dk, dv, df, x, dx_res, gain, w_in_t)


MESH_IDS = pl.DeviceIdType.MESH


def _me():
    return lax.axis_index("x"), lax.axis_index("y"), lax.axis_index("c")


def _peer(x, y, c, p):
    px = 1 - x if p & 4 else x
    py = 1 - y if p & 2 else y
    pc = 1 - c if p & 1 else c
    return (px, py, pc), 4 * px + 2 * py + pc


HBM_SPEC = pl.BlockSpec(memory_space=pltpu.HBM)
SEM_SPEC = pl.BlockSpec(memory_space=pltpu.SEMAPHORE)
SPLIT_COPY = pltpu.CompilerParams(has_side_effects=pltpu.SideEffectType.DATAFLOW_SIDE_EFFECTING)
PEERS = N_DEV - 1


def _hbm(a):
    return pltpu.with_memory_space_constraint(a, pltpu.HBM)


def _row_block(ref, dev, rows):
    return ref.at[pl.ds(pl.multiple_of(dev * rows, BF16_ROWS), rows)]


def _copy_ends(gather, src, land, me, peer_id):
    if gather:
        rows = src.shape[0]
        return src, _row_block(land, me, rows), _row_block(land, peer_id, rows), src, _row_block(land, me, rows)
    rows = src.shape[0] // N_DEV
    return (_row_block(src, peer_id, rows), land.at[me], land.at[peer_id], _row_block(src, me, rows), land.at[me])


def _land_shape(gather, s):
    return (N_DEV * s.shape[0], s.shape[1]) if gather else (N_DEV, s.shape[0] // N_DEV, s.shape[1])


SIBLING = 1
SAME_CORE_PEERS = (2, 4, 6)
RELAYS = len(SAME_CORE_PEERS)


def _copies_start(groups, gather, name, after=None, relayed=()):
    flat = [s for g in groups for s in g]
    n, ng = len(flat), len(groups)
    lands = [lax.empty(_land_shape(gather, s), s.dtype) for s in flat]
    n_in = 2 * n + (after is not None)

    def body(*refs):
        ins, lnd = refs[:n], refs[n:2 * n]
        sems = refs[n_in:n_in + 2 * ng]
        token = refs[-1]
        x, y, c = _me()
        me = 4 * x + 2 * y + c
        w = 0
        for gi, g in enumerate(groups):
            for k in range(len(g)):
                for p in ((SIBLING,) + SAME_CORE_PEERS if gi in relayed else range(1, N_DEV)):
                    peer, peer_id = _peer(x, y, c, p)
                    src, dst, _, _, _ = _copy_ends(gather, ins[w], lnd[w], me, peer_id)
                    pltpu.make_async_remote_copy(src, dst, sems[2 * gi].at[k * PEERS + p - 1],
                                                 sems[2 * gi + 1].at[k * PEERS + p - 1], device_id=peer,
                                                 device_id_type=MESH_IDS).start()
                w += 1
        token[...] = jnp.zeros_like(token)

    sem_shapes = []
    for g in groups:
        sem_shapes += [pltpu.SemaphoreType.DMA((len(g) * PEERS,))] * 2
    out = pl.pallas_call(
        body, name=name,
        out_shape=(*sem_shapes, *[pltpu.HBM(s.shape, s.dtype) for s in flat],
                   *[pltpu.HBM(l.shape, l.dtype) for l in lands], jax.ShapeDtypeStruct((8, LANES), F32)),
        in_specs=[HBM_SPEC] * (2 * n) + [pl.BlockSpec(memory_space=pl.ANY)] * (after is not None),
        out_specs=(*[SEM_SPEC] * (2 * ng), *[HBM_SPEC] * (2 * n), pl.BlockSpec(memory_space=pltpu.VMEM)),
        input_output_aliases={i: 2 * ng + i for i in range(2 * n)},
        compiler_params=SPLIT_COPY,
    )(*[_hbm(s) for s in flat], *[_hbm(l) for l in lands], *([after] if after is not None else []))
    sems, thru, token = out[:2 * ng], out[2 * ng:2 * ng + 2 * n], out[-1]
    res, w = [], 0
    for gi, g in enumerate(groups):
        res.append((sems[2 * gi], sems[2 * gi + 1], list(thru[w:w + len(g)]), list(thru[n + w:n + w + len(g)])))
        w += len(g)
    return res, token


def _copies_wait(started, gather, after, name):
    send, recv, srcs, lands = started
    n = len(srcs)
    after = list(after) if isinstance(after, (list, tuple)) else [after]

    own_shapes = [s.shape if gather else (s.shape[0] // N_DEV, s.shape[1]) for s in srcs]

    def body(*refs):
        ins, lnd = refs[:n], refs[n:2 * n]
        send_sems, recv_sems = refs[2 * n], refs[2 * n + 1]
        bounce, in_sems, out_sems = refs[-n - 2:-2], refs[-2], refs[-1]
        x, y, c = _me()
        me = 4 * x + 2 * y + c
        ends = [_copy_ends(gather, ins[w], lnd[w], me, me)[3:] for w in range(n)]
        loads = [pltpu.make_async_copy(ends[w][0], bounce[w], in_sems.at[w]) for w in range(n)]
        stores = [pltpu.make_async_copy(bounce[w], ends[w][1], out_sems.at[w]) for w in range(n)]
        for cp in loads:
            cp.start()
        for w in range(n):
            loads[w].wait()
            stores[w].start()
        for w in range(n):
            for p in range(1, N_DEV):
                peer, peer_id = _peer(x, y, c, p)
                src, _, arrival, _, _ = _copy_ends(gather, ins[w], lnd[w], me, peer_id)
                cp = pltpu.make_async_remote_copy(src, arrival, send_sems.at[w * PEERS + p - 1],
                                                  recv_sems.at[w * PEERS + p - 1], device_id=peer,
                                                  device_id_type=MESH_IDS)
                cp.wait_send()
                cp.wait_recv()
        for cp in stores:
            cp.wait()

    out = pl.pallas_call(
        body, name=name,
        out_shape=(*[pltpu.HBM(s.shape, s.dtype) for s in srcs], *[pltpu.HBM(l.shape, l.dtype) for l in lands]),
        in_specs=[HBM_SPEC] * (2 * n) + [SEM_SPEC, SEM_SPEC] + [pl.BlockSpec(memory_space=pl.ANY)] * len(after),
        out_specs=[HBM_SPEC] * (2 * n),
        input_output_aliases={i: i for i in range(2 * n)},
        scratch_shapes=[*[pltpu.VMEM(shape, s.dtype) for shape, s in zip(own_shapes, srcs)],
                        pltpu.SemaphoreType.DMA((n,)), pltpu.SemaphoreType.DMA((n,))],
        compiler_params=SPLIT_COPY,
    )(*srcs, *lands, send, recv, *after)
    return list(out[n:])


def _relay_to_sibling(started, name):
    send, recv, srcs, lands = started
    n = len(srcs)

    def body(*refs):
        ins, lnd = refs[:n], refs[n:2 * n]
        send_sems, recv_sems = refs[2 * n], refs[2 * n + 1]
        relay_send, relay_recv = refs[2 * n + 2], refs[2 * n + 3]
        x, y, c = _me()
        sibling, _ = _peer(x, y, c, SIBLING)
        for w in range(n):
            rows = ins[w].shape[0]
            for k, p in enumerate(SAME_CORE_PEERS):
                peer, peer_id = _peer(x, y, c, p)
                arrived = _row_block(lnd[w], peer_id, rows)
                first = pltpu.make_async_remote_copy(ins[w], arrived, send_sems.at[w * PEERS + p - 1],
                                                     recv_sems.at[w * PEERS + p - 1], device_id=peer,
                                                     device_id_type=MESH_IDS)
                first.wait_recv()
                pltpu.make_async_remote_copy(arrived, arrived, relay_send.at[w * RELAYS + k],
                                             relay_recv.at[w * RELAYS + k], device_id=sibling,
                                             device_id_type=MESH_IDS).start()
                first.wait_send()

    sems = pltpu.SemaphoreType.DMA((n * RELAYS,))
    out = pl.pallas_call(
        body, name=name,
        out_shape=(sems, sems, *[pltpu.HBM(s.shape, s.dtype) for s in srcs], *[pltpu.HBM(l.shape, l.dtype) for l in lands]),
        in_specs=[HBM_SPEC] * (2 * n) + [SEM_SPEC, SEM_SPEC],
        out_specs=(SEM_SPEC, SEM_SPEC, *[HBM_SPEC] * (2 * n)),
        input_output_aliases={i: 2 + i for i in range(2 * n)},
        compiler_params=SPLIT_COPY,
    )(*srcs, *lands, send, recv)
    return send, recv, out[0], out[1], list(out[2:2 + n]), list(out[2 + n:])


def _relayed_wait(relayed, after, name):
    send, recv, relay_send, relay_recv, srcs, lands = relayed
    n = len(srcs)
    after = list(after) if isinstance(after, (list, tuple)) else [after]

    def body(*refs):
        ins, lnd = refs[:n], refs[n:2 * n]
        send_sems, recv_sems, relay_send_sems, relay_recv_sems = refs[2 * n:2 * n + 4]
        bounce, in_sems, out_sems = refs[-n - 2:-2], refs[-2], refs[-1]
        x, y, c = _me()
        me = 4 * x + 2 * y + c
        sibling, sibling_id = _peer(x, y, c, SIBLING)
        loads = [pltpu.make_async_copy(ins[w], bounce[w], in_sems.at[w]) for w in range(n)]
        stores = [pltpu.make_async_copy(bounce[w], _row_block(lnd[w], me, ins[w].shape[0]), out_sems.at[w])
                  for w in range(n)]
        for cp in loads:
            cp.start()
        for w in range(n):
            loads[w].wait()
            stores[w].start()
        for w in range(n):
            rows = ins[w].shape[0]
            direct = pltpu.make_async_remote_copy(ins[w], _row_block(lnd[w], sibling_id, rows),
                                                  send_sems.at[w * PEERS + SIBLING - 1],
                                                  recv_sems.at[w * PEERS + SIBLING - 1], device_id=sibling,
                                                  device_id_type=MESH_IDS)
            direct.wait_send()
            direct.wait_recv()
            for k, p in enumerate(SAME_CORE_PEERS):
                _, sent_id = _peer(x, y, c, p)
                _, got_id = _peer(x, y, c, p + SIBLING)
                relay = pltpu.make_async_remote_copy(_row_block(lnd[w], sent_id, rows), _row_block(lnd[w], got_id, rows),
                                                     relay_send_sems.at[w * RELAYS + k],
                                                     relay_recv_sems.at[w * RELAYS + k], device_id=sibling,
                                                     device_id_type=MESH_IDS)
                relay.wait_send()
                relay.wait_recv()
        for cp in stores:
            cp.wait()

    out = pl.pallas_call(
        body, name=name,
        out_shape=(*[pltpu.HBM(s.shape, s.dtype) for s in srcs], *[pltpu.HBM(l.shape, l.dtype) for l in lands]),
        in_specs=[HBM_SPEC] * (2 * n) + [SEM_SPEC] * 4 + [pl.BlockSpec(memory_space=pl.ANY)] * len(after),
        out_specs=[HBM_SPEC] * (2 * n),
        input_output_aliases={i: i for i in range(2 * n)},
        scratch_shapes=[*[pltpu.VMEM(s.shape, s.dtype) for s in srcs],
                        pltpu.SemaphoreType.DMA((n,)), pltpu.SemaphoreType.DMA((n,))],
        compiler_params=SPLIT_COPY,
    )(*srcs, *lands, send, recv, relay_send, relay_recv, *after)
    return list(out[n:])


def _adamw_update(w, g, m, v):
    nm = ADAM_B1 * m + (1.0 - ADAM_B1) * g
    nv = ADAM_B2 * v + (1.0 - ADAM_B2) * (g * g)
    m_hat = nm / (1.0 - ADAM_B1 ** ADAM_STEP)
    v_hat = nv / (1.0 - ADAM_B2 ** ADAM_STEP)
    return -ADAM_LR * (m_hat / (jnp.sqrt(v_hat) + ADAM_EPS) + ADAM_WD * w), nm, nv


SUM_ADAMW_COLS = 256


def _sum_adamw(parts, w, m, v, name):
    _, rows, d = parts.shape
    n = w.shape[0]
    tc = SUM_ADAMW_COLS

    def body(p_ref, w_ref, m_ref, v_ref, g_ref, d_ref, nm_ref, nv_ref):
        g = p_ref[0].astype(F32)
        for dev in range(1, N_DEV):
            g = g + p_ref[dev].astype(F32)
        g = g[:n]
        g_ref[...] = g
        d_ref[...], nm_ref[...], nv_ref[...] = _adamw_update(w_ref[...], g, m_ref[...], v_ref[...])

    spec = pl.BlockSpec((n, tc), lambda j: (0, j))
    shape = jax.ShapeDtypeStruct((n, d), F32)
    return pl.pallas_call(
        body, name=name, grid=(d // tc,),
        in_specs=[pl.BlockSpec((N_DEV, rows, tc), lambda j: (0, 0, j)), spec, spec, spec],
        out_specs=[spec] * 4, out_shape=[shape] * 4,
        compiler_params=_params("arbitrary"),
    )(parts, w, m, v)


def _pad_rows(a, rows):
    return jnp.pad(a, ((0, rows - a.shape[0]), (0, 0)))


def _row1(vec, width=D_MODEL):
    return jnp.pad(vec.reshape(1, -1), ((0, 0), (0, width - vec.shape[-1])))


COLUMN_SHARDED = ("ffn1_w_gate", "ffn1_w_up", "w_in", "ffn2_w_gate", "ffn2_w_up")
VEC_NAMES = ("ffn1_norm", "mix_norm", "ffn2_norm", "b_forget", "pool_scale", "q_norm", "k_norm", "out_norm_pool",
             "out_norm_attn")
VEC_ROWS = 16
LOSS_ROW = len(VEC_NAMES)


def _pack_vector_grads(parts, loss_part):
    def body(*refs):
        loss_ref, out_ref = refs[-2], refs[-1]
        out_ref[...] = jnp.zeros_like(out_ref)
        lane = lax.broadcasted_iota(jnp.int32, (1, LANES), 1)
        for i, (name, ref) in enumerate(zip(VEC_NAMES, refs[:-2])):
            val = ref[...]
            if name in ("q_norm", "k_norm"):
                val = val[:, 0:LANES] + val[:, LANES:2 * LANES] + val[:, 2 * LANES:3 * LANES] + val[:, 3 * LANES:]
                val = jnp.where(lane < HEAD_DIM, val + pltpu.roll(val, HEAD_DIM, 1), 0.0)
            out_ref[pl.ds(i, 1), pl.ds(0, val.shape[1])] = val
        out_ref[pl.ds(LOSS_ROW, 1), pl.ds(0, 1)] = loss_ref[...]

    vmem = pl.BlockSpec(memory_space=pltpu.VMEM)
    return pl.pallas_call(
        body, name="pack_vector_grads", in_specs=[vmem] * (len(parts) + 1), out_specs=vmem,
        out_shape=jax.ShapeDtypeStruct((VEC_ROWS, D_MODEL), F32),
    )(*parts, loss_part)


def _small_adamw(vec_all, pool_all, vec_params, pool_params):
    nv = len(vec_params)
    pool_rows = pool_params[0].shape[0]

    def body(*refs):
        vec_ref, pool_ref = refs[0], refs[1]
        ins = refs[2:2 + 3 * nv + 3]
        outs = refs[2 + 3 * nv + 3:-1]
        rows = refs[-1]
        total = vec_ref[pl.ds(0, VEC_ROWS), :]
        for dev in range(1, N_DEV):
            total = total + vec_ref[pl.ds(dev * VEC_ROWS, VEC_ROWS), :]
        rows[...] = total
        outs[4 * nv + 4][...] = rows[pl.ds(LOSS_ROW, 1), pl.ds(0, 1)]
        for i in range(nv):
            w_ref, m_ref, v_ref = ins[3 * i:3 * i + 3]
            g = rows[pl.ds(i, 1), pl.ds(0, w_ref.shape[1])]
            outs[4 * i][...] = g
            outs[4 * i + 1][...], outs[4 * i + 2][...], outs[4 * i + 3][...] = _adamw_update(
                w_ref[...], g, m_ref[...], v_ref[...])
        g = pool_ref[pl.ds(0, pool_rows), :]
        for dev in range(1, N_DEV):
            g = g + pool_ref[pl.ds(dev * pool_rows, pool_rows), :]
        w_ref, m_ref, v_ref = ins[3 * nv:]
        outs[4 * nv][...] = g
        outs[4 * nv + 1][...], outs[4 * nv + 2][...], outs[4 * nv + 3][...] = _adamw_update(
            w_ref[...], g, m_ref[...], v_ref[...])

    vmem = pl.BlockSpec(memory_space=pltpu.VMEM)
    flat = [a for trio in vec_params for a in trio] + list(pool_params)
    out_shape = []
    for trio in list(vec_params) + [pool_params]:
        out_shape += [jax.ShapeDtypeStruct(trio[0].shape, F32)] * 4
    out_shape.append(jax.ShapeDtypeStruct((1, 1), F32))
    return pl.pallas_call(
        body, name="adamw_small", in_specs=[vmem] * (2 + len(flat)), out_specs=[vmem] * len(out_shape),
        out_shape=out_shape, scratch_shapes=[pltpu.VMEM((VEC_ROWS, D_MODEL), F32)],
    )(vec_all, pool_all, *flat)


def kernel(x, ffn1_norm, ffn1_w_gate, ffn1_w_up, ffn1_w_down, mix_norm, w_in, b_forget, pool_w, pool_scale, q_norm, k_norm, out_norm_pool, out_norm_attn, w_out, ffn2_norm, ffn2_w_gate, ffn2_w_up, ffn2_w_down, loss_target, m_ffn1_norm, m_ffn1_w_gate, m_ffn1_w_up, m_ffn1_w_down, m_mix_norm, m_w_in, m_b_forget, m_pool_w, m_pool_scale, m_q_norm, m_k_norm, m_out_norm_pool, m_out_norm_attn, m_w_out, m_ffn2_norm, m_ffn2_w_gate, m_ffn2_w_up, m_ffn2_w_down, v_ffn1_norm, v_ffn1_w_gate, v_ffn1_w_up, v_ffn1_w_down, v_mix_norm, v_w_in, v_b_forget, v_pool_w, v_pool_scale, v_q_norm, v_k_norm, v_out_norm_pool, v_out_norm_attn, v_w_out, v_ffn2_norm, v_ffn2_w_gate, v_ffn2_w_up, v_ffn2_w_down):
    bsz, seq, d = x.shape
    t = bsz * seq
    x0 = x.reshape(t, d)
    target = loss_target.reshape(t, d)
    in_rows = -(-w_in.shape[1] // BF16_ROWS) * BF16_ROWS

    slabs = [s.astype(BF16) for s in (ffn1_w_gate.T, ffn1_w_up.T, ffn1_w_down, _pad_rows(w_in.T, in_rows), w_out,
                                       ffn2_w_gate.T, ffn2_w_up.T, ffn2_w_down)]
    gathers, started = _copies_start([slabs[0:2], slabs[2:3], slabs[3:4], slabs[4:5], slabs[5:8]], True, "gather_start",
                                     relayed=(0,))

    g1, gm, g2 = ffn1_norm.reshape(1, d), mix_norm.reshape(1, d), ffn2_norm.reshape(1, d)
    bf_row = _row1(b_forget, LANES)
    gq = jnp.tile(q_norm, N_HEADS).reshape(1, ATTN_WIDTH)
    gk = jnp.tile(k_norm, N_HEADS).reshape(1, ATTN_WIDTH)
    scale_row = pool_scale.reshape(1, POOL_WIDTH)
    gp, ga = out_norm_pool.reshape(1, POOL_WIDTH), out_norm_attn.reshape(1, ATTN_WIDTH)

    wg1, wu1 = _relayed_wait(_relay_to_sibling(gathers[0], "gather_relay_ffn1_up"), started, "gather_wait_ffn1_up")
    h1, sa1, sb1, s1 = _ffn_up(x0, g1, wg1, wu1, "ffn1_up")
    (wd1,) = _copies_wait(gathers[1], True, s1, "gather_wait_ffn1_down")
    (x1,) = _ffn_down(s1, wd1, x0, None, "ffn1_down")
    (win_g,) = _copies_wait(gathers[2], True, x1, "gather_wait_w_in")
    win_cols = win_g.reshape(N_DEV, in_rows, d)[:, :w_in.shape[1]].reshape(MIX_COLS, d)
    win_t = _pad_rows(win_cols, MIX_PAD)
    hm, pv, q, k, v, f = _mix_in_fwd(x1, gm, win_t)
    pooled, mixed, y_pool = _pool_fwd(pv, pool_w, scale_row, gp, bsz, seq)
    qp, kp = _attn_prep_fwd(q, k, f, bf_row, gq, gk, bsz, seq)
    o, lse = _flash_fwd(qp, kp, v, bsz, seq)
    (wout,) = _copies_wait(gathers[3], True, o, "gather_wait_w_out")
    ycat, x2 = _mix_out_fwd(o, y_pool, x1, ga, wout)
    wg2, wu2, wd2 = _copies_wait(gathers[4], True, x2, "gather_wait_ffn2")
    h2, sa2, sb2, s2 = _ffn_up(x2, g2, wg2, wu2, "ffn2_up")
    dx3, dyh2, loss_part = _ffn_down(s2, wd2, x2, target, "ffn2_down")

    da2, db2 = _ffn_bwd_act(dyh2, sa2, sb2, wd2, "ffn2_bwd_act")
    dwg2, dwu2 = _wgrad([da2, db2], h2, "ffn2_up_wgrad")
    (dwd2,) = _wgrad([s2], dyh2, "ffn2_down_wgrad")
    (sent_ffn2,), tok = _copies_start([[dwg2, dwu2, dwd2]], False, "exchange_start_ffn2")
    dx2, dg2 = _ffn_bwd_dx(da2, db2, dx3, x2, g2 + tok[0, 0], wg2, wu2, "ffn2_bwd_dx")
    dx2b, dy_pool, do, dga = _mix_out_bwd(dx2, o, ga, wout)
    (dwout,) = _wgrad([ycat], dx2b, "w_out_wgrad")
    (sent_out,), tok = _copies_start([[dwout]], False, "exchange_start_w_out")
    dqp, dkp, dv = _flash_bwd(qp, kp, v, o, do, lse, bsz, seq)
    dq, dk, df, dgq, dgk, dbf = _attn_prep_bwd(dqp, dkp, q, k, f, bf_row + tok[0, 0], gq, gk, bsz, seq)
    dpv, dpool_w, dscale, dgp = _pool_bwd(dy_pool, mixed, pooled, pool_w, scale_row, gp, bsz, seq)
    dhcat, dx1, dyh1, dgm = _mix_in_bwd(dpv, dq, dk, dv, df, x1, dx2, gm, win_t)
    (dwin,) = _wgrad([dhcat], hm, "w_in_wgrad")
    dwin_blocks = jnp.pad(dwin[:MIX_COLS].reshape(N_DEV, w_in.shape[1], d), ((0, 0), (0, in_rows - w_in.shape[1]), (0, 0)))
    (sent_in,), tok = _copies_start([[dwin_blocks.reshape(N_DEV * in_rows, d)]], False, "exchange_start_w_in")
    (dwd1,) = _wgrad([s1], dyh1, "ffn1_down_wgrad")
    (sent_down1,), tok = _copies_start([[dwd1]], False, "exchange_start_ffn1_down", after=tok)
    da1, db1 = _ffn_bwd_act(dyh1, sa1, sb1, wd1, "ffn1_bwd_act")
    dwg1, dwu1 = _wgrad([da1, db1], h1, "ffn1_up_wgrad")
    (sent_up1,), tok = _copies_start([[dwg1, dwu1]], False, "exchange_start_ffn1_up", after=tok)
    dx0, dg1 = _ffn_bwd_dx(da1, db1, dx1, x0, g1 + tok[0, 0], wg1, wu1, "ffn1_bwd_dx")

    pool_rows = POOL_GROUPS * POOL_GROUP_DIM
    packed = _pack_vector_grads([dg1, dgm, dg2, dbf, dscale, dgq, dgk, dgp, dga], loss_part)
    (sent_small,), tok = _copies_start([[packed, dpool_w.reshape(pool_rows, POOL_GROUP_DIM)]], True, "small_grads_start")

    weights = dict(ffn1_norm=ffn1_norm, ffn1_w_gate=ffn1_w_gate, ffn1_w_up=ffn1_w_up, ffn1_w_down=ffn1_w_down,
                   mix_norm=mix_norm, w_in=w_in, b_forget=b_forget, pool_w=pool_w, pool_scale=pool_scale,
                   q_norm=q_norm, k_norm=k_norm, out_norm_pool=out_norm_pool, out_norm_attn=out_norm_attn,
                   w_out=w_out, ffn2_norm=ffn2_norm, ffn2_w_gate=ffn2_w_gate, ffn2_w_up=ffn2_w_up,
                   ffn2_w_down=ffn2_w_down)
    m_in = dict(ffn1_norm=m_ffn1_norm, ffn1_w_gate=m_ffn1_w_gate, ffn1_w_up=m_ffn1_w_up, ffn1_w_down=m_ffn1_w_down,
                mix_norm=m_mix_norm, w_in=m_w_in, b_forget=m_b_forget, pool_w=m_pool_w, pool_scale=m_pool_scale,
                q_norm=m_q_norm, k_norm=m_k_norm, out_norm_pool=m_out_norm_pool, out_norm_attn=m_out_norm_attn,
                w_out=m_w_out, ffn2_norm=m_ffn2_norm, ffn2_w_gate=m_ffn2_w_gate, ffn2_w_up=m_ffn2_w_up,
                ffn2_w_down=m_ffn2_w_down)
    v_in = dict(ffn1_norm=v_ffn1_norm, ffn1_w_gate=v_ffn1_w_gate, ffn1_w_up=v_ffn1_w_up, ffn1_w_down=v_ffn1_w_down,
                mix_norm=v_mix_norm, w_in=v_w_in, b_forget=v_b_forget, pool_w=v_pool_w, pool_scale=v_pool_scale,
                q_norm=v_q_norm, k_norm=v_k_norm, out_norm_pool=v_out_norm_pool, out_norm_attn=v_out_norm_attn,
                w_out=v_w_out, ffn2_norm=v_ffn2_norm, ffn2_w_gate=v_ffn2_w_gate, ffn2_w_up=v_ffn2_w_up,
                ffn2_w_down=v_ffn2_w_down)
    grads, delta, new_m, new_v = {}, {}, {}, {}
    after = [tok]
    plan = ((sent_ffn2, "ffn2", ("ffn2_w_gate", "ffn2_w_up", "ffn2_w_down")), (sent_out, "w_out", ("w_out",)),
            (sent_in, "w_in", ("w_in",)), (sent_down1, "ffn1_down", ("ffn1_w_down",)),
            (sent_up1, "ffn1_up", ("ffn1_w_gate", "ffn1_w_up")))
    for sent, tag, names in plan:
        parts = _copies_wait(sent, False, after, f"exchange_wait_{tag}")
        after = []
        for n, part in zip(names, parts):
            turn = (lambda a: a.T) if n in COLUMN_SHARDED else (lambda a: a)
            done = _sum_adamw(part, turn(weights[n]), turn(m_in[n]), turn(v_in[n]), f"adamw_{n}")
            grads[n], delta[n], new_m[n], new_v[n] = (turn(a) for a in done)
            after.append(done[3])
    vec_all, pool_all = _copies_wait(sent_small, True, after, "small_grads_wait")
    as_row = lambda a: a.reshape(1, -1)
    as_pool = lambda a: a.reshape(pool_rows, POOL_GROUP_DIM)
    small = _small_adamw(vec_all, pool_all, [tuple(as_row(z[n]) for z in (weights, m_in, v_in)) for n in VEC_NAMES],
                         tuple(as_pool(z["pool_w"]) for z in (weights, m_in, v_in)))
    for i, n in enumerate(VEC_NAMES + ("pool_w",)):
        grads[n], delta[n], new_m[n], new_v[n] = (a.reshape(weights[n].shape) for a in small[4 * i:4 * i + 4])
    loss = small[-1].reshape(())

    order = ("ffn1_norm", "ffn1_w_gate", "ffn1_w_up", "ffn1_w_down", "mix_norm", "w_in", "b_forget", "pool_w",
             "pool_scale", "q_norm", "k_norm", "out_norm_pool", "out_norm_attn", "w_out", "ffn2_norm", "ffn2_w_gate",
             "ffn2_w_up", "ffn2_w_down")
    return (loss, dx0.reshape(bsz, seq, d), *[grads[n] for n in order], *[delta[n] for n in order],
            *[new_m[n] for n in order], *[new_v[n] for n in order])
```

```python
import functools

import jax
import jax.numpy as jnp
from jax import lax
from jax.experimental import pallas as pl
from jax.experimental.pallas import tpu as pltpu

F32 = jnp.float32
BF16 = jnp.bfloat16

EPS = 1e-6
D_MODEL = 1024
D_FF = 2816
N_HEADS = 8
HEAD_DIM = 64
POOL_WIDTH = 512
ATTN_WIDTH = 512
POOL_GROUPS = 4
POOL_GROUP_DIM = 128
POOL_WINDOWS = (2, 4, 8, 16)
POOL_HALO = 16
MIX_PAD = POOL_WIDTH + 3 * ATTN_WIDTH + 128
N_DEV = 8
BF16_ROWS = 16
LANES = 128
VMEM_LIMIT = 56 * 1024 * 1024

ADAM_LR = 0.001
ADAM_B1 = 0.9
ADAM_B2 = 0.999
ADAM_EPS = 1e-08
ADAM_WD = 0.01
ADAM_STEP = 10


def _params(*sem):
    return pltpu.CompilerParams(dimension_semantics=sem, vmem_limit_bytes=VMEM_LIMIT)


def _dot(a, b):
    return jnp.dot(a, b, preferred_element_type=F32)


def _dot_nt(a, b):
    return lax.dot_general(a, b, (((1,), (1,)), ((), ())), preferred_element_type=F32)


def _dot_tn(a, b):
    return lax.dot_general(a, b, (((0,), (0,)), ((), ())), preferred_element_type=F32)


def _resident(shape):
    return pl.BlockSpec(shape, lambda *_: (0,) * len(shape), pipeline_mode=pl.Buffered(1))


def _rows(tm, width):
    return pl.BlockSpec((tm, width), lambda i: (i, 0))


def _rms_scale(x):
    return lax.rsqrt(jnp.mean(x * x, axis=-1, keepdims=True) + EPS)


def _rms_bwd(dh, x, gain):
    r = _rms_scale(x)
    n = x * r
    dgain = jnp.sum(dh * n, axis=0, keepdims=True)
    dn = dh * gain
    dx = r * (dn - n * jnp.mean(dn * n, axis=-1, keepdims=True))
    return dx, dgain


def _split3(x):
    hi = x.astype(BF16)
    r1 = x - hi.astype(F32)
    mid = r1.astype(BF16)
    lo = (r1 - mid.astype(F32)).astype(BF16)
    return hi, mid, lo


def _split2(x):
    hi = x.astype(BF16)
    return hi, (x - hi.astype(F32)).astype(BF16)


FF_CHUNK = 256


def _ffn_up(x, gain, wg_t, wu_t, name):
    t, d = x.shape
    f = wg_t.shape[0]
    tm = 512

    def body(x_ref, g_ref, wg_ref, wu_ref, h_ref, sa_ref, sb_ref, s_ref):
        xv = x_ref[...]
        h = (xv * _rms_scale(xv) * g_ref[...]).astype(BF16)
        h_ref[...] = h
        for c in range(f // FF_CHUNK):
            sl = pl.ds(c * FF_CHUNK, FF_CHUNK)
            a = _dot_nt(h, wg_ref[sl, :])
            b = _dot_nt(h, wu_ref[sl, :])
            sig = jax.nn.sigmoid(a)
            silu = a * sig
            sa_ref[:, sl] = (b * (sig + silu * (1.0 - sig))).astype(BF16)
            sb_ref[:, sl] = silu.astype(BF16)
            s_ref[:, sl] = (silu * b).astype(BF16)

    wide = jax.ShapeDtypeStruct((t, f), BF16)
    return pl.pallas_call(
        body, name=name, grid=(t // tm,),
        in_specs=[_rows(tm, d), _resident((1, d)), _resident((f, d)), _resident((f, d))],
        out_specs=[_rows(tm, d), _rows(tm, f), _rows(tm, f), _rows(tm, f)],
        out_shape=[jax.ShapeDtypeStruct((t, d), BF16), wide, wide, wide],
        compiler_params=_params("arbitrary"),
    )(x, gain, wg_t, wu_t)


def _ffn_down(s, wd, x, target, name):
    t, d = x.shape
    f = wd.shape[0]
    tm = 512
    with_loss = target is not None

    def body(*refs):
        if with_loss:
            s_ref, w_ref, x_ref, t_ref, dy_ref, dyh_ref, loss_ref = refs
        else:
            s_ref, w_ref, x_ref, y_ref = refs
        y = x_ref[...] + 0.5 * _dot(s_ref[...], w_ref[...])
        if with_loss:
            e = y - t_ref[...]
            dy = e * (1.0 / d)
            dy_ref[...] = dy
            dyh_ref[...] = (0.5 * dy).astype(BF16)

            @pl.when(pl.program_id(0) == 0)
            def _():
                loss_ref[...] = jnp.zeros_like(loss_ref)

            part = jnp.sum(jnp.sum(e * e, axis=0, keepdims=True), axis=1, keepdims=True)
            loss_ref[...] += part * (0.5 / d)
        else:
            y_ref[...] = y

    in_specs = [_rows(tm, f), _resident((f, d)), _rows(tm, d)]
    args = [s, wd, x]
    if with_loss:
        in_specs.append(_rows(tm, d))
        args.append(target)
        out_shape = [jax.ShapeDtypeStruct((t, d), F32), jax.ShapeDtypeStruct((t, d), BF16),
                     jax.ShapeDtypeStruct((1, 1), F32)]
        out_specs = [_rows(tm, d), _rows(tm, d), pl.BlockSpec((1, 1), lambda i: (0, 0))]
    else:
        out_shape = [jax.ShapeDtypeStruct((t, d), F32)]
        out_specs = [_rows(tm, d)]
    return pl.pallas_call(
        body, name=name, grid=(t // tm,), in_specs=in_specs, out_specs=out_specs, out_shape=out_shape,
        compiler_params=_params("arbitrary"),
    )(*args)


def _ffn_bwd_act(dyh, sa, sb, h, wd, name):
    t, d = dyh.shape
    f = wd.shape[0]
    tn = f // 2
    tk = 512
    nk = t // tk

    def body(dy_ref, sa_ref, sb_ref, h_ref, wd_ref, da_ref, db_ref, dwg_ref, dwu_ref, acc_g, acc_u):
        k = pl.program_id(1)

        @pl.when(k == 0)
        def _():
            acc_g[...] = jnp.zeros_like(acc_g)
            acc_u[...] = jnp.zeros_like(acc_u)

        ds = _dot_nt(dy_ref[...], wd_ref[...])
        da = (ds * sa_ref[...].astype(F32)).astype(BF16)
        db = (ds * sb_ref[...].astype(F32)).astype(BF16)
        da_ref[...] = da
        db_ref[...] = db
        hv = h_ref[...]
        acc_g[...] += _dot_tn(da, hv)
        acc_u[...] += _dot_tn(db, hv)

        @pl.when(k == nk - 1)
        def _():
            dwg_ref[...] = acc_g[...].astype(BF16)
            dwu_ref[...] = acc_u[...].astype(BF16)

    tokens = pl.BlockSpec((tk, d), lambda j, k: (k, 0))
    wide = pl.BlockSpec((tk, tn), lambda j, k: (k, j))
    weight = pl.BlockSpec((tn, d), lambda j, k: (j, 0))
    return pl.pallas_call(
        body, name=name, grid=(f // tn, nk),
        in_specs=[tokens, wide, wide, tokens, weight],
        out_specs=[wide, wide, weight, weight],
        out_shape=[jax.ShapeDtypeStruct((t, f), BF16)] * 2 + [jax.ShapeDtypeStruct((f, d), BF16)] * 2,
        scratch_shapes=[pltpu.VMEM((tn, d), F32)] * 2,
        compiler_params=_params("arbitrary", "arbitrary"),
    )(dyh, sa, sb, h, wd)


def _ffn_bwd_dx(da, db, dy, x, gain, wg_t, wu_t, name):
    t, d = x.shape
    f = wg_t.shape[0]
    tm = 512

    def body(da_ref, db_ref, dy_ref, x_ref, g_ref, wg_ref, wu_ref, dx_ref, dg_ref):
        dh = _dot(da_ref[...], wg_ref[...]) + _dot(db_ref[...], wu_ref[...])
        dx, dgain = _rms_bwd(dh, x_ref[...], g_ref[...])
        dx_ref[...] = dy_ref[...] + dx

        @pl.when(pl.program_id(0) == 0)
        def _():
            dg_ref[...] = jnp.zeros_like(dg_ref)

        dg_ref[...] += dgain

    return pl.pallas_call(
        body, name=name, grid=(t // tm,),
        in_specs=[_rows(tm, f), _rows(tm, f), _rows(tm, d), _rows(tm, d), _resident((1, d)), _resident((f, d)),
                  _resident((f, d))],
        out_specs=[_rows(tm, d), pl.BlockSpec((1, d), lambda i: (0, 0))],
        out_shape=[jax.ShapeDtypeStruct((t, d), F32), jax.ShapeDtypeStruct((1, d), F32)],
        compiler_params=_params("arbitrary"),
    )(da, db, dy, x, gain, wg_t, wu_t)


def _wgrad(lhs, b, name):
    t, n = lhs[0].shape
    d = b.shape[1]
    m = len(lhs)
    tn = n // 2 if n * d * m > (4 << 20) else n
    tk = 1024
    nk = t // tk

    def body(*refs):
        a_refs, b_ref, o_refs, accs = refs[:m], refs[m], refs[m + 1:2 * m + 1], refs[2 * m + 1:]
        k = pl.program_id(1)

        @pl.when(k == 0)
        def _():
            for acc in accs:
                acc[...] = jnp.zeros_like(acc)

        bv = b_ref[...]
        for a_ref, acc in zip(a_refs, accs):
            acc[...] += _dot_tn(a_ref[...], bv)

        @pl.when(k == nk - 1)
        def _():
            for o_ref, acc in zip(o_refs, accs):
                o_ref[...] = acc[...].astype(BF16)

    return pl.pallas_call(
        body, name=name, grid=(n // tn, nk),
        in_specs=[pl.BlockSpec((tk, tn), lambda j, k: (k, j))] * m + [pl.BlockSpec((tk, d), lambda j, k: (k, 0))],
        out_specs=[pl.BlockSpec((tn, d), lambda j, k: (j, 0))] * m,
        out_shape=[jax.ShapeDtypeStruct((n, d), BF16)] * m,
        scratch_shapes=[pltpu.VMEM((tn, d), F32)] * m,
        compiler_params=_params("arbitrary", "arbitrary"),
    )(*lhs, b)


def _repack_rows(a, rows_in, rows_out, blocks, name):
    total, d = a.shape
    real = min(rows_in, rows_out)

    def body(a_ref, o_ref, wide_in, wide_out):
        wide_in[...] = a_ref[...].astype(F32)
        wide_out[...] = jnp.zeros_like(wide_out)
        for j in range(blocks):
            wide_out[pl.ds(j * rows_out, real), :] = wide_in[pl.ds(j * rows_in, real), :]
        o_ref[...] = wide_out[...].astype(BF16)

    full = pl.BlockSpec((total, d), lambda i: (0, 0))
    return pl.pallas_call(
        body, name=name, grid=(1,), in_specs=[full], out_specs=full, out_shape=jax.ShapeDtypeStruct((total, d), BF16),
        scratch_shapes=[pltpu.VMEM((total, d), F32)] * 2,
        compiler_params=_params("arbitrary"),
    )(a)


def _mix_in_fwd(x, gain, w_in_t):
    t, d = x.shape
    tm = 1024
    pw, aw = POOL_WIDTH, ATTN_WIDTH

    def body(x_ref, g_ref, w_ref, hm_ref, pv_ref, q_ref, k_ref, v_ref, f_ref):
        xv = x_ref[...]
        hm = (xv * _rms_scale(xv) * g_ref[...]).astype(BF16)
        hm_ref[...] = hm
        pv_ref[...] = _dot_nt(hm, w_ref[pl.ds(0, pw), :])
        q_ref[...] = _dot_nt(hm, w_ref[pl.ds(pw, aw), :])
        k_ref[...] = _dot_nt(hm, w_ref[pl.ds(pw + aw, aw), :])
        v_ref[...] = _dot_nt(hm, w_ref[pl.ds(pw + 2 * aw, aw), :]).astype(BF16)
        f_ref[...] = _dot_nt(hm, w_ref[pl.ds(pw + 3 * aw, LANES), :])

    return pl.pallas_call(
        body, name="mix_in_fwd", grid=(t // tm,),
        in_specs=[_rows(tm, d), _resident((1, d)), _resident((MIX_PAD, d))],
        out_specs=[_rows(tm, d), _rows(tm, pw), _rows(tm, aw), _rows(tm, aw), _rows(tm, aw), _rows(tm, LANES)],
        out_shape=[jax.ShapeDtypeStruct((t, d), BF16), jax.ShapeDtypeStruct((t, pw), F32),
                   jax.ShapeDtypeStruct((t, aw), F32), jax.ShapeDtypeStruct((t, aw), F32),
                   jax.ShapeDtypeStruct((t, aw), BF16), jax.ShapeDtypeStruct((t, LANES), F32)],
        compiler_params=_params("arbitrary"),
    )(x, gain, w_in_t)


def _pool_fwd(pv, pool_w, pool_scale, gain, bsz, seq):
    ts = 512
    ns = seq // ts
    pw = POOL_WIDTH

    def body(pv_ref, w_ref, sc_ref, g_ref, pooled_ref, mixed_ref, y_ref, ext):
        s = pl.program_id(1)

        @pl.when(s == 0)
        def _():
            ext[pl.ds(0, POOL_HALO), :] = jnp.zeros((POOL_HALO, pw), F32)

        p = pv_ref[...]
        ext[pl.ds(POOL_HALO, ts), :] = p
        pos = s * ts + lax.broadcasted_iota(jnp.int32, (ts, 1), 0)
        parts = []
        for g, w in enumerate(POOL_WINDOWS):
            lanes = pl.ds(g * POOL_GROUP_DIM, POOL_GROUP_DIM)
            win = ext[pl.ds(POOL_HALO, ts), lanes]
            for i in range(1, w):
                win = win + ext[pl.ds(POOL_HALO - i, ts), lanes]
            cnt = jnp.minimum(pos + 1, w).astype(F32)
            pooled = (win / cnt - ext[pl.ds(POOL_HALO, ts), lanes]).astype(BF16)
            pooled_ref[:, lanes] = pooled
            parts.append(_dot(pooled, w_ref[g].astype(BF16)))
        mixed = jnp.concatenate(parts, axis=1)
        mixed_ref[...] = mixed
        pm = mixed * sc_ref[...]
        y_ref[...] = (pm * _rms_scale(pm) * g_ref[...]).astype(BF16)
        ext[pl.ds(0, POOL_HALO), :] = p[ts - POOL_HALO:, :]

    blk = pl.BlockSpec((ts, pw), lambda b, s: (b * ns + s, 0))
    t = bsz * seq
    return pl.pallas_call(
        body, name="pool_fwd", grid=(bsz, ns),
        in_specs=[blk, pl.BlockSpec((POOL_GROUPS, POOL_GROUP_DIM, POOL_GROUP_DIM), lambda b, s: (0, 0, 0)),
                  pl.BlockSpec((1, pw), lambda b, s: (0, 0)), pl.BlockSpec((1, pw), lambda b, s: (0, 0))],
        out_specs=[blk, blk, blk],
        out_shape=[jax.ShapeDtypeStruct((t, pw), BF16), jax.ShapeDtypeStruct((t, pw), F32),
                   jax.ShapeDtypeStruct((t, pw), BF16)],
        scratch_shapes=[pltpu.VMEM((POOL_HALO + ts, pw), F32)],
        compiler_params=_params("arbitrary", "arbitrary"),
    )(pv, pool_w, pool_scale, gain)


def _pool_bwd(dy, mixed, pooled, pool_w, pool_scale, gain, bsz, seq):
    ts = 512
    ns = seq // ts
    pw = POOL_WIDTH

    def body(dy_ref, mixed_ref, pooled_ref, w_ref, sc_ref, g_ref, dpv_ref, dw_ref, dsc_ref, dg_ref, ext):
        b = pl.program_id(0)
        sr = pl.program_id(1)
        s = ns - 1 - sr

        @pl.when(jnp.logical_and(b == 0, sr == 0))
        def _():
            dw_ref[...] = jnp.zeros_like(dw_ref)
            dsc_ref[...] = jnp.zeros_like(dsc_ref)
            dg_ref[...] = jnp.zeros_like(dg_ref)

        @pl.when(sr == 0)
        def _():
            ext[pl.ds(ts, POOL_HALO), :] = jnp.zeros((POOL_HALO, pw), F32)

        mixed = mixed_ref[...]
        sc = sc_ref[...]
        dpm, dgain = _rms_bwd(dy_ref[...], mixed * sc, g_ref[...])
        dg_ref[...] += dgain
        dsc_ref[...] += jnp.sum(dpm * mixed, axis=0, keepdims=True)
        dmixed = (dpm * sc).astype(BF16)
        pos = s * ts + lax.broadcasted_iota(jnp.int32, (ts, 1), 0)
        dpooled = []
        for g, w in enumerate(POOL_WINDOWS):
            lanes = pl.ds(g * POOL_GROUP_DIM, POOL_GROUP_DIM)
            dm = dmixed[:, g * POOL_GROUP_DIM:(g + 1) * POOL_GROUP_DIM]
            dw_ref[g] += _dot_tn(pooled_ref[:, lanes], dm)
            dp = _dot_nt(dm, w_ref[g].astype(BF16))
            dpooled.append(dp)
            cnt = jnp.minimum(pos + 1, w).astype(F32)
            ext[pl.ds(0, ts), lanes] = dp / cnt
        for g, w in enumerate(POOL_WINDOWS):
            lanes = pl.ds(g * POOL_GROUP_DIM, POOL_GROUP_DIM)
            win = ext[pl.ds(0, ts), lanes]
            for i in range(1, w):
                win = win + ext[pl.ds(i, ts), lanes]
            dpv_ref[:, lanes] = (win - dpooled[g]).astype(BF16)
        head = ext[pl.ds(0, POOL_HALO), :]
        ext[pl.ds(ts, POOL_HALO), :] = head

    blk = pl.BlockSpec((ts, pw), lambda b, s: (b * ns + (ns - 1 - s), 0))
    vec = pl.BlockSpec((1, pw), lambda b, s: (0, 0))
    wspec = pl.BlockSpec((POOL_GROUPS, POOL_GROUP_DIM, POOL_GROUP_DIM), lambda b, s: (0, 0, 0))
    t = bsz * seq
    return pl.pallas_call(
        body, name="pool_bwd", grid=(bsz, ns),
        in_specs=[blk, blk, blk, wspec, vec, vec],
        out_specs=[blk, wspec, vec, vec],
        out_shape=[jax.ShapeDtypeStruct((t, pw), BF16),
                   jax.ShapeDtypeStruct((POOL_GROUPS, POOL_GROUP_DIM, POOL_GROUP_DIM), F32),
                   jax.ShapeDtypeStruct((1, pw), F32), jax.ShapeDtypeStruct((1, pw), F32)],
        scratch_shapes=[pltpu.VMEM((ts + POOL_HALO, pw), F32)],
        compiler_params=_params("arbitrary", "arbitrary"),
    )(dy, mixed, pooled, pool_w, pool_scale, gain)


AUX_ONE = 64
AUX_F = 67

ATTN_PREP_ROWS = 512


def _seg_ones(width, seg):
    r = lax.broadcasted_iota(jnp.int32, (width, width), 0) // seg
    c = lax.broadcasted_iota(jnp.int32, (width, width), 1) // seg
    return (r == c).astype(BF16)


def _tri_ones(n, lower):
    r = lax.broadcasted_iota(jnp.int32, (n, n), 0)
    c = lax.broadcasted_iota(jnp.int32, (n, n), 1)
    return ((r >= c) if lower else (r <= c)).astype(BF16)


def _place_pieces(first_lane):
    r = lax.broadcasted_iota(jnp.int32, (3 * LANES, N_HEADS * LANES), 0)
    c = lax.broadcasted_iota(jnp.int32, (3 * LANES, N_HEADS * LANES), 1)
    piece, head = r // LANES, r % LANES
    return jnp.logical_and(head < N_HEADS, c == head * LANES + first_lane + piece).astype(BF16)


def _head_sums(x, seg_ones):
    hi, lo = _split2(x)
    return _dot(hi, seg_ones) + _dot(lo, seg_ones)


def _log_sigmoid(x):
    return jnp.minimum(x, 0.0) - jnp.log(1.0 + jnp.exp(-jnp.abs(x)))


def _attn_prep_fwd(q, k, f, b_forget, q_gain, k_gain, bsz, seq):
    ts = ATTN_PREP_ROWS
    ns = seq // ts
    aw = ATTN_WIDTH
    t = bsz * seq
    seg = _seg_ones(aw, HEAD_DIM)
    tri = _tri_ones(ts, True)

    def body(q_ref, k_ref, f_ref, bf_ref, gq_ref, gk_ref, seg_ref, tri_ref, pq_ref, pk_ref, qp_ref, kp_ref, carry):
        s = pl.program_id(1)

        @pl.when(s == 0)
        def _():
            carry[...] = jnp.zeros_like(carry)

        logf = _log_sigmoid(f_ref[...] + bf_ref[...])
        hi, mid, lo = _split3(logf)
        tri_v = tri_ref[...]
        fc = _dot(tri_v, hi) + _dot(tri_v, mid) + _dot(tri_v, lo) + carry[pl.ds(0, 1), :]
        carry[pl.ds(0, 1), :] = fc[ts - 1:, :]
        pcs = jnp.concatenate(_split3(fc), axis=1)
        lane = lax.broadcasted_iota(jnp.int32, (1, LANES), 1)
        ones_q = jnp.logical_and(lane >= AUX_ONE, lane < AUX_ONE + 3).astype(F32)
        ones_k = jnp.logical_and(lane >= AUX_F, lane < AUX_F + 3).astype(F32)
        seg_v = seg_ref[...]

        def build(x_ref, g_ref, scale, out_ref, ones, place_ref, f_sign):
            xv = x_ref[...]
            r = lax.rsqrt(_head_sums(xv * xv, seg_v) * (1.0 / HEAD_DIM) + EPS)
            xn = xv * r * g_ref[...] * scale
            aux = _dot(pcs, place_ref[...]) * f_sign
            for h in range(N_HEADS):
                pair = xn[:, (h // 2) * LANES:(h // 2 + 1) * LANES]
                feat = pair if h % 2 == 0 else pltpu.roll(pair, HEAD_DIM, 1)
                aux_h = aux[:, h * LANES:(h + 1) * LANES] + ones
                out_ref[:, h * LANES:(h + 1) * LANES] = jnp.where(lane < HEAD_DIM, feat, aux_h).astype(BF16)

        build(q_ref, gq_ref, 0.125, qp_ref, ones_q, pq_ref, 1.0)
        build(k_ref, gk_ref, 1.0, kp_ref, ones_k, pk_ref, -1.0)

    blk = pl.BlockSpec((ts, aw), lambda b, s: (b * ns + s, 0))
    fblk = pl.BlockSpec((ts, LANES), lambda b, s: (b * ns + s, 0))
    oblk = pl.BlockSpec((ts, N_HEADS * LANES), lambda b, s: (b * ns + s, 0))
    const = lambda shape: pl.BlockSpec(shape, lambda b, s: (0, 0))
    return pl.pallas_call(
        body, name="attn_prep_fwd", grid=(bsz, ns),
        in_specs=[blk, blk, fblk, const((1, LANES)), const((1, aw)), const((1, aw)), const((aw, aw)), const((ts, ts)),
                  const((3 * LANES, N_HEADS * LANES)), const((3 * LANES, N_HEADS * LANES))],
        out_specs=[oblk, oblk],
        out_shape=[jax.ShapeDtypeStruct((t, N_HEADS * LANES), BF16)] * 2,
        scratch_shapes=[pltpu.VMEM((8, LANES), F32)],
        compiler_params=_params("arbitrary", "arbitrary"),
    )(q, k, f, b_forget, q_gain, k_gain, seg, tri, _place_pieces(AUX_F), _place_pieces(AUX_ONE))


def _attn_prep_bwd(dqp, dkp, q, k, f, b_forget, q_gain, k_gain, bsz, seq):
    ts = ATTN_PREP_ROWS
    ns = seq // ts
    aw = ATTN_WIDTH
    t = bsz * seq
    seg = _seg_ones(aw, HEAD_DIM)
    tri = _tri_ones(ts, False)

    def body(dqp_ref, dkp_ref, q_ref, k_ref, f_ref, bf_ref, gq_ref, gk_ref, seg_ref, tri_ref,
             dq_ref, dk_ref, df_ref, dgq_ref, dgk_ref, dbf_ref, carry):
        b = pl.program_id(0)
        sr = pl.program_id(1)

        @pl.when(jnp.logical_and(b == 0, sr == 0))
        def _():
            dgq_ref[...] = jnp.zeros_like(dgq_ref)
            dgk_ref[...] = jnp.zeros_like(dgk_ref)
            dbf_ref[...] = jnp.zeros_like(dbf_ref)

        @pl.when(sr == 0)
        def _():
            carry[...] = jnp.zeros_like(carry)

        lane = lax.broadcasted_iota(jnp.int32, (1, LANES), 1)
        seg_v = seg_ref[...]

        def norm_bwd(dp_ref, x_ref, g_ref, scale, dx_ref, dgain_ref):
            parts = []
            for j in range(N_HEADS // 2):
                even = dp_ref[:, (2 * j) * LANES:(2 * j + 1) * LANES]
                odd = dp_ref[:, (2 * j + 1) * LANES:(2 * j + 2) * LANES]
                parts.append(jnp.where(lane < HEAD_DIM, even, pltpu.roll(odd, HEAD_DIM, 1)))
            dxn = jnp.concatenate(parts, axis=1) * scale
            xv = x_ref[...]
            r = lax.rsqrt(_head_sums(xv * xv, seg_v) * (1.0 / HEAD_DIM) + EPS)
            n = xv * r
            dgain_ref[...] += jnp.sum(dxn * n, axis=0, keepdims=True)
            dn = dxn * g_ref[...]
            m = _head_sums(dn * n, seg_v) * (1.0 / HEAD_DIM)
            dx_ref[...] = (r * (dn - n * m)).astype(BF16)

        norm_bwd(dqp_ref, q_ref, gq_ref, 0.125, dq_ref, dgq_ref)
        norm_bwd(dkp_ref, k_ref, gk_ref, 1.0, dk_ref, dgk_ref)

        dfc = jnp.zeros((ts, LANES), F32)
        for h in range(N_HEADS):
            cols = pl.ds(h * LANES, LANES)
            both = jnp.where(lane == AUX_F, dqp_ref[:, cols], 0.0) - jnp.where(lane == AUX_ONE, dkp_ref[:, cols], 0.0)
            dfc = jnp.where(lane == h, jnp.sum(both, axis=1, keepdims=True), dfc)
        hi, mid, lo = _split3(dfc)
        tri_v = tri_ref[...]
        dlogf = _dot(tri_v, hi) + _dot(tri_v, mid) + _dot(tri_v, lo) + carry[pl.ds(0, 1), :]
        carry[pl.ds(0, 1), :] = dlogf[0:1, :]
        df = jnp.where(lane < N_HEADS, dlogf * jax.nn.sigmoid(-(f_ref[...] + bf_ref[...])), 0.0)
        df_ref[...] = df.astype(BF16)
        dbf_ref[...] += jnp.sum(df, axis=0, keepdims=True)

    rev = lambda b, s: (b * ns + (ns - 1 - s), 0)
    blk = pl.BlockSpec((ts, aw), rev)
    fblk = pl.BlockSpec((ts, LANES), rev)
    pblk = pl.BlockSpec((ts, N_HEADS * LANES), rev)
    const = lambda shape: pl.BlockSpec(shape, lambda b, s: (0, 0))
    return pl.pallas_call(
        body, name="attn_prep_bwd", grid=(bsz, ns),
        in_specs=[pblk, pblk, blk, blk, fblk, const((1, LANES)), const((1, aw)), const((1, aw)), const((aw, aw)),
                  const((ts, ts))],
        out_specs=[blk, blk, fblk, const((1, aw)), const((1, aw)), const((1, LANES))],
        out_shape=[jax.ShapeDtypeStruct((t, aw), BF16), jax.ShapeDtypeStruct((t, aw), BF16),
                   jax.ShapeDtypeStruct((t, LANES), BF16), jax.ShapeDtypeStruct((1, aw), F32),
                   jax.ShapeDtypeStruct((1, aw), F32), jax.ShapeDtypeStruct((1, LANES), F32)],
        scratch_shapes=[pltpu.VMEM((8, LANES), F32)],
        compiler_params=_params("arbitrary", "arbitrary"),
    )(dqp, dkp, q, k, f, b_forget, q_gain, k_gain, seg, tri)


ATTN_BLOCK = 512
HEAD_PAIRS = N_HEADS // 2


def _flash_fwd(qp, kp, v, bsz, seq):
    tq = ATTN_BLOCK
    nq = seq // tq
    t = bsz * seq

    def body(q_ref, k_ref, v_ref, o_ref, lse_ref, m_sc, l_sc, acc_sc):
        i = pl.program_id(2)
        m_sc[...] = jnp.full(m_sc.shape, -jnp.inf, F32)
        l_sc[...] = jnp.zeros_like(l_sc)
        acc_sc[...] = jnp.zeros_like(acc_sc)
        lane = lax.broadcasted_iota(jnp.int32, (1, LANES), 1)
        low = lane < HEAD_DIM

        def key_block(j, masked):
            rows = pl.ds(pl.multiple_of(j * tq, tq), tq)
            vv = v_ref[rows, :]
            for h in range(2):
                mine = low if h == 0 else jnp.logical_not(low)
                s = _dot_nt(q_ref[:, h * LANES:(h + 1) * LANES], k_ref[rows, pl.ds(h * LANES, LANES)])
                if masked:
                    row = lax.broadcasted_iota(jnp.int32, (tq, tq), 0)
                    col = lax.broadcasted_iota(jnp.int32, (tq, tq), 1)
                    s = jnp.where(row >= col, s, -jnp.inf)
                m_prev = m_sc[h]
                m_new = jnp.maximum(m_prev, jnp.max(s, axis=1, keepdims=True))
                p = jnp.exp(s - jnp.tile(m_new, (1, tq // LANES)))
                alpha = jnp.exp(m_prev - m_new)
                l_sc[h] = alpha * l_sc[h] + jnp.sum(p, axis=1, keepdims=True)
                m_sc[h] = m_new
                pv = _dot(p.astype(BF16), jnp.where(mine, vv, jnp.zeros_like(vv)))
                acc_sc[...] = acc_sc[...] * jnp.where(mine, alpha, 1.0) + pv

        def below_diagonal(j, carry):
            key_block(j, False)
            return carry

        lax.fori_loop(0, i, below_diagonal, 0)
        key_block(i, True)
        l = jnp.where(low, l_sc[0], l_sc[1])
        m = jnp.where(low, m_sc[0], m_sc[1])
        o_ref[...] = acc_sc[...] / l
        lse_ref[...] = m + jnp.log(l)

    qspec = pl.BlockSpec((tq, 2 * LANES), lambda b, hp, i: (b * nq + i, hp))
    kspec = pl.BlockSpec((seq, 2 * LANES), lambda b, hp, i: (b, hp))
    vspec = pl.BlockSpec((seq, LANES), lambda b, hp, i: (b, hp))
    ospec = pl.BlockSpec((tq, LANES), lambda b, hp, i: (b * nq + i, hp))
    return pl.pallas_call(
        body, name="flash_fwd", grid=(bsz, HEAD_PAIRS, nq),
        in_specs=[qspec, kspec, vspec], out_specs=[ospec, ospec],
        out_shape=[jax.ShapeDtypeStruct((t, ATTN_WIDTH), F32), jax.ShapeDtypeStruct((t, ATTN_WIDTH), F32)],
        scratch_shapes=[pltpu.VMEM((2, tq, LANES), F32), pltpu.VMEM((2, tq, LANES), F32), pltpu.VMEM((tq, LANES), F32)],
        compiler_params=_params("arbitrary", "arbitrary", "arbitrary"),
    )(qp, kp, v)


def _flash_bwd(qp, kp, v, o, do, lse, bsz, seq):
    tq = ATTN_BLOCK
    nq = seq // tq
    t = bsz * seq

    def body(q_ref, k_ref, v_ref, o_ref, do_ref, lse_ref, dq_ref, dk_ref, dv_ref, dk_acc, dv_acc):
        j = pl.program_id(2)

        @pl.when(j == 0)
        def _():
            dq_ref[...] = jnp.zeros_like(dq_ref)

        dk_acc[...] = jnp.zeros_like(dk_acc)
        dv_acc[...] = jnp.zeros_like(dv_acc)
        lane = lax.broadcasted_iota(jnp.int32, (1, LANES), 1)
        low = lane < HEAD_DIM

        def query_block(i, masked):
            rows = pl.ds(pl.multiple_of(i * tq, tq), tq)
            dov = do_ref[rows, :]
            dd = dov * o_ref[rows, :]
            dob = dov.astype(BF16)
            vv = v_ref[...]
            lse_v = lse_ref[rows, :]
            for h in range(2):
                mine = low if h == 0 else jnp.logical_not(low)
                cols = pl.ds(h * LANES, LANES)
                qh = q_ref[rows, cols]
                kh = k_ref[:, cols]
                s = _dot_nt(qh, kh)
                lse_h = jnp.where(mine, lse_v, pltpu.roll(lse_v, HEAD_DIM, 1))
                p = jnp.exp(s - jnp.tile(lse_h, (1, tq // LANES)))
                if masked:
                    row = lax.broadcasted_iota(jnp.int32, (tq, tq), 0)
                    col = lax.broadcasted_iota(jnp.int32, (tq, tq), 1)
                    p = jnp.where(row >= col, p, 0.0)
                delta = jnp.sum(jnp.where(mine, dd, 0.0), axis=1, keepdims=True)
                dp = _dot_nt(dob, jnp.where(mine, vv, jnp.zeros_like(vv)))
                ds = (p * (dp - delta)).astype(BF16)
                dv_acc[...] += jnp.where(mine, _dot_tn(p.astype(BF16), dob), 0.0)
                dk_acc[:, cols] += _dot_tn(ds, qh)
                dq_ref[rows, cols] += _dot(ds, kh)

        def above_diagonal(i, carry):
            query_block(i, False)
            return carry

        query_block(j, True)
        lax.fori_loop(j + 1, nq, above_diagonal, 0)
        dk_ref[...] = dk_acc[...]
        dv_ref[...] = dv_acc[...].astype(BF16)

    qspec = pl.BlockSpec((seq, 2 * LANES), lambda b, hp, j: (b, hp))
    kspec = pl.BlockSpec((tq, 2 * LANES), lambda b, hp, j: (b * nq + j, hp))
    vspec = pl.BlockSpec((tq, LANES), lambda b, hp, j: (b * nq + j, hp))
    ospec = pl.BlockSpec((seq, LANES), lambda b, hp, j: (b, hp))
    return pl.pallas_call(
        body, name="flash_bwd", grid=(bsz, HEAD_PAIRS, nq),
        in_specs=[qspec, kspec, vspec, ospec, ospec, ospec], out_specs=[qspec, kspec, vspec],
        out_shape=[jax.ShapeDtypeStruct((t, N_HEADS * LANES), F32), jax.ShapeDtypeStruct((t, N_HEADS * LANES), F32),
                   jax.ShapeDtypeStruct((t, ATTN_WIDTH), BF16)],
        scratch_shapes=[pltpu.VMEM((tq, 2 * LANES), F32), pltpu.VMEM((tq, LANES), F32)],
        compiler_params=_params("arbitrary", "arbitrary", "arbitrary"),
    )(qp, kp, v, o, do, lse)


def _mix_out_fwd(o, y_pool, x, gain, w_out):
    t, d = x.shape
    tm = 1024
    pw, aw = POOL_WIDTH, ATTN_WIDTH

    def body(o_ref, yp_ref, x_ref, g_ref, w_ref, ycat_ref, y_ref):
        ov = o_ref[...]
        ya = (ov * _rms_scale(ov) * g_ref[...]).astype(BF16)
        ycat = jnp.concatenate([yp_ref[...], ya], axis=1)
        ycat_ref[...] = ycat
        y_ref[...] = x_ref[...] + _dot(ycat, w_ref[...])

    return pl.pallas_call(
        body, name="mix_out_fwd", grid=(t // tm,),
        in_specs=[_rows(tm, aw), _rows(tm, pw), _rows(tm, d), _resident((1, aw)), _resident((pw + aw, d))],
        out_specs=[_rows(tm, pw + aw), _rows(tm, d)],
        out_shape=[jax.ShapeDtypeStruct((t, pw + aw), BF16), jax.ShapeDtypeStruct((t, d), F32)],
        compiler_params=_params("arbitrary"),
    )(o, y_pool, x, gain, w_out)


def _mix_out_bwd(dx, o, gain, w_out):
    t, d = dx.shape
    tm = 1024
    pw, aw = POOL_WIDTH, ATTN_WIDTH

    def body(dx_ref, o_ref, g_ref, w_ref, dxb_ref, dyp_ref, do_ref, dg_ref):
        dxb = dx_ref[...].astype(BF16)
        dxb_ref[...] = dxb
        dyp_ref[...] = _dot_nt(dxb, w_ref[pl.ds(0, pw), :])
        dya = _dot_nt(dxb, w_ref[pl.ds(pw, aw), :])
        do, dgain = _rms_bwd(dya, o_ref[...], g_ref[...])
        do_ref[...] = do

        @pl.when(pl.program_id(0) == 0)
        def _():
            dg_ref[...] = jnp.zeros_like(dg_ref)

        dg_ref[...] += dgain

    return pl.pallas_call(
        body, name="mix_out_bwd", grid=(t // tm,),
        in_specs=[_rows(tm, d), _rows(tm, aw), _resident((1, aw)), _resident((pw + aw, d))],
        out_specs=[_rows(tm, d), _rows(tm, pw), _rows(tm, aw), pl.BlockSpec((1, aw), lambda i: (0, 0))],
        out_shape=[jax.ShapeDtypeStruct((t, d), BF16), jax.ShapeDtypeStruct((t, pw), F32),
                   jax.ShapeDtypeStruct((t, aw), F32), jax.ShapeDtypeStruct((1, aw), F32)],
        compiler_params=_params("arbitrary"),
    )(dx, o, gain, w_out)


def _mix_in_bwd(dpv, dq, dk, dv, df, x, dx_res, gain, w_in_t):
    t, d = x.shape
    tm = 512
    pw, aw = POOL_WIDTH, ATTN_WIDTH

    def body(dpv_ref, dq_ref, dk_ref, dv_ref, df_ref, x_ref, dxr_ref, g_ref, w_ref, dh_ref, dx_ref, dxh_ref, dg_ref):
        dh = jnp.concatenate([dpv_ref[...], dq_ref[...], dk_ref[...], dv_ref[...], df_ref[...]], axis=1)
        dh_ref[...] = dh
        dhm = _dot(dh, w_ref[...])
        dx, dgain = _rms_bwd(dhm, x_ref[...], g_ref[...])
        dx = dxr_ref[...] + dx
        dx_ref[...] = dx
        dxh_ref[...] = (0.5 * dx).astype(BF16)

        @pl.when(pl.program_id(0) == 0)
        def _():
            dg_ref[...] = jnp.zeros_like(dg_ref)

        dg_ref[...] += dgain

    return pl.pallas_call(
        body, name="mix_in_bwd", grid=(t // tm,),
        in_specs=[_rows(tm, pw), _rows(tm, aw), _rows(tm, aw), _rows(tm, aw), _rows(tm, LANES), _rows(tm, d),
                  _rows(tm, d), _resident((1, d)), _resident((MIX_PAD, d))],
        out_specs=[_rows(tm, MIX_PAD), _rows(tm, d), _rows(tm, d), pl.BlockSpec((1, d), lambda i: (0, 0))],
        out_shape=[jax.ShapeDtypeStruct((t, MIX_PAD), BF16), jax.ShapeDtypeStruct((t, d), F32),
                   jax.ShapeDtypeStruct((t, d), BF16), jax.ShapeDtypeStruct((1, d), F32)],
        compiler_params=_params("arbitrary"),
    )(dpv, dq, dk, dv, df, x, dx_res, gain, w_in_t)


MESH_IDS = pl.DeviceIdType.MESH


def _me():
    return lax.axis_index("x"), lax.axis_index("y"), lax.axis_index("c")


def _peer(x, y, c, p):
    px = 1 - x if p & 4 else x
    py = 1 - y if p & 2 else y
    pc = 1 - c if p & 1 else c
    return (px, py, pc), 4 * px + 2 * py + pc


HBM_SPEC = pl.BlockSpec(memory_space=pltpu.HBM)
SEM_SPEC = pl.BlockSpec(memory_space=pltpu.SEMAPHORE)
SPLIT_COPY = pltpu.CompilerParams(has_side_effects=pltpu.SideEffectType.DATAFLOW_SIDE_EFFECTING)
PEERS = N_DEV - 1


def _hbm(a):
    return pltpu.with_memory_space_constraint(a, pltpu.HBM)


def _row_block(ref, dev, rows):
    return ref.at[pl.ds(pl.multiple_of(dev * rows, BF16_ROWS), rows)]


def _copy_ends(gather, src, land, me, peer_id):
    if gather:
        rows = src.shape[0]
        return src, _row_block(land, me, rows), _row_block(land, peer_id, rows), src, _row_block(land, me, rows)
    rows = src.shape[0] // N_DEV
    return (_row_block(src, peer_id, rows), land.at[me], land.at[peer_id], _row_block(src, me, rows), land.at[me])


def _land_shape(gather, s):
    return (N_DEV * s.shape[0], s.shape[1]) if gather else (N_DEV, s.shape[0] // N_DEV, s.shape[1])


SIBLING = 1
SAME_CORE_PEERS = (2, 4, 6)
RELAYS = len(SAME_CORE_PEERS)


def _copies_start(groups, gather, name, after=None, relayed=()):
    flat = [s for g in groups for s in g]
    n, ng = len(flat), len(groups)
    lands = [lax.empty(_land_shape(gather, s), s.dtype) for s in flat]
    n_in = 2 * n + (after is not None)

    def body(*refs):
        ins, lnd = refs[:n], refs[n:2 * n]
        sems = refs[n_in:n_in + 2 * ng]
        token = refs[-1]
        x, y, c = _me()
        me = 4 * x + 2 * y + c
        w = 0
        for gi, g in enumerate(groups):
            for k in range(len(g)):
                for p in ((SIBLING,) + SAME_CORE_PEERS if gi in relayed else range(1, N_DEV)):
                    peer, peer_id = _peer(x, y, c, p)
                    src, dst, _, _, _ = _copy_ends(gather, ins[w], lnd[w], me, peer_id)
                    pltpu.make_async_remote_copy(src, dst, sems[2 * gi].at[k * PEERS + p - 1],
                                                 sems[2 * gi + 1].at[k * PEERS + p - 1], device_id=peer,
                                                 device_id_type=MESH_IDS).start()
                w += 1
        token[...] = jnp.zeros_like(token)

    sem_shapes = []
    for g in groups:
        sem_shapes += [pltpu.SemaphoreType.DMA((len(g) * PEERS,))] * 2
    out = pl.pallas_call(
        body, name=name,
        out_shape=(*sem_shapes, *[pltpu.HBM(s.shape, s.dtype) for s in flat],
                   *[pltpu.HBM(l.shape, l.dtype) for l in lands], jax.ShapeDtypeStruct((8, LANES), F32)),
        in_specs=[HBM_SPEC] * (2 * n) + [pl.BlockSpec(memory_space=pl.ANY)] * (after is not None),
        out_specs=(*[SEM_SPEC] * (2 * ng), *[HBM_SPEC] * (2 * n), pl.BlockSpec(memory_space=pltpu.VMEM)),
        input_output_aliases={i: 2 * ng + i for i in range(2 * n)},
        compiler_params=SPLIT_COPY,
    )(*[_hbm(s) for s in flat], *[_hbm(l) for l in lands], *([after] if after is not None else []))
    sems, thru, token = out[:2 * ng], out[2 * ng:2 * ng + 2 * n], out[-1]
    res, w = [], 0
    for gi, g in enumerate(groups):
        res.append((sems[2 * gi], sems[2 * gi + 1], list(thru[w:w + len(g)]), list(thru[n + w:n + w + len(g)])))
        w += len(g)
    return res, token


def _copies_wait(started, gather, after, name):
    send, recv, srcs, lands = started
    n = len(srcs)
    after = list(after) if isinstance(after, (list, tuple)) else [after]

    own_shapes = [s.shape if gather else (s.shape[0] // N_DEV, s.shape[1]) for s in srcs]

    def body(*refs):
        ins, lnd = refs[:n], refs[n:2 * n]
        send_sems, recv_sems = refs[2 * n], refs[2 * n + 1]
        bounce, in_sems, out_sems = refs[-n - 2:-2], refs[-2], refs[-1]
        x, y, c = _me()
        me = 4 * x + 2 * y + c
        ends = [_copy_ends(gather, ins[w], lnd[w], me, me)[3:] for w in range(n)]
        loads = [pltpu.make_async_copy(ends[w][0], bounce[w], in_sems.at[w]) for w in range(n)]
        stores = [pltpu.make_async_copy(bounce[w], ends[w][1], out_sems.at[w]) for w in range(n)]
        for cp in loads:
            cp.start()
        for w in range(n):
            loads[w].wait()
            stores[w].start()
        for w in range(n):
            for p in range(1, N_DEV):
                peer, peer_id = _peer(x, y, c, p)
                src, _, arrival, _, _ = _copy_ends(gather, ins[w], lnd[w], me, peer_id)
                cp = pltpu.make_async_remote_copy(src, arrival, send_sems.at[w * PEERS + p - 1],
                                                  recv_sems.at[w * PEERS + p - 1], device_id=peer,
                                                  device_id_type=MESH_IDS)
                cp.wait_send()
                cp.wait_recv()
        for cp in stores:
            cp.wait()

    out = pl.pallas_call(
        body, name=name,
        out_shape=(*[pltpu.HBM(s.shape, s.dtype) for s in srcs], *[pltpu.HBM(l.shape, l.dtype) for l in lands]),
        in_specs=[HBM_SPEC] * (2 * n) + [SEM_SPEC, SEM_SPEC] + [pl.BlockSpec(memory_space=pl.ANY)] * len(after),
        out_specs=[HBM_SPEC] * (2 * n),
        input_output_aliases={i: i for i in range(2 * n)},
        scratch_shapes=[*[pltpu.VMEM(shape, s.dtype) for shape, s in zip(own_shapes, srcs)],
                        pltpu.SemaphoreType.DMA((n,)), pltpu.SemaphoreType.DMA((n,))],
        compiler_params=SPLIT_COPY,
    )(*srcs, *lands, send, recv, *after)
    return list(out[n:])


def _relay_to_sibling(started, name):
    send, recv, srcs, lands = started
    n = len(srcs)

    def body(*refs):
        ins, lnd = refs[:n], refs[n:2 * n]
        send_sems, recv_sems = refs[2 * n], refs[2 * n + 1]
        relay_send, relay_recv = refs[2 * n + 2], refs[2 * n + 3]
        x, y, c = _me()
        sibling, _ = _peer(x, y, c, SIBLING)
        for w in range(n):
            rows = ins[w].shape[0]
            for k, p in enumerate(SAME_CORE_PEERS):
                peer, peer_id = _peer(x, y, c, p)
                arrived = _row_block(lnd[w], peer_id, rows)
                first = pltpu.make_async_remote_copy(ins[w], arrived, send_sems.at[w * PEERS + p - 1],
                                                     recv_sems.at[w * PEERS + p - 1], device_id=peer,
                                                     device_id_type=MESH_IDS)
                first.wait_recv()
                pltpu.make_async_remote_copy(arrived, arrived, relay_send.at[w * RELAYS + k],
                                             relay_recv.at[w * RELAYS + k], device_id=sibling,
                                             device_id_type=MESH_IDS).start()
                first.wait_send()

    sems = pltpu.SemaphoreType.DMA((n * RELAYS,))
    out = pl.pallas_call(
        body, name=name,
        out_shape=(sems, sems, *[pltpu.HBM(s.shape, s.dtype) for s in srcs], *[pltpu.HBM(l.shape, l.dtype) for l in lands]),
        in_specs=[HBM_SPEC] * (2 * n) + [SEM_SPEC, SEM_SPEC],
        out_specs=(SEM_SPEC, SEM_SPEC, *[HBM_SPEC] * (2 * n)),
        input_output_aliases={i: 2 + i for i in range(2 * n)},
        compiler_params=SPLIT_COPY,
    )(*srcs, *lands, send, recv)
    return send, recv, out[0], out[1], list(out[2:2 + n]), list(out[2 + n:])


def _relayed_wait(relayed, after, name):
    send, recv, relay_send, relay_recv, srcs, lands = relayed
    n = len(srcs)
    after = list(after) if isinstance(after, (list, tuple)) else [after]

    def body(*refs):
        ins, lnd = refs[:n], refs[n:2 * n]
        send_sems, recv_sems, relay_send_sems, relay_recv_sems = refs[2 * n:2 * n + 4]
        bounce, in_sems, out_sems = refs[-n - 2:-2], refs[-2], refs[-1]
        x, y, c = _me()
        me = 4 * x + 2 * y + c
        sibling, sibling_id = _peer(x, y, c, SIBLING)
        loads = [pltpu.make_async_copy(ins[w], bounce[w], in_sems.at[w]) for w in range(n)]
        stores = [pltpu.make_async_copy(bounce[w], _row_block(lnd[w], me, ins[w].shape[0]), out_sems.at[w])
                  for w in range(n)]
        for cp in loads:
            cp.start()
        for w in range(n):
            loads[w].wait()
            stores[w].start()
        for w in range(n):
            rows = ins[w].shape[0]
            direct = pltpu.make_async_remote_copy(ins[w], _row_block(lnd[w], sibling_id, rows),
                                                  send_sems.at[w * PEERS + SIBLING - 1],
                                                  recv_sems.at[w * PEERS + SIBLING - 1], device_id=sibling,
                                                  device_id_type=MESH_IDS)
            direct.wait_send()
            direct.wait_recv()
            for k, p in enumerate(SAME_CORE_PEERS):
                _, sent_id = _peer(x, y, c, p)
                _, got_id = _peer(x, y, c, p + SIBLING)
                relay = pltpu.make_async_remote_copy(_row_block(lnd[w], sent_id, rows), _row_block(lnd[w], got_id, rows),
                                                     relay_send_sems.at[w * RELAYS + k],
                                                     relay_recv_sems.at[w * RELAYS + k], device_id=sibling,
                                                     device_id_type=MESH_IDS)
                relay.wait_send()
                relay.wait_recv()
        for cp in stores:
            cp.wait()

    out = pl.pallas_call(
        body, name=name,
        out_shape=(*[pltpu.HBM(s.shape, s.dtype) for s in srcs], *[pltpu.HBM(l.shape, l.dtype) for l in lands]),
        in_specs=[HBM_SPEC] * (2 * n) + [SEM_SPEC] * 4 + [pl.BlockSpec(memory_space=pl.ANY)] * len(after),
        out_specs=[HBM_SPEC] * (2 * n),
        input_output_aliases={i: i for i in range(2 * n)},
        scratch_shapes=[*[pltpu.VMEM(s.shape, s.dtype) for s in srcs],
                        pltpu.SemaphoreType.DMA((n,)), pltpu.SemaphoreType.DMA((n,))],
        compiler_params=SPLIT_COPY,
    )(*srcs, *lands, send, recv, relay_send, relay_recv, *after)
    return list(out[n:])


def _adamw_update(w, g, m, v):
    nm = ADAM_B1 * m + (1.0 - ADAM_B1) * g
    nv = ADAM_B2 * v + (1.0 - ADAM_B2) * (g * g)
    m_hat = nm / (1.0 - ADAM_B1 ** ADAM_STEP)
    v_hat = nv / (1.0 - ADAM_B2 ** ADAM_STEP)
    return -ADAM_LR * (m_hat / (jnp.sqrt(v_hat) + ADAM_EPS) + ADAM_WD * w), nm, nv


SUM_ADAMW_COLS = 256


def _sum_adamw(parts, w, m, v, name):
    _, rows, d = parts.shape
    n = w.shape[0]
    tc = SUM_ADAMW_COLS

    def body(p_ref, w_ref, m_ref, v_ref, g_ref, d_ref, nm_ref, nv_ref):
        g = p_ref[0].astype(F32)
        for dev in range(1, N_DEV):
            g = g + p_ref[dev].astype(F32)
        g = g[:n]
        g_ref[...] = g
        d_ref[...], nm_ref[...], nv_ref[...] = _adamw_update(w_ref[...], g, m_ref[...], v_ref[...])

    spec = pl.BlockSpec((n, tc), lambda j: (0, j))
    shape = jax.ShapeDtypeStruct((n, d), F32)
    return pl.pallas_call(
        body, name=name, grid=(d // tc,),
        in_specs=[pl.BlockSpec((N_DEV, rows, tc), lambda j: (0, 0, j)), spec, spec, spec],
        out_specs=[spec] * 4, out_shape=[shape] * 4,
        compiler_params=_params("arbitrary"),
    )(parts, w, m, v)


def _pad_rows(a, rows):
    return jnp.pad(a, ((0, rows - a.shape[0]), (0, 0)))


def _row1(vec, width=D_MODEL):
    return jnp.pad(vec.reshape(1, -1), ((0, 0), (0, width - vec.shape[-1])))


COLUMN_SHARDED = ("ffn1_w_gate", "ffn1_w_up", "w_in", "ffn2_w_gate", "ffn2_w_up")
VEC_NAMES = ("ffn1_norm", "mix_norm", "ffn2_norm", "b_forget", "pool_scale", "q_norm", "k_norm", "out_norm_pool",
             "out_norm_attn")
VEC_ROWS = 16
LOSS_ROW = len(VEC_NAMES)


def _pack_vector_grads(parts, loss_part):
    def body(*refs):
        loss_ref, out_ref = refs[-2], refs[-1]
        out_ref[...] = jnp.zeros_like(out_ref)
        lane = lax.broadcasted_iota(jnp.int32, (1, LANES), 1)
        for i, (name, ref) in enumerate(zip(VEC_NAMES, refs[:-2])):
            val = ref[...]
            if name in ("q_norm", "k_norm"):
                val = val[:, 0:LANES] + val[:, LANES:2 * LANES] + val[:, 2 * LANES:3 * LANES] + val[:, 3 * LANES:]
                val = jnp.where(lane < HEAD_DIM, val + pltpu.roll(val, HEAD_DIM, 1), 0.0)
            out_ref[pl.ds(i, 1), pl.ds(0, val.shape[1])] = val
        out_ref[pl.ds(LOSS_ROW, 1), pl.ds(0, 1)] = loss_ref[...]

    vmem = pl.BlockSpec(memory_space=pltpu.VMEM)
    return pl.pallas_call(
        body, name="pack_vector_grads", in_specs=[vmem] * (len(parts) + 1), out_specs=vmem,
        out_shape=jax.ShapeDtypeStruct((VEC_ROWS, D_MODEL), F32),
    )(*parts, loss_part)


def _small_adamw(vec_all, pool_all, vec_params, pool_params):
    nv = len(vec_params)
    pool_rows = pool_params[0].shape[0]

    def body(*refs):
        vec_ref, pool_ref = refs[0], refs[1]
        ins = refs[2:2 + 3 * nv + 3]
        outs = refs[2 + 3 * nv + 3:-1]
        rows = refs[-1]
        total = vec_ref[pl.ds(0, VEC_ROWS), :]
        for dev in range(1, N_DEV):
            total = total + vec_ref[pl.ds(dev * VEC_ROWS, VEC_ROWS), :]
        rows[...] = total
        outs[4 * nv + 4][...] = rows[pl.ds(LOSS_ROW, 1), pl.ds(0, 1)]
        for i in range(nv):
            w_ref, m_ref, v_ref = ins[3 * i:3 * i + 3]
            g = rows[pl.ds(i, 1), pl.ds(0, w_ref.shape[1])]
            outs[4 * i][...] = g
            outs[4 * i + 1][...], outs[4 * i + 2][...], outs[4 * i + 3][...] = _adamw_update(
                w_ref[...], g, m_ref[...], v_ref[...])
        g = pool_ref[pl.ds(0, pool_rows), :]
        for dev in range(1, N_DEV):
            g = g + pool_ref[pl.ds(dev * pool_rows, pool_rows), :]
        w_ref, m_ref, v_ref = ins[3 * nv:]
        outs[4 * nv][...] = g
        outs[4 * nv + 1][...], outs[4 * nv + 2][...], outs[4 * nv + 3][...] = _adamw_update(
            w_ref[...], g, m_ref[...], v_ref[...])

    vmem = pl.BlockSpec(memory_space=pltpu.VMEM)
    flat = [a for trio in vec_params for a in trio] + list(pool_params)
    out_shape = []
    for trio in list(vec_params) + [pool_params]:
        out_shape += [jax.ShapeDtypeStruct(trio[0].shape, F32)] * 4
    out_shape.append(jax.ShapeDtypeStruct((1, 1), F32))
    return pl.pallas_call(
        body, name="adamw_small", in_specs=[vmem] * (2 + len(flat)), out_specs=[vmem] * len(out_shape),
        out_shape=out_shape, scratch_shapes=[pltpu.VMEM((VEC_ROWS, D_MODEL), F32)],
    )(vec_all, pool_all, *flat)


def kernel(x, ffn1_norm, ffn1_w_gate, ffn1_w_up, ffn1_w_down, mix_norm, w_in, b_forget, pool_w, pool_scale, q_norm, k_norm, out_norm_pool, out_norm_attn, w_out, ffn2_norm, ffn2_w_gate, ffn2_w_up, ffn2_w_down, loss_target, m_ffn1_norm, m_ffn1_w_gate, m_ffn1_w_up, m_ffn1_w_down, m_mix_norm, m_w_in, m_b_forget, m_pool_w, m_pool_scale, m_q_norm, m_k_norm, m_out_norm_pool, m_out_norm_attn, m_w_out, m_ffn2_norm, m_ffn2_w_gate, m_ffn2_w_up, m_ffn2_w_down, v_ffn1_norm, v_ffn1_w_gate, v_ffn1_w_up, v_ffn1_w_down, v_mix_norm, v_w_in, v_b_forget, v_pool_w, v_pool_scale, v_q_norm, v_k_norm, v_out_norm_pool, v_out_norm_attn, v_w_out, v_ffn2_norm, v_ffn2_w_gate, v_ffn2_w_up, v_ffn2_w_down):
    bsz, seq, d = x.shape
    t = bsz * seq
    x0 = x.reshape(t, d)
    target = loss_target.reshape(t, d)
    in_rows = -(-w_in.shape[1] // BF16_ROWS) * BF16_ROWS

    slabs = [s.astype(BF16) for s in (ffn1_w_gate.T, ffn1_w_up.T, ffn1_w_down, _pad_rows(w_in.T, in_rows), w_out,
                                       ffn2_w_gate.T, ffn2_w_up.T, ffn2_w_down)]
    gathers, started = _copies_start([slabs[0:2], slabs[2:3], slabs[3:4], slabs[4:5], slabs[5:8]], True, "gather_start",
                                     relayed=(0,))

    g1, gm, g2 = ffn1_norm.reshape(1, d), mix_norm.reshape(1, d), ffn2_norm.reshape(1, d)
    bf_row = _row1(b_forget, LANES)
    gq = jnp.tile(q_norm, N_HEADS).reshape(1, ATTN_WIDTH)
    gk = jnp.tile(k_norm, N_HEADS).reshape(1, ATTN_WIDTH)
    scale_row = pool_scale.reshape(1, POOL_WIDTH)
    gp, ga = out_norm_pool.reshape(1, POOL_WIDTH), out_norm_attn.reshape(1, ATTN_WIDTH)

    wg1, wu1 = _relayed_wait(_relay_to_sibling(gathers[0], "gather_relay_ffn1_up"), started, "gather_wait_ffn1_up")
    h1, sa1, sb1, s1 = _ffn_up(x0, g1, wg1, wu1, "ffn1_up")
    (wd1,) = _copies_wait(gathers[1], True, s1, "gather_wait_ffn1_down")
    (x1,) = _ffn_down(s1, wd1, x0, None, "ffn1_down")
    (win_g,) = _copies_wait(gathers[2], True, x1, "gather_wait_w_in")
    win_t = _repack_rows(win_g, in_rows, w_in.shape[1], N_DEV, "w_in_rows")
    hm, pv, q, k, v, f = _mix_in_fwd(x1, gm, win_t)
    pooled, mixed, y_pool = _pool_fwd(pv, pool_w, scale_row, gp, bsz, seq)
    qp, kp = _attn_prep_fwd(q, k, f, bf_row, gq, gk, bsz, seq)
    o, lse = _flash_fwd(qp, kp, v, bsz, seq)
    (wout,) = _copies_wait(gathers[3], True, o, "gather_wait_w_out")
    ycat, x2 = _mix_out_fwd(o, y_pool, x1, ga, wout)
    wg2, wu2, wd2 = _copies_wait(gathers[4], True, x2, "gather_wait_ffn2")
    h2, sa2, sb2, s2 = _ffn_up(x2, g2, wg2, wu2, "ffn2_up")
    dx3, dyh2, loss_part = _ffn_down(s2, wd2, x2, target, "ffn2_down")

    da2, db2, dwg2, dwu2 = _ffn_bwd_act(dyh2, sa2, sb2, h2, wd2, "ffn2_bwd_act")
    (dwd2,) = _wgrad([s2], dyh2, "ffn2_down_wgrad")
    (sent_ffn2,), tok = _copies_start([[dwg2, dwu2, dwd2]], False, "exchange_start_ffn2")
    dx2, dg2 = _ffn_bwd_dx(da2, db2, dx3, x2, g2 + tok[0, 0], wg2, wu2, "ffn2_bwd_dx")
    dx2b, dy_pool, do, dga = _mix_out_bwd(dx2, o, ga, wout)
    (dwout,) = _wgrad([ycat], dx2b, "w_out_wgrad")
    (sent_out,), tok = _copies_start([[dwout]], False, "exchange_start_w_out")
    dqp, dkp, dv = _flash_bwd(qp, kp, v, o, do, lse, bsz, seq)
    dq, dk, df, dgq, dgk, dbf = _attn_prep_bwd(dqp, dkp, q, k, f, bf_row + tok[0, 0], gq, gk, bsz, seq)
    dpv, dpool_w, dscale, dgp = _pool_bwd(dy_pool, mixed, pooled, pool_w, scale_row, gp, bsz, seq)
    dhcat, dx1, dyh1, dgm = _mix_in_bwd(dpv, dq, dk, dv, df, x1, dx2, gm, win_t)
    (dwin,) = _wgrad([dhcat], hm, "w_in_wgrad")
    dwin_blocks = _repack_rows(dwin, w_in.shape[1], in_rows, N_DEV, "w_in_grad_blocks")
    (sent_in,), tok = _copies_start([[dwin_blocks]], False, "exchange_start_w_in")
    (dwd1,) = _wgrad([s1], dyh1, "ffn1_down_wgrad")
    (sent_down1,), tok = _copies_start([[dwd1]], False, "exchange_start_ffn1_down", after=tok)
    da1, db1, dwg1, dwu1 = _ffn_bwd_act(dyh1, sa1, sb1, h1, wd1, "ffn1_bwd_act")
    (sent_up1,), tok = _copies_start([[dwg1, dwu1]], False, "exchange_start_ffn1_up", after=tok)
    dx0, dg1 = _ffn_bwd_dx(da1, db1, dx1, x0, g1 + tok[0, 0], wg1, wu1, "ffn1_bwd_dx")

    pool_rows = POOL_GROUPS * POOL_GROUP_DIM
    packed = _pack_vector_grads([dg1, dgm, dg2, dbf, dscale, dgq, dgk, dgp, dga], loss_part)
    (sent_small,), tok = _copies_start([[packed, dpool_w.reshape(pool_rows, POOL_GROUP_DIM)]], True, "small_grads_start")

    weights = dict(ffn1_norm=ffn1_norm, ffn1_w_gate=ffn1_w_gate, ffn1_w_up=ffn1_w_up, ffn1_w_down=ffn1_w_down,
                   mix_norm=mix_norm, w_in=w_in, b_forget=b_forget, pool_w=pool_w, pool_scale=pool_scale,
                   q_norm=q_norm, k_norm=k_norm, out_norm_pool=out_norm_pool, out_norm_attn=out_norm_attn,
                   w_out=w_out, ffn2_norm=ffn2_norm, ffn2_w_gate=ffn2_w_gate, ffn2_w_up=ffn2_w_up,
                   ffn2_w_down=ffn2_w_down)
    m_in = dict(ffn1_norm=m_ffn1_norm, ffn1_w_gate=m_ffn1_w_gate, ffn1_w_up=m_ffn1_w_up, ffn1_w_down=m_ffn1_w_down,
                mix_norm=m_mix_norm, w_in=m_w_in, b_forget=m_b_forget, pool_w=m_pool_w, pool_scale=m_pool_scale,
                q_norm=m_q_norm, k_norm=m_k_norm, out_norm_pool=m_out_norm_pool, out_norm_attn=m_out_norm_attn,
                w_out=m_w_out, ffn2_norm=m_ffn2_norm, ffn2_w_gate=m_ffn2_w_gate, ffn2_w_up=m_ffn2_w_up,
                ffn2_w_down=m_ffn2_w_down)
    v_in = dict(ffn1_norm=v_ffn1_norm, ffn1_w_gate=v_ffn1_w_gate, ffn1_w_up=v_ffn1_w_up, ffn1_w_down=v_ffn1_w_down,
                mix_norm=v_mix_norm, w_in=v_w_in, b_forget=v_b_forget, pool_w=v_pool_w, pool_scale=v_pool_scale,
                q_norm=v_q_norm, k_norm=v_k_norm, out_norm_pool=v_out_norm_pool, out_norm_attn=v_out_norm_attn,
                w_out=v_w_out, ffn2_norm=v_ffn2_norm, ffn2_w_gate=v_ffn2_w_gate, ffn2_w_up=v_ffn2_w_up,
                ffn2_w_down=v_ffn2_w_down)
    grads, delta, new_m, new_v = {}, {}, {}, {}
    after = [tok]
    plan = ((sent_ffn2, "ffn2", ("ffn2_w_gate", "ffn2_w_up", "ffn2_w_down")), (sent_out, "w_out", ("w_out",)),
            (sent_in, "w_in", ("w_in",)), (sent_down1, "ffn1_down", ("ffn1_w_down",)),
            (sent_up1, "ffn1_up", ("ffn1_w_gate", "ffn1_w_up")))
    for sent, tag, names in plan:
        parts = _copies_wait(sent, False, after, f"exchange_wait_{tag}")
        after = []
        for n, part in zip(names, parts):
            turn = (lambda a: a.T) if n in COLUMN_SHARDED else (lambda a: a)
            done = _sum_adamw(part, turn(weights[n]), turn(m_in[n]), turn(v_in[n]), f"adamw_{n}")
            grads[n], delta[n], new_m[n], new_v[n] = (turn(a) for a in done)
            after.append(done[3])
    vec_all, pool_all = _copies_wait(sent_small, True, after, "small_grads_wait")
    as_row = lambda a: a.reshape(1, -1)
    as_pool = lambda a: a.reshape(pool_rows, POOL_GROUP_DIM)
    small = _small_adamw(vec_all, pool_all, [tuple(as_row(z[n]) for z in (weights, m_in, v_in)) for n in VEC_NAMES],
                         tuple(as_pool(z["pool_w"]) for z in (weights, m_in, v_in)))
    for i, n in enumerate(VEC_NAMES + ("pool_w",)):
        grads[n], delta[n], new_m[n], new_v[n] = (a.reshape(weights[n].shape) for a in small[4 * i:4 * i + 4])
    loss = small[-1].reshape(())

    order = ("ffn1_norm", "ffn1_w_gate", "ffn1_w_up", "ffn1_w_down", "mix_norm", "w_in", "b_forget", "pool_w",
             "pool_scale", "q_norm", "k_norm", "out_norm_pool", "out_norm_attn", "w_out", "ffn2_norm", "ffn2_w_gate",
             "ffn2_w_up", "ffn2_w_down")
    return (loss, dx0.reshape(bsz, seq, d), *[grads[n] for n in order], *[delta[n] for n in order],
            *[new_m[n] for n in order], *[new_v[n] for n in order])
```

```python
import functools

import jax
import jax.numpy as jnp
from jax import lax
from jax.experimental import pallas as pl
from jax.experimental.pallas import tpu as pltpu

F32 = jnp.float32
BF16 = jnp.bfloat16

EPS = 1e-6
D_MODEL = 1024
D_FF = 2816
N_HEADS = 8
HEAD_DIM = 64
POOL_WIDTH = 512
ATTN_WIDTH = 512
POOL_GROUPS = 4
POOL_GROUP_DIM = 128
POOL_WINDOWS = (2, 4, 8, 16)
POOL_HALO = 16
MIX_PAD = POOL_WIDTH + 3 * ATTN_WIDTH + 128
N_DEV = 8
BF16_ROWS = 16
LANES = 128
VMEM_LIMIT = 56 * 1024 * 1024

ADAM_LR = 0.001
ADAM_B1 = 0.9
ADAM_B2 = 0.999
ADAM_EPS = 1e-08
ADAM_WD = 0.01
ADAM_STEP = 10


def _params(*sem):
    return pltpu.CompilerParams(dimension_semantics=sem, vmem_limit_bytes=VMEM_LIMIT)


def _dot(a, b):
    return jnp.dot(a, b, preferred_element_type=F32)


def _dot_nt(a, b):
    return lax.dot_general(a, b, (((1,), (1,)), ((), ())), preferred_element_type=F32)


def _dot_tn(a, b):
    return lax.dot_general(a, b, (((0,), (0,)), ((), ())), preferred_element_type=F32)


def _resident(shape):
    return pl.BlockSpec(shape, lambda *_: (0,) * len(shape), pipeline_mode=pl.Buffered(1))


def _rows(tm, width):
    return pl.BlockSpec((tm, width), lambda i: (i, 0))


def _rms_scale(x):
    return lax.rsqrt(jnp.mean(x * x, axis=-1, keepdims=True) + EPS)


def _rms_bwd(dh, x, gain):
    r = _rms_scale(x)
    n = x * r
    dgain = jnp.sum(dh * n, axis=0, keepdims=True)
    dn = dh * gain
    dx = r * (dn - n * jnp.mean(dn * n, axis=-1, keepdims=True))
    return dx, dgain


def _split3(x):
    hi = x.astype(BF16)
    r1 = x - hi.astype(F32)
    mid = r1.astype(BF16)
    lo = (r1 - mid.astype(F32)).astype(BF16)
    return hi, mid, lo


def _split2(x):
    hi = x.astype(BF16)
    return hi, (x - hi.astype(F32)).astype(BF16)


FF_CHUNK = 256


def _swiglu_parts(a, b):
    sig = jax.nn.sigmoid(a)
    silu = a * sig
    return (b * (sig + silu * (1.0 - sig))).astype(BF16), silu.astype(BF16), (silu * b).astype(BF16)


def _ffn_gate(x, gain, wg_t, name):
    t, d = x.shape
    f = wg_t.shape[0]
    tm = 512

    def body(x_ref, g_ref, wg_ref, h_ref, a_ref):
        xv = x_ref[...]
        h = (xv * _rms_scale(xv) * g_ref[...]).astype(BF16)
        h_ref[...] = h
        for c in range(f // FF_CHUNK):
            sl = pl.ds(c * FF_CHUNK, FF_CHUNK)
            a_ref[:, sl] = _dot_nt(h, wg_ref[sl, :])

    return pl.pallas_call(
        body, name=name, grid=(t // tm,),
        in_specs=[_rows(tm, d), _resident((1, d)), _resident((f, d))],
        out_specs=[_rows(tm, d), _rows(tm, f)],
        out_shape=[jax.ShapeDtypeStruct((t, d), BF16), jax.ShapeDtypeStruct((t, f), F32)],
        compiler_params=_params("arbitrary"),
    )(x, gain, wg_t)


def _ffn_up_after_gate(h, a, wu_t, name):
    t, d = h.shape
    f = wu_t.shape[0]
    tm = 512

    def body(h_ref, a_ref, wu_ref, sa_ref, sb_ref, s_ref):
        hv = h_ref[...]
        for c in range(f // FF_CHUNK):
            sl = pl.ds(c * FF_CHUNK, FF_CHUNK)
            sa_ref[:, sl], sb_ref[:, sl], s_ref[:, sl] = _swiglu_parts(a_ref[:, sl], _dot_nt(hv, wu_ref[sl, :]))

    wide = jax.ShapeDtypeStruct((t, f), BF16)
    return pl.pallas_call(
        body, name=name, grid=(t // tm,),
        in_specs=[_rows(tm, d), _rows(tm, f), _resident((f, d))],
        out_specs=[_rows(tm, f)] * 3, out_shape=[wide] * 3,
        compiler_params=_params("arbitrary"),
    )(h, a, wu_t)


def _ffn_up(x, gain, wg_t, wu_t, name):
    t, d = x.shape
    f = wg_t.shape[0]
    tm = 512

    def body(x_ref, g_ref, wg_ref, wu_ref, h_ref, sa_ref, sb_ref, s_ref):
        xv = x_ref[...]
        h = (xv * _rms_scale(xv) * g_ref[...]).astype(BF16)
        h_ref[...] = h
        for c in range(f // FF_CHUNK):
            sl = pl.ds(c * FF_CHUNK, FF_CHUNK)
            sa_ref[:, sl], sb_ref[:, sl], s_ref[:, sl] = _swiglu_parts(_dot_nt(h, wg_ref[sl, :]), _dot_nt(h, wu_ref[sl, :]))

    wide = jax.ShapeDtypeStruct((t, f), BF16)
    return pl.pallas_call(
        body, name=name, grid=(t // tm,),
        in_specs=[_rows(tm, d), _resident((1, d)), _resident((f, d)), _resident((f, d))],
        out_specs=[_rows(tm, d), _rows(tm, f), _rows(tm, f), _rows(tm, f)],
        out_shape=[jax.ShapeDtypeStruct((t, d), BF16), wide, wide, wide],
        compiler_params=_params("arbitrary"),
    )(x, gain, wg_t, wu_t)


def _ffn_down(s, wd, x, target, name):
    t, d = x.shape
    f = wd.shape[0]
    tm = 512
    with_loss = target is not None

    def body(*refs):
        if with_loss:
            s_ref, w_ref, x_ref, t_ref, dy_ref, dyh_ref, loss_ref = refs
        else:
            s_ref, w_ref, x_ref, y_ref = refs
        y = x_ref[...] + 0.5 * _dot(s_ref[...], w_ref[...])
        if with_loss:
            e = y - t_ref[...]
            dy = e * (1.0 / d)
            dy_ref[...] = dy
            dyh_ref[...] = (0.5 * dy).astype(BF16)

            @pl.when(pl.program_id(0) == 0)
            def _():
                loss_ref[...] = jnp.zeros_like(loss_ref)

            part = jnp.sum(jnp.sum(e * e, axis=0, keepdims=True), axis=1, keepdims=True)
            loss_ref[...] += part * (0.5 / d)
        else:
            y_ref[...] = y

    in_specs = [_rows(tm, f), _resident((f, d)), _rows(tm, d)]
    args = [s, wd, x]
    if with_loss:
        in_specs.append(_rows(tm, d))
        args.append(target)
        out_shape = [jax.ShapeDtypeStruct((t, d), F32), jax.ShapeDtypeStruct((t, d), BF16),
                     jax.ShapeDtypeStruct((1, 1), F32)]
        out_specs = [_rows(tm, d), _rows(tm, d), pl.BlockSpec((1, 1), lambda i: (0, 0))]
    else:
        out_shape = [jax.ShapeDtypeStruct((t, d), F32)]
        out_specs = [_rows(tm, d)]
    return pl.pallas_call(
        body, name=name, grid=(t // tm,), in_specs=in_specs, out_specs=out_specs, out_shape=out_shape,
        compiler_params=_params("arbitrary"),
    )(*args)


def _ffn_bwd_act(dyh, sa, sb, h, wd, name):
    t, d = dyh.shape
    f = wd.shape[0]
    tn = f // 2
    tk = 512
    nk = t // tk

    def body(dy_ref, sa_ref, sb_ref, h_ref, wd_ref, da_ref, db_ref, dwg_ref, dwu_ref, acc_g, acc_u):
        k = pl.program_id(1)

        @pl.when(k == 0)
        def _():
            acc_g[...] = jnp.zeros_like(acc_g)
            acc_u[...] = jnp.zeros_like(acc_u)

        ds = _dot_nt(dy_ref[...], wd_ref[...])
        da = (ds * sa_ref[...].astype(F32)).astype(BF16)
        db = (ds * sb_ref[...].astype(F32)).astype(BF16)
        da_ref[...] = da
        db_ref[...] = db
        hv = h_ref[...]
        acc_g[...] += _dot_tn(da, hv)
        acc_u[...] += _dot_tn(db, hv)

        @pl.when(k == nk - 1)
        def _():
            dwg_ref[...] = acc_g[...].astype(BF16)
            dwu_ref[...] = acc_u[...].astype(BF16)

    tokens = pl.BlockSpec((tk, d), lambda j, k: (k, 0))
    wide = pl.BlockSpec((tk, tn), lambda j, k: (k, j))
    weight = pl.BlockSpec((tn, d), lambda j, k: (j, 0))
    return pl.pallas_call(
        body, name=name, grid=(f // tn, nk),
        in_specs=[tokens, wide, wide, tokens, weight],
        out_specs=[wide, wide, weight, weight],
        out_shape=[jax.ShapeDtypeStruct((t, f), BF16)] * 2 + [jax.ShapeDtypeStruct((f, d), BF16)] * 2,
        scratch_shapes=[pltpu.VMEM((tn, d), F32)] * 2,
        compiler_params=_params("arbitrary", "arbitrary"),
    )(dyh, sa, sb, h, wd)


def _ffn_bwd_dx(da, db, dy, x, gain, wg_t, wu_t, name):
    t, d = x.shape
    f = wg_t.shape[0]
    tm = 512

    def body(da_ref, db_ref, dy_ref, x_ref, g_ref, wg_ref, wu_ref, dx_ref, dg_ref):
        dh = _dot(da_ref[...], wg_ref[...]) + _dot(db_ref[...], wu_ref[...])
        dx, dgain = _rms_bwd(dh, x_ref[...], g_ref[...])
        dx_ref[...] = dy_ref[...] + dx

        @pl.when(pl.program_id(0) == 0)
        def _():
            dg_ref[...] = jnp.zeros_like(dg_ref)

        dg_ref[...] += dgain

    return pl.pallas_call(
        body, name=name, grid=(t // tm,),
        in_specs=[_rows(tm, f), _rows(tm, f), _rows(tm, d), _rows(tm, d), _resident((1, d)), _resident((f, d)),
                  _resident((f, d))],
        out_specs=[_rows(tm, d), pl.BlockSpec((1, d), lambda i: (0, 0))],
        out_shape=[jax.ShapeDtypeStruct((t, d), F32), jax.ShapeDtypeStruct((1, d), F32)],
        compiler_params=_params("arbitrary"),
    )(da, db, dy, x, gain, wg_t, wu_t)


def _wgrad(lhs, b, name):
    t, n = lhs[0].shape
    d = b.shape[1]
    m = len(lhs)
    tn = n // 2 if n * d * m > (4 << 20) else n
    tk = 1024
    nk = t // tk

    def body(*refs):
        a_refs, b_ref, o_refs, accs = refs[:m], refs[m], refs[m + 1:2 * m + 1], refs[2 * m + 1:]
        k = pl.program_id(1)

        @pl.when(k == 0)
        def _():
            for acc in accs:
                acc[...] = jnp.zeros_like(acc)

        bv = b_ref[...]
        for a_ref, acc in zip(a_refs, accs):
            acc[...] += _dot_tn(a_ref[...], bv)

        @pl.when(k == nk - 1)
        def _():
            for o_ref, acc in zip(o_refs, accs):
                o_ref[...] = acc[...].astype(BF16)

    return pl.pallas_call(
        body, name=name, grid=(n // tn, nk),
        in_specs=[pl.BlockSpec((tk, tn), lambda j, k: (k, j))] * m + [pl.BlockSpec((tk, d), lambda j, k: (k, 0))],
        out_specs=[pl.BlockSpec((tn, d), lambda j, k: (j, 0))] * m,
        out_shape=[jax.ShapeDtypeStruct((n, d), BF16)] * m,
        scratch_shapes=[pltpu.VMEM((tn, d), F32)] * m,
        compiler_params=_params("arbitrary", "arbitrary"),
    )(*lhs, b)


def _repack_rows(a, rows_in, rows_out, blocks, name):
    total, d = a.shape
    real = min(rows_in, rows_out)

    def body(a_ref, o_ref, wide_in, wide_out):
        wide_in[...] = a_ref[...].astype(F32)
        wide_out[...] = jnp.zeros_like(wide_out)
        for j in range(blocks):
            wide_out[pl.ds(j * rows_out, real), :] = wide_in[pl.ds(j * rows_in, real), :]
        o_ref[...] = wide_out[...].astype(BF16)

    full = pl.BlockSpec((total, d), lambda i: (0, 0))
    return pl.pallas_call(
        body, name=name, grid=(1,), in_specs=[full], out_specs=full, out_shape=jax.ShapeDtypeStruct((total, d), BF16),
        scratch_shapes=[pltpu.VMEM((total, d), F32)] * 2,
        compiler_params=_params("arbitrary"),
    )(a)


def _mix_in_fwd(x, gain, w_in_t):
    t, d = x.shape
    tm = 1024
    pw, aw = POOL_WIDTH, ATTN_WIDTH

    def body(x_ref, g_ref, w_ref, hm_ref, pv_ref, q_ref, k_ref, v_ref, f_ref):
        xv = x_ref[...]
        hm = (xv * _rms_scale(xv) * g_ref[...]).astype(BF16)
        hm_ref[...] = hm
        pv_ref[...] = _dot_nt(hm, w_ref[pl.ds(0, pw), :])
        q_ref[...] = _dot_nt(hm, w_ref[pl.ds(pw, aw), :])
        k_ref[...] = _dot_nt(hm, w_ref[pl.ds(pw + aw, aw), :])
        v_ref[...] = _dot_nt(hm, w_ref[pl.ds(pw + 2 * aw, aw), :]).astype(BF16)
        f_ref[...] = _dot_nt(hm, w_ref[pl.ds(pw + 3 * aw, LANES), :])

    return pl.pallas_call(
        body, name="mix_in_fwd", grid=(t // tm,),
        in_specs=[_rows(tm, d), _resident((1, d)), _resident((MIX_PAD, d))],
        out_specs=[_rows(tm, d), _rows(tm, pw), _rows(tm, aw), _rows(tm, aw), _rows(tm, aw), _rows(tm, LANES)],
        out_shape=[jax.ShapeDtypeStruct((t, d), BF16), jax.ShapeDtypeStruct((t, pw), F32),
                   jax.ShapeDtypeStruct((t, aw), F32), jax.ShapeDtypeStruct((t, aw), F32),
                   jax.ShapeDtypeStruct((t, aw), BF16), jax.ShapeDtypeStruct((t, LANES), F32)],
        compiler_params=_params("arbitrary"),
    )(x, gain, w_in_t)


def _pool_fwd(pv, pool_w, pool_scale, gain, bsz, seq):
    ts = 512
    ns = seq // ts
    pw = POOL_WIDTH

    def body(pv_ref, w_ref, sc_ref, g_ref, pooled_ref, mixed_ref, y_ref, ext):
        s = pl.program_id(1)

        @pl.when(s == 0)
        def _():
            ext[pl.ds(0, POOL_HALO), :] = jnp.zeros((POOL_HALO, pw), F32)

        p = pv_ref[...]
        ext[pl.ds(POOL_HALO, ts), :] = p
        pos = s * ts + lax.broadcasted_iota(jnp.int32, (ts, 1), 0)
        parts = []
        for g, w in enumerate(POOL_WINDOWS):
            lanes = pl.ds(g * POOL_GROUP_DIM, POOL_GROUP_DIM)
            win = ext[pl.ds(POOL_HALO, ts), lanes]
            for i in range(1, w):
                win = win + ext[pl.ds(POOL_HALO - i, ts), lanes]
            cnt = jnp.minimum(pos + 1, w).astype(F32)
            pooled = (win / cnt - ext[pl.ds(POOL_HALO, ts), lanes]).astype(BF16)
            pooled_ref[:, lanes] = pooled
            parts.append(_dot(pooled, w_ref[g].astype(BF16)))
        mixed = jnp.concatenate(parts, axis=1)
        mixed_ref[...] = mixed
        pm = mixed * sc_ref[...]
        y_ref[...] = (pm * _rms_scale(pm) * g_ref[...]).astype(BF16)
        ext[pl.ds(0, POOL_HALO), :] = p[ts - POOL_HALO:, :]

    blk = pl.BlockSpec((ts, pw), lambda b, s: (b * ns + s, 0))
    t = bsz * seq
    return pl.pallas_call(
        body, name="pool_fwd", grid=(bsz, ns),
        in_specs=[blk, pl.BlockSpec((POOL_GROUPS, POOL_GROUP_DIM, POOL_GROUP_DIM), lambda b, s: (0, 0, 0)),
                  pl.BlockSpec((1, pw), lambda b, s: (0, 0)), pl.BlockSpec((1, pw), lambda b, s: (0, 0))],
        out_specs=[blk, blk, blk],
        out_shape=[jax.ShapeDtypeStruct((t, pw), BF16), jax.ShapeDtypeStruct((t, pw), F32),
                   jax.ShapeDtypeStruct((t, pw), BF16)],
        scratch_shapes=[pltpu.VMEM((POOL_HALO + ts, pw), F32)],
        compiler_params=_params("arbitrary", "arbitrary"),
    )(pv, pool_w, pool_scale, gain)


def _pool_bwd(dy, mixed, pooled, pool_w, pool_scale, gain, bsz, seq):
    ts = 512
    ns = seq // ts
    pw = POOL_WIDTH

    def body(dy_ref, mixed_ref, pooled_ref, w_ref, sc_ref, g_ref, dpv_ref, dw_ref, dsc_ref, dg_ref, ext):
        b = pl.program_id(0)
        sr = pl.program_id(1)
        s = ns - 1 - sr

        @pl.when(jnp.logical_and(b == 0, sr == 0))
        def _():
            dw_ref[...] = jnp.zeros_like(dw_ref)
            dsc_ref[...] = jnp.zeros_like(dsc_ref)
            dg_ref[...] = jnp.zeros_like(dg_ref)

        @pl.when(sr == 0)
        def _():
            ext[pl.ds(ts, POOL_HALO), :] = jnp.zeros((POOL_HALO, pw), F32)

        mixed = mixed_ref[...]
        sc = sc_ref[...]
        dpm, dgain = _rms_bwd(dy_ref[...], mixed * sc, g_ref[...])
        dg_ref[...] += dgain
        dsc_ref[...] += jnp.sum(dpm * mixed, axis=0, keepdims=True)
        dmixed = (dpm * sc).astype(BF16)
        pos = s * ts + lax.broadcasted_iota(jnp.int32, (ts, 1), 0)
        dpooled = []
        for g, w in enumerate(POOL_WINDOWS):
            lanes = pl.ds(g * POOL_GROUP_DIM, POOL_GROUP_DIM)
            dm = dmixed[:, g * POOL_GROUP_DIM:(g + 1) * POOL_GROUP_DIM]
            dw_ref[g] += _dot_tn(pooled_ref[:, lanes], dm)
            dp = _dot_nt(dm, w_ref[g].astype(BF16))
            dpooled.append(dp)
            cnt = jnp.minimum(pos + 1, w).astype(F32)
            ext[pl.ds(0, ts), lanes] = dp / cnt
        for g, w in enumerate(POOL_WINDOWS):
            lanes = pl.ds(g * POOL_GROUP_DIM, POOL_GROUP_DIM)
            win = ext[pl.ds(0, ts), lanes]
            for i in range(1, w):
                win = win + ext[pl.ds(i, ts), lanes]
            dpv_ref[:, lanes] = (win - dpooled[g]).astype(BF16)
        head = ext[pl.ds(0, POOL_HALO), :]
        ext[pl.ds(ts, POOL_HALO), :] = head

    blk = pl.BlockSpec((ts, pw), lambda b, s: (b * ns + (ns - 1 - s), 0))
    vec = pl.BlockSpec((1, pw), lambda b, s: (0, 0))
    wspec = pl.BlockSpec((POOL_GROUPS, POOL_GROUP_DIM, POOL_GROUP_DIM), lambda b, s: (0, 0, 0))
    t = bsz * seq
    return pl.pallas_call(
        body, name="pool_bwd", grid=(bsz, ns),
        in_specs=[blk, blk, blk, wspec, vec, vec],
        out_specs=[blk, wspec, vec, vec],
        out_shape=[jax.ShapeDtypeStruct((t, pw), BF16),
                   jax.ShapeDtypeStruct((POOL_GROUPS, POOL_GROUP_DIM, POOL_GROUP_DIM), F32),
                   jax.ShapeDtypeStruct((1, pw), F32), jax.ShapeDtypeStruct((1, pw), F32)],
        scratch_shapes=[pltpu.VMEM((ts + POOL_HALO, pw), F32)],
        compiler_params=_params("arbitrary", "arbitrary"),
    )(dy, mixed, pooled, pool_w, pool_scale, gain)


AUX_ONE = 64
AUX_F = 67

ATTN_PREP_ROWS = 512


def _seg_ones(width, seg):
    r = lax.broadcasted_iota(jnp.int32, (width, width), 0) // seg
    c = lax.broadcasted_iota(jnp.int32, (width, width), 1) // seg
    return (r == c).astype(BF16)


def _tri_ones(n, lower):
    r = lax.broadcasted_iota(jnp.int32, (n, n), 0)
    c = lax.broadcasted_iota(jnp.int32, (n, n), 1)
    return ((r >= c) if lower else (r <= c)).astype(BF16)


def _place_pieces(first_lane):
    r = lax.broadcasted_iota(jnp.int32, (3 * LANES, N_HEADS * LANES), 0)
    c = lax.broadcasted_iota(jnp.int32, (3 * LANES, N_HEADS * LANES), 1)
    piece, head = r // LANES, r % LANES
    return jnp.logical_and(head < N_HEADS, c == head * LANES + first_lane + piece).astype(BF16)


def _head_sums(x, seg_ones):
    hi, lo = _split2(x)
    return _dot(hi, seg_ones) + _dot(lo, seg_ones)


def _log_sigmoid(x):
    return jnp.minimum(x, 0.0) - jnp.log(1.0 + jnp.exp(-jnp.abs(x)))


def _attn_prep_fwd(q, k, f, b_forget, q_gain, k_gain, bsz, seq):
    ts = ATTN_PREP_ROWS
    ns = seq // ts
    aw = ATTN_WIDTH
    t = bsz * seq
    seg = _seg_ones(aw, HEAD_DIM)
    tri = _tri_ones(ts, True)

    def body(q_ref, k_ref, f_ref, bf_ref, gq_ref, gk_ref, seg_ref, tri_ref, pq_ref, pk_ref, qp_ref, kp_ref, carry):
        s = pl.program_id(1)

        @pl.when(s == 0)
        def _():
            carry[...] = jnp.zeros_like(carry)

        logf = _log_sigmoid(f_ref[...] + bf_ref[...])
        hi, mid, lo = _split3(logf)
        tri_v = tri_ref[...]
        fc = _dot(tri_v, hi) + _dot(tri_v, mid) + _dot(tri_v, lo) + carry[pl.ds(0, 1), :]
        carry[pl.ds(0, 1), :] = fc[ts - 1:, :]
        pcs = jnp.concatenate(_split3(fc), axis=1)
        lane = lax.broadcasted_iota(jnp.int32, (1, LANES), 1)
        ones_q = jnp.logical_and(lane >= AUX_ONE, lane < AUX_ONE + 3).astype(F32)
        ones_k = jnp.logical_and(lane >= AUX_F, lane < AUX_F + 3).astype(F32)
        seg_v = seg_ref[...]

        def build(x_ref, g_ref, scale, out_ref, ones, place_ref, f_sign):
            xv = x_ref[...]
            r = lax.rsqrt(_head_sums(xv * xv, seg_v) * (1.0 / HEAD_DIM) + EPS)
            xn = xv * r * g_ref[...] * scale
            aux = _dot(pcs, place_ref[...]) * f_sign
            for h in range(N_HEADS):
                pair = xn[:, (h // 2) * LANES:(h // 2 + 1) * LANES]
                feat = pair if h % 2 == 0 else pltpu.roll(pair, HEAD_DIM, 1)
                aux_h = aux[:, h * LANES:(h + 1) * LANES] + ones
                out_ref[:, h * LANES:(h + 1) * LANES] = jnp.where(lane < HEAD_DIM, feat, aux_h).astype(BF16)

        build(q_ref, gq_ref, 0.125, qp_ref, ones_q, pq_ref, 1.0)
        build(k_ref, gk_ref, 1.0, kp_ref, ones_k, pk_ref, -1.0)

    blk = pl.BlockSpec((ts, aw), lambda b, s: (b * ns + s, 0))
    fblk = pl.BlockSpec((ts, LANES), lambda b, s: (b * ns + s, 0))
    oblk = pl.BlockSpec((ts, N_HEADS * LANES), lambda b, s: (b * ns + s, 0))
    const = lambda shape: pl.BlockSpec(shape, lambda b, s: (0, 0))
    return pl.pallas_call(
        body, name="attn_prep_fwd", grid=(bsz, ns),
        in_specs=[blk, blk, fblk, const((1, LANES)), const((1, aw)), const((1, aw)), const((aw, aw)), const((ts, ts)),
                  const((3 * LANES, N_HEADS * LANES)), const((3 * LANES, N_HEADS * LANES))],
        out_specs=[oblk, oblk],
        out_shape=[jax.ShapeDtypeStruct((t, N_HEADS * LANES), BF16)] * 2,
        scratch_shapes=[pltpu.VMEM((8, LANES), F32)],
        compiler_params=_params("arbitrary", "arbitrary"),
    )(q, k, f, b_forget, q_gain, k_gain, seg, tri, _place_pieces(AUX_F), _place_pieces(AUX_ONE))


def _attn_prep_bwd(dqp, dkp, q, k, f, b_forget, q_gain, k_gain, bsz, seq):
    ts = ATTN_PREP_ROWS
    ns = seq // ts
    aw = ATTN_WIDTH
    t = bsz * seq
    seg = _seg_ones(aw, HEAD_DIM)
    tri = _tri_ones(ts, False)

    def body(dqp_ref, dkp_ref, q_ref, k_ref, f_ref, bf_ref, gq_ref, gk_ref, seg_ref, tri_ref,
             dq_ref, dk_ref, df_ref, dgq_ref, dgk_ref, dbf_ref, carry):
        b = pl.program_id(0)
        sr = pl.program_id(1)

        @pl.when(jnp.logical_and(b == 0, sr == 0))
        def _():
            dgq_ref[...] = jnp.zeros_like(dgq_ref)
            dgk_ref[...] = jnp.zeros_like(dgk_ref)
            dbf_ref[...] = jnp.zeros_like(dbf_ref)

        @pl.when(sr == 0)
        def _():
            carry[...] = jnp.zeros_like(carry)

        lane = lax.broadcasted_iota(jnp.int32, (1, LANES), 1)
        seg_v = seg_ref[...]

        def norm_bwd(dp_ref, x_ref, g_ref, scale, dx_ref, dgain_ref):
            parts = []
            for j in range(N_HEADS // 2):
                even = dp_ref[:, (2 * j) * LANES:(2 * j + 1) * LANES]
                odd = dp_ref[:, (2 * j + 1) * LANES:(2 * j + 2) * LANES]
                parts.append(jnp.where(lane < HEAD_DIM, even, pltpu.roll(odd, HEAD_DIM, 1)))
            dxn = jnp.concatenate(parts, axis=1) * scale
            xv = x_ref[...]
            r = lax.rsqrt(_head_sums(xv * xv, seg_v) * (1.0 / HEAD_DIM) + EPS)
            n = xv * r
            dgain_ref[...] += jnp.sum(dxn * n, axis=0, keepdims=True)
            dn = dxn * g_ref[...]
            m = _head_sums(dn * n, seg_v) * (1.0 / HEAD_DIM)
            dx_ref[...] = (r * (dn - n * m)).astype(BF16)

        norm_bwd(dqp_ref, q_ref, gq_ref, 0.125, dq_ref, dgq_ref)
        norm_bwd(dkp_ref, k_ref, gk_ref, 1.0, dk_ref, dgk_ref)

        dfc = jnp.zeros((ts, LANES), F32)
        for h in range(N_HEADS):
            cols = pl.ds(h * LANES, LANES)
            both = jnp.where(lane == AUX_F, dqp_ref[:, cols], 0.0) - jnp.where(lane == AUX_ONE, dkp_ref[:, cols], 0.0)
            dfc = jnp.where(lane == h, jnp.sum(both, axis=1, keepdims=True), dfc)
        hi, mid, lo = _split3(dfc)
        tri_v = tri_ref[...]
        dlogf = _dot(tri_v, hi) + _dot(tri_v, mid) + _dot(tri_v, lo) + carry[pl.ds(0, 1), :]
        carry[pl.ds(0, 1), :] = dlogf[0:1, :]
        df = jnp.where(lane < N_HEADS, dlogf * jax.nn.sigmoid(-(f_ref[...] + bf_ref[...])), 0.0)
        df_ref[...] = df.astype(BF16)
        dbf_ref[...] += jnp.sum(df, axis=0, keepdims=True)

    rev = lambda b, s: (b * ns + (ns - 1 - s), 0)
    blk = pl.BlockSpec((ts, aw), rev)
    fblk = pl.BlockSpec((ts, LANES), rev)
    pblk = pl.BlockSpec((ts, N_HEADS * LANES), rev)
    const = lambda shape: pl.BlockSpec(shape, lambda b, s: (0, 0))
    return pl.pallas_call(
        body, name="attn_prep_bwd", grid=(bsz, ns),
        in_specs=[pblk, pblk, blk, blk, fblk, const((1, LANES)), const((1, aw)), const((1, aw)), const((aw, aw)),
                  const((ts, ts))],
        out_specs=[blk, blk, fblk, const((1, aw)), const((1, aw)), const((1, LANES))],
        out_shape=[jax.ShapeDtypeStruct((t, aw), BF16), jax.ShapeDtypeStruct((t, aw), BF16),
                   jax.ShapeDtypeStruct((t, LANES), BF16), jax.ShapeDtypeStruct((1, aw), F32),
                   jax.ShapeDtypeStruct((1, aw), F32), jax.ShapeDtypeStruct((1, LANES), F32)],
        scratch_shapes=[pltpu.VMEM((8, LANES), F32)],
        compiler_params=_params("arbitrary", "arbitrary"),
    )(dqp, dkp, q, k, f, b_forget, q_gain, k_gain, seg, tri)


ATTN_BLOCK = 512
HEAD_PAIRS = N_HEADS // 2


def _flash_fwd(qp, kp, v, bsz, seq):
    tq = ATTN_BLOCK
    nq = seq // tq
    t = bsz * seq

    def body(q_ref, k_ref, v_ref, o_ref, lse_ref, m_sc, l_sc, acc_sc):
        i = pl.program_id(2)
        m_sc[...] = jnp.full(m_sc.shape, -jnp.inf, F32)
        l_sc[...] = jnp.zeros_like(l_sc)
        acc_sc[...] = jnp.zeros_like(acc_sc)
        lane = lax.broadcasted_iota(jnp.int32, (1, LANES), 1)
        low = lane < HEAD_DIM

        def key_block(j, masked):
            rows = pl.ds(pl.multiple_of(j * tq, tq), tq)
            vv = v_ref[rows, :]
            for h in range(2):
                mine = low if h == 0 else jnp.logical_not(low)
                s = _dot_nt(q_ref[:, h * LANES:(h + 1) * LANES], k_ref[rows, pl.ds(h * LANES, LANES)])
                if masked:
                    row = lax.broadcasted_iota(jnp.int32, (tq, tq), 0)
                    col = lax.broadcasted_iota(jnp.int32, (tq, tq), 1)
                    s = jnp.where(row >= col, s, -jnp.inf)
                m_prev = m_sc[h]
                m_new = jnp.maximum(m_prev, jnp.max(s, axis=1, keepdims=True))
                p = jnp.exp(s - jnp.tile(m_new, (1, tq // LANES)))
                alpha = jnp.exp(m_prev - m_new)
                l_sc[h] = alpha * l_sc[h] + jnp.sum(p, axis=1, keepdims=True)
                m_sc[h] = m_new
                pv = _dot(p.astype(BF16), jnp.where(mine, vv, jnp.zeros_like(vv)))
                acc_sc[...] = acc_sc[...] * jnp.where(mine, alpha, 1.0) + pv

        def below_diagonal(j, carry):
            key_block(j, False)
            return carry

        lax.fori_loop(0, i, below_diagonal, 0)
        key_block(i, True)
        l = jnp.where(low, l_sc[0], l_sc[1])
        m = jnp.where(low, m_sc[0], m_sc[1])
        o_ref[...] = acc_sc[...] / l
        lse_ref[...] = m + jnp.log(l)

    qspec = pl.BlockSpec((tq, 2 * LANES), lambda b, hp, i: (b * nq + i, hp))
    kspec = pl.BlockSpec((seq, 2 * LANES), lambda b, hp, i: (b, hp))
    vspec = pl.BlockSpec((seq, LANES), lambda b, hp, i: (b, hp))
    ospec = pl.BlockSpec((tq, LANES), lambda b, hp, i: (b * nq + i, hp))
    return pl.pallas_call(
        body, name="flash_fwd", grid=(bsz, HEAD_PAIRS, nq),
        in_specs=[qspec, kspec, vspec], out_specs=[ospec, ospec],
        out_shape=[jax.ShapeDtypeStruct((t, ATTN_WIDTH), F32), jax.ShapeDtypeStruct((t, ATTN_WIDTH), F32)],
        scratch_shapes=[pltpu.VMEM((2, tq, LANES), F32), pltpu.VMEM((2, tq, LANES), F32), pltpu.VMEM((tq, LANES), F32)],
        compiler_params=_params("arbitrary", "arbitrary", "arbitrary"),
    )(qp, kp, v)


def _flash_bwd(qp, kp, v, o, do, lse, bsz, seq):
    tq = ATTN_BLOCK
    nq = seq // tq
    t = bsz * seq

    def body(q_ref, k_ref, v_ref, o_ref, do_ref, lse_ref, dq_ref, dk_ref, dv_ref, dk_acc, dv_acc):
        j = pl.program_id(2)

        @pl.when(j == 0)
        def _():
            dq_ref[...] = jnp.zeros_like(dq_ref)

        dk_acc[...] = jnp.zeros_like(dk_acc)
        dv_acc[...] = jnp.zeros_like(dv_acc)
        lane = lax.broadcasted_iota(jnp.int32, (1, LANES), 1)
        low = lane < HEAD_DIM

        def query_block(i, masked):
            rows = pl.ds(pl.multiple_of(i * tq, tq), tq)
            dov = do_ref[rows, :]
            dd = dov * o_ref[rows, :]
            dob = dov.astype(BF16)
            vv = v_ref[...]
            lse_v = lse_ref[rows, :]
            for h in range(2):
                mine = low if h == 0 else jnp.logical_not(low)
                cols = pl.ds(h * LANES, LANES)
                qh = q_ref[rows, cols]
                kh = k_ref[:, cols]
                s = _dot_nt(qh, kh)
                lse_h = jnp.where(mine, lse_v, pltpu.roll(lse_v, HEAD_DIM, 1))
                p = jnp.exp(s - jnp.tile(lse_h, (1, tq // LANES)))
                if masked:
                    row = lax.broadcasted_iota(jnp.int32, (tq, tq), 0)
                    col = lax.broadcasted_iota(jnp.int32, (tq, tq), 1)
                    p = jnp.where(row >= col, p, 0.0)
                delta = jnp.sum(jnp.where(mine, dd, 0.0), axis=1, keepdims=True)
                dp = _dot_nt(dob, jnp.where(mine, vv, jnp.zeros_like(vv)))
                ds = (p * (dp - delta)).astype(BF16)
                dv_acc[...] += jnp.where(mine, _dot_tn(p.astype(BF16), dob), 0.0)
                dk_acc[:, cols] += _dot_tn(ds, qh)
                dq_ref[rows, cols] += _dot(ds, kh)

        def above_diagonal(i, carry):
            query_block(i, False)
            return carry

        query_block(j, True)
        lax.fori_loop(j + 1, nq, above_diagonal, 0)
        dk_ref[...] = dk_acc[...]
        dv_ref[...] = dv_acc[...].astype(BF16)

    qspec = pl.BlockSpec((seq, 2 * LANES), lambda b, hp, j: (b, hp))
    kspec = pl.BlockSpec((tq, 2 * LANES), lambda b, hp, j: (b * nq + j, hp))
    vspec = pl.BlockSpec((tq, LANES), lambda b, hp, j: (b * nq + j, hp))
    ospec = pl.BlockSpec((seq, LANES), lambda b, hp, j: (b, hp))
    return pl.pallas_call(
        body, name="flash_bwd", grid=(bsz, HEAD_PAIRS, nq),
        in_specs=[qspec, kspec, vspec, ospec, ospec, ospec], out_specs=[qspec, kspec, vspec],
        out_shape=[jax.ShapeDtypeStruct((t, N_HEADS * LANES), F32), jax.ShapeDtypeStruct((t, N_HEADS * LANES), F32),
                   jax.ShapeDtypeStruct((t, ATTN_WIDTH), BF16)],
        scratch_shapes=[pltpu.VMEM((tq, 2 * LANES), F32), pltpu.VMEM((tq, LANES), F32)],
        compiler_params=_params("arbitrary", "arbitrary", "arbitrary"),
    )(qp, kp, v, o, do, lse)


def _mix_out_fwd(o, y_pool, x, gain, w_out):
    t, d = x.shape
    tm = 1024
    pw, aw = POOL_WIDTH, ATTN_WIDTH

    def body(o_ref, yp_ref, x_ref, g_ref, w_ref, ycat_ref, y_ref):
        ov = o_ref[...]
        ya = (ov * _rms_scale(ov) * g_ref[...]).astype(BF16)
        ycat = jnp.concatenate([yp_ref[...], ya], axis=1)
        ycat_ref[...] = ycat
        y_ref[...] = x_ref[...] + _dot(ycat, w_ref[...])

    return pl.pallas_call(
        body, name="mix_out_fwd", grid=(t // tm,),
        in_specs=[_rows(tm, aw), _rows(tm, pw), _rows(tm, d), _resident((1, aw)), _resident((pw + aw, d))],
        out_specs=[_rows(tm, pw + aw), _rows(tm, d)],
        out_shape=[jax.ShapeDtypeStruct((t, pw + aw), BF16), jax.ShapeDtypeStruct((t, d), F32)],
        compiler_params=_params("arbitrary"),
    )(o, y_pool, x, gain, w_out)


def _mix_out_bwd(dx, o, gain, w_out):
    t, d = dx.shape
    tm = 1024
    pw, aw = POOL_WIDTH, ATTN_WIDTH

    def body(dx_ref, o_ref, g_ref, w_ref, dxb_ref, dyp_ref, do_ref, dg_ref):
        dxb = dx_ref[...].astype(BF16)
        dxb_ref[...] = dxb
        dyp_ref[...] = _dot_nt(dxb, w_ref[pl.ds(0, pw), :])
        dya = _dot_nt(dxb, w_ref[pl.ds(pw, aw), :])
        do, dgain = _rms_bwd(dya, o_ref[...], g_ref[...])
        do_ref[...] = do

        @pl.when(pl.program_id(0) == 0)
        def _():
            dg_ref[...] = jnp.zeros_like(dg_ref)

        dg_ref[...] += dgain

    return pl.pallas_call(
        body, name="mix_out_bwd", grid=(t // tm,),
        in_specs=[_rows(tm, d), _rows(tm, aw), _resident((1, aw)), _resident((pw + aw, d))],
        out_specs=[_rows(tm, d), _rows(tm, pw), _rows(tm, aw), pl.BlockSpec((1, aw), lambda i: (0, 0))],
        out_shape=[jax.ShapeDtypeStruct((t, d), BF16), jax.ShapeDtypeStruct((t, pw), F32),
                   jax.ShapeDtypeStruct((t, aw), F32), jax.ShapeDtypeStruct((1, aw), F32)],
        compiler_params=_params("arbitrary"),
    )(dx, o, gain, w_out)


def _mix_in_bwd(dpv, dq, dk, dv, df, x, dx_res, gain, w_in_t):
    t, d = x.shape
    tm = 512
    pw, aw = POOL_WIDTH, ATTN_WIDTH

    def body(dpv_ref, dq_ref, dk_ref, dv_ref, df_ref, x_ref, dxr_ref, g_ref, w_ref, dh_ref, dx_ref, dxh_ref, dg_ref):
        dh = jnp.concatenate([dpv_ref[...], dq_ref[...], dk_ref[...], dv_ref[...], df_ref[...]], axis=1)
        dh_ref[...] = dh
        dhm = _dot(dh, w_ref[...])
        dx, dgain = _rms_bwd(dhm, x_ref[...], g_ref[...])
        dx = dxr_ref[...] + dx
        dx_ref[...] = dx
        dxh_ref[...] = (0.5 * dx).astype(BF16)

        @pl.when(pl.program_id(0) == 0)
        def _():
            dg_ref[...] = jnp.zeros_like(dg_ref)

        dg_ref[...] += dgain

    return pl.pallas_call(
        body, name="mix_in_bwd", grid=(t // tm,),
        in_specs=[_rows(tm, pw), _rows(tm, aw), _rows(tm, aw), _rows(tm, aw), _rows(tm, LANES), _rows(tm, d),
                  _rows(tm, d), _resident((1, d)), _resident((MIX_PAD, d))],
        out_specs=[_rows(tm, MIX_PAD), _rows(tm, d), _rows(tm, d), pl.BlockSpec((1, d), lambda i: (0, 0))],
        out_shape=[jax.ShapeDtypeStruct((t, MIX_PAD), BF16), jax.ShapeDtypeStruct((t, d), F32),
                   jax.ShapeDtypeStruct((t, d), BF16), jax.ShapeDtypeStruct((1, d), F32)],
        compiler_params=_params("arbitrary"),
    )(dpv, dq, dk, dv, df, x, dx_res, gain, w_in_t)


MESH_IDS = pl.DeviceIdType.MESH


def _me():
    return lax.axis_index("x"), lax.axis_index("y"), lax.axis_index("c")


def _peer(x, y, c, p):
    px = 1 - x if p & 4 else x
    py = 1 - y if p & 2 else y
    pc = 1 - c if p & 1 else c
    return (px, py, pc), 4 * px + 2 * py + pc


HBM_SPEC = pl.BlockSpec(memory_space=pltpu.HBM)
SEM_SPEC = pl.BlockSpec(memory_space=pltpu.SEMAPHORE)
SPLIT_COPY = pltpu.CompilerParams(has_side_effects=pltpu.SideEffectType.DATAFLOW_SIDE_EFFECTING)
PEERS = N_DEV - 1


def _hbm(a):
    return pltpu.with_memory_space_constraint(a, pltpu.HBM)


def _row_block(ref, dev, rows):
    return ref.at[pl.ds(pl.multiple_of(dev * rows, BF16_ROWS), rows)]


def _copy_ends(gather, src, land, me, peer_id):
    if gather:
        rows = src.shape[0]
        return src, _row_block(land, me, rows), _row_block(land, peer_id, rows), src, _row_block(land, me, rows)
    rows = src.shape[0] // N_DEV
    return (_row_block(src, peer_id, rows), land.at[me], land.at[peer_id], _row_block(src, me, rows), land.at[me])


def _land_shape(gather, s):
    return (N_DEV * s.shape[0], s.shape[1]) if gather else (N_DEV, s.shape[0] // N_DEV, s.shape[1])


SIBLING = 1
SAME_CORE_PEERS = (2, 4, 6)
RELAYS = len(SAME_CORE_PEERS)


def _copies_start(groups, gather, name, after=None, relayed=()):
    flat = [s for g in groups for s in g]
    n, ng = len(flat), len(groups)
    lands = [lax.empty(_land_shape(gather, s), s.dtype) for s in flat]
    n_in = 2 * n + (after is not None)

    def body(*refs):
        ins, lnd = refs[:n], refs[n:2 * n]
        sems = refs[n_in:n_in + 2 * ng]
        token = refs[-1]
        x, y, c = _me()
        me = 4 * x + 2 * y + c
        w = 0
        for gi, g in enumerate(groups):
            for k in range(len(g)):
                for p in ((SIBLING,) + SAME_CORE_PEERS if gi in relayed else range(1, N_DEV)):
                    peer, peer_id = _peer(x, y, c, p)
                    src, dst, _, _, _ = _copy_ends(gather, ins[w], lnd[w], me, peer_id)
                    pltpu.make_async_remote_copy(src, dst, sems[2 * gi].at[k * PEERS + p - 1],
                                                 sems[2 * gi + 1].at[k * PEERS + p - 1], device_id=peer,
                                                 device_id_type=MESH_IDS).start()
                w += 1
        token[...] = jnp.zeros_like(token)

    sem_shapes = []
    for g in groups:
        sem_shapes += [pltpu.SemaphoreType.DMA((len(g) * PEERS,))] * 2
    out = pl.pallas_call(
        body, name=name,
        out_shape=(*sem_shapes, *[pltpu.HBM(s.shape, s.dtype) for s in flat],
                   *[pltpu.HBM(l.shape, l.dtype) for l in lands], jax.ShapeDtypeStruct((8, LANES), F32)),
        in_specs=[HBM_SPEC] * (2 * n) + [pl.BlockSpec(memory_space=pl.ANY)] * (after is not None),
        out_specs=(*[SEM_SPEC] * (2 * ng), *[HBM_SPEC] * (2 * n), pl.BlockSpec(memory_space=pltpu.VMEM)),
        input_output_aliases={i: 2 * ng + i for i in range(2 * n)},
        compiler_params=SPLIT_COPY,
    )(*[_hbm(s) for s in flat], *[_hbm(l) for l in lands], *([after] if after is not None else []))
    sems, thru, token = out[:2 * ng], out[2 * ng:2 * ng + 2 * n], out[-1]
    res, w = [], 0
    for gi, g in enumerate(groups):
        res.append((sems[2 * gi], sems[2 * gi + 1], list(thru[w:w + len(g)]), list(thru[n + w:n + w + len(g)])))
        w += len(g)
    return res, token


def _copies_wait(started, gather, after, name):
    send, recv, srcs, lands = started
    n = len(srcs)
    after = list(after) if isinstance(after, (list, tuple)) else [after]

    own_shapes = [s.shape if gather else (s.shape[0] // N_DEV, s.shape[1]) for s in srcs]

    def body(*refs):
        ins, lnd = refs[:n], refs[n:2 * n]
        send_sems, recv_sems = refs[2 * n], refs[2 * n + 1]
        bounce, in_sems, out_sems = refs[-n - 2:-2], refs[-2], refs[-1]
        x, y, c = _me()
        me = 4 * x + 2 * y + c
        ends = [_copy_ends(gather, ins[w], lnd[w], me, me)[3:] for w in range(n)]
        loads = [pltpu.make_async_copy(ends[w][0], bounce[w], in_sems.at[w]) for w in range(n)]
        stores = [pltpu.make_async_copy(bounce[w], ends[w][1], out_sems.at[w]) for w in range(n)]
        for cp in loads:
            cp.start()
        for w in range(n):
            loads[w].wait()
            stores[w].start()
        for w in range(n):
            for p in range(1, N_DEV):
                peer, peer_id = _peer(x, y, c, p)
                src, _, arrival, _, _ = _copy_ends(gather, ins[w], lnd[w], me, peer_id)
                cp = pltpu.make_async_remote_copy(src, arrival, send_sems.at[w * PEERS + p - 1],
                                                  recv_sems.at[w * PEERS + p - 1], device_id=peer,
                                                  device_id_type=MESH_IDS)
                cp.wait_send()
                cp.wait_recv()
        for cp in stores:
            cp.wait()

    out = pl.pallas_call(
        body, name=name,
        out_shape=(*[pltpu.HBM(s.shape, s.dtype) for s in srcs], *[pltpu.HBM(l.shape, l.dtype) for l in lands]),
        in_specs=[HBM_SPEC] * (2 * n) + [SEM_SPEC, SEM_SPEC] + [pl.BlockSpec(memory_space=pl.ANY)] * len(after),
        out_specs=[HBM_SPEC] * (2 * n),
        input_output_aliases={i: i for i in range(2 * n)},
        scratch_shapes=[*[pltpu.VMEM(shape, s.dtype) for shape, s in zip(own_shapes, srcs)],
                        pltpu.SemaphoreType.DMA((n,)), pltpu.SemaphoreType.DMA((n,))],
        compiler_params=SPLIT_COPY,
    )(*srcs, *lands, send, recv, *after)
    return list(out[n:])


def _relay_to_sibling(started, name, after=None):
    send, recv, srcs, lands = started
    n = len(srcs)
    after = [] if after is None else [after]

    def body(*refs):
        ins, lnd = refs[:n], refs[n:2 * n]
        send_sems, recv_sems = refs[2 * n], refs[2 * n + 1]
        relay_send, relay_recv = refs[2 * n + 2 + len(after)], refs[2 * n + 3 + len(after)]
        x, y, c = _me()
        sibling, _ = _peer(x, y, c, SIBLING)
        for w in range(n):
            rows = ins[w].shape[0]
            for k, p in enumerate(SAME_CORE_PEERS):
                peer, peer_id = _peer(x, y, c, p)
                arrived = _row_block(lnd[w], peer_id, rows)
                first = pltpu.make_async_remote_copy(ins[w], arrived, send_sems.at[w * PEERS + p - 1],
                                                     recv_sems.at[w * PEERS + p - 1], device_id=peer,
                                                     device_id_type=MESH_IDS)
                first.wait_recv()
                pltpu.make_async_remote_copy(arrived, arrived, relay_send.at[w * RELAYS + k],
                                             relay_recv.at[w * RELAYS + k], device_id=sibling,
                                             device_id_type=MESH_IDS).start()
                first.wait_send()

    sems = pltpu.SemaphoreType.DMA((n * RELAYS,))
    out = pl.pallas_call(
        body, name=name,
        out_shape=(sems, sems, *[pltpu.HBM(s.shape, s.dtype) for s in srcs], *[pltpu.HBM(l.shape, l.dtype) for l in lands]),
        in_specs=[HBM_SPEC] * (2 * n) + [SEM_SPEC, SEM_SPEC] + [pl.BlockSpec(memory_space=pl.ANY)] * len(after),
        out_specs=(SEM_SPEC, SEM_SPEC, *[HBM_SPEC] * (2 * n)),
        input_output_aliases={i: 2 + i for i in range(2 * n)},
        compiler_params=SPLIT_COPY,
    )(*srcs, *lands, send, recv, *after)
    return send, recv, out[0], out[1], list(out[2:2 + n]), list(out[2 + n:])


def _relayed_wait(relayed, after, name):
    send, recv, relay_send, relay_recv, srcs, lands = relayed
    n = len(srcs)
    after = list(after) if isinstance(after, (list, tuple)) else [after]

    def body(*refs):
        ins, lnd = refs[:n], refs[n:2 * n]
        send_sems, recv_sems, relay_send_sems, relay_recv_sems = refs[2 * n:2 * n + 4]
        bounce, in_sems, out_sems = refs[-n - 2:-2], refs[-2], refs[-1]
        x, y, c = _me()
        me = 4 * x + 2 * y + c
        sibling, sibling_id = _peer(x, y, c, SIBLING)
        loads = [pltpu.make_async_copy(ins[w], bounce[w], in_sems.at[w]) for w in range(n)]
        stores = [pltpu.make_async_copy(bounce[w], _row_block(lnd[w], me, ins[w].shape[0]), out_sems.at[w])
                  for w in range(n)]
        for cp in loads:
            cp.start()
        for w in range(n):
            loads[w].wait()
            stores[w].start()
        for w in range(n):
            rows = ins[w].shape[0]
            direct = pltpu.make_async_remote_copy(ins[w], _row_block(lnd[w], sibling_id, rows),
                                                  send_sems.at[w * PEERS + SIBLING - 1],
                                                  recv_sems.at[w * PEERS + SIBLING - 1], device_id=sibling,
                                                  device_id_type=MESH_IDS)
            direct.wait_send()
            direct.wait_recv()
            for k, p in enumerate(SAME_CORE_PEERS):
                _, sent_id = _peer(x, y, c, p)
                _, got_id = _peer(x, y, c, p + SIBLING)
                relay = pltpu.make_async_remote_copy(_row_block(lnd[w], sent_id, rows), _row_block(lnd[w], got_id, rows),
                                                     relay_send_sems.at[w * RELAYS + k],
                                                     relay_recv_sems.at[w * RELAYS + k], device_id=sibling,
                                                     device_id_type=MESH_IDS)
                relay.wait_send()
                relay.wait_recv()
        for cp in stores:
            cp.wait()

    out = pl.pallas_call(
        body, name=name,
        out_shape=(*[pltpu.HBM(s.shape, s.dtype) for s in srcs], *[pltpu.HBM(l.shape, l.dtype) for l in lands]),
        in_specs=[HBM_SPEC] * (2 * n) + [SEM_SPEC] * 4 + [pl.BlockSpec(memory_space=pl.ANY)] * len(after),
        out_specs=[HBM_SPEC] * (2 * n),
        input_output_aliases={i: i for i in range(2 * n)},
        scratch_shapes=[*[pltpu.VMEM(s.shape, s.dtype) for s in srcs],
                        pltpu.SemaphoreType.DMA((n,)), pltpu.SemaphoreType.DMA((n,))],
        compiler_params=SPLIT_COPY,
    )(*srcs, *lands, send, recv, relay_send, relay_recv, *after)
    return list(out[n:])


def _adamw_update(w, g, m, v):
    nm = ADAM_B1 * m + (1.0 - ADAM_B1) * g
    nv = ADAM_B2 * v + (1.0 - ADAM_B2) * (g * g)
    m_hat = nm / (1.0 - ADAM_B1 ** ADAM_STEP)
    v_hat = nv / (1.0 - ADAM_B2 ** ADAM_STEP)
    return -ADAM_LR * (m_hat / (jnp.sqrt(v_hat) + ADAM_EPS) + ADAM_WD * w), nm, nv


SUM_ADAMW_COLS = 256


def _sum_adamw(parts, w, m, v, name):
    _, rows, d = parts.shape
    n = w.shape[0]
    tc = SUM_ADAMW_COLS

    def body(p_ref, w_ref, m_ref, v_ref, g_ref, d_ref, nm_ref, nv_ref):
        g = p_ref[0].astype(F32)
        for dev in range(1, N_DEV):
            g = g + p_ref[dev].astype(F32)
        g = g[:n]
        g_ref[...] = g
        d_ref[...], nm_ref[...], nv_ref[...] = _adamw_update(w_ref[...], g, m_ref[...], v_ref[...])

    spec = pl.BlockSpec((n, tc), lambda j: (0, j))
    shape = jax.ShapeDtypeStruct((n, d), F32)
    return pl.pallas_call(
        body, name=name, grid=(d // tc,),
        in_specs=[pl.BlockSpec((N_DEV, rows, tc), lambda j: (0, 0, j)), spec, spec, spec],
        out_specs=[spec] * 4, out_shape=[shape] * 4,
        compiler_params=_params("arbitrary"),
    )(parts, w, m, v)


def _pad_rows(a, rows):
    return jnp.pad(a, ((0, rows - a.shape[0]), (0, 0)))


def _row1(vec, width=D_MODEL):
    return jnp.pad(vec.reshape(1, -1), ((0, 0), (0, width - vec.shape[-1])))


COLUMN_SHARDED = ("ffn1_w_gate", "ffn1_w_up", "w_in", "ffn2_w_gate", "ffn2_w_up")
VEC_NAMES = ("ffn1_norm", "mix_norm", "ffn2_norm", "b_forget", "pool_scale", "q_norm", "k_norm", "out_norm_pool",
             "out_norm_attn")
VEC_ROWS = 16
LOSS_ROW = len(VEC_NAMES)


def _pack_vector_grads(parts, loss_part):
    def body(*refs):
        loss_ref, out_ref = refs[-2], refs[-1]
        out_ref[...] = jnp.zeros_like(out_ref)
        lane = lax.broadcasted_iota(jnp.int32, (1, LANES), 1)
        for i, (name, ref) in enumerate(zip(VEC_NAMES, refs[:-2])):
            val = ref[...]
            if name in ("q_norm", "k_norm"):
                val = val[:, 0:LANES] + val[:, LANES:2 * LANES] + val[:, 2 * LANES:3 * LANES] + val[:, 3 * LANES:]
                val = jnp.where(lane < HEAD_DIM, val + pltpu.roll(val, HEAD_DIM, 1), 0.0)
            out_ref[pl.ds(i, 1), pl.ds(0, val.shape[1])] = val
        out_ref[pl.ds(LOSS_ROW, 1), pl.ds(0, 1)] = loss_ref[...]

    vmem = pl.BlockSpec(memory_space=pltpu.VMEM)
    return pl.pallas_call(
        body, name="pack_vector_grads", in_specs=[vmem] * (len(parts) + 1), out_specs=vmem,
        out_shape=jax.ShapeDtypeStruct((VEC_ROWS, D_MODEL), F32),
    )(*parts, loss_part)


def _small_adamw(vec_all, pool_all, vec_params, pool_params):
    nv = len(vec_params)
    pool_rows = pool_params[0].shape[0]

    def body(*refs):
        vec_ref, pool_ref = refs[0], refs[1]
        ins = refs[2:2 + 3 * nv + 3]
        outs = refs[2 + 3 * nv + 3:-1]
        rows = refs[-1]
        total = vec_ref[pl.ds(0, VEC_ROWS), :]
        for dev in range(1, N_DEV):
            total = total + vec_ref[pl.ds(dev * VEC_ROWS, VEC_ROWS), :]
        rows[...] = total
        outs[4 * nv + 4][...] = rows[pl.ds(LOSS_ROW, 1), pl.ds(0, 1)]
        for i in range(nv):
            w_ref, m_ref, v_ref = ins[3 * i:3 * i + 3]
            g = rows[pl.ds(i, 1), pl.ds(0, w_ref.shape[1])]
            outs[4 * i][...] = g
            outs[4 * i + 1][...], outs[4 * i + 2][...], outs[4 * i + 3][...] = _adamw_update(
                w_ref[...], g, m_ref[...], v_ref[...])
        g = pool_ref[pl.ds(0, pool_rows), :]
        for dev in range(1, N_DEV):
            g = g + pool_ref[pl.ds(dev * pool_rows, pool_rows), :]
        w_ref, m_ref, v_ref = ins[3 * nv:]
        outs[4 * nv][...] = g
        outs[4 * nv + 1][...], outs[4 * nv + 2][...], outs[4 * nv + 3][...] = _adamw_update(
            w_ref[...], g, m_ref[...], v_ref[...])

    vmem = pl.BlockSpec(memory_space=pltpu.VMEM)
    flat = [a for trio in vec_params for a in trio] + list(pool_params)
    out_shape = []
    for trio in list(vec_params) + [pool_params]:
        out_shape += [jax.ShapeDtypeStruct(trio[0].shape, F32)] * 4
    out_shape.append(jax.ShapeDtypeStruct((1, 1), F32))
    return pl.pallas_call(
        body, name="adamw_small", in_specs=[vmem] * (2 + len(flat)), out_specs=[vmem] * len(out_shape),
        out_shape=out_shape, scratch_shapes=[pltpu.VMEM((VEC_ROWS, D_MODEL), F32)],
    )(vec_all, pool_all, *flat)


def kernel(x, ffn1_norm, ffn1_w_gate, ffn1_w_up, ffn1_w_down, mix_norm, w_in, b_forget, pool_w, pool_scale, q_norm, k_norm, out_norm_pool, out_norm_attn, w_out, ffn2_norm, ffn2_w_gate, ffn2_w_up, ffn2_w_down, loss_target, m_ffn1_norm, m_ffn1_w_gate, m_ffn1_w_up, m_ffn1_w_down, m_mix_norm, m_w_in, m_b_forget, m_pool_w, m_pool_scale, m_q_norm, m_k_norm, m_out_norm_pool, m_out_norm_attn, m_w_out, m_ffn2_norm, m_ffn2_w_gate, m_ffn2_w_up, m_ffn2_w_down, v_ffn1_norm, v_ffn1_w_gate, v_ffn1_w_up, v_ffn1_w_down, v_mix_norm, v_w_in, v_b_forget, v_pool_w, v_pool_scale, v_q_norm, v_k_norm, v_out_norm_pool, v_out_norm_attn, v_w_out, v_ffn2_norm, v_ffn2_w_gate, v_ffn2_w_up, v_ffn2_w_down):
    bsz, seq, d = x.shape
    t = bsz * seq
    x0 = x.reshape(t, d)
    target = loss_target.reshape(t, d)
    in_rows = -(-w_in.shape[1] // BF16_ROWS) * BF16_ROWS

    slabs = [s.astype(BF16) for s in (ffn1_w_gate.T, ffn1_w_up.T, ffn1_w_down, _pad_rows(w_in.T, in_rows), w_out,
                                       ffn2_w_gate.T, ffn2_w_up.T, ffn2_w_down)]
    gathers, started = _copies_start([slabs[0:1], slabs[1:2], slabs[2:3], slabs[3:4], slabs[4:5], slabs[5:8]], True,
                                     "gather_start", relayed=(0, 1))

    g1, gm, g2 = ffn1_norm.reshape(1, d), mix_norm.reshape(1, d), ffn2_norm.reshape(1, d)
    bf_row = _row1(b_forget, LANES)
    gq = jnp.tile(q_norm, N_HEADS).reshape(1, ATTN_WIDTH)
    gk = jnp.tile(k_norm, N_HEADS).reshape(1, ATTN_WIDTH)
    scale_row = pool_scale.reshape(1, POOL_WIDTH)
    gp, ga = out_norm_pool.reshape(1, POOL_WIDTH), out_norm_attn.reshape(1, ATTN_WIDTH)

    (wg1,) = _relayed_wait(_relay_to_sibling(gathers[0], "gather_relay_ffn1_gate"), started, "gather_wait_ffn1_gate")
    h1, a1 = _ffn_gate(x0, g1, wg1, "ffn1_gate")
    (wu1,) = _relayed_wait(_relay_to_sibling(gathers[1], "gather_relay_ffn1_up", a1), a1, "gather_wait_ffn1_up")
    sa1, sb1, s1 = _ffn_up_after_gate(h1, a1, wu1, "ffn1_up")
    (wd1,) = _copies_wait(gathers[2], True, s1, "gather_wait_ffn1_down")
    (x1,) = _ffn_down(s1, wd1, x0, None, "ffn1_down")
    (win_g,) = _copies_wait(gathers[3], True, x1, "gather_wait_w_in")
    win_t = _repack_rows(win_g, in_rows, w_in.shape[1], N_DEV, "w_in_rows")
    hm, pv, q, k, v, f = _mix_in_fwd(x1, gm, win_t)
    pooled, mixed, y_pool = _pool_fwd(pv, pool_w, scale_row, gp, bsz, seq)
    qp, kp = _attn_prep_fwd(q, k, f, bf_row, gq, gk, bsz, seq)
    o, lse = _flash_fwd(qp, kp, v, bsz, seq)
    (wout,) = _copies_wait(gathers[4], True, o, "gather_wait_w_out")
    ycat, x2 = _mix_out_fwd(o, y_pool, x1, ga, wout)
    wg2, wu2, wd2 = _copies_wait(gathers[5], True, x2, "gather_wait_ffn2")
    h2, sa2, sb2, s2 = _ffn_up(x2, g2, wg2, wu2, "ffn2_up")
    dx3, dyh2, loss_part = _ffn_down(s2, wd2, x2, target, "ffn2_down")

    da2, db2, dwg2, dwu2 = _ffn_bwd_act(dyh2, sa2, sb2, h2, wd2, "ffn2_bwd_act")
    (dwd2,) = _wgrad([s2], dyh2, "ffn2_down_wgrad")
    (sent_ffn2,), tok = _copies_start([[dwg2, dwu2, dwd2]], False, "exchange_start_ffn2")
    dx2, dg2 = _ffn_bwd_dx(da2, db2, dx3, x2, g2 + tok[0, 0], wg2, wu2, "ffn2_bwd_dx")
    dx2b, dy_pool, do, dga = _mix_out_bwd(dx2, o, ga, wout)
    (dwout,) = _wgrad([ycat], dx2b, "w_out_wgrad")
    (sent_out,), tok = _copies_start([[dwout]], False, "exchange_start_w_out")
    dqp, dkp, dv = _flash_bwd(qp, kp, v, o, do, lse, bsz, seq)
    dq, dk, df, dgq, dgk, dbf = _attn_prep_bwd(dqp, dkp, q, k, f, bf_row + tok[0, 0], gq, gk, bsz, seq)
    dpv, dpool_w, dscale, dgp = _pool_bwd(dy_pool, mixed, pooled, pool_w, scale_row, gp, bsz, seq)
    dhcat, dx1, dyh1, dgm = _mix_in_bwd(dpv, dq, dk, dv, df, x1, dx2, gm, win_t)
    (dwin,) = _wgrad([dhcat], hm, "w_in_wgrad")
    dwin_blocks = _repack_rows(dwin, w_in.shape[1], in_rows, N_DEV, "w_in_grad_blocks")
    (sent_in,), tok = _copies_start([[dwin_blocks]], False, "exchange_start_w_in")
    (dwd1,) = _wgrad([s1], dyh1, "ffn1_down_wgrad")
    (sent_down1,), tok = _copies_start([[dwd1]], False, "exchange_start_ffn1_down", after=tok)
    da1, db1, dwg1, dwu1 = _ffn_bwd_act(dyh1, sa1, sb1, h1, wd1, "ffn1_bwd_act")
    (sent_up1,), tok = _copies_start([[dwg1, dwu1]], False, "exchange_start_ffn1_up", after=tok)
    dx0, dg1 = _ffn_bwd_dx(da1, db1, dx1, x0, g1 + tok[0, 0], wg1, wu1, "ffn1_bwd_dx")

    pool_rows = POOL_GROUPS * POOL_GROUP_DIM
    packed = _pack_vector_grads([dg1, dgm, dg2, dbf, dscale, dgq, dgk, dgp, dga], loss_part)
    (sent_small,), tok = _copies_start([[packed, dpool_w.reshape(pool_rows, POOL_GROUP_DIM)]], True, "small_grads_start")

    weights = dict(ffn1_norm=ffn1_norm, ffn1_w_gate=ffn1_w_gate, ffn1_w_up=ffn1_w_up, ffn1_w_down=ffn1_w_down,
                   mix_norm=mix_norm, w_in=w_in, b_forget=b_forget, pool_w=pool_w, pool_scale=pool_scale,
                   q_norm=q_norm, k_norm=k_norm, out_norm_pool=out_norm_pool, out_norm_attn=out_norm_attn,
                   w_out=w_out, ffn2_norm=ffn2_norm, ffn2_w_gate=ffn2_w_gate, ffn2_w_up=ffn2_w_up,
                   ffn2_w_down=ffn2_w_down)
    m_in = dict(ffn1_norm=m_ffn1_norm, ffn1_w_gate=m_ffn1_w_gate, ffn1_w_up=m_ffn1_w_up, ffn1_w_down=m_ffn1_w_down,
                mix_norm=m_mix_norm, w_in=m_w_in, b_forget=m_b_forget, pool_w=m_pool_w, pool_scale=m_pool_scale,
                q_norm=m_q_norm, k_norm=m_k_norm, out_norm_pool=m_out_norm_pool, out_norm_attn=m_out_norm_attn,
                w_out=m_w_out, ffn2_norm=m_ffn2_norm, ffn2_w_gate=m_ffn2_w_gate, ffn2_w_up=m_ffn2_w_up,
                ffn2_w_down=m_ffn2_w_down)
    v_in = dict(ffn1_norm=v_ffn1_norm, ffn1_w_gate=v_ffn1_w_gate, ffn1_w_up=v_ffn1_w_up, ffn1_w_down=v_ffn1_w_down,
                mix_norm=v_mix_norm, w_in=v_w_in, b_forget=v_b_forget, pool_w=v_pool_w, pool_scale=v_pool_scale,
                q_norm=v_q_norm, k_norm=v_k_norm, out_norm_pool=v_out_norm_pool, out_norm_attn=v_out_norm_attn,
                w_out=v_w_out, ffn2_norm=v_ffn2_norm, ffn2_w_gate=v_ffn2_w_gate, ffn2_w_up=v_ffn2_w_up,
                ffn2_w_down=v_ffn2_w_down)
    grads, delta, new_m, new_v = {}, {}, {}, {}
    after = [tok]
    plan = ((sent_ffn2, "ffn2", ("ffn2_w_gate", "ffn2_w_up", "ffn2_w_down")), (sent_out, "w_out", ("w_out",)),
            (sent_in, "w_in", ("w_in",)), (sent_down1, "ffn1_down", ("ffn1_w_down",)),
            (sent_up1, "ffn1_up", ("ffn1_w_gate", "ffn1_w_up")))
    for sent, tag, names in plan:
        parts = _copies_wait(sent, False, after, f"exchange_wait_{tag}")
        after = []
        for n, part in zip(names, parts):
            turn = (lambda a: a.T) if n in COLUMN_SHARDED else (lambda a: a)
            done = _sum_adamw(part, turn(weights[n]), turn(m_in[n]), turn(v_in[n]), f"adamw_{n}")
            grads[n], delta[n], new_m[n], new_v[n] = (turn(a) for a in done)
            after.append(done[3])
    vec_all, pool_all = _copies_wait(sent_small, True, after, "small_grads_wait")
    as_row = lambda a: a.reshape(1, -1)
    as_pool = lambda a: a.reshape(pool_rows, POOL_GROUP_DIM)
    small = _small_adamw(vec_all, pool_all, [tuple(as_row(z[n]) for z in (weights, m_in, v_in)) for n in VEC_NAMES],
                         tuple(as_pool(z["pool_w"]) for z in (weights, m_in, v_in)))
    for i, n in enumerate(VEC_NAMES + ("pool_w",)):
        grads[n], delta[n], new_m[n], new_v[n] = (a.reshape(weights[n].shape) for a in small[4 * i:4 * i + 4])
    loss = small[-1].reshape(())

    order = ("ffn1_norm", "ffn1_w_gate", "ffn1_w_up", "ffn1_w_down", "mix_norm", "w_in", "b_forget", "pool_w",
             "pool_scale", "q_norm", "k_norm", "out_norm_pool", "out_norm_attn", "w_out", "ffn2_norm", "ffn2_w_gate",
             "ffn2_w_up", "ffn2_w_down")
    return (loss, dx0.reshape(bsz, seq, d), *[grads[n] for n in order], *[delta[n] for n in order],
            *[new_m[n] for n in order], *[new_v[n] for n in order])
```

```python
import functools

import jax
import jax.numpy as jnp
from jax import lax
from jax.experimental import pallas as pl
from jax.experimental.pallas import tpu as pltpu

F32 = jnp.float32
BF16 = jnp.bfloat16

EPS = 1e-6
D_MODEL = 1024
D_FF = 2816
N_HEADS = 8
HEAD_DIM = 64
POOL_WIDTH = 512
ATTN_WIDTH = 512
POOL_GROUPS = 4
POOL_GROUP_DIM = 128
POOL_WINDOWS = (2, 4, 8, 16)
POOL_HALO = 16
MIX_PAD = POOL_WIDTH + 3 * ATTN_WIDTH + 128
N_DEV = 8
BF16_ROWS = 16
LANES = 128
VMEM_LIMIT = 56 * 1024 * 1024

ADAM_LR = 0.001
ADAM_B1 = 0.9
ADAM_B2 = 0.999
ADAM_EPS = 1e-08
ADAM_WD = 0.01
ADAM_STEP = 10


def _params(*sem):
    return pltpu.CompilerParams(dimension_semantics=sem, vmem_limit_bytes=VMEM_LIMIT)


def _dot(a, b):
    return jnp.dot(a, b, preferred_element_type=F32)


def _dot_nt(a, b):
    return lax.dot_general(a, b, (((1,), (1,)), ((), ())), preferred_element_type=F32)


def _dot_tn(a, b):
    return lax.dot_general(a, b, (((0,), (0,)), ((), ())), preferred_element_type=F32)


def _resident(shape):
    return pl.BlockSpec(shape, lambda *_: (0,) * len(shape), pipeline_mode=pl.Buffered(1))


def _rows(tm, width):
    return pl.BlockSpec((tm, width), lambda i: (i, 0))


def _rms_scale(x):
    return lax.rsqrt(jnp.mean(x * x, axis=-1, keepdims=True) + EPS)


def _rms_bwd(dh, x, gain):
    r = _rms_scale(x)
    n = x * r
    dgain = jnp.sum(dh * n, axis=0, keepdims=True)
    dn = dh * gain
    dx = r * (dn - n * jnp.mean(dn * n, axis=-1, keepdims=True))
    return dx, dgain


def _split3(x):
    hi = x.astype(BF16)
    r1 = x - hi.astype(F32)
    mid = r1.astype(BF16)
    lo = (r1 - mid.astype(F32)).astype(BF16)
    return hi, mid, lo


def _split2(x):
    hi = x.astype(BF16)
    return hi, (x - hi.astype(F32)).astype(BF16)


FF_CHUNK = 256


def _swiglu_parts(a, b):
    sig = jax.nn.sigmoid(a)
    silu = a * sig
    return (b * (sig + silu * (1.0 - sig))).astype(BF16), silu.astype(BF16), (silu * b).astype(BF16)


def _ffn_up(x, gain, wg_t, wu_t, name):
    t, d = x.shape
    f = wg_t.shape[0]
    tm = 512

    def body(x_ref, g_ref, wg_ref, wu_ref, h_ref, sa_ref, sb_ref, s_ref):
        xv = x_ref[...]
        h = (xv * _rms_scale(xv) * g_ref[...]).astype(BF16)
        h_ref[...] = h
        for c in range(f // FF_CHUNK):
            sl = pl.ds(c * FF_CHUNK, FF_CHUNK)
            sa_ref[:, sl], sb_ref[:, sl], s_ref[:, sl] = _swiglu_parts(_dot_nt(h, wg_ref[sl, :]), _dot_nt(h, wu_ref[sl, :]))

    wide = jax.ShapeDtypeStruct((t, f), BF16)
    return pl.pallas_call(
        body, name=name, grid=(t // tm,),
        in_specs=[_rows(tm, d), _resident((1, d)), _resident((f, d)), _resident((f, d))],
        out_specs=[_rows(tm, d), _rows(tm, f), _rows(tm, f), _rows(tm, f)],
        out_shape=[jax.ShapeDtypeStruct((t, d), BF16), wide, wide, wide],
        compiler_params=_params("arbitrary"),
    )(x, gain, wg_t, wu_t)


def _ffn_down(s, wd, x, target, name):
    t, d = x.shape
    f = wd.shape[0]
    tm = 512
    with_loss = target is not None

    def body(*refs):
        if with_loss:
            s_ref, w_ref, x_ref, t_ref, dy_ref, dyh_ref, loss_ref = refs
        else:
            s_ref, w_ref, x_ref, y_ref = refs
        y = x_ref[...] + 0.5 * _dot(s_ref[...], w_ref[...])
        if with_loss:
            e = y - t_ref[...]
            dy = e * (1.0 / d)
            dy_ref[...] = dy
            dyh_ref[...] = (0.5 * dy).astype(BF16)

            @pl.when(pl.program_id(0) == 0)
            def _():
                loss_ref[...] = jnp.zeros_like(loss_ref)

            part = jnp.sum(jnp.sum(e * e, axis=0, keepdims=True), axis=1, keepdims=True)
            loss_ref[...] += part * (0.5 / d)
        else:
            y_ref[...] = y

    in_specs = [_rows(tm, f), _resident((f, d)), _rows(tm, d)]
    args = [s, wd, x]
    if with_loss:
        in_specs.append(_rows(tm, d))
        args.append(target)
        out_shape = [jax.ShapeDtypeStruct((t, d), F32), jax.ShapeDtypeStruct((t, d), BF16),
                     jax.ShapeDtypeStruct((1, 1), F32)]
        out_specs = [_rows(tm, d), _rows(tm, d), pl.BlockSpec((1, 1), lambda i: (0, 0))]
    else:
        out_shape = [jax.ShapeDtypeStruct((t, d), F32)]
        out_specs = [_rows(tm, d)]
    return pl.pallas_call(
        body, name=name, grid=(t // tm,), in_specs=in_specs, out_specs=out_specs, out_shape=out_shape,
        compiler_params=_params("arbitrary"),
    )(*args)


def _ffn_bwd_act(dyh, sa, sb, h, wd, name):
    t, d = dyh.shape
    f = wd.shape[0]
    tn = f // 2
    tk = 512
    nk = t // tk

    def body(dy_ref, sa_ref, sb_ref, h_ref, wd_ref, da_ref, db_ref, dwg_ref, dwu_ref, acc_g, acc_u):
        k = pl.program_id(1)

        @pl.when(k == 0)
        def _():
            acc_g[...] = jnp.zeros_like(acc_g)
            acc_u[...] = jnp.zeros_like(acc_u)

        ds = _dot_nt(dy_ref[...], wd_ref[...])
        da = (ds * sa_ref[...].astype(F32)).astype(BF16)
        db = (ds * sb_ref[...].astype(F32)).astype(BF16)
        da_ref[...] = da
        db_ref[...] = db
        hv = h_ref[...]
        acc_g[...] += _dot_tn(da, hv)
        acc_u[...] += _dot_tn(db, hv)

        @pl.when(k == nk - 1)
        def _():
            dwg_ref[...] = acc_g[...].astype(BF16)
            dwu_ref[...] = acc_u[...].astype(BF16)

    tokens = pl.BlockSpec((tk, d), lambda j, k: (k, 0))
    wide = pl.BlockSpec((tk, tn), lambda j, k: (k, j))
    weight = pl.BlockSpec((tn, d), lambda j, k: (j, 0))
    return pl.pallas_call(
        body, name=name, grid=(f // tn, nk),
        in_specs=[tokens, wide, wide, tokens, weight],
        out_specs=[wide, wide, weight, weight],
        out_shape=[jax.ShapeDtypeStruct((t, f), BF16)] * 2 + [jax.ShapeDtypeStruct((f, d), BF16)] * 2,
        scratch_shapes=[pltpu.VMEM((tn, d), F32)] * 2,
        compiler_params=_params("arbitrary", "arbitrary"),
    )(dyh, sa, sb, h, wd)


def _ffn_bwd_dx(da, db, dy, x, gain, wg_t, wu_t, name):
    t, d = x.shape
    f = wg_t.shape[0]
    tm = 512

    def body(da_ref, db_ref, dy_ref, x_ref, g_ref, wg_ref, wu_ref, dx_ref, dg_ref):
        dh = _dot(da_ref[...], wg_ref[...]) + _dot(db_ref[...], wu_ref[...])
        dx, dgain = _rms_bwd(dh, x_ref[...], g_ref[...])
        dx_ref[...] = dy_ref[...] + dx

        @pl.when(pl.program_id(0) == 0)
        def _():
            dg_ref[...] = jnp.zeros_like(dg_ref)

        dg_ref[...] += dgain

    return pl.pallas_call(
        body, name=name, grid=(t // tm,),
        in_specs=[_rows(tm, f), _rows(tm, f), _rows(tm, d), _rows(tm, d), _resident((1, d)), _resident((f, d)),
                  _resident((f, d))],
        out_specs=[_rows(tm, d), pl.BlockSpec((1, d), lambda i: (0, 0))],
        out_shape=[jax.ShapeDtypeStruct((t, d), F32), jax.ShapeDtypeStruct((1, d), F32)],
        compiler_params=_params("arbitrary"),
    )(da, db, dy, x, gain, wg_t, wu_t)


def _wgrad(lhs, b, name):
    t, n = lhs[0].shape
    d = b.shape[1]
    m = len(lhs)
    tn = n // 2 if n * d * m > (4 << 20) else n
    tk = 1024
    nk = t // tk

    def body(*refs):
        a_refs, b_ref, o_refs, accs = refs[:m], refs[m], refs[m + 1:2 * m + 1], refs[2 * m + 1:]
        k = pl.program_id(1)

        @pl.when(k == 0)
        def _():
            for acc in accs:
                acc[...] = jnp.zeros_like(acc)

        bv = b_ref[...]
        for a_ref, acc in zip(a_refs, accs):
            acc[...] += _dot_tn(a_ref[...], bv)

        @pl.when(k == nk - 1)
        def _():
            for o_ref, acc in zip(o_refs, accs):
                o_ref[...] = acc[...].astype(BF16)

    return pl.pallas_call(
        body, name=name, grid=(n // tn, nk),
        in_specs=[pl.BlockSpec((tk, tn), lambda j, k: (k, j))] * m + [pl.BlockSpec((tk, d), lambda j, k: (k, 0))],
        out_specs=[pl.BlockSpec((tn, d), lambda j, k: (j, 0))] * m,
        out_shape=[jax.ShapeDtypeStruct((n, d), BF16)] * m,
        scratch_shapes=[pltpu.VMEM((tn, d), F32)] * m,
        compiler_params=_params("arbitrary", "arbitrary"),
    )(*lhs, b)


def _repack_rows(a, rows_in, rows_out, blocks, name):
    total, d = a.shape
    real = min(rows_in, rows_out)

    def body(a_ref, o_ref, wide_in, wide_out):
        wide_in[...] = a_ref[...].astype(F32)
        wide_out[...] = jnp.zeros_like(wide_out)
        for j in range(blocks):
            wide_out[pl.ds(j * rows_out, real), :] = wide_in[pl.ds(j * rows_in, real), :]
        o_ref[...] = wide_out[...].astype(BF16)

    full = pl.BlockSpec((total, d), lambda i: (0, 0))
    return pl.pallas_call(
        body, name=name, grid=(1,), in_specs=[full], out_specs=full, out_shape=jax.ShapeDtypeStruct((total, d), BF16),
        scratch_shapes=[pltpu.VMEM((total, d), F32)] * 2,
        compiler_params=_params("arbitrary"),
    )(a)


def _mix_in_fwd(x, gain, w_in_t):
    t, d = x.shape
    tm = 1024
    pw, aw = POOL_WIDTH, ATTN_WIDTH

    def body(x_ref, g_ref, w_ref, hm_ref, pv_ref, q_ref, k_ref, v_ref, f_ref):
        xv = x_ref[...]
        hm = (xv * _rms_scale(xv) * g_ref[...]).astype(BF16)
        hm_ref[...] = hm
        pv_ref[...] = _dot_nt(hm, w_ref[pl.ds(0, pw), :])
        q_ref[...] = _dot_nt(hm, w_ref[pl.ds(pw, aw), :])
        k_ref[...] = _dot_nt(hm, w_ref[pl.ds(pw + aw, aw), :])
        v_ref[...] = _dot_nt(hm, w_ref[pl.ds(pw + 2 * aw, aw), :]).astype(BF16)
        f_ref[...] = _dot_nt(hm, w_ref[pl.ds(pw + 3 * aw, LANES), :])

    return pl.pallas_call(
        body, name="mix_in_fwd", grid=(t // tm,),
        in_specs=[_rows(tm, d), _resident((1, d)), _resident((MIX_PAD, d))],
        out_specs=[_rows(tm, d), _rows(tm, pw), _rows(tm, aw), _rows(tm, aw), _rows(tm, aw), _rows(tm, LANES)],
        out_shape=[jax.ShapeDtypeStruct((t, d), BF16), jax.ShapeDtypeStruct((t, pw), F32),
                   jax.ShapeDtypeStruct((t, aw), F32), jax.ShapeDtypeStruct((t, aw), F32),
                   jax.ShapeDtypeStruct((t, aw), BF16), jax.ShapeDtypeStruct((t, LANES), F32)],
        compiler_params=_params("arbitrary"),
    )(x, gain, w_in_t)


def _pool_fwd(pv, pool_w, pool_scale, gain, bsz, seq):
    ts = 512
    ns = seq // ts
    pw = POOL_WIDTH

    def body(pv_ref, w_ref, sc_ref, g_ref, pooled_ref, mixed_ref, y_ref, ext):
        s = pl.program_id(1)

        @pl.when(s == 0)
        def _():
            ext[pl.ds(0, POOL_HALO), :] = jnp.zeros((POOL_HALO, pw), F32)

        p = pv_ref[...]
        ext[pl.ds(POOL_HALO, ts), :] = p
        pos = s * ts + lax.broadcasted_iota(jnp.int32, (ts, 1), 0)
        parts = []
        for g, w in enumerate(POOL_WINDOWS):
            lanes = pl.ds(g * POOL_GROUP_DIM, POOL_GROUP_DIM)
            win = ext[pl.ds(POOL_HALO, ts), lanes]
            for i in range(1, w):
                win = win + ext[pl.ds(POOL_HALO - i, ts), lanes]
            cnt = jnp.minimum(pos + 1, w).astype(F32)
            pooled = (win / cnt - ext[pl.ds(POOL_HALO, ts), lanes]).astype(BF16)
            pooled_ref[:, lanes] = pooled
            parts.append(_dot(pooled, w_ref[g].astype(BF16)))
        mixed = jnp.concatenate(parts, axis=1)
        mixed_ref[...] = mixed
        pm = mixed * sc_ref[...]
        y_ref[...] = (pm * _rms_scale(pm) * g_ref[...]).astype(BF16)
        ext[pl.ds(0, POOL_HALO), :] = p[ts - POOL_HALO:, :]

    blk = pl.BlockSpec((ts, pw), lambda b, s: (b * ns + s, 0))
    t = bsz * seq
    return pl.pallas_call(
        body, name="pool_fwd", grid=(bsz, ns),
        in_specs=[blk, pl.BlockSpec((POOL_GROUPS, POOL_GROUP_DIM, POOL_GROUP_DIM), lambda b, s: (0, 0, 0)),
                  pl.BlockSpec((1, pw), lambda b, s: (0, 0)), pl.BlockSpec((1, pw), lambda b, s: (0, 0))],
        out_specs=[blk, blk, blk],
        out_shape=[jax.ShapeDtypeStruct((t, pw), BF16), jax.ShapeDtypeStruct((t, pw), F32),
                   jax.ShapeDtypeStruct((t, pw), BF16)],
        scratch_shapes=[pltpu.VMEM((POOL_HALO + ts, pw), F32)],
        compiler_params=_params("arbitrary", "arbitrary"),
    )(pv, pool_w, pool_scale, gain)


def _pool_bwd(dy, mixed, pooled, pool_w, pool_scale, gain, bsz, seq):
    ts = 512
    ns = seq // ts
    pw = POOL_WIDTH

    def body(dy_ref, mixed_ref, pooled_ref, w_ref, sc_ref, g_ref, dpv_ref, dw_ref, dsc_ref, dg_ref, ext):
        b = pl.program_id(0)
        sr = pl.program_id(1)
        s = ns - 1 - sr

        @pl.when(jnp.logical_and(b == 0, sr == 0))
        def _():
            dw_ref[...] = jnp.zeros_like(dw_ref)
            dsc_ref[...] = jnp.zeros_like(dsc_ref)
            dg_ref[...] = jnp.zeros_like(dg_ref)

        @pl.when(sr == 0)
        def _():
            ext[pl.ds(ts, POOL_HALO), :] = jnp.zeros((POOL_HALO, pw), F32)

        mixed = mixed_ref[...]
        sc = sc_ref[...]
        dpm, dgain = _rms_bwd(dy_ref[...], mixed * sc, g_ref[...])
        dg_ref[...] += dgain
        dsc_ref[...] += jnp.sum(dpm * mixed, axis=0, keepdims=True)
        dmixed = (dpm * sc).astype(BF16)
        pos = s * ts + lax.broadcasted_iota(jnp.int32, (ts, 1), 0)
        dpooled = []
        for g, w in enumerate(POOL_WINDOWS):
            lanes = pl.ds(g * POOL_GROUP_DIM, POOL_GROUP_DIM)
            dm = dmixed[:, g * POOL_GROUP_DIM:(g + 1) * POOL_GROUP_DIM]
            dw_ref[g] += _dot_tn(pooled_ref[:, lanes], dm)
            dp = _dot_nt(dm, w_ref[g].astype(BF16))
            dpooled.append(dp)
            cnt = jnp.minimum(pos + 1, w).astype(F32)
            ext[pl.ds(0, ts), lanes] = dp / cnt
        for g, w in enumerate(POOL_WINDOWS):
            lanes = pl.ds(g * POOL_GROUP_DIM, POOL_GROUP_DIM)
            win = ext[pl.ds(0, ts), lanes]
            for i in range(1, w):
                win = win + ext[pl.ds(i, ts), lanes]
            dpv_ref[:, lanes] = (win - dpooled[g]).astype(BF16)
        head = ext[pl.ds(0, POOL_HALO), :]
        ext[pl.ds(ts, POOL_HALO), :] = head

    blk = pl.BlockSpec((ts, pw), lambda b, s: (b * ns + (ns - 1 - s), 0))
    vec = pl.BlockSpec((1, pw), lambda b, s: (0, 0))
    wspec = pl.BlockSpec((POOL_GROUPS, POOL_GROUP_DIM, POOL_GROUP_DIM), lambda b, s: (0, 0, 0))
    t = bsz * seq
    return pl.pallas_call(
        body, name="pool_bwd", grid=(bsz, ns),
        in_specs=[blk, blk, blk, wspec, vec, vec],
        out_specs=[blk, wspec, vec, vec],
        out_shape=[jax.ShapeDtypeStruct((t, pw), BF16),
                   jax.ShapeDtypeStruct((POOL_GROUPS, POOL_GROUP_DIM, POOL_GROUP_DIM), F32),
                   jax.ShapeDtypeStruct((1, pw), F32), jax.ShapeDtypeStruct((1, pw), F32)],
        scratch_shapes=[pltpu.VMEM((ts + POOL_HALO, pw), F32)],
        compiler_params=_params("arbitrary", "arbitrary"),
    )(dy, mixed, pooled, pool_w, pool_scale, gain)


AUX_ONE = 64
AUX_F = 67

ATTN_PREP_ROWS = 512


def _seg_ones(width, seg):
    r = lax.broadcasted_iota(jnp.int32, (width, width), 0) // seg
    c = lax.broadcasted_iota(jnp.int32, (width, width), 1) // seg
    return (r == c).astype(BF16)


def _tri_ones(n, lower):
    r = lax.broadcasted_iota(jnp.int32, (n, n), 0)
    c = lax.broadcasted_iota(jnp.int32, (n, n), 1)
    return ((r >= c) if lower else (r <= c)).astype(BF16)


def _place_pieces(first_lane):
    r = lax.broadcasted_iota(jnp.int32, (3 * LANES, N_HEADS * LANES), 0)
    c = lax.broadcasted_iota(jnp.int32, (3 * LANES, N_HEADS * LANES), 1)
    piece, head = r // LANES, r % LANES
    return jnp.logical_and(head < N_HEADS, c == head * LANES + first_lane + piece).astype(BF16)


def _head_sums(x, seg_ones):
    hi, lo = _split2(x)
    return _dot(hi, seg_ones) + _dot(lo, seg_ones)


def _log_sigmoid(x):
    return jnp.minimum(x, 0.0) - jnp.log(1.0 + jnp.exp(-jnp.abs(x)))


def _attn_prep_fwd(q, k, f, b_forget, q_gain, k_gain, bsz, seq):
    ts = ATTN_PREP_ROWS
    ns = seq // ts
    aw = ATTN_WIDTH
    t = bsz * seq
    seg = _seg_ones(aw, HEAD_DIM)
    tri = _tri_ones(ts, True)

    def body(q_ref, k_ref, f_ref, bf_ref, gq_ref, gk_ref, seg_ref, tri_ref, pq_ref, pk_ref, qp_ref, kp_ref, carry):
        s = pl.program_id(1)

        @pl.when(s == 0)
        def _():
            carry[...] = jnp.zeros_like(carry)

        logf = _log_sigmoid(f_ref[...] + bf_ref[...])
        hi, mid, lo = _split3(logf)
        tri_v = tri_ref[...]
        fc = _dot(tri_v, hi) + _dot(tri_v, mid) + _dot(tri_v, lo) + carry[pl.ds(0, 1), :]
        carry[pl.ds(0, 1), :] = fc[ts - 1:, :]
        pcs = jnp.concatenate(_split3(fc), axis=1)
        lane = lax.broadcasted_iota(jnp.int32, (1, LANES), 1)
        ones_q = jnp.logical_and(lane >= AUX_ONE, lane < AUX_ONE + 3).astype(F32)
        ones_k = jnp.logical_and(lane >= AUX_F, lane < AUX_F + 3).astype(F32)
        seg_v = seg_ref[...]

        def build(x_ref, g_ref, scale, out_ref, ones, place_ref, f_sign):
            xv = x_ref[...]
            r = lax.rsqrt(_head_sums(xv * xv, seg_v) * (1.0 / HEAD_DIM) + EPS)
            xn = xv * r * g_ref[...] * scale
            aux = _dot(pcs, place_ref[...]) * f_sign
            for h in range(N_HEADS):
                pair = xn[:, (h // 2) * LANES:(h // 2 + 1) * LANES]
                feat = pair if h % 2 == 0 else pltpu.roll(pair, HEAD_DIM, 1)
                aux_h = aux[:, h * LANES:(h + 1) * LANES] + ones
                out_ref[:, h * LANES:(h + 1) * LANES] = jnp.where(lane < HEAD_DIM, feat, aux_h).astype(BF16)

        build(q_ref, gq_ref, 0.125, qp_ref, ones_q, pq_ref, 1.0)
        build(k_ref, gk_ref, 1.0, kp_ref, ones_k, pk_ref, -1.0)

    blk = pl.BlockSpec((ts, aw), lambda b, s: (b * ns + s, 0))
    fblk = pl.BlockSpec((ts, LANES), lambda b, s: (b * ns + s, 0))
    oblk = pl.BlockSpec((ts, N_HEADS * LANES), lambda b, s: (b * ns + s, 0))
    const = lambda shape: pl.BlockSpec(shape, lambda b, s: (0, 0))
    return pl.pallas_call(
        body, name="attn_prep_fwd", grid=(bsz, ns),
        in_specs=[blk, blk, fblk, const((1, LANES)), const((1, aw)), const((1, aw)), const((aw, aw)), const((ts, ts)),
                  const((3 * LANES, N_HEADS * LANES)), const((3 * LANES, N_HEADS * LANES))],
        out_specs=[oblk, oblk],
        out_shape=[jax.ShapeDtypeStruct((t, N_HEADS * LANES), BF16)] * 2,
        scratch_shapes=[pltpu.VMEM((8, LANES), F32)],
        compiler_params=_params("arbitrary", "arbitrary"),
    )(q, k, f, b_forget, q_gain, k_gain, seg, tri, _place_pieces(AUX_F), _place_pieces(AUX_ONE))


def _attn_prep_bwd(dqp, dkp, q, k, f, b_forget, q_gain, k_gain, bsz, seq):
    ts = ATTN_PREP_ROWS
    ns = seq // ts
    aw = ATTN_WIDTH
    t = bsz * seq
    seg = _seg_ones(aw, HEAD_DIM)
    tri = _tri_ones(ts, False)

    def body(dqp_ref, dkp_ref, q_ref, k_ref, f_ref, bf_ref, gq_ref, gk_ref, seg_ref, tri_ref,
             dq_ref, dk_ref, df_ref, dgq_ref, dgk_ref, dbf_ref, carry):
        b = pl.program_id(0)
        sr = pl.program_id(1)

        @pl.when(jnp.logical_and(b == 0, sr == 0))
        def _():
            dgq_ref[...] = jnp.zeros_like(dgq_ref)
            dgk_ref[...] = jnp.zeros_like(dgk_ref)
            dbf_ref[...] = jnp.zeros_like(dbf_ref)

        @pl.when(sr == 0)
        def _():
            carry[...] = jnp.zeros_like(carry)

        lane = lax.broadcasted_iota(jnp.int32, (1, LANES), 1)
        seg_v = seg_ref[...]

        def norm_bwd(dp_ref, x_ref, g_ref, scale, dx_ref, dgain_ref):
            parts = []
            for j in range(N_HEADS // 2):
                even = dp_ref[:, (2 * j) * LANES:(2 * j + 1) * LANES]
                odd = dp_ref[:, (2 * j + 1) * LANES:(2 * j + 2) * LANES]
                parts.append(jnp.where(lane < HEAD_DIM, even, pltpu.roll(odd, HEAD_DIM, 1)))
            dxn = jnp.concatenate(parts, axis=1) * scale
            xv = x_ref[...]
            r = lax.rsqrt(_head_sums(xv * xv, seg_v) * (1.0 / HEAD_DIM) + EPS)
            n = xv * r
            dgain_ref[...] += jnp.sum(dxn * n, axis=0, keepdims=True)
            dn = dxn * g_ref[...]
            m = _head_sums(dn * n, seg_v) * (1.0 / HEAD_DIM)
            dx_ref[...] = (r * (dn - n * m)).astype(BF16)

        norm_bwd(dqp_ref, q_ref, gq_ref, 0.125, dq_ref, dgq_ref)
        norm_bwd(dkp_ref, k_ref, gk_ref, 1.0, dk_ref, dgk_ref)

        dfc = jnp.zeros((ts, LANES), F32)
        for h in range(N_HEADS):
            cols = pl.ds(h * LANES, LANES)
            both = jnp.where(lane == AUX_F, dqp_ref[:, cols], 0.0) - jnp.where(lane == AUX_ONE, dkp_ref[:, cols], 0.0)
            dfc = jnp.where(lane == h, jnp.sum(both, axis=1, keepdims=True), dfc)
        hi, mid, lo = _split3(dfc)
        tri_v = tri_ref[...]
        dlogf = _dot(tri_v, hi) + _dot(tri_v, mid) + _dot(tri_v, lo) + carry[pl.ds(0, 1), :]
        carry[pl.ds(0, 1), :] = dlogf[0:1, :]
        df = jnp.where(lane < N_HEADS, dlogf * jax.nn.sigmoid(-(f_ref[...] + bf_ref[...])), 0.0)
        df_ref[...] = df.astype(BF16)
        dbf_ref[...] += jnp.sum(df, axis=0, keepdims=True)

    rev = lambda b, s: (b * ns + (ns - 1 - s), 0)
    blk = pl.BlockSpec((ts, aw), rev)
    fblk = pl.BlockSpec((ts, LANES), rev)
    pblk = pl.BlockSpec((ts, N_HEADS * LANES), rev)
    const = lambda shape: pl.BlockSpec(shape, lambda b, s: (0, 0))
    return pl.pallas_call(
        body, name="attn_prep_bwd", grid=(bsz, ns),
        in_specs=[pblk, pblk, blk, blk, fblk, const((1, LANES)), const((1, aw)), const((1, aw)), const((aw, aw)),
                  const((ts, ts))],
        out_specs=[blk, blk, fblk, const((1, aw)), const((1, aw)), const((1, LANES))],
        out_shape=[jax.ShapeDtypeStruct((t, aw), BF16), jax.ShapeDtypeStruct((t, aw), BF16),
                   jax.ShapeDtypeStruct((t, LANES), BF16), jax.ShapeDtypeStruct((1, aw), F32),
                   jax.ShapeDtypeStruct((1, aw), F32), jax.ShapeDtypeStruct((1, LANES), F32)],
        scratch_shapes=[pltpu.VMEM((8, LANES), F32)],
        compiler_params=_params("arbitrary", "arbitrary"),
    )(dqp, dkp, q, k, f, b_forget, q_gain, k_gain, seg, tri)


ATTN_BLOCK = 1024
HEAD_PAIRS = N_HEADS // 2


def _flash_fwd(qp, kp, v, bsz, seq):
    tq = ATTN_BLOCK
    half = tq // 2
    nq = seq // tq
    t = bsz * seq

    def body(q_ref, k_ref, v_ref, o_ref, lse_ref, m_sc, l_sc, acc_sc):
        i = pl.program_id(2)
        m_sc[...] = jnp.full(m_sc.shape, -jnp.inf, F32)
        l_sc[...] = jnp.zeros_like(l_sc)
        acc_sc[...] = jnp.zeros_like(acc_sc)
        lane = lax.broadcasted_iota(jnp.int32, (1, LANES), 1)
        low = lane < HEAD_DIM

        def tile(q0, qn, k_start, kn, k0=None):
            qs = pl.ds(q0, qn)
            ks = pl.ds(k_start, kn)
            vv = v_ref[ks, :]
            for h in range(2):
                mine = low if h == 0 else jnp.logical_not(low)
                cols = pl.ds(h * LANES, LANES)
                s = _dot_nt(q_ref[qs, cols], k_ref[ks, cols])
                if k0 is not None:
                    row = lax.broadcasted_iota(jnp.int32, (qn, kn), 0) + q0
                    col = lax.broadcasted_iota(jnp.int32, (qn, kn), 1) + k0
                    s = jnp.where(row >= col, s, -jnp.inf)
                m_prev = m_sc[h, qs, :]
                m_new = jnp.maximum(m_prev, jnp.max(s, axis=1, keepdims=True))
                p = jnp.exp(s - jnp.tile(m_new, (1, kn // LANES)))
                alpha = jnp.exp(m_prev - m_new)
                l_sc[h, qs, :] = alpha * l_sc[h, qs, :] + jnp.sum(p, axis=1, keepdims=True)
                m_sc[h, qs, :] = m_new
                pv = _dot(p.astype(BF16), jnp.where(mine, vv, jnp.zeros_like(vv)))
                acc_sc[qs, :] = acc_sc[qs, :] * jnp.where(mine, alpha, 1.0) + pv

        def below_diagonal(j, carry):
            tile(0, tq, pl.multiple_of(j * tq, tq), tq)
            return carry

        lax.fori_loop(0, i, below_diagonal, 0)
        diagonal = pl.multiple_of(i * tq, tq)
        tile(0, tq, diagonal, half, k0=0)
        tile(half, half, diagonal + half, half, k0=half)
        l = jnp.where(low, l_sc[0], l_sc[1])
        m = jnp.where(low, m_sc[0], m_sc[1])
        o_ref[...] = acc_sc[...] / l
        lse_ref[...] = m + jnp.log(l)

    qspec = pl.BlockSpec((tq, 2 * LANES), lambda b, hp, i: (b * nq + i, hp))
    kspec = pl.BlockSpec((seq, 2 * LANES), lambda b, hp, i: (b, hp))
    vspec = pl.BlockSpec((seq, LANES), lambda b, hp, i: (b, hp))
    ospec = pl.BlockSpec((tq, LANES), lambda b, hp, i: (b * nq + i, hp))
    return pl.pallas_call(
        body, name="flash_fwd", grid=(bsz, HEAD_PAIRS, nq),
        in_specs=[qspec, kspec, vspec], out_specs=[ospec, ospec],
        out_shape=[jax.ShapeDtypeStruct((t, ATTN_WIDTH), F32), jax.ShapeDtypeStruct((t, ATTN_WIDTH), F32)],
        scratch_shapes=[pltpu.VMEM((2, tq, LANES), F32), pltpu.VMEM((2, tq, LANES), F32), pltpu.VMEM((tq, LANES), F32)],
        compiler_params=_params("arbitrary", "arbitrary", "arbitrary"),
    )(qp, kp, v)


def _flash_bwd(qp, kp, v, o, do, lse, bsz, seq):
    tq = ATTN_BLOCK
    half = tq // 2
    nq = seq // tq
    t = bsz * seq

    def body(q_ref, k_ref, v_ref, o_ref, do_ref, lse_ref, dq_ref, dk_ref, dv_ref, dk_acc, dv_acc):
        j = pl.program_id(2)

        @pl.when(j == 0)
        def _():
            dq_ref[...] = jnp.zeros_like(dq_ref)

        dk_acc[...] = jnp.zeros_like(dk_acc)
        dv_acc[...] = jnp.zeros_like(dv_acc)
        lane = lax.broadcasted_iota(jnp.int32, (1, LANES), 1)
        low = lane < HEAD_DIM

        def tile(q_start, qn, k0, kn, q0=None):
            rows = pl.ds(q_start, qn)
            ks = pl.ds(k0, kn)
            dov = do_ref[rows, :]
            dd = dov * o_ref[rows, :]
            dob = dov.astype(BF16)
            vv = v_ref[ks, :]
            lse_v = lse_ref[rows, :]
            for h in range(2):
                mine = low if h == 0 else jnp.logical_not(low)
                cols = pl.ds(h * LANES, LANES)
                qh = q_ref[rows, cols]
                kh = k_ref[ks, cols]
                s = _dot_nt(qh, kh)
                lse_h = jnp.where(mine, lse_v, pltpu.roll(lse_v, HEAD_DIM, 1))
                p = jnp.exp(s - jnp.tile(lse_h, (1, kn // LANES)))
                if q0 is not None:
                    row = lax.broadcasted_iota(jnp.int32, (qn, kn), 0) + q0
                    col = lax.broadcasted_iota(jnp.int32, (qn, kn), 1) + k0
                    p = jnp.where(row >= col, p, 0.0)
                delta = jnp.sum(jnp.where(mine, dd, 0.0), axis=1, keepdims=True)
                dp = _dot_nt(dob, jnp.where(mine, vv, jnp.zeros_like(vv)))
                ds = (p * (dp - delta)).astype(BF16)
                dv_acc[ks, :] += jnp.where(mine, _dot_tn(p.astype(BF16), dob), 0.0)
                dk_acc[ks, cols] += _dot_tn(ds, qh)
                dq_ref[rows, cols] += _dot(ds, kh)

        def above_diagonal(i, carry):
            tile(pl.multiple_of(i * tq, tq), tq, 0, tq)
            return carry

        diagonal = pl.multiple_of(j * tq, tq)
        tile(diagonal, tq, 0, half, q0=0)
        tile(diagonal + half, half, half, half, q0=half)
        lax.fori_loop(j + 1, nq, above_diagonal, 0)
        dk_ref[...] = dk_acc[...]
        dv_ref[...] = dv_acc[...].astype(BF16)

    qspec = pl.BlockSpec((seq, 2 * LANES), lambda b, hp, j: (b, hp))
    kspec = pl.BlockSpec((tq, 2 * LANES), lambda b, hp, j: (b * nq + j, hp))
    vspec = pl.BlockSpec((tq, LANES), lambda b, hp, j: (b * nq + j, hp))
    ospec = pl.BlockSpec((seq, LANES), lambda b, hp, j: (b, hp))
    return pl.pallas_call(
        body, name="flash_bwd", grid=(bsz, HEAD_PAIRS, nq),
        in_specs=[qspec, kspec, vspec, ospec, ospec, ospec], out_specs=[qspec, kspec, vspec],
        out_shape=[jax.ShapeDtypeStruct((t, N_HEADS * LANES), F32), jax.ShapeDtypeStruct((t, N_HEADS * LANES), F32),
                   jax.ShapeDtypeStruct((t, ATTN_WIDTH), BF16)],
        scratch_shapes=[pltpu.VMEM((tq, 2 * LANES), F32), pltpu.VMEM((tq, LANES), F32)],
        compiler_params=_params("arbitrary", "arbitrary", "arbitrary"),
    )(qp, kp, v, o, do, lse)


def _mix_out_fwd(o, y_pool, x, gain, w_out):
    t, d = x.shape
    tm = 1024
    pw, aw = POOL_WIDTH, ATTN_WIDTH

    def body(o_ref, yp_ref, x_ref, g_ref, w_ref, ycat_ref, y_ref):
        ov = o_ref[...]
        ya = (ov * _rms_scale(ov) * g_ref[...]).astype(BF16)
        ycat = jnp.concatenate([yp_ref[...], ya], axis=1)
        ycat_ref[...] = ycat
        y_ref[...] = x_ref[...] + _dot(ycat, w_ref[...])

    return pl.pallas_call(
        body, name="mix_out_fwd", grid=(t // tm,),
        in_specs=[_rows(tm, aw), _rows(tm, pw), _rows(tm, d), _resident((1, aw)), _resident((pw + aw, d))],
        out_specs=[_rows(tm, pw + aw), _rows(tm, d)],
        out_shape=[jax.ShapeDtypeStruct((t, pw + aw), BF16), jax.ShapeDtypeStruct((t, d), F32)],
        compiler_params=_params("arbitrary"),
    )(o, y_pool, x, gain, w_out)


def _mix_out_bwd(dx, o, gain, w_out):
    t, d = dx.shape
    tm = 1024
    pw, aw = POOL_WIDTH, ATTN_WIDTH

    def body(dx_ref, o_ref, g_ref, w_ref, dxb_ref, dyp_ref, do_ref, dg_ref):
        dxb = dx_ref[...].astype(BF16)
        dxb_ref[...] = dxb
        dyp_ref[...] = _dot_nt(dxb, w_ref[pl.ds(0, pw), :])
        dya = _dot_nt(dxb, w_ref[pl.ds(pw, aw), :])
        do, dgain = _rms_bwd(dya, o_ref[...], g_ref[...])
        do_ref[...] = do

        @pl.when(pl.program_id(0) == 0)
        def _():
            dg_ref[...] = jnp.zeros_like(dg_ref)

        dg_ref[...] += dgain

    return pl.pallas_call(
        body, name="mix_out_bwd", grid=(t // tm,),
        in_specs=[_rows(tm, d), _rows(tm, aw), _resident((1, aw)), _resident((pw + aw, d))],
        out_specs=[_rows(tm, d), _rows(tm, pw), _rows(tm, aw), pl.BlockSpec((1, aw), lambda i: (0, 0))],
        out_shape=[jax.ShapeDtypeStruct((t, d), BF16), jax.ShapeDtypeStruct((t, pw), F32),
                   jax.ShapeDtypeStruct((t, aw), F32), jax.ShapeDtypeStruct((1, aw), F32)],
        compiler_params=_params("arbitrary"),
    )(dx, o, gain, w_out)


def _mix_in_bwd(dpv, dq, dk, dv, df, x, dx_res, gain, w_in_t):
    t, d = x.shape
    tm = 512
    pw, aw = POOL_WIDTH, ATTN_WIDTH

    def body(dpv_ref, dq_ref, dk_ref, dv_ref, df_ref, x_ref, dxr_ref, g_ref, w_ref, dh_ref, dx_ref, dxh_ref, dg_ref):
        dh = jnp.concatenate([dpv_ref[...], dq_ref[...], dk_ref[...], dv_ref[...], df_ref[...]], axis=1)
        dh_ref[...] = dh
        dhm = _dot(dh, w_ref[...])
        dx, dgain = _rms_bwd(dhm, x_ref[...], g_ref[...])
        dx = dxr_ref[...] + dx
        dx_ref[...] = dx
        dxh_ref[...] = (0.5 * dx).astype(BF16)

        @pl.when(pl.program_id(0) == 0)
        def _():
            dg_ref[...] = jnp.zeros_like(dg_ref)

        dg_ref[...] += dgain

    return pl.pallas_call(
        body, name="mix_in_bwd", grid=(t // tm,),
        in_specs=[_rows(tm, pw), _rows(tm, aw), _rows(tm, aw), _rows(tm, aw), _rows(tm, LANES), _rows(tm, d),
                  _rows(tm, d), _resident((1, d)), _resident((MIX_PAD, d))],
        out_specs=[_rows(tm, MIX_PAD), _rows(tm, d), _rows(tm, d), pl.BlockSpec((1, d), lambda i: (0, 0))],
        out_shape=[jax.ShapeDtypeStruct((t, MIX_PAD), BF16), jax.ShapeDtypeStruct((t, d), F32),
                   jax.ShapeDtypeStruct((t, d), BF16), jax.ShapeDtypeStruct((1, d), F32)],
        compiler_params=_params("arbitrary"),
    )(dpv, dq, dk, dv, df, x, dx_res, gain, w_in_t)


MESH_IDS = pl.DeviceIdType.MESH


def _me():
    return lax.axis_index("x"), lax.axis_index("y"), lax.axis_index("c")


def _peer(x, y, c, p):
    px = 1 - x if p & 4 else x
    py = 1 - y if p & 2 else y
    pc = 1 - c if p & 1 else c
    return (px, py, pc), 4 * px + 2 * py + pc


HBM_SPEC = pl.BlockSpec(memory_space=pltpu.HBM)
SEM_SPEC = pl.BlockSpec(memory_space=pltpu.SEMAPHORE)
SPLIT_COPY = pltpu.CompilerParams(has_side_effects=pltpu.SideEffectType.DATAFLOW_SIDE_EFFECTING)
PEERS = N_DEV - 1


def _hbm(a):
    return pltpu.with_memory_space_constraint(a, pltpu.HBM)


def _row_block(ref, dev, rows):
    return ref.at[pl.ds(pl.multiple_of(dev * rows, BF16_ROWS), rows)]


def _copy_ends(gather, src, land, me, peer_id):
    if gather:
        rows = src.shape[0]
        return src, _row_block(land, me, rows), _row_block(land, peer_id, rows), src, _row_block(land, me, rows)
    rows = src.shape[0] // N_DEV
    return (_row_block(src, peer_id, rows), land.at[me], land.at[peer_id], _row_block(src, me, rows), land.at[me])


def _land_shape(gather, s):
    return (N_DEV * s.shape[0], s.shape[1]) if gather else (N_DEV, s.shape[0] // N_DEV, s.shape[1])


SIBLING = 1
SAME_CORE_PEERS = (2, 4, 6)
RELAYS = len(SAME_CORE_PEERS)


def _copies_start(groups, gather, name, after=None, relayed=()):
    flat = [s for g in groups for s in g]
    n, ng = len(flat), len(groups)
    lands = [lax.empty(_land_shape(gather, s), s.dtype) for s in flat]
    n_in = 2 * n + (after is not None)

    def body(*refs):
        ins, lnd = refs[:n], refs[n:2 * n]
        sems = refs[n_in:n_in + 2 * ng]
        token = refs[-1]
        x, y, c = _me()
        me = 4 * x + 2 * y + c
        w = 0
        for gi, g in enumerate(groups):
            for k in range(len(g)):
                for p in ((SIBLING,) + SAME_CORE_PEERS if gi in relayed else range(1, N_DEV)):
                    peer, peer_id = _peer(x, y, c, p)
                    src, dst, _, _, _ = _copy_ends(gather, ins[w], lnd[w], me, peer_id)
                    pltpu.make_async_remote_copy(src, dst, sems[2 * gi].at[k * PEERS + p - 1],
                                                 sems[2 * gi + 1].at[k * PEERS + p - 1], device_id=peer,
                                                 device_id_type=MESH_IDS).start()
                w += 1
        token[...] = jnp.zeros_like(token)

    sem_shapes = []
    for g in groups:
        sem_shapes += [pltpu.SemaphoreType.DMA((len(g) * PEERS,))] * 2
    out = pl.pallas_call(
        body, name=name,
        out_shape=(*sem_shapes, *[pltpu.HBM(s.shape, s.dtype) for s in flat],
                   *[pltpu.HBM(l.shape, l.dtype) for l in lands], jax.ShapeDtypeStruct((8, LANES), F32)),
        in_specs=[HBM_SPEC] * (2 * n) + [pl.BlockSpec(memory_space=pl.ANY)] * (after is not None),
        out_specs=(*[SEM_SPEC] * (2 * ng), *[HBM_SPEC] * (2 * n), pl.BlockSpec(memory_space=pltpu.VMEM)),
        input_output_aliases={i: 2 * ng + i for i in range(2 * n)},
        compiler_params=SPLIT_COPY,
    )(*[_hbm(s) for s in flat], *[_hbm(l) for l in lands], *([after] if after is not None else []))
    sems, thru, token = out[:2 * ng], out[2 * ng:2 * ng + 2 * n], out[-1]
    res, w = [], 0
    for gi, g in enumerate(groups):
        res.append((sems[2 * gi], sems[2 * gi + 1], list(thru[w:w + len(g)]), list(thru[n + w:n + w + len(g)])))
        w += len(g)
    return res, token


def _copies_wait(started, gather, after, name):
    send, recv, srcs, lands = started
    n = len(srcs)
    after = list(after) if isinstance(after, (list, tuple)) else [after]

    own_shapes = [s.shape if gather else (s.shape[0] // N_DEV, s.shape[1]) for s in srcs]

    def body(*refs):
        ins, lnd = refs[:n], refs[n:2 * n]
        send_sems, recv_sems = refs[2 * n], refs[2 * n + 1]
        bounce, in_sems, out_sems = refs[-n - 2:-2], refs[-2], refs[-1]
        x, y, c = _me()
        me = 4 * x + 2 * y + c
        ends = [_copy_ends(gather, ins[w], lnd[w], me, me)[3:] for w in range(n)]
        loads = [pltpu.make_async_copy(ends[w][0], bounce[w], in_sems.at[w]) for w in range(n)]
        stores = [pltpu.make_async_copy(bounce[w], ends[w][1], out_sems.at[w]) for w in range(n)]
        for cp in loads:
            cp.start()
        for w in range(n):
            loads[w].wait()
            stores[w].start()
        for w in range(n):
            for p in range(1, N_DEV):
                peer, peer_id = _peer(x, y, c, p)
                src, _, arrival, _, _ = _copy_ends(gather, ins[w], lnd[w], me, peer_id)
                cp = pltpu.make_async_remote_copy(src, arrival, send_sems.at[w * PEERS + p - 1],
                                                  recv_sems.at[w * PEERS + p - 1], device_id=peer,
                                                  device_id_type=MESH_IDS)
                cp.wait_send()
                cp.wait_recv()
        for cp in stores:
            cp.wait()

    out = pl.pallas_call(
        body, name=name,
        out_shape=(*[pltpu.HBM(s.shape, s.dtype) for s in srcs], *[pltpu.HBM(l.shape, l.dtype) for l in lands]),
        in_specs=[HBM_SPEC] * (2 * n) + [SEM_SPEC, SEM_SPEC] + [pl.BlockSpec(memory_space=pl.ANY)] * len(after),
        out_specs=[HBM_SPEC] * (2 * n),
        input_output_aliases={i: i for i in range(2 * n)},
        scratch_shapes=[*[pltpu.VMEM(shape, s.dtype) for shape, s in zip(own_shapes, srcs)],
                        pltpu.SemaphoreType.DMA((n,)), pltpu.SemaphoreType.DMA((n,))],
        compiler_params=SPLIT_COPY,
    )(*srcs, *lands, send, recv, *after)
    return list(out[n:])


def _relay_to_sibling(started, name, after=None):
    send, recv, srcs, lands = started
    n = len(srcs)
    after = [] if after is None else [after]

    def body(*refs):
        ins, lnd = refs[:n], refs[n:2 * n]
        send_sems, recv_sems = refs[2 * n], refs[2 * n + 1]
        relay_send, relay_recv = refs[2 * n + 2 + len(after)], refs[2 * n + 3 + len(after)]
        x, y, c = _me()
        sibling, _ = _peer(x, y, c, SIBLING)
        for w in range(n):
            rows = ins[w].shape[0]
            for k, p in enumerate(SAME_CORE_PEERS):
                peer, peer_id = _peer(x, y, c, p)
                arrived = _row_block(lnd[w], peer_id, rows)
                first = pltpu.make_async_remote_copy(ins[w], arrived, send_sems.at[w * PEERS + p - 1],
                                                     recv_sems.at[w * PEERS + p - 1], device_id=peer,
                                                     device_id_type=MESH_IDS)
                first.wait_recv()
                pltpu.make_async_remote_copy(arrived, arrived, relay_send.at[w * RELAYS + k],
                                             relay_recv.at[w * RELAYS + k], device_id=sibling,
                                             device_id_type=MESH_IDS).start()
                first.wait_send()

    sems = pltpu.SemaphoreType.DMA((n * RELAYS,))
    out = pl.pallas_call(
        body, name=name,
        out_shape=(sems, sems, *[pltpu.HBM(s.shape, s.dtype) for s in srcs], *[pltpu.HBM(l.shape, l.dtype) for l in lands]),
        in_specs=[HBM_SPEC] * (2 * n) + [SEM_SPEC, SEM_SPEC] + [pl.BlockSpec(memory_space=pl.ANY)] * len(after),
        out_specs=(SEM_SPEC, SEM_SPEC, *[HBM_SPEC] * (2 * n)),
        input_output_aliases={i: 2 + i for i in range(2 * n)},
        compiler_params=SPLIT_COPY,
    )(*srcs, *lands, send, recv, *after)
    return send, recv, out[0], out[1], list(out[2:2 + n]), list(out[2 + n:])


def _relayed_wait(relayed, after, name):
    send, recv, relay_send, relay_recv, srcs, lands = relayed
    n = len(srcs)
    after = list(after) if isinstance(after, (list, tuple)) else [after]

    def body(*refs):
        ins, lnd = refs[:n], refs[n:2 * n]
        send_sems, recv_sems, relay_send_sems, relay_recv_sems = refs[2 * n:2 * n + 4]
        bounce, in_sems, out_sems = refs[-n - 2:-2], refs[-2], refs[-1]
        x, y, c = _me()
        me = 4 * x + 2 * y + c
        sibling, sibling_id = _peer(x, y, c, SIBLING)
        loads = [pltpu.make_async_copy(ins[w], bounce[w], in_sems.at[w]) for w in range(n)]
        stores = [pltpu.make_async_copy(bounce[w], _row_block(lnd[w], me, ins[w].shape[0]), out_sems.at[w])
                  for w in range(n)]
        for cp in loads:
            cp.start()
        for w in range(n):
            loads[w].wait()
            stores[w].start()
        for w in range(n):
            rows = ins[w].shape[0]
            direct = pltpu.make_async_remote_copy(ins[w], _row_block(lnd[w], sibling_id, rows),
                                                  send_sems.at[w * PEERS + SIBLING - 1],
                                                  recv_sems.at[w * PEERS + SIBLING - 1], device_id=sibling,
                                                  device_id_type=MESH_IDS)
            direct.wait_send()
            direct.wait_recv()
            for k, p in enumerate(SAME_CORE_PEERS):
                _, sent_id = _peer(x, y, c, p)
                _, got_id = _peer(x, y, c, p + SIBLING)
                relay = pltpu.make_async_remote_copy(_row_block(lnd[w], sent_id, rows), _row_block(lnd[w], got_id, rows),
                                                     relay_send_sems.at[w * RELAYS + k],
                                                     relay_recv_sems.at[w * RELAYS + k], device_id=sibling,
                                                     device_id_type=MESH_IDS)
                relay.wait_send()
                relay.wait_recv()
        for cp in stores:
            cp.wait()

    out = pl.pallas_call(
        body, name=name,
        out_shape=(*[pltpu.HBM(s.shape, s.dtype) for s in srcs], *[pltpu.HBM(l.shape, l.dtype) for l in lands]),
        in_specs=[HBM_SPEC] * (2 * n) + [SEM_SPEC] * 4 + [pl.BlockSpec(memory_space=pl.ANY)] * len(after),
        out_specs=[HBM_SPEC] * (2 * n),
        input_output_aliases={i: i for i in range(2 * n)},
        scratch_shapes=[*[pltpu.VMEM(s.shape, s.dtype) for s in srcs],
                        pltpu.SemaphoreType.DMA((n,)), pltpu.SemaphoreType.DMA((n,))],
        compiler_params=SPLIT_COPY,
    )(*srcs, *lands, send, recv, relay_send, relay_recv, *after)
    return list(out[n:])


def _adamw_update(w, g, m, v):
    nm = ADAM_B1 * m + (1.0 - ADAM_B1) * g
    nv = ADAM_B2 * v + (1.0 - ADAM_B2) * (g * g)
    m_hat = nm / (1.0 - ADAM_B1 ** ADAM_STEP)
    v_hat = nv / (1.0 - ADAM_B2 ** ADAM_STEP)
    return -ADAM_LR * (m_hat / (jnp.sqrt(v_hat) + ADAM_EPS) + ADAM_WD * w), nm, nv


SUM_ADAMW_COLS = 512


def _sum_adamw(parts, w, m, v, name):
    _, rows, d = parts.shape
    n = w.shape[0]
    tc = SUM_ADAMW_COLS

    def body(p_ref, w_ref, m_ref, v_ref, g_ref, d_ref, nm_ref, nv_ref):
        g = p_ref[0].astype(F32)
        for dev in range(1, N_DEV):
            g = g + p_ref[dev].astype(F32)
        g = g[:n]
        g_ref[...] = g
        d_ref[...], nm_ref[...], nv_ref[...] = _adamw_update(w_ref[...], g, m_ref[...], v_ref[...])

    spec = pl.BlockSpec((n, tc), lambda j: (0, j))
    shape = jax.ShapeDtypeStruct((n, d), F32)
    return pl.pallas_call(
        body, name=name, grid=(d // tc,),
        in_specs=[pl.BlockSpec((N_DEV, rows, tc), lambda j: (0, 0, j)), spec, spec, spec],
        out_specs=[spec] * 4, out_shape=[shape] * 4,
        compiler_params=_params("arbitrary"),
    )(parts, w, m, v)


def _pad_rows(a, rows):
    return jnp.pad(a, ((0, rows - a.shape[0]), (0, 0)))


def _row1(vec, width=D_MODEL):
    return jnp.pad(vec.reshape(1, -1), ((0, 0), (0, width - vec.shape[-1])))


COLUMN_SHARDED = ("ffn1_w_gate", "ffn1_w_up", "w_in", "ffn2_w_gate", "ffn2_w_up")
VEC_NAMES = ("ffn1_norm", "mix_norm", "ffn2_norm", "b_forget", "pool_scale", "q_norm", "k_norm", "out_norm_pool",
             "out_norm_attn")
VEC_ROWS = 16
LOSS_ROW = len(VEC_NAMES)


def _pack_vector_grads(parts, loss_part):
    def body(*refs):
        loss_ref, out_ref = refs[-2], refs[-1]
        out_ref[...] = jnp.zeros_like(out_ref)
        lane = lax.broadcasted_iota(jnp.int32, (1, LANES), 1)
        for i, (name, ref) in enumerate(zip(VEC_NAMES, refs[:-2])):
            val = ref[...]
            if name in ("q_norm", "k_norm"):
                val = val[:, 0:LANES] + val[:, LANES:2 * LANES] + val[:, 2 * LANES:3 * LANES] + val[:, 3 * LANES:]
                val = jnp.where(lane < HEAD_DIM, val + pltpu.roll(val, HEAD_DIM, 1), 0.0)
            out_ref[pl.ds(i, 1), pl.ds(0, val.shape[1])] = val
        out_ref[pl.ds(LOSS_ROW, 1), pl.ds(0, 1)] = loss_ref[...]

    vmem = pl.BlockSpec(memory_space=pltpu.VMEM)
    return pl.pallas_call(
        body, name="pack_vector_grads", in_specs=[vmem] * (len(parts) + 1), out_specs=vmem,
        out_shape=jax.ShapeDtypeStruct((VEC_ROWS, D_MODEL), F32),
    )(*parts, loss_part)


def _small_adamw(vec_all, pool_all, vec_params, pool_params):
    nv = len(vec_params)
    pool_rows = pool_params[0].shape[0]

    def body(*refs):
        vec_ref, pool_ref = refs[0], refs[1]
        ins = refs[2:2 + 3 * nv + 3]
        outs = refs[2 + 3 * nv + 3:-1]
        rows = refs[-1]
        total = vec_ref[pl.ds(0, VEC_ROWS), :]
        for dev in range(1, N_DEV):
            total = total + vec_ref[pl.ds(dev * VEC_ROWS, VEC_ROWS), :]
        rows[...] = total
        outs[4 * nv + 4][...] = rows[pl.ds(LOSS_ROW, 1), pl.ds(0, 1)]
        for i in range(nv):
            w_ref, m_ref, v_ref = ins[3 * i:3 * i + 3]
            g = rows[pl.ds(i, 1), pl.ds(0, w_ref.shape[1])]
            outs[4 * i][...] = g
            outs[4 * i + 1][...], outs[4 * i + 2][...], outs[4 * i + 3][...] = _adamw_update(
                w_ref[...], g, m_ref[...], v_ref[...])
        g = pool_ref[pl.ds(0, pool_rows), :]
        for dev in range(1, N_DEV):
            g = g + pool_ref[pl.ds(dev * pool_rows, pool_rows), :]
        w_ref, m_ref, v_ref = ins[3 * nv:]
        outs[4 * nv][...] = g
        outs[4 * nv + 1][...], outs[4 * nv + 2][...], outs[4 * nv + 3][...] = _adamw_update(
            w_ref[...], g, m_ref[...], v_ref[...])

    vmem = pl.BlockSpec(memory_space=pltpu.VMEM)
    flat = [a for trio in vec_params for a in trio] + list(pool_params)
    out_shape = []
    for trio in list(vec_params) + [pool_params]:
        out_shape += [jax.ShapeDtypeStruct(trio[0].shape, F32)] * 4
    out_shape.append(jax.ShapeDtypeStruct((1, 1), F32))
    return pl.pallas_call(
        body, name="adamw_small", in_specs=[vmem] * (2 + len(flat)), out_specs=[vmem] * len(out_shape),
        out_shape=out_shape, scratch_shapes=[pltpu.VMEM((VEC_ROWS, D_MODEL), F32)],
    )(vec_all, pool_all, *flat)


def kernel(x, ffn1_norm, ffn1_w_gate, ffn1_w_up, ffn1_w_down, mix_norm, w_in, b_forget, pool_w, pool_scale, q_norm, k_norm, out_norm_pool, out_norm_attn, w_out, ffn2_norm, ffn2_w_gate, ffn2_w_up, ffn2_w_down, loss_target, m_ffn1_norm, m_ffn1_w_gate, m_ffn1_w_up, m_ffn1_w_down, m_mix_norm, m_w_in, m_b_forget, m_pool_w, m_pool_scale, m_q_norm, m_k_norm, m_out_norm_pool, m_out_norm_attn, m_w_out, m_ffn2_norm, m_ffn2_w_gate, m_ffn2_w_up, m_ffn2_w_down, v_ffn1_norm, v_ffn1_w_gate, v_ffn1_w_up, v_ffn1_w_down, v_mix_norm, v_w_in, v_b_forget, v_pool_w, v_pool_scale, v_q_norm, v_k_norm, v_out_norm_pool, v_out_norm_attn, v_w_out, v_ffn2_norm, v_ffn2_w_gate, v_ffn2_w_up, v_ffn2_w_down):
    bsz, seq, d = x.shape
    t = bsz * seq
    x0 = x.reshape(t, d)
    target = loss_target.reshape(t, d)
    in_rows = -(-w_in.shape[1] // BF16_ROWS) * BF16_ROWS

    slabs = [s.astype(BF16) for s in (ffn1_w_gate.T, ffn1_w_up.T, ffn1_w_down, _pad_rows(w_in.T, in_rows), w_out,
                                       ffn2_w_gate.T, ffn2_w_up.T, ffn2_w_down)]
    gathers, started = _copies_start([slabs[0:2], slabs[2:3], slabs[3:4], slabs[4:5], slabs[5:8]], True, "gather_start",
                                     relayed=(0, 4))

    g1, gm, g2 = ffn1_norm.reshape(1, d), mix_norm.reshape(1, d), ffn2_norm.reshape(1, d)
    bf_row = _row1(b_forget, LANES)
    gq = jnp.tile(q_norm, N_HEADS).reshape(1, ATTN_WIDTH)
    gk = jnp.tile(k_norm, N_HEADS).reshape(1, ATTN_WIDTH)
    scale_row = pool_scale.reshape(1, POOL_WIDTH)
    gp, ga = out_norm_pool.reshape(1, POOL_WIDTH), out_norm_attn.reshape(1, ATTN_WIDTH)

    wg1, wu1 = _relayed_wait(_relay_to_sibling(gathers[0], "gather_relay_ffn1_up"), started, "gather_wait_ffn1_up")
    h1, sa1, sb1, s1 = _ffn_up(x0, g1, wg1, wu1, "ffn1_up")
    (wd1,) = _copies_wait(gathers[1], True, s1, "gather_wait_ffn1_down")
    (x1,) = _ffn_down(s1, wd1, x0, None, "ffn1_down")
    (win_g,) = _copies_wait(gathers[2], True, x1, "gather_wait_w_in")
    win_t = _repack_rows(win_g, in_rows, w_in.shape[1], N_DEV, "w_in_rows")
    hm, pv, q, k, v, f = _mix_in_fwd(x1, gm, win_t)
    pooled, mixed, y_pool = _pool_fwd(pv, pool_w, scale_row, gp, bsz, seq)
    qp, kp = _attn_prep_fwd(q, k, f, bf_row, gq, gk, bsz, seq)
    o, lse = _flash_fwd(qp, kp, v, bsz, seq)
    relayed_ffn2 = _relay_to_sibling(gathers[4], "gather_relay_ffn2", o)
    (wout,) = _copies_wait(gathers[3], True, [o, relayed_ffn2[4][0]], "gather_wait_w_out")
    ycat, x2 = _mix_out_fwd(o, y_pool, x1, ga, wout)
    wg2, wu2, wd2 = _relayed_wait(relayed_ffn2, x2, "gather_wait_ffn2")
    h2, sa2, sb2, s2 = _ffn_up(x2, g2, wg2, wu2, "ffn2_up")
    dx3, dyh2, loss_part = _ffn_down(s2, wd2, x2, target, "ffn2_down")

    da2, db2, dwg2, dwu2 = _ffn_bwd_act(dyh2, sa2, sb2, h2, wd2, "ffn2_bwd_act")
    (dwd2,) = _wgrad([s2], dyh2, "ffn2_down_wgrad")
    (sent_ffn2,), tok = _copies_start([[dwg2, dwu2, dwd2]], False, "exchange_start_ffn2")
    dx2, dg2 = _ffn_bwd_dx(da2, db2, dx3, x2, g2 + tok[0, 0], wg2, wu2, "ffn2_bwd_dx")
    dx2b, dy_pool, do, dga = _mix_out_bwd(dx2, o, ga, wout)
    (dwout,) = _wgrad([ycat], dx2b, "w_out_wgrad")
    (sent_out,), tok = _copies_start([[dwout]], False, "exchange_start_w_out")
    dqp, dkp, dv = _flash_bwd(qp, kp, v, o, do, lse, bsz, seq)
    dq, dk, df, dgq, dgk, dbf = _attn_prep_bwd(dqp, dkp, q, k, f, bf_row + tok[0, 0], gq, gk, bsz, seq)
    dpv, dpool_w, dscale, dgp = _pool_bwd(dy_pool, mixed, pooled, pool_w, scale_row, gp, bsz, seq)
    dhcat, dx1, dyh1, dgm = _mix_in_bwd(dpv, dq, dk, dv, df, x1, dx2, gm, win_t)
    (dwin,) = _wgrad([dhcat], hm, "w_in_wgrad")
    dwin_blocks = _repack_rows(dwin, w_in.shape[1], in_rows, N_DEV, "w_in_grad_blocks")
    (sent_in,), tok = _copies_start([[dwin_blocks]], False, "exchange_start_w_in")
    (dwd1,) = _wgrad([s1], dyh1, "ffn1_down_wgrad")
    (sent_down1,), tok = _copies_start([[dwd1]], False, "exchange_start_ffn1_down", after=tok)
    da1, db1, dwg1, dwu1 = _ffn_bwd_act(dyh1, sa1, sb1, h1, wd1, "ffn1_bwd_act")
    (sent_up1,), tok = _copies_start([[dwg1, dwu1]], False, "exchange_start_ffn1_up", after=tok)
    dx0, dg1 = _ffn_bwd_dx(da1, db1, dx1, x0, g1 + tok[0, 0], wg1, wu1, "ffn1_bwd_dx")

    pool_rows = POOL_GROUPS * POOL_GROUP_DIM
    packed = _pack_vector_grads([dg1, dgm, dg2, dbf, dscale, dgq, dgk, dgp, dga], loss_part)
    (sent_small,), tok = _copies_start([[packed, dpool_w.reshape(pool_rows, POOL_GROUP_DIM)]], True, "small_grads_start")

    weights = dict(ffn1_norm=ffn1_norm, ffn1_w_gate=ffn1_w_gate, ffn1_w_up=ffn1_w_up, ffn1_w_down=ffn1_w_down,
                   mix_norm=mix_norm, w_in=w_in, b_forget=b_forget, pool_w=pool_w, pool_scale=pool_scale,
                   q_norm=q_norm, k_norm=k_norm, out_norm_pool=out_norm_pool, out_norm_attn=out_norm_attn,
                   w_out=w_out, ffn2_norm=ffn2_norm, ffn2_w_gate=ffn2_w_gate, ffn2_w_up=ffn2_w_up,
                   ffn2_w_down=ffn2_w_down)
    m_in = dict(ffn1_norm=m_ffn1_norm, ffn1_w_gate=m_ffn1_w_gate, ffn1_w_up=m_ffn1_w_up, ffn1_w_down=m_ffn1_w_down,
                mix_norm=m_mix_norm, w_in=m_w_in, b_forget=m_b_forget, pool_w=m_pool_w, pool_scale=m_pool_scale,
                q_norm=m_q_norm, k_norm=m_k_norm, out_norm_pool=m_out_norm_pool, out_norm_attn=m_out_norm_attn,
                w_out=m_w_out, ffn2_norm=m_ffn2_norm, ffn2_w_gate=m_ffn2_w_gate, ffn2_w_up=m_ffn2_w_up,
                ffn2_w_down=m_ffn2_w_down)
    v_in = dict(ffn1_norm=v_ffn1_norm, ffn1_w_gate=v_ffn1_w_gate, ffn1_w_up=v_ffn1_w_up, ffn1_w_down=v_ffn1_w_down,
                mix_norm=v_mix_norm, w_in=v_w_in, b_forget=v_b_forget, pool_w=v_pool_w, pool_scale=v_pool_scale,
                q_norm=v_q_norm, k_norm=v_k_norm, out_norm_pool=v_out_norm_pool, out_norm_attn=v_out_norm_attn,
                w_out=v_w_out, ffn2_norm=v_ffn2_norm, ffn2_w_gate=v_ffn2_w_gate, ffn2_w_up=v_ffn2_w_up,
                ffn2_w_down=v_ffn2_w_down)
    grads, delta, new_m, new_v = {}, {}, {}, {}
    after = [tok]
    plan = ((sent_ffn2, "ffn2", ("ffn2_w_gate", "ffn2_w_up", "ffn2_w_down")), (sent_out, "w_out", ("w_out",)),
            (sent_in, "w_in", ("w_in",)), (sent_down1, "ffn1_down", ("ffn1_w_down",)),
            (sent_up1, "ffn1_up", ("ffn1_w_gate", "ffn1_w_up")))
    for sent, tag, names in plan:
        parts = _copies_wait(sent, False, after, f"exchange_wait_{tag}")
        after = []
        for n, part in zip(names, parts):
            turn = (lambda a: a.T) if n in COLUMN_SHARDED else (lambda a: a)
            done = _sum_adamw(part, turn(weights[n]), turn(m_in[n]), turn(v_in[n]), f"adamw_{n}")
            grads[n], delta[n], new_m[n], new_v[n] = (turn(a) for a in done)
            after.append(done[3])
    vec_all, pool_all = _copies_wait(sent_small, True, after, "small_grads_wait")
    as_row = lambda a: a.reshape(1, -1)
    as_pool = lambda a: a.reshape(pool_rows, POOL_GROUP_DIM)
    small = _small_adamw(vec_all, pool_all, [tuple(as_row(z[n]) for z in (weights, m_in, v_in)) for n in VEC_NAMES],
                         tuple(as_pool(z["pool_w"]) for z in (weights, m_in, v_in)))
    for i, n in enumerate(VEC_NAMES + ("pool_w",)):
        grads[n], delta[n], new_m[n], new_v[n] = (a.reshape(weights[n].shape) for a in small[4 * i:4 * i + 4])
    loss = small[-1].reshape(())

    order = ("ffn1_norm", "ffn1_w_gate", "ffn1_w_up", "ffn1_w_down", "mix_norm", "w_in", "b_forget", "pool_w",
             "pool_scale", "q_norm", "k_norm", "out_norm_pool", "out_norm_attn", "w_out", "ffn2_norm", "ffn2_w_gate",
             "ffn2_w_up", "ffn2_w_down")
    return (loss, dx0.reshape(bsz, seq, d), *[grads[n] for n in order], *[delta[n] for n in order],
            *[new_m[n] for n in order], *[new_v[n] for n in order])
```

```python
import functools

import jax
import jax.numpy as jnp
from jax import lax
from jax.experimental import pallas as pl
from jax.experimental.pallas import tpu as pltpu

F32 = jnp.float32
BF16 = jnp.bfloat16

EPS = 1e-6
D_MODEL = 1024
D_FF = 2816
N_HEADS = 8
HEAD_DIM = 64
POOL_WIDTH = 512
ATTN_WIDTH = 512
POOL_GROUPS = 4
POOL_GROUP_DIM = 128
POOL_WINDOWS = (2, 4, 8, 16)
POOL_HALO = 16
MIX_PAD = POOL_WIDTH + 3 * ATTN_WIDTH + 128
N_DEV = 8
BF16_ROWS = 16
LANES = 128
VMEM_LIMIT = 56 * 1024 * 1024

ADAM_LR = 0.001
ADAM_B1 = 0.9
ADAM_B2 = 0.999
ADAM_EPS = 1e-08
ADAM_WD = 0.01
ADAM_STEP = 10


def _params(*sem):
    return pltpu.CompilerParams(dimension_semantics=sem, vmem_limit_bytes=VMEM_LIMIT)


def _dot(a, b):
    return jnp.dot(a, b, preferred_element_type=F32)


def _dot_nt(a, b):
    return lax.dot_general(a, b, (((1,), (1,)), ((), ())), preferred_element_type=F32)


def _dot_tn(a, b):
    return lax.dot_general(a, b, (((0,), (0,)), ((), ())), preferred_element_type=F32)


def _resident(shape):
    return pl.BlockSpec(shape, lambda *_: (0,) * len(shape), pipeline_mode=pl.Buffered(1))


def _rows(tm, width):
    return pl.BlockSpec((tm, width), lambda i: (i, 0))


def _rms_scale(x):
    return lax.rsqrt(jnp.mean(x * x, axis=-1, keepdims=True) + EPS)


def _rms_bwd(dh, x, gain):
    r = _rms_scale(x)
    n = x * r
    dgain = jnp.sum(dh * n, axis=0, keepdims=True)
    dn = dh * gain
    dx = r * (dn - n * jnp.mean(dn * n, axis=-1, keepdims=True))
    return dx, dgain


def _split3(x):
    hi = x.astype(BF16)
    r1 = x - hi.astype(F32)
    mid = r1.astype(BF16)
    lo = (r1 - mid.astype(F32)).astype(BF16)
    return hi, mid, lo


def _split2(x):
    hi = x.astype(BF16)
    return hi, (x - hi.astype(F32)).astype(BF16)


FF_CHUNK = 256


def _swiglu_parts(a, b):
    sig = jax.nn.sigmoid(a)
    silu = a * sig
    return (b * (sig + silu * (1.0 - sig))).astype(BF16), silu.astype(BF16), (silu * b).astype(BF16)


def _ffn_up(x, gain, wg_t, wu_t, name):
    t, d = x.shape
    f = wg_t.shape[0]
    tm = 512

    def body(x_ref, g_ref, wg_ref, wu_ref, h_ref, sa_ref, sb_ref, s_ref):
        xv = x_ref[...]
        h = (xv * _rms_scale(xv) * g_ref[...]).astype(BF16)
        h_ref[...] = h
        for c in range(f // FF_CHUNK):
            sl = pl.ds(c * FF_CHUNK, FF_CHUNK)
            sa_ref[:, sl], sb_ref[:, sl], s_ref[:, sl] = _swiglu_parts(_dot_nt(h, wg_ref[sl, :]), _dot_nt(h, wu_ref[sl, :]))

    wide = jax.ShapeDtypeStruct((t, f), BF16)
    return pl.pallas_call(
        body, name=name, grid=(t // tm,),
        in_specs=[_rows(tm, d), _resident((1, d)), _resident((f, d)), _resident((f, d))],
        out_specs=[_rows(tm, d), _rows(tm, f), _rows(tm, f), _rows(tm, f)],
        out_shape=[jax.ShapeDtypeStruct((t, d), BF16), wide, wide, wide],
        compiler_params=_params("arbitrary"),
    )(x, gain, wg_t, wu_t)


def _ffn_down(s, wd, x, target, name):
    t, d = x.shape
    f = wd.shape[0]
    tm = 512
    with_loss = target is not None

    def body(*refs):
        if with_loss:
            s_ref, w_ref, x_ref, t_ref, dy_ref, dyh_ref, loss_ref = refs
        else:
            s_ref, w_ref, x_ref, y_ref = refs
        y = x_ref[...] + 0.5 * _dot(s_ref[...], w_ref[...])
        if with_loss:
            e = y - t_ref[...]
            dy = e * (1.0 / d)
            dy_ref[...] = dy
            dyh_ref[...] = (0.5 * dy).astype(BF16)

            @pl.when(pl.program_id(0) == 0)
            def _():
                loss_ref[...] = jnp.zeros_like(loss_ref)

            part = jnp.sum(jnp.sum(e * e, axis=0, keepdims=True), axis=1, keepdims=True)
            loss_ref[...] += part * (0.5 / d)
        else:
            y_ref[...] = y

    in_specs = [_rows(tm, f), _resident((f, d)), _rows(tm, d)]
    args = [s, wd, x]
    if with_loss:
        in_specs.append(_rows(tm, d))
        args.append(target)
        out_shape = [jax.ShapeDtypeStruct((t, d), F32), jax.ShapeDtypeStruct((t, d), BF16),
                     jax.ShapeDtypeStruct((1, 1), F32)]
        out_specs = [_rows(tm, d), _rows(tm, d), pl.BlockSpec((1, 1), lambda i: (0, 0))]
    else:
        out_shape = [jax.ShapeDtypeStruct((t, d), F32)]
        out_specs = [_rows(tm, d)]
    return pl.pallas_call(
        body, name=name, grid=(t // tm,), in_specs=in_specs, out_specs=out_specs, out_shape=out_shape,
        compiler_params=_params("arbitrary"),
    )(*args)


def _ffn_bwd_act(dyh, sa, sb, h, wd, name):
    t, d = dyh.shape
    f = wd.shape[0]
    tn = f // 2
    tk = 512
    nk = t // tk

    def body(dy_ref, sa_ref, sb_ref, h_ref, wd_ref, da_ref, db_ref, dwg_ref, dwu_ref, acc_g, acc_u):
        k = pl.program_id(1)

        @pl.when(k == 0)
        def _():
            acc_g[...] = jnp.zeros_like(acc_g)
            acc_u[...] = jnp.zeros_like(acc_u)

        ds = _dot_nt(dy_ref[...], wd_ref[...])
        da = (ds * sa_ref[...].astype(F32)).astype(BF16)
        db = (ds * sb_ref[...].astype(F32)).astype(BF16)
        da_ref[...] = da
        db_ref[...] = db
        hv = h_ref[...]
        acc_g[...] += _dot_tn(da, hv)
        acc_u[...] += _dot_tn(db, hv)

        @pl.when(k == nk - 1)
        def _():
            dwg_ref[...] = acc_g[...].astype(BF16)
            dwu_ref[...] = acc_u[...].astype(BF16)

    tokens = pl.BlockSpec((tk, d), lambda j, k: (k, 0))
    wide = pl.BlockSpec((tk, tn), lambda j, k: (k, j))
    weight = pl.BlockSpec((tn, d), lambda j, k: (j, 0))
    return pl.pallas_call(
        body, name=name, grid=(f // tn, nk),
        in_specs=[tokens, wide, wide, tokens, weight],
        out_specs=[wide, wide, weight, weight],
        out_shape=[jax.ShapeDtypeStruct((t, f), BF16)] * 2 + [jax.ShapeDtypeStruct((f, d), BF16)] * 2,
        scratch_shapes=[pltpu.VMEM((tn, d), F32)] * 2,
        compiler_params=_params("arbitrary", "arbitrary"),
    )(dyh, sa, sb, h, wd)


def _ffn_bwd_dx(da, db, dy, x, gain, wg_t, wu_t, name):
    t, d = x.shape
    f = wg_t.shape[0]
    tm = 512

    def body(da_ref, db_ref, dy_ref, x_ref, g_ref, wg_ref, wu_ref, dx_ref, dg_ref):
        dh = _dot(da_ref[...], wg_ref[...]) + _dot(db_ref[...], wu_ref[...])
        dx, dgain = _rms_bwd(dh, x_ref[...], g_ref[...])
        dx_ref[...] = dy_ref[...] + dx

        @pl.when(pl.program_id(0) == 0)
        def _():
            dg_ref[...] = jnp.zeros_like(dg_ref)

        dg_ref[...] += dgain

    return pl.pallas_call(
        body, name=name, grid=(t // tm,),
        in_specs=[_rows(tm, f), _rows(tm, f), _rows(tm, d), _rows(tm, d), _resident((1, d)), _resident((f, d)),
                  _resident((f, d))],
        out_specs=[_rows(tm, d), pl.BlockSpec((1, d), lambda i: (0, 0))],
        out_shape=[jax.ShapeDtypeStruct((t, d), F32), jax.ShapeDtypeStruct((1, d), F32)],
        compiler_params=_params("arbitrary"),
    )(da, db, dy, x, gain, wg_t, wu_t)


def _wgrad(lhs, b, name):
    t, n = lhs[0].shape
    d = b.shape[1]
    m = len(lhs)
    tn = n // 2 if n * d * m > (4 << 20) else n
    tk = 1024
    nk = t // tk

    def body(*refs):
        a_refs, b_ref, o_refs, accs = refs[:m], refs[m], refs[m + 1:2 * m + 1], refs[2 * m + 1:]
        k = pl.program_id(1)

        @pl.when(k == 0)
        def _():
            for acc in accs:
                acc[...] = jnp.zeros_like(acc)

        bv = b_ref[...]
        for a_ref, acc in zip(a_refs, accs):
            acc[...] += _dot_tn(a_ref[...], bv)

        @pl.when(k == nk - 1)
        def _():
            for o_ref, acc in zip(o_refs, accs):
                o_ref[...] = acc[...].astype(BF16)

    return pl.pallas_call(
        body, name=name, grid=(n // tn, nk),
        in_specs=[pl.BlockSpec((tk, tn), lambda j, k: (k, j))] * m + [pl.BlockSpec((tk, d), lambda j, k: (k, 0))],
        out_specs=[pl.BlockSpec((tn, d), lambda j, k: (j, 0))] * m,
        out_shape=[jax.ShapeDtypeStruct((n, d), BF16)] * m,
        scratch_shapes=[pltpu.VMEM((tn, d), F32)] * m,
        compiler_params=_params("arbitrary", "arbitrary"),
    )(*lhs, b)


def _repack_rows(a, rows_in, rows_out, blocks, name):
    total, d = a.shape
    real = min(rows_in, rows_out)

    def body(a_ref, o_ref, wide_in, wide_out):
        wide_in[...] = a_ref[...].astype(F32)
        wide_out[...] = jnp.zeros_like(wide_out)
        for j in range(blocks):
            wide_out[pl.ds(j * rows_out, real), :] = wide_in[pl.ds(j * rows_in, real), :]
        o_ref[...] = wide_out[...].astype(BF16)

    full = pl.BlockSpec((total, d), lambda i: (0, 0))
    return pl.pallas_call(
        body, name=name, grid=(1,), in_specs=[full], out_specs=full, out_shape=jax.ShapeDtypeStruct((total, d), BF16),
        scratch_shapes=[pltpu.VMEM((total, d), F32)] * 2,
        compiler_params=_params("arbitrary"),
    )(a)


def _mix_in_fwd(x, gain, w_in_t):
    t, d = x.shape
    tm = 1024
    pw, aw = POOL_WIDTH, ATTN_WIDTH

    def body(x_ref, g_ref, w_ref, hm_ref, pv_ref, q_ref, k_ref, v_ref, f_ref):
        xv = x_ref[...]
        hm = (xv * _rms_scale(xv) * g_ref[...]).astype(BF16)
        hm_ref[...] = hm
        pv_ref[...] = _dot_nt(hm, w_ref[pl.ds(0, pw), :])
        q_ref[...] = _dot_nt(hm, w_ref[pl.ds(pw, aw), :])
        k_ref[...] = _dot_nt(hm, w_ref[pl.ds(pw + aw, aw), :])
        v_ref[...] = _dot_nt(hm, w_ref[pl.ds(pw + 2 * aw, aw), :]).astype(BF16)
        f_ref[...] = _dot_nt(hm, w_ref[pl.ds(pw + 3 * aw, LANES), :])

    return pl.pallas_call(
        body, name="mix_in_fwd", grid=(t // tm,),
        in_specs=[_rows(tm, d), _resident((1, d)), _resident((MIX_PAD, d))],
        out_specs=[_rows(tm, d), _rows(tm, pw), _rows(tm, aw), _rows(tm, aw), _rows(tm, aw), _rows(tm, LANES)],
        out_shape=[jax.ShapeDtypeStruct((t, d), BF16), jax.ShapeDtypeStruct((t, pw), F32),
                   jax.ShapeDtypeStruct((t, aw), F32), jax.ShapeDtypeStruct((t, aw), F32),
                   jax.ShapeDtypeStruct((t, aw), BF16), jax.ShapeDtypeStruct((t, LANES), F32)],
        compiler_params=_params("arbitrary"),
    )(x, gain, w_in_t)


def _pool_fwd(pv, pool_w, pool_scale, gain, bsz, seq):
    ts = 512
    ns = seq // ts
    pw = POOL_WIDTH

    def body(pv_ref, w_ref, sc_ref, g_ref, pooled_ref, mixed_ref, y_ref, ext):
        s = pl.program_id(1)

        @pl.when(s == 0)
        def _():
            ext[pl.ds(0, POOL_HALO), :] = jnp.zeros((POOL_HALO, pw), F32)

        p = pv_ref[...]
        ext[pl.ds(POOL_HALO, ts), :] = p
        pos = s * ts + lax.broadcasted_iota(jnp.int32, (ts, 1), 0)
        parts = []
        for g, w in enumerate(POOL_WINDOWS):
            lanes = pl.ds(g * POOL_GROUP_DIM, POOL_GROUP_DIM)
            win = ext[pl.ds(POOL_HALO, ts), lanes]
            for i in range(1, w):
                win = win + ext[pl.ds(POOL_HALO - i, ts), lanes]
            cnt = jnp.minimum(pos + 1, w).astype(F32)
            pooled = (win / cnt - ext[pl.ds(POOL_HALO, ts), lanes]).astype(BF16)
            pooled_ref[:, lanes] = pooled
            parts.append(_dot(pooled, w_ref[g].astype(BF16)))
        mixed = jnp.concatenate(parts, axis=1)
        mixed_ref[...] = mixed
        pm = mixed * sc_ref[...]
        y_ref[...] = (pm * _rms_scale(pm) * g_ref[...]).astype(BF16)
        ext[pl.ds(0, POOL_HALO), :] = p[ts - POOL_HALO:, :]

    blk = pl.BlockSpec((ts, pw), lambda b, s: (b * ns + s, 0))
    t = bsz * seq
    return pl.pallas_call(
        body, name="pool_fwd", grid=(bsz, ns),
        in_specs=[blk, pl.BlockSpec((POOL_GROUPS, POOL_GROUP_DIM, POOL_GROUP_DIM), lambda b, s: (0, 0, 0)),
                  pl.BlockSpec((1, pw), lambda b, s: (0, 0)), pl.BlockSpec((1, pw), lambda b, s: (0, 0))],
        out_specs=[blk, blk, blk],
        out_shape=[jax.ShapeDtypeStruct((t, pw), BF16), jax.ShapeDtypeStruct((t, pw), F32),
                   jax.ShapeDtypeStruct((t, pw), BF16)],
        scratch_shapes=[pltpu.VMEM((POOL_HALO + ts, pw), F32)],
        compiler_params=_params("arbitrary", "arbitrary"),
    )(pv, pool_w, pool_scale, gain)


def _pool_bwd(dy, mixed, pooled, pool_w, pool_scale, gain, bsz, seq):
    ts = 512
    ns = seq // ts
    pw = POOL_WIDTH

    def body(dy_ref, mixed_ref, pooled_ref, w_ref, sc_ref, g_ref, dpv_ref, dw_ref, dsc_ref, dg_ref, ext):
        b = pl.program_id(0)
        sr = pl.program_id(1)
        s = ns - 1 - sr

        @pl.when(jnp.logical_and(b == 0, sr == 0))
        def _():
            dw_ref[...] = jnp.zeros_like(dw_ref)
            dsc_ref[...] = jnp.zeros_like(dsc_ref)
            dg_ref[...] = jnp.zeros_like(dg_ref)

        @pl.when(sr == 0)
        def _():
            ext[pl.ds(ts, POOL_HALO), :] = jnp.zeros((POOL_HALO, pw), F32)

        mixed = mixed_ref[...]
        sc = sc_ref[...]
        dpm, dgain = _rms_bwd(dy_ref[...], mixed * sc, g_ref[...])
        dg_ref[...] += dgain
        dsc_ref[...] += jnp.sum(dpm * mixed, axis=0, keepdims=True)
        dmixed = (dpm * sc).astype(BF16)
        pos = s * ts + lax.broadcasted_iota(jnp.int32, (ts, 1), 0)
        dpooled = []
        for g, w in enumerate(POOL_WINDOWS):
            lanes = pl.ds(g * POOL_GROUP_DIM, POOL_GROUP_DIM)
            dm = dmixed[:, g * POOL_GROUP_DIM:(g + 1) * POOL_GROUP_DIM]
            dw_ref[g] += _dot_tn(pooled_ref[:, lanes], dm)
            dp = _dot_nt(dm, w_ref[g].astype(BF16))
            dpooled.append(dp)
            cnt = jnp.minimum(pos + 1, w).astype(F32)
            ext[pl.ds(0, ts), lanes] = dp / cnt
        for g, w in enumerate(POOL_WINDOWS):
            lanes = pl.ds(g * POOL_GROUP_DIM, POOL_GROUP_DIM)
            win = ext[pl.ds(0, ts), lanes]
            for i in range(1, w):
                win = win + ext[pl.ds(i, ts), lanes]
            dpv_ref[:, lanes] = (win - dpooled[g]).astype(BF16)
        head = ext[pl.ds(0, POOL_HALO), :]
        ext[pl.ds(ts, POOL_HALO), :] = head

    blk = pl.BlockSpec((ts, pw), lambda b, s: (b * ns + (ns - 1 - s), 0))
    vec = pl.BlockSpec((1, pw), lambda b, s: (0, 0))
    wspec = pl.BlockSpec((POOL_GROUPS, POOL_GROUP_DIM, POOL_GROUP_DIM), lambda b, s: (0, 0, 0))
    t = bsz * seq
    return pl.pallas_call(
        body, name="pool_bwd", grid=(bsz, ns),
        in_specs=[blk, blk, blk, wspec, vec, vec],
        out_specs=[blk, wspec, vec, vec],
        out_shape=[jax.ShapeDtypeStruct((t, pw), BF16),
                   jax.ShapeDtypeStruct((POOL_GROUPS, POOL_GROUP_DIM, POOL_GROUP_DIM), F32),
                   jax.ShapeDtypeStruct((1, pw), F32), jax.ShapeDtypeStruct((1, pw), F32)],
        scratch_shapes=[pltpu.VMEM((ts + POOL_HALO, pw), F32)],
        compiler_params=_params("arbitrary", "arbitrary"),
    )(dy, mixed, pooled, pool_w, pool_scale, gain)


AUX_ONE = 64
AUX_F = 67

ATTN_PREP_ROWS = 512


def _seg_ones(width, seg):
    r = lax.broadcasted_iota(jnp.int32, (width, width), 0) // seg
    c = lax.broadcasted_iota(jnp.int32, (width, width), 1) // seg
    return (r == c).astype(BF16)


def _tri_ones(n, lower):
    r = lax.broadcasted_iota(jnp.int32, (n, n), 0)
    c = lax.broadcasted_iota(jnp.int32, (n, n), 1)
    return ((r >= c) if lower else (r <= c)).astype(BF16)


def _place_pieces(first_lane):
    r = lax.broadcasted_iota(jnp.int32, (3 * LANES, N_HEADS * LANES), 0)
    c = lax.broadcasted_iota(jnp.int32, (3 * LANES, N_HEADS * LANES), 1)
    piece, head = r // LANES, r % LANES
    return jnp.logical_and(head < N_HEADS, c == head * LANES + first_lane + piece).astype(BF16)


def _head_sums(x, seg_ones):
    hi, lo = _split2(x)
    return _dot(hi, seg_ones) + _dot(lo, seg_ones)


def _log_sigmoid(x):
    return jnp.minimum(x, 0.0) - jnp.log(1.0 + jnp.exp(-jnp.abs(x)))


def _attn_prep_fwd(q, k, f, b_forget, q_gain, k_gain, bsz, seq):
    ts = ATTN_PREP_ROWS
    ns = seq // ts
    aw = ATTN_WIDTH
    t = bsz * seq
    seg = _seg_ones(aw, HEAD_DIM)
    tri = _tri_ones(ts, True)

    def body(q_ref, k_ref, f_ref, bf_ref, gq_ref, gk_ref, seg_ref, tri_ref, pq_ref, pk_ref, qp_ref, kp_ref, carry):
        s = pl.program_id(1)

        @pl.when(s == 0)
        def _():
            carry[...] = jnp.zeros_like(carry)

        logf = _log_sigmoid(f_ref[...] + bf_ref[...])
        hi, mid, lo = _split3(logf)
        tri_v = tri_ref[...]
        fc = _dot(tri_v, hi) + _dot(tri_v, mid) + _dot(tri_v, lo) + carry[pl.ds(0, 1), :]
        carry[pl.ds(0, 1), :] = fc[ts - 1:, :]
        pcs = jnp.concatenate(_split3(fc), axis=1)
        lane = lax.broadcasted_iota(jnp.int32, (1, LANES), 1)
        ones_q = jnp.logical_and(lane >= AUX_ONE, lane < AUX_ONE + 3).astype(F32)
        ones_k = jnp.logical_and(lane >= AUX_F, lane < AUX_F + 3).astype(F32)
        seg_v = seg_ref[...]

        def build(x_ref, g_ref, scale, out_ref, ones, place_ref, f_sign):
            xv = x_ref[...]
            r = lax.rsqrt(_head_sums(xv * xv, seg_v) * (1.0 / HEAD_DIM) + EPS)
            xn = xv * r * g_ref[...] * scale
            aux = _dot(pcs, place_ref[...]) * f_sign
            for h in range(N_HEADS):
                pair = xn[:, (h // 2) * LANES:(h // 2 + 1) * LANES]
                feat = pair if h % 2 == 0 else pltpu.roll(pair, HEAD_DIM, 1)
                aux_h = aux[:, h * LANES:(h + 1) * LANES] + ones
                out_ref[:, h * LANES:(h + 1) * LANES] = jnp.where(lane < HEAD_DIM, feat, aux_h).astype(BF16)

        build(q_ref, gq_ref, 0.125, qp_ref, ones_q, pq_ref, 1.0)
        build(k_ref, gk_ref, 1.0, kp_ref, ones_k, pk_ref, -1.0)

    blk = pl.BlockSpec((ts, aw), lambda b, s: (b * ns + s, 0))
    fblk = pl.BlockSpec((ts, LANES), lambda b, s: (b * ns + s, 0))
    oblk = pl.BlockSpec((ts, N_HEADS * LANES), lambda b, s: (b * ns + s, 0))
    const = lambda shape: pl.BlockSpec(shape, lambda b, s: (0, 0))
    return pl.pallas_call(
        body, name="attn_prep_fwd", grid=(bsz, ns),
        in_specs=[blk, blk, fblk, const((1, LANES)), const((1, aw)), const((1, aw)), const((aw, aw)), const((ts, ts)),
                  const((3 * LANES, N_HEADS * LANES)), const((3 * LANES, N_HEADS * LANES))],
        out_specs=[oblk, oblk],
        out_shape=[jax.ShapeDtypeStruct((t, N_HEADS * LANES), BF16)] * 2,
        scratch_shapes=[pltpu.VMEM((8, LANES), F32)],
        compiler_params=_params("arbitrary", "arbitrary"),
    )(q, k, f, b_forget, q_gain, k_gain, seg, tri, _place_pieces(AUX_F), _place_pieces(AUX_ONE))


def _attn_prep_bwd(dqp, dkp, q, k, f, b_forget, q_gain, k_gain, bsz, seq):
    ts = ATTN_PREP_ROWS
    ns = seq // ts
    aw = ATTN_WIDTH
    t = bsz * seq
    seg = _seg_ones(aw, HEAD_DIM)
    tri = _tri_ones(ts, False)

    def body(dqp_ref, dkp_ref, q_ref, k_ref, f_ref, bf_ref, gq_ref, gk_ref, seg_ref, tri_ref,
             dq_ref, dk_ref, df_ref, dgq_ref, dgk_ref, dbf_ref, carry):
        b = pl.program_id(0)
        sr = pl.program_id(1)

        @pl.when(jnp.logical_and(b == 0, sr == 0))
        def _():
            dgq_ref[...] = jnp.zeros_like(dgq_ref)
            dgk_ref[...] = jnp.zeros_like(dgk_ref)
            dbf_ref[...] = jnp.zeros_like(dbf_ref)

        @pl.when(sr == 0)
        def _():
            carry[...] = jnp.zeros_like(carry)

        lane = lax.broadcasted_iota(jnp.int32, (1, LANES), 1)
        seg_v = seg_ref[...]

        def norm_bwd(dp_ref, x_ref, g_ref, scale, dx_ref, dgain_ref):
            parts = []
            for j in range(N_HEADS // 2):
                even = dp_ref[:, (2 * j) * LANES:(2 * j + 1) * LANES]
                odd = dp_ref[:, (2 * j + 1) * LANES:(2 * j + 2) * LANES]
                parts.append(jnp.where(lane < HEAD_DIM, even, pltpu.roll(odd, HEAD_DIM, 1)))
            dxn = jnp.concatenate(parts, axis=1) * scale
            xv = x_ref[...]
            r = lax.rsqrt(_head_sums(xv * xv, seg_v) * (1.0 / HEAD_DIM) + EPS)
            n = xv * r
            dgain_ref[...] += jnp.sum(dxn * n, axis=0, keepdims=True)
            dn = dxn * g_ref[...]
            m = _head_sums(dn * n, seg_v) * (1.0 / HEAD_DIM)
            dx_ref[...] = (r * (dn - n * m)).astype(BF16)

        norm_bwd(dqp_ref, q_ref, gq_ref, 0.125, dq_ref, dgq_ref)
        norm_bwd(dkp_ref, k_ref, gk_ref, 1.0, dk_ref, dgk_ref)

        dfc = jnp.zeros((ts, LANES), F32)
        for h in range(N_HEADS):
            cols = pl.ds(h * LANES, LANES)
            both = jnp.where(lane == AUX_F, dqp_ref[:, cols], 0.0) - jnp.where(lane == AUX_ONE, dkp_ref[:, cols], 0.0)
            dfc = jnp.where(lane == h, jnp.sum(both, axis=1, keepdims=True), dfc)
        hi, mid, lo = _split3(dfc)
        tri_v = tri_ref[...]
        dlogf = _dot(tri_v, hi) + _dot(tri_v, mid) + _dot(tri_v, lo) + carry[pl.ds(0, 1), :]
        carry[pl.ds(0, 1), :] = dlogf[0:1, :]
        df = jnp.where(lane < N_HEADS, dlogf * jax.nn.sigmoid(-(f_ref[...] + bf_ref[...])), 0.0)
        df_ref[...] = df.astype(BF16)
        dbf_ref[...] += jnp.sum(df, axis=0, keepdims=True)

    rev = lambda b, s: (b * ns + (ns - 1 - s), 0)
    blk = pl.BlockSpec((ts, aw), rev)
    fblk = pl.BlockSpec((ts, LANES), rev)
    pblk = pl.BlockSpec((ts, N_HEADS * LANES), rev)
    const = lambda shape: pl.BlockSpec(shape, lambda b, s: (0, 0))
    return pl.pallas_call(
        body, name="attn_prep_bwd", grid=(bsz, ns),
        in_specs=[pblk, pblk, blk, blk, fblk, const((1, LANES)), const((1, aw)), const((1, aw)), const((aw, aw)),
                  const((ts, ts))],
        out_specs=[blk, blk, fblk, const((1, aw)), const((1, aw)), const((1, LANES))],
        out_shape=[jax.ShapeDtypeStruct((t, aw), BF16), jax.ShapeDtypeStruct((t, aw), BF16),
                   jax.ShapeDtypeStruct((t, LANES), BF16), jax.ShapeDtypeStruct((1, aw), F32),
                   jax.ShapeDtypeStruct((1, aw), F32), jax.ShapeDtypeStruct((1, LANES), F32)],
        scratch_shapes=[pltpu.VMEM((8, LANES), F32)],
        compiler_params=_params("arbitrary", "arbitrary"),
    )(dqp, dkp, q, k, f, b_forget, q_gain, k_gain, seg, tri)


ATTN_BLOCK = 1024
HEAD_PAIRS = N_HEADS // 2


def _flash_fwd(qp, kp, v, bsz, seq):
    tq = ATTN_BLOCK
    half = tq // 2
    nq = seq // tq
    t = bsz * seq

    def body(q_ref, k_ref, v_ref, o_ref, lse_ref, m_sc, l_sc, acc_sc):
        i = pl.program_id(2)
        m_sc[...] = jnp.full(m_sc.shape, -jnp.inf, F32)
        l_sc[...] = jnp.zeros_like(l_sc)
        acc_sc[...] = jnp.zeros_like(acc_sc)
        lane = lax.broadcasted_iota(jnp.int32, (1, LANES), 1)
        low = lane < HEAD_DIM

        def tile(q0, qn, k_start, kn, k0=None):
            qs = pl.ds(q0, qn)
            ks = pl.ds(k_start, kn)
            vv = v_ref[ks, :]
            for h in range(2):
                mine = low if h == 0 else jnp.logical_not(low)
                cols = pl.ds(h * LANES, LANES)
                s = _dot_nt(q_ref[qs, cols], k_ref[ks, cols])
                if k0 is not None:
                    row = lax.broadcasted_iota(jnp.int32, (qn, kn), 0) + q0
                    col = lax.broadcasted_iota(jnp.int32, (qn, kn), 1) + k0
                    s = jnp.where(row >= col, s, -jnp.inf)
                m_prev = m_sc[h, qs, :]
                m_new = jnp.maximum(m_prev, jnp.max(s, axis=1, keepdims=True))
                p = jnp.exp(s - jnp.tile(m_new, (1, kn // LANES)))
                alpha = jnp.exp(m_prev - m_new)
                l_sc[h, qs, :] = alpha * l_sc[h, qs, :] + jnp.sum(p, axis=1, keepdims=True)
                m_sc[h, qs, :] = m_new
                pv = _dot(p.astype(BF16), jnp.where(mine, vv, jnp.zeros_like(vv)))
                acc_sc[qs, :] = acc_sc[qs, :] * jnp.where(mine, alpha, 1.0) + pv

        def below_diagonal(j, carry):
            tile(0, tq, pl.multiple_of(j * tq, tq), tq)
            return carry

        lax.fori_loop(0, i, below_diagonal, 0)
        diagonal = pl.multiple_of(i * tq, tq)
        tile(0, tq, diagonal, half, k0=0)
        tile(half, half, diagonal + half, half, k0=half)
        l = jnp.where(low, l_sc[0], l_sc[1])
        m = jnp.where(low, m_sc[0], m_sc[1])
        o_ref[...] = acc_sc[...] / l
        lse_ref[...] = m + jnp.log(l)

    qspec = pl.BlockSpec((tq, 2 * LANES), lambda b, hp, i: (b * nq + i, hp))
    kspec = pl.BlockSpec((seq, 2 * LANES), lambda b, hp, i: (b, hp))
    vspec = pl.BlockSpec((seq, LANES), lambda b, hp, i: (b, hp))
    ospec = pl.BlockSpec((tq, LANES), lambda b, hp, i: (b * nq + i, hp))
    return pl.pallas_call(
        body, name="flash_fwd", grid=(bsz, HEAD_PAIRS, nq),
        in_specs=[qspec, kspec, vspec], out_specs=[ospec, ospec],
        out_shape=[jax.ShapeDtypeStruct((t, ATTN_WIDTH), F32), jax.ShapeDtypeStruct((t, ATTN_WIDTH), F32)],
        scratch_shapes=[pltpu.VMEM((2, tq, LANES), F32), pltpu.VMEM((2, tq, LANES), F32), pltpu.VMEM((tq, LANES), F32)],
        compiler_params=_params("arbitrary", "arbitrary", "arbitrary"),
    )(qp, kp, v)


def _flash_bwd(qp, kp, v, o, do, lse, bsz, seq):
    tq = ATTN_BLOCK
    half = tq // 2
    nq = seq // tq
    t = bsz * seq

    def body(q_ref, k_ref, v_ref, o_ref, do_ref, lse_ref, dq_ref, dk_ref, dv_ref, dk_acc, dv_acc):
        j = pl.program_id(2)

        @pl.when(j == 0)
        def _():
            dq_ref[...] = jnp.zeros_like(dq_ref)

        dk_acc[...] = jnp.zeros_like(dk_acc)
        dv_acc[...] = jnp.zeros_like(dv_acc)
        lane = lax.broadcasted_iota(jnp.int32, (1, LANES), 1)
        low = lane < HEAD_DIM

        def tile(q_start, qn, k0, kn, q0=None):
            rows = pl.ds(q_start, qn)
            ks = pl.ds(k0, kn)
            dov = do_ref[rows, :]
            dd = dov * o_ref[rows, :]
            dob = dov.astype(BF16)
            vv = v_ref[ks, :]
            lse_v = lse_ref[rows, :]
            for h in range(2):
                mine = low if h == 0 else jnp.logical_not(low)
                cols = pl.ds(h * LANES, LANES)
                qh = q_ref[rows, cols]
                kh = k_ref[ks, cols]
                s = _dot_nt(qh, kh)
                lse_h = jnp.where(mine, lse_v, pltpu.roll(lse_v, HEAD_DIM, 1))
                p = jnp.exp(s - jnp.tile(lse_h, (1, kn // LANES)))
                if q0 is not None:
                    row = lax.broadcasted_iota(jnp.int32, (qn, kn), 0) + q0
                    col = lax.broadcasted_iota(jnp.int32, (qn, kn), 1) + k0
                    p = jnp.where(row >= col, p, 0.0)
                delta = jnp.sum(jnp.where(mine, dd, 0.0), axis=1, keepdims=True)
                dp = _dot_nt(dob, jnp.where(mine, vv, jnp.zeros_like(vv)))
                ds = (p * (dp - delta)).astype(BF16)
                dv_acc[ks, :] += jnp.where(mine, _dot_tn(p.astype(BF16), dob), 0.0)
                dk_acc[ks, cols] += _dot_tn(ds, qh)
                dq_ref[rows, cols] += _dot(ds, kh)

        def above_diagonal(i, carry):
            tile(pl.multiple_of(i * tq, tq), tq, 0, tq)
            return carry

        diagonal = pl.multiple_of(j * tq, tq)
        tile(diagonal, tq, 0, half, q0=0)
        tile(diagonal + half, half, half, half, q0=half)
        lax.fori_loop(j + 1, nq, above_diagonal, 0)
        dk_ref[...] = dk_acc[...]
        dv_ref[...] = dv_acc[...].astype(BF16)

    qspec = pl.BlockSpec((seq, 2 * LANES), lambda b, hp, j: (b, hp))
    kspec = pl.BlockSpec((tq, 2 * LANES), lambda b, hp, j: (b * nq + j, hp))
    vspec = pl.BlockSpec((tq, LANES), lambda b, hp, j: (b * nq + j, hp))
    ospec = pl.BlockSpec((seq, LANES), lambda b, hp, j: (b, hp))
    return pl.pallas_call(
        body, name="flash_bwd", grid=(bsz, HEAD_PAIRS, nq),
        in_specs=[qspec, kspec, vspec, ospec, ospec, ospec], out_specs=[qspec, kspec, vspec],
        out_shape=[jax.ShapeDtypeStruct((t, N_HEADS * LANES), F32), jax.ShapeDtypeStruct((t, N_HEADS * LANES), F32),
                   jax.ShapeDtypeStruct((t, ATTN_WIDTH), BF16)],
        scratch_shapes=[pltpu.VMEM((tq, 2 * LANES), F32), pltpu.VMEM((tq, LANES), F32)],
        compiler_params=_params("arbitrary", "arbitrary", "arbitrary"),
    )(qp, kp, v, o, do, lse)


def _mix_out_fwd(o, y_pool, x, gain, w_out):
    t, d = x.shape
    tm = 1024
    pw, aw = POOL_WIDTH, ATTN_WIDTH

    def body(o_ref, yp_ref, x_ref, g_ref, w_ref, ycat_ref, y_ref):
        ov = o_ref[...]
        ya = (ov * _rms_scale(ov) * g_ref[...]).astype(BF16)
        ycat = jnp.concatenate([yp_ref[...], ya], axis=1)
        ycat_ref[...] = ycat
        y_ref[...] = x_ref[...] + _dot(ycat, w_ref[...])

    return pl.pallas_call(
        body, name="mix_out_fwd", grid=(t // tm,),
        in_specs=[_rows(tm, aw), _rows(tm, pw), _rows(tm, d), _resident((1, aw)), _resident((pw + aw, d))],
        out_specs=[_rows(tm, pw + aw), _rows(tm, d)],
        out_shape=[jax.ShapeDtypeStruct((t, pw + aw), BF16), jax.ShapeDtypeStruct((t, d), F32)],
        compiler_params=_params("arbitrary"),
    )(o, y_pool, x, gain, w_out)


def _mix_out_bwd(dx, o, gain, w_out):
    t, d = dx.shape
    tm = 1024
    pw, aw = POOL_WIDTH, ATTN_WIDTH

    def body(dx_ref, o_ref, g_ref, w_ref, dxb_ref, dyp_ref, do_ref, dg_ref):
        dxb = dx_ref[...].astype(BF16)
        dxb_ref[...] = dxb
        dyp_ref[...] = _dot_nt(dxb, w_ref[pl.ds(0, pw), :])
        dya = _dot_nt(dxb, w_ref[pl.ds(pw, aw), :])
        do, dgain = _rms_bwd(dya, o_ref[...], g_ref[...])
        do_ref[...] = do

        @pl.when(pl.program_id(0) == 0)
        def _():
            dg_ref[...] = jnp.zeros_like(dg_ref)

        dg_ref[...] += dgain

    return pl.pallas_call(
        body, name="mix_out_bwd", grid=(t // tm,),
        in_specs=[_rows(tm, d), _rows(tm, aw), _resident((1, aw)), _resident((pw + aw, d))],
        out_specs=[_rows(tm, d), _rows(tm, pw), _rows(tm, aw), pl.BlockSpec((1, aw), lambda i: (0, 0))],
        out_shape=[jax.ShapeDtypeStruct((t, d), BF16), jax.ShapeDtypeStruct((t, pw), F32),
                   jax.ShapeDtypeStruct((t, aw), F32), jax.ShapeDtypeStruct((1, aw), F32)],
        compiler_params=_params("arbitrary"),
    )(dx, o, gain, w_out)


def _mix_in_bwd(dpv, dq, dk, dv, df, x, dx_res, gain, w_in_t):
    t, d = x.shape
    tm = 512
    pw, aw = POOL_WIDTH, ATTN_WIDTH

    def body(dpv_ref, dq_ref, dk_ref, dv_ref, df_ref, x_ref, dxr_ref, g_ref, w_ref, dh_ref, dx_ref, dxh_ref, dg_ref):
        dh = jnp.concatenate([dpv_ref[...], dq_ref[...], dk_ref[...], dv_ref[...], df_ref[...]], axis=1)
        dh_ref[...] = dh
        dhm = _dot(dh, w_ref[...])
        dx, dgain = _rms_bwd(dhm, x_ref[...], g_ref[...])
        dx = dxr_ref[...] + dx
        dx_ref[...] = dx
        dxh_ref[...] = (0.5 * dx).astype(BF16)

        @pl.when(pl.program_id(0) == 0)
        def _():
            dg_ref[...] = jnp.zeros_like(dg_ref)

        dg_ref[...] += dgain

    return pl.pallas_call(
        body, name="mix_in_bwd", grid=(t // tm,),
        in_specs=[_rows(tm, pw), _rows(tm, aw), _rows(tm, aw), _rows(tm, aw), _rows(tm, LANES), _rows(tm, d),
                  _rows(tm, d), _resident((1, d)), _resident((MIX_PAD, d))],
        out_specs=[_rows(tm, MIX_PAD), _rows(tm, d), _rows(tm, d), pl.BlockSpec((1, d), lambda i: (0, 0))],
        out_shape=[jax.ShapeDtypeStruct((t, MIX_PAD), BF16), jax.ShapeDtypeStruct((t, d), F32),
                   jax.ShapeDtypeStruct((t, d), BF16), jax.ShapeDtypeStruct((1, d), F32)],
        compiler_params=_params("arbitrary"),
    )(dpv, dq, dk, dv, df, x, dx_res, gain, w_in_t)


MESH_IDS = pl.DeviceIdType.MESH


def _me():
    return lax.axis_index("x"), lax.axis_index("y"), lax.axis_index("c")


def _peer(x, y, c, p):
    px = 1 - x if p & 4 else x
    py = 1 - y if p & 2 else y
    pc = 1 - c if p & 1 else c
    return (px, py, pc), 4 * px + 2 * py + pc


HBM_SPEC = pl.BlockSpec(memory_space=pltpu.HBM)
SEM_SPEC = pl.BlockSpec(memory_space=pltpu.SEMAPHORE)
SPLIT_COPY = pltpu.CompilerParams(has_side_effects=pltpu.SideEffectType.DATAFLOW_SIDE_EFFECTING)
PEERS = N_DEV - 1


def _hbm(a):
    return pltpu.with_memory_space_constraint(a, pltpu.HBM)


def _row_block(ref, dev, rows):
    return ref.at[pl.ds(pl.multiple_of(dev * rows, BF16_ROWS), rows)]


def _copy_ends(gather, src, land, me, peer_id):
    if gather:
        rows = src.shape[0]
        return src, _row_block(land, me, rows), _row_block(land, peer_id, rows), src, _row_block(land, me, rows)
    rows = src.shape[0] // N_DEV
    return (_row_block(src, peer_id, rows), land.at[me], land.at[peer_id], _row_block(src, me, rows), land.at[me])


def _land_shape(gather, s):
    return (N_DEV * s.shape[0], s.shape[1]) if gather else (N_DEV, s.shape[0] // N_DEV, s.shape[1])


SIBLING = 1
SAME_CORE_PEERS = (2, 4, 6)
RELAYS = len(SAME_CORE_PEERS)


def _copies_start(groups, gather, name, after=None, relayed=()):
    flat = [s for g in groups for s in g]
    n, ng = len(flat), len(groups)
    lands = [lax.empty(_land_shape(gather, s), s.dtype) for s in flat]
    n_in = 2 * n + (after is not None)

    def body(*refs):
        ins, lnd = refs[:n], refs[n:2 * n]
        sems = refs[n_in:n_in + 2 * ng]
        token = refs[-1]
        x, y, c = _me()
        me = 4 * x + 2 * y + c
        w = 0
        for gi, g in enumerate(groups):
            for k in range(len(g)):
                for p in ((SIBLING,) + SAME_CORE_PEERS if gi in relayed else range(1, N_DEV)):
                    peer, peer_id = _peer(x, y, c, p)
                    src, dst, _, _, _ = _copy_ends(gather, ins[w], lnd[w], me, peer_id)
                    pltpu.make_async_remote_copy(src, dst, sems[2 * gi].at[k * PEERS + p - 1],
                                                 sems[2 * gi + 1].at[k * PEERS + p - 1], device_id=peer,
                                                 device_id_type=MESH_IDS).start()
                w += 1
        token[...] = jnp.zeros_like(token)

    sem_shapes = []
    for g in groups:
        sem_shapes += [pltpu.SemaphoreType.DMA((len(g) * PEERS,))] * 2
    out = pl.pallas_call(
        body, name=name,
        out_shape=(*sem_shapes, *[pltpu.HBM(s.shape, s.dtype) for s in flat],
                   *[pltpu.HBM(l.shape, l.dtype) for l in lands], jax.ShapeDtypeStruct((8, LANES), F32)),
        in_specs=[HBM_SPEC] * (2 * n) + [pl.BlockSpec(memory_space=pl.ANY)] * (after is not None),
        out_specs=(*[SEM_SPEC] * (2 * ng), *[HBM_SPEC] * (2 * n), pl.BlockSpec(memory_space=pltpu.VMEM)),
        input_output_aliases={i: 2 * ng + i for i in range(2 * n)},
        compiler_params=SPLIT_COPY,
    )(*[_hbm(s) for s in flat], *[_hbm(l) for l in lands], *([after] if after is not None else []))
    sems, thru, token = out[:2 * ng], out[2 * ng:2 * ng + 2 * n], out[-1]
    res, w = [], 0
    for gi, g in enumerate(groups):
        res.append((sems[2 * gi], sems[2 * gi + 1], list(thru[w:w + len(g)]), list(thru[n + w:n + w + len(g)])))
        w += len(g)
    return res, token


def _copies_wait(started, gather, after, name):
    send, recv, srcs, lands = started
    n = len(srcs)
    after = list(after) if isinstance(after, (list, tuple)) else [after]

    own_shapes = [s.shape if gather else (s.shape[0] // N_DEV, s.shape[1]) for s in srcs]

    def body(*refs):
        ins, lnd = refs[:n], refs[n:2 * n]
        send_sems, recv_sems = refs[2 * n], refs[2 * n + 1]
        bounce, in_sems, out_sems = refs[-n - 2:-2], refs[-2], refs[-1]
        x, y, c = _me()
        me = 4 * x + 2 * y + c
        ends = [_copy_ends(gather, ins[w], lnd[w], me, me)[3:] for w in range(n)]
        loads = [pltpu.make_async_copy(ends[w][0], bounce[w], in_sems.at[w]) for w in range(n)]
        stores = [pltpu.make_async_copy(bounce[w], ends[w][1], out_sems.at[w]) for w in range(n)]
        for cp in loads:
            cp.start()
        for w in range(n):
            loads[w].wait()
            stores[w].start()
        for w in range(n):
            for p in range(1, N_DEV):
                peer, peer_id = _peer(x, y, c, p)
                src, _, arrival, _, _ = _copy_ends(gather, ins[w], lnd[w], me, peer_id)
                cp = pltpu.make_async_remote_copy(src, arrival, send_sems.at[w * PEERS + p - 1],
                                                  recv_sems.at[w * PEERS + p - 1], device_id=peer,
                                                  device_id_type=MESH_IDS)
                cp.wait_send()
                cp.wait_recv()
        for cp in stores:
            cp.wait()

    out = pl.pallas_call(
        body, name=name,
        out_shape=(*[pltpu.HBM(s.shape, s.dtype) for s in srcs], *[pltpu.HBM(l.shape, l.dtype) for l in lands]),
        in_specs=[HBM_SPEC] * (2 * n) + [SEM_SPEC, SEM_SPEC] + [pl.BlockSpec(memory_space=pl.ANY)] * len(after),
        out_specs=[HBM_SPEC] * (2 * n),
        input_output_aliases={i: i for i in range(2 * n)},
        scratch_shapes=[*[pltpu.VMEM(shape, s.dtype) for shape, s in zip(own_shapes, srcs)],
                        pltpu.SemaphoreType.DMA((n,)), pltpu.SemaphoreType.DMA((n,))],
        compiler_params=SPLIT_COPY,
    )(*srcs, *lands, send, recv, *after)
    return list(out[n:])


def _relay_to_sibling(started, name, after=None):
    send, recv, srcs, lands = started
    n = len(srcs)
    after = [] if after is None else [after]

    def body(*refs):
        ins, lnd = refs[:n], refs[n:2 * n]
        send_sems, recv_sems = refs[2 * n], refs[2 * n + 1]
        relay_send, relay_recv = refs[2 * n + 2 + len(after)], refs[2 * n + 3 + len(after)]
        x, y, c = _me()
        sibling, _ = _peer(x, y, c, SIBLING)
        for w in range(n):
            rows = ins[w].shape[0]
            for k, p in enumerate(SAME_CORE_PEERS):
                peer, peer_id = _peer(x, y, c, p)
                arrived = _row_block(lnd[w], peer_id, rows)
                first = pltpu.make_async_remote_copy(ins[w], arrived, send_sems.at[w * PEERS + p - 1],
                                                     recv_sems.at[w * PEERS + p - 1], device_id=peer,
                                                     device_id_type=MESH_IDS)
                first.wait_recv()
                pltpu.make_async_remote_copy(arrived, arrived, relay_send.at[w * RELAYS + k],
                                             relay_recv.at[w * RELAYS + k], device_id=sibling,
                                             device_id_type=MESH_IDS).start()
                first.wait_send()

    sems = pltpu.SemaphoreType.DMA((n * RELAYS,))
    out = pl.pallas_call(
        body, name=name,
        out_shape=(sems, sems, *[pltpu.HBM(s.shape, s.dtype) for s in srcs], *[pltpu.HBM(l.shape, l.dtype) for l in lands]),
        in_specs=[HBM_SPEC] * (2 * n) + [SEM_SPEC, SEM_SPEC] + [pl.BlockSpec(memory_space=pl.ANY)] * len(after),
        out_specs=(SEM_SPEC, SEM_SPEC, *[HBM_SPEC] * (2 * n)),
        input_output_aliases={i: 2 + i for i in range(2 * n)},
        compiler_params=SPLIT_COPY,
    )(*srcs, *lands, send, recv, *after)
    return send, recv, out[0], out[1], list(out[2:2 + n]), list(out[2 + n:])


def _relayed_wait(relayed, after, name):
    send, recv, relay_send, relay_recv, srcs, lands = relayed
    n = len(srcs)
    after = list(after) if isinstance(after, (list, tuple)) else [after]

    def body(*refs):
        ins, lnd = refs[:n], refs[n:2 * n]
        send_sems, recv_sems, relay_send_sems, relay_recv_sems = refs[2 * n:2 * n + 4]
        bounce, in_sems, out_sems = refs[-n - 2:-2], refs[-2], refs[-1]
        x, y, c = _me()
        me = 4 * x + 2 * y + c
        sibling, sibling_id = _peer(x, y, c, SIBLING)
        loads = [pltpu.make_async_copy(ins[w], bounce[w], in_sems.at[w]) for w in range(n)]
        stores = [pltpu.make_async_copy(bounce[w], _row_block(lnd[w], me, ins[w].shape[0]), out_sems.at[w])
                  for w in range(n)]
        for cp in loads:
            cp.start()
        for w in range(n):
            loads[w].wait()
            stores[w].start()
        for w in range(n):
            rows = ins[w].shape[0]
            direct = pltpu.make_async_remote_copy(ins[w], _row_block(lnd[w], sibling_id, rows),
                                                  send_sems.at[w * PEERS + SIBLING - 1],
                                                  recv_sems.at[w * PEERS + SIBLING - 1], device_id=sibling,
                                                  device_id_type=MESH_IDS)
            direct.wait_send()
            direct.wait_recv()
            for k, p in enumerate(SAME_CORE_PEERS):
                _, sent_id = _peer(x, y, c, p)
                _, got_id = _peer(x, y, c, p + SIBLING)
                relay = pltpu.make_async_remote_copy(_row_block(lnd[w], sent_id, rows), _row_block(lnd[w], got_id, rows),
                                                     relay_send_sems.at[w * RELAYS + k],
                                                     relay_recv_sems.at[w * RELAYS + k], device_id=sibling,
                                                     device_id_type=MESH_IDS)
                relay.wait_send()
                relay.wait_recv()
        for cp in stores:
            cp.wait()

    out = pl.pallas_call(
        body, name=name,
        out_shape=(*[pltpu.HBM(s.shape, s.dtype) for s in srcs], *[pltpu.HBM(l.shape, l.dtype) for l in lands]),
        in_specs=[HBM_SPEC] * (2 * n) + [SEM_SPEC] * 4 + [pl.BlockSpec(memory_space=pl.ANY)] * len(after),
        out_specs=[HBM_SPEC] * (2 * n),
        input_output_aliases={i: i for i in range(2 * n)},
        scratch_shapes=[*[pltpu.VMEM(s.shape, s.dtype) for s in srcs],
                        pltpu.SemaphoreType.DMA((n,)), pltpu.SemaphoreType.DMA((n,))],
        compiler_params=SPLIT_COPY,
    )(*srcs, *lands, send, recv, relay_send, relay_recv, *after)
    return list(out[n:])


def _adamw_update(w, g, m, v):
    nm = ADAM_B1 * m + (1.0 - ADAM_B1) * g
    nv = ADAM_B2 * v + (1.0 - ADAM_B2) * (g * g)
    m_hat = nm / (1.0 - ADAM_B1 ** ADAM_STEP)
    v_hat = nv / (1.0 - ADAM_B2 ** ADAM_STEP)
    return -ADAM_LR * (m_hat / (jnp.sqrt(v_hat) + ADAM_EPS) + ADAM_WD * w), nm, nv


SUM_ADAMW_COLS = 512


def _sum_adamw(parts, w, m, v, name):
    _, rows, d = parts.shape
    n = w.shape[0]
    tc = SUM_ADAMW_COLS

    def body(p_ref, w_ref, m_ref, v_ref, g_ref, d_ref, nm_ref, nv_ref):
        g = p_ref[0].astype(F32)
        for dev in range(1, N_DEV):
            g = g + p_ref[dev].astype(F32)
        g = g[:n]
        g_ref[...] = g
        d_ref[...], nm_ref[...], nv_ref[...] = _adamw_update(w_ref[...], g, m_ref[...], v_ref[...])

    spec = pl.BlockSpec((n, tc), lambda j: (0, j))
    shape = jax.ShapeDtypeStruct((n, d), F32)
    return pl.pallas_call(
        body, name=name, grid=(d // tc,),
        in_specs=[pl.BlockSpec((N_DEV, rows, tc), lambda j: (0, 0, j)), spec, spec, spec],
        out_specs=[spec] * 4, out_shape=[shape] * 4,
        compiler_params=_params("arbitrary"),
    )(parts, w, m, v)


def _pad_rows(a, rows):
    return jnp.pad(a, ((0, rows - a.shape[0]), (0, 0)))


def _row1(vec, width=D_MODEL):
    return jnp.pad(vec.reshape(1, -1), ((0, 0), (0, width - vec.shape[-1])))


COLUMN_SHARDED = ("ffn1_w_gate", "ffn1_w_up", "w_in", "ffn2_w_gate", "ffn2_w_up")
VEC_NAMES = ("ffn1_norm", "mix_norm", "ffn2_norm", "b_forget", "pool_scale", "q_norm", "k_norm", "out_norm_pool",
             "out_norm_attn")
VEC_ROWS = 16
LOSS_ROW = len(VEC_NAMES)


def _pack_vector_grads(parts, loss_part, name):
    names = [n for n in VEC_NAMES if n in parts]
    extra = [] if loss_part is None else [loss_part]

    def body(*refs):
        out_ref = refs[-1]
        out_ref[...] = jnp.zeros_like(out_ref)
        lane = lax.broadcasted_iota(jnp.int32, (1, LANES), 1)
        for n, ref in zip(names, refs):
            val = ref[...]
            if n in ("q_norm", "k_norm"):
                val = val[:, 0:LANES] + val[:, LANES:2 * LANES] + val[:, 2 * LANES:3 * LANES] + val[:, 3 * LANES:]
                val = jnp.where(lane < HEAD_DIM, val + pltpu.roll(val, HEAD_DIM, 1), 0.0)
            out_ref[pl.ds(VEC_NAMES.index(n), 1), pl.ds(0, val.shape[1])] = val
        if extra:
            out_ref[pl.ds(LOSS_ROW, 1), pl.ds(0, 1)] = refs[len(names)][...]

    vmem = pl.BlockSpec(memory_space=pltpu.VMEM)
    return pl.pallas_call(
        body, name=name, in_specs=[vmem] * (len(names) + len(extra)), out_specs=vmem,
        out_shape=jax.ShapeDtypeStruct((VEC_ROWS, D_MODEL), F32),
    )(*[parts[n] for n in names], *extra)


def _small_adamw(vec_alls, pool_all, vec_params, pool_params):
    nv = len(vec_params)
    na = len(vec_alls)
    pool_rows = pool_params[0].shape[0]

    def body(*refs):
        vec_refs, pool_ref = refs[:na], refs[na]
        ins = refs[na + 1:na + 1 + 3 * nv + 3]
        outs = refs[na + 1 + 3 * nv + 3:-1]
        rows = refs[-1]
        total = jnp.zeros((VEC_ROWS, D_MODEL), F32)
        for vec_ref in vec_refs:
            for dev in range(N_DEV):
                total = total + vec_ref[pl.ds(dev * VEC_ROWS, VEC_ROWS), :]
        rows[...] = total
        outs[4 * nv + 4][...] = rows[pl.ds(LOSS_ROW, 1), pl.ds(0, 1)]
        for i in range(nv):
            w_ref, m_ref, v_ref = ins[3 * i:3 * i + 3]
            g = rows[pl.ds(i, 1), pl.ds(0, w_ref.shape[1])]
            outs[4 * i][...] = g
            outs[4 * i + 1][...], outs[4 * i + 2][...], outs[4 * i + 3][...] = _adamw_update(
                w_ref[...], g, m_ref[...], v_ref[...])
        g = pool_ref[pl.ds(0, pool_rows), :]
        for dev in range(1, N_DEV):
            g = g + pool_ref[pl.ds(dev * pool_rows, pool_rows), :]
        w_ref, m_ref, v_ref = ins[3 * nv:]
        outs[4 * nv][...] = g
        outs[4 * nv + 1][...], outs[4 * nv + 2][...], outs[4 * nv + 3][...] = _adamw_update(
            w_ref[...], g, m_ref[...], v_ref[...])

    vmem = pl.BlockSpec(memory_space=pltpu.VMEM)
    flat = [a for trio in vec_params for a in trio] + list(pool_params)
    out_shape = []
    for trio in list(vec_params) + [pool_params]:
        out_shape += [jax.ShapeDtypeStruct(trio[0].shape, F32)] * 4
    out_shape.append(jax.ShapeDtypeStruct((1, 1), F32))
    return pl.pallas_call(
        body, name="adamw_small", in_specs=[vmem] * (na + 1 + len(flat)), out_specs=[vmem] * len(out_shape),
        out_shape=out_shape, scratch_shapes=[pltpu.VMEM((VEC_ROWS, D_MODEL), F32)],
    )(*vec_alls, pool_all, *flat)


def kernel(x, ffn1_norm, ffn1_w_gate, ffn1_w_up, ffn1_w_down, mix_norm, w_in, b_forget, pool_w, pool_scale, q_norm, k_norm, out_norm_pool, out_norm_attn, w_out, ffn2_norm, ffn2_w_gate, ffn2_w_up, ffn2_w_down, loss_target, m_ffn1_norm, m_ffn1_w_gate, m_ffn1_w_up, m_ffn1_w_down, m_mix_norm, m_w_in, m_b_forget, m_pool_w, m_pool_scale, m_q_norm, m_k_norm, m_out_norm_pool, m_out_norm_attn, m_w_out, m_ffn2_norm, m_ffn2_w_gate, m_ffn2_w_up, m_ffn2_w_down, v_ffn1_norm, v_ffn1_w_gate, v_ffn1_w_up, v_ffn1_w_down, v_mix_norm, v_w_in, v_b_forget, v_pool_w, v_pool_scale, v_q_norm, v_k_norm, v_out_norm_pool, v_out_norm_attn, v_w_out, v_ffn2_norm, v_ffn2_w_gate, v_ffn2_w_up, v_ffn2_w_down):
    bsz, seq, d = x.shape
    t = bsz * seq
    x0 = x.reshape(t, d)
    target = loss_target.reshape(t, d)
    in_rows = -(-w_in.shape[1] // BF16_ROWS) * BF16_ROWS

    slabs = [s.astype(BF16) for s in (ffn1_w_gate.T, ffn1_w_up.T, ffn1_w_down, _pad_rows(w_in.T, in_rows), w_out,
                                       ffn2_w_gate.T, ffn2_w_up.T, ffn2_w_down)]
    gathers, started = _copies_start([slabs[0:2], slabs[2:3], slabs[3:4], slabs[4:5], slabs[5:8]], True, "gather_start",
                                     relayed=(0, 4))

    g1, gm, g2 = ffn1_norm.reshape(1, d), mix_norm.reshape(1, d), ffn2_norm.reshape(1, d)
    bf_row = _row1(b_forget, LANES)
    gq = jnp.tile(q_norm, N_HEADS).reshape(1, ATTN_WIDTH)
    gk = jnp.tile(k_norm, N_HEADS).reshape(1, ATTN_WIDTH)
    scale_row = pool_scale.reshape(1, POOL_WIDTH)
    gp, ga = out_norm_pool.reshape(1, POOL_WIDTH), out_norm_attn.reshape(1, ATTN_WIDTH)

    wg1, wu1 = _relayed_wait(_relay_to_sibling(gathers[0], "gather_relay_ffn1_up"), started, "gather_wait_ffn1_up")
    h1, sa1, sb1, s1 = _ffn_up(x0, g1, wg1, wu1, "ffn1_up")
    (wd1,) = _copies_wait(gathers[1], True, s1, "gather_wait_ffn1_down")
    (x1,) = _ffn_down(s1, wd1, x0, None, "ffn1_down")
    (win_g,) = _copies_wait(gathers[2], True, x1, "gather_wait_w_in")
    win_t = _repack_rows(win_g, in_rows, w_in.shape[1], N_DEV, "w_in_rows")
    hm, pv, q, k, v, f = _mix_in_fwd(x1, gm, win_t)
    pooled, mixed, y_pool = _pool_fwd(pv, pool_w, scale_row, gp, bsz, seq)
    qp, kp = _attn_prep_fwd(q, k, f, bf_row, gq, gk, bsz, seq)
    o, lse = _flash_fwd(qp, kp, v, bsz, seq)
    relayed_ffn2 = _relay_to_sibling(gathers[4], "gather_relay_ffn2", o)
    (wout,) = _copies_wait(gathers[3], True, [o, relayed_ffn2[4][0]], "gather_wait_w_out")
    ycat, x2 = _mix_out_fwd(o, y_pool, x1, ga, wout)
    wg2, wu2, wd2 = _relayed_wait(relayed_ffn2, x2, "gather_wait_ffn2")
    h2, sa2, sb2, s2 = _ffn_up(x2, g2, wg2, wu2, "ffn2_up")
    dx3, dyh2, loss_part = _ffn_down(s2, wd2, x2, target, "ffn2_down")

    da2, db2, dwg2, dwu2 = _ffn_bwd_act(dyh2, sa2, sb2, h2, wd2, "ffn2_bwd_act")
    (dwd2,) = _wgrad([s2], dyh2, "ffn2_down_wgrad")
    (sent_ffn2,), tok = _copies_start([[dwg2, dwu2, dwd2]], False, "exchange_start_ffn2")
    dx2, dg2 = _ffn_bwd_dx(da2, db2, dx3, x2, g2 + tok[0, 0], wg2, wu2, "ffn2_bwd_dx")
    dx2b, dy_pool, do, dga = _mix_out_bwd(dx2, o, ga, wout)
    (dwout,) = _wgrad([ycat], dx2b, "w_out_wgrad")
    (sent_out,), tok = _copies_start([[dwout]], False, "exchange_start_w_out")
    dqp, dkp, dv = _flash_bwd(qp, kp, v, o, do, lse, bsz, seq)
    dq, dk, df, dgq, dgk, dbf = _attn_prep_bwd(dqp, dkp, q, k, f, bf_row + tok[0, 0], gq, gk, bsz, seq)
    dpv, dpool_w, dscale, dgp = _pool_bwd(dy_pool, mixed, pooled, pool_w, scale_row, gp, bsz, seq)
    dhcat, dx1, dyh1, dgm = _mix_in_bwd(dpv, dq, dk, dv, df, x1, dx2, gm, win_t)
    (dwin,) = _wgrad([dhcat], hm, "w_in_wgrad")
    dwin_blocks = _repack_rows(dwin, w_in.shape[1], in_rows, N_DEV, "w_in_grad_blocks")
    (sent_in,), tok = _copies_start([[dwin_blocks]], False, "exchange_start_w_in")
    pool_rows = POOL_GROUPS * POOL_GROUP_DIM
    packed = _pack_vector_grads(dict(mix_norm=dgm, ffn2_norm=dg2, b_forget=dbf, pool_scale=dscale, q_norm=dgq,
                                     k_norm=dgk, out_norm_pool=dgp, out_norm_attn=dga), loss_part, "pack_vector_grads")
    (sent_small,), tok = _copies_start([[packed, dpool_w.reshape(pool_rows, POOL_GROUP_DIM)]], True, "small_grads_start",
                                       after=tok)
    (dwd1,) = _wgrad([s1], dyh1, "ffn1_down_wgrad")
    (sent_down1,), tok = _copies_start([[dwd1]], False, "exchange_start_ffn1_down", after=tok)
    da1, db1, dwg1, dwu1 = _ffn_bwd_act(dyh1, sa1, sb1, h1, wd1, "ffn1_bwd_act")
    (sent_gate1, sent_up1), tok = _copies_start([[dwg1], [dwu1]], False, "exchange_start_ffn1_up", after=tok)
    dx0, dg1 = _ffn_bwd_dx(da1, db1, dx1, x0, g1 + tok[0, 0], wg1, wu1, "ffn1_bwd_dx")
    (sent_last,), tok = _copies_start([[_pack_vector_grads(dict(ffn1_norm=dg1), None, "pack_ffn1_norm_grad")]], True,
                                      "ffn1_norm_grad_start")

    weights = dict(ffn1_norm=ffn1_norm, ffn1_w_gate=ffn1_w_gate, ffn1_w_up=ffn1_w_up, ffn1_w_down=ffn1_w_down,
                   mix_norm=mix_norm, w_in=w_in, b_forget=b_forget, pool_w=pool_w, pool_scale=pool_scale,
                   q_norm=q_norm, k_norm=k_norm, out_norm_pool=out_norm_pool, out_norm_attn=out_norm_attn,
                   w_out=w_out, ffn2_norm=ffn2_norm, ffn2_w_gate=ffn2_w_gate, ffn2_w_up=ffn2_w_up,
                   ffn2_w_down=ffn2_w_down)
    m_in = dict(ffn1_norm=m_ffn1_norm, ffn1_w_gate=m_ffn1_w_gate, ffn1_w_up=m_ffn1_w_up, ffn1_w_down=m_ffn1_w_down,
                mix_norm=m_mix_norm, w_in=m_w_in, b_forget=m_b_forget, pool_w=m_pool_w, pool_scale=m_pool_scale,
                q_norm=m_q_norm, k_norm=m_k_norm, out_norm_pool=m_out_norm_pool, out_norm_attn=m_out_norm_attn,
                w_out=m_w_out, ffn2_norm=m_ffn2_norm, ffn2_w_gate=m_ffn2_w_gate, ffn2_w_up=m_ffn2_w_up,
                ffn2_w_down=m_ffn2_w_down)
    v_in = dict(ffn1_norm=v_ffn1_norm, ffn1_w_gate=v_ffn1_w_gate, ffn1_w_up=v_ffn1_w_up, ffn1_w_down=v_ffn1_w_down,
                mix_norm=v_mix_norm, w_in=v_w_in, b_forget=v_b_forget, pool_w=v_pool_w, pool_scale=v_pool_scale,
                q_norm=v_q_norm, k_norm=v_k_norm, out_norm_pool=v_out_norm_pool, out_norm_attn=v_out_norm_attn,
                w_out=v_w_out, ffn2_norm=v_ffn2_norm, ffn2_w_gate=v_ffn2_w_gate, ffn2_w_up=v_ffn2_w_up,
                ffn2_w_down=v_ffn2_w_down)
    grads, delta, new_m, new_v = {}, {}, {}, {}
    after = [tok]
    plan = ((sent_ffn2, "ffn2", ("ffn2_w_gate", "ffn2_w_up", "ffn2_w_down")), (sent_out, "w_out", ("w_out",)),
            (sent_in, "w_in", ("w_in",)), (sent_down1, "ffn1_down", ("ffn1_w_down",)),
            (sent_gate1, "ffn1_gate", ("ffn1_w_gate",)), (sent_up1, "ffn1_up", ("ffn1_w_up",)))
    for sent, tag, names in plan:
        parts = _copies_wait(sent, False, after, f"exchange_wait_{tag}")
        after = []
        for n, part in zip(names, parts):
            turn = (lambda a: a.T) if n in COLUMN_SHARDED else (lambda a: a)
            done = _sum_adamw(part, turn(weights[n]), turn(m_in[n]), turn(v_in[n]), f"adamw_{n}")
            grads[n], delta[n], new_m[n], new_v[n] = (turn(a) for a in done)
            after.append(done[3])
    vec_all, pool_all = _copies_wait(sent_small, True, after, "small_grads_wait")
    (vec_last,) = _copies_wait(sent_last, True, vec_all, "ffn1_norm_grad_wait")
    as_row = lambda a: a.reshape(1, -1)
    as_pool = lambda a: a.reshape(pool_rows, POOL_GROUP_DIM)
    small = _small_adamw([vec_all, vec_last], pool_all,
                         [tuple(as_row(z[n]) for z in (weights, m_in, v_in)) for n in VEC_NAMES],
                         tuple(as_pool(z["pool_w"]) for z in (weights, m_in, v_in)))
    for i, n in enumerate(VEC_NAMES + ("pool_w",)):
        grads[n], delta[n], new_m[n], new_v[n] = (a.reshape(weights[n].shape) for a in small[4 * i:4 * i + 4])
    loss = small[-1].reshape(())

    order = ("ffn1_norm", "ffn1_w_gate", "ffn1_w_up", "ffn1_w_down", "mix_norm", "w_in", "b_forget", "pool_w",
             "pool_scale", "q_norm", "k_norm", "out_norm_pool", "out_norm_attn", "w_out", "ffn2_norm", "ffn2_w_gate",
             "ffn2_w_up", "ffn2_w_down")
    return (loss, dx0.reshape(bsz, seq, d), *[grads[n] for n in order], *[delta[n] for n in order],
            *[new_m[n] for n in order], *[new_v[n] for n in order])
```

```python
import functools

import jax
import jax.numpy as jnp
from jax import lax
from jax.experimental import pallas as pl
from jax.experimental.pallas import tpu as pltpu

F32 = jnp.float32
BF16 = jnp.bfloat16

EPS = 1e-6
D_MODEL = 1024
D_FF = 2816
N_HEADS = 8
HEAD_DIM = 64
POOL_WIDTH = 512
ATTN_WIDTH = 512
POOL_GROUPS = 4
POOL_GROUP_DIM = 128
POOL_WINDOWS = (2, 4, 8, 16)
POOL_HALO = 16
MIX_PAD = POOL_WIDTH + 3 * ATTN_WIDTH + 128
N_DEV = 8
BF16_ROWS = 16
LANES = 128
VMEM_LIMIT = 56 * 1024 * 1024

ADAM_LR = 0.001
ADAM_B1 = 0.9
ADAM_B2 = 0.999
ADAM_EPS = 1e-08
ADAM_WD = 0.01
ADAM_STEP = 10


def _params(*sem):
    return pltpu.CompilerParams(dimension_semantics=sem, vmem_limit_bytes=VMEM_LIMIT)


def _dot(a, b):
    return jnp.dot(a, b, preferred_element_type=F32)


def _dot_nt(a, b):
    return lax.dot_general(a, b, (((1,), (1,)), ((), ())), preferred_element_type=F32)


def _dot_tn(a, b):
    return lax.dot_general(a, b, (((0,), (0,)), ((), ())), preferred_element_type=F32)


def _resident(shape):
    return pl.BlockSpec(shape, lambda *_: (0,) * len(shape), pipeline_mode=pl.Buffered(1))


def _rows(tm, width):
    return pl.BlockSpec((tm, width), lambda i: (i, 0))


def _rms_scale(x):
    return lax.rsqrt(jnp.mean(x * x, axis=-1, keepdims=True) + EPS)


def _rms_bwd(dh, x, gain):
    r = _rms_scale(x)
    n = x * r
    dgain = jnp.sum(dh * n, axis=0, keepdims=True)
    dn = dh * gain
    dx = r * (dn - n * jnp.mean(dn * n, axis=-1, keepdims=True))
    return dx, dgain


def _split3(x):
    hi = x.astype(BF16)
    r1 = x - hi.astype(F32)
    mid = r1.astype(BF16)
    lo = (r1 - mid.astype(F32)).astype(BF16)
    return hi, mid, lo


def _split2(x):
    hi = x.astype(BF16)
    return hi, (x - hi.astype(F32)).astype(BF16)


FF_CHUNK = 256


def _swiglu_parts(a, b):
    sig = jax.nn.sigmoid(a)
    silu = a * sig
    return (b * (sig + silu * (1.0 - sig))).astype(BF16), silu.astype(BF16), (silu * b).astype(BF16)


def _ffn_up(x, gain, wg_t, wu_t, name):
    t, d = x.shape
    f = wg_t.shape[0]
    tm = 512

    def body(x_ref, g_ref, wg_ref, wu_ref, h_ref, sa_ref, sb_ref, s_ref):
        xv = x_ref[...]
        h = (xv * _rms_scale(xv) * g_ref[...]).astype(BF16)
        h_ref[...] = h
        for c in range(f // FF_CHUNK):
            sl = pl.ds(c * FF_CHUNK, FF_CHUNK)
            sa_ref[:, sl], sb_ref[:, sl], s_ref[:, sl] = _swiglu_parts(_dot_nt(h, wg_ref[sl, :]), _dot_nt(h, wu_ref[sl, :]))

    wide = jax.ShapeDtypeStruct((t, f), BF16)
    return pl.pallas_call(
        body, name=name, grid=(t // tm,),
        in_specs=[_rows(tm, d), _resident((1, d)), _resident((f, d)), _resident((f, d))],
        out_specs=[_rows(tm, d), _rows(tm, f), _rows(tm, f), _rows(tm, f)],
        out_shape=[jax.ShapeDtypeStruct((t, d), BF16), wide, wide, wide],
        compiler_params=_params("arbitrary"),
    )(x, gain, wg_t, wu_t)


def _ffn_down(s, wd, x, target, name):
    t, d = x.shape
    f = wd.shape[0]
    tm = 512
    with_loss = target is not None

    def body(*refs):
        if with_loss:
            s_ref, w_ref, x_ref, t_ref, dy_ref, dyh_ref, loss_ref = refs
        else:
            s_ref, w_ref, x_ref, y_ref = refs
        y = x_ref[...] + 0.5 * _dot(s_ref[...], w_ref[...])
        if with_loss:
            e = y - t_ref[...]
            dy = e * (1.0 / d)
            dy_ref[...] = dy
            dyh_ref[...] = (0.5 * dy).astype(BF16)

            @pl.when(pl.program_id(0) == 0)
            def _():
                loss_ref[...] = jnp.zeros_like(loss_ref)

            part = jnp.sum(jnp.sum(e * e, axis=0, keepdims=True), axis=1, keepdims=True)
            loss_ref[...] += part * (0.5 / d)
        else:
            y_ref[...] = y

    in_specs = [_rows(tm, f), _resident((f, d)), _rows(tm, d)]
    args = [s, wd, x]
    if with_loss:
        in_specs.append(_rows(tm, d))
        args.append(target)
        out_shape = [jax.ShapeDtypeStruct((t, d), F32), jax.ShapeDtypeStruct((t, d), BF16),
                     jax.ShapeDtypeStruct((1, 1), F32)]
        out_specs = [_rows(tm, d), _rows(tm, d), pl.BlockSpec((1, 1), lambda i: (0, 0))]
    else:
        out_shape = [jax.ShapeDtypeStruct((t, d), F32)]
        out_specs = [_rows(tm, d)]
    return pl.pallas_call(
        body, name=name, grid=(t // tm,), in_specs=in_specs, out_specs=out_specs, out_shape=out_shape,
        compiler_params=_params("arbitrary"),
    )(*args)


def _ffn_bwd_act(dyh, sa, sb, h, wd, name):
    t, d = dyh.shape
    f = wd.shape[0]
    tn = f // 2
    tk = 512
    nk = t // tk

    def body(dy_ref, sa_ref, sb_ref, h_ref, wd_ref, da_ref, db_ref, dwg_ref, dwu_ref, acc_g, acc_u):
        k = pl.program_id(1)

        @pl.when(k == 0)
        def _():
            acc_g[...] = jnp.zeros_like(acc_g)
            acc_u[...] = jnp.zeros_like(acc_u)

        ds = _dot_nt(dy_ref[...], wd_ref[...])
        da = (ds * sa_ref[...].astype(F32)).astype(BF16)
        db = (ds * sb_ref[...].astype(F32)).astype(BF16)
        da_ref[...] = da
        db_ref[...] = db
        hv = h_ref[...]
        acc_g[...] += _dot_tn(da, hv)
        acc_u[...] += _dot_tn(db, hv)

        @pl.when(k == nk - 1)
        def _():
            dwg_ref[...] = acc_g[...].astype(BF16)
            dwu_ref[...] = acc_u[...].astype(BF16)

    tokens = pl.BlockSpec((tk, d), lambda j, k: (k, 0))
    wide = pl.BlockSpec((tk, tn), lambda j, k: (k, j))
    weight = pl.BlockSpec((tn, d), lambda j, k: (j, 0))
    return pl.pallas_call(
        body, name=name, grid=(f // tn, nk),
        in_specs=[tokens, wide, wide, tokens, weight],
        out_specs=[wide, wide, weight, weight],
        out_shape=[jax.ShapeDtypeStruct((t, f), BF16)] * 2 + [jax.ShapeDtypeStruct((f, d), BF16)] * 2,
        scratch_shapes=[pltpu.VMEM((tn, d), F32)] * 2,
        compiler_params=_params("arbitrary", "arbitrary"),
    )(dyh, sa, sb, h, wd)


def _ffn_bwd_dx(da, db, dy, x, gain, wg_t, wu_t, name):
    t, d = x.shape
    f = wg_t.shape[0]
    tm = 512

    def body(da_ref, db_ref, dy_ref, x_ref, g_ref, wg_ref, wu_ref, dx_ref, dg_ref):
        dh = _dot(da_ref[...], wg_ref[...]) + _dot(db_ref[...], wu_ref[...])
        dx, dgain = _rms_bwd(dh, x_ref[...], g_ref[...])
        dx_ref[...] = dy_ref[...] + dx

        @pl.when(pl.program_id(0) == 0)
        def _():
            dg_ref[...] = jnp.zeros_like(dg_ref)

        dg_ref[...] += dgain

    return pl.pallas_call(
        body, name=name, grid=(t // tm,),
        in_specs=[_rows(tm, f), _rows(tm, f), _rows(tm, d), _rows(tm, d), _resident((1, d)), _resident((f, d)),
                  _resident((f, d))],
        out_specs=[_rows(tm, d), pl.BlockSpec((1, d), lambda i: (0, 0))],
        out_shape=[jax.ShapeDtypeStruct((t, d), F32), jax.ShapeDtypeStruct((1, d), F32)],
        compiler_params=_params("arbitrary"),
    )(da, db, dy, x, gain, wg_t, wu_t)


def _wgrad(lhs, b, name):
    t, n = lhs[0].shape
    d = b.shape[1]
    m = len(lhs)
    tn = n // 2 if n * d * m > (4 << 20) else n
    tk = 1024
    nk = t // tk

    def body(*refs):
        a_refs, b_ref, o_refs, accs = refs[:m], refs[m], refs[m + 1:2 * m + 1], refs[2 * m + 1:]
        k = pl.program_id(1)

        @pl.when(k == 0)
        def _():
            for acc in accs:
                acc[...] = jnp.zeros_like(acc)

        bv = b_ref[...]
        for a_ref, acc in zip(a_refs, accs):
            acc[...] += _dot_tn(a_ref[...], bv)

        @pl.when(k == nk - 1)
        def _():
            for o_ref, acc in zip(o_refs, accs):
                o_ref[...] = acc[...].astype(BF16)

    return pl.pallas_call(
        body, name=name, grid=(n // tn, nk),
        in_specs=[pl.BlockSpec((tk, tn), lambda j, k: (k, j))] * m + [pl.BlockSpec((tk, d), lambda j, k: (k, 0))],
        out_specs=[pl.BlockSpec((tn, d), lambda j, k: (j, 0))] * m,
        out_shape=[jax.ShapeDtypeStruct((n, d), BF16)] * m,
        scratch_shapes=[pltpu.VMEM((tn, d), F32)] * m,
        compiler_params=_params("arbitrary", "arbitrary"),
    )(*lhs, b)


def _repack_rows(a, rows_in, rows_out, blocks, name):
    total, d = a.shape
    real = min(rows_in, rows_out)

    def body(a_ref, o_ref, wide_in, wide_out):
        wide_in[...] = a_ref[...].astype(F32)
        wide_out[...] = jnp.zeros_like(wide_out)
        for j in range(blocks):
            wide_out[pl.ds(j * rows_out, real), :] = wide_in[pl.ds(j * rows_in, real), :]
        o_ref[...] = wide_out[...].astype(BF16)

    full = pl.BlockSpec((total, d), lambda i: (0, 0))
    return pl.pallas_call(
        body, name=name, grid=(1,), in_specs=[full], out_specs=full, out_shape=jax.ShapeDtypeStruct((total, d), BF16),
        scratch_shapes=[pltpu.VMEM((total, d), F32)] * 2,
        compiler_params=_params("arbitrary"),
    )(a)


def _mix_in_fwd(x, gain, w_in_t):
    t, d = x.shape
    tm = 1024
    pw, aw = POOL_WIDTH, ATTN_WIDTH

    def body(x_ref, g_ref, w_ref, hm_ref, pv_ref, q_ref, k_ref, v_ref, f_ref):
        xv = x_ref[...]
        hm = (xv * _rms_scale(xv) * g_ref[...]).astype(BF16)
        hm_ref[...] = hm
        pv_ref[...] = _dot_nt(hm, w_ref[pl.ds(0, pw), :])
        q_ref[...] = _dot_nt(hm, w_ref[pl.ds(pw, aw), :])
        k_ref[...] = _dot_nt(hm, w_ref[pl.ds(pw + aw, aw), :])
        v_ref[...] = _dot_nt(hm, w_ref[pl.ds(pw + 2 * aw, aw), :]).astype(BF16)
        f_ref[...] = _dot_nt(hm, w_ref[pl.ds(pw + 3 * aw, LANES), :])

    return pl.pallas_call(
        body, name="mix_in_fwd", grid=(t // tm,),
        in_specs=[_rows(tm, d), _resident((1, d)), _resident((MIX_PAD, d))],
        out_specs=[_rows(tm, d), _rows(tm, pw), _rows(tm, aw), _rows(tm, aw), _rows(tm, aw), _rows(tm, LANES)],
        out_shape=[jax.ShapeDtypeStruct((t, d), BF16), jax.ShapeDtypeStruct((t, pw), F32),
                   jax.ShapeDtypeStruct((t, aw), F32), jax.ShapeDtypeStruct((t, aw), F32),
                   jax.ShapeDtypeStruct((t, aw), BF16), jax.ShapeDtypeStruct((t, LANES), F32)],
        compiler_params=_params("arbitrary"),
    )(x, gain, w_in_t)


def _pool_fwd(pv, pool_w, pool_scale, gain, bsz, seq):
    ts = 512
    ns = seq // ts
    pw = POOL_WIDTH

    def body(pv_ref, w_ref, sc_ref, g_ref, pooled_ref, mixed_ref, y_ref, ext):
        s = pl.program_id(1)

        @pl.when(s == 0)
        def _():
            ext[pl.ds(0, POOL_HALO), :] = jnp.zeros((POOL_HALO, pw), F32)

        p = pv_ref[...]
        ext[pl.ds(POOL_HALO, ts), :] = p
        pos = s * ts + lax.broadcasted_iota(jnp.int32, (ts, 1), 0)
        parts = []
        for g, w in enumerate(POOL_WINDOWS):
            lanes = pl.ds(g * POOL_GROUP_DIM, POOL_GROUP_DIM)
            win = ext[pl.ds(POOL_HALO, ts), lanes]
            for i in range(1, w):
                win = win + ext[pl.ds(POOL_HALO - i, ts), lanes]
            cnt = jnp.minimum(pos + 1, w).astype(F32)
            pooled = (win / cnt - ext[pl.ds(POOL_HALO, ts), lanes]).astype(BF16)
            pooled_ref[:, lanes] = pooled
            parts.append(_dot(pooled, w_ref[g].astype(BF16)))
        mixed = jnp.concatenate(parts, axis=1)
        mixed_ref[...] = mixed
        pm = mixed * sc_ref[...]
        y_ref[...] = (pm * _rms_scale(pm) * g_ref[...]).astype(BF16)
        ext[pl.ds(0, POOL_HALO), :] = p[ts - POOL_HALO:, :]

    blk = pl.BlockSpec((ts, pw), lambda b, s: (b * ns + s, 0))
    t = bsz * seq
    return pl.pallas_call(
        body, name="pool_fwd", grid=(bsz, ns),
        in_specs=[blk, pl.BlockSpec((POOL_GROUPS, POOL_GROUP_DIM, POOL_GROUP_DIM), lambda b, s: (0, 0, 0)),
                  pl.BlockSpec((1, pw), lambda b, s: (0, 0)), pl.BlockSpec((1, pw), lambda b, s: (0, 0))],
        out_specs=[blk, blk, blk],
        out_shape=[jax.ShapeDtypeStruct((t, pw), BF16), jax.ShapeDtypeStruct((t, pw), F32),
                   jax.ShapeDtypeStruct((t, pw), BF16)],
        scratch_shapes=[pltpu.VMEM((POOL_HALO + ts, pw), F32)],
        compiler_params=_params("arbitrary", "arbitrary"),
    )(pv, pool_w, pool_scale, gain)


def _pool_bwd(dy, mixed, pooled, pool_w, pool_scale, gain, bsz, seq):
    ts = 512
    ns = seq // ts
    pw = POOL_WIDTH

    def body(dy_ref, mixed_ref, pooled_ref, w_ref, sc_ref, g_ref, dpv_ref, dw_ref, dsc_ref, dg_ref, ext):
        b = pl.program_id(0)
        sr = pl.program_id(1)
        s = ns - 1 - sr

        @pl.when(jnp.logical_and(b == 0, sr == 0))
        def _():
            dw_ref[...] = jnp.zeros_like(dw_ref)
            dsc_ref[...] = jnp.zeros_like(dsc_ref)
            dg_ref[...] = jnp.zeros_like(dg_ref)

        @pl.when(sr == 0)
        def _():
            ext[pl.ds(ts, POOL_HALO), :] = jnp.zeros((POOL_HALO, pw), F32)

        mixed = mixed_ref[...]
        sc = sc_ref[...]
        dpm, dgain = _rms_bwd(dy_ref[...], mixed * sc, g_ref[...])
        dg_ref[...] += dgain
        dsc_ref[...] += jnp.sum(dpm * mixed, axis=0, keepdims=True)
        dmixed = (dpm * sc).astype(BF16)
        pos = s * ts + lax.broadcasted_iota(jnp.int32, (ts, 1), 0)
        dpooled = []
        for g, w in enumerate(POOL_WINDOWS):
            lanes = pl.ds(g * POOL_GROUP_DIM, POOL_GROUP_DIM)
            dm = dmixed[:, g * POOL_GROUP_DIM:(g + 1) * POOL_GROUP_DIM]
            dw_ref[g] += _dot_tn(pooled_ref[:, lanes], dm)
            dp = _dot_nt(dm, w_ref[g].astype(BF16))
            dpooled.append(dp)
            cnt = jnp.minimum(pos + 1, w).astype(F32)
            ext[pl.ds(0, ts), lanes] = dp / cnt
        for g, w in enumerate(POOL_WINDOWS):
            lanes = pl.ds(g * POOL_GROUP_DIM, POOL_GROUP_DIM)
            win = ext[pl.ds(0, ts), lanes]
            for i in range(1, w):
                win = win + ext[pl.ds(i, ts), lanes]
            dpv_ref[:, lanes] = (win - dpooled[g]).astype(BF16)
        head = ext[pl.ds(0, POOL_HALO), :]
        ext[pl.ds(ts, POOL_HALO), :] = head

    blk = pl.BlockSpec((ts, pw), lambda b, s: (b * ns + (ns - 1 - s), 0))
    vec = pl.BlockSpec((1, pw), lambda b, s: (0, 0))
    wspec = pl.BlockSpec((POOL_GROUPS, POOL_GROUP_DIM, POOL_GROUP_DIM), lambda b, s: (0, 0, 0))
    t = bsz * seq
    return pl.pallas_call(
        body, name="pool_bwd", grid=(bsz, ns),
        in_specs=[blk, blk, blk, wspec, vec, vec],
        out_specs=[blk, wspec, vec, vec],
        out_shape=[jax.ShapeDtypeStruct((t, pw), BF16),
                   jax.ShapeDtypeStruct((POOL_GROUPS, POOL_GROUP_DIM, POOL_GROUP_DIM), F32),
                   jax.ShapeDtypeStruct((1, pw), F32), jax.ShapeDtypeStruct((1, pw), F32)],
        scratch_shapes=[pltpu.VMEM((ts + POOL_HALO, pw), F32)],
        compiler_params=_params("arbitrary", "arbitrary"),
    )(dy, mixed, pooled, pool_w, pool_scale, gain)


AUX_ONE = 64
AUX_F = 67

ATTN_PREP_ROWS = 512


def _seg_ones(width, seg):
    r = lax.broadcasted_iota(jnp.int32, (width, width), 0) // seg
    c = lax.broadcasted_iota(jnp.int32, (width, width), 1) // seg
    return (r == c).astype(BF16)


def _tri_ones(n, lower):
    r = lax.broadcasted_iota(jnp.int32, (n, n), 0)
    c = lax.broadcasted_iota(jnp.int32, (n, n), 1)
    return ((r >= c) if lower else (r <= c)).astype(BF16)


def _place_pieces(first_lane):
    r = lax.broadcasted_iota(jnp.int32, (3 * LANES, N_HEADS * LANES), 0)
    c = lax.broadcasted_iota(jnp.int32, (3 * LANES, N_HEADS * LANES), 1)
    piece, head = r // LANES, r % LANES
    return jnp.logical_and(head < N_HEADS, c == head * LANES + first_lane + piece).astype(BF16)


def _head_sums(x, seg_ones):
    hi, lo = _split2(x)
    return _dot(hi, seg_ones) + _dot(lo, seg_ones)


def _log_sigmoid(x):
    return jnp.minimum(x, 0.0) - jnp.log(1.0 + jnp.exp(-jnp.abs(x)))


def _attn_prep_fwd(q, k, f, b_forget, q_gain, k_gain, bsz, seq):
    ts = ATTN_PREP_ROWS
    ns = seq // ts
    aw = ATTN_WIDTH
    t = bsz * seq
    seg = _seg_ones(aw, HEAD_DIM)
    tri = _tri_ones(ts, True)

    def body(q_ref, k_ref, f_ref, bf_ref, gq_ref, gk_ref, seg_ref, tri_ref, pq_ref, pk_ref, qp_ref, kp_ref, carry):
        s = pl.program_id(1)

        @pl.when(s == 0)
        def _():
            carry[...] = jnp.zeros_like(carry)

        logf = _log_sigmoid(f_ref[...] + bf_ref[...])
        hi, mid, lo = _split3(logf)
        tri_v = tri_ref[...]
        fc = _dot(tri_v, hi) + _dot(tri_v, mid) + _dot(tri_v, lo) + carry[pl.ds(0, 1), :]
        carry[pl.ds(0, 1), :] = fc[ts - 1:, :]
        pcs = jnp.concatenate(_split3(fc), axis=1)
        lane = lax.broadcasted_iota(jnp.int32, (1, LANES), 1)
        ones_q = jnp.logical_and(lane >= AUX_ONE, lane < AUX_ONE + 3).astype(F32)
        ones_k = jnp.logical_and(lane >= AUX_F, lane < AUX_F + 3).astype(F32)
        seg_v = seg_ref[...]

        def build(x_ref, g_ref, scale, out_ref, ones, place_ref, f_sign):
            xv = x_ref[...]
            r = lax.rsqrt(_head_sums(xv * xv, seg_v) * (1.0 / HEAD_DIM) + EPS)
            xn = xv * r * g_ref[...] * scale
            aux = _dot(pcs, place_ref[...]) * f_sign
            for h in range(N_HEADS):
                pair = xn[:, (h // 2) * LANES:(h // 2 + 1) * LANES]
                feat = pair if h % 2 == 0 else pltpu.roll(pair, HEAD_DIM, 1)
                aux_h = aux[:, h * LANES:(h + 1) * LANES] + ones
                out_ref[:, h * LANES:(h + 1) * LANES] = jnp.where(lane < HEAD_DIM, feat, aux_h).astype(BF16)

        build(q_ref, gq_ref, 0.125, qp_ref, ones_q, pq_ref, 1.0)
        build(k_ref, gk_ref, 1.0, kp_ref, ones_k, pk_ref, -1.0)

    blk = pl.BlockSpec((ts, aw), lambda b, s: (b * ns + s, 0))
    fblk = pl.BlockSpec((ts, LANES), lambda b, s: (b * ns + s, 0))
    oblk = pl.BlockSpec((ts, N_HEADS * LANES), lambda b, s: (b * ns + s, 0))
    const = lambda shape: pl.BlockSpec(shape, lambda b, s: (0, 0))
    return pl.pallas_call(
        body, name="attn_prep_fwd", grid=(bsz, ns),
        in_specs=[blk, blk, fblk, const((1, LANES)), const((1, aw)), const((1, aw)), const((aw, aw)), const((ts, ts)),
                  const((3 * LANES, N_HEADS * LANES)), const((3 * LANES, N_HEADS * LANES))],
        out_specs=[oblk, oblk],
        out_shape=[jax.ShapeDtypeStruct((t, N_HEADS * LANES), BF16)] * 2,
        scratch_shapes=[pltpu.VMEM((8, LANES), F32)],
        compiler_params=_params("arbitrary", "arbitrary"),
    )(q, k, f, b_forget, q_gain, k_gain, seg, tri, _place_pieces(AUX_F), _place_pieces(AUX_ONE))


def _attn_prep_bwd(dqp, dkp, q, k, f, b_forget, q_gain, k_gain, bsz, seq):
    ts = ATTN_PREP_ROWS
    ns = seq // ts
    aw = ATTN_WIDTH
    t = bsz * seq
    seg = _seg_ones(aw, HEAD_DIM)
    tri = _tri_ones(ts, False)

    def body(dqp_ref, dkp_ref, q_ref, k_ref, f_ref, bf_ref, gq_ref, gk_ref, seg_ref, tri_ref,
             dq_ref, dk_ref, df_ref, dgq_ref, dgk_ref, dbf_ref, carry):
        b = pl.program_id(0)
        sr = pl.program_id(1)

        @pl.when(jnp.logical_and(b == 0, sr == 0))
        def _():
            dgq_ref[...] = jnp.zeros_like(dgq_ref)
            dgk_ref[...] = jnp.zeros_like(dgk_ref)
            dbf_ref[...] = jnp.zeros_like(dbf_ref)

        @pl.when(sr == 0)
        def _():
            carry[...] = jnp.zeros_like(carry)

        lane = lax.broadcasted_iota(jnp.int32, (1, LANES), 1)
        seg_v = seg_ref[...]

        def norm_bwd(dp_ref, x_ref, g_ref, scale, dx_ref, dgain_ref):
            parts = []
            for j in range(N_HEADS // 2):
                even = dp_ref[:, (2 * j) * LANES:(2 * j + 1) * LANES]
                odd = dp_ref[:, (2 * j + 1) * LANES:(2 * j + 2) * LANES]
                parts.append(jnp.where(lane < HEAD_DIM, even, pltpu.roll(odd, HEAD_DIM, 1)))
            dxn = jnp.concatenate(parts, axis=1) * scale
            xv = x_ref[...]
            r = lax.rsqrt(_head_sums(xv * xv, seg_v) * (1.0 / HEAD_DIM) + EPS)
            n = xv * r
            dgain_ref[...] += jnp.sum(dxn * n, axis=0, keepdims=True)
            dn = dxn * g_ref[...]
            m = _head_sums(dn * n, seg_v) * (1.0 / HEAD_DIM)
            dx_ref[...] = (r * (dn - n * m)).astype(BF16)

        norm_bwd(dqp_ref, q_ref, gq_ref, 0.125, dq_ref, dgq_ref)
        norm_bwd(dkp_ref, k_ref, gk_ref, 1.0, dk_ref, dgk_ref)

        dfc = jnp.zeros((ts, LANES), F32)
        for h in range(N_HEADS):
            cols = pl.ds(h * LANES, LANES)
            both = jnp.where(lane == AUX_F, dqp_ref[:, cols], 0.0) - jnp.where(lane == AUX_ONE, dkp_ref[:, cols], 0.0)
            dfc = jnp.where(lane == h, jnp.sum(both, axis=1, keepdims=True), dfc)
        hi, mid, lo = _split3(dfc)
        tri_v = tri_ref[...]
        dlogf = _dot(tri_v, hi) + _dot(tri_v, mid) + _dot(tri_v, lo) + carry[pl.ds(0, 1), :]
        carry[pl.ds(0, 1), :] = dlogf[0:1, :]
        df = jnp.where(lane < N_HEADS, dlogf * jax.nn.sigmoid(-(f_ref[...] + bf_ref[...])), 0.0)
        df_ref[...] = df.astype(BF16)
        dbf_ref[...] += jnp.sum(df, axis=0, keepdims=True)

    rev = lambda b, s: (b * ns + (ns - 1 - s), 0)
    blk = pl.BlockSpec((ts, aw), rev)
    fblk = pl.BlockSpec((ts, LANES), rev)
    pblk = pl.BlockSpec((ts, N_HEADS * LANES), rev)
    const = lambda shape: pl.BlockSpec(shape, lambda b, s: (0, 0))
    return pl.pallas_call(
        body, name="attn_prep_bwd", grid=(bsz, ns),
        in_specs=[pblk, pblk, blk, blk, fblk, const((1, LANES)), const((1, aw)), const((1, aw)), const((aw, aw)),
                  const((ts, ts))],
        out_specs=[blk, blk, fblk, const((1, aw)), const((1, aw)), const((1, LANES))],
        out_shape=[jax.ShapeDtypeStruct((t, aw), BF16), jax.ShapeDtypeStruct((t, aw), BF16),
                   jax.ShapeDtypeStruct((t, LANES), BF16), jax.ShapeDtypeStruct((1, aw), F32),
                   jax.ShapeDtypeStruct((1, aw), F32), jax.ShapeDtypeStruct((1, LANES), F32)],
        scratch_shapes=[pltpu.VMEM((8, LANES), F32)],
        compiler_params=_params("arbitrary", "arbitrary"),
    )(dqp, dkp, q, k, f, b_forget, q_gain, k_gain, seg, tri)


ATTN_BLOCK = 1024
HEAD_PAIRS = N_HEADS // 2


def _flash_fwd(qp, kp, v, bsz, seq):
    tq = ATTN_BLOCK
    half = tq // 2
    nq = seq // tq
    t = bsz * seq

    def body(q_ref, k_ref, v_ref, o_ref, lse_ref, m_sc, l_sc, acc_sc):
        i = pl.program_id(2)
        m_sc[...] = jnp.full(m_sc.shape, -jnp.inf, F32)
        l_sc[...] = jnp.zeros_like(l_sc)
        acc_sc[...] = jnp.zeros_like(acc_sc)
        lane = lax.broadcasted_iota(jnp.int32, (1, LANES), 1)
        low = lane < HEAD_DIM

        def tile(q0, qn, k_start, kn, k0=None):
            qs = pl.ds(q0, qn)
            ks = pl.ds(k_start, kn)
            vv = v_ref[ks, :]
            for h in range(2):
                mine = low if h == 0 else jnp.logical_not(low)
                cols = pl.ds(h * LANES, LANES)
                s = _dot_nt(q_ref[qs, cols], k_ref[ks, cols])
                if k0 is not None:
                    row = lax.broadcasted_iota(jnp.int32, (qn, kn), 0) + q0
                    col = lax.broadcasted_iota(jnp.int32, (qn, kn), 1) + k0
                    s = jnp.where(row >= col, s, -jnp.inf)
                m_prev = m_sc[h, qs, :]
                m_new = jnp.maximum(m_prev, jnp.max(s, axis=1, keepdims=True))
                p = jnp.exp(s - jnp.tile(m_new, (1, kn // LANES)))
                alpha = jnp.exp(m_prev - m_new)
                l_sc[h, qs, :] = alpha * l_sc[h, qs, :] + jnp.sum(p, axis=1, keepdims=True)
                m_sc[h, qs, :] = m_new
                pv = _dot(p.astype(BF16), jnp.where(mine, vv, jnp.zeros_like(vv)))
                acc_sc[qs, :] = acc_sc[qs, :] * jnp.where(mine, alpha, 1.0) + pv

        def below_diagonal(j, carry):
            tile(0, tq, pl.multiple_of(j * tq, tq), tq)
            return carry

        lax.fori_loop(0, i, below_diagonal, 0)
        diagonal = pl.multiple_of(i * tq, tq)
        tile(0, tq, diagonal, half, k0=0)
        tile(half, half, diagonal + half, half, k0=half)
        l = jnp.where(low, l_sc[0], l_sc[1])
        m = jnp.where(low, m_sc[0], m_sc[1])
        o_ref[...] = acc_sc[...] / l
        lse_ref[...] = m + jnp.log(l)

    qspec = pl.BlockSpec((tq, 2 * LANES), lambda b, hp, i: (b * nq + i, hp))
    kspec = pl.BlockSpec((seq, 2 * LANES), lambda b, hp, i: (b, hp))
    vspec = pl.BlockSpec((seq, LANES), lambda b, hp, i: (b, hp))
    ospec = pl.BlockSpec((tq, LANES), lambda b, hp, i: (b * nq + i, hp))
    return pl.pallas_call(
        body, name="flash_fwd", grid=(bsz, HEAD_PAIRS, nq),
        in_specs=[qspec, kspec, vspec], out_specs=[ospec, ospec],
        out_shape=[jax.ShapeDtypeStruct((t, ATTN_WIDTH), F32), jax.ShapeDtypeStruct((t, ATTN_WIDTH), F32)],
        scratch_shapes=[pltpu.VMEM((2, tq, LANES), F32), pltpu.VMEM((2, tq, LANES), F32), pltpu.VMEM((tq, LANES), F32)],
        compiler_params=_params("arbitrary", "arbitrary", "arbitrary"),
    )(qp, kp, v)


def _flash_bwd(qp, kp, v, o, do, lse, bsz, seq):
    tq = ATTN_BLOCK
    half = tq // 2
    nq = seq // tq
    t = bsz * seq

    def body(q_ref, k_ref, v_ref, o_ref, do_ref, lse_ref, dq_ref, dk_ref, dv_ref, dk_acc, dv_acc):
        j = pl.program_id(2)

        @pl.when(j == 0)
        def _():
            dq_ref[...] = jnp.zeros_like(dq_ref)

        dk_acc[...] = jnp.zeros_like(dk_acc)
        dv_acc[...] = jnp.zeros_like(dv_acc)
        lane = lax.broadcasted_iota(jnp.int32, (1, LANES), 1)
        low = lane < HEAD_DIM

        def tile(q_start, qn, k0, kn, q0=None):
            rows = pl.ds(q_start, qn)
            ks = pl.ds(k0, kn)
            dov = do_ref[rows, :]
            dd = dov * o_ref[rows, :]
            dob = dov.astype(BF16)
            vv = v_ref[ks, :]
            lse_v = lse_ref[rows, :]
            for h in range(2):
                mine = low if h == 0 else jnp.logical_not(low)
                cols = pl.ds(h * LANES, LANES)
                qh = q_ref[rows, cols]
                kh = k_ref[ks, cols]
                s = _dot_nt(qh, kh)
                lse_h = jnp.where(mine, lse_v, pltpu.roll(lse_v, HEAD_DIM, 1))
                p = jnp.exp(s - jnp.tile(lse_h, (1, kn // LANES)))
                if q0 is not None:
                    row = lax.broadcasted_iota(jnp.int32, (qn, kn), 0) + q0
                    col = lax.broadcasted_iota(jnp.int32, (qn, kn), 1) + k0
                    p = jnp.where(row >= col, p, 0.0)
                delta = jnp.sum(jnp.where(mine, dd, 0.0), axis=1, keepdims=True)
                dp = _dot_nt(dob, jnp.where(mine, vv, jnp.zeros_like(vv)))
                ds = (p * (dp - delta)).astype(BF16)
                dv_acc[ks, :] += jnp.where(mine, _dot_tn(p.astype(BF16), dob), 0.0)
                dk_acc[ks, cols] += _dot_tn(ds, qh)
                dq_ref[rows, cols] += _dot(ds, kh)

        def above_diagonal(i, carry):
            tile(pl.multiple_of(i * tq, tq), tq, 0, tq)
            return carry

        diagonal = pl.multiple_of(j * tq, tq)
        tile(diagonal, tq, 0, half, q0=0)
        tile(diagonal + half, half, half, half, q0=half)
        lax.fori_loop(j + 1, nq, above_diagonal, 0)
        dk_ref[...] = dk_acc[...]
        dv_ref[...] = dv_acc[...].astype(BF16)

    qspec = pl.BlockSpec((seq, 2 * LANES), lambda b, hp, j: (b, hp))
    kspec = pl.BlockSpec((tq, 2 * LANES), lambda b, hp, j: (b * nq + j, hp))
    vspec = pl.BlockSpec((tq, LANES), lambda b, hp, j: (b * nq + j, hp))
    ospec = pl.BlockSpec((seq, LANES), lambda b, hp, j: (b, hp))
    return pl.pallas_call(
        body, name="flash_bwd", grid=(bsz, HEAD_PAIRS, nq),
        in_specs=[qspec, kspec, vspec, ospec, ospec, ospec], out_specs=[qspec, kspec, vspec],
        out_shape=[jax.ShapeDtypeStruct((t, N_HEADS * LANES), F32), jax.ShapeDtypeStruct((t, N_HEADS * LANES), F32),
                   jax.ShapeDtypeStruct((t, ATTN_WIDTH), BF16)],
        scratch_shapes=[pltpu.VMEM((tq, 2 * LANES), F32), pltpu.VMEM((tq, LANES), F32)],
        compiler_params=_params("arbitrary", "arbitrary", "arbitrary"),
    )(qp, kp, v, o, do, lse)


def _mix_out_fwd(o, y_pool, x, gain, w_out):
    t, d = x.shape
    tm = 1024
    pw, aw = POOL_WIDTH, ATTN_WIDTH

    def body(o_ref, yp_ref, x_ref, g_ref, w_ref, ycat_ref, y_ref):
        ov = o_ref[...]
        ya = (ov * _rms_scale(ov) * g_ref[...]).astype(BF16)
        ycat = jnp.concatenate([yp_ref[...], ya], axis=1)
        ycat_ref[...] = ycat
        y_ref[...] = x_ref[...] + _dot(ycat, w_ref[...])

    return pl.pallas_call(
        body, name="mix_out_fwd", grid=(t // tm,),
        in_specs=[_rows(tm, aw), _rows(tm, pw), _rows(tm, d), _resident((1, aw)), _resident((pw + aw, d))],
        out_specs=[_rows(tm, pw + aw), _rows(tm, d)],
        out_shape=[jax.ShapeDtypeStruct((t, pw + aw), BF16), jax.ShapeDtypeStruct((t, d), F32)],
        compiler_params=_params("arbitrary"),
    )(o, y_pool, x, gain, w_out)


def _mix_out_bwd(dx, o, gain, w_out):
    t, d = dx.shape
    tm = 1024
    pw, aw = POOL_WIDTH, ATTN_WIDTH

    def body(dx_ref, o_ref, g_ref, w_ref, dxb_ref, dyp_ref, do_ref, dg_ref):
        dxb = dx_ref[...].astype(BF16)
        dxb_ref[...] = dxb
        dyp_ref[...] = _dot_nt(dxb, w_ref[pl.ds(0, pw), :])
        dya = _dot_nt(dxb, w_ref[pl.ds(pw, aw), :])
        do, dgain = _rms_bwd(dya, o_ref[...], g_ref[...])
        do_ref[...] = do

        @pl.when(pl.program_id(0) == 0)
        def _():
            dg_ref[...] = jnp.zeros_like(dg_ref)

        dg_ref[...] += dgain

    return pl.pallas_call(
        body, name="mix_out_bwd", grid=(t // tm,),
        in_specs=[_rows(tm, d), _rows(tm, aw), _resident((1, aw)), _resident((pw + aw, d))],
        out_specs=[_rows(tm, d), _rows(tm, pw), _rows(tm, aw), pl.BlockSpec((1, aw), lambda i: (0, 0))],
        out_shape=[jax.ShapeDtypeStruct((t, d), BF16), jax.ShapeDtypeStruct((t, pw), F32),
                   jax.ShapeDtypeStruct((t, aw), F32), jax.ShapeDtypeStruct((1, aw), F32)],
        compiler_params=_params("arbitrary"),
    )(dx, o, gain, w_out)


def _mix_in_bwd(dpv, dq, dk, dv, df, x, dx_res, gain, w_in_t):
    t, d = x.shape
    tm = 512
    pw, aw = POOL_WIDTH, ATTN_WIDTH

    def body(dpv_ref, dq_ref, dk_ref, dv_ref, df_ref, x_ref, dxr_ref, g_ref, w_ref, dh_ref, dx_ref, dxh_ref, dg_ref):
        dh = jnp.concatenate([dpv_ref[...], dq_ref[...], dk_ref[...], dv_ref[...], df_ref[...]], axis=1)
        dh_ref[...] = dh
        dhm = _dot(dh, w_ref[...])
        dx, dgain = _rms_bwd(dhm, x_ref[...], g_ref[...])
        dx = dxr_ref[...] + dx
        dx_ref[...] = dx
        dxh_ref[...] = (0.5 * dx).astype(BF16)

        @pl.when(pl.program_id(0) == 0)
        def _():
            dg_ref[...] = jnp.zeros_like(dg_ref)

        dg_ref[...] += dgain

    return pl.pallas_call(
        body, name="mix_in_bwd", grid=(t // tm,),
        in_specs=[_rows(tm, pw), _rows(tm, aw), _rows(tm, aw), _rows(tm, aw), _rows(tm, LANES), _rows(tm, d),
                  _rows(tm, d), _resident((1, d)), _resident((MIX_PAD, d))],
        out_specs=[_rows(tm, MIX_PAD), _rows(tm, d), _rows(tm, d), pl.BlockSpec((1, d), lambda i: (0, 0))],
        out_shape=[jax.ShapeDtypeStruct((t, MIX_PAD), BF16), jax.ShapeDtypeStruct((t, d), F32),
                   jax.ShapeDtypeStruct((t, d), BF16), jax.ShapeDtypeStruct((1, d), F32)],
        compiler_params=_params("arbitrary"),
    )(dpv, dq, dk, dv, df, x, dx_res, gain, w_in_t)


MESH_IDS = pl.DeviceIdType.MESH


def _me():
    return lax.axis_index("x"), lax.axis_index("y"), lax.axis_index("c")


def _peer(x, y, c, p):
    px = 1 - x if p & 4 else x
    py = 1 - y if p & 2 else y
    pc = 1 - c if p & 1 else c
    return (px, py, pc), 4 * px + 2 * py + pc


HBM_SPEC = pl.BlockSpec(memory_space=pltpu.HBM)
SEM_SPEC = pl.BlockSpec(memory_space=pltpu.SEMAPHORE)
SPLIT_COPY = pltpu.CompilerParams(has_side_effects=pltpu.SideEffectType.DATAFLOW_SIDE_EFFECTING)
PEERS = N_DEV - 1


def _hbm(a):
    return pltpu.with_memory_space_constraint(a, pltpu.HBM)


def _row_block(ref, dev, rows):
    return ref.at[pl.ds(pl.multiple_of(dev * rows, BF16_ROWS), rows)]


def _copy_ends(gather, src, land, me, peer_id):
    if gather:
        rows = src.shape[0]
        return src, _row_block(land, me, rows), _row_block(land, peer_id, rows), src, _row_block(land, me, rows)
    rows = src.shape[0] // N_DEV
    return (_row_block(src, peer_id, rows), land.at[me], land.at[peer_id], _row_block(src, me, rows), land.at[me])


def _land_shape(gather, s):
    return (N_DEV * s.shape[0], s.shape[1]) if gather else (N_DEV, s.shape[0] // N_DEV, s.shape[1])


SIBLING = 1
SAME_CORE_PEERS = (2, 4, 6)
RELAYS = len(SAME_CORE_PEERS)


def _copies_start(groups, gather, name, after=None, relayed=()):
    flat = [s for g in groups for s in g]
    n, ng = len(flat), len(groups)
    lands = [lax.empty(_land_shape(gather, s), s.dtype) for s in flat]
    n_in = 2 * n + (after is not None)

    def body(*refs):
        ins, lnd = refs[:n], refs[n:2 * n]
        sems = refs[n_in:n_in + 2 * ng]
        token = refs[-1]
        x, y, c = _me()
        me = 4 * x + 2 * y + c
        w = 0
        for gi, g in enumerate(groups):
            for k in range(len(g)):
                for p in ((SIBLING,) + SAME_CORE_PEERS if gi in relayed else range(1, N_DEV)):
                    peer, peer_id = _peer(x, y, c, p)
                    src, dst, _, _, _ = _copy_ends(gather, ins[w], lnd[w], me, peer_id)
                    pltpu.make_async_remote_copy(src, dst, sems[2 * gi].at[k * PEERS + p - 1],
                                                 sems[2 * gi + 1].at[k * PEERS + p - 1], device_id=peer,
                                                 device_id_type=MESH_IDS).start()
                w += 1
        token[...] = jnp.zeros_like(token)

    sem_shapes = []
    for g in groups:
        sem_shapes += [pltpu.SemaphoreType.DMA((len(g) * PEERS,))] * 2
    out = pl.pallas_call(
        body, name=name,
        out_shape=(*sem_shapes, *[pltpu.HBM(s.shape, s.dtype) for s in flat],
                   *[pltpu.HBM(l.shape, l.dtype) for l in lands], jax.ShapeDtypeStruct((8, LANES), F32)),
        in_specs=[HBM_SPEC] * (2 * n) + [pl.BlockSpec(memory_space=pl.ANY)] * (after is not None),
        out_specs=(*[SEM_SPEC] * (2 * ng), *[HBM_SPEC] * (2 * n), pl.BlockSpec(memory_space=pltpu.VMEM)),
        input_output_aliases={i: 2 * ng + i for i in range(2 * n)},
        compiler_params=SPLIT_COPY,
    )(*[_hbm(s) for s in flat], *[_hbm(l) for l in lands], *([after] if after is not None else []))
    sems, thru, token = out[:2 * ng], out[2 * ng:2 * ng + 2 * n], out[-1]
    res, w = [], 0
    for gi, g in enumerate(groups):
        res.append((sems[2 * gi], sems[2 * gi + 1], list(thru[w:w + len(g)]), list(thru[n + w:n + w + len(g)])))
        w += len(g)
    return res, token


def _copies_wait(started, gather, after, name):
    send, recv, srcs, lands = started
    n = len(srcs)
    after = list(after) if isinstance(after, (list, tuple)) else [after]

    own_shapes = [s.shape if gather else (s.shape[0] // N_DEV, s.shape[1]) for s in srcs]

    def body(*refs):
        ins, lnd = refs[:n], refs[n:2 * n]
        send_sems, recv_sems = refs[2 * n], refs[2 * n + 1]
        bounce, in_sems, out_sems = refs[-n - 2:-2], refs[-2], refs[-1]
        x, y, c = _me()
        me = 4 * x + 2 * y + c
        ends = [_copy_ends(gather, ins[w], lnd[w], me, me)[3:] for w in range(n)]
        loads = [pltpu.make_async_copy(ends[w][0], bounce[w], in_sems.at[w]) for w in range(n)]
        stores = [pltpu.make_async_copy(bounce[w], ends[w][1], out_sems.at[w]) for w in range(n)]
        for cp in loads:
            cp.start()
        for w in range(n):
            loads[w].wait()
            stores[w].start()
        for w in range(n):
            for p in range(1, N_DEV):
                peer, peer_id = _peer(x, y, c, p)
                src, _, arrival, _, _ = _copy_ends(gather, ins[w], lnd[w], me, peer_id)
                cp = pltpu.make_async_remote_copy(src, arrival, send_sems.at[w * PEERS + p - 1],
                                                  recv_sems.at[w * PEERS + p - 1], device_id=peer,
                                                  device_id_type=MESH_IDS)
                cp.wait_send()
                cp.wait_recv()
        for cp in stores:
            cp.wait()

    out = pl.pallas_call(
        body, name=name,
        out_shape=(*[pltpu.HBM(s.shape, s.dtype) for s in srcs], *[pltpu.HBM(l.shape, l.dtype) for l in lands]),
        in_specs=[HBM_SPEC] * (2 * n) + [SEM_SPEC, SEM_SPEC] + [pl.BlockSpec(memory_space=pl.ANY)] * len(after),
        out_specs=[HBM_SPEC] * (2 * n),
        input_output_aliases={i: i for i in range(2 * n)},
        scratch_shapes=[*[pltpu.VMEM(shape, s.dtype) for shape, s in zip(own_shapes, srcs)],
                        pltpu.SemaphoreType.DMA((n,)), pltpu.SemaphoreType.DMA((n,))],
        compiler_params=SPLIT_COPY,
    )(*srcs, *lands, send, recv, *after)
    return list(out[n:])


def _relay_to_sibling(started, name, after=None):
    send, recv, srcs, lands = started
    n = len(srcs)
    after = [] if after is None else [after]

    def body(*refs):
        ins, lnd = refs[:n], refs[n:2 * n]
        send_sems, recv_sems = refs[2 * n], refs[2 * n + 1]
        relay_send, relay_recv = refs[2 * n + 2 + len(after)], refs[2 * n + 3 + len(after)]
        x, y, c = _me()
        sibling, _ = _peer(x, y, c, SIBLING)
        for w in range(n):
            rows = ins[w].shape[0]
            for k, p in enumerate(SAME_CORE_PEERS):
                peer, peer_id = _peer(x, y, c, p)
                arrived = _row_block(lnd[w], peer_id, rows)
                first = pltpu.make_async_remote_copy(ins[w], arrived, send_sems.at[w * PEERS + p - 1],
                                                     recv_sems.at[w * PEERS + p - 1], device_id=peer,
                                                     device_id_type=MESH_IDS)
                first.wait_recv()
                pltpu.make_async_remote_copy(arrived, arrived, relay_send.at[w * RELAYS + k],
                                             relay_recv.at[w * RELAYS + k], device_id=sibling,
                                             device_id_type=MESH_IDS).start()
                first.wait_send()

    sems = pltpu.SemaphoreType.DMA((n * RELAYS,))
    out = pl.pallas_call(
        body, name=name,
        out_shape=(sems, sems, *[pltpu.HBM(s.shape, s.dtype) for s in srcs], *[pltpu.HBM(l.shape, l.dtype) for l in lands]),
        in_specs=[HBM_SPEC] * (2 * n) + [SEM_SPEC, SEM_SPEC] + [pl.BlockSpec(memory_space=pl.ANY)] * len(after),
        out_specs=(SEM_SPEC, SEM_SPEC, *[HBM_SPEC] * (2 * n)),
        input_output_aliases={i: 2 + i for i in range(2 * n)},
        compiler_params=SPLIT_COPY,
    )(*srcs, *lands, send, recv, *after)
    return send, recv, out[0], out[1], list(out[2:2 + n]), list(out[2 + n:])


def _relayed_wait(relayed, after, name):
    send, recv, relay_send, relay_recv, srcs, lands = relayed
    n = len(srcs)
    after = list(after) if isinstance(after, (list, tuple)) else [after]

    def body(*refs):
        ins, lnd = refs[:n], refs[n:2 * n]
        send_sems, recv_sems, relay_send_sems, relay_recv_sems = refs[2 * n:2 * n + 4]
        bounce, in_sems, out_sems = refs[-n - 2:-2], refs[-2], refs[-1]
        x, y, c = _me()
        me = 4 * x + 2 * y + c
        sibling, sibling_id = _peer(x, y, c, SIBLING)
        loads = [pltpu.make_async_copy(ins[w], bounce[w], in_sems.at[w]) for w in range(n)]
        stores = [pltpu.make_async_copy(bounce[w], _row_block(lnd[w], me, ins[w].shape[0]), out_sems.at[w])
                  for w in range(n)]
        for cp in loads:
            cp.start()
        for w in range(n):
            loads[w].wait()
            stores[w].start()
        for w in range(n):
            rows = ins[w].shape[0]
            direct = pltpu.make_async_remote_copy(ins[w], _row_block(lnd[w], sibling_id, rows),
                                                  send_sems.at[w * PEERS + SIBLING - 1],
                                                  recv_sems.at[w * PEERS + SIBLING - 1], device_id=sibling,
                                                  device_id_type=MESH_IDS)
            direct.wait_send()
            direct.wait_recv()
            for k, p in enumerate(SAME_CORE_PEERS):
                _, sent_id = _peer(x, y, c, p)
                _, got_id = _peer(x, y, c, p + SIBLING)
                relay = pltpu.make_async_remote_copy(_row_block(lnd[w], sent_id, rows), _row_block(lnd[w], got_id, rows),
                                                     relay_send_sems.at[w * RELAYS + k],
                                                     relay_recv_sems.at[w * RELAYS + k], device_id=sibling,
                                                     device_id_type=MESH_IDS)
                relay.wait_send()
                relay.wait_recv()
        for cp in stores:
            cp.wait()

    out = pl.pallas_call(
        body, name=name,
        out_shape=(*[pltpu.HBM(s.shape, s.dtype) for s in srcs], *[pltpu.HBM(l.shape, l.dtype) for l in lands]),
        in_specs=[HBM_SPEC] * (2 * n) + [SEM_SPEC] * 4 + [pl.BlockSpec(memory_space=pl.ANY)] * len(after),
        out_specs=[HBM_SPEC] * (2 * n),
        input_output_aliases={i: i for i in range(2 * n)},
        scratch_shapes=[*[pltpu.VMEM(s.shape, s.dtype) for s in srcs],
                        pltpu.SemaphoreType.DMA((n,)), pltpu.SemaphoreType.DMA((n,))],
        compiler_params=SPLIT_COPY,
    )(*srcs, *lands, send, recv, relay_send, relay_recv, *after)
    return list(out[n:])


def _adamw_update(w, g, m, v):
    nm = ADAM_B1 * m + (1.0 - ADAM_B1) * g
    nv = ADAM_B2 * v + (1.0 - ADAM_B2) * (g * g)
    m_hat = nm / (1.0 - ADAM_B1 ** ADAM_STEP)
    v_hat = nv / (1.0 - ADAM_B2 ** ADAM_STEP)
    return -ADAM_LR * (m_hat / (jnp.sqrt(v_hat) + ADAM_EPS) + ADAM_WD * w), nm, nv


SUM_ADAMW_COLS = 512


def _sum_adamw(parts, w, m, v, name):
    _, rows, d = parts.shape
    n = w.shape[0]
    tc = SUM_ADAMW_COLS

    def body(p_ref, w_ref, m_ref, v_ref, g_ref, d_ref, nm_ref, nv_ref):
        g = p_ref[0].astype(F32)
        for dev in range(1, N_DEV):
            g = g + p_ref[dev].astype(F32)
        g = g[:n]
        g_ref[...] = g
        d_ref[...], nm_ref[...], nv_ref[...] = _adamw_update(w_ref[...], g, m_ref[...], v_ref[...])

    spec = pl.BlockSpec((n, tc), lambda j: (0, j))
    shape = jax.ShapeDtypeStruct((n, d), F32)
    return pl.pallas_call(
        body, name=name, grid=(d // tc,),
        in_specs=[pl.BlockSpec((N_DEV, rows, tc), lambda j: (0, 0, j)), spec, spec, spec],
        out_specs=[spec] * 4, out_shape=[shape] * 4,
        compiler_params=_params("arbitrary"),
    )(parts, w, m, v)


def _pad_rows(a, rows):
    return jnp.pad(a, ((0, rows - a.shape[0]), (0, 0)))


def _row1(vec, width=D_MODEL):
    return jnp.pad(vec.reshape(1, -1), ((0, 0), (0, width - vec.shape[-1])))


COLUMN_SHARDED = ("ffn1_w_gate", "ffn1_w_up", "w_in", "ffn2_w_gate", "ffn2_w_up")
VEC_NAMES = ("ffn1_norm", "mix_norm", "ffn2_norm", "b_forget", "pool_scale", "q_norm", "k_norm", "out_norm_pool",
             "out_norm_attn")
VEC_ROWS = 16
LOSS_ROW = len(VEC_NAMES)


def _pack_vector_grads(parts, loss_part, name):
    names = [n for n in VEC_NAMES if n in parts]
    extra = [] if loss_part is None else [loss_part]

    def body(*refs):
        out_ref = refs[-1]
        out_ref[...] = jnp.zeros_like(out_ref)
        lane = lax.broadcasted_iota(jnp.int32, (1, LANES), 1)
        for n, ref in zip(names, refs):
            val = ref[...]
            if n in ("q_norm", "k_norm"):
                val = val[:, 0:LANES] + val[:, LANES:2 * LANES] + val[:, 2 * LANES:3 * LANES] + val[:, 3 * LANES:]
                val = jnp.where(lane < HEAD_DIM, val + pltpu.roll(val, HEAD_DIM, 1), 0.0)
            out_ref[pl.ds(VEC_NAMES.index(n), 1), pl.ds(0, val.shape[1])] = val
        if extra:
            out_ref[pl.ds(LOSS_ROW, 1), pl.ds(0, 1)] = refs[len(names)][...]

    vmem = pl.BlockSpec(memory_space=pltpu.VMEM)
    return pl.pallas_call(
        body, name=name, in_specs=[vmem] * (len(names) + len(extra)), out_specs=vmem,
        out_shape=jax.ShapeDtypeStruct((VEC_ROWS, D_MODEL), F32),
    )(*[parts[n] for n in names], *extra)


def _small_adamw(vec_alls, pool_all, vec_params, pool_params):
    nv = len(vec_params)
    na = len(vec_alls)
    pool_rows = pool_params[0].shape[0]

    def body(*refs):
        vec_refs, pool_ref = refs[:na], refs[na]
        ins = refs[na + 1:na + 1 + 3 * nv + 3]
        outs = refs[na + 1 + 3 * nv + 3:-1]
        rows = refs[-1]
        total = jnp.zeros((VEC_ROWS, D_MODEL), F32)
        for vec_ref in vec_refs:
            for dev in range(N_DEV):
                total = total + vec_ref[pl.ds(dev * VEC_ROWS, VEC_ROWS), :]
        rows[...] = total
        outs[4 * nv + 4][...] = rows[pl.ds(LOSS_ROW, 1), pl.ds(0, 1)]
        for i in range(nv):
            w_ref, m_ref, v_ref = ins[3 * i:3 * i + 3]
            g = rows[pl.ds(i, 1), pl.ds(0, w_ref.shape[1])]
            outs[4 * i][...] = g
            outs[4 * i + 1][...], outs[4 * i + 2][...], outs[4 * i + 3][...] = _adamw_update(
                w_ref[...], g, m_ref[...], v_ref[...])
        g = pool_ref[pl.ds(0, pool_rows), :].astype(F32)
        for dev in range(1, N_DEV):
            g = g + pool_ref[pl.ds(dev * pool_rows, pool_rows), :].astype(F32)
        w_ref, m_ref, v_ref = ins[3 * nv:]
        outs[4 * nv][...] = g
        outs[4 * nv + 1][...], outs[4 * nv + 2][...], outs[4 * nv + 3][...] = _adamw_update(
            w_ref[...], g, m_ref[...], v_ref[...])

    vmem = pl.BlockSpec(memory_space=pltpu.VMEM)
    flat = [a for trio in vec_params for a in trio] + list(pool_params)
    out_shape = []
    for trio in list(vec_params) + [pool_params]:
        out_shape += [jax.ShapeDtypeStruct(trio[0].shape, F32)] * 4
    out_shape.append(jax.ShapeDtypeStruct((1, 1), F32))
    return pl.pallas_call(
        body, name="adamw_small", in_specs=[vmem] * (na + 1 + len(flat)), out_specs=[vmem] * len(out_shape),
        out_shape=out_shape, scratch_shapes=[pltpu.VMEM((VEC_ROWS, D_MODEL), F32)],
    )(*vec_alls, pool_all, *flat)


def kernel(x, ffn1_norm, ffn1_w_gate, ffn1_w_up, ffn1_w_down, mix_norm, w_in, b_forget, pool_w, pool_scale, q_norm, k_norm, out_norm_pool, out_norm_attn, w_out, ffn2_norm, ffn2_w_gate, ffn2_w_up, ffn2_w_down, loss_target, m_ffn1_norm, m_ffn1_w_gate, m_ffn1_w_up, m_ffn1_w_down, m_mix_norm, m_w_in, m_b_forget, m_pool_w, m_pool_scale, m_q_norm, m_k_norm, m_out_norm_pool, m_out_norm_attn, m_w_out, m_ffn2_norm, m_ffn2_w_gate, m_ffn2_w_up, m_ffn2_w_down, v_ffn1_norm, v_ffn1_w_gate, v_ffn1_w_up, v_ffn1_w_down, v_mix_norm, v_w_in, v_b_forget, v_pool_w, v_pool_scale, v_q_norm, v_k_norm, v_out_norm_pool, v_out_norm_attn, v_w_out, v_ffn2_norm, v_ffn2_w_gate, v_ffn2_w_up, v_ffn2_w_down):
    bsz, seq, d = x.shape
    t = bsz * seq
    x0 = x.reshape(t, d)
    target = loss_target.reshape(t, d)
    in_rows = -(-w_in.shape[1] // BF16_ROWS) * BF16_ROWS

    slabs = [s.astype(BF16) for s in (ffn1_w_gate.T, ffn1_w_up.T, ffn1_w_down, _pad_rows(w_in.T, in_rows), w_out,
                                       ffn2_w_gate.T, ffn2_w_up.T, ffn2_w_down)]
    gathers, started = _copies_start([slabs[0:2], slabs[2:3], slabs[3:4], slabs[4:5], slabs[5:8]], True, "gather_start",
                                     relayed=(0, 4))

    g1, gm, g2 = ffn1_norm.reshape(1, d), mix_norm.reshape(1, d), ffn2_norm.reshape(1, d)
    bf_row = _row1(b_forget, LANES)
    gq = jnp.tile(q_norm, N_HEADS).reshape(1, ATTN_WIDTH)
    gk = jnp.tile(k_norm, N_HEADS).reshape(1, ATTN_WIDTH)
    scale_row = pool_scale.reshape(1, POOL_WIDTH)
    gp, ga = out_norm_pool.reshape(1, POOL_WIDTH), out_norm_attn.reshape(1, ATTN_WIDTH)

    wg1, wu1 = _relayed_wait(_relay_to_sibling(gathers[0], "gather_relay_ffn1_up"), started, "gather_wait_ffn1_up")
    h1, sa1, sb1, s1 = _ffn_up(x0, g1, wg1, wu1, "ffn1_up")
    (wd1,) = _copies_wait(gathers[1], True, s1, "gather_wait_ffn1_down")
    (x1,) = _ffn_down(s1, wd1, x0, None, "ffn1_down")
    (win_g,) = _copies_wait(gathers[2], True, x1, "gather_wait_w_in")
    win_t = _repack_rows(win_g, in_rows, w_in.shape[1], N_DEV, "w_in_rows")
    hm, pv, q, k, v, f = _mix_in_fwd(x1, gm, win_t)
    pooled, mixed, y_pool = _pool_fwd(pv, pool_w, scale_row, gp, bsz, seq)
    qp, kp = _attn_prep_fwd(q, k, f, bf_row, gq, gk, bsz, seq)
    o, lse = _flash_fwd(qp, kp, v, bsz, seq)
    relayed_ffn2 = _relay_to_sibling(gathers[4], "gather_relay_ffn2", o)
    (wout,) = _copies_wait(gathers[3], True, [o, relayed_ffn2[4][0]], "gather_wait_w_out")
    ycat, x2 = _mix_out_fwd(o, y_pool, x1, ga, wout)
    wg2, wu2, wd2 = _relayed_wait(relayed_ffn2, x2, "gather_wait_ffn2")
    h2, sa2, sb2, s2 = _ffn_up(x2, g2, wg2, wu2, "ffn2_up")
    dx3, dyh2, loss_part = _ffn_down(s2, wd2, x2, target, "ffn2_down")

    da2, db2, dwg2, dwu2 = _ffn_bwd_act(dyh2, sa2, sb2, h2, wd2, "ffn2_bwd_act")
    (dwd2,) = _wgrad([s2], dyh2, "ffn2_down_wgrad")
    (sent_ffn2,), tok = _copies_start([[dwg2, dwu2, dwd2]], False, "exchange_start_ffn2")
    dx2, dg2 = _ffn_bwd_dx(da2, db2, dx3, x2, g2 + tok[0, 0], wg2, wu2, "ffn2_bwd_dx")
    dx2b, dy_pool, do, dga = _mix_out_bwd(dx2, o, ga, wout)
    (dwout,) = _wgrad([ycat], dx2b, "w_out_wgrad")
    (sent_out,), tok = _copies_start([[dwout]], False, "exchange_start_w_out")
    dqp, dkp, dv = _flash_bwd(qp, kp, v, o, do, lse, bsz, seq)
    dq, dk, df, dgq, dgk, dbf = _attn_prep_bwd(dqp, dkp, q, k, f, bf_row + tok[0, 0], gq, gk, bsz, seq)
    dpv, dpool_w, dscale, dgp = _pool_bwd(dy_pool, mixed, pooled, pool_w, scale_row, gp, bsz, seq)
    dhcat, dx1, dyh1, dgm = _mix_in_bwd(dpv, dq, dk, dv, df, x1, dx2, gm, win_t)
    (dwin,) = _wgrad([dhcat], hm, "w_in_wgrad")
    dwin_blocks = _repack_rows(dwin, w_in.shape[1], in_rows, N_DEV, "w_in_grad_blocks")
    (sent_in,), tok = _copies_start([[dwin_blocks]], False, "exchange_start_w_in")
    (dwd1,) = _wgrad([s1], dyh1, "ffn1_down_wgrad")
    (sent_down1,), tok = _copies_start([[dwd1]], False, "exchange_start_ffn1_down", after=tok)
    da1, db1, dwg1, dwu1 = _ffn_bwd_act(dyh1, sa1, sb1, h1, wd1, "ffn1_bwd_act")
    (sent_up1,), tok = _copies_start([[dwg1, dwu1]], False, "exchange_start_ffn1_up", after=tok)
    dx0, dg1 = _ffn_bwd_dx(da1, db1, dx1, x0, g1 + tok[0, 0], wg1, wu1, "ffn1_bwd_dx")

    pool_rows = POOL_GROUPS * POOL_GROUP_DIM
    packed = _pack_vector_grads(dict(ffn1_norm=dg1, mix_norm=dgm, ffn2_norm=dg2, b_forget=dbf, pool_scale=dscale,
                                     q_norm=dgq, k_norm=dgk, out_norm_pool=dgp, out_norm_attn=dga), loss_part,
                                "pack_vector_grads")
    pool_part = dpool_w.reshape(pool_rows, POOL_GROUP_DIM).astype(BF16)
    (sent_small,), tok = _copies_start([[packed, pool_part]], True, "small_grads_start")

    weights = dict(ffn1_norm=ffn1_norm, ffn1_w_gate=ffn1_w_gate, ffn1_w_up=ffn1_w_up, ffn1_w_down=ffn1_w_down,
                   mix_norm=mix_norm, w_in=w_in, b_forget=b_forget, pool_w=pool_w, pool_scale=pool_scale,
                   q_norm=q_norm, k_norm=k_norm, out_norm_pool=out_norm_pool, out_norm_attn=out_norm_attn,
                   w_out=w_out, ffn2_norm=ffn2_norm, ffn2_w_gate=ffn2_w_gate, ffn2_w_up=ffn2_w_up,
                   ffn2_w_down=ffn2_w_down)
    m_in = dict(ffn1_norm=m_ffn1_norm, ffn1_w_gate=m_ffn1_w_gate, ffn1_w_up=m_ffn1_w_up, ffn1_w_down=m_ffn1_w_down,
                mix_norm=m_mix_norm, w_in=m_w_in, b_forget=m_b_forget, pool_w=m_pool_w, pool_scale=m_pool_scale,
                q_norm=m_q_norm, k_norm=m_k_norm, out_norm_pool=m_out_norm_pool, out_norm_attn=m_out_norm_attn,
                w_out=m_w_out, ffn2_norm=m_ffn2_norm, ffn2_w_gate=m_ffn2_w_gate, ffn2_w_up=m_ffn2_w_up,
                ffn2_w_down=m_ffn2_w_down)
    v_in = dict(ffn1_norm=v_ffn1_norm, ffn1_w_gate=v_ffn1_w_gate, ffn1_w_up=v_ffn1_w_up, ffn1_w_down=v_ffn1_w_down,
                mix_norm=v_mix_norm, w_in=v_w_in, b_forget=v_b_forget, pool_w=v_pool_w, pool_scale=v_pool_scale,
                q_norm=v_q_norm, k_norm=v_k_norm, out_norm_pool=v_out_norm_pool, out_norm_attn=v_out_norm_attn,
                w_out=v_w_out, ffn2_norm=v_ffn2_norm, ffn2_w_gate=v_ffn2_w_gate, ffn2_w_up=v_ffn2_w_up,
                ffn2_w_down=v_ffn2_w_down)
    grads, delta, new_m, new_v = {}, {}, {}, {}
    after = [tok]
    plan = ((sent_ffn2, "ffn2", ("ffn2_w_gate", "ffn2_w_up", "ffn2_w_down")), (sent_out, "w_out", ("w_out",)),
            (sent_in, "w_in", ("w_in",)), (sent_down1, "ffn1_down", ("ffn1_w_down",)),
            (sent_up1, "ffn1_up", ("ffn1_w_gate", "ffn1_w_up")))
    for sent, tag, names in plan:
        parts = _copies_wait(sent, False, after, f"exchange_wait_{tag}")
        after = []
        for n, part in zip(names, parts):
            turn = (lambda a: a.T) if n in COLUMN_SHARDED else (lambda a: a)
            done = _sum_adamw(part, turn(weights[n]), turn(m_in[n]), turn(v_in[n]), f"adamw_{n}")
            grads[n], delta[n], new_m[n], new_v[n] = (turn(a) for a in done)
            after.append(done[3])
    vec_all, pool_all = _copies_wait(sent_small, True, after, "small_grads_wait")
    as_row = lambda a: a.reshape(1, -1)
    as_pool = lambda a: a.reshape(pool_rows, POOL_GROUP_DIM)
    small = _small_adamw([vec_all], pool_all,
                         [tuple(as_row(z[n]) for z in (weights, m_in, v_in)) for n in VEC_NAMES],
                         tuple(as_pool(z["pool_w"]) for z in (weights, m_in, v_in)))
    for i, n in enumerate(VEC_NAMES + ("pool_w",)):
        grads[n], delta[n], new_m[n], new_v[n] = (a.reshape(weights[n].shape) for a in small[4 * i:4 * i + 4])
    loss = small[-1].reshape(())

    order = ("ffn1_norm", "ffn1_w_gate", "ffn1_w_up", "ffn1_w_down", "mix_norm", "w_in", "b_forget", "pool_w",
             "pool_scale", "q_norm", "k_norm", "out_norm_pool", "out_norm_attn", "w_out", "ffn2_norm", "ffn2_w_gate",
             "ffn2_w_up", "ffn2_w_down")
    return (loss, dx0.reshape(bsz, seq, d), *[grads[n] for n in order], *[delta[n] for n in order],
            *[new_m[n] for n in order], *[new_v[n] for n in order])
```

```python
import jax
import jax.numpy as jnp
from jax import lax
from jax.experimental import pallas as pl
from jax.experimental.pallas import tpu as pltpu

F32 = jnp.float32
BF16 = jnp.bfloat16

EPS = 1e-6
D_MODEL = 1024
N_HEADS = 8
HEAD_DIM = 64
POOL_WIDTH = 512
ATTN_WIDTH = 512
POOL_GROUPS = 4
POOL_GROUP_DIM = 128
POOL_WINDOWS = (2, 4, 8, 16)
POOL_HALO = 16
MIX_PAD = POOL_WIDTH + 3 * ATTN_WIDTH + 128
N_DEV = 8
BF16_ROWS = 16
LANES = 128
VMEM_LIMIT = 56 * 1024 * 1024

ADAM_LR = 0.001
ADAM_B1 = 0.9
ADAM_B2 = 0.999
ADAM_EPS = 1e-08
ADAM_WD = 0.01
ADAM_STEP = 10


def _params(*sem):
    return pltpu.CompilerParams(dimension_semantics=sem, vmem_limit_bytes=VMEM_LIMIT)


def _dot(a, b):
    return jnp.dot(a, b, preferred_element_type=F32)


def _dot_nt(a, b):
    return lax.dot_general(a, b, (((1,), (1,)), ((), ())), preferred_element_type=F32)


def _dot_tn(a, b):
    return lax.dot_general(a, b, (((0,), (0,)), ((), ())), preferred_element_type=F32)


def _resident(shape):
    return pl.BlockSpec(shape, lambda *_: (0,) * len(shape), pipeline_mode=pl.Buffered(1))


def _rows(tm, width):
    return pl.BlockSpec((tm, width), lambda i: (i, 0))


def _rms_scale(x):
    return lax.rsqrt(jnp.mean(x * x, axis=-1, keepdims=True) + EPS)


def _rms_bwd(dh, x, gain):
    r = _rms_scale(x)
    n = x * r
    dgain = jnp.sum(dh * n, axis=0, keepdims=True)
    dn = dh * gain
    dx = r * (dn - n * jnp.mean(dn * n, axis=-1, keepdims=True))
    return dx, dgain


def _split3(x):
    hi = x.astype(BF16)
    r1 = x - hi.astype(F32)
    mid = r1.astype(BF16)
    lo = (r1 - mid.astype(F32)).astype(BF16)
    return hi, mid, lo


def _split2(x):
    hi = x.astype(BF16)
    return hi, (x - hi.astype(F32)).astype(BF16)


FF_CHUNK = 256


def _swiglu_parts(a, b):
    sig = jax.nn.sigmoid(a)
    silu = a * sig
    return (b * (sig + silu * (1.0 - sig))).astype(BF16), silu.astype(BF16), (silu * b).astype(BF16)


def _ffn_up(x, gain, wg_t, wu_t, name):
    t, d = x.shape
    f = wg_t.shape[0]
    tm = 512

    def body(x_ref, g_ref, wg_ref, wu_ref, h_ref, sa_ref, sb_ref, s_ref):
        xv = x_ref[...]
        h = (xv * _rms_scale(xv) * g_ref[...]).astype(BF16)
        h_ref[...] = h
        for c in range(f // FF_CHUNK):
            sl = pl.ds(c * FF_CHUNK, FF_CHUNK)
            sa_ref[:, sl], sb_ref[:, sl], s_ref[:, sl] = _swiglu_parts(_dot_nt(h, wg_ref[sl, :]), _dot_nt(h, wu_ref[sl, :]))

    wide = jax.ShapeDtypeStruct((t, f), BF16)
    return pl.pallas_call(
        body, name=name, grid=(t // tm,),
        in_specs=[_rows(tm, d), _resident((1, d)), _resident((f, d)), _resident((f, d))],
        out_specs=[_rows(tm, d), _rows(tm, f), _rows(tm, f), _rows(tm, f)],
        out_shape=[jax.ShapeDtypeStruct((t, d), BF16), wide, wide, wide],
        compiler_params=_params("arbitrary"),
    )(x, gain, wg_t, wu_t)


def _ffn_down(s, wd, x, target, name):
    t, d = x.shape
    f = wd.shape[0]
    tm = 512
    with_loss = target is not None

    def body(*refs):
        if with_loss:
            s_ref, w_ref, x_ref, t_ref, dy_ref, dyh_ref, loss_ref = refs
        else:
            s_ref, w_ref, x_ref, y_ref = refs
        y = x_ref[...] + 0.5 * _dot(s_ref[...], w_ref[...])
        if with_loss:
            e = y - t_ref[...]
            dy = e * (1.0 / d)
            dy_ref[...] = dy
            dyh_ref[...] = (0.5 * dy).astype(BF16)

            @pl.when(pl.program_id(0) == 0)
            def _():
                loss_ref[...] = jnp.zeros_like(loss_ref)

            part = jnp.sum(jnp.sum(e * e, axis=0, keepdims=True), axis=1, keepdims=True)
            loss_ref[...] += part * (0.5 / d)
        else:
            y_ref[...] = y

    in_specs = [_rows(tm, f), _resident((f, d)), _rows(tm, d)]
    args = [s, wd, x]
    if with_loss:
        in_specs.append(_rows(tm, d))
        args.append(target)
        out_shape = [jax.ShapeDtypeStruct((t, d), F32), jax.ShapeDtypeStruct((t, d), BF16),
                     jax.ShapeDtypeStruct((1, 1), F32)]
        out_specs = [_rows(tm, d), _rows(tm, d), pl.BlockSpec((1, 1), lambda i: (0, 0))]
    else:
        out_shape = [jax.ShapeDtypeStruct((t, d), F32)]
        out_specs = [_rows(tm, d)]
    return pl.pallas_call(
        body, name=name, grid=(t // tm,), in_specs=in_specs, out_specs=out_specs, out_shape=out_shape,
        compiler_params=_params("arbitrary"),
    )(*args)


def _ffn_bwd_act(dyh, sa, sb, h, wd, name):
    t, d = dyh.shape
    f = wd.shape[0]
    tn = f // 2
    tk = 512
    nk = t // tk

    def body(dy_ref, sa_ref, sb_ref, h_ref, wd_ref, da_ref, db_ref, dwg_ref, dwu_ref, acc_g, acc_u):
        k = pl.program_id(1)

        @pl.when(k == 0)
        def _():
            acc_g[...] = jnp.zeros_like(acc_g)
            acc_u[...] = jnp.zeros_like(acc_u)

        ds = _dot_nt(dy_ref[...], wd_ref[...])
        da = (ds * sa_ref[...].astype(F32)).astype(BF16)
        db = (ds * sb_ref[...].astype(F32)).astype(BF16)
        da_ref[...] = da
        db_ref[...] = db
        hv = h_ref[...]
        acc_g[...] += _dot_tn(da, hv)
        acc_u[...] += _dot_tn(db, hv)

        @pl.when(k == nk - 1)
        def _():
            dwg_ref[...] = acc_g[...].astype(BF16)
            dwu_ref[...] = acc_u[...].astype(BF16)

    tokens = pl.BlockSpec((tk, d), lambda j, k: (k, 0))
    wide = pl.BlockSpec((tk, tn), lambda j, k: (k, j))
    weight = pl.BlockSpec((tn, d), lambda j, k: (j, 0))
    return pl.pallas_call(
        body, name=name, grid=(f // tn, nk),
        in_specs=[tokens, wide, wide, tokens, weight],
        out_specs=[wide, wide, weight, weight],
        out_shape=[jax.ShapeDtypeStruct((t, f), BF16)] * 2 + [jax.ShapeDtypeStruct((f, d), BF16)] * 2,
        scratch_shapes=[pltpu.VMEM((tn, d), F32)] * 2,
        compiler_params=_params("arbitrary", "arbitrary"),
    )(dyh, sa, sb, h, wd)


def _ffn_bwd_dx(da, db, dy, x, gain, wg_t, wu_t, name):
    t, d = x.shape
    f = wg_t.shape[0]
    tm = 512

    def body(da_ref, db_ref, dy_ref, x_ref, g_ref, wg_ref, wu_ref, dx_ref, dg_ref):
        dh = _dot(da_ref[...], wg_ref[...]) + _dot(db_ref[...], wu_ref[...])
        dx, dgain = _rms_bwd(dh, x_ref[...], g_ref[...])
        dx_ref[...] = dy_ref[...] + dx

        @pl.when(pl.program_id(0) == 0)
        def _():
            dg_ref[...] = jnp.zeros_like(dg_ref)

        dg_ref[...] += dgain

    return pl.pallas_call(
        body, name=name, grid=(t // tm,),
        in_specs=[_rows(tm, f), _rows(tm, f), _rows(tm, d), _rows(tm, d), _resident((1, d)), _resident((f, d)),
                  _resident((f, d))],
        out_specs=[_rows(tm, d), pl.BlockSpec((1, d), lambda i: (0, 0))],
        out_shape=[jax.ShapeDtypeStruct((t, d), F32), jax.ShapeDtypeStruct((1, d), F32)],
        compiler_params=_params("arbitrary"),
    )(da, db, dy, x, gain, wg_t, wu_t)


def _wgrad(lhs, b, name):
    t, n = lhs[0].shape
    d = b.shape[1]
    m = len(lhs)
    tn = n // 2 if n * d * m > (4 << 20) else n
    tk = 1024
    nk = t // tk

    def body(*refs):
        a_refs, b_ref, o_refs, accs = refs[:m], refs[m], refs[m + 1:2 * m + 1], refs[2 * m + 1:]
        k = pl.program_id(1)

        @pl.when(k == 0)
        def _():
            for acc in accs:
                acc[...] = jnp.zeros_like(acc)

        bv = b_ref[...]
        for a_ref, acc in zip(a_refs, accs):
            acc[...] += _dot_tn(a_ref[...], bv)

        @pl.when(k == nk - 1)
        def _():
            for o_ref, acc in zip(o_refs, accs):
                o_ref[...] = acc[...].astype(BF16)

    return pl.pallas_call(
        body, name=name, grid=(n // tn, nk),
        in_specs=[pl.BlockSpec((tk, tn), lambda j, k: (k, j))] * m + [pl.BlockSpec((tk, d), lambda j, k: (k, 0))],
        out_specs=[pl.BlockSpec((tn, d), lambda j, k: (j, 0))] * m,
        out_shape=[jax.ShapeDtypeStruct((n, d), BF16)] * m,
        scratch_shapes=[pltpu.VMEM((tn, d), F32)] * m,
        compiler_params=_params("arbitrary", "arbitrary"),
    )(*lhs, b)


def _repack_rows(a, rows_in, rows_out, blocks, name):
    total, d = a.shape
    real = min(rows_in, rows_out)

    def body(a_ref, o_ref, wide_in, wide_out):
        wide_in[...] = a_ref[...].astype(F32)
        wide_out[...] = jnp.zeros_like(wide_out)
        for j in range(blocks):
            wide_out[pl.ds(j * rows_out, real), :] = wide_in[pl.ds(j * rows_in, real), :]
        o_ref[...] = wide_out[...].astype(BF16)

    full = pl.BlockSpec((total, d), lambda i: (0, 0))
    return pl.pallas_call(
        body, name=name, grid=(1,), in_specs=[full], out_specs=full, out_shape=jax.ShapeDtypeStruct((total, d), BF16),
        scratch_shapes=[pltpu.VMEM((total, d), F32)] * 2,
        compiler_params=_params("arbitrary"),
    )(a)


def _mix_in_fwd(x, gain, w_in_t):
    t, d = x.shape
    tm = 1024
    pw, aw = POOL_WIDTH, ATTN_WIDTH

    def body(x_ref, g_ref, w_ref, hm_ref, pv_ref, q_ref, k_ref, v_ref, f_ref):
        xv = x_ref[...]
        hm = (xv * _rms_scale(xv) * g_ref[...]).astype(BF16)
        hm_ref[...] = hm
        pv_ref[...] = _dot_nt(hm, w_ref[pl.ds(0, pw), :])
        q_ref[...] = _dot_nt(hm, w_ref[pl.ds(pw, aw), :])
        k_ref[...] = _dot_nt(hm, w_ref[pl.ds(pw + aw, aw), :])
        v_ref[...] = _dot_nt(hm, w_ref[pl.ds(pw + 2 * aw, aw), :]).astype(BF16)
        f_ref[...] = _dot_nt(hm, w_ref[pl.ds(pw + 3 * aw, LANES), :])

    return pl.pallas_call(
        body, name="mix_in_fwd", grid=(t // tm,),
        in_specs=[_rows(tm, d), _resident((1, d)), _resident((MIX_PAD, d))],
        out_specs=[_rows(tm, d), _rows(tm, pw), _rows(tm, aw), _rows(tm, aw), _rows(tm, aw), _rows(tm, LANES)],
        out_shape=[jax.ShapeDtypeStruct((t, d), BF16), jax.ShapeDtypeStruct((t, pw), F32),
                   jax.ShapeDtypeStruct((t, aw), F32), jax.ShapeDtypeStruct((t, aw), F32),
                   jax.ShapeDtypeStruct((t, aw), BF16), jax.ShapeDtypeStruct((t, LANES), F32)],
        compiler_params=_params("arbitrary"),
    )(x, gain, w_in_t)


def _pool_fwd(pv, pool_w, pool_scale, gain, bsz, seq):
    ts = 512
    ns = seq // ts
    pw = POOL_WIDTH

    def body(pv_ref, w_ref, sc_ref, g_ref, pooled_ref, mixed_ref, y_ref, ext):
        s = pl.program_id(1)

        @pl.when(s == 0)
        def _():
            ext[pl.ds(0, POOL_HALO), :] = jnp.zeros((POOL_HALO, pw), F32)

        p = pv_ref[...]
        ext[pl.ds(POOL_HALO, ts), :] = p
        pos = s * ts + lax.broadcasted_iota(jnp.int32, (ts, 1), 0)
        parts = []
        for g, w in enumerate(POOL_WINDOWS):
            lanes = pl.ds(g * POOL_GROUP_DIM, POOL_GROUP_DIM)
            win = ext[pl.ds(POOL_HALO, ts), lanes]
            for i in range(1, w):
                win = win + ext[pl.ds(POOL_HALO - i, ts), lanes]
            cnt = jnp.minimum(pos + 1, w).astype(F32)
            pooled = (win / cnt - ext[pl.ds(POOL_HALO, ts), lanes]).astype(BF16)
            pooled_ref[:, lanes] = pooled
            parts.append(_dot(pooled, w_ref[g].astype(BF16)))
        mixed = jnp.concatenate(parts, axis=1)
        mixed_ref[...] = mixed
        pm = mixed * sc_ref[...]
        y_ref[...] = (pm * _rms_scale(pm) * g_ref[...]).astype(BF16)
        ext[pl.ds(0, POOL_HALO), :] = p[ts - POOL_HALO:, :]

    blk = pl.BlockSpec((ts, pw), lambda b, s: (b * ns + s, 0))
    t = bsz * seq
    return pl.pallas_call(
        body, name="pool_fwd", grid=(bsz, ns),
        in_specs=[blk, pl.BlockSpec((POOL_GROUPS, POOL_GROUP_DIM, POOL_GROUP_DIM), lambda b, s: (0, 0, 0)),
                  pl.BlockSpec((1, pw), lambda b, s: (0, 0)), pl.BlockSpec((1, pw), lambda b, s: (0, 0))],
        out_specs=[blk, blk, blk],
        out_shape=[jax.ShapeDtypeStruct((t, pw), BF16), jax.ShapeDtypeStruct((t, pw), F32),
                   jax.ShapeDtypeStruct((t, pw), BF16)],
        scratch_shapes=[pltpu.VMEM((POOL_HALO + ts, pw), F32)],
        compiler_params=_params("arbitrary", "arbitrary"),
    )(pv, pool_w, pool_scale, gain)


def _pool_bwd(dy, mixed, pooled, pool_w, pool_scale, gain, bsz, seq):
    ts = 512
    ns = seq // ts
    pw = POOL_WIDTH

    def body(dy_ref, mixed_ref, pooled_ref, w_ref, sc_ref, g_ref, dpv_ref, dw_ref, dsc_ref, dg_ref, ext):
        b = pl.program_id(0)
        sr = pl.program_id(1)
        s = ns - 1 - sr

        @pl.when(jnp.logical_and(b == 0, sr == 0))
        def _():
            dw_ref[...] = jnp.zeros_like(dw_ref)
            dsc_ref[...] = jnp.zeros_like(dsc_ref)
            dg_ref[...] = jnp.zeros_like(dg_ref)

        @pl.when(sr == 0)
        def _():
            ext[pl.ds(ts, POOL_HALO), :] = jnp.zeros((POOL_HALO, pw), F32)

        mixed = mixed_ref[...]
        sc = sc_ref[...]
        dpm, dgain = _rms_bwd(dy_ref[...], mixed * sc, g_ref[...])
        dg_ref[...] += dgain
        dsc_ref[...] += jnp.sum(dpm * mixed, axis=0, keepdims=True)
        dmixed = (dpm * sc).astype(BF16)
        pos = s * ts + lax.broadcasted_iota(jnp.int32, (ts, 1), 0)
        dpooled = []
        for g, w in enumerate(POOL_WINDOWS):
            lanes = pl.ds(g * POOL_GROUP_DIM, POOL_GROUP_DIM)
            dm = dmixed[:, g * POOL_GROUP_DIM:(g + 1) * POOL_GROUP_DIM]
            dw_ref[g] += _dot_tn(pooled_ref[:, lanes], dm)
            dp = _dot_nt(dm, w_ref[g].astype(BF16))
            dpooled.append(dp)
            cnt = jnp.minimum(pos + 1, w).astype(F32)
            ext[pl.ds(0, ts), lanes] = dp / cnt
        for g, w in enumerate(POOL_WINDOWS):
            lanes = pl.ds(g * POOL_GROUP_DIM, POOL_GROUP_DIM)
            win = ext[pl.ds(0, ts), lanes]
            for i in range(1, w):
                win = win + ext[pl.ds(i, ts), lanes]
            dpv_ref[:, lanes] = (win - dpooled[g]).astype(BF16)
        head = ext[pl.ds(0, POOL_HALO), :]
        ext[pl.ds(ts, POOL_HALO), :] = head

    blk = pl.BlockSpec((ts, pw), lambda b, s: (b * ns + (ns - 1 - s), 0))
    vec = pl.BlockSpec((1, pw), lambda b, s: (0, 0))
    wspec = pl.BlockSpec((POOL_GROUPS, POOL_GROUP_DIM, POOL_GROUP_DIM), lambda b, s: (0, 0, 0))
    t = bsz * seq
    return pl.pallas_call(
        body, name="pool_bwd", grid=(bsz, ns),
        in_specs=[blk, blk, blk, wspec, vec, vec],
        out_specs=[blk, wspec, vec, vec],
        out_shape=[jax.ShapeDtypeStruct((t, pw), BF16),
                   jax.ShapeDtypeStruct((POOL_GROUPS, POOL_GROUP_DIM, POOL_GROUP_DIM), F32),
                   jax.ShapeDtypeStruct((1, pw), F32), jax.ShapeDtypeStruct((1, pw), F32)],
        scratch_shapes=[pltpu.VMEM((ts + POOL_HALO, pw), F32)],
        compiler_params=_params("arbitrary", "arbitrary"),
    )(dy, mixed, pooled, pool_w, pool_scale, gain)


AUX_ONE = 64
AUX_F = 67

ATTN_PREP_ROWS = 512


def _seg_ones(width, seg):
    r = lax.broadcasted_iota(jnp.int32, (width, width), 0) // seg
    c = lax.broadcasted_iota(jnp.int32, (width, width), 1) // seg
    return (r == c).astype(BF16)


def _tri_ones(n, lower):
    r = lax.broadcasted_iota(jnp.int32, (n, n), 0)
    c = lax.broadcasted_iota(jnp.int32, (n, n), 1)
    return ((r >= c) if lower else (r <= c)).astype(BF16)


def _place_pieces(first_lane):
    r = lax.broadcasted_iota(jnp.int32, (3 * LANES, N_HEADS * LANES), 0)
    c = lax.broadcasted_iota(jnp.int32, (3 * LANES, N_HEADS * LANES), 1)
    piece, head = r // LANES, r % LANES
    return jnp.logical_and(head < N_HEADS, c == head * LANES + first_lane + piece).astype(BF16)


def _head_sums(x, seg_ones):
    hi, lo = _split2(x)
    return _dot(hi, seg_ones) + _dot(lo, seg_ones)


def _log_sigmoid(x):
    return jnp.minimum(x, 0.0) - jnp.log(1.0 + jnp.exp(-jnp.abs(x)))


def _attn_prep_fwd(q, k, f, b_forget, q_gain, k_gain, bsz, seq):
    ts = ATTN_PREP_ROWS
    ns = seq // ts
    aw = ATTN_WIDTH
    t = bsz * seq
    seg = _seg_ones(aw, HEAD_DIM)
    tri = _tri_ones(ts, True)

    def body(q_ref, k_ref, f_ref, bf_ref, gq_ref, gk_ref, seg_ref, tri_ref, pq_ref, pk_ref, qp_ref, kp_ref, carry):
        s = pl.program_id(1)

        @pl.when(s == 0)
        def _():
            carry[...] = jnp.zeros_like(carry)

        logf = _log_sigmoid(f_ref[...] + bf_ref[...])
        hi, mid, lo = _split3(logf)
        tri_v = tri_ref[...]
        fc = _dot(tri_v, hi) + _dot(tri_v, mid) + _dot(tri_v, lo) + carry[pl.ds(0, 1), :]
        carry[pl.ds(0, 1), :] = fc[ts - 1:, :]
        pcs = jnp.concatenate(_split3(fc), axis=1)
        lane = lax.broadcasted_iota(jnp.int32, (1, LANES), 1)
        ones_q = jnp.logical_and(lane >= AUX_ONE, lane < AUX_ONE + 3).astype(F32)
        ones_k = jnp.logical_and(lane >= AUX_F, lane < AUX_F + 3).astype(F32)
        seg_v = seg_ref[...]

        def build(x_ref, g_ref, scale, out_ref, ones, place_ref, f_sign):
            xv = x_ref[...]
            r = lax.rsqrt(_head_sums(xv * xv, seg_v) * (1.0 / HEAD_DIM) + EPS)
            xn = xv * r * g_ref[...] * scale
            aux = _dot(pcs, place_ref[...]) * f_sign
            for h in range(N_HEADS):
                pair = xn[:, (h // 2) * LANES:(h // 2 + 1) * LANES]
                feat = pair if h % 2 == 0 else pltpu.roll(pair, HEAD_DIM, 1)
                aux_h = aux[:, h * LANES:(h + 1) * LANES] + ones
                out_ref[:, h * LANES:(h + 1) * LANES] = jnp.where(lane < HEAD_DIM, feat, aux_h).astype(BF16)

        build(q_ref, gq_ref, 0.125, qp_ref, ones_q, pq_ref, 1.0)
        build(k_ref, gk_ref, 1.0, kp_ref, ones_k, pk_ref, -1.0)

    blk = pl.BlockSpec((ts, aw), lambda b, s: (b * ns + s, 0))
    fblk = pl.BlockSpec((ts, LANES), lambda b, s: (b * ns + s, 0))
    oblk = pl.BlockSpec((ts, N_HEADS * LANES), lambda b, s: (b * ns + s, 0))
    const = lambda shape: pl.BlockSpec(shape, lambda b, s: (0, 0))
    return pl.pallas_call(
        body, name="attn_prep_fwd", grid=(bsz, ns),
        in_specs=[blk, blk, fblk, const((1, LANES)), const((1, aw)), const((1, aw)), const((aw, aw)), const((ts, ts)),
                  const((3 * LANES, N_HEADS * LANES)), const((3 * LANES, N_HEADS * LANES))],
        out_specs=[oblk, oblk],
        out_shape=[jax.ShapeDtypeStruct((t, N_HEADS * LANES), BF16)] * 2,
        scratch_shapes=[pltpu.VMEM((8, LANES), F32)],
        compiler_params=_params("arbitrary", "arbitrary"),
    )(q, k, f, b_forget, q_gain, k_gain, seg, tri, _place_pieces(AUX_F), _place_pieces(AUX_ONE))


def _attn_prep_bwd(dqp, dkp, q, k, f, b_forget, q_gain, k_gain, bsz, seq):
    ts = ATTN_PREP_ROWS
    ns = seq // ts
    aw = ATTN_WIDTH
    t = bsz * seq
    seg = _seg_ones(aw, HEAD_DIM)
    tri = _tri_ones(ts, False)

    def body(dqp_ref, dkp_ref, q_ref, k_ref, f_ref, bf_ref, gq_ref, gk_ref, seg_ref, tri_ref,
             dq_ref, dk_ref, df_ref, dgq_ref, dgk_ref, dbf_ref, carry):
        b = pl.program_id(0)
        sr = pl.program_id(1)

        @pl.when(jnp.logical_and(b == 0, sr == 0))
        def _():
            dgq_ref[...] = jnp.zeros_like(dgq_ref)
            dgk_ref[...] = jnp.zeros_like(dgk_ref)
            dbf_ref[...] = jnp.zeros_like(dbf_ref)

        @pl.when(sr == 0)
        def _():
            carry[...] = jnp.zeros_like(carry)

        lane = lax.broadcasted_iota(jnp.int32, (1, LANES), 1)
        seg_v = seg_ref[...]

        def norm_bwd(dp_ref, x_ref, g_ref, scale, dx_ref, dgain_ref):
            parts = []
            for j in range(N_HEADS // 2):
                even = dp_ref[:, (2 * j) * LANES:(2 * j + 1) * LANES]
                odd = dp_ref[:, (2 * j + 1) * LANES:(2 * j + 2) * LANES]
                parts.append(jnp.where(lane < HEAD_DIM, even, pltpu.roll(odd, HEAD_DIM, 1)))
            dxn = jnp.concatenate(parts, axis=1) * scale
            xv = x_ref[...]
            r = lax.rsqrt(_head_sums(xv * xv, seg_v) * (1.0 / HEAD_DIM) + EPS)
            n = xv * r
            dgain_ref[...] += jnp.sum(dxn * n, axis=0, keepdims=True)
            dn = dxn * g_ref[...]
            m = _head_sums(dn * n, seg_v) * (1.0 / HEAD_DIM)
            dx_ref[...] = (r * (dn - n * m)).astype(BF16)

        norm_bwd(dqp_ref, q_ref, gq_ref, 0.125, dq_ref, dgq_ref)
        norm_bwd(dkp_ref, k_ref, gk_ref, 1.0, dk_ref, dgk_ref)

        dfc = jnp.zeros((ts, LANES), F32)
        for h in range(N_HEADS):
            cols = pl.ds(h * LANES, LANES)
            both = jnp.where(lane == AUX_F, dqp_ref[:, cols], 0.0) - jnp.where(lane == AUX_ONE, dkp_ref[:, cols], 0.0)
            dfc = jnp.where(lane == h, jnp.sum(both, axis=1, keepdims=True), dfc)
        hi, mid, lo = _split3(dfc)
        tri_v = tri_ref[...]
        dlogf = _dot(tri_v, hi) + _dot(tri_v, mid) + _dot(tri_v, lo) + carry[pl.ds(0, 1), :]
        carry[pl.ds(0, 1), :] = dlogf[0:1, :]
        df = jnp.where(lane < N_HEADS, dlogf * jax.nn.sigmoid(-(f_ref[...] + bf_ref[...])), 0.0)
        df_ref[...] = df.astype(BF16)
        dbf_ref[...] += jnp.sum(df, axis=0, keepdims=True)

    rev = lambda b, s: (b * ns + (ns - 1 - s), 0)
    blk = pl.BlockSpec((ts, aw), rev)
    fblk = pl.BlockSpec((ts, LANES), rev)
    pblk = pl.BlockSpec((ts, N_HEADS * LANES), rev)
    const = lambda shape: pl.BlockSpec(shape, lambda b, s: (0, 0))
    return pl.pallas_call(
        body, name="attn_prep_bwd", grid=(bsz, ns),
        in_specs=[pblk, pblk, blk, blk, fblk, const((1, LANES)), const((1, aw)), const((1, aw)), const((aw, aw)),
                  const((ts, ts))],
        out_specs=[blk, blk, fblk, const((1, aw)), const((1, aw)), const((1, LANES))],
        out_shape=[jax.ShapeDtypeStruct((t, aw), BF16), jax.ShapeDtypeStruct((t, aw), BF16),
                   jax.ShapeDtypeStruct((t, LANES), BF16), jax.ShapeDtypeStruct((1, aw), F32),
                   jax.ShapeDtypeStruct((1, aw), F32), jax.ShapeDtypeStruct((1, LANES), F32)],
        scratch_shapes=[pltpu.VMEM((8, LANES), F32)],
        compiler_params=_params("arbitrary", "arbitrary"),
    )(dqp, dkp, q, k, f, b_forget, q_gain, k_gain, seg, tri)


ATTN_BLOCK = 1024
HEAD_PAIRS = N_HEADS // 2


def _flash_fwd(qp, kp, v, bsz, seq):
    tq = ATTN_BLOCK
    half = tq // 2
    nq = seq // tq
    t = bsz * seq

    def body(q_ref, k_ref, v_ref, o_ref, lse_ref, m_sc, l_sc, acc_sc):
        i = pl.program_id(2)
        m_sc[...] = jnp.full(m_sc.shape, -jnp.inf, F32)
        l_sc[...] = jnp.zeros_like(l_sc)
        acc_sc[...] = jnp.zeros_like(acc_sc)
        lane = lax.broadcasted_iota(jnp.int32, (1, LANES), 1)
        low = lane < HEAD_DIM

        def tile(q0, qn, k_start, kn, k0=None):
            qs = pl.ds(q0, qn)
            ks = pl.ds(k_start, kn)
            vv = v_ref[ks, :]
            for h in range(2):
                mine = low if h == 0 else jnp.logical_not(low)
                cols = pl.ds(h * LANES, LANES)
                s = _dot_nt(q_ref[qs, cols], k_ref[ks, cols])
                if k0 is not None:
                    row = lax.broadcasted_iota(jnp.int32, (qn, kn), 0) + q0
                    col = lax.broadcasted_iota(jnp.int32, (qn, kn), 1) + k0
                    s = jnp.where(row >= col, s, -jnp.inf)
                m_prev = m_sc[h, qs, :]
                m_new = jnp.maximum(m_prev, jnp.max(s, axis=1, keepdims=True))
                p = jnp.exp(s - jnp.tile(m_new, (1, kn // LANES)))
                alpha = jnp.exp(m_prev - m_new)
                l_sc[h, qs, :] = alpha * l_sc[h, qs, :] + jnp.sum(p, axis=1, keepdims=True)
                m_sc[h, qs, :] = m_new
                pv = _dot(p.astype(BF16), jnp.where(mine, vv, jnp.zeros_like(vv)))
                acc_sc[qs, :] = acc_sc[qs, :] * jnp.where(mine, alpha, 1.0) + pv

        def below_diagonal(j, carry):
            tile(0, tq, pl.multiple_of(j * tq, tq), tq)
            return carry

        lax.fori_loop(0, i, below_diagonal, 0)
        diagonal = pl.multiple_of(i * tq, tq)
        tile(0, tq, diagonal, half, k0=0)
        tile(half, half, diagonal + half, half, k0=half)
        l = jnp.where(low, l_sc[0], l_sc[1])
        m = jnp.where(low, m_sc[0], m_sc[1])
        o_ref[...] = acc_sc[...] / l
        lse_ref[...] = m + jnp.log(l)

    qspec = pl.BlockSpec((tq, 2 * LANES), lambda b, hp, i: (b * nq + i, hp))
    kspec = pl.BlockSpec((seq, 2 * LANES), lambda b, hp, i: (b, hp))
    vspec = pl.BlockSpec((seq, LANES), lambda b, hp, i: (b, hp))
    ospec = pl.BlockSpec((tq, LANES), lambda b, hp, i: (b * nq + i, hp))
    return pl.pallas_call(
        body, name="flash_fwd", grid=(bsz, HEAD_PAIRS, nq),
        in_specs=[qspec, kspec, vspec], out_specs=[ospec, ospec],
        out_shape=[jax.ShapeDtypeStruct((t, ATTN_WIDTH), F32), jax.ShapeDtypeStruct((t, ATTN_WIDTH), F32)],
        scratch_shapes=[pltpu.VMEM((2, tq, LANES), F32), pltpu.VMEM((2, tq, LANES), F32), pltpu.VMEM((tq, LANES), F32)],
        compiler_params=_params("arbitrary", "arbitrary", "arbitrary"),
    )(qp, kp, v)


def _flash_bwd(qp, kp, v, o, do, lse, bsz, seq):
    tq = ATTN_BLOCK
    half = tq // 2
    nq = seq // tq
    t = bsz * seq

    def body(q_ref, k_ref, v_ref, o_ref, do_ref, lse_ref, dq_ref, dk_ref, dv_ref, dk_acc, dv_acc):
        j = pl.program_id(2)

        @pl.when(j == 0)
        def _():
            dq_ref[...] = jnp.zeros_like(dq_ref)

        dk_acc[...] = jnp.zeros_like(dk_acc)
        dv_acc[...] = jnp.zeros_like(dv_acc)
        lane = lax.broadcasted_iota(jnp.int32, (1, LANES), 1)
        low = lane < HEAD_DIM

        def tile(q_start, qn, k0, kn, q0=None):
            rows = pl.ds(q_start, qn)
            ks = pl.ds(k0, kn)
            dov = do_ref[rows, :]
            dd = dov * o_ref[rows, :]
            dob = dov.astype(BF16)
            vv = v_ref[ks, :]
            lse_v = lse_ref[rows, :]
            for h in range(2):
                mine = low if h == 0 else jnp.logical_not(low)
                cols = pl.ds(h * LANES, LANES)
                qh = q_ref[rows, cols]
                kh = k_ref[ks, cols]
                s = _dot_nt(qh, kh)
                lse_h = jnp.where(mine, lse_v, pltpu.roll(lse_v, HEAD_DIM, 1))
                p = jnp.exp(s - jnp.tile(lse_h, (1, kn // LANES)))
                if q0 is not None:
                    row = lax.broadcasted_iota(jnp.int32, (qn, kn), 0) + q0
                    col = lax.broadcasted_iota(jnp.int32, (qn, kn), 1) + k0
                    p = jnp.where(row >= col, p, 0.0)
                delta = jnp.sum(jnp.where(mine, dd, 0.0), axis=1, keepdims=True)
                dp = _dot_nt(dob, jnp.where(mine, vv, jnp.zeros_like(vv)))
                ds = (p * (dp - delta)).astype(BF16)
                dv_acc[ks, :] += jnp.where(mine, _dot_tn(p.astype(BF16), dob), 0.0)
                dk_acc[ks, cols] += _dot_tn(ds, qh)
                dq_ref[rows, cols] += _dot(ds, kh)

        def above_diagonal(i, carry):
            tile(pl.multiple_of(i * tq, tq), tq, 0, tq)
            return carry

        diagonal = pl.multiple_of(j * tq, tq)
        tile(diagonal, tq, 0, half, q0=0)
        tile(diagonal + half, half, half, half, q0=half)
        lax.fori_loop(j + 1, nq, above_diagonal, 0)
        dk_ref[...] = dk_acc[...]
        dv_ref[...] = dv_acc[...].astype(BF16)

    qspec = pl.BlockSpec((seq, 2 * LANES), lambda b, hp, j: (b, hp))
    kspec = pl.BlockSpec((tq, 2 * LANES), lambda b, hp, j: (b * nq + j, hp))
    vspec = pl.BlockSpec((tq, LANES), lambda b, hp, j: (b * nq + j, hp))
    ospec = pl.BlockSpec((seq, LANES), lambda b, hp, j: (b, hp))
    return pl.pallas_call(
        body, name="flash_bwd", grid=(bsz, HEAD_PAIRS, nq),
        in_specs=[qspec, kspec, vspec, ospec, ospec, ospec], out_specs=[qspec, kspec, vspec],
        out_shape=[jax.ShapeDtypeStruct((t, N_HEADS * LANES), F32), jax.ShapeDtypeStruct((t, N_HEADS * LANES), F32),
                   jax.ShapeDtypeStruct((t, ATTN_WIDTH), BF16)],
        scratch_shapes=[pltpu.VMEM((tq, 2 * LANES), F32), pltpu.VMEM((tq, LANES), F32)],
        compiler_params=_params("arbitrary", "arbitrary", "arbitrary"),
    )(qp, kp, v, o, do, lse)


def _mix_out_fwd(o, y_pool, x, gain, w_out):
    t, d = x.shape
    tm = 1024
    pw, aw = POOL_WIDTH, ATTN_WIDTH

    def body(o_ref, yp_ref, x_ref, g_ref, w_ref, ycat_ref, y_ref):
        ov = o_ref[...]
        ya = (ov * _rms_scale(ov) * g_ref[...]).astype(BF16)
        ycat = jnp.concatenate([yp_ref[...], ya], axis=1)
        ycat_ref[...] = ycat
        y_ref[...] = x_ref[...] + _dot(ycat, w_ref[...])

    return pl.pallas_call(
        body, name="mix_out_fwd", grid=(t // tm,),
        in_specs=[_rows(tm, aw), _rows(tm, pw), _rows(tm, d), _resident((1, aw)), _resident((pw + aw, d))],
        out_specs=[_rows(tm, pw + aw), _rows(tm, d)],
        out_shape=[jax.ShapeDtypeStruct((t, pw + aw), BF16), jax.ShapeDtypeStruct((t, d), F32)],
        compiler_params=_params("arbitrary"),
    )(o, y_pool, x, gain, w_out)


def _mix_out_bwd(dx, o, ycat, gain, w_out):
    t, d = dx.shape
    tm = 1024
    nm = t // tm
    pw, aw = POOL_WIDTH, ATTN_WIDTH

    def body(dx_ref, o_ref, ycat_ref, g_ref, w_ref, dw_ref, dyp_ref, do_ref, dg_ref, acc):
        i = pl.program_id(0)

        @pl.when(i == 0)
        def _():
            dg_ref[...] = jnp.zeros_like(dg_ref)
            acc[...] = jnp.zeros_like(acc)

        dxb = dx_ref[...].astype(BF16)
        acc[...] += _dot_tn(ycat_ref[...], dxb)
        dyp_ref[...] = _dot_nt(dxb, w_ref[pl.ds(0, pw), :])
        dya = _dot_nt(dxb, w_ref[pl.ds(pw, aw), :])
        do, dgain = _rms_bwd(dya, o_ref[...], g_ref[...])
        do_ref[...] = do
        dg_ref[...] += dgain

        @pl.when(i == nm - 1)
        def _():
            dw_ref[...] = acc[...].astype(BF16)

    return pl.pallas_call(
        body, name="mix_out_bwd", grid=(nm,),
        in_specs=[_rows(tm, d), _rows(tm, aw), _rows(tm, pw + aw), _resident((1, aw)), _resident((pw + aw, d))],
        out_specs=[pl.BlockSpec((pw + aw, d), lambda i: (0, 0)), _rows(tm, pw), _rows(tm, aw),
                   pl.BlockSpec((1, aw), lambda i: (0, 0))],
        out_shape=[jax.ShapeDtypeStruct((pw + aw, d), BF16), jax.ShapeDtypeStruct((t, pw), F32),
                   jax.ShapeDtypeStruct((t, aw), F32), jax.ShapeDtypeStruct((1, aw), F32)],
        scratch_shapes=[pltpu.VMEM((pw + aw, d), F32)],
        compiler_params=_params("arbitrary"),
    )(dx, o, ycat, gain, w_out)


def _mix_in_bwd(dpv, dq, dk, dv, df, hm, x, dx_res, gain, w_in_t):
    t, d = x.shape
    tm = 512
    nm = t // tm
    pw, aw = POOL_WIDTH, ATTN_WIDTH

    def body(dpv_ref, dq_ref, dk_ref, dv_ref, df_ref, hm_ref, x_ref, dxr_ref, g_ref, w_ref, dw_ref, dx_ref, dxh_ref,
             dg_ref, acc):
        i = pl.program_id(0)

        @pl.when(i == 0)
        def _():
            dg_ref[...] = jnp.zeros_like(dg_ref)
            acc[...] = jnp.zeros_like(acc)

        dh = jnp.concatenate([dpv_ref[...], dq_ref[...], dk_ref[...], dv_ref[...], df_ref[...]], axis=1)
        acc[...] += _dot_tn(dh, hm_ref[...])
        dx, dgain = _rms_bwd(_dot(dh, w_ref[...]), x_ref[...], g_ref[...])
        dx = dxr_ref[...] + dx
        dx_ref[...] = dx
        dxh_ref[...] = (0.5 * dx).astype(BF16)
        dg_ref[...] += dgain

        @pl.when(i == nm - 1)
        def _():
            dw_ref[...] = acc[...].astype(BF16)

    return pl.pallas_call(
        body, name="mix_in_bwd", grid=(nm,),
        in_specs=[_rows(tm, pw), _rows(tm, aw), _rows(tm, aw), _rows(tm, aw), _rows(tm, LANES), _rows(tm, d),
                  _rows(tm, d), _rows(tm, d), _resident((1, d)), _resident((MIX_PAD, d))],
        out_specs=[pl.BlockSpec((MIX_PAD, d), lambda i: (0, 0)), _rows(tm, d), _rows(tm, d),
                   pl.BlockSpec((1, d), lambda i: (0, 0))],
        out_shape=[jax.ShapeDtypeStruct((MIX_PAD, d), BF16), jax.ShapeDtypeStruct((t, d), F32),
                   jax.ShapeDtypeStruct((t, d), BF16), jax.ShapeDtypeStruct((1, d), F32)],
        scratch_shapes=[pltpu.VMEM((MIX_PAD, d), F32)],
        compiler_params=_params("arbitrary"),
    )(dpv, dq, dk, dv, df, hm, x, dx_res, gain, w_in_t)


MESH_IDS = pl.DeviceIdType.MESH


def _me():
    return lax.axis_index("x"), lax.axis_index("y"), lax.axis_index("c")


def _peer(x, y, c, p):
    px = 1 - x if p & 4 else x
    py = 1 - y if p & 2 else y
    pc = 1 - c if p & 1 else c
    return (px, py, pc), 4 * px + 2 * py + pc


HBM_SPEC = pl.BlockSpec(memory_space=pltpu.HBM)
SEM_SPEC = pl.BlockSpec(memory_space=pltpu.SEMAPHORE)
SPLIT_COPY = pltpu.CompilerParams(has_side_effects=pltpu.SideEffectType.DATAFLOW_SIDE_EFFECTING)
PEERS = N_DEV - 1


def _hbm(a):
    return pltpu.with_memory_space_constraint(a, pltpu.HBM)


def _row_block(ref, dev, rows):
    return ref.at[pl.ds(pl.multiple_of(dev * rows, BF16_ROWS), rows)]


def _copy_ends(gather, src, land, me, peer_id):
    if gather:
        rows = src.shape[0]
        return src, _row_block(land, me, rows), _row_block(land, peer_id, rows), src, _row_block(land, me, rows)
    rows = src.shape[0] // N_DEV
    return (_row_block(src, peer_id, rows), land.at[me], land.at[peer_id], _row_block(src, me, rows), land.at[me])


def _land_shape(gather, s):
    return (N_DEV * s.shape[0], s.shape[1]) if gather else (N_DEV, s.shape[0] // N_DEV, s.shape[1])


SIBLING = 1
SAME_CORE_PEERS = (2, 4, 6)
RELAYS = len(SAME_CORE_PEERS)


def _copies_start(groups, gather, name, after=None, relayed=()):
    flat = [s for g in groups for s in g]
    n, ng = len(flat), len(groups)
    lands = [lax.empty(_land_shape(gather, s), s.dtype) for s in flat]
    n_in = 2 * n + (after is not None)

    def body(*refs):
        ins, lnd = refs[:n], refs[n:2 * n]
        sems = refs[n_in:n_in + 2 * ng]
        token = refs[-1]
        x, y, c = _me()
        me = 4 * x + 2 * y + c
        w = 0
        for gi, g in enumerate(groups):
            for k in range(len(g)):
                for p in ((SIBLING,) + SAME_CORE_PEERS if gi in relayed else range(1, N_DEV)):
                    peer, peer_id = _peer(x, y, c, p)
                    src, dst, _, _, _ = _copy_ends(gather, ins[w], lnd[w], me, peer_id)
                    pltpu.make_async_remote_copy(src, dst, sems[2 * gi].at[k * PEERS + p - 1],
                                                 sems[2 * gi + 1].at[k * PEERS + p - 1], device_id=peer,
                                                 device_id_type=MESH_IDS).start()
                w += 1
        token[...] = jnp.zeros_like(token)

    sem_shapes = []
    for g in groups:
        sem_shapes += [pltpu.SemaphoreType.DMA((len(g) * PEERS,))] * 2
    out = pl.pallas_call(
        body, name=name,
        out_shape=(*sem_shapes, *[pltpu.HBM(s.shape, s.dtype) for s in flat],
                   *[pltpu.HBM(l.shape, l.dtype) for l in lands], jax.ShapeDtypeStruct((8, LANES), F32)),
        in_specs=[HBM_SPEC] * (2 * n) + [pl.BlockSpec(memory_space=pl.ANY)] * (after is not None),
        out_specs=(*[SEM_SPEC] * (2 * ng), *[HBM_SPEC] * (2 * n), pl.BlockSpec(memory_space=pltpu.VMEM)),
        input_output_aliases={i: 2 * ng + i for i in range(2 * n)},
        compiler_params=SPLIT_COPY,
    )(*[_hbm(s) for s in flat], *[_hbm(l) for l in lands], *([after] if after is not None else []))
    sems, thru, token = out[:2 * ng], out[2 * ng:2 * ng + 2 * n], out[-1]
    res, w = [], 0
    for gi, g in enumerate(groups):
        res.append((sems[2 * gi], sems[2 * gi + 1], list(thru[w:w + len(g)]), list(thru[n + w:n + w + len(g)])))
        w += len(g)
    return res, token


def _copies_wait(started, gather, after, name):
    send, recv, srcs, lands = started
    n = len(srcs)
    after = list(after) if isinstance(after, (list, tuple)) else [after]

    own_shapes = [s.shape if gather else (s.shape[0] // N_DEV, s.shape[1]) for s in srcs]

    def body(*refs):
        ins, lnd = refs[:n], refs[n:2 * n]
        send_sems, recv_sems = refs[2 * n], refs[2 * n + 1]
        bounce, in_sems, out_sems = refs[-n - 2:-2], refs[-2], refs[-1]
        x, y, c = _me()
        me = 4 * x + 2 * y + c
        ends = [_copy_ends(gather, ins[w], lnd[w], me, me)[3:] for w in range(n)]
        loads = [pltpu.make_async_copy(ends[w][0], bounce[w], in_sems.at[w]) for w in range(n)]
        stores = [pltpu.make_async_copy(bounce[w], ends[w][1], out_sems.at[w]) for w in range(n)]
        for cp in loads:
            cp.start()
        for w in range(n):
            loads[w].wait()
            stores[w].start()
        for w in range(n):
            for p in range(1, N_DEV):
                peer, peer_id = _peer(x, y, c, p)
                src, _, arrival, _, _ = _copy_ends(gather, ins[w], lnd[w], me, peer_id)
                cp = pltpu.make_async_remote_copy(src, arrival, send_sems.at[w * PEERS + p - 1],
                                                  recv_sems.at[w * PEERS + p - 1], device_id=peer,
                                                  device_id_type=MESH_IDS)
                cp.wait_send()
                cp.wait_recv()
        for cp in stores:
            cp.wait()

    out = pl.pallas_call(
        body, name=name,
        out_shape=(*[pltpu.HBM(s.shape, s.dtype) for s in srcs], *[pltpu.HBM(l.shape, l.dtype) for l in lands]),
        in_specs=[HBM_SPEC] * (2 * n) + [SEM_SPEC, SEM_SPEC] + [pl.BlockSpec(memory_space=pl.ANY)] * len(after),
        out_specs=[HBM_SPEC] * (2 * n),
        input_output_aliases={i: i for i in range(2 * n)},
        scratch_shapes=[*[pltpu.VMEM(shape, s.dtype) for shape, s in zip(own_shapes, srcs)],
                        pltpu.SemaphoreType.DMA((n,)), pltpu.SemaphoreType.DMA((n,))],
        compiler_params=SPLIT_COPY,
    )(*srcs, *lands, send, recv, *after)
    return list(out[n:])


def _relay_to_sibling(started, name, after=None):
    send, recv, srcs, lands = started
    n = len(srcs)
    after = [] if after is None else [after]

    def body(*refs):
        ins, lnd = refs[:n], refs[n:2 * n]
        send_sems, recv_sems = refs[2 * n], refs[2 * n + 1]
        relay_send, relay_recv = refs[2 * n + 2 + len(after)], refs[2 * n + 3 + len(after)]
        x, y, c = _me()
        sibling, _ = _peer(x, y, c, SIBLING)
        for w in range(n):
            rows = ins[w].shape[0]
            for k, p in enumerate(SAME_CORE_PEERS):
                peer, peer_id = _peer(x, y, c, p)
                arrived = _row_block(lnd[w], peer_id, rows)
                first = pltpu.make_async_remote_copy(ins[w], arrived, send_sems.at[w * PEERS + p - 1],
                                                     recv_sems.at[w * PEERS + p - 1], device_id=peer,
                                                     device_id_type=MESH_IDS)
                first.wait_recv()
                pltpu.make_async_remote_copy(arrived, arrived, relay_send.at[w * RELAYS + k],
                                             relay_recv.at[w * RELAYS + k], device_id=sibling,
                                             device_id_type=MESH_IDS).start()
                first.wait_send()

    sems = pltpu.SemaphoreType.DMA((n * RELAYS,))
    out = pl.pallas_call(
        body, name=name,
        out_shape=(sems, sems, *[pltpu.HBM(s.shape, s.dtype) for s in srcs], *[pltpu.HBM(l.shape, l.dtype) for l in lands]),
        in_specs=[HBM_SPEC] * (2 * n) + [SEM_SPEC, SEM_SPEC] + [pl.BlockSpec(memory_space=pl.ANY)] * len(after),
        out_specs=(SEM_SPEC, SEM_SPEC, *[HBM_SPEC] * (2 * n)),
        input_output_aliases={i: 2 + i for i in range(2 * n)},
        compiler_params=SPLIT_COPY,
    )(*srcs, *lands, send, recv, *after)
    return send, recv, out[0], out[1], list(out[2:2 + n]), list(out[2 + n:])


def _relayed_wait(relayed, after, name):
    send, recv, relay_send, relay_recv, srcs, lands = relayed
    n = len(srcs)
    after = list(after) if isinstance(after, (list, tuple)) else [after]

    def body(*refs):
        ins, lnd = refs[:n], refs[n:2 * n]
        send_sems, recv_sems, relay_send_sems, relay_recv_sems = refs[2 * n:2 * n + 4]
        bounce, in_sems, out_sems = refs[-n - 2:-2], refs[-2], refs[-1]
        x, y, c = _me()
        me = 4 * x + 2 * y + c
        sibling, sibling_id = _peer(x, y, c, SIBLING)
        loads = [pltpu.make_async_copy(ins[w], bounce[w], in_sems.at[w]) for w in range(n)]
        stores = [pltpu.make_async_copy(bounce[w], _row_block(lnd[w], me, ins[w].shape[0]), out_sems.at[w])
                  for w in range(n)]
        for cp in loads:
            cp.start()
        for w in range(n):
            loads[w].wait()
            stores[w].start()
        for w in range(n):
            rows = ins[w].shape[0]
            direct = pltpu.make_async_remote_copy(ins[w], _row_block(lnd[w], sibling_id, rows),
                                                  send_sems.at[w * PEERS + SIBLING - 1],
                                                  recv_sems.at[w * PEERS + SIBLING - 1], device_id=sibling,
                                                  device_id_type=MESH_IDS)
            direct.wait_send()
            direct.wait_recv()
            for k, p in enumerate(SAME_CORE_PEERS):
                _, sent_id = _peer(x, y, c, p)
                _, got_id = _peer(x, y, c, p + SIBLING)
                relay = pltpu.make_async_remote_copy(_row_block(lnd[w], sent_id, rows), _row_block(lnd[w], got_id, rows),
                                                     relay_send_sems.at[w * RELAYS + k],
                                                     relay_recv_sems.at[w * RELAYS + k], device_id=sibling,
                                                     device_id_type=MESH_IDS)
                relay.wait_send()
                relay.wait_recv()
        for cp in stores:
            cp.wait()

    out = pl.pallas_call(
        body, name=name,
        out_shape=(*[pltpu.HBM(s.shape, s.dtype) for s in srcs], *[pltpu.HBM(l.shape, l.dtype) for l in lands]),
        in_specs=[HBM_SPEC] * (2 * n) + [SEM_SPEC] * 4 + [pl.BlockSpec(memory_space=pl.ANY)] * len(after),
        out_specs=[HBM_SPEC] * (2 * n),
        input_output_aliases={i: i for i in range(2 * n)},
        scratch_shapes=[*[pltpu.VMEM(s.shape, s.dtype) for s in srcs],
                        pltpu.SemaphoreType.DMA((n,)), pltpu.SemaphoreType.DMA((n,))],
        compiler_params=SPLIT_COPY,
    )(*srcs, *lands, send, recv, relay_send, relay_recv, *after)
    return list(out[n:])


def _adamw_update(w, g, m, v):
    nm = ADAM_B1 * m + (1.0 - ADAM_B1) * g
    nv = ADAM_B2 * v + (1.0 - ADAM_B2) * (g * g)
    m_hat = nm / (1.0 - ADAM_B1 ** ADAM_STEP)
    v_hat = nv / (1.0 - ADAM_B2 ** ADAM_STEP)
    return -ADAM_LR * (m_hat / (jnp.sqrt(v_hat) + ADAM_EPS) + ADAM_WD * w), nm, nv


SUM_ADAMW_COLS = 512


def _sum_adamw(parts, w, m, v, name):
    _, rows, d = parts.shape
    n = w.shape[0]
    tc = SUM_ADAMW_COLS

    def body(p_ref, w_ref, m_ref, v_ref, g_ref, d_ref, nm_ref, nv_ref):
        g = p_ref[0].astype(F32)
        for dev in range(1, N_DEV):
            g = g + p_ref[dev].astype(F32)
        g = g[:n]
        g_ref[...] = g
        d_ref[...], nm_ref[...], nv_ref[...] = _adamw_update(w_ref[...], g, m_ref[...], v_ref[...])

    spec = pl.BlockSpec((n, tc), lambda j: (0, j))
    shape = jax.ShapeDtypeStruct((n, d), F32)
    return pl.pallas_call(
        body, name=name, grid=(d // tc,),
        in_specs=[pl.BlockSpec((N_DEV, rows, tc), lambda j: (0, 0, j)), spec, spec, spec],
        out_specs=[spec] * 4, out_shape=[shape] * 4,
        compiler_params=_params("arbitrary"),
    )(parts, w, m, v)


def _pad_rows(a, rows):
    return jnp.pad(a, ((0, rows - a.shape[0]), (0, 0)))


def _row1(vec, width=D_MODEL):
    return jnp.pad(vec.reshape(1, -1), ((0, 0), (0, width - vec.shape[-1])))


COLUMN_SHARDED = ("ffn1_w_gate", "ffn1_w_up", "w_in", "ffn2_w_gate", "ffn2_w_up")
VEC_NAMES = ("ffn1_norm", "mix_norm", "ffn2_norm", "b_forget", "pool_scale", "q_norm", "k_norm", "out_norm_pool",
             "out_norm_attn")
VEC_ROWS = 16
LOSS_ROW = len(VEC_NAMES)


def _pack_vector_grads(parts, loss_part, name):
    names = [n for n in VEC_NAMES if n in parts]
    extra = [] if loss_part is None else [loss_part]

    def body(*refs):
        out_ref = refs[-1]
        out_ref[...] = jnp.zeros_like(out_ref)
        lane = lax.broadcasted_iota(jnp.int32, (1, LANES), 1)
        for n, ref in zip(names, refs):
            val = ref[...]
            if n in ("q_norm", "k_norm"):
                val = val[:, 0:LANES] + val[:, LANES:2 * LANES] + val[:, 2 * LANES:3 * LANES] + val[:, 3 * LANES:]
                val = jnp.where(lane < HEAD_DIM, val + pltpu.roll(val, HEAD_DIM, 1), 0.0)
            out_ref[pl.ds(VEC_NAMES.index(n), 1), pl.ds(0, val.shape[1])] = val
        if extra:
            out_ref[pl.ds(LOSS_ROW, 1), pl.ds(0, 1)] = refs[len(names)][...]

    vmem = pl.BlockSpec(memory_space=pltpu.VMEM)
    return pl.pallas_call(
        body, name=name, in_specs=[vmem] * (len(names) + len(extra)), out_specs=vmem,
        out_shape=jax.ShapeDtypeStruct((VEC_ROWS, D_MODEL), F32),
    )(*[parts[n] for n in names], *extra)


def _small_adamw(vec_alls, pool_all, vec_params, pool_params):
    nv = len(vec_params)
    na = len(vec_alls)
    pool_rows = pool_params[0].shape[0]

    def body(*refs):
        vec_refs, pool_ref = refs[:na], refs[na]
        ins = refs[na + 1:na + 1 + 3 * nv + 3]
        outs = refs[na + 1 + 3 * nv + 3:-1]
        rows = refs[-1]
        total = jnp.zeros((VEC_ROWS, D_MODEL), F32)
        for vec_ref in vec_refs:
            for dev in range(N_DEV):
                total = total + vec_ref[pl.ds(dev * VEC_ROWS, VEC_ROWS), :]
        rows[...] = total
        outs[4 * nv + 4][...] = rows[pl.ds(LOSS_ROW, 1), pl.ds(0, 1)]
        for i in range(nv):
            w_ref, m_ref, v_ref = ins[3 * i:3 * i + 3]
            g = rows[pl.ds(i, 1), pl.ds(0, w_ref.shape[1])]
            outs[4 * i][...] = g
            outs[4 * i + 1][...], outs[4 * i + 2][...], outs[4 * i + 3][...] = _adamw_update(
                w_ref[...], g, m_ref[...], v_ref[...])
        g = pool_ref[pl.ds(0, pool_rows), :].astype(F32)
        for dev in range(1, N_DEV):
            g = g + pool_ref[pl.ds(dev * pool_rows, pool_rows), :].astype(F32)
        w_ref, m_ref, v_ref = ins[3 * nv:]
        outs[4 * nv][...] = g
        outs[4 * nv + 1][...], outs[4 * nv + 2][...], outs[4 * nv + 3][...] = _adamw_update(
            w_ref[...], g, m_ref[...], v_ref[...])

    vmem = pl.BlockSpec(memory_space=pltpu.VMEM)
    flat = [a for trio in vec_params for a in trio] + list(pool_params)
    out_shape = []
    for trio in list(vec_params) + [pool_params]:
        out_shape += [jax.ShapeDtypeStruct(trio[0].shape, F32)] * 4
    out_shape.append(jax.ShapeDtypeStruct((1, 1), F32))
    return pl.pallas_call(
        body, name="adamw_small", in_specs=[vmem] * (na + 1 + len(flat)), out_specs=[vmem] * len(out_shape),
        out_shape=out_shape, scratch_shapes=[pltpu.VMEM((VEC_ROWS, D_MODEL), F32)],
    )(*vec_alls, pool_all, *flat)


def kernel(x, ffn1_norm, ffn1_w_gate, ffn1_w_up, ffn1_w_down, mix_norm, w_in, b_forget, pool_w, pool_scale, q_norm, k_norm, out_norm_pool, out_norm_attn, w_out, ffn2_norm, ffn2_w_gate, ffn2_w_up, ffn2_w_down, loss_target, m_ffn1_norm, m_ffn1_w_gate, m_ffn1_w_up, m_ffn1_w_down, m_mix_norm, m_w_in, m_b_forget, m_pool_w, m_pool_scale, m_q_norm, m_k_norm, m_out_norm_pool, m_out_norm_attn, m_w_out, m_ffn2_norm, m_ffn2_w_gate, m_ffn2_w_up, m_ffn2_w_down, v_ffn1_norm, v_ffn1_w_gate, v_ffn1_w_up, v_ffn1_w_down, v_mix_norm, v_w_in, v_b_forget, v_pool_w, v_pool_scale, v_q_norm, v_k_norm, v_out_norm_pool, v_out_norm_attn, v_w_out, v_ffn2_norm, v_ffn2_w_gate, v_ffn2_w_up, v_ffn2_w_down):
    bsz, seq, d = x.shape
    t = bsz * seq
    x0 = x.reshape(t, d)
    target = loss_target.reshape(t, d)
    in_rows = -(-w_in.shape[1] // BF16_ROWS) * BF16_ROWS

    slabs = [s.astype(BF16) for s in (ffn1_w_gate.T, ffn1_w_up.T, ffn1_w_down, _pad_rows(w_in.T, in_rows), w_out,
                                       ffn2_w_gate.T, ffn2_w_up.T, ffn2_w_down)]
    gathers, started = _copies_start([slabs[0:2], slabs[2:3], slabs[3:4], slabs[4:5], slabs[5:8]], True, "gather_start",
                                     relayed=(0, 4))

    g1, gm, g2 = ffn1_norm.reshape(1, d), mix_norm.reshape(1, d), ffn2_norm.reshape(1, d)
    bf_row = _row1(b_forget, LANES)
    gq = jnp.tile(q_norm, N_HEADS).reshape(1, ATTN_WIDTH)
    gk = jnp.tile(k_norm, N_HEADS).reshape(1, ATTN_WIDTH)
    scale_row = pool_scale.reshape(1, POOL_WIDTH)
    gp, ga = out_norm_pool.reshape(1, POOL_WIDTH), out_norm_attn.reshape(1, ATTN_WIDTH)

    wg1, wu1 = _relayed_wait(_relay_to_sibling(gathers[0], "gather_relay_ffn1_up"), started, "gather_wait_ffn1_up")
    h1, sa1, sb1, s1 = _ffn_up(x0, g1, wg1, wu1, "ffn1_up")
    (wd1,) = _copies_wait(gathers[1], True, s1, "gather_wait_ffn1_down")
    (x1,) = _ffn_down(s1, wd1, x0, None, "ffn1_down")
    (win_g,) = _copies_wait(gathers[2], True, x1, "gather_wait_w_in")
    win_t = _repack_rows(win_g, in_rows, w_in.shape[1], N_DEV, "w_in_rows")
    hm, pv, q, k, v, f = _mix_in_fwd(x1, gm, win_t)
    pooled, mixed, y_pool = _pool_fwd(pv, pool_w, scale_row, gp, bsz, seq)
    qp, kp = _attn_prep_fwd(q, k, f, bf_row, gq, gk, bsz, seq)
    o, lse = _flash_fwd(qp, kp, v, bsz, seq)
    relayed_ffn2 = _relay_to_sibling(gathers[4], "gather_relay_ffn2", o)
    (wout,) = _copies_wait(gathers[3], True, [o, relayed_ffn2[4][0]], "gather_wait_w_out")
    ycat, x2 = _mix_out_fwd(o, y_pool, x1, ga, wout)
    wg2, wu2, wd2 = _relayed_wait(relayed_ffn2, x2, "gather_wait_ffn2")
    h2, sa2, sb2, s2 = _ffn_up(x2, g2, wg2, wu2, "ffn2_up")
    dx3, dyh2, loss_part = _ffn_down(s2, wd2, x2, target, "ffn2_down")

    da2, db2, dwg2, dwu2 = _ffn_bwd_act(dyh2, sa2, sb2, h2, wd2, "ffn2_bwd_act")
    (dwd2,) = _wgrad([s2], dyh2, "ffn2_down_wgrad")
    (sent_ffn2,), tok = _copies_start([[dwg2, dwu2, dwd2]], False, "exchange_start_ffn2")
    dx2, dg2 = _ffn_bwd_dx(da2, db2, dx3, x2, g2 + tok[0, 0], wg2, wu2, "ffn2_bwd_dx")
    dwout, dy_pool, do, dga = _mix_out_bwd(dx2, o, ycat, ga, wout)
    (sent_out,), tok = _copies_start([[dwout]], False, "exchange_start_w_out")
    dqp, dkp, dv = _flash_bwd(qp, kp, v, o, do, lse, bsz, seq)
    dq, dk, df, dgq, dgk, dbf = _attn_prep_bwd(dqp, dkp, q, k, f, bf_row + tok[0, 0], gq, gk, bsz, seq)
    dpv, dpool_w, dscale, dgp = _pool_bwd(dy_pool, mixed, pooled, pool_w, scale_row, gp, bsz, seq)
    dwin, dx1, dyh1, dgm = _mix_in_bwd(dpv, dq, dk, dv, df, hm, x1, dx2, gm, win_t)
    dwin_blocks = _repack_rows(dwin, w_in.shape[1], in_rows, N_DEV, "w_in_grad_blocks")
    (sent_in,), tok = _copies_start([[dwin_blocks]], False, "exchange_start_w_in")
    (dwd1,) = _wgrad([s1], dyh1, "ffn1_down_wgrad")
    (sent_down1,), tok = _copies_start([[dwd1]], False, "exchange_start_ffn1_down", after=tok)
    da1, db1, dwg1, dwu1 = _ffn_bwd_act(dyh1, sa1, sb1, h1, wd1, "ffn1_bwd_act")
    (sent_up1,), tok = _copies_start([[dwg1, dwu1]], False, "exchange_start_ffn1_up", after=tok)
    dx0, dg1 = _ffn_bwd_dx(da1, db1, dx1, x0, g1 + tok[0, 0], wg1, wu1, "ffn1_bwd_dx")

    pool_rows = POOL_GROUPS * POOL_GROUP_DIM
    packed = _pack_vector_grads(dict(ffn1_norm=dg1, mix_norm=dgm, ffn2_norm=dg2, b_forget=dbf, pool_scale=dscale,
                                     q_norm=dgq, k_norm=dgk, out_norm_pool=dgp, out_norm_attn=dga), loss_part,
                                "pack_vector_grads")
    pool_part = dpool_w.reshape(pool_rows, POOL_GROUP_DIM).astype(BF16)
    (sent_small,), tok = _copies_start([[packed, pool_part]], True, "small_grads_start")

    weights = dict(ffn1_norm=ffn1_norm, ffn1_w_gate=ffn1_w_gate, ffn1_w_up=ffn1_w_up, ffn1_w_down=ffn1_w_down,
                   mix_norm=mix_norm, w_in=w_in, b_forget=b_forget, pool_w=pool_w, pool_scale=pool_scale,
                   q_norm=q_norm, k_norm=k_norm, out_norm_pool=out_norm_pool, out_norm_attn=out_norm_attn,
                   w_out=w_out, ffn2_norm=ffn2_norm, ffn2_w_gate=ffn2_w_gate, ffn2_w_up=ffn2_w_up,
                   ffn2_w_down=ffn2_w_down)
    m_in = dict(ffn1_norm=m_ffn1_norm, ffn1_w_gate=m_ffn1_w_gate, ffn1_w_up=m_ffn1_w_up, ffn1_w_down=m_ffn1_w_down,
                mix_norm=m_mix_norm, w_in=m_w_in, b_forget=m_b_forget, pool_w=m_pool_w, pool_scale=m_pool_scale,
                q_norm=m_q_norm, k_norm=m_k_norm, out_norm_pool=m_out_norm_pool, out_norm_attn=m_out_norm_attn,
                w_out=m_w_out, ffn2_norm=m_ffn2_norm, ffn2_w_gate=m_ffn2_w_gate, ffn2_w_up=m_ffn2_w_up,
                ffn2_w_down=m_ffn2_w_down)
    v_in = dict(ffn1_norm=v_ffn1_norm, ffn1_w_gate=v_ffn1_w_gate, ffn1_w_up=v_ffn1_w_up, ffn1_w_down=v_ffn1_w_down,
                mix_norm=v_mix_norm, w_in=v_w_in, b_forget=v_b_forget, pool_w=v_pool_w, pool_scale=v_pool_scale,
                q_norm=v_q_norm, k_norm=v_k_norm, out_norm_pool=v_out_norm_pool, out_norm_attn=v_out_norm_attn,
                w_out=v_w_out, ffn2_norm=v_ffn2_norm, ffn2_w_gate=v_ffn2_w_gate, ffn2_w_up=v_ffn2_w_up,
                ffn2_w_down=v_ffn2_w_down)
    grads, delta, new_m, new_v = {}, {}, {}, {}
    after = [tok]
    plan = ((sent_ffn2, "ffn2", ("ffn2_w_gate", "ffn2_w_up", "ffn2_w_down")), (sent_out, "w_out", ("w_out",)),
            (sent_in, "w_in", ("w_in",)), (sent_down1, "ffn1_down", ("ffn1_w_down",)),
            (sent_up1, "ffn1_up", ("ffn1_w_gate", "ffn1_w_up")))
    for sent, tag, names in plan:
        parts = _copies_wait(sent, False, after, f"exchange_wait_{tag}")
        after = []
        for n, part in zip(names, parts):
            turn = (lambda a: a.T) if n in COLUMN_SHARDED else (lambda a: a)
            done = _sum_adamw(part, turn(weights[n]), turn(m_in[n]), turn(v_in[n]), f"adamw_{n}")
            grads[n], delta[n], new_m[n], new_v[n] = (turn(a) for a in done)
            after.append(done[3])
    vec_all, pool_all = _copies_wait(sent_small, True, after, "small_grads_wait")
    as_row = lambda a: a.reshape(1, -1)
    as_pool = lambda a: a.reshape(pool_rows, POOL_GROUP_DIM)
    small = _small_adamw([vec_all], pool_all,
                         [tuple(as_row(z[n]) for z in (weights, m_in, v_in)) for n in VEC_NAMES],
                         tuple(as_pool(z["pool_w"]) for z in (weights, m_in, v_in)))
    for i, n in enumerate(VEC_NAMES + ("pool_w",)):
        grads[n], delta[n], new_m[n], new_v[n] = (a.reshape(weights[n].shape) for a in small[4 * i:4 * i + 4])
    loss = small[-1].reshape(())

    order = ("ffn1_norm", "ffn1_w_gate", "ffn1_w_up", "ffn1_w_down", "mix_norm", "w_in", "b_forget", "pool_w",
             "pool_scale", "q_norm", "k_norm", "out_norm_pool", "out_norm_attn", "w_out", "ffn2_norm", "ffn2_w_gate",
             "ffn2_w_up", "ffn2_w_down")
    return (loss, dx0.reshape(bsz, seq, d), *[grads[n] for n in order], *[delta[n] for n in order],
            *[new_m[n] for n in order], *[new_v[n] for n in order])
```

```python
import jax
import jax.numpy as jnp
from jax import lax
from jax.experimental import pallas as pl
from jax.experimental.pallas import tpu as pltpu

F32 = jnp.float32
BF16 = jnp.bfloat16

EPS = 1e-6
D_MODEL = 1024
N_HEADS = 8
HEAD_DIM = 64
POOL_WIDTH = 512
ATTN_WIDTH = 512
POOL_GROUPS = 4
POOL_GROUP_DIM = 128
POOL_WINDOWS = (2, 4, 8, 16)
POOL_HALO = 16
MIX_PAD = POOL_WIDTH + 3 * ATTN_WIDTH + 128
N_DEV = 8
BF16_ROWS = 16
LANES = 128
VMEM_LIMIT = 56 * 1024 * 1024

ADAM_LR = 0.001
ADAM_B1 = 0.9
ADAM_B2 = 0.999
ADAM_EPS = 1e-08
ADAM_WD = 0.01
ADAM_STEP = 10


def _params(*sem):
    return pltpu.CompilerParams(dimension_semantics=sem, vmem_limit_bytes=VMEM_LIMIT)


def _dot(a, b):
    return jnp.dot(a, b, preferred_element_type=F32)


def _dot_nt(a, b):
    return lax.dot_general(a, b, (((1,), (1,)), ((), ())), preferred_element_type=F32)


def _dot_tn(a, b):
    return lax.dot_general(a, b, (((0,), (0,)), ((), ())), preferred_element_type=F32)


def _resident(shape):
    return pl.BlockSpec(shape, lambda *_: (0,) * len(shape), pipeline_mode=pl.Buffered(1))


def _rows(tm, width):
    return pl.BlockSpec((tm, width), lambda i: (i, 0))


def _rms_scale(x):
    return lax.rsqrt(jnp.mean(x * x, axis=-1, keepdims=True) + EPS)


def _rms_bwd(dh, x, gain):
    r = _rms_scale(x)
    n = x * r
    dgain = jnp.sum(dh * n, axis=0, keepdims=True)
    dn = dh * gain
    dx = r * (dn - n * jnp.mean(dn * n, axis=-1, keepdims=True))
    return dx, dgain


def _split3(x):
    hi = x.astype(BF16)
    r1 = x - hi.astype(F32)
    mid = r1.astype(BF16)
    lo = (r1 - mid.astype(F32)).astype(BF16)
    return hi, mid, lo


def _split2(x):
    hi = x.astype(BF16)
    return hi, (x - hi.astype(F32)).astype(BF16)


FF_CHUNK = 256


def _swiglu_parts(a, b):
    sig = jax.nn.sigmoid(a)
    silu = a * sig
    return (b * (sig + silu * (1.0 - sig))).astype(BF16), silu.astype(BF16), (silu * b).astype(BF16)


def _ffn_up(x, gain, wg_t, wu_t, name):
    t, d = x.shape
    f = wg_t.shape[0]
    tm = 512

    def body(x_ref, g_ref, wg_ref, wu_ref, h_ref, sa_ref, sb_ref, s_ref):
        xv = x_ref[...]
        h = (xv * _rms_scale(xv) * g_ref[...]).astype(BF16)
        h_ref[...] = h
        for c in range(f // FF_CHUNK):
            sl = pl.ds(c * FF_CHUNK, FF_CHUNK)
            sa_ref[:, sl], sb_ref[:, sl], s_ref[:, sl] = _swiglu_parts(_dot_nt(h, wg_ref[sl, :]), _dot_nt(h, wu_ref[sl, :]))

    wide = jax.ShapeDtypeStruct((t, f), BF16)
    return pl.pallas_call(
        body, name=name, grid=(t // tm,),
        in_specs=[_rows(tm, d), _resident((1, d)), _resident((f, d)), _resident((f, d))],
        out_specs=[_rows(tm, d), _rows(tm, f), _rows(tm, f), _rows(tm, f)],
        out_shape=[jax.ShapeDtypeStruct((t, d), BF16), wide, wide, wide],
        compiler_params=_params("arbitrary"),
    )(x, gain, wg_t, wu_t)


def _ffn_down(s, wd, x, target, name):
    t, d = x.shape
    f = wd.shape[0]
    tm = 512
    with_loss = target is not None

    def body(*refs):
        if with_loss:
            s_ref, w_ref, x_ref, t_ref, dy_ref, dyh_ref, loss_ref = refs
        else:
            s_ref, w_ref, x_ref, y_ref = refs
        y = x_ref[...] + 0.5 * _dot(s_ref[...], w_ref[...])
        if with_loss:
            e = y - t_ref[...]
            dy = e * (1.0 / d)
            dy_ref[...] = dy
            dyh_ref[...] = (0.5 * dy).astype(BF16)

            @pl.when(pl.program_id(0) == 0)
            def _():
                loss_ref[...] = jnp.zeros_like(loss_ref)

            part = jnp.sum(jnp.sum(e * e, axis=0, keepdims=True), axis=1, keepdims=True)
            loss_ref[...] += part * (0.5 / d)
        else:
            y_ref[...] = y

    in_specs = [_rows(tm, f), _resident((f, d)), _rows(tm, d)]
    args = [s, wd, x]
    if with_loss:
        in_specs.append(_rows(tm, d))
        args.append(target)
        out_shape = [jax.ShapeDtypeStruct((t, d), F32), jax.ShapeDtypeStruct((t, d), BF16),
                     jax.ShapeDtypeStruct((1, 1), F32)]
        out_specs = [_rows(tm, d), _rows(tm, d), pl.BlockSpec((1, 1), lambda i: (0, 0))]
    else:
        out_shape = [jax.ShapeDtypeStruct((t, d), F32)]
        out_specs = [_rows(tm, d)]
    return pl.pallas_call(
        body, name=name, grid=(t // tm,), in_specs=in_specs, out_specs=out_specs, out_shape=out_shape,
        compiler_params=_params("arbitrary"),
    )(*args)


def _ffn_bwd_act(dyh, sa, sb, h, wd, name):
    t, d = dyh.shape
    f = wd.shape[0]
    tn = f // 2
    tk = 512
    nk = t // tk

    def body(dy_ref, sa_ref, sb_ref, h_ref, wd_ref, da_ref, db_ref, dwg_ref, dwu_ref, acc_g, acc_u):
        k = pl.program_id(1)

        @pl.when(k == 0)
        def _():
            acc_g[...] = jnp.zeros_like(acc_g)
            acc_u[...] = jnp.zeros_like(acc_u)

        ds = _dot_nt(dy_ref[...], wd_ref[...])
        da = (ds * sa_ref[...].astype(F32)).astype(BF16)
        db = (ds * sb_ref[...].astype(F32)).astype(BF16)
        da_ref[...] = da
        db_ref[...] = db
        hv = h_ref[...]
        acc_g[...] += _dot_tn(da, hv)
        acc_u[...] += _dot_tn(db, hv)

        @pl.when(k == nk - 1)
        def _():
            dwg_ref[...] = acc_g[...].astype(BF16)
            dwu_ref[...] = acc_u[...].astype(BF16)

    tokens = pl.BlockSpec((tk, d), lambda j, k: (k, 0))
    wide = pl.BlockSpec((tk, tn), lambda j, k: (k, j))
    weight = pl.BlockSpec((tn, d), lambda j, k: (j, 0))
    return pl.pallas_call(
        body, name=name, grid=(f // tn, nk),
        in_specs=[tokens, wide, wide, tokens, weight],
        out_specs=[wide, wide, weight, weight],
        out_shape=[jax.ShapeDtypeStruct((t, f), BF16)] * 2 + [jax.ShapeDtypeStruct((f, d), BF16)] * 2,
        scratch_shapes=[pltpu.VMEM((tn, d), F32)] * 2,
        compiler_params=_params("arbitrary", "arbitrary"),
    )(dyh, sa, sb, h, wd)


def _ffn_bwd_dx(da, db, dy, x, gain, wg_t, wu_t, name):
    t, d = x.shape
    f = wg_t.shape[0]
    tm = 512

    def body(da_ref, db_ref, dy_ref, x_ref, g_ref, wg_ref, wu_ref, dx_ref, dg_ref):
        dh = _dot(da_ref[...], wg_ref[...]) + _dot(db_ref[...], wu_ref[...])
        dx, dgain = _rms_bwd(dh, x_ref[...], g_ref[...])
        dx_ref[...] = dy_ref[...] + dx

        @pl.when(pl.program_id(0) == 0)
        def _():
            dg_ref[...] = jnp.zeros_like(dg_ref)

        dg_ref[...] += dgain

    return pl.pallas_call(
        body, name=name, grid=(t // tm,),
        in_specs=[_rows(tm, f), _rows(tm, f), _rows(tm, d), _rows(tm, d), _resident((1, d)), _resident((f, d)),
                  _resident((f, d))],
        out_specs=[_rows(tm, d), pl.BlockSpec((1, d), lambda i: (0, 0))],
        out_shape=[jax.ShapeDtypeStruct((t, d), F32), jax.ShapeDtypeStruct((1, d), F32)],
        compiler_params=_params("arbitrary"),
    )(da, db, dy, x, gain, wg_t, wu_t)


def _wgrad(lhs, b, name):
    t, n = lhs[0].shape
    d = b.shape[1]
    m = len(lhs)
    tn = n // 2 if n * d * m > (4 << 20) else n
    tk = 1024
    nk = t // tk

    def body(*refs):
        a_refs, b_ref, o_refs, accs = refs[:m], refs[m], refs[m + 1:2 * m + 1], refs[2 * m + 1:]
        k = pl.program_id(1)

        @pl.when(k == 0)
        def _():
            for acc in accs:
                acc[...] = jnp.zeros_like(acc)

        bv = b_ref[...]
        for a_ref, acc in zip(a_refs, accs):
            acc[...] += _dot_tn(a_ref[...], bv)

        @pl.when(k == nk - 1)
        def _():
            for o_ref, acc in zip(o_refs, accs):
                o_ref[...] = acc[...].astype(BF16)

    return pl.pallas_call(
        body, name=name, grid=(n // tn, nk),
        in_specs=[pl.BlockSpec((tk, tn), lambda j, k: (k, j))] * m + [pl.BlockSpec((tk, d), lambda j, k: (k, 0))],
        out_specs=[pl.BlockSpec((tn, d), lambda j, k: (j, 0))] * m,
        out_shape=[jax.ShapeDtypeStruct((n, d), BF16)] * m,
        scratch_shapes=[pltpu.VMEM((tn, d), F32)] * m,
        compiler_params=_params("arbitrary", "arbitrary"),
    )(*lhs, b)


def _repack_rows(a, rows_in, rows_out, blocks, name):
    total, d = a.shape
    real = min(rows_in, rows_out)

    def body(a_ref, o_ref, wide_in, wide_out):
        wide_in[...] = a_ref[...].astype(F32)
        wide_out[...] = jnp.zeros_like(wide_out)
        for j in range(blocks):
            wide_out[pl.ds(j * rows_out, real), :] = wide_in[pl.ds(j * rows_in, real), :]
        o_ref[...] = wide_out[...].astype(BF16)

    full = pl.BlockSpec((total, d), lambda i: (0, 0))
    return pl.pallas_call(
        body, name=name, grid=(1,), in_specs=[full], out_specs=full, out_shape=jax.ShapeDtypeStruct((total, d), BF16),
        scratch_shapes=[pltpu.VMEM((total, d), F32)] * 2,
        compiler_params=_params("arbitrary"),
    )(a)


def _mix_in_fwd(x, gain, w_in_t):
    t, d = x.shape
    tm = 1024
    pw, aw = POOL_WIDTH, ATTN_WIDTH

    def body(x_ref, g_ref, w_ref, hm_ref, pv_ref, q_ref, k_ref, v_ref, f_ref):
        xv = x_ref[...]
        hm = (xv * _rms_scale(xv) * g_ref[...]).astype(BF16)
        hm_ref[...] = hm
        pv_ref[...] = _dot_nt(hm, w_ref[pl.ds(0, pw), :])
        q_ref[...] = _dot_nt(hm, w_ref[pl.ds(pw, aw), :])
        k_ref[...] = _dot_nt(hm, w_ref[pl.ds(pw + aw, aw), :])
        v_ref[...] = _dot_nt(hm, w_ref[pl.ds(pw + 2 * aw, aw), :]).astype(BF16)
        f_ref[...] = _dot_nt(hm, w_ref[pl.ds(pw + 3 * aw, LANES), :])

    return pl.pallas_call(
        body, name="mix_in_fwd", grid=(t // tm,),
        in_specs=[_rows(tm, d), _resident((1, d)), _resident((MIX_PAD, d))],
        out_specs=[_rows(tm, d), _rows(tm, pw), _rows(tm, aw), _rows(tm, aw), _rows(tm, aw), _rows(tm, LANES)],
        out_shape=[jax.ShapeDtypeStruct((t, d), BF16), jax.ShapeDtypeStruct((t, pw), F32),
                   jax.ShapeDtypeStruct((t, aw), F32), jax.ShapeDtypeStruct((t, aw), F32),
                   jax.ShapeDtypeStruct((t, aw), BF16), jax.ShapeDtypeStruct((t, LANES), F32)],
        compiler_params=_params("arbitrary"),
    )(x, gain, w_in_t)


def _pool_fwd(pv, pool_w, pool_scale, gain, bsz, seq):
    ts = 512
    ns = seq // ts
    pw = POOL_WIDTH

    def body(pv_ref, w_ref, sc_ref, g_ref, pooled_ref, mixed_ref, y_ref, ext):
        s = pl.program_id(1)

        @pl.when(s == 0)
        def _():
            ext[pl.ds(0, POOL_HALO), :] = jnp.zeros((POOL_HALO, pw), F32)

        p = pv_ref[...]
        ext[pl.ds(POOL_HALO, ts), :] = p
        pos = s * ts + lax.broadcasted_iota(jnp.int32, (ts, 1), 0)
        parts = []
        for g, w in enumerate(POOL_WINDOWS):
            lanes = pl.ds(g * POOL_GROUP_DIM, POOL_GROUP_DIM)
            win = ext[pl.ds(POOL_HALO, ts), lanes]
            for i in range(1, w):
                win = win + ext[pl.ds(POOL_HALO - i, ts), lanes]
            cnt = jnp.minimum(pos + 1, w).astype(F32)
            pooled = (win / cnt - ext[pl.ds(POOL_HALO, ts), lanes]).astype(BF16)
            pooled_ref[:, lanes] = pooled
            parts.append(_dot(pooled, w_ref[g].astype(BF16)))
        mixed = jnp.concatenate(parts, axis=1)
        mixed_ref[...] = mixed
        pm = mixed * sc_ref[...]
        y_ref[...] = (pm * _rms_scale(pm) * g_ref[...]).astype(BF16)
        ext[pl.ds(0, POOL_HALO), :] = p[ts - POOL_HALO:, :]

    blk = pl.BlockSpec((ts, pw), lambda b, s: (b * ns + s, 0))
    t = bsz * seq
    return pl.pallas_call(
        body, name="pool_fwd", grid=(bsz, ns),
        in_specs=[blk, pl.BlockSpec((POOL_GROUPS, POOL_GROUP_DIM, POOL_GROUP_DIM), lambda b, s: (0, 0, 0)),
                  pl.BlockSpec((1, pw), lambda b, s: (0, 0)), pl.BlockSpec((1, pw), lambda b, s: (0, 0))],
        out_specs=[blk, blk, blk],
        out_shape=[jax.ShapeDtypeStruct((t, pw), BF16), jax.ShapeDtypeStruct((t, pw), F32),
                   jax.ShapeDtypeStruct((t, pw), BF16)],
        scratch_shapes=[pltpu.VMEM((POOL_HALO + ts, pw), F32)],
        compiler_params=_params("arbitrary", "arbitrary"),
    )(pv, pool_w, pool_scale, gain)


def _pool_bwd(dy, mixed, pooled, pool_w, pool_scale, gain, bsz, seq):
    ts = 512
    ns = seq // ts
    pw = POOL_WIDTH

    def body(dy_ref, mixed_ref, pooled_ref, w_ref, sc_ref, g_ref, dpv_ref, dw_ref, dsc_ref, dg_ref, ext):
        b = pl.program_id(0)
        sr = pl.program_id(1)
        s = ns - 1 - sr

        @pl.when(jnp.logical_and(b == 0, sr == 0))
        def _():
            dw_ref[...] = jnp.zeros_like(dw_ref)
            dsc_ref[...] = jnp.zeros_like(dsc_ref)
            dg_ref[...] = jnp.zeros_like(dg_ref)

        @pl.when(sr == 0)
        def _():
            ext[pl.ds(ts, POOL_HALO), :] = jnp.zeros((POOL_HALO, pw), F32)

        mixed = mixed_ref[...]
        sc = sc_ref[...]
        dpm, dgain = _rms_bwd(dy_ref[...], mixed * sc, g_ref[...])
        dg_ref[...] += dgain
        dsc_ref[...] += jnp.sum(dpm * mixed, axis=0, keepdims=True)
        dmixed = (dpm * sc).astype(BF16)
        pos = s * ts + lax.broadcasted_iota(jnp.int32, (ts, 1), 0)
        dpooled = []
        for g, w in enumerate(POOL_WINDOWS):
            lanes = pl.ds(g * POOL_GROUP_DIM, POOL_GROUP_DIM)
            dm = dmixed[:, g * POOL_GROUP_DIM:(g + 1) * POOL_GROUP_DIM]
            dw_ref[g] += _dot_tn(pooled_ref[:, lanes], dm)
            dp = _dot_nt(dm, w_ref[g].astype(BF16))
            dpooled.append(dp)
            cnt = jnp.minimum(pos + 1, w).astype(F32)
            ext[pl.ds(0, ts), lanes] = dp / cnt
        for g, w in enumerate(POOL_WINDOWS):
            lanes = pl.ds(g * POOL_GROUP_DIM, POOL_GROUP_DIM)
            win = ext[pl.ds(0, ts), lanes]
            for i in range(1, w):
                win = win + ext[pl.ds(i, ts), lanes]
            dpv_ref[:, lanes] = (win - dpooled[g]).astype(BF16)
        head = ext[pl.ds(0, POOL_HALO), :]
        ext[pl.ds(ts, POOL_HALO), :] = head

    blk = pl.BlockSpec((ts, pw), lambda b, s: (b * ns + (ns - 1 - s), 0))
    vec = pl.BlockSpec((1, pw), lambda b, s: (0, 0))
    wspec = pl.BlockSpec((POOL_GROUPS, POOL_GROUP_DIM, POOL_GROUP_DIM), lambda b, s: (0, 0, 0))
    t = bsz * seq
    return pl.pallas_call(
        body, name="pool_bwd", grid=(bsz, ns),
        in_specs=[blk, blk, blk, wspec, vec, vec],
        out_specs=[blk, wspec, vec, vec],
        out_shape=[jax.ShapeDtypeStruct((t, pw), BF16),
                   jax.ShapeDtypeStruct((POOL_GROUPS, POOL_GROUP_DIM, POOL_GROUP_DIM), F32),
                   jax.ShapeDtypeStruct((1, pw), F32), jax.ShapeDtypeStruct((1, pw), F32)],
        scratch_shapes=[pltpu.VMEM((ts + POOL_HALO, pw), F32)],
        compiler_params=_params("arbitrary", "arbitrary"),
    )(dy, mixed, pooled, pool_w, pool_scale, gain)


AUX_ONE = 64
AUX_F = 67

ATTN_PREP_ROWS = 512


def _seg_ones(width, seg):
    r = lax.broadcasted_iota(jnp.int32, (width, width), 0) // seg
    c = lax.broadcasted_iota(jnp.int32, (width, width), 1) // seg
    return (r == c).astype(BF16)


def _tri_ones(n, lower):
    r = lax.broadcasted_iota(jnp.int32, (n, n), 0)
    c = lax.broadcasted_iota(jnp.int32, (n, n), 1)
    return ((r >= c) if lower else (r <= c)).astype(BF16)


def _place_pieces(first_lane):
    r = lax.broadcasted_iota(jnp.int32, (3 * LANES, N_HEADS * LANES), 0)
    c = lax.broadcasted_iota(jnp.int32, (3 * LANES, N_HEADS * LANES), 1)
    piece, head = r // LANES, r % LANES
    return jnp.logical_and(head < N_HEADS, c == head * LANES + first_lane + piece).astype(BF16)


def _head_sums(x, seg_ones):
    hi, lo = _split2(x)
    return _dot(hi, seg_ones) + _dot(lo, seg_ones)


def _log_sigmoid(x):
    return jnp.minimum(x, 0.0) - jnp.log(1.0 + jnp.exp(-jnp.abs(x)))


def _attn_prep_fwd(q, k, f, b_forget, q_gain, k_gain, bsz, seq):
    ts = ATTN_PREP_ROWS
    ns = seq // ts
    aw = ATTN_WIDTH
    t = bsz * seq
    seg = _seg_ones(aw, HEAD_DIM)
    tri = _tri_ones(ts, True)

    def body(q_ref, k_ref, f_ref, bf_ref, gq_ref, gk_ref, seg_ref, tri_ref, pq_ref, pk_ref, qp_ref, kp_ref, carry):
        s = pl.program_id(1)

        @pl.when(s == 0)
        def _():
            carry[...] = jnp.zeros_like(carry)

        logf = _log_sigmoid(f_ref[...] + bf_ref[...])
        hi, mid, lo = _split3(logf)
        tri_v = tri_ref[...]
        fc = _dot(tri_v, hi) + _dot(tri_v, mid) + _dot(tri_v, lo) + carry[pl.ds(0, 1), :]
        carry[pl.ds(0, 1), :] = fc[ts - 1:, :]
        pcs = jnp.concatenate(_split3(fc), axis=1)
        lane = lax.broadcasted_iota(jnp.int32, (1, LANES), 1)
        ones_q = jnp.logical_and(lane >= AUX_ONE, lane < AUX_ONE + 3).astype(F32)
        ones_k = jnp.logical_and(lane >= AUX_F, lane < AUX_F + 3).astype(F32)
        seg_v = seg_ref[...]

        def build(x_ref, g_ref, scale, out_ref, ones, place_ref, f_sign):
            xv = x_ref[...]
            r = lax.rsqrt(_head_sums(xv * xv, seg_v) * (1.0 / HEAD_DIM) + EPS)
            xn = xv * r * g_ref[...] * scale
            aux = _dot(pcs, place_ref[...]) * f_sign
            for h in range(N_HEADS):
                pair = xn[:, (h // 2) * LANES:(h // 2 + 1) * LANES]
                feat = pair if h % 2 == 0 else pltpu.roll(pair, HEAD_DIM, 1)
                aux_h = aux[:, h * LANES:(h + 1) * LANES] + ones
                out_ref[:, h * LANES:(h + 1) * LANES] = jnp.where(lane < HEAD_DIM, feat, aux_h).astype(BF16)

        build(q_ref, gq_ref, 0.125, qp_ref, ones_q, pq_ref, 1.0)
        build(k_ref, gk_ref, 1.0, kp_ref, ones_k, pk_ref, -1.0)

    blk = pl.BlockSpec((ts, aw), lambda b, s: (b * ns + s, 0))
    fblk = pl.BlockSpec((ts, LANES), lambda b, s: (b * ns + s, 0))
    oblk = pl.BlockSpec((ts, N_HEADS * LANES), lambda b, s: (b * ns + s, 0))
    const = lambda shape: pl.BlockSpec(shape, lambda b, s: (0, 0))
    return pl.pallas_call(
        body, name="attn_prep_fwd", grid=(bsz, ns),
        in_specs=[blk, blk, fblk, const((1, LANES)), const((1, aw)), const((1, aw)), const((aw, aw)), const((ts, ts)),
                  const((3 * LANES, N_HEADS * LANES)), const((3 * LANES, N_HEADS * LANES))],
        out_specs=[oblk, oblk],
        out_shape=[jax.ShapeDtypeStruct((t, N_HEADS * LANES), BF16)] * 2,
        scratch_shapes=[pltpu.VMEM((8, LANES), F32)],
        compiler_params=_params("arbitrary", "arbitrary"),
    )(q, k, f, b_forget, q_gain, k_gain, seg, tri, _place_pieces(AUX_F), _place_pieces(AUX_ONE))


def _attn_prep_bwd(dqp, dkp, q, k, f, b_forget, q_gain, k_gain, bsz, seq):
    ts = ATTN_PREP_ROWS
    ns = seq // ts
    aw = ATTN_WIDTH
    t = bsz * seq
    seg = _seg_ones(aw, HEAD_DIM)
    tri = _tri_ones(ts, False)

    def body(dqp_ref, dkp_ref, q_ref, k_ref, f_ref, bf_ref, gq_ref, gk_ref, seg_ref, tri_ref,
             dq_ref, dk_ref, df_ref, dgq_ref, dgk_ref, dbf_ref, carry):
        b = pl.program_id(0)
        sr = pl.program_id(1)

        @pl.when(jnp.logical_and(b == 0, sr == 0))
        def _():
            dgq_ref[...] = jnp.zeros_like(dgq_ref)
            dgk_ref[...] = jnp.zeros_like(dgk_ref)
            dbf_ref[...] = jnp.zeros_like(dbf_ref)

        @pl.when(sr == 0)
        def _():
            carry[...] = jnp.zeros_like(carry)

        lane = lax.broadcasted_iota(jnp.int32, (1, LANES), 1)
        seg_v = seg_ref[...]

        def norm_bwd(dp_ref, x_ref, g_ref, scale, dx_ref, dgain_ref):
            parts = []
            for j in range(N_HEADS // 2):
                even = dp_ref[:, (2 * j) * LANES:(2 * j + 1) * LANES]
                odd = dp_ref[:, (2 * j + 1) * LANES:(2 * j + 2) * LANES]
                parts.append(jnp.where(lane < HEAD_DIM, even, pltpu.roll(odd, HEAD_DIM, 1)))
            dxn = jnp.concatenate(parts, axis=1) * scale
            xv = x_ref[...]
            r = lax.rsqrt(_head_sums(xv * xv, seg_v) * (1.0 / HEAD_DIM) + EPS)
            n = xv * r
            dgain_ref[...] += jnp.sum(dxn * n, axis=0, keepdims=True)
            dn = dxn * g_ref[...]
            m = _head_sums(dn * n, seg_v) * (1.0 / HEAD_DIM)
            dx_ref[...] = (r * (dn - n * m)).astype(BF16)

        norm_bwd(dqp_ref, q_ref, gq_ref, 0.125, dq_ref, dgq_ref)
        norm_bwd(dkp_ref, k_ref, gk_ref, 1.0, dk_ref, dgk_ref)

        dfc = jnp.zeros((ts, LANES), F32)
        for h in range(N_HEADS):
            cols = pl.ds(h * LANES, LANES)
            both = jnp.where(lane == AUX_F, dqp_ref[:, cols], 0.0) - jnp.where(lane == AUX_ONE, dkp_ref[:, cols], 0.0)
            dfc = jnp.where(lane == h, jnp.sum(both, axis=1, keepdims=True), dfc)
        hi, mid, lo = _split3(dfc)
        tri_v = tri_ref[...]
        dlogf = _dot(tri_v, hi) + _dot(tri_v, mid) + _dot(tri_v, lo) + carry[pl.ds(0, 1), :]
        carry[pl.ds(0, 1), :] = dlogf[0:1, :]
        df = jnp.where(lane < N_HEADS, dlogf * jax.nn.sigmoid(-(f_ref[...] + bf_ref[...])), 0.0)
        df_ref[...] = df.astype(BF16)
        dbf_ref[...] += jnp.sum(df, axis=0, keepdims=True)

    rev = lambda b, s: (b * ns + (ns - 1 - s), 0)
    blk = pl.BlockSpec((ts, aw), rev)
    fblk = pl.BlockSpec((ts, LANES), rev)
    pblk = pl.BlockSpec((ts, N_HEADS * LANES), rev)
    const = lambda shape: pl.BlockSpec(shape, lambda b, s: (0, 0))
    return pl.pallas_call(
        body, name="attn_prep_bwd", grid=(bsz, ns),
        in_specs=[pblk, pblk, blk, blk, fblk, const((1, LANES)), const((1, aw)), const((1, aw)), const((aw, aw)),
                  const((ts, ts))],
        out_specs=[blk, blk, fblk, const((1, aw)), const((1, aw)), const((1, LANES))],
        out_shape=[jax.ShapeDtypeStruct((t, aw), BF16), jax.ShapeDtypeStruct((t, aw), BF16),
                   jax.ShapeDtypeStruct((t, LANES), BF16), jax.ShapeDtypeStruct((1, aw), F32),
                   jax.ShapeDtypeStruct((1, aw), F32), jax.ShapeDtypeStruct((1, LANES), F32)],
        scratch_shapes=[pltpu.VMEM((8, LANES), F32)],
        compiler_params=_params("arbitrary", "arbitrary"),
    )(dqp, dkp, q, k, f, b_forget, q_gain, k_gain, seg, tri)


ATTN_BLOCK = 1024
HEAD_PAIRS = N_HEADS // 2


def _flash_fwd(qp, kp, v, bsz, seq):
    tq = ATTN_BLOCK
    half = tq // 2
    nq = seq // tq
    t = bsz * seq

    def body(q_ref, k_ref, v_ref, o_ref, lse_ref, m_sc, l_sc, acc_sc):
        i = pl.program_id(2)
        m_sc[...] = jnp.full(m_sc.shape, -jnp.inf, F32)
        l_sc[...] = jnp.zeros_like(l_sc)
        acc_sc[...] = jnp.zeros_like(acc_sc)
        lane = lax.broadcasted_iota(jnp.int32, (1, LANES), 1)
        low = lane < HEAD_DIM

        def tile(q0, qn, k_start, kn, k0=None):
            qs = pl.ds(q0, qn)
            ks = pl.ds(k_start, kn)
            vv = v_ref[ks, :]
            for h in range(2):
                mine = low if h == 0 else jnp.logical_not(low)
                cols = pl.ds(h * LANES, LANES)
                s = _dot_nt(q_ref[qs, cols], k_ref[ks, cols])
                if k0 is not None:
                    row = lax.broadcasted_iota(jnp.int32, (qn, kn), 0) + q0
                    col = lax.broadcasted_iota(jnp.int32, (qn, kn), 1) + k0
                    s = jnp.where(row >= col, s, -jnp.inf)
                m_prev = m_sc[h, qs, :]
                m_new = jnp.maximum(m_prev, jnp.max(s, axis=1, keepdims=True))
                p = jnp.exp(s - jnp.tile(m_new, (1, kn // LANES)))
                alpha = jnp.exp(m_prev - m_new)
                l_sc[h, qs, :] = alpha * l_sc[h, qs, :] + jnp.sum(p, axis=1, keepdims=True)
                m_sc[h, qs, :] = m_new
                pv = _dot(p.astype(BF16), jnp.where(mine, vv, jnp.zeros_like(vv)))
                acc_sc[qs, :] = acc_sc[qs, :] * jnp.where(mine, alpha, 1.0) + pv

        def below_diagonal(j, carry):
            tile(0, tq, pl.multiple_of(j * tq, tq), tq)
            return carry

        lax.fori_loop(0, i, below_diagonal, 0)
        diagonal = pl.multiple_of(i * tq, tq)
        tile(0, tq, diagonal, half, k0=0)
        tile(half, half, diagonal + half, half, k0=half)
        l = jnp.where(low, l_sc[0], l_sc[1])
        m = jnp.where(low, m_sc[0], m_sc[1])
        o_ref[...] = acc_sc[...] / l
        lse_ref[...] = m + jnp.log(l)

    qspec = pl.BlockSpec((tq, 2 * LANES), lambda b, hp, i: (b * nq + i, hp))
    kspec = pl.BlockSpec((seq, 2 * LANES), lambda b, hp, i: (b, hp))
    vspec = pl.BlockSpec((seq, LANES), lambda b, hp, i: (b, hp))
    ospec = pl.BlockSpec((tq, LANES), lambda b, hp, i: (b * nq + i, hp))
    return pl.pallas_call(
        body, name="flash_fwd", grid=(bsz, HEAD_PAIRS, nq),
        in_specs=[qspec, kspec, vspec], out_specs=[ospec, ospec],
        out_shape=[jax.ShapeDtypeStruct((t, ATTN_WIDTH), F32), jax.ShapeDtypeStruct((t, ATTN_WIDTH), F32)],
        scratch_shapes=[pltpu.VMEM((2, tq, LANES), F32), pltpu.VMEM((2, tq, LANES), F32), pltpu.VMEM((tq, LANES), F32)],
        compiler_params=_params("arbitrary", "arbitrary", "arbitrary"),
    )(qp, kp, v)


def _flash_bwd(qp, kp, v, o, do, lse, bsz, seq):
    tq = ATTN_BLOCK
    half = tq // 2
    nq = seq // tq
    t = bsz * seq

    def body(q_ref, k_ref, v_ref, o_ref, do_ref, lse_ref, dq_ref, dk_ref, dv_ref, dk_acc, dv_acc):
        j = pl.program_id(2)

        @pl.when(j == 0)
        def _():
            dq_ref[...] = jnp.zeros_like(dq_ref)

        dk_acc[...] = jnp.zeros_like(dk_acc)
        dv_acc[...] = jnp.zeros_like(dv_acc)
        lane = lax.broadcasted_iota(jnp.int32, (1, LANES), 1)
        low = lane < HEAD_DIM

        def tile(q_start, qn, k0, kn, q0=None):
            rows = pl.ds(q_start, qn)
            ks = pl.ds(k0, kn)
            dov = do_ref[rows, :]
            dd = dov * o_ref[rows, :]
            dob = dov.astype(BF16)
            vv = v_ref[ks, :]
            lse_v = lse_ref[rows, :]
            for h in range(2):
                mine = low if h == 0 else jnp.logical_not(low)
                cols = pl.ds(h * LANES, LANES)
                qh = q_ref[rows, cols]
                kh = k_ref[ks, cols]
                s = _dot_nt(qh, kh)
                lse_h = jnp.where(mine, lse_v, pltpu.roll(lse_v, HEAD_DIM, 1))
                p = jnp.exp(s - jnp.tile(lse_h, (1, kn // LANES)))
                if q0 is not None:
                    row = lax.broadcasted_iota(jnp.int32, (qn, kn), 0) + q0
                    col = lax.broadcasted_iota(jnp.int32, (qn, kn), 1) + k0
                    p = jnp.where(row >= col, p, 0.0)
                delta = jnp.sum(jnp.where(mine, dd, 0.0), axis=1, keepdims=True)
                dp = _dot_nt(dob, jnp.where(mine, vv, jnp.zeros_like(vv)))
                ds = (p * (dp - delta)).astype(BF16)
                dv_acc[ks, :] += jnp.where(mine, _dot_tn(p.astype(BF16), dob), 0.0)
                dk_acc[ks, cols] += _dot_tn(ds, qh)
                dq_ref[rows, cols] += _dot(ds, kh)

        def above_diagonal(i, carry):
            tile(pl.multiple_of(i * tq, tq), tq, 0, tq)
            return carry

        diagonal = pl.multiple_of(j * tq, tq)
        tile(diagonal, tq, 0, half, q0=0)
        tile(diagonal + half, half, half, half, q0=half)
        lax.fori_loop(j + 1, nq, above_diagonal, 0)
        dk_ref[...] = dk_acc[...]
        dv_ref[...] = dv_acc[...].astype(BF16)

    qspec = pl.BlockSpec((seq, 2 * LANES), lambda b, hp, j: (b, hp))
    kspec = pl.BlockSpec((tq, 2 * LANES), lambda b, hp, j: (b * nq + j, hp))
    vspec = pl.BlockSpec((tq, LANES), lambda b, hp, j: (b * nq + j, hp))
    ospec = pl.BlockSpec((seq, LANES), lambda b, hp, j: (b, hp))
    return pl.pallas_call(
        body, name="flash_bwd", grid=(bsz, HEAD_PAIRS, nq),
        in_specs=[qspec, kspec, vspec, ospec, ospec, ospec], out_specs=[qspec, kspec, vspec],
        out_shape=[jax.ShapeDtypeStruct((t, N_HEADS * LANES), F32), jax.ShapeDtypeStruct((t, N_HEADS * LANES), F32),
                   jax.ShapeDtypeStruct((t, ATTN_WIDTH), BF16)],
        scratch_shapes=[pltpu.VMEM((tq, 2 * LANES), F32), pltpu.VMEM((tq, LANES), F32)],
        compiler_params=_params("arbitrary", "arbitrary", "arbitrary"),
    )(qp, kp, v, o, do, lse)


def _mix_out_fwd(o, y_pool, x, gain, w_out):
    t, d = x.shape
    tm = 1024
    pw, aw = POOL_WIDTH, ATTN_WIDTH

    def body(o_ref, yp_ref, x_ref, g_ref, w_ref, ycat_ref, y_ref):
        ov = o_ref[...]
        ya = (ov * _rms_scale(ov) * g_ref[...]).astype(BF16)
        ycat = jnp.concatenate([yp_ref[...], ya], axis=1)
        ycat_ref[...] = ycat
        y_ref[...] = x_ref[...] + _dot(ycat, w_ref[...])

    return pl.pallas_call(
        body, name="mix_out_fwd", grid=(t // tm,),
        in_specs=[_rows(tm, aw), _rows(tm, pw), _rows(tm, d), _resident((1, aw)), _resident((pw + aw, d))],
        out_specs=[_rows(tm, pw + aw), _rows(tm, d)],
        out_shape=[jax.ShapeDtypeStruct((t, pw + aw), BF16), jax.ShapeDtypeStruct((t, d), F32)],
        compiler_params=_params("arbitrary"),
    )(o, y_pool, x, gain, w_out)


def _mix_out_bwd(dx, o, ycat, gain, w_out):
    t, d = dx.shape
    tm = 1024
    nm = t // tm
    pw, aw = POOL_WIDTH, ATTN_WIDTH

    def body(dx_ref, o_ref, ycat_ref, g_ref, w_ref, dw_ref, dyp_ref, do_ref, dg_ref, acc):
        i = pl.program_id(0)

        @pl.when(i == 0)
        def _():
            dg_ref[...] = jnp.zeros_like(dg_ref)
            acc[...] = jnp.zeros_like(acc)

        dxb = dx_ref[...].astype(BF16)
        acc[...] += _dot_tn(ycat_ref[...], dxb)
        dyp_ref[...] = _dot_nt(dxb, w_ref[pl.ds(0, pw), :])
        dya = _dot_nt(dxb, w_ref[pl.ds(pw, aw), :])
        do, dgain = _rms_bwd(dya, o_ref[...], g_ref[...])
        do_ref[...] = do
        dg_ref[...] += dgain

        @pl.when(i == nm - 1)
        def _():
            dw_ref[...] = acc[...].astype(BF16)

    return pl.pallas_call(
        body, name="mix_out_bwd", grid=(nm,),
        in_specs=[_rows(tm, d), _rows(tm, aw), _rows(tm, pw + aw), _resident((1, aw)), _resident((pw + aw, d))],
        out_specs=[pl.BlockSpec((pw + aw, d), lambda i: (0, 0)), _rows(tm, pw), _rows(tm, aw),
                   pl.BlockSpec((1, aw), lambda i: (0, 0))],
        out_shape=[jax.ShapeDtypeStruct((pw + aw, d), BF16), jax.ShapeDtypeStruct((t, pw), F32),
                   jax.ShapeDtypeStruct((t, aw), F32), jax.ShapeDtypeStruct((1, aw), F32)],
        scratch_shapes=[pltpu.VMEM((pw + aw, d), F32)],
        compiler_params=_params("arbitrary"),
    )(dx, o, ycat, gain, w_out)


def _mix_in_bwd(dpv, dq, dk, dv, df, hm, x, dx_res, gain, w_in_t):
    t, d = x.shape
    tm = 512
    nm = t // tm
    pw, aw = POOL_WIDTH, ATTN_WIDTH

    def body(dpv_ref, dq_ref, dk_ref, dv_ref, df_ref, hm_ref, x_ref, dxr_ref, g_ref, w_ref, dw_ref, dx_ref, dxh_ref,
             dg_ref, acc):
        i = pl.program_id(0)

        @pl.when(i == 0)
        def _():
            dg_ref[...] = jnp.zeros_like(dg_ref)
            acc[...] = jnp.zeros_like(acc)

        dh = jnp.concatenate([dpv_ref[...], dq_ref[...], dk_ref[...], dv_ref[...], df_ref[...]], axis=1)
        acc[...] += _dot_tn(dh, hm_ref[...])
        dx, dgain = _rms_bwd(_dot(dh, w_ref[...]), x_ref[...], g_ref[...])
        dx = dxr_ref[...] + dx
        dx_ref[...] = dx
        dxh_ref[...] = (0.5 * dx).astype(BF16)
        dg_ref[...] += dgain

        @pl.when(i == nm - 1)
        def _():
            dw_ref[...] = acc[...].astype(BF16)

    return pl.pallas_call(
        body, name="mix_in_bwd", grid=(nm,),
        in_specs=[_rows(tm, pw), _rows(tm, aw), _rows(tm, aw), _rows(tm, aw), _rows(tm, LANES), _rows(tm, d),
                  _rows(tm, d), _rows(tm, d), _resident((1, d)), _resident((MIX_PAD, d))],
        out_specs=[pl.BlockSpec((MIX_PAD, d), lambda i: (0, 0)), _rows(tm, d), _rows(tm, d),
                   pl.BlockSpec((1, d), lambda i: (0, 0))],
        out_shape=[jax.ShapeDtypeStruct((MIX_PAD, d), BF16), jax.ShapeDtypeStruct((t, d), F32),
                   jax.ShapeDtypeStruct((t, d), BF16), jax.ShapeDtypeStruct((1, d), F32)],
        scratch_shapes=[pltpu.VMEM((MIX_PAD, d), F32)],
        compiler_params=_params("arbitrary"),
    )(dpv, dq, dk, dv, df, hm, x, dx_res, gain, w_in_t)


MESH_IDS = pl.DeviceIdType.MESH


def _me():
    return lax.axis_index("x"), lax.axis_index("y"), lax.axis_index("c")


def _peer(x, y, c, p):
    px = 1 - x if p & 4 else x
    py = 1 - y if p & 2 else y
    pc = 1 - c if p & 1 else c
    return (px, py, pc), 4 * px + 2 * py + pc


HBM_SPEC = pl.BlockSpec(memory_space=pltpu.HBM)
SEM_SPEC = pl.BlockSpec(memory_space=pltpu.SEMAPHORE)
SPLIT_COPY = pltpu.CompilerParams(has_side_effects=pltpu.SideEffectType.DATAFLOW_SIDE_EFFECTING)
PEERS = N_DEV - 1


def _hbm(a):
    return pltpu.with_memory_space_constraint(a, pltpu.HBM)


def _row_block(ref, dev, rows):
    return ref.at[pl.ds(pl.multiple_of(dev * rows, BF16_ROWS), rows)]


def _copy_ends(gather, src, land, me, peer_id):
    if gather:
        rows = src.shape[0]
        return src, _row_block(land, me, rows), _row_block(land, peer_id, rows), src, _row_block(land, me, rows)
    rows = src.shape[0] // N_DEV
    return (_row_block(src, peer_id, rows), land.at[me], land.at[peer_id], _row_block(src, me, rows), land.at[me])


def _land_shape(gather, s):
    return (N_DEV * s.shape[0], s.shape[1]) if gather else (N_DEV, s.shape[0] // N_DEV, s.shape[1])


SIBLING = 1
SAME_CORE_PEERS = (2, 4, 6)
RELAYS = len(SAME_CORE_PEERS)


def _copies_start(groups, gather, name, after=None, relayed=()):
    flat = [s for g in groups for s in g]
    n, ng = len(flat), len(groups)
    lands = [lax.empty(_land_shape(gather, s), s.dtype) for s in flat]
    n_in = 2 * n + (after is not None)

    def body(*refs):
        ins, lnd = refs[:n], refs[n:2 * n]
        sems = refs[n_in:n_in + 2 * ng]
        token = refs[-1]
        x, y, c = _me()
        me = 4 * x + 2 * y + c
        w = 0
        for gi, g in enumerate(groups):
            for k in range(len(g)):
                for p in ((SIBLING,) + SAME_CORE_PEERS if gi in relayed else range(1, N_DEV)):
                    peer, peer_id = _peer(x, y, c, p)
                    src, dst, _, _, _ = _copy_ends(gather, ins[w], lnd[w], me, peer_id)
                    pltpu.make_async_remote_copy(src, dst, sems[2 * gi].at[k * PEERS + p - 1],
                                                 sems[2 * gi + 1].at[k * PEERS + p - 1], device_id=peer,
                                                 device_id_type=MESH_IDS).start()
                w += 1
        token[...] = jnp.zeros_like(token)

    sem_shapes = []
    for g in groups:
        sem_shapes += [pltpu.SemaphoreType.DMA((len(g) * PEERS,))] * 2
    out = pl.pallas_call(
        body, name=name,
        out_shape=(*sem_shapes, *[pltpu.HBM(s.shape, s.dtype) for s in flat],
                   *[pltpu.HBM(l.shape, l.dtype) for l in lands], jax.ShapeDtypeStruct((8, LANES), F32)),
        in_specs=[HBM_SPEC] * (2 * n) + [pl.BlockSpec(memory_space=pl.ANY)] * (after is not None),
        out_specs=(*[SEM_SPEC] * (2 * ng), *[HBM_SPEC] * (2 * n), pl.BlockSpec(memory_space=pltpu.VMEM)),
        input_output_aliases={i: 2 * ng + i for i in range(2 * n)},
        compiler_params=SPLIT_COPY,
    )(*[_hbm(s) for s in flat], *[_hbm(l) for l in lands], *([after] if after is not None else []))
    sems, thru, token = out[:2 * ng], out[2 * ng:2 * ng + 2 * n], out[-1]
    res, w = [], 0
    for gi, g in enumerate(groups):
        res.append((sems[2 * gi], sems[2 * gi + 1], list(thru[w:w + len(g)]), list(thru[n + w:n + w + len(g)])))
        w += len(g)
    return res, token


def _copies_wait(started, gather, after, name):
    send, recv, srcs, lands = started
    n = len(srcs)
    after = list(after) if isinstance(after, (list, tuple)) else [after]

    own_shapes = [s.shape if gather else (s.shape[0] // N_DEV, s.shape[1]) for s in srcs]

    def body(*refs):
        ins, lnd = refs[:n], refs[n:2 * n]
        send_sems, recv_sems = refs[2 * n], refs[2 * n + 1]
        bounce, in_sems, out_sems = refs[-n - 2:-2], refs[-2], refs[-1]
        x, y, c = _me()
        me = 4 * x + 2 * y + c
        ends = [_copy_ends(gather, ins[w], lnd[w], me, me)[3:] for w in range(n)]
        loads = [pltpu.make_async_copy(ends[w][0], bounce[w], in_sems.at[w]) for w in range(n)]
        stores = [pltpu.make_async_copy(bounce[w], ends[w][1], out_sems.at[w]) for w in range(n)]
        for cp in loads:
            cp.start()
        for w in range(n):
            loads[w].wait()
            stores[w].start()
        for w in range(n):
            for p in range(1, N_DEV):
                peer, peer_id = _peer(x, y, c, p)
                src, _, arrival, _, _ = _copy_ends(gather, ins[w], lnd[w], me, peer_id)
                cp = pltpu.make_async_remote_copy(src, arrival, send_sems.at[w * PEERS + p - 1],
                                                  recv_sems.at[w * PEERS + p - 1], device_id=peer,
                                                  device_id_type=MESH_IDS)
                cp.wait_send()
                cp.wait_recv()
        for cp in stores:
            cp.wait()

    out = pl.pallas_call(
        body, name=name,
        out_shape=(*[pltpu.HBM(s.shape, s.dtype) for s in srcs], *[pltpu.HBM(l.shape, l.dtype) for l in lands]),
        in_specs=[HBM_SPEC] * (2 * n) + [SEM_SPEC, SEM_SPEC] + [pl.BlockSpec(memory_space=pl.ANY)] * len(after),
        out_specs=[HBM_SPEC] * (2 * n),
        input_output_aliases={i: i for i in range(2 * n)},
        scratch_shapes=[*[pltpu.VMEM(shape, s.dtype) for shape, s in zip(own_shapes, srcs)],
                        pltpu.SemaphoreType.DMA((n,)), pltpu.SemaphoreType.DMA((n,))],
        compiler_params=SPLIT_COPY,
    )(*srcs, *lands, send, recv, *after)
    return list(out[n:])


def _relay_to_sibling(started, name, after=None):
    send, recv, srcs, lands = started
    n = len(srcs)
    after = [] if after is None else [after]

    def body(*refs):
        ins, lnd = refs[:n], refs[n:2 * n]
        send_sems, recv_sems = refs[2 * n], refs[2 * n + 1]
        relay_send, relay_recv = refs[2 * n + 2 + len(after)], refs[2 * n + 3 + len(after)]
        x, y, c = _me()
        sibling, _ = _peer(x, y, c, SIBLING)
        for w in range(n):
            rows = ins[w].shape[0]
            for k, p in enumerate(SAME_CORE_PEERS):
                peer, peer_id = _peer(x, y, c, p)
                arrived = _row_block(lnd[w], peer_id, rows)
                first = pltpu.make_async_remote_copy(ins[w], arrived, send_sems.at[w * PEERS + p - 1],
                                                     recv_sems.at[w * PEERS + p - 1], device_id=peer,
                                                     device_id_type=MESH_IDS)
                first.wait_recv()
                pltpu.make_async_remote_copy(arrived, arrived, relay_send.at[w * RELAYS + k],
                                             relay_recv.at[w * RELAYS + k], device_id=sibling,
                                             device_id_type=MESH_IDS).start()
                first.wait_send()

    sems = pltpu.SemaphoreType.DMA((n * RELAYS,))
    out = pl.pallas_call(
        body, name=name,
        out_shape=(sems, sems, *[pltpu.HBM(s.shape, s.dtype) for s in srcs], *[pltpu.HBM(l.shape, l.dtype) for l in lands]),
        in_specs=[HBM_SPEC] * (2 * n) + [SEM_SPEC, SEM_SPEC] + [pl.BlockSpec(memory_space=pl.ANY)] * len(after),
        out_specs=(SEM_SPEC, SEM_SPEC, *[HBM_SPEC] * (2 * n)),
        input_output_aliases={i: 2 + i for i in range(2 * n)},
        compiler_params=SPLIT_COPY,
    )(*srcs, *lands, send, recv, *after)
    return send, recv, out[0], out[1], list(out[2:2 + n]), list(out[2 + n:])


def _relayed_wait(relayed, after, name):
    send, recv, relay_send, relay_recv, srcs, lands = relayed
    n = len(srcs)
    after = list(after) if isinstance(after, (list, tuple)) else [after]

    def body(*refs):
        ins, lnd = refs[:n], refs[n:2 * n]
        send_sems, recv_sems, relay_send_sems, relay_recv_sems = refs[2 * n:2 * n + 4]
        bounce, in_sems, out_sems = refs[-n - 2:-2], refs[-2], refs[-1]
        x, y, c = _me()
        me = 4 * x + 2 * y + c
        sibling, sibling_id = _peer(x, y, c, SIBLING)
        loads = [pltpu.make_async_copy(ins[w], bounce[w], in_sems.at[w]) for w in range(n)]
        stores = [pltpu.make_async_copy(bounce[w], _row_block(lnd[w], me, ins[w].shape[0]), out_sems.at[w])
                  for w in range(n)]
        for cp in loads:
            cp.start()
        for w in range(n):
            loads[w].wait()
            stores[w].start()
        for w in range(n):
            rows = ins[w].shape[0]
            direct = pltpu.make_async_remote_copy(ins[w], _row_block(lnd[w], sibling_id, rows),
                                                  send_sems.at[w * PEERS + SIBLING - 1],
                                                  recv_sems.at[w * PEERS + SIBLING - 1], device_id=sibling,
                                                  device_id_type=MESH_IDS)
            direct.wait_send()
            direct.wait_recv()
            for k, p in enumerate(SAME_CORE_PEERS):
                _, sent_id = _peer(x, y, c, p)
                _, got_id = _peer(x, y, c, p + SIBLING)
                relay = pltpu.make_async_remote_copy(_row_block(lnd[w], sent_id, rows), _row_block(lnd[w], got_id, rows),
                                                     relay_send_sems.at[w * RELAYS + k],
                                                     relay_recv_sems.at[w * RELAYS + k], device_id=sibling,
                                                     device_id_type=MESH_IDS)
                relay.wait_send()
                relay.wait_recv()
        for cp in stores:
            cp.wait()

    out = pl.pallas_call(
        body, name=name,
        out_shape=(*[pltpu.HBM(s.shape, s.dtype) for s in srcs], *[pltpu.HBM(l.shape, l.dtype) for l in lands]),
        in_specs=[HBM_SPEC] * (2 * n) + [SEM_SPEC] * 4 + [pl.BlockSpec(memory_space=pl.ANY)] * len(after),
        out_specs=[HBM_SPEC] * (2 * n),
        input_output_aliases={i: i for i in range(2 * n)},
        scratch_shapes=[*[pltpu.VMEM(s.shape, s.dtype) for s in srcs],
                        pltpu.SemaphoreType.DMA((n,)), pltpu.SemaphoreType.DMA((n,))],
        compiler_params=SPLIT_COPY,
    )(*srcs, *lands, send, recv, relay_send, relay_recv, *after)
    return list(out[n:])


def _adamw_update(w, g, m, v):
    nm = ADAM_B1 * m + (1.0 - ADAM_B1) * g
    nv = ADAM_B2 * v + (1.0 - ADAM_B2) * (g * g)
    m_hat = nm / (1.0 - ADAM_B1 ** ADAM_STEP)
    v_hat = nv / (1.0 - ADAM_B2 ** ADAM_STEP)
    return -ADAM_LR * (m_hat / (jnp.sqrt(v_hat) + ADAM_EPS) + ADAM_WD * w), nm, nv


SUM_ADAMW_COLS = 512


def _sum_adamw(parts, w, m, v, name):
    _, rows, d = parts.shape
    n = w.shape[0]
    tc = SUM_ADAMW_COLS

    def body(p_ref, w_ref, m_ref, v_ref, g_ref, d_ref, nm_ref, nv_ref):
        g = p_ref[0].astype(F32)
        for dev in range(1, N_DEV):
            g = g + p_ref[dev].astype(F32)
        g = g[:n]
        g_ref[...] = g
        d_ref[...], nm_ref[...], nv_ref[...] = _adamw_update(w_ref[...], g, m_ref[...], v_ref[...])

    spec = pl.BlockSpec((n, tc), lambda j: (0, j))
    shape = jax.ShapeDtypeStruct((n, d), F32)
    return pl.pallas_call(
        body, name=name, grid=(d // tc,),
        in_specs=[pl.BlockSpec((N_DEV, rows, tc), lambda j: (0, 0, j)), spec, spec, spec],
        out_specs=[spec] * 4, out_shape=[shape] * 4,
        compiler_params=_params("arbitrary"),
    )(parts, w, m, v)


def _pad_rows(a, rows):
    return jnp.pad(a, ((0, rows - a.shape[0]), (0, 0)))


def _row1(vec, width=D_MODEL):
    return jnp.pad(vec.reshape(1, -1), ((0, 0), (0, width - vec.shape[-1])))


COLUMN_SHARDED = ("ffn1_w_gate", "ffn1_w_up", "w_in", "ffn2_w_gate", "ffn2_w_up")
VEC_NAMES = ("ffn1_norm", "mix_norm", "ffn2_norm", "b_forget", "pool_scale", "q_norm", "k_norm", "out_norm_pool",
             "out_norm_attn")
VEC_ROWS = 16
LOSS_ROW = len(VEC_NAMES)


def _pack_vector_grads(parts, loss_part, name):
    names = [n for n in VEC_NAMES if n in parts]
    extra = [] if loss_part is None else [loss_part]

    def body(*refs):
        out_ref = refs[-1]
        out_ref[...] = jnp.zeros_like(out_ref)
        lane = lax.broadcasted_iota(jnp.int32, (1, LANES), 1)
        for n, ref in zip(names, refs):
            val = ref[...]
            if n in ("q_norm", "k_norm"):
                val = val[:, 0:LANES] + val[:, LANES:2 * LANES] + val[:, 2 * LANES:3 * LANES] + val[:, 3 * LANES:]
                val = jnp.where(lane < HEAD_DIM, val + pltpu.roll(val, HEAD_DIM, 1), 0.0)
            out_ref[pl.ds(VEC_NAMES.index(n), 1), pl.ds(0, val.shape[1])] = val
        if extra:
            out_ref[pl.ds(LOSS_ROW, 1), pl.ds(0, 1)] = refs[len(names)][...]

    vmem = pl.BlockSpec(memory_space=pltpu.VMEM)
    return pl.pallas_call(
        body, name=name, in_specs=[vmem] * (len(names) + len(extra)), out_specs=vmem,
        out_shape=jax.ShapeDtypeStruct((VEC_ROWS, D_MODEL), F32),
    )(*[parts[n] for n in names], *extra)


def _small_adamw(vec_all, pool_all, vec_params, pool_params):
    nv = len(vec_params)
    pool_rows = pool_params[0].shape[0]

    def body(*refs):
        vec_ref, pool_ref = refs[0], refs[1]
        ins = refs[2:2 + 3 * nv + 3]
        outs = refs[2 + 3 * nv + 3:-1]
        rows = refs[-1]
        total = vec_ref[pl.ds(0, VEC_ROWS), :]
        for dev in range(1, N_DEV):
            total = total + vec_ref[pl.ds(dev * VEC_ROWS, VEC_ROWS), :]
        rows[...] = total
        outs[4 * nv + 4][...] = rows[pl.ds(LOSS_ROW, 1), pl.ds(0, 1)]
        for i in range(nv):
            w_ref, m_ref, v_ref = ins[3 * i:3 * i + 3]
            g = rows[pl.ds(i, 1), pl.ds(0, w_ref.shape[1])]
            outs[4 * i][...] = g
            outs[4 * i + 1][...], outs[4 * i + 2][...], outs[4 * i + 3][...] = _adamw_update(
                w_ref[...], g, m_ref[...], v_ref[...])
        g = pool_ref[pl.ds(0, pool_rows), :].astype(F32)
        for dev in range(1, N_DEV):
            g = g + pool_ref[pl.ds(dev * pool_rows, pool_rows), :].astype(F32)
        w_ref, m_ref, v_ref = ins[3 * nv:]
        outs[4 * nv][...] = g
        outs[4 * nv + 1][...], outs[4 * nv + 2][...], outs[4 * nv + 3][...] = _adamw_update(
            w_ref[...], g, m_ref[...], v_ref[...])

    vmem = pl.BlockSpec(memory_space=pltpu.VMEM)
    flat = [a for trio in vec_params for a in trio] + list(pool_params)
    out_shape = []
    for trio in list(vec_params) + [pool_params]:
        out_shape += [jax.ShapeDtypeStruct(trio[0].shape, F32)] * 4
    out_shape.append(jax.ShapeDtypeStruct((1, 1), F32))
    return pl.pallas_call(
        body, name="adamw_small", in_specs=[vmem] * (2 + len(flat)), out_specs=[vmem] * len(out_shape),
        out_shape=out_shape, scratch_shapes=[pltpu.VMEM((VEC_ROWS, D_MODEL), F32)],
    )(vec_all, pool_all, *flat)


def kernel(x, ffn1_norm, ffn1_w_gate, ffn1_w_up, ffn1_w_down, mix_norm, w_in, b_forget, pool_w, pool_scale, q_norm, k_norm, out_norm_pool, out_norm_attn, w_out, ffn2_norm, ffn2_w_gate, ffn2_w_up, ffn2_w_down, loss_target, m_ffn1_norm, m_ffn1_w_gate, m_ffn1_w_up, m_ffn1_w_down, m_mix_norm, m_w_in, m_b_forget, m_pool_w, m_pool_scale, m_q_norm, m_k_norm, m_out_norm_pool, m_out_norm_attn, m_w_out, m_ffn2_norm, m_ffn2_w_gate, m_ffn2_w_up, m_ffn2_w_down, v_ffn1_norm, v_ffn1_w_gate, v_ffn1_w_up, v_ffn1_w_down, v_mix_norm, v_w_in, v_b_forget, v_pool_w, v_pool_scale, v_q_norm, v_k_norm, v_out_norm_pool, v_out_norm_attn, v_w_out, v_ffn2_norm, v_ffn2_w_gate, v_ffn2_w_up, v_ffn2_w_down):
    bsz, seq, d = x.shape
    t = bsz * seq
    x0 = x.reshape(t, d)
    target = loss_target.reshape(t, d)
    in_rows = -(-w_in.shape[1] // BF16_ROWS) * BF16_ROWS

    slabs = [s.astype(BF16) for s in (ffn1_w_gate.T, ffn1_w_up.T, ffn1_w_down, _pad_rows(w_in.T, in_rows), w_out,
                                       ffn2_w_gate.T, ffn2_w_up.T, ffn2_w_down)]
    gathers, started = _copies_start([slabs[0:2], slabs[2:3], slabs[3:4], slabs[4:5], slabs[5:8]], True, "gather_start",
                                     relayed=(0, 4))

    g1, gm, g2 = ffn1_norm.reshape(1, d), mix_norm.reshape(1, d), ffn2_norm.reshape(1, d)
    bf_row = _row1(b_forget, LANES)
    gq = jnp.tile(q_norm, N_HEADS).reshape(1, ATTN_WIDTH)
    gk = jnp.tile(k_norm, N_HEADS).reshape(1, ATTN_WIDTH)
    scale_row = pool_scale.reshape(1, POOL_WIDTH)
    gp, ga = out_norm_pool.reshape(1, POOL_WIDTH), out_norm_attn.reshape(1, ATTN_WIDTH)

    wg1, wu1 = _relayed_wait(_relay_to_sibling(gathers[0], "gather_relay_ffn1_up"), started, "gather_wait_ffn1_up")
    h1, sa1, sb1, s1 = _ffn_up(x0, g1, wg1, wu1, "ffn1_up")
    (wd1,) = _copies_wait(gathers[1], True, s1, "gather_wait_ffn1_down")
    (x1,) = _ffn_down(s1, wd1, x0, None, "ffn1_down")
    (win_g,) = _copies_wait(gathers[2], True, x1, "gather_wait_w_in")
    win_t = _repack_rows(win_g, in_rows, w_in.shape[1], N_DEV, "w_in_rows")
    hm, pv, q, k, v, f = _mix_in_fwd(x1, gm, win_t)
    pooled, mixed, y_pool = _pool_fwd(pv, pool_w, scale_row, gp, bsz, seq)
    qp, kp = _attn_prep_fwd(q, k, f, bf_row, gq, gk, bsz, seq)
    o, lse = _flash_fwd(qp, kp, v, bsz, seq)
    relayed_ffn2 = _relay_to_sibling(gathers[4], "gather_relay_ffn2", o)
    (wout,) = _copies_wait(gathers[3], True, [o, relayed_ffn2[4][0]], "gather_wait_w_out")
    ycat, x2 = _mix_out_fwd(o, y_pool, x1, ga, wout)
    wg2, wu2, wd2 = _relayed_wait(relayed_ffn2, x2, "gather_wait_ffn2")
    h2, sa2, sb2, s2 = _ffn_up(x2, g2, wg2, wu2, "ffn2_up")
    dx3, dyh2, loss_part = _ffn_down(s2, wd2, x2, target, "ffn2_down")

    da2, db2, dwg2, dwu2 = _ffn_bwd_act(dyh2, sa2, sb2, h2, wd2, "ffn2_bwd_act")
    (dwd2,) = _wgrad([s2], dyh2, "ffn2_down_wgrad")
    (sent_ffn2,), tok = _copies_start([[dwg2, dwu2, dwd2]], False, "exchange_start_ffn2")
    dx2, dg2 = _ffn_bwd_dx(da2, db2, dx3, x2, g2 + tok[0, 0], wg2, wu2, "ffn2_bwd_dx")
    dwout, dy_pool, do, dga = _mix_out_bwd(dx2, o, ycat, ga, wout)
    (sent_out,), tok = _copies_start([[dwout]], False, "exchange_start_w_out")
    dqp, dkp, dv = _flash_bwd(qp, kp, v, o, do, lse, bsz, seq)
    dq, dk, df, dgq, dgk, dbf = _attn_prep_bwd(dqp, dkp, q, k, f, bf_row + tok[0, 0], gq, gk, bsz, seq)
    dpv, dpool_w, dscale, dgp = _pool_bwd(dy_pool, mixed, pooled, pool_w, scale_row, gp, bsz, seq)
    dwin, dx1, dyh1, dgm = _mix_in_bwd(dpv, dq, dk, dv, df, hm, x1, dx2, gm, win_t)
    dwin_blocks = _repack_rows(dwin, w_in.shape[1], in_rows, N_DEV, "w_in_grad_blocks")
    (sent_in,), tok = _copies_start([[dwin_blocks]], False, "exchange_start_w_in")
    (dwd1,) = _wgrad([s1], dyh1, "ffn1_down_wgrad")
    (sent_down1,), tok = _copies_start([[dwd1]], False, "exchange_start_ffn1_down", after=tok)
    da1, db1, dwg1, dwu1 = _ffn_bwd_act(dyh1, sa1, sb1, h1, wd1, "ffn1_bwd_act")
    (sent_up1,), tok = _copies_start([[dwg1, dwu1]], False, "exchange_start_ffn1_up", after=tok)
    dx0, dg1 = _ffn_bwd_dx(da1, db1, dx1, x0, g1 + tok[0, 0], wg1, wu1, "ffn1_bwd_dx")

    pool_rows = POOL_GROUPS * POOL_GROUP_DIM
    packed = _pack_vector_grads(dict(ffn1_norm=dg1, mix_norm=dgm, ffn2_norm=dg2, b_forget=dbf, pool_scale=dscale,
                                     q_norm=dgq, k_norm=dgk, out_norm_pool=dgp, out_norm_attn=dga), loss_part,
                                "pack_vector_grads")
    pool_part = dpool_w.reshape(pool_rows, POOL_GROUP_DIM).astype(BF16)
    (sent_small,), tok = _copies_start([[packed, pool_part]], True, "small_grads_start")

    weights = dict(ffn1_norm=ffn1_norm, ffn1_w_gate=ffn1_w_gate, ffn1_w_up=ffn1_w_up, ffn1_w_down=ffn1_w_down,
                   mix_norm=mix_norm, w_in=w_in, b_forget=b_forget, pool_w=pool_w, pool_scale=pool_scale,
                   q_norm=q_norm, k_norm=k_norm, out_norm_pool=out_norm_pool, out_norm_attn=out_norm_attn,
                   w_out=w_out, ffn2_norm=ffn2_norm, ffn2_w_gate=ffn2_w_gate, ffn2_w_up=ffn2_w_up,
                   ffn2_w_down=ffn2_w_down)
    m_in = dict(ffn1_norm=m_ffn1_norm, ffn1_w_gate=m_ffn1_w_gate, ffn1_w_up=m_ffn1_w_up, ffn1_w_down=m_ffn1_w_down,
                mix_norm=m_mix_norm, w_in=m_w_in, b_forget=m_b_forget, pool_w=m_pool_w, pool_scale=m_pool_scale,
                q_norm=m_q_norm, k_norm=m_k_norm, out_norm_pool=m_out_norm_pool, out_norm_attn=m_out_norm_attn,
                w_out=m_w_out, ffn2_norm=m_ffn2_norm, ffn2_w_gate=m_ffn2_w_gate, ffn2_w_up=m_ffn2_w_up,
                ffn2_w_down=m_ffn2_w_down)
    v_in = dict(ffn1_norm=v_ffn1_norm, ffn1_w_gate=v_ffn1_w_gate, ffn1_w_up=v_ffn1_w_up, ffn1_w_down=v_ffn1_w_down,
                mix_norm=v_mix_norm, w_in=v_w_in, b_forget=v_b_forget, pool_w=v_pool_w, pool_scale=v_pool_scale,
                q_norm=v_q_norm, k_norm=v_k_norm, out_norm_pool=v_out_norm_pool, out_norm_attn=v_out_norm_attn,
                w_out=v_w_out, ffn2_norm=v_ffn2_norm, ffn2_w_gate=v_ffn2_w_gate, ffn2_w_up=v_ffn2_w_up,
                ffn2_w_down=v_ffn2_w_down)
    grads, delta, new_m, new_v = {}, {}, {}, {}
    after = [tok]
    plan = ((sent_ffn2, "ffn2", ("ffn2_w_gate", "ffn2_w_up", "ffn2_w_down")), (sent_out, "w_out", ("w_out",)),
            (sent_in, "w_in", ("w_in",)), (sent_down1, "ffn1_down", ("ffn1_w_down",)),
            (sent_up1, "ffn1_up", ("ffn1_w_gate", "ffn1_w_up")))
    for sent, tag, names in plan:
        parts = _copies_wait(sent, False, after, f"exchange_wait_{tag}")
        after = []
        for n, part in zip(names, parts):
            turn = (lambda a: a.T) if n in COLUMN_SHARDED else (lambda a: a)
            done = _sum_adamw(part, turn(weights[n]), turn(m_in[n]), turn(v_in[n]), f"adamw_{n}")
            grads[n], delta[n], new_m[n], new_v[n] = (turn(a) for a in done)
            after.append(done[3])
    vec_all, pool_all = _copies_wait(sent_small, True, after, "small_grads_wait")
    as_row = lambda a: a.reshape(1, -1)
    as_pool = lambda a: a.reshape(pool_rows, POOL_GROUP_DIM)
    small = _small_adamw(vec_all, pool_all,
                         [tuple(as_row(z[n]) for z in (weights, m_in, v_in)) for n in VEC_NAMES],
                         tuple(as_pool(z["pool_w"]) for z in (weights, m_in, v_in)))
    for i, n in enumerate(VEC_NAMES + ("pool_w",)):
        grads[n], delta[n], new_m[n], new_v[n] = (a.reshape(weights[n].shape) for a in small[4 * i:4 * i + 4])
    loss = small[-1].reshape(())

    order = ("ffn1_norm", "ffn1_w_gate", "ffn1_w_up", "ffn1_w_down", "mix_norm", "w_in", "b_forget", "pool_w",
             "pool_scale", "q_norm", "k_norm", "out_norm_pool", "out_norm_attn", "w_out", "ffn2_norm", "ffn2_w_gate",
             "ffn2_w_up", "ffn2_w_down")
    return (loss, dx0.reshape(bsz, seq, d), *[grads[n] for n in order], *[delta[n] for n in order],
            *[new_m[n] for n in order], *[new_v[n] for n in order])
```

```python
import jax
import jax.numpy as jnp
from jax import lax
from jax.experimental import pallas as pl
from jax.experimental.pallas import tpu as pltpu

F32 = jnp.float32
BF16 = jnp.bfloat16

EPS = 1e-6
D_MODEL = 1024
N_HEADS = 8
HEAD_DIM = 64
POOL_WIDTH = 512
ATTN_WIDTH = 512
POOL_GROUPS = 4
POOL_GROUP_DIM = 128
POOL_WINDOWS = (2, 4, 8, 16)
POOL_HALO = 16
MIX_PAD = POOL_WIDTH + 3 * ATTN_WIDTH + 128
N_DEV = 8
BF16_ROWS = 16
LANES = 128
VMEM_LIMIT = 56 * 1024 * 1024

ADAM_LR = 0.001
ADAM_B1 = 0.9
ADAM_B2 = 0.999
ADAM_EPS = 1e-08
ADAM_WD = 0.01
ADAM_STEP = 10


def _params(*sem):
    return pltpu.CompilerParams(dimension_semantics=sem, vmem_limit_bytes=VMEM_LIMIT)


def _dot(a, b):
    return jnp.dot(a, b, preferred_element_type=F32)


def _dot_nt(a, b):
    return lax.dot_general(a, b, (((1,), (1,)), ((), ())), preferred_element_type=F32)


def _dot_tn(a, b):
    return lax.dot_general(a, b, (((0,), (0,)), ((), ())), preferred_element_type=F32)


def _resident(shape):
    return pl.BlockSpec(shape, lambda *_: (0,) * len(shape), pipeline_mode=pl.Buffered(1))


def _rows(tm, width):
    return pl.BlockSpec((tm, width), lambda i: (i, 0))


def _rms_scale(x):
    return lax.rsqrt(jnp.mean(x * x, axis=-1, keepdims=True) + EPS)


def _rms_bwd(dh, x, gain):
    r = _rms_scale(x)
    n = x * r
    dgain = jnp.sum(dh * n, axis=0, keepdims=True)
    dn = dh * gain
    dx = r * (dn - n * jnp.mean(dn * n, axis=-1, keepdims=True))
    return dx, dgain


def _split3(x):
    hi = x.astype(BF16)
    r1 = x - hi.astype(F32)
    mid = r1.astype(BF16)
    lo = (r1 - mid.astype(F32)).astype(BF16)
    return hi, mid, lo


FF_CHUNK = 256


def _swiglu_parts(a, b):
    sig = jax.nn.sigmoid(a)
    silu = a * sig
    return (b * (sig + silu * (1.0 - sig))).astype(BF16), silu.astype(BF16), (silu * b).astype(BF16)


def _ffn_up(x, gain, wg_t, wu_t, name):
    t, d = x.shape
    f = wg_t.shape[0]
    tm = 512

    def body(x_ref, g_ref, wg_ref, wu_ref, h_ref, sa_ref, sb_ref, s_ref):
        xv = x_ref[...]
        h = (xv * _rms_scale(xv) * g_ref[...]).astype(BF16)
        h_ref[...] = h
        for c in range(f // FF_CHUNK):
            sl = pl.ds(c * FF_CHUNK, FF_CHUNK)
            sa_ref[:, sl], sb_ref[:, sl], s_ref[:, sl] = _swiglu_parts(_dot_nt(h, wg_ref[sl, :]), _dot_nt(h, wu_ref[sl, :]))

    wide = jax.ShapeDtypeStruct((t, f), BF16)
    return pl.pallas_call(
        body, name=name, grid=(t // tm,),
        in_specs=[_rows(tm, d), _resident((1, d)), _resident((f, d)), _resident((f, d))],
        out_specs=[_rows(tm, d), _rows(tm, f), _rows(tm, f), _rows(tm, f)],
        out_shape=[jax.ShapeDtypeStruct((t, d), BF16), wide, wide, wide],
        compiler_params=_params("arbitrary"),
    )(x, gain, wg_t, wu_t)


def _ffn_down(s, wd, x, target, name):
    t, d = x.shape
    f = wd.shape[0]
    tm = 512
    with_loss = target is not None

    def body(*refs):
        if with_loss:
            s_ref, w_ref, x_ref, t_ref, dy_ref, dyh_ref, loss_ref = refs
        else:
            s_ref, w_ref, x_ref, y_ref = refs
        y = x_ref[...] + 0.5 * _dot(s_ref[...], w_ref[...])
        if with_loss:
            e = y - t_ref[...]
            dy = e * (1.0 / d)
            dy_ref[...] = dy
            dyh_ref[...] = (0.5 * dy).astype(BF16)

            @pl.when(pl.program_id(0) == 0)
            def _():
                loss_ref[...] = jnp.zeros_like(loss_ref)

            part = jnp.sum(jnp.sum(e * e, axis=0, keepdims=True), axis=1, keepdims=True)
            loss_ref[...] += part * (0.5 / d)
        else:
            y_ref[...] = y

    in_specs = [_rows(tm, f), _resident((f, d)), _rows(tm, d)]
    args = [s, wd, x]
    if with_loss:
        in_specs.append(_rows(tm, d))
        args.append(target)
        out_shape = [jax.ShapeDtypeStruct((t, d), F32), jax.ShapeDtypeStruct((t, d), BF16),
                     jax.ShapeDtypeStruct((1, 1), F32)]
        out_specs = [_rows(tm, d), _rows(tm, d), pl.BlockSpec((1, 1), lambda i: (0, 0))]
    else:
        out_shape = [jax.ShapeDtypeStruct((t, d), F32)]
        out_specs = [_rows(tm, d)]
    return pl.pallas_call(
        body, name=name, grid=(t // tm,), in_specs=in_specs, out_specs=out_specs, out_shape=out_shape,
        compiler_params=_params("arbitrary"),
    )(*args)


def _ffn_bwd_act(dyh, sa, sb, h, wd, name):
    t, d = dyh.shape
    f = wd.shape[0]
    tn = f // 2
    tk = 512
    nk = t // tk

    def body(dy_ref, sa_ref, sb_ref, h_ref, wd_ref, da_ref, db_ref, dwg_ref, dwu_ref, acc_g, acc_u):
        k = pl.program_id(1)

        @pl.when(k == 0)
        def _():
            acc_g[...] = jnp.zeros_like(acc_g)
            acc_u[...] = jnp.zeros_like(acc_u)

        ds = _dot_nt(dy_ref[...], wd_ref[...])
        da = (ds * sa_ref[...].astype(F32)).astype(BF16)
        db = (ds * sb_ref[...].astype(F32)).astype(BF16)
        da_ref[...] = da
        db_ref[...] = db
        hv = h_ref[...]
        acc_g[...] += _dot_tn(da, hv)
        acc_u[...] += _dot_tn(db, hv)

        @pl.when(k == nk - 1)
        def _():
            dwg_ref[...] = acc_g[...].astype(BF16)
            dwu_ref[...] = acc_u[...].astype(BF16)

    tokens = pl.BlockSpec((tk, d), lambda j, k: (k, 0))
    wide = pl.BlockSpec((tk, tn), lambda j, k: (k, j))
    weight = pl.BlockSpec((tn, d), lambda j, k: (j, 0))
    return pl.pallas_call(
        body, name=name, grid=(f // tn, nk),
        in_specs=[tokens, wide, wide, tokens, weight],
        out_specs=[wide, wide, weight, weight],
        out_shape=[jax.ShapeDtypeStruct((t, f), BF16)] * 2 + [jax.ShapeDtypeStruct((f, d), BF16)] * 2,
        scratch_shapes=[pltpu.VMEM((tn, d), F32)] * 2,
        compiler_params=_params("arbitrary", "arbitrary"),
    )(dyh, sa, sb, h, wd)


def _ffn_bwd_dx(da, db, dy, x, gain, wg_t, wu_t, name):
    t, d = x.shape
    f = wg_t.shape[0]
    tm = 512

    def body(da_ref, db_ref, dy_ref, x_ref, g_ref, wg_ref, wu_ref, dx_ref, dg_ref):
        dh = _dot(da_ref[...], wg_ref[...]) + _dot(db_ref[...], wu_ref[...])
        dx, dgain = _rms_bwd(dh, x_ref[...], g_ref[...])
        dx_ref[...] = dy_ref[...] + dx

        @pl.when(pl.program_id(0) == 0)
        def _():
            dg_ref[...] = jnp.zeros_like(dg_ref)

        dg_ref[...] += dgain

    return pl.pallas_call(
        body, name=name, grid=(t // tm,),
        in_specs=[_rows(tm, f), _rows(tm, f), _rows(tm, d), _rows(tm, d), _resident((1, d)), _resident((f, d)),
                  _resident((f, d))],
        out_specs=[_rows(tm, d), pl.BlockSpec((1, d), lambda i: (0, 0))],
        out_shape=[jax.ShapeDtypeStruct((t, d), F32), jax.ShapeDtypeStruct((1, d), F32)],
        compiler_params=_params("arbitrary"),
    )(da, db, dy, x, gain, wg_t, wu_t)


def _wgrad(lhs, b, name):
    t, n = lhs[0].shape
    d = b.shape[1]
    m = len(lhs)
    tn = n // 2 if n * d * m > (4 << 20) else n
    tk = 1024
    nk = t // tk

    def body(*refs):
        a_refs, b_ref, o_refs, accs = refs[:m], refs[m], refs[m + 1:2 * m + 1], refs[2 * m + 1:]
        k = pl.program_id(1)

        @pl.when(k == 0)
        def _():
            for acc in accs:
                acc[...] = jnp.zeros_like(acc)

        bv = b_ref[...]
        for a_ref, acc in zip(a_refs, accs):
            acc[...] += _dot_tn(a_ref[...], bv)

        @pl.when(k == nk - 1)
        def _():
            for o_ref, acc in zip(o_refs, accs):
                o_ref[...] = acc[...].astype(BF16)

    return pl.pallas_call(
        body, name=name, grid=(n // tn, nk),
        in_specs=[pl.BlockSpec((tk, tn), lambda j, k: (k, j))] * m + [pl.BlockSpec((tk, d), lambda j, k: (k, 0))],
        out_specs=[pl.BlockSpec((tn, d), lambda j, k: (j, 0))] * m,
        out_shape=[jax.ShapeDtypeStruct((n, d), BF16)] * m,
        scratch_shapes=[pltpu.VMEM((tn, d), F32)] * m,
        compiler_params=_params("arbitrary", "arbitrary"),
    )(*lhs, b)


def _repack_rows(a, rows_in, rows_out, blocks, name):
    total, d = a.shape
    real = min(rows_in, rows_out)

    def body(a_ref, o_ref, wide_in, wide_out):
        wide_in[...] = a_ref[...].astype(F32)
        wide_out[...] = jnp.zeros_like(wide_out)
        for j in range(blocks):
            wide_out[pl.ds(j * rows_out, real), :] = wide_in[pl.ds(j * rows_in, real), :]
        o_ref[...] = wide_out[...].astype(BF16)

    full = pl.BlockSpec((total, d), lambda i: (0, 0))
    return pl.pallas_call(
        body, name=name, grid=(1,), in_specs=[full], out_specs=full, out_shape=jax.ShapeDtypeStruct((total, d), BF16),
        scratch_shapes=[pltpu.VMEM((total, d), F32)] * 2,
        compiler_params=_params("arbitrary"),
    )(a)


def _mix_in_fwd(x, gain, w_in_t):
    t, d = x.shape
    tm = 1024
    pw, aw = POOL_WIDTH, ATTN_WIDTH

    def body(x_ref, g_ref, w_ref, hm_ref, pv_ref, q_ref, k_ref, v_ref, f_ref):
        xv = x_ref[...]
        hm = (xv * _rms_scale(xv) * g_ref[...]).astype(BF16)
        hm_ref[...] = hm
        pv_ref[...] = _dot_nt(hm, w_ref[pl.ds(0, pw), :])
        q_ref[...] = _dot_nt(hm, w_ref[pl.ds(pw, aw), :])
        k_ref[...] = _dot_nt(hm, w_ref[pl.ds(pw + aw, aw), :])
        v_ref[...] = _dot_nt(hm, w_ref[pl.ds(pw + 2 * aw, aw), :]).astype(BF16)
        f_ref[...] = _dot_nt(hm, w_ref[pl.ds(pw + 3 * aw, LANES), :])

    return pl.pallas_call(
        body, name="mix_in_fwd", grid=(t // tm,),
        in_specs=[_rows(tm, d), _resident((1, d)), _resident((MIX_PAD, d))],
        out_specs=[_rows(tm, d), _rows(tm, pw), _rows(tm, aw), _rows(tm, aw), _rows(tm, aw), _rows(tm, LANES)],
        out_shape=[jax.ShapeDtypeStruct((t, d), BF16), jax.ShapeDtypeStruct((t, pw), F32),
                   jax.ShapeDtypeStruct((t, aw), F32), jax.ShapeDtypeStruct((t, aw), F32),
                   jax.ShapeDtypeStruct((t, aw), BF16), jax.ShapeDtypeStruct((t, LANES), F32)],
        compiler_params=_params("arbitrary"),
    )(x, gain, w_in_t)


def _pool_fwd(pv, pool_w, pool_scale, gain, bsz, seq):
    ts = 512
    ns = seq // ts
    pw = POOL_WIDTH

    def body(pv_ref, w_ref, sc_ref, g_ref, pooled_ref, mixed_ref, y_ref, ext):
        s = pl.program_id(1)

        @pl.when(s == 0)
        def _():
            ext[pl.ds(0, POOL_HALO), :] = jnp.zeros((POOL_HALO, pw), F32)

        p = pv_ref[...]
        ext[pl.ds(POOL_HALO, ts), :] = p
        pos = s * ts + lax.broadcasted_iota(jnp.int32, (ts, 1), 0)
        parts = []
        for g, w in enumerate(POOL_WINDOWS):
            lanes = pl.ds(g * POOL_GROUP_DIM, POOL_GROUP_DIM)
            win = ext[pl.ds(POOL_HALO, ts), lanes]
            for i in range(1, w):
                win = win + ext[pl.ds(POOL_HALO - i, ts), lanes]
            cnt = jnp.minimum(pos + 1, w).astype(F32)
            pooled = (win / cnt - ext[pl.ds(POOL_HALO, ts), lanes]).astype(BF16)
            pooled_ref[:, lanes] = pooled
            parts.append(_dot(pooled, w_ref[g].astype(BF16)))
        mixed = jnp.concatenate(parts, axis=1)
        mixed_ref[...] = mixed
        pm = mixed * sc_ref[...]
        y_ref[...] = (pm * _rms_scale(pm) * g_ref[...]).astype(BF16)
        ext[pl.ds(0, POOL_HALO), :] = p[ts - POOL_HALO:, :]

    blk = pl.BlockSpec((ts, pw), lambda b, s: (b * ns + s, 0))
    t = bsz * seq
    return pl.pallas_call(
        body, name="pool_fwd", grid=(bsz, ns),
        in_specs=[blk, pl.BlockSpec((POOL_GROUPS, POOL_GROUP_DIM, POOL_GROUP_DIM), lambda b, s: (0, 0, 0)),
                  pl.BlockSpec((1, pw), lambda b, s: (0, 0)), pl.BlockSpec((1, pw), lambda b, s: (0, 0))],
        out_specs=[blk, blk, blk],
        out_shape=[jax.ShapeDtypeStruct((t, pw), BF16), jax.ShapeDtypeStruct((t, pw), F32),
                   jax.ShapeDtypeStruct((t, pw), BF16)],
        scratch_shapes=[pltpu.VMEM((POOL_HALO + ts, pw), F32)],
        compiler_params=_params("arbitrary", "arbitrary"),
    )(pv, pool_w, pool_scale, gain)


def _pool_bwd(dy, mixed, pooled, pool_w, pool_scale, gain, bsz, seq):
    ts = 512
    ns = seq // ts
    pw = POOL_WIDTH

    def body(dy_ref, mixed_ref, pooled_ref, w_ref, sc_ref, g_ref, dpv_ref, dw_ref, dsc_ref, dg_ref, ext):
        b = pl.program_id(0)
        sr = pl.program_id(1)
        s = ns - 1 - sr

        @pl.when(jnp.logical_and(b == 0, sr == 0))
        def _():
            dw_ref[...] = jnp.zeros_like(dw_ref)
            dsc_ref[...] = jnp.zeros_like(dsc_ref)
            dg_ref[...] = jnp.zeros_like(dg_ref)

        @pl.when(sr == 0)
        def _():
            ext[pl.ds(ts, POOL_HALO), :] = jnp.zeros((POOL_HALO, pw), F32)

        mixed = mixed_ref[...]
        sc = sc_ref[...]
        dpm, dgain = _rms_bwd(dy_ref[...], mixed * sc, g_ref[...])
        dg_ref[...] += dgain
        dsc_ref[...] += jnp.sum(dpm * mixed, axis=0, keepdims=True)
        dmixed = (dpm * sc).astype(BF16)
        pos = s * ts + lax.broadcasted_iota(jnp.int32, (ts, 1), 0)
        dpooled = []
        for g, w in enumerate(POOL_WINDOWS):
            lanes = pl.ds(g * POOL_GROUP_DIM, POOL_GROUP_DIM)
            dm = dmixed[:, g * POOL_GROUP_DIM:(g + 1) * POOL_GROUP_DIM]
            dw_ref[g] += _dot_tn(pooled_ref[:, lanes], dm)
            dp = _dot_nt(dm, w_ref[g].astype(BF16))
            dpooled.append(dp)
            cnt = jnp.minimum(pos + 1, w).astype(F32)
            ext[pl.ds(0, ts), lanes] = dp / cnt
        for g, w in enumerate(POOL_WINDOWS):
            lanes = pl.ds(g * POOL_GROUP_DIM, POOL_GROUP_DIM)
            win = ext[pl.ds(0, ts), lanes]
            for i in range(1, w):
                win = win + ext[pl.ds(i, ts), lanes]
            dpv_ref[:, lanes] = (win - dpooled[g]).astype(BF16)
        head = ext[pl.ds(0, POOL_HALO), :]
        ext[pl.ds(ts, POOL_HALO), :] = head

    blk = pl.BlockSpec((ts, pw), lambda b, s: (b * ns + (ns - 1 - s), 0))
    vec = pl.BlockSpec((1, pw), lambda b, s: (0, 0))
    wspec = pl.BlockSpec((POOL_GROUPS, POOL_GROUP_DIM, POOL_GROUP_DIM), lambda b, s: (0, 0, 0))
    t = bsz * seq
    return pl.pallas_call(
        body, name="pool_bwd", grid=(bsz, ns),
        in_specs=[blk, blk, blk, wspec, vec, vec],
        out_specs=[blk, wspec, vec, vec],
        out_shape=[jax.ShapeDtypeStruct((t, pw), BF16),
                   jax.ShapeDtypeStruct((POOL_GROUPS, POOL_GROUP_DIM, POOL_GROUP_DIM), F32),
                   jax.ShapeDtypeStruct((1, pw), F32), jax.ShapeDtypeStruct((1, pw), F32)],
        scratch_shapes=[pltpu.VMEM((ts + POOL_HALO, pw), F32)],
        compiler_params=_params("arbitrary", "arbitrary"),
    )(dy, mixed, pooled, pool_w, pool_scale, gain)


AUX_ONE = 64
AUX_F = 67

ATTN_PREP_ROWS = 512


def _seg_ones(width, seg):
    r = lax.broadcasted_iota(jnp.int32, (width, width), 0) // seg
    c = lax.broadcasted_iota(jnp.int32, (width, width), 1) // seg
    return (r == c).astype(BF16)


def _tri_ones(n, lower):
    r = lax.broadcasted_iota(jnp.int32, (n, n), 0)
    c = lax.broadcasted_iota(jnp.int32, (n, n), 1)
    return ((r >= c) if lower else (r <= c)).astype(BF16)


def _place_pieces(first_lane):
    r = lax.broadcasted_iota(jnp.int32, (3 * LANES, N_HEADS * LANES), 0)
    c = lax.broadcasted_iota(jnp.int32, (3 * LANES, N_HEADS * LANES), 1)
    piece, head = r // LANES, r % LANES
    return jnp.logical_and(head < N_HEADS, c == head * LANES + first_lane + piece).astype(BF16)


def _head_sums(x, seg_ones):
    return _dot(x.astype(BF16), seg_ones)


def _log_sigmoid(x):
    return jnp.minimum(x, 0.0) - jnp.log(1.0 + jnp.exp(-jnp.abs(x)))


def _attn_prep_fwd(q, k, f, b_forget, q_gain, k_gain, bsz, seq):
    ts = ATTN_PREP_ROWS
    ns = seq // ts
    aw = ATTN_WIDTH
    t = bsz * seq
    seg = _seg_ones(aw, HEAD_DIM)
    tri = _tri_ones(ts, True)

    def body(q_ref, k_ref, f_ref, bf_ref, gq_ref, gk_ref, seg_ref, tri_ref, pq_ref, pk_ref, qp_ref, kp_ref, carry):
        s = pl.program_id(1)

        @pl.when(s == 0)
        def _():
            carry[...] = jnp.zeros_like(carry)

        logf = _log_sigmoid(f_ref[...] + bf_ref[...])
        hi, mid, lo = _split3(logf)
        tri_v = tri_ref[...]
        fc = _dot(tri_v, hi) + _dot(tri_v, mid) + _dot(tri_v, lo) + carry[pl.ds(0, 1), :]
        carry[pl.ds(0, 1), :] = fc[ts - 1:, :]
        pcs = jnp.concatenate(_split3(fc), axis=1)
        lane = lax.broadcasted_iota(jnp.int32, (1, LANES), 1)
        ones_q = jnp.logical_and(lane >= AUX_ONE, lane < AUX_ONE + 3).astype(F32)
        ones_k = jnp.logical_and(lane >= AUX_F, lane < AUX_F + 3).astype(F32)
        seg_v = seg_ref[...]

        def build(x_ref, g_ref, scale, out_ref, ones, place_ref, f_sign):
            xv = x_ref[...]
            r = lax.rsqrt(_head_sums(xv * xv, seg_v) * (1.0 / HEAD_DIM) + EPS)
            xn = xv * r * g_ref[...] * scale
            aux = _dot(pcs, place_ref[...]) * f_sign
            for h in range(N_HEADS):
                pair = xn[:, (h // 2) * LANES:(h // 2 + 1) * LANES]
                feat = pair if h % 2 == 0 else pltpu.roll(pair, HEAD_DIM, 1)
                aux_h = aux[:, h * LANES:(h + 1) * LANES] + ones
                out_ref[:, h * LANES:(h + 1) * LANES] = jnp.where(lane < HEAD_DIM, feat, aux_h).astype(BF16)

        build(q_ref, gq_ref, 0.125, qp_ref, ones_q, pq_ref, 1.0)
        build(k_ref, gk_ref, 1.0, kp_ref, ones_k, pk_ref, -1.0)

    blk = pl.BlockSpec((ts, aw), lambda b, s: (b * ns + s, 0))
    fblk = pl.BlockSpec((ts, LANES), lambda b, s: (b * ns + s, 0))
    oblk = pl.BlockSpec((ts, N_HEADS * LANES), lambda b, s: (b * ns + s, 0))
    const = lambda shape: pl.BlockSpec(shape, lambda b, s: (0, 0))
    return pl.pallas_call(
        body, name="attn_prep_fwd", grid=(bsz, ns),
        in_specs=[blk, blk, fblk, const((1, LANES)), const((1, aw)), const((1, aw)), const((aw, aw)), const((ts, ts)),
                  const((3 * LANES, N_HEADS * LANES)), const((3 * LANES, N_HEADS * LANES))],
        out_specs=[oblk, oblk],
        out_shape=[jax.ShapeDtypeStruct((t, N_HEADS * LANES), BF16)] * 2,
        scratch_shapes=[pltpu.VMEM((8, LANES), F32)],
        compiler_params=_params("arbitrary", "arbitrary"),
    )(q, k, f, b_forget, q_gain, k_gain, seg, tri, _place_pieces(AUX_F), _place_pieces(AUX_ONE))


def _attn_prep_bwd(dqp, dkp, q, k, f, b_forget, q_gain, k_gain, bsz, seq):
    ts = ATTN_PREP_ROWS
    ns = seq // ts
    aw = ATTN_WIDTH
    t = bsz * seq
    seg = _seg_ones(aw, HEAD_DIM)
    tri = _tri_ones(ts, False)

    def body(dqp_ref, dkp_ref, q_ref, k_ref, f_ref, bf_ref, gq_ref, gk_ref, seg_ref, tri_ref,
             dq_ref, dk_ref, df_ref, dgq_ref, dgk_ref, dbf_ref, carry):
        b = pl.program_id(0)
        sr = pl.program_id(1)

        @pl.when(jnp.logical_and(b == 0, sr == 0))
        def _():
            dgq_ref[...] = jnp.zeros_like(dgq_ref)
            dgk_ref[...] = jnp.zeros_like(dgk_ref)
            dbf_ref[...] = jnp.zeros_like(dbf_ref)

        @pl.when(sr == 0)
        def _():
            carry[...] = jnp.zeros_like(carry)

        lane = lax.broadcasted_iota(jnp.int32, (1, LANES), 1)
        seg_v = seg_ref[...]

        def norm_bwd(dp_ref, x_ref, g_ref, scale, dx_ref, dgain_ref):
            parts = []
            for j in range(N_HEADS // 2):
                even = dp_ref[:, (2 * j) * LANES:(2 * j + 1) * LANES]
                odd = dp_ref[:, (2 * j + 1) * LANES:(2 * j + 2) * LANES]
                parts.append(jnp.where(lane < HEAD_DIM, even, pltpu.roll(odd, HEAD_DIM, 1)))
            dxn = jnp.concatenate(parts, axis=1) * scale
            xv = x_ref[...]
            r = lax.rsqrt(_head_sums(xv * xv, seg_v) * (1.0 / HEAD_DIM) + EPS)
            n = xv * r
            dgain_ref[...] += jnp.sum(dxn * n, axis=0, keepdims=True)
            dn = dxn * g_ref[...]
            m = _head_sums(dn * n, seg_v) * (1.0 / HEAD_DIM)
            dx_ref[...] = (r * (dn - n * m)).astype(BF16)

        norm_bwd(dqp_ref, q_ref, gq_ref, 0.125, dq_ref, dgq_ref)
        norm_bwd(dkp_ref, k_ref, gk_ref, 1.0, dk_ref, dgk_ref)

        dfc = jnp.zeros((ts, LANES), F32)
        for h in range(N_HEADS):
            cols = pl.ds(h * LANES, LANES)
            both = jnp.where(lane == AUX_F, dqp_ref[:, cols], 0.0) - jnp.where(lane == AUX_ONE, dkp_ref[:, cols], 0.0)
            dfc = jnp.where(lane == h, jnp.sum(both, axis=1, keepdims=True), dfc)
        hi, mid, lo = _split3(dfc)
        tri_v = tri_ref[...]
        dlogf = _dot(tri_v, hi) + _dot(tri_v, mid) + _dot(tri_v, lo) + carry[pl.ds(0, 1), :]
        carry[pl.ds(0, 1), :] = dlogf[0:1, :]
        df = jnp.where(lane < N_HEADS, dlogf * jax.nn.sigmoid(-(f_ref[...] + bf_ref[...])), 0.0)
        df_ref[...] = df.astype(BF16)
        dbf_ref[...] += jnp.sum(df, axis=0, keepdims=True)

    rev = lambda b, s: (b * ns + (ns - 1 - s), 0)
    blk = pl.BlockSpec((ts, aw), rev)
    fblk = pl.BlockSpec((ts, LANES), rev)
    pblk = pl.BlockSpec((ts, N_HEADS * LANES), rev)
    const = lambda shape: pl.BlockSpec(shape, lambda b, s: (0, 0))
    return pl.pallas_call(
        body, name="attn_prep_bwd", grid=(bsz, ns),
        in_specs=[pblk, pblk, blk, blk, fblk, const((1, LANES)), const((1, aw)), const((1, aw)), const((aw, aw)),
                  const((ts, ts))],
        out_specs=[blk, blk, fblk, const((1, aw)), const((1, aw)), const((1, LANES))],
        out_shape=[jax.ShapeDtypeStruct((t, aw), BF16), jax.ShapeDtypeStruct((t, aw), BF16),
                   jax.ShapeDtypeStruct((t, LANES), BF16), jax.ShapeDtypeStruct((1, aw), F32),
                   jax.ShapeDtypeStruct((1, aw), F32), jax.ShapeDtypeStruct((1, LANES), F32)],
        scratch_shapes=[pltpu.VMEM((8, LANES), F32)],
        compiler_params=_params("arbitrary", "arbitrary"),
    )(dqp, dkp, q, k, f, b_forget, q_gain, k_gain, seg, tri)


ATTN_BLOCK = 1024
HEAD_PAIRS = N_HEADS // 2


def _flash_fwd(qp, kp, v, bsz, seq):
    tq = ATTN_BLOCK
    half = tq // 2
    nq = seq // tq
    t = bsz * seq

    def body(q_ref, k_ref, v_ref, o_ref, lse_ref, m_sc, l_sc, acc_sc):
        i = pl.program_id(2)
        m_sc[...] = jnp.full(m_sc.shape, -jnp.inf, F32)
        l_sc[...] = jnp.zeros_like(l_sc)
        acc_sc[...] = jnp.zeros_like(acc_sc)
        lane = lax.broadcasted_iota(jnp.int32, (1, LANES), 1)
        low = lane < HEAD_DIM

        def tile(q0, qn, k_start, kn, k0=None):
            qs = pl.ds(q0, qn)
            ks = pl.ds(k_start, kn)
            vv = v_ref[ks, :]
            for h in range(2):
                mine = low if h == 0 else jnp.logical_not(low)
                cols = pl.ds(h * LANES, LANES)
                s = _dot_nt(q_ref[qs, cols], k_ref[ks, cols])
                if k0 is not None:
                    row = lax.broadcasted_iota(jnp.int32, (qn, kn), 0) + q0
                    col = lax.broadcasted_iota(jnp.int32, (qn, kn), 1) + k0
                    s = jnp.where(row >= col, s, -jnp.inf)
                m_prev = m_sc[h, qs, :]
                m_new = jnp.maximum(m_prev, jnp.max(s, axis=1, keepdims=True))
                p = jnp.exp(s - jnp.tile(m_new, (1, kn // LANES)))
                alpha = jnp.exp(m_prev - m_new)
                l_sc[h, qs, :] = alpha * l_sc[h, qs, :] + jnp.sum(p, axis=1, keepdims=True)
                m_sc[h, qs, :] = m_new
                pv = _dot(p.astype(BF16), jnp.where(mine, vv, jnp.zeros_like(vv)))
                acc_sc[qs, :] = acc_sc[qs, :] * jnp.where(mine, alpha, 1.0) + pv

        def below_diagonal(j, carry):
            tile(0, tq, pl.multiple_of(j * tq, tq), tq)
            return carry

        lax.fori_loop(0, i, below_diagonal, 0)
        diagonal = pl.multiple_of(i * tq, tq)
        tile(0, tq, diagonal, half, k0=0)
        tile(half, half, diagonal + half, half, k0=half)
        l = jnp.where(low, l_sc[0], l_sc[1])
        m = jnp.where(low, m_sc[0], m_sc[1])
        o_ref[...] = acc_sc[...] / l
        lse_ref[...] = m + jnp.log(l)

    qspec = pl.BlockSpec((tq, 2 * LANES), lambda b, hp, i: (b * nq + i, hp))
    kspec = pl.BlockSpec((seq, 2 * LANES), lambda b, hp, i: (b, hp))
    vspec = pl.BlockSpec((seq, LANES), lambda b, hp, i: (b, hp))
    ospec = pl.BlockSpec((tq, LANES), lambda b, hp, i: (b * nq + i, hp))
    return pl.pallas_call(
        body, name="flash_fwd", grid=(bsz, HEAD_PAIRS, nq),
        in_specs=[qspec, kspec, vspec], out_specs=[ospec, ospec],
        out_shape=[jax.ShapeDtypeStruct((t, ATTN_WIDTH), F32), jax.ShapeDtypeStruct((t, ATTN_WIDTH), F32)],
        scratch_shapes=[pltpu.VMEM((2, tq, LANES), F32), pltpu.VMEM((2, tq, LANES), F32), pltpu.VMEM((tq, LANES), F32)],
        compiler_params=_params("arbitrary", "arbitrary", "arbitrary"),
    )(qp, kp, v)


def _flash_bwd(qp, kp, v, o, do, lse, bsz, seq):
    tq = ATTN_BLOCK
    half = tq // 2
    nq = seq // tq
    t = bsz * seq

    def body(q_ref, k_ref, v_ref, o_ref, do_ref, lse_ref, dq_ref, dk_ref, dv_ref, dk_acc, dv_acc):
        j = pl.program_id(2)

        @pl.when(j == 0)
        def _():
            dq_ref[...] = jnp.zeros_like(dq_ref)

        dk_acc[...] = jnp.zeros_like(dk_acc)
        dv_acc[...] = jnp.zeros_like(dv_acc)
        lane = lax.broadcasted_iota(jnp.int32, (1, LANES), 1)
        low = lane < HEAD_DIM

        def tile(q_start, qn, k0, kn, q0=None):
            rows = pl.ds(q_start, qn)
            ks = pl.ds(k0, kn)
            dov = do_ref[rows, :]
            dd = dov * o_ref[rows, :]
            dob = dov.astype(BF16)
            vv = v_ref[ks, :]
            lse_v = lse_ref[rows, :]
            for h in range(2):
                mine = low if h == 0 else jnp.logical_not(low)
                cols = pl.ds(h * LANES, LANES)
                qh = q_ref[rows, cols]
                kh = k_ref[ks, cols]
                s = _dot_nt(qh, kh)
                lse_h = jnp.where(mine, lse_v, pltpu.roll(lse_v, HEAD_DIM, 1))
                p = jnp.exp(s - jnp.tile(lse_h, (1, kn // LANES)))
                if q0 is not None:
                    row = lax.broadcasted_iota(jnp.int32, (qn, kn), 0) + q0
                    col = lax.broadcasted_iota(jnp.int32, (qn, kn), 1) + k0
                    p = jnp.where(row >= col, p, 0.0)
                delta = jnp.sum(jnp.where(mine, dd, 0.0), axis=1, keepdims=True)
                dp = _dot_nt(dob, jnp.where(mine, vv, jnp.zeros_like(vv)))
                ds = (p * (dp - delta)).astype(BF16)
                dv_acc[ks, :] += jnp.where(mine, _dot_tn(p.astype(BF16), dob), 0.0)
                dk_acc[ks, cols] += _dot_tn(ds, qh)
                dq_ref[rows, cols] += _dot(ds, kh)

        def above_diagonal(i, carry):
            tile(pl.multiple_of(i * tq, tq), tq, 0, tq)
            return carry

        diagonal = pl.multiple_of(j * tq, tq)
        tile(diagonal, tq, 0, half, q0=0)
        tile(diagonal + half, half, half, half, q0=half)
        lax.fori_loop(j + 1, nq, above_diagonal, 0)
        dk_ref[...] = dk_acc[...]
        dv_ref[...] = dv_acc[...].astype(BF16)

    qspec = pl.BlockSpec((seq, 2 * LANES), lambda b, hp, j: (b, hp))
    kspec = pl.BlockSpec((tq, 2 * LANES), lambda b, hp, j: (b * nq + j, hp))
    vspec = pl.BlockSpec((tq, LANES), lambda b, hp, j: (b * nq + j, hp))
    ospec = pl.BlockSpec((seq, LANES), lambda b, hp, j: (b, hp))
    return pl.pallas_call(
        body, name="flash_bwd", grid=(bsz, HEAD_PAIRS, nq),
        in_specs=[qspec, kspec, vspec, ospec, ospec, ospec], out_specs=[qspec, kspec, vspec],
        out_shape=[jax.ShapeDtypeStruct((t, N_HEADS * LANES), F32), jax.ShapeDtypeStruct((t, N_HEADS * LANES), F32),
                   jax.ShapeDtypeStruct((t, ATTN_WIDTH), BF16)],
        scratch_shapes=[pltpu.VMEM((tq, 2 * LANES), F32), pltpu.VMEM((tq, LANES), F32)],
        compiler_params=_params("arbitrary", "arbitrary", "arbitrary"),
    )(qp, kp, v, o, do, lse)


def _mix_out_fwd(o, y_pool, x, gain, w_out):
    t, d = x.shape
    tm = 1024
    pw, aw = POOL_WIDTH, ATTN_WIDTH

    def body(o_ref, yp_ref, x_ref, g_ref, w_ref, ycat_ref, y_ref):
        ov = o_ref[...]
        ya = (ov * _rms_scale(ov) * g_ref[...]).astype(BF16)
        ycat = jnp.concatenate([yp_ref[...], ya], axis=1)
        ycat_ref[...] = ycat
        y_ref[...] = x_ref[...] + _dot(ycat, w_ref[...])

    return pl.pallas_call(
        body, name="mix_out_fwd", grid=(t // tm,),
        in_specs=[_rows(tm, aw), _rows(tm, pw), _rows(tm, d), _resident((1, aw)), _resident((pw + aw, d))],
        out_specs=[_rows(tm, pw + aw), _rows(tm, d)],
        out_shape=[jax.ShapeDtypeStruct((t, pw + aw), BF16), jax.ShapeDtypeStruct((t, d), F32)],
        compiler_params=_params("arbitrary"),
    )(o, y_pool, x, gain, w_out)


def _mix_out_bwd(dx, o, ycat, gain, w_out):
    t, d = dx.shape
    tm = 1024
    nm = t // tm
    pw, aw = POOL_WIDTH, ATTN_WIDTH

    def body(dx_ref, o_ref, ycat_ref, g_ref, w_ref, dw_ref, dyp_ref, do_ref, dg_ref, acc):
        i = pl.program_id(0)

        @pl.when(i == 0)
        def _():
            dg_ref[...] = jnp.zeros_like(dg_ref)
            acc[...] = jnp.zeros_like(acc)

        dxb = dx_ref[...].astype(BF16)
        acc[...] += _dot_tn(ycat_ref[...], dxb)
        dyp_ref[...] = _dot_nt(dxb, w_ref[pl.ds(0, pw), :])
        dya = _dot_nt(dxb, w_ref[pl.ds(pw, aw), :])
        do, dgain = _rms_bwd(dya, o_ref[...], g_ref[...])
        do_ref[...] = do
        dg_ref[...] += dgain

        @pl.when(i == nm - 1)
        def _():
            dw_ref[...] = acc[...].astype(BF16)

    return pl.pallas_call(
        body, name="mix_out_bwd", grid=(nm,),
        in_specs=[_rows(tm, d), _rows(tm, aw), _rows(tm, pw + aw), _resident((1, aw)), _resident((pw + aw, d))],
        out_specs=[pl.BlockSpec((pw + aw, d), lambda i: (0, 0)), _rows(tm, pw), _rows(tm, aw),
                   pl.BlockSpec((1, aw), lambda i: (0, 0))],
        out_shape=[jax.ShapeDtypeStruct((pw + aw, d), BF16), jax.ShapeDtypeStruct((t, pw), F32),
                   jax.ShapeDtypeStruct((t, aw), F32), jax.ShapeDtypeStruct((1, aw), F32)],
        scratch_shapes=[pltpu.VMEM((pw + aw, d), F32)],
        compiler_params=_params("arbitrary"),
    )(dx, o, ycat, gain, w_out)


def _mix_in_bwd(dpv, dq, dk, dv, df, hm, x, dx_res, gain, w_in_t):
    t, d = x.shape
    tm = 512
    nm = t // tm
    pw, aw = POOL_WIDTH, ATTN_WIDTH

    def body(dpv_ref, dq_ref, dk_ref, dv_ref, df_ref, hm_ref, x_ref, dxr_ref, g_ref, w_ref, dw_ref, dx_ref, dxh_ref,
             dg_ref, acc):
        i = pl.program_id(0)

        @pl.when(i == 0)
        def _():
            dg_ref[...] = jnp.zeros_like(dg_ref)
            acc[...] = jnp.zeros_like(acc)

        dh = jnp.concatenate([dpv_ref[...], dq_ref[...], dk_ref[...], dv_ref[...], df_ref[...]], axis=1)
        acc[...] += _dot_tn(dh, hm_ref[...])
        dx, dgain = _rms_bwd(_dot(dh, w_ref[...]), x_ref[...], g_ref[...])
        dx = dxr_ref[...] + dx
        dx_ref[...] = dx
        dxh_ref[...] = (0.5 * dx).astype(BF16)
        dg_ref[...] += dgain

        @pl.when(i == nm - 1)
        def _():
            dw_ref[...] = acc[...].astype(BF16)

    return pl.pallas_call(
        body, name="mix_in_bwd", grid=(nm,),
        in_specs=[_rows(tm, pw), _rows(tm, aw), _rows(tm, aw), _rows(tm, aw), _rows(tm, LANES), _rows(tm, d),
                  _rows(tm, d), _rows(tm, d), _resident((1, d)), _resident((MIX_PAD, d))],
        out_specs=[pl.BlockSpec((MIX_PAD, d), lambda i: (0, 0)), _rows(tm, d), _rows(tm, d),
                   pl.BlockSpec((1, d), lambda i: (0, 0))],
        out_shape=[jax.ShapeDtypeStruct((MIX_PAD, d), BF16), jax.ShapeDtypeStruct((t, d), F32),
                   jax.ShapeDtypeStruct((t, d), BF16), jax.ShapeDtypeStruct((1, d), F32)],
        scratch_shapes=[pltpu.VMEM((MIX_PAD, d), F32)],
        compiler_params=_params("arbitrary"),
    )(dpv, dq, dk, dv, df, hm, x, dx_res, gain, w_in_t)


MESH_IDS = pl.DeviceIdType.MESH


def _me():
    return lax.axis_index("x"), lax.axis_index("y"), lax.axis_index("c")


def _peer(x, y, c, p):
    px = 1 - x if p & 4 else x
    py = 1 - y if p & 2 else y
    pc = 1 - c if p & 1 else c
    return (px, py, pc), 4 * px + 2 * py + pc


HBM_SPEC = pl.BlockSpec(memory_space=pltpu.HBM)
SEM_SPEC = pl.BlockSpec(memory_space=pltpu.SEMAPHORE)
SPLIT_COPY = pltpu.CompilerParams(has_side_effects=pltpu.SideEffectType.DATAFLOW_SIDE_EFFECTING)
PEERS = N_DEV - 1


def _hbm(a):
    return pltpu.with_memory_space_constraint(a, pltpu.HBM)


def _row_block(ref, dev, rows):
    return ref.at[pl.ds(pl.multiple_of(dev * rows, BF16_ROWS), rows)]


def _copy_ends(gather, src, land, me, peer_id):
    if gather:
        rows = src.shape[0]
        return src, _row_block(land, me, rows), _row_block(land, peer_id, rows), src, _row_block(land, me, rows)
    rows = src.shape[0] // N_DEV
    return (_row_block(src, peer_id, rows), land.at[me], land.at[peer_id], _row_block(src, me, rows), land.at[me])


def _land_shape(gather, s):
    return (N_DEV * s.shape[0], s.shape[1]) if gather else (N_DEV, s.shape[0] // N_DEV, s.shape[1])


SIBLING = 1
SAME_CORE_PEERS = (2, 4, 6)
RELAYS = len(SAME_CORE_PEERS)


def _copies_start(groups, gather, name, after=None, relayed=()):
    flat = [s for g in groups for s in g]
    n, ng = len(flat), len(groups)
    lands = [lax.empty(_land_shape(gather, s), s.dtype) for s in flat]
    n_in = 2 * n + (after is not None)

    def body(*refs):
        ins, lnd = refs[:n], refs[n:2 * n]
        sems = refs[n_in:n_in + 2 * ng]
        token = refs[-1]
        x, y, c = _me()
        me = 4 * x + 2 * y + c
        w = 0
        for gi, g in enumerate(groups):
            for k in range(len(g)):
                for p in ((SIBLING,) + SAME_CORE_PEERS if gi in relayed else range(1, N_DEV)):
                    peer, peer_id = _peer(x, y, c, p)
                    src, dst, _, _, _ = _copy_ends(gather, ins[w], lnd[w], me, peer_id)
                    pltpu.make_async_remote_copy(src, dst, sems[2 * gi].at[k * PEERS + p - 1],
                                                 sems[2 * gi + 1].at[k * PEERS + p - 1], device_id=peer,
                                                 device_id_type=MESH_IDS).start()
                w += 1
        token[...] = jnp.zeros_like(token)

    sem_shapes = []
    for g in groups:
        sem_shapes += [pltpu.SemaphoreType.DMA((len(g) * PEERS,))] * 2
    out = pl.pallas_call(
        body, name=name,
        out_shape=(*sem_shapes, *[pltpu.HBM(s.shape, s.dtype) for s in flat],
                   *[pltpu.HBM(l.shape, l.dtype) for l in lands], jax.ShapeDtypeStruct((8, LANES), F32)),
        in_specs=[HBM_SPEC] * (2 * n) + [pl.BlockSpec(memory_space=pl.ANY)] * (after is not None),
        out_specs=(*[SEM_SPEC] * (2 * ng), *[HBM_SPEC] * (2 * n), pl.BlockSpec(memory_space=pltpu.VMEM)),
        input_output_aliases={i: 2 * ng + i for i in range(2 * n)},
        compiler_params=SPLIT_COPY,
    )(*[_hbm(s) for s in flat], *[_hbm(l) for l in lands], *([after] if after is not None else []))
    sems, thru, token = out[:2 * ng], out[2 * ng:2 * ng + 2 * n], out[-1]
    res, w = [], 0
    for gi, g in enumerate(groups):
        res.append((sems[2 * gi], sems[2 * gi + 1], list(thru[w:w + len(g)]), list(thru[n + w:n + w + len(g)])))
        w += len(g)
    return res, token


def _copies_wait(started, gather, after, name):
    send, recv, srcs, lands = started
    n = len(srcs)
    after = list(after) if isinstance(after, (list, tuple)) else [after]

    own_shapes = [s.shape if gather else (s.shape[0] // N_DEV, s.shape[1]) for s in srcs]

    def body(*refs):
        ins, lnd = refs[:n], refs[n:2 * n]
        send_sems, recv_sems = refs[2 * n], refs[2 * n + 1]
        bounce, in_sems, out_sems = refs[-n - 2:-2], refs[-2], refs[-1]
        x, y, c = _me()
        me = 4 * x + 2 * y + c
        ends = [_copy_ends(gather, ins[w], lnd[w], me, me)[3:] for w in range(n)]
        loads = [pltpu.make_async_copy(ends[w][0], bounce[w], in_sems.at[w]) for w in range(n)]
        stores = [pltpu.make_async_copy(bounce[w], ends[w][1], out_sems.at[w]) for w in range(n)]
        for cp in loads:
            cp.start()
        for w in range(n):
            loads[w].wait()
            stores[w].start()
        for w in range(n):
            for p in range(1, N_DEV):
                peer, peer_id = _peer(x, y, c, p)
                src, _, arrival, _, _ = _copy_ends(gather, ins[w], lnd[w], me, peer_id)
                cp = pltpu.make_async_remote_copy(src, arrival, send_sems.at[w * PEERS + p - 1],
                                                  recv_sems.at[w * PEERS + p - 1], device_id=peer,
                                                  device_id_type=MESH_IDS)
                cp.wait_send()
                cp.wait_recv()
        for cp in stores:
            cp.wait()

    out = pl.pallas_call(
        body, name=name,
        out_shape=(*[pltpu.HBM(s.shape, s.dtype) for s in srcs], *[pltpu.HBM(l.shape, l.dtype) for l in lands]),
        in_specs=[HBM_SPEC] * (2 * n) + [SEM_SPEC, SEM_SPEC] + [pl.BlockSpec(memory_space=pl.ANY)] * len(after),
        out_specs=[HBM_SPEC] * (2 * n),
        input_output_aliases={i: i for i in range(2 * n)},
        scratch_shapes=[*[pltpu.VMEM(shape, s.dtype) for shape, s in zip(own_shapes, srcs)],
                        pltpu.SemaphoreType.DMA((n,)), pltpu.SemaphoreType.DMA((n,))],
        compiler_params=SPLIT_COPY,
    )(*srcs, *lands, send, recv, *after)
    return list(out[n:])


def _relay_to_sibling(started, name, after=None):
    send, recv, srcs, lands = started
    n = len(srcs)
    after = [] if after is None else [after]

    def body(*refs):
        ins, lnd = refs[:n], refs[n:2 * n]
        send_sems, recv_sems = refs[2 * n], refs[2 * n + 1]
        relay_send, relay_recv = refs[2 * n + 2 + len(after)], refs[2 * n + 3 + len(after)]
        x, y, c = _me()
        sibling, _ = _peer(x, y, c, SIBLING)
        for w in range(n):
            rows = ins[w].shape[0]
            for k, p in enumerate(SAME_CORE_PEERS):
                peer, peer_id = _peer(x, y, c, p)
                arrived = _row_block(lnd[w], peer_id, rows)
                first = pltpu.make_async_remote_copy(ins[w], arrived, send_sems.at[w * PEERS + p - 1],
                                                     recv_sems.at[w * PEERS + p - 1], device_id=peer,
                                                     device_id_type=MESH_IDS)
                first.wait_recv()
                pltpu.make_async_remote_copy(arrived, arrived, relay_send.at[w * RELAYS + k],
                                             relay_recv.at[w * RELAYS + k], device_id=sibling,
                                             device_id_type=MESH_IDS).start()
                first.wait_send()

    sems = pltpu.SemaphoreType.DMA((n * RELAYS,))
    out = pl.pallas_call(
        body, name=name,
        out_shape=(sems, sems, *[pltpu.HBM(s.shape, s.dtype) for s in srcs], *[pltpu.HBM(l.shape, l.dtype) for l in lands]),
        in_specs=[HBM_SPEC] * (2 * n) + [SEM_SPEC, SEM_SPEC] + [pl.BlockSpec(memory_space=pl.ANY)] * len(after),
        out_specs=(SEM_SPEC, SEM_SPEC, *[HBM_SPEC] * (2 * n)),
        input_output_aliases={i: 2 + i for i in range(2 * n)},
        compiler_params=SPLIT_COPY,
    )(*srcs, *lands, send, recv, *after)
    return send, recv, out[0], out[1], list(out[2:2 + n]), list(out[2 + n:])


def _relayed_wait(relayed, after, name):
    send, recv, relay_send, relay_recv, srcs, lands = relayed
    n = len(srcs)
    after = list(after) if isinstance(after, (list, tuple)) else [after]

    def body(*refs):
        ins, lnd = refs[:n], refs[n:2 * n]
        send_sems, recv_sems, relay_send_sems, relay_recv_sems = refs[2 * n:2 * n + 4]
        bounce, in_sems, out_sems = refs[-n - 2:-2], refs[-2], refs[-1]
        x, y, c = _me()
        me = 4 * x + 2 * y + c
        sibling, sibling_id = _peer(x, y, c, SIBLING)
        loads = [pltpu.make_async_copy(ins[w], bounce[w], in_sems.at[w]) for w in range(n)]
        stores = [pltpu.make_async_copy(bounce[w], _row_block(lnd[w], me, ins[w].shape[0]), out_sems.at[w])
                  for w in range(n)]
        for cp in loads:
            cp.start()
        for w in range(n):
            loads[w].wait()
            stores[w].start()
        for w in range(n):
            rows = ins[w].shape[0]
            direct = pltpu.make_async_remote_copy(ins[w], _row_block(lnd[w], sibling_id, rows),
                                                  send_sems.at[w * PEERS + SIBLING - 1],
                                                  recv_sems.at[w * PEERS + SIBLING - 1], device_id=sibling,
                                                  device_id_type=MESH_IDS)
            direct.wait_send()
            direct.wait_recv()
            for k, p in enumerate(SAME_CORE_PEERS):
                _, sent_id = _peer(x, y, c, p)
                _, got_id = _peer(x, y, c, p + SIBLING)
                relay = pltpu.make_async_remote_copy(_row_block(lnd[w], sent_id, rows), _row_block(lnd[w], got_id, rows),
                                                     relay_send_sems.at[w * RELAYS + k],
                                                     relay_recv_sems.at[w * RELAYS + k], device_id=sibling,
                                                     device_id_type=MESH_IDS)
                relay.wait_send()
                relay.wait_recv()
        for cp in stores:
            cp.wait()

    out = pl.pallas_call(
        body, name=name,
        out_shape=(*[pltpu.HBM(s.shape, s.dtype) for s in srcs], *[pltpu.HBM(l.shape, l.dtype) for l in lands]),
        in_specs=[HBM_SPEC] * (2 * n) + [SEM_SPEC] * 4 + [pl.BlockSpec(memory_space=pl.ANY)] * len(after),
        out_specs=[HBM_SPEC] * (2 * n),
        input_output_aliases={i: i for i in range(2 * n)},
        scratch_shapes=[*[pltpu.VMEM(s.shape, s.dtype) for s in srcs],
                        pltpu.SemaphoreType.DMA((n,)), pltpu.SemaphoreType.DMA((n,))],
        compiler_params=SPLIT_COPY,
    )(*srcs, *lands, send, recv, relay_send, relay_recv, *after)
    return list(out[n:])


def _adamw_update(w, g, m, v):
    nm = ADAM_B1 * m + (1.0 - ADAM_B1) * g
    nv = ADAM_B2 * v + (1.0 - ADAM_B2) * (g * g)
    m_hat = nm / (1.0 - ADAM_B1 ** ADAM_STEP)
    v_hat = nv / (1.0 - ADAM_B2 ** ADAM_STEP)
    return -ADAM_LR * (m_hat / (jnp.sqrt(v_hat) + ADAM_EPS) + ADAM_WD * w), nm, nv


SUM_ADAMW_COLS = 512


def _sum_adamw(parts, w, m, v, name):
    _, rows, d = parts.shape
    n = w.shape[0]
    tc = SUM_ADAMW_COLS

    def body(p_ref, w_ref, m_ref, v_ref, g_ref, d_ref, nm_ref, nv_ref):
        g = p_ref[0].astype(F32)
        for dev in range(1, N_DEV):
            g = g + p_ref[dev].astype(F32)
        g = g[:n]
        g_ref[...] = g
        d_ref[...], nm_ref[...], nv_ref[...] = _adamw_update(w_ref[...], g, m_ref[...], v_ref[...])

    spec = pl.BlockSpec((n, tc), lambda j: (0, j))
    shape = jax.ShapeDtypeStruct((n, d), F32)
    return pl.pallas_call(
        body, name=name, grid=(d // tc,),
        in_specs=[pl.BlockSpec((N_DEV, rows, tc), lambda j: (0, 0, j)), spec, spec, spec],
        out_specs=[spec] * 4, out_shape=[shape] * 4,
        compiler_params=_params("arbitrary"),
    )(parts, w, m, v)


def _pad_rows(a, rows):
    return jnp.pad(a, ((0, rows - a.shape[0]), (0, 0)))


def _row1(vec, width=D_MODEL):
    return jnp.pad(vec.reshape(1, -1), ((0, 0), (0, width - vec.shape[-1])))


COLUMN_SHARDED = ("ffn1_w_gate", "ffn1_w_up", "w_in", "ffn2_w_gate", "ffn2_w_up")
VEC_NAMES = ("ffn1_norm", "mix_norm", "ffn2_norm", "b_forget", "pool_scale", "q_norm", "k_norm", "out_norm_pool",
             "out_norm_attn")
VEC_ROWS = 16
LOSS_ROW = len(VEC_NAMES)


def _pack_vector_grads(parts, loss_part, name):
    names = [n for n in VEC_NAMES if n in parts]
    extra = [] if loss_part is None else [loss_part]

    def body(*refs):
        out_ref = refs[-1]
        out_ref[...] = jnp.zeros_like(out_ref)
        lane = lax.broadcasted_iota(jnp.int32, (1, LANES), 1)
        for n, ref in zip(names, refs):
            val = ref[...]
            if n in ("q_norm", "k_norm"):
                val = val[:, 0:LANES] + val[:, LANES:2 * LANES] + val[:, 2 * LANES:3 * LANES] + val[:, 3 * LANES:]
                val = jnp.where(lane < HEAD_DIM, val + pltpu.roll(val, HEAD_DIM, 1), 0.0)
            out_ref[pl.ds(VEC_NAMES.index(n), 1), pl.ds(0, val.shape[1])] = val
        if extra:
            out_ref[pl.ds(LOSS_ROW, 1), pl.ds(0, 1)] = refs[len(names)][...]

    vmem = pl.BlockSpec(memory_space=pltpu.VMEM)
    return pl.pallas_call(
        body, name=name, in_specs=[vmem] * (len(names) + len(extra)), out_specs=vmem,
        out_shape=jax.ShapeDtypeStruct((VEC_ROWS, D_MODEL), F32),
    )(*[parts[n] for n in names], *extra)


def _small_adamw(vec_all, pool_all, vec_params, pool_params):
    nv = len(vec_params)
    pool_rows = pool_params[0].shape[0]

    def body(*refs):
        vec_ref, pool_ref = refs[0], refs[1]
        ins = refs[2:2 + 3 * nv + 3]
        outs = refs[2 + 3 * nv + 3:-1]
        rows = refs[-1]
        total = vec_ref[pl.ds(0, VEC_ROWS), :]
        for dev in range(1, N_DEV):
            total = total + vec_ref[pl.ds(dev * VEC_ROWS, VEC_ROWS), :]
        rows[...] = total
        outs[4 * nv + 4][...] = rows[pl.ds(LOSS_ROW, 1), pl.ds(0, 1)]
        for i in range(nv):
            w_ref, m_ref, v_ref = ins[3 * i:3 * i + 3]
            g = rows[pl.ds(i, 1), pl.ds(0, w_ref.shape[1])]
            outs[4 * i][...] = g
            outs[4 * i + 1][...], outs[4 * i + 2][...], outs[4 * i + 3][...] = _adamw_update(
                w_ref[...], g, m_ref[...], v_ref[...])
        g = pool_ref[pl.ds(0, pool_rows), :].astype(F32)
        for dev in range(1, N_DEV):
            g = g + pool_ref[pl.ds(dev * pool_rows, pool_rows), :].astype(F32)
        w_ref, m_ref, v_ref = ins[3 * nv:]
        outs[4 * nv][...] = g
        outs[4 * nv + 1][...], outs[4 * nv + 2][...], outs[4 * nv + 3][...] = _adamw_update(
            w_ref[...], g, m_ref[...], v_ref[...])

    vmem = pl.BlockSpec(memory_space=pltpu.VMEM)
    flat = [a for trio in vec_params for a in trio] + list(pool_params)
    out_shape = []
    for trio in list(vec_params) + [pool_params]:
        out_shape += [jax.ShapeDtypeStruct(trio[0].shape, F32)] * 4
    out_shape.append(jax.ShapeDtypeStruct((1, 1), F32))
    return pl.pallas_call(
        body, name="adamw_small", in_specs=[vmem] * (2 + len(flat)), out_specs=[vmem] * len(out_shape),
        out_shape=out_shape, scratch_shapes=[pltpu.VMEM((VEC_ROWS, D_MODEL), F32)],
    )(vec_all, pool_all, *flat)


def kernel(x, ffn1_norm, ffn1_w_gate, ffn1_w_up, ffn1_w_down, mix_norm, w_in, b_forget, pool_w, pool_scale, q_norm, k_norm, out_norm_pool, out_norm_attn, w_out, ffn2_norm, ffn2_w_gate, ffn2_w_up, ffn2_w_down, loss_target, m_ffn1_norm, m_ffn1_w_gate, m_ffn1_w_up, m_ffn1_w_down, m_mix_norm, m_w_in, m_b_forget, m_pool_w, m_pool_scale, m_q_norm, m_k_norm, m_out_norm_pool, m_out_norm_attn, m_w_out, m_ffn2_norm, m_ffn2_w_gate, m_ffn2_w_up, m_ffn2_w_down, v_ffn1_norm, v_ffn1_w_gate, v_ffn1_w_up, v_ffn1_w_down, v_mix_norm, v_w_in, v_b_forget, v_pool_w, v_pool_scale, v_q_norm, v_k_norm, v_out_norm_pool, v_out_norm_attn, v_w_out, v_ffn2_norm, v_ffn2_w_gate, v_ffn2_w_up, v_ffn2_w_down):
    bsz, seq, d = x.shape
    t = bsz * seq
    x0 = x.reshape(t, d)
    target = loss_target.reshape(t, d)
    in_rows = -(-w_in.shape[1] // BF16_ROWS) * BF16_ROWS

    slabs = [s.astype(BF16) for s in (ffn1_w_gate.T, ffn1_w_up.T, ffn1_w_down, _pad_rows(w_in.T, in_rows), w_out,
                                       ffn2_w_gate.T, ffn2_w_up.T, ffn2_w_down)]
    gathers, started = _copies_start([slabs[0:2], slabs[2:3], slabs[3:4], slabs[4:5], slabs[5:8]], True, "gather_start",
                                     relayed=(0, 4))

    g1, gm, g2 = ffn1_norm.reshape(1, d), mix_norm.reshape(1, d), ffn2_norm.reshape(1, d)
    bf_row = _row1(b_forget, LANES)
    gq = jnp.tile(q_norm, N_HEADS).reshape(1, ATTN_WIDTH)
    gk = jnp.tile(k_norm, N_HEADS).reshape(1, ATTN_WIDTH)
    scale_row = pool_scale.reshape(1, POOL_WIDTH)
    gp, ga = out_norm_pool.reshape(1, POOL_WIDTH), out_norm_attn.reshape(1, ATTN_WIDTH)

    wg1, wu1 = _relayed_wait(_relay_to_sibling(gathers[0], "gather_relay_ffn1_up"), started, "gather_wait_ffn1_up")
    h1, sa1, sb1, s1 = _ffn_up(x0, g1, wg1, wu1, "ffn1_up")
    (wd1,) = _copies_wait(gathers[1], True, s1, "gather_wait_ffn1_down")
    (x1,) = _ffn_down(s1, wd1, x0, None, "ffn1_down")
    (win_g,) = _copies_wait(gathers[2], True, x1, "gather_wait_w_in")
    win_t = _repack_rows(win_g, in_rows, w_in.shape[1], N_DEV, "w_in_rows")
    hm, pv, q, k, v, f = _mix_in_fwd(x1, gm, win_t)
    pooled, mixed, y_pool = _pool_fwd(pv, pool_w, scale_row, gp, bsz, seq)
    qp, kp = _attn_prep_fwd(q, k, f, bf_row, gq, gk, bsz, seq)
    o, lse = _flash_fwd(qp, kp, v, bsz, seq)
    relayed_ffn2 = _relay_to_sibling(gathers[4], "gather_relay_ffn2", o)
    (wout,) = _copies_wait(gathers[3], True, [o, relayed_ffn2[4][0]], "gather_wait_w_out")
    ycat, x2 = _mix_out_fwd(o, y_pool, x1, ga, wout)
    wg2, wu2, wd2 = _relayed_wait(relayed_ffn2, x2, "gather_wait_ffn2")
    h2, sa2, sb2, s2 = _ffn_up(x2, g2, wg2, wu2, "ffn2_up")
    dx3, dyh2, loss_part = _ffn_down(s2, wd2, x2, target, "ffn2_down")

    da2, db2, dwg2, dwu2 = _ffn_bwd_act(dyh2, sa2, sb2, h2, wd2, "ffn2_bwd_act")
    (dwd2,) = _wgrad([s2], dyh2, "ffn2_down_wgrad")
    (sent_ffn2,), tok = _copies_start([[dwg2, dwu2, dwd2]], False, "exchange_start_ffn2")
    dx2, dg2 = _ffn_bwd_dx(da2, db2, dx3, x2, g2 + tok[0, 0], wg2, wu2, "ffn2_bwd_dx")
    dwout, dy_pool, do, dga = _mix_out_bwd(dx2, o, ycat, ga, wout)
    (sent_out,), tok = _copies_start([[dwout]], False, "exchange_start_w_out")
    dqp, dkp, dv = _flash_bwd(qp, kp, v, o, do, lse, bsz, seq)
    dq, dk, df, dgq, dgk, dbf = _attn_prep_bwd(dqp, dkp, q, k, f, bf_row + tok[0, 0], gq, gk, bsz, seq)
    dpv, dpool_w, dscale, dgp = _pool_bwd(dy_pool, mixed, pooled, pool_w, scale_row, gp, bsz, seq)
    dwin, dx1, dyh1, dgm = _mix_in_bwd(dpv, dq, dk, dv, df, hm, x1, dx2, gm, win_t)
    dwin_blocks = _repack_rows(dwin, w_in.shape[1], in_rows, N_DEV, "w_in_grad_blocks")
    (sent_in,), tok = _copies_start([[dwin_blocks]], False, "exchange_start_w_in")
    (dwd1,) = _wgrad([s1], dyh1, "ffn1_down_wgrad")
    (sent_down1,), tok = _copies_start([[dwd1]], False, "exchange_start_ffn1_down", after=tok)
    da1, db1, dwg1, dwu1 = _ffn_bwd_act(dyh1, sa1, sb1, h1, wd1, "ffn1_bwd_act")
    (sent_up1,), tok = _copies_start([[dwg1, dwu1]], False, "exchange_start_ffn1_up", after=tok)
    dx0, dg1 = _ffn_bwd_dx(da1, db1, dx1, x0, g1 + tok[0, 0], wg1, wu1, "ffn1_bwd_dx")

    pool_rows = POOL_GROUPS * POOL_GROUP_DIM
    packed = _pack_vector_grads(dict(ffn1_norm=dg1, mix_norm=dgm, ffn2_norm=dg2, b_forget=dbf, pool_scale=dscale,
                                     q_norm=dgq, k_norm=dgk, out_norm_pool=dgp, out_norm_attn=dga), loss_part,
                                "pack_vector_grads")
    pool_part = dpool_w.reshape(pool_rows, POOL_GROUP_DIM).astype(BF16)
    (sent_small,), tok = _copies_start([[packed, pool_part]], True, "small_grads_start")

    weights = dict(ffn1_norm=ffn1_norm, ffn1_w_gate=ffn1_w_gate, ffn1_w_up=ffn1_w_up, ffn1_w_down=ffn1_w_down,
                   mix_norm=mix_norm, w_in=w_in, b_forget=b_forget, pool_w=pool_w, pool_scale=pool_scale,
                   q_norm=q_norm, k_norm=k_norm, out_norm_pool=out_norm_pool, out_norm_attn=out_norm_attn,
                   w_out=w_out, ffn2_norm=ffn2_norm, ffn2_w_gate=ffn2_w_gate, ffn2_w_up=ffn2_w_up,
                   ffn2_w_down=ffn2_w_down)
    m_in = dict(ffn1_norm=m_ffn1_norm, ffn1_w_gate=m_ffn1_w_gate, ffn1_w_up=m_ffn1_w_up, ffn1_w_down=m_ffn1_w_down,
                mix_norm=m_mix_norm, w_in=m_w_in, b_forget=m_b_forget, pool_w=m_pool_w, pool_scale=m_pool_scale,
                q_norm=m_q_norm, k_norm=m_k_norm, out_norm_pool=m_out_norm_pool, out_norm_attn=m_out_norm_attn,
                w_out=m_w_out, ffn2_norm=m_ffn2_norm, ffn2_w_gate=m_ffn2_w_gate, ffn2_w_up=m_ffn2_w_up,
                ffn2_w_down=m_ffn2_w_down)
    v_in = dict(ffn1_norm=v_ffn1_norm, ffn1_w_gate=v_ffn1_w_gate, ffn1_w_up=v_ffn1_w_up, ffn1_w_down=v_ffn1_w_down,
                mix_norm=v_mix_norm, w_in=v_w_in, b_forget=v_b_forget, pool_w=v_pool_w, pool_scale=v_pool_scale,
                q_norm=v_q_norm, k_norm=v_k_norm, out_norm_pool=v_out_norm_pool, out_norm_attn=v_out_norm_attn,
                w_out=v_w_out, ffn2_norm=v_ffn2_norm, ffn2_w_gate=v_ffn2_w_gate, ffn2_w_up=v_ffn2_w_up,
                ffn2_w_down=v_ffn2_w_down)
    grads, delta, new_m, new_v = {}, {}, {}, {}
    after = [tok]
    plan = ((sent_ffn2, "ffn2", ("ffn2_w_gate", "ffn2_w_up", "ffn2_w_down")), (sent_out, "w_out", ("w_out",)),
            (sent_in, "w_in", ("w_in",)), (sent_down1, "ffn1_down", ("ffn1_w_down",)),
            (sent_up1, "ffn1_up", ("ffn1_w_gate", "ffn1_w_up")))
    for sent, tag, names in plan:
        parts = _copies_wait(sent, False, after, f"exchange_wait_{tag}")
        after = []
        for n, part in zip(names, parts):
            turn = (lambda a: a.T) if n in COLUMN_SHARDED else (lambda a: a)
            done = _sum_adamw(part, turn(weights[n]), turn(m_in[n]), turn(v_in[n]), f"adamw_{n}")
            grads[n], delta[n], new_m[n], new_v[n] = (turn(a) for a in done)
            after.append(done[3])
    vec_all, pool_all = _copies_wait(sent_small, True, after, "small_grads_wait")
    as_row = lambda a: a.reshape(1, -1)
    as_pool = lambda a: a.reshape(pool_rows, POOL_GROUP_DIM)
    small = _small_adamw(vec_all, pool_all,
                         [tuple(as_row(z[n]) for z in (weights, m_in, v_in)) for n in VEC_NAMES],
                         tuple(as_pool(z["pool_w"]) for z in (weights, m_in, v_in)))
    for i, n in enumerate(VEC_NAMES + ("pool_w",)):
        grads[n], delta[n], new_m[n], new_v[n] = (a.reshape(weights[n].shape) for a in small[4 * i:4 * i + 4])
    loss = small[-1].reshape(())

    order = ("ffn1_norm", "ffn1_w_gate", "ffn1_w_up", "ffn1_w_down", "mix_norm", "w_in", "b_forget", "pool_w",
             "pool_scale", "q_norm", "k_norm", "out_norm_pool", "out_norm_attn", "w_out", "ffn2_norm", "ffn2_w_gate",
             "ffn2_w_up", "ffn2_w_down")
    return (loss, dx0.reshape(bsz, seq, d), *[grads[n] for n in order], *[delta[n] for n in order],
            *[new_m[n] for n in order], *[new_v[n] for n in order])
```

```python
import jax
import jax.numpy as jnp
from jax import lax
from jax.experimental import pallas as pl
from jax.experimental.pallas import tpu as pltpu

F32 = jnp.float32
BF16 = jnp.bfloat16

EPS = 1e-6
D_MODEL = 1024
N_HEADS = 8
HEAD_DIM = 64
POOL_WIDTH = 512
ATTN_WIDTH = 512
POOL_GROUPS = 4
POOL_GROUP_DIM = 128
POOL_WINDOWS = (2, 4, 8, 16)
POOL_HALO = 16
MIX_PAD = POOL_WIDTH + 3 * ATTN_WIDTH + 128
N_DEV = 8
BF16_ROWS = 16
LANES = 128
VMEM_LIMIT = 56 * 1024 * 1024

ADAM_LR = 0.001
ADAM_B1 = 0.9
ADAM_B2 = 0.999
ADAM_EPS = 1e-08
ADAM_WD = 0.01
ADAM_STEP = 10


def _params(*sem):
    return pltpu.CompilerParams(dimension_semantics=sem, vmem_limit_bytes=VMEM_LIMIT)


def _dot(a, b):
    return jnp.dot(a, b, preferred_element_type=F32)


def _dot_nt(a, b):
    return lax.dot_general(a, b, (((1,), (1,)), ((), ())), preferred_element_type=F32)


def _dot_tn(a, b):
    return lax.dot_general(a, b, (((0,), (0,)), ((), ())), preferred_element_type=F32)


def _resident(shape):
    return pl.BlockSpec(shape, lambda *_: (0,) * len(shape), pipeline_mode=pl.Buffered(1))


def _rows(tm, width):
    return pl.BlockSpec((tm, width), lambda i: (i, 0))


def _rms_scale(x):
    return lax.rsqrt(jnp.mean(x * x, axis=-1, keepdims=True) + EPS)


def _rms_bwd(dh, x, gain):
    r = _rms_scale(x)
    n = x * r
    dgain = jnp.sum(dh * n, axis=0, keepdims=True)
    dn = dh * gain
    dx = r * (dn - n * jnp.mean(dn * n, axis=-1, keepdims=True))
    return dx, dgain


def _split3(x):
    hi = x.astype(BF16)
    r1 = x - hi.astype(F32)
    mid = r1.astype(BF16)
    lo = (r1 - mid.astype(F32)).astype(BF16)
    return hi, mid, lo


FF_CHUNK = 256


def _swiglu_parts(a, b):
    sig = jax.nn.sigmoid(a)
    silu = a * sig
    return (b * (sig + silu * (1.0 - sig))).astype(BF16), silu.astype(BF16), (silu * b).astype(BF16)


def _ffn_up(x, gain, wg_t, wu_t, name):
    t, d = x.shape
    f = wg_t.shape[0]
    tm = 512

    def body(x_ref, g_ref, wg_ref, wu_ref, h_ref, sa_ref, sb_ref, s_ref):
        xv = x_ref[...]
        h = (xv * _rms_scale(xv) * g_ref[...]).astype(BF16)
        h_ref[...] = h
        for c in range(f // FF_CHUNK):
            sl = pl.ds(c * FF_CHUNK, FF_CHUNK)
            sa_ref[:, sl], sb_ref[:, sl], s_ref[:, sl] = _swiglu_parts(_dot_nt(h, wg_ref[sl, :]), _dot_nt(h, wu_ref[sl, :]))

    wide = jax.ShapeDtypeStruct((t, f), BF16)
    return pl.pallas_call(
        body, name=name, grid=(t // tm,),
        in_specs=[_rows(tm, d), _resident((1, d)), _resident((f, d)), _resident((f, d))],
        out_specs=[_rows(tm, d), _rows(tm, f), _rows(tm, f), _rows(tm, f)],
        out_shape=[jax.ShapeDtypeStruct((t, d), BF16), wide, wide, wide],
        compiler_params=_params("arbitrary"),
    )(x, gain, wg_t, wu_t)


def _ffn_down(s, wd, x, target, name):
    t, d = x.shape
    f = wd.shape[0]
    tm = 512
    with_loss = target is not None

    def body(*refs):
        if with_loss:
            s_ref, w_ref, x_ref, t_ref, dy_ref, dyh_ref, loss_ref = refs
        else:
            s_ref, w_ref, x_ref, y_ref = refs
        y = x_ref[...] + 0.5 * _dot(s_ref[...], w_ref[...])
        if with_loss:
            e = y - t_ref[...]
            dy = e * (1.0 / d)
            dy_ref[...] = dy
            dyh_ref[...] = (0.5 * dy).astype(BF16)

            @pl.when(pl.program_id(0) == 0)
            def _():
                loss_ref[...] = jnp.zeros_like(loss_ref)

            part = jnp.sum(jnp.sum(e * e, axis=0, keepdims=True), axis=1, keepdims=True)
            loss_ref[...] += part * (0.5 / d)
        else:
            y_ref[...] = y

    in_specs = [_rows(tm, f), _resident((f, d)), _rows(tm, d)]
    args = [s, wd, x]
    if with_loss:
        in_specs.append(_rows(tm, d))
        args.append(target)
        out_shape = [jax.ShapeDtypeStruct((t, d), F32), jax.ShapeDtypeStruct((t, d), BF16),
                     jax.ShapeDtypeStruct((1, 1), F32)]
        out_specs = [_rows(tm, d), _rows(tm, d), pl.BlockSpec((1, 1), lambda i: (0, 0))]
    else:
        out_shape = [jax.ShapeDtypeStruct((t, d), F32)]
        out_specs = [_rows(tm, d)]
    return pl.pallas_call(
        body, name=name, grid=(t // tm,), in_specs=in_specs, out_specs=out_specs, out_shape=out_shape,
        compiler_params=_params("arbitrary"),
    )(*args)


def _ffn_bwd_act(dyh, sa, sb, h, wd, name):
    t, d = dyh.shape
    f = wd.shape[0]
    tn = f // 2
    tk = 512
    nk = t // tk

    def body(dy_ref, sa_ref, sb_ref, h_ref, wd_ref, da_ref, db_ref, dwg_ref, dwu_ref, acc_g, acc_u):
        k = pl.program_id(1)

        @pl.when(k == 0)
        def _():
            acc_g[...] = jnp.zeros_like(acc_g)
            acc_u[...] = jnp.zeros_like(acc_u)

        ds = _dot_nt(dy_ref[...], wd_ref[...])
        da = (ds * sa_ref[...].astype(F32)).astype(BF16)
        db = (ds * sb_ref[...].astype(F32)).astype(BF16)
        da_ref[...] = da
        db_ref[...] = db
        hv = h_ref[...]
        acc_g[...] += _dot_tn(da, hv)
        acc_u[...] += _dot_tn(db, hv)

        @pl.when(k == nk - 1)
        def _():
            dwg_ref[...] = acc_g[...].astype(BF16)
            dwu_ref[...] = acc_u[...].astype(BF16)

    tokens = pl.BlockSpec((tk, d), lambda j, k: (k, 0))
    wide = pl.BlockSpec((tk, tn), lambda j, k: (k, j))
    weight = pl.BlockSpec((tn, d), lambda j, k: (j, 0))
    return pl.pallas_call(
        body, name=name, grid=(f // tn, nk),
        in_specs=[tokens, wide, wide, tokens, weight],
        out_specs=[wide, wide, weight, weight],
        out_shape=[jax.ShapeDtypeStruct((t, f), BF16)] * 2 + [jax.ShapeDtypeStruct((f, d), BF16)] * 2,
        scratch_shapes=[pltpu.VMEM((tn, d), F32)] * 2,
        compiler_params=_params("arbitrary", "arbitrary"),
    )(dyh, sa, sb, h, wd)


def _ffn_bwd_dx(da, db, dy, x, gain, wg_t, wu_t, name):
    t, d = x.shape
    f = wg_t.shape[0]
    tm = 512

    def body(da_ref, db_ref, dy_ref, x_ref, g_ref, wg_ref, wu_ref, dx_ref, dg_ref):
        dh = _dot(da_ref[...], wg_ref[...]) + _dot(db_ref[...], wu_ref[...])
        dx, dgain = _rms_bwd(dh, x_ref[...], g_ref[...])
        dx_ref[...] = dy_ref[...] + dx

        @pl.when(pl.program_id(0) == 0)
        def _():
            dg_ref[...] = jnp.zeros_like(dg_ref)

        dg_ref[...] += dgain

    return pl.pallas_call(
        body, name=name, grid=(t // tm,),
        in_specs=[_rows(tm, f), _rows(tm, f), _rows(tm, d), _rows(tm, d), _resident((1, d)), _resident((f, d)),
                  _resident((f, d))],
        out_specs=[_rows(tm, d), pl.BlockSpec((1, d), lambda i: (0, 0))],
        out_shape=[jax.ShapeDtypeStruct((t, d), F32), jax.ShapeDtypeStruct((1, d), F32)],
        compiler_params=_params("arbitrary"),
    )(da, db, dy, x, gain, wg_t, wu_t)


def _wgrad(lhs, b, name):
    t, n = lhs[0].shape
    d = b.shape[1]
    m = len(lhs)
    tn = n // 2 if n * d * m > (4 << 20) else n
    tk = 1024
    nk = t // tk

    def body(*refs):
        a_refs, b_ref, o_refs, accs = refs[:m], refs[m], refs[m + 1:2 * m + 1], refs[2 * m + 1:]
        k = pl.program_id(1)

        @pl.when(k == 0)
        def _():
            for acc in accs:
                acc[...] = jnp.zeros_like(acc)

        bv = b_ref[...]
        for a_ref, acc in zip(a_refs, accs):
            acc[...] += _dot_tn(a_ref[...], bv)

        @pl.when(k == nk - 1)
        def _():
            for o_ref, acc in zip(o_refs, accs):
                o_ref[...] = acc[...].astype(BF16)

    return pl.pallas_call(
        body, name=name, grid=(n // tn, nk),
        in_specs=[pl.BlockSpec((tk, tn), lambda j, k: (k, j))] * m + [pl.BlockSpec((tk, d), lambda j, k: (k, 0))],
        out_specs=[pl.BlockSpec((tn, d), lambda j, k: (j, 0))] * m,
        out_shape=[jax.ShapeDtypeStruct((n, d), BF16)] * m,
        scratch_shapes=[pltpu.VMEM((tn, d), F32)] * m,
        compiler_params=_params("arbitrary", "arbitrary"),
    )(*lhs, b)


def _repack_rows(a, rows_in, rows_out, blocks, name):
    total, d = a.shape
    real = min(rows_in, rows_out)

    def body(a_ref, o_ref, wide_in, wide_out):
        wide_in[...] = a_ref[...].astype(F32)
        wide_out[...] = jnp.zeros_like(wide_out)
        for j in range(blocks):
            wide_out[pl.ds(j * rows_out, real), :] = wide_in[pl.ds(j * rows_in, real), :]
        o_ref[...] = wide_out[...].astype(BF16)

    full = pl.BlockSpec((total, d), lambda i: (0, 0))
    return pl.pallas_call(
        body, name=name, grid=(1,), in_specs=[full], out_specs=full, out_shape=jax.ShapeDtypeStruct((total, d), BF16),
        scratch_shapes=[pltpu.VMEM((total, d), F32)] * 2,
        compiler_params=_params("arbitrary"),
    )(a)


def _mix_in_fwd(x, gain, w_in_t):
    t, d = x.shape
    tm = 1024
    pw, aw = POOL_WIDTH, ATTN_WIDTH

    def body(x_ref, g_ref, w_ref, hm_ref, pv_ref, q_ref, k_ref, v_ref, f_ref):
        xv = x_ref[...]
        hm = (xv * _rms_scale(xv) * g_ref[...]).astype(BF16)
        hm_ref[...] = hm
        pv_ref[...] = _dot_nt(hm, w_ref[pl.ds(0, pw), :])
        q_ref[...] = _dot_nt(hm, w_ref[pl.ds(pw, aw), :])
        k_ref[...] = _dot_nt(hm, w_ref[pl.ds(pw + aw, aw), :])
        v_ref[...] = _dot_nt(hm, w_ref[pl.ds(pw + 2 * aw, aw), :]).astype(BF16)
        f_ref[...] = _dot_nt(hm, w_ref[pl.ds(pw + 3 * aw, LANES), :])

    return pl.pallas_call(
        body, name="mix_in_fwd", grid=(t // tm,),
        in_specs=[_rows(tm, d), _resident((1, d)), _resident((MIX_PAD, d))],
        out_specs=[_rows(tm, d), _rows(tm, pw), _rows(tm, aw), _rows(tm, aw), _rows(tm, aw), _rows(tm, LANES)],
        out_shape=[jax.ShapeDtypeStruct((t, d), BF16), jax.ShapeDtypeStruct((t, pw), F32),
                   jax.ShapeDtypeStruct((t, aw), F32), jax.ShapeDtypeStruct((t, aw), F32),
                   jax.ShapeDtypeStruct((t, aw), BF16), jax.ShapeDtypeStruct((t, LANES), F32)],
        compiler_params=_params("arbitrary"),
    )(x, gain, w_in_t)


def _pool_fwd(pv, pool_w, pool_scale, gain, bsz, seq):
    ts = 512
    ns = seq // ts
    pw = POOL_WIDTH

    def body(pv_ref, w_ref, sc_ref, g_ref, pooled_ref, mixed_ref, y_ref, ext):
        s = pl.program_id(1)

        @pl.when(s == 0)
        def _():
            ext[pl.ds(0, POOL_HALO), :] = jnp.zeros((POOL_HALO, pw), F32)

        p = pv_ref[...]
        ext[pl.ds(POOL_HALO, ts), :] = p
        pos = s * ts + lax.broadcasted_iota(jnp.int32, (ts, 1), 0)
        parts = []
        for g, w in enumerate(POOL_WINDOWS):
            lanes = pl.ds(g * POOL_GROUP_DIM, POOL_GROUP_DIM)
            win = ext[pl.ds(POOL_HALO, ts), lanes]
            for i in range(1, w):
                win = win + ext[pl.ds(POOL_HALO - i, ts), lanes]
            cnt = jnp.minimum(pos + 1, w).astype(F32)
            pooled = (win / cnt - ext[pl.ds(POOL_HALO, ts), lanes]).astype(BF16)
            pooled_ref[:, lanes] = pooled
            parts.append(_dot(pooled, w_ref[g].astype(BF16)))
        mixed = jnp.concatenate(parts, axis=1)
        mixed_ref[...] = mixed
        pm = mixed * sc_ref[...]
        y_ref[...] = (pm * _rms_scale(pm) * g_ref[...]).astype(BF16)
        ext[pl.ds(0, POOL_HALO), :] = p[ts - POOL_HALO:, :]

    blk = pl.BlockSpec((ts, pw), lambda b, s: (b * ns + s, 0))
    t = bsz * seq
    return pl.pallas_call(
        body, name="pool_fwd", grid=(bsz, ns),
        in_specs=[blk, pl.BlockSpec((POOL_GROUPS, POOL_GROUP_DIM, POOL_GROUP_DIM), lambda b, s: (0, 0, 0)),
                  pl.BlockSpec((1, pw), lambda b, s: (0, 0)), pl.BlockSpec((1, pw), lambda b, s: (0, 0))],
        out_specs=[blk, blk, blk],
        out_shape=[jax.ShapeDtypeStruct((t, pw), BF16), jax.ShapeDtypeStruct((t, pw), F32),
                   jax.ShapeDtypeStruct((t, pw), BF16)],
        scratch_shapes=[pltpu.VMEM((POOL_HALO + ts, pw), F32)],
        compiler_params=_params("arbitrary", "arbitrary"),
    )(pv, pool_w, pool_scale, gain)


def _pool_bwd(dy, mixed, pooled, pool_w, pool_scale, gain, bsz, seq):
    ts = 512
    ns = seq // ts
    pw = POOL_WIDTH

    def body(dy_ref, mixed_ref, pooled_ref, w_ref, sc_ref, g_ref, dpv_ref, dw_ref, dsc_ref, dg_ref, ext):
        b = pl.program_id(0)
        sr = pl.program_id(1)
        s = ns - 1 - sr

        @pl.when(jnp.logical_and(b == 0, sr == 0))
        def _():
            dw_ref[...] = jnp.zeros_like(dw_ref)
            dsc_ref[...] = jnp.zeros_like(dsc_ref)
            dg_ref[...] = jnp.zeros_like(dg_ref)

        @pl.when(sr == 0)
        def _():
            ext[pl.ds(ts, POOL_HALO), :] = jnp.zeros((POOL_HALO, pw), F32)

        mixed = mixed_ref[...]
        sc = sc_ref[...]
        dpm, dgain = _rms_bwd(dy_ref[...], mixed * sc, g_ref[...])
        dg_ref[...] += dgain
        dsc_ref[...] += jnp.sum(dpm * mixed, axis=0, keepdims=True)
        dmixed = (dpm * sc).astype(BF16)
        pos = s * ts + lax.broadcasted_iota(jnp.int32, (ts, 1), 0)
        dpooled = []
        for g, w in enumerate(POOL_WINDOWS):
            lanes = pl.ds(g * POOL_GROUP_DIM, POOL_GROUP_DIM)
            dm = dmixed[:, g * POOL_GROUP_DIM:(g + 1) * POOL_GROUP_DIM]
            dw_ref[g] += _dot_tn(pooled_ref[:, lanes], dm)
            dp = _dot_nt(dm, w_ref[g].astype(BF16))
            dpooled.append(dp)
            cnt = jnp.minimum(pos + 1, w).astype(F32)
            ext[pl.ds(0, ts), lanes] = dp / cnt
        for g, w in enumerate(POOL_WINDOWS):
            lanes = pl.ds(g * POOL_GROUP_DIM, POOL_GROUP_DIM)
            win = ext[pl.ds(0, ts), lanes]
            for i in range(1, w):
                win = win + ext[pl.ds(i, ts), lanes]
            dpv_ref[:, lanes] = (win - dpooled[g]).astype(BF16)
        head = ext[pl.ds(0, POOL_HALO), :]
        ext[pl.ds(ts, POOL_HALO), :] = head

    blk = pl.BlockSpec((ts, pw), lambda b, s: (b * ns + (ns - 1 - s), 0))
    vec = pl.BlockSpec((1, pw), lambda b, s: (0, 0))
    wspec = pl.BlockSpec((POOL_GROUPS, POOL_GROUP_DIM, POOL_GROUP_DIM), lambda b, s: (0, 0, 0))
    t = bsz * seq
    return pl.pallas_call(
        body, name="pool_bwd", grid=(bsz, ns),
        in_specs=[blk, blk, blk, wspec, vec, vec],
        out_specs=[blk, wspec, vec, vec],
        out_shape=[jax.ShapeDtypeStruct((t, pw), BF16),
                   jax.ShapeDtypeStruct((POOL_GROUPS, POOL_GROUP_DIM, POOL_GROUP_DIM), F32),
                   jax.ShapeDtypeStruct((1, pw), F32), jax.ShapeDtypeStruct((1, pw), F32)],
        scratch_shapes=[pltpu.VMEM((ts + POOL_HALO, pw), F32)],
        compiler_params=_params("arbitrary", "arbitrary"),
    )(dy, mixed, pooled, pool_w, pool_scale, gain)


AUX_ONE = 64
AUX_F = 67

ATTN_PREP_ROWS = 512


def _seg_ones(width, seg):
    r = lax.broadcasted_iota(jnp.int32, (width, width), 0) // seg
    c = lax.broadcasted_iota(jnp.int32, (width, width), 1) // seg
    return (r == c).astype(BF16)


def _tri_ones(n, lower):
    r = lax.broadcasted_iota(jnp.int32, (n, n), 0)
    c = lax.broadcasted_iota(jnp.int32, (n, n), 1)
    return ((r >= c) if lower else (r <= c)).astype(BF16)


def _place_pieces(first_lane):
    r = lax.broadcasted_iota(jnp.int32, (3 * LANES, N_HEADS * LANES), 0)
    c = lax.broadcasted_iota(jnp.int32, (3 * LANES, N_HEADS * LANES), 1)
    piece, head = r // LANES, r % LANES
    return jnp.logical_and(head < N_HEADS, c == head * LANES + first_lane + piece).astype(BF16)


def _head_sums(x, seg_ones):
    return _dot(x.astype(BF16), seg_ones)


def _log_sigmoid(x):
    return jnp.minimum(x, 0.0) - jnp.log(1.0 + jnp.exp(-jnp.abs(x)))


def _attn_prep_fwd(q, k, f, b_forget, q_gain, k_gain, bsz, seq):
    ts = ATTN_PREP_ROWS
    ns = seq // ts
    aw = ATTN_WIDTH
    t = bsz * seq
    seg = _seg_ones(aw, HEAD_DIM)
    tri = _tri_ones(ts, True)

    def body(q_ref, k_ref, f_ref, bf_ref, gq_ref, gk_ref, seg_ref, tri_ref, place_ref, qp_ref, kp_ref, carry):
        s = pl.program_id(1)

        @pl.when(s == 0)
        def _():
            carry[...] = jnp.zeros_like(carry)

        logf = _log_sigmoid(f_ref[...] + bf_ref[...])
        hi, mid, lo = _split3(logf)
        tri_v = tri_ref[...]
        fc = _dot(tri_v, hi) + _dot(tri_v, mid) + _dot(tri_v, lo) + carry[pl.ds(0, 1), :]
        carry[pl.ds(0, 1), :] = fc[ts - 1:, :]
        pcs = jnp.concatenate(_split3(fc), axis=1)
        lane = lax.broadcasted_iota(jnp.int32, (1, LANES), 1)
        ones_q = jnp.logical_and(lane >= AUX_ONE, lane < AUX_ONE + 3).astype(F32)
        ones_k = jnp.logical_and(lane >= AUX_F, lane < AUX_F + 3).astype(F32)
        seg_v = seg_ref[...]
        placed = _dot(pcs, place_ref[...])

        def build(x_ref, g_ref, scale, out_ref, ones, for_keys):
            xv = x_ref[...]
            r = lax.rsqrt(_head_sums(xv * xv, seg_v) * (1.0 / HEAD_DIM) + EPS)
            xn = xv * r * g_ref[...] * scale
            for h in range(N_HEADS):
                pair = xn[:, (h // 2) * LANES:(h // 2 + 1) * LANES]
                feat = pair if h % 2 == 0 else pltpu.roll(pair, HEAD_DIM, 1)
                aux_h = placed[:, h * LANES:(h + 1) * LANES]
                if for_keys:
                    aux_h = -pltpu.roll(aux_h, LANES - (AUX_F - AUX_ONE), 1)
                out_ref[:, h * LANES:(h + 1) * LANES] = jnp.where(lane < HEAD_DIM, feat, aux_h + ones).astype(BF16)

        build(q_ref, gq_ref, 0.125, qp_ref, ones_q, False)
        build(k_ref, gk_ref, 1.0, kp_ref, ones_k, True)

    blk = pl.BlockSpec((ts, aw), lambda b, s: (b * ns + s, 0))
    fblk = pl.BlockSpec((ts, LANES), lambda b, s: (b * ns + s, 0))
    oblk = pl.BlockSpec((ts, N_HEADS * LANES), lambda b, s: (b * ns + s, 0))
    const = lambda shape: pl.BlockSpec(shape, lambda b, s: (0, 0))
    return pl.pallas_call(
        body, name="attn_prep_fwd", grid=(bsz, ns),
        in_specs=[blk, blk, fblk, const((1, LANES)), const((1, aw)), const((1, aw)), const((aw, aw)), const((ts, ts)),
                  const((3 * LANES, N_HEADS * LANES))],
        out_specs=[oblk, oblk],
        out_shape=[jax.ShapeDtypeStruct((t, N_HEADS * LANES), BF16)] * 2,
        scratch_shapes=[pltpu.VMEM((8, LANES), F32)],
        compiler_params=_params("arbitrary", "arbitrary"),
    )(q, k, f, b_forget, q_gain, k_gain, seg, tri, _place_pieces(AUX_F))


def _attn_prep_bwd(dqp, dkp, q, k, f, b_forget, q_gain, k_gain, bsz, seq):
    ts = ATTN_PREP_ROWS
    ns = seq // ts
    aw = ATTN_WIDTH
    t = bsz * seq
    seg = _seg_ones(aw, HEAD_DIM)
    tri = _tri_ones(ts, False)

    def body(dqp_ref, dkp_ref, q_ref, k_ref, f_ref, bf_ref, gq_ref, gk_ref, seg_ref, tri_ref,
             dq_ref, dk_ref, df_ref, dgq_ref, dgk_ref, dbf_ref, carry):
        b = pl.program_id(0)
        sr = pl.program_id(1)

        @pl.when(jnp.logical_and(b == 0, sr == 0))
        def _():
            dgq_ref[...] = jnp.zeros_like(dgq_ref)
            dgk_ref[...] = jnp.zeros_like(dgk_ref)
            dbf_ref[...] = jnp.zeros_like(dbf_ref)

        @pl.when(sr == 0)
        def _():
            carry[...] = jnp.zeros_like(carry)

        lane = lax.broadcasted_iota(jnp.int32, (1, LANES), 1)
        seg_v = seg_ref[...]

        def norm_bwd(dp_ref, x_ref, g_ref, scale, dx_ref, dgain_ref):
            parts = []
            for j in range(N_HEADS // 2):
                even = dp_ref[:, (2 * j) * LANES:(2 * j + 1) * LANES]
                odd = dp_ref[:, (2 * j + 1) * LANES:(2 * j + 2) * LANES]
                parts.append(jnp.where(lane < HEAD_DIM, even, pltpu.roll(odd, HEAD_DIM, 1)))
            dxn = jnp.concatenate(parts, axis=1) * scale
            xv = x_ref[...]
            r = lax.rsqrt(_head_sums(xv * xv, seg_v) * (1.0 / HEAD_DIM) + EPS)
            n = xv * r
            dgain_ref[...] += jnp.sum(dxn * n, axis=0, keepdims=True)
            dn = dxn * g_ref[...]
            m = _head_sums(dn * n, seg_v) * (1.0 / HEAD_DIM)
            dx_ref[...] = (r * (dn - n * m)).astype(BF16)

        norm_bwd(dqp_ref, q_ref, gq_ref, 0.125, dq_ref, dgq_ref)
        norm_bwd(dkp_ref, k_ref, gk_ref, 1.0, dk_ref, dgk_ref)

        dfc = jnp.zeros((ts, LANES), F32)
        for h in range(N_HEADS):
            cols = pl.ds(h * LANES, LANES)
            both = jnp.where(lane == AUX_F, dqp_ref[:, cols], 0.0) - jnp.where(lane == AUX_ONE, dkp_ref[:, cols], 0.0)
            dfc = jnp.where(lane == h, jnp.sum(both, axis=1, keepdims=True), dfc)
        hi, mid, lo = _split3(dfc)
        tri_v = tri_ref[...]
        dlogf = _dot(tri_v, hi) + _dot(tri_v, mid) + _dot(tri_v, lo) + carry[pl.ds(0, 1), :]
        carry[pl.ds(0, 1), :] = dlogf[0:1, :]
        df = jnp.where(lane < N_HEADS, dlogf * jax.nn.sigmoid(-(f_ref[...] + bf_ref[...])), 0.0)
        df_ref[...] = df.astype(BF16)
        dbf_ref[...] += jnp.sum(df, axis=0, keepdims=True)

    rev = lambda b, s: (b * ns + (ns - 1 - s), 0)
    blk = pl.BlockSpec((ts, aw), rev)
    fblk = pl.BlockSpec((ts, LANES), rev)
    pblk = pl.BlockSpec((ts, N_HEADS * LANES), rev)
    const = lambda shape: pl.BlockSpec(shape, lambda b, s: (0, 0))
    return pl.pallas_call(
        body, name="attn_prep_bwd", grid=(bsz, ns),
        in_specs=[pblk, pblk, blk, blk, fblk, const((1, LANES)), const((1, aw)), const((1, aw)), const((aw, aw)),
                  const((ts, ts))],
        out_specs=[blk, blk, fblk, const((1, aw)), const((1, aw)), const((1, LANES))],
        out_shape=[jax.ShapeDtypeStruct((t, aw), BF16), jax.ShapeDtypeStruct((t, aw), BF16),
                   jax.ShapeDtypeStruct((t, LANES), BF16), jax.ShapeDtypeStruct((1, aw), F32),
                   jax.ShapeDtypeStruct((1, aw), F32), jax.ShapeDtypeStruct((1, LANES), F32)],
        scratch_shapes=[pltpu.VMEM((8, LANES), F32)],
        compiler_params=_params("arbitrary", "arbitrary"),
    )(dqp, dkp, q, k, f, b_forget, q_gain, k_gain, seg, tri)


ATTN_BLOCK = 1024
HEAD_PAIRS = N_HEADS // 2


def _flash_fwd(qp, kp, v, bsz, seq):
    tq = ATTN_BLOCK
    half = tq // 2
    nq = seq // tq
    t = bsz * seq

    def body(q_ref, k_ref, v_ref, o_ref, lse_ref, m_sc, l_sc, acc_sc):
        i = pl.program_id(2)
        m_sc[...] = jnp.full(m_sc.shape, -jnp.inf, F32)
        l_sc[...] = jnp.zeros_like(l_sc)
        acc_sc[...] = jnp.zeros_like(acc_sc)
        lane = lax.broadcasted_iota(jnp.int32, (1, LANES), 1)
        low = lane < HEAD_DIM

        def tile(q0, qn, k_start, kn, k0=None):
            qs = pl.ds(q0, qn)
            ks = pl.ds(k_start, kn)
            vv = v_ref[ks, :]
            for h in range(2):
                mine = low if h == 0 else jnp.logical_not(low)
                cols = pl.ds(h * LANES, LANES)
                s = _dot_nt(q_ref[qs, cols], k_ref[ks, cols])
                if k0 is not None:
                    row = lax.broadcasted_iota(jnp.int32, (qn, kn), 0) + q0
                    col = lax.broadcasted_iota(jnp.int32, (qn, kn), 1) + k0
                    s = jnp.where(row >= col, s, -jnp.inf)
                m_prev = m_sc[h, qs, :]
                m_new = jnp.maximum(m_prev, jnp.max(s, axis=1, keepdims=True))
                p = jnp.exp(s - jnp.tile(m_new, (1, kn // LANES)))
                alpha = jnp.exp(m_prev - m_new)
                l_sc[h, qs, :] = alpha * l_sc[h, qs, :] + jnp.sum(p, axis=1, keepdims=True)
                m_sc[h, qs, :] = m_new
                pv = _dot(p.astype(BF16), jnp.where(mine, vv, jnp.zeros_like(vv)))
                acc_sc[qs, :] = acc_sc[qs, :] * jnp.where(mine, alpha, 1.0) + pv

        def below_diagonal(j, carry):
            tile(0, tq, pl.multiple_of(j * tq, tq), tq)
            return carry

        lax.fori_loop(0, i, below_diagonal, 0)
        diagonal = pl.multiple_of(i * tq, tq)
        tile(0, tq, diagonal, half, k0=0)
        tile(half, half, diagonal + half, half, k0=half)
        l = jnp.where(low, l_sc[0], l_sc[1])
        m = jnp.where(low, m_sc[0], m_sc[1])
        o_ref[...] = acc_sc[...] / l
        lse_ref[...] = m + jnp.log(l)

    qspec = pl.BlockSpec((tq, 2 * LANES), lambda b, hp, i: (b * nq + i, hp))
    kspec = pl.BlockSpec((seq, 2 * LANES), lambda b, hp, i: (b, hp))
    vspec = pl.BlockSpec((seq, LANES), lambda b, hp, i: (b, hp))
    ospec = pl.BlockSpec((tq, LANES), lambda b, hp, i: (b * nq + i, hp))
    return pl.pallas_call(
        body, name="flash_fwd", grid=(bsz, HEAD_PAIRS, nq),
        in_specs=[qspec, kspec, vspec], out_specs=[ospec, ospec],
        out_shape=[jax.ShapeDtypeStruct((t, ATTN_WIDTH), F32), jax.ShapeDtypeStruct((t, ATTN_WIDTH), F32)],
        scratch_shapes=[pltpu.VMEM((2, tq, LANES), F32), pltpu.VMEM((2, tq, LANES), F32), pltpu.VMEM((tq, LANES), F32)],
        compiler_params=_params("arbitrary", "arbitrary", "arbitrary"),
    )(qp, kp, v)


def _flash_bwd(qp, kp, v, o, do, lse, bsz, seq):
    tq = ATTN_BLOCK
    half = tq // 2
    nq = seq // tq
    t = bsz * seq

    def body(q_ref, k_ref, v_ref, o_ref, do_ref, lse_ref, dq_ref, dk_ref, dv_ref, dk_acc, dv_acc):
        j = pl.program_id(2)

        @pl.when(j == 0)
        def _():
            dq_ref[...] = jnp.zeros_like(dq_ref)

        dk_acc[...] = jnp.zeros_like(dk_acc)
        dv_acc[...] = jnp.zeros_like(dv_acc)
        lane = lax.broadcasted_iota(jnp.int32, (1, LANES), 1)
        low = lane < HEAD_DIM

        def tile(q_start, qn, k0, kn, q0=None):
            rows = pl.ds(q_start, qn)
            ks = pl.ds(k0, kn)
            dov = do_ref[rows, :]
            dd = dov * o_ref[rows, :]
            dob = dov.astype(BF16)
            vv = v_ref[ks, :]
            lse_v = lse_ref[rows, :]
            for h in range(2):
                mine = low if h == 0 else jnp.logical_not(low)
                cols = pl.ds(h * LANES, LANES)
                qh = q_ref[rows, cols]
                kh = k_ref[ks, cols]
                s = _dot_nt(qh, kh)
                lse_h = jnp.where(mine, lse_v, pltpu.roll(lse_v, HEAD_DIM, 1))
                p = jnp.exp(s - jnp.tile(lse_h, (1, kn // LANES)))
                if q0 is not None:
                    row = lax.broadcasted_iota(jnp.int32, (qn, kn), 0) + q0
                    col = lax.broadcasted_iota(jnp.int32, (qn, kn), 1) + k0
                    p = jnp.where(row >= col, p, 0.0)
                delta = jnp.sum(jnp.where(mine, dd, 0.0), axis=1, keepdims=True)
                dp = _dot_nt(dob, jnp.where(mine, vv, jnp.zeros_like(vv)))
                ds = (p * (dp - delta)).astype(BF16)
                dv_acc[ks, :] += jnp.where(mine, _dot_tn(p.astype(BF16), dob), 0.0)
                dk_acc[ks, cols] += _dot_tn(ds, qh)
                dq_ref[rows, cols] += _dot(ds, kh)

        def above_diagonal(i, carry):
            tile(pl.multiple_of(i * tq, tq), tq, 0, tq)
            return carry

        diagonal = pl.multiple_of(j * tq, tq)
        tile(diagonal, tq, 0, half, q0=0)
        tile(diagonal + half, half, half, half, q0=half)
        lax.fori_loop(j + 1, nq, above_diagonal, 0)
        dk_ref[...] = dk_acc[...]
        dv_ref[...] = dv_acc[...].astype(BF16)

    qspec = pl.BlockSpec((seq, 2 * LANES), lambda b, hp, j: (b, hp))
    kspec = pl.BlockSpec((tq, 2 * LANES), lambda b, hp, j: (b * nq + j, hp))
    vspec = pl.BlockSpec((tq, LANES), lambda b, hp, j: (b * nq + j, hp))
    ospec = pl.BlockSpec((seq, LANES), lambda b, hp, j: (b, hp))
    return pl.pallas_call(
        body, name="flash_bwd", grid=(bsz, HEAD_PAIRS, nq),
        in_specs=[qspec, kspec, vspec, ospec, ospec, ospec], out_specs=[qspec, kspec, vspec],
        out_shape=[jax.ShapeDtypeStruct((t, N_HEADS * LANES), F32), jax.ShapeDtypeStruct((t, N_HEADS * LANES), F32),
                   jax.ShapeDtypeStruct((t, ATTN_WIDTH), BF16)],
        scratch_shapes=[pltpu.VMEM((tq, 2 * LANES), F32), pltpu.VMEM((tq, LANES), F32)],
        compiler_params=_params("arbitrary", "arbitrary", "arbitrary"),
    )(qp, kp, v, o, do, lse)


def _mix_out_fwd(o, y_pool, x, gain, w_out):
    t, d = x.shape
    tm = 1024
    pw, aw = POOL_WIDTH, ATTN_WIDTH

    def body(o_ref, yp_ref, x_ref, g_ref, w_ref, ycat_ref, y_ref):
        ov = o_ref[...]
        ya = (ov * _rms_scale(ov) * g_ref[...]).astype(BF16)
        ycat = jnp.concatenate([yp_ref[...], ya], axis=1)
        ycat_ref[...] = ycat
        y_ref[...] = x_ref[...] + _dot(ycat, w_ref[...])

    return pl.pallas_call(
        body, name="mix_out_fwd", grid=(t // tm,),
        in_specs=[_rows(tm, aw), _rows(tm, pw), _rows(tm, d), _resident((1, aw)), _resident((pw + aw, d))],
        out_specs=[_rows(tm, pw + aw), _rows(tm, d)],
        out_shape=[jax.ShapeDtypeStruct((t, pw + aw), BF16), jax.ShapeDtypeStruct((t, d), F32)],
        compiler_params=_params("arbitrary"),
    )(o, y_pool, x, gain, w_out)


def _mix_out_bwd(dx, o, ycat, gain, w_out):
    t, d = dx.shape
    tm = 1024
    nm = t // tm
    pw, aw = POOL_WIDTH, ATTN_WIDTH

    def body(dx_ref, o_ref, ycat_ref, g_ref, w_ref, dw_ref, dyp_ref, do_ref, dg_ref, acc):
        i = pl.program_id(0)

        @pl.when(i == 0)
        def _():
            dg_ref[...] = jnp.zeros_like(dg_ref)
            acc[...] = jnp.zeros_like(acc)

        dxb = dx_ref[...].astype(BF16)
        acc[...] += _dot_tn(ycat_ref[...], dxb)
        dyp_ref[...] = _dot_nt(dxb, w_ref[pl.ds(0, pw), :])
        dya = _dot_nt(dxb, w_ref[pl.ds(pw, aw), :])
        do, dgain = _rms_bwd(dya, o_ref[...], g_ref[...])
        do_ref[...] = do
        dg_ref[...] += dgain

        @pl.when(i == nm - 1)
        def _():
            dw_ref[...] = acc[...].astype(BF16)

    return pl.pallas_call(
        body, name="mix_out_bwd", grid=(nm,),
        in_specs=[_rows(tm, d), _rows(tm, aw), _rows(tm, pw + aw), _resident((1, aw)), _resident((pw + aw, d))],
        out_specs=[pl.BlockSpec((pw + aw, d), lambda i: (0, 0)), _rows(tm, pw), _rows(tm, aw),
                   pl.BlockSpec((1, aw), lambda i: (0, 0))],
        out_shape=[jax.ShapeDtypeStruct((pw + aw, d), BF16), jax.ShapeDtypeStruct((t, pw), F32),
                   jax.ShapeDtypeStruct((t, aw), F32), jax.ShapeDtypeStruct((1, aw), F32)],
        scratch_shapes=[pltpu.VMEM((pw + aw, d), F32)],
        compiler_params=_params("arbitrary"),
    )(dx, o, ycat, gain, w_out)


def _mix_in_bwd(dpv, dq, dk, dv, df, hm, x, dx_res, gain, w_in_t):
    t, d = x.shape
    tm = 512
    nm = t // tm
    pw, aw = POOL_WIDTH, ATTN_WIDTH

    def body(dpv_ref, dq_ref, dk_ref, dv_ref, df_ref, hm_ref, x_ref, dxr_ref, g_ref, w_ref, dw_ref, dx_ref, dxh_ref,
             dg_ref, acc):
        i = pl.program_id(0)

        @pl.when(i == 0)
        def _():
            dg_ref[...] = jnp.zeros_like(dg_ref)
            acc[...] = jnp.zeros_like(acc)

        dh = jnp.concatenate([dpv_ref[...], dq_ref[...], dk_ref[...], dv_ref[...], df_ref[...]], axis=1)
        acc[...] += _dot_tn(dh, hm_ref[...])
        dx, dgain = _rms_bwd(_dot(dh, w_ref[...]), x_ref[...], g_ref[...])
        dx = dxr_ref[...] + dx
        dx_ref[...] = dx
        dxh_ref[...] = (0.5 * dx).astype(BF16)
        dg_ref[...] += dgain

        @pl.when(i == nm - 1)
        def _():
            dw_ref[...] = acc[...].astype(BF16)

    return pl.pallas_call(
        body, name="mix_in_bwd", grid=(nm,),
        in_specs=[_rows(tm, pw), _rows(tm, aw), _rows(tm, aw), _rows(tm, aw), _rows(tm, LANES), _rows(tm, d),
                  _rows(tm, d), _rows(tm, d), _resident((1, d)), _resident((MIX_PAD, d))],
        out_specs=[pl.BlockSpec((MIX_PAD, d), lambda i: (0, 0)), _rows(tm, d), _rows(tm, d),
                   pl.BlockSpec((1, d), lambda i: (0, 0))],
        out_shape=[jax.ShapeDtypeStruct((MIX_PAD, d), BF16), jax.ShapeDtypeStruct((t, d), F32),
                   jax.ShapeDtypeStruct((t, d), BF16), jax.ShapeDtypeStruct((1, d), F32)],
        scratch_shapes=[pltpu.VMEM((MIX_PAD, d), F32)],
        compiler_params=_params("arbitrary"),
    )(dpv, dq, dk, dv, df, hm, x, dx_res, gain, w_in_t)


MESH_IDS = pl.DeviceIdType.MESH


def _me():
    return lax.axis_index("x"), lax.axis_index("y"), lax.axis_index("c")


def _peer(x, y, c, p):
    px = 1 - x if p & 4 else x
    py = 1 - y if p & 2 else y
    pc = 1 - c if p & 1 else c
    return (px, py, pc), 4 * px + 2 * py + pc


HBM_SPEC = pl.BlockSpec(memory_space=pltpu.HBM)
SEM_SPEC = pl.BlockSpec(memory_space=pltpu.SEMAPHORE)
SPLIT_COPY = pltpu.CompilerParams(has_side_effects=pltpu.SideEffectType.DATAFLOW_SIDE_EFFECTING)
PEERS = N_DEV - 1


def _hbm(a):
    return pltpu.with_memory_space_constraint(a, pltpu.HBM)


def _row_block(ref, dev, rows):
    return ref.at[pl.ds(pl.multiple_of(dev * rows, BF16_ROWS), rows)]


def _copy_ends(gather, src, land, me, peer_id):
    if gather:
        rows = src.shape[0]
        return src, _row_block(land, me, rows), _row_block(land, peer_id, rows), src, _row_block(land, me, rows)
    rows = src.shape[0] // N_DEV
    return (_row_block(src, peer_id, rows), land.at[me], land.at[peer_id], _row_block(src, me, rows), land.at[me])


def _land_shape(gather, s):
    return (N_DEV * s.shape[0], s.shape[1]) if gather else (N_DEV, s.shape[0] // N_DEV, s.shape[1])


SIBLING = 1
SAME_CORE_PEERS = (2, 4, 6)
RELAYS = len(SAME_CORE_PEERS)


def _copies_start(groups, gather, name, after=None, relayed=()):
    flat = [s for g in groups for s in g]
    n, ng = len(flat), len(groups)
    lands = [lax.empty(_land_shape(gather, s), s.dtype) for s in flat]
    n_in = 2 * n + (after is not None)

    def body(*refs):
        ins, lnd = refs[:n], refs[n:2 * n]
        sems = refs[n_in:n_in + 2 * ng]
        token = refs[-1]
        x, y, c = _me()
        me = 4 * x + 2 * y + c
        w = 0
        for gi, g in enumerate(groups):
            for k in range(len(g)):
                for p in ((SIBLING,) + SAME_CORE_PEERS if gi in relayed else range(1, N_DEV)):
                    peer, peer_id = _peer(x, y, c, p)
                    src, dst, _, _, _ = _copy_ends(gather, ins[w], lnd[w], me, peer_id)
                    pltpu.make_async_remote_copy(src, dst, sems[2 * gi].at[k * PEERS + p - 1],
                                                 sems[2 * gi + 1].at[k * PEERS + p - 1], device_id=peer,
                                                 device_id_type=MESH_IDS).start()
                w += 1
        token[...] = jnp.zeros_like(token)

    sem_shapes = []
    for g in groups:
        sem_shapes += [pltpu.SemaphoreType.DMA((len(g) * PEERS,))] * 2
    out = pl.pallas_call(
        body, name=name,
        out_shape=(*sem_shapes, *[pltpu.HBM(s.shape, s.dtype) for s in flat],
                   *[pltpu.HBM(l.shape, l.dtype) for l in lands], jax.ShapeDtypeStruct((8, LANES), F32)),
        in_specs=[HBM_SPEC] * (2 * n) + [pl.BlockSpec(memory_space=pl.ANY)] * (after is not None),
        out_specs=(*[SEM_SPEC] * (2 * ng), *[HBM_SPEC] * (2 * n), pl.BlockSpec(memory_space=pltpu.VMEM)),
        input_output_aliases={i: 2 * ng + i for i in range(2 * n)},
        compiler_params=SPLIT_COPY,
    )(*[_hbm(s) for s in flat], *[_hbm(l) for l in lands], *([after] if after is not None else []))
    sems, thru, token = out[:2 * ng], out[2 * ng:2 * ng + 2 * n], out[-1]
    res, w = [], 0
    for gi, g in enumerate(groups):
        res.append((sems[2 * gi], sems[2 * gi + 1], list(thru[w:w + len(g)]), list(thru[n + w:n + w + len(g)])))
        w += len(g)
    return res, token


def _copies_wait(started, gather, after, name):
    send, recv, srcs, lands = started
    n = len(srcs)
    after = list(after) if isinstance(after, (list, tuple)) else [after]

    own_shapes = [s.shape if gather else (s.shape[0] // N_DEV, s.shape[1]) for s in srcs]

    def body(*refs):
        ins, lnd = refs[:n], refs[n:2 * n]
        send_sems, recv_sems = refs[2 * n], refs[2 * n + 1]
        bounce, in_sems, out_sems = refs[-n - 2:-2], refs[-2], refs[-1]
        x, y, c = _me()
        me = 4 * x + 2 * y + c
        ends = [_copy_ends(gather, ins[w], lnd[w], me, me)[3:] for w in range(n)]
        loads = [pltpu.make_async_copy(ends[w][0], bounce[w], in_sems.at[w]) for w in range(n)]
        stores = [pltpu.make_async_copy(bounce[w], ends[w][1], out_sems.at[w]) for w in range(n)]
        for cp in loads:
            cp.start()
        for w in range(n):
            loads[w].wait()
            stores[w].start()
        for w in range(n):
            for p in range(1, N_DEV):
                peer, peer_id = _peer(x, y, c, p)
                src, _, arrival, _, _ = _copy_ends(gather, ins[w], lnd[w], me, peer_id)
                cp = pltpu.make_async_remote_copy(src, arrival, send_sems.at[w * PEERS + p - 1],
                                                  recv_sems.at[w * PEERS + p - 1], device_id=peer,
                                                  device_id_type=MESH_IDS)
                cp.wait_send()
                cp.wait_recv()
        for cp in stores:
            cp.wait()

    out = pl.pallas_call(
        body, name=name,
        out_shape=(*[pltpu.HBM(s.shape, s.dtype) for s in srcs], *[pltpu.HBM(l.shape, l.dtype) for l in lands]),
        in_specs=[HBM_SPEC] * (2 * n) + [SEM_SPEC, SEM_SPEC] + [pl.BlockSpec(memory_space=pl.ANY)] * len(after),
        out_specs=[HBM_SPEC] * (2 * n),
        input_output_aliases={i: i for i in range(2 * n)},
        scratch_shapes=[*[pltpu.VMEM(shape, s.dtype) for shape, s in zip(own_shapes, srcs)],
                        pltpu.SemaphoreType.DMA((n,)), pltpu.SemaphoreType.DMA((n,))],
        compiler_params=SPLIT_COPY,
    )(*srcs, *lands, send, recv, *after)
    return list(out[n:])


def _relay_to_sibling(started, name, after=None):
    send, recv, srcs, lands = started
    n = len(srcs)
    after = [] if after is None else [after]

    def body(*refs):
        ins, lnd = refs[:n], refs[n:2 * n]
        send_sems, recv_sems = refs[2 * n], refs[2 * n + 1]
        relay_send, relay_recv = refs[2 * n + 2 + len(after)], refs[2 * n + 3 + len(after)]
        x, y, c = _me()
        sibling, _ = _peer(x, y, c, SIBLING)
        for w in range(n):
            rows = ins[w].shape[0]
            for k, p in enumerate(SAME_CORE_PEERS):
                peer, peer_id = _peer(x, y, c, p)
                arrived = _row_block(lnd[w], peer_id, rows)
                first = pltpu.make_async_remote_copy(ins[w], arrived, send_sems.at[w * PEERS + p - 1],
                                                     recv_sems.at[w * PEERS + p - 1], device_id=peer,
                                                     device_id_type=MESH_IDS)
                first.wait_recv()
                pltpu.make_async_remote_copy(arrived, arrived, relay_send.at[w * RELAYS + k],
                                             relay_recv.at[w * RELAYS + k], device_id=sibling,
                                             device_id_type=MESH_IDS).start()
                first.wait_send()

    sems = pltpu.SemaphoreType.DMA((n * RELAYS,))
    out = pl.pallas_call(
        body, name=name,
        out_shape=(sems, sems, *[pltpu.HBM(s.shape, s.dtype) for s in srcs], *[pltpu.HBM(l.shape, l.dtype) for l in lands]),
        in_specs=[HBM_SPEC] * (2 * n) + [SEM_SPEC, SEM_SPEC] + [pl.BlockSpec(memory_space=pl.ANY)] * len(after),
        out_specs=(SEM_SPEC, SEM_SPEC, *[HBM_SPEC] * (2 * n)),
        input_output_aliases={i: 2 + i for i in range(2 * n)},
        compiler_params=SPLIT_COPY,
    )(*srcs, *lands, send, recv, *after)
    return send, recv, out[0], out[1], list(out[2:2 + n]), list(out[2 + n:])


def _relayed_wait(relayed, after, name):
    send, recv, relay_send, relay_recv, srcs, lands = relayed
    n = len(srcs)
    after = list(after) if isinstance(after, (list, tuple)) else [after]

    def body(*refs):
        ins, lnd = refs[:n], refs[n:2 * n]
        send_sems, recv_sems, relay_send_sems, relay_recv_sems = refs[2 * n:2 * n + 4]
        bounce, in_sems, out_sems = refs[-n - 2:-2], refs[-2], refs[-1]
        x, y, c = _me()
        me = 4 * x + 2 * y + c
        sibling, sibling_id = _peer(x, y, c, SIBLING)
        loads = [pltpu.make_async_copy(ins[w], bounce[w], in_sems.at[w]) for w in range(n)]
        stores = [pltpu.make_async_copy(bounce[w], _row_block(lnd[w], me, ins[w].shape[0]), out_sems.at[w])
                  for w in range(n)]
        for cp in loads:
            cp.start()
        for w in range(n):
            loads[w].wait()
            stores[w].start()
        for w in range(n):
            rows = ins[w].shape[0]
            direct = pltpu.make_async_remote_copy(ins[w], _row_block(lnd[w], sibling_id, rows),
                                                  send_sems.at[w * PEERS + SIBLING - 1],
                                                  recv_sems.at[w * PEERS + SIBLING - 1], device_id=sibling,
                                                  device_id_type=MESH_IDS)
            direct.wait_send()
            direct.wait_recv()
            for k, p in enumerate(SAME_CORE_PEERS):
                _, sent_id = _peer(x, y, c, p)
                _, got_id = _peer(x, y, c, p + SIBLING)
                relay = pltpu.make_async_remote_copy(_row_block(lnd[w], sent_id, rows), _row_block(lnd[w], got_id, rows),
                                                     relay_send_sems.at[w * RELAYS + k],
                                                     relay_recv_sems.at[w * RELAYS + k], device_id=sibling,
                                                     device_id_type=MESH_IDS)
                relay.wait_send()
                relay.wait_recv()
        for cp in stores:
            cp.wait()

    out = pl.pallas_call(
        body, name=name,
        out_shape=(*[pltpu.HBM(s.shape, s.dtype) for s in srcs], *[pltpu.HBM(l.shape, l.dtype) for l in lands]),
        in_specs=[HBM_SPEC] * (2 * n) + [SEM_SPEC] * 4 + [pl.BlockSpec(memory_space=pl.ANY)] * len(after),
        out_specs=[HBM_SPEC] * (2 * n),
        input_output_aliases={i: i for i in range(2 * n)},
        scratch_shapes=[*[pltpu.VMEM(s.shape, s.dtype) for s in srcs],
                        pltpu.SemaphoreType.DMA((n,)), pltpu.SemaphoreType.DMA((n,))],
        compiler_params=SPLIT_COPY,
    )(*srcs, *lands, send, recv, relay_send, relay_recv, *after)
    return list(out[n:])


def _adamw_update(w, g, m, v):
    nm = ADAM_B1 * m + (1.0 - ADAM_B1) * g
    nv = ADAM_B2 * v + (1.0 - ADAM_B2) * (g * g)
    m_hat = nm / (1.0 - ADAM_B1 ** ADAM_STEP)
    v_hat = nv / (1.0 - ADAM_B2 ** ADAM_STEP)
    return -ADAM_LR * (m_hat / (jnp.sqrt(v_hat) + ADAM_EPS) + ADAM_WD * w), nm, nv


SUM_ADAMW_COLS = 512


def _sum_adamw(parts, w, m, v, name):
    _, rows, d = parts.shape
    n = w.shape[0]
    tc = SUM_ADAMW_COLS

    def body(p_ref, w_ref, m_ref, v_ref, g_ref, d_ref, nm_ref, nv_ref):
        g = p_ref[0].astype(F32)
        for dev in range(1, N_DEV):
            g = g + p_ref[dev].astype(F32)
        g = g[:n]
        g_ref[...] = g
        d_ref[...], nm_ref[...], nv_ref[...] = _adamw_update(w_ref[...], g, m_ref[...], v_ref[...])

    spec = pl.BlockSpec((n, tc), lambda j: (0, j))
    shape = jax.ShapeDtypeStruct((n, d), F32)
    return pl.pallas_call(
        body, name=name, grid=(d // tc,),
        in_specs=[pl.BlockSpec((N_DEV, rows, tc), lambda j: (0, 0, j)), spec, spec, spec],
        out_specs=[spec] * 4, out_shape=[shape] * 4,
        compiler_params=_params("arbitrary"),
    )(parts, w, m, v)


def _pad_rows(a, rows):
    return jnp.pad(a, ((0, rows - a.shape[0]), (0, 0)))


def _row1(vec, width=D_MODEL):
    return jnp.pad(vec.reshape(1, -1), ((0, 0), (0, width - vec.shape[-1])))


COLUMN_SHARDED = ("ffn1_w_gate", "ffn1_w_up", "w_in", "ffn2_w_gate", "ffn2_w_up")
VEC_NAMES = ("ffn1_norm", "mix_norm", "ffn2_norm", "b_forget", "pool_scale", "q_norm", "k_norm", "out_norm_pool",
             "out_norm_attn")
VEC_ROWS = 16
LOSS_ROW = len(VEC_NAMES)


def _pack_vector_grads(parts, loss_part, name):
    names = [n for n in VEC_NAMES if n in parts]
    extra = [] if loss_part is None else [loss_part]

    def body(*refs):
        out_ref = refs[-1]
        out_ref[...] = jnp.zeros_like(out_ref)
        lane = lax.broadcasted_iota(jnp.int32, (1, LANES), 1)
        for n, ref in zip(names, refs):
            val = ref[...]
            if n in ("q_norm", "k_norm"):
                val = val[:, 0:LANES] + val[:, LANES:2 * LANES] + val[:, 2 * LANES:3 * LANES] + val[:, 3 * LANES:]
                val = jnp.where(lane < HEAD_DIM, val + pltpu.roll(val, HEAD_DIM, 1), 0.0)
            out_ref[pl.ds(VEC_NAMES.index(n), 1), pl.ds(0, val.shape[1])] = val
        if extra:
            out_ref[pl.ds(LOSS_ROW, 1), pl.ds(0, 1)] = refs[len(names)][...]

    vmem = pl.BlockSpec(memory_space=pltpu.VMEM)
    return pl.pallas_call(
        body, name=name, in_specs=[vmem] * (len(names) + len(extra)), out_specs=vmem,
        out_shape=jax.ShapeDtypeStruct((VEC_ROWS, D_MODEL), F32),
    )(*[parts[n] for n in names], *extra)


def _small_adamw(vec_all, pool_all, vec_params, pool_params):
    nv = len(vec_params)
    pool_rows = pool_params[0].shape[0]

    def body(*refs):
        vec_ref, pool_ref = refs[0], refs[1]
        ins = refs[2:2 + 3 * nv + 3]
        outs = refs[2 + 3 * nv + 3:-1]
        rows = refs[-1]
        total = vec_ref[pl.ds(0, VEC_ROWS), :]
        for dev in range(1, N_DEV):
            total = total + vec_ref[pl.ds(dev * VEC_ROWS, VEC_ROWS), :]
        rows[...] = total
        outs[4 * nv + 4][...] = rows[pl.ds(LOSS_ROW, 1), pl.ds(0, 1)]
        for i in range(nv):
            w_ref, m_ref, v_ref = ins[3 * i:3 * i + 3]
            g = rows[pl.ds(i, 1), pl.ds(0, w_ref.shape[1])]
            outs[4 * i][...] = g
            outs[4 * i + 1][...], outs[4 * i + 2][...], outs[4 * i + 3][...] = _adamw_update(
                w_ref[...], g, m_ref[...], v_ref[...])
        g = pool_ref[pl.ds(0, pool_rows), :].astype(F32)
        for dev in range(1, N_DEV):
            g = g + pool_ref[pl.ds(dev * pool_rows, pool_rows), :].astype(F32)
        w_ref, m_ref, v_ref = ins[3 * nv:]
        outs[4 * nv][...] = g
        outs[4 * nv + 1][...], outs[4 * nv + 2][...], outs[4 * nv + 3][...] = _adamw_update(
            w_ref[...], g, m_ref[...], v_ref[...])

    vmem = pl.BlockSpec(memory_space=pltpu.VMEM)
    flat = [a for trio in vec_params for a in trio] + list(pool_params)
    out_shape = []
    for trio in list(vec_params) + [pool_params]:
        out_shape += [jax.ShapeDtypeStruct(trio[0].shape, F32)] * 4
    out_shape.append(jax.ShapeDtypeStruct((1, 1), F32))
    return pl.pallas_call(
        body, name="adamw_small", in_specs=[vmem] * (2 + len(flat)), out_specs=[vmem] * len(out_shape),
        out_shape=out_shape, scratch_shapes=[pltpu.VMEM((VEC_ROWS, D_MODEL), F32)],
    )(vec_all, pool_all, *flat)


def kernel(x, ffn1_norm, ffn1_w_gate, ffn1_w_up, ffn1_w_down, mix_norm, w_in, b_forget, pool_w, pool_scale, q_norm, k_norm, out_norm_pool, out_norm_attn, w_out, ffn2_norm, ffn2_w_gate, ffn2_w_up, ffn2_w_down, loss_target, m_ffn1_norm, m_ffn1_w_gate, m_ffn1_w_up, m_ffn1_w_down, m_mix_norm, m_w_in, m_b_forget, m_pool_w, m_pool_scale, m_q_norm, m_k_norm, m_out_norm_pool, m_out_norm_attn, m_w_out, m_ffn2_norm, m_ffn2_w_gate, m_ffn2_w_up, m_ffn2_w_down, v_ffn1_norm, v_ffn1_w_gate, v_ffn1_w_up, v_ffn1_w_down, v_mix_norm, v_w_in, v_b_forget, v_pool_w, v_pool_scale, v_q_norm, v_k_norm, v_out_norm_pool, v_out_norm_attn, v_w_out, v_ffn2_norm, v_ffn2_w_gate, v_ffn2_w_up, v_ffn2_w_down):
    bsz, seq, d = x.shape
    t = bsz * seq
    x0 = x.reshape(t, d)
    target = loss_target.reshape(t, d)
    in_rows = -(-w_in.shape[1] // BF16_ROWS) * BF16_ROWS

    slabs = [s.astype(BF16) for s in (ffn1_w_gate.T, ffn1_w_up.T, ffn1_w_down, _pad_rows(w_in.T, in_rows), w_out,
                                       ffn2_w_gate.T, ffn2_w_up.T, ffn2_w_down)]
    gathers, started = _copies_start([slabs[0:2], slabs[2:3], slabs[3:4], slabs[4:5], slabs[5:8]], True, "gather_start",
                                     relayed=(0, 4))

    g1, gm, g2 = ffn1_norm.reshape(1, d), mix_norm.reshape(1, d), ffn2_norm.reshape(1, d)
    bf_row = _row1(b_forget, LANES)
    gq = jnp.tile(q_norm, N_HEADS).reshape(1, ATTN_WIDTH)
    gk = jnp.tile(k_norm, N_HEADS).reshape(1, ATTN_WIDTH)
    scale_row = pool_scale.reshape(1, POOL_WIDTH)
    gp, ga = out_norm_pool.reshape(1, POOL_WIDTH), out_norm_attn.reshape(1, ATTN_WIDTH)

    wg1, wu1 = _relayed_wait(_relay_to_sibling(gathers[0], "gather_relay_ffn1_up"), started, "gather_wait_ffn1_up")
    h1, sa1, sb1, s1 = _ffn_up(x0, g1, wg1, wu1, "ffn1_up")
    (wd1,) = _copies_wait(gathers[1], True, s1, "gather_wait_ffn1_down")
    (x1,) = _ffn_down(s1, wd1, x0, None, "ffn1_down")
    (win_g,) = _copies_wait(gathers[2], True, x1, "gather_wait_w_in")
    win_t = _repack_rows(win_g, in_rows, w_in.shape[1], N_DEV, "w_in_rows")
    hm, pv, q, k, v, f = _mix_in_fwd(x1, gm, win_t)
    pooled, mixed, y_pool = _pool_fwd(pv, pool_w, scale_row, gp, bsz, seq)
    qp, kp = _attn_prep_fwd(q, k, f, bf_row, gq, gk, bsz, seq)
    o, lse = _flash_fwd(qp, kp, v, bsz, seq)
    relayed_ffn2 = _relay_to_sibling(gathers[4], "gather_relay_ffn2", o)
    (wout,) = _copies_wait(gathers[3], True, [o, relayed_ffn2[4][0]], "gather_wait_w_out")
    ycat, x2 = _mix_out_fwd(o, y_pool, x1, ga, wout)
    wg2, wu2, wd2 = _relayed_wait(relayed_ffn2, x2, "gather_wait_ffn2")
    h2, sa2, sb2, s2 = _ffn_up(x2, g2, wg2, wu2, "ffn2_up")
    dx3, dyh2, loss_part = _ffn_down(s2, wd2, x2, target, "ffn2_down")

    da2, db2, dwg2, dwu2 = _ffn_bwd_act(dyh2, sa2, sb2, h2, wd2, "ffn2_bwd_act")
    (dwd2,) = _wgrad([s2], dyh2, "ffn2_down_wgrad")
    (sent_ffn2,), tok = _copies_start([[dwg2, dwu2, dwd2]], False, "exchange_start_ffn2")
    dx2, dg2 = _ffn_bwd_dx(da2, db2, dx3, x2, g2 + tok[0, 0], wg2, wu2, "ffn2_bwd_dx")
    dwout, dy_pool, do, dga = _mix_out_bwd(dx2, o, ycat, ga, wout)
    (sent_out,), tok = _copies_start([[dwout]], False, "exchange_start_w_out")
    dqp, dkp, dv = _flash_bwd(qp, kp, v, o, do, lse, bsz, seq)
    dq, dk, df, dgq, dgk, dbf = _attn_prep_bwd(dqp, dkp, q, k, f, bf_row + tok[0, 0], gq, gk, bsz, seq)
    dpv, dpool_w, dscale, dgp = _pool_bwd(dy_pool, mixed, pooled, pool_w, scale_row, gp, bsz, seq)
    dwin, dx1, dyh1, dgm = _mix_in_bwd(dpv, dq, dk, dv, df, hm, x1, dx2, gm, win_t)
    dwin_blocks = _repack_rows(dwin, w_in.shape[1], in_rows, N_DEV, "w_in_grad_blocks")
    (sent_in,), tok = _copies_start([[dwin_blocks]], False, "exchange_start_w_in")
    (dwd1,) = _wgrad([s1], dyh1, "ffn1_down_wgrad")
    (sent_down1,), tok = _copies_start([[dwd1]], False, "exchange_start_ffn1_down", after=tok)
    da1, db1, dwg1, dwu1 = _ffn_bwd_act(dyh1, sa1, sb1, h1, wd1, "ffn1_bwd_act")
    (sent_up1,), tok = _copies_start([[dwg1, dwu1]], False, "exchange_start_ffn1_up", after=tok)
    dx0, dg1 = _ffn_bwd_dx(da1, db1, dx1, x0, g1 + tok[0, 0], wg1, wu1, "ffn1_bwd_dx")

    pool_rows = POOL_GROUPS * POOL_GROUP_DIM
    packed = _pack_vector_grads(dict(ffn1_norm=dg1, mix_norm=dgm, ffn2_norm=dg2, b_forget=dbf, pool_scale=dscale,
                                     q_norm=dgq, k_norm=dgk, out_norm_pool=dgp, out_norm_attn=dga), loss_part,
                                "pack_vector_grads")
    pool_part = dpool_w.reshape(pool_rows, POOL_GROUP_DIM).astype(BF16)
    (sent_small,), tok = _copies_start([[packed, pool_part]], True, "small_grads_start")

    weights = dict(ffn1_norm=ffn1_norm, ffn1_w_gate=ffn1_w_gate, ffn1_w_up=ffn1_w_up, ffn1_w_down=ffn1_w_down,
                   mix_norm=mix_norm, w_in=w_in, b_forget=b_forget, pool_w=pool_w, pool_scale=pool_scale,
                   q_norm=q_norm, k_norm=k_norm, out_norm_pool=out_norm_pool, out_norm_attn=out_norm_attn,
                   w_out=w_out, ffn2_norm=ffn2_norm, ffn2_w_gate=ffn2_w_gate, ffn2_w_up=ffn2_w_up,
                   ffn2_w_down=ffn2_w_down)
    m_in = dict(ffn1_norm=m_ffn1_norm, ffn1_w_gate=m_ffn1_w_gate, ffn1_w_up=m_ffn1_w_up, ffn1_w_down=m_ffn1_w_down,
                mix_norm=m_mix_norm, w_in=m_w_in, b_forget=m_b_forget, pool_w=m_pool_w, pool_scale=m_pool_scale,
                q_norm=m_q_norm, k_norm=m_k_norm, out_norm_pool=m_out_norm_pool, out_norm_attn=m_out_norm_attn,
                w_out=m_w_out, ffn2_norm=m_ffn2_norm, ffn2_w_gate=m_ffn2_w_gate, ffn2_w_up=m_ffn2_w_up,
                ffn2_w_down=m_ffn2_w_down)
    v_in = dict(ffn1_norm=v_ffn1_norm, ffn1_w_gate=v_ffn1_w_gate, ffn1_w_up=v_ffn1_w_up, ffn1_w_down=v_ffn1_w_down,
                mix_norm=v_mix_norm, w_in=v_w_in, b_forget=v_b_forget, pool_w=v_pool_w, pool_scale=v_pool_scale,
                q_norm=v_q_norm, k_norm=v_k_norm, out_norm_pool=v_out_norm_pool, out_norm_attn=v_out_norm_attn,
                w_out=v_w_out, ffn2_norm=v_ffn2_norm, ffn2_w_gate=v_ffn2_w_gate, ffn2_w_up=v_ffn2_w_up,
                ffn2_w_down=v_ffn2_w_down)
    grads, delta, new_m, new_v = {}, {}, {}, {}
    after = [tok]
    plan = ((sent_ffn2, "ffn2", ("ffn2_w_gate", "ffn2_w_up", "ffn2_w_down")), (sent_out, "w_out", ("w_out",)),
            (sent_in, "w_in", ("w_in",)), (sent_down1, "ffn1_down", ("ffn1_w_down",)),
            (sent_up1, "ffn1_up", ("ffn1_w_gate", "ffn1_w_up")))
    for sent, tag, names in plan:
        parts = _copies_wait(sent, False, after, f"exchange_wait_{tag}")
        after = []
        for n, part in zip(names, parts):
            turn = (lambda a: a.T) if n in COLUMN_SHARDED else (lambda a: a)
            done = _sum_adamw(part, turn(weights[n]), turn(m_in[n]), turn(v_in[n]), f"adamw_{n}")
            grads[n], delta[n], new_m[n], new_v[n] = (turn(a) for a in done)
            after.append(done[3])
    vec_all, pool_all = _copies_wait(sent_small, True, after, "small_grads_wait")
    as_row = lambda a: a.reshape(1, -1)
    as_pool = lambda a: a.reshape(pool_rows, POOL_GROUP_DIM)
    small = _small_adamw(vec_all, pool_all,
                         [tuple(as_row(z[n]) for z in (weights, m_in, v_in)) for n in VEC_NAMES],
                         tuple(as_pool(z["pool_w"]) for z in (weights, m_in, v_in)))
    for i, n in enumerate(VEC_NAMES + ("pool_w",)):
        grads[n], delta[n], new_m[n], new_v[n] = (a.reshape(weights[n].shape) for a in small[4 * i:4 * i + 4])
    loss = small[-1].reshape(())

    order = ("ffn1_norm", "ffn1_w_gate", "ffn1_w_up", "ffn1_w_down", "mix_norm", "w_in", "b_forget", "pool_w",
             "pool_scale", "q_norm", "k_norm", "out_norm_pool", "out_norm_attn", "w_out", "ffn2_norm", "ffn2_w_gate",
             "ffn2_w_up", "ffn2_w_down")
    return (loss, dx0.reshape(bsz, seq, d), *[grads[n] for n in order], *[delta[n] for n in order],
            *[new_m[n] for n in order], *[new_v[n] for n in order])
```

```python
import jax
import jax.numpy as jnp
from jax import lax
from jax.experimental import pallas as pl
from jax.experimental.pallas import tpu as pltpu

F32 = jnp.float32
BF16 = jnp.bfloat16

EPS = 1e-6
D_MODEL = 1024
N_HEADS = 8
HEAD_DIM = 64
POOL_WIDTH = 512
ATTN_WIDTH = 512
POOL_GROUPS = 4
POOL_GROUP_DIM = 128
POOL_WINDOWS = (2, 4, 8, 16)
POOL_HALO = 16
MIX_PAD = POOL_WIDTH + 3 * ATTN_WIDTH + 128
N_DEV = 8
BF16_ROWS = 16
LANES = 128
VMEM_LIMIT = 56 * 1024 * 1024

ADAM_LR = 0.001
ADAM_B1 = 0.9
ADAM_B2 = 0.999
ADAM_EPS = 1e-08
ADAM_WD = 0.01
ADAM_STEP = 10


def _params(*sem):
    return pltpu.CompilerParams(dimension_semantics=sem, vmem_limit_bytes=VMEM_LIMIT)


def _dot(a, b):
    return jnp.dot(a, b, preferred_element_type=F32)


def _dot_nt(a, b):
    return lax.dot_general(a, b, (((1,), (1,)), ((), ())), preferred_element_type=F32)


def _dot_tn(a, b):
    return lax.dot_general(a, b, (((0,), (0,)), ((), ())), preferred_element_type=F32)


def _resident(shape):
    return pl.BlockSpec(shape, lambda *_: (0,) * len(shape), pipeline_mode=pl.Buffered(1))


def _rows(tm, width):
    return pl.BlockSpec((tm, width), lambda i: (i, 0))


def _rms_scale(x):
    return lax.rsqrt(jnp.mean(x * x, axis=-1, keepdims=True) + EPS)


def _rms_bwd(dh, x, gain):
    r = _rms_scale(x)
    n = x * r
    dgain = jnp.sum(dh * n, axis=0, keepdims=True)
    dn = dh * gain
    dx = r * (dn - n * jnp.mean(dn * n, axis=-1, keepdims=True))
    return dx, dgain


def _split3(x):
    hi = x.astype(BF16)
    r1 = x - hi.astype(F32)
    mid = r1.astype(BF16)
    lo = (r1 - mid.astype(F32)).astype(BF16)
    return hi, mid, lo


FF_CHUNK = 256


def _swiglu_parts(a, b):
    sig = jax.nn.sigmoid(a)
    silu = a * sig
    return (b * (sig + silu * (1.0 - sig))).astype(BF16), silu.astype(BF16), (silu * b).astype(BF16)


def _ffn_up(x, gain, wg_t, wu_t, name):
    t, d = x.shape
    f = wg_t.shape[0]
    tm = 512

    def body(x_ref, g_ref, wg_ref, wu_ref, h_ref, sa_ref, sb_ref, s_ref):
        xv = x_ref[...]
        h = (xv * _rms_scale(xv) * g_ref[...]).astype(BF16)
        h_ref[...] = h
        for c in range(f // FF_CHUNK):
            sl = pl.ds(c * FF_CHUNK, FF_CHUNK)
            sa_ref[:, sl], sb_ref[:, sl], s_ref[:, sl] = _swiglu_parts(_dot_nt(h, wg_ref[sl, :]), _dot_nt(h, wu_ref[sl, :]))

    wide = jax.ShapeDtypeStruct((t, f), BF16)
    return pl.pallas_call(
        body, name=name, grid=(t // tm,),
        in_specs=[_rows(tm, d), _resident((1, d)), _resident((f, d)), _resident((f, d))],
        out_specs=[_rows(tm, d), _rows(tm, f), _rows(tm, f), _rows(tm, f)],
        out_shape=[jax.ShapeDtypeStruct((t, d), BF16), wide, wide, wide],
        compiler_params=_params("arbitrary"),
    )(x, gain, wg_t, wu_t)


def _ffn_down(s, wd, x, target, name):
    t, d = x.shape
    f = wd.shape[0]
    tm = 512
    with_loss = target is not None

    def body(*refs):
        if with_loss:
            s_ref, w_ref, x_ref, t_ref, dy_ref, dyh_ref, loss_ref = refs
        else:
            s_ref, w_ref, x_ref, y_ref = refs
        y = x_ref[...] + 0.5 * _dot(s_ref[...], w_ref[...])
        if with_loss:
            e = y - t_ref[...]
            dy = e * (1.0 / d)
            dy_ref[...] = dy
            dyh_ref[...] = (0.5 * dy).astype(BF16)

            @pl.when(pl.program_id(0) == 0)
            def _():
                loss_ref[...] = jnp.zeros_like(loss_ref)

            part = jnp.sum(jnp.sum(e * e, axis=0, keepdims=True), axis=1, keepdims=True)
            loss_ref[...] += part * (0.5 / d)
        else:
            y_ref[...] = y

    in_specs = [_rows(tm, f), _resident((f, d)), _rows(tm, d)]
    args = [s, wd, x]
    if with_loss:
        in_specs.append(_rows(tm, d))
        args.append(target)
        out_shape = [jax.ShapeDtypeStruct((t, d), F32), jax.ShapeDtypeStruct((t, d), BF16),
                     jax.ShapeDtypeStruct((1, 1), F32)]
        out_specs = [_rows(tm, d), _rows(tm, d), pl.BlockSpec((1, 1), lambda i: (0, 0))]
    else:
        out_shape = [jax.ShapeDtypeStruct((t, d), F32)]
        out_specs = [_rows(tm, d)]
    return pl.pallas_call(
        body, name=name, grid=(t // tm,), in_specs=in_specs, out_specs=out_specs, out_shape=out_shape,
        compiler_params=_params("arbitrary"),
    )(*args)


def _ffn_bwd_act(dyh, sa, sb, h, wd, name):
    t, d = dyh.shape
    f = wd.shape[0]
    tn = f // 2
    tk = 512
    nk = t // tk

    def body(dy_ref, sa_ref, sb_ref, h_ref, wd_ref, da_ref, db_ref, dwg_ref, dwu_ref, acc_g, acc_u):
        k = pl.program_id(1)

        @pl.when(k == 0)
        def _():
            acc_g[...] = jnp.zeros_like(acc_g)
            acc_u[...] = jnp.zeros_like(acc_u)

        ds = _dot_nt(dy_ref[...], wd_ref[...])
        da = (ds * sa_ref[...].astype(F32)).astype(BF16)
        db = (ds * sb_ref[...].astype(F32)).astype(BF16)
        da_ref[...] = da
        db_ref[...] = db
        hv = h_ref[...]
        acc_g[...] += _dot_tn(da, hv)
        acc_u[...] += _dot_tn(db, hv)

        @pl.when(k == nk - 1)
        def _():
            dwg_ref[...] = acc_g[...].astype(BF16)
            dwu_ref[...] = acc_u[...].astype(BF16)

    tokens = pl.BlockSpec((tk, d), lambda j, k: (k, 0))
    wide = pl.BlockSpec((tk, tn), lambda j, k: (k, j))
    weight = pl.BlockSpec((tn, d), lambda j, k: (j, 0))
    return pl.pallas_call(
        body, name=name, grid=(f // tn, nk),
        in_specs=[tokens, wide, wide, tokens, weight],
        out_specs=[wide, wide, weight, weight],
        out_shape=[jax.ShapeDtypeStruct((t, f), BF16)] * 2 + [jax.ShapeDtypeStruct((f, d), BF16)] * 2,
        scratch_shapes=[pltpu.VMEM((tn, d), F32)] * 2,
        compiler_params=_params("arbitrary", "arbitrary"),
    )(dyh, sa, sb, h, wd)


ORDER_ONLY = pl.BlockSpec(memory_space=pl.ANY)


def _ffn_bwd_dx(da, db, dy, x, gain, wg_t, wu_t, after, name):
    t, d = x.shape
    f = wg_t.shape[0]
    tm = 512

    def body(da_ref, db_ref, dy_ref, x_ref, g_ref, wg_ref, wu_ref, after_ref, dx_ref, dg_ref):
        dh = _dot(da_ref[...], wg_ref[...]) + _dot(db_ref[...], wu_ref[...])
        dx, dgain = _rms_bwd(dh, x_ref[...], g_ref[...])
        dx_ref[...] = dy_ref[...] + dx

        @pl.when(pl.program_id(0) == 0)
        def _():
            dg_ref[...] = jnp.zeros_like(dg_ref)

        dg_ref[...] += dgain

    return pl.pallas_call(
        body, name=name, grid=(t // tm,),
        in_specs=[_rows(tm, f), _rows(tm, f), _rows(tm, d), _rows(tm, d), _resident((1, d)), _resident((f, d)),
                  _resident((f, d)), ORDER_ONLY],
        out_specs=[_rows(tm, d), pl.BlockSpec((1, d), lambda i: (0, 0))],
        out_shape=[jax.ShapeDtypeStruct((t, d), F32), jax.ShapeDtypeStruct((1, d), F32)],
        compiler_params=_params("arbitrary"),
    )(da, db, dy, x, gain, wg_t, wu_t, after)


def _wgrad(lhs, b, after, name):
    t, n = lhs[0].shape
    d = b.shape[1]
    m = len(lhs)
    tn = n // 2 if n * d * m > (4 << 20) else n
    tk = 1024
    nk = t // tk

    def body(*refs):
        a_refs, b_ref, o_refs, accs = refs[:m], refs[m], refs[m + 2:2 * m + 2], refs[2 * m + 2:]
        k = pl.program_id(1)

        @pl.when(k == 0)
        def _():
            for acc in accs:
                acc[...] = jnp.zeros_like(acc)

        bv = b_ref[...]
        for a_ref, acc in zip(a_refs, accs):
            acc[...] += _dot_tn(a_ref[...], bv)

        @pl.when(k == nk - 1)
        def _():
            for o_ref, acc in zip(o_refs, accs):
                o_ref[...] = acc[...].astype(BF16)

    return pl.pallas_call(
        body, name=name, grid=(n // tn, nk),
        in_specs=[pl.BlockSpec((tk, tn), lambda j, k: (k, j))] * m + [pl.BlockSpec((tk, d), lambda j, k: (k, 0)),
                                                                       ORDER_ONLY],
        out_specs=[pl.BlockSpec((tn, d), lambda j, k: (j, 0))] * m,
        out_shape=[jax.ShapeDtypeStruct((n, d), BF16)] * m,
        scratch_shapes=[pltpu.VMEM((tn, d), F32)] * m,
        compiler_params=_params("arbitrary", "arbitrary"),
    )(*lhs, b, after)


def _repack_rows(a, rows_in, rows_out, blocks, name):
    total, d = a.shape
    real = min(rows_in, rows_out)

    def body(a_ref, o_ref, wide_in, wide_out):
        wide_in[...] = a_ref[...].astype(F32)
        wide_out[...] = jnp.zeros_like(wide_out)
        for j in range(blocks):
            wide_out[pl.ds(j * rows_out, real), :] = wide_in[pl.ds(j * rows_in, real), :]
        o_ref[...] = wide_out[...].astype(BF16)

    full = pl.BlockSpec((total, d), lambda i: (0, 0))
    return pl.pallas_call(
        body, name=name, grid=(1,), in_specs=[full], out_specs=full, out_shape=jax.ShapeDtypeStruct((total, d), BF16),
        scratch_shapes=[pltpu.VMEM((total, d), F32)] * 2,
        compiler_params=_params("arbitrary"),
    )(a)


def _mix_in_fwd(x, gain, w_in_t):
    t, d = x.shape
    tm = 1024
    pw, aw = POOL_WIDTH, ATTN_WIDTH

    def body(x_ref, g_ref, w_ref, hm_ref, pv_ref, q_ref, k_ref, v_ref, f_ref):
        xv = x_ref[...]
        hm = (xv * _rms_scale(xv) * g_ref[...]).astype(BF16)
        hm_ref[...] = hm
        pv_ref[...] = _dot_nt(hm, w_ref[pl.ds(0, pw), :])
        q_ref[...] = _dot_nt(hm, w_ref[pl.ds(pw, aw), :])
        k_ref[...] = _dot_nt(hm, w_ref[pl.ds(pw + aw, aw), :])
        v_ref[...] = _dot_nt(hm, w_ref[pl.ds(pw + 2 * aw, aw), :]).astype(BF16)
        f_ref[...] = _dot_nt(hm, w_ref[pl.ds(pw + 3 * aw, LANES), :])

    return pl.pallas_call(
        body, name="mix_in_fwd", grid=(t // tm,),
        in_specs=[_rows(tm, d), _resident((1, d)), _resident((MIX_PAD, d))],
        out_specs=[_rows(tm, d), _rows(tm, pw), _rows(tm, aw), _rows(tm, aw), _rows(tm, aw), _rows(tm, LANES)],
        out_shape=[jax.ShapeDtypeStruct((t, d), BF16), jax.ShapeDtypeStruct((t, pw), F32),
                   jax.ShapeDtypeStruct((t, aw), F32), jax.ShapeDtypeStruct((t, aw), F32),
                   jax.ShapeDtypeStruct((t, aw), BF16), jax.ShapeDtypeStruct((t, LANES), F32)],
        compiler_params=_params("arbitrary"),
    )(x, gain, w_in_t)


def _pool_fwd(pv, pool_w, pool_scale, gain, bsz, seq):
    ts = 512
    ns = seq // ts
    pw = POOL_WIDTH

    def body(pv_ref, w_ref, sc_ref, g_ref, pooled_ref, mixed_ref, y_ref, ext):
        s = pl.program_id(1)

        @pl.when(s == 0)
        def _():
            ext[pl.ds(0, POOL_HALO), :] = jnp.zeros((POOL_HALO, pw), F32)

        p = pv_ref[...]
        ext[pl.ds(POOL_HALO, ts), :] = p
        pos = s * ts + lax.broadcasted_iota(jnp.int32, (ts, 1), 0)
        parts = []
        for g, w in enumerate(POOL_WINDOWS):
            lanes = pl.ds(g * POOL_GROUP_DIM, POOL_GROUP_DIM)
            win = ext[pl.ds(POOL_HALO, ts), lanes]
            for i in range(1, w):
                win = win + ext[pl.ds(POOL_HALO - i, ts), lanes]
            cnt = jnp.minimum(pos + 1, w).astype(F32)
            pooled = (win / cnt - ext[pl.ds(POOL_HALO, ts), lanes]).astype(BF16)
            pooled_ref[:, lanes] = pooled
            parts.append(_dot(pooled, w_ref[g].astype(BF16)))
        mixed = jnp.concatenate(parts, axis=1)
        mixed_ref[...] = mixed
        pm = mixed * sc_ref[...]
        y_ref[...] = (pm * _rms_scale(pm) * g_ref[...]).astype(BF16)
        ext[pl.ds(0, POOL_HALO), :] = p[ts - POOL_HALO:, :]

    blk = pl.BlockSpec((ts, pw), lambda b, s: (b * ns + s, 0))
    t = bsz * seq
    return pl.pallas_call(
        body, name="pool_fwd", grid=(bsz, ns),
        in_specs=[blk, pl.BlockSpec((POOL_GROUPS, POOL_GROUP_DIM, POOL_GROUP_DIM), lambda b, s: (0, 0, 0)),
                  pl.BlockSpec((1, pw), lambda b, s: (0, 0)), pl.BlockSpec((1, pw), lambda b, s: (0, 0))],
        out_specs=[blk, blk, blk],
        out_shape=[jax.ShapeDtypeStruct((t, pw), BF16), jax.ShapeDtypeStruct((t, pw), F32),
                   jax.ShapeDtypeStruct((t, pw), BF16)],
        scratch_shapes=[pltpu.VMEM((POOL_HALO + ts, pw), F32)],
        compiler_params=_params("arbitrary", "arbitrary"),
    )(pv, pool_w, pool_scale, gain)


def _pool_bwd(dy, mixed, pooled, pool_w, pool_scale, gain, bsz, seq):
    ts = 512
    ns = seq // ts
    pw = POOL_WIDTH

    def body(dy_ref, mixed_ref, pooled_ref, w_ref, sc_ref, g_ref, dpv_ref, dw_ref, dsc_ref, dg_ref, ext):
        b = pl.program_id(0)
        sr = pl.program_id(1)
        s = ns - 1 - sr

        @pl.when(jnp.logical_and(b == 0, sr == 0))
        def _():
            dw_ref[...] = jnp.zeros_like(dw_ref)
            dsc_ref[...] = jnp.zeros_like(dsc_ref)
            dg_ref[...] = jnp.zeros_like(dg_ref)

        @pl.when(sr == 0)
        def _():
            ext[pl.ds(ts, POOL_HALO), :] = jnp.zeros((POOL_HALO, pw), F32)

        mixed = mixed_ref[...]
        sc = sc_ref[...]
        dpm, dgain = _rms_bwd(dy_ref[...], mixed * sc, g_ref[...])
        dg_ref[...] += dgain
        dsc_ref[...] += jnp.sum(dpm * mixed, axis=0, keepdims=True)
        dmixed = (dpm * sc).astype(BF16)
        pos = s * ts + lax.broadcasted_iota(jnp.int32, (ts, 1), 0)
        dpooled = []
        for g, w in enumerate(POOL_WINDOWS):
            lanes = pl.ds(g * POOL_GROUP_DIM, POOL_GROUP_DIM)
            dm = dmixed[:, g * POOL_GROUP_DIM:(g + 1) * POOL_GROUP_DIM]
            dw_ref[g] += _dot_tn(pooled_ref[:, lanes], dm)
            dp = _dot_nt(dm, w_ref[g].astype(BF16))
            dpooled.append(dp)
            cnt = jnp.minimum(pos + 1, w).astype(F32)
            ext[pl.ds(0, ts), lanes] = dp / cnt
        for g, w in enumerate(POOL_WINDOWS):
            lanes = pl.ds(g * POOL_GROUP_DIM, POOL_GROUP_DIM)
            win = ext[pl.ds(0, ts), lanes]
            for i in range(1, w):
                win = win + ext[pl.ds(i, ts), lanes]
            dpv_ref[:, lanes] = (win - dpooled[g]).astype(BF16)
        head = ext[pl.ds(0, POOL_HALO), :]
        ext[pl.ds(ts, POOL_HALO), :] = head

    blk = pl.BlockSpec((ts, pw), lambda b, s: (b * ns + (ns - 1 - s), 0))
    vec = pl.BlockSpec((1, pw), lambda b, s: (0, 0))
    wspec = pl.BlockSpec((POOL_GROUPS, POOL_GROUP_DIM, POOL_GROUP_DIM), lambda b, s: (0, 0, 0))
    t = bsz * seq
    return pl.pallas_call(
        body, name="pool_bwd", grid=(bsz, ns),
        in_specs=[blk, blk, blk, wspec, vec, vec],
        out_specs=[blk, wspec, vec, vec],
        out_shape=[jax.ShapeDtypeStruct((t, pw), BF16),
                   jax.ShapeDtypeStruct((POOL_GROUPS, POOL_GROUP_DIM, POOL_GROUP_DIM), F32),
                   jax.ShapeDtypeStruct((1, pw), F32), jax.ShapeDtypeStruct((1, pw), F32)],
        scratch_shapes=[pltpu.VMEM((ts + POOL_HALO, pw), F32)],
        compiler_params=_params("arbitrary", "arbitrary"),
    )(dy, mixed, pooled, pool_w, pool_scale, gain)


AUX_ONE = 64
AUX_F = 67

ATTN_PREP_ROWS = 512


def _seg_ones(width, seg):
    r = lax.broadcasted_iota(jnp.int32, (width, width), 0) // seg
    c = lax.broadcasted_iota(jnp.int32, (width, width), 1) // seg
    return (r == c).astype(BF16)


def _tri_ones(n, lower):
    r = lax.broadcasted_iota(jnp.int32, (n, n), 0)
    c = lax.broadcasted_iota(jnp.int32, (n, n), 1)
    return ((r >= c) if lower else (r <= c)).astype(BF16)


def _place_pieces(first_lane):
    r = lax.broadcasted_iota(jnp.int32, (3 * LANES, N_HEADS * LANES), 0)
    c = lax.broadcasted_iota(jnp.int32, (3 * LANES, N_HEADS * LANES), 1)
    piece, head = r // LANES, r % LANES
    return jnp.logical_and(head < N_HEADS, c == head * LANES + first_lane + piece).astype(BF16)


def _head_sums(x, seg_ones):
    return _dot(x.astype(BF16), seg_ones)


def _log_sigmoid(x):
    return jnp.minimum(x, 0.0) - jnp.log(1.0 + jnp.exp(-jnp.abs(x)))


def _attn_prep_fwd(q, k, f, b_forget, q_gain, k_gain, bsz, seq):
    ts = ATTN_PREP_ROWS
    ns = seq // ts
    aw = ATTN_WIDTH
    t = bsz * seq
    seg = _seg_ones(aw, HEAD_DIM)
    tri = _tri_ones(ts, True)

    def body(q_ref, k_ref, f_ref, bf_ref, gq_ref, gk_ref, seg_ref, tri_ref, place_ref, qp_ref, kp_ref, carry):
        s = pl.program_id(1)

        @pl.when(s == 0)
        def _():
            carry[...] = jnp.zeros_like(carry)

        logf = _log_sigmoid(f_ref[...] + bf_ref[...])
        hi, mid, lo = _split3(logf)
        tri_v = tri_ref[...]
        fc = _dot(tri_v, hi) + _dot(tri_v, mid) + _dot(tri_v, lo) + carry[pl.ds(0, 1), :]
        carry[pl.ds(0, 1), :] = fc[ts - 1:, :]
        pcs = jnp.concatenate(_split3(fc), axis=1)
        lane = lax.broadcasted_iota(jnp.int32, (1, LANES), 1)
        ones_q = jnp.logical_and(lane >= AUX_ONE, lane < AUX_ONE + 3).astype(F32)
        ones_k = jnp.logical_and(lane >= AUX_F, lane < AUX_F + 3).astype(F32)
        seg_v = seg_ref[...]
        placed = _dot(pcs, place_ref[...])

        def build(x_ref, g_ref, scale, out_ref, ones, for_keys):
            xv = x_ref[...]
            r = lax.rsqrt(_head_sums(xv * xv, seg_v) * (1.0 / HEAD_DIM) + EPS)
            xn = xv * r * g_ref[...] * scale
            for h in range(N_HEADS):
                pair = xn[:, (h // 2) * LANES:(h // 2 + 1) * LANES]
                feat = pair if h % 2 == 0 else pltpu.roll(pair, HEAD_DIM, 1)
                aux_h = placed[:, h * LANES:(h + 1) * LANES]
                if for_keys:
                    aux_h = -pltpu.roll(aux_h, LANES - (AUX_F - AUX_ONE), 1)
                out_ref[:, h * LANES:(h + 1) * LANES] = jnp.where(lane < HEAD_DIM, feat, aux_h + ones).astype(BF16)

        build(q_ref, gq_ref, 0.125, qp_ref, ones_q, False)
        build(k_ref, gk_ref, 1.0, kp_ref, ones_k, True)

    blk = pl.BlockSpec((ts, aw), lambda b, s: (b * ns + s, 0))
    fblk = pl.BlockSpec((ts, LANES), lambda b, s: (b * ns + s, 0))
    oblk = pl.BlockSpec((ts, N_HEADS * LANES), lambda b, s: (b * ns + s, 0))
    const = lambda shape: pl.BlockSpec(shape, lambda b, s: (0, 0))
    return pl.pallas_call(
        body, name="attn_prep_fwd", grid=(bsz, ns),
        in_specs=[blk, blk, fblk, const((1, LANES)), const((1, aw)), const((1, aw)), const((aw, aw)), const((ts, ts)),
                  const((3 * LANES, N_HEADS * LANES))],
        out_specs=[oblk, oblk],
        out_shape=[jax.ShapeDtypeStruct((t, N_HEADS * LANES), BF16)] * 2,
        scratch_shapes=[pltpu.VMEM((8, LANES), F32)],
        compiler_params=_params("arbitrary", "arbitrary"),
    )(q, k, f, b_forget, q_gain, k_gain, seg, tri, _place_pieces(AUX_F))


def _attn_prep_bwd(dqp, dkp, q, k, f, b_forget, q_gain, k_gain, bsz, seq):
    ts = ATTN_PREP_ROWS
    ns = seq // ts
    aw = ATTN_WIDTH
    t = bsz * seq
    seg = _seg_ones(aw, HEAD_DIM)
    tri = _tri_ones(ts, False)

    def body(dqp_ref, dkp_ref, q_ref, k_ref, f_ref, bf_ref, gq_ref, gk_ref, seg_ref, tri_ref,
             dq_ref, dk_ref, df_ref, dgq_ref, dgk_ref, dbf_ref, carry):
        b = pl.program_id(0)
        sr = pl.program_id(1)

        @pl.when(jnp.logical_and(b == 0, sr == 0))
        def _():
            dgq_ref[...] = jnp.zeros_like(dgq_ref)
            dgk_ref[...] = jnp.zeros_like(dgk_ref)
            dbf_ref[...] = jnp.zeros_like(dbf_ref)

        @pl.when(sr == 0)
        def _():
            carry[...] = jnp.zeros_like(carry)

        lane = lax.broadcasted_iota(jnp.int32, (1, LANES), 1)
        seg_v = seg_ref[...]

        def norm_bwd(dp_ref, x_ref, g_ref, scale, dx_ref, dgain_ref):
            parts = []
            for j in range(N_HEADS // 2):
                even = dp_ref[:, (2 * j) * LANES:(2 * j + 1) * LANES]
                odd = dp_ref[:, (2 * j + 1) * LANES:(2 * j + 2) * LANES]
                parts.append(jnp.where(lane < HEAD_DIM, even, pltpu.roll(odd, HEAD_DIM, 1)))
            dxn = jnp.concatenate(parts, axis=1) * scale
            xv = x_ref[...]
            r = lax.rsqrt(_head_sums(xv * xv, seg_v) * (1.0 / HEAD_DIM) + EPS)
            n = xv * r
            dgain_ref[...] += jnp.sum(dxn * n, axis=0, keepdims=True)
            dn = dxn * g_ref[...]
            m = _head_sums(dn * n, seg_v) * (1.0 / HEAD_DIM)
            dx_ref[...] = (r * (dn - n * m)).astype(BF16)

        norm_bwd(dqp_ref, q_ref, gq_ref, 0.125, dq_ref, dgq_ref)
        norm_bwd(dkp_ref, k_ref, gk_ref, 1.0, dk_ref, dgk_ref)

        dfc = jnp.zeros((ts, LANES), F32)
        for h in range(N_HEADS):
            cols = pl.ds(h * LANES, LANES)
            both = jnp.where(lane == AUX_F, dqp_ref[:, cols], 0.0) - jnp.where(lane == AUX_ONE, dkp_ref[:, cols], 0.0)
            dfc = jnp.where(lane == h, jnp.sum(both, axis=1, keepdims=True), dfc)
        hi, mid, lo = _split3(dfc)
        tri_v = tri_ref[...]
        dlogf = _dot(tri_v, hi) + _dot(tri_v, mid) + _dot(tri_v, lo) + carry[pl.ds(0, 1), :]
        carry[pl.ds(0, 1), :] = dlogf[0:1, :]
        df = jnp.where(lane < N_HEADS, dlogf * jax.nn.sigmoid(-(f_ref[...] + bf_ref[...])), 0.0)
        df_ref[...] = df.astype(BF16)
        dbf_ref[...] += jnp.sum(df, axis=0, keepdims=True)

    rev = lambda b, s: (b * ns + (ns - 1 - s), 0)
    blk = pl.BlockSpec((ts, aw), rev)
    fblk = pl.BlockSpec((ts, LANES), rev)
    pblk = pl.BlockSpec((ts, N_HEADS * LANES), rev)
    const = lambda shape: pl.BlockSpec(shape, lambda b, s: (0, 0))
    return pl.pallas_call(
        body, name="attn_prep_bwd", grid=(bsz, ns),
        in_specs=[pblk, pblk, blk, blk, fblk, const((1, LANES)), const((1, aw)), const((1, aw)), const((aw, aw)),
                  const((ts, ts))],
        out_specs=[blk, blk, fblk, const((1, aw)), const((1, aw)), const((1, LANES))],
        out_shape=[jax.ShapeDtypeStruct((t, aw), BF16), jax.ShapeDtypeStruct((t, aw), BF16),
                   jax.ShapeDtypeStruct((t, LANES), BF16), jax.ShapeDtypeStruct((1, aw), F32),
                   jax.ShapeDtypeStruct((1, aw), F32), jax.ShapeDtypeStruct((1, LANES), F32)],
        scratch_shapes=[pltpu.VMEM((8, LANES), F32)],
        compiler_params=_params("arbitrary", "arbitrary"),
    )(dqp, dkp, q, k, f, b_forget, q_gain, k_gain, seg, tri)


ATTN_BLOCK = 1024
HEAD_PAIRS = N_HEADS // 2


def _flash_fwd(qp, kp, v, bsz, seq):
    tq = ATTN_BLOCK
    half = tq // 2
    nq = seq // tq
    t = bsz * seq

    def body(q_ref, k_ref, v_ref, o_ref, lse_ref, m_sc, l_sc, acc_sc):
        i = pl.program_id(2)
        m_sc[...] = jnp.full(m_sc.shape, -jnp.inf, F32)
        l_sc[...] = jnp.zeros_like(l_sc)
        acc_sc[...] = jnp.zeros_like(acc_sc)
        lane = lax.broadcasted_iota(jnp.int32, (1, LANES), 1)
        low = lane < HEAD_DIM

        def tile(q0, qn, k_start, kn, k0=None):
            qs = pl.ds(q0, qn)
            ks = pl.ds(k_start, kn)
            vv = v_ref[ks, :]
            for h in range(2):
                mine = low if h == 0 else jnp.logical_not(low)
                cols = pl.ds(h * LANES, LANES)
                s = _dot_nt(q_ref[qs, cols], k_ref[ks, cols])
                if k0 is not None:
                    row = lax.broadcasted_iota(jnp.int32, (qn, kn), 0) + q0
                    col = lax.broadcasted_iota(jnp.int32, (qn, kn), 1) + k0
                    s = jnp.where(row >= col, s, -jnp.inf)
                m_prev = m_sc[h, qs, :]
                m_new = jnp.maximum(m_prev, jnp.max(s, axis=1, keepdims=True))
                p = jnp.exp(s - jnp.tile(m_new, (1, kn // LANES)))
                alpha = jnp.exp(m_prev - m_new)
                l_sc[h, qs, :] = alpha * l_sc[h, qs, :] + jnp.sum(p, axis=1, keepdims=True)
                m_sc[h, qs, :] = m_new
                pv = _dot(p.astype(BF16), jnp.where(mine, vv, jnp.zeros_like(vv)))
                acc_sc[qs, :] = acc_sc[qs, :] * jnp.where(mine, alpha, 1.0) + pv

        def below_diagonal(j, carry):
            tile(0, tq, pl.multiple_of(j * tq, tq), tq)
            return carry

        lax.fori_loop(0, i, below_diagonal, 0)
        diagonal = pl.multiple_of(i * tq, tq)
        tile(0, tq, diagonal, half, k0=0)
        tile(half, half, diagonal + half, half, k0=half)
        l = jnp.where(low, l_sc[0], l_sc[1])
        m = jnp.where(low, m_sc[0], m_sc[1])
        o_ref[...] = acc_sc[...] / l
        lse_ref[...] = m + jnp.log(l)

    qspec = pl.BlockSpec((tq, 2 * LANES), lambda b, hp, i: (b * nq + i, hp))
    kspec = pl.BlockSpec((seq, 2 * LANES), lambda b, hp, i: (b, hp))
    vspec = pl.BlockSpec((seq, LANES), lambda b, hp, i: (b, hp))
    ospec = pl.BlockSpec((tq, LANES), lambda b, hp, i: (b * nq + i, hp))
    return pl.pallas_call(
        body, name="flash_fwd", grid=(bsz, HEAD_PAIRS, nq),
        in_specs=[qspec, kspec, vspec], out_specs=[ospec, ospec],
        out_shape=[jax.ShapeDtypeStruct((t, ATTN_WIDTH), F32), jax.ShapeDtypeStruct((t, ATTN_WIDTH), F32)],
        scratch_shapes=[pltpu.VMEM((2, tq, LANES), F32), pltpu.VMEM((2, tq, LANES), F32), pltpu.VMEM((tq, LANES), F32)],
        compiler_params=_params("arbitrary", "arbitrary", "arbitrary"),
    )(qp, kp, v)


def _flash_bwd(qp, kp, v, o, do, lse, bsz, seq):
    tq = ATTN_BLOCK
    half = tq // 2
    nq = seq // tq
    t = bsz * seq

    def body(q_ref, k_ref, v_ref, o_ref, do_ref, lse_ref, dq_ref, dk_ref, dv_ref, dk_acc, dv_acc):
        j = pl.program_id(2)

        @pl.when(j == 0)
        def _():
            dq_ref[...] = jnp.zeros_like(dq_ref)

        dk_acc[...] = jnp.zeros_like(dk_acc)
        dv_acc[...] = jnp.zeros_like(dv_acc)
        lane = lax.broadcasted_iota(jnp.int32, (1, LANES), 1)
        low = lane < HEAD_DIM

        def tile(q_start, qn, k0, kn, q0=None):
            rows = pl.ds(q_start, qn)
            ks = pl.ds(k0, kn)
            dov = do_ref[rows, :]
            dd = dov * o_ref[rows, :]
            dob = dov.astype(BF16)
            vv = v_ref[ks, :]
            lse_v = lse_ref[rows, :]
            for h in range(2):
                mine = low if h == 0 else jnp.logical_not(low)
                cols = pl.ds(h * LANES, LANES)
                qh = q_ref[rows, cols]
                kh = k_ref[ks, cols]
                s = _dot_nt(qh, kh)
                lse_h = jnp.where(mine, lse_v, pltpu.roll(lse_v, HEAD_DIM, 1))
                p = jnp.exp(s - jnp.tile(lse_h, (1, kn // LANES)))
                if q0 is not None:
                    row = lax.broadcasted_iota(jnp.int32, (qn, kn), 0) + q0
                    col = lax.broadcasted_iota(jnp.int32, (qn, kn), 1) + k0
                    p = jnp.where(row >= col, p, 0.0)
                delta = jnp.sum(jnp.where(mine, dd, 0.0), axis=1, keepdims=True)
                dp = _dot_nt(dob, jnp.where(mine, vv, jnp.zeros_like(vv)))
                ds = (p * (dp - delta)).astype(BF16)
                dv_acc[ks, :] += jnp.where(mine, _dot_tn(p.astype(BF16), dob), 0.0)
                dk_acc[ks, cols] += _dot_tn(ds, qh)
                dq_ref[rows, cols] += _dot(ds, kh)

        def above_diagonal(i, carry):
            tile(pl.multiple_of(i * tq, tq), tq, 0, tq)
            return carry

        diagonal = pl.multiple_of(j * tq, tq)
        tile(diagonal, tq, 0, half, q0=0)
        tile(diagonal + half, half, half, half, q0=half)
        lax.fori_loop(j + 1, nq, above_diagonal, 0)
        dk_ref[...] = dk_acc[...]
        dv_ref[...] = dv_acc[...].astype(BF16)

    qspec = pl.BlockSpec((seq, 2 * LANES), lambda b, hp, j: (b, hp))
    kspec = pl.BlockSpec((tq, 2 * LANES), lambda b, hp, j: (b * nq + j, hp))
    vspec = pl.BlockSpec((tq, LANES), lambda b, hp, j: (b * nq + j, hp))
    ospec = pl.BlockSpec((seq, LANES), lambda b, hp, j: (b, hp))
    return pl.pallas_call(
        body, name="flash_bwd", grid=(bsz, HEAD_PAIRS, nq),
        in_specs=[qspec, kspec, vspec, ospec, ospec, ospec], out_specs=[qspec, kspec, vspec],
        out_shape=[jax.ShapeDtypeStruct((t, N_HEADS * LANES), F32), jax.ShapeDtypeStruct((t, N_HEADS * LANES), F32),
                   jax.ShapeDtypeStruct((t, ATTN_WIDTH), BF16)],
        scratch_shapes=[pltpu.VMEM((tq, 2 * LANES), F32), pltpu.VMEM((tq, LANES), F32)],
        compiler_params=_params("arbitrary", "arbitrary", "arbitrary"),
    )(qp, kp, v, o, do, lse)


def _mix_out_fwd(o, y_pool, x, gain, w_out):
    t, d = x.shape
    tm = 1024
    pw, aw = POOL_WIDTH, ATTN_WIDTH

    def body(o_ref, yp_ref, x_ref, g_ref, w_ref, ycat_ref, y_ref):
        ov = o_ref[...]
        ya = (ov * _rms_scale(ov) * g_ref[...]).astype(BF16)
        ycat = jnp.concatenate([yp_ref[...], ya], axis=1)
        ycat_ref[...] = ycat
        y_ref[...] = x_ref[...] + _dot(ycat, w_ref[...])

    return pl.pallas_call(
        body, name="mix_out_fwd", grid=(t // tm,),
        in_specs=[_rows(tm, aw), _rows(tm, pw), _rows(tm, d), _resident((1, aw)), _resident((pw + aw, d))],
        out_specs=[_rows(tm, pw + aw), _rows(tm, d)],
        out_shape=[jax.ShapeDtypeStruct((t, pw + aw), BF16), jax.ShapeDtypeStruct((t, d), F32)],
        compiler_params=_params("arbitrary"),
    )(o, y_pool, x, gain, w_out)


def _mix_out_bwd(dx, o, ycat, gain, w_out):
    t, d = dx.shape
    tm = 1024
    nm = t // tm
    pw, aw = POOL_WIDTH, ATTN_WIDTH

    def body(dx_ref, o_ref, ycat_ref, g_ref, w_ref, dw_ref, dyp_ref, do_ref, dg_ref, acc):
        i = pl.program_id(0)

        @pl.when(i == 0)
        def _():
            dg_ref[...] = jnp.zeros_like(dg_ref)
            acc[...] = jnp.zeros_like(acc)

        dxb = dx_ref[...].astype(BF16)
        acc[...] += _dot_tn(ycat_ref[...], dxb)
        dyp_ref[...] = _dot_nt(dxb, w_ref[pl.ds(0, pw), :])
        dya = _dot_nt(dxb, w_ref[pl.ds(pw, aw), :])
        do, dgain = _rms_bwd(dya, o_ref[...], g_ref[...])
        do_ref[...] = do
        dg_ref[...] += dgain

        @pl.when(i == nm - 1)
        def _():
            dw_ref[...] = acc[...].astype(BF16)

    return pl.pallas_call(
        body, name="mix_out_bwd", grid=(nm,),
        in_specs=[_rows(tm, d), _rows(tm, aw), _rows(tm, pw + aw), _resident((1, aw)), _resident((pw + aw, d))],
        out_specs=[pl.BlockSpec((pw + aw, d), lambda i: (0, 0)), _rows(tm, pw), _rows(tm, aw),
                   pl.BlockSpec((1, aw), lambda i: (0, 0))],
        out_shape=[jax.ShapeDtypeStruct((pw + aw, d), BF16), jax.ShapeDtypeStruct((t, pw), F32),
                   jax.ShapeDtypeStruct((t, aw), F32), jax.ShapeDtypeStruct((1, aw), F32)],
        scratch_shapes=[pltpu.VMEM((pw + aw, d), F32)],
        compiler_params=_params("arbitrary"),
    )(dx, o, ycat, gain, w_out)


def _mix_in_bwd(dpv, dq, dk, dv, df, hm, x, dx_res, gain, w_in_t):
    t, d = x.shape
    tm = 512
    nm = t // tm
    pw, aw = POOL_WIDTH, ATTN_WIDTH

    def body(dpv_ref, dq_ref, dk_ref, dv_ref, df_ref, hm_ref, x_ref, dxr_ref, g_ref, w_ref, dw_ref, dx_ref, dxh_ref,
             dg_ref, acc):
        i = pl.program_id(0)

        @pl.when(i == 0)
        def _():
            dg_ref[...] = jnp.zeros_like(dg_ref)
            acc[...] = jnp.zeros_like(acc)

        dh = jnp.concatenate([dpv_ref[...], dq_ref[...], dk_ref[...], dv_ref[...], df_ref[...]], axis=1)
        acc[...] += _dot_tn(dh, hm_ref[...])
        dx, dgain = _rms_bwd(_dot(dh, w_ref[...]), x_ref[...], g_ref[...])
        dx = dxr_ref[...] + dx
        dx_ref[...] = dx
        dxh_ref[...] = (0.5 * dx).astype(BF16)
        dg_ref[...] += dgain

        @pl.when(i == nm - 1)
        def _():
            dw_ref[...] = acc[...].astype(BF16)

    return pl.pallas_call(
        body, name="mix_in_bwd", grid=(nm,),
        in_specs=[_rows(tm, pw), _rows(tm, aw), _rows(tm, aw), _rows(tm, aw), _rows(tm, LANES), _rows(tm, d),
                  _rows(tm, d), _rows(tm, d), _resident((1, d)), _resident((MIX_PAD, d))],
        out_specs=[pl.BlockSpec((MIX_PAD, d), lambda i: (0, 0)), _rows(tm, d), _rows(tm, d),
                   pl.BlockSpec((1, d), lambda i: (0, 0))],
        out_shape=[jax.ShapeDtypeStruct((MIX_PAD, d), BF16), jax.ShapeDtypeStruct((t, d), F32),
                   jax.ShapeDtypeStruct((t, d), BF16), jax.ShapeDtypeStruct((1, d), F32)],
        scratch_shapes=[pltpu.VMEM((MIX_PAD, d), F32)],
        compiler_params=_params("arbitrary"),
    )(dpv, dq, dk, dv, df, hm, x, dx_res, gain, w_in_t)


MESH_IDS = pl.DeviceIdType.MESH


def _me():
    return lax.axis_index("x"), lax.axis_index("y"), lax.axis_index("c")


def _peer(x, y, c, p):
    px = 1 - x if p & 4 else x
    py = 1 - y if p & 2 else y
    pc = 1 - c if p & 1 else c
    return (px, py, pc), 4 * px + 2 * py + pc


HBM_SPEC = pl.BlockSpec(memory_space=pltpu.HBM)
SEM_SPEC = pl.BlockSpec(memory_space=pltpu.SEMAPHORE)
SPLIT_COPY = pltpu.CompilerParams(has_side_effects=pltpu.SideEffectType.DATAFLOW_SIDE_EFFECTING)
PEERS = N_DEV - 1


def _hbm(a):
    return pltpu.with_memory_space_constraint(a, pltpu.HBM)


def _row_block(ref, dev, rows):
    return ref.at[pl.ds(pl.multiple_of(dev * rows, BF16_ROWS), rows)]


def _copy_ends(gather, src, land, me, peer_id):
    if gather:
        rows = src.shape[0]
        return src, _row_block(land, me, rows), _row_block(land, peer_id, rows), src, _row_block(land, me, rows)
    rows = src.shape[0] // N_DEV
    return (_row_block(src, peer_id, rows), land.at[me], land.at[peer_id], _row_block(src, me, rows), land.at[me])


def _land_shape(gather, s):
    return (N_DEV * s.shape[0], s.shape[1]) if gather else (N_DEV, s.shape[0] // N_DEV, s.shape[1])


SIBLING = 1
SAME_CORE_PEERS = (2, 4, 6)
RELAYS = len(SAME_CORE_PEERS)


def _copies_start(groups, gather, name, after=None, relayed=()):
    flat = [s for g in groups for s in g]
    n, ng = len(flat), len(groups)
    lands = [lax.empty(_land_shape(gather, s), s.dtype) for s in flat]
    n_in = 2 * n + (after is not None)

    def body(*refs):
        ins, lnd = refs[:n], refs[n:2 * n]
        sems = refs[n_in:n_in + 2 * ng]
        token = refs[-1]
        x, y, c = _me()
        me = 4 * x + 2 * y + c
        w = 0
        for gi, g in enumerate(groups):
            for k in range(len(g)):
                for p in ((SIBLING,) + SAME_CORE_PEERS if gi in relayed else range(1, N_DEV)):
                    peer, peer_id = _peer(x, y, c, p)
                    src, dst, _, _, _ = _copy_ends(gather, ins[w], lnd[w], me, peer_id)
                    pltpu.make_async_remote_copy(src, dst, sems[2 * gi].at[k * PEERS + p - 1],
                                                 sems[2 * gi + 1].at[k * PEERS + p - 1], device_id=peer,
                                                 device_id_type=MESH_IDS).start()
                w += 1
        token[...] = jnp.zeros_like(token)

    sem_shapes = []
    for g in groups:
        sem_shapes += [pltpu.SemaphoreType.DMA((len(g) * PEERS,))] * 2
    out = pl.pallas_call(
        body, name=name,
        out_shape=(*sem_shapes, *[pltpu.HBM(s.shape, s.dtype) for s in flat],
                   *[pltpu.HBM(l.shape, l.dtype) for l in lands], jax.ShapeDtypeStruct((8, LANES), F32)),
        in_specs=[HBM_SPEC] * (2 * n) + [pl.BlockSpec(memory_space=pl.ANY)] * (after is not None),
        out_specs=(*[SEM_SPEC] * (2 * ng), *[HBM_SPEC] * (2 * n), pl.BlockSpec(memory_space=pltpu.VMEM)),
        input_output_aliases={i: 2 * ng + i for i in range(2 * n)},
        compiler_params=SPLIT_COPY,
    )(*[_hbm(s) for s in flat], *[_hbm(l) for l in lands], *([after] if after is not None else []))
    sems, thru, token = out[:2 * ng], out[2 * ng:2 * ng + 2 * n], out[-1]
    res, w = [], 0
    for gi, g in enumerate(groups):
        res.append((sems[2 * gi], sems[2 * gi + 1], list(thru[w:w + len(g)]), list(thru[n + w:n + w + len(g)])))
        w += len(g)
    return res, token


def _copies_wait(started, gather, after, name):
    send, recv, srcs, lands = started
    n = len(srcs)
    after = list(after) if isinstance(after, (list, tuple)) else [after]

    own_shapes = [s.shape if gather else (s.shape[0] // N_DEV, s.shape[1]) for s in srcs]

    def body(*refs):
        ins, lnd = refs[:n], refs[n:2 * n]
        send_sems, recv_sems = refs[2 * n], refs[2 * n + 1]
        bounce, in_sems, out_sems = refs[-n - 2:-2], refs[-2], refs[-1]
        x, y, c = _me()
        me = 4 * x + 2 * y + c
        ends = [_copy_ends(gather, ins[w], lnd[w], me, me)[3:] for w in range(n)]
        loads = [pltpu.make_async_copy(ends[w][0], bounce[w], in_sems.at[w]) for w in range(n)]
        stores = [pltpu.make_async_copy(bounce[w], ends[w][1], out_sems.at[w]) for w in range(n)]
        for cp in loads:
            cp.start()
        for w in range(n):
            loads[w].wait()
            stores[w].start()
        for w in range(n):
            for p in range(1, N_DEV):
                peer, peer_id = _peer(x, y, c, p)
                src, _, arrival, _, _ = _copy_ends(gather, ins[w], lnd[w], me, peer_id)
                cp = pltpu.make_async_remote_copy(src, arrival, send_sems.at[w * PEERS + p - 1],
                                                  recv_sems.at[w * PEERS + p - 1], device_id=peer,
                                                  device_id_type=MESH_IDS)
                cp.wait_send()
                cp.wait_recv()
        for cp in stores:
            cp.wait()

    out = pl.pallas_call(
        body, name=name,
        out_shape=(*[pltpu.HBM(s.shape, s.dtype) for s in srcs], *[pltpu.HBM(l.shape, l.dtype) for l in lands]),
        in_specs=[HBM_SPEC] * (2 * n) + [SEM_SPEC, SEM_SPEC] + [pl.BlockSpec(memory_space=pl.ANY)] * len(after),
        out_specs=[HBM_SPEC] * (2 * n),
        input_output_aliases={i: i for i in range(2 * n)},
        scratch_shapes=[*[pltpu.VMEM(shape, s.dtype) for shape, s in zip(own_shapes, srcs)],
                        pltpu.SemaphoreType.DMA((n,)), pltpu.SemaphoreType.DMA((n,))],
        compiler_params=SPLIT_COPY,
    )(*srcs, *lands, send, recv, *after)
    return list(out[n:])


def _relay_to_sibling(started, name, after=None):
    send, recv, srcs, lands = started
    n = len(srcs)
    after = [] if after is None else [after]

    def body(*refs):
        ins, lnd = refs[:n], refs[n:2 * n]
        send_sems, recv_sems = refs[2 * n], refs[2 * n + 1]
        relay_send, relay_recv = refs[2 * n + 2 + len(after)], refs[2 * n + 3 + len(after)]
        x, y, c = _me()
        sibling, _ = _peer(x, y, c, SIBLING)
        for w in range(n):
            rows = ins[w].shape[0]
            for k, p in enumerate(SAME_CORE_PEERS):
                peer, peer_id = _peer(x, y, c, p)
                arrived = _row_block(lnd[w], peer_id, rows)
                first = pltpu.make_async_remote_copy(ins[w], arrived, send_sems.at[w * PEERS + p - 1],
                                                     recv_sems.at[w * PEERS + p - 1], device_id=peer,
                                                     device_id_type=MESH_IDS)
                first.wait_recv()
                pltpu.make_async_remote_copy(arrived, arrived, relay_send.at[w * RELAYS + k],
                                             relay_recv.at[w * RELAYS + k], device_id=sibling,
                                             device_id_type=MESH_IDS).start()
                first.wait_send()

    sems = pltpu.SemaphoreType.DMA((n * RELAYS,))
    out = pl.pallas_call(
        body, name=name,
        out_shape=(sems, sems, *[pltpu.HBM(s.shape, s.dtype) for s in srcs], *[pltpu.HBM(l.shape, l.dtype) for l in lands]),
        in_specs=[HBM_SPEC] * (2 * n) + [SEM_SPEC, SEM_SPEC] + [pl.BlockSpec(memory_space=pl.ANY)] * len(after),
        out_specs=(SEM_SPEC, SEM_SPEC, *[HBM_SPEC] * (2 * n)),
        input_output_aliases={i: 2 + i for i in range(2 * n)},
        compiler_params=SPLIT_COPY,
    )(*srcs, *lands, send, recv, *after)
    return send, recv, out[0], out[1], list(out[2:2 + n]), list(out[2 + n:])


def _relayed_wait(relayed, after, name):
    send, recv, relay_send, relay_recv, srcs, lands = relayed
    n = len(srcs)
    after = list(after) if isinstance(after, (list, tuple)) else [after]

    def body(*refs):
        ins, lnd = refs[:n], refs[n:2 * n]
        send_sems, recv_sems, relay_send_sems, relay_recv_sems = refs[2 * n:2 * n + 4]
        bounce, in_sems, out_sems = refs[-n - 2:-2], refs[-2], refs[-1]
        x, y, c = _me()
        me = 4 * x + 2 * y + c
        sibling, sibling_id = _peer(x, y, c, SIBLING)
        loads = [pltpu.make_async_copy(ins[w], bounce[w], in_sems.at[w]) for w in range(n)]
        stores = [pltpu.make_async_copy(bounce[w], _row_block(lnd[w], me, ins[w].shape[0]), out_sems.at[w])
                  for w in range(n)]
        for cp in loads:
            cp.start()
        for w in range(n):
            loads[w].wait()
            stores[w].start()
        for w in range(n):
            rows = ins[w].shape[0]
            direct = pltpu.make_async_remote_copy(ins[w], _row_block(lnd[w], sibling_id, rows),
                                                  send_sems.at[w * PEERS + SIBLING - 1],
                                                  recv_sems.at[w * PEERS + SIBLING - 1], device_id=sibling,
                                                  device_id_type=MESH_IDS)
            direct.wait_send()
            direct.wait_recv()
            for k, p in enumerate(SAME_CORE_PEERS):
                _, sent_id = _peer(x, y, c, p)
                _, got_id = _peer(x, y, c, p + SIBLING)
                relay = pltpu.make_async_remote_copy(_row_block(lnd[w], sent_id, rows), _row_block(lnd[w], got_id, rows),
                                                     relay_send_sems.at[w * RELAYS + k],
                                                     relay_recv_sems.at[w * RELAYS + k], device_id=sibling,
                                                     device_id_type=MESH_IDS)
                relay.wait_send()
                relay.wait_recv()
        for cp in stores:
            cp.wait()

    out = pl.pallas_call(
        body, name=name,
        out_shape=(*[pltpu.HBM(s.shape, s.dtype) for s in srcs], *[pltpu.HBM(l.shape, l.dtype) for l in lands]),
        in_specs=[HBM_SPEC] * (2 * n) + [SEM_SPEC] * 4 + [pl.BlockSpec(memory_space=pl.ANY)] * len(after),
        out_specs=[HBM_SPEC] * (2 * n),
        input_output_aliases={i: i for i in range(2 * n)},
        scratch_shapes=[*[pltpu.VMEM(s.shape, s.dtype) for s in srcs],
                        pltpu.SemaphoreType.DMA((n,)), pltpu.SemaphoreType.DMA((n,))],
        compiler_params=SPLIT_COPY,
    )(*srcs, *lands, send, recv, relay_send, relay_recv, *after)
    return list(out[n:])


def _adamw_update(w, g, m, v):
    nm = ADAM_B1 * m + (1.0 - ADAM_B1) * g
    nv = ADAM_B2 * v + (1.0 - ADAM_B2) * (g * g)
    m_hat = nm / (1.0 - ADAM_B1 ** ADAM_STEP)
    v_hat = nv / (1.0 - ADAM_B2 ** ADAM_STEP)
    return -ADAM_LR * (m_hat / (jnp.sqrt(v_hat) + ADAM_EPS) + ADAM_WD * w), nm, nv


SUM_ADAMW_COLS = 512


def _sum_adamw(parts, w, m, v, name):
    _, rows, d = parts.shape
    n = w.shape[0]
    tc = SUM_ADAMW_COLS

    def body(p_ref, w_ref, m_ref, v_ref, g_ref, d_ref, nm_ref, nv_ref):
        g = p_ref[0].astype(F32)
        for dev in range(1, N_DEV):
            g = g + p_ref[dev].astype(F32)
        g = g[:n]
        g_ref[...] = g
        d_ref[...], nm_ref[...], nv_ref[...] = _adamw_update(w_ref[...], g, m_ref[...], v_ref[...])

    spec = pl.BlockSpec((n, tc), lambda j: (0, j))
    shape = jax.ShapeDtypeStruct((n, d), F32)
    return pl.pallas_call(
        body, name=name, grid=(d // tc,),
        in_specs=[pl.BlockSpec((N_DEV, rows, tc), lambda j: (0, 0, j)), spec, spec, spec],
        out_specs=[spec] * 4, out_shape=[shape] * 4,
        compiler_params=_params("arbitrary"),
    )(parts, w, m, v)


def _pad_rows(a, rows):
    return jnp.pad(a, ((0, rows - a.shape[0]), (0, 0)))


def _row1(vec, width=D_MODEL):
    return jnp.pad(vec.reshape(1, -1), ((0, 0), (0, width - vec.shape[-1])))


COLUMN_SHARDED = ("ffn1_w_gate", "ffn1_w_up", "w_in", "ffn2_w_gate", "ffn2_w_up")
VEC_NAMES = ("ffn1_norm", "mix_norm", "ffn2_norm", "b_forget", "pool_scale", "q_norm", "k_norm", "out_norm_pool",
             "out_norm_attn")
VEC_ROWS = 16
LOSS_ROW = len(VEC_NAMES)


def _pack_vector_grads(parts, loss_part, name):
    names = [n for n in VEC_NAMES if n in parts]
    extra = [] if loss_part is None else [loss_part]

    def body(*refs):
        out_ref = refs[-1]
        out_ref[...] = jnp.zeros_like(out_ref)
        lane = lax.broadcasted_iota(jnp.int32, (1, LANES), 1)
        for n, ref in zip(names, refs):
            val = ref[...]
            if n in ("q_norm", "k_norm"):
                val = val[:, 0:LANES] + val[:, LANES:2 * LANES] + val[:, 2 * LANES:3 * LANES] + val[:, 3 * LANES:]
                val = jnp.where(lane < HEAD_DIM, val + pltpu.roll(val, HEAD_DIM, 1), 0.0)
            out_ref[pl.ds(VEC_NAMES.index(n), 1), pl.ds(0, val.shape[1])] = val
        if extra:
            out_ref[pl.ds(LOSS_ROW, 1), pl.ds(0, 1)] = refs[len(names)][...]

    vmem = pl.BlockSpec(memory_space=pltpu.VMEM)
    return pl.pallas_call(
        body, name=name, in_specs=[vmem] * (len(names) + len(extra)), out_specs=vmem,
        out_shape=jax.ShapeDtypeStruct((VEC_ROWS, D_MODEL), F32),
    )(*[parts[n] for n in names], *extra)


def _small_adamw(vec_all, pool_all, vec_params, pool_params):
    nv = len(vec_params)
    pool_rows = pool_params[0].shape[0]

    def body(*refs):
        vec_ref, pool_ref = refs[0], refs[1]
        ins = refs[2:2 + 3 * nv + 3]
        outs = refs[2 + 3 * nv + 3:-1]
        rows = refs[-1]
        total = vec_ref[pl.ds(0, VEC_ROWS), :]
        for dev in range(1, N_DEV):
            total = total + vec_ref[pl.ds(dev * VEC_ROWS, VEC_ROWS), :]
        rows[...] = total
        outs[4 * nv + 4][...] = rows[pl.ds(LOSS_ROW, 1), pl.ds(0, 1)]
        for i in range(nv):
            w_ref, m_ref, v_ref = ins[3 * i:3 * i + 3]
            g = rows[pl.ds(i, 1), pl.ds(0, w_ref.shape[1])]
            outs[4 * i][...] = g
            outs[4 * i + 1][...], outs[4 * i + 2][...], outs[4 * i + 3][...] = _adamw_update(
                w_ref[...], g, m_ref[...], v_ref[...])
        g = pool_ref[pl.ds(0, pool_rows), :].astype(F32)
        for dev in range(1, N_DEV):
            g = g + pool_ref[pl.ds(dev * pool_rows, pool_rows), :].astype(F32)
        w_ref, m_ref, v_ref = ins[3 * nv:]
        outs[4 * nv][...] = g
        outs[4 * nv + 1][...], outs[4 * nv + 2][...], outs[4 * nv + 3][...] = _adamw_update(
            w_ref[...], g, m_ref[...], v_ref[...])

    vmem = pl.BlockSpec(memory_space=pltpu.VMEM)
    flat = [a for trio in vec_params for a in trio] + list(pool_params)
    out_shape = []
    for trio in list(vec_params) + [pool_params]:
        out_shape += [jax.ShapeDtypeStruct(trio[0].shape, F32)] * 4
    out_shape.append(jax.ShapeDtypeStruct((1, 1), F32))
    return pl.pallas_call(
        body, name="adamw_small", in_specs=[vmem] * (2 + len(flat)), out_specs=[vmem] * len(out_shape),
        out_shape=out_shape, scratch_shapes=[pltpu.VMEM((VEC_ROWS, D_MODEL), F32)],
    )(vec_all, pool_all, *flat)


def kernel(x, ffn1_norm, ffn1_w_gate, ffn1_w_up, ffn1_w_down, mix_norm, w_in, b_forget, pool_w, pool_scale, q_norm, k_norm, out_norm_pool, out_norm_attn, w_out, ffn2_norm, ffn2_w_gate, ffn2_w_up, ffn2_w_down, loss_target, m_ffn1_norm, m_ffn1_w_gate, m_ffn1_w_up, m_ffn1_w_down, m_mix_norm, m_w_in, m_b_forget, m_pool_w, m_pool_scale, m_q_norm, m_k_norm, m_out_norm_pool, m_out_norm_attn, m_w_out, m_ffn2_norm, m_ffn2_w_gate, m_ffn2_w_up, m_ffn2_w_down, v_ffn1_norm, v_ffn1_w_gate, v_ffn1_w_up, v_ffn1_w_down, v_mix_norm, v_w_in, v_b_forget, v_pool_w, v_pool_scale, v_q_norm, v_k_norm, v_out_norm_pool, v_out_norm_attn, v_w_out, v_ffn2_norm, v_ffn2_w_gate, v_ffn2_w_up, v_ffn2_w_down):
    bsz, seq, d = x.shape
    t = bsz * seq
    x0 = x.reshape(t, d)
    target = loss_target.reshape(t, d)
    in_rows = -(-w_in.shape[1] // BF16_ROWS) * BF16_ROWS

    slabs = [s.astype(BF16) for s in (ffn1_w_gate.T, ffn1_w_up.T, ffn1_w_down, _pad_rows(w_in.T, in_rows), w_out,
                                       ffn2_w_gate.T, ffn2_w_up.T, ffn2_w_down)]
    gathers, started = _copies_start([slabs[0:2], slabs[2:3], slabs[3:4], slabs[4:5], slabs[5:8]], True, "gather_start",
                                     relayed=(0, 4))

    g1, gm, g2 = ffn1_norm.reshape(1, d), mix_norm.reshape(1, d), ffn2_norm.reshape(1, d)
    bf_row = _row1(b_forget, LANES)
    gq = jnp.tile(q_norm, N_HEADS).reshape(1, ATTN_WIDTH)
    gk = jnp.tile(k_norm, N_HEADS).reshape(1, ATTN_WIDTH)
    scale_row = pool_scale.reshape(1, POOL_WIDTH)
    gp, ga = out_norm_pool.reshape(1, POOL_WIDTH), out_norm_attn.reshape(1, ATTN_WIDTH)

    wg1, wu1 = _relayed_wait(_relay_to_sibling(gathers[0], "gather_relay_ffn1_up"), started, "gather_wait_ffn1_up")
    h1, sa1, sb1, s1 = _ffn_up(x0, g1, wg1, wu1, "ffn1_up")
    (wd1,) = _copies_wait(gathers[1], True, s1, "gather_wait_ffn1_down")
    (x1,) = _ffn_down(s1, wd1, x0, None, "ffn1_down")
    (win_g,) = _copies_wait(gathers[2], True, x1, "gather_wait_w_in")
    win_t = _repack_rows(win_g, in_rows, w_in.shape[1], N_DEV, "w_in_rows")
    hm, pv, q, k, v, f = _mix_in_fwd(x1, gm, win_t)
    pooled, mixed, y_pool = _pool_fwd(pv, pool_w, scale_row, gp, bsz, seq)
    qp, kp = _attn_prep_fwd(q, k, f, bf_row, gq, gk, bsz, seq)
    o, lse = _flash_fwd(qp, kp, v, bsz, seq)
    relayed_ffn2 = _relay_to_sibling(gathers[4], "gather_relay_ffn2", o)
    (wout,) = _copies_wait(gathers[3], True, [o, relayed_ffn2[4][0]], "gather_wait_w_out")
    ycat, x2 = _mix_out_fwd(o, y_pool, x1, ga, wout)
    wg2, wu2, wd2 = _relayed_wait(relayed_ffn2, x2, "gather_wait_ffn2")
    h2, sa2, sb2, s2 = _ffn_up(x2, g2, wg2, wu2, "ffn2_up")
    dx3, dyh2, loss_part = _ffn_down(s2, wd2, x2, target, "ffn2_down")

    da2, db2, dwg2, dwu2 = _ffn_bwd_act(dyh2, sa2, sb2, h2, wd2, "ffn2_bwd_act")
    (dwd2,) = _wgrad([s2], dyh2, da2, "ffn2_down_wgrad")
    (sent_ffn2,), tok = _copies_start([[dwg2, dwu2, dwd2]], False, "exchange_start_ffn2")
    dx2, dg2 = _ffn_bwd_dx(da2, db2, dx3, x2, g2, wg2, wu2, tok, "ffn2_bwd_dx")
    dwout, dy_pool, do, dga = _mix_out_bwd(dx2, o, ycat, ga, wout)
    (sent_out,), tok = _copies_start([[dwout]], False, "exchange_start_w_out")
    dqp, dkp, dv = _flash_bwd(qp, kp, v, o, do, lse, bsz, seq)
    dq, dk, df, dgq, dgk, dbf = _attn_prep_bwd(dqp, dkp, q, k, f, bf_row, gq, gk, bsz, seq)
    dpv, dpool_w, dscale, dgp = _pool_bwd(dy_pool, mixed, pooled, pool_w, scale_row, gp, bsz, seq)
    dwin, dx1, dyh1, dgm = _mix_in_bwd(dpv, dq, dk, dv, df, hm, x1, dx2, gm, win_t)
    dwin_blocks = _repack_rows(dwin, w_in.shape[1], in_rows, N_DEV, "w_in_grad_blocks")
    (sent_in,), tok = _copies_start([[dwin_blocks]], False, "exchange_start_w_in")
    (dwd1,) = _wgrad([s1], dyh1, tok, "ffn1_down_wgrad")
    (sent_down1,), tok = _copies_start([[dwd1]], False, "exchange_start_ffn1_down", after=tok)
    da1, db1, dwg1, dwu1 = _ffn_bwd_act(dyh1, sa1, sb1, h1, wd1, "ffn1_bwd_act")
    (sent_up1,), tok = _copies_start([[dwg1, dwu1]], False, "exchange_start_ffn1_up", after=tok)
    dx0, dg1 = _ffn_bwd_dx(da1, db1, dx1, x0, g1, wg1, wu1, tok, "ffn1_bwd_dx")

    pool_rows = POOL_GROUPS * POOL_GROUP_DIM
    packed = _pack_vector_grads(dict(ffn1_norm=dg1, mix_norm=dgm, ffn2_norm=dg2, b_forget=dbf, pool_scale=dscale,
                                     q_norm=dgq, k_norm=dgk, out_norm_pool=dgp, out_norm_attn=dga), loss_part,
                                "pack_vector_grads")
    pool_part = dpool_w.reshape(pool_rows, POOL_GROUP_DIM).astype(BF16)
    (sent_small,), tok = _copies_start([[packed, pool_part]], True, "small_grads_start")

    weights = dict(ffn1_norm=ffn1_norm, ffn1_w_gate=ffn1_w_gate, ffn1_w_up=ffn1_w_up, ffn1_w_down=ffn1_w_down,
                   mix_norm=mix_norm, w_in=w_in, b_forget=b_forget, pool_w=pool_w, pool_scale=pool_scale,
                   q_norm=q_norm, k_norm=k_norm, out_norm_pool=out_norm_pool, out_norm_attn=out_norm_attn,
                   w_out=w_out, ffn2_norm=ffn2_norm, ffn2_w_gate=ffn2_w_gate, ffn2_w_up=ffn2_w_up,
                   ffn2_w_down=ffn2_w_down)
    m_in = dict(ffn1_norm=m_ffn1_norm, ffn1_w_gate=m_ffn1_w_gate, ffn1_w_up=m_ffn1_w_up, ffn1_w_down=m_ffn1_w_down,
                mix_norm=m_mix_norm, w_in=m_w_in, b_forget=m_b_forget, pool_w=m_pool_w, pool_scale=m_pool_scale,
                q_norm=m_q_norm, k_norm=m_k_norm, out_norm_pool=m_out_norm_pool, out_norm_attn=m_out_norm_attn,
                w_out=m_w_out, ffn2_norm=m_ffn2_norm, ffn2_w_gate=m_ffn2_w_gate, ffn2_w_up=m_ffn2_w_up,
                ffn2_w_down=m_ffn2_w_down)
    v_in = dict(ffn1_norm=v_ffn1_norm, ffn1_w_gate=v_ffn1_w_gate, ffn1_w_up=v_ffn1_w_up, ffn1_w_down=v_ffn1_w_down,
                mix_norm=v_mix_norm, w_in=v_w_in, b_forget=v_b_forget, pool_w=v_pool_w, pool_scale=v_pool_scale,
                q_norm=v_q_norm, k_norm=v_k_norm, out_norm_pool=v_out_norm_pool, out_norm_attn=v_out_norm_attn,
                w_out=v_w_out, ffn2_norm=v_ffn2_norm, ffn2_w_gate=v_ffn2_w_gate, ffn2_w_up=v_ffn2_w_up,
                ffn2_w_down=v_ffn2_w_down)
    grads, delta, new_m, new_v = {}, {}, {}, {}
    after = [tok]
    plan = ((sent_ffn2, "ffn2", ("ffn2_w_gate", "ffn2_w_up", "ffn2_w_down")), (sent_out, "w_out", ("w_out",)),
            (sent_in, "w_in", ("w_in",)), (sent_down1, "ffn1_down", ("ffn1_w_down",)),
            (sent_up1, "ffn1_up", ("ffn1_w_gate", "ffn1_w_up")))
    for sent, tag, names in plan:
        parts = _copies_wait(sent, False, after, f"exchange_wait_{tag}")
        after = []
        for n, part in zip(names, parts):
            turn = (lambda a: a.T) if n in COLUMN_SHARDED else (lambda a: a)
            done = _sum_adamw(part, turn(weights[n]), turn(m_in[n]), turn(v_in[n]), f"adamw_{n}")
            grads[n], delta[n], new_m[n], new_v[n] = (turn(a) for a in done)
            after.append(done[3])
    vec_all, pool_all = _copies_wait(sent_small, True, after, "small_grads_wait")
    as_row = lambda a: a.reshape(1, -1)
    as_pool = lambda a: a.reshape(pool_rows, POOL_GROUP_DIM)
    small = _small_adamw(vec_all, pool_all,
                         [tuple(as_row(z[n]) for z in (weights, m_in, v_in)) for n in VEC_NAMES],
                         tuple(as_pool(z["pool_w"]) for z in (weights, m_in, v_in)))
    for i, n in enumerate(VEC_NAMES + ("pool_w",)):
        grads[n], delta[n], new_m[n], new_v[n] = (a.reshape(weights[n].shape) for a in small[4 * i:4 * i + 4])
    loss = small[-1].reshape(())

    order = ("ffn1_norm", "ffn1_w_gate", "ffn1_w_up", "ffn1_w_down", "mix_norm", "w_in", "b_forget", "pool_w",
             "pool_scale", "q_norm", "k_norm", "out_norm_pool", "out_norm_attn", "w_out", "ffn2_norm", "ffn2_w_gate",
             "ffn2_w_up", "ffn2_w_down")
    return (loss, dx0.reshape(bsz, seq, d), *[grads[n] for n in order], *[delta[n] for n in order],
            *[new_m[n] for n in order], *[new_v[n] for n in order])
```

```python
import jax
import jax.numpy as jnp
from jax import lax
from jax.experimental import pallas as pl
from jax.experimental.pallas import tpu as pltpu

F32 = jnp.float32
BF16 = jnp.bfloat16

EPS = 1e-6
D_MODEL = 1024
N_HEADS = 8
HEAD_DIM = 64
POOL_WIDTH = 512
ATTN_WIDTH = 512
POOL_GROUPS = 4
POOL_GROUP_DIM = 128
POOL_WINDOWS = (2, 4, 8, 16)
POOL_HALO = 16
MIX_PAD = POOL_WIDTH + 3 * ATTN_WIDTH + 128
N_DEV = 8
BF16_ROWS = 16
LANES = 128
VMEM_LIMIT = 56 * 1024 * 1024

ADAM_LR = 0.001
ADAM_B1 = 0.9
ADAM_B2 = 0.999
ADAM_EPS = 1e-08
ADAM_WD = 0.01
ADAM_STEP = 10


def _params(*sem):
    return pltpu.CompilerParams(dimension_semantics=sem, vmem_limit_bytes=VMEM_LIMIT)


def _dot(a, b):
    return jnp.dot(a, b, preferred_element_type=F32)


def _dot_nt(a, b):
    return lax.dot_general(a, b, (((1,), (1,)), ((), ())), preferred_element_type=F32)


def _dot_tn(a, b):
    return lax.dot_general(a, b, (((0,), (0,)), ((), ())), preferred_element_type=F32)


def _resident(shape):
    return pl.BlockSpec(shape, lambda *_: (0,) * len(shape), pipeline_mode=pl.Buffered(1))


def _rows(tm, width):
    return pl.BlockSpec((tm, width), lambda i: (i, 0))


ORDER_ONLY = pl.BlockSpec(memory_space=pl.ANY)


def _rms_scale(x):
    return lax.rsqrt(jnp.mean(x * x, axis=-1, keepdims=True) + EPS)


def _rms_bwd(dh, x, gain):
    r = _rms_scale(x)
    n = x * r
    dgain = jnp.sum(dh * n, axis=0, keepdims=True)
    dn = dh * gain
    dx = r * (dn - n * jnp.mean(dn * n, axis=-1, keepdims=True))
    return dx, dgain


def _split3(x):
    hi = x.astype(BF16)
    r1 = x - hi.astype(F32)
    mid = r1.astype(BF16)
    lo = (r1 - mid.astype(F32)).astype(BF16)
    return hi, mid, lo


FF_CHUNK = 256


def _swiglu_parts(a, b):
    sig = jax.nn.sigmoid(a)
    silu = a * sig
    return (b * (sig + silu * (1.0 - sig))).astype(BF16), silu.astype(BF16), (silu * b).astype(BF16)


def _ffn_up(x, gain, wg_t, wu_t, name):
    t, d = x.shape
    f = wg_t.shape[0]
    tm = 512

    def body(x_ref, g_ref, wg_ref, wu_ref, h_ref, sa_ref, sb_ref, s_ref):
        xv = x_ref[...]
        h = (xv * _rms_scale(xv) * g_ref[...]).astype(BF16)
        h_ref[...] = h
        for c in range(f // FF_CHUNK):
            sl = pl.ds(c * FF_CHUNK, FF_CHUNK)
            sa_ref[:, sl], sb_ref[:, sl], s_ref[:, sl] = _swiglu_parts(_dot_nt(h, wg_ref[sl, :]), _dot_nt(h, wu_ref[sl, :]))

    wide = jax.ShapeDtypeStruct((t, f), BF16)
    return pl.pallas_call(
        body, name=name, grid=(t // tm,),
        in_specs=[_rows(tm, d), _resident((1, d)), _resident((f, d)), _resident((f, d))],
        out_specs=[_rows(tm, d), _rows(tm, f), _rows(tm, f), _rows(tm, f)],
        out_shape=[jax.ShapeDtypeStruct((t, d), BF16), wide, wide, wide],
        compiler_params=_params("arbitrary"),
    )(x, gain, wg_t, wu_t)


def _ffn_down(s, wd, x, target, name):
    t, d = x.shape
    f = wd.shape[0]
    tm = 512
    with_loss = target is not None

    def body(*refs):
        if with_loss:
            s_ref, w_ref, x_ref, t_ref, dy_ref, dyh_ref, loss_ref = refs
        else:
            s_ref, w_ref, x_ref, y_ref = refs
        y = x_ref[...] + 0.5 * _dot(s_ref[...], w_ref[...])
        if with_loss:
            e = y - t_ref[...]
            dy = e * (1.0 / d)
            dy_ref[...] = dy
            dyh_ref[...] = (0.5 * dy).astype(BF16)

            @pl.when(pl.program_id(0) == 0)
            def _():
                loss_ref[...] = jnp.zeros_like(loss_ref)

            part = jnp.sum(jnp.sum(e * e, axis=0, keepdims=True), axis=1, keepdims=True)
            loss_ref[...] += part * (0.5 / d)
        else:
            y_ref[...] = y

    in_specs = [_rows(tm, f), _resident((f, d)), _rows(tm, d)]
    args = [s, wd, x]
    if with_loss:
        in_specs.append(_rows(tm, d))
        args.append(target)
        out_shape = [jax.ShapeDtypeStruct((t, d), F32), jax.ShapeDtypeStruct((t, d), BF16),
                     jax.ShapeDtypeStruct((1, 1), F32)]
        out_specs = [_rows(tm, d), _rows(tm, d), pl.BlockSpec((1, 1), lambda i: (0, 0))]
    else:
        out_shape = [jax.ShapeDtypeStruct((t, d), F32)]
        out_specs = [_rows(tm, d)]
    return pl.pallas_call(
        body, name=name, grid=(t // tm,), in_specs=in_specs, out_specs=out_specs, out_shape=out_shape,
        compiler_params=_params("arbitrary"),
    )(*args)


def _ffn_bwd_act(dyh, sa, sb, h, wd, after, name):
    t, d = dyh.shape
    f = wd.shape[0]
    tn = f // 2
    tk = 512
    nk = t // tk

    def body(dy_ref, sa_ref, sb_ref, h_ref, wd_ref, after_ref, da_ref, db_ref, dwg_ref, dwu_ref, acc_g, acc_u):
        k = pl.program_id(1)

        @pl.when(k == 0)
        def _():
            acc_g[...] = jnp.zeros_like(acc_g)
            acc_u[...] = jnp.zeros_like(acc_u)

        ds = _dot_nt(dy_ref[...], wd_ref[...])
        da = (ds * sa_ref[...].astype(F32)).astype(BF16)
        db = (ds * sb_ref[...].astype(F32)).astype(BF16)
        da_ref[...] = da
        db_ref[...] = db
        hv = h_ref[...]
        acc_g[...] += _dot_tn(da, hv)
        acc_u[...] += _dot_tn(db, hv)

        @pl.when(k == nk - 1)
        def _():
            dwg_ref[...] = acc_g[...].astype(BF16)
            dwu_ref[...] = acc_u[...].astype(BF16)

    tokens = pl.BlockSpec((tk, d), lambda j, k: (k, 0))
    wide = pl.BlockSpec((tk, tn), lambda j, k: (k, j))
    weight = pl.BlockSpec((tn, d), lambda j, k: (j, 0))
    return pl.pallas_call(
        body, name=name, grid=(f // tn, nk),
        in_specs=[tokens, wide, wide, tokens, weight, ORDER_ONLY],
        out_specs=[wide, wide, weight, weight],
        out_shape=[jax.ShapeDtypeStruct((t, f), BF16)] * 2 + [jax.ShapeDtypeStruct((f, d), BF16)] * 2,
        scratch_shapes=[pltpu.VMEM((tn, d), F32)] * 2,
        compiler_params=_params("arbitrary", "arbitrary"),
    )(dyh, sa, sb, h, wd, after)


def _ffn_bwd_dx(da, db, dy, x, gain, wg_t, wu_t, after, name):
    t, d = x.shape
    f = wg_t.shape[0]
    tm = 512

    def body(da_ref, db_ref, dy_ref, x_ref, g_ref, wg_ref, wu_ref, after_ref, dx_ref, dg_ref):
        dh = _dot(da_ref[...], wg_ref[...]) + _dot(db_ref[...], wu_ref[...])
        dx, dgain = _rms_bwd(dh, x_ref[...], g_ref[...])
        dx_ref[...] = dy_ref[...] + dx

        @pl.when(pl.program_id(0) == 0)
        def _():
            dg_ref[...] = jnp.zeros_like(dg_ref)

        dg_ref[...] += dgain

    return pl.pallas_call(
        body, name=name, grid=(t // tm,),
        in_specs=[_rows(tm, f), _rows(tm, f), _rows(tm, d), _rows(tm, d), _resident((1, d)), _resident((f, d)),
                  _resident((f, d)), ORDER_ONLY],
        out_specs=[_rows(tm, d), pl.BlockSpec((1, d), lambda i: (0, 0))],
        out_shape=[jax.ShapeDtypeStruct((t, d), F32), jax.ShapeDtypeStruct((1, d), F32)],
        compiler_params=_params("arbitrary"),
    )(da, db, dy, x, gain, wg_t, wu_t, after)


def _wgrad(lhs, b, after, name):
    t, n = lhs[0].shape
    d = b.shape[1]
    m = len(lhs)
    tn = n // 2 if n * d * m > (4 << 20) else n
    tk = 1024
    nk = t // tk

    def body(*refs):
        a_refs, b_ref, o_refs, accs = refs[:m], refs[m], refs[m + 2:2 * m + 2], refs[2 * m + 2:]
        k = pl.program_id(1)

        @pl.when(k == 0)
        def _():
            for acc in accs:
                acc[...] = jnp.zeros_like(acc)

        bv = b_ref[...]
        for a_ref, acc in zip(a_refs, accs):
            acc[...] += _dot_tn(a_ref[...], bv)

        @pl.when(k == nk - 1)
        def _():
            for o_ref, acc in zip(o_refs, accs):
                o_ref[...] = acc[...].astype(BF16)

    return pl.pallas_call(
        body, name=name, grid=(n // tn, nk),
        in_specs=[pl.BlockSpec((tk, tn), lambda j, k: (k, j))] * m + [pl.BlockSpec((tk, d), lambda j, k: (k, 0)),
                                                                       ORDER_ONLY],
        out_specs=[pl.BlockSpec((tn, d), lambda j, k: (j, 0))] * m,
        out_shape=[jax.ShapeDtypeStruct((n, d), BF16)] * m,
        scratch_shapes=[pltpu.VMEM((tn, d), F32)] * m,
        compiler_params=_params("arbitrary", "arbitrary"),
    )(*lhs, b, after)


def _repack_rows(a, rows_in, rows_out, blocks, name):
    total, d = a.shape
    real = min(rows_in, rows_out)

    def body(a_ref, o_ref, wide_in, wide_out):
        wide_in[...] = a_ref[...].astype(F32)
        wide_out[...] = jnp.zeros_like(wide_out)
        for j in range(blocks):
            wide_out[pl.ds(j * rows_out, real), :] = wide_in[pl.ds(j * rows_in, real), :]
        o_ref[...] = wide_out[...].astype(BF16)

    full = pl.BlockSpec((total, d), lambda i: (0, 0))
    return pl.pallas_call(
        body, name=name, grid=(1,), in_specs=[full], out_specs=full, out_shape=jax.ShapeDtypeStruct((total, d), BF16),
        scratch_shapes=[pltpu.VMEM((total, d), F32)] * 2,
        compiler_params=_params("arbitrary"),
    )(a)


def _mix_in_fwd(x, gain, w_in_t):
    t, d = x.shape
    tm = 1024
    pw, aw = POOL_WIDTH, ATTN_WIDTH

    def body(x_ref, g_ref, w_ref, hm_ref, pv_ref, q_ref, k_ref, v_ref, f_ref):
        xv = x_ref[...]
        hm = (xv * _rms_scale(xv) * g_ref[...]).astype(BF16)
        hm_ref[...] = hm
        pv_ref[...] = _dot_nt(hm, w_ref[pl.ds(0, pw), :])
        q_ref[...] = _dot_nt(hm, w_ref[pl.ds(pw, aw), :])
        k_ref[...] = _dot_nt(hm, w_ref[pl.ds(pw + aw, aw), :])
        v_ref[...] = _dot_nt(hm, w_ref[pl.ds(pw + 2 * aw, aw), :]).astype(BF16)
        f_ref[...] = _dot_nt(hm, w_ref[pl.ds(pw + 3 * aw, LANES), :])

    return pl.pallas_call(
        body, name="mix_in_fwd", grid=(t // tm,),
        in_specs=[_rows(tm, d), _resident((1, d)), _resident((MIX_PAD, d))],
        out_specs=[_rows(tm, d), _rows(tm, pw), _rows(tm, aw), _rows(tm, aw), _rows(tm, aw), _rows(tm, LANES)],
        out_shape=[jax.ShapeDtypeStruct((t, d), BF16), jax.ShapeDtypeStruct((t, pw), F32),
                   jax.ShapeDtypeStruct((t, aw), F32), jax.ShapeDtypeStruct((t, aw), F32),
                   jax.ShapeDtypeStruct((t, aw), BF16), jax.ShapeDtypeStruct((t, LANES), F32)],
        compiler_params=_params("arbitrary"),
    )(x, gain, w_in_t)


def _pool_fwd(pv, pool_w, pool_scale, gain, bsz, seq):
    ts = 512
    ns = seq // ts
    pw = POOL_WIDTH

    def body(pv_ref, w_ref, sc_ref, g_ref, pooled_ref, mixed_ref, y_ref, ext):
        s = pl.program_id(1)

        @pl.when(s == 0)
        def _():
            ext[pl.ds(0, POOL_HALO), :] = jnp.zeros((POOL_HALO, pw), F32)

        p = pv_ref[...]
        ext[pl.ds(POOL_HALO, ts), :] = p
        pos = s * ts + lax.broadcasted_iota(jnp.int32, (ts, 1), 0)
        parts = []
        for g, w in enumerate(POOL_WINDOWS):
            lanes = pl.ds(g * POOL_GROUP_DIM, POOL_GROUP_DIM)
            win = ext[pl.ds(POOL_HALO, ts), lanes]
            for i in range(1, w):
                win = win + ext[pl.ds(POOL_HALO - i, ts), lanes]
            cnt = jnp.minimum(pos + 1, w).astype(F32)
            pooled = (win / cnt - ext[pl.ds(POOL_HALO, ts), lanes]).astype(BF16)
            pooled_ref[:, lanes] = pooled
            parts.append(_dot(pooled, w_ref[g].astype(BF16)))
        mixed = jnp.concatenate(parts, axis=1)
        mixed_ref[...] = mixed
        pm = mixed * sc_ref[...]
        y_ref[...] = (pm * _rms_scale(pm) * g_ref[...]).astype(BF16)
        ext[pl.ds(0, POOL_HALO), :] = p[ts - POOL_HALO:, :]

    blk = pl.BlockSpec((ts, pw), lambda b, s: (b * ns + s, 0))
    t = bsz * seq
    return pl.pallas_call(
        body, name="pool_fwd", grid=(bsz, ns),
        in_specs=[blk, pl.BlockSpec((POOL_GROUPS, POOL_GROUP_DIM, POOL_GROUP_DIM), lambda b, s: (0, 0, 0)),
                  pl.BlockSpec((1, pw), lambda b, s: (0, 0)), pl.BlockSpec((1, pw), lambda b, s: (0, 0))],
        out_specs=[blk, blk, blk],
        out_shape=[jax.ShapeDtypeStruct((t, pw), BF16), jax.ShapeDtypeStruct((t, pw), F32),
                   jax.ShapeDtypeStruct((t, pw), BF16)],
        scratch_shapes=[pltpu.VMEM((POOL_HALO + ts, pw), F32)],
        compiler_params=_params("arbitrary", "arbitrary"),
    )(pv, pool_w, pool_scale, gain)


def _pool_bwd(dy, mixed, pooled, pool_w, pool_scale, gain, bsz, seq):
    ts = 512
    ns = seq // ts
    pw = POOL_WIDTH

    def body(dy_ref, mixed_ref, pooled_ref, w_ref, sc_ref, g_ref, dpv_ref, dw_ref, dsc_ref, dg_ref, ext):
        b = pl.program_id(0)
        sr = pl.program_id(1)
        s = ns - 1 - sr

        @pl.when(jnp.logical_and(b == 0, sr == 0))
        def _():
            dw_ref[...] = jnp.zeros_like(dw_ref)
            dsc_ref[...] = jnp.zeros_like(dsc_ref)
            dg_ref[...] = jnp.zeros_like(dg_ref)

        @pl.when(sr == 0)
        def _():
            ext[pl.ds(ts, POOL_HALO), :] = jnp.zeros((POOL_HALO, pw), F32)

        mixed = mixed_ref[...]
        sc = sc_ref[...]
        dpm, dgain = _rms_bwd(dy_ref[...], mixed * sc, g_ref[...])
        dg_ref[...] += dgain
        dsc_ref[...] += jnp.sum(dpm * mixed, axis=0, keepdims=True)
        dmixed = (dpm * sc).astype(BF16)
        pos = s * ts + lax.broadcasted_iota(jnp.int32, (ts, 1), 0)
        dpooled = []
        for g, w in enumerate(POOL_WINDOWS):
            lanes = pl.ds(g * POOL_GROUP_DIM, POOL_GROUP_DIM)
            dm = dmixed[:, g * POOL_GROUP_DIM:(g + 1) * POOL_GROUP_DIM]
            dw_ref[g] += _dot_tn(pooled_ref[:, lanes], dm)
            dp = _dot_nt(dm, w_ref[g].astype(BF16))
            dpooled.append(dp)
            cnt = jnp.minimum(pos + 1, w).astype(F32)
            ext[pl.ds(0, ts), lanes] = dp / cnt
        for g, w in enumerate(POOL_WINDOWS):
            lanes = pl.ds(g * POOL_GROUP_DIM, POOL_GROUP_DIM)
            win = ext[pl.ds(0, ts), lanes]
            for i in range(1, w):
                win = win + ext[pl.ds(i, ts), lanes]
            dpv_ref[:, lanes] = (win - dpooled[g]).astype(BF16)
        head = ext[pl.ds(0, POOL_HALO), :]
        ext[pl.ds(ts, POOL_HALO), :] = head

    blk = pl.BlockSpec((ts, pw), lambda b, s: (b * ns + (ns - 1 - s), 0))
    vec = pl.BlockSpec((1, pw), lambda b, s: (0, 0))
    wspec = pl.BlockSpec((POOL_GROUPS, POOL_GROUP_DIM, POOL_GROUP_DIM), lambda b, s: (0, 0, 0))
    t = bsz * seq
    return pl.pallas_call(
        body, name="pool_bwd", grid=(bsz, ns),
        in_specs=[blk, blk, blk, wspec, vec, vec],
        out_specs=[blk, wspec, vec, vec],
        out_shape=[jax.ShapeDtypeStruct((t, pw), BF16),
                   jax.ShapeDtypeStruct((POOL_GROUPS, POOL_GROUP_DIM, POOL_GROUP_DIM), F32),
                   jax.ShapeDtypeStruct((1, pw), F32), jax.ShapeDtypeStruct((1, pw), F32)],
        scratch_shapes=[pltpu.VMEM((ts + POOL_HALO, pw), F32)],
        compiler_params=_params("arbitrary", "arbitrary"),
    )(dy, mixed, pooled, pool_w, pool_scale, gain)


AUX_ONE = 64
AUX_F = 67

ATTN_PREP_ROWS = 512


def _seg_ones(width, seg):
    r = lax.broadcasted_iota(jnp.int32, (width, width), 0) // seg
    c = lax.broadcasted_iota(jnp.int32, (width, width), 1) // seg
    return (r == c).astype(BF16)


def _tri_ones(n, lower):
    r = lax.broadcasted_iota(jnp.int32, (n, n), 0)
    c = lax.broadcasted_iota(jnp.int32, (n, n), 1)
    return ((r >= c) if lower else (r <= c)).astype(BF16)


def _place_pieces(first_lane):
    r = lax.broadcasted_iota(jnp.int32, (3 * LANES, N_HEADS * LANES), 0)
    c = lax.broadcasted_iota(jnp.int32, (3 * LANES, N_HEADS * LANES), 1)
    piece, head = r // LANES, r % LANES
    return jnp.logical_and(head < N_HEADS, c == head * LANES + first_lane + piece).astype(BF16)


def _head_sums(x, seg_ones):
    return _dot(x.astype(BF16), seg_ones)


def _log_sigmoid(x):
    return jnp.minimum(x, 0.0) - jnp.log(1.0 + jnp.exp(-jnp.abs(x)))


def _attn_prep_fwd(q, k, f, b_forget, q_gain, k_gain, bsz, seq):
    ts = ATTN_PREP_ROWS
    ns = seq // ts
    aw = ATTN_WIDTH
    t = bsz * seq
    seg = _seg_ones(aw, HEAD_DIM)
    tri = _tri_ones(ts, True)

    def body(q_ref, k_ref, f_ref, bf_ref, gq_ref, gk_ref, seg_ref, tri_ref, place_ref, qp_ref, kp_ref, carry):
        s = pl.program_id(1)

        @pl.when(s == 0)
        def _():
            carry[...] = jnp.zeros_like(carry)

        logf = _log_sigmoid(f_ref[...] + bf_ref[...])
        hi, mid, lo = _split3(logf)
        tri_v = tri_ref[...]
        fc = _dot(tri_v, hi) + _dot(tri_v, mid) + _dot(tri_v, lo) + carry[pl.ds(0, 1), :]
        carry[pl.ds(0, 1), :] = fc[ts - 1:, :]
        pcs = jnp.concatenate(_split3(fc), axis=1)
        lane = lax.broadcasted_iota(jnp.int32, (1, LANES), 1)
        ones_q = jnp.logical_and(lane >= AUX_ONE, lane < AUX_ONE + 3).astype(F32)
        ones_k = jnp.logical_and(lane >= AUX_F, lane < AUX_F + 3).astype(F32)
        seg_v = seg_ref[...]
        placed = _dot(pcs, place_ref[...])

        def build(x_ref, g_ref, scale, out_ref, ones, for_keys):
            xv = x_ref[...]
            r = lax.rsqrt(_head_sums(xv * xv, seg_v) * (1.0 / HEAD_DIM) + EPS)
            xn = xv * r * g_ref[...] * scale
            for h in range(N_HEADS):
                pair = xn[:, (h // 2) * LANES:(h // 2 + 1) * LANES]
                feat = pair if h % 2 == 0 else pltpu.roll(pair, HEAD_DIM, 1)
                aux_h = placed[:, h * LANES:(h + 1) * LANES]
                if for_keys:
                    aux_h = -pltpu.roll(aux_h, LANES - (AUX_F - AUX_ONE), 1)
                out_ref[:, h * LANES:(h + 1) * LANES] = jnp.where(lane < HEAD_DIM, feat, aux_h + ones).astype(BF16)

        build(q_ref, gq_ref, 0.125, qp_ref, ones_q, False)
        build(k_ref, gk_ref, 1.0, kp_ref, ones_k, True)

    blk = pl.BlockSpec((ts, aw), lambda b, s: (b * ns + s, 0))
    fblk = pl.BlockSpec((ts, LANES), lambda b, s: (b * ns + s, 0))
    oblk = pl.BlockSpec((ts, N_HEADS * LANES), lambda b, s: (b * ns + s, 0))
    const = lambda shape: pl.BlockSpec(shape, lambda b, s: (0, 0))
    return pl.pallas_call(
        body, name="attn_prep_fwd", grid=(bsz, ns),
        in_specs=[blk, blk, fblk, const((1, LANES)), const((1, aw)), const((1, aw)), const((aw, aw)), const((ts, ts)),
                  const((3 * LANES, N_HEADS * LANES))],
        out_specs=[oblk, oblk],
        out_shape=[jax.ShapeDtypeStruct((t, N_HEADS * LANES), BF16)] * 2,
        scratch_shapes=[pltpu.VMEM((8, LANES), F32)],
        compiler_params=_params("arbitrary", "arbitrary"),
    )(q, k, f, b_forget, q_gain, k_gain, seg, tri, _place_pieces(AUX_F))


def _attn_prep_bwd(dqp, dkp, q, k, f, b_forget, q_gain, k_gain, bsz, seq):
    ts = ATTN_PREP_ROWS
    ns = seq // ts
    aw = ATTN_WIDTH
    t = bsz * seq
    seg = _seg_ones(aw, HEAD_DIM)
    tri = _tri_ones(ts, False)

    def body(dqp_ref, dkp_ref, q_ref, k_ref, f_ref, bf_ref, gq_ref, gk_ref, seg_ref, tri_ref,
             dq_ref, dk_ref, df_ref, dgq_ref, dgk_ref, dbf_ref, carry):
        b = pl.program_id(0)
        sr = pl.program_id(1)

        @pl.when(jnp.logical_and(b == 0, sr == 0))
        def _():
            dgq_ref[...] = jnp.zeros_like(dgq_ref)
            dgk_ref[...] = jnp.zeros_like(dgk_ref)
            dbf_ref[...] = jnp.zeros_like(dbf_ref)

        @pl.when(sr == 0)
        def _():
            carry[...] = jnp.zeros_like(carry)

        lane = lax.broadcasted_iota(jnp.int32, (1, LANES), 1)
        seg_v = seg_ref[...]

        def norm_bwd(dp_ref, x_ref, g_ref, scale, dx_ref, dgain_ref):
            parts = []
            for j in range(N_HEADS // 2):
                even = dp_ref[:, (2 * j) * LANES:(2 * j + 1) * LANES]
                odd = dp_ref[:, (2 * j + 1) * LANES:(2 * j + 2) * LANES]
                parts.append(jnp.where(lane < HEAD_DIM, even, pltpu.roll(odd, HEAD_DIM, 1)))
            dxn = jnp.concatenate(parts, axis=1) * scale
            xv = x_ref[...]
            r = lax.rsqrt(_head_sums(xv * xv, seg_v) * (1.0 / HEAD_DIM) + EPS)
            n = xv * r
            dgain_ref[...] += jnp.sum(dxn * n, axis=0, keepdims=True)
            dn = dxn * g_ref[...]
            m = _head_sums(dn * n, seg_v) * (1.0 / HEAD_DIM)
            dx_ref[...] = (r * (dn - n * m)).astype(BF16)

        norm_bwd(dqp_ref, q_ref, gq_ref, 0.125, dq_ref, dgq_ref)
        norm_bwd(dkp_ref, k_ref, gk_ref, 1.0, dk_ref, dgk_ref)

        dfc = jnp.zeros((ts, LANES), F32)
        for h in range(N_HEADS):
            cols = pl.ds(h * LANES, LANES)
            both = jnp.where(lane == AUX_F, dqp_ref[:, cols], 0.0) - jnp.where(lane == AUX_ONE, dkp_ref[:, cols], 0.0)
            dfc = jnp.where(lane == h, jnp.sum(both, axis=1, keepdims=True), dfc)
        hi, mid, lo = _split3(dfc)
        tri_v = tri_ref[...]
        dlogf = _dot(tri_v, hi) + _dot(tri_v, mid) + _dot(tri_v, lo) + carry[pl.ds(0, 1), :]
        carry[pl.ds(0, 1), :] = dlogf[0:1, :]
        df = jnp.where(lane < N_HEADS, dlogf * jax.nn.sigmoid(-(f_ref[...] + bf_ref[...])), 0.0)
        df_ref[...] = df.astype(BF16)
        dbf_ref[...] += jnp.sum(df, axis=0, keepdims=True)

    rev = lambda b, s: (b * ns + (ns - 1 - s), 0)
    blk = pl.BlockSpec((ts, aw), rev)
    fblk = pl.BlockSpec((ts, LANES), rev)
    pblk = pl.BlockSpec((ts, N_HEADS * LANES), rev)
    const = lambda shape: pl.BlockSpec(shape, lambda b, s: (0, 0))
    return pl.pallas_call(
        body, name="attn_prep_bwd", grid=(bsz, ns),
        in_specs=[pblk, pblk, blk, blk, fblk, const((1, LANES)), const((1, aw)), const((1, aw)), const((aw, aw)),
                  const((ts, ts))],
        out_specs=[blk, blk, fblk, const((1, aw)), const((1, aw)), const((1, LANES))],
        out_shape=[jax.ShapeDtypeStruct((t, aw), BF16), jax.ShapeDtypeStruct((t, aw), BF16),
                   jax.ShapeDtypeStruct((t, LANES), BF16), jax.ShapeDtypeStruct((1, aw), F32),
                   jax.ShapeDtypeStruct((1, aw), F32), jax.ShapeDtypeStruct((1, LANES), F32)],
        scratch_shapes=[pltpu.VMEM((8, LANES), F32)],
        compiler_params=_params("arbitrary", "arbitrary"),
    )(dqp, dkp, q, k, f, b_forget, q_gain, k_gain, seg, tri)


ATTN_BLOCK = 1024
HEAD_PAIRS = N_HEADS // 2


def _flash_fwd(qp, kp, v, bsz, seq):
    tq = ATTN_BLOCK
    half = tq // 2
    nq = seq // tq
    t = bsz * seq

    def body(q_ref, k_ref, v_ref, o_ref, lse_ref, m_sc, l_sc, acc_sc):
        i = pl.program_id(2)
        m_sc[...] = jnp.full(m_sc.shape, -jnp.inf, F32)
        l_sc[...] = jnp.zeros_like(l_sc)
        acc_sc[...] = jnp.zeros_like(acc_sc)
        lane = lax.broadcasted_iota(jnp.int32, (1, LANES), 1)
        low = lane < HEAD_DIM

        def tile(q0, qn, k_start, kn, k0=None):
            qs = pl.ds(q0, qn)
            ks = pl.ds(k_start, kn)
            vv = v_ref[ks, :]
            for h in range(2):
                mine = low if h == 0 else jnp.logical_not(low)
                cols = pl.ds(h * LANES, LANES)
                s = _dot_nt(q_ref[qs, cols], k_ref[ks, cols])
                if k0 is not None:
                    row = lax.broadcasted_iota(jnp.int32, (qn, kn), 0) + q0
                    col = lax.broadcasted_iota(jnp.int32, (qn, kn), 1) + k0
                    s = jnp.where(row >= col, s, -jnp.inf)
                m_prev = m_sc[h, qs, :]
                m_new = jnp.maximum(m_prev, jnp.max(s, axis=1, keepdims=True))
                p = jnp.exp(s - jnp.tile(m_new, (1, kn // LANES)))
                alpha = jnp.exp(m_prev - m_new)
                l_sc[h, qs, :] = alpha * l_sc[h, qs, :] + jnp.sum(p, axis=1, keepdims=True)
                m_sc[h, qs, :] = m_new
                pv = _dot(p.astype(BF16), jnp.where(mine, vv, jnp.zeros_like(vv)))
                acc_sc[qs, :] = acc_sc[qs, :] * jnp.where(mine, alpha, 1.0) + pv

        def below_diagonal(j, carry):
            tile(0, tq, pl.multiple_of(j * tq, tq), tq)
            return carry

        lax.fori_loop(0, i, below_diagonal, 0)
        diagonal = pl.multiple_of(i * tq, tq)
        tile(0, tq, diagonal, half, k0=0)
        tile(half, half, diagonal + half, half, k0=half)
        l = jnp.where(low, l_sc[0], l_sc[1])
        m = jnp.where(low, m_sc[0], m_sc[1])
        o_ref[...] = acc_sc[...] / l
        lse_ref[...] = m + jnp.log(l)

    qspec = pl.BlockSpec((tq, 2 * LANES), lambda b, hp, i: (b * nq + i, hp))
    kspec = pl.BlockSpec((seq, 2 * LANES), lambda b, hp, i: (b, hp))
    vspec = pl.BlockSpec((seq, LANES), lambda b, hp, i: (b, hp))
    ospec = pl.BlockSpec((tq, LANES), lambda b, hp, i: (b * nq + i, hp))
    return pl.pallas_call(
        body, name="flash_fwd", grid=(bsz, HEAD_PAIRS, nq),
        in_specs=[qspec, kspec, vspec], out_specs=[ospec, ospec],
        out_shape=[jax.ShapeDtypeStruct((t, ATTN_WIDTH), F32), jax.ShapeDtypeStruct((t, ATTN_WIDTH), F32)],
        scratch_shapes=[pltpu.VMEM((2, tq, LANES), F32), pltpu.VMEM((2, tq, LANES), F32), pltpu.VMEM((tq, LANES), F32)],
        compiler_params=_params("arbitrary", "arbitrary", "arbitrary"),
    )(qp, kp, v)


def _flash_bwd(qp, kp, v, o, do, lse, after, bsz, seq):
    tq = ATTN_BLOCK
    half = tq // 2
    nq = seq // tq
    t = bsz * seq

    def body(q_ref, k_ref, v_ref, o_ref, do_ref, lse_ref, after_ref, dq_ref, dk_ref, dv_ref, dk_acc, dv_acc):
        j = pl.program_id(2)

        @pl.when(j == 0)
        def _():
            dq_ref[...] = jnp.zeros_like(dq_ref)

        dk_acc[...] = jnp.zeros_like(dk_acc)
        dv_acc[...] = jnp.zeros_like(dv_acc)
        lane = lax.broadcasted_iota(jnp.int32, (1, LANES), 1)
        low = lane < HEAD_DIM

        def tile(q_start, qn, k0, kn, q0=None):
            rows = pl.ds(q_start, qn)
            ks = pl.ds(k0, kn)
            dov = do_ref[rows, :]
            dd = dov * o_ref[rows, :]
            dob = dov.astype(BF16)
            vv = v_ref[ks, :]
            lse_v = lse_ref[rows, :]
            for h in range(2):
                mine = low if h == 0 else jnp.logical_not(low)
                cols = pl.ds(h * LANES, LANES)
                qh = q_ref[rows, cols]
                kh = k_ref[ks, cols]
                s = _dot_nt(qh, kh)
                lse_h = jnp.where(mine, lse_v, pltpu.roll(lse_v, HEAD_DIM, 1))
                p = jnp.exp(s - jnp.tile(lse_h, (1, kn // LANES)))
                if q0 is not None:
                    row = lax.broadcasted_iota(jnp.int32, (qn, kn), 0) + q0
                    col = lax.broadcasted_iota(jnp.int32, (qn, kn), 1) + k0
                    p = jnp.where(row >= col, p, 0.0)
                delta = jnp.sum(jnp.where(mine, dd, 0.0), axis=1, keepdims=True)
                dp = _dot_nt(dob, jnp.where(mine, vv, jnp.zeros_like(vv)))
                ds = (p * (dp - delta)).astype(BF16)
                dv_acc[ks, :] += jnp.where(mine, _dot_tn(p.astype(BF16), dob), 0.0)
                dk_acc[ks, cols] += _dot_tn(ds, qh)
                dq_ref[rows, cols] += _dot(ds, kh)

        def above_diagonal(i, carry):
            tile(pl.multiple_of(i * tq, tq), tq, 0, tq)
            return carry

        diagonal = pl.multiple_of(j * tq, tq)
        tile(diagonal, tq, 0, half, q0=0)
        tile(diagonal + half, half, half, half, q0=half)
        lax.fori_loop(j + 1, nq, above_diagonal, 0)
        dk_ref[...] = dk_acc[...]
        dv_ref[...] = dv_acc[...].astype(BF16)

    qspec = pl.BlockSpec((seq, 2 * LANES), lambda b, hp, j: (b, hp))
    kspec = pl.BlockSpec((tq, 2 * LANES), lambda b, hp, j: (b * nq + j, hp))
    vspec = pl.BlockSpec((tq, LANES), lambda b, hp, j: (b * nq + j, hp))
    ospec = pl.BlockSpec((seq, LANES), lambda b, hp, j: (b, hp))
    return pl.pallas_call(
        body, name="flash_bwd", grid=(bsz, HEAD_PAIRS, nq),
        in_specs=[qspec, kspec, vspec, ospec, ospec, ospec, ORDER_ONLY], out_specs=[qspec, kspec, vspec],
        out_shape=[jax.ShapeDtypeStruct((t, N_HEADS * LANES), F32), jax.ShapeDtypeStruct((t, N_HEADS * LANES), F32),
                   jax.ShapeDtypeStruct((t, ATTN_WIDTH), BF16)],
        scratch_shapes=[pltpu.VMEM((tq, 2 * LANES), F32), pltpu.VMEM((tq, LANES), F32)],
        compiler_params=_params("arbitrary", "arbitrary", "arbitrary"),
    )(qp, kp, v, o, do, lse, after)


def _mix_out_fwd(o, y_pool, x, gain, w_out):
    t, d = x.shape
    tm = 1024
    pw, aw = POOL_WIDTH, ATTN_WIDTH

    def body(o_ref, yp_ref, x_ref, g_ref, w_ref, ycat_ref, y_ref):
        ov = o_ref[...]
        ya = (ov * _rms_scale(ov) * g_ref[...]).astype(BF16)
        ycat = jnp.concatenate([yp_ref[...], ya], axis=1)
        ycat_ref[...] = ycat
        y_ref[...] = x_ref[...] + _dot(ycat, w_ref[...])

    return pl.pallas_call(
        body, name="mix_out_fwd", grid=(t // tm,),
        in_specs=[_rows(tm, aw), _rows(tm, pw), _rows(tm, d), _resident((1, aw)), _resident((pw + aw, d))],
        out_specs=[_rows(tm, pw + aw), _rows(tm, d)],
        out_shape=[jax.ShapeDtypeStruct((t, pw + aw), BF16), jax.ShapeDtypeStruct((t, d), F32)],
        compiler_params=_params("arbitrary"),
    )(o, y_pool, x, gain, w_out)


def _mix_out_bwd(dx, o, ycat, gain, w_out):
    t, d = dx.shape
    tm = 1024
    nm = t // tm
    pw, aw = POOL_WIDTH, ATTN_WIDTH

    def body(dx_ref, o_ref, ycat_ref, g_ref, w_ref, dw_ref, dyp_ref, do_ref, dg_ref, acc):
        i = pl.program_id(0)

        @pl.when(i == 0)
        def _():
            dg_ref[...] = jnp.zeros_like(dg_ref)
            acc[...] = jnp.zeros_like(acc)

        dxb = dx_ref[...].astype(BF16)
        acc[...] += _dot_tn(ycat_ref[...], dxb)
        dyp_ref[...] = _dot_nt(dxb, w_ref[pl.ds(0, pw), :])
        dya = _dot_nt(dxb, w_ref[pl.ds(pw, aw), :])
        do, dgain = _rms_bwd(dya, o_ref[...], g_ref[...])
        do_ref[...] = do
        dg_ref[...] += dgain

        @pl.when(i == nm - 1)
        def _():
            dw_ref[...] = acc[...].astype(BF16)

    return pl.pallas_call(
        body, name="mix_out_bwd", grid=(nm,),
        in_specs=[_rows(tm, d), _rows(tm, aw), _rows(tm, pw + aw), _resident((1, aw)), _resident((pw + aw, d))],
        out_specs=[pl.BlockSpec((pw + aw, d), lambda i: (0, 0)), _rows(tm, pw), _rows(tm, aw),
                   pl.BlockSpec((1, aw), lambda i: (0, 0))],
        out_shape=[jax.ShapeDtypeStruct((pw + aw, d), BF16), jax.ShapeDtypeStruct((t, pw), F32),
                   jax.ShapeDtypeStruct((t, aw), F32), jax.ShapeDtypeStruct((1, aw), F32)],
        scratch_shapes=[pltpu.VMEM((pw + aw, d), F32)],
        compiler_params=_params("arbitrary"),
    )(dx, o, ycat, gain, w_out)


def _mix_in_bwd(dpv, dq, dk, dv, df, hm, x, dx_res, gain, w_in_t):
    t, d = x.shape
    tm = 512
    nm = t // tm
    pw, aw = POOL_WIDTH, ATTN_WIDTH

    def body(dpv_ref, dq_ref, dk_ref, dv_ref, df_ref, hm_ref, x_ref, dxr_ref, g_ref, w_ref, dw_ref, dx_ref, dxh_ref,
             dg_ref, acc):
        i = pl.program_id(0)

        @pl.when(i == 0)
        def _():
            dg_ref[...] = jnp.zeros_like(dg_ref)
            acc[...] = jnp.zeros_like(acc)

        dh = jnp.concatenate([dpv_ref[...], dq_ref[...], dk_ref[...], dv_ref[...], df_ref[...]], axis=1)
        acc[...] += _dot_tn(dh, hm_ref[...])
        dx, dgain = _rms_bwd(_dot(dh, w_ref[...]), x_ref[...], g_ref[...])
        dx = dxr_ref[...] + dx
        dx_ref[...] = dx
        dxh_ref[...] = (0.5 * dx).astype(BF16)
        dg_ref[...] += dgain

        @pl.when(i == nm - 1)
        def _():
            dw_ref[...] = acc[...].astype(BF16)

    return pl.pallas_call(
        body, name="mix_in_bwd", grid=(nm,),
        in_specs=[_rows(tm, pw), _rows(tm, aw), _rows(tm, aw), _rows(tm, aw), _rows(tm, LANES), _rows(tm, d),
                  _rows(tm, d), _rows(tm, d), _resident((1, d)), _resident((MIX_PAD, d))],
        out_specs=[pl.BlockSpec((MIX_PAD, d), lambda i: (0, 0)), _rows(tm, d), _rows(tm, d),
                   pl.BlockSpec((1, d), lambda i: (0, 0))],
        out_shape=[jax.ShapeDtypeStruct((MIX_PAD, d), BF16), jax.ShapeDtypeStruct((t, d), F32),
                   jax.ShapeDtypeStruct((t, d), BF16), jax.ShapeDtypeStruct((1, d), F32)],
        scratch_shapes=[pltpu.VMEM((MIX_PAD, d), F32)],
        compiler_params=_params("arbitrary"),
    )(dpv, dq, dk, dv, df, hm, x, dx_res, gain, w_in_t)


MESH_IDS = pl.DeviceIdType.MESH


def _me():
    return lax.axis_index("x"), lax.axis_index("y"), lax.axis_index("c")


def _peer(x, y, c, p):
    px = 1 - x if p & 4 else x
    py = 1 - y if p & 2 else y
    pc = 1 - c if p & 1 else c
    return (px, py, pc), 4 * px + 2 * py + pc


HBM_SPEC = pl.BlockSpec(memory_space=pltpu.HBM)
SEM_SPEC = pl.BlockSpec(memory_space=pltpu.SEMAPHORE)
SPLIT_COPY = pltpu.CompilerParams(has_side_effects=pltpu.SideEffectType.DATAFLOW_SIDE_EFFECTING)
PEERS = N_DEV - 1


def _hbm(a):
    return pltpu.with_memory_space_constraint(a, pltpu.HBM)


def _row_block(ref, dev, rows):
    return ref.at[pl.ds(pl.multiple_of(dev * rows, BF16_ROWS), rows)]


def _copy_ends(gather, src, land, me, peer_id):
    if gather:
        rows = src.shape[0]
        return src, _row_block(land, me, rows), _row_block(land, peer_id, rows), src, _row_block(land, me, rows)
    rows = src.shape[0] // N_DEV
    return (_row_block(src, peer_id, rows), land.at[me], land.at[peer_id], _row_block(src, me, rows), land.at[me])


def _land_shape(gather, s):
    return (N_DEV * s.shape[0], s.shape[1]) if gather else (N_DEV, s.shape[0] // N_DEV, s.shape[1])


SIBLING = 1
SAME_CORE_PEERS = (2, 4, 6)
RELAYS = len(SAME_CORE_PEERS)


def _copies_start(groups, gather, name, after=None, relayed=()):
    flat = [s for g in groups for s in g]
    n, ng = len(flat), len(groups)
    lands = [lax.empty(_land_shape(gather, s), s.dtype) for s in flat]
    n_in = 2 * n + (after is not None)

    def body(*refs):
        ins, lnd = refs[:n], refs[n:2 * n]
        sems = refs[n_in:n_in + 2 * ng]
        token = refs[-1]
        x, y, c = _me()
        me = 4 * x + 2 * y + c
        w = 0
        for gi, g in enumerate(groups):
            for k in range(len(g)):
                for p in ((SIBLING,) + SAME_CORE_PEERS if gi in relayed else range(1, N_DEV)):
                    peer, peer_id = _peer(x, y, c, p)
                    src, dst, _, _, _ = _copy_ends(gather, ins[w], lnd[w], me, peer_id)
                    pltpu.make_async_remote_copy(src, dst, sems[2 * gi].at[k * PEERS + p - 1],
                                                 sems[2 * gi + 1].at[k * PEERS + p - 1], device_id=peer,
                                                 device_id_type=MESH_IDS).start()
                w += 1
        token[...] = jnp.zeros_like(token)

    sem_shapes = []
    for g in groups:
        sem_shapes += [pltpu.SemaphoreType.DMA((len(g) * PEERS,))] * 2
    out = pl.pallas_call(
        body, name=name,
        out_shape=(*sem_shapes, *[pltpu.HBM(s.shape, s.dtype) for s in flat],
                   *[pltpu.HBM(l.shape, l.dtype) for l in lands], jax.ShapeDtypeStruct((8, LANES), F32)),
        in_specs=[HBM_SPEC] * (2 * n) + [pl.BlockSpec(memory_space=pl.ANY)] * (after is not None),
        out_specs=(*[SEM_SPEC] * (2 * ng), *[HBM_SPEC] * (2 * n), pl.BlockSpec(memory_space=pltpu.VMEM)),
        input_output_aliases={i: 2 * ng + i for i in range(2 * n)},
        compiler_params=SPLIT_COPY,
    )(*[_hbm(s) for s in flat], *[_hbm(l) for l in lands], *([after] if after is not None else []))
    sems, thru, token = out[:2 * ng], out[2 * ng:2 * ng + 2 * n], out[-1]
    res, w = [], 0
    for gi, g in enumerate(groups):
        res.append((sems[2 * gi], sems[2 * gi + 1], list(thru[w:w + len(g)]), list(thru[n + w:n + w + len(g)])))
        w += len(g)
    return res, token


def _copies_wait(started, gather, after, name):
    send, recv, srcs, lands = started
    n = len(srcs)
    after = list(after) if isinstance(after, (list, tuple)) else [after]

    own_shapes = [s.shape if gather else (s.shape[0] // N_DEV, s.shape[1]) for s in srcs]

    def body(*refs):
        ins, lnd = refs[:n], refs[n:2 * n]
        send_sems, recv_sems = refs[2 * n], refs[2 * n + 1]
        bounce, in_sems, out_sems = refs[-n - 2:-2], refs[-2], refs[-1]
        x, y, c = _me()
        me = 4 * x + 2 * y + c
        ends = [_copy_ends(gather, ins[w], lnd[w], me, me)[3:] for w in range(n)]
        loads = [pltpu.make_async_copy(ends[w][0], bounce[w], in_sems.at[w]) for w in range(n)]
        stores = [pltpu.make_async_copy(bounce[w], ends[w][1], out_sems.at[w]) for w in range(n)]
        for cp in loads:
            cp.start()
        for w in range(n):
            loads[w].wait()
            stores[w].start()
        for w in range(n):
            for p in range(1, N_DEV):
                peer, peer_id = _peer(x, y, c, p)
                src, _, arrival, _, _ = _copy_ends(gather, ins[w], lnd[w], me, peer_id)
                cp = pltpu.make_async_remote_copy(src, arrival, send_sems.at[w * PEERS + p - 1],
                                                  recv_sems.at[w * PEERS + p - 1], device_id=peer,
                                                  device_id_type=MESH_IDS)
                cp.wait_send()
                cp.wait_recv()
        for cp in stores:
            cp.wait()

    out = pl.pallas_call(
        body, name=name,
        out_shape=(*[pltpu.HBM(s.shape, s.dtype) for s in srcs], *[pltpu.HBM(l.shape, l.dtype) for l in lands]),
        in_specs=[HBM_SPEC] * (2 * n) + [SEM_SPEC, SEM_SPEC] + [pl.BlockSpec(memory_space=pl.ANY)] * len(after),
        out_specs=[HBM_SPEC] * (2 * n),
        input_output_aliases={i: i for i in range(2 * n)},
        scratch_shapes=[*[pltpu.VMEM(shape, s.dtype) for shape, s in zip(own_shapes, srcs)],
                        pltpu.SemaphoreType.DMA((n,)), pltpu.SemaphoreType.DMA((n,))],
        compiler_params=SPLIT_COPY,
    )(*srcs, *lands, send, recv, *after)
    return list(out[n:])


def _relay_to_sibling(started, name, after=None):
    send, recv, srcs, lands = started
    n = len(srcs)
    after = [] if after is None else [after]

    def body(*refs):
        ins, lnd = refs[:n], refs[n:2 * n]
        send_sems, recv_sems = refs[2 * n], refs[2 * n + 1]
        relay_send, relay_recv = refs[2 * n + 2 + len(after)], refs[2 * n + 3 + len(after)]
        x, y, c = _me()
        sibling, _ = _peer(x, y, c, SIBLING)
        for w in range(n):
            rows = ins[w].shape[0]
            for k, p in enumerate(SAME_CORE_PEERS):
                peer, peer_id = _peer(x, y, c, p)
                arrived = _row_block(lnd[w], peer_id, rows)
                first = pltpu.make_async_remote_copy(ins[w], arrived, send_sems.at[w * PEERS + p - 1],
                                                     recv_sems.at[w * PEERS + p - 1], device_id=peer,
                                                     device_id_type=MESH_IDS)
                first.wait_recv()
                pltpu.make_async_remote_copy(arrived, arrived, relay_send.at[w * RELAYS + k],
                                             relay_recv.at[w * RELAYS + k], device_id=sibling,
                                             device_id_type=MESH_IDS).start()
                first.wait_send()

    sems = pltpu.SemaphoreType.DMA((n * RELAYS,))
    out = pl.pallas_call(
        body, name=name,
        out_shape=(sems, sems, *[pltpu.HBM(s.shape, s.dtype) for s in srcs], *[pltpu.HBM(l.shape, l.dtype) for l in lands]),
        in_specs=[HBM_SPEC] * (2 * n) + [SEM_SPEC, SEM_SPEC] + [pl.BlockSpec(memory_space=pl.ANY)] * len(after),
        out_specs=(SEM_SPEC, SEM_SPEC, *[HBM_SPEC] * (2 * n)),
        input_output_aliases={i: 2 + i for i in range(2 * n)},
        compiler_params=SPLIT_COPY,
    )(*srcs, *lands, send, recv, *after)
    return send, recv, out[0], out[1], list(out[2:2 + n]), list(out[2 + n:])


def _relayed_wait(relayed, after, name):
    send, recv, relay_send, relay_recv, srcs, lands = relayed
    n = len(srcs)
    after = list(after) if isinstance(after, (list, tuple)) else [after]

    def body(*refs):
        ins, lnd = refs[:n], refs[n:2 * n]
        send_sems, recv_sems, relay_send_sems, relay_recv_sems = refs[2 * n:2 * n + 4]
        bounce, in_sems, out_sems = refs[-n - 2:-2], refs[-2], refs[-1]
        x, y, c = _me()
        me = 4 * x + 2 * y + c
        sibling, sibling_id = _peer(x, y, c, SIBLING)
        loads = [pltpu.make_async_copy(ins[w], bounce[w], in_sems.at[w]) for w in range(n)]
        stores = [pltpu.make_async_copy(bounce[w], _row_block(lnd[w], me, ins[w].shape[0]), out_sems.at[w])
                  for w in range(n)]
        for cp in loads:
            cp.start()
        for w in range(n):
            loads[w].wait()
            stores[w].start()
        for w in range(n):
            rows = ins[w].shape[0]
            direct = pltpu.make_async_remote_copy(ins[w], _row_block(lnd[w], sibling_id, rows),
                                                  send_sems.at[w * PEERS + SIBLING - 1],
                                                  recv_sems.at[w * PEERS + SIBLING - 1], device_id=sibling,
                                                  device_id_type=MESH_IDS)
            direct.wait_send()
            direct.wait_recv()
            for k, p in enumerate(SAME_CORE_PEERS):
                _, sent_id = _peer(x, y, c, p)
                _, got_id = _peer(x, y, c, p + SIBLING)
                relay = pltpu.make_async_remote_copy(_row_block(lnd[w], sent_id, rows), _row_block(lnd[w], got_id, rows),
                                                     relay_send_sems.at[w * RELAYS + k],
                                                     relay_recv_sems.at[w * RELAYS + k], device_id=sibling,
                                                     device_id_type=MESH_IDS)
                relay.wait_send()
                relay.wait_recv()
        for cp in stores:
            cp.wait()

    out = pl.pallas_call(
        body, name=name,
        out_shape=(*[pltpu.HBM(s.shape, s.dtype) for s in srcs], *[pltpu.HBM(l.shape, l.dtype) for l in lands]),
        in_specs=[HBM_SPEC] * (2 * n) + [SEM_SPEC] * 4 + [pl.BlockSpec(memory_space=pl.ANY)] * len(after),
        out_specs=[HBM_SPEC] * (2 * n),
        input_output_aliases={i: i for i in range(2 * n)},
        scratch_shapes=[*[pltpu.VMEM(s.shape, s.dtype) for s in srcs],
                        pltpu.SemaphoreType.DMA((n,)), pltpu.SemaphoreType.DMA((n,))],
        compiler_params=SPLIT_COPY,
    )(*srcs, *lands, send, recv, relay_send, relay_recv, *after)
    return list(out[n:])


def _adamw_update(w, g, m, v):
    nm = ADAM_B1 * m + (1.0 - ADAM_B1) * g
    nv = ADAM_B2 * v + (1.0 - ADAM_B2) * (g * g)
    m_hat = nm / (1.0 - ADAM_B1 ** ADAM_STEP)
    v_hat = nv / (1.0 - ADAM_B2 ** ADAM_STEP)
    return -ADAM_LR * (m_hat / (jnp.sqrt(v_hat) + ADAM_EPS) + ADAM_WD * w), nm, nv


SUM_ADAMW_COLS = 512


def _sum_adamw(parts, w, m, v, name):
    _, rows, d = parts.shape
    n = w.shape[0]
    tc = SUM_ADAMW_COLS

    def body(p_ref, w_ref, m_ref, v_ref, g_ref, d_ref, nm_ref, nv_ref):
        g = p_ref[0].astype(F32)
        for dev in range(1, N_DEV):
            g = g + p_ref[dev].astype(F32)
        g = g[:n]
        g_ref[...] = g
        d_ref[...], nm_ref[...], nv_ref[...] = _adamw_update(w_ref[...], g, m_ref[...], v_ref[...])

    spec = pl.BlockSpec((n, tc), lambda j: (0, j))
    shape = jax.ShapeDtypeStruct((n, d), F32)
    return pl.pallas_call(
        body, name=name, grid=(d // tc,),
        in_specs=[pl.BlockSpec((N_DEV, rows, tc), lambda j: (0, 0, j)), spec, spec, spec],
        out_specs=[spec] * 4, out_shape=[shape] * 4,
        compiler_params=_params("arbitrary"),
    )(parts, w, m, v)


def _pad_rows(a, rows):
    return jnp.pad(a, ((0, rows - a.shape[0]), (0, 0)))


def _row1(vec, width=D_MODEL):
    return jnp.pad(vec.reshape(1, -1), ((0, 0), (0, width - vec.shape[-1])))


COLUMN_SHARDED = ("ffn1_w_gate", "ffn1_w_up", "w_in", "ffn2_w_gate", "ffn2_w_up")
VEC_NAMES = ("ffn1_norm", "mix_norm", "ffn2_norm", "b_forget", "pool_scale", "q_norm", "k_norm", "out_norm_pool",
             "out_norm_attn")
VEC_ROWS = 16
LOSS_ROW = len(VEC_NAMES)


def _pack_vector_grads(parts, loss_part, name):
    names = [n for n in VEC_NAMES if n in parts]
    extra = [] if loss_part is None else [loss_part]

    def body(*refs):
        out_ref = refs[-1]
        out_ref[...] = jnp.zeros_like(out_ref)
        lane = lax.broadcasted_iota(jnp.int32, (1, LANES), 1)
        for n, ref in zip(names, refs):
            val = ref[...]
            if n in ("q_norm", "k_norm"):
                val = val[:, 0:LANES] + val[:, LANES:2 * LANES] + val[:, 2 * LANES:3 * LANES] + val[:, 3 * LANES:]
                val = jnp.where(lane < HEAD_DIM, val + pltpu.roll(val, HEAD_DIM, 1), 0.0)
            out_ref[pl.ds(VEC_NAMES.index(n), 1), pl.ds(0, val.shape[1])] = val
        if extra:
            out_ref[pl.ds(LOSS_ROW, 1), pl.ds(0, 1)] = refs[len(names)][...]

    vmem = pl.BlockSpec(memory_space=pltpu.VMEM)
    return pl.pallas_call(
        body, name=name, in_specs=[vmem] * (len(names) + len(extra)), out_specs=vmem,
        out_shape=jax.ShapeDtypeStruct((VEC_ROWS, D_MODEL), F32),
    )(*[parts[n] for n in names], *extra)


def _small_adamw(vec_all, pool_all, vec_params, pool_params):
    nv = len(vec_params)
    pool_rows = pool_params[0].shape[0]

    def body(*refs):
        vec_ref, pool_ref = refs[0], refs[1]
        ins = refs[2:2 + 3 * nv + 3]
        outs = refs[2 + 3 * nv + 3:-1]
        rows = refs[-1]
        total = vec_ref[pl.ds(0, VEC_ROWS), :]
        for dev in range(1, N_DEV):
            total = total + vec_ref[pl.ds(dev * VEC_ROWS, VEC_ROWS), :]
        rows[...] = total
        outs[4 * nv + 4][...] = rows[pl.ds(LOSS_ROW, 1), pl.ds(0, 1)]
        for i in range(nv):
            w_ref, m_ref, v_ref = ins[3 * i:3 * i + 3]
            g = rows[pl.ds(i, 1), pl.ds(0, w_ref.shape[1])]
            outs[4 * i][...] = g
            outs[4 * i + 1][...], outs[4 * i + 2][...], outs[4 * i + 3][...] = _adamw_update(
                w_ref[...], g, m_ref[...], v_ref[...])
        g = pool_ref[pl.ds(0, pool_rows), :].astype(F32)
        for dev in range(1, N_DEV):
            g = g + pool_ref[pl.ds(dev * pool_rows, pool_rows), :].astype(F32)
        w_ref, m_ref, v_ref = ins[3 * nv:]
        outs[4 * nv][...] = g
        outs[4 * nv + 1][...], outs[4 * nv + 2][...], outs[4 * nv + 3][...] = _adamw_update(
            w_ref[...], g, m_ref[...], v_ref[...])

    vmem = pl.BlockSpec(memory_space=pltpu.VMEM)
    flat = [a for trio in vec_params for a in trio] + list(pool_params)
    out_shape = []
    for trio in list(vec_params) + [pool_params]:
        out_shape += [jax.ShapeDtypeStruct(trio[0].shape, F32)] * 4
    out_shape.append(jax.ShapeDtypeStruct((1, 1), F32))
    return pl.pallas_call(
        body, name="adamw_small", in_specs=[vmem] * (2 + len(flat)), out_specs=[vmem] * len(out_shape),
        out_shape=out_shape, scratch_shapes=[pltpu.VMEM((VEC_ROWS, D_MODEL), F32)],
    )(vec_all, pool_all, *flat)


def kernel(x, ffn1_norm, ffn1_w_gate, ffn1_w_up, ffn1_w_down, mix_norm, w_in, b_forget, pool_w, pool_scale, q_norm, k_norm, out_norm_pool, out_norm_attn, w_out, ffn2_norm, ffn2_w_gate, ffn2_w_up, ffn2_w_down, loss_target, m_ffn1_norm, m_ffn1_w_gate, m_ffn1_w_up, m_ffn1_w_down, m_mix_norm, m_w_in, m_b_forget, m_pool_w, m_pool_scale, m_q_norm, m_k_norm, m_out_norm_pool, m_out_norm_attn, m_w_out, m_ffn2_norm, m_ffn2_w_gate, m_ffn2_w_up, m_ffn2_w_down, v_ffn1_norm, v_ffn1_w_gate, v_ffn1_w_up, v_ffn1_w_down, v_mix_norm, v_w_in, v_b_forget, v_pool_w, v_pool_scale, v_q_norm, v_k_norm, v_out_norm_pool, v_out_norm_attn, v_w_out, v_ffn2_norm, v_ffn2_w_gate, v_ffn2_w_up, v_ffn2_w_down):
    bsz, seq, d = x.shape
    t = bsz * seq
    x0 = x.reshape(t, d)
    target = loss_target.reshape(t, d)
    in_rows = -(-w_in.shape[1] // BF16_ROWS) * BF16_ROWS

    slabs = [s.astype(BF16) for s in (ffn1_w_gate.T, ffn1_w_up.T, ffn1_w_down, _pad_rows(w_in.T, in_rows), w_out,
                                       ffn2_w_gate.T, ffn2_w_up.T, ffn2_w_down)]
    gathers, started = _copies_start([slabs[0:2], slabs[2:3], slabs[3:4], slabs[4:5], slabs[5:8]], True, "gather_start",
                                     relayed=(0, 4))

    g1, gm, g2 = ffn1_norm.reshape(1, d), mix_norm.reshape(1, d), ffn2_norm.reshape(1, d)
    bf_row = _row1(b_forget, LANES)
    gq = jnp.tile(q_norm, N_HEADS).reshape(1, ATTN_WIDTH)
    gk = jnp.tile(k_norm, N_HEADS).reshape(1, ATTN_WIDTH)
    scale_row = pool_scale.reshape(1, POOL_WIDTH)
    gp, ga = out_norm_pool.reshape(1, POOL_WIDTH), out_norm_attn.reshape(1, ATTN_WIDTH)

    wg1, wu1 = _relayed_wait(_relay_to_sibling(gathers[0], "gather_relay_ffn1_up"), started, "gather_wait_ffn1_up")
    h1, sa1, sb1, s1 = _ffn_up(x0, g1, wg1, wu1, "ffn1_up")
    (wd1,) = _copies_wait(gathers[1], True, s1, "gather_wait_ffn1_down")
    (x1,) = _ffn_down(s1, wd1, x0, None, "ffn1_down")
    (win_g,) = _copies_wait(gathers[2], True, x1, "gather_wait_w_in")
    win_t = _repack_rows(win_g, in_rows, w_in.shape[1], N_DEV, "w_in_rows")
    hm, pv, q, k, v, f = _mix_in_fwd(x1, gm, win_t)
    pooled, mixed, y_pool = _pool_fwd(pv, pool_w, scale_row, gp, bsz, seq)
    qp, kp = _attn_prep_fwd(q, k, f, bf_row, gq, gk, bsz, seq)
    o, lse = _flash_fwd(qp, kp, v, bsz, seq)
    relayed_ffn2 = _relay_to_sibling(gathers[4], "gather_relay_ffn2", o)
    (wout,) = _copies_wait(gathers[3], True, [o, relayed_ffn2[4][0]], "gather_wait_w_out")
    ycat, x2 = _mix_out_fwd(o, y_pool, x1, ga, wout)
    wg2, wu2, wd2 = _relayed_wait(relayed_ffn2, x2, "gather_wait_ffn2")
    h2, sa2, sb2, s2 = _ffn_up(x2, g2, wg2, wu2, "ffn2_up")
    dx3, dyh2, loss_part = _ffn_down(s2, wd2, x2, target, "ffn2_down")

    da2, db2, dwg2, dwu2 = _ffn_bwd_act(dyh2, sa2, sb2, h2, wd2, dx3, "ffn2_bwd_act")
    (dwd2,) = _wgrad([s2], dyh2, da2, "ffn2_down_wgrad")
    (sent_ffn2,), tok = _copies_start([[dwg2, dwu2, dwd2]], False, "exchange_start_ffn2")
    dx2, dg2 = _ffn_bwd_dx(da2, db2, dx3, x2, g2, wg2, wu2, tok, "ffn2_bwd_dx")
    dwout, dy_pool, do, dga = _mix_out_bwd(dx2, o, ycat, ga, wout)
    (sent_out,), tok = _copies_start([[dwout]], False, "exchange_start_w_out")
    dqp, dkp, dv = _flash_bwd(qp, kp, v, o, do, lse, tok, bsz, seq)
    dq, dk, df, dgq, dgk, dbf = _attn_prep_bwd(dqp, dkp, q, k, f, bf_row, gq, gk, bsz, seq)
    dpv, dpool_w, dscale, dgp = _pool_bwd(dy_pool, mixed, pooled, pool_w, scale_row, gp, bsz, seq)
    dwin, dx1, dyh1, dgm = _mix_in_bwd(dpv, dq, dk, dv, df, hm, x1, dx2, gm, win_t)
    dwin_blocks = _repack_rows(dwin, w_in.shape[1], in_rows, N_DEV, "w_in_grad_blocks")
    (sent_in,), tok = _copies_start([[dwin_blocks]], False, "exchange_start_w_in")
    (dwd1,) = _wgrad([s1], dyh1, tok, "ffn1_down_wgrad")
    (sent_down1,), tok = _copies_start([[dwd1]], False, "exchange_start_ffn1_down", after=tok)
    da1, db1, dwg1, dwu1 = _ffn_bwd_act(dyh1, sa1, sb1, h1, wd1, tok, "ffn1_bwd_act")
    (sent_up1,), tok = _copies_start([[dwg1, dwu1]], False, "exchange_start_ffn1_up", after=tok)
    dx0, dg1 = _ffn_bwd_dx(da1, db1, dx1, x0, g1, wg1, wu1, tok, "ffn1_bwd_dx")

    pool_rows = POOL_GROUPS * POOL_GROUP_DIM
    packed = _pack_vector_grads(dict(ffn1_norm=dg1, mix_norm=dgm, ffn2_norm=dg2, b_forget=dbf, pool_scale=dscale,
                                     q_norm=dgq, k_norm=dgk, out_norm_pool=dgp, out_norm_attn=dga), loss_part,
                                "pack_vector_grads")
    pool_part = dpool_w.reshape(pool_rows, POOL_GROUP_DIM).astype(BF16)
    (sent_small,), tok = _copies_start([[packed, pool_part]], True, "small_grads_start")

    weights = dict(ffn1_norm=ffn1_norm, ffn1_w_gate=ffn1_w_gate, ffn1_w_up=ffn1_w_up, ffn1_w_down=ffn1_w_down,
                   mix_norm=mix_norm, w_in=w_in, b_forget=b_forget, pool_w=pool_w, pool_scale=pool_scale,
                   q_norm=q_norm, k_norm=k_norm, out_norm_pool=out_norm_pool, out_norm_attn=out_norm_attn,
                   w_out=w_out, ffn2_norm=ffn2_norm, ffn2_w_gate=ffn2_w_gate, ffn2_w_up=ffn2_w_up,
                   ffn2_w_down=ffn2_w_down)
    m_in = dict(ffn1_norm=m_ffn1_norm, ffn1_w_gate=m_ffn1_w_gate, ffn1_w_up=m_ffn1_w_up, ffn1_w_down=m_ffn1_w_down,
                mix_norm=m_mix_norm, w_in=m_w_in, b_forget=m_b_forget, pool_w=m_pool_w, pool_scale=m_pool_scale,
                q_norm=m_q_norm, k_norm=m_k_norm, out_norm_pool=m_out_norm_pool, out_norm_attn=m_out_norm_attn,
                w_out=m_w_out, ffn2_norm=m_ffn2_norm, ffn2_w_gate=m_ffn2_w_gate, ffn2_w_up=m_ffn2_w_up,
                ffn2_w_down=m_ffn2_w_down)
    v_in = dict(ffn1_norm=v_ffn1_norm, ffn1_w_gate=v_ffn1_w_gate, ffn1_w_up=v_ffn1_w_up, ffn1_w_down=v_ffn1_w_down,
                mix_norm=v_mix_norm, w_in=v_w_in, b_forget=v_b_forget, pool_w=v_pool_w, pool_scale=v_pool_scale,
                q_norm=v_q_norm, k_norm=v_k_norm, out_norm_pool=v_out_norm_pool, out_norm_attn=v_out_norm_attn,
                w_out=v_w_out, ffn2_norm=v_ffn2_norm, ffn2_w_gate=v_ffn2_w_gate, ffn2_w_up=v_ffn2_w_up,
                ffn2_w_down=v_ffn2_w_down)
    grads, delta, new_m, new_v = {}, {}, {}, {}
    after = [tok]
    plan = ((sent_ffn2, "ffn2", ("ffn2_w_gate", "ffn2_w_up", "ffn2_w_down")), (sent_out, "w_out", ("w_out",)),
            (sent_in, "w_in", ("w_in",)), (sent_down1, "ffn1_down", ("ffn1_w_down",)),
            (sent_up1, "ffn1_up", ("ffn1_w_gate", "ffn1_w_up")))
    for sent, tag, names in plan:
        parts = _copies_wait(sent, False, after, f"exchange_wait_{tag}")
        after = []
        for n, part in zip(names, parts):
            turn = (lambda a: a.T) if n in COLUMN_SHARDED else (lambda a: a)
            done = _sum_adamw(part, turn(weights[n]), turn(m_in[n]), turn(v_in[n]), f"adamw_{n}")
            grads[n], delta[n], new_m[n], new_v[n] = (turn(a) for a in done)
            after.append(done[3])
    vec_all, pool_all = _copies_wait(sent_small, True, after, "small_grads_wait")
    as_row = lambda a: a.reshape(1, -1)
    as_pool = lambda a: a.reshape(pool_rows, POOL_GROUP_DIM)
    small = _small_adamw(vec_all, pool_all,
                         [tuple(as_row(z[n]) for z in (weights, m_in, v_in)) for n in VEC_NAMES],
                         tuple(as_pool(z["pool_w"]) for z in (weights, m_in, v_in)))
    for i, n in enumerate(VEC_NAMES + ("pool_w",)):
        grads[n], delta[n], new_m[n], new_v[n] = (a.reshape(weights[n].shape) for a in small[4 * i:4 * i + 4])
    loss = small[-1].reshape(())

    order = ("ffn1_norm", "ffn1_w_gate", "ffn1_w_up", "ffn1_w_down", "mix_norm", "w_in", "b_forget", "pool_w",
             "pool_scale", "q_norm", "k_norm", "out_norm_pool", "out_norm_attn", "w_out", "ffn2_norm", "ffn2_w_gate",
             "ffn2_w_up", "ffn2_w_down")
    return (loss, dx0.reshape(bsz, seq, d), *[grads[n] for n in order], *[delta[n] for n in order],
            *[new_m[n] for n in order], *[new_v[n] for n in order])
```

```python
import jax
import jax.numpy as jnp
from jax import lax
from jax.experimental import pallas as pl
from jax.experimental.pallas import tpu as pltpu

F32 = jnp.float32
BF16 = jnp.bfloat16

EPS = 1e-6
D_MODEL = 1024
N_HEADS = 8
HEAD_DIM = 64
POOL_WIDTH = 512
ATTN_WIDTH = 512
POOL_GROUPS = 4
POOL_GROUP_DIM = 128
POOL_WINDOWS = (2, 4, 8, 16)
POOL_HALO = 16
MIX_PAD = POOL_WIDTH + 3 * ATTN_WIDTH + 128
N_DEV = 8
BF16_ROWS = 16
LANES = 128
VMEM_LIMIT = 56 * 1024 * 1024

ADAM_LR = 0.001
ADAM_B1 = 0.9
ADAM_B2 = 0.999
ADAM_EPS = 1e-08
ADAM_WD = 0.01
ADAM_STEP = 10


def _params(*sem):
    return pltpu.CompilerParams(dimension_semantics=sem, vmem_limit_bytes=VMEM_LIMIT)


def _dot(a, b):
    return jnp.dot(a, b, preferred_element_type=F32)


def _dot_nt(a, b):
    return lax.dot_general(a, b, (((1,), (1,)), ((), ())), preferred_element_type=F32)


def _dot_tn(a, b):
    return lax.dot_general(a, b, (((0,), (0,)), ((), ())), preferred_element_type=F32)


def _resident(shape):
    return pl.BlockSpec(shape, lambda *_: (0,) * len(shape), pipeline_mode=pl.Buffered(1))


def _rows(tm, width):
    return pl.BlockSpec((tm, width), lambda i: (i, 0))


ORDER_ONLY = pl.BlockSpec(memory_space=pl.ANY)


def _rms_scale(x):
    return lax.rsqrt(jnp.mean(x * x, axis=-1, keepdims=True) + EPS)


def _rms_bwd(dh, x, gain):
    r = _rms_scale(x)
    n = x * r
    dgain = jnp.sum(dh * n, axis=0, keepdims=True)
    dn = dh * gain
    dx = r * (dn - n * jnp.mean(dn * n, axis=-1, keepdims=True))
    return dx, dgain


def _split3(x):
    hi = x.astype(BF16)
    r1 = x - hi.astype(F32)
    mid = r1.astype(BF16)
    lo = (r1 - mid.astype(F32)).astype(BF16)
    return hi, mid, lo


FF_CHUNK = 256


def _swiglu_parts(a, b):
    sig = jax.nn.sigmoid(a)
    silu = a * sig
    return (b * (sig + silu * (1.0 - sig))).astype(BF16), silu.astype(BF16), (silu * b).astype(BF16)


def _ffn_up(x, gain, wg_t, wu_t, name):
    t, d = x.shape
    f = wg_t.shape[0]
    tm = 512

    def body(x_ref, g_ref, wg_ref, wu_ref, h_ref, sa_ref, sb_ref, s_ref):
        xv = x_ref[...]
        h = (xv * _rms_scale(xv) * g_ref[...]).astype(BF16)
        h_ref[...] = h
        for c in range(f // FF_CHUNK):
            sl = pl.ds(c * FF_CHUNK, FF_CHUNK)
            sa_ref[:, sl], sb_ref[:, sl], s_ref[:, sl] = _swiglu_parts(_dot_nt(h, wg_ref[sl, :]), _dot_nt(h, wu_ref[sl, :]))

    wide = jax.ShapeDtypeStruct((t, f), BF16)
    return pl.pallas_call(
        body, name=name, grid=(t // tm,),
        in_specs=[_rows(tm, d), _resident((1, d)), _resident((f, d)), _resident((f, d))],
        out_specs=[_rows(tm, d), _rows(tm, f), _rows(tm, f), _rows(tm, f)],
        out_shape=[jax.ShapeDtypeStruct((t, d), BF16), wide, wide, wide],
        compiler_params=_params("arbitrary"),
    )(x, gain, wg_t, wu_t)


def _ffn_down(s, wd, x, target, name):
    t, d = x.shape
    f = wd.shape[0]
    tm = 512
    with_loss = target is not None

    def body(*refs):
        if with_loss:
            s_ref, w_ref, x_ref, t_ref, dy_ref, dyh_ref, loss_ref = refs
        else:
            s_ref, w_ref, x_ref, y_ref = refs
        y = x_ref[...] + 0.5 * _dot(s_ref[...], w_ref[...])
        if with_loss:
            e = y - t_ref[...]
            dy = e * (1.0 / d)
            dy_ref[...] = dy
            dyh_ref[...] = (0.5 * dy).astype(BF16)

            @pl.when(pl.program_id(0) == 0)
            def _():
                loss_ref[...] = jnp.zeros_like(loss_ref)

            part = jnp.sum(jnp.sum(e * e, axis=0, keepdims=True), axis=1, keepdims=True)
            loss_ref[...] += part * (0.5 / d)
        else:
            y_ref[...] = y

    in_specs = [_rows(tm, f), _resident((f, d)), _rows(tm, d)]
    args = [s, wd, x]
    if with_loss:
        in_specs.append(_rows(tm, d))
        args.append(target)
        out_shape = [jax.ShapeDtypeStruct((t, d), F32), jax.ShapeDtypeStruct((t, d), BF16),
                     jax.ShapeDtypeStruct((1, 1), F32)]
        out_specs = [_rows(tm, d), _rows(tm, d), pl.BlockSpec((1, 1), lambda i: (0, 0))]
    else:
        out_shape = [jax.ShapeDtypeStruct((t, d), F32)]
        out_specs = [_rows(tm, d)]
    return pl.pallas_call(
        body, name=name, grid=(t // tm,), in_specs=in_specs, out_specs=out_specs, out_shape=out_shape,
        compiler_params=_params("arbitrary"),
    )(*args)


def _ffn_bwd_act(dyh, sa, sb, h, wd, after, name):
    t, d = dyh.shape
    f = wd.shape[0]
    tn = f // 2
    tk = 512
    nk = t // tk

    def body(dy_ref, sa_ref, sb_ref, h_ref, wd_ref, after_ref, da_ref, db_ref, dwg_ref, dwu_ref, acc_g, acc_u):
        k = pl.program_id(1)

        @pl.when(k == 0)
        def _():
            acc_g[...] = jnp.zeros_like(acc_g)
            acc_u[...] = jnp.zeros_like(acc_u)

        ds = _dot_nt(dy_ref[...], wd_ref[...])
        da = (ds * sa_ref[...].astype(F32)).astype(BF16)
        db = (ds * sb_ref[...].astype(F32)).astype(BF16)
        da_ref[...] = da
        db_ref[...] = db
        hv = h_ref[...]
        acc_g[...] += _dot_tn(da, hv)
        acc_u[...] += _dot_tn(db, hv)

        @pl.when(k == nk - 1)
        def _():
            dwg_ref[...] = acc_g[...].astype(BF16)
            dwu_ref[...] = acc_u[...].astype(BF16)

    tokens = pl.BlockSpec((tk, d), lambda j, k: (k, 0))
    wide = pl.BlockSpec((tk, tn), lambda j, k: (k, j))
    weight = pl.BlockSpec((tn, d), lambda j, k: (j, 0))
    return pl.pallas_call(
        body, name=name, grid=(f // tn, nk),
        in_specs=[tokens, wide, wide, tokens, weight, ORDER_ONLY],
        out_specs=[wide, wide, weight, weight],
        out_shape=[jax.ShapeDtypeStruct((t, f), BF16)] * 2 + [jax.ShapeDtypeStruct((f, d), BF16)] * 2,
        scratch_shapes=[pltpu.VMEM((tn, d), F32)] * 2,
        compiler_params=_params("arbitrary", "arbitrary"),
    )(dyh, sa, sb, h, wd, after)


def _ffn_bwd_dx(da, db, dy, x, gain, wg_t, wu_t, after, name):
    t, d = x.shape
    f = wg_t.shape[0]
    tm = 512

    def body(da_ref, db_ref, dy_ref, x_ref, g_ref, wg_ref, wu_ref, after_ref, dx_ref, dg_ref):
        dh = _dot(da_ref[...], wg_ref[...]) + _dot(db_ref[...], wu_ref[...])
        dx, dgain = _rms_bwd(dh, x_ref[...], g_ref[...])
        dx_ref[...] = dy_ref[...] + dx

        @pl.when(pl.program_id(0) == 0)
        def _():
            dg_ref[...] = jnp.zeros_like(dg_ref)

        dg_ref[...] += dgain

    return pl.pallas_call(
        body, name=name, grid=(t // tm,),
        in_specs=[_rows(tm, f), _rows(tm, f), _rows(tm, d), _rows(tm, d), _resident((1, d)), _resident((f, d)),
                  _resident((f, d)), ORDER_ONLY],
        out_specs=[_rows(tm, d), pl.BlockSpec((1, d), lambda i: (0, 0))],
        out_shape=[jax.ShapeDtypeStruct((t, d), F32), jax.ShapeDtypeStruct((1, d), F32)],
        compiler_params=_params("arbitrary"),
    )(da, db, dy, x, gain, wg_t, wu_t, after)


def _wgrad(lhs, b, after, name):
    t, n = lhs[0].shape
    d = b.shape[1]
    m = len(lhs)
    tn = n // 2 if n * d * m > (4 << 20) else n
    tk = 1024
    nk = t // tk

    def body(*refs):
        a_refs, b_ref, o_refs, accs = refs[:m], refs[m], refs[m + 2:2 * m + 2], refs[2 * m + 2:]
        k = pl.program_id(1)

        @pl.when(k == 0)
        def _():
            for acc in accs:
                acc[...] = jnp.zeros_like(acc)

        bv = b_ref[...]
        for a_ref, acc in zip(a_refs, accs):
            acc[...] += _dot_tn(a_ref[...], bv)

        @pl.when(k == nk - 1)
        def _():
            for o_ref, acc in zip(o_refs, accs):
                o_ref[...] = acc[...].astype(BF16)

    return pl.pallas_call(
        body, name=name, grid=(n // tn, nk),
        in_specs=[pl.BlockSpec((tk, tn), lambda j, k: (k, j))] * m + [pl.BlockSpec((tk, d), lambda j, k: (k, 0)),
                                                                       ORDER_ONLY],
        out_specs=[pl.BlockSpec((tn, d), lambda j, k: (j, 0))] * m,
        out_shape=[jax.ShapeDtypeStruct((n, d), BF16)] * m,
        scratch_shapes=[pltpu.VMEM((tn, d), F32)] * m,
        compiler_params=_params("arbitrary", "arbitrary"),
    )(*lhs, b, after)


def _repack_rows(a, rows_in, rows_out, blocks, name):
    total, d = a.shape
    real = min(rows_in, rows_out)

    def body(a_ref, o_ref, wide_in, wide_out):
        wide_in[...] = a_ref[...].astype(F32)
        wide_out[...] = jnp.zeros_like(wide_out)
        for j in range(blocks):
            wide_out[pl.ds(j * rows_out, real), :] = wide_in[pl.ds(j * rows_in, real), :]
        o_ref[...] = wide_out[...].astype(BF16)

    full = pl.BlockSpec((total, d), lambda i: (0, 0))
    return pl.pallas_call(
        body, name=name, grid=(1,), in_specs=[full], out_specs=full, out_shape=jax.ShapeDtypeStruct((total, d), BF16),
        scratch_shapes=[pltpu.VMEM((total, d), F32)] * 2,
        compiler_params=_params("arbitrary"),
    )(a)


def _mix_in_fwd(x, gain, w_in_t):
    t, d = x.shape
    tm = 1024
    pw, aw = POOL_WIDTH, ATTN_WIDTH

    def body(x_ref, g_ref, w_ref, hm_ref, pv_ref, q_ref, k_ref, v_ref, f_ref):
        xv = x_ref[...]
        hm = (xv * _rms_scale(xv) * g_ref[...]).astype(BF16)
        hm_ref[...] = hm
        pv_ref[...] = _dot_nt(hm, w_ref[pl.ds(0, pw), :])
        q_ref[...] = _dot_nt(hm, w_ref[pl.ds(pw, aw), :])
        k_ref[...] = _dot_nt(hm, w_ref[pl.ds(pw + aw, aw), :])
        v_ref[...] = _dot_nt(hm, w_ref[pl.ds(pw + 2 * aw, aw), :]).astype(BF16)
        f_ref[...] = _dot_nt(hm, w_ref[pl.ds(pw + 3 * aw, LANES), :])

    return pl.pallas_call(
        body, name="mix_in_fwd", grid=(t // tm,),
        in_specs=[_rows(tm, d), _resident((1, d)), _resident((MIX_PAD, d))],
        out_specs=[_rows(tm, d), _rows(tm, pw), _rows(tm, aw), _rows(tm, aw), _rows(tm, aw), _rows(tm, LANES)],
        out_shape=[jax.ShapeDtypeStruct((t, d), BF16), jax.ShapeDtypeStruct((t, pw), F32),
                   jax.ShapeDtypeStruct((t, aw), F32), jax.ShapeDtypeStruct((t, aw), F32),
                   jax.ShapeDtypeStruct((t, aw), BF16), jax.ShapeDtypeStruct((t, LANES), F32)],
        compiler_params=_params("arbitrary"),
    )(x, gain, w_in_t)


def _pool_fwd(pv, pool_w, pool_scale, gain, bsz, seq):
    ts = 512
    ns = seq // ts
    pw = POOL_WIDTH

    def body(pv_ref, w_ref, sc_ref, g_ref, pooled_ref, mixed_ref, y_ref, ext):
        s = pl.program_id(1)

        @pl.when(s == 0)
        def _():
            ext[pl.ds(0, POOL_HALO), :] = jnp.zeros((POOL_HALO, pw), F32)

        p = pv_ref[...]
        ext[pl.ds(POOL_HALO, ts), :] = p
        pos = s * ts + lax.broadcasted_iota(jnp.int32, (ts, 1), 0)
        parts = []
        for g, w in enumerate(POOL_WINDOWS):
            lanes = pl.ds(g * POOL_GROUP_DIM, POOL_GROUP_DIM)
            win = ext[pl.ds(POOL_HALO, ts), lanes]
            for i in range(1, w):
                win = win + ext[pl.ds(POOL_HALO - i, ts), lanes]
            cnt = jnp.minimum(pos + 1, w).astype(F32)
            pooled = (win / cnt - ext[pl.ds(POOL_HALO, ts), lanes]).astype(BF16)
            pooled_ref[:, lanes] = pooled
            parts.append(_dot(pooled, w_ref[g].astype(BF16)))
        mixed = jnp.concatenate(parts, axis=1)
        mixed_ref[...] = mixed
        pm = mixed * sc_ref[...]
        y_ref[...] = (pm * _rms_scale(pm) * g_ref[...]).astype(BF16)
        ext[pl.ds(0, POOL_HALO), :] = p[ts - POOL_HALO:, :]

    blk = pl.BlockSpec((ts, pw), lambda b, s: (b * ns + s, 0))
    t = bsz * seq
    return pl.pallas_call(
        body, name="pool_fwd", grid=(bsz, ns),
        in_specs=[blk, pl.BlockSpec((POOL_GROUPS, POOL_GROUP_DIM, POOL_GROUP_DIM), lambda b, s: (0, 0, 0)),
                  pl.BlockSpec((1, pw), lambda b, s: (0, 0)), pl.BlockSpec((1, pw), lambda b, s: (0, 0))],
        out_specs=[blk, blk, blk],
        out_shape=[jax.ShapeDtypeStruct((t, pw), BF16), jax.ShapeDtypeStruct((t, pw), F32),
                   jax.ShapeDtypeStruct((t, pw), BF16)],
        scratch_shapes=[pltpu.VMEM((POOL_HALO + ts, pw), F32)],
        compiler_params=_params("arbitrary", "arbitrary"),
    )(pv, pool_w, pool_scale, gain)


def _pool_bwd(dy, mixed, pooled, pool_w, pool_scale, gain, bsz, seq):
    ts = 512
    ns = seq // ts
    pw = POOL_WIDTH

    def body(dy_ref, mixed_ref, pooled_ref, w_ref, sc_ref, g_ref, dpv_ref, dw_ref, dsc_ref, dg_ref, ext):
        b = pl.program_id(0)
        sr = pl.program_id(1)
        s = ns - 1 - sr

        @pl.when(jnp.logical_and(b == 0, sr == 0))
        def _():
            dw_ref[...] = jnp.zeros_like(dw_ref)
            dsc_ref[...] = jnp.zeros_like(dsc_ref)
            dg_ref[...] = jnp.zeros_like(dg_ref)

        @pl.when(sr == 0)
        def _():
            ext[pl.ds(ts, POOL_HALO), :] = jnp.zeros((POOL_HALO, pw), F32)

        mixed = mixed_ref[...]
        sc = sc_ref[...]
        dpm, dgain = _rms_bwd(dy_ref[...], mixed * sc, g_ref[...])
        dg_ref[...] += dgain
        dsc_ref[...] += jnp.sum(dpm * mixed, axis=0, keepdims=True)
        dmixed = (dpm * sc).astype(BF16)
        pos = s * ts + lax.broadcasted_iota(jnp.int32, (ts, 1), 0)
        dpooled = []
        for g, w in enumerate(POOL_WINDOWS):
            lanes = pl.ds(g * POOL_GROUP_DIM, POOL_GROUP_DIM)
            dm = dmixed[:, g * POOL_GROUP_DIM:(g + 1) * POOL_GROUP_DIM]
            dw_ref[g] += _dot_tn(pooled_ref[:, lanes], dm)
            dp = _dot_nt(dm, w_ref[g].astype(BF16))
            dpooled.append(dp)
            cnt = jnp.minimum(pos + 1, w).astype(F32)
            ext[pl.ds(0, ts), lanes] = dp / cnt
        for g, w in enumerate(POOL_WINDOWS):
            lanes = pl.ds(g * POOL_GROUP_DIM, POOL_GROUP_DIM)
            win = ext[pl.ds(0, ts), lanes]
            for i in range(1, w):
                win = win + ext[pl.ds(i, ts), lanes]
            dpv_ref[:, lanes] = (win - dpooled[g]).astype(BF16)
        head = ext[pl.ds(0, POOL_HALO), :]
        ext[pl.ds(ts, POOL_HALO), :] = head

    blk = pl.BlockSpec((ts, pw), lambda b, s: (b * ns + (ns - 1 - s), 0))
    vec = pl.BlockSpec((1, pw), lambda b, s: (0, 0))
    wspec = pl.BlockSpec((POOL_GROUPS, POOL_GROUP_DIM, POOL_GROUP_DIM), lambda b, s: (0, 0, 0))
    t = bsz * seq
    return pl.pallas_call(
        body, name="pool_bwd", grid=(bsz, ns),
        in_specs=[blk, blk, blk, wspec, vec, vec],
        out_specs=[blk, wspec, vec, vec],
        out_shape=[jax.ShapeDtypeStruct((t, pw), BF16),
                   jax.ShapeDtypeStruct((POOL_GROUPS, POOL_GROUP_DIM, POOL_GROUP_DIM), F32),
                   jax.ShapeDtypeStruct((1, pw), F32), jax.ShapeDtypeStruct((1, pw), F32)],
        scratch_shapes=[pltpu.VMEM((ts + POOL_HALO, pw), F32)],
        compiler_params=_params("arbitrary", "arbitrary"),
    )(dy, mixed, pooled, pool_w, pool_scale, gain)


AUX_ONE = 64
AUX_F = 67

ATTN_PREP_ROWS = 512


def _seg_ones(width, seg):
    r = lax.broadcasted_iota(jnp.int32, (width, width), 0) // seg
    c = lax.broadcasted_iota(jnp.int32, (width, width), 1) // seg
    return (r == c).astype(BF16)


def _tri_ones(n, lower):
    r = lax.broadcasted_iota(jnp.int32, (n, n), 0)
    c = lax.broadcasted_iota(jnp.int32, (n, n), 1)
    return ((r >= c) if lower else (r <= c)).astype(BF16)


def _place_pieces(first_lane):
    r = lax.broadcasted_iota(jnp.int32, (3 * LANES, N_HEADS * LANES), 0)
    c = lax.broadcasted_iota(jnp.int32, (3 * LANES, N_HEADS * LANES), 1)
    piece, head = r // LANES, r % LANES
    return jnp.logical_and(head < N_HEADS, c == head * LANES + first_lane + piece).astype(BF16)


def _head_sums(x, seg_ones):
    return _dot(x.astype(BF16), seg_ones)


def _log_sigmoid(x):
    return jnp.minimum(x, 0.0) - jnp.log(1.0 + jnp.exp(-jnp.abs(x)))


def _attn_prep_fwd(q, k, f, b_forget, q_gain, k_gain, bsz, seq):
    ts = ATTN_PREP_ROWS
    ns = seq // ts
    aw = ATTN_WIDTH
    t = bsz * seq
    seg = _seg_ones(aw, HEAD_DIM)
    tri = _tri_ones(ts, True)

    def body(q_ref, k_ref, f_ref, bf_ref, gq_ref, gk_ref, seg_ref, tri_ref, place_ref, qp_ref, kp_ref, carry):
        s = pl.program_id(1)

        @pl.when(s == 0)
        def _():
            carry[...] = jnp.zeros_like(carry)

        logf = _log_sigmoid(f_ref[...] + bf_ref[...])
        hi, mid, lo = _split3(logf)
        tri_v = tri_ref[...]
        fc = _dot(tri_v, hi) + _dot(tri_v, mid) + _dot(tri_v, lo) + carry[pl.ds(0, 1), :]
        carry[pl.ds(0, 1), :] = fc[ts - 1:, :]
        pcs = jnp.concatenate(_split3(fc), axis=1)
        lane = lax.broadcasted_iota(jnp.int32, (1, LANES), 1)
        ones_q = jnp.logical_and(lane >= AUX_ONE, lane < AUX_ONE + 3).astype(F32)
        ones_k = jnp.logical_and(lane >= AUX_F, lane < AUX_F + 3).astype(F32)
        seg_v = seg_ref[...]
        placed = _dot(pcs, place_ref[...])

        def build(x_ref, g_ref, scale, out_ref, ones, for_keys):
            xv = x_ref[...]
            r = lax.rsqrt(_head_sums(xv * xv, seg_v) * (1.0 / HEAD_DIM) + EPS)
            xn = xv * r * g_ref[...] * scale
            for h in range(N_HEADS):
                pair = xn[:, (h // 2) * LANES:(h // 2 + 1) * LANES]
                feat = pair if h % 2 == 0 else pltpu.roll(pair, HEAD_DIM, 1)
                aux_h = placed[:, h * LANES:(h + 1) * LANES]
                if for_keys:
                    aux_h = -pltpu.roll(aux_h, LANES - (AUX_F - AUX_ONE), 1)
                out_ref[:, h * LANES:(h + 1) * LANES] = jnp.where(lane < HEAD_DIM, feat, aux_h + ones).astype(BF16)

        build(q_ref, gq_ref, 0.125, qp_ref, ones_q, False)
        build(k_ref, gk_ref, 1.0, kp_ref, ones_k, True)

    blk = pl.BlockSpec((ts, aw), lambda b, s: (b * ns + s, 0))
    fblk = pl.BlockSpec((ts, LANES), lambda b, s: (b * ns + s, 0))
    oblk = pl.BlockSpec((ts, N_HEADS * LANES), lambda b, s: (b * ns + s, 0))
    const = lambda shape: pl.BlockSpec(shape, lambda b, s: (0, 0))
    return pl.pallas_call(
        body, name="attn_prep_fwd", grid=(bsz, ns),
        in_specs=[blk, blk, fblk, const((1, LANES)), const((1, aw)), const((1, aw)), const((aw, aw)), const((ts, ts)),
                  const((3 * LANES, N_HEADS * LANES))],
        out_specs=[oblk, oblk],
        out_shape=[jax.ShapeDtypeStruct((t, N_HEADS * LANES), BF16)] * 2,
        scratch_shapes=[pltpu.VMEM((8, LANES), F32)],
        compiler_params=_params("arbitrary", "arbitrary"),
    )(q, k, f, b_forget, q_gain, k_gain, seg, tri, _place_pieces(AUX_F))


def _attn_prep_bwd(dqp, dkp, q, k, f, b_forget, q_gain, k_gain, bsz, seq):
    ts = ATTN_PREP_ROWS
    ns = seq // ts
    aw = ATTN_WIDTH
    t = bsz * seq
    seg = _seg_ones(aw, HEAD_DIM)
    tri = _tri_ones(ts, False)

    def body(dqp_ref, dkp_ref, q_ref, k_ref, f_ref, bf_ref, gq_ref, gk_ref, seg_ref, tri_ref,
             dq_ref, dk_ref, df_ref, dgq_ref, dgk_ref, dbf_ref, carry):
        b = pl.program_id(0)
        sr = pl.program_id(1)

        @pl.when(jnp.logical_and(b == 0, sr == 0))
        def _():
            dgq_ref[...] = jnp.zeros_like(dgq_ref)
            dgk_ref[...] = jnp.zeros_like(dgk_ref)
            dbf_ref[...] = jnp.zeros_like(dbf_ref)

        @pl.when(sr == 0)
        def _():
            carry[...] = jnp.zeros_like(carry)

        lane = lax.broadcasted_iota(jnp.int32, (1, LANES), 1)
        seg_v = seg_ref[...]

        def norm_bwd(dp_ref, x_ref, g_ref, scale, dx_ref, dgain_ref):
            parts = []
            for j in range(N_HEADS // 2):
                even = dp_ref[:, (2 * j) * LANES:(2 * j + 1) * LANES]
                odd = dp_ref[:, (2 * j + 1) * LANES:(2 * j + 2) * LANES]
                parts.append(jnp.where(lane < HEAD_DIM, even, pltpu.roll(odd, HEAD_DIM, 1)))
            dxn = jnp.concatenate(parts, axis=1) * scale
            xv = x_ref[...]
            r = lax.rsqrt(_head_sums(xv * xv, seg_v) * (1.0 / HEAD_DIM) + EPS)
            n = xv * r
            dgain_ref[...] += jnp.sum(dxn * n, axis=0, keepdims=True)
            dn = dxn * g_ref[...]
            m = _head_sums(dn * n, seg_v) * (1.0 / HEAD_DIM)
            dx_ref[...] = (r * (dn - n * m)).astype(BF16)

        norm_bwd(dqp_ref, q_ref, gq_ref, 0.125, dq_ref, dgq_ref)
        norm_bwd(dkp_ref, k_ref, gk_ref, 1.0, dk_ref, dgk_ref)

        dfc = jnp.zeros((ts, LANES), F32)
        for h in range(N_HEADS):
            cols = pl.ds(h * LANES, LANES)
            both = jnp.where(lane == AUX_F, dqp_ref[:, cols], 0.0) - jnp.where(lane == AUX_ONE, dkp_ref[:, cols], 0.0)
            dfc = jnp.where(lane == h, jnp.sum(both, axis=1, keepdims=True), dfc)
        hi, mid, lo = _split3(dfc)
        tri_v = tri_ref[...]
        dlogf = _dot(tri_v, hi) + _dot(tri_v, mid) + _dot(tri_v, lo) + carry[pl.ds(0, 1), :]
        carry[pl.ds(0, 1), :] = dlogf[0:1, :]
        df = jnp.where(lane < N_HEADS, dlogf * jax.nn.sigmoid(-(f_ref[...] + bf_ref[...])), 0.0)
        df_ref[...] = df.astype(BF16)
        dbf_ref[...] += jnp.sum(df, axis=0, keepdims=True)

    rev = lambda b, s: (b * ns + (ns - 1 - s), 0)
    blk = pl.BlockSpec((ts, aw), rev)
    fblk = pl.BlockSpec((ts, LANES), rev)
    pblk = pl.BlockSpec((ts, N_HEADS * LANES), rev)
    const = lambda shape: pl.BlockSpec(shape, lambda b, s: (0, 0))
    return pl.pallas_call(
        body, name="attn_prep_bwd", grid=(bsz, ns),
        in_specs=[pblk, pblk, blk, blk, fblk, const((1, LANES)), const((1, aw)), const((1, aw)), const((aw, aw)),
                  const((ts, ts))],
        out_specs=[blk, blk, fblk, const((1, aw)), const((1, aw)), const((1, LANES))],
        out_shape=[jax.ShapeDtypeStruct((t, aw), BF16), jax.ShapeDtypeStruct((t, aw), BF16),
                   jax.ShapeDtypeStruct((t, LANES), BF16), jax.ShapeDtypeStruct((1, aw), F32),
                   jax.ShapeDtypeStruct((1, aw), F32), jax.ShapeDtypeStruct((1, LANES), F32)],
        scratch_shapes=[pltpu.VMEM((8, LANES), F32)],
        compiler_params=_params("arbitrary", "arbitrary"),
    )(dqp, dkp, q, k, f, b_forget, q_gain, k_gain, seg, tri)


ATTN_BLOCK = 1024
HEAD_PAIRS = N_HEADS // 2


def _flash_fwd(qp, kp, v, bsz, seq):
    tq = ATTN_BLOCK
    half = tq // 2
    nq = seq // tq
    t = bsz * seq

    def body(q_ref, k_ref, v_ref, o_ref, lse_ref, m_sc, l_sc, acc_sc):
        i = pl.program_id(2)
        m_sc[...] = jnp.full(m_sc.shape, -jnp.inf, F32)
        l_sc[...] = jnp.zeros_like(l_sc)
        acc_sc[...] = jnp.zeros_like(acc_sc)
        lane = lax.broadcasted_iota(jnp.int32, (1, LANES), 1)
        low = lane < HEAD_DIM

        def tile(q0, qn, k_start, kn, k0=None):
            qs = pl.ds(q0, qn)
            ks = pl.ds(k_start, kn)
            vv = v_ref[ks, :]
            for h in range(2):
                mine = low if h == 0 else jnp.logical_not(low)
                cols = pl.ds(h * LANES, LANES)
                s = _dot_nt(q_ref[qs, cols], k_ref[ks, cols])
                if k0 is not None:
                    row = lax.broadcasted_iota(jnp.int32, (qn, kn), 0) + q0
                    col = lax.broadcasted_iota(jnp.int32, (qn, kn), 1) + k0
                    s = jnp.where(row >= col, s, -jnp.inf)
                m_prev = m_sc[h, qs, :]
                m_new = jnp.maximum(m_prev, jnp.max(s, axis=1, keepdims=True))
                p = jnp.exp(s - jnp.tile(m_new, (1, kn // LANES)))
                alpha = jnp.exp(m_prev - m_new)
                l_sc[h, qs, :] = alpha * l_sc[h, qs, :] + jnp.sum(p, axis=1, keepdims=True)
                m_sc[h, qs, :] = m_new
                pv = _dot(p.astype(BF16), jnp.where(mine, vv, jnp.zeros_like(vv)))
                acc_sc[qs, :] = acc_sc[qs, :] * jnp.where(mine, alpha, 1.0) + pv

        def below_diagonal(j, carry):
            tile(0, tq, pl.multiple_of(j * tq, tq), tq)
            return carry

        lax.fori_loop(0, i, below_diagonal, 0)
        diagonal = pl.multiple_of(i * tq, tq)
        tile(0, tq, diagonal, half, k0=0)
        tile(half, half, diagonal + half, half, k0=half)
        l = jnp.where(low, l_sc[0], l_sc[1])
        m = jnp.where(low, m_sc[0], m_sc[1])
        o_ref[...] = acc_sc[...] / l
        lse_ref[...] = m + jnp.log(l)

    qspec = pl.BlockSpec((tq, 2 * LANES), lambda b, hp, i: (b * nq + i, hp))
    kspec = pl.BlockSpec((seq, 2 * LANES), lambda b, hp, i: (b, hp))
    vspec = pl.BlockSpec((seq, LANES), lambda b, hp, i: (b, hp))
    ospec = pl.BlockSpec((tq, LANES), lambda b, hp, i: (b * nq + i, hp))
    return pl.pallas_call(
        body, name="flash_fwd", grid=(bsz, HEAD_PAIRS, nq),
        in_specs=[qspec, kspec, vspec], out_specs=[ospec, ospec],
        out_shape=[jax.ShapeDtypeStruct((t, ATTN_WIDTH), F32), jax.ShapeDtypeStruct((t, ATTN_WIDTH), F32)],
        scratch_shapes=[pltpu.VMEM((2, tq, LANES), F32), pltpu.VMEM((2, tq, LANES), F32), pltpu.VMEM((tq, LANES), F32)],
        compiler_params=_params("arbitrary", "arbitrary", "arbitrary"),
    )(qp, kp, v)


def _flash_bwd(qp, kp, v, o, do, lse, after, bsz, seq):
    tq = ATTN_BLOCK
    half = tq // 2
    nq = seq // tq
    t = bsz * seq

    def body(q_ref, k_ref, v_ref, o_ref, do_ref, lse_ref, after_ref, dq_ref, dk_ref, dv_ref, dk_acc, dv_acc):
        j = pl.program_id(2)

        @pl.when(j == 0)
        def _():
            dq_ref[...] = jnp.zeros_like(dq_ref)

        dk_acc[...] = jnp.zeros_like(dk_acc)
        dv_acc[...] = jnp.zeros_like(dv_acc)
        lane = lax.broadcasted_iota(jnp.int32, (1, LANES), 1)
        low = lane < HEAD_DIM

        def tile(q_start, qn, k0, kn, q0=None):
            rows = pl.ds(q_start, qn)
            ks = pl.ds(k0, kn)
            dov = do_ref[rows, :]
            dd = dov * o_ref[rows, :]
            dob = dov.astype(BF16)
            vv = v_ref[ks, :]
            lse_v = lse_ref[rows, :]
            for h in range(2):
                mine = low if h == 0 else jnp.logical_not(low)
                cols = pl.ds(h * LANES, LANES)
                qh = q_ref[rows, cols]
                kh = k_ref[ks, cols]
                s = _dot_nt(qh, kh)
                lse_h = jnp.where(mine, lse_v, pltpu.roll(lse_v, HEAD_DIM, 1))
                p = jnp.exp(s - jnp.tile(lse_h, (1, kn // LANES)))
                if q0 is not None:
                    row = lax.broadcasted_iota(jnp.int32, (qn, kn), 0) + q0
                    col = lax.broadcasted_iota(jnp.int32, (qn, kn), 1) + k0
                    p = jnp.where(row >= col, p, 0.0)
                delta = jnp.sum(jnp.where(mine, dd, 0.0), axis=1, keepdims=True)
                dp = _dot_nt(dob, jnp.where(mine, vv, jnp.zeros_like(vv)))
                ds = (p * (dp - delta)).astype(BF16)
                dv_acc[ks, :] += jnp.where(mine, _dot_tn(p.astype(BF16), dob), 0.0)
                dk_acc[ks, cols] += _dot_tn(ds, qh)
                dq_ref[rows, cols] += _dot(ds, kh)

        def above_diagonal(i, carry):
            tile(pl.multiple_of(i * tq, tq), tq, 0, tq)
            return carry

        diagonal = pl.multiple_of(j * tq, tq)
        tile(diagonal, tq, 0, half, q0=0)
        tile(diagonal + half, half, half, half, q0=half)
        lax.fori_loop(j + 1, nq, above_diagonal, 0)
        dk_ref[...] = dk_acc[...]
        dv_ref[...] = dv_acc[...].astype(BF16)

    qspec = pl.BlockSpec((seq, 2 * LANES), lambda b, hp, j: (b, hp))
    kspec = pl.BlockSpec((tq, 2 * LANES), lambda b, hp, j: (b * nq + j, hp))
    vspec = pl.BlockSpec((tq, LANES), lambda b, hp, j: (b * nq + j, hp))
    ospec = pl.BlockSpec((seq, LANES), lambda b, hp, j: (b, hp))
    return pl.pallas_call(
        body, name="flash_bwd", grid=(bsz, HEAD_PAIRS, nq),
        in_specs=[qspec, kspec, vspec, ospec, ospec, ospec, ORDER_ONLY], out_specs=[qspec, kspec, vspec],
        out_shape=[jax.ShapeDtypeStruct((t, N_HEADS * LANES), F32), jax.ShapeDtypeStruct((t, N_HEADS * LANES), F32),
                   jax.ShapeDtypeStruct((t, ATTN_WIDTH), BF16)],
        scratch_shapes=[pltpu.VMEM((tq, 2 * LANES), F32), pltpu.VMEM((tq, LANES), F32)],
        compiler_params=_params("arbitrary", "arbitrary", "arbitrary"),
    )(qp, kp, v, o, do, lse, after)


def _mix_out_fwd(o, y_pool, x, gain, w_out):
    t, d = x.shape
    tm = 1024
    pw, aw = POOL_WIDTH, ATTN_WIDTH

    def body(o_ref, yp_ref, x_ref, g_ref, w_ref, ycat_ref, y_ref):
        ov = o_ref[...]
        ya = (ov * _rms_scale(ov) * g_ref[...]).astype(BF16)
        ycat = jnp.concatenate([yp_ref[...], ya], axis=1)
        ycat_ref[...] = ycat
        y_ref[...] = x_ref[...] + _dot(ycat, w_ref[...])

    return pl.pallas_call(
        body, name="mix_out_fwd", grid=(t // tm,),
        in_specs=[_rows(tm, aw), _rows(tm, pw), _rows(tm, d), _resident((1, aw)), _resident((pw + aw, d))],
        out_specs=[_rows(tm, pw + aw), _rows(tm, d)],
        out_shape=[jax.ShapeDtypeStruct((t, pw + aw), BF16), jax.ShapeDtypeStruct((t, d), F32)],
        compiler_params=_params("arbitrary"),
    )(o, y_pool, x, gain, w_out)


def _mix_out_bwd(dx, o, ycat, gain, w_out):
    t, d = dx.shape
    tm = 1024
    nm = t // tm
    pw, aw = POOL_WIDTH, ATTN_WIDTH

    def body(dx_ref, o_ref, ycat_ref, g_ref, w_ref, dw_ref, dyp_ref, do_ref, dg_ref, acc):
        i = pl.program_id(0)

        @pl.when(i == 0)
        def _():
            dg_ref[...] = jnp.zeros_like(dg_ref)
            acc[...] = jnp.zeros_like(acc)

        dxb = dx_ref[...].astype(BF16)
        acc[...] += _dot_tn(ycat_ref[...], dxb)
        dyp_ref[...] = _dot_nt(dxb, w_ref[pl.ds(0, pw), :])
        dya = _dot_nt(dxb, w_ref[pl.ds(pw, aw), :])
        do, dgain = _rms_bwd(dya, o_ref[...], g_ref[...])
        do_ref[...] = do
        dg_ref[...] += dgain

        @pl.when(i == nm - 1)
        def _():
            dw_ref[...] = acc[...].astype(BF16)

    return pl.pallas_call(
        body, name="mix_out_bwd", grid=(nm,),
        in_specs=[_rows(tm, d), _rows(tm, aw), _rows(tm, pw + aw), _resident((1, aw)), _resident((pw + aw, d))],
        out_specs=[pl.BlockSpec((pw + aw, d), lambda i: (0, 0)), _rows(tm, pw), _rows(tm, aw),
                   pl.BlockSpec((1, aw), lambda i: (0, 0))],
        out_shape=[jax.ShapeDtypeStruct((pw + aw, d), BF16), jax.ShapeDtypeStruct((t, pw), F32),
                   jax.ShapeDtypeStruct((t, aw), F32), jax.ShapeDtypeStruct((1, aw), F32)],
        scratch_shapes=[pltpu.VMEM((pw + aw, d), F32)],
        compiler_params=_params("arbitrary"),
    )(dx, o, ycat, gain, w_out)


def _mix_in_bwd(dpv, dq, dk, dv, df, hm, x, dx_res, gain, w_in_t):
    t, d = x.shape
    tm = 512
    nm = t // tm
    pw, aw = POOL_WIDTH, ATTN_WIDTH

    def body(dpv_ref, dq_ref, dk_ref, dv_ref, df_ref, hm_ref, x_ref, dxr_ref, g_ref, w_ref, dw_ref, dx_ref, dxh_ref,
             dg_ref, acc):
        i = pl.program_id(0)

        @pl.when(i == 0)
        def _():
            dg_ref[...] = jnp.zeros_like(dg_ref)
            acc[...] = jnp.zeros_like(acc)

        dh = jnp.concatenate([dpv_ref[...], dq_ref[...], dk_ref[...], dv_ref[...], df_ref[...]], axis=1)
        acc[...] += _dot_tn(dh, hm_ref[...])
        dx, dgain = _rms_bwd(_dot(dh, w_ref[...]), x_ref[...], g_ref[...])
        dx = dxr_ref[...] + dx
        dx_ref[...] = dx
        dxh_ref[...] = (0.5 * dx).astype(BF16)
        dg_ref[...] += dgain

        @pl.when(i == nm - 1)
        def _():
            dw_ref[...] = acc[...].astype(BF16)

    return pl.pallas_call(
        body, name="mix_in_bwd", grid=(nm,),
        in_specs=[_rows(tm, pw), _rows(tm, aw), _rows(tm, aw), _rows(tm, aw), _rows(tm, LANES), _rows(tm, d),
                  _rows(tm, d), _rows(tm, d), _resident((1, d)), _resident((MIX_PAD, d))],
        out_specs=[pl.BlockSpec((MIX_PAD, d), lambda i: (0, 0)), _rows(tm, d), _rows(tm, d),
                   pl.BlockSpec((1, d), lambda i: (0, 0))],
        out_shape=[jax.ShapeDtypeStruct((MIX_PAD, d), BF16), jax.ShapeDtypeStruct((t, d), F32),
                   jax.ShapeDtypeStruct((t, d), BF16), jax.ShapeDtypeStruct((1, d), F32)],
        scratch_shapes=[pltpu.VMEM((MIX_PAD, d), F32)],
        compiler_params=_params("arbitrary"),
    )(dpv, dq, dk, dv, df, hm, x, dx_res, gain, w_in_t)


MESH_IDS = pl.DeviceIdType.MESH


def _me():
    return lax.axis_index("x"), lax.axis_index("y"), lax.axis_index("c")


def _peer(x, y, c, p):
    px = 1 - x if p & 4 else x
    py = 1 - y if p & 2 else y
    pc = 1 - c if p & 1 else c
    return (px, py, pc), 4 * px + 2 * py + pc


HBM_SPEC = pl.BlockSpec(memory_space=pltpu.HBM)
SEM_SPEC = pl.BlockSpec(memory_space=pltpu.SEMAPHORE)
SPLIT_COPY = pltpu.CompilerParams(has_side_effects=pltpu.SideEffectType.DATAFLOW_SIDE_EFFECTING)
PEERS = N_DEV - 1


def _hbm(a):
    return pltpu.with_memory_space_constraint(a, pltpu.HBM)


def _row_block(ref, dev, rows):
    return ref.at[pl.ds(pl.multiple_of(dev * rows, BF16_ROWS), rows)]


def _copy_ends(gather, src, land, me, peer_id):
    if gather:
        rows = src.shape[0]
        return src, _row_block(land, me, rows), _row_block(land, peer_id, rows), src, _row_block(land, me, rows)
    rows = src.shape[0] // N_DEV
    return (_row_block(src, peer_id, rows), land.at[me], land.at[peer_id], _row_block(src, me, rows), land.at[me])


def _land_shape(gather, s):
    return (N_DEV * s.shape[0], s.shape[1]) if gather else (N_DEV, s.shape[0] // N_DEV, s.shape[1])


SIBLING = 1
SAME_CORE_PEERS = (2, 4, 6)
RELAYS = len(SAME_CORE_PEERS)


def _copies_start(groups, gather, name, after=None, relayed=()):
    flat = [s for g in groups for s in g]
    n, ng = len(flat), len(groups)
    lands = [lax.empty(_land_shape(gather, s), s.dtype) for s in flat]
    n_in = 2 * n + (after is not None)

    def body(*refs):
        ins, lnd = refs[:n], refs[n:2 * n]
        sems = refs[n_in:n_in + 2 * ng]
        token = refs[-1]
        x, y, c = _me()
        me = 4 * x + 2 * y + c
        w = 0
        for gi, g in enumerate(groups):
            for k in range(len(g)):
                for p in ((SIBLING,) + SAME_CORE_PEERS if gi in relayed else range(1, N_DEV)):
                    peer, peer_id = _peer(x, y, c, p)
                    src, dst, _, _, _ = _copy_ends(gather, ins[w], lnd[w], me, peer_id)
                    pltpu.make_async_remote_copy(src, dst, sems[2 * gi].at[k * PEERS + p - 1],
                                                 sems[2 * gi + 1].at[k * PEERS + p - 1], device_id=peer,
                                                 device_id_type=MESH_IDS).start()
                w += 1
        token[...] = jnp.zeros_like(token)

    sem_shapes = []
    for g in groups:
        sem_shapes += [pltpu.SemaphoreType.DMA((len(g) * PEERS,))] * 2
    out = pl.pallas_call(
        body, name=name,
        out_shape=(*sem_shapes, *[pltpu.HBM(s.shape, s.dtype) for s in flat],
                   *[pltpu.HBM(l.shape, l.dtype) for l in lands], jax.ShapeDtypeStruct((8, LANES), F32)),
        in_specs=[HBM_SPEC] * (2 * n) + [pl.BlockSpec(memory_space=pl.ANY)] * (after is not None),
        out_specs=(*[SEM_SPEC] * (2 * ng), *[HBM_SPEC] * (2 * n), pl.BlockSpec(memory_space=pltpu.VMEM)),
        input_output_aliases={i: 2 * ng + i for i in range(2 * n)},
        compiler_params=SPLIT_COPY,
    )(*[_hbm(s) for s in flat], *[_hbm(l) for l in lands], *([after] if after is not None else []))
    sems, thru, token = out[:2 * ng], out[2 * ng:2 * ng + 2 * n], out[-1]
    res, w = [], 0
    for gi, g in enumerate(groups):
        res.append((sems[2 * gi], sems[2 * gi + 1], list(thru[w:w + len(g)]), list(thru[n + w:n + w + len(g)])))
        w += len(g)
    return res, token


def _copies_wait(started, gather, after, name):
    send, recv, srcs, lands = started
    n = len(srcs)
    after = list(after) if isinstance(after, (list, tuple)) else [after]

    own_shapes = [s.shape if gather else (s.shape[0] // N_DEV, s.shape[1]) for s in srcs]

    def body(*refs):
        ins, lnd = refs[:n], refs[n:2 * n]
        send_sems, recv_sems = refs[2 * n], refs[2 * n + 1]
        bounce, in_sems, out_sems = refs[-n - 2:-2], refs[-2], refs[-1]
        x, y, c = _me()
        me = 4 * x + 2 * y + c
        ends = [_copy_ends(gather, ins[w], lnd[w], me, me)[3:] for w in range(n)]
        loads = [pltpu.make_async_copy(ends[w][0], bounce[w], in_sems.at[w]) for w in range(n)]
        stores = [pltpu.make_async_copy(bounce[w], ends[w][1], out_sems.at[w]) for w in range(n)]
        for cp in loads:
            cp.start()
        for w in range(n):
            loads[w].wait()
            stores[w].start()
        for w in range(n):
            for p in range(1, N_DEV):
                peer, peer_id = _peer(x, y, c, p)
                src, _, arrival, _, _ = _copy_ends(gather, ins[w], lnd[w], me, peer_id)
                cp = pltpu.make_async_remote_copy(src, arrival, send_sems.at[w * PEERS + p - 1],
                                                  recv_sems.at[w * PEERS + p - 1], device_id=peer,
                                                  device_id_type=MESH_IDS)
                cp.wait_send()
                cp.wait_recv()
        for cp in stores:
            cp.wait()

    out = pl.pallas_call(
        body, name=name,
        out_shape=(*[pltpu.HBM(s.shape, s.dtype) for s in srcs], *[pltpu.HBM(l.shape, l.dtype) for l in lands]),
        in_specs=[HBM_SPEC] * (2 * n) + [SEM_SPEC, SEM_SPEC] + [pl.BlockSpec(memory_space=pl.ANY)] * len(after),
        out_specs=[HBM_SPEC] * (2 * n),
        input_output_aliases={i: i for i in range(2 * n)},
        scratch_shapes=[*[pltpu.VMEM(shape, s.dtype) for shape, s in zip(own_shapes, srcs)],
                        pltpu.SemaphoreType.DMA((n,)), pltpu.SemaphoreType.DMA((n,))],
        compiler_params=SPLIT_COPY,
    )(*srcs, *lands, send, recv, *after)
    return list(out[n:])


def _relay_to_sibling(started, name, after=None):
    send, recv, srcs, lands = started
    n = len(srcs)
    after = [] if after is None else [after]

    def body(*refs):
        ins, lnd = refs[:n], refs[n:2 * n]
        send_sems, recv_sems = refs[2 * n], refs[2 * n + 1]
        relay_send, relay_recv = refs[2 * n + 2 + len(after)], refs[2 * n + 3 + len(after)]
        x, y, c = _me()
        sibling, _ = _peer(x, y, c, SIBLING)
        for w in range(n):
            rows = ins[w].shape[0]
            for k, p in enumerate(SAME_CORE_PEERS):
                peer, peer_id = _peer(x, y, c, p)
                arrived = _row_block(lnd[w], peer_id, rows)
                first = pltpu.make_async_remote_copy(ins[w], arrived, send_sems.at[w * PEERS + p - 1],
                                                     recv_sems.at[w * PEERS + p - 1], device_id=peer,
                                                     device_id_type=MESH_IDS)
                first.wait_recv()
                pltpu.make_async_remote_copy(arrived, arrived, relay_send.at[w * RELAYS + k],
                                             relay_recv.at[w * RELAYS + k], device_id=sibling,
                                             device_id_type=MESH_IDS).start()
                first.wait_send()

    sems = pltpu.SemaphoreType.DMA((n * RELAYS,))
    out = pl.pallas_call(
        body, name=name,
        out_shape=(sems, sems, *[pltpu.HBM(s.shape, s.dtype) for s in srcs], *[pltpu.HBM(l.shape, l.dtype) for l in lands]),
        in_specs=[HBM_SPEC] * (2 * n) + [SEM_SPEC, SEM_SPEC] + [pl.BlockSpec(memory_space=pl.ANY)] * len(after),
        out_specs=(SEM_SPEC, SEM_SPEC, *[HBM_SPEC] * (2 * n)),
        input_output_aliases={i: 2 + i for i in range(2 * n)},
        compiler_params=SPLIT_COPY,
    )(*srcs, *lands, send, recv, *after)
    return send, recv, out[0], out[1], list(out[2:2 + n]), list(out[2 + n:])


def _relayed_wait(relayed, after, name):
    send, recv, relay_send, relay_recv, srcs, lands = relayed
    n = len(srcs)
    after = list(after) if isinstance(after, (list, tuple)) else [after]

    def body(*refs):
        ins, lnd = refs[:n], refs[n:2 * n]
        send_sems, recv_sems, relay_send_sems, relay_recv_sems = refs[2 * n:2 * n + 4]
        bounce, in_sems, out_sems = refs[-n - 2:-2], refs[-2], refs[-1]
        x, y, c = _me()
        me = 4 * x + 2 * y + c
        sibling, sibling_id = _peer(x, y, c, SIBLING)
        loads = [pltpu.make_async_copy(ins[w], bounce[w], in_sems.at[w]) for w in range(n)]
        stores = [pltpu.make_async_copy(bounce[w], _row_block(lnd[w], me, ins[w].shape[0]), out_sems.at[w])
                  for w in range(n)]
        for cp in loads:
            cp.start()
        for w in range(n):
            loads[w].wait()
            stores[w].start()
        for w in range(n):
            rows = ins[w].shape[0]
            direct = pltpu.make_async_remote_copy(ins[w], _row_block(lnd[w], sibling_id, rows),
                                                  send_sems.at[w * PEERS + SIBLING - 1],
                                                  recv_sems.at[w * PEERS + SIBLING - 1], device_id=sibling,
                                                  device_id_type=MESH_IDS)
            direct.wait_send()
            direct.wait_recv()
            for k, p in enumerate(SAME_CORE_PEERS):
                _, sent_id = _peer(x, y, c, p)
                _, got_id = _peer(x, y, c, p + SIBLING)
                relay = pltpu.make_async_remote_copy(_row_block(lnd[w], sent_id, rows), _row_block(lnd[w], got_id, rows),
                                                     relay_send_sems.at[w * RELAYS + k],
                                                     relay_recv_sems.at[w * RELAYS + k], device_id=sibling,
                                                     device_id_type=MESH_IDS)
                relay.wait_send()
                relay.wait_recv()
        for cp in stores:
            cp.wait()

    out = pl.pallas_call(
        body, name=name,
        out_shape=(*[pltpu.HBM(s.shape, s.dtype) for s in srcs], *[pltpu.HBM(l.shape, l.dtype) for l in lands]),
        in_specs=[HBM_SPEC] * (2 * n) + [SEM_SPEC] * 4 + [pl.BlockSpec(memory_space=pl.ANY)] * len(after),
        out_specs=[HBM_SPEC] * (2 * n),
        input_output_aliases={i: i for i in range(2 * n)},
        scratch_shapes=[*[pltpu.VMEM(s.shape, s.dtype) for s in srcs],
                        pltpu.SemaphoreType.DMA((n,)), pltpu.SemaphoreType.DMA((n,))],
        compiler_params=SPLIT_COPY,
    )(*srcs, *lands, send, recv, relay_send, relay_recv, *after)
    return list(out[n:])


def _adamw_update(w, g, m, v):
    nm = ADAM_B1 * m + (1.0 - ADAM_B1) * g
    nv = ADAM_B2 * v + (1.0 - ADAM_B2) * (g * g)
    m_hat = nm / (1.0 - ADAM_B1 ** ADAM_STEP)
    v_hat = nv / (1.0 - ADAM_B2 ** ADAM_STEP)
    return -ADAM_LR * (m_hat / (jnp.sqrt(v_hat) + ADAM_EPS) + ADAM_WD * w), nm, nv


SUM_ADAMW_COLS = 512


def _sum_adamw(parts, w, m, v, name):
    _, rows, d = parts.shape
    n = w.shape[0]
    tc = SUM_ADAMW_COLS

    def body(p_ref, w_ref, m_ref, v_ref, g_ref, d_ref, nm_ref, nv_ref):
        g = p_ref[0].astype(F32)
        for dev in range(1, N_DEV):
            g = g + p_ref[dev].astype(F32)
        g = g[:n]
        g_ref[...] = g
        d_ref[...], nm_ref[...], nv_ref[...] = _adamw_update(w_ref[...], g, m_ref[...], v_ref[...])

    spec = pl.BlockSpec((n, tc), lambda j: (0, j))
    shape = jax.ShapeDtypeStruct((n, d), F32)
    return pl.pallas_call(
        body, name=name, grid=(d // tc,),
        in_specs=[pl.BlockSpec((N_DEV, rows, tc), lambda j: (0, 0, j)), spec, spec, spec],
        out_specs=[spec] * 4, out_shape=[shape] * 4,
        compiler_params=_params("arbitrary"),
    )(parts, w, m, v)


def _pad_rows(a, rows):
    return jnp.pad(a, ((0, rows - a.shape[0]), (0, 0)))


def _row1(vec, width=D_MODEL):
    return jnp.pad(vec.reshape(1, -1), ((0, 0), (0, width - vec.shape[-1])))


COLUMN_SHARDED = ("ffn1_w_gate", "ffn1_w_up", "w_in", "ffn2_w_gate", "ffn2_w_up")
VEC_NAMES = ("ffn1_norm", "mix_norm", "ffn2_norm", "b_forget", "pool_scale", "q_norm", "k_norm", "out_norm_pool",
             "out_norm_attn")
VEC_ROWS = 16
LOSS_ROW = len(VEC_NAMES)


def _pack_vector_grads(parts, loss_part, name):
    names = [n for n in VEC_NAMES if n in parts]
    extra = [] if loss_part is None else [loss_part]

    def body(*refs):
        out_ref = refs[-1]
        out_ref[...] = jnp.zeros_like(out_ref)
        lane = lax.broadcasted_iota(jnp.int32, (1, LANES), 1)
        for n, ref in zip(names, refs):
            val = ref[...]
            if n in ("q_norm", "k_norm"):
                val = val[:, 0:LANES] + val[:, LANES:2 * LANES] + val[:, 2 * LANES:3 * LANES] + val[:, 3 * LANES:]
                val = jnp.where(lane < HEAD_DIM, val + pltpu.roll(val, HEAD_DIM, 1), 0.0)
            out_ref[pl.ds(VEC_NAMES.index(n), 1), pl.ds(0, val.shape[1])] = val
        if extra:
            out_ref[pl.ds(LOSS_ROW, 1), pl.ds(0, 1)] = refs[len(names)][...]

    vmem = pl.BlockSpec(memory_space=pltpu.VMEM)
    return pl.pallas_call(
        body, name=name, in_specs=[vmem] * (len(names) + len(extra)), out_specs=vmem,
        out_shape=jax.ShapeDtypeStruct((VEC_ROWS, D_MODEL), F32),
    )(*[parts[n] for n in names], *extra)


def _small_adamw(vec_all, pool_all, vec_params, pool_params):
    nv = len(vec_params)
    pool_rows = pool_params[0].shape[0]

    def body(*refs):
        vec_ref, pool_ref = refs[0], refs[1]
        ins = refs[2:2 + 3 * nv + 3]
        outs = refs[2 + 3 * nv + 3:-1]
        rows = refs[-1]
        total = vec_ref[pl.ds(0, VEC_ROWS), :]
        for dev in range(1, N_DEV):
            total = total + vec_ref[pl.ds(dev * VEC_ROWS, VEC_ROWS), :]
        rows[...] = total
        outs[4 * nv + 4][...] = rows[pl.ds(LOSS_ROW, 1), pl.ds(0, 1)]
        for i in range(nv):
            w_ref, m_ref, v_ref = ins[3 * i:3 * i + 3]
            g = rows[pl.ds(i, 1), pl.ds(0, w_ref.shape[1])]
            outs[4 * i][...] = g
            outs[4 * i + 1][...], outs[4 * i + 2][...], outs[4 * i + 3][...] = _adamw_update(
                w_ref[...], g, m_ref[...], v_ref[...])
        g = pool_ref[pl.ds(0, pool_rows), :].astype(F32)
        for dev in range(1, N_DEV):
            g = g + pool_ref[pl.ds(dev * pool_rows, pool_rows), :].astype(F32)
        w_ref, m_ref, v_ref = ins[3 * nv:]
        outs[4 * nv][...] = g
        outs[4 * nv + 1][...], outs[4 * nv + 2][...], outs[4 * nv + 3][...] = _adamw_update(
            w_ref[...], g, m_ref[...], v_ref[...])

    vmem = pl.BlockSpec(memory_space=pltpu.VMEM)
    flat = [a for trio in vec_params for a in trio] + list(pool_params)
    out_shape = []
    for trio in list(vec_params) + [pool_params]:
        out_shape += [jax.ShapeDtypeStruct(trio[0].shape, F32)] * 4
    out_shape.append(jax.ShapeDtypeStruct((1, 1), F32))
    return pl.pallas_call(
        body, name="adamw_small", in_specs=[vmem] * (2 + len(flat)), out_specs=[vmem] * len(out_shape),
        out_shape=out_shape, scratch_shapes=[pltpu.VMEM((VEC_ROWS, D_MODEL), F32)],
    )(vec_all, pool_all, *flat)


def kernel(x, ffn1_norm, ffn1_w_gate, ffn1_w_up, ffn1_w_down, mix_norm, w_in, b_forget, pool_w, pool_scale, q_norm, k_norm, out_norm_pool, out_norm_attn, w_out, ffn2_norm, ffn2_w_gate, ffn2_w_up, ffn2_w_down, loss_target, m_ffn1_norm, m_ffn1_w_gate, m_ffn1_w_up, m_ffn1_w_down, m_mix_norm, m_w_in, m_b_forget, m_pool_w, m_pool_scale, m_q_norm, m_k_norm, m_out_norm_pool, m_out_norm_attn, m_w_out, m_ffn2_norm, m_ffn2_w_gate, m_ffn2_w_up, m_ffn2_w_down, v_ffn1_norm, v_ffn1_w_gate, v_ffn1_w_up, v_ffn1_w_down, v_mix_norm, v_w_in, v_b_forget, v_pool_w, v_pool_scale, v_q_norm, v_k_norm, v_out_norm_pool, v_out_norm_attn, v_w_out, v_ffn2_norm, v_ffn2_w_gate, v_ffn2_w_up, v_ffn2_w_down):
    bsz, seq, d = x.shape
    t = bsz * seq
    x0 = x.reshape(t, d)
    target = loss_target.reshape(t, d)
    in_rows = -(-w_in.shape[1] // BF16_ROWS) * BF16_ROWS

    slabs = [s.astype(BF16) for s in (ffn1_w_gate.T, ffn1_w_up.T, ffn1_w_down, _pad_rows(w_in.T, in_rows), w_out,
                                       ffn2_w_gate.T, ffn2_w_up.T, ffn2_w_down)]
    first, started = _copies_start([slabs[0:2]], True, "gather_start_first", relayed=(0,))
    rest, started = _copies_start([slabs[2:3], slabs[3:4], slabs[4:5], slabs[5:8]], True, "gather_start", after=started,
                                  relayed=(3,))
    gathers = first + rest

    g1, gm, g2 = ffn1_norm.reshape(1, d), mix_norm.reshape(1, d), ffn2_norm.reshape(1, d)
    bf_row = _row1(b_forget, LANES)
    gq = jnp.tile(q_norm, N_HEADS).reshape(1, ATTN_WIDTH)
    gk = jnp.tile(k_norm, N_HEADS).reshape(1, ATTN_WIDTH)
    scale_row = pool_scale.reshape(1, POOL_WIDTH)
    gp, ga = out_norm_pool.reshape(1, POOL_WIDTH), out_norm_attn.reshape(1, ATTN_WIDTH)

    wg1, wu1 = _relayed_wait(_relay_to_sibling(gathers[0], "gather_relay_ffn1_up", started), started,
                             "gather_wait_ffn1_up")
    h1, sa1, sb1, s1 = _ffn_up(x0, g1, wg1, wu1, "ffn1_up")
    (wd1,) = _copies_wait(gathers[1], True, s1, "gather_wait_ffn1_down")
    (x1,) = _ffn_down(s1, wd1, x0, None, "ffn1_down")
    (win_g,) = _copies_wait(gathers[2], True, x1, "gather_wait_w_in")
    win_t = _repack_rows(win_g, in_rows, w_in.shape[1], N_DEV, "w_in_rows")
    hm, pv, q, k, v, f = _mix_in_fwd(x1, gm, win_t)
    pooled, mixed, y_pool = _pool_fwd(pv, pool_w, scale_row, gp, bsz, seq)
    qp, kp = _attn_prep_fwd(q, k, f, bf_row, gq, gk, bsz, seq)
    o, lse = _flash_fwd(qp, kp, v, bsz, seq)
    relayed_ffn2 = _relay_to_sibling(gathers[4], "gather_relay_ffn2", o)
    (wout,) = _copies_wait(gathers[3], True, [o, relayed_ffn2[4][0]], "gather_wait_w_out")
    ycat, x2 = _mix_out_fwd(o, y_pool, x1, ga, wout)
    wg2, wu2, wd2 = _relayed_wait(relayed_ffn2, x2, "gather_wait_ffn2")
    h2, sa2, sb2, s2 = _ffn_up(x2, g2, wg2, wu2, "ffn2_up")
    dx3, dyh2, loss_part = _ffn_down(s2, wd2, x2, target, "ffn2_down")

    da2, db2, dwg2, dwu2 = _ffn_bwd_act(dyh2, sa2, sb2, h2, wd2, dx3, "ffn2_bwd_act")
    (dwd2,) = _wgrad([s2], dyh2, da2, "ffn2_down_wgrad")
    (sent_ffn2,), tok = _copies_start([[dwg2, dwu2, dwd2]], False, "exchange_start_ffn2")
    dx2, dg2 = _ffn_bwd_dx(da2, db2, dx3, x2, g2, wg2, wu2, tok, "ffn2_bwd_dx")
    dwout, dy_pool, do, dga = _mix_out_bwd(dx2, o, ycat, ga, wout)
    (sent_out,), tok = _copies_start([[dwout]], False, "exchange_start_w_out")
    dqp, dkp, dv = _flash_bwd(qp, kp, v, o, do, lse, tok, bsz, seq)
    dq, dk, df, dgq, dgk, dbf = _attn_prep_bwd(dqp, dkp, q, k, f, bf_row, gq, gk, bsz, seq)
    dpv, dpool_w, dscale, dgp = _pool_bwd(dy_pool, mixed, pooled, pool_w, scale_row, gp, bsz, seq)
    dwin, dx1, dyh1, dgm = _mix_in_bwd(dpv, dq, dk, dv, df, hm, x1, dx2, gm, win_t)
    dwin_blocks = _repack_rows(dwin, w_in.shape[1], in_rows, N_DEV, "w_in_grad_blocks")
    (sent_in,), tok = _copies_start([[dwin_blocks]], False, "exchange_start_w_in")
    (dwd1,) = _wgrad([s1], dyh1, tok, "ffn1_down_wgrad")
    (sent_down1,), tok = _copies_start([[dwd1]], False, "exchange_start_ffn1_down", after=tok)
    da1, db1, dwg1, dwu1 = _ffn_bwd_act(dyh1, sa1, sb1, h1, wd1, tok, "ffn1_bwd_act")
    (sent_up1,), tok = _copies_start([[dwg1, dwu1]], False, "exchange_start_ffn1_up", after=tok)
    dx0, dg1 = _ffn_bwd_dx(da1, db1, dx1, x0, g1, wg1, wu1, tok, "ffn1_bwd_dx")

    pool_rows = POOL_GROUPS * POOL_GROUP_DIM
    packed = _pack_vector_grads(dict(ffn1_norm=dg1, mix_norm=dgm, ffn2_norm=dg2, b_forget=dbf, pool_scale=dscale,
                                     q_norm=dgq, k_norm=dgk, out_norm_pool=dgp, out_norm_attn=dga), loss_part,
                                "pack_vector_grads")
    pool_part = dpool_w.reshape(pool_rows, POOL_GROUP_DIM).astype(BF16)
    (sent_small,), tok = _copies_start([[packed, pool_part]], True, "small_grads_start")

    weights = dict(ffn1_norm=ffn1_norm, ffn1_w_gate=ffn1_w_gate, ffn1_w_up=ffn1_w_up, ffn1_w_down=ffn1_w_down,
                   mix_norm=mix_norm, w_in=w_in, b_forget=b_forget, pool_w=pool_w, pool_scale=pool_scale,
                   q_norm=q_norm, k_norm=k_norm, out_norm_pool=out_norm_pool, out_norm_attn=out_norm_attn,
                   w_out=w_out, ffn2_norm=ffn2_norm, ffn2_w_gate=ffn2_w_gate, ffn2_w_up=ffn2_w_up,
                   ffn2_w_down=ffn2_w_down)
    m_in = dict(ffn1_norm=m_ffn1_norm, ffn1_w_gate=m_ffn1_w_gate, ffn1_w_up=m_ffn1_w_up, ffn1_w_down=m_ffn1_w_down,
                mix_norm=m_mix_norm, w_in=m_w_in, b_forget=m_b_forget, pool_w=m_pool_w, pool_scale=m_pool_scale,
                q_norm=m_q_norm, k_norm=m_k_norm, out_norm_pool=m_out_norm_pool, out_norm_attn=m_out_norm_attn,
                w_out=m_w_out, ffn2_norm=m_ffn2_norm, ffn2_w_gate=m_ffn2_w_gate, ffn2_w_up=m_ffn2_w_up,
                ffn2_w_down=m_ffn2_w_down)
    v_in = dict(ffn1_norm=v_ffn1_norm, ffn1_w_gate=v_ffn1_w_gate, ffn1_w_up=v_ffn1_w_up, ffn1_w_down=v_ffn1_w_down,
                mix_norm=v_mix_norm, w_in=v_w_in, b_forget=v_b_forget, pool_w=v_pool_w, pool_scale=v_pool_scale,
                q_norm=v_q_norm, k_norm=v_k_norm, out_norm_pool=v_out_norm_pool, out_norm_attn=v_out_norm_attn,
                w_out=v_w_out, ffn2_norm=v_ffn2_norm, ffn2_w_gate=v_ffn2_w_gate, ffn2_w_up=v_ffn2_w_up,
                ffn2_w_down=v_ffn2_w_down)
    grads, delta, new_m, new_v = {}, {}, {}, {}
    after = [tok]
    plan = ((sent_ffn2, "ffn2", ("ffn2_w_gate", "ffn2_w_up", "ffn2_w_down")), (sent_out, "w_out", ("w_out",)),
            (sent_in, "w_in", ("w_in",)), (sent_down1, "ffn1_down", ("ffn1_w_down",)),
            (sent_up1, "ffn1_up", ("ffn1_w_gate", "ffn1_w_up")))
    for sent, tag, names in plan:
        parts = _copies_wait(sent, False, after, f"exchange_wait_{tag}")
        after = []
        for n, part in zip(names, parts):
            turn = (lambda a: a.T) if n in COLUMN_SHARDED else (lambda a: a)
            done = _sum_adamw(part, turn(weights[n]), turn(m_in[n]), turn(v_in[n]), f"adamw_{n}")
            grads[n], delta[n], new_m[n], new_v[n] = (turn(a) for a in done)
            after.append(done[3])
    vec_all, pool_all = _copies_wait(sent_small, True, after, "small_grads_wait")
    as_row = lambda a: a.reshape(1, -1)
    as_pool = lambda a: a.reshape(pool_rows, POOL_GROUP_DIM)
    small = _small_adamw(vec_all, pool_all,
                         [tuple(as_row(z[n]) for z in (weights, m_in, v_in)) for n in VEC_NAMES],
                         tuple(as_pool(z["pool_w"]) for z in (weights, m_in, v_in)))
    for i, n in enumerate(VEC_NAMES + ("pool_w",)):
        grads[n], delta[n], new_m[n], new_v[n] = (a.reshape(weights[n].shape) for a in small[4 * i:4 * i + 4])
    loss = small[-1].reshape(())

    order = ("ffn1_norm", "ffn1_w_gate", "ffn1_w_up", "ffn1_w_down", "mix_norm", "w_in", "b_forget", "pool_w",
             "pool_scale", "q_norm", "k_norm", "out_norm_pool", "out_norm_attn", "w_out", "ffn2_norm", "ffn2_w_gate",
             "ffn2_w_up", "ffn2_w_down")
    return (loss, dx0.reshape(bsz, seq, d), *[grads[n] for n in order], *[delta[n] for n in order],
            *[new_m[n] for n in order], *[new_v[n] for n in order])
```

```python
import jax
import jax.numpy as jnp
from jax import lax
from jax.experimental import pallas as pl
from jax.experimental.pallas import tpu as pltpu

F32 = jnp.float32
BF16 = jnp.bfloat16

EPS = 1e-6
D_MODEL = 1024
N_HEADS = 8
HEAD_DIM = 64
POOL_WIDTH = 512
ATTN_WIDTH = 512
POOL_GROUPS = 4
POOL_GROUP_DIM = 128
POOL_WINDOWS = (2, 4, 8, 16)
POOL_HALO = 16
MIX_PAD = POOL_WIDTH + 3 * ATTN_WIDTH + 128
N_DEV = 8
BF16_ROWS = 16
LANES = 128
VMEM_LIMIT = 56 * 1024 * 1024

ADAM_LR = 0.001
ADAM_B1 = 0.9
ADAM_B2 = 0.999
ADAM_EPS = 1e-08
ADAM_WD = 0.01
ADAM_STEP = 10


def _params(*sem):
    return pltpu.CompilerParams(dimension_semantics=sem, vmem_limit_bytes=VMEM_LIMIT)


def _dot(a, b):
    return jnp.dot(a, b, preferred_element_type=F32)


def _dot_nt(a, b):
    return lax.dot_general(a, b, (((1,), (1,)), ((), ())), preferred_element_type=F32)


def _dot_tn(a, b):
    return lax.dot_general(a, b, (((0,), (0,)), ((), ())), preferred_element_type=F32)


def _resident(shape):
    return pl.BlockSpec(shape, lambda *_: (0,) * len(shape), pipeline_mode=pl.Buffered(1))


def _rows(tm, width):
    return pl.BlockSpec((tm, width), lambda i: (i, 0))


ORDER_ONLY = pl.BlockSpec(memory_space=pl.ANY)


def _rms_scale(x):
    return lax.rsqrt(jnp.mean(x * x, axis=-1, keepdims=True) + EPS)


def _rms_bwd(dh, x, gain):
    r = _rms_scale(x)
    n = x * r
    dgain = jnp.sum(dh * n, axis=0, keepdims=True)
    dn = dh * gain
    dx = r * (dn - n * jnp.mean(dn * n, axis=-1, keepdims=True))
    return dx, dgain


def _split3(x):
    hi = x.astype(BF16)
    r1 = x - hi.astype(F32)
    mid = r1.astype(BF16)
    lo = (r1 - mid.astype(F32)).astype(BF16)
    return hi, mid, lo


FF_CHUNK = 256


def _swiglu_parts(a, b):
    sig = jax.nn.sigmoid(a)
    silu = a * sig
    return (b * (sig + silu * (1.0 - sig))).astype(BF16), silu.astype(BF16), (silu * b).astype(BF16)


def _ffn_up(x, gain, wg_t, wu_t, name):
    t, d = x.shape
    f = wg_t.shape[0]
    tm = 512

    def body(x_ref, g_ref, wg_ref, wu_ref, h_ref, sa_ref, sb_ref, s_ref):
        xv = x_ref[...]
        h = (xv * _rms_scale(xv) * g_ref[...]).astype(BF16)
        h_ref[...] = h
        for c in range(f // FF_CHUNK):
            sl = pl.ds(c * FF_CHUNK, FF_CHUNK)
            sa_ref[:, sl], sb_ref[:, sl], s_ref[:, sl] = _swiglu_parts(_dot_nt(h, wg_ref[sl, :]), _dot_nt(h, wu_ref[sl, :]))

    wide = jax.ShapeDtypeStruct((t, f), BF16)
    return pl.pallas_call(
        body, name=name, grid=(t // tm,),
        in_specs=[_rows(tm, d), _resident((1, d)), _resident((f, d)), _resident((f, d))],
        out_specs=[_rows(tm, d), _rows(tm, f), _rows(tm, f), _rows(tm, f)],
        out_shape=[jax.ShapeDtypeStruct((t, d), BF16), wide, wide, wide],
        compiler_params=_params("arbitrary"),
    )(x, gain, wg_t, wu_t)


def _ffn_down(s, wd, x, target, name):
    t, d = x.shape
    f = wd.shape[0]
    tm = 512
    with_loss = target is not None

    def body(*refs):
        if with_loss:
            s_ref, w_ref, x_ref, t_ref, dy_ref, dyh_ref, loss_ref = refs
        else:
            s_ref, w_ref, x_ref, y_ref = refs
        y = x_ref[...] + 0.5 * _dot(s_ref[...], w_ref[...])
        if with_loss:
            e = y - t_ref[...]
            dy = e * (1.0 / d)
            dy_ref[...] = dy
            dyh_ref[...] = (0.5 * dy).astype(BF16)

            @pl.when(pl.program_id(0) == 0)
            def _():
                loss_ref[...] = jnp.zeros_like(loss_ref)

            part = jnp.sum(jnp.sum(e * e, axis=0, keepdims=True), axis=1, keepdims=True)
            loss_ref[...] += part * (0.5 / d)
        else:
            y_ref[...] = y

    in_specs = [_rows(tm, f), _resident((f, d)), _rows(tm, d)]
    args = [s, wd, x]
    if with_loss:
        in_specs.append(_rows(tm, d))
        args.append(target)
        out_shape = [jax.ShapeDtypeStruct((t, d), F32), jax.ShapeDtypeStruct((t, d), BF16),
                     jax.ShapeDtypeStruct((1, 1), F32)]
        out_specs = [_rows(tm, d), _rows(tm, d), pl.BlockSpec((1, 1), lambda i: (0, 0))]
    else:
        out_shape = [jax.ShapeDtypeStruct((t, d), F32)]
        out_specs = [_rows(tm, d)]
    return pl.pallas_call(
        body, name=name, grid=(t // tm,), in_specs=in_specs, out_specs=out_specs, out_shape=out_shape,
        compiler_params=_params("arbitrary"),
    )(*args)


def _ffn_bwd_act(dyh, sa, sb, h, wd, after, name):
    t, d = dyh.shape
    f = wd.shape[0]
    tn = f // 2
    tk = 512
    nk = t // tk

    def body(dy_ref, sa_ref, sb_ref, h_ref, wd_ref, after_ref, da_ref, db_ref, dwg_ref, dwu_ref, acc_g, acc_u):
        k = pl.program_id(1)

        @pl.when(k == 0)
        def _():
            acc_g[...] = jnp.zeros_like(acc_g)
            acc_u[...] = jnp.zeros_like(acc_u)

        ds = _dot_nt(dy_ref[...], wd_ref[...])
        da = (ds * sa_ref[...].astype(F32)).astype(BF16)
        db = (ds * sb_ref[...].astype(F32)).astype(BF16)
        da_ref[...] = da
        db_ref[...] = db
        hv = h_ref[...]
        acc_g[...] += _dot_tn(da, hv)
        acc_u[...] += _dot_tn(db, hv)

        @pl.when(k == nk - 1)
        def _():
            dwg_ref[...] = acc_g[...].astype(BF16)
            dwu_ref[...] = acc_u[...].astype(BF16)

    tokens = pl.BlockSpec((tk, d), lambda j, k: (k, 0))
    wide = pl.BlockSpec((tk, tn), lambda j, k: (k, j))
    weight = pl.BlockSpec((tn, d), lambda j, k: (j, 0))
    return pl.pallas_call(
        body, name=name, grid=(f // tn, nk),
        in_specs=[tokens, wide, wide, tokens, weight, ORDER_ONLY],
        out_specs=[wide, wide, weight, weight],
        out_shape=[jax.ShapeDtypeStruct((t, f), BF16)] * 2 + [jax.ShapeDtypeStruct((f, d), BF16)] * 2,
        scratch_shapes=[pltpu.VMEM((tn, d), F32)] * 2,
        compiler_params=_params("arbitrary", "arbitrary"),
    )(dyh, sa, sb, h, wd, after)


def _ffn_bwd_dx(da, db, dy, x, gain, wg_t, wu_t, after, name):
    t, d = x.shape
    f = wg_t.shape[0]
    tm = 512

    def body(da_ref, db_ref, dy_ref, x_ref, g_ref, wg_ref, wu_ref, after_ref, dx_ref, dg_ref):
        dh = _dot(da_ref[...], wg_ref[...]) + _dot(db_ref[...], wu_ref[...])
        dx, dgain = _rms_bwd(dh, x_ref[...], g_ref[...])
        dx_ref[...] = dy_ref[...] + dx

        @pl.when(pl.program_id(0) == 0)
        def _():
            dg_ref[...] = jnp.zeros_like(dg_ref)

        dg_ref[...] += dgain

    return pl.pallas_call(
        body, name=name, grid=(t // tm,),
        in_specs=[_rows(tm, f), _rows(tm, f), _rows(tm, d), _rows(tm, d), _resident((1, d)), _resident((f, d)),
                  _resident((f, d)), ORDER_ONLY],
        out_specs=[_rows(tm, d), pl.BlockSpec((1, d), lambda i: (0, 0))],
        out_shape=[jax.ShapeDtypeStruct((t, d), F32), jax.ShapeDtypeStruct((1, d), F32)],
        compiler_params=_params("arbitrary"),
    )(da, db, dy, x, gain, wg_t, wu_t, after)


def _wgrad(lhs, b, after, name):
    t, n = lhs[0].shape
    d = b.shape[1]
    m = len(lhs)
    tn = n // 2 if n * d * m > (4 << 20) else n
    tk = 1024
    nk = t // tk

    def body(*refs):
        a_refs, b_ref, o_refs, accs = refs[:m], refs[m], refs[m + 2:2 * m + 2], refs[2 * m + 2:]
        k = pl.program_id(1)

        @pl.when(k == 0)
        def _():
            for acc in accs:
                acc[...] = jnp.zeros_like(acc)

        bv = b_ref[...]
        for a_ref, acc in zip(a_refs, accs):
            acc[...] += _dot_tn(a_ref[...], bv)

        @pl.when(k == nk - 1)
        def _():
            for o_ref, acc in zip(o_refs, accs):
                o_ref[...] = acc[...].astype(BF16)

    return pl.pallas_call(
        body, name=name, grid=(n // tn, nk),
        in_specs=[pl.BlockSpec((tk, tn), lambda j, k: (k, j))] * m + [pl.BlockSpec((tk, d), lambda j, k: (k, 0)),
                                                                       ORDER_ONLY],
        out_specs=[pl.BlockSpec((tn, d), lambda j, k: (j, 0))] * m,
        out_shape=[jax.ShapeDtypeStruct((n, d), BF16)] * m,
        scratch_shapes=[pltpu.VMEM((tn, d), F32)] * m,
        compiler_params=_params("arbitrary", "arbitrary"),
    )(*lhs, b, after)


def _repack_rows(a, rows_in, rows_out, blocks, name):
    total, d = a.shape
    real = min(rows_in, rows_out)

    def body(a_ref, o_ref, wide_in, wide_out):
        wide_in[...] = a_ref[...].astype(F32)
        wide_out[...] = jnp.zeros_like(wide_out)
        for j in range(blocks):
            wide_out[pl.ds(j * rows_out, real), :] = wide_in[pl.ds(j * rows_in, real), :]
        o_ref[...] = wide_out[...].astype(BF16)

    full = pl.BlockSpec((total, d), lambda i: (0, 0))
    return pl.pallas_call(
        body, name=name, grid=(1,), in_specs=[full], out_specs=full, out_shape=jax.ShapeDtypeStruct((total, d), BF16),
        scratch_shapes=[pltpu.VMEM((total, d), F32)] * 2,
        compiler_params=_params("arbitrary"),
    )(a)


def _mix_in_fwd(x, gain, w_in_t):
    t, d = x.shape
    tm = 1024
    pw, aw = POOL_WIDTH, ATTN_WIDTH

    def body(x_ref, g_ref, w_ref, hm_ref, pv_ref, q_ref, k_ref, v_ref, f_ref):
        xv = x_ref[...]
        hm = (xv * _rms_scale(xv) * g_ref[...]).astype(BF16)
        hm_ref[...] = hm
        pv_ref[...] = _dot_nt(hm, w_ref[pl.ds(0, pw), :])
        q_ref[...] = _dot_nt(hm, w_ref[pl.ds(pw, aw), :])
        k_ref[...] = _dot_nt(hm, w_ref[pl.ds(pw + aw, aw), :])
        v_ref[...] = _dot_nt(hm, w_ref[pl.ds(pw + 2 * aw, aw), :]).astype(BF16)
        f_ref[...] = _dot_nt(hm, w_ref[pl.ds(pw + 3 * aw, LANES), :])

    return pl.pallas_call(
        body, name="mix_in_fwd", grid=(t // tm,),
        in_specs=[_rows(tm, d), _resident((1, d)), _resident((MIX_PAD, d))],
        out_specs=[_rows(tm, d), _rows(tm, pw), _rows(tm, aw), _rows(tm, aw), _rows(tm, aw), _rows(tm, LANES)],
        out_shape=[jax.ShapeDtypeStruct((t, d), BF16), jax.ShapeDtypeStruct((t, pw), F32),
                   jax.ShapeDtypeStruct((t, aw), F32), jax.ShapeDtypeStruct((t, aw), F32),
                   jax.ShapeDtypeStruct((t, aw), BF16), jax.ShapeDtypeStruct((t, LANES), F32)],
        compiler_params=_params("arbitrary"),
    )(x, gain, w_in_t)


def _pool_fwd(pv, pool_w, pool_scale, gain, bsz, seq):
    ts = 512
    ns = seq // ts
    pw = POOL_WIDTH

    def body(pv_ref, w_ref, sc_ref, g_ref, pooled_ref, mixed_ref, y_ref, ext):
        s = pl.program_id(1)

        @pl.when(s == 0)
        def _():
            ext[pl.ds(0, POOL_HALO), :] = jnp.zeros((POOL_HALO, pw), F32)

        p = pv_ref[...]
        ext[pl.ds(POOL_HALO, ts), :] = p
        pos = s * ts + lax.broadcasted_iota(jnp.int32, (ts, 1), 0)
        parts = []
        for g, w in enumerate(POOL_WINDOWS):
            lanes = pl.ds(g * POOL_GROUP_DIM, POOL_GROUP_DIM)
            win = ext[pl.ds(POOL_HALO, ts), lanes]
            for i in range(1, w):
                win = win + ext[pl.ds(POOL_HALO - i, ts), lanes]
            cnt = jnp.minimum(pos + 1, w).astype(F32)
            pooled = (win / cnt - ext[pl.ds(POOL_HALO, ts), lanes]).astype(BF16)
            pooled_ref[:, lanes] = pooled
            parts.append(_dot(pooled, w_ref[g].astype(BF16)))
        mixed = jnp.concatenate(parts, axis=1)
        mixed_ref[...] = mixed
        pm = mixed * sc_ref[...]
        y_ref[...] = (pm * _rms_scale(pm) * g_ref[...]).astype(BF16)
        ext[pl.ds(0, POOL_HALO), :] = p[ts - POOL_HALO:, :]

    blk = pl.BlockSpec((ts, pw), lambda b, s: (b * ns + s, 0))
    t = bsz * seq
    return pl.pallas_call(
        body, name="pool_fwd", grid=(bsz, ns),
        in_specs=[blk, pl.BlockSpec((POOL_GROUPS, POOL_GROUP_DIM, POOL_GROUP_DIM), lambda b, s: (0, 0, 0)),
                  pl.BlockSpec((1, pw), lambda b, s: (0, 0)), pl.BlockSpec((1, pw), lambda b, s: (0, 0))],
        out_specs=[blk, blk, blk],
        out_shape=[jax.ShapeDtypeStruct((t, pw), BF16), jax.ShapeDtypeStruct((t, pw), F32),
                   jax.ShapeDtypeStruct((t, pw), BF16)],
        scratch_shapes=[pltpu.VMEM((POOL_HALO + ts, pw), F32)],
        compiler_params=_params("arbitrary", "arbitrary"),
    )(pv, pool_w, pool_scale, gain)


def _pool_bwd(dy, mixed, pooled, pool_w, pool_scale, gain, bsz, seq):
    ts = 512
    ns = seq // ts
    pw = POOL_WIDTH

    def body(dy_ref, mixed_ref, pooled_ref, w_ref, sc_ref, g_ref, dpv_ref, dw_ref, dsc_ref, dg_ref, ext):
        b = pl.program_id(0)
        sr = pl.program_id(1)
        s = ns - 1 - sr

        @pl.when(jnp.logical_and(b == 0, sr == 0))
        def _():
            dw_ref[...] = jnp.zeros_like(dw_ref)
            dsc_ref[...] = jnp.zeros_like(dsc_ref)
            dg_ref[...] = jnp.zeros_like(dg_ref)

        @pl.when(sr == 0)
        def _():
            ext[pl.ds(ts, POOL_HALO), :] = jnp.zeros((POOL_HALO, pw), F32)

        mixed = mixed_ref[...]
        sc = sc_ref[...]
        dpm, dgain = _rms_bwd(dy_ref[...], mixed * sc, g_ref[...])
        dg_ref[...] += dgain
        dsc_ref[...] += jnp.sum(dpm * mixed, axis=0, keepdims=True)
        dmixed = (dpm * sc).astype(BF16)
        pos = s * ts + lax.broadcasted_iota(jnp.int32, (ts, 1), 0)
        dpooled = []
        for g, w in enumerate(POOL_WINDOWS):
            lanes = pl.ds(g * POOL_GROUP_DIM, POOL_GROUP_DIM)
            dm = dmixed[:, g * POOL_GROUP_DIM:(g + 1) * POOL_GROUP_DIM]
            dw_ref[g] += _dot_tn(pooled_ref[:, lanes], dm)
            dp = _dot_nt(dm, w_ref[g].astype(BF16))
            dpooled.append(dp)
            cnt = jnp.minimum(pos + 1, w).astype(F32)
            ext[pl.ds(0, ts), lanes] = dp / cnt
        for g, w in enumerate(POOL_WINDOWS):
            lanes = pl.ds(g * POOL_GROUP_DIM, POOL_GROUP_DIM)
            win = ext[pl.ds(0, ts), lanes]
            for i in range(1, w):
                win = win + ext[pl.ds(i, ts), lanes]
            dpv_ref[:, lanes] = (win - dpooled[g]).astype(BF16)
        head = ext[pl.ds(0, POOL_HALO), :]
        ext[pl.ds(ts, POOL_HALO), :] = head

    blk = pl.BlockSpec((ts, pw), lambda b, s: (b * ns + (ns - 1 - s), 0))
    vec = pl.BlockSpec((1, pw), lambda b, s: (0, 0))
    wspec = pl.BlockSpec((POOL_GROUPS, POOL_GROUP_DIM, POOL_GROUP_DIM), lambda b, s: (0, 0, 0))
    t = bsz * seq
    return pl.pallas_call(
        body, name="pool_bwd", grid=(bsz, ns),
        in_specs=[blk, blk, blk, wspec, vec, vec],
        out_specs=[blk, wspec, vec, vec],
        out_shape=[jax.ShapeDtypeStruct((t, pw), BF16),
                   jax.ShapeDtypeStruct((POOL_GROUPS, POOL_GROUP_DIM, POOL_GROUP_DIM), F32),
                   jax.ShapeDtypeStruct((1, pw), F32), jax.ShapeDtypeStruct((1, pw), F32)],
        scratch_shapes=[pltpu.VMEM((ts + POOL_HALO, pw), F32)],
        compiler_params=_params("arbitrary", "arbitrary"),
    )(dy, mixed, pooled, pool_w, pool_scale, gain)


AUX_ONE = 64
AUX_F = 67

ATTN_PREP_ROWS = 512


def _seg_ones(width, seg):
    r = lax.broadcasted_iota(jnp.int32, (width, width), 0) // seg
    c = lax.broadcasted_iota(jnp.int32, (width, width), 1) // seg
    return (r == c).astype(BF16)


def _tri_ones(n, lower):
    r = lax.broadcasted_iota(jnp.int32, (n, n), 0)
    c = lax.broadcasted_iota(jnp.int32, (n, n), 1)
    return ((r >= c) if lower else (r <= c)).astype(BF16)


def _place_pieces(first_lane):
    r = lax.broadcasted_iota(jnp.int32, (3 * LANES, N_HEADS * LANES), 0)
    c = lax.broadcasted_iota(jnp.int32, (3 * LANES, N_HEADS * LANES), 1)
    piece, head = r // LANES, r % LANES
    return jnp.logical_and(head < N_HEADS, c == head * LANES + first_lane + piece).astype(BF16)


def _head_sums(x, seg_ones):
    return _dot(x.astype(BF16), seg_ones)


def _log_sigmoid(x):
    return jnp.minimum(x, 0.0) - jnp.log(1.0 + jnp.exp(-jnp.abs(x)))


def _attn_prep_fwd(q, k, f, b_forget, q_gain, k_gain, bsz, seq):
    ts = ATTN_PREP_ROWS
    ns = seq // ts
    aw = ATTN_WIDTH
    t = bsz * seq
    seg = _seg_ones(aw, HEAD_DIM)
    tri = _tri_ones(ts, True)

    def body(q_ref, k_ref, f_ref, bf_ref, gq_ref, gk_ref, seg_ref, tri_ref, place_ref, qp_ref, kp_ref, carry):
        s = pl.program_id(1)

        @pl.when(s == 0)
        def _():
            carry[...] = jnp.zeros_like(carry)

        logf = _log_sigmoid(f_ref[...] + bf_ref[...])
        hi, mid, lo = _split3(logf)
        tri_v = tri_ref[...]
        fc = _dot(tri_v, hi) + _dot(tri_v, mid) + _dot(tri_v, lo) + carry[pl.ds(0, 1), :]
        carry[pl.ds(0, 1), :] = fc[ts - 1:, :]
        pcs = jnp.concatenate(_split3(fc), axis=1)
        lane = lax.broadcasted_iota(jnp.int32, (1, LANES), 1)
        ones_q = jnp.logical_and(lane >= AUX_ONE, lane < AUX_ONE + 3).astype(F32)
        ones_k = jnp.logical_and(lane >= AUX_F, lane < AUX_F + 3).astype(F32)
        seg_v = seg_ref[...]
        placed = _dot(pcs, place_ref[...])

        def build(x_ref, g_ref, scale, out_ref, ones, for_keys):
            xv = x_ref[...]
            r = lax.rsqrt(_head_sums(xv * xv, seg_v) * (1.0 / HEAD_DIM) + EPS)
            xn = xv * r * g_ref[...] * scale
            for h in range(N_HEADS):
                pair = xn[:, (h // 2) * LANES:(h // 2 + 1) * LANES]
                feat = pair if h % 2 == 0 else pltpu.roll(pair, HEAD_DIM, 1)
                aux_h = placed[:, h * LANES:(h + 1) * LANES]
                if for_keys:
                    aux_h = -pltpu.roll(aux_h, LANES - (AUX_F - AUX_ONE), 1)
                out_ref[:, h * LANES:(h + 1) * LANES] = jnp.where(lane < HEAD_DIM, feat, aux_h + ones).astype(BF16)

        build(q_ref, gq_ref, 0.125, qp_ref, ones_q, False)
        build(k_ref, gk_ref, 1.0, kp_ref, ones_k, True)

    blk = pl.BlockSpec((ts, aw), lambda b, s: (b * ns + s, 0))
    fblk = pl.BlockSpec((ts, LANES), lambda b, s: (b * ns + s, 0))
    oblk = pl.BlockSpec((ts, N_HEADS * LANES), lambda b, s: (b * ns + s, 0))
    const = lambda shape: pl.BlockSpec(shape, lambda b, s: (0, 0))
    return pl.pallas_call(
        body, name="attn_prep_fwd", grid=(bsz, ns),
        in_specs=[blk, blk, fblk, const((1, LANES)), const((1, aw)), const((1, aw)), const((aw, aw)), const((ts, ts)),
                  const((3 * LANES, N_HEADS * LANES))],
        out_specs=[oblk, oblk],
        out_shape=[jax.ShapeDtypeStruct((t, N_HEADS * LANES), BF16)] * 2,
        scratch_shapes=[pltpu.VMEM((8, LANES), F32)],
        compiler_params=_params("arbitrary", "arbitrary"),
    )(q, k, f, b_forget, q_gain, k_gain, seg, tri, _place_pieces(AUX_F))


def _attn_prep_bwd(dqp, dkp, q, k, f, b_forget, q_gain, k_gain, bsz, seq):
    ts = ATTN_PREP_ROWS
    ns = seq // ts
    aw = ATTN_WIDTH
    t = bsz * seq
    seg = _seg_ones(aw, HEAD_DIM)
    tri = _tri_ones(ts, False)

    def body(dqp_ref, dkp_ref, q_ref, k_ref, f_ref, bf_ref, gq_ref, gk_ref, seg_ref, tri_ref,
             dq_ref, dk_ref, df_ref, dgq_ref, dgk_ref, dbf_ref, carry):
        b = pl.program_id(0)
        sr = pl.program_id(1)

        @pl.when(jnp.logical_and(b == 0, sr == 0))
        def _():
            dgq_ref[...] = jnp.zeros_like(dgq_ref)
            dgk_ref[...] = jnp.zeros_like(dgk_ref)
            dbf_ref[...] = jnp.zeros_like(dbf_ref)

        @pl.when(sr == 0)
        def _():
            carry[...] = jnp.zeros_like(carry)

        lane = lax.broadcasted_iota(jnp.int32, (1, LANES), 1)
        seg_v = seg_ref[...]

        def norm_bwd(dp_ref, x_ref, g_ref, scale, dx_ref, dgain_ref):
            parts = []
            for j in range(N_HEADS // 2):
                even = dp_ref[:, (2 * j) * LANES:(2 * j + 1) * LANES]
                odd = dp_ref[:, (2 * j + 1) * LANES:(2 * j + 2) * LANES]
                parts.append(jnp.where(lane < HEAD_DIM, even, pltpu.roll(odd, HEAD_DIM, 1)))
            dxn = jnp.concatenate(parts, axis=1) * scale
            xv = x_ref[...]
            r = lax.rsqrt(_head_sums(xv * xv, seg_v) * (1.0 / HEAD_DIM) + EPS)
            n = xv * r
            dgain_ref[...] += jnp.sum(dxn * n, axis=0, keepdims=True)
            dn = dxn * g_ref[...]
            m = _head_sums(dn * n, seg_v) * (1.0 / HEAD_DIM)
            dx_ref[...] = (r * (dn - n * m)).astype(BF16)

        norm_bwd(dqp_ref, q_ref, gq_ref, 0.125, dq_ref, dgq_ref)
        norm_bwd(dkp_ref, k_ref, gk_ref, 1.0, dk_ref, dgk_ref)

        dfc = jnp.zeros((ts, LANES), F32)
        for h in range(N_HEADS):
            cols = pl.ds(h * LANES, LANES)
            both = jnp.where(lane == AUX_F, dqp_ref[:, cols], 0.0) - jnp.where(lane == AUX_ONE, dkp_ref[:, cols], 0.0)
            dfc = jnp.where(lane == h, jnp.sum(both, axis=1, keepdims=True), dfc)
        hi, mid, lo = _split3(dfc)
        tri_v = tri_ref[...]
        dlogf = _dot(tri_v, hi) + _dot(tri_v, mid) + _dot(tri_v, lo) + carry[pl.ds(0, 1), :]
        carry[pl.ds(0, 1), :] = dlogf[0:1, :]
        df = jnp.where(lane < N_HEADS, dlogf * jax.nn.sigmoid(-(f_ref[...] + bf_ref[...])), 0.0)
        df_ref[...] = df.astype(BF16)
        dbf_ref[...] += jnp.sum(df, axis=0, keepdims=True)

    rev = lambda b, s: (b * ns + (ns - 1 - s), 0)
    blk = pl.BlockSpec((ts, aw), rev)
    fblk = pl.BlockSpec((ts, LANES), rev)
    pblk = pl.BlockSpec((ts, N_HEADS * LANES), rev)
    const = lambda shape: pl.BlockSpec(shape, lambda b, s: (0, 0))
    return pl.pallas_call(
        body, name="attn_prep_bwd", grid=(bsz, ns),
        in_specs=[pblk, pblk, blk, blk, fblk, const((1, LANES)), const((1, aw)), const((1, aw)), const((aw, aw)),
                  const((ts, ts))],
        out_specs=[blk, blk, fblk, const((1, aw)), const((1, aw)), const((1, LANES))],
        out_shape=[jax.ShapeDtypeStruct((t, aw), BF16), jax.ShapeDtypeStruct((t, aw), BF16),
                   jax.ShapeDtypeStruct((t, LANES), BF16), jax.ShapeDtypeStruct((1, aw), F32),
                   jax.ShapeDtypeStruct((1, aw), F32), jax.ShapeDtypeStruct((1, LANES), F32)],
        scratch_shapes=[pltpu.VMEM((8, LANES), F32)],
        compiler_params=_params("arbitrary", "arbitrary"),
    )(dqp, dkp, q, k, f, b_forget, q_gain, k_gain, seg, tri)


ATTN_BLOCK = 1024
HEAD_PAIRS = N_HEADS // 2


def _flash_fwd(qp, kp, v, bsz, seq):
    tq = ATTN_BLOCK
    half = tq // 2
    nq = seq // tq
    t = bsz * seq

    def body(q_ref, k_ref, v_ref, o_ref, lse_ref, m_sc, l_sc, acc_sc):
        i = pl.program_id(2)
        m_sc[...] = jnp.full(m_sc.shape, -jnp.inf, F32)
        l_sc[...] = jnp.zeros_like(l_sc)
        acc_sc[...] = jnp.zeros_like(acc_sc)
        lane = lax.broadcasted_iota(jnp.int32, (1, LANES), 1)
        low = lane < HEAD_DIM

        def tile(q0, qn, k_start, kn, k0=None):
            qs = pl.ds(q0, qn)
            ks = pl.ds(k_start, kn)
            vv = v_ref[ks, :]
            for h in range(2):
                mine = low if h == 0 else jnp.logical_not(low)
                cols = pl.ds(h * LANES, LANES)
                s = _dot_nt(q_ref[qs, cols], k_ref[ks, cols])
                if k0 is not None:
                    row = lax.broadcasted_iota(jnp.int32, (qn, kn), 0) + q0
                    col = lax.broadcasted_iota(jnp.int32, (qn, kn), 1) + k0
                    s = jnp.where(row >= col, s, -jnp.inf)
                m_prev = m_sc[h, qs, :]
                m_new = jnp.maximum(m_prev, jnp.max(s, axis=1, keepdims=True))
                p = jnp.exp(s - jnp.tile(m_new, (1, kn // LANES)))
                alpha = jnp.exp(m_prev - m_new)
                l_sc[h, qs, :] = alpha * l_sc[h, qs, :] + jnp.sum(p, axis=1, keepdims=True)
                m_sc[h, qs, :] = m_new
                pv = _dot(p.astype(BF16), jnp.where(mine, vv, jnp.zeros_like(vv)))
                acc_sc[qs, :] = acc_sc[qs, :] * jnp.where(mine, alpha, 1.0) + pv

        def below_diagonal(j, carry):
            tile(0, tq, pl.multiple_of(j * tq, tq), tq)
            return carry

        lax.fori_loop(0, i, below_diagonal, 0)
        diagonal = pl.multiple_of(i * tq, tq)
        tile(0, tq, diagonal, half, k0=0)
        tile(half, half, diagonal + half, half, k0=half)
        l = jnp.where(low, l_sc[0], l_sc[1])
        m = jnp.where(low, m_sc[0], m_sc[1])
        o_ref[...] = acc_sc[...] / l
        lse_ref[...] = m + jnp.log(l)

    qspec = pl.BlockSpec((tq, 2 * LANES), lambda b, hp, i: (b * nq + i, hp))
    kspec = pl.BlockSpec((seq, 2 * LANES), lambda b, hp, i: (b, hp))
    vspec = pl.BlockSpec((seq, LANES), lambda b, hp, i: (b, hp))
    ospec = pl.BlockSpec((tq, LANES), lambda b, hp, i: (b * nq + i, hp))
    return pl.pallas_call(
        body, name="flash_fwd", grid=(bsz, HEAD_PAIRS, nq),
        in_specs=[qspec, kspec, vspec], out_specs=[ospec, ospec],
        out_shape=[jax.ShapeDtypeStruct((t, ATTN_WIDTH), F32), jax.ShapeDtypeStruct((t, ATTN_WIDTH), F32)],
        scratch_shapes=[pltpu.VMEM((2, tq, LANES), F32), pltpu.VMEM((2, tq, LANES), F32), pltpu.VMEM((tq, LANES), F32)],
        compiler_params=_params("arbitrary", "arbitrary", "arbitrary"),
    )(qp, kp, v)


def _flash_bwd(qp, kp, v, o, do, lse, after, bsz, seq):
    tq = ATTN_BLOCK
    half = tq // 2
    nq = seq // tq
    t = bsz * seq

    def body(q_ref, k_ref, v_ref, o_ref, do_ref, lse_ref, after_ref, dq_ref, dk_ref, dv_ref, dk_acc, dv_acc):
        j = pl.program_id(2)

        @pl.when(j == 0)
        def _():
            dq_ref[...] = jnp.zeros_like(dq_ref)

        dk_acc[...] = jnp.zeros_like(dk_acc)
        dv_acc[...] = jnp.zeros_like(dv_acc)
        lane = lax.broadcasted_iota(jnp.int32, (1, LANES), 1)
        low = lane < HEAD_DIM

        def tile(q_start, qn, k0, kn, q0=None):
            rows = pl.ds(q_start, qn)
            ks = pl.ds(k0, kn)
            dov = do_ref[rows, :]
            dd = dov * o_ref[rows, :]
            dob = dov.astype(BF16)
            vv = v_ref[ks, :]
            lse_v = lse_ref[rows, :]
            for h in range(2):
                mine = low if h == 0 else jnp.logical_not(low)
                cols = pl.ds(h * LANES, LANES)
                qh = q_ref[rows, cols]
                kh = k_ref[ks, cols]
                s = _dot_nt(qh, kh)
                lse_h = jnp.where(mine, lse_v, pltpu.roll(lse_v, HEAD_DIM, 1))
                p = jnp.exp(s - jnp.tile(lse_h, (1, kn // LANES)))
                if q0 is not None:
                    row = lax.broadcasted_iota(jnp.int32, (qn, kn), 0) + q0
                    col = lax.broadcasted_iota(jnp.int32, (qn, kn), 1) + k0
                    p = jnp.where(row >= col, p, 0.0)
                delta = jnp.sum(jnp.where(mine, dd, 0.0), axis=1, keepdims=True)
                dp = _dot_nt(dob, jnp.where(mine, vv, jnp.zeros_like(vv)))
                ds = (p * (dp - delta)).astype(BF16)
                dv_acc[ks, :] += jnp.where(mine, _dot_tn(p.astype(BF16), dob), 0.0)
                dk_acc[ks, cols] += _dot_tn(ds, qh)
                dq_ref[rows, cols] += _dot(ds, kh)

        def above_diagonal(i, carry):
            tile(pl.multiple_of(i * tq, tq), tq, 0, tq)
            return carry

        diagonal = pl.multiple_of(j * tq, tq)
        tile(diagonal, tq, 0, half, q0=0)
        tile(diagonal + half, half, half, half, q0=half)
        lax.fori_loop(j + 1, nq, above_diagonal, 0)
        dk_ref[...] = dk_acc[...]
        dv_ref[...] = dv_acc[...].astype(BF16)

    qspec = pl.BlockSpec((seq, 2 * LANES), lambda b, hp, j: (b, hp))
    kspec = pl.BlockSpec((tq, 2 * LANES), lambda b, hp, j: (b * nq + j, hp))
    vspec = pl.BlockSpec((tq, LANES), lambda b, hp, j: (b * nq + j, hp))
    ospec = pl.BlockSpec((seq, LANES), lambda b, hp, j: (b, hp))
    return pl.pallas_call(
        body, name="flash_bwd", grid=(bsz, HEAD_PAIRS, nq),
        in_specs=[qspec, kspec, vspec, ospec, ospec, ospec, ORDER_ONLY], out_specs=[qspec, kspec, vspec],
        out_shape=[jax.ShapeDtypeStruct((t, N_HEADS * LANES), F32), jax.ShapeDtypeStruct((t, N_HEADS * LANES), F32),
                   jax.ShapeDtypeStruct((t, ATTN_WIDTH), BF16)],
        scratch_shapes=[pltpu.VMEM((tq, 2 * LANES), F32), pltpu.VMEM((tq, LANES), F32)],
        compiler_params=_params("arbitrary", "arbitrary", "arbitrary"),
    )(qp, kp, v, o, do, lse, after)


def _mix_out_fwd(o, y_pool, x, gain, w_out):
    t, d = x.shape
    tm = 1024
    pw, aw = POOL_WIDTH, ATTN_WIDTH

    def body(o_ref, yp_ref, x_ref, g_ref, w_ref, ycat_ref, y_ref):
        ov = o_ref[...]
        ya = (ov * _rms_scale(ov) * g_ref[...]).astype(BF16)
        ycat = jnp.concatenate([yp_ref[...], ya], axis=1)
        ycat_ref[...] = ycat
        y_ref[...] = x_ref[...] + _dot(ycat, w_ref[...])

    return pl.pallas_call(
        body, name="mix_out_fwd", grid=(t // tm,),
        in_specs=[_rows(tm, aw), _rows(tm, pw), _rows(tm, d), _resident((1, aw)), _resident((pw + aw, d))],
        out_specs=[_rows(tm, pw + aw), _rows(tm, d)],
        out_shape=[jax.ShapeDtypeStruct((t, pw + aw), BF16), jax.ShapeDtypeStruct((t, d), F32)],
        compiler_params=_params("arbitrary"),
    )(o, y_pool, x, gain, w_out)


def _mix_out_bwd(dx, o, ycat, gain, w_out):
    t, d = dx.shape
    tm = 1024
    nm = t // tm
    pw, aw = POOL_WIDTH, ATTN_WIDTH

    def body(dx_ref, o_ref, ycat_ref, g_ref, w_ref, dw_ref, dyp_ref, do_ref, dg_ref, acc):
        i = pl.program_id(0)

        @pl.when(i == 0)
        def _():
            dg_ref[...] = jnp.zeros_like(dg_ref)
            acc[...] = jnp.zeros_like(acc)

        dxb = dx_ref[...].astype(BF16)
        acc[...] += _dot_tn(ycat_ref[...], dxb)
        dyp_ref[...] = _dot_nt(dxb, w_ref[pl.ds(0, pw), :])
        dya = _dot_nt(dxb, w_ref[pl.ds(pw, aw), :])
        do, dgain = _rms_bwd(dya, o_ref[...], g_ref[...])
        do_ref[...] = do
        dg_ref[...] += dgain

        @pl.when(i == nm - 1)
        def _():
            dw_ref[...] = acc[...].astype(BF16)

    return pl.pallas_call(
        body, name="mix_out_bwd", grid=(nm,),
        in_specs=[_rows(tm, d), _rows(tm, aw), _rows(tm, pw + aw), _resident((1, aw)), _resident((pw + aw, d))],
        out_specs=[pl.BlockSpec((pw + aw, d), lambda i: (0, 0)), _rows(tm, pw), _rows(tm, aw),
                   pl.BlockSpec((1, aw), lambda i: (0, 0))],
        out_shape=[jax.ShapeDtypeStruct((pw + aw, d), BF16), jax.ShapeDtypeStruct((t, pw), F32),
                   jax.ShapeDtypeStruct((t, aw), F32), jax.ShapeDtypeStruct((1, aw), F32)],
        scratch_shapes=[pltpu.VMEM((pw + aw, d), F32)],
        compiler_params=_params("arbitrary"),
    )(dx, o, ycat, gain, w_out)


def _mix_in_bwd(dpv, dq, dk, dv, df, hm, x, dx_res, gain, w_in_t):
    t, d = x.shape
    tm = 512
    nm = t // tm
    pw, aw = POOL_WIDTH, ATTN_WIDTH

    def body(dpv_ref, dq_ref, dk_ref, dv_ref, df_ref, hm_ref, x_ref, dxr_ref, g_ref, w_ref, dw_ref, dx_ref, dxh_ref,
             dg_ref, acc):
        i = pl.program_id(0)

        @pl.when(i == 0)
        def _():
            dg_ref[...] = jnp.zeros_like(dg_ref)
            acc[...] = jnp.zeros_like(acc)

        dh = jnp.concatenate([dpv_ref[...], dq_ref[...], dk_ref[...], dv_ref[...], df_ref[...]], axis=1)
        acc[...] += _dot_tn(dh, hm_ref[...])
        dx, dgain = _rms_bwd(_dot(dh, w_ref[...]), x_ref[...], g_ref[...])
        dx = dxr_ref[...] + dx
        dx_ref[...] = dx
        dxh_ref[...] = (0.5 * dx).astype(BF16)
        dg_ref[...] += dgain

        @pl.when(i == nm - 1)
        def _():
            dw_ref[...] = acc[...].astype(BF16)

    return pl.pallas_call(
        body, name="mix_in_bwd", grid=(nm,),
        in_specs=[_rows(tm, pw), _rows(tm, aw), _rows(tm, aw), _rows(tm, aw), _rows(tm, LANES), _rows(tm, d),
                  _rows(tm, d), _rows(tm, d), _resident((1, d)), _resident((MIX_PAD, d))],
        out_specs=[pl.BlockSpec((MIX_PAD, d), lambda i: (0, 0)), _rows(tm, d), _rows(tm, d),
                   pl.BlockSpec((1, d), lambda i: (0, 0))],
        out_shape=[jax.ShapeDtypeStruct((MIX_PAD, d), BF16), jax.ShapeDtypeStruct((t, d), F32),
                   jax.ShapeDtypeStruct((t, d), BF16), jax.ShapeDtypeStruct((1, d), F32)],
        scratch_shapes=[pltpu.VMEM((MIX_PAD, d), F32)],
        compiler_params=_params("arbitrary"),
    )(dpv, dq, dk, dv, df, hm, x, dx_res, gain, w_in_t)


MESH_IDS = pl.DeviceIdType.MESH


def _me():
    return lax.axis_index("x"), lax.axis_index("y"), lax.axis_index("c")


def _peer(x, y, c, p):
    px = 1 - x if p & 4 else x
    py = 1 - y if p & 2 else y
    pc = 1 - c if p & 1 else c
    return (px, py, pc), 4 * px + 2 * py + pc


HBM_SPEC = pl.BlockSpec(memory_space=pltpu.HBM)
SEM_SPEC = pl.BlockSpec(memory_space=pltpu.SEMAPHORE)
SPLIT_COPY = pltpu.CompilerParams(has_side_effects=pltpu.SideEffectType.DATAFLOW_SIDE_EFFECTING)
PEERS = N_DEV - 1


def _hbm(a):
    return pltpu.with_memory_space_constraint(a, pltpu.HBM)


def _row_block(ref, dev, rows):
    return ref.at[pl.ds(pl.multiple_of(dev * rows, BF16_ROWS), rows)]


def _copy_ends(gather, src, land, me, peer_id):
    if gather:
        rows = src.shape[0]
        return src, _row_block(land, me, rows), _row_block(land, peer_id, rows), src, _row_block(land, me, rows)
    rows = src.shape[0] // N_DEV
    return (_row_block(src, peer_id, rows), land.at[me], land.at[peer_id], _row_block(src, me, rows), land.at[me])


def _land_shape(gather, s):
    return (N_DEV * s.shape[0], s.shape[1]) if gather else (N_DEV, s.shape[0] // N_DEV, s.shape[1])


SIBLING = 1
SAME_CORE_PEERS = (2, 4, 6)
RELAYS = len(SAME_CORE_PEERS)


def _copies_start(groups, gather, name, after=None, relayed=()):
    flat = [s for g in groups for s in g]
    n, ng = len(flat), len(groups)
    lands = [lax.empty(_land_shape(gather, s), s.dtype) for s in flat]
    n_in = 2 * n + (after is not None)

    def body(*refs):
        ins, lnd = refs[:n], refs[n:2 * n]
        sems = refs[n_in:n_in + 2 * ng]
        token = refs[-1]
        x, y, c = _me()
        me = 4 * x + 2 * y + c
        w = 0
        for gi, g in enumerate(groups):
            for k in range(len(g)):
                for p in ((SIBLING,) + SAME_CORE_PEERS if gi in relayed else range(1, N_DEV)):
                    peer, peer_id = _peer(x, y, c, p)
                    src, dst, _, _, _ = _copy_ends(gather, ins[w], lnd[w], me, peer_id)
                    pltpu.make_async_remote_copy(src, dst, sems[2 * gi].at[k * PEERS + p - 1],
                                                 sems[2 * gi + 1].at[k * PEERS + p - 1], device_id=peer,
                                                 device_id_type=MESH_IDS).start()
                w += 1
        token[...] = jnp.zeros_like(token)

    sem_shapes = []
    for g in groups:
        sem_shapes += [pltpu.SemaphoreType.DMA((len(g) * PEERS,))] * 2
    out = pl.pallas_call(
        body, name=name,
        out_shape=(*sem_shapes, *[pltpu.HBM(s.shape, s.dtype) for s in flat],
                   *[pltpu.HBM(l.shape, l.dtype) for l in lands], jax.ShapeDtypeStruct((8, LANES), F32)),
        in_specs=[HBM_SPEC] * (2 * n) + [pl.BlockSpec(memory_space=pl.ANY)] * (after is not None),
        out_specs=(*[SEM_SPEC] * (2 * ng), *[HBM_SPEC] * (2 * n), pl.BlockSpec(memory_space=pltpu.VMEM)),
        input_output_aliases={i: 2 * ng + i for i in range(2 * n)},
        compiler_params=SPLIT_COPY,
    )(*[_hbm(s) for s in flat], *[_hbm(l) for l in lands], *([after] if after is not None else []))
    sems, thru, token = out[:2 * ng], out[2 * ng:2 * ng + 2 * n], out[-1]
    res, w = [], 0
    for gi, g in enumerate(groups):
        res.append((sems[2 * gi], sems[2 * gi + 1], list(thru[w:w + len(g)]), list(thru[n + w:n + w + len(g)])))
        w += len(g)
    return res, token


def _copies_wait(started, gather, after, name):
    send, recv, srcs, lands = started
    n = len(srcs)
    after = list(after) if isinstance(after, (list, tuple)) else [after]

    own_shapes = [s.shape if gather else (s.shape[0] // N_DEV, s.shape[1]) for s in srcs]

    def body(*refs):
        ins, lnd = refs[:n], refs[n:2 * n]
        send_sems, recv_sems = refs[2 * n], refs[2 * n + 1]
        bounce, in_sems, out_sems = refs[-n - 2:-2], refs[-2], refs[-1]
        x, y, c = _me()
        me = 4 * x + 2 * y + c
        ends = [_copy_ends(gather, ins[w], lnd[w], me, me)[3:] for w in range(n)]
        loads = [pltpu.make_async_copy(ends[w][0], bounce[w], in_sems.at[w]) for w in range(n)]
        stores = [pltpu.make_async_copy(bounce[w], ends[w][1], out_sems.at[w]) for w in range(n)]
        for cp in loads:
            cp.start()
        for w in range(n):
            loads[w].wait()
            stores[w].start()
        for w in range(n):
            for p in range(1, N_DEV):
                peer, peer_id = _peer(x, y, c, p)
                src, _, arrival, _, _ = _copy_ends(gather, ins[w], lnd[w], me, peer_id)
                cp = pltpu.make_async_remote_copy(src, arrival, send_sems.at[w * PEERS + p - 1],
                                                  recv_sems.at[w * PEERS + p - 1], device_id=peer,
                                                  device_id_type=MESH_IDS)
                cp.wait_send()
                cp.wait_recv()
        for cp in stores:
            cp.wait()

    out = pl.pallas_call(
        body, name=name,
        out_shape=(*[pltpu.HBM(s.shape, s.dtype) for s in srcs], *[pltpu.HBM(l.shape, l.dtype) for l in lands]),
        in_specs=[HBM_SPEC] * (2 * n) + [SEM_SPEC, SEM_SPEC] + [pl.BlockSpec(memory_space=pl.ANY)] * len(after),
        out_specs=[HBM_SPEC] * (2 * n),
        input_output_aliases={i: i for i in range(2 * n)},
        scratch_shapes=[*[pltpu.VMEM(shape, s.dtype) for shape, s in zip(own_shapes, srcs)],
                        pltpu.SemaphoreType.DMA((n,)), pltpu.SemaphoreType.DMA((n,))],
        compiler_params=SPLIT_COPY,
    )(*srcs, *lands, send, recv, *after)
    return list(out[n:])


def _relay_to_sibling(started, name, after):
    send, recv, srcs, lands = started
    n = len(srcs)
    after = [after]

    def body(*refs):
        ins, lnd = refs[:n], refs[n:2 * n]
        send_sems, recv_sems = refs[2 * n], refs[2 * n + 1]
        relay_send, relay_recv = refs[2 * n + 2 + len(after)], refs[2 * n + 3 + len(after)]
        x, y, c = _me()
        sibling, _ = _peer(x, y, c, SIBLING)
        for w in range(n):
            rows = ins[w].shape[0]
            for k, p in enumerate(SAME_CORE_PEERS):
                peer, peer_id = _peer(x, y, c, p)
                arrived = _row_block(lnd[w], peer_id, rows)
                first = pltpu.make_async_remote_copy(ins[w], arrived, send_sems.at[w * PEERS + p - 1],
                                                     recv_sems.at[w * PEERS + p - 1], device_id=peer,
                                                     device_id_type=MESH_IDS)
                first.wait_recv()
                pltpu.make_async_remote_copy(arrived, arrived, relay_send.at[w * RELAYS + k],
                                             relay_recv.at[w * RELAYS + k], device_id=sibling,
                                             device_id_type=MESH_IDS).start()
                first.wait_send()

    sems = pltpu.SemaphoreType.DMA((n * RELAYS,))
    out = pl.pallas_call(
        body, name=name,
        out_shape=(sems, sems, *[pltpu.HBM(s.shape, s.dtype) for s in srcs], *[pltpu.HBM(l.shape, l.dtype) for l in lands]),
        in_specs=[HBM_SPEC] * (2 * n) + [SEM_SPEC, SEM_SPEC] + [pl.BlockSpec(memory_space=pl.ANY)] * len(after),
        out_specs=(SEM_SPEC, SEM_SPEC, *[HBM_SPEC] * (2 * n)),
        input_output_aliases={i: 2 + i for i in range(2 * n)},
        compiler_params=SPLIT_COPY,
    )(*srcs, *lands, send, recv, *after)
    return send, recv, out[0], out[1], list(out[2:2 + n]), list(out[2 + n:])


def _relayed_wait(relayed, after, name):
    send, recv, relay_send, relay_recv, srcs, lands = relayed
    n = len(srcs)
    after = list(after) if isinstance(after, (list, tuple)) else [after]

    def body(*refs):
        ins, lnd = refs[:n], refs[n:2 * n]
        send_sems, recv_sems, relay_send_sems, relay_recv_sems = refs[2 * n:2 * n + 4]
        bounce, in_sems, out_sems = refs[-n - 2:-2], refs[-2], refs[-1]
        x, y, c = _me()
        me = 4 * x + 2 * y + c
        sibling, sibling_id = _peer(x, y, c, SIBLING)
        loads = [pltpu.make_async_copy(ins[w], bounce[w], in_sems.at[w]) for w in range(n)]
        stores = [pltpu.make_async_copy(bounce[w], _row_block(lnd[w], me, ins[w].shape[0]), out_sems.at[w])
                  for w in range(n)]
        for cp in loads:
            cp.start()
        for w in range(n):
            loads[w].wait()
            stores[w].start()
        for w in range(n):
            rows = ins[w].shape[0]
            direct = pltpu.make_async_remote_copy(ins[w], _row_block(lnd[w], sibling_id, rows),
                                                  send_sems.at[w * PEERS + SIBLING - 1],
                                                  recv_sems.at[w * PEERS + SIBLING - 1], device_id=sibling,
                                                  device_id_type=MESH_IDS)
            direct.wait_send()
            direct.wait_recv()
            for k, p in enumerate(SAME_CORE_PEERS):
                _, sent_id = _peer(x, y, c, p)
                _, got_id = _peer(x, y, c, p + SIBLING)
                relay = pltpu.make_async_remote_copy(_row_block(lnd[w], sent_id, rows), _row_block(lnd[w], got_id, rows),
                                                     relay_send_sems.at[w * RELAYS + k],
                                                     relay_recv_sems.at[w * RELAYS + k], device_id=sibling,
                                                     device_id_type=MESH_IDS)
                relay.wait_send()
                relay.wait_recv()
        for cp in stores:
            cp.wait()

    out = pl.pallas_call(
        body, name=name,
        out_shape=(*[pltpu.HBM(s.shape, s.dtype) for s in srcs], *[pltpu.HBM(l.shape, l.dtype) for l in lands]),
        in_specs=[HBM_SPEC] * (2 * n) + [SEM_SPEC] * 4 + [pl.BlockSpec(memory_space=pl.ANY)] * len(after),
        out_specs=[HBM_SPEC] * (2 * n),
        input_output_aliases={i: i for i in range(2 * n)},
        scratch_shapes=[*[pltpu.VMEM(s.shape, s.dtype) for s in srcs],
                        pltpu.SemaphoreType.DMA((n,)), pltpu.SemaphoreType.DMA((n,))],
        compiler_params=SPLIT_COPY,
    )(*srcs, *lands, send, recv, relay_send, relay_recv, *after)
    return list(out[n:])


def _adamw_update(w, g, m, v):
    nm = ADAM_B1 * m + (1.0 - ADAM_B1) * g
    nv = ADAM_B2 * v + (1.0 - ADAM_B2) * (g * g)
    m_hat = nm / (1.0 - ADAM_B1 ** ADAM_STEP)
    v_hat = nv / (1.0 - ADAM_B2 ** ADAM_STEP)
    return -ADAM_LR * (m_hat / (jnp.sqrt(v_hat) + ADAM_EPS) + ADAM_WD * w), nm, nv


SUM_ADAMW_COLS = 512


def _sum_adamw(parts, w, m, v, name):
    _, rows, d = parts.shape
    n = w.shape[0]
    tc = SUM_ADAMW_COLS

    def body(p_ref, w_ref, m_ref, v_ref, g_ref, d_ref, nm_ref, nv_ref):
        g = p_ref[0].astype(F32)
        for dev in range(1, N_DEV):
            g = g + p_ref[dev].astype(F32)
        g = g[:n]
        g_ref[...] = g
        d_ref[...], nm_ref[...], nv_ref[...] = _adamw_update(w_ref[...], g, m_ref[...], v_ref[...])

    spec = pl.BlockSpec((n, tc), lambda j: (0, j))
    shape = jax.ShapeDtypeStruct((n, d), F32)
    return pl.pallas_call(
        body, name=name, grid=(d // tc,),
        in_specs=[pl.BlockSpec((N_DEV, rows, tc), lambda j: (0, 0, j)), spec, spec, spec],
        out_specs=[spec] * 4, out_shape=[shape] * 4,
        compiler_params=_params("arbitrary"),
    )(parts, w, m, v)


def _pad_rows(a, rows):
    return jnp.pad(a, ((0, rows - a.shape[0]), (0, 0)))


def _row1(vec, width=D_MODEL):
    return jnp.pad(vec.reshape(1, -1), ((0, 0), (0, width - vec.shape[-1])))


COLUMN_SHARDED = ("ffn1_w_gate", "ffn1_w_up", "w_in", "ffn2_w_gate", "ffn2_w_up")
VEC_NAMES = ("ffn1_norm", "mix_norm", "ffn2_norm", "b_forget", "pool_scale", "q_norm", "k_norm", "out_norm_pool",
             "out_norm_attn")
VEC_ROWS = 16
LOSS_ROW = len(VEC_NAMES)


def _pack_vector_grads(parts, loss_part, name):
    names = [n for n in VEC_NAMES if n in parts]
    extra = [] if loss_part is None else [loss_part]

    def body(*refs):
        out_ref = refs[-1]
        out_ref[...] = jnp.zeros_like(out_ref)
        lane = lax.broadcasted_iota(jnp.int32, (1, LANES), 1)
        for n, ref in zip(names, refs):
            val = ref[...]
            if n in ("q_norm", "k_norm"):
                val = val[:, 0:LANES] + val[:, LANES:2 * LANES] + val[:, 2 * LANES:3 * LANES] + val[:, 3 * LANES:]
                val = jnp.where(lane < HEAD_DIM, val + pltpu.roll(val, HEAD_DIM, 1), 0.0)
            out_ref[pl.ds(VEC_NAMES.index(n), 1), pl.ds(0, val.shape[1])] = val
        if extra:
            out_ref[pl.ds(LOSS_ROW, 1), pl.ds(0, 1)] = refs[len(names)][...]

    vmem = pl.BlockSpec(memory_space=pltpu.VMEM)
    return pl.pallas_call(
        body, name=name, in_specs=[vmem] * (len(names) + len(extra)), out_specs=vmem,
        out_shape=jax.ShapeDtypeStruct((VEC_ROWS, D_MODEL), F32),
    )(*[parts[n] for n in names], *extra)


def _small_adamw(vec_all, pool_all, vec_params, pool_params):
    nv = len(vec_params)
    pool_rows = pool_params[0].shape[0]

    def body(*refs):
        vec_ref, pool_ref = refs[0], refs[1]
        ins = refs[2:2 + 3 * nv + 3]
        outs = refs[2 + 3 * nv + 3:-1]
        rows = refs[-1]
        total = vec_ref[pl.ds(0, VEC_ROWS), :]
        for dev in range(1, N_DEV):
            total = total + vec_ref[pl.ds(dev * VEC_ROWS, VEC_ROWS), :]
        rows[...] = total
        outs[4 * nv + 4][...] = rows[pl.ds(LOSS_ROW, 1), pl.ds(0, 1)]
        for i in range(nv):
            w_ref, m_ref, v_ref = ins[3 * i:3 * i + 3]
            g = rows[pl.ds(i, 1), pl.ds(0, w_ref.shape[1])]
            outs[4 * i][...] = g
            outs[4 * i + 1][...], outs[4 * i + 2][...], outs[4 * i + 3][...] = _adamw_update(
                w_ref[...], g, m_ref[...], v_ref[...])
        g = pool_ref[pl.ds(0, pool_rows), :].astype(F32)
        for dev in range(1, N_DEV):
            g = g + pool_ref[pl.ds(dev * pool_rows, pool_rows), :].astype(F32)
        w_ref, m_ref, v_ref = ins[3 * nv:]
        outs[4 * nv][...] = g
        outs[4 * nv + 1][...], outs[4 * nv + 2][...], outs[4 * nv + 3][...] = _adamw_update(
            w_ref[...], g, m_ref[...], v_ref[...])

    vmem = pl.BlockSpec(memory_space=pltpu.VMEM)
    flat = [a for trio in vec_params for a in trio] + list(pool_params)
    out_shape = []
    for trio in list(vec_params) + [pool_params]:
        out_shape += [jax.ShapeDtypeStruct(trio[0].shape, F32)] * 4
    out_shape.append(jax.ShapeDtypeStruct((1, 1), F32))
    return pl.pallas_call(
        body, name="adamw_small", in_specs=[vmem] * (2 + len(flat)), out_specs=[vmem] * len(out_shape),
        out_shape=out_shape, scratch_shapes=[pltpu.VMEM((VEC_ROWS, D_MODEL), F32)],
    )(vec_all, pool_all, *flat)


def kernel(x, ffn1_norm, ffn1_w_gate, ffn1_w_up, ffn1_w_down, mix_norm, w_in, b_forget, pool_w, pool_scale, q_norm, k_norm, out_norm_pool, out_norm_attn, w_out, ffn2_norm, ffn2_w_gate, ffn2_w_up, ffn2_w_down, loss_target, m_ffn1_norm, m_ffn1_w_gate, m_ffn1_w_up, m_ffn1_w_down, m_mix_norm, m_w_in, m_b_forget, m_pool_w, m_pool_scale, m_q_norm, m_k_norm, m_out_norm_pool, m_out_norm_attn, m_w_out, m_ffn2_norm, m_ffn2_w_gate, m_ffn2_w_up, m_ffn2_w_down, v_ffn1_norm, v_ffn1_w_gate, v_ffn1_w_up, v_ffn1_w_down, v_mix_norm, v_w_in, v_b_forget, v_pool_w, v_pool_scale, v_q_norm, v_k_norm, v_out_norm_pool, v_out_norm_attn, v_w_out, v_ffn2_norm, v_ffn2_w_gate, v_ffn2_w_up, v_ffn2_w_down):
    bsz, seq, d = x.shape
    t = bsz * seq
    x0 = x.reshape(t, d)
    target = loss_target.reshape(t, d)
    in_rows = -(-w_in.shape[1] // BF16_ROWS) * BF16_ROWS

    slabs = [s.astype(BF16) for s in (ffn1_w_gate.T, ffn1_w_up.T, ffn1_w_down, _pad_rows(w_in.T, in_rows), w_out,
                                       ffn2_w_gate.T, ffn2_w_up.T, ffn2_w_down)]
    first, started = _copies_start([slabs[0:2]], True, "gather_start_first", relayed=(0,))
    rest, started = _copies_start([slabs[2:3], slabs[3:4], slabs[4:5], slabs[5:8]], True, "gather_start", after=started,
                                  relayed=(3,))
    gathers = first + rest

    g1, gm, g2 = ffn1_norm.reshape(1, d), mix_norm.reshape(1, d), ffn2_norm.reshape(1, d)
    bf_row = _row1(b_forget, LANES)
    gq = jnp.tile(q_norm, N_HEADS).reshape(1, ATTN_WIDTH)
    gk = jnp.tile(k_norm, N_HEADS).reshape(1, ATTN_WIDTH)
    scale_row = pool_scale.reshape(1, POOL_WIDTH)
    gp, ga = out_norm_pool.reshape(1, POOL_WIDTH), out_norm_attn.reshape(1, ATTN_WIDTH)

    wg1, wu1 = _relayed_wait(_relay_to_sibling(gathers[0], "gather_relay_ffn1_up", started), started,
                             "gather_wait_ffn1_up")
    h1, sa1, sb1, s1 = _ffn_up(x0, g1, wg1, wu1, "ffn1_up")
    (wd1,) = _copies_wait(gathers[1], True, s1, "gather_wait_ffn1_down")
    (x1,) = _ffn_down(s1, wd1, x0, None, "ffn1_down")
    (win_g,) = _copies_wait(gathers[2], True, x1, "gather_wait_w_in")
    win_t = _repack_rows(win_g, in_rows, w_in.shape[1], N_DEV, "w_in_rows")
    hm, pv, q, k, v, f = _mix_in_fwd(x1, gm, win_t)
    pooled, mixed, y_pool = _pool_fwd(pv, pool_w, scale_row, gp, bsz, seq)
    qp, kp = _attn_prep_fwd(q, k, f, bf_row, gq, gk, bsz, seq)
    o, lse = _flash_fwd(qp, kp, v, bsz, seq)
    relayed_ffn2 = _relay_to_sibling(gathers[4], "gather_relay_ffn2", o)
    (wout,) = _copies_wait(gathers[3], True, [o, relayed_ffn2[4][0]], "gather_wait_w_out")
    ycat, x2 = _mix_out_fwd(o, y_pool, x1, ga, wout)
    wg2, wu2, wd2 = _relayed_wait(relayed_ffn2, x2, "gather_wait_ffn2")
    h2, sa2, sb2, s2 = _ffn_up(x2, g2, wg2, wu2, "ffn2_up")
    dx3, dyh2, loss_part = _ffn_down(s2, wd2, x2, target, "ffn2_down")

    da2, db2, dwg2, dwu2 = _ffn_bwd_act(dyh2, sa2, sb2, h2, wd2, dx3, "ffn2_bwd_act")
    (dwd2,) = _wgrad([s2], dyh2, da2, "ffn2_down_wgrad")
    (sent_ffn2,), tok = _copies_start([[dwg2, dwu2, dwd2]], False, "exchange_start_ffn2")
    dx2, dg2 = _ffn_bwd_dx(da2, db2, dx3, x2, g2, wg2, wu2, tok, "ffn2_bwd_dx")
    dwout, dy_pool, do, dga = _mix_out_bwd(dx2, o, ycat, ga, wout)
    (sent_out,), tok = _copies_start([[dwout]], False, "exchange_start_w_out")
    dqp, dkp, dv = _flash_bwd(qp, kp, v, o, do, lse, tok, bsz, seq)
    dq, dk, df, dgq, dgk, dbf = _attn_prep_bwd(dqp, dkp, q, k, f, bf_row, gq, gk, bsz, seq)
    dpv, dpool_w, dscale, dgp = _pool_bwd(dy_pool, mixed, pooled, pool_w, scale_row, gp, bsz, seq)
    dwin, dx1, dyh1, dgm = _mix_in_bwd(dpv, dq, dk, dv, df, hm, x1, dx2, gm, win_t)
    dwin_blocks = _repack_rows(dwin, w_in.shape[1], in_rows, N_DEV, "w_in_grad_blocks")
    (sent_in,), tok = _copies_start([[dwin_blocks]], False, "exchange_start_w_in")
    (dwd1,) = _wgrad([s1], dyh1, tok, "ffn1_down_wgrad")
    (sent_down1,), tok = _copies_start([[dwd1]], False, "exchange_start_ffn1_down", after=tok)
    da1, db1, dwg1, dwu1 = _ffn_bwd_act(dyh1, sa1, sb1, h1, wd1, tok, "ffn1_bwd_act")
    (sent_gate1, sent_up1), tok = _copies_start([[dwg1], [dwu1]], False, "exchange_start_ffn1_up", after=tok)
    dx0, dg1 = _ffn_bwd_dx(da1, db1, dx1, x0, g1, wg1, wu1, tok, "ffn1_bwd_dx")

    pool_rows = POOL_GROUPS * POOL_GROUP_DIM
    packed = _pack_vector_grads(dict(ffn1_norm=dg1, mix_norm=dgm, ffn2_norm=dg2, b_forget=dbf, pool_scale=dscale,
                                     q_norm=dgq, k_norm=dgk, out_norm_pool=dgp, out_norm_attn=dga), loss_part,
                                "pack_vector_grads")
    pool_part = dpool_w.reshape(pool_rows, POOL_GROUP_DIM).astype(BF16)
    (sent_small,), tok = _copies_start([[packed, pool_part]], True, "small_grads_start")

    weights = dict(ffn1_norm=ffn1_norm, ffn1_w_gate=ffn1_w_gate, ffn1_w_up=ffn1_w_up, ffn1_w_down=ffn1_w_down,
                   mix_norm=mix_norm, w_in=w_in, b_forget=b_forget, pool_w=pool_w, pool_scale=pool_scale,
                   q_norm=q_norm, k_norm=k_norm, out_norm_pool=out_norm_pool, out_norm_attn=out_norm_attn,
                   w_out=w_out, ffn2_norm=ffn2_norm, ffn2_w_gate=ffn2_w_gate, ffn2_w_up=ffn2_w_up,
                   ffn2_w_down=ffn2_w_down)
    m_in = dict(ffn1_norm=m_ffn1_norm, ffn1_w_gate=m_ffn1_w_gate, ffn1_w_up=m_ffn1_w_up, ffn1_w_down=m_ffn1_w_down,
                mix_norm=m_mix_norm, w_in=m_w_in, b_forget=m_b_forget, pool_w=m_pool_w, pool_scale=m_pool_scale,
                q_norm=m_q_norm, k_norm=m_k_norm, out_norm_pool=m_out_norm_pool, out_norm_attn=m_out_norm_attn,
                w_out=m_w_out, ffn2_norm=m_ffn2_norm, ffn2_w_gate=m_ffn2_w_gate, ffn2_w_up=m_ffn2_w_up,
                ffn2_w_down=m_ffn2_w_down)
    v_in = dict(ffn1_norm=v_ffn1_norm, ffn1_w_gate=v_ffn1_w_gate, ffn1_w_up=v_ffn1_w_up, ffn1_w_down=v_ffn1_w_down,
                mix_norm=v_mix_norm, w_in=v_w_in, b_forget=v_b_forget, pool_w=v_pool_w, pool_scale=v_pool_scale,
                q_norm=v_q_norm, k_norm=v_k_norm, out_norm_pool=v_out_norm_pool, out_norm_attn=v_out_norm_attn,
                w_out=v_w_out, ffn2_norm=v_ffn2_norm, ffn2_w_gate=v_ffn2_w_gate, ffn2_w_up=v_ffn2_w_up,
                ffn2_w_down=v_ffn2_w_down)
    grads, delta, new_m, new_v = {}, {}, {}, {}
    after = [tok]
    plan = ((sent_ffn2, "ffn2", ("ffn2_w_gate", "ffn2_w_up", "ffn2_w_down")), (sent_out, "w_out", ("w_out",)),
            (sent_in, "w_in", ("w_in",)), (sent_down1, "ffn1_down", ("ffn1_w_down",)),
            (sent_gate1, "ffn1_gate", ("ffn1_w_gate",)), (sent_up1, "ffn1_up", ("ffn1_w_up",)))
    for sent, tag, names in plan:
        parts = _copies_wait(sent, False, after, f"exchange_wait_{tag}")
        after = []
        for n, part in zip(names, parts):
            turn = (lambda a: a.T) if n in COLUMN_SHARDED else (lambda a: a)
            done = _sum_adamw(part, turn(weights[n]), turn(m_in[n]), turn(v_in[n]), f"adamw_{n}")
            grads[n], delta[n], new_m[n], new_v[n] = (turn(a) for a in done)
            after.append(done[3])
    vec_all, pool_all = _copies_wait(sent_small, True, after, "small_grads_wait")
    as_row = lambda a: a.reshape(1, -1)
    as_pool = lambda a: a.reshape(pool_rows, POOL_GROUP_DIM)
    small = _small_adamw(vec_all, pool_all,
                         [tuple(as_row(z[n]) for z in (weights, m_in, v_in)) for n in VEC_NAMES],
                         tuple(as_pool(z["pool_w"]) for z in (weights, m_in, v_in)))
    for i, n in enumerate(VEC_NAMES + ("pool_w",)):
        grads[n], delta[n], new_m[n], new_v[n] = (a.reshape(weights[n].shape) for a in small[4 * i:4 * i + 4])
    loss = small[-1].reshape(())

    order = ("ffn1_norm", "ffn1_w_gate", "ffn1_w_up", "ffn1_w_down", "mix_norm", "w_in", "b_forget", "pool_w",
             "pool_scale", "q_norm", "k_norm", "out_norm_pool", "out_norm_attn", "w_out", "ffn2_norm", "ffn2_w_gate",
             "ffn2_w_up", "ffn2_w_down")
    return (loss, dx0.reshape(bsz, seq, d), *[grads[n] for n in order], *[delta[n] for n in order],
            *[new_m[n] for n in order], *[new_v[n] for n in order])
```

```python
import jax
import jax.numpy as jnp
from jax import lax
from jax.experimental import pallas as pl
from jax.experimental.pallas import tpu as pltpu

F32 = jnp.float32
BF16 = jnp.bfloat16

EPS = 1e-6
D_MODEL = 1024
N_HEADS = 8
HEAD_DIM = 64
POOL_WIDTH = 512
ATTN_WIDTH = 512
POOL_GROUPS = 4
POOL_GROUP_DIM = 128
POOL_WINDOWS = (2, 4, 8, 16)
POOL_HALO = 16
MIX_PAD = POOL_WIDTH + 3 * ATTN_WIDTH + 128
N_DEV = 8
BF16_ROWS = 16
LANES = 128
VMEM_LIMIT = 56 * 1024 * 1024

ADAM_LR = 0.001
ADAM_B1 = 0.9
ADAM_B2 = 0.999
ADAM_EPS = 1e-08
ADAM_WD = 0.01
ADAM_STEP = 10


def _params(*sem):
    return pltpu.CompilerParams(dimension_semantics=sem, vmem_limit_bytes=VMEM_LIMIT)


def _dot(a, b):
    return jnp.dot(a, b, preferred_element_type=F32)


def _dot_nt(a, b):
    return lax.dot_general(a, b, (((1,), (1,)), ((), ())), preferred_element_type=F32)


def _dot_tn(a, b):
    return lax.dot_general(a, b, (((0,), (0,)), ((), ())), preferred_element_type=F32)


def _resident(shape):
    return pl.BlockSpec(shape, lambda *_: (0,) * len(shape), pipeline_mode=pl.Buffered(1))


def _rows(tm, width):
    return pl.BlockSpec((tm, width), lambda i: (i, 0))


ORDER_ONLY = pl.BlockSpec(memory_space=pl.ANY)


def _rms_scale(x):
    return lax.rsqrt(jnp.mean(x * x, axis=-1, keepdims=True) + EPS)


def _rms_bwd(dh, x, gain):
    r = _rms_scale(x)
    n = x * r
    dgain = jnp.sum(dh * n, axis=0, keepdims=True)
    dn = dh * gain
    dx = r * (dn - n * jnp.mean(dn * n, axis=-1, keepdims=True))
    return dx, dgain


def _split3(x):
    hi = x.astype(BF16)
    r1 = x - hi.astype(F32)
    mid = r1.astype(BF16)
    lo = (r1 - mid.astype(F32)).astype(BF16)
    return hi, mid, lo


FF_CHUNK = 256


def _swiglu_parts(a, b):
    sig = jax.nn.sigmoid(a)
    silu = a * sig
    return (b * (sig + silu * (1.0 - sig))).astype(BF16), silu.astype(BF16), (silu * b).astype(BF16)


def _ffn_up(x, gain, wg_t, wu_t, name):
    t, d = x.shape
    f = wg_t.shape[0]
    tm = 512

    def body(x_ref, g_ref, wg_ref, wu_ref, h_ref, sa_ref, sb_ref, s_ref):
        xv = x_ref[...]
        h = (xv * _rms_scale(xv) * g_ref[...]).astype(BF16)
        h_ref[...] = h
        for c in range(f // FF_CHUNK):
            sl = pl.ds(c * FF_CHUNK, FF_CHUNK)
            sa_ref[:, sl], sb_ref[:, sl], s_ref[:, sl] = _swiglu_parts(_dot_nt(h, wg_ref[sl, :]), _dot_nt(h, wu_ref[sl, :]))

    wide = jax.ShapeDtypeStruct((t, f), BF16)
    return pl.pallas_call(
        body, name=name, grid=(t // tm,),
        in_specs=[_rows(tm, d), _resident((1, d)), _resident((f, d)), _resident((f, d))],
        out_specs=[_rows(tm, d), _rows(tm, f), _rows(tm, f), _rows(tm, f)],
        out_shape=[jax.ShapeDtypeStruct((t, d), BF16), wide, wide, wide],
        compiler_params=_params("arbitrary"),
    )(x, gain, wg_t, wu_t)


def _ffn_down(s, wd, x, target, name):
    t, d = x.shape
    f = wd.shape[0]
    tm = 512
    with_loss = target is not None

    def body(*refs):
        if with_loss:
            s_ref, w_ref, x_ref, t_ref, dy_ref, dyh_ref, loss_ref = refs
        else:
            s_ref, w_ref, x_ref, y_ref = refs
        y = x_ref[...] + 0.5 * _dot(s_ref[...], w_ref[...])
        if with_loss:
            e = y - t_ref[...]
            dy = e * (1.0 / d)
            dy_ref[...] = dy
            dyh_ref[...] = (0.5 * dy).astype(BF16)

            @pl.when(pl.program_id(0) == 0)
            def _():
                loss_ref[...] = jnp.zeros_like(loss_ref)

            part = jnp.sum(jnp.sum(e * e, axis=0, keepdims=True), axis=1, keepdims=True)
            loss_ref[...] += part * (0.5 / d)
        else:
            y_ref[...] = y

    in_specs = [_rows(tm, f), _resident((f, d)), _rows(tm, d)]
    args = [s, wd, x]
    if with_loss:
        in_specs.append(_rows(tm, d))
        args.append(target)
        out_shape = [jax.ShapeDtypeStruct((t, d), F32), jax.ShapeDtypeStruct((t, d), BF16),
                     jax.ShapeDtypeStruct((1, 1), F32)]
        out_specs = [_rows(tm, d), _rows(tm, d), pl.BlockSpec((1, 1), lambda i: (0, 0))]
    else:
        out_shape = [jax.ShapeDtypeStruct((t, d), F32)]
        out_specs = [_rows(tm, d)]
    return pl.pallas_call(
        body, name=name, grid=(t // tm,), in_specs=in_specs, out_specs=out_specs, out_shape=out_shape,
        compiler_params=_params("arbitrary"),
    )(*args)


def _ffn_bwd_act(dyh, sa, sb, h, wd, after, name):
    t, d = dyh.shape
    f = wd.shape[0]
    tn = f // 2
    tk = 512
    nk = t // tk

    def body(dy_ref, sa_ref, sb_ref, h_ref, wd_ref, after_ref, da_ref, db_ref, dwg_ref, dwu_ref, acc_g, acc_u):
        k = pl.program_id(1)

        @pl.when(k == 0)
        def _():
            acc_g[...] = jnp.zeros_like(acc_g)
            acc_u[...] = jnp.zeros_like(acc_u)

        ds = _dot_nt(dy_ref[...], wd_ref[...])
        da = (ds * sa_ref[...].astype(F32)).astype(BF16)
        db = (ds * sb_ref[...].astype(F32)).astype(BF16)
        da_ref[...] = da
        db_ref[...] = db
        hv = h_ref[...]
        acc_g[...] += _dot_tn(da, hv)
        acc_u[...] += _dot_tn(db, hv)

        @pl.when(k == nk - 1)
        def _():
            dwg_ref[...] = acc_g[...].astype(BF16)
            dwu_ref[...] = acc_u[...].astype(BF16)

    tokens = pl.BlockSpec((tk, d), lambda j, k: (k, 0))
    wide = pl.BlockSpec((tk, tn), lambda j, k: (k, j))
    weight = pl.BlockSpec((tn, d), lambda j, k: (j, 0))
    return pl.pallas_call(
        body, name=name, grid=(f // tn, nk),
        in_specs=[tokens, wide, wide, tokens, weight, ORDER_ONLY],
        out_specs=[wide, wide, weight, weight],
        out_shape=[jax.ShapeDtypeStruct((t, f), BF16)] * 2 + [jax.ShapeDtypeStruct((f, d), BF16)] * 2,
        scratch_shapes=[pltpu.VMEM((tn, d), F32)] * 2,
        compiler_params=_params("arbitrary", "arbitrary"),
    )(dyh, sa, sb, h, wd, after)


def _ffn_bwd_dx(da, db, dy, x, gain, wg_t, wu_t, after, name):
    t, d = x.shape
    f = wg_t.shape[0]
    tm = 512

    def body(da_ref, db_ref, dy_ref, x_ref, g_ref, wg_ref, wu_ref, after_ref, dx_ref, dg_ref):
        dh = _dot(da_ref[...], wg_ref[...]) + _dot(db_ref[...], wu_ref[...])
        dx, dgain = _rms_bwd(dh, x_ref[...], g_ref[...])
        dx_ref[...] = dy_ref[...] + dx

        @pl.when(pl.program_id(0) == 0)
        def _():
            dg_ref[...] = jnp.zeros_like(dg_ref)

        dg_ref[...] += dgain

    return pl.pallas_call(
        body, name=name, grid=(t // tm,),
        in_specs=[_rows(tm, f), _rows(tm, f), _rows(tm, d), _rows(tm, d), _resident((1, d)), _resident((f, d)),
                  _resident((f, d)), ORDER_ONLY],
        out_specs=[_rows(tm, d), pl.BlockSpec((1, d), lambda i: (0, 0))],
        out_shape=[jax.ShapeDtypeStruct((t, d), F32), jax.ShapeDtypeStruct((1, d), F32)],
        compiler_params=_params("arbitrary"),
    )(da, db, dy, x, gain, wg_t, wu_t, after)


def _wgrad(lhs, b, after, name):
    t, n = lhs[0].shape
    d = b.shape[1]
    m = len(lhs)
    tn = n // 2 if n * d * m > (4 << 20) else n
    tk = 1024
    nk = t // tk

    def body(*refs):
        a_refs, b_ref, o_refs, accs = refs[:m], refs[m], refs[m + 2:2 * m + 2], refs[2 * m + 2:]
        k = pl.program_id(1)

        @pl.when(k == 0)
        def _():
            for acc in accs:
                acc[...] = jnp.zeros_like(acc)

        bv = b_ref[...]
        for a_ref, acc in zip(a_refs, accs):
            acc[...] += _dot_tn(a_ref[...], bv)

        @pl.when(k == nk - 1)
        def _():
            for o_ref, acc in zip(o_refs, accs):
                o_ref[...] = acc[...].astype(BF16)

    return pl.pallas_call(
        body, name=name, grid=(n // tn, nk),
        in_specs=[pl.BlockSpec((tk, tn), lambda j, k: (k, j))] * m + [pl.BlockSpec((tk, d), lambda j, k: (k, 0)),
                                                                       ORDER_ONLY],
        out_specs=[pl.BlockSpec((tn, d), lambda j, k: (j, 0))] * m,
        out_shape=[jax.ShapeDtypeStruct((n, d), BF16)] * m,
        scratch_shapes=[pltpu.VMEM((tn, d), F32)] * m,
        compiler_params=_params("arbitrary", "arbitrary"),
    )(*lhs, b, after)


def _repack_rows(a, rows_in, rows_out, blocks, name):
    total, d = a.shape
    real = min(rows_in, rows_out)

    def body(a_ref, o_ref, wide_in, wide_out):
        wide_in[...] = a_ref[...].astype(F32)
        wide_out[...] = jnp.zeros_like(wide_out)
        for j in range(blocks):
            wide_out[pl.ds(j * rows_out, real), :] = wide_in[pl.ds(j * rows_in, real), :]
        o_ref[...] = wide_out[...].astype(BF16)

    full = pl.BlockSpec((total, d), lambda i: (0, 0))
    return pl.pallas_call(
        body, name=name, grid=(1,), in_specs=[full], out_specs=full, out_shape=jax.ShapeDtypeStruct((total, d), BF16),
        scratch_shapes=[pltpu.VMEM((total, d), F32)] * 2,
        compiler_params=_params("arbitrary"),
    )(a)


def _mix_in_fwd(x, gain, w_in_t):
    t, d = x.shape
    tm = 1024
    pw, aw = POOL_WIDTH, ATTN_WIDTH

    def body(x_ref, g_ref, w_ref, hm_ref, pv_ref, q_ref, k_ref, v_ref, f_ref):
        xv = x_ref[...]
        hm = (xv * _rms_scale(xv) * g_ref[...]).astype(BF16)
        hm_ref[...] = hm
        pv_ref[...] = _dot_nt(hm, w_ref[pl.ds(0, pw), :])
        q_ref[...] = _dot_nt(hm, w_ref[pl.ds(pw, aw), :])
        k_ref[...] = _dot_nt(hm, w_ref[pl.ds(pw + aw, aw), :])
        v_ref[...] = _dot_nt(hm, w_ref[pl.ds(pw + 2 * aw, aw), :]).astype(BF16)
        f_ref[...] = _dot_nt(hm, w_ref[pl.ds(pw + 3 * aw, LANES), :])

    return pl.pallas_call(
        body, name="mix_in_fwd", grid=(t // tm,),
        in_specs=[_rows(tm, d), _resident((1, d)), _resident((MIX_PAD, d))],
        out_specs=[_rows(tm, d), _rows(tm, pw), _rows(tm, aw), _rows(tm, aw), _rows(tm, aw), _rows(tm, LANES)],
        out_shape=[jax.ShapeDtypeStruct((t, d), BF16), jax.ShapeDtypeStruct((t, pw), F32),
                   jax.ShapeDtypeStruct((t, aw), F32), jax.ShapeDtypeStruct((t, aw), F32),
                   jax.ShapeDtypeStruct((t, aw), BF16), jax.ShapeDtypeStruct((t, LANES), F32)],
        compiler_params=_params("arbitrary"),
    )(x, gain, w_in_t)


def _pool_fwd(pv, pool_w, pool_scale, gain, bsz, seq):
    ts = 512
    ns = seq // ts
    pw = POOL_WIDTH

    def body(pv_ref, w_ref, sc_ref, g_ref, pooled_ref, mixed_ref, y_ref, ext):
        s = pl.program_id(1)

        @pl.when(s == 0)
        def _():
            ext[pl.ds(0, POOL_HALO), :] = jnp.zeros((POOL_HALO, pw), F32)

        p = pv_ref[...]
        ext[pl.ds(POOL_HALO, ts), :] = p
        pos = s * ts + lax.broadcasted_iota(jnp.int32, (ts, 1), 0)
        parts = []
        for g, w in enumerate(POOL_WINDOWS):
            lanes = pl.ds(g * POOL_GROUP_DIM, POOL_GROUP_DIM)
            win = ext[pl.ds(POOL_HALO, ts), lanes]
            for i in range(1, w):
                win = win + ext[pl.ds(POOL_HALO - i, ts), lanes]
            cnt = jnp.minimum(pos + 1, w).astype(F32)
            pooled = (win / cnt - ext[pl.ds(POOL_HALO, ts), lanes]).astype(BF16)
            pooled_ref[:, lanes] = pooled
            parts.append(_dot(pooled, w_ref[g].astype(BF16)))
        mixed = jnp.concatenate(parts, axis=1)
        mixed_ref[...] = mixed
        pm = mixed * sc_ref[...]
        y_ref[...] = (pm * _rms_scale(pm) * g_ref[...]).astype(BF16)
        ext[pl.ds(0, POOL_HALO), :] = p[ts - POOL_HALO:, :]

    blk = pl.BlockSpec((ts, pw), lambda b, s: (b * ns + s, 0))
    t = bsz * seq
    return pl.pallas_call(
        body, name="pool_fwd", grid=(bsz, ns),
        in_specs=[blk, pl.BlockSpec((POOL_GROUPS, POOL_GROUP_DIM, POOL_GROUP_DIM), lambda b, s: (0, 0, 0)),
                  pl.BlockSpec((1, pw), lambda b, s: (0, 0)), pl.BlockSpec((1, pw), lambda b, s: (0, 0))],
        out_specs=[blk, blk, blk],
        out_shape=[jax.ShapeDtypeStruct((t, pw), BF16), jax.ShapeDtypeStruct((t, pw), F32),
                   jax.ShapeDtypeStruct((t, pw), BF16)],
        scratch_shapes=[pltpu.VMEM((POOL_HALO + ts, pw), F32)],
        compiler_params=_params("arbitrary", "arbitrary"),
    )(pv, pool_w, pool_scale, gain)


def _pool_bwd(dy, mixed, pooled, pool_w, pool_scale, gain, bsz, seq):
    ts = 512
    ns = seq // ts
    pw = POOL_WIDTH

    def body(dy_ref, mixed_ref, pooled_ref, w_ref, sc_ref, g_ref, dpv_ref, dw_ref, dsc_ref, dg_ref, ext):
        b = pl.program_id(0)
        sr = pl.program_id(1)
        s = ns - 1 - sr

        @pl.when(jnp.logical_and(b == 0, sr == 0))
        def _():
            dw_ref[...] = jnp.zeros_like(dw_ref)
            dsc_ref[...] = jnp.zeros_like(dsc_ref)
            dg_ref[...] = jnp.zeros_like(dg_ref)

        @pl.when(sr == 0)
        def _():
            ext[pl.ds(ts, POOL_HALO), :] = jnp.zeros((POOL_HALO, pw), F32)

        mixed = mixed_ref[...]
        sc = sc_ref[...]
        dpm, dgain = _rms_bwd(dy_ref[...], mixed * sc, g_ref[...])
        dg_ref[...] += dgain
        dsc_ref[...] += jnp.sum(dpm * mixed, axis=0, keepdims=True)
        dmixed = (dpm * sc).astype(BF16)
        pos = s * ts + lax.broadcasted_iota(jnp.int32, (ts, 1), 0)
        dpooled = []
        for g, w in enumerate(POOL_WINDOWS):
            lanes = pl.ds(g * POOL_GROUP_DIM, POOL_GROUP_DIM)
            dm = dmixed[:, g * POOL_GROUP_DIM:(g + 1) * POOL_GROUP_DIM]
            dw_ref[g] += _dot_tn(pooled_ref[:, lanes], dm)
            dp = _dot_nt(dm, w_ref[g].astype(BF16))
            dpooled.append(dp)
            cnt = jnp.minimum(pos + 1, w).astype(F32)
            ext[pl.ds(0, ts), lanes] = dp / cnt
        for g, w in enumerate(POOL_WINDOWS):
            lanes = pl.ds(g * POOL_GROUP_DIM, POOL_GROUP_DIM)
            win = ext[pl.ds(0, ts), lanes]
            for i in range(1, w):
                win = win + ext[pl.ds(i, ts), lanes]
            dpv_ref[:, lanes] = (win - dpooled[g]).astype(BF16)
        head = ext[pl.ds(0, POOL_HALO), :]
        ext[pl.ds(ts, POOL_HALO), :] = head

    blk = pl.BlockSpec((ts, pw), lambda b, s: (b * ns + (ns - 1 - s), 0))
    vec = pl.BlockSpec((1, pw), lambda b, s: (0, 0))
    wspec = pl.BlockSpec((POOL_GROUPS, POOL_GROUP_DIM, POOL_GROUP_DIM), lambda b, s: (0, 0, 0))
    t = bsz * seq
    return pl.pallas_call(
        body, name="pool_bwd", grid=(bsz, ns),
        in_specs=[blk, blk, blk, wspec, vec, vec],
        out_specs=[blk, wspec, vec, vec],
        out_shape=[jax.ShapeDtypeStruct((t, pw), BF16),
                   jax.ShapeDtypeStruct((POOL_GROUPS, POOL_GROUP_DIM, POOL_GROUP_DIM), F32),
                   jax.ShapeDtypeStruct((1, pw), F32), jax.ShapeDtypeStruct((1, pw), F32)],
        scratch_shapes=[pltpu.VMEM((ts + POOL_HALO, pw), F32)],
        compiler_params=_params("arbitrary", "arbitrary"),
    )(dy, mixed, pooled, pool_w, pool_scale, gain)


AUX_ONE = 64
AUX_F = 67

ATTN_PREP_ROWS = 512


def _seg_ones(width, seg):
    r = lax.broadcasted_iota(jnp.int32, (width, width), 0) // seg
    c = lax.broadcasted_iota(jnp.int32, (width, width), 1) // seg
    return (r == c).astype(BF16)


def _tri_ones(n, lower):
    r = lax.broadcasted_iota(jnp.int32, (n, n), 0)
    c = lax.broadcasted_iota(jnp.int32, (n, n), 1)
    return ((r >= c) if lower else (r <= c)).astype(BF16)


def _place_pieces(first_lane):
    r = lax.broadcasted_iota(jnp.int32, (3 * LANES, N_HEADS * LANES), 0)
    c = lax.broadcasted_iota(jnp.int32, (3 * LANES, N_HEADS * LANES), 1)
    piece, head = r // LANES, r % LANES
    return jnp.logical_and(head < N_HEADS, c == head * LANES + first_lane + piece).astype(BF16)


def _head_sums(x, seg_ones):
    return _dot(x.astype(BF16), seg_ones)


def _log_sigmoid(x):
    return jnp.minimum(x, 0.0) - jnp.log(1.0 + jnp.exp(-jnp.abs(x)))


def _attn_prep_fwd(q, k, f, b_forget, q_gain, k_gain, bsz, seq):
    ts = ATTN_PREP_ROWS
    ns = seq // ts
    aw = ATTN_WIDTH
    t = bsz * seq
    seg = _seg_ones(aw, HEAD_DIM)
    tri = _tri_ones(ts, True)

    def body(q_ref, k_ref, f_ref, bf_ref, gq_ref, gk_ref, seg_ref, tri_ref, place_ref, qp_ref, kp_ref, carry):
        s = pl.program_id(1)

        @pl.when(s == 0)
        def _():
            carry[...] = jnp.zeros_like(carry)

        logf = _log_sigmoid(f_ref[...] + bf_ref[...])
        hi, mid, lo = _split3(logf)
        tri_v = tri_ref[...]
        fc = _dot(tri_v, hi) + _dot(tri_v, mid) + _dot(tri_v, lo) + carry[pl.ds(0, 1), :]
        carry[pl.ds(0, 1), :] = fc[ts - 1:, :]
        pcs = jnp.concatenate(_split3(fc), axis=1)
        lane = lax.broadcasted_iota(jnp.int32, (1, LANES), 1)
        ones_q = jnp.logical_and(lane >= AUX_ONE, lane < AUX_ONE + 3).astype(F32)
        ones_k = jnp.logical_and(lane >= AUX_F, lane < AUX_F + 3).astype(F32)
        seg_v = seg_ref[...]
        placed = _dot(pcs, place_ref[...])

        def build(x_ref, g_ref, scale, out_ref, ones, for_keys):
            xv = x_ref[...]
            r = lax.rsqrt(_head_sums(xv * xv, seg_v) * (1.0 / HEAD_DIM) + EPS)
            xn = xv * r * g_ref[...] * scale
            for h in range(N_HEADS):
                pair = xn[:, (h // 2) * LANES:(h // 2 + 1) * LANES]
                feat = pair if h % 2 == 0 else pltpu.roll(pair, HEAD_DIM, 1)
                aux_h = placed[:, h * LANES:(h + 1) * LANES]
                if for_keys:
                    aux_h = -pltpu.roll(aux_h, LANES - (AUX_F - AUX_ONE), 1)
                out_ref[:, h * LANES:(h + 1) * LANES] = jnp.where(lane < HEAD_DIM, feat, aux_h + ones).astype(BF16)

        build(q_ref, gq_ref, 0.125, qp_ref, ones_q, False)
        build(k_ref, gk_ref, 1.0, kp_ref, ones_k, True)

    blk = pl.BlockSpec((ts, aw), lambda b, s: (b * ns + s, 0))
    fblk = pl.BlockSpec((ts, LANES), lambda b, s: (b * ns + s, 0))
    oblk = pl.BlockSpec((ts, N_HEADS * LANES), lambda b, s: (b * ns + s, 0))
    const = lambda shape: pl.BlockSpec(shape, lambda b, s: (0, 0))
    return pl.pallas_call(
        body, name="attn_prep_fwd", grid=(bsz, ns),
        in_specs=[blk, blk, fblk, const((1, LANES)), const((1, aw)), const((1, aw)), const((aw, aw)), const((ts, ts)),
                  const((3 * LANES, N_HEADS * LANES))],
        out_specs=[oblk, oblk],
        out_shape=[jax.ShapeDtypeStruct((t, N_HEADS * LANES), BF16)] * 2,
        scratch_shapes=[pltpu.VMEM((8, LANES), F32)],
        compiler_params=_params("arbitrary", "arbitrary"),
    )(q, k, f, b_forget, q_gain, k_gain, seg, tri, _place_pieces(AUX_F))


def _attn_prep_bwd(dqp, dkp, q, k, f, b_forget, q_gain, k_gain, bsz, seq):
    ts = ATTN_PREP_ROWS
    ns = seq // ts
    aw = ATTN_WIDTH
    t = bsz * seq
    seg = _seg_ones(aw, HEAD_DIM)
    tri = _tri_ones(ts, False)

    def body(dqp_ref, dkp_ref, q_ref, k_ref, f_ref, bf_ref, gq_ref, gk_ref, seg_ref, tri_ref,
             dq_ref, dk_ref, df_ref, dgq_ref, dgk_ref, dbf_ref, carry):
        b = pl.program_id(0)
        sr = pl.program_id(1)

        @pl.when(jnp.logical_and(b == 0, sr == 0))
        def _():
            dgq_ref[...] = jnp.zeros_like(dgq_ref)
            dgk_ref[...] = jnp.zeros_like(dgk_ref)
            dbf_ref[...] = jnp.zeros_like(dbf_ref)

        @pl.when(sr == 0)
        def _():
            carry[...] = jnp.zeros_like(carry)

        lane = lax.broadcasted_iota(jnp.int32, (1, LANES), 1)
        seg_v = seg_ref[...]

        def norm_bwd(dp_ref, x_ref, g_ref, scale, dx_ref, dgain_ref):
            parts = []
            for j in range(N_HEADS // 2):
                even = dp_ref[:, (2 * j) * LANES:(2 * j + 1) * LANES]
                odd = dp_ref[:, (2 * j + 1) * LANES:(2 * j + 2) * LANES]
                parts.append(jnp.where(lane < HEAD_DIM, even, pltpu.roll(odd, HEAD_DIM, 1)))
            dxn = jnp.concatenate(parts, axis=1) * scale
            xv = x_ref[...]
            r = lax.rsqrt(_head_sums(xv * xv, seg_v) * (1.0 / HEAD_DIM) + EPS)
            n = xv * r
            dgain_ref[...] += jnp.sum(dxn * n, axis=0, keepdims=True)
            dn = dxn * g_ref[...]
            m = _head_sums(dn * n, seg_v) * (1.0 / HEAD_DIM)
            dx_ref[...] = (r * (dn - n * m)).astype(BF16)

        norm_bwd(dqp_ref, q_ref, gq_ref, 0.125, dq_ref, dgq_ref)
        norm_bwd(dkp_ref, k_ref, gk_ref, 1.0, dk_ref, dgk_ref)

        dfc = jnp.zeros((ts, LANES), F32)
        for h in range(N_HEADS):
            cols = pl.ds(h * LANES, LANES)
            both = jnp.where(lane == AUX_F, dqp_ref[:, cols], 0.0) - jnp.where(lane == AUX_ONE, dkp_ref[:, cols], 0.0)
            dfc = jnp.where(lane == h, jnp.sum(both, axis=1, keepdims=True), dfc)
        hi, mid, lo = _split3(dfc)
        tri_v = tri_ref[...]
        dlogf = _dot(tri_v, hi) + _dot(tri_v, mid) + _dot(tri_v, lo) + carry[pl.ds(0, 1), :]
        carry[pl.ds(0, 1), :] = dlogf[0:1, :]
        df = jnp.where(lane < N_HEADS, dlogf * jax.nn.sigmoid(-(f_ref[...] + bf_ref[...])), 0.0)
        df_ref[...] = df.astype(BF16)
        dbf_ref[...] += jnp.sum(df, axis=0, keepdims=True)

    rev = lambda b, s: (b * ns + (ns - 1 - s), 0)
    blk = pl.BlockSpec((ts, aw), rev)
    fblk = pl.BlockSpec((ts, LANES), rev)
    pblk = pl.BlockSpec((ts, N_HEADS * LANES), rev)
    const = lambda shape: pl.BlockSpec(shape, lambda b, s: (0, 0))
    return pl.pallas_call(
        body, name="attn_prep_bwd", grid=(bsz, ns),
        in_specs=[pblk, pblk, blk, blk, fblk, const((1, LANES)), const((1, aw)), const((1, aw)), const((aw, aw)),
                  const((ts, ts))],
        out_specs=[blk, blk, fblk, const((1, aw)), const((1, aw)), const((1, LANES))],
        out_shape=[jax.ShapeDtypeStruct((t, aw), BF16), jax.ShapeDtypeStruct((t, aw), BF16),
                   jax.ShapeDtypeStruct((t, LANES), BF16), jax.ShapeDtypeStruct((1, aw), F32),
                   jax.ShapeDtypeStruct((1, aw), F32), jax.ShapeDtypeStruct((1, LANES), F32)],
        scratch_shapes=[pltpu.VMEM((8, LANES), F32)],
        compiler_params=_params("arbitrary", "arbitrary"),
    )(dqp, dkp, q, k, f, b_forget, q_gain, k_gain, seg, tri)


ATTN_BLOCK = 1024
HEAD_PAIRS = N_HEADS // 2


def _flash_fwd(qp, kp, v, bsz, seq):
    tq = ATTN_BLOCK
    half = tq // 2
    nq = seq // tq
    t = bsz * seq

    def body(q_ref, k_ref, v_ref, o_ref, lse_ref, m_sc, l_sc, acc_sc):
        i = pl.program_id(2)
        m_sc[...] = jnp.full(m_sc.shape, -jnp.inf, F32)
        l_sc[...] = jnp.zeros_like(l_sc)
        acc_sc[...] = jnp.zeros_like(acc_sc)
        lane = lax.broadcasted_iota(jnp.int32, (1, LANES), 1)
        low = lane < HEAD_DIM

        def tile(q0, qn, k_start, kn, k0=None):
            qs = pl.ds(q0, qn)
            ks = pl.ds(k_start, kn)
            vv = v_ref[ks, :]
            for h in range(2):
                mine = low if h == 0 else jnp.logical_not(low)
                cols = pl.ds(h * LANES, LANES)
                s = _dot_nt(q_ref[qs, cols], k_ref[ks, cols])
                if k0 is not None:
                    row = lax.broadcasted_iota(jnp.int32, (qn, kn), 0) + q0
                    col = lax.broadcasted_iota(jnp.int32, (qn, kn), 1) + k0
                    s = jnp.where(row >= col, s, -jnp.inf)
                m_prev = m_sc[h, qs, :]
                m_new = jnp.maximum(m_prev, jnp.max(s, axis=1, keepdims=True))
                p = jnp.exp(s - jnp.tile(m_new, (1, kn // LANES)))
                alpha = jnp.exp(m_prev - m_new)
                l_sc[h, qs, :] = alpha * l_sc[h, qs, :] + jnp.sum(p, axis=1, keepdims=True)
                m_sc[h, qs, :] = m_new
                pv = _dot(p.astype(BF16), jnp.where(mine, vv, jnp.zeros_like(vv)))
                acc_sc[qs, :] = acc_sc[qs, :] * jnp.where(mine, alpha, 1.0) + pv

        def below_diagonal(j, carry):
            tile(0, tq, pl.multiple_of(j * tq, tq), tq)
            return carry

        lax.fori_loop(0, i, below_diagonal, 0)
        diagonal = pl.multiple_of(i * tq, tq)
        tile(0, tq, diagonal, half, k0=0)
        tile(half, half, diagonal + half, half, k0=half)
        l = jnp.where(low, l_sc[0], l_sc[1])
        m = jnp.where(low, m_sc[0], m_sc[1])
        o_ref[...] = acc_sc[...] / l
        lse_ref[...] = m + jnp.log(l)

    qspec = pl.BlockSpec((tq, 2 * LANES), lambda b, hp, i: (b * nq + i, hp))
    kspec = pl.BlockSpec((seq, 2 * LANES), lambda b, hp, i: (b, hp))
    vspec = pl.BlockSpec((seq, LANES), lambda b, hp, i: (b, hp))
    ospec = pl.BlockSpec((tq, LANES), lambda b, hp, i: (b * nq + i, hp))
    return pl.pallas_call(
        body, name="flash_fwd", grid=(bsz, HEAD_PAIRS, nq),
        in_specs=[qspec, kspec, vspec], out_specs=[ospec, ospec],
        out_shape=[jax.ShapeDtypeStruct((t, ATTN_WIDTH), F32), jax.ShapeDtypeStruct((t, ATTN_WIDTH), F32)],
        scratch_shapes=[pltpu.VMEM((2, tq, LANES), F32), pltpu.VMEM((2, tq, LANES), F32), pltpu.VMEM((tq, LANES), F32)],
        compiler_params=_params("arbitrary", "arbitrary", "arbitrary"),
    )(qp, kp, v)


def _flash_bwd(qp, kp, v, o, do, lse, after, bsz, seq):
    tq = ATTN_BLOCK
    half = tq // 2
    nq = seq // tq
    t = bsz * seq

    def body(q_ref, k_ref, v_ref, o_ref, do_ref, lse_ref, after_ref, dq_ref, dk_ref, dv_ref, dk_acc, dv_acc):
        j = pl.program_id(2)

        @pl.when(j == 0)
        def _():
            dq_ref[...] = jnp.zeros_like(dq_ref)

        dk_acc[...] = jnp.zeros_like(dk_acc)
        dv_acc[...] = jnp.zeros_like(dv_acc)
        lane = lax.broadcasted_iota(jnp.int32, (1, LANES), 1)
        low = lane < HEAD_DIM

        def tile(q_start, qn, k0, kn, q0=None):
            rows = pl.ds(q_start, qn)
            ks = pl.ds(k0, kn)
            dov = do_ref[rows, :]
            dd = dov * o_ref[rows, :]
            dob = dov.astype(BF16)
            vv = v_ref[ks, :]
            lse_v = lse_ref[rows, :]
            for h in range(2):
                mine = low if h == 0 else jnp.logical_not(low)
                cols = pl.ds(h * LANES, LANES)
                qh = q_ref[rows, cols]
                kh = k_ref[ks, cols]
                s = _dot_nt(qh, kh)
                lse_h = jnp.where(mine, lse_v, pltpu.roll(lse_v, HEAD_DIM, 1))
                p = jnp.exp(s - jnp.tile(lse_h, (1, kn // LANES)))
                if q0 is not None:
                    row = lax.broadcasted_iota(jnp.int32, (qn, kn), 0) + q0
                    col = lax.broadcasted_iota(jnp.int32, (qn, kn), 1) + k0
                    p = jnp.where(row >= col, p, 0.0)
                delta = jnp.sum(jnp.where(mine, dd, 0.0), axis=1, keepdims=True)
                dp = _dot_nt(dob, jnp.where(mine, vv, jnp.zeros_like(vv)))
                ds = (p * (dp - delta)).astype(BF16)
                dv_acc[ks, :] += jnp.where(mine, _dot_tn(p.astype(BF16), dob), 0.0)
                dk_acc[ks, cols] += _dot_tn(ds, qh)
                dq_ref[rows, cols] += _dot(ds, kh)

        def above_diagonal(i, carry):
            tile(pl.multiple_of(i * tq, tq), tq, 0, tq)
            return carry

        diagonal = pl.multiple_of(j * tq, tq)
        tile(diagonal, tq, 0, half, q0=0)
        tile(diagonal + half, half, half, half, q0=half)
        lax.fori_loop(j + 1, nq, above_diagonal, 0)
        dk_ref[...] = dk_acc[...]
        dv_ref[...] = dv_acc[...].astype(BF16)

    qspec = pl.BlockSpec((seq, 2 * LANES), lambda b, hp, j: (b, hp))
    kspec = pl.BlockSpec((tq, 2 * LANES), lambda b, hp, j: (b * nq + j, hp))
    vspec = pl.BlockSpec((tq, LANES), lambda b, hp, j: (b * nq + j, hp))
    ospec = pl.BlockSpec((seq, LANES), lambda b, hp, j: (b, hp))
    return pl.pallas_call(
        body, name="flash_bwd", grid=(bsz, HEAD_PAIRS, nq),
        in_specs=[qspec, kspec, vspec, ospec, ospec, ospec, ORDER_ONLY], out_specs=[qspec, kspec, vspec],
        out_shape=[jax.ShapeDtypeStruct((t, N_HEADS * LANES), F32), jax.ShapeDtypeStruct((t, N_HEADS * LANES), F32),
                   jax.ShapeDtypeStruct((t, ATTN_WIDTH), BF16)],
        scratch_shapes=[pltpu.VMEM((tq, 2 * LANES), F32), pltpu.VMEM((tq, LANES), F32)],
        compiler_params=_params("arbitrary", "arbitrary", "arbitrary"),
    )(qp, kp, v, o, do, lse, after)


def _mix_out_fwd(o, y_pool, x, gain, w_out):
    t, d = x.shape
    tm = 1024
    pw, aw = POOL_WIDTH, ATTN_WIDTH

    def body(o_ref, yp_ref, x_ref, g_ref, w_ref, ycat_ref, y_ref):
        ov = o_ref[...]
        ya = (ov * _rms_scale(ov) * g_ref[...]).astype(BF16)
        ycat = jnp.concatenate([yp_ref[...], ya], axis=1)
        ycat_ref[...] = ycat
        y_ref[...] = x_ref[...] + _dot(ycat, w_ref[...])

    return pl.pallas_call(
        body, name="mix_out_fwd", grid=(t // tm,),
        in_specs=[_rows(tm, aw), _rows(tm, pw), _rows(tm, d), _resident((1, aw)), _resident((pw + aw, d))],
        out_specs=[_rows(tm, pw + aw), _rows(tm, d)],
        out_shape=[jax.ShapeDtypeStruct((t, pw + aw), BF16), jax.ShapeDtypeStruct((t, d), F32)],
        compiler_params=_params("arbitrary"),
    )(o, y_pool, x, gain, w_out)


def _mix_out_bwd(dx, o, ycat, gain, w_out):
    t, d = dx.shape
    tm = 1024
    nm = t // tm
    pw, aw = POOL_WIDTH, ATTN_WIDTH

    def body(dx_ref, o_ref, ycat_ref, g_ref, w_ref, dw_ref, dyp_ref, do_ref, dg_ref, acc):
        i = pl.program_id(0)

        @pl.when(i == 0)
        def _():
            dg_ref[...] = jnp.zeros_like(dg_ref)
            acc[...] = jnp.zeros_like(acc)

        dxb = dx_ref[...].astype(BF16)
        acc[...] += _dot_tn(ycat_ref[...], dxb)
        dyp_ref[...] = _dot_nt(dxb, w_ref[pl.ds(0, pw), :])
        dya = _dot_nt(dxb, w_ref[pl.ds(pw, aw), :])
        do, dgain = _rms_bwd(dya, o_ref[...], g_ref[...])
        do_ref[...] = do
        dg_ref[...] += dgain

        @pl.when(i == nm - 1)
        def _():
            dw_ref[...] = acc[...].astype(BF16)

    return pl.pallas_call(
        body, name="mix_out_bwd", grid=(nm,),
        in_specs=[_rows(tm, d), _rows(tm, aw), _rows(tm, pw + aw), _resident((1, aw)), _resident((pw + aw, d))],
        out_specs=[pl.BlockSpec((pw + aw, d), lambda i: (0, 0)), _rows(tm, pw), _rows(tm, aw),
                   pl.BlockSpec((1, aw), lambda i: (0, 0))],
        out_shape=[jax.ShapeDtypeStruct((pw + aw, d), BF16), jax.ShapeDtypeStruct((t, pw), F32),
                   jax.ShapeDtypeStruct((t, aw), F32), jax.ShapeDtypeStruct((1, aw), F32)],
        scratch_shapes=[pltpu.VMEM((pw + aw, d), F32)],
        compiler_params=_params("arbitrary"),
    )(dx, o, ycat, gain, w_out)


def _mix_in_bwd(dpv, dq, dk, dv, df, hm, x, dx_res, gain, w_in_t):
    t, d = x.shape
    tm = 512
    nm = t // tm
    pw, aw = POOL_WIDTH, ATTN_WIDTH

    def body(dpv_ref, dq_ref, dk_ref, dv_ref, df_ref, hm_ref, x_ref, dxr_ref, g_ref, w_ref, dw_ref, dx_ref, dxh_ref,
             dg_ref, acc):
        i = pl.program_id(0)

        @pl.when(i == 0)
        def _():
            dg_ref[...] = jnp.zeros_like(dg_ref)
            acc[...] = jnp.zeros_like(acc)

        dh = jnp.concatenate([dpv_ref[...], dq_ref[...], dk_ref[...], dv_ref[...], df_ref[...]], axis=1)
        acc[...] += _dot_tn(dh, hm_ref[...])
        dx, dgain = _rms_bwd(_dot(dh, w_ref[...]), x_ref[...], g_ref[...])
        dx = dxr_ref[...] + dx
        dx_ref[...] = dx
        dxh_ref[...] = (0.5 * dx).astype(BF16)
        dg_ref[...] += dgain

        @pl.when(i == nm - 1)
        def _():
            dw_ref[...] = acc[...].astype(BF16)

    return pl.pallas_call(
        body, name="mix_in_bwd", grid=(nm,),
        in_specs=[_rows(tm, pw), _rows(tm, aw), _rows(tm, aw), _rows(tm, aw), _rows(tm, LANES), _rows(tm, d),
                  _rows(tm, d), _rows(tm, d), _resident((1, d)), _resident((MIX_PAD, d))],
        out_specs=[pl.BlockSpec((MIX_PAD, d), lambda i: (0, 0)), _rows(tm, d), _rows(tm, d),
                   pl.BlockSpec((1, d), lambda i: (0, 0))],
        out_shape=[jax.ShapeDtypeStruct((MIX_PAD, d), BF16), jax.ShapeDtypeStruct((t, d), F32),
                   jax.ShapeDtypeStruct((t, d), BF16), jax.ShapeDtypeStruct((1, d), F32)],
        scratch_shapes=[pltpu.VMEM((MIX_PAD, d), F32)],
        compiler_params=_params("arbitrary"),
    )(dpv, dq, dk, dv, df, hm, x, dx_res, gain, w_in_t)


MESH_IDS = pl.DeviceIdType.MESH


def _me():
    return lax.axis_index("x"), lax.axis_index("y"), lax.axis_index("c")


def _peer(x, y, c, p):
    px = 1 - x if p & 4 else x
    py = 1 - y if p & 2 else y
    pc = 1 - c if p & 1 else c
    return (px, py, pc), 4 * px + 2 * py + pc


HBM_SPEC = pl.BlockSpec(memory_space=pltpu.HBM)
SEM_SPEC = pl.BlockSpec(memory_space=pltpu.SEMAPHORE)
SPLIT_COPY = pltpu.CompilerParams(has_side_effects=pltpu.SideEffectType.DATAFLOW_SIDE_EFFECTING)
PEERS = N_DEV - 1


def _hbm(a):
    return pltpu.with_memory_space_constraint(a, pltpu.HBM)


def _row_block(ref, dev, rows):
    return ref.at[pl.ds(pl.multiple_of(dev * rows, BF16_ROWS), rows)]


def _copy_ends(gather, src, land, me, peer_id):
    if gather:
        rows = src.shape[0]
        return src, _row_block(land, me, rows), _row_block(land, peer_id, rows), src, _row_block(land, me, rows)
    rows = src.shape[0] // N_DEV
    return (_row_block(src, peer_id, rows), land.at[me], land.at[peer_id], _row_block(src, me, rows), land.at[me])


def _land_shape(gather, s):
    return (N_DEV * s.shape[0], s.shape[1]) if gather else (N_DEV, s.shape[0] // N_DEV, s.shape[1])


SIBLING = 1
SAME_CORE_PEERS = (2, 4, 6)
RELAYS = len(SAME_CORE_PEERS)


def _copies_start(groups, gather, name, after=None, relayed=()):
    flat = [s for g in groups for s in g]
    n, ng = len(flat), len(groups)
    lands = [lax.empty(_land_shape(gather, s), s.dtype) for s in flat]
    n_in = 2 * n + (after is not None)

    def body(*refs):
        ins, lnd = refs[:n], refs[n:2 * n]
        sems = refs[n_in:n_in + 2 * ng]
        token = refs[-1]
        x, y, c = _me()
        me = 4 * x + 2 * y + c
        w = 0
        for gi, g in enumerate(groups):
            for k in range(len(g)):
                for p in ((SIBLING,) + SAME_CORE_PEERS if gi in relayed else range(1, N_DEV)):
                    peer, peer_id = _peer(x, y, c, p)
                    src, dst, _, _, _ = _copy_ends(gather, ins[w], lnd[w], me, peer_id)
                    pltpu.make_async_remote_copy(src, dst, sems[2 * gi].at[k * PEERS + p - 1],
                                                 sems[2 * gi + 1].at[k * PEERS + p - 1], device_id=peer,
                                                 device_id_type=MESH_IDS).start()
                w += 1
        token[...] = jnp.zeros_like(token)

    sem_shapes = []
    for g in groups:
        sem_shapes += [pltpu.SemaphoreType.DMA((len(g) * PEERS,))] * 2
    out = pl.pallas_call(
        body, name=name,
        out_shape=(*sem_shapes, *[pltpu.HBM(s.shape, s.dtype) for s in flat],
                   *[pltpu.HBM(l.shape, l.dtype) for l in lands], jax.ShapeDtypeStruct((8, LANES), F32)),
        in_specs=[HBM_SPEC] * (2 * n) + [pl.BlockSpec(memory_space=pl.ANY)] * (after is not None),
        out_specs=(*[SEM_SPEC] * (2 * ng), *[HBM_SPEC] * (2 * n), pl.BlockSpec(memory_space=pltpu.VMEM)),
        input_output_aliases={i: 2 * ng + i for i in range(2 * n)},
        compiler_params=SPLIT_COPY,
    )(*[_hbm(s) for s in flat], *[_hbm(l) for l in lands], *([after] if after is not None else []))
    sems, thru, token = out[:2 * ng], out[2 * ng:2 * ng + 2 * n], out[-1]
    res, w = [], 0
    for gi, g in enumerate(groups):
        res.append((sems[2 * gi], sems[2 * gi + 1], list(thru[w:w + len(g)]), list(thru[n + w:n + w + len(g)])))
        w += len(g)
    return res, token


def _copies_wait(started, gather, after, name):
    send, recv, srcs, lands = started
    n = len(srcs)
    after = list(after) if isinstance(after, (list, tuple)) else [after]

    own_shapes = [s.shape if gather else (s.shape[0] // N_DEV, s.shape[1]) for s in srcs]

    def body(*refs):
        ins, lnd = refs[:n], refs[n:2 * n]
        send_sems, recv_sems = refs[2 * n], refs[2 * n + 1]
        bounce, in_sems, out_sems = refs[-n - 2:-2], refs[-2], refs[-1]
        x, y, c = _me()
        me = 4 * x + 2 * y + c
        ends = [_copy_ends(gather, ins[w], lnd[w], me, me)[3:] for w in range(n)]
        loads = [pltpu.make_async_copy(ends[w][0], bounce[w], in_sems.at[w]) for w in range(n)]
        stores = [pltpu.make_async_copy(bounce[w], ends[w][1], out_sems.at[w]) for w in range(n)]
        for cp in loads:
            cp.start()
        for w in range(n):
            loads[w].wait()
            stores[w].start()
        for w in range(n):
            for p in range(1, N_DEV):
                peer, peer_id = _peer(x, y, c, p)
                src, _, arrival, _, _ = _copy_ends(gather, ins[w], lnd[w], me, peer_id)
                cp = pltpu.make_async_remote_copy(src, arrival, send_sems.at[w * PEERS + p - 1],
                                                  recv_sems.at[w * PEERS + p - 1], device_id=peer,
                                                  device_id_type=MESH_IDS)
                cp.wait_send()
                cp.wait_recv()
        for cp in stores:
            cp.wait()

    out = pl.pallas_call(
        body, name=name,
        out_shape=(*[pltpu.HBM(s.shape, s.dtype) for s in srcs], *[pltpu.HBM(l.shape, l.dtype) for l in lands]),
        in_specs=[HBM_SPEC] * (2 * n) + [SEM_SPEC, SEM_SPEC] + [pl.BlockSpec(memory_space=pl.ANY)] * len(after),
        out_specs=[HBM_SPEC] * (2 * n),
        input_output_aliases={i: i for i in range(2 * n)},
        scratch_shapes=[*[pltpu.VMEM(shape, s.dtype) for shape, s in zip(own_shapes, srcs)],
                        pltpu.SemaphoreType.DMA((n,)), pltpu.SemaphoreType.DMA((n,))],
        compiler_params=SPLIT_COPY,
    )(*srcs, *lands, send, recv, *after)
    return list(out[n:])


def _relay_to_sibling(started, name, after=None):
    send, recv, srcs, lands = started
    n = len(srcs)
    after = [] if after is None else [after]

    def body(*refs):
        ins, lnd = refs[:n], refs[n:2 * n]
        send_sems, recv_sems = refs[2 * n], refs[2 * n + 1]
        relay_send, relay_recv = refs[2 * n + 2 + len(after)], refs[2 * n + 3 + len(after)]
        x, y, c = _me()
        sibling, _ = _peer(x, y, c, SIBLING)
        for w in range(n):
            rows = ins[w].shape[0]
            for k, p in enumerate(SAME_CORE_PEERS):
                peer, peer_id = _peer(x, y, c, p)
                arrived = _row_block(lnd[w], peer_id, rows)
                first = pltpu.make_async_remote_copy(ins[w], arrived, send_sems.at[w * PEERS + p - 1],
                                                     recv_sems.at[w * PEERS + p - 1], device_id=peer,
                                                     device_id_type=MESH_IDS)
                first.wait_recv()
                pltpu.make_async_remote_copy(arrived, arrived, relay_send.at[w * RELAYS + k],
                                             relay_recv.at[w * RELAYS + k], device_id=sibling,
                                             device_id_type=MESH_IDS).start()
                first.wait_send()

    sems = pltpu.SemaphoreType.DMA((n * RELAYS,))
    out = pl.pallas_call(
        body, name=name,
        out_shape=(sems, sems, *[pltpu.HBM(s.shape, s.dtype) for s in srcs], *[pltpu.HBM(l.shape, l.dtype) for l in lands]),
        in_specs=[HBM_SPEC] * (2 * n) + [SEM_SPEC, SEM_SPEC] + [pl.BlockSpec(memory_space=pl.ANY)] * len(after),
        out_specs=(SEM_SPEC, SEM_SPEC, *[HBM_SPEC] * (2 * n)),
        input_output_aliases={i: 2 + i for i in range(2 * n)},
        compiler_params=SPLIT_COPY,
    )(*srcs, *lands, send, recv, *after)
    return send, recv, out[0], out[1], list(out[2:2 + n]), list(out[2 + n:])


def _relayed_wait(relayed, after, name):
    send, recv, relay_send, relay_recv, srcs, lands = relayed
    n = len(srcs)
    after = list(after) if isinstance(after, (list, tuple)) else [after]

    def body(*refs):
        ins, lnd = refs[:n], refs[n:2 * n]
        send_sems, recv_sems, relay_send_sems, relay_recv_sems = refs[2 * n:2 * n + 4]
        bounce, in_sems, out_sems = refs[-n - 2:-2], refs[-2], refs[-1]
        x, y, c = _me()
        me = 4 * x + 2 * y + c
        sibling, sibling_id = _peer(x, y, c, SIBLING)
        loads = [pltpu.make_async_copy(ins[w], bounce[w], in_sems.at[w]) for w in range(n)]
        stores = [pltpu.make_async_copy(bounce[w], _row_block(lnd[w], me, ins[w].shape[0]), out_sems.at[w])
                  for w in range(n)]
        for cp in loads:
            cp.start()
        for w in range(n):
            loads[w].wait()
            stores[w].start()
        for w in range(n):
            rows = ins[w].shape[0]
            direct = pltpu.make_async_remote_copy(ins[w], _row_block(lnd[w], sibling_id, rows),
                                                  send_sems.at[w * PEERS + SIBLING - 1],
                                                  recv_sems.at[w * PEERS + SIBLING - 1], device_id=sibling,
                                                  device_id_type=MESH_IDS)
            direct.wait_send()
            direct.wait_recv()
            for k, p in enumerate(SAME_CORE_PEERS):
                _, sent_id = _peer(x, y, c, p)
                _, got_id = _peer(x, y, c, p + SIBLING)
                relay = pltpu.make_async_remote_copy(_row_block(lnd[w], sent_id, rows), _row_block(lnd[w], got_id, rows),
                                                     relay_send_sems.at[w * RELAYS + k],
                                                     relay_recv_sems.at[w * RELAYS + k], device_id=sibling,
                                                     device_id_type=MESH_IDS)
                relay.wait_send()
                relay.wait_recv()
        for cp in stores:
            cp.wait()

    out = pl.pallas_call(
        body, name=name,
        out_shape=(*[pltpu.HBM(s.shape, s.dtype) for s in srcs], *[pltpu.HBM(l.shape, l.dtype) for l in lands]),
        in_specs=[HBM_SPEC] * (2 * n) + [SEM_SPEC] * 4 + [pl.BlockSpec(memory_space=pl.ANY)] * len(after),
        out_specs=[HBM_SPEC] * (2 * n),
        input_output_aliases={i: i for i in range(2 * n)},
        scratch_shapes=[*[pltpu.VMEM(s.shape, s.dtype) for s in srcs],
                        pltpu.SemaphoreType.DMA((n,)), pltpu.SemaphoreType.DMA((n,))],
        compiler_params=SPLIT_COPY,
    )(*srcs, *lands, send, recv, relay_send, relay_recv, *after)
    return list(out[n:])


def _adamw_update(w, g, m, v):
    nm = ADAM_B1 * m + (1.0 - ADAM_B1) * g
    nv = ADAM_B2 * v + (1.0 - ADAM_B2) * (g * g)
    m_hat = nm / (1.0 - ADAM_B1 ** ADAM_STEP)
    v_hat = nv / (1.0 - ADAM_B2 ** ADAM_STEP)
    return -ADAM_LR * (m_hat / (jnp.sqrt(v_hat) + ADAM_EPS) + ADAM_WD * w), nm, nv


SUM_ADAMW_COLS = 512


def _sum_adamw(parts, ws, ms, vs, name):
    k = len(parts)
    _, rows, d = parts[0].shape
    n = ws[0].shape[0]
    tc = SUM_ADAMW_COLS

    def body(*refs):
        for i in range(k):
            p_ref, w_ref, m_ref, v_ref = refs[i], refs[k + i], refs[2 * k + i], refs[3 * k + i]
            g_ref, d_ref, nm_ref, nv_ref = refs[4 * k + 4 * i:4 * k + 4 * i + 4]
            g = p_ref[0].astype(F32)
            for dev in range(1, N_DEV):
                g = g + p_ref[dev].astype(F32)
            g = g[:n]
            g_ref[...] = g
            d_ref[...], nm_ref[...], nv_ref[...] = _adamw_update(w_ref[...], g, m_ref[...], v_ref[...])

    spec = pl.BlockSpec((n, tc), lambda j: (0, j))
    shape = jax.ShapeDtypeStruct((n, d), F32)
    out = pl.pallas_call(
        body, name=name, grid=(d // tc,),
        in_specs=[pl.BlockSpec((N_DEV, rows, tc), lambda j: (0, 0, j))] * k + [spec] * (3 * k),
        out_specs=[spec] * (4 * k), out_shape=[shape] * (4 * k),
        compiler_params=_params("arbitrary"),
    )(*parts, *ws, *ms, *vs)
    return [out[4 * i:4 * i + 4] for i in range(k)]


def _pad_rows(a, rows):
    return jnp.pad(a, ((0, rows - a.shape[0]), (0, 0)))


def _row1(vec, width=D_MODEL):
    return jnp.pad(vec.reshape(1, -1), ((0, 0), (0, width - vec.shape[-1])))


COLUMN_SHARDED = ("ffn1_w_gate", "ffn1_w_up", "w_in", "ffn2_w_gate", "ffn2_w_up")
VEC_NAMES = ("ffn1_norm", "mix_norm", "ffn2_norm", "b_forget", "pool_scale", "q_norm", "k_norm", "out_norm_pool",
             "out_norm_attn")
VEC_ROWS = 16
LOSS_ROW = len(VEC_NAMES)


def _pack_vector_grads(parts, loss_part, name):
    names = [n for n in VEC_NAMES if n in parts]
    extra = [] if loss_part is None else [loss_part]

    def body(*refs):
        out_ref = refs[-1]
        out_ref[...] = jnp.zeros_like(out_ref)
        lane = lax.broadcasted_iota(jnp.int32, (1, LANES), 1)
        for n, ref in zip(names, refs):
            val = ref[...]
            if n in ("q_norm", "k_norm"):
                val = val[:, 0:LANES] + val[:, LANES:2 * LANES] + val[:, 2 * LANES:3 * LANES] + val[:, 3 * LANES:]
                val = jnp.where(lane < HEAD_DIM, val + pltpu.roll(val, HEAD_DIM, 1), 0.0)
            out_ref[pl.ds(VEC_NAMES.index(n), 1), pl.ds(0, val.shape[1])] = val
        if extra:
            out_ref[pl.ds(LOSS_ROW, 1), pl.ds(0, 1)] = refs[len(names)][...]

    vmem = pl.BlockSpec(memory_space=pltpu.VMEM)
    return pl.pallas_call(
        body, name=name, in_specs=[vmem] * (len(names) + len(extra)), out_specs=vmem,
        out_shape=jax.ShapeDtypeStruct((VEC_ROWS, D_MODEL), F32),
    )(*[parts[n] for n in names], *extra)


def _small_adamw(vec_all, pool_all, vec_params, pool_params):
    nv = len(vec_params)
    pool_rows = pool_params[0].shape[0]

    def body(*refs):
        vec_ref, pool_ref = refs[0], refs[1]
        ins = refs[2:2 + 3 * nv + 3]
        outs = refs[2 + 3 * nv + 3:-1]
        rows = refs[-1]
        total = vec_ref[pl.ds(0, VEC_ROWS), :]
        for dev in range(1, N_DEV):
            total = total + vec_ref[pl.ds(dev * VEC_ROWS, VEC_ROWS), :]
        rows[...] = total
        outs[4 * nv + 4][...] = rows[pl.ds(LOSS_ROW, 1), pl.ds(0, 1)]
        for i in range(nv):
            w_ref, m_ref, v_ref = ins[3 * i:3 * i + 3]
            g = rows[pl.ds(i, 1), pl.ds(0, w_ref.shape[1])]
            outs[4 * i][...] = g
            outs[4 * i + 1][...], outs[4 * i + 2][...], outs[4 * i + 3][...] = _adamw_update(
                w_ref[...], g, m_ref[...], v_ref[...])
        g = pool_ref[pl.ds(0, pool_rows), :].astype(F32)
        for dev in range(1, N_DEV):
            g = g + pool_ref[pl.ds(dev * pool_rows, pool_rows), :].astype(F32)
        w_ref, m_ref, v_ref = ins[3 * nv:]
        outs[4 * nv][...] = g
        outs[4 * nv + 1][...], outs[4 * nv + 2][...], outs[4 * nv + 3][...] = _adamw_update(
            w_ref[...], g, m_ref[...], v_ref[...])

    vmem = pl.BlockSpec(memory_space=pltpu.VMEM)
    flat = [a for trio in vec_params for a in trio] + list(pool_params)
    out_shape = []
    for trio in list(vec_params) + [pool_params]:
        out_shape += [jax.ShapeDtypeStruct(trio[0].shape, F32)] * 4
    out_shape.append(jax.ShapeDtypeStruct((1, 1), F32))
    return pl.pallas_call(
        body, name="adamw_small", in_specs=[vmem] * (2 + len(flat)), out_specs=[vmem] * len(out_shape),
        out_shape=out_shape, scratch_shapes=[pltpu.VMEM((VEC_ROWS, D_MODEL), F32)],
    )(vec_all, pool_all, *flat)


def kernel(x, ffn1_norm, ffn1_w_gate, ffn1_w_up, ffn1_w_down, mix_norm, w_in, b_forget, pool_w, pool_scale, q_norm, k_norm, out_norm_pool, out_norm_attn, w_out, ffn2_norm, ffn2_w_gate, ffn2_w_up, ffn2_w_down, loss_target, m_ffn1_norm, m_ffn1_w_gate, m_ffn1_w_up, m_ffn1_w_down, m_mix_norm, m_w_in, m_b_forget, m_pool_w, m_pool_scale, m_q_norm, m_k_norm, m_out_norm_pool, m_out_norm_attn, m_w_out, m_ffn2_norm, m_ffn2_w_gate, m_ffn2_w_up, m_ffn2_w_down, v_ffn1_norm, v_ffn1_w_gate, v_ffn1_w_up, v_ffn1_w_down, v_mix_norm, v_w_in, v_b_forget, v_pool_w, v_pool_scale, v_q_norm, v_k_norm, v_out_norm_pool, v_out_norm_attn, v_w_out, v_ffn2_norm, v_ffn2_w_gate, v_ffn2_w_up, v_ffn2_w_down):
    bsz, seq, d = x.shape
    t = bsz * seq
    x0 = x.reshape(t, d)
    target = loss_target.reshape(t, d)
    in_rows = -(-w_in.shape[1] // BF16_ROWS) * BF16_ROWS

    slabs = [s.astype(BF16) for s in (ffn1_w_gate.T, ffn1_w_up.T, ffn1_w_down, _pad_rows(w_in.T, in_rows), w_out,
                                       ffn2_w_gate.T, ffn2_w_up.T, ffn2_w_down)]
    first, started = _copies_start([slabs[0:2]], True, "gather_start_first", relayed=(0,))
    rest, started = _copies_start([slabs[2:3], slabs[3:4], slabs[4:5], slabs[5:8]], True, "gather_start", after=started,
                                  relayed=(3,))
    gathers = first + rest

    g1, gm, g2 = ffn1_norm.reshape(1, d), mix_norm.reshape(1, d), ffn2_norm.reshape(1, d)
    bf_row = _row1(b_forget, LANES)
    gq = jnp.tile(q_norm, N_HEADS).reshape(1, ATTN_WIDTH)
    gk = jnp.tile(k_norm, N_HEADS).reshape(1, ATTN_WIDTH)
    scale_row = pool_scale.reshape(1, POOL_WIDTH)
    gp, ga = out_norm_pool.reshape(1, POOL_WIDTH), out_norm_attn.reshape(1, ATTN_WIDTH)

    wg1, wu1 = _relayed_wait(_relay_to_sibling(gathers[0], "gather_relay_ffn1_up", started), started,
                             "gather_wait_ffn1_up")
    h1, sa1, sb1, s1 = _ffn_up(x0, g1, wg1, wu1, "ffn1_up")
    (wd1,) = _copies_wait(gathers[1], True, s1, "gather_wait_ffn1_down")
    (x1,) = _ffn_down(s1, wd1, x0, None, "ffn1_down")
    (win_g,) = _copies_wait(gathers[2], True, x1, "gather_wait_w_in")
    win_t = _repack_rows(win_g, in_rows, w_in.shape[1], N_DEV, "w_in_rows")
    hm, pv, q, k, v, f = _mix_in_fwd(x1, gm, win_t)
    pooled, mixed, y_pool = _pool_fwd(pv, pool_w, scale_row, gp, bsz, seq)
    qp, kp = _attn_prep_fwd(q, k, f, bf_row, gq, gk, bsz, seq)
    o, lse = _flash_fwd(qp, kp, v, bsz, seq)
    relayed_ffn2 = _relay_to_sibling(gathers[4], "gather_relay_ffn2", o)
    (wout,) = _copies_wait(gathers[3], True, [o, relayed_ffn2[4][0]], "gather_wait_w_out")
    ycat, x2 = _mix_out_fwd(o, y_pool, x1, ga, wout)
    wg2, wu2, wd2 = _relayed_wait(relayed_ffn2, x2, "gather_wait_ffn2")
    h2, sa2, sb2, s2 = _ffn_up(x2, g2, wg2, wu2, "ffn2_up")
    dx3, dyh2, loss_part = _ffn_down(s2, wd2, x2, target, "ffn2_down")

    da2, db2, dwg2, dwu2 = _ffn_bwd_act(dyh2, sa2, sb2, h2, wd2, dx3, "ffn2_bwd_act")
    (dwd2,) = _wgrad([s2], dyh2, da2, "ffn2_down_wgrad")
    (sent_ffn2,), tok = _copies_start([[dwg2, dwu2, dwd2]], False, "exchange_start_ffn2")
    dx2, dg2 = _ffn_bwd_dx(da2, db2, dx3, x2, g2, wg2, wu2, tok, "ffn2_bwd_dx")
    dwout, dy_pool, do, dga = _mix_out_bwd(dx2, o, ycat, ga, wout)
    (sent_out,), tok = _copies_start([[dwout]], False, "exchange_start_w_out")
    dqp, dkp, dv = _flash_bwd(qp, kp, v, o, do, lse, tok, bsz, seq)
    dq, dk, df, dgq, dgk, dbf = _attn_prep_bwd(dqp, dkp, q, k, f, bf_row, gq, gk, bsz, seq)
    dpv, dpool_w, dscale, dgp = _pool_bwd(dy_pool, mixed, pooled, pool_w, scale_row, gp, bsz, seq)
    dwin, dx1, dyh1, dgm = _mix_in_bwd(dpv, dq, dk, dv, df, hm, x1, dx2, gm, win_t)
    dwin_blocks = _repack_rows(dwin, w_in.shape[1], in_rows, N_DEV, "w_in_grad_blocks")
    (sent_in,), tok = _copies_start([[dwin_blocks]], False, "exchange_start_w_in")
    (dwd1,) = _wgrad([s1], dyh1, tok, "ffn1_down_wgrad")
    (sent_down1,), tok = _copies_start([[dwd1]], False, "exchange_start_ffn1_down", after=tok)
    da1, db1, dwg1, dwu1 = _ffn_bwd_act(dyh1, sa1, sb1, h1, wd1, tok, "ffn1_bwd_act")
    (sent_up1,), tok = _copies_start([[dwg1, dwu1]], False, "exchange_start_ffn1_up", after=tok)
    dx0, dg1 = _ffn_bwd_dx(da1, db1, dx1, x0, g1, wg1, wu1, tok, "ffn1_bwd_dx")

    pool_rows = POOL_GROUPS * POOL_GROUP_DIM
    packed = _pack_vector_grads(dict(ffn1_norm=dg1, mix_norm=dgm, ffn2_norm=dg2, b_forget=dbf, pool_scale=dscale,
                                     q_norm=dgq, k_norm=dgk, out_norm_pool=dgp, out_norm_attn=dga), loss_part,
                                "pack_vector_grads")
    pool_part = dpool_w.reshape(pool_rows, POOL_GROUP_DIM).astype(BF16)
    (sent_small,), tok = _copies_start([[packed, pool_part]], True, "small_grads_start")

    weights = dict(ffn1_norm=ffn1_norm, ffn1_w_gate=ffn1_w_gate, ffn1_w_up=ffn1_w_up, ffn1_w_down=ffn1_w_down,
                   mix_norm=mix_norm, w_in=w_in, b_forget=b_forget, pool_w=pool_w, pool_scale=pool_scale,
                   q_norm=q_norm, k_norm=k_norm, out_norm_pool=out_norm_pool, out_norm_attn=out_norm_attn,
                   w_out=w_out, ffn2_norm=ffn2_norm, ffn2_w_gate=ffn2_w_gate, ffn2_w_up=ffn2_w_up,
                   ffn2_w_down=ffn2_w_down)
    m_in = dict(ffn1_norm=m_ffn1_norm, ffn1_w_gate=m_ffn1_w_gate, ffn1_w_up=m_ffn1_w_up, ffn1_w_down=m_ffn1_w_down,
                mix_norm=m_mix_norm, w_in=m_w_in, b_forget=m_b_forget, pool_w=m_pool_w, pool_scale=m_pool_scale,
                q_norm=m_q_norm, k_norm=m_k_norm, out_norm_pool=m_out_norm_pool, out_norm_attn=m_out_norm_attn,
                w_out=m_w_out, ffn2_norm=m_ffn2_norm, ffn2_w_gate=m_ffn2_w_gate, ffn2_w_up=m_ffn2_w_up,
                ffn2_w_down=m_ffn2_w_down)
    v_in = dict(ffn1_norm=v_ffn1_norm, ffn1_w_gate=v_ffn1_w_gate, ffn1_w_up=v_ffn1_w_up, ffn1_w_down=v_ffn1_w_down,
                mix_norm=v_mix_norm, w_in=v_w_in, b_forget=v_b_forget, pool_w=v_pool_w, pool_scale=v_pool_scale,
                q_norm=v_q_norm, k_norm=v_k_norm, out_norm_pool=v_out_norm_pool, out_norm_attn=v_out_norm_attn,
                w_out=v_w_out, ffn2_norm=v_ffn2_norm, ffn2_w_gate=v_ffn2_w_gate, ffn2_w_up=v_ffn2_w_up,
                ffn2_w_down=v_ffn2_w_down)
    grads, delta, new_m, new_v = {}, {}, {}, {}
    after = [tok]
    plan = ((sent_ffn2, "ffn2", ("ffn2_w_gate", "ffn2_w_up", "ffn2_w_down")), (sent_out, "w_out", ("w_out",)),
            (sent_in, "w_in", ("w_in",)), (sent_down1, "ffn1_down", ("ffn1_w_down",)),
            (sent_up1, "ffn1_up", ("ffn1_w_gate", "ffn1_w_up")))
    for sent, tag, names in plan:
        parts = _copies_wait(sent, False, after, f"exchange_wait_{tag}")
        turn = {n: (lambda a: a.T) if n in COLUMN_SHARDED else (lambda a: a) for n in names}
        done = _sum_adamw(parts, *[[turn[n](z[n]) for n in names] for z in (weights, m_in, v_in)], f"adamw_{tag}")
        for n, four in zip(names, done):
            grads[n], delta[n], new_m[n], new_v[n] = (turn[n](a) for a in four)
        after = [four[3] for four in done]
    vec_all, pool_all = _copies_wait(sent_small, True, after, "small_grads_wait")
    as_row = lambda a: a.reshape(1, -1)
    as_pool = lambda a: a.reshape(pool_rows, POOL_GROUP_DIM)
    small = _small_adamw(vec_all, pool_all,
                         [tuple(as_row(z[n]) for z in (weights, m_in, v_in)) for n in VEC_NAMES],
                         tuple(as_pool(z["pool_w"]) for z in (weights, m_in, v_in)))
    for i, n in enumerate(VEC_NAMES + ("pool_w",)):
        grads[n], delta[n], new_m[n], new_v[n] = (a.reshape(weights[n].shape) for a in small[4 * i:4 * i + 4])
    loss = small[-1].reshape(())

    order = ("ffn1_norm", "ffn1_w_gate", "ffn1_w_up", "ffn1_w_down", "mix_norm", "w_in", "b_forget", "pool_w",
             "pool_scale", "q_norm", "k_norm", "out_norm_pool", "out_norm_attn", "w_out", "ffn2_norm", "ffn2_w_gate",
             "ffn2_w_up", "ffn2_w_down")
    return (loss, dx0.reshape(bsz, seq, d), *[grads[n] for n in order], *[delta[n] for n in order],
            *[new_m[n] for n in order], *[new_v[n] for n in order])
```

```python
import jax
import jax.numpy as jnp
from jax import lax
from jax.experimental import pallas as pl
from jax.experimental.pallas import tpu as pltpu

F32 = jnp.float32
BF16 = jnp.bfloat16

EPS = 1e-6
D_MODEL = 1024
N_HEADS = 8
HEAD_DIM = 64
POOL_WIDTH = 512
ATTN_WIDTH = 512
POOL_GROUPS = 4
POOL_GROUP_DIM = 128
POOL_WINDOWS = (2, 4, 8, 16)
POOL_HALO = 16
MIX_PAD = POOL_WIDTH + 3 * ATTN_WIDTH + 128
N_DEV = 8
BF16_ROWS = 16
LANES = 128
VMEM_LIMIT = 56 * 1024 * 1024

ADAM_LR = 0.001
ADAM_B1 = 0.9
ADAM_B2 = 0.999
ADAM_EPS = 1e-08
ADAM_WD = 0.01
ADAM_STEP = 10


def _params(*sem):
    return pltpu.CompilerParams(dimension_semantics=sem, vmem_limit_bytes=VMEM_LIMIT)


def _dot(a, b):
    return jnp.dot(a, b, preferred_element_type=F32)


def _dot_nt(a, b):
    return lax.dot_general(a, b, (((1,), (1,)), ((), ())), preferred_element_type=F32)


def _dot_tn(a, b):
    return lax.dot_general(a, b, (((0,), (0,)), ((), ())), preferred_element_type=F32)


def _resident(shape):
    return pl.BlockSpec(shape, lambda *_: (0,) * len(shape), pipeline_mode=pl.Buffered(1))


def _rows(tm, width):
    return pl.BlockSpec((tm, width), lambda i: (i, 0))


ORDER_ONLY = pl.BlockSpec(memory_space=pl.ANY)


def _rms_scale(x):
    return lax.rsqrt(jnp.mean(x * x, axis=-1, keepdims=True) + EPS)


def _rms_bwd(dh, x, gain):
    r = _rms_scale(x)
    n = x * r
    dgain = jnp.sum(dh * n, axis=0, keepdims=True)
    dn = dh * gain
    dx = r * (dn - n * jnp.mean(dn * n, axis=-1, keepdims=True))
    return dx, dgain


def _split3(x):
    hi = x.astype(BF16)
    r1 = x - hi.astype(F32)
    mid = r1.astype(BF16)
    lo = (r1 - mid.astype(F32)).astype(BF16)
    return hi, mid, lo


FF_CHUNK = 256


def _swiglu_parts(a, b):
    sig = jax.nn.sigmoid(a)
    silu = a * sig
    return (b * (sig + silu * (1.0 - sig))).astype(BF16), silu.astype(BF16), (silu * b).astype(BF16)


def _ffn_up(x, gain, wg_t, wu_t, name):
    t, d = x.shape
    f = wg_t.shape[0]
    tm = 512

    def body(x_ref, g_ref, wg_ref, wu_ref, h_ref, sa_ref, sb_ref, s_ref):
        xv = x_ref[...]
        h = (xv * _rms_scale(xv) * g_ref[...]).astype(BF16)
        h_ref[...] = h
        for c in range(f // FF_CHUNK):
            sl = pl.ds(c * FF_CHUNK, FF_CHUNK)
            sa_ref[:, sl], sb_ref[:, sl], s_ref[:, sl] = _swiglu_parts(_dot_nt(h, wg_ref[sl, :]), _dot_nt(h, wu_ref[sl, :]))

    wide = jax.ShapeDtypeStruct((t, f), BF16)
    return pl.pallas_call(
        body, name=name, grid=(t // tm,),
        in_specs=[_rows(tm, d), _resident((1, d)), _resident((f, d)), _resident((f, d))],
        out_specs=[_rows(tm, d), _rows(tm, f), _rows(tm, f), _rows(tm, f)],
        out_shape=[jax.ShapeDtypeStruct((t, d), BF16), wide, wide, wide],
        compiler_params=_params("arbitrary"),
    )(x, gain, wg_t, wu_t)


def _ffn_down(s, wd, x, target, name):
    t, d = x.shape
    f = wd.shape[0]
    tm = 512
    with_loss = target is not None

    def body(*refs):
        if with_loss:
            s_ref, w_ref, x_ref, t_ref, dy_ref, dyh_ref, loss_ref = refs
        else:
            s_ref, w_ref, x_ref, y_ref = refs
        y = x_ref[...] + 0.5 * _dot(s_ref[...], w_ref[...])
        if with_loss:
            e = y - t_ref[...]
            dy = e * (1.0 / d)
            dy_ref[...] = dy
            dyh_ref[...] = (0.5 * dy).astype(BF16)

            @pl.when(pl.program_id(0) == 0)
            def _():
                loss_ref[...] = jnp.zeros_like(loss_ref)

            part = jnp.sum(jnp.sum(e * e, axis=0, keepdims=True), axis=1, keepdims=True)
            loss_ref[...] += part * (0.5 / d)
        else:
            y_ref[...] = y

    in_specs = [_rows(tm, f), _resident((f, d)), _rows(tm, d)]
    args = [s, wd, x]
    if with_loss:
        in_specs.append(_rows(tm, d))
        args.append(target)
        out_shape = [jax.ShapeDtypeStruct((t, d), F32), jax.ShapeDtypeStruct((t, d), BF16),
                     jax.ShapeDtypeStruct((1, 1), F32)]
        out_specs = [_rows(tm, d), _rows(tm, d), pl.BlockSpec((1, 1), lambda i: (0, 0))]
    else:
        out_shape = [jax.ShapeDtypeStruct((t, d), F32)]
        out_specs = [_rows(tm, d)]
    return pl.pallas_call(
        body, name=name, grid=(t // tm,), in_specs=in_specs, out_specs=out_specs, out_shape=out_shape,
        compiler_params=_params("arbitrary"),
    )(*args)


def _ffn_bwd_act(dyh, sa, sb, h, wd, after, name):
    t, d = dyh.shape
    f = wd.shape[0]
    tn = f // 2
    tk = 512
    nk = t // tk

    def body(dy_ref, sa_ref, sb_ref, h_ref, wd_ref, after_ref, da_ref, db_ref, dwg_ref, dwu_ref, acc_g, acc_u):
        k = pl.program_id(1)

        @pl.when(k == 0)
        def _():
            acc_g[...] = jnp.zeros_like(acc_g)
            acc_u[...] = jnp.zeros_like(acc_u)

        ds = _dot_nt(dy_ref[...], wd_ref[...])
        da = (ds * sa_ref[...].astype(F32)).astype(BF16)
        db = (ds * sb_ref[...].astype(F32)).astype(BF16)
        da_ref[...] = da
        db_ref[...] = db
        hv = h_ref[...]
        acc_g[...] += _dot_tn(da, hv)
        acc_u[...] += _dot_tn(db, hv)

        @pl.when(k == nk - 1)
        def _():
            dwg_ref[...] = acc_g[...].astype(BF16)
            dwu_ref[...] = acc_u[...].astype(BF16)

    tokens = pl.BlockSpec((tk, d), lambda j, k: (k, 0))
    wide = pl.BlockSpec((tk, tn), lambda j, k: (k, j))
    weight = pl.BlockSpec((tn, d), lambda j, k: (j, 0))
    return pl.pallas_call(
        body, name=name, grid=(f // tn, nk),
        in_specs=[tokens, wide, wide, tokens, weight, ORDER_ONLY],
        out_specs=[wide, wide, weight, weight],
        out_shape=[jax.ShapeDtypeStruct((t, f), BF16)] * 2 + [jax.ShapeDtypeStruct((f, d), BF16)] * 2,
        scratch_shapes=[pltpu.VMEM((tn, d), F32)] * 2,
        compiler_params=_params("arbitrary", "arbitrary"),
    )(dyh, sa, sb, h, wd, after)


def _ffn_bwd_dx(da, db, dy, x, gain, wg_t, wu_t, after, name):
    t, d = x.shape
    f = wg_t.shape[0]
    tm = 512

    def body(da_ref, db_ref, dy_ref, x_ref, g_ref, wg_ref, wu_ref, after_ref, dx_ref, dg_ref):
        dh = _dot(da_ref[...], wg_ref[...]) + _dot(db_ref[...], wu_ref[...])
        dx, dgain = _rms_bwd(dh, x_ref[...], g_ref[...])
        dx_ref[...] = dy_ref[...] + dx

        @pl.when(pl.program_id(0) == 0)
        def _():
            dg_ref[...] = jnp.zeros_like(dg_ref)

        dg_ref[...] += dgain

    return pl.pallas_call(
        body, name=name, grid=(t // tm,),
        in_specs=[_rows(tm, f), _rows(tm, f), _rows(tm, d), _rows(tm, d), _resident((1, d)), _resident((f, d)),
                  _resident((f, d)), ORDER_ONLY],
        out_specs=[_rows(tm, d), pl.BlockSpec((1, d), lambda i: (0, 0))],
        out_shape=[jax.ShapeDtypeStruct((t, d), F32), jax.ShapeDtypeStruct((1, d), F32)],
        compiler_params=_params("arbitrary"),
    )(da, db, dy, x, gain, wg_t, wu_t, after)


def _wgrad(lhs, b, after, name):
    t, n = lhs[0].shape
    d = b.shape[1]
    m = len(lhs)
    tn = n // 2 if n * d * m > (4 << 20) else n
    tk = 1024
    nk = t // tk

    def body(*refs):
        a_refs, b_ref, o_refs, accs = refs[:m], refs[m], refs[m + 2:2 * m + 2], refs[2 * m + 2:]
        k = pl.program_id(1)

        @pl.when(k == 0)
        def _():
            for acc in accs:
                acc[...] = jnp.zeros_like(acc)

        bv = b_ref[...]
        for a_ref, acc in zip(a_refs, accs):
            acc[...] += _dot_tn(a_ref[...], bv)

        @pl.when(k == nk - 1)
        def _():
            for o_ref, acc in zip(o_refs, accs):
                o_ref[...] = acc[...].astype(BF16)

    return pl.pallas_call(
        body, name=name, grid=(n // tn, nk),
        in_specs=[pl.BlockSpec((tk, tn), lambda j, k: (k, j))] * m + [pl.BlockSpec((tk, d), lambda j, k: (k, 0)),
                                                                       ORDER_ONLY],
        out_specs=[pl.BlockSpec((tn, d), lambda j, k: (j, 0))] * m,
        out_shape=[jax.ShapeDtypeStruct((n, d), BF16)] * m,
        scratch_shapes=[pltpu.VMEM((tn, d), F32)] * m,
        compiler_params=_params("arbitrary", "arbitrary"),
    )(*lhs, b, after)


def _repack_rows(a, rows_in, rows_out, blocks, name):
    total, d = a.shape
    real = min(rows_in, rows_out)

    def body(a_ref, o_ref, wide_in, wide_out):
        wide_in[...] = a_ref[...].astype(F32)
        wide_out[...] = jnp.zeros_like(wide_out)
        for j in range(blocks):
            wide_out[pl.ds(j * rows_out, real), :] = wide_in[pl.ds(j * rows_in, real), :]
        o_ref[...] = wide_out[...].astype(BF16)

    full = pl.BlockSpec((total, d), lambda i: (0, 0))
    return pl.pallas_call(
        body, name=name, grid=(1,), in_specs=[full], out_specs=full, out_shape=jax.ShapeDtypeStruct((total, d), BF16),
        scratch_shapes=[pltpu.VMEM((total, d), F32)] * 2,
        compiler_params=_params("arbitrary"),
    )(a)


def _mix_in_fwd(x, gain, w_in_t):
    t, d = x.shape
    tm = 1024
    pw, aw = POOL_WIDTH, ATTN_WIDTH

    def body(x_ref, g_ref, w_ref, hm_ref, pv_ref, q_ref, k_ref, v_ref, f_ref):
        xv = x_ref[...]
        hm = (xv * _rms_scale(xv) * g_ref[...]).astype(BF16)
        hm_ref[...] = hm
        pv_ref[...] = _dot_nt(hm, w_ref[pl.ds(0, pw), :])
        q_ref[...] = _dot_nt(hm, w_ref[pl.ds(pw, aw), :])
        k_ref[...] = _dot_nt(hm, w_ref[pl.ds(pw + aw, aw), :])
        v_ref[...] = _dot_nt(hm, w_ref[pl.ds(pw + 2 * aw, aw), :]).astype(BF16)
        f_ref[...] = _dot_nt(hm, w_ref[pl.ds(pw + 3 * aw, LANES), :])

    return pl.pallas_call(
        body, name="mix_in_fwd", grid=(t // tm,),
        in_specs=[_rows(tm, d), _resident((1, d)), _resident((MIX_PAD, d))],
        out_specs=[_rows(tm, d), _rows(tm, pw), _rows(tm, aw), _rows(tm, aw), _rows(tm, aw), _rows(tm, LANES)],
        out_shape=[jax.ShapeDtypeStruct((t, d), BF16), jax.ShapeDtypeStruct((t, pw), F32),
                   jax.ShapeDtypeStruct((t, aw), F32), jax.ShapeDtypeStruct((t, aw), F32),
                   jax.ShapeDtypeStruct((t, aw), BF16), jax.ShapeDtypeStruct((t, LANES), F32)],
        compiler_params=_params("arbitrary"),
    )(x, gain, w_in_t)


def _pool_fwd(pv, pool_w, pool_scale, gain, bsz, seq):
    ts = 512
    ns = seq // ts
    pw = POOL_WIDTH

    def body(pv_ref, w_ref, sc_ref, g_ref, pooled_ref, mixed_ref, y_ref, ext):
        s = pl.program_id(1)

        @pl.when(s == 0)
        def _():
            ext[pl.ds(0, POOL_HALO), :] = jnp.zeros((POOL_HALO, pw), F32)

        p = pv_ref[...]
        ext[pl.ds(POOL_HALO, ts), :] = p
        pos = s * ts + lax.broadcasted_iota(jnp.int32, (ts, 1), 0)
        parts = []
        for g, w in enumerate(POOL_WINDOWS):
            lanes = pl.ds(g * POOL_GROUP_DIM, POOL_GROUP_DIM)
            win = ext[pl.ds(POOL_HALO, ts), lanes]
            for i in range(1, w):
                win = win + ext[pl.ds(POOL_HALO - i, ts), lanes]
            cnt = jnp.minimum(pos + 1, w).astype(F32)
            pooled = (win / cnt - ext[pl.ds(POOL_HALO, ts), lanes]).astype(BF16)
            pooled_ref[:, lanes] = pooled
            parts.append(_dot(pooled, w_ref[g].astype(BF16)))
        mixed = jnp.concatenate(parts, axis=1)
        mixed_ref[...] = mixed
        pm = mixed * sc_ref[...]
        y_ref[...] = (pm * _rms_scale(pm) * g_ref[...]).astype(BF16)
        ext[pl.ds(0, POOL_HALO), :] = p[ts - POOL_HALO:, :]

    blk = pl.BlockSpec((ts, pw), lambda b, s: (b * ns + s, 0))
    t = bsz * seq
    return pl.pallas_call(
        body, name="pool_fwd", grid=(bsz, ns),
        in_specs=[blk, pl.BlockSpec((POOL_GROUPS, POOL_GROUP_DIM, POOL_GROUP_DIM), lambda b, s: (0, 0, 0)),
                  pl.BlockSpec((1, pw), lambda b, s: (0, 0)), pl.BlockSpec((1, pw), lambda b, s: (0, 0))],
        out_specs=[blk, blk, blk],
        out_shape=[jax.ShapeDtypeStruct((t, pw), BF16), jax.ShapeDtypeStruct((t, pw), F32),
                   jax.ShapeDtypeStruct((t, pw), BF16)],
        scratch_shapes=[pltpu.VMEM((POOL_HALO + ts, pw), F32)],
        compiler_params=_params("arbitrary", "arbitrary"),
    )(pv, pool_w, pool_scale, gain)


def _pool_bwd(dy, mixed, pooled, pool_w, pool_scale, gain, bsz, seq):
    ts = 512
    ns = seq // ts
    pw = POOL_WIDTH

    def body(dy_ref, mixed_ref, pooled_ref, w_ref, sc_ref, g_ref, dpv_ref, dw_ref, dsc_ref, dg_ref, ext):
        b = pl.program_id(0)
        sr = pl.program_id(1)
        s = ns - 1 - sr

        @pl.when(jnp.logical_and(b == 0, sr == 0))
        def _():
            dw_ref[...] = jnp.zeros_like(dw_ref)
            dsc_ref[...] = jnp.zeros_like(dsc_ref)
            dg_ref[...] = jnp.zeros_like(dg_ref)

        @pl.when(sr == 0)
        def _():
            ext[pl.ds(ts, POOL_HALO), :] = jnp.zeros((POOL_HALO, pw), F32)

        mixed = mixed_ref[...]
        sc = sc_ref[...]
        dpm, dgain = _rms_bwd(dy_ref[...], mixed * sc, g_ref[...])
        dg_ref[...] += dgain
        dsc_ref[...] += jnp.sum(dpm * mixed, axis=0, keepdims=True)
        dmixed = (dpm * sc).astype(BF16)
        pos = s * ts + lax.broadcasted_iota(jnp.int32, (ts, 1), 0)
        dpooled = []
        for g, w in enumerate(POOL_WINDOWS):
            lanes = pl.ds(g * POOL_GROUP_DIM, POOL_GROUP_DIM)
            dm = dmixed[:, g * POOL_GROUP_DIM:(g + 1) * POOL_GROUP_DIM]
            dw_ref[g] += _dot_tn(pooled_ref[:, lanes], dm)
            dp = _dot_nt(dm, w_ref[g].astype(BF16))
            dpooled.append(dp)
            cnt = jnp.minimum(pos + 1, w).astype(F32)
            ext[pl.ds(0, ts), lanes] = dp / cnt
        for g, w in enumerate(POOL_WINDOWS):
            lanes = pl.ds(g * POOL_GROUP_DIM, POOL_GROUP_DIM)
            win = ext[pl.ds(0, ts), lanes]
            for i in range(1, w):
                win = win + ext[pl.ds(i, ts), lanes]
            dpv_ref[:, lanes] = (win - dpooled[g]).astype(BF16)
        head = ext[pl.ds(0, POOL_HALO), :]
        ext[pl.ds(ts, POOL_HALO), :] = head

    blk = pl.BlockSpec((ts, pw), lambda b, s: (b * ns + (ns - 1 - s), 0))
    vec = pl.BlockSpec((1, pw), lambda b, s: (0, 0))
    wspec = pl.BlockSpec((POOL_GROUPS, POOL_GROUP_DIM, POOL_GROUP_DIM), lambda b, s: (0, 0, 0))
    t = bsz * seq
    return pl.pallas_call(
        body, name="pool_bwd", grid=(bsz, ns),
        in_specs=[blk, blk, blk, wspec, vec, vec],
        out_specs=[blk, wspec, vec, vec],
        out_shape=[jax.ShapeDtypeStruct((t, pw), BF16),
                   jax.ShapeDtypeStruct((POOL_GROUPS, POOL_GROUP_DIM, POOL_GROUP_DIM), F32),
                   jax.ShapeDtypeStruct((1, pw), F32), jax.ShapeDtypeStruct((1, pw), F32)],
        scratch_shapes=[pltpu.VMEM((ts + POOL_HALO, pw), F32)],
        compiler_params=_params("arbitrary", "arbitrary"),
    )(dy, mixed, pooled, pool_w, pool_scale, gain)


AUX_ONE = 64
AUX_F = 67

ATTN_PREP_ROWS = 512


def _seg_ones(width, seg):
    r = lax.broadcasted_iota(jnp.int32, (width, width), 0) // seg
    c = lax.broadcasted_iota(jnp.int32, (width, width), 1) // seg
    return (r == c).astype(BF16)


def _tri_ones(n, lower):
    r = lax.broadcasted_iota(jnp.int32, (n, n), 0)
    c = lax.broadcasted_iota(jnp.int32, (n, n), 1)
    return ((r >= c) if lower else (r <= c)).astype(BF16)


def _place_pieces(first_lane):
    r = lax.broadcasted_iota(jnp.int32, (3 * LANES, N_HEADS * LANES), 0)
    c = lax.broadcasted_iota(jnp.int32, (3 * LANES, N_HEADS * LANES), 1)
    piece, head = r // LANES, r % LANES
    return jnp.logical_and(head < N_HEADS, c == head * LANES + first_lane + piece).astype(BF16)


def _head_sums(x, seg_ones):
    return _dot(x.astype(BF16), seg_ones)


def _log_sigmoid(x):
    return jnp.minimum(x, 0.0) - jnp.log(1.0 + jnp.exp(-jnp.abs(x)))


def _attn_prep_fwd(q, k, f, b_forget, q_gain, k_gain, bsz, seq):
    ts = ATTN_PREP_ROWS
    ns = seq // ts
    aw = ATTN_WIDTH
    t = bsz * seq
    seg = _seg_ones(aw, HEAD_DIM)
    tri = _tri_ones(ts, True)

    def body(q_ref, k_ref, f_ref, bf_ref, gq_ref, gk_ref, seg_ref, tri_ref, place_ref, qp_ref, kp_ref, carry):
        s = pl.program_id(1)

        @pl.when(s == 0)
        def _():
            carry[...] = jnp.zeros_like(carry)

        logf = _log_sigmoid(f_ref[...] + bf_ref[...])
        hi, mid, lo = _split3(logf)
        tri_v = tri_ref[...]
        fc = _dot(tri_v, hi) + _dot(tri_v, mid) + _dot(tri_v, lo) + carry[pl.ds(0, 1), :]
        carry[pl.ds(0, 1), :] = fc[ts - 1:, :]
        pcs = jnp.concatenate(_split3(fc), axis=1)
        lane = lax.broadcasted_iota(jnp.int32, (1, LANES), 1)
        ones_q = jnp.logical_and(lane >= AUX_ONE, lane < AUX_ONE + 3).astype(F32)
        ones_k = jnp.logical_and(lane >= AUX_F, lane < AUX_F + 3).astype(F32)
        seg_v = seg_ref[...]
        placed = _dot(pcs, place_ref[...])

        def build(x_ref, g_ref, scale, out_ref, ones, for_keys):
            xv = x_ref[...]
            r = lax.rsqrt(_head_sums(xv * xv, seg_v) * (1.0 / HEAD_DIM) + EPS)
            xn = xv * r * g_ref[...] * scale
            for h in range(N_HEADS):
                pair = xn[:, (h // 2) * LANES:(h // 2 + 1) * LANES]
                feat = pair if h % 2 == 0 else pltpu.roll(pair, HEAD_DIM, 1)
                aux_h = placed[:, h * LANES:(h + 1) * LANES]
                if for_keys:
                    aux_h = -pltpu.roll(aux_h, LANES - (AUX_F - AUX_ONE), 1)
                out_ref[:, h * LANES:(h + 1) * LANES] = jnp.where(lane < HEAD_DIM, feat, aux_h + ones).astype(BF16)

        build(q_ref, gq_ref, 0.125, qp_ref, ones_q, False)
        build(k_ref, gk_ref, 1.0, kp_ref, ones_k, True)

    blk = pl.BlockSpec((ts, aw), lambda b, s: (b * ns + s, 0))
    fblk = pl.BlockSpec((ts, LANES), lambda b, s: (b * ns + s, 0))
    oblk = pl.BlockSpec((ts, N_HEADS * LANES), lambda b, s: (b * ns + s, 0))
    const = lambda shape: pl.BlockSpec(shape, lambda b, s: (0, 0))
    return pl.pallas_call(
        body, name="attn_prep_fwd", grid=(bsz, ns),
        in_specs=[blk, blk, fblk, const((1, LANES)), const((1, aw)), const((1, aw)), const((aw, aw)), const((ts, ts)),
                  const((3 * LANES, N_HEADS * LANES))],
        out_specs=[oblk, oblk],
        out_shape=[jax.ShapeDtypeStruct((t, N_HEADS * LANES), BF16)] * 2,
        scratch_shapes=[pltpu.VMEM((8, LANES), F32)],
        compiler_params=_params("arbitrary", "arbitrary"),
    )(q, k, f, b_forget, q_gain, k_gain, seg, tri, _place_pieces(AUX_F))


def _attn_prep_bwd(dqp, dkp, q, k, f, b_forget, q_gain, k_gain, bsz, seq):
    ts = ATTN_PREP_ROWS
    ns = seq // ts
    aw = ATTN_WIDTH
    t = bsz * seq
    seg = _seg_ones(aw, HEAD_DIM)
    tri = _tri_ones(ts, False)

    def body(dqp_ref, dkp_ref, q_ref, k_ref, f_ref, bf_ref, gq_ref, gk_ref, seg_ref, tri_ref,
             dq_ref, dk_ref, df_ref, dgq_ref, dgk_ref, dbf_ref, carry):
        b = pl.program_id(0)
        sr = pl.program_id(1)

        @pl.when(jnp.logical_and(b == 0, sr == 0))
        def _():
            dgq_ref[...] = jnp.zeros_like(dgq_ref)
            dgk_ref[...] = jnp.zeros_like(dgk_ref)
            dbf_ref[...] = jnp.zeros_like(dbf_ref)

        @pl.when(sr == 0)
        def _():
            carry[...] = jnp.zeros_like(carry)

        lane = lax.broadcasted_iota(jnp.int32, (1, LANES), 1)
        seg_v = seg_ref[...]

        def norm_bwd(dp_ref, x_ref, g_ref, scale, dx_ref, dgain_ref):
            parts = []
            for j in range(N_HEADS // 2):
                even = dp_ref[:, (2 * j) * LANES:(2 * j + 1) * LANES]
                odd = dp_ref[:, (2 * j + 1) * LANES:(2 * j + 2) * LANES]
                parts.append(jnp.where(lane < HEAD_DIM, even, pltpu.roll(odd, HEAD_DIM, 1)))
            dxn = jnp.concatenate(parts, axis=1) * scale
            xv = x_ref[...]
            r = lax.rsqrt(_head_sums(xv * xv, seg_v) * (1.0 / HEAD_DIM) + EPS)
            n = xv * r
            dgain_ref[...] += jnp.sum(dxn * n, axis=0, keepdims=True)
            dn = dxn * g_ref[...]
            m = _head_sums(dn * n, seg_v) * (1.0 / HEAD_DIM)
            dx_ref[...] = (r * (dn - n * m)).astype(BF16)

        norm_bwd(dqp_ref, q_ref, gq_ref, 0.125, dq_ref, dgq_ref)
        norm_bwd(dkp_ref, k_ref, gk_ref, 1.0, dk_ref, dgk_ref)

        dfc = jnp.zeros((ts, LANES), F32)
        for h in range(N_HEADS):
            cols = pl.ds(h * LANES, LANES)
            both = jnp.where(lane == AUX_F, dqp_ref[:, cols], 0.0) - jnp.where(lane == AUX_ONE, dkp_ref[:, cols], 0.0)
            dfc = jnp.where(lane == h, jnp.sum(both, axis=1, keepdims=True), dfc)
        hi, mid, lo = _split3(dfc)
        tri_v = tri_ref[...]
        dlogf = _dot(tri_v, hi) + _dot(tri_v, mid) + _dot(tri_v, lo) + carry[pl.ds(0, 1), :]
        carry[pl.ds(0, 1), :] = dlogf[0:1, :]
        df = jnp.where(lane < N_HEADS, dlogf * jax.nn.sigmoid(-(f_ref[...] + bf_ref[...])), 0.0)
        df_ref[...] = df.astype(BF16)
        dbf_ref[...] += jnp.sum(df, axis=0, keepdims=True)

    rev = lambda b, s: (b * ns + (ns - 1 - s), 0)
    blk = pl.BlockSpec((ts, aw), rev)
    fblk = pl.BlockSpec((ts, LANES), rev)
    pblk = pl.BlockSpec((ts, N_HEADS * LANES), rev)
    const = lambda shape: pl.BlockSpec(shape, lambda b, s: (0, 0))
    return pl.pallas_call(
        body, name="attn_prep_bwd", grid=(bsz, ns),
        in_specs=[pblk, pblk, blk, blk, fblk, const((1, LANES)), const((1, aw)), const((1, aw)), const((aw, aw)),
                  const((ts, ts))],
        out_specs=[blk, blk, fblk, const((1, aw)), const((1, aw)), const((1, LANES))],
        out_shape=[jax.ShapeDtypeStruct((t, aw), BF16), jax.ShapeDtypeStruct((t, aw), BF16),
                   jax.ShapeDtypeStruct((t, LANES), BF16), jax.ShapeDtypeStruct((1, aw), F32),
                   jax.ShapeDtypeStruct((1, aw), F32), jax.ShapeDtypeStruct((1, LANES), F32)],
        scratch_shapes=[pltpu.VMEM((8, LANES), F32)],
        compiler_params=_params("arbitrary", "arbitrary"),
    )(dqp, dkp, q, k, f, b_forget, q_gain, k_gain, seg, tri)


ATTN_BLOCK = 1024
HEAD_PAIRS = N_HEADS // 2


def _flash_fwd(qp, kp, v, bsz, seq):
    tq = ATTN_BLOCK
    half = tq // 2
    nq = seq // tq
    t = bsz * seq

    def body(q_ref, k_ref, v_ref, o_ref, lse_ref, m_sc, l_sc, acc_sc):
        i = pl.program_id(2)
        m_sc[...] = jnp.full(m_sc.shape, -jnp.inf, F32)
        l_sc[...] = jnp.zeros_like(l_sc)
        acc_sc[...] = jnp.zeros_like(acc_sc)
        lane = lax.broadcasted_iota(jnp.int32, (1, LANES), 1)
        low = lane < HEAD_DIM

        def tile(q0, qn, k_start, kn, k0=None):
            qs = pl.ds(q0, qn)
            ks = pl.ds(k_start, kn)
            vv = v_ref[ks, :]
            for h in range(2):
                mine = low if h == 0 else jnp.logical_not(low)
                cols = pl.ds(h * LANES, LANES)
                s = _dot_nt(q_ref[qs, cols], k_ref[ks, cols])
                if k0 is not None:
                    row = lax.broadcasted_iota(jnp.int32, (qn, kn), 0) + q0
                    col = lax.broadcasted_iota(jnp.int32, (qn, kn), 1) + k0
                    s = jnp.where(row >= col, s, -jnp.inf)
                m_prev = m_sc[h, qs, :]
                m_new = jnp.maximum(m_prev, jnp.max(s, axis=1, keepdims=True))
                p = jnp.exp(s - jnp.tile(m_new, (1, kn // LANES)))
                alpha = jnp.exp(m_prev - m_new)
                l_sc[h, qs, :] = alpha * l_sc[h, qs, :] + jnp.sum(p, axis=1, keepdims=True)
                m_sc[h, qs, :] = m_new
                pv = _dot(p.astype(BF16), jnp.where(mine, vv, jnp.zeros_like(vv)))
                acc_sc[qs, :] = acc_sc[qs, :] * jnp.where(mine, alpha, 1.0) + pv

        def below_diagonal(j, carry):
            tile(0, tq, pl.multiple_of(j * tq, tq), tq)
            return carry

        lax.fori_loop(0, i, below_diagonal, 0)
        diagonal = pl.multiple_of(i * tq, tq)
        tile(0, tq, diagonal, half, k0=0)
        tile(half, half, diagonal + half, half, k0=half)
        l = jnp.where(low, l_sc[0], l_sc[1])
        m = jnp.where(low, m_sc[0], m_sc[1])
        o_ref[...] = acc_sc[...] / l
        lse_ref[...] = m + jnp.log(l)

    qspec = pl.BlockSpec((tq, 2 * LANES), lambda b, hp, i: (b * nq + i, hp))
    kspec = pl.BlockSpec((seq, 2 * LANES), lambda b, hp, i: (b, hp))
    vspec = pl.BlockSpec((seq, LANES), lambda b, hp, i: (b, hp))
    ospec = pl.BlockSpec((tq, LANES), lambda b, hp, i: (b * nq + i, hp))
    return pl.pallas_call(
        body, name="flash_fwd", grid=(bsz, HEAD_PAIRS, nq),
        in_specs=[qspec, kspec, vspec], out_specs=[ospec, ospec],
        out_shape=[jax.ShapeDtypeStruct((t, ATTN_WIDTH), F32), jax.ShapeDtypeStruct((t, ATTN_WIDTH), F32)],
        scratch_shapes=[pltpu.VMEM((2, tq, LANES), F32), pltpu.VMEM((2, tq, LANES), F32), pltpu.VMEM((tq, LANES), F32)],
        compiler_params=_params("arbitrary", "arbitrary", "arbitrary"),
    )(qp, kp, v)


def _flash_bwd(qp, kp, v, o, do, lse, after, bsz, seq):
    tq = ATTN_BLOCK
    half = tq // 2
    nq = seq // tq
    t = bsz * seq

    def body(q_ref, k_ref, v_ref, o_ref, do_ref, lse_ref, after_ref, dq_ref, dk_ref, dv_ref, dk_acc, dv_acc):
        j = pl.program_id(2)

        @pl.when(j == 0)
        def _():
            dq_ref[...] = jnp.zeros_like(dq_ref)

        dk_acc[...] = jnp.zeros_like(dk_acc)
        dv_acc[...] = jnp.zeros_like(dv_acc)
        lane = lax.broadcasted_iota(jnp.int32, (1, LANES), 1)
        low = lane < HEAD_DIM

        def tile(q_start, qn, k0, kn, q0=None):
            rows = pl.ds(q_start, qn)
            ks = pl.ds(k0, kn)
            dov = do_ref[rows, :]
            dd = dov * o_ref[rows, :]
            dob = dov.astype(BF16)
            vv = v_ref[ks, :]
            lse_v = lse_ref[rows, :]
            for h in range(2):
                mine = low if h == 0 else jnp.logical_not(low)
                cols = pl.ds(h * LANES, LANES)
                qh = q_ref[rows, cols]
                kh = k_ref[ks, cols]
                s = _dot_nt(qh, kh)
                lse_h = jnp.where(mine, lse_v, pltpu.roll(lse_v, HEAD_DIM, 1))
                p = jnp.exp(s - jnp.tile(lse_h, (1, kn // LANES)))
                if q0 is not None:
                    row = lax.broadcasted_iota(jnp.int32, (qn, kn), 0) + q0
                    col = lax.broadcasted_iota(jnp.int32, (qn, kn), 1) + k0
                    p = jnp.where(row >= col, p, 0.0)
                delta = jnp.sum(jnp.where(mine, dd, 0.0), axis=1, keepdims=True)
                dp = _dot_nt(dob, jnp.where(mine, vv, jnp.zeros_like(vv)))
                ds = (p * (dp - delta)).astype(BF16)
                dv_acc[ks, :] += jnp.where(mine, _dot_tn(p.astype(BF16), dob), 0.0)
                dk_acc[ks, cols] += _dot_tn(ds, qh)
                dq_ref[rows, cols] += _dot(ds, kh)

        def above_diagonal(i, carry):
            tile(pl.multiple_of(i * tq, tq), tq, 0, tq)
            return carry

        diagonal = pl.multiple_of(j * tq, tq)
        tile(diagonal, tq, 0, half, q0=0)
        tile(diagonal + half, half, half, half, q0=half)
        lax.fori_loop(j + 1, nq, above_diagonal, 0)
        dk_ref[...] = dk_acc[...]
        dv_ref[...] = dv_acc[...].astype(BF16)

    qspec = pl.BlockSpec((seq, 2 * LANES), lambda b, hp, j: (b, hp))
    kspec = pl.BlockSpec((tq, 2 * LANES), lambda b, hp, j: (b * nq + j, hp))
    vspec = pl.BlockSpec((tq, LANES), lambda b, hp, j: (b * nq + j, hp))
    ospec = pl.BlockSpec((seq, LANES), lambda b, hp, j: (b, hp))
    return pl.pallas_call(
        body, name="flash_bwd", grid=(bsz, HEAD_PAIRS, nq),
        in_specs=[qspec, kspec, vspec, ospec, ospec, ospec, ORDER_ONLY], out_specs=[qspec, kspec, vspec],
        out_shape=[jax.ShapeDtypeStruct((t, N_HEADS * LANES), F32), jax.ShapeDtypeStruct((t, N_HEADS * LANES), F32),
                   jax.ShapeDtypeStruct((t, ATTN_WIDTH), BF16)],
        scratch_shapes=[pltpu.VMEM((tq, 2 * LANES), F32), pltpu.VMEM((tq, LANES), F32)],
        compiler_params=_params("arbitrary", "arbitrary", "arbitrary"),
    )(qp, kp, v, o, do, lse, after)


def _mix_out_fwd(o, y_pool, x, gain, w_out):
    t, d = x.shape
    tm = 1024
    pw, aw = POOL_WIDTH, ATTN_WIDTH

    def body(o_ref, yp_ref, x_ref, g_ref, w_ref, ycat_ref, y_ref):
        ov = o_ref[...]
        ya = (ov * _rms_scale(ov) * g_ref[...]).astype(BF16)
        ycat = jnp.concatenate([yp_ref[...], ya], axis=1)
        ycat_ref[...] = ycat
        y_ref[...] = x_ref[...] + _dot(ycat, w_ref[...])

    return pl.pallas_call(
        body, name="mix_out_fwd", grid=(t // tm,),
        in_specs=[_rows(tm, aw), _rows(tm, pw), _rows(tm, d), _resident((1, aw)), _resident((pw + aw, d))],
        out_specs=[_rows(tm, pw + aw), _rows(tm, d)],
        out_shape=[jax.ShapeDtypeStruct((t, pw + aw), BF16), jax.ShapeDtypeStruct((t, d), F32)],
        compiler_params=_params("arbitrary"),
    )(o, y_pool, x, gain, w_out)


def _mix_out_bwd(dx, o, ycat, gain, w_out):
    t, d = dx.shape
    tm = 1024
    nm = t // tm
    pw, aw = POOL_WIDTH, ATTN_WIDTH

    def body(dx_ref, o_ref, ycat_ref, g_ref, w_ref, dw_ref, dyp_ref, do_ref, dg_ref, acc):
        i = pl.program_id(0)

        @pl.when(i == 0)
        def _():
            dg_ref[...] = jnp.zeros_like(dg_ref)
            acc[...] = jnp.zeros_like(acc)

        dxb = dx_ref[...].astype(BF16)
        acc[...] += _dot_tn(ycat_ref[...], dxb)
        dyp_ref[...] = _dot_nt(dxb, w_ref[pl.ds(0, pw), :])
        dya = _dot_nt(dxb, w_ref[pl.ds(pw, aw), :])
        do, dgain = _rms_bwd(dya, o_ref[...], g_ref[...])
        do_ref[...] = do
        dg_ref[...] += dgain

        @pl.when(i == nm - 1)
        def _():
            dw_ref[...] = acc[...].astype(BF16)

    return pl.pallas_call(
        body, name="mix_out_bwd", grid=(nm,),
        in_specs=[_rows(tm, d), _rows(tm, aw), _rows(tm, pw + aw), _resident((1, aw)), _resident((pw + aw, d))],
        out_specs=[pl.BlockSpec((pw + aw, d), lambda i: (0, 0)), _rows(tm, pw), _rows(tm, aw),
                   pl.BlockSpec((1, aw), lambda i: (0, 0))],
        out_shape=[jax.ShapeDtypeStruct((pw + aw, d), BF16), jax.ShapeDtypeStruct((t, pw), F32),
                   jax.ShapeDtypeStruct((t, aw), F32), jax.ShapeDtypeStruct((1, aw), F32)],
        scratch_shapes=[pltpu.VMEM((pw + aw, d), F32)],
        compiler_params=_params("arbitrary"),
    )(dx, o, ycat, gain, w_out)


def _mix_in_bwd(dpv, dq, dk, dv, df, hm, x, dx_res, gain, w_in_t):
    t, d = x.shape
    tm = 512
    nm = t // tm
    pw, aw = POOL_WIDTH, ATTN_WIDTH

    def body(dpv_ref, dq_ref, dk_ref, dv_ref, df_ref, hm_ref, x_ref, dxr_ref, g_ref, w_ref, dw_ref, dx_ref, dxh_ref,
             dg_ref, acc):
        i = pl.program_id(0)

        @pl.when(i == 0)
        def _():
            dg_ref[...] = jnp.zeros_like(dg_ref)
            acc[...] = jnp.zeros_like(acc)

        dh = jnp.concatenate([dpv_ref[...], dq_ref[...], dk_ref[...], dv_ref[...], df_ref[...]], axis=1)
        acc[...] += _dot_tn(dh, hm_ref[...])
        dx, dgain = _rms_bwd(_dot(dh, w_ref[...]), x_ref[...], g_ref[...])
        dx = dxr_ref[...] + dx
        dx_ref[...] = dx
        dxh_ref[...] = (0.5 * dx).astype(BF16)
        dg_ref[...] += dgain

        @pl.when(i == nm - 1)
        def _():
            dw_ref[...] = acc[...].astype(BF16)

    return pl.pallas_call(
        body, name="mix_in_bwd", grid=(nm,),
        in_specs=[_rows(tm, pw), _rows(tm, aw), _rows(tm, aw), _rows(tm, aw), _rows(tm, LANES), _rows(tm, d),
                  _rows(tm, d), _rows(tm, d), _resident((1, d)), _resident((MIX_PAD, d))],
        out_specs=[pl.BlockSpec((MIX_PAD, d), lambda i: (0, 0)), _rows(tm, d), _rows(tm, d),
                   pl.BlockSpec((1, d), lambda i: (0, 0))],
        out_shape=[jax.ShapeDtypeStruct((MIX_PAD, d), BF16), jax.ShapeDtypeStruct((t, d), F32),
                   jax.ShapeDtypeStruct((t, d), BF16), jax.ShapeDtypeStruct((1, d), F32)],
        scratch_shapes=[pltpu.VMEM((MIX_PAD, d), F32)],
        compiler_params=_params("arbitrary"),
    )(dpv, dq, dk, dv, df, hm, x, dx_res, gain, w_in_t)


MESH_IDS = pl.DeviceIdType.MESH


def _me():
    return lax.axis_index("x"), lax.axis_index("y"), lax.axis_index("c")


def _peer(x, y, c, p):
    px = 1 - x if p & 4 else x
    py = 1 - y if p & 2 else y
    pc = 1 - c if p & 1 else c
    return (px, py, pc), 4 * px + 2 * py + pc


HBM_SPEC = pl.BlockSpec(memory_space=pltpu.HBM)
SEM_SPEC = pl.BlockSpec(memory_space=pltpu.SEMAPHORE)
SPLIT_COPY = pltpu.CompilerParams(has_side_effects=pltpu.SideEffectType.DATAFLOW_SIDE_EFFECTING)
PEERS = N_DEV - 1


def _hbm(a):
    return pltpu.with_memory_space_constraint(a, pltpu.HBM)


def _row_block(ref, dev, rows):
    return ref.at[pl.ds(pl.multiple_of(dev * rows, BF16_ROWS), rows)]


def _copy_ends(gather, src, land, me, peer_id):
    if gather:
        rows = src.shape[0]
        return src, _row_block(land, me, rows), _row_block(land, peer_id, rows), src, _row_block(land, me, rows)
    rows = src.shape[0] // N_DEV
    return (_row_block(src, peer_id, rows), land.at[me], land.at[peer_id], _row_block(src, me, rows), land.at[me])


def _land_shape(gather, s):
    return (N_DEV * s.shape[0], s.shape[1]) if gather else (N_DEV, s.shape[0] // N_DEV, s.shape[1])


SIBLING = 1
SAME_CORE_PEERS = (2, 4, 6)
RELAYS = len(SAME_CORE_PEERS)


def _copies_start(groups, gather, name, after=None, relayed=()):
    flat = [s for g in groups for s in g]
    n, ng = len(flat), len(groups)
    lands = [lax.empty(_land_shape(gather, s), s.dtype) for s in flat]
    n_in = 2 * n + (after is not None)

    def body(*refs):
        ins, lnd = refs[:n], refs[n:2 * n]
        sems = refs[n_in:n_in + 2 * ng]
        token = refs[-1]
        x, y, c = _me()
        me = 4 * x + 2 * y + c
        w = 0
        for gi, g in enumerate(groups):
            for k in range(len(g)):
                for p in ((SIBLING,) + SAME_CORE_PEERS if gi in relayed else range(1, N_DEV)):
                    peer, peer_id = _peer(x, y, c, p)
                    src, dst, _, _, _ = _copy_ends(gather, ins[w], lnd[w], me, peer_id)
                    pltpu.make_async_remote_copy(src, dst, sems[2 * gi].at[k * PEERS + p - 1],
                                                 sems[2 * gi + 1].at[k * PEERS + p - 1], device_id=peer,
                                                 device_id_type=MESH_IDS).start()
                w += 1
        token[...] = jnp.zeros_like(token)

    sem_shapes = []
    for g in groups:
        sem_shapes += [pltpu.SemaphoreType.DMA((len(g) * PEERS,))] * 2
    out = pl.pallas_call(
        body, name=name,
        out_shape=(*sem_shapes, *[pltpu.HBM(s.shape, s.dtype) for s in flat],
                   *[pltpu.HBM(l.shape, l.dtype) for l in lands], jax.ShapeDtypeStruct((8, LANES), F32)),
        in_specs=[HBM_SPEC] * (2 * n) + [pl.BlockSpec(memory_space=pl.ANY)] * (after is not None),
        out_specs=(*[SEM_SPEC] * (2 * ng), *[HBM_SPEC] * (2 * n), pl.BlockSpec(memory_space=pltpu.VMEM)),
        input_output_aliases={i: 2 * ng + i for i in range(2 * n)},
        compiler_params=SPLIT_COPY,
    )(*[_hbm(s) for s in flat], *[_hbm(l) for l in lands], *([after] if after is not None else []))
    sems, thru, token = out[:2 * ng], out[2 * ng:2 * ng + 2 * n], out[-1]
    res, w = [], 0
    for gi, g in enumerate(groups):
        res.append((sems[2 * gi], sems[2 * gi + 1], list(thru[w:w + len(g)]), list(thru[n + w:n + w + len(g)])))
        w += len(g)
    return res, token


def _copies_wait(started, gather, after, name):
    send, recv, srcs, lands = started
    n = len(srcs)
    after = list(after) if isinstance(after, (list, tuple)) else [after]

    own_shapes = [s.shape if gather else (s.shape[0] // N_DEV, s.shape[1]) for s in srcs]

    def body(*refs):
        ins, lnd = refs[:n], refs[n:2 * n]
        send_sems, recv_sems = refs[2 * n], refs[2 * n + 1]
        bounce, in_sems, out_sems = refs[-n - 2:-2], refs[-2], refs[-1]
        x, y, c = _me()
        me = 4 * x + 2 * y + c
        ends = [_copy_ends(gather, ins[w], lnd[w], me, me)[3:] for w in range(n)]
        loads = [pltpu.make_async_copy(ends[w][0], bounce[w], in_sems.at[w]) for w in range(n)]
        stores = [pltpu.make_async_copy(bounce[w], ends[w][1], out_sems.at[w]) for w in range(n)]
        for cp in loads:
            cp.start()
        for w in range(n):
            loads[w].wait()
            stores[w].start()
        for w in range(n):
            for p in range(1, N_DEV):
                peer, peer_id = _peer(x, y, c, p)
                src, _, arrival, _, _ = _copy_ends(gather, ins[w], lnd[w], me, peer_id)
                cp = pltpu.make_async_remote_copy(src, arrival, send_sems.at[w * PEERS + p - 1],
                                                  recv_sems.at[w * PEERS + p - 1], device_id=peer,
                                                  device_id_type=MESH_IDS)
                cp.wait_send()
                cp.wait_recv()
        for cp in stores:
            cp.wait()

    out = pl.pallas_call(
        body, name=name,
        out_shape=(*[pltpu.HBM(s.shape, s.dtype) for s in srcs], *[pltpu.HBM(l.shape, l.dtype) for l in lands]),
        in_specs=[HBM_SPEC] * (2 * n) + [SEM_SPEC, SEM_SPEC] + [pl.BlockSpec(memory_space=pl.ANY)] * len(after),
        out_specs=[HBM_SPEC] * (2 * n),
        input_output_aliases={i: i for i in range(2 * n)},
        scratch_shapes=[*[pltpu.VMEM(shape, s.dtype) for shape, s in zip(own_shapes, srcs)],
                        pltpu.SemaphoreType.DMA((n,)), pltpu.SemaphoreType.DMA((n,))],
        compiler_params=SPLIT_COPY,
    )(*srcs, *lands, send, recv, *after)
    return list(out[n:])


def _relay_to_sibling(started, name, after=None):
    send, recv, srcs, lands = started
    n = len(srcs)
    after = [] if after is None else [after]

    def body(*refs):
        ins, lnd = refs[:n], refs[n:2 * n]
        send_sems, recv_sems = refs[2 * n], refs[2 * n + 1]
        relay_send, relay_recv = refs[2 * n + 2 + len(after)], refs[2 * n + 3 + len(after)]
        x, y, c = _me()
        sibling, _ = _peer(x, y, c, SIBLING)
        for w in range(n):
            rows = ins[w].shape[0]
            for k, p in enumerate(SAME_CORE_PEERS):
                peer, peer_id = _peer(x, y, c, p)
                arrived = _row_block(lnd[w], peer_id, rows)
                first = pltpu.make_async_remote_copy(ins[w], arrived, send_sems.at[w * PEERS + p - 1],
                                                     recv_sems.at[w * PEERS + p - 1], device_id=peer,
                                                     device_id_type=MESH_IDS)
                first.wait_recv()
                pltpu.make_async_remote_copy(arrived, arrived, relay_send.at[w * RELAYS + k],
                                             relay_recv.at[w * RELAYS + k], device_id=sibling,
                                             device_id_type=MESH_IDS).start()
                first.wait_send()

    sems = pltpu.SemaphoreType.DMA((n * RELAYS,))
    out = pl.pallas_call(
        body, name=name,
        out_shape=(sems, sems, *[pltpu.HBM(s.shape, s.dtype) for s in srcs], *[pltpu.HBM(l.shape, l.dtype) for l in lands]),
        in_specs=[HBM_SPEC] * (2 * n) + [SEM_SPEC, SEM_SPEC] + [pl.BlockSpec(memory_space=pl.ANY)] * len(after),
        out_specs=(SEM_SPEC, SEM_SPEC, *[HBM_SPEC] * (2 * n)),
        input_output_aliases={i: 2 + i for i in range(2 * n)},
        compiler_params=SPLIT_COPY,
    )(*srcs, *lands, send, recv, *after)
    return send, recv, out[0], out[1], list(out[2:2 + n]), list(out[2 + n:])


def _relayed_wait(relayed, after, name):
    send, recv, relay_send, relay_recv, srcs, lands = relayed
    n = len(srcs)
    after = list(after) if isinstance(after, (list, tuple)) else [after]

    def body(*refs):
        ins, lnd = refs[:n], refs[n:2 * n]
        send_sems, recv_sems, relay_send_sems, relay_recv_sems = refs[2 * n:2 * n + 4]
        bounce, in_sems, out_sems = refs[-n - 2:-2], refs[-2], refs[-1]
        x, y, c = _me()
        me = 4 * x + 2 * y + c
        sibling, sibling_id = _peer(x, y, c, SIBLING)
        loads = [pltpu.make_async_copy(ins[w], bounce[w], in_sems.at[w]) for w in range(n)]
        stores = [pltpu.make_async_copy(bounce[w], _row_block(lnd[w], me, ins[w].shape[0]), out_sems.at[w])
                  for w in range(n)]
        for cp in loads:
            cp.start()
        for w in range(n):
            loads[w].wait()
            stores[w].start()
        for w in range(n):
            rows = ins[w].shape[0]
            direct = pltpu.make_async_remote_copy(ins[w], _row_block(lnd[w], sibling_id, rows),
                                                  send_sems.at[w * PEERS + SIBLING - 1],
                                                  recv_sems.at[w * PEERS + SIBLING - 1], device_id=sibling,
                                                  device_id_type=MESH_IDS)
            direct.wait_send()
            direct.wait_recv()
            for k, p in enumerate(SAME_CORE_PEERS):
                _, sent_id = _peer(x, y, c, p)
                _, got_id = _peer(x, y, c, p + SIBLING)
                relay = pltpu.make_async_remote_copy(_row_block(lnd[w], sent_id, rows), _row_block(lnd[w], got_id, rows),
                                                     relay_send_sems.at[w * RELAYS + k],
                                                     relay_recv_sems.at[w * RELAYS + k], device_id=sibling,
                                                     device_id_type=MESH_IDS)
                relay.wait_send()
                relay.wait_recv()
        for cp in stores:
            cp.wait()

    out = pl.pallas_call(
        body, name=name,
        out_shape=(*[pltpu.HBM(s.shape, s.dtype) for s in srcs], *[pltpu.HBM(l.shape, l.dtype) for l in lands]),
        in_specs=[HBM_SPEC] * (2 * n) + [SEM_SPEC] * 4 + [pl.BlockSpec(memory_space=pl.ANY)] * len(after),
        out_specs=[HBM_SPEC] * (2 * n),
        input_output_aliases={i: i for i in range(2 * n)},
        scratch_shapes=[*[pltpu.VMEM(s.shape, s.dtype) for s in srcs],
                        pltpu.SemaphoreType.DMA((n,)), pltpu.SemaphoreType.DMA((n,))],
        compiler_params=SPLIT_COPY,
    )(*srcs, *lands, send, recv, relay_send, relay_recv, *after)
    return list(out[n:])


def _adamw_update(w, g, m, v):
    nm = ADAM_B1 * m + (1.0 - ADAM_B1) * g
    nv = ADAM_B2 * v + (1.0 - ADAM_B2) * (g * g)
    m_hat = nm / (1.0 - ADAM_B1 ** ADAM_STEP)
    v_hat = nv / (1.0 - ADAM_B2 ** ADAM_STEP)
    return -ADAM_LR * (m_hat / (jnp.sqrt(v_hat) + ADAM_EPS) + ADAM_WD * w), nm, nv


SUM_ADAMW_COLS = 512


def _sum_adamw(parts, w, m, v, name):
    _, rows, d = parts.shape
    n = w.shape[0]
    tc = SUM_ADAMW_COLS

    def body(p_ref, w_ref, m_ref, v_ref, g_ref, d_ref, nm_ref, nv_ref):
        g = p_ref[0].astype(F32)
        for dev in range(1, N_DEV):
            g = g + p_ref[dev].astype(F32)
        g = g[:n]
        g_ref[...] = g
        d_ref[...], nm_ref[...], nv_ref[...] = _adamw_update(w_ref[...], g, m_ref[...], v_ref[...])

    spec = pl.BlockSpec((n, tc), lambda j: (0, j))
    shape = jax.ShapeDtypeStruct((n, d), F32)
    return pl.pallas_call(
        body, name=name, grid=(d // tc,),
        in_specs=[pl.BlockSpec((N_DEV, rows, tc), lambda j: (0, 0, j)), spec, spec, spec],
        out_specs=[spec] * 4, out_shape=[shape] * 4,
        compiler_params=_params("arbitrary"),
    )(parts, w, m, v)


def _pad_rows(a, rows):
    return jnp.pad(a, ((0, rows - a.shape[0]), (0, 0)))


def _row1(vec, width=D_MODEL):
    return jnp.pad(vec.reshape(1, -1), ((0, 0), (0, width - vec.shape[-1])))


COLUMN_SHARDED = ("ffn1_w_gate", "ffn1_w_up", "w_in", "ffn2_w_gate", "ffn2_w_up")
VEC_NAMES = ("ffn1_norm", "mix_norm", "ffn2_norm", "b_forget", "pool_scale", "q_norm", "k_norm", "out_norm_pool",
             "out_norm_attn")
VEC_ROWS = 16
LOSS_ROW = len(VEC_NAMES)


def _pack_vector_grads(parts, loss_part, name):
    names = [n for n in VEC_NAMES if n in parts]
    extra = [] if loss_part is None else [loss_part]

    def body(*refs):
        out_ref = refs[-1]
        out_ref[...] = jnp.zeros_like(out_ref)
        lane = lax.broadcasted_iota(jnp.int32, (1, LANES), 1)
        for n, ref in zip(names, refs):
            val = ref[...]
            if n in ("q_norm", "k_norm"):
                val = val[:, 0:LANES] + val[:, LANES:2 * LANES] + val[:, 2 * LANES:3 * LANES] + val[:, 3 * LANES:]
                val = jnp.where(lane < HEAD_DIM, val + pltpu.roll(val, HEAD_DIM, 1), 0.0)
            out_ref[pl.ds(VEC_NAMES.index(n), 1), pl.ds(0, val.shape[1])] = val
        if extra:
            out_ref[pl.ds(LOSS_ROW, 1), pl.ds(0, 1)] = refs[len(names)][...]

    vmem = pl.BlockSpec(memory_space=pltpu.VMEM)
    return pl.pallas_call(
        body, name=name, in_specs=[vmem] * (len(names) + len(extra)), out_specs=vmem,
        out_shape=jax.ShapeDtypeStruct((VEC_ROWS, D_MODEL), F32),
    )(*[parts[n] for n in names], *extra)


def _small_adamw(vec_all, vec_late, pool_all, vec_params, pool_params):
    nv = len(vec_params)
    pool_rows = pool_params[0].shape[0]

    def body(*refs):
        vec_ref, late_ref, pool_ref = refs[0], refs[1], refs[2]
        ins = refs[3:3 + 3 * nv + 3]
        outs = refs[3 + 3 * nv + 3:-1]
        rows = refs[-1]
        total = vec_ref[pl.ds(0, VEC_ROWS), :] + late_ref[pl.ds(0, VEC_ROWS), :]
        for dev in range(1, N_DEV):
            total = total + (vec_ref[pl.ds(dev * VEC_ROWS, VEC_ROWS), :] + late_ref[pl.ds(dev * VEC_ROWS, VEC_ROWS), :])
        rows[...] = total
        outs[4 * nv + 4][...] = rows[pl.ds(LOSS_ROW, 1), pl.ds(0, 1)]
        for i in range(nv):
            w_ref, m_ref, v_ref = ins[3 * i:3 * i + 3]
            g = rows[pl.ds(i, 1), pl.ds(0, w_ref.shape[1])]
            outs[4 * i][...] = g
            outs[4 * i + 1][...], outs[4 * i + 2][...], outs[4 * i + 3][...] = _adamw_update(
                w_ref[...], g, m_ref[...], v_ref[...])
        g = pool_ref[pl.ds(0, pool_rows), :].astype(F32)
        for dev in range(1, N_DEV):
            g = g + pool_ref[pl.ds(dev * pool_rows, pool_rows), :].astype(F32)
        w_ref, m_ref, v_ref = ins[3 * nv:]
        outs[4 * nv][...] = g
        outs[4 * nv + 1][...], outs[4 * nv + 2][...], outs[4 * nv + 3][...] = _adamw_update(
            w_ref[...], g, m_ref[...], v_ref[...])

    vmem = pl.BlockSpec(memory_space=pltpu.VMEM)
    flat = [a for trio in vec_params for a in trio] + list(pool_params)
    out_shape = []
    for trio in list(vec_params) + [pool_params]:
        out_shape += [jax.ShapeDtypeStruct(trio[0].shape, F32)] * 4
    out_shape.append(jax.ShapeDtypeStruct((1, 1), F32))
    return pl.pallas_call(
        body, name="adamw_small", in_specs=[vmem] * (3 + len(flat)), out_specs=[vmem] * len(out_shape),
        out_shape=out_shape, scratch_shapes=[pltpu.VMEM((VEC_ROWS, D_MODEL), F32)],
    )(vec_all, vec_late, pool_all, *flat)


def kernel(x, ffn1_norm, ffn1_w_gate, ffn1_w_up, ffn1_w_down, mix_norm, w_in, b_forget, pool_w, pool_scale, q_norm, k_norm, out_norm_pool, out_norm_attn, w_out, ffn2_norm, ffn2_w_gate, ffn2_w_up, ffn2_w_down, loss_target, m_ffn1_norm, m_ffn1_w_gate, m_ffn1_w_up, m_ffn1_w_down, m_mix_norm, m_w_in, m_b_forget, m_pool_w, m_pool_scale, m_q_norm, m_k_norm, m_out_norm_pool, m_out_norm_attn, m_w_out, m_ffn2_norm, m_ffn2_w_gate, m_ffn2_w_up, m_ffn2_w_down, v_ffn1_norm, v_ffn1_w_gate, v_ffn1_w_up, v_ffn1_w_down, v_mix_norm, v_w_in, v_b_forget, v_pool_w, v_pool_scale, v_q_norm, v_k_norm, v_out_norm_pool, v_out_norm_attn, v_w_out, v_ffn2_norm, v_ffn2_w_gate, v_ffn2_w_up, v_ffn2_w_down):
    bsz, seq, d = x.shape
    t = bsz * seq
    x0 = x.reshape(t, d)
    target = loss_target.reshape(t, d)
    in_rows = -(-w_in.shape[1] // BF16_ROWS) * BF16_ROWS

    slabs = [s.astype(BF16) for s in (ffn1_w_gate.T, ffn1_w_up.T, ffn1_w_down, _pad_rows(w_in.T, in_rows), w_out,
                                       ffn2_w_gate.T, ffn2_w_up.T, ffn2_w_down)]
    first, started = _copies_start([slabs[0:2]], True, "gather_start_first", relayed=(0,))
    rest, started = _copies_start([slabs[2:3], slabs[3:4], slabs[4:5], slabs[5:8]], True, "gather_start", after=started,
                                  relayed=(3,))
    gathers = first + rest

    g1, gm, g2 = ffn1_norm.reshape(1, d), mix_norm.reshape(1, d), ffn2_norm.reshape(1, d)
    bf_row = _row1(b_forget, LANES)
    gq = jnp.tile(q_norm, N_HEADS).reshape(1, ATTN_WIDTH)
    gk = jnp.tile(k_norm, N_HEADS).reshape(1, ATTN_WIDTH)
    scale_row = pool_scale.reshape(1, POOL_WIDTH)
    gp, ga = out_norm_pool.reshape(1, POOL_WIDTH), out_norm_attn.reshape(1, ATTN_WIDTH)

    wg1, wu1 = _relayed_wait(_relay_to_sibling(gathers[0], "gather_relay_ffn1_up", started), started,
                             "gather_wait_ffn1_up")
    h1, sa1, sb1, s1 = _ffn_up(x0, g1, wg1, wu1, "ffn1_up")
    (wd1,) = _copies_wait(gathers[1], True, s1, "gather_wait_ffn1_down")
    (x1,) = _ffn_down(s1, wd1, x0, None, "ffn1_down")
    (win_g,) = _copies_wait(gathers[2], True, x1, "gather_wait_w_in")
    win_t = _repack_rows(win_g, in_rows, w_in.shape[1], N_DEV, "w_in_rows")
    hm, pv, q, k, v, f = _mix_in_fwd(x1, gm, win_t)
    pooled, mixed, y_pool = _pool_fwd(pv, pool_w, scale_row, gp, bsz, seq)
    qp, kp = _attn_prep_fwd(q, k, f, bf_row, gq, gk, bsz, seq)
    o, lse = _flash_fwd(qp, kp, v, bsz, seq)
    relayed_ffn2 = _relay_to_sibling(gathers[4], "gather_relay_ffn2", o)
    (wout,) = _copies_wait(gathers[3], True, [o, relayed_ffn2[4][0]], "gather_wait_w_out")
    ycat, x2 = _mix_out_fwd(o, y_pool, x1, ga, wout)
    wg2, wu2, wd2 = _relayed_wait(relayed_ffn2, x2, "gather_wait_ffn2")
    h2, sa2, sb2, s2 = _ffn_up(x2, g2, wg2, wu2, "ffn2_up")
    dx3, dyh2, loss_part = _ffn_down(s2, wd2, x2, target, "ffn2_down")

    da2, db2, dwg2, dwu2 = _ffn_bwd_act(dyh2, sa2, sb2, h2, wd2, dx3, "ffn2_bwd_act")
    (dwd2,) = _wgrad([s2], dyh2, da2, "ffn2_down_wgrad")
    (sent_ffn2,), tok = _copies_start([[dwg2, dwu2, dwd2]], False, "exchange_start_ffn2")
    dx2, dg2 = _ffn_bwd_dx(da2, db2, dx3, x2, g2, wg2, wu2, tok, "ffn2_bwd_dx")
    dwout, dy_pool, do, dga = _mix_out_bwd(dx2, o, ycat, ga, wout)
    (sent_out,), tok = _copies_start([[dwout]], False, "exchange_start_w_out")
    dqp, dkp, dv = _flash_bwd(qp, kp, v, o, do, lse, tok, bsz, seq)
    dq, dk, df, dgq, dgk, dbf = _attn_prep_bwd(dqp, dkp, q, k, f, bf_row, gq, gk, bsz, seq)
    dpv, dpool_w, dscale, dgp = _pool_bwd(dy_pool, mixed, pooled, pool_w, scale_row, gp, bsz, seq)
    dwin, dx1, dyh1, dgm = _mix_in_bwd(dpv, dq, dk, dv, df, hm, x1, dx2, gm, win_t)
    dwin_blocks = _repack_rows(dwin, w_in.shape[1], in_rows, N_DEV, "w_in_grad_blocks")
    (sent_in,), tok = _copies_start([[dwin_blocks]], False, "exchange_start_w_in")
    (dwd1,) = _wgrad([s1], dyh1, tok, "ffn1_down_wgrad")
    (sent_down1,), tok = _copies_start([[dwd1]], False, "exchange_start_ffn1_down", after=tok)
    pool_rows = POOL_GROUPS * POOL_GROUP_DIM
    packed = _pack_vector_grads(dict(mix_norm=dgm, ffn2_norm=dg2, b_forget=dbf, pool_scale=dscale, q_norm=dgq,
                                     k_norm=dgk, out_norm_pool=dgp, out_norm_attn=dga), loss_part, "pack_vector_grads")
    pool_part = dpool_w.reshape(pool_rows, POOL_GROUP_DIM).astype(BF16)
    (sent_small,), tok = _copies_start([[packed, pool_part]], True, "small_grads_start", after=tok)
    da1, db1, dwg1, dwu1 = _ffn_bwd_act(dyh1, sa1, sb1, h1, wd1, tok, "ffn1_bwd_act")
    (sent_up1,), tok = _copies_start([[dwg1, dwu1]], False, "exchange_start_ffn1_up", after=tok)
    dx0, dg1 = _ffn_bwd_dx(da1, db1, dx1, x0, g1, wg1, wu1, tok, "ffn1_bwd_dx")
    (sent_late,), tok = _copies_start([[_pack_vector_grads(dict(ffn1_norm=dg1), None, "pack_ffn1_norm_grad")]], True,
                                      "ffn1_norm_grad_start")

    weights = dict(ffn1_norm=ffn1_norm, ffn1_w_gate=ffn1_w_gate, ffn1_w_up=ffn1_w_up, ffn1_w_down=ffn1_w_down,
                   mix_norm=mix_norm, w_in=w_in, b_forget=b_forget, pool_w=pool_w, pool_scale=pool_scale,
                   q_norm=q_norm, k_norm=k_norm, out_norm_pool=out_norm_pool, out_norm_attn=out_norm_attn,
                   w_out=w_out, ffn2_norm=ffn2_norm, ffn2_w_gate=ffn2_w_gate, ffn2_w_up=ffn2_w_up,
                   ffn2_w_down=ffn2_w_down)
    m_in = dict(ffn1_norm=m_ffn1_norm, ffn1_w_gate=m_ffn1_w_gate, ffn1_w_up=m_ffn1_w_up, ffn1_w_down=m_ffn1_w_down,
                mix_norm=m_mix_norm, w_in=m_w_in, b_forget=m_b_forget, pool_w=m_pool_w, pool_scale=m_pool_scale,
                q_norm=m_q_norm, k_norm=m_k_norm, out_norm_pool=m_out_norm_pool, out_norm_attn=m_out_norm_attn,
                w_out=m_w_out, ffn2_norm=m_ffn2_norm, ffn2_w_gate=m_ffn2_w_gate, ffn2_w_up=m_ffn2_w_up,
                ffn2_w_down=m_ffn2_w_down)
    v_in = dict(ffn1_norm=v_ffn1_norm, ffn1_w_gate=v_ffn1_w_gate, ffn1_w_up=v_ffn1_w_up, ffn1_w_down=v_ffn1_w_down,
                mix_norm=v_mix_norm, w_in=v_w_in, b_forget=v_b_forget, pool_w=v_pool_w, pool_scale=v_pool_scale,
                q_norm=v_q_norm, k_norm=v_k_norm, out_norm_pool=v_out_norm_pool, out_norm_attn=v_out_norm_attn,
                w_out=v_w_out, ffn2_norm=v_ffn2_norm, ffn2_w_gate=v_ffn2_w_gate, ffn2_w_up=v_ffn2_w_up,
                ffn2_w_down=v_ffn2_w_down)
    grads, delta, new_m, new_v = {}, {}, {}, {}
    after = [tok]
    plan = ((sent_ffn2, "ffn2", ("ffn2_w_gate", "ffn2_w_up", "ffn2_w_down")), (sent_out, "w_out", ("w_out",)),
            (sent_in, "w_in", ("w_in",)), (sent_down1, "ffn1_down", ("ffn1_w_down",)),
            (sent_up1, "ffn1_up", ("ffn1_w_gate", "ffn1_w_up")))
    for sent, tag, names in plan:
        parts = _copies_wait(sent, False, after, f"exchange_wait_{tag}")
        after = []
        for n, part in zip(names, parts):
            turn = (lambda a: a.T) if n in COLUMN_SHARDED else (lambda a: a)
            done = _sum_adamw(part, turn(weights[n]), turn(m_in[n]), turn(v_in[n]), f"adamw_{n}")
            grads[n], delta[n], new_m[n], new_v[n] = (turn(a) for a in done)
            after.append(done[3])
    vec_all, pool_all = _copies_wait(sent_small, True, after, "small_grads_wait")
    (vec_late,) = _copies_wait(sent_late, True, vec_all, "ffn1_norm_grad_wait")
    as_row = lambda a: a.reshape(1, -1)
    as_pool = lambda a: a.reshape(pool_rows, POOL_GROUP_DIM)
    small = _small_adamw(vec_all, vec_late, pool_all,
                         [tuple(as_row(z[n]) for z in (weights, m_in, v_in)) for n in VEC_NAMES],
                         tuple(as_pool(z["pool_w"]) for z in (weights, m_in, v_in)))
    for i, n in enumerate(VEC_NAMES + ("pool_w",)):
        grads[n], delta[n], new_m[n], new_v[n] = (a.reshape(weights[n].shape) for a in small[4 * i:4 * i + 4])
    loss = small[-1].reshape(())

    order = ("ffn1_norm", "ffn1_w_gate", "ffn1_w_up", "ffn1_w_down", "mix_norm", "w_in", "b_forget", "pool_w",
             "pool_scale", "q_norm", "k_norm", "out_norm_pool", "out_norm_attn", "w_out", "ffn2_norm", "ffn2_w_gate",
             "ffn2_w_up", "ffn2_w_down")
    return (loss, dx0.reshape(bsz, seq, d), *[grads[n] for n in order], *[delta[n] for n in order],
            *[new_m[n] for n in order], *[new_v[n] for n in order])
```

```python
import jax
import jax.numpy as jnp
from jax import lax
from jax.experimental import pallas as pl
from jax.experimental.pallas import tpu as pltpu

F32 = jnp.float32
BF16 = jnp.bfloat16

EPS = 1e-6
D_MODEL = 1024
N_HEADS = 8
HEAD_DIM = 64
POOL_WIDTH = 512
ATTN_WIDTH = 512
POOL_GROUPS = 4
POOL_GROUP_DIM = 128
POOL_WINDOWS = (2, 4, 8, 16)
POOL_HALO = 16
MIX_PAD = POOL_WIDTH + 3 * ATTN_WIDTH + 128
N_DEV = 8
BF16_ROWS = 16
LANES = 128
VMEM_LIMIT = 56 * 1024 * 1024

ADAM_LR = 0.001
ADAM_B1 = 0.9
ADAM_B2 = 0.999
ADAM_EPS = 1e-08
ADAM_WD = 0.01
ADAM_STEP = 10


def _params(*sem):
    return pltpu.CompilerParams(dimension_semantics=sem, vmem_limit_bytes=VMEM_LIMIT)


def _dot(a, b):
    return jnp.dot(a, b, preferred_element_type=F32)


def _dot_nt(a, b):
    return lax.dot_general(a, b, (((1,), (1,)), ((), ())), preferred_element_type=F32)


def _dot_tn(a, b):
    return lax.dot_general(a, b, (((0,), (0,)), ((), ())), preferred_element_type=F32)


def _resident(shape):
    return pl.BlockSpec(shape, lambda *_: (0,) * len(shape), pipeline_mode=pl.Buffered(1))


def _rows(tm, width):
    return pl.BlockSpec((tm, width), lambda i: (i, 0))


WIDE_STREAM_BUFFERS = 3


ORDER_ONLY = pl.BlockSpec(memory_space=pl.ANY)


def _rms_scale(x):
    return lax.rsqrt(jnp.mean(x * x, axis=-1, keepdims=True) + EPS)


def _rms_bwd(dh, x, gain):
    r = _rms_scale(x)
    n = x * r
    dgain = jnp.sum(dh * n, axis=0, keepdims=True)
    dn = dh * gain
    dx = r * (dn - n * jnp.mean(dn * n, axis=-1, keepdims=True))
    return dx, dgain


def _split3(x):
    hi = x.astype(BF16)
    r1 = x - hi.astype(F32)
    mid = r1.astype(BF16)
    lo = (r1 - mid.astype(F32)).astype(BF16)
    return hi, mid, lo


FF_CHUNK = 256


def _swiglu_parts(a, b):
    sig = jax.nn.sigmoid(a)
    silu = a * sig
    return (b * (sig + silu * (1.0 - sig))).astype(BF16), silu.astype(BF16), (silu * b).astype(BF16)


def _ffn_up(x, gain, wg_t, wu_t, name):
    t, d = x.shape
    f = wg_t.shape[0]
    tm = 512

    def body(x_ref, g_ref, wg_ref, wu_ref, h_ref, sa_ref, sb_ref, s_ref):
        xv = x_ref[...]
        h = (xv * _rms_scale(xv) * g_ref[...]).astype(BF16)
        h_ref[...] = h
        for c in range(f // FF_CHUNK):
            sl = pl.ds(c * FF_CHUNK, FF_CHUNK)
            sa_ref[:, sl], sb_ref[:, sl], s_ref[:, sl] = _swiglu_parts(_dot_nt(h, wg_ref[sl, :]), _dot_nt(h, wu_ref[sl, :]))

    wide = jax.ShapeDtypeStruct((t, f), BF16)
    return pl.pallas_call(
        body, name=name, grid=(t // tm,),
        in_specs=[_rows(tm, d), _resident((1, d)), _resident((f, d)), _resident((f, d))],
        out_specs=[_rows(tm, d), _rows(tm, f), _rows(tm, f), _rows(tm, f)],
        out_shape=[jax.ShapeDtypeStruct((t, d), BF16), wide, wide, wide],
        compiler_params=_params("arbitrary"),
    )(x, gain, wg_t, wu_t)


def _ffn_down(s, wd, x, target, name):
    t, d = x.shape
    f = wd.shape[0]
    tm = 512
    with_loss = target is not None

    def body(*refs):
        if with_loss:
            s_ref, w_ref, x_ref, t_ref, dy_ref, dyh_ref, loss_ref = refs
        else:
            s_ref, w_ref, x_ref, y_ref = refs
        y = x_ref[...] + 0.5 * _dot(s_ref[...], w_ref[...])
        if with_loss:
            e = y - t_ref[...]
            dy = e * (1.0 / d)
            dy_ref[...] = dy
            dyh_ref[...] = (0.5 * dy).astype(BF16)

            @pl.when(pl.program_id(0) == 0)
            def _():
                loss_ref[...] = jnp.zeros_like(loss_ref)

            part = jnp.sum(jnp.sum(e * e, axis=0, keepdims=True), axis=1, keepdims=True)
            loss_ref[...] += part * (0.5 / d)
        else:
            y_ref[...] = y

    in_specs = [_rows(tm, f), _resident((f, d)), _rows(tm, d)]
    args = [s, wd, x]
    if with_loss:
        in_specs.append(_rows(tm, d))
        args.append(target)
        out_shape = [jax.ShapeDtypeStruct((t, d), F32), jax.ShapeDtypeStruct((t, d), BF16),
                     jax.ShapeDtypeStruct((1, 1), F32)]
        out_specs = [_rows(tm, d), _rows(tm, d), pl.BlockSpec((1, 1), lambda i: (0, 0))]
    else:
        out_shape = [jax.ShapeDtypeStruct((t, d), F32)]
        out_specs = [_rows(tm, d)]
    return pl.pallas_call(
        body, name=name, grid=(t // tm,), in_specs=in_specs, out_specs=out_specs, out_shape=out_shape,
        compiler_params=_params("arbitrary"),
    )(*args)


def _ffn_bwd_act(dyh, sa, sb, h, wd, after, name):
    t, d = dyh.shape
    f = wd.shape[0]
    tn = f // 2
    tk = 512
    nk = t // tk

    def body(dy_ref, sa_ref, sb_ref, h_ref, wd_ref, after_ref, da_ref, db_ref, dwg_ref, dwu_ref, acc_g, acc_u):
        k = pl.program_id(1)

        @pl.when(k == 0)
        def _():
            acc_g[...] = jnp.zeros_like(acc_g)
            acc_u[...] = jnp.zeros_like(acc_u)

        ds = _dot_nt(dy_ref[...], wd_ref[...])
        da = (ds * sa_ref[...].astype(F32)).astype(BF16)
        db = (ds * sb_ref[...].astype(F32)).astype(BF16)
        da_ref[...] = da
        db_ref[...] = db
        hv = h_ref[...]
        acc_g[...] += _dot_tn(da, hv)
        acc_u[...] += _dot_tn(db, hv)

        @pl.when(k == nk - 1)
        def _():
            dwg_ref[...] = acc_g[...].astype(BF16)
            dwu_ref[...] = acc_u[...].astype(BF16)

    tokens = pl.BlockSpec((tk, d), lambda j, k: (k, 0))
    wide = pl.BlockSpec((tk, tn), lambda j, k: (k, j))
    weight = pl.BlockSpec((tn, d), lambda j, k: (j, 0))
    return pl.pallas_call(
        body, name=name, grid=(f // tn, nk),
        in_specs=[tokens, wide, wide, tokens, weight, ORDER_ONLY],
        out_specs=[wide, wide, weight, weight],
        out_shape=[jax.ShapeDtypeStruct((t, f), BF16)] * 2 + [jax.ShapeDtypeStruct((f, d), BF16)] * 2,
        scratch_shapes=[pltpu.VMEM((tn, d), F32)] * 2,
        compiler_params=_params("arbitrary", "arbitrary"),
    )(dyh, sa, sb, h, wd, after)


def _ffn_bwd_dx(da, db, dy, x, gain, wg_t, wu_t, after, name):
    t, d = x.shape
    f = wg_t.shape[0]
    tm = 512
    nm = t // tm
    ahead = WIDE_STREAM_BUFFERS - 1

    def body(da_hbm, db_hbm, dy_ref, x_ref, g_ref, wg_ref, wu_ref, after_ref, dx_ref, dg_ref, da_buf, db_buf, sems):
        i = pl.program_id(0)

        def fetch(blk):
            slot = blk % WIDE_STREAM_BUFFERS
            rows = pl.ds(pl.multiple_of(blk * tm, tm), tm)
            return (pltpu.make_async_copy(da_hbm.at[rows], da_buf.at[slot], sems.at[0, slot]),
                    pltpu.make_async_copy(db_hbm.at[rows], db_buf.at[slot], sems.at[1, slot]))

        @pl.when(i == 0)
        def _():
            dg_ref[...] = jnp.zeros_like(dg_ref)
            for blk in range(min(ahead, nm)):
                for cp in fetch(blk):
                    cp.start()

        @pl.when(i + ahead < nm)
        def _():
            for cp in fetch(i + ahead):
                cp.start()

        for cp in fetch(i):
            cp.wait()
        slot = i % WIDE_STREAM_BUFFERS
        dh = _dot(da_buf[slot], wg_ref[...]) + _dot(db_buf[slot], wu_ref[...])
        dx, dgain = _rms_bwd(dh, x_ref[...], g_ref[...])
        dx_ref[...] = dy_ref[...] + dx
        dg_ref[...] += dgain

    return pl.pallas_call(
        body, name=name, grid=(nm,),
        in_specs=[ORDER_ONLY, ORDER_ONLY, _rows(tm, d), _rows(tm, d), _resident((1, d)), _resident((f, d)),
                  _resident((f, d)), ORDER_ONLY],
        out_specs=[_rows(tm, d), pl.BlockSpec((1, d), lambda i: (0, 0))],
        out_shape=[jax.ShapeDtypeStruct((t, d), F32), jax.ShapeDtypeStruct((1, d), F32)],
        scratch_shapes=[pltpu.VMEM((WIDE_STREAM_BUFFERS, tm, f), BF16), pltpu.VMEM((WIDE_STREAM_BUFFERS, tm, f), BF16),
                        pltpu.SemaphoreType.DMA((2, WIDE_STREAM_BUFFERS))],
        compiler_params=_params("arbitrary"),
    )(da, db, dy, x, gain, wg_t, wu_t, after)


def _wgrad(lhs, b, after, name):
    t, n = lhs[0].shape
    d = b.shape[1]
    m = len(lhs)
    tn = n // 2 if n * d * m > (4 << 20) else n
    tk = 1024
    nk = t // tk

    def body(*refs):
        a_refs, b_ref, o_refs, accs = refs[:m], refs[m], refs[m + 2:2 * m + 2], refs[2 * m + 2:]
        k = pl.program_id(1)

        @pl.when(k == 0)
        def _():
            for acc in accs:
                acc[...] = jnp.zeros_like(acc)

        bv = b_ref[...]
        for a_ref, acc in zip(a_refs, accs):
            acc[...] += _dot_tn(a_ref[...], bv)

        @pl.when(k == nk - 1)
        def _():
            for o_ref, acc in zip(o_refs, accs):
                o_ref[...] = acc[...].astype(BF16)

    return pl.pallas_call(
        body, name=name, grid=(n // tn, nk),
        in_specs=[pl.BlockSpec((tk, tn), lambda j, k: (k, j))] * m + [pl.BlockSpec((tk, d), lambda j, k: (k, 0)),
                                                                       ORDER_ONLY],
        out_specs=[pl.BlockSpec((tn, d), lambda j, k: (j, 0))] * m,
        out_shape=[jax.ShapeDtypeStruct((n, d), BF16)] * m,
        scratch_shapes=[pltpu.VMEM((tn, d), F32)] * m,
        compiler_params=_params("arbitrary", "arbitrary"),
    )(*lhs, b, after)


def _repack_rows(a, rows_in, rows_out, blocks, name):
    total, d = a.shape
    real = min(rows_in, rows_out)

    def body(a_ref, o_ref, wide_in, wide_out):
        wide_in[...] = a_ref[...].astype(F32)
        wide_out[...] = jnp.zeros_like(wide_out)
        for j in range(blocks):
            wide_out[pl.ds(j * rows_out, real), :] = wide_in[pl.ds(j * rows_in, real), :]
        o_ref[...] = wide_out[...].astype(BF16)

    full = pl.BlockSpec((total, d), lambda i: (0, 0))
    return pl.pallas_call(
        body, name=name, grid=(1,), in_specs=[full], out_specs=full, out_shape=jax.ShapeDtypeStruct((total, d), BF16),
        scratch_shapes=[pltpu.VMEM((total, d), F32)] * 2,
        compiler_params=_params("arbitrary"),
    )(a)


def _mix_in_fwd(x, gain, w_in_t):
    t, d = x.shape
    tm = 1024
    pw, aw = POOL_WIDTH, ATTN_WIDTH

    def body(x_ref, g_ref, w_ref, hm_ref, pv_ref, q_ref, k_ref, v_ref, f_ref):
        xv = x_ref[...]
        hm = (xv * _rms_scale(xv) * g_ref[...]).astype(BF16)
        hm_ref[...] = hm
        pv_ref[...] = _dot_nt(hm, w_ref[pl.ds(0, pw), :])
        q_ref[...] = _dot_nt(hm, w_ref[pl.ds(pw, aw), :])
        k_ref[...] = _dot_nt(hm, w_ref[pl.ds(pw + aw, aw), :])
        v_ref[...] = _dot_nt(hm, w_ref[pl.ds(pw + 2 * aw, aw), :]).astype(BF16)
        f_ref[...] = _dot_nt(hm, w_ref[pl.ds(pw + 3 * aw, LANES), :])

    return pl.pallas_call(
        body, name="mix_in_fwd", grid=(t // tm,),
        in_specs=[_rows(tm, d), _resident((1, d)), _resident((MIX_PAD, d))],
        out_specs=[_rows(tm, d), _rows(tm, pw), _rows(tm, aw), _rows(tm, aw), _rows(tm, aw), _rows(tm, LANES)],
        out_shape=[jax.ShapeDtypeStruct((t, d), BF16), jax.ShapeDtypeStruct((t, pw), F32),
                   jax.ShapeDtypeStruct((t, aw), F32), jax.ShapeDtypeStruct((t, aw), F32),
                   jax.ShapeDtypeStruct((t, aw), BF16), jax.ShapeDtypeStruct((t, LANES), F32)],
        compiler_params=_params("arbitrary"),
    )(x, gain, w_in_t)


def _pool_fwd(pv, pool_w, pool_scale, gain, bsz, seq):
    ts = 512
    ns = seq // ts
    pw = POOL_WIDTH

    def body(pv_ref, w_ref, sc_ref, g_ref, pooled_ref, mixed_ref, y_ref, ext):
        s = pl.program_id(1)

        @pl.when(s == 0)
        def _():
            ext[pl.ds(0, POOL_HALO), :] = jnp.zeros((POOL_HALO, pw), F32)

        p = pv_ref[...]
        ext[pl.ds(POOL_HALO, ts), :] = p
        pos = s * ts + lax.broadcasted_iota(jnp.int32, (ts, 1), 0)
        parts = []
        for g, w in enumerate(POOL_WINDOWS):
            lanes = pl.ds(g * POOL_GROUP_DIM, POOL_GROUP_DIM)
            win = ext[pl.ds(POOL_HALO, ts), lanes]
            for i in range(1, w):
                win = win + ext[pl.ds(POOL_HALO - i, ts), lanes]
            cnt = jnp.minimum(pos + 1, w).astype(F32)
            pooled = (win / cnt - ext[pl.ds(POOL_HALO, ts), lanes]).astype(BF16)
            pooled_ref[:, lanes] = pooled
            parts.append(_dot(pooled, w_ref[g].astype(BF16)))
        mixed = jnp.concatenate(parts, axis=1)
        mixed_ref[...] = mixed
        pm = mixed * sc_ref[...]
        y_ref[...] = (pm * _rms_scale(pm) * g_ref[...]).astype(BF16)
        ext[pl.ds(0, POOL_HALO), :] = p[ts - POOL_HALO:, :]

    blk = pl.BlockSpec((ts, pw), lambda b, s: (b * ns + s, 0))
    t = bsz * seq
    return pl.pallas_call(
        body, name="pool_fwd", grid=(bsz, ns),
        in_specs=[blk, pl.BlockSpec((POOL_GROUPS, POOL_GROUP_DIM, POOL_GROUP_DIM), lambda b, s: (0, 0, 0)),
                  pl.BlockSpec((1, pw), lambda b, s: (0, 0)), pl.BlockSpec((1, pw), lambda b, s: (0, 0))],
        out_specs=[blk, blk, blk],
        out_shape=[jax.ShapeDtypeStruct((t, pw), BF16), jax.ShapeDtypeStruct((t, pw), F32),
                   jax.ShapeDtypeStruct((t, pw), BF16)],
        scratch_shapes=[pltpu.VMEM((POOL_HALO + ts, pw), F32)],
        compiler_params=_params("arbitrary", "arbitrary"),
    )(pv, pool_w, pool_scale, gain)


def _pool_bwd(dy, mixed, pooled, pool_w, pool_scale, gain, bsz, seq):
    ts = 512
    ns = seq // ts
    pw = POOL_WIDTH

    def body(dy_ref, mixed_ref, pooled_ref, w_ref, sc_ref, g_ref, dpv_ref, dw_ref, dsc_ref, dg_ref, ext):
        b = pl.program_id(0)
        sr = pl.program_id(1)
        s = ns - 1 - sr

        @pl.when(jnp.logical_and(b == 0, sr == 0))
        def _():
            dw_ref[...] = jnp.zeros_like(dw_ref)
            dsc_ref[...] = jnp.zeros_like(dsc_ref)
            dg_ref[...] = jnp.zeros_like(dg_ref)

        @pl.when(sr == 0)
        def _():
            ext[pl.ds(ts, POOL_HALO), :] = jnp.zeros((POOL_HALO, pw), F32)

        mixed = mixed_ref[...]
        sc = sc_ref[...]
        dpm, dgain = _rms_bwd(dy_ref[...], mixed * sc, g_ref[...])
        dg_ref[...] += dgain
        dsc_ref[...] += jnp.sum(dpm * mixed, axis=0, keepdims=True)
        dmixed = (dpm * sc).astype(BF16)
        pos = s * ts + lax.broadcasted_iota(jnp.int32, (ts, 1), 0)
        dpooled = []
        for g, w in enumerate(POOL_WINDOWS):
            lanes = pl.ds(g * POOL_GROUP_DIM, POOL_GROUP_DIM)
            dm = dmixed[:, g * POOL_GROUP_DIM:(g + 1) * POOL_GROUP_DIM]
            dw_ref[g] += _dot_tn(pooled_ref[:, lanes], dm)
            dp = _dot_nt(dm, w_ref[g].astype(BF16))
            dpooled.append(dp)
            cnt = jnp.minimum(pos + 1, w).astype(F32)
            ext[pl.ds(0, ts), lanes] = dp / cnt
        for g, w in enumerate(POOL_WINDOWS):
            lanes = pl.ds(g * POOL_GROUP_DIM, POOL_GROUP_DIM)
            win = ext[pl.ds(0, ts), lanes]
            for i in range(1, w):
                win = win + ext[pl.ds(i, ts), lanes]
            dpv_ref[:, lanes] = (win - dpooled[g]).astype(BF16)
        head = ext[pl.ds(0, POOL_HALO), :]
        ext[pl.ds(ts, POOL_HALO), :] = head

    blk = pl.BlockSpec((ts, pw), lambda b, s: (b * ns + (ns - 1 - s), 0))
    vec = pl.BlockSpec((1, pw), lambda b, s: (0, 0))
    wspec = pl.BlockSpec((POOL_GROUPS, POOL_GROUP_DIM, POOL_GROUP_DIM), lambda b, s: (0, 0, 0))
    t = bsz * seq
    return pl.pallas_call(
        body, name="pool_bwd", grid=(bsz, ns),
        in_specs=[blk, blk, blk, wspec, vec, vec],
        out_specs=[blk, wspec, vec, vec],
        out_shape=[jax.ShapeDtypeStruct((t, pw), BF16),
                   jax.ShapeDtypeStruct((POOL_GROUPS, POOL_GROUP_DIM, POOL_GROUP_DIM), F32),
                   jax.ShapeDtypeStruct((1, pw), F32), jax.ShapeDtypeStruct((1, pw), F32)],
        scratch_shapes=[pltpu.VMEM((ts + POOL_HALO, pw), F32)],
        compiler_params=_params("arbitrary", "arbitrary"),
    )(dy, mixed, pooled, pool_w, pool_scale, gain)


AUX_ONE = 64
AUX_F = 67

ATTN_PREP_ROWS = 512


def _seg_ones(width, seg):
    r = lax.broadcasted_iota(jnp.int32, (width, width), 0) // seg
    c = lax.broadcasted_iota(jnp.int32, (width, width), 1) // seg
    return (r == c).astype(BF16)


def _tri_ones(n, lower):
    r = lax.broadcasted_iota(jnp.int32, (n, n), 0)
    c = lax.broadcasted_iota(jnp.int32, (n, n), 1)
    return ((r >= c) if lower else (r <= c)).astype(BF16)


def _place_pieces(first_lane):
    r = lax.broadcasted_iota(jnp.int32, (3 * LANES, N_HEADS * LANES), 0)
    c = lax.broadcasted_iota(jnp.int32, (3 * LANES, N_HEADS * LANES), 1)
    piece, head = r // LANES, r % LANES
    return jnp.logical_and(head < N_HEADS, c == head * LANES + first_lane + piece).astype(BF16)


def _head_sums(x, seg_ones):
    return _dot(x.astype(BF16), seg_ones)


def _log_sigmoid(x):
    return jnp.minimum(x, 0.0) - jnp.log(1.0 + jnp.exp(-jnp.abs(x)))


def _attn_prep_fwd(q, k, f, b_forget, q_gain, k_gain, bsz, seq):
    ts = ATTN_PREP_ROWS
    ns = seq // ts
    aw = ATTN_WIDTH
    t = bsz * seq
    seg = _seg_ones(aw, HEAD_DIM)
    tri = _tri_ones(ts, True)

    def body(q_ref, k_ref, f_ref, bf_ref, gq_ref, gk_ref, seg_ref, tri_ref, place_ref, qp_ref, kp_ref, carry):
        s = pl.program_id(1)

        @pl.when(s == 0)
        def _():
            carry[...] = jnp.zeros_like(carry)

        logf = _log_sigmoid(f_ref[...] + bf_ref[...])
        hi, mid, lo = _split3(logf)
        tri_v = tri_ref[...]
        fc = _dot(tri_v, hi) + _dot(tri_v, mid) + _dot(tri_v, lo) + carry[pl.ds(0, 1), :]
        carry[pl.ds(0, 1), :] = fc[ts - 1:, :]
        pcs = jnp.concatenate(_split3(fc), axis=1)
        lane = lax.broadcasted_iota(jnp.int32, (1, LANES), 1)
        ones_q = jnp.logical_and(lane >= AUX_ONE, lane < AUX_ONE + 3).astype(F32)
        ones_k = jnp.logical_and(lane >= AUX_F, lane < AUX_F + 3).astype(F32)
        seg_v = seg_ref[...]
        placed = _dot(pcs, place_ref[...])

        def build(x_ref, g_ref, scale, out_ref, ones, for_keys):
            xv = x_ref[...]
            r = lax.rsqrt(_head_sums(xv * xv, seg_v) * (1.0 / HEAD_DIM) + EPS)
            xn = xv * r * g_ref[...] * scale
            for h in range(N_HEADS):
                pair = xn[:, (h // 2) * LANES:(h // 2 + 1) * LANES]
                feat = pair if h % 2 == 0 else pltpu.roll(pair, HEAD_DIM, 1)
                aux_h = placed[:, h * LANES:(h + 1) * LANES]
                if for_keys:
                    aux_h = -pltpu.roll(aux_h, LANES - (AUX_F - AUX_ONE), 1)
                out_ref[:, h * LANES:(h + 1) * LANES] = jnp.where(lane < HEAD_DIM, feat, aux_h + ones).astype(BF16)

        build(q_ref, gq_ref, 0.125, qp_ref, ones_q, False)
        build(k_ref, gk_ref, 1.0, kp_ref, ones_k, True)

    blk = pl.BlockSpec((ts, aw), lambda b, s: (b * ns + s, 0))
    fblk = pl.BlockSpec((ts, LANES), lambda b, s: (b * ns + s, 0))
    oblk = pl.BlockSpec((ts, N_HEADS * LANES), lambda b, s: (b * ns + s, 0))
    const = lambda shape: pl.BlockSpec(shape, lambda b, s: (0, 0))
    return pl.pallas_call(
        body, name="attn_prep_fwd", grid=(bsz, ns),
        in_specs=[blk, blk, fblk, const((1, LANES)), const((1, aw)), const((1, aw)), const((aw, aw)), const((ts, ts)),
                  const((3 * LANES, N_HEADS * LANES))],
        out_specs=[oblk, oblk],
        out_shape=[jax.ShapeDtypeStruct((t, N_HEADS * LANES), BF16)] * 2,
        scratch_shapes=[pltpu.VMEM((8, LANES), F32)],
        compiler_params=_params("arbitrary", "arbitrary"),
    )(q, k, f, b_forget, q_gain, k_gain, seg, tri, _place_pieces(AUX_F))


def _attn_prep_bwd(dqp, dkp, q, k, f, b_forget, q_gain, k_gain, bsz, seq):
    ts = ATTN_PREP_ROWS
    ns = seq // ts
    aw = ATTN_WIDTH
    t = bsz * seq
    seg = _seg_ones(aw, HEAD_DIM)
    tri = _tri_ones(ts, False)

    def body(dqp_ref, dkp_ref, q_ref, k_ref, f_ref, bf_ref, gq_ref, gk_ref, seg_ref, tri_ref,
             dq_ref, dk_ref, df_ref, dgq_ref, dgk_ref, dbf_ref, carry):
        b = pl.program_id(0)
        sr = pl.program_id(1)

        @pl.when(jnp.logical_and(b == 0, sr == 0))
        def _():
            dgq_ref[...] = jnp.zeros_like(dgq_ref)
            dgk_ref[...] = jnp.zeros_like(dgk_ref)
            dbf_ref[...] = jnp.zeros_like(dbf_ref)

        @pl.when(sr == 0)
        def _():
            carry[...] = jnp.zeros_like(carry)

        lane = lax.broadcasted_iota(jnp.int32, (1, LANES), 1)
        seg_v = seg_ref[...]

        def norm_bwd(dp_ref, x_ref, g_ref, scale, dx_ref, dgain_ref):
            parts = []
            for j in range(N_HEADS // 2):
                even = dp_ref[:, (2 * j) * LANES:(2 * j + 1) * LANES]
                odd = dp_ref[:, (2 * j + 1) * LANES:(2 * j + 2) * LANES]
                parts.append(jnp.where(lane < HEAD_DIM, even, pltpu.roll(odd, HEAD_DIM, 1)))
            dxn = jnp.concatenate(parts, axis=1) * scale
            xv = x_ref[...]
            r = lax.rsqrt(_head_sums(xv * xv, seg_v) * (1.0 / HEAD_DIM) + EPS)
            n = xv * r
            dgain_ref[...] += jnp.sum(dxn * n, axis=0, keepdims=True)
            dn = dxn * g_ref[...]
            m = _head_sums(dn * n, seg_v) * (1.0 / HEAD_DIM)
            dx_ref[...] = (r * (dn - n * m)).astype(BF16)

        norm_bwd(dqp_ref, q_ref, gq_ref, 0.125, dq_ref, dgq_ref)
        norm_bwd(dkp_ref, k_ref, gk_ref, 1.0, dk_ref, dgk_ref)

        dfc = jnp.zeros((ts, LANES), F32)
        for h in range(N_HEADS):
            cols = pl.ds(h * LANES, LANES)
            both = jnp.where(lane == AUX_F, dqp_ref[:, cols], 0.0) - jnp.where(lane == AUX_ONE, dkp_ref[:, cols], 0.0)
            dfc = jnp.where(lane == h, jnp.sum(both, axis=1, keepdims=True), dfc)
        hi, mid, lo = _split3(dfc)
        tri_v = tri_ref[...]
        dlogf = _dot(tri_v, hi) + _dot(tri_v, mid) + _dot(tri_v, lo) + carry[pl.ds(0, 1), :]
        carry[pl.ds(0, 1), :] = dlogf[0:1, :]
        df = jnp.where(lane < N_HEADS, dlogf * jax.nn.sigmoid(-(f_ref[...] + bf_ref[...])), 0.0)
        df_ref[...] = df.astype(BF16)
        dbf_ref[...] += jnp.sum(df, axis=0, keepdims=True)

    rev = lambda b, s: (b * ns + (ns - 1 - s), 0)
    blk = pl.BlockSpec((ts, aw), rev)
    fblk = pl.BlockSpec((ts, LANES), rev)
    pblk = pl.BlockSpec((ts, N_HEADS * LANES), rev)
    const = lambda shape: pl.BlockSpec(shape, lambda b, s: (0, 0))
    return pl.pallas_call(
        body, name="attn_prep_bwd", grid=(bsz, ns),
        in_specs=[pblk, pblk, blk, blk, fblk, const((1, LANES)), const((1, aw)), const((1, aw)), const((aw, aw)),
                  const((ts, ts))],
        out_specs=[blk, blk, fblk, const((1, aw)), const((1, aw)), const((1, LANES))],
        out_shape=[jax.ShapeDtypeStruct((t, aw), BF16), jax.ShapeDtypeStruct((t, aw), BF16),
                   jax.ShapeDtypeStruct((t, LANES), BF16), jax.ShapeDtypeStruct((1, aw), F32),
                   jax.ShapeDtypeStruct((1, aw), F32), jax.ShapeDtypeStruct((1, LANES), F32)],
        scratch_shapes=[pltpu.VMEM((8, LANES), F32)],
        compiler_params=_params("arbitrary", "arbitrary"),
    )(dqp, dkp, q, k, f, b_forget, q_gain, k_gain, seg, tri)


ATTN_BLOCK = 1024
HEAD_PAIRS = N_HEADS // 2


def _flash_fwd(qp, kp, v, bsz, seq):
    tq = ATTN_BLOCK
    half = tq // 2
    nq = seq // tq
    t = bsz * seq

    def body(q_ref, k_ref, v_ref, o_ref, lse_ref, m_sc, l_sc, acc_sc):
        i = pl.program_id(2)
        m_sc[...] = jnp.full(m_sc.shape, -jnp.inf, F32)
        l_sc[...] = jnp.zeros_like(l_sc)
        acc_sc[...] = jnp.zeros_like(acc_sc)
        lane = lax.broadcasted_iota(jnp.int32, (1, LANES), 1)
        low = lane < HEAD_DIM

        def tile(q0, qn, k_start, kn, k0=None):
            qs = pl.ds(q0, qn)
            ks = pl.ds(k_start, kn)
            vv = v_ref[ks, :]
            for h in range(2):
                mine = low if h == 0 else jnp.logical_not(low)
                cols = pl.ds(h * LANES, LANES)
                s = _dot_nt(q_ref[qs, cols], k_ref[ks, cols])
                if k0 is not None:
                    row = lax.broadcasted_iota(jnp.int32, (qn, kn), 0) + q0
                    col = lax.broadcasted_iota(jnp.int32, (qn, kn), 1) + k0
                    s = jnp.where(row >= col, s, -jnp.inf)
                m_prev = m_sc[h, qs, :]
                m_new = jnp.maximum(m_prev, jnp.max(s, axis=1, keepdims=True))
                p = jnp.exp(s - jnp.tile(m_new, (1, kn // LANES)))
                alpha = jnp.exp(m_prev - m_new)
                l_sc[h, qs, :] = alpha * l_sc[h, qs, :] + jnp.sum(p, axis=1, keepdims=True)
                m_sc[h, qs, :] = m_new
                pv = _dot(p.astype(BF16), jnp.where(mine, vv, jnp.zeros_like(vv)))
                acc_sc[qs, :] = acc_sc[qs, :] * jnp.where(mine, alpha, 1.0) + pv

        def below_diagonal(j, carry):
            tile(0, tq, pl.multiple_of(j * tq, tq), tq)
            return carry

        lax.fori_loop(0, i, below_diagonal, 0)
        diagonal = pl.multiple_of(i * tq, tq)
        tile(0, tq, diagonal, half, k0=0)
        tile(half, half, diagonal + half, half, k0=half)
        l = jnp.where(low, l_sc[0], l_sc[1])
        m = jnp.where(low, m_sc[0], m_sc[1])
        o_ref[...] = acc_sc[...] / l
        lse_ref[...] = m + jnp.log(l)

    qspec = pl.BlockSpec((tq, 2 * LANES), lambda b, hp, i: (b * nq + i, hp))
    kspec = pl.BlockSpec((seq, 2 * LANES), lambda b, hp, i: (b, hp))
    vspec = pl.BlockSpec((seq, LANES), lambda b, hp, i: (b, hp))
    ospec = pl.BlockSpec((tq, LANES), lambda b, hp, i: (b * nq + i, hp))
    return pl.pallas_call(
        body, name="flash_fwd", grid=(bsz, HEAD_PAIRS, nq),
        in_specs=[qspec, kspec, vspec], out_specs=[ospec, ospec],
        out_shape=[jax.ShapeDtypeStruct((t, ATTN_WIDTH), F32), jax.ShapeDtypeStruct((t, ATTN_WIDTH), F32)],
        scratch_shapes=[pltpu.VMEM((2, tq, LANES), F32), pltpu.VMEM((2, tq, LANES), F32), pltpu.VMEM((tq, LANES), F32)],
        compiler_params=_params("arbitrary", "arbitrary", "arbitrary"),
    )(qp, kp, v)


def _flash_bwd(qp, kp, v, o, do, lse, after, bsz, seq):
    tq = ATTN_BLOCK
    half = tq // 2
    nq = seq // tq
    t = bsz * seq

    def body(q_ref, k_ref, v_ref, o_ref, do_ref, lse_ref, after_ref, dq_ref, dk_ref, dv_ref, dk_acc, dv_acc):
        j = pl.program_id(2)

        @pl.when(j == 0)
        def _():
            dq_ref[...] = jnp.zeros_like(dq_ref)

        dk_acc[...] = jnp.zeros_like(dk_acc)
        dv_acc[...] = jnp.zeros_like(dv_acc)
        lane = lax.broadcasted_iota(jnp.int32, (1, LANES), 1)
        low = lane < HEAD_DIM

        def tile(q_start, qn, k0, kn, q0=None):
            rows = pl.ds(q_start, qn)
            ks = pl.ds(k0, kn)
            dov = do_ref[rows, :]
            dd = dov * o_ref[rows, :]
            dob = dov.astype(BF16)
            vv = v_ref[ks, :]
            lse_v = lse_ref[rows, :]
            for h in range(2):
                mine = low if h == 0 else jnp.logical_not(low)
                cols = pl.ds(h * LANES, LANES)
                qh = q_ref[rows, cols]
                kh = k_ref[ks, cols]
                s = _dot_nt(qh, kh)
                lse_h = jnp.where(mine, lse_v, pltpu.roll(lse_v, HEAD_DIM, 1))
                p = jnp.exp(s - jnp.tile(lse_h, (1, kn // LANES)))
                if q0 is not None:
                    row = lax.broadcasted_iota(jnp.int32, (qn, kn), 0) + q0
                    col = lax.broadcasted_iota(jnp.int32, (qn, kn), 1) + k0
                    p = jnp.where(row >= col, p, 0.0)
                delta = jnp.sum(jnp.where(mine, dd, 0.0), axis=1, keepdims=True)
                dp = _dot_nt(dob, jnp.where(mine, vv, jnp.zeros_like(vv)))
                ds = (p * (dp - delta)).astype(BF16)
                dv_acc[ks, :] += jnp.where(mine, _dot_tn(p.astype(BF16), dob), 0.0)
                dk_acc[ks, cols] += _dot_tn(ds, qh)
                dq_ref[rows, cols] += _dot(ds, kh)

        def above_diagonal(i, carry):
            tile(pl.multiple_of(i * tq, tq), tq, 0, tq)
            return carry

        diagonal = pl.multiple_of(j * tq, tq)
        tile(diagonal, tq, 0, half, q0=0)
        tile(diagonal + half, half, half, half, q0=half)
        lax.fori_loop(j + 1, nq, above_diagonal, 0)
        dk_ref[...] = dk_acc[...]
        dv_ref[...] = dv_acc[...].astype(BF16)

    qspec = pl.BlockSpec((seq, 2 * LANES), lambda b, hp, j: (b, hp))
    kspec = pl.BlockSpec((tq, 2 * LANES), lambda b, hp, j: (b * nq + j, hp))
    vspec = pl.BlockSpec((tq, LANES), lambda b, hp, j: (b * nq + j, hp))
    ospec = pl.BlockSpec((seq, LANES), lambda b, hp, j: (b, hp))
    return pl.pallas_call(
        body, name="flash_bwd", grid=(bsz, HEAD_PAIRS, nq),
        in_specs=[qspec, kspec, vspec, ospec, ospec, ospec, ORDER_ONLY], out_specs=[qspec, kspec, vspec],
        out_shape=[jax.ShapeDtypeStruct((t, N_HEADS * LANES), F32), jax.ShapeDtypeStruct((t, N_HEADS * LANES), F32),
                   jax.ShapeDtypeStruct((t, ATTN_WIDTH), BF16)],
        scratch_shapes=[pltpu.VMEM((tq, 2 * LANES), F32), pltpu.VMEM((tq, LANES), F32)],
        compiler_params=_params("arbitrary", "arbitrary", "arbitrary"),
    )(qp, kp, v, o, do, lse, after)


def _mix_out_fwd(o, y_pool, x, gain, w_out):
    t, d = x.shape
    tm = 1024
    pw, aw = POOL_WIDTH, ATTN_WIDTH

    def body(o_ref, yp_ref, x_ref, g_ref, w_ref, ycat_ref, y_ref):
        ov = o_ref[...]
        ya = (ov * _rms_scale(ov) * g_ref[...]).astype(BF16)
        ycat = jnp.concatenate([yp_ref[...], ya], axis=1)
        ycat_ref[...] = ycat
        y_ref[...] = x_ref[...] + _dot(ycat, w_ref[...])

    return pl.pallas_call(
        body, name="mix_out_fwd", grid=(t // tm,),
        in_specs=[_rows(tm, aw), _rows(tm, pw), _rows(tm, d), _resident((1, aw)), _resident((pw + aw, d))],
        out_specs=[_rows(tm, pw + aw), _rows(tm, d)],
        out_shape=[jax.ShapeDtypeStruct((t, pw + aw), BF16), jax.ShapeDtypeStruct((t, d), F32)],
        compiler_params=_params("arbitrary"),
    )(o, y_pool, x, gain, w_out)


def _mix_out_bwd(dx, o, ycat, gain, w_out):
    t, d = dx.shape
    tm = 1024
    nm = t // tm
    pw, aw = POOL_WIDTH, ATTN_WIDTH

    def body(dx_ref, o_ref, ycat_ref, g_ref, w_ref, dw_ref, dyp_ref, do_ref, dg_ref, acc):
        i = pl.program_id(0)

        @pl.when(i == 0)
        def _():
            dg_ref[...] = jnp.zeros_like(dg_ref)
            acc[...] = jnp.zeros_like(acc)

        dxb = dx_ref[...].astype(BF16)
        acc[...] += _dot_tn(ycat_ref[...], dxb)
        dyp_ref[...] = _dot_nt(dxb, w_ref[pl.ds(0, pw), :])
        dya = _dot_nt(dxb, w_ref[pl.ds(pw, aw), :])
        do, dgain = _rms_bwd(dya, o_ref[...], g_ref[...])
        do_ref[...] = do
        dg_ref[...] += dgain

        @pl.when(i == nm - 1)
        def _():
            dw_ref[...] = acc[...].astype(BF16)

    return pl.pallas_call(
        body, name="mix_out_bwd", grid=(nm,),
        in_specs=[_rows(tm, d), _rows(tm, aw), _rows(tm, pw + aw), _resident((1, aw)), _resident((pw + aw, d))],
        out_specs=[pl.BlockSpec((pw + aw, d), lambda i: (0, 0)), _rows(tm, pw), _rows(tm, aw),
                   pl.BlockSpec((1, aw), lambda i: (0, 0))],
        out_shape=[jax.ShapeDtypeStruct((pw + aw, d), BF16), jax.ShapeDtypeStruct((t, pw), F32),
                   jax.ShapeDtypeStruct((t, aw), F32), jax.ShapeDtypeStruct((1, aw), F32)],
        scratch_shapes=[pltpu.VMEM((pw + aw, d), F32)],
        compiler_params=_params("arbitrary"),
    )(dx, o, ycat, gain, w_out)


def _mix_in_bwd(dpv, dq, dk, dv, df, hm, x, dx_res, gain, w_in_t):
    t, d = x.shape
    tm = 512
    nm = t // tm
    pw, aw = POOL_WIDTH, ATTN_WIDTH

    def body(dpv_ref, dq_ref, dk_ref, dv_ref, df_ref, hm_ref, x_ref, dxr_ref, g_ref, w_ref, dw_ref, dx_ref, dxh_ref,
             dg_ref, acc):
        i = pl.program_id(0)

        @pl.when(i == 0)
        def _():
            dg_ref[...] = jnp.zeros_like(dg_ref)
            acc[...] = jnp.zeros_like(acc)

        dh = jnp.concatenate([dpv_ref[...], dq_ref[...], dk_ref[...], dv_ref[...], df_ref[...]], axis=1)
        acc[...] += _dot_tn(dh, hm_ref[...])
        dx, dgain = _rms_bwd(_dot(dh, w_ref[...]), x_ref[...], g_ref[...])
        dx = dxr_ref[...] + dx
        dx_ref[...] = dx
        dxh_ref[...] = (0.5 * dx).astype(BF16)
        dg_ref[...] += dgain

        @pl.when(i == nm - 1)
        def _():
            dw_ref[...] = acc[...].astype(BF16)

    return pl.pallas_call(
        body, name="mix_in_bwd", grid=(nm,),
        in_specs=[_rows(tm, pw), _rows(tm, aw), _rows(tm, aw), _rows(tm, aw), _rows(tm, LANES), _rows(tm, d),
                  _rows(tm, d), _rows(tm, d), _resident((1, d)), _resident((MIX_PAD, d))],
        out_specs=[pl.BlockSpec((MIX_PAD, d), lambda i: (0, 0)), _rows(tm, d), _rows(tm, d),
                   pl.BlockSpec((1, d), lambda i: (0, 0))],
        out_shape=[jax.ShapeDtypeStruct((MIX_PAD, d), BF16), jax.ShapeDtypeStruct((t, d), F32),
                   jax.ShapeDtypeStruct((t, d), BF16), jax.ShapeDtypeStruct((1, d), F32)],
        scratch_shapes=[pltpu.VMEM((MIX_PAD, d), F32)],
        compiler_params=_params("arbitrary"),
    )(dpv, dq, dk, dv, df, hm, x, dx_res, gain, w_in_t)


MESH_IDS = pl.DeviceIdType.MESH


def _me():
    return lax.axis_index("x"), lax.axis_index("y"), lax.axis_index("c")


def _peer(x, y, c, p):
    px = 1 - x if p & 4 else x
    py = 1 - y if p & 2 else y
    pc = 1 - c if p & 1 else c
    return (px, py, pc), 4 * px + 2 * py + pc


HBM_SPEC = pl.BlockSpec(memory_space=pltpu.HBM)
SEM_SPEC = pl.BlockSpec(memory_space=pltpu.SEMAPHORE)
SPLIT_COPY = pltpu.CompilerParams(has_side_effects=pltpu.SideEffectType.DATAFLOW_SIDE_EFFECTING)
PEERS = N_DEV - 1


def _hbm(a):
    return pltpu.with_memory_space_constraint(a, pltpu.HBM)


def _row_block(ref, dev, rows):
    return ref.at[pl.ds(pl.multiple_of(dev * rows, BF16_ROWS), rows)]


def _copy_ends(gather, src, land, me, peer_id):
    if gather:
        rows = src.shape[0]
        return src, _row_block(land, me, rows), _row_block(land, peer_id, rows), src, _row_block(land, me, rows)
    rows = src.shape[0] // N_DEV
    return (_row_block(src, peer_id, rows), land.at[me], land.at[peer_id], _row_block(src, me, rows), land.at[me])


def _land_shape(gather, s):
    return (N_DEV * s.shape[0], s.shape[1]) if gather else (N_DEV, s.shape[0] // N_DEV, s.shape[1])


SIBLING = 1
SAME_CORE_PEERS = (2, 4, 6)
RELAYS = len(SAME_CORE_PEERS)


def _copies_start(groups, gather, name, after=None, relayed=()):
    flat = [s for g in groups for s in g]
    n, ng = len(flat), len(groups)
    lands = [lax.empty(_land_shape(gather, s), s.dtype) for s in flat]
    n_in = 2 * n + (after is not None)

    def body(*refs):
        ins, lnd = refs[:n], refs[n:2 * n]
        sems = refs[n_in:n_in + 2 * ng]
        token = refs[-1]
        x, y, c = _me()
        me = 4 * x + 2 * y + c
        w = 0
        for gi, g in enumerate(groups):
            for k in range(len(g)):
                for p in ((SIBLING,) + SAME_CORE_PEERS if gi in relayed else range(1, N_DEV)):
                    peer, peer_id = _peer(x, y, c, p)
                    src, dst, _, _, _ = _copy_ends(gather, ins[w], lnd[w], me, peer_id)
                    pltpu.make_async_remote_copy(src, dst, sems[2 * gi].at[k * PEERS + p - 1],
                                                 sems[2 * gi + 1].at[k * PEERS + p - 1], device_id=peer,
                                                 device_id_type=MESH_IDS).start()
                w += 1
        token[...] = jnp.zeros_like(token)

    sem_shapes = []
    for g in groups:
        sem_shapes += [pltpu.SemaphoreType.DMA((len(g) * PEERS,))] * 2
    out = pl.pallas_call(
        body, name=name,
        out_shape=(*sem_shapes, *[pltpu.HBM(s.shape, s.dtype) for s in flat],
                   *[pltpu.HBM(l.shape, l.dtype) for l in lands], jax.ShapeDtypeStruct((8, LANES), F32)),
        in_specs=[HBM_SPEC] * (2 * n) + [pl.BlockSpec(memory_space=pl.ANY)] * (after is not None),
        out_specs=(*[SEM_SPEC] * (2 * ng), *[HBM_SPEC] * (2 * n), pl.BlockSpec(memory_space=pltpu.VMEM)),
        input_output_aliases={i: 2 * ng + i for i in range(2 * n)},
        compiler_params=SPLIT_COPY,
    )(*[_hbm(s) for s in flat], *[_hbm(l) for l in lands], *([after] if after is not None else []))
    sems, thru, token = out[:2 * ng], out[2 * ng:2 * ng + 2 * n], out[-1]
    res, w = [], 0
    for gi, g in enumerate(groups):
        res.append((sems[2 * gi], sems[2 * gi + 1], list(thru[w:w + len(g)]), list(thru[n + w:n + w + len(g)])))
        w += len(g)
    return res, token


def _copies_wait(started, gather, after, name):
    send, recv, srcs, lands = started
    n = len(srcs)
    after = list(after) if isinstance(after, (list, tuple)) else [after]

    own_shapes = [s.shape if gather else (s.shape[0] // N_DEV, s.shape[1]) for s in srcs]

    def body(*refs):
        ins, lnd = refs[:n], refs[n:2 * n]
        send_sems, recv_sems = refs[2 * n], refs[2 * n + 1]
        bounce, in_sems, out_sems = refs[-n - 2:-2], refs[-2], refs[-1]
        x, y, c = _me()
        me = 4 * x + 2 * y + c
        ends = [_copy_ends(gather, ins[w], lnd[w], me, me)[3:] for w in range(n)]
        loads = [pltpu.make_async_copy(ends[w][0], bounce[w], in_sems.at[w]) for w in range(n)]
        stores = [pltpu.make_async_copy(bounce[w], ends[w][1], out_sems.at[w]) for w in range(n)]
        for cp in loads:
            cp.start()
        for w in range(n):
            loads[w].wait()
            stores[w].start()
        for w in range(n):
            for p in range(1, N_DEV):
                peer, peer_id = _peer(x, y, c, p)
                src, _, arrival, _, _ = _copy_ends(gather, ins[w], lnd[w], me, peer_id)
                cp = pltpu.make_async_remote_copy(src, arrival, send_sems.at[w * PEERS + p - 1],
                                                  recv_sems.at[w * PEERS + p - 1], device_id=peer,
                                                  device_id_type=MESH_IDS)
                cp.wait_send()
                cp.wait_recv()
        for cp in stores:
            cp.wait()

    out = pl.pallas_call(
        body, name=name,
        out_shape=(*[pltpu.HBM(s.shape, s.dtype) for s in srcs], *[pltpu.HBM(l.shape, l.dtype) for l in lands]),
        in_specs=[HBM_SPEC] * (2 * n) + [SEM_SPEC, SEM_SPEC] + [pl.BlockSpec(memory_space=pl.ANY)] * len(after),
        out_specs=[HBM_SPEC] * (2 * n),
        input_output_aliases={i: i for i in range(2 * n)},
        scratch_shapes=[*[pltpu.VMEM(shape, s.dtype) for shape, s in zip(own_shapes, srcs)],
                        pltpu.SemaphoreType.DMA((n,)), pltpu.SemaphoreType.DMA((n,))],
        compiler_params=SPLIT_COPY,
    )(*srcs, *lands, send, recv, *after)
    return list(out[n:])


def _relay_to_sibling(started, name, after=None):
    send, recv, srcs, lands = started
    n = len(srcs)
    after = [] if after is None else [after]

    def body(*refs):
        ins, lnd = refs[:n], refs[n:2 * n]
        send_sems, recv_sems = refs[2 * n], refs[2 * n + 1]
        relay_send, relay_recv = refs[2 * n + 2 + len(after)], refs[2 * n + 3 + len(after)]
        x, y, c = _me()
        sibling, _ = _peer(x, y, c, SIBLING)
        for w in range(n):
            rows = ins[w].shape[0]
            for k, p in enumerate(SAME_CORE_PEERS):
                peer, peer_id = _peer(x, y, c, p)
                arrived = _row_block(lnd[w], peer_id, rows)
                first = pltpu.make_async_remote_copy(ins[w], arrived, send_sems.at[w * PEERS + p - 1],
                                                     recv_sems.at[w * PEERS + p - 1], device_id=peer,
                                                     device_id_type=MESH_IDS)
                first.wait_recv()
                pltpu.make_async_remote_copy(arrived, arrived, relay_send.at[w * RELAYS + k],
                                             relay_recv.at[w * RELAYS + k], device_id=sibling,
                                             device_id_type=MESH_IDS).start()
                first.wait_send()

    sems = pltpu.SemaphoreType.DMA((n * RELAYS,))
    out = pl.pallas_call(
        body, name=name,
        out_shape=(sems, sems, *[pltpu.HBM(s.shape, s.dtype) for s in srcs], *[pltpu.HBM(l.shape, l.dtype) for l in lands]),
        in_specs=[HBM_SPEC] * (2 * n) + [SEM_SPEC, SEM_SPEC] + [pl.BlockSpec(memory_space=pl.ANY)] * len(after),
        out_specs=(SEM_SPEC, SEM_SPEC, *[HBM_SPEC] * (2 * n)),
        input_output_aliases={i: 2 + i for i in range(2 * n)},
        compiler_params=SPLIT_COPY,
    )(*srcs, *lands, send, recv, *after)
    return send, recv, out[0], out[1], list(out[2:2 + n]), list(out[2 + n:])


def _relayed_wait(relayed, after, name):
    send, recv, relay_send, relay_recv, srcs, lands = relayed
    n = len(srcs)
    after = list(after) if isinstance(after, (list, tuple)) else [after]

    def body(*refs):
        ins, lnd = refs[:n], refs[n:2 * n]
        send_sems, recv_sems, relay_send_sems, relay_recv_sems = refs[2 * n:2 * n + 4]
        bounce, in_sems, out_sems = refs[-n - 2:-2], refs[-2], refs[-1]
        x, y, c = _me()
        me = 4 * x + 2 * y + c
        sibling, sibling_id = _peer(x, y, c, SIBLING)
        loads = [pltpu.make_async_copy(ins[w], bounce[w], in_sems.at[w]) for w in range(n)]
        stores = [pltpu.make_async_copy(bounce[w], _row_block(lnd[w], me, ins[w].shape[0]), out_sems.at[w])
                  for w in range(n)]
        for cp in loads:
            cp.start()
        for w in range(n):
            loads[w].wait()
            stores[w].start()
        for w in range(n):
            rows = ins[w].shape[0]
            direct = pltpu.make_async_remote_copy(ins[w], _row_block(lnd[w], sibling_id, rows),
                                                  send_sems.at[w * PEERS + SIBLING - 1],
                                                  recv_sems.at[w * PEERS + SIBLING - 1], device_id=sibling,
                                                  device_id_type=MESH_IDS)
            direct.wait_send()
            direct.wait_recv()
            for k, p in enumerate(SAME_CORE_PEERS):
                _, sent_id = _peer(x, y, c, p)
                _, got_id = _peer(x, y, c, p + SIBLING)
                relay = pltpu.make_async_remote_copy(_row_block(lnd[w], sent_id, rows), _row_block(lnd[w], got_id, rows),
                                                     relay_send_sems.at[w * RELAYS + k],
                                                     relay_recv_sems.at[w * RELAYS + k], device_id=sibling,
                                                     device_id_type=MESH_IDS)
                relay.wait_send()
                relay.wait_recv()
        for cp in stores:
            cp.wait()

    out = pl.pallas_call(
        body, name=name,
        out_shape=(*[pltpu.HBM(s.shape, s.dtype) for s in srcs], *[pltpu.HBM(l.shape, l.dtype) for l in lands]),
        in_specs=[HBM_SPEC] * (2 * n) + [SEM_SPEC] * 4 + [pl.BlockSpec(memory_space=pl.ANY)] * len(after),
        out_specs=[HBM_SPEC] * (2 * n),
        input_output_aliases={i: i for i in range(2 * n)},
        scratch_shapes=[*[pltpu.VMEM(s.shape, s.dtype) for s in srcs],
                        pltpu.SemaphoreType.DMA((n,)), pltpu.SemaphoreType.DMA((n,))],
        compiler_params=SPLIT_COPY,
    )(*srcs, *lands, send, recv, relay_send, relay_recv, *after)
    return list(out[n:])


def _adamw_update(w, g, m, v):
    nm = ADAM_B1 * m + (1.0 - ADAM_B1) * g
    nv = ADAM_B2 * v + (1.0 - ADAM_B2) * (g * g)
    m_hat = nm / (1.0 - ADAM_B1 ** ADAM_STEP)
    v_hat = nv / (1.0 - ADAM_B2 ** ADAM_STEP)
    return -ADAM_LR * (m_hat / (jnp.sqrt(v_hat) + ADAM_EPS) + ADAM_WD * w), nm, nv


SUM_ADAMW_COLS = 512


def _sum_adamw(parts, w, m, v, name):
    _, rows, d = parts.shape
    n = w.shape[0]
    tc = SUM_ADAMW_COLS

    def body(p_ref, w_ref, m_ref, v_ref, g_ref, d_ref, nm_ref, nv_ref):
        g = p_ref[0].astype(F32)
        for dev in range(1, N_DEV):
            g = g + p_ref[dev].astype(F32)
        g = g[:n]
        g_ref[...] = g
        d_ref[...], nm_ref[...], nv_ref[...] = _adamw_update(w_ref[...], g, m_ref[...], v_ref[...])

    spec = pl.BlockSpec((n, tc), lambda j: (0, j))
    shape = jax.ShapeDtypeStruct((n, d), F32)
    return pl.pallas_call(
        body, name=name, grid=(d // tc,),
        in_specs=[pl.BlockSpec((N_DEV, rows, tc), lambda j: (0, 0, j)), spec, spec, spec],
        out_specs=[spec] * 4, out_shape=[shape] * 4,
        compiler_params=_params("arbitrary"),
    )(parts, w, m, v)


def _pad_rows(a, rows):
    return jnp.pad(a, ((0, rows - a.shape[0]), (0, 0)))


def _row1(vec, width=D_MODEL):
    return jnp.pad(vec.reshape(1, -1), ((0, 0), (0, width - vec.shape[-1])))


COLUMN_SHARDED = ("ffn1_w_gate", "ffn1_w_up", "w_in", "ffn2_w_gate", "ffn2_w_up")
VEC_NAMES = ("ffn1_norm", "mix_norm", "ffn2_norm", "b_forget", "pool_scale", "q_norm", "k_norm", "out_norm_pool",
             "out_norm_attn")
VEC_ROWS = 16
LOSS_ROW = len(VEC_NAMES)


def _pack_vector_grads(parts, loss_part, name):
    names = [n for n in VEC_NAMES if n in parts]
    extra = [] if loss_part is None else [loss_part]

    def body(*refs):
        out_ref = refs[-1]
        out_ref[...] = jnp.zeros_like(out_ref)
        lane = lax.broadcasted_iota(jnp.int32, (1, LANES), 1)
        for n, ref in zip(names, refs):
            val = ref[...]
            if n in ("q_norm", "k_norm"):
                val = val[:, 0:LANES] + val[:, LANES:2 * LANES] + val[:, 2 * LANES:3 * LANES] + val[:, 3 * LANES:]
                val = jnp.where(lane < HEAD_DIM, val + pltpu.roll(val, HEAD_DIM, 1), 0.0)
            out_ref[pl.ds(VEC_NAMES.index(n), 1), pl.ds(0, val.shape[1])] = val
        if extra:
            out_ref[pl.ds(LOSS_ROW, 1), pl.ds(0, 1)] = refs[len(names)][...]

    vmem = pl.BlockSpec(memory_space=pltpu.VMEM)
    return pl.pallas_call(
        body, name=name, in_specs=[vmem] * (len(names) + len(extra)), out_specs=vmem,
        out_shape=jax.ShapeDtypeStruct((VEC_ROWS, D_MODEL), F32),
    )(*[parts[n] for n in names], *extra)


def _small_adamw(vec_all, pool_all, vec_params, pool_params):
    nv = len(vec_params)
    pool_rows = pool_params[0].shape[0]

    def body(*refs):
        vec_ref, pool_ref = refs[0], refs[1]
        ins = refs[2:2 + 3 * nv + 3]
        outs = refs[2 + 3 * nv + 3:-1]
        rows = refs[-1]
        total = vec_ref[pl.ds(0, VEC_ROWS), :]
        for dev in range(1, N_DEV):
            total = total + vec_ref[pl.ds(dev * VEC_ROWS, VEC_ROWS), :]
        rows[...] = total
        outs[4 * nv + 4][...] = rows[pl.ds(LOSS_ROW, 1), pl.ds(0, 1)]
        for i in range(nv):
            w_ref, m_ref, v_ref = ins[3 * i:3 * i + 3]
            g = rows[pl.ds(i, 1), pl.ds(0, w_ref.shape[1])]
            outs[4 * i][...] = g
            outs[4 * i + 1][...], outs[4 * i + 2][...], outs[4 * i + 3][...] = _adamw_update(
                w_ref[...], g, m_ref[...], v_ref[...])
        g = pool_ref[pl.ds(0, pool_rows), :].astype(F32)
        for dev in range(1, N_DEV):
            g = g + pool_ref[pl.ds(dev * pool_rows, pool_rows), :].astype(F32)
        w_ref, m_ref, v_ref = ins[3 * nv:]
        outs[4 * nv][...] = g
        outs[4 * nv + 1][...], outs[4 * nv + 2][...], outs[4 * nv + 3][...] = _adamw_update(
            w_ref[...], g, m_ref[...], v_ref[...])

    vmem = pl.BlockSpec(memory_space=pltpu.VMEM)
    flat = [a for trio in vec_params for a in trio] + list(pool_params)
    out_shape = []
    for trio in list(vec_params) + [pool_params]:
        out_shape += [jax.ShapeDtypeStruct(trio[0].shape, F32)] * 4
    out_shape.append(jax.ShapeDtypeStruct((1, 1), F32))
    return pl.pallas_call(
        body, name="adamw_small", in_specs=[vmem] * (2 + len(flat)), out_specs=[vmem] * len(out_shape),
        out_shape=out_shape, scratch_shapes=[pltpu.VMEM((VEC_ROWS, D_MODEL), F32)],
    )(vec_all, pool_all, *flat)


def kernel(x, ffn1_norm, ffn1_w_gate, ffn1_w_up, ffn1_w_down, mix_norm, w_in, b_forget, pool_w, pool_scale, q_norm, k_norm, out_norm_pool, out_norm_attn, w_out, ffn2_norm, ffn2_w_gate, ffn2_w_up, ffn2_w_down, loss_target, m_ffn1_norm, m_ffn1_w_gate, m_ffn1_w_up, m_ffn1_w_down, m_mix_norm, m_w_in, m_b_forget, m_pool_w, m_pool_scale, m_q_norm, m_k_norm, m_out_norm_pool, m_out_norm_attn, m_w_out, m_ffn2_norm, m_ffn2_w_gate, m_ffn2_w_up, m_ffn2_w_down, v_ffn1_norm, v_ffn1_w_gate, v_ffn1_w_up, v_ffn1_w_down, v_mix_norm, v_w_in, v_b_forget, v_pool_w, v_pool_scale, v_q_norm, v_k_norm, v_out_norm_pool, v_out_norm_attn, v_w_out, v_ffn2_norm, v_ffn2_w_gate, v_ffn2_w_up, v_ffn2_w_down):
    bsz, seq, d = x.shape
    t = bsz * seq
    x0 = x.reshape(t, d)
    target = loss_target.reshape(t, d)
    in_rows = -(-w_in.shape[1] // BF16_ROWS) * BF16_ROWS

    slabs = [s.astype(BF16) for s in (ffn1_w_gate.T, ffn1_w_up.T, ffn1_w_down, _pad_rows(w_in.T, in_rows), w_out,
                                       ffn2_w_gate.T, ffn2_w_up.T, ffn2_w_down)]
    first, started = _copies_start([slabs[0:2]], True, "gather_start_first", relayed=(0,))
    rest, started = _copies_start([slabs[2:3], slabs[3:4], slabs[4:5], slabs[5:8]], True, "gather_start", after=started,
                                  relayed=(3,))
    gathers = first + rest

    g1, gm, g2 = ffn1_norm.reshape(1, d), mix_norm.reshape(1, d), ffn2_norm.reshape(1, d)
    bf_row = _row1(b_forget, LANES)
    gq = jnp.tile(q_norm, N_HEADS).reshape(1, ATTN_WIDTH)
    gk = jnp.tile(k_norm, N_HEADS).reshape(1, ATTN_WIDTH)
    scale_row = pool_scale.reshape(1, POOL_WIDTH)
    gp, ga = out_norm_pool.reshape(1, POOL_WIDTH), out_norm_attn.reshape(1, ATTN_WIDTH)

    wg1, wu1 = _relayed_wait(_relay_to_sibling(gathers[0], "gather_relay_ffn1_up", started), started,
                             "gather_wait_ffn1_up")
    h1, sa1, sb1, s1 = _ffn_up(x0, g1, wg1, wu1, "ffn1_up")
    (wd1,) = _copies_wait(gathers[1], True, s1, "gather_wait_ffn1_down")
    (x1,) = _ffn_down(s1, wd1, x0, None, "ffn1_down")
    (win_g,) = _copies_wait(gathers[2], True, x1, "gather_wait_w_in")
    win_t = _repack_rows(win_g, in_rows, w_in.shape[1], N_DEV, "w_in_rows")
    hm, pv, q, k, v, f = _mix_in_fwd(x1, gm, win_t)
    pooled, mixed, y_pool = _pool_fwd(pv, pool_w, scale_row, gp, bsz, seq)
    qp, kp = _attn_prep_fwd(q, k, f, bf_row, gq, gk, bsz, seq)
    o, lse = _flash_fwd(qp, kp, v, bsz, seq)
    relayed_ffn2 = _relay_to_sibling(gathers[4], "gather_relay_ffn2", o)
    (wout,) = _copies_wait(gathers[3], True, [o, relayed_ffn2[4][0]], "gather_wait_w_out")
    ycat, x2 = _mix_out_fwd(o, y_pool, x1, ga, wout)
    wg2, wu2, wd2 = _relayed_wait(relayed_ffn2, x2, "gather_wait_ffn2")
    h2, sa2, sb2, s2 = _ffn_up(x2, g2, wg2, wu2, "ffn2_up")
    dx3, dyh2, loss_part = _ffn_down(s2, wd2, x2, target, "ffn2_down")

    da2, db2, dwg2, dwu2 = _ffn_bwd_act(dyh2, sa2, sb2, h2, wd2, dx3, "ffn2_bwd_act")
    (dwd2,) = _wgrad([s2], dyh2, da2, "ffn2_down_wgrad")
    (sent_ffn2,), tok = _copies_start([[dwg2, dwu2, dwd2]], False, "exchange_start_ffn2")
    dx2, dg2 = _ffn_bwd_dx(da2, db2, dx3, x2, g2, wg2, wu2, tok, "ffn2_bwd_dx")
    dwout, dy_pool, do, dga = _mix_out_bwd(dx2, o, ycat, ga, wout)
    (sent_out,), tok = _copies_start([[dwout]], False, "exchange_start_w_out")
    dqp, dkp, dv = _flash_bwd(qp, kp, v, o, do, lse, tok, bsz, seq)
    dq, dk, df, dgq, dgk, dbf = _attn_prep_bwd(dqp, dkp, q, k, f, bf_row, gq, gk, bsz, seq)
    dpv, dpool_w, dscale, dgp = _pool_bwd(dy_pool, mixed, pooled, pool_w, scale_row, gp, bsz, seq)
    dwin, dx1, dyh1, dgm = _mix_in_bwd(dpv, dq, dk, dv, df, hm, x1, dx2, gm, win_t)
    dwin_blocks = _repack_rows(dwin, w_in.shape[1], in_rows, N_DEV, "w_in_grad_blocks")
    (sent_in,), tok = _copies_start([[dwin_blocks]], False, "exchange_start_w_in")
    (dwd1,) = _wgrad([s1], dyh1, tok, "ffn1_down_wgrad")
    (sent_down1,), tok = _copies_start([[dwd1]], False, "exchange_start_ffn1_down", after=tok)
    da1, db1, dwg1, dwu1 = _ffn_bwd_act(dyh1, sa1, sb1, h1, wd1, tok, "ffn1_bwd_act")
    (sent_up1,), tok = _copies_start([[dwg1, dwu1]], False, "exchange_start_ffn1_up", after=tok)
    dx0, dg1 = _ffn_bwd_dx(da1, db1, dx1, x0, g1, wg1, wu1, tok, "ffn1_bwd_dx")

    pool_rows = POOL_GROUPS * POOL_GROUP_DIM
    packed = _pack_vector_grads(dict(ffn1_norm=dg1, mix_norm=dgm, ffn2_norm=dg2, b_forget=dbf, pool_scale=dscale,
                                     q_norm=dgq, k_norm=dgk, out_norm_pool=dgp, out_norm_attn=dga), loss_part,
                                "pack_vector_grads")
    pool_part = dpool_w.reshape(pool_rows, POOL_GROUP_DIM).astype(BF16)
    (sent_small,), tok = _copies_start([[packed, pool_part]], True, "small_grads_start")

    weights = dict(ffn1_norm=ffn1_norm, ffn1_w_gate=ffn1_w_gate, ffn1_w_up=ffn1_w_up, ffn1_w_down=ffn1_w_down,
                   mix_norm=mix_norm, w_in=w_in, b_forget=b_forget, pool_w=pool_w, pool_scale=pool_scale,
                   q_norm=q_norm, k_norm=k_norm, out_norm_pool=out_norm_pool, out_norm_attn=out_norm_attn,
                   w_out=w_out, ffn2_norm=ffn2_norm, ffn2_w_gate=ffn2_w_gate, ffn2_w_up=ffn2_w_up,
                   ffn2_w_down=ffn2_w_down)
    m_in = dict(ffn1_norm=m_ffn1_norm, ffn1_w_gate=m_ffn1_w_gate, ffn1_w_up=m_ffn1_w_up, ffn1_w_down=m_ffn1_w_down,
                mix_norm=m_mix_norm, w_in=m_w_in, b_forget=m_b_forget, pool_w=m_pool_w, pool_scale=m_pool_scale,
                q_norm=m_q_norm, k_norm=m_k_norm, out_norm_pool=m_out_norm_pool, out_norm_attn=m_out_norm_attn,
                w_out=m_w_out, ffn2_norm=m_ffn2_norm, ffn2_w_gate=m_ffn2_w_gate, ffn2_w_up=m_ffn2_w_up,
                ffn2_w_down=m_ffn2_w_down)
    v_in = dict(ffn1_norm=v_ffn1_norm, ffn1_w_gate=v_ffn1_w_gate, ffn1_w_up=v_ffn1_w_up, ffn1_w_down=v_ffn1_w_down,
                mix_norm=v_mix_norm, w_in=v_w_in, b_forget=v_b_forget, pool_w=v_pool_w, pool_scale=v_pool_scale,
                q_norm=v_q_norm, k_norm=v_k_norm, out_norm_pool=v_out_norm_pool, out_norm_attn=v_out_norm_attn,
                w_out=v_w_out, ffn2_norm=v_ffn2_norm, ffn2_w_gate=v_ffn2_w_gate, ffn2_w_up=v_ffn2_w_up,
                ffn2_w_down=v_ffn2_w_down)
    grads, delta, new_m, new_v = {}, {}, {}, {}
    after = [tok]
    plan = ((sent_ffn2, "ffn2", ("ffn2_w_gate", "ffn2_w_up", "ffn2_w_down")), (sent_out, "w_out", ("w_out",)),
            (sent_in, "w_in", ("w_in",)), (sent_down1, "ffn1_down", ("ffn1_w_down",)),
            (sent_up1, "ffn1_up", ("ffn1_w_gate", "ffn1_w_up")))
    for sent, tag, names in plan:
        parts = _copies_wait(sent, False, after, f"exchange_wait_{tag}")
        after = []
        for n, part in zip(names, parts):
            turn = (lambda a: a.T) if n in COLUMN_SHARDED else (lambda a: a)
            done = _sum_adamw(part, turn(weights[n]), turn(m_in[n]), turn(v_in[n]), f"adamw_{n}")
            grads[n], delta[n], new_m[n], new_v[n] = (turn(a) for a in done)
            after.append(done[3])
    vec_all, pool_all = _copies_wait(sent_small, True, after, "small_grads_wait")
    as_row = lambda a: a.reshape(1, -1)
    as_pool = lambda a: a.reshape(pool_rows, POOL_GROUP_DIM)
    small = _small_adamw(vec_all, pool_all,
                         [tuple(as_row(z[n]) for z in (weights, m_in, v_in)) for n in VEC_NAMES],
                         tuple(as_pool(z["pool_w"]) for z in (weights, m_in, v_in)))
    for i, n in enumerate(VEC_NAMES + ("pool_w",)):
        grads[n], delta[n], new_m[n], new_v[n] = (a.reshape(weights[n].shape) for a in small[4 * i:4 * i + 4])
    loss = small[-1].reshape(())

    order = ("ffn1_norm", "ffn1_w_gate", "ffn1_w_up", "ffn1_w_down", "mix_norm", "w_in", "b_forget", "pool_w",
             "pool_scale", "q_norm", "k_norm", "out_norm_pool", "out_norm_attn", "w_out", "ffn2_norm", "ffn2_w_gate",
             "ffn2_w_up", "ffn2_w_down")
    return (loss, dx0.reshape(bsz, seq, d), *[grads[n] for n in order], *[delta[n] for n in order],
            *[new_m[n] for n in order], *[new_v[n] for n in order])
```

```python
import jax
import jax.numpy as jnp
from jax import lax
from jax.experimental import pallas as pl
from jax.experimental.pallas import tpu as pltpu

F32 = jnp.float32
BF16 = jnp.bfloat16

EPS = 1e-6
D_MODEL = 1024
N_HEADS = 8
HEAD_DIM = 64
POOL_WIDTH = 512
ATTN_WIDTH = 512
POOL_GROUPS = 4
POOL_GROUP_DIM = 128
POOL_WINDOWS = (2, 4, 8, 16)
POOL_HALO = 16
MIX_PAD = POOL_WIDTH + 3 * ATTN_WIDTH + 128
N_DEV = 8
BF16_ROWS = 16
LANES = 128
VMEM_LIMIT = 56 * 1024 * 1024

ADAM_LR = 0.001
ADAM_B1 = 0.9
ADAM_B2 = 0.999
ADAM_EPS = 1e-08
ADAM_WD = 0.01
ADAM_STEP = 10


def _params(*sem):
    return pltpu.CompilerParams(dimension_semantics=sem, vmem_limit_bytes=VMEM_LIMIT)


def _dot(a, b):
    return jnp.dot(a, b, preferred_element_type=F32)


def _dot_nt(a, b):
    return lax.dot_general(a, b, (((1,), (1,)), ((), ())), preferred_element_type=F32)


def _dot_tn(a, b):
    return lax.dot_general(a, b, (((0,), (0,)), ((), ())), preferred_element_type=F32)


def _resident(shape):
    return pl.BlockSpec(shape, lambda *_: (0,) * len(shape), pipeline_mode=pl.Buffered(1))


def _rows(tm, width):
    return pl.BlockSpec((tm, width), lambda i: (i, 0))


ORDER_ONLY = pl.BlockSpec(memory_space=pl.ANY)


def _rms_scale(x):
    return lax.rsqrt(jnp.mean(x * x, axis=-1, keepdims=True) + EPS)


def _rms_bwd(dh, x, gain):
    r = _rms_scale(x)
    n = x * r
    dgain = jnp.sum(dh * n, axis=0, keepdims=True)
    dn = dh * gain
    dx = r * (dn - n * jnp.mean(dn * n, axis=-1, keepdims=True))
    return dx, dgain


def _split3(x):
    hi = x.astype(BF16)
    r1 = x - hi.astype(F32)
    mid = r1.astype(BF16)
    lo = (r1 - mid.astype(F32)).astype(BF16)
    return hi, mid, lo


FF_CHUNK = 256


def _swiglu_parts(a, b):
    sig = jax.nn.sigmoid(a)
    silu = a * sig
    return (b * (sig + silu * (1.0 - sig))).astype(BF16), silu.astype(BF16), (silu * b).astype(BF16)


def _ffn_up(x, gain, wg_t, wu_t, name):
    t, d = x.shape
    f = wg_t.shape[0]
    tm = 512

    def body(x_ref, g_ref, wg_ref, wu_ref, h_ref, sa_ref, sb_ref, s_ref):
        xv = x_ref[...]
        h = (xv * _rms_scale(xv) * g_ref[...]).astype(BF16)
        h_ref[...] = h
        for c in range(f // FF_CHUNK):
            sl = pl.ds(c * FF_CHUNK, FF_CHUNK)
            sa_ref[:, sl], sb_ref[:, sl], s_ref[:, sl] = _swiglu_parts(_dot_nt(h, wg_ref[sl, :]), _dot_nt(h, wu_ref[sl, :]))

    wide = jax.ShapeDtypeStruct((t, f), BF16)
    return pl.pallas_call(
        body, name=name, grid=(t // tm,),
        in_specs=[_rows(tm, d), _resident((1, d)), _resident((f, d)), _resident((f, d))],
        out_specs=[_rows(tm, d), _rows(tm, f), _rows(tm, f), _rows(tm, f)],
        out_shape=[jax.ShapeDtypeStruct((t, d), BF16), wide, wide, wide],
        compiler_params=_params("arbitrary"),
    )(x, gain, wg_t, wu_t)


def _norm_bf16(x, gain, after, name):
    t, d = x.shape
    tm = 512

    def body(x_ref, g_ref, after_ref, h_ref):
        xv = x_ref[...]
        h_ref[...] = (xv * _rms_scale(xv) * g_ref[...]).astype(BF16)

    return pl.pallas_call(
        body, name=name, grid=(t // tm,), in_specs=[_rows(tm, d), _resident((1, d)), ORDER_ONLY],
        out_specs=_rows(tm, d), out_shape=jax.ShapeDtypeStruct((t, d), BF16), compiler_params=_params("arbitrary"),
    )(x, gain, after)


def _ffn_up_normed(h, wg_t, wu_t, name):
    t, d = h.shape
    f = wg_t.shape[0]
    tm = 512

    def body(h_ref, wg_ref, wu_ref, sa_ref, sb_ref, s_ref):
        h = h_ref[...]
        for c in range(f // FF_CHUNK):
            sl = pl.ds(c * FF_CHUNK, FF_CHUNK)
            sa_ref[:, sl], sb_ref[:, sl], s_ref[:, sl] = _swiglu_parts(_dot_nt(h, wg_ref[sl, :]), _dot_nt(h, wu_ref[sl, :]))

    wide = jax.ShapeDtypeStruct((t, f), BF16)
    return pl.pallas_call(
        body, name=name, grid=(t // tm,),
        in_specs=[_rows(tm, d), _resident((f, d)), _resident((f, d))],
        out_specs=[_rows(tm, f)] * 3, out_shape=[wide] * 3,
        compiler_params=_params("arbitrary"),
    )(h, wg_t, wu_t)


def _ffn_down(s, wd, x, target, name):
    t, d = x.shape
    f = wd.shape[0]
    tm = 512
    with_loss = target is not None

    def body(*refs):
        if with_loss:
            s_ref, w_ref, x_ref, t_ref, dy_ref, dyh_ref, loss_ref = refs
        else:
            s_ref, w_ref, x_ref, y_ref = refs
        y = x_ref[...] + 0.5 * _dot(s_ref[...], w_ref[...])
        if with_loss:
            e = y - t_ref[...]
            dy = e * (1.0 / d)
            dy_ref[...] = dy
            dyh_ref[...] = (0.5 * dy).astype(BF16)

            @pl.when(pl.program_id(0) == 0)
            def _():
                loss_ref[...] = jnp.zeros_like(loss_ref)

            part = jnp.sum(jnp.sum(e * e, axis=0, keepdims=True), axis=1, keepdims=True)
            loss_ref[...] += part * (0.5 / d)
        else:
            y_ref[...] = y

    in_specs = [_rows(tm, f), _resident((f, d)), _rows(tm, d)]
    args = [s, wd, x]
    if with_loss:
        in_specs.append(_rows(tm, d))
        args.append(target)
        out_shape = [jax.ShapeDtypeStruct((t, d), F32), jax.ShapeDtypeStruct((t, d), BF16),
                     jax.ShapeDtypeStruct((1, 1), F32)]
        out_specs = [_rows(tm, d), _rows(tm, d), pl.BlockSpec((1, 1), lambda i: (0, 0))]
    else:
        out_shape = [jax.ShapeDtypeStruct((t, d), F32)]
        out_specs = [_rows(tm, d)]
    return pl.pallas_call(
        body, name=name, grid=(t // tm,), in_specs=in_specs, out_specs=out_specs, out_shape=out_shape,
        compiler_params=_params("arbitrary"),
    )(*args)


def _ffn_bwd_act(dyh, sa, sb, h, wd, after, name):
    t, d = dyh.shape
    f = wd.shape[0]
    tn = f // 2
    tk = 512
    nk = t // tk

    def body(dy_ref, sa_ref, sb_ref, h_ref, wd_ref, after_ref, da_ref, db_ref, dwg_ref, dwu_ref, acc_g, acc_u):
        k = pl.program_id(1)

        @pl.when(k == 0)
        def _():
            acc_g[...] = jnp.zeros_like(acc_g)
            acc_u[...] = jnp.zeros_like(acc_u)

        ds = _dot_nt(dy_ref[...], wd_ref[...])
        da = (ds * sa_ref[...].astype(F32)).astype(BF16)
        db = (ds * sb_ref[...].astype(F32)).astype(BF16)
        da_ref[...] = da
        db_ref[...] = db
        hv = h_ref[...]
        acc_g[...] += _dot_tn(da, hv)
        acc_u[...] += _dot_tn(db, hv)

        @pl.when(k == nk - 1)
        def _():
            dwg_ref[...] = acc_g[...].astype(BF16)
            dwu_ref[...] = acc_u[...].astype(BF16)

    tokens = pl.BlockSpec((tk, d), lambda j, k: (k, 0))
    wide = pl.BlockSpec((tk, tn), lambda j, k: (k, j))
    weight = pl.BlockSpec((tn, d), lambda j, k: (j, 0))
    return pl.pallas_call(
        body, name=name, grid=(f // tn, nk),
        in_specs=[tokens, wide, wide, tokens, weight, ORDER_ONLY],
        out_specs=[wide, wide, weight, weight],
        out_shape=[jax.ShapeDtypeStruct((t, f), BF16)] * 2 + [jax.ShapeDtypeStruct((f, d), BF16)] * 2,
        scratch_shapes=[pltpu.VMEM((tn, d), F32)] * 2,
        compiler_params=_params("arbitrary", "arbitrary"),
    )(dyh, sa, sb, h, wd, after)


def _ffn_bwd_dx(da, db, dy, x, gain, wg_t, wu_t, after, name):
    t, d = x.shape
    f = wg_t.shape[0]
    tm = 512

    def body(da_ref, db_ref, dy_ref, x_ref, g_ref, wg_ref, wu_ref, after_ref, dx_ref, dg_ref):
        dh = _dot(da_ref[...], wg_ref[...]) + _dot(db_ref[...], wu_ref[...])
        dx, dgain = _rms_bwd(dh, x_ref[...], g_ref[...])
        dx_ref[...] = dy_ref[...] + dx

        @pl.when(pl.program_id(0) == 0)
        def _():
            dg_ref[...] = jnp.zeros_like(dg_ref)

        dg_ref[...] += dgain

    return pl.pallas_call(
        body, name=name, grid=(t // tm,),
        in_specs=[_rows(tm, f), _rows(tm, f), _rows(tm, d), _rows(tm, d), _resident((1, d)), _resident((f, d)),
                  _resident((f, d)), ORDER_ONLY],
        out_specs=[_rows(tm, d), pl.BlockSpec((1, d), lambda i: (0, 0))],
        out_shape=[jax.ShapeDtypeStruct((t, d), F32), jax.ShapeDtypeStruct((1, d), F32)],
        compiler_params=_params("arbitrary"),
    )(da, db, dy, x, gain, wg_t, wu_t, after)


def _wgrad(lhs, b, after, name):
    t, n = lhs[0].shape
    d = b.shape[1]
    m = len(lhs)
    tn = n // 2 if n * d * m > (4 << 20) else n
    tk = 1024
    nk = t // tk

    def body(*refs):
        a_refs, b_ref, o_refs, accs = refs[:m], refs[m], refs[m + 2:2 * m + 2], refs[2 * m + 2:]
        k = pl.program_id(1)

        @pl.when(k == 0)
        def _():
            for acc in accs:
                acc[...] = jnp.zeros_like(acc)

        bv = b_ref[...]
        for a_ref, acc in zip(a_refs, accs):
            acc[...] += _dot_tn(a_ref[...], bv)

        @pl.when(k == nk - 1)
        def _():
            for o_ref, acc in zip(o_refs, accs):
                o_ref[...] = acc[...].astype(BF16)

    return pl.pallas_call(
        body, name=name, grid=(n // tn, nk),
        in_specs=[pl.BlockSpec((tk, tn), lambda j, k: (k, j))] * m + [pl.BlockSpec((tk, d), lambda j, k: (k, 0)),
                                                                       ORDER_ONLY],
        out_specs=[pl.BlockSpec((tn, d), lambda j, k: (j, 0))] * m,
        out_shape=[jax.ShapeDtypeStruct((n, d), BF16)] * m,
        scratch_shapes=[pltpu.VMEM((tn, d), F32)] * m,
        compiler_params=_params("arbitrary", "arbitrary"),
    )(*lhs, b, after)


def _repack_rows(a, rows_in, rows_out, blocks, name):
    total, d = a.shape
    real = min(rows_in, rows_out)

    def body(a_ref, o_ref, wide_in, wide_out):
        wide_in[...] = a_ref[...].astype(F32)
        wide_out[...] = jnp.zeros_like(wide_out)
        for j in range(blocks):
            wide_out[pl.ds(j * rows_out, real), :] = wide_in[pl.ds(j * rows_in, real), :]
        o_ref[...] = wide_out[...].astype(BF16)

    full = pl.BlockSpec((total, d), lambda i: (0, 0))
    return pl.pallas_call(
        body, name=name, grid=(1,), in_specs=[full], out_specs=full, out_shape=jax.ShapeDtypeStruct((total, d), BF16),
        scratch_shapes=[pltpu.VMEM((total, d), F32)] * 2,
        compiler_params=_params("arbitrary"),
    )(a)


def _mix_in_fwd(x, gain, w_in_t):
    t, d = x.shape
    tm = 1024
    pw, aw = POOL_WIDTH, ATTN_WIDTH

    def body(x_ref, g_ref, w_ref, hm_ref, pv_ref, q_ref, k_ref, v_ref, f_ref):
        xv = x_ref[...]
        hm = (xv * _rms_scale(xv) * g_ref[...]).astype(BF16)
        hm_ref[...] = hm
        pv_ref[...] = _dot_nt(hm, w_ref[pl.ds(0, pw), :])
        q_ref[...] = _dot_nt(hm, w_ref[pl.ds(pw, aw), :])
        k_ref[...] = _dot_nt(hm, w_ref[pl.ds(pw + aw, aw), :])
        v_ref[...] = _dot_nt(hm, w_ref[pl.ds(pw + 2 * aw, aw), :]).astype(BF16)
        f_ref[...] = _dot_nt(hm, w_ref[pl.ds(pw + 3 * aw, LANES), :])

    return pl.pallas_call(
        body, name="mix_in_fwd", grid=(t // tm,),
        in_specs=[_rows(tm, d), _resident((1, d)), _resident((MIX_PAD, d))],
        out_specs=[_rows(tm, d), _rows(tm, pw), _rows(tm, aw), _rows(tm, aw), _rows(tm, aw), _rows(tm, LANES)],
        out_shape=[jax.ShapeDtypeStruct((t, d), BF16), jax.ShapeDtypeStruct((t, pw), F32),
                   jax.ShapeDtypeStruct((t, aw), F32), jax.ShapeDtypeStruct((t, aw), F32),
                   jax.ShapeDtypeStruct((t, aw), BF16), jax.ShapeDtypeStruct((t, LANES), F32)],
        compiler_params=_params("arbitrary"),
    )(x, gain, w_in_t)


def _pool_fwd(pv, pool_w, pool_scale, gain, bsz, seq):
    ts = 512
    ns = seq // ts
    pw = POOL_WIDTH

    def body(pv_ref, w_ref, sc_ref, g_ref, pooled_ref, mixed_ref, y_ref, ext):
        s = pl.program_id(1)

        @pl.when(s == 0)
        def _():
            ext[pl.ds(0, POOL_HALO), :] = jnp.zeros((POOL_HALO, pw), F32)

        p = pv_ref[...]
        ext[pl.ds(POOL_HALO, ts), :] = p
        pos = s * ts + lax.broadcasted_iota(jnp.int32, (ts, 1), 0)
        parts = []
        for g, w in enumerate(POOL_WINDOWS):
            lanes = pl.ds(g * POOL_GROUP_DIM, POOL_GROUP_DIM)
            win = ext[pl.ds(POOL_HALO, ts), lanes]
            for i in range(1, w):
                win = win + ext[pl.ds(POOL_HALO - i, ts), lanes]
            cnt = jnp.minimum(pos + 1, w).astype(F32)
            pooled = (win / cnt - ext[pl.ds(POOL_HALO, ts), lanes]).astype(BF16)
            pooled_ref[:, lanes] = pooled
            parts.append(_dot(pooled, w_ref[g].astype(BF16)))
        mixed = jnp.concatenate(parts, axis=1)
        mixed_ref[...] = mixed
        pm = mixed * sc_ref[...]
        y_ref[...] = (pm * _rms_scale(pm) * g_ref[...]).astype(BF16)
        ext[pl.ds(0, POOL_HALO), :] = p[ts - POOL_HALO:, :]

    blk = pl.BlockSpec((ts, pw), lambda b, s: (b * ns + s, 0))
    t = bsz * seq
    return pl.pallas_call(
        body, name="pool_fwd", grid=(bsz, ns),
        in_specs=[blk, pl.BlockSpec((POOL_GROUPS, POOL_GROUP_DIM, POOL_GROUP_DIM), lambda b, s: (0, 0, 0)),
                  pl.BlockSpec((1, pw), lambda b, s: (0, 0)), pl.BlockSpec((1, pw), lambda b, s: (0, 0))],
        out_specs=[blk, blk, blk],
        out_shape=[jax.ShapeDtypeStruct((t, pw), BF16), jax.ShapeDtypeStruct((t, pw), F32),
                   jax.ShapeDtypeStruct((t, pw), BF16)],
        scratch_shapes=[pltpu.VMEM((POOL_HALO + ts, pw), F32)],
        compiler_params=_params("arbitrary", "arbitrary"),
    )(pv, pool_w, pool_scale, gain)


def _pool_bwd(dy, mixed, pooled, pool_w, pool_scale, gain, bsz, seq):
    ts = 512
    ns = seq // ts
    pw = POOL_WIDTH

    def body(dy_ref, mixed_ref, pooled_ref, w_ref, sc_ref, g_ref, dpv_ref, dw_ref, dsc_ref, dg_ref, ext):
        b = pl.program_id(0)
        sr = pl.program_id(1)
        s = ns - 1 - sr

        @pl.when(jnp.logical_and(b == 0, sr == 0))
        def _():
            dw_ref[...] = jnp.zeros_like(dw_ref)
            dsc_ref[...] = jnp.zeros_like(dsc_ref)
            dg_ref[...] = jnp.zeros_like(dg_ref)

        @pl.when(sr == 0)
        def _():
            ext[pl.ds(ts, POOL_HALO), :] = jnp.zeros((POOL_HALO, pw), F32)

        mixed = mixed_ref[...]
        sc = sc_ref[...]
        dpm, dgain = _rms_bwd(dy_ref[...], mixed * sc, g_ref[...])
        dg_ref[...] += dgain
        dsc_ref[...] += jnp.sum(dpm * mixed, axis=0, keepdims=True)
        dmixed = (dpm * sc).astype(BF16)
        pos = s * ts + lax.broadcasted_iota(jnp.int32, (ts, 1), 0)
        dpooled = []
        for g, w in enumerate(POOL_WINDOWS):
            lanes = pl.ds(g * POOL_GROUP_DIM, POOL_GROUP_DIM)
            dm = dmixed[:, g * POOL_GROUP_DIM:(g + 1) * POOL_GROUP_DIM]
            dw_ref[g] += _dot_tn(pooled_ref[:, lanes], dm)
            dp = _dot_nt(dm, w_ref[g].astype(BF16))
            dpooled.append(dp)
            cnt = jnp.minimum(pos + 1, w).astype(F32)
            ext[pl.ds(0, ts), lanes] = dp / cnt
        for g, w in enumerate(POOL_WINDOWS):
            lanes = pl.ds(g * POOL_GROUP_DIM, POOL_GROUP_DIM)
            win = ext[pl.ds(0, ts), lanes]
            for i in range(1, w):
                win = win + ext[pl.ds(i, ts), lanes]
            dpv_ref[:, lanes] = (win - dpooled[g]).astype(BF16)
        head = ext[pl.ds(0, POOL_HALO), :]
        ext[pl.ds(ts, POOL_HALO), :] = head

    blk = pl.BlockSpec((ts, pw), lambda b, s: (b * ns + (ns - 1 - s), 0))
    vec = pl.BlockSpec((1, pw), lambda b, s: (0, 0))
    wspec = pl.BlockSpec((POOL_GROUPS, POOL_GROUP_DIM, POOL_GROUP_DIM), lambda b, s: (0, 0, 0))
    t = bsz * seq
    return pl.pallas_call(
        body, name="pool_bwd", grid=(bsz, ns),
        in_specs=[blk, blk, blk, wspec, vec, vec],
        out_specs=[blk, wspec, vec, vec],
        out_shape=[jax.ShapeDtypeStruct((t, pw), BF16),
                   jax.ShapeDtypeStruct((POOL_GROUPS, POOL_GROUP_DIM, POOL_GROUP_DIM), F32),
                   jax.ShapeDtypeStruct((1, pw), F32), jax.ShapeDtypeStruct((1, pw), F32)],
        scratch_shapes=[pltpu.VMEM((ts + POOL_HALO, pw), F32)],
        compiler_params=_params("arbitrary", "arbitrary"),
    )(dy, mixed, pooled, pool_w, pool_scale, gain)


AUX_ONE = 64
AUX_F = 67

ATTN_PREP_ROWS = 512


def _seg_ones(width, seg):
    r = lax.broadcasted_iota(jnp.int32, (width, width), 0) // seg
    c = lax.broadcasted_iota(jnp.int32, (width, width), 1) // seg
    return (r == c).astype(BF16)


def _tri_ones(n, lower):
    r = lax.broadcasted_iota(jnp.int32, (n, n), 0)
    c = lax.broadcasted_iota(jnp.int32, (n, n), 1)
    return ((r >= c) if lower else (r <= c)).astype(BF16)


def _place_pieces(first_lane):
    r = lax.broadcasted_iota(jnp.int32, (3 * LANES, N_HEADS * LANES), 0)
    c = lax.broadcasted_iota(jnp.int32, (3 * LANES, N_HEADS * LANES), 1)
    piece, head = r // LANES, r % LANES
    return jnp.logical_and(head < N_HEADS, c == head * LANES + first_lane + piece).astype(BF16)


def _head_sums(x, seg_ones):
    return _dot(x.astype(BF16), seg_ones)


def _log_sigmoid(x):
    return jnp.minimum(x, 0.0) - jnp.log(1.0 + jnp.exp(-jnp.abs(x)))


def _attn_prep_fwd(q, k, f, b_forget, q_gain, k_gain, bsz, seq):
    ts = ATTN_PREP_ROWS
    ns = seq // ts
    aw = ATTN_WIDTH
    t = bsz * seq
    seg = _seg_ones(aw, HEAD_DIM)
    tri = _tri_ones(ts, True)

    def body(q_ref, k_ref, f_ref, bf_ref, gq_ref, gk_ref, seg_ref, tri_ref, place_ref, qp_ref, kp_ref, carry):
        s = pl.program_id(1)

        @pl.when(s == 0)
        def _():
            carry[...] = jnp.zeros_like(carry)

        logf = _log_sigmoid(f_ref[...] + bf_ref[...])
        hi, mid, lo = _split3(logf)
        tri_v = tri_ref[...]
        fc = _dot(tri_v, hi) + _dot(tri_v, mid) + _dot(tri_v, lo) + carry[pl.ds(0, 1), :]
        carry[pl.ds(0, 1), :] = fc[ts - 1:, :]
        pcs = jnp.concatenate(_split3(fc), axis=1)
        lane = lax.broadcasted_iota(jnp.int32, (1, LANES), 1)
        ones_q = jnp.logical_and(lane >= AUX_ONE, lane < AUX_ONE + 3).astype(F32)
        ones_k = jnp.logical_and(lane >= AUX_F, lane < AUX_F + 3).astype(F32)
        seg_v = seg_ref[...]
        placed = _dot(pcs, place_ref[...])

        def build(x_ref, g_ref, scale, out_ref, ones, for_keys):
            xv = x_ref[...]
            r = lax.rsqrt(_head_sums(xv * xv, seg_v) * (1.0 / HEAD_DIM) + EPS)
            xn = xv * r * g_ref[...] * scale
            for h in range(N_HEADS):
                pair = xn[:, (h // 2) * LANES:(h // 2 + 1) * LANES]
                feat = pair if h % 2 == 0 else pltpu.roll(pair, HEAD_DIM, 1)
                aux_h = placed[:, h * LANES:(h + 1) * LANES]
                if for_keys:
                    aux_h = -pltpu.roll(aux_h, LANES - (AUX_F - AUX_ONE), 1)
                out_ref[:, h * LANES:(h + 1) * LANES] = jnp.where(lane < HEAD_DIM, feat, aux_h + ones).astype(BF16)

        build(q_ref, gq_ref, 0.125, qp_ref, ones_q, False)
        build(k_ref, gk_ref, 1.0, kp_ref, ones_k, True)

    blk = pl.BlockSpec((ts, aw), lambda b, s: (b * ns + s, 0))
    fblk = pl.BlockSpec((ts, LANES), lambda b, s: (b * ns + s, 0))
    oblk = pl.BlockSpec((ts, N_HEADS * LANES), lambda b, s: (b * ns + s, 0))
    const = lambda shape: pl.BlockSpec(shape, lambda b, s: (0, 0))
    return pl.pallas_call(
        body, name="attn_prep_fwd", grid=(bsz, ns),
        in_specs=[blk, blk, fblk, const((1, LANES)), const((1, aw)), const((1, aw)), const((aw, aw)), const((ts, ts)),
                  const((3 * LANES, N_HEADS * LANES))],
        out_specs=[oblk, oblk],
        out_shape=[jax.ShapeDtypeStruct((t, N_HEADS * LANES), BF16)] * 2,
        scratch_shapes=[pltpu.VMEM((8, LANES), F32)],
        compiler_params=_params("arbitrary", "arbitrary"),
    )(q, k, f, b_forget, q_gain, k_gain, seg, tri, _place_pieces(AUX_F))


def _attn_prep_bwd(dqp, dkp, q, k, f, b_forget, q_gain, k_gain, bsz, seq):
    ts = ATTN_PREP_ROWS
    ns = seq // ts
    aw = ATTN_WIDTH
    t = bsz * seq
    seg = _seg_ones(aw, HEAD_DIM)
    tri = _tri_ones(ts, False)

    def body(dqp_ref, dkp_ref, q_ref, k_ref, f_ref, bf_ref, gq_ref, gk_ref, seg_ref, tri_ref,
             dq_ref, dk_ref, df_ref, dgq_ref, dgk_ref, dbf_ref, carry):
        b = pl.program_id(0)
        sr = pl.program_id(1)

        @pl.when(jnp.logical_and(b == 0, sr == 0))
        def _():
            dgq_ref[...] = jnp.zeros_like(dgq_ref)
            dgk_ref[...] = jnp.zeros_like(dgk_ref)
            dbf_ref[...] = jnp.zeros_like(dbf_ref)

        @pl.when(sr == 0)
        def _():
            carry[...] = jnp.zeros_like(carry)

        lane = lax.broadcasted_iota(jnp.int32, (1, LANES), 1)
        seg_v = seg_ref[...]

        def norm_bwd(dp_ref, x_ref, g_ref, scale, dx_ref, dgain_ref):
            parts = []
            for j in range(N_HEADS // 2):
                even = dp_ref[:, (2 * j) * LANES:(2 * j + 1) * LANES]
                odd = dp_ref[:, (2 * j + 1) * LANES:(2 * j + 2) * LANES]
                parts.append(jnp.where(lane < HEAD_DIM, even, pltpu.roll(odd, HEAD_DIM, 1)))
            dxn = jnp.concatenate(parts, axis=1) * scale
            xv = x_ref[...]
            r = lax.rsqrt(_head_sums(xv * xv, seg_v) * (1.0 / HEAD_DIM) + EPS)
            n = xv * r
            dgain_ref[...] += jnp.sum(dxn * n, axis=0, keepdims=True)
            dn = dxn * g_ref[...]
            m = _head_sums(dn * n, seg_v) * (1.0 / HEAD_DIM)
            dx_ref[...] = (r * (dn - n * m)).astype(BF16)

        norm_bwd(dqp_ref, q_ref, gq_ref, 0.125, dq_ref, dgq_ref)
        norm_bwd(dkp_ref, k_ref, gk_ref, 1.0, dk_ref, dgk_ref)

        dfc = jnp.zeros((ts, LANES), F32)
        for h in range(N_HEADS):
            cols = pl.ds(h * LANES, LANES)
            both = jnp.where(lane == AUX_F, dqp_ref[:, cols], 0.0) - jnp.where(lane == AUX_ONE, dkp_ref[:, cols], 0.0)
            dfc = jnp.where(lane == h, jnp.sum(both, axis=1, keepdims=True), dfc)
        hi, mid, lo = _split3(dfc)
        tri_v = tri_ref[...]
        dlogf = _dot(tri_v, hi) + _dot(tri_v, mid) + _dot(tri_v, lo) + carry[pl.ds(0, 1), :]
        carry[pl.ds(0, 1), :] = dlogf[0:1, :]
        df = jnp.where(lane < N_HEADS, dlogf * jax.nn.sigmoid(-(f_ref[...] + bf_ref[...])), 0.0)
        df_ref[...] = df.astype(BF16)
        dbf_ref[...] += jnp.sum(df, axis=0, keepdims=True)

    rev = lambda b, s: (b * ns + (ns - 1 - s), 0)
    blk = pl.BlockSpec((ts, aw), rev)
    fblk = pl.BlockSpec((ts, LANES), rev)
    pblk = pl.BlockSpec((ts, N_HEADS * LANES), rev)
    const = lambda shape: pl.BlockSpec(shape, lambda b, s: (0, 0))
    return pl.pallas_call(
        body, name="attn_prep_bwd", grid=(bsz, ns),
        in_specs=[pblk, pblk, blk, blk, fblk, const((1, LANES)), const((1, aw)), const((1, aw)), const((aw, aw)),
                  const((ts, ts))],
        out_specs=[blk, blk, fblk, const((1, aw)), const((1, aw)), const((1, LANES))],
        out_shape=[jax.ShapeDtypeStruct((t, aw), BF16), jax.ShapeDtypeStruct((t, aw), BF16),
                   jax.ShapeDtypeStruct((t, LANES), BF16), jax.ShapeDtypeStruct((1, aw), F32),
                   jax.ShapeDtypeStruct((1, aw), F32), jax.ShapeDtypeStruct((1, LANES), F32)],
        scratch_shapes=[pltpu.VMEM((8, LANES), F32)],
        compiler_params=_params("arbitrary", "arbitrary"),
    )(dqp, dkp, q, k, f, b_forget, q_gain, k_gain, seg, tri)


ATTN_BLOCK = 1024
HEAD_PAIRS = N_HEADS // 2


def _flash_fwd(qp, kp, v, bsz, seq):
    tq = ATTN_BLOCK
    half = tq // 2
    nq = seq // tq
    t = bsz * seq

    def body(q_ref, k_ref, v_ref, o_ref, lse_ref, m_sc, l_sc, acc_sc):
        i = pl.program_id(2)
        m_sc[...] = jnp.full(m_sc.shape, -jnp.inf, F32)
        l_sc[...] = jnp.zeros_like(l_sc)
        acc_sc[...] = jnp.zeros_like(acc_sc)
        lane = lax.broadcasted_iota(jnp.int32, (1, LANES), 1)
        low = lane < HEAD_DIM

        def tile(q0, qn, k_start, kn, k0=None):
            qs = pl.ds(q0, qn)
            ks = pl.ds(k_start, kn)
            vv = v_ref[ks, :]
            for h in range(2):
                mine = low if h == 0 else jnp.logical_not(low)
                cols = pl.ds(h * LANES, LANES)
                s = _dot_nt(q_ref[qs, cols], k_ref[ks, cols])
                if k0 is not None:
                    row = lax.broadcasted_iota(jnp.int32, (qn, kn), 0) + q0
                    col = lax.broadcasted_iota(jnp.int32, (qn, kn), 1) + k0
                    s = jnp.where(row >= col, s, -jnp.inf)
                m_prev = m_sc[h, qs, :]
                m_new = jnp.maximum(m_prev, jnp.max(s, axis=1, keepdims=True))
                p = jnp.exp(s - jnp.tile(m_new, (1, kn // LANES)))
                alpha = jnp.exp(m_prev - m_new)
                l_sc[h, qs, :] = alpha * l_sc[h, qs, :] + jnp.sum(p, axis=1, keepdims=True)
                m_sc[h, qs, :] = m_new
                pv = _dot(p.astype(BF16), jnp.where(mine, vv, jnp.zeros_like(vv)))
                acc_sc[qs, :] = acc_sc[qs, :] * jnp.where(mine, alpha, 1.0) + pv

        def below_diagonal(j, carry):
            tile(0, tq, pl.multiple_of(j * tq, tq), tq)
            return carry

        lax.fori_loop(0, i, below_diagonal, 0)
        diagonal = pl.multiple_of(i * tq, tq)
        tile(0, tq, diagonal, half, k0=0)
        tile(half, half, diagonal + half, half, k0=half)
        l = jnp.where(low, l_sc[0], l_sc[1])
        m = jnp.where(low, m_sc[0], m_sc[1])
        o_ref[...] = acc_sc[...] / l
        lse_ref[...] = m + jnp.log(l)

    qspec = pl.BlockSpec((tq, 2 * LANES), lambda b, hp, i: (b * nq + i, hp))
    kspec = pl.BlockSpec((seq, 2 * LANES), lambda b, hp, i: (b, hp))
    vspec = pl.BlockSpec((seq, LANES), lambda b, hp, i: (b, hp))
    ospec = pl.BlockSpec((tq, LANES), lambda b, hp, i: (b * nq + i, hp))
    return pl.pallas_call(
        body, name="flash_fwd", grid=(bsz, HEAD_PAIRS, nq),
        in_specs=[qspec, kspec, vspec], out_specs=[ospec, ospec],
        out_shape=[jax.ShapeDtypeStruct((t, ATTN_WIDTH), F32), jax.ShapeDtypeStruct((t, ATTN_WIDTH), F32)],
        scratch_shapes=[pltpu.VMEM((2, tq, LANES), F32), pltpu.VMEM((2, tq, LANES), F32), pltpu.VMEM((tq, LANES), F32)],
        compiler_params=_params("arbitrary", "arbitrary", "arbitrary"),
    )(qp, kp, v)


def _flash_bwd(qp, kp, v, o, do, lse, after, bsz, seq):
    tq = ATTN_BLOCK
    half = tq // 2
    nq = seq // tq
    t = bsz * seq

    def body(q_ref, k_ref, v_ref, o_ref, do_ref, lse_ref, after_ref, dq_ref, dk_ref, dv_ref, dk_acc, dv_acc):
        j = pl.program_id(2)

        @pl.when(j == 0)
        def _():
            dq_ref[...] = jnp.zeros_like(dq_ref)

        dk_acc[...] = jnp.zeros_like(dk_acc)
        dv_acc[...] = jnp.zeros_like(dv_acc)
        lane = lax.broadcasted_iota(jnp.int32, (1, LANES), 1)
        low = lane < HEAD_DIM

        def tile(q_start, qn, k0, kn, q0=None):
            rows = pl.ds(q_start, qn)
            ks = pl.ds(k0, kn)
            dov = do_ref[rows, :]
            dd = dov * o_ref[rows, :]
            dob = dov.astype(BF16)
            vv = v_ref[ks, :]
            lse_v = lse_ref[rows, :]
            for h in range(2):
                mine = low if h == 0 else jnp.logical_not(low)
                cols = pl.ds(h * LANES, LANES)
                qh = q_ref[rows, cols]
                kh = k_ref[ks, cols]
                s = _dot_nt(qh, kh)
                lse_h = jnp.where(mine, lse_v, pltpu.roll(lse_v, HEAD_DIM, 1))
                p = jnp.exp(s - jnp.tile(lse_h, (1, kn // LANES)))
                if q0 is not None:
                    row = lax.broadcasted_iota(jnp.int32, (qn, kn), 0) + q0
                    col = lax.broadcasted_iota(jnp.int32, (qn, kn), 1) + k0
                    p = jnp.where(row >= col, p, 0.0)
                delta = jnp.sum(jnp.where(mine, dd, 0.0), axis=1, keepdims=True)
                dp = _dot_nt(dob, jnp.where(mine, vv, jnp.zeros_like(vv)))
                ds = (p * (dp - delta)).astype(BF16)
                dv_acc[ks, :] += jnp.where(mine, _dot_tn(p.astype(BF16), dob), 0.0)
                dk_acc[ks, cols] += _dot_tn(ds, qh)
                dq_ref[rows, cols] += _dot(ds, kh)

        def above_diagonal(i, carry):
            tile(pl.multiple_of(i * tq, tq), tq, 0, tq)
            return carry

        diagonal = pl.multiple_of(j * tq, tq)
        tile(diagonal, tq, 0, half, q0=0)
        tile(diagonal + half, half, half, half, q0=half)
        lax.fori_loop(j + 1, nq, above_diagonal, 0)
        dk_ref[...] = dk_acc[...]
        dv_ref[...] = dv_acc[...].astype(BF16)

    qspec = pl.BlockSpec((seq, 2 * LANES), lambda b, hp, j: (b, hp))
    kspec = pl.BlockSpec((tq, 2 * LANES), lambda b, hp, j: (b * nq + j, hp))
    vspec = pl.BlockSpec((tq, LANES), lambda b, hp, j: (b * nq + j, hp))
    ospec = pl.BlockSpec((seq, LANES), lambda b, hp, j: (b, hp))
    return pl.pallas_call(
        body, name="flash_bwd", grid=(bsz, HEAD_PAIRS, nq),
        in_specs=[qspec, kspec, vspec, ospec, ospec, ospec, ORDER_ONLY], out_specs=[qspec, kspec, vspec],
        out_shape=[jax.ShapeDtypeStruct((t, N_HEADS * LANES), F32), jax.ShapeDtypeStruct((t, N_HEADS * LANES), F32),
                   jax.ShapeDtypeStruct((t, ATTN_WIDTH), BF16)],
        scratch_shapes=[pltpu.VMEM((tq, 2 * LANES), F32), pltpu.VMEM((tq, LANES), F32)],
        compiler_params=_params("arbitrary", "arbitrary", "arbitrary"),
    )(qp, kp, v, o, do, lse, after)


def _mix_out_fwd(o, y_pool, x, gain, w_out):
    t, d = x.shape
    tm = 1024
    pw, aw = POOL_WIDTH, ATTN_WIDTH

    def body(o_ref, yp_ref, x_ref, g_ref, w_ref, ycat_ref, y_ref):
        ov = o_ref[...]
        ya = (ov * _rms_scale(ov) * g_ref[...]).astype(BF16)
        ycat = jnp.concatenate([yp_ref[...], ya], axis=1)
        ycat_ref[...] = ycat
        y_ref[...] = x_ref[...] + _dot(ycat, w_ref[...])

    return pl.pallas_call(
        body, name="mix_out_fwd", grid=(t // tm,),
        in_specs=[_rows(tm, aw), _rows(tm, pw), _rows(tm, d), _resident((1, aw)), _resident((pw + aw, d))],
        out_specs=[_rows(tm, pw + aw), _rows(tm, d)],
        out_shape=[jax.ShapeDtypeStruct((t, pw + aw), BF16), jax.ShapeDtypeStruct((t, d), F32)],
        compiler_params=_params("arbitrary"),
    )(o, y_pool, x, gain, w_out)


def _mix_out_bwd(dx, o, ycat, gain, w_out):
    t, d = dx.shape
    tm = 1024
    nm = t // tm
    pw, aw = POOL_WIDTH, ATTN_WIDTH

    def body(dx_ref, o_ref, ycat_ref, g_ref, w_ref, dw_ref, dyp_ref, do_ref, dg_ref, acc):
        i = pl.program_id(0)

        @pl.when(i == 0)
        def _():
            dg_ref[...] = jnp.zeros_like(dg_ref)
            acc[...] = jnp.zeros_like(acc)

        dxb = dx_ref[...].astype(BF16)
        acc[...] += _dot_tn(ycat_ref[...], dxb)
        dyp_ref[...] = _dot_nt(dxb, w_ref[pl.ds(0, pw), :])
        dya = _dot_nt(dxb, w_ref[pl.ds(pw, aw), :])
        do, dgain = _rms_bwd(dya, o_ref[...], g_ref[...])
        do_ref[...] = do
        dg_ref[...] += dgain

        @pl.when(i == nm - 1)
        def _():
            dw_ref[...] = acc[...].astype(BF16)

    return pl.pallas_call(
        body, name="mix_out_bwd", grid=(nm,),
        in_specs=[_rows(tm, d), _rows(tm, aw), _rows(tm, pw + aw), _resident((1, aw)), _resident((pw + aw, d))],
        out_specs=[pl.BlockSpec((pw + aw, d), lambda i: (0, 0)), _rows(tm, pw), _rows(tm, aw),
                   pl.BlockSpec((1, aw), lambda i: (0, 0))],
        out_shape=[jax.ShapeDtypeStruct((pw + aw, d), BF16), jax.ShapeDtypeStruct((t, pw), F32),
                   jax.ShapeDtypeStruct((t, aw), F32), jax.ShapeDtypeStruct((1, aw), F32)],
        scratch_shapes=[pltpu.VMEM((pw + aw, d), F32)],
        compiler_params=_params("arbitrary"),
    )(dx, o, ycat, gain, w_out)


def _mix_in_bwd(dpv, dq, dk, dv, df, hm, x, dx_res, gain, w_in_t):
    t, d = x.shape
    tm = 512
    nm = t // tm
    pw, aw = POOL_WIDTH, ATTN_WIDTH

    def body(dpv_ref, dq_ref, dk_ref, dv_ref, df_ref, hm_ref, x_ref, dxr_ref, g_ref, w_ref, dw_ref, dx_ref, dxh_ref,
             dg_ref, acc):
        i = pl.program_id(0)

        @pl.when(i == 0)
        def _():
            dg_ref[...] = jnp.zeros_like(dg_ref)
            acc[...] = jnp.zeros_like(acc)

        dh = jnp.concatenate([dpv_ref[...], dq_ref[...], dk_ref[...], dv_ref[...], df_ref[...]], axis=1)
        acc[...] += _dot_tn(dh, hm_ref[...])
        dx, dgain = _rms_bwd(_dot(dh, w_ref[...]), x_ref[...], g_ref[...])
        dx = dxr_ref[...] + dx
        dx_ref[...] = dx
        dxh_ref[...] = (0.5 * dx).astype(BF16)
        dg_ref[...] += dgain

        @pl.when(i == nm - 1)
        def _():
            dw_ref[...] = acc[...].astype(BF16)

    return pl.pallas_call(
        body, name="mix_in_bwd", grid=(nm,),
        in_specs=[_rows(tm, pw), _rows(tm, aw), _rows(tm, aw), _rows(tm, aw), _rows(tm, LANES), _rows(tm, d),
                  _rows(tm, d), _rows(tm, d), _resident((1, d)), _resident((MIX_PAD, d))],
        out_specs=[pl.BlockSpec((MIX_PAD, d), lambda i: (0, 0)), _rows(tm, d), _rows(tm, d),
                   pl.BlockSpec((1, d), lambda i: (0, 0))],
        out_shape=[jax.ShapeDtypeStruct((MIX_PAD, d), BF16), jax.ShapeDtypeStruct((t, d), F32),
                   jax.ShapeDtypeStruct((t, d), BF16), jax.ShapeDtypeStruct((1, d), F32)],
        scratch_shapes=[pltpu.VMEM((MIX_PAD, d), F32)],
        compiler_params=_params("arbitrary"),
    )(dpv, dq, dk, dv, df, hm, x, dx_res, gain, w_in_t)


MESH_IDS = pl.DeviceIdType.MESH


def _me():
    return lax.axis_index("x"), lax.axis_index("y"), lax.axis_index("c")


def _peer(x, y, c, p):
    px = 1 - x if p & 4 else x
    py = 1 - y if p & 2 else y
    pc = 1 - c if p & 1 else c
    return (px, py, pc), 4 * px + 2 * py + pc


HBM_SPEC = pl.BlockSpec(memory_space=pltpu.HBM)
SEM_SPEC = pl.BlockSpec(memory_space=pltpu.SEMAPHORE)
SPLIT_COPY = pltpu.CompilerParams(has_side_effects=pltpu.SideEffectType.DATAFLOW_SIDE_EFFECTING)
PEERS = N_DEV - 1


def _hbm(a):
    return pltpu.with_memory_space_constraint(a, pltpu.HBM)


def _row_block(ref, dev, rows):
    return ref.at[pl.ds(pl.multiple_of(dev * rows, BF16_ROWS), rows)]


def _copy_ends(gather, src, land, me, peer_id):
    if gather:
        rows = src.shape[0]
        return src, _row_block(land, me, rows), _row_block(land, peer_id, rows), src, _row_block(land, me, rows)
    rows = src.shape[0] // N_DEV
    return (_row_block(src, peer_id, rows), land.at[me], land.at[peer_id], _row_block(src, me, rows), land.at[me])


def _land_shape(gather, s):
    return (N_DEV * s.shape[0], s.shape[1]) if gather else (N_DEV, s.shape[0] // N_DEV, s.shape[1])


SIBLING = 1
SAME_CORE_PEERS = (2, 4, 6)
RELAYS = len(SAME_CORE_PEERS)


def _copies_start(groups, gather, name, after=None, relayed=()):
    flat = [s for g in groups for s in g]
    n, ng = len(flat), len(groups)
    lands = [lax.empty(_land_shape(gather, s), s.dtype) for s in flat]
    n_in = 2 * n + (after is not None)

    def body(*refs):
        ins, lnd = refs[:n], refs[n:2 * n]
        sems = refs[n_in:n_in + 2 * ng]
        token = refs[-1]
        x, y, c = _me()
        me = 4 * x + 2 * y + c
        w = 0
        for gi, g in enumerate(groups):
            for k in range(len(g)):
                for p in ((SIBLING,) + SAME_CORE_PEERS if gi in relayed else range(1, N_DEV)):
                    peer, peer_id = _peer(x, y, c, p)
                    src, dst, _, _, _ = _copy_ends(gather, ins[w], lnd[w], me, peer_id)
                    pltpu.make_async_remote_copy(src, dst, sems[2 * gi].at[k * PEERS + p - 1],
                                                 sems[2 * gi + 1].at[k * PEERS + p - 1], device_id=peer,
                                                 device_id_type=MESH_IDS).start()
                w += 1
        token[...] = jnp.zeros_like(token)

    sem_shapes = []
    for g in groups:
        sem_shapes += [pltpu.SemaphoreType.DMA((len(g) * PEERS,))] * 2
    out = pl.pallas_call(
        body, name=name,
        out_shape=(*sem_shapes, *[pltpu.HBM(s.shape, s.dtype) for s in flat],
                   *[pltpu.HBM(l.shape, l.dtype) for l in lands], jax.ShapeDtypeStruct((8, LANES), F32)),
        in_specs=[HBM_SPEC] * (2 * n) + [pl.BlockSpec(memory_space=pl.ANY)] * (after is not None),
        out_specs=(*[SEM_SPEC] * (2 * ng), *[HBM_SPEC] * (2 * n), pl.BlockSpec(memory_space=pltpu.VMEM)),
        input_output_aliases={i: 2 * ng + i for i in range(2 * n)},
        compiler_params=SPLIT_COPY,
    )(*[_hbm(s) for s in flat], *[_hbm(l) for l in lands], *([after] if after is not None else []))
    sems, thru, token = out[:2 * ng], out[2 * ng:2 * ng + 2 * n], out[-1]
    res, w = [], 0
    for gi, g in enumerate(groups):
        res.append((sems[2 * gi], sems[2 * gi + 1], list(thru[w:w + len(g)]), list(thru[n + w:n + w + len(g)])))
        w += len(g)
    return res, token


def _copies_wait(started, gather, after, name):
    send, recv, srcs, lands = started
    n = len(srcs)
    after = list(after) if isinstance(after, (list, tuple)) else [after]

    own_shapes = [s.shape if gather else (s.shape[0] // N_DEV, s.shape[1]) for s in srcs]

    def body(*refs):
        ins, lnd = refs[:n], refs[n:2 * n]
        send_sems, recv_sems = refs[2 * n], refs[2 * n + 1]
        bounce, in_sems, out_sems = refs[-n - 2:-2], refs[-2], refs[-1]
        x, y, c = _me()
        me = 4 * x + 2 * y + c
        ends = [_copy_ends(gather, ins[w], lnd[w], me, me)[3:] for w in range(n)]
        loads = [pltpu.make_async_copy(ends[w][0], bounce[w], in_sems.at[w]) for w in range(n)]
        stores = [pltpu.make_async_copy(bounce[w], ends[w][1], out_sems.at[w]) for w in range(n)]
        for cp in loads:
            cp.start()
        for w in range(n):
            loads[w].wait()
            stores[w].start()
        for w in range(n):
            for p in range(1, N_DEV):
                peer, peer_id = _peer(x, y, c, p)
                src, _, arrival, _, _ = _copy_ends(gather, ins[w], lnd[w], me, peer_id)
                cp = pltpu.make_async_remote_copy(src, arrival, send_sems.at[w * PEERS + p - 1],
                                                  recv_sems.at[w * PEERS + p - 1], device_id=peer,
                                                  device_id_type=MESH_IDS)
                cp.wait_send()
                cp.wait_recv()
        for cp in stores:
            cp.wait()

    out = pl.pallas_call(
        body, name=name,
        out_shape=(*[pltpu.HBM(s.shape, s.dtype) for s in srcs], *[pltpu.HBM(l.shape, l.dtype) for l in lands]),
        in_specs=[HBM_SPEC] * (2 * n) + [SEM_SPEC, SEM_SPEC] + [pl.BlockSpec(memory_space=pl.ANY)] * len(after),
        out_specs=[HBM_SPEC] * (2 * n),
        input_output_aliases={i: i for i in range(2 * n)},
        scratch_shapes=[*[pltpu.VMEM(shape, s.dtype) for shape, s in zip(own_shapes, srcs)],
                        pltpu.SemaphoreType.DMA((n,)), pltpu.SemaphoreType.DMA((n,))],
        compiler_params=SPLIT_COPY,
    )(*srcs, *lands, send, recv, *after)
    return list(out[n:])


def _relay_to_sibling(started, name, after=None):
    send, recv, srcs, lands = started
    n = len(srcs)
    after = [] if after is None else [after]

    def body(*refs):
        ins, lnd = refs[:n], refs[n:2 * n]
        send_sems, recv_sems = refs[2 * n], refs[2 * n + 1]
        relay_send, relay_recv = refs[2 * n + 2 + len(after)], refs[2 * n + 3 + len(after)]
        x, y, c = _me()
        sibling, _ = _peer(x, y, c, SIBLING)
        for w in range(n):
            rows = ins[w].shape[0]
            for k, p in enumerate(SAME_CORE_PEERS):
                peer, peer_id = _peer(x, y, c, p)
                arrived = _row_block(lnd[w], peer_id, rows)
                first = pltpu.make_async_remote_copy(ins[w], arrived, send_sems.at[w * PEERS + p - 1],
                                                     recv_sems.at[w * PEERS + p - 1], device_id=peer,
                                                     device_id_type=MESH_IDS)
                first.wait_recv()
                pltpu.make_async_remote_copy(arrived, arrived, relay_send.at[w * RELAYS + k],
                                             relay_recv.at[w * RELAYS + k], device_id=sibling,
                                             device_id_type=MESH_IDS).start()
                first.wait_send()

    sems = pltpu.SemaphoreType.DMA((n * RELAYS,))
    out = pl.pallas_call(
        body, name=name,
        out_shape=(sems, sems, *[pltpu.HBM(s.shape, s.dtype) for s in srcs], *[pltpu.HBM(l.shape, l.dtype) for l in lands]),
        in_specs=[HBM_SPEC] * (2 * n) + [SEM_SPEC, SEM_SPEC] + [pl.BlockSpec(memory_space=pl.ANY)] * len(after),
        out_specs=(SEM_SPEC, SEM_SPEC, *[HBM_SPEC] * (2 * n)),
        input_output_aliases={i: 2 + i for i in range(2 * n)},
        compiler_params=SPLIT_COPY,
    )(*srcs, *lands, send, recv, *after)
    return send, recv, out[0], out[1], list(out[2:2 + n]), list(out[2 + n:])


def _relayed_wait(relayed, after, name):
    send, recv, relay_send, relay_recv, srcs, lands = relayed
    n = len(srcs)
    after = list(after) if isinstance(after, (list, tuple)) else [after]

    def body(*refs):
        ins, lnd = refs[:n], refs[n:2 * n]
        send_sems, recv_sems, relay_send_sems, relay_recv_sems = refs[2 * n:2 * n + 4]
        bounce, in_sems, out_sems = refs[-n - 2:-2], refs[-2], refs[-1]
        x, y, c = _me()
        me = 4 * x + 2 * y + c
        sibling, sibling_id = _peer(x, y, c, SIBLING)
        loads = [pltpu.make_async_copy(ins[w], bounce[w], in_sems.at[w]) for w in range(n)]
        stores = [pltpu.make_async_copy(bounce[w], _row_block(lnd[w], me, ins[w].shape[0]), out_sems.at[w])
                  for w in range(n)]
        for cp in loads:
            cp.start()
        for w in range(n):
            loads[w].wait()
            stores[w].start()
        for w in range(n):
            rows = ins[w].shape[0]
            direct = pltpu.make_async_remote_copy(ins[w], _row_block(lnd[w], sibling_id, rows),
                                                  send_sems.at[w * PEERS + SIBLING - 1],
                                                  recv_sems.at[w * PEERS + SIBLING - 1], device_id=sibling,
                                                  device_id_type=MESH_IDS)
            direct.wait_send()
            direct.wait_recv()
            for k, p in enumerate(SAME_CORE_PEERS):
                _, sent_id = _peer(x, y, c, p)
                _, got_id = _peer(x, y, c, p + SIBLING)
                relay = pltpu.make_async_remote_copy(_row_block(lnd[w], sent_id, rows), _row_block(lnd[w], got_id, rows),
                                                     relay_send_sems.at[w * RELAYS + k],
                                                     relay_recv_sems.at[w * RELAYS + k], device_id=sibling,
                                                     device_id_type=MESH_IDS)
                relay.wait_send()
                relay.wait_recv()
        for cp in stores:
            cp.wait()

    out = pl.pallas_call(
        body, name=name,
        out_shape=(*[pltpu.HBM(s.shape, s.dtype) for s in srcs], *[pltpu.HBM(l.shape, l.dtype) for l in lands]),
        in_specs=[HBM_SPEC] * (2 * n) + [SEM_SPEC] * 4 + [pl.BlockSpec(memory_space=pl.ANY)] * len(after),
        out_specs=[HBM_SPEC] * (2 * n),
        input_output_aliases={i: i for i in range(2 * n)},
        scratch_shapes=[*[pltpu.VMEM(s.shape, s.dtype) for s in srcs],
                        pltpu.SemaphoreType.DMA((n,)), pltpu.SemaphoreType.DMA((n,))],
        compiler_params=SPLIT_COPY,
    )(*srcs, *lands, send, recv, relay_send, relay_recv, *after)
    return list(out[n:])


def _adamw_update(w, g, m, v):
    nm = ADAM_B1 * m + (1.0 - ADAM_B1) * g
    nv = ADAM_B2 * v + (1.0 - ADAM_B2) * (g * g)
    m_hat = nm / (1.0 - ADAM_B1 ** ADAM_STEP)
    v_hat = nv / (1.0 - ADAM_B2 ** ADAM_STEP)
    return -ADAM_LR * (m_hat / (jnp.sqrt(v_hat) + ADAM_EPS) + ADAM_WD * w), nm, nv


SUM_ADAMW_COLS = 512


def _sum_adamw(parts, w, m, v, name):
    _, rows, d = parts.shape
    n = w.shape[0]
    tc = SUM_ADAMW_COLS

    def body(p_ref, w_ref, m_ref, v_ref, g_ref, d_ref, nm_ref, nv_ref):
        g = p_ref[0].astype(F32)
        for dev in range(1, N_DEV):
            g = g + p_ref[dev].astype(F32)
        g = g[:n]
        g_ref[...] = g
        d_ref[...], nm_ref[...], nv_ref[...] = _adamw_update(w_ref[...], g, m_ref[...], v_ref[...])

    spec = pl.BlockSpec((n, tc), lambda j: (0, j))
    shape = jax.ShapeDtypeStruct((n, d), F32)
    return pl.pallas_call(
        body, name=name, grid=(d // tc,),
        in_specs=[pl.BlockSpec((N_DEV, rows, tc), lambda j: (0, 0, j)), spec, spec, spec],
        out_specs=[spec] * 4, out_shape=[shape] * 4,
        compiler_params=_params("arbitrary"),
    )(parts, w, m, v)


def _pad_rows(a, rows):
    return jnp.pad(a, ((0, rows - a.shape[0]), (0, 0)))


def _row1(vec, width=D_MODEL):
    return jnp.pad(vec.reshape(1, -1), ((0, 0), (0, width - vec.shape[-1])))


COLUMN_SHARDED = ("ffn1_w_gate", "ffn1_w_up", "w_in", "ffn2_w_gate", "ffn2_w_up")
VEC_NAMES = ("ffn1_norm", "mix_norm", "ffn2_norm", "b_forget", "pool_scale", "q_norm", "k_norm", "out_norm_pool",
             "out_norm_attn")
VEC_ROWS = 16
LOSS_ROW = len(VEC_NAMES)


def _pack_vector_grads(parts, loss_part, name):
    names = [n for n in VEC_NAMES if n in parts]
    extra = [] if loss_part is None else [loss_part]

    def body(*refs):
        out_ref = refs[-1]
        out_ref[...] = jnp.zeros_like(out_ref)
        lane = lax.broadcasted_iota(jnp.int32, (1, LANES), 1)
        for n, ref in zip(names, refs):
            val = ref[...]
            if n in ("q_norm", "k_norm"):
                val = val[:, 0:LANES] + val[:, LANES:2 * LANES] + val[:, 2 * LANES:3 * LANES] + val[:, 3 * LANES:]
                val = jnp.where(lane < HEAD_DIM, val + pltpu.roll(val, HEAD_DIM, 1), 0.0)
            out_ref[pl.ds(VEC_NAMES.index(n), 1), pl.ds(0, val.shape[1])] = val
        if extra:
            out_ref[pl.ds(LOSS_ROW, 1), pl.ds(0, 1)] = refs[len(names)][...]

    vmem = pl.BlockSpec(memory_space=pltpu.VMEM)
    return pl.pallas_call(
        body, name=name, in_specs=[vmem] * (len(names) + len(extra)), out_specs=vmem,
        out_shape=jax.ShapeDtypeStruct((VEC_ROWS, D_MODEL), F32),
    )(*[parts[n] for n in names], *extra)


def _small_adamw(vec_all, pool_all, vec_params, pool_params):
    nv = len(vec_params)
    pool_rows = pool_params[0].shape[0]

    def body(*refs):
        vec_ref, pool_ref = refs[0], refs[1]
        ins = refs[2:2 + 3 * nv + 3]
        outs = refs[2 + 3 * nv + 3:-1]
        rows = refs[-1]
        total = vec_ref[pl.ds(0, VEC_ROWS), :]
        for dev in range(1, N_DEV):
            total = total + vec_ref[pl.ds(dev * VEC_ROWS, VEC_ROWS), :]
        rows[...] = total
        outs[4 * nv + 4][...] = rows[pl.ds(LOSS_ROW, 1), pl.ds(0, 1)]
        for i in range(nv):
            w_ref, m_ref, v_ref = ins[3 * i:3 * i + 3]
            g = rows[pl.ds(i, 1), pl.ds(0, w_ref.shape[1])]
            outs[4 * i][...] = g
            outs[4 * i + 1][...], outs[4 * i + 2][...], outs[4 * i + 3][...] = _adamw_update(
                w_ref[...], g, m_ref[...], v_ref[...])
        g = pool_ref[pl.ds(0, pool_rows), :].astype(F32)
        for dev in range(1, N_DEV):
            g = g + pool_ref[pl.ds(dev * pool_rows, pool_rows), :].astype(F32)
        w_ref, m_ref, v_ref = ins[3 * nv:]
        outs[4 * nv][...] = g
        outs[4 * nv + 1][...], outs[4 * nv + 2][...], outs[4 * nv + 3][...] = _adamw_update(
            w_ref[...], g, m_ref[...], v_ref[...])

    vmem = pl.BlockSpec(memory_space=pltpu.VMEM)
    flat = [a for trio in vec_params for a in trio] + list(pool_params)
    out_shape = []
    for trio in list(vec_params) + [pool_params]:
        out_shape += [jax.ShapeDtypeStruct(trio[0].shape, F32)] * 4
    out_shape.append(jax.ShapeDtypeStruct((1, 1), F32))
    return pl.pallas_call(
        body, name="adamw_small", in_specs=[vmem] * (2 + len(flat)), out_specs=[vmem] * len(out_shape),
        out_shape=out_shape, scratch_shapes=[pltpu.VMEM((VEC_ROWS, D_MODEL), F32)],
    )(vec_all, pool_all, *flat)


def kernel(x, ffn1_norm, ffn1_w_gate, ffn1_w_up, ffn1_w_down, mix_norm, w_in, b_forget, pool_w, pool_scale, q_norm, k_norm, out_norm_pool, out_norm_attn, w_out, ffn2_norm, ffn2_w_gate, ffn2_w_up, ffn2_w_down, loss_target, m_ffn1_norm, m_ffn1_w_gate, m_ffn1_w_up, m_ffn1_w_down, m_mix_norm, m_w_in, m_b_forget, m_pool_w, m_pool_scale, m_q_norm, m_k_norm, m_out_norm_pool, m_out_norm_attn, m_w_out, m_ffn2_norm, m_ffn2_w_gate, m_ffn2_w_up, m_ffn2_w_down, v_ffn1_norm, v_ffn1_w_gate, v_ffn1_w_up, v_ffn1_w_down, v_mix_norm, v_w_in, v_b_forget, v_pool_w, v_pool_scale, v_q_norm, v_k_norm, v_out_norm_pool, v_out_norm_attn, v_w_out, v_ffn2_norm, v_ffn2_w_gate, v_ffn2_w_up, v_ffn2_w_down):
    bsz, seq, d = x.shape
    t = bsz * seq
    x0 = x.reshape(t, d)
    target = loss_target.reshape(t, d)
    in_rows = -(-w_in.shape[1] // BF16_ROWS) * BF16_ROWS

    slabs = [s.astype(BF16) for s in (ffn1_w_gate.T, ffn1_w_up.T, ffn1_w_down, _pad_rows(w_in.T, in_rows), w_out,
                                       ffn2_w_gate.T, ffn2_w_up.T, ffn2_w_down)]
    first, started = _copies_start([slabs[0:2]], True, "gather_start_first", relayed=(0,))
    rest, started = _copies_start([slabs[2:3], slabs[3:4], slabs[4:5], slabs[5:8]], True, "gather_start", after=started,
                                  relayed=(3,))
    gathers = first + rest

    g1, gm, g2 = ffn1_norm.reshape(1, d), mix_norm.reshape(1, d), ffn2_norm.reshape(1, d)
    bf_row = _row1(b_forget, LANES)
    gq = jnp.tile(q_norm, N_HEADS).reshape(1, ATTN_WIDTH)
    gk = jnp.tile(k_norm, N_HEADS).reshape(1, ATTN_WIDTH)
    scale_row = pool_scale.reshape(1, POOL_WIDTH)
    gp, ga = out_norm_pool.reshape(1, POOL_WIDTH), out_norm_attn.reshape(1, ATTN_WIDTH)

    h1 = _norm_bf16(x0, g1, started, "ffn1_norm")
    wg1, wu1 = _relayed_wait(_relay_to_sibling(gathers[0], "gather_relay_ffn1_up", h1), started, "gather_wait_ffn1_up")
    sa1, sb1, s1 = _ffn_up_normed(h1, wg1, wu1, "ffn1_up")
    (wd1,) = _copies_wait(gathers[1], True, s1, "gather_wait_ffn1_down")
    (x1,) = _ffn_down(s1, wd1, x0, None, "ffn1_down")
    (win_g,) = _copies_wait(gathers[2], True, x1, "gather_wait_w_in")
    win_t = _repack_rows(win_g, in_rows, w_in.shape[1], N_DEV, "w_in_rows")
    hm, pv, q, k, v, f = _mix_in_fwd(x1, gm, win_t)
    pooled, mixed, y_pool = _pool_fwd(pv, pool_w, scale_row, gp, bsz, seq)
    qp, kp = _attn_prep_fwd(q, k, f, bf_row, gq, gk, bsz, seq)
    o, lse = _flash_fwd(qp, kp, v, bsz, seq)
    relayed_ffn2 = _relay_to_sibling(gathers[4], "gather_relay_ffn2", o)
    (wout,) = _copies_wait(gathers[3], True, [o, relayed_ffn2[4][0]], "gather_wait_w_out")
    ycat, x2 = _mix_out_fwd(o, y_pool, x1, ga, wout)
    wg2, wu2, wd2 = _relayed_wait(relayed_ffn2, x2, "gather_wait_ffn2")
    h2, sa2, sb2, s2 = _ffn_up(x2, g2, wg2, wu2, "ffn2_up")
    dx3, dyh2, loss_part = _ffn_down(s2, wd2, x2, target, "ffn2_down")

    da2, db2, dwg2, dwu2 = _ffn_bwd_act(dyh2, sa2, sb2, h2, wd2, dx3, "ffn2_bwd_act")
    (dwd2,) = _wgrad([s2], dyh2, da2, "ffn2_down_wgrad")
    (sent_ffn2,), tok = _copies_start([[dwg2, dwu2, dwd2]], False, "exchange_start_ffn2")
    dx2, dg2 = _ffn_bwd_dx(da2, db2, dx3, x2, g2, wg2, wu2, tok, "ffn2_bwd_dx")
    dwout, dy_pool, do, dga = _mix_out_bwd(dx2, o, ycat, ga, wout)
    (sent_out,), tok = _copies_start([[dwout]], False, "exchange_start_w_out")
    dqp, dkp, dv = _flash_bwd(qp, kp, v, o, do, lse, tok, bsz, seq)
    dq, dk, df, dgq, dgk, dbf = _attn_prep_bwd(dqp, dkp, q, k, f, bf_row, gq, gk, bsz, seq)
    dpv, dpool_w, dscale, dgp = _pool_bwd(dy_pool, mixed, pooled, pool_w, scale_row, gp, bsz, seq)
    dwin, dx1, dyh1, dgm = _mix_in_bwd(dpv, dq, dk, dv, df, hm, x1, dx2, gm, win_t)
    dwin_blocks = _repack_rows(dwin, w_in.shape[1], in_rows, N_DEV, "w_in_grad_blocks")
    (sent_in,), tok = _copies_start([[dwin_blocks]], False, "exchange_start_w_in")
    (dwd1,) = _wgrad([s1], dyh1, tok, "ffn1_down_wgrad")
    (sent_down1,), tok = _copies_start([[dwd1]], False, "exchange_start_ffn1_down", after=tok)
    da1, db1, dwg1, dwu1 = _ffn_bwd_act(dyh1, sa1, sb1, h1, wd1, tok, "ffn1_bwd_act")
    (sent_up1,), tok = _copies_start([[dwg1, dwu1]], False, "exchange_start_ffn1_up", after=tok)
    dx0, dg1 = _ffn_bwd_dx(da1, db1, dx1, x0, g1, wg1, wu1, tok, "ffn1_bwd_dx")

    pool_rows = POOL_GROUPS * POOL_GROUP_DIM
    packed = _pack_vector_grads(dict(ffn1_norm=dg1, mix_norm=dgm, ffn2_norm=dg2, b_forget=dbf, pool_scale=dscale,
                                     q_norm=dgq, k_norm=dgk, out_norm_pool=dgp, out_norm_attn=dga), loss_part,
                                "pack_vector_grads")
    pool_part = dpool_w.reshape(pool_rows, POOL_GROUP_DIM).astype(BF16)
    (sent_small,), tok = _copies_start([[packed, pool_part]], True, "small_grads_start")

    weights = dict(ffn1_norm=ffn1_norm, ffn1_w_gate=ffn1_w_gate, ffn1_w_up=ffn1_w_up, ffn1_w_down=ffn1_w_down,
                   mix_norm=mix_norm, w_in=w_in, b_forget=b_forget, pool_w=pool_w, pool_scale=pool_scale,
                   q_norm=q_norm, k_norm=k_norm, out_norm_pool=out_norm_pool, out_norm_attn=out_norm_attn,
                   w_out=w_out, ffn2_norm=ffn2_norm, ffn2_w_gate=ffn2_w_gate, ffn2_w_up=ffn2_w_up,
                   ffn2_w_down=ffn2_w_down)
    m_in = dict(ffn1_norm=m_ffn1_norm, ffn1_w_gate=m_ffn1_w_gate, ffn1_w_up=m_ffn1_w_up, ffn1_w_down=m_ffn1_w_down,
                mix_norm=m_mix_norm, w_in=m_w_in, b_forget=m_b_forget, pool_w=m_pool_w, pool_scale=m_pool_scale,
                q_norm=m_q_norm, k_norm=m_k_norm, out_norm_pool=m_out_norm_pool, out_norm_attn=m_out_norm_attn,
                w_out=m_w_out, ffn2_norm=m_ffn2_norm, ffn2_w_gate=m_ffn2_w_gate, ffn2_w_up=m_ffn2_w_up,
                ffn2_w_down=m_ffn2_w_down)
    v_in = dict(ffn1_norm=v_ffn1_norm, ffn1_w_gate=v_ffn1_w_gate, ffn1_w_up=v_ffn1_w_up, ffn1_w_down=v_ffn1_w_down,
                mix_norm=v_mix_norm, w_in=v_w_in, b_forget=v_b_forget, pool_w=v_pool_w, pool_scale=v_pool_scale,
                q_norm=v_q_norm, k_norm=v_k_norm, out_norm_pool=v_out_norm_pool, out_norm_attn=v_out_norm_attn,
                w_out=v_w_out, ffn2_norm=v_ffn2_norm, ffn2_w_gate=v_ffn2_w_gate, ffn2_w_up=v_ffn2_w_up,
                ffn2_w_down=v_ffn2_w_down)
    grads, delta, new_m, new_v = {}, {}, {}, {}
    after = [tok]
    plan = ((sent_ffn2, "ffn2", ("ffn2_w_gate", "ffn2_w_up", "ffn2_w_down")), (sent_out, "w_out", ("w_out",)),
            (sent_in, "w_in", ("w_in",)), (sent_down1, "ffn1_down", ("ffn1_w_down",)),
            (sent_up1, "ffn1_up", ("ffn1_w_gate", "ffn1_w_up")))
    for sent, tag, names in plan:
        parts = _copies_wait(sent, False, after, f"exchange_wait_{tag}")
        after = []
        for n, part in zip(names, parts):
            turn = (lambda a: a.T) if n in COLUMN_SHARDED else (lambda a: a)
            done = _sum_adamw(part, turn(weights[n]), turn(m_in[n]), turn(v_in[n]), f"adamw_{n}")
            grads[n], delta[n], new_m[n], new_v[n] = (turn(a) for a in done)
            after.append(done[3])
    vec_all, pool_all = _copies_wait(sent_small, True, after, "small_grads_wait")
    as_row = lambda a: a.reshape(1, -1)
    as_pool = lambda a: a.reshape(pool_rows, POOL_GROUP_DIM)
    small = _small_adamw(vec_all, pool_all,
                         [tuple(as_row(z[n]) for z in (weights, m_in, v_in)) for n in VEC_NAMES],
                         tuple(as_pool(z["pool_w"]) for z in (weights, m_in, v_in)))
    for i, n in enumerate(VEC_NAMES + ("pool_w",)):
        grads[n], delta[n], new_m[n], new_v[n] = (a.reshape(weights[n].shape) for a in small[4 * i:4 * i + 4])
    loss = small[-1].reshape(())

    order = ("ffn1_norm", "ffn1_w_gate", "ffn1_w_up", "ffn1_w_down", "mix_norm", "w_in", "b_forget", "pool_w",
             "pool_scale", "q_norm", "k_norm", "out_norm_pool", "out_norm_attn", "w_out", "ffn2_norm", "ffn2_w_gate",
             "ffn2_w_up", "ffn2_w_down")
    return (loss, dx0.reshape(bsz, seq, d), *[grads[n] for n in order], *[delta[n] for n in order],
            *[new_m[n] for n in order], *[new_v[n] for n in order])
```

```python
import jax
import jax.numpy as jnp
from jax import lax
from jax.experimental import pallas as pl
from jax.experimental.pallas import tpu as pltpu

F32 = jnp.float32
BF16 = jnp.bfloat16

EPS = 1e-6
D_MODEL = 1024
N_HEADS = 8
HEAD_DIM = 64
POOL_WIDTH = 512
ATTN_WIDTH = 512
POOL_GROUPS = 4
POOL_GROUP_DIM = 128
POOL_WINDOWS = (2, 4, 8, 16)
POOL_HALO = 16
MIX_PAD = POOL_WIDTH + 3 * ATTN_WIDTH + 128
N_DEV = 8
BF16_ROWS = 16
LANES = 128
VMEM_LIMIT = 56 * 1024 * 1024

ADAM_LR = 0.001
ADAM_B1 = 0.9
ADAM_B2 = 0.999
ADAM_EPS = 1e-08
ADAM_WD = 0.01
ADAM_STEP = 10


def _params(*sem):
    return pltpu.CompilerParams(dimension_semantics=sem, vmem_limit_bytes=VMEM_LIMIT)


def _dot(a, b):
    return jnp.dot(a, b, preferred_element_type=F32)


def _dot_nt(a, b):
    return lax.dot_general(a, b, (((1,), (1,)), ((), ())), preferred_element_type=F32)


def _dot_tn(a, b):
    return lax.dot_general(a, b, (((0,), (0,)), ((), ())), preferred_element_type=F32)


def _resident(shape):
    return pl.BlockSpec(shape, lambda *_: (0,) * len(shape), pipeline_mode=pl.Buffered(1))


def _rows(tm, width):
    return pl.BlockSpec((tm, width), lambda i: (i, 0))


ORDER_ONLY = pl.BlockSpec(memory_space=pl.ANY)


def _rms_scale(x):
    return lax.rsqrt(jnp.mean(x * x, axis=-1, keepdims=True) + EPS)


def _rms_bwd(dh, x, gain):
    r = _rms_scale(x)
    n = x * r
    dgain = jnp.sum(dh * n, axis=0, keepdims=True)
    dn = dh * gain
    dx = r * (dn - n * jnp.mean(dn * n, axis=-1, keepdims=True))
    return dx, dgain


def _split3(x):
    hi = x.astype(BF16)
    r1 = x - hi.astype(F32)
    mid = r1.astype(BF16)
    lo = (r1 - mid.astype(F32)).astype(BF16)
    return hi, mid, lo


FF_CHUNK = 256


def _swiglu_parts(a, b):
    sig = jax.nn.sigmoid(a)
    silu = a * sig
    return (b * (sig + silu * (1.0 - sig))).astype(BF16), silu.astype(BF16), (silu * b).astype(BF16)


def _ffn_up(x, gain, wg_t, wu_t, name):
    t, d = x.shape
    f = wg_t.shape[0]
    tm = 512

    def body(x_ref, g_ref, wg_ref, wu_ref, h_ref, sa_ref, sb_ref, s_ref):
        xv = x_ref[...]
        h = (xv * _rms_scale(xv) * g_ref[...]).astype(BF16)
        h_ref[...] = h
        for c in range(f // FF_CHUNK):
            sl = pl.ds(c * FF_CHUNK, FF_CHUNK)
            sa_ref[:, sl], sb_ref[:, sl], s_ref[:, sl] = _swiglu_parts(_dot_nt(h, wg_ref[sl, :]), _dot_nt(h, wu_ref[sl, :]))

    wide = jax.ShapeDtypeStruct((t, f), BF16)
    return pl.pallas_call(
        body, name=name, grid=(t // tm,),
        in_specs=[_rows(tm, d), _resident((1, d)), _resident((f, d)), _resident((f, d))],
        out_specs=[_rows(tm, d), _rows(tm, f), _rows(tm, f), _rows(tm, f)],
        out_shape=[jax.ShapeDtypeStruct((t, d), BF16), wide, wide, wide],
        compiler_params=_params("arbitrary"),
    )(x, gain, wg_t, wu_t)


def _ffn_down(s, wd, x, target, name):
    t, d = x.shape
    f = wd.shape[0]
    tm = 512
    with_loss = target is not None

    def body(*refs):
        if with_loss:
            s_ref, w_ref, x_ref, t_ref, dy_ref, dyh_ref, loss_ref = refs
        else:
            s_ref, w_ref, x_ref, y_ref = refs
        y = x_ref[...] + 0.5 * _dot(s_ref[...], w_ref[...])
        if with_loss:
            e = y - t_ref[...]
            dy = e * (1.0 / d)
            dy_ref[...] = dy
            dyh_ref[...] = (0.5 * dy).astype(BF16)

            @pl.when(pl.program_id(0) == 0)
            def _():
                loss_ref[...] = jnp.zeros_like(loss_ref)

            part = jnp.sum(jnp.sum(e * e, axis=0, keepdims=True), axis=1, keepdims=True)
            loss_ref[...] += part * (0.5 / d)
        else:
            y_ref[...] = y

    in_specs = [_rows(tm, f), _resident((f, d)), _rows(tm, d)]
    args = [s, wd, x]
    if with_loss:
        in_specs.append(_rows(tm, d))
        args.append(target)
        out_shape = [jax.ShapeDtypeStruct((t, d), F32), jax.ShapeDtypeStruct((t, d), BF16),
                     jax.ShapeDtypeStruct((1, 1), F32)]
        out_specs = [_rows(tm, d), _rows(tm, d), pl.BlockSpec((1, 1), lambda i: (0, 0))]
    else:
        out_shape = [jax.ShapeDtypeStruct((t, d), F32)]
        out_specs = [_rows(tm, d)]
    return pl.pallas_call(
        body, name=name, grid=(t // tm,), in_specs=in_specs, out_specs=out_specs, out_shape=out_shape,
        compiler_params=_params("arbitrary"),
    )(*args)


def _ffn_bwd_act(dyh, sa, sb, h, wd, after, name):
    t, d = dyh.shape
    f = wd.shape[0]
    tn = f // 2
    tk = 512
    nk = t // tk

    def body(dy_ref, sa_ref, sb_ref, h_ref, wd_ref, after_ref, da_ref, db_ref, dwg_ref, dwu_ref, acc_g, acc_u):
        k = pl.program_id(1)

        @pl.when(k == 0)
        def _():
            acc_g[...] = jnp.zeros_like(acc_g)
            acc_u[...] = jnp.zeros_like(acc_u)

        ds = _dot_nt(dy_ref[...], wd_ref[...])
        da = (ds * sa_ref[...].astype(F32)).astype(BF16)
        db = (ds * sb_ref[...].astype(F32)).astype(BF16)
        da_ref[...] = da
        db_ref[...] = db
        hv = h_ref[...]
        acc_g[...] += _dot_tn(da, hv)
        acc_u[...] += _dot_tn(db, hv)

        @pl.when(k == nk - 1)
        def _():
            dwg_ref[...] = acc_g[...].astype(BF16)
            dwu_ref[...] = acc_u[...].astype(BF16)

    tokens = pl.BlockSpec((tk, d), lambda j, k: (k, 0))
    wide = pl.BlockSpec((tk, tn), lambda j, k: (k, j))
    weight = pl.BlockSpec((tn, d), lambda j, k: (j, 0))
    return pl.pallas_call(
        body, name=name, grid=(f // tn, nk),
        in_specs=[tokens, wide, wide, tokens, weight, ORDER_ONLY],
        out_specs=[wide, wide, weight, weight],
        out_shape=[jax.ShapeDtypeStruct((t, f), BF16)] * 2 + [jax.ShapeDtypeStruct((f, d), BF16)] * 2,
        scratch_shapes=[pltpu.VMEM((tn, d), F32)] * 2,
        compiler_params=_params("arbitrary", "arbitrary"),
    )(dyh, sa, sb, h, wd, after)


def _ffn_bwd_dx(da, db, dy, x, gain, wg_t, wu_t, after, name):
    t, d = x.shape
    f = wg_t.shape[0]
    tm = 512

    def body(da_ref, db_ref, dy_ref, x_ref, g_ref, wg_ref, wu_ref, after_ref, dx_ref, dg_ref):
        dh = _dot(da_ref[...], wg_ref[...]) + _dot(db_ref[...], wu_ref[...])
        dx, dgain = _rms_bwd(dh, x_ref[...], g_ref[...])
        dx_ref[...] = dy_ref[...] + dx

        @pl.when(pl.program_id(0) == 0)
        def _():
            dg_ref[...] = jnp.zeros_like(dg_ref)

        dg_ref[...] += dgain

    return pl.pallas_call(
        body, name=name, grid=(t // tm,),
        in_specs=[_rows(tm, f), _rows(tm, f), _rows(tm, d), _rows(tm, d), _resident((1, d)), _resident((f, d)),
                  _resident((f, d)), ORDER_ONLY],
        out_specs=[_rows(tm, d), pl.BlockSpec((1, d), lambda i: (0, 0))],
        out_shape=[jax.ShapeDtypeStruct((t, d), F32), jax.ShapeDtypeStruct((1, d), F32)],
        compiler_params=_params("arbitrary"),
    )(da, db, dy, x, gain, wg_t, wu_t, after)


def _wgrad(lhs, b, after, name):
    t, n = lhs[0].shape
    d = b.shape[1]
    m = len(lhs)
    tn = n // 2 if n * d * m > (4 << 20) else n
    tk = 1024
    nk = t // tk

    def body(*refs):
        a_refs, b_ref, o_refs, accs = refs[:m], refs[m], refs[m + 2:2 * m + 2], refs[2 * m + 2:]
        k = pl.program_id(1)

        @pl.when(k == 0)
        def _():
            for acc in accs:
                acc[...] = jnp.zeros_like(acc)

        bv = b_ref[...]
        for a_ref, acc in zip(a_refs, accs):
            acc[...] += _dot_tn(a_ref[...], bv)

        @pl.when(k == nk - 1)
        def _():
            for o_ref, acc in zip(o_refs, accs):
                o_ref[...] = acc[...].astype(BF16)

    return pl.pallas_call(
        body, name=name, grid=(n // tn, nk),
        in_specs=[pl.BlockSpec((tk, tn), lambda j, k: (k, j))] * m + [pl.BlockSpec((tk, d), lambda j, k: (k, 0)),
                                                                       ORDER_ONLY],
        out_specs=[pl.BlockSpec((tn, d), lambda j, k: (j, 0))] * m,
        out_shape=[jax.ShapeDtypeStruct((n, d), BF16)] * m,
        scratch_shapes=[pltpu.VMEM((tn, d), F32)] * m,
        compiler_params=_params("arbitrary", "arbitrary"),
    )(*lhs, b, after)


def _repack_rows(a, rows_in, rows_out, blocks, name):
    total, d = a.shape
    real = min(rows_in, rows_out)

    def body(a_ref, o_ref, wide_in, wide_out):
        wide_in[...] = a_ref[...].astype(F32)
        wide_out[...] = jnp.zeros_like(wide_out)
        for j in range(blocks):
            wide_out[pl.ds(j * rows_out, real), :] = wide_in[pl.ds(j * rows_in, real), :]
        o_ref[...] = wide_out[...].astype(BF16)

    full = pl.BlockSpec((total, d), lambda i: (0, 0))
    return pl.pallas_call(
        body, name=name, grid=(1,), in_specs=[full], out_specs=full, out_shape=jax.ShapeDtypeStruct((total, d), BF16),
        scratch_shapes=[pltpu.VMEM((total, d), F32)] * 2,
        compiler_params=_params("arbitrary"),
    )(a)


def _mix_in_fwd(x, gain, w_in_t):
    t, d = x.shape
    tm = 1024
    pw, aw = POOL_WIDTH, ATTN_WIDTH

    def body(x_ref, g_ref, w_ref, hm_ref, pv_ref, q_ref, k_ref, v_ref, f_ref):
        xv = x_ref[...]
        hm = (xv * _rms_scale(xv) * g_ref[...]).astype(BF16)
        hm_ref[...] = hm
        pv_ref[...] = _dot_nt(hm, w_ref[pl.ds(0, pw), :])
        q_ref[...] = _dot_nt(hm, w_ref[pl.ds(pw, aw), :])
        k_ref[...] = _dot_nt(hm, w_ref[pl.ds(pw + aw, aw), :])
        v_ref[...] = _dot_nt(hm, w_ref[pl.ds(pw + 2 * aw, aw), :]).astype(BF16)
        f_ref[...] = _dot_nt(hm, w_ref[pl.ds(pw + 3 * aw, LANES), :])

    return pl.pallas_call(
        body, name="mix_in_fwd", grid=(t // tm,),
        in_specs=[_rows(tm, d), _resident((1, d)), _resident((MIX_PAD, d))],
        out_specs=[_rows(tm, d), _rows(tm, pw), _rows(tm, aw), _rows(tm, aw), _rows(tm, aw), _rows(tm, LANES)],
        out_shape=[jax.ShapeDtypeStruct((t, d), BF16), jax.ShapeDtypeStruct((t, pw), F32),
                   jax.ShapeDtypeStruct((t, aw), F32), jax.ShapeDtypeStruct((t, aw), F32),
                   jax.ShapeDtypeStruct((t, aw), BF16), jax.ShapeDtypeStruct((t, LANES), F32)],
        compiler_params=_params("arbitrary"),
    )(x, gain, w_in_t)


def _pool_fwd(pv, pool_w, pool_scale, gain, bsz, seq):
    ts = 512
    ns = seq // ts
    pw = POOL_WIDTH

    def body(pv_ref, w_ref, sc_ref, g_ref, pooled_ref, mixed_ref, y_ref, ext):
        s = pl.program_id(1)

        @pl.when(s == 0)
        def _():
            ext[pl.ds(0, POOL_HALO), :] = jnp.zeros((POOL_HALO, pw), F32)

        p = pv_ref[...]
        ext[pl.ds(POOL_HALO, ts), :] = p
        pos = s * ts + lax.broadcasted_iota(jnp.int32, (ts, 1), 0)
        parts = []
        for g, w in enumerate(POOL_WINDOWS):
            lanes = pl.ds(g * POOL_GROUP_DIM, POOL_GROUP_DIM)
            win = ext[pl.ds(POOL_HALO, ts), lanes]
            for i in range(1, w):
                win = win + ext[pl.ds(POOL_HALO - i, ts), lanes]
            cnt = jnp.minimum(pos + 1, w).astype(F32)
            pooled = (win / cnt - ext[pl.ds(POOL_HALO, ts), lanes]).astype(BF16)
            pooled_ref[:, lanes] = pooled
            parts.append(_dot(pooled, w_ref[g].astype(BF16)))
        mixed = jnp.concatenate(parts, axis=1)
        mixed_ref[...] = mixed
        pm = mixed * sc_ref[...]
        y_ref[...] = (pm * _rms_scale(pm) * g_ref[...]).astype(BF16)
        ext[pl.ds(0, POOL_HALO), :] = p[ts - POOL_HALO:, :]

    blk = pl.BlockSpec((ts, pw), lambda b, s: (b * ns + s, 0))
    t = bsz * seq
    return pl.pallas_call(
        body, name="pool_fwd", grid=(bsz, ns),
        in_specs=[blk, pl.BlockSpec((POOL_GROUPS, POOL_GROUP_DIM, POOL_GROUP_DIM), lambda b, s: (0, 0, 0)),
                  pl.BlockSpec((1, pw), lambda b, s: (0, 0)), pl.BlockSpec((1, pw), lambda b, s: (0, 0))],
        out_specs=[blk, blk, blk],
        out_shape=[jax.ShapeDtypeStruct((t, pw), BF16), jax.ShapeDtypeStruct((t, pw), F32),
                   jax.ShapeDtypeStruct((t, pw), BF16)],
        scratch_shapes=[pltpu.VMEM((POOL_HALO + ts, pw), F32)],
        compiler_params=_params("arbitrary", "arbitrary"),
    )(pv, pool_w, pool_scale, gain)


def _pool_bwd(dy, mixed, pooled, pool_w, pool_scale, gain, bsz, seq):
    ts = 512
    ns = seq // ts
    pw = POOL_WIDTH

    def body(dy_ref, mixed_ref, pooled_ref, w_ref, sc_ref, g_ref, dpv_ref, dw_ref, dsc_ref, dg_ref, ext):
        b = pl.program_id(0)
        sr = pl.program_id(1)
        s = ns - 1 - sr

        @pl.when(jnp.logical_and(b == 0, sr == 0))
        def _():
            dw_ref[...] = jnp.zeros_like(dw_ref)
            dsc_ref[...] = jnp.zeros_like(dsc_ref)
            dg_ref[...] = jnp.zeros_like(dg_ref)

        @pl.when(sr == 0)
        def _():
            ext[pl.ds(ts, POOL_HALO), :] = jnp.zeros((POOL_HALO, pw), F32)

        mixed = mixed_ref[...]
        sc = sc_ref[...]
        dpm, dgain = _rms_bwd(dy_ref[...], mixed * sc, g_ref[...])
        dg_ref[...] += dgain
        dsc_ref[...] += jnp.sum(dpm * mixed, axis=0, keepdims=True)
        dmixed = (dpm * sc).astype(BF16)
        pos = s * ts + lax.broadcasted_iota(jnp.int32, (ts, 1), 0)
        dpooled = []
        for g, w in enumerate(POOL_WINDOWS):
            lanes = pl.ds(g * POOL_GROUP_DIM, POOL_GROUP_DIM)
            dm = dmixed[:, g * POOL_GROUP_DIM:(g + 1) * POOL_GROUP_DIM]
            dw_ref[g] += _dot_tn(pooled_ref[:, lanes], dm)
            dp = _dot_nt(dm, w_ref[g].astype(BF16))
            dpooled.append(dp)
            cnt = jnp.minimum(pos + 1, w).astype(F32)
            ext[pl.ds(0, ts), lanes] = dp / cnt
        for g, w in enumerate(POOL_WINDOWS):
            lanes = pl.ds(g * POOL_GROUP_DIM, POOL_GROUP_DIM)
            win = ext[pl.ds(0, ts), lanes]
            for i in range(1, w):
                win = win + ext[pl.ds(i, ts), lanes]
            dpv_ref[:, lanes] = (win - dpooled[g]).astype(BF16)
        head = ext[pl.ds(0, POOL_HALO), :]
        ext[pl.ds(ts, POOL_HALO), :] = head

    blk = pl.BlockSpec((ts, pw), lambda b, s: (b * ns + (ns - 1 - s), 0))
    vec = pl.BlockSpec((1, pw), lambda b, s: (0, 0))
    wspec = pl.BlockSpec((POOL_GROUPS, POOL_GROUP_DIM, POOL_GROUP_DIM), lambda b, s: (0, 0, 0))
    t = bsz * seq
    return pl.pallas_call(
        body, name="pool_bwd", grid=(bsz, ns),
        in_specs=[blk, blk, blk, wspec, vec, vec],
        out_specs=[blk, wspec, vec, vec],
        out_shape=[jax.ShapeDtypeStruct((t, pw), BF16),
                   jax.ShapeDtypeStruct((POOL_GROUPS, POOL_GROUP_DIM, POOL_GROUP_DIM), F32),
                   jax.ShapeDtypeStruct((1, pw), F32), jax.ShapeDtypeStruct((1, pw), F32)],
        scratch_shapes=[pltpu.VMEM((ts + POOL_HALO, pw), F32)],
        compiler_params=_params("arbitrary", "arbitrary"),
    )(dy, mixed, pooled, pool_w, pool_scale, gain)


AUX_ONE = 64
AUX_F = 67

ATTN_PREP_ROWS = 512


def _seg_ones(width, seg):
    r = lax.broadcasted_iota(jnp.int32, (width, width), 0) // seg
    c = lax.broadcasted_iota(jnp.int32, (width, width), 1) // seg
    return (r == c).astype(BF16)


def _tri_ones(n, lower):
    r = lax.broadcasted_iota(jnp.int32, (n, n), 0)
    c = lax.broadcasted_iota(jnp.int32, (n, n), 1)
    return ((r >= c) if lower else (r <= c)).astype(BF16)


def _place_pieces(first_lane):
    r = lax.broadcasted_iota(jnp.int32, (3 * LANES, N_HEADS * LANES), 0)
    c = lax.broadcasted_iota(jnp.int32, (3 * LANES, N_HEADS * LANES), 1)
    piece, head = r // LANES, r % LANES
    return jnp.logical_and(head < N_HEADS, c == head * LANES + first_lane + piece).astype(BF16)


def _head_sums(x, seg_ones):
    return _dot(x.astype(BF16), seg_ones)


def _log_sigmoid(x):
    return jnp.minimum(x, 0.0) - jnp.log(1.0 + jnp.exp(-jnp.abs(x)))


def _attn_prep_fwd(q, k, f, b_forget, q_gain, k_gain, bsz, seq):
    ts = ATTN_PREP_ROWS
    ns = seq // ts
    aw = ATTN_WIDTH
    t = bsz * seq
    seg = _seg_ones(aw, HEAD_DIM)
    tri = _tri_ones(ts, True)

    def body(q_ref, k_ref, f_ref, bf_ref, gq_ref, gk_ref, seg_ref, tri_ref, place_ref, qp_ref, kp_ref, carry):
        s = pl.program_id(1)

        @pl.when(s == 0)
        def _():
            carry[...] = jnp.zeros_like(carry)

        logf = _log_sigmoid(f_ref[...] + bf_ref[...])
        hi, mid, lo = _split3(logf)
        tri_v = tri_ref[...]
        fc = _dot(tri_v, hi) + _dot(tri_v, mid) + _dot(tri_v, lo) + carry[pl.ds(0, 1), :]
        carry[pl.ds(0, 1), :] = fc[ts - 1:, :]
        pcs = jnp.concatenate(_split3(fc), axis=1)
        lane = lax.broadcasted_iota(jnp.int32, (1, LANES), 1)
        ones_q = jnp.logical_and(lane >= AUX_ONE, lane < AUX_ONE + 3).astype(F32)
        ones_k = jnp.logical_and(lane >= AUX_F, lane < AUX_F + 3).astype(F32)
        seg_v = seg_ref[...]
        placed = _dot(pcs, place_ref[...])

        def build(x_ref, g_ref, scale, out_ref, ones, for_keys):
            xv = x_ref[...]
            r = lax.rsqrt(_head_sums(xv * xv, seg_v) * (1.0 / HEAD_DIM) + EPS)
            xn = xv * r * g_ref[...] * scale
            for h in range(N_HEADS):
                pair = xn[:, (h // 2) * LANES:(h // 2 + 1) * LANES]
                feat = pair if h % 2 == 0 else pltpu.roll(pair, HEAD_DIM, 1)
                aux_h = placed[:, h * LANES:(h + 1) * LANES]
                if for_keys:
                    aux_h = -pltpu.roll(aux_h, LANES - (AUX_F - AUX_ONE), 1)
                out_ref[:, h * LANES:(h + 1) * LANES] = jnp.where(lane < HEAD_DIM, feat, aux_h + ones).astype(BF16)

        build(q_ref, gq_ref, 0.125, qp_ref, ones_q, False)
        build(k_ref, gk_ref, 1.0, kp_ref, ones_k, True)

    blk = pl.BlockSpec((ts, aw), lambda b, s: (b * ns + s, 0))
    fblk = pl.BlockSpec((ts, LANES), lambda b, s: (b * ns + s, 0))
    oblk = pl.BlockSpec((ts, N_HEADS * LANES), lambda b, s: (b * ns + s, 0))
    const = lambda shape: pl.BlockSpec(shape, lambda b, s: (0, 0))
    return pl.pallas_call(
        body, name="attn_prep_fwd", grid=(bsz, ns),
        in_specs=[blk, blk, fblk, const((1, LANES)), const((1, aw)), const((1, aw)), const((aw, aw)), const((ts, ts)),
                  const((3 * LANES, N_HEADS * LANES))],
        out_specs=[oblk, oblk],
        out_shape=[jax.ShapeDtypeStruct((t, N_HEADS * LANES), BF16)] * 2,
        scratch_shapes=[pltpu.VMEM((8, LANES), F32)],
        compiler_params=_params("arbitrary", "arbitrary"),
    )(q, k, f, b_forget, q_gain, k_gain, seg, tri, _place_pieces(AUX_F))


def _attn_prep_bwd(dqp, dkp, q, k, f, b_forget, q_gain, k_gain, bsz, seq):
    ts = ATTN_PREP_ROWS
    ns = seq // ts
    aw = ATTN_WIDTH
    t = bsz * seq
    seg = _seg_ones(aw, HEAD_DIM)
    tri = _tri_ones(ts, False)

    def body(dqp_ref, dkp_ref, q_ref, k_ref, f_ref, bf_ref, gq_ref, gk_ref, seg_ref, tri_ref,
             dq_ref, dk_ref, df_ref, dgq_ref, dgk_ref, dbf_ref, carry):
        b = pl.program_id(0)
        sr = pl.program_id(1)

        @pl.when(jnp.logical_and(b == 0, sr == 0))
        def _():
            dgq_ref[...] = jnp.zeros_like(dgq_ref)
            dgk_ref[...] = jnp.zeros_like(dgk_ref)
            dbf_ref[...] = jnp.zeros_like(dbf_ref)

        @pl.when(sr == 0)
        def _():
            carry[...] = jnp.zeros_like(carry)

        lane = lax.broadcasted_iota(jnp.int32, (1, LANES), 1)
        seg_v = seg_ref[...]

        def norm_bwd(dp_ref, x_ref, g_ref, scale, dx_ref, dgain_ref):
            parts = []
            for j in range(N_HEADS // 2):
                even = dp_ref[:, (2 * j) * LANES:(2 * j + 1) * LANES]
                odd = dp_ref[:, (2 * j + 1) * LANES:(2 * j + 2) * LANES]
                parts.append(jnp.where(lane < HEAD_DIM, even, pltpu.roll(odd, HEAD_DIM, 1)))
            dxn = jnp.concatenate(parts, axis=1) * scale
            xv = x_ref[...]
            r = lax.rsqrt(_head_sums(xv * xv, seg_v) * (1.0 / HEAD_DIM) + EPS)
            n = xv * r
            dgain_ref[...] += jnp.sum(dxn * n, axis=0, keepdims=True)
            dn = dxn * g_ref[...]
            m = _head_sums(dn * n, seg_v) * (1.0 / HEAD_DIM)
            dx_ref[...] = (r * (dn - n * m)).astype(BF16)

        norm_bwd(dqp_ref, q_ref, gq_ref, 0.125, dq_ref, dgq_ref)
        norm_bwd(dkp_ref, k_ref, gk_ref, 1.0, dk_ref, dgk_ref)

        dfc = jnp.zeros((ts, LANES), F32)
        for h in range(N_HEADS):
            cols = pl.ds(h * LANES, LANES)
            both = jnp.where(lane == AUX_F, dqp_ref[:, cols], 0.0) - jnp.where(lane == AUX_ONE, dkp_ref[:, cols], 0.0)
            dfc = jnp.where(lane == h, jnp.sum(both, axis=1, keepdims=True), dfc)
        hi, mid, lo = _split3(dfc)
        tri_v = tri_ref[...]
        dlogf = _dot(tri_v, hi) + _dot(tri_v, mid) + _dot(tri_v, lo) + carry[pl.ds(0, 1), :]
        carry[pl.ds(0, 1), :] = dlogf[0:1, :]
        df = jnp.where(lane < N_HEADS, dlogf * jax.nn.sigmoid(-(f_ref[...] + bf_ref[...])), 0.0)
        df_ref[...] = df.astype(BF16)
        dbf_ref[...] += jnp.sum(df, axis=0, keepdims=True)

    rev = lambda b, s: (b * ns + (ns - 1 - s), 0)
    blk = pl.BlockSpec((ts, aw), rev)
    fblk = pl.BlockSpec((ts, LANES), rev)
    pblk = pl.BlockSpec((ts, N_HEADS * LANES), rev)
    const = lambda shape: pl.BlockSpec(shape, lambda b, s: (0, 0))
    return pl.pallas_call(
        body, name="attn_prep_bwd", grid=(bsz, ns),
        in_specs=[pblk, pblk, blk, blk, fblk, const((1, LANES)), const((1, aw)), const((1, aw)), const((aw, aw)),
                  const((ts, ts))],
        out_specs=[blk, blk, fblk, const((1, aw)), const((1, aw)), const((1, LANES))],
        out_shape=[jax.ShapeDtypeStruct((t, aw), BF16), jax.ShapeDtypeStruct((t, aw), BF16),
                   jax.ShapeDtypeStruct((t, LANES), BF16), jax.ShapeDtypeStruct((1, aw), F32),
                   jax.ShapeDtypeStruct((1, aw), F32), jax.ShapeDtypeStruct((1, LANES), F32)],
        scratch_shapes=[pltpu.VMEM((8, LANES), F32)],
        compiler_params=_params("arbitrary", "arbitrary"),
    )(dqp, dkp, q, k, f, b_forget, q_gain, k_gain, seg, tri)


ATTN_BLOCK = 1024
HEAD_PAIRS = N_HEADS // 2


def _flash_fwd(qp, kp, v, bsz, seq):
    tq = ATTN_BLOCK
    half = tq // 2
    nq = seq // tq
    t = bsz * seq

    def body(q_ref, k_ref, v_ref, o_ref, lse_ref, m_sc, l_sc, acc_sc):
        i = pl.program_id(2)
        m_sc[...] = jnp.full(m_sc.shape, -jnp.inf, F32)
        l_sc[...] = jnp.zeros_like(l_sc)
        acc_sc[...] = jnp.zeros_like(acc_sc)
        lane = lax.broadcasted_iota(jnp.int32, (1, LANES), 1)
        low = lane < HEAD_DIM

        def tile(q0, qn, k_start, kn, k0=None):
            qs = pl.ds(q0, qn)
            ks = pl.ds(k_start, kn)
            vv = v_ref[ks, :]
            for h in range(2):
                mine = low if h == 0 else jnp.logical_not(low)
                cols = pl.ds(h * LANES, LANES)
                s = _dot_nt(q_ref[qs, cols], k_ref[ks, cols])
                if k0 is not None:
                    row = lax.broadcasted_iota(jnp.int32, (qn, kn), 0) + q0
                    col = lax.broadcasted_iota(jnp.int32, (qn, kn), 1) + k0
                    s = jnp.where(row >= col, s, -jnp.inf)
                m_prev = m_sc[h, qs, :]
                m_new = jnp.maximum(m_prev, jnp.max(s, axis=1, keepdims=True))
                p = jnp.exp(s - jnp.tile(m_new, (1, kn // LANES)))
                alpha = jnp.exp(m_prev - m_new)
                l_sc[h, qs, :] = alpha * l_sc[h, qs, :] + jnp.sum(p, axis=1, keepdims=True)
                m_sc[h, qs, :] = m_new
                pv = _dot(p.astype(BF16), jnp.where(mine, vv, jnp.zeros_like(vv)))
                acc_sc[qs, :] = acc_sc[qs, :] * jnp.where(mine, alpha, 1.0) + pv

        def below_diagonal(j, carry):
            tile(0, tq, pl.multiple_of(j * tq, tq), tq)
            return carry

        lax.fori_loop(0, i, below_diagonal, 0)
        diagonal = pl.multiple_of(i * tq, tq)
        tile(0, tq, diagonal, half, k0=0)
        tile(half, half, diagonal + half, half, k0=half)
        l = jnp.where(low, l_sc[0], l_sc[1])
        m = jnp.where(low, m_sc[0], m_sc[1])
        o_ref[...] = acc_sc[...] / l
        lse_ref[...] = m + jnp.log(l)

    qspec = pl.BlockSpec((tq, 2 * LANES), lambda b, hp, i: (b * nq + i, hp))
    kspec = pl.BlockSpec((seq, 2 * LANES), lambda b, hp, i: (b, hp))
    vspec = pl.BlockSpec((seq, LANES), lambda b, hp, i: (b, hp))
    ospec = pl.BlockSpec((tq, LANES), lambda b, hp, i: (b * nq + i, hp))
    return pl.pallas_call(
        body, name="flash_fwd", grid=(bsz, HEAD_PAIRS, nq),
        in_specs=[qspec, kspec, vspec], out_specs=[ospec, ospec],
        out_shape=[jax.ShapeDtypeStruct((t, ATTN_WIDTH), F32), jax.ShapeDtypeStruct((t, ATTN_WIDTH), F32)],
        scratch_shapes=[pltpu.VMEM((2, tq, LANES), F32), pltpu.VMEM((2, tq, LANES), F32), pltpu.VMEM((tq, LANES), F32)],
        compiler_params=_params("arbitrary", "arbitrary", "arbitrary"),
    )(qp, kp, v)


def _flash_bwd(qp, kp, v, o, do, lse, after, bsz, seq):
    tq = ATTN_BLOCK
    half = tq // 2
    nq = seq // tq
    t = bsz * seq

    def body(q_ref, k_ref, v_ref, o_ref, do_ref, lse_ref, after_ref, dq_ref, dk_ref, dv_ref, dk_acc, dv_acc):
        j = pl.program_id(2)

        @pl.when(j == 0)
        def _():
            dq_ref[...] = jnp.zeros_like(dq_ref)

        dk_acc[...] = jnp.zeros_like(dk_acc)
        dv_acc[...] = jnp.zeros_like(dv_acc)
        lane = lax.broadcasted_iota(jnp.int32, (1, LANES), 1)
        low = lane < HEAD_DIM

        def tile(q_start, qn, k0, kn, q0=None):
            rows = pl.ds(q_start, qn)
            ks = pl.ds(k0, kn)
            dov = do_ref[rows, :]
            dd = dov * o_ref[rows, :]
            dob = dov.astype(BF16)
            vv = v_ref[ks, :]
            lse_v = lse_ref[rows, :]
            for h in range(2):
                mine = low if h == 0 else jnp.logical_not(low)
                cols = pl.ds(h * LANES, LANES)
                qh = q_ref[rows, cols]
                kh = k_ref[ks, cols]
                s = _dot_nt(qh, kh)
                lse_h = jnp.where(mine, lse_v, pltpu.roll(lse_v, HEAD_DIM, 1))
                p = jnp.exp(s - jnp.tile(lse_h, (1, kn // LANES)))
                if q0 is not None:
                    row = lax.broadcasted_iota(jnp.int32, (qn, kn), 0) + q0
                    col = lax.broadcasted_iota(jnp.int32, (qn, kn), 1) + k0
                    p = jnp.where(row >= col, p, 0.0)
                delta = jnp.sum(jnp.where(mine, dd, 0.0), axis=1, keepdims=True)
                dp = _dot_nt(dob, jnp.where(mine, vv, jnp.zeros_like(vv)))
                ds = (p * (dp - delta)).astype(BF16)
                dv_acc[ks, :] += jnp.where(mine, _dot_tn(p.astype(BF16), dob), 0.0)
                dk_acc[ks, cols] += _dot_tn(ds, qh)
                dq_ref[rows, cols] += _dot(ds, kh)

        def above_diagonal(i, carry):
            tile(pl.multiple_of(i * tq, tq), tq, 0, tq)
            return carry

        diagonal = pl.multiple_of(j * tq, tq)
        tile(diagonal, tq, 0, half, q0=0)
        tile(diagonal + half, half, half, half, q0=half)
        lax.fori_loop(j + 1, nq, above_diagonal, 0)
        dk_ref[...] = dk_acc[...]
        dv_ref[...] = dv_acc[...].astype(BF16)

    qspec = pl.BlockSpec((seq, 2 * LANES), lambda b, hp, j: (b, hp))
    kspec = pl.BlockSpec((tq, 2 * LANES), lambda b, hp, j: (b * nq + j, hp))
    vspec = pl.BlockSpec((tq, LANES), lambda b, hp, j: (b * nq + j, hp))
    ospec = pl.BlockSpec((seq, LANES), lambda b, hp, j: (b, hp))
    return pl.pallas_call(
        body, name="flash_bwd", grid=(bsz, HEAD_PAIRS, nq),
        in_specs=[qspec, kspec, vspec, ospec, ospec, ospec, ORDER_ONLY], out_specs=[qspec, kspec, vspec],
        out_shape=[jax.ShapeDtypeStruct((t, N_HEADS * LANES), F32), jax.ShapeDtypeStruct((t, N_HEADS * LANES), F32),
                   jax.ShapeDtypeStruct((t, ATTN_WIDTH), BF16)],
        scratch_shapes=[pltpu.VMEM((tq, 2 * LANES), F32), pltpu.VMEM((tq, LANES), F32)],
        compiler_params=_params("arbitrary", "arbitrary", "arbitrary"),
    )(qp, kp, v, o, do, lse, after)


def _mix_out_fwd(o, y_pool, x, gain, w_out):
    t, d = x.shape
    tm = 1024
    pw, aw = POOL_WIDTH, ATTN_WIDTH

    def body(o_ref, yp_ref, x_ref, g_ref, w_ref, ycat_ref, y_ref):
        ov = o_ref[...]
        ya = (ov * _rms_scale(ov) * g_ref[...]).astype(BF16)
        ycat = jnp.concatenate([yp_ref[...], ya], axis=1)
        ycat_ref[...] = ycat
        y_ref[...] = x_ref[...] + _dot(ycat, w_ref[...])

    return pl.pallas_call(
        body, name="mix_out_fwd", grid=(t // tm,),
        in_specs=[_rows(tm, aw), _rows(tm, pw), _rows(tm, d), _resident((1, aw)), _resident((pw + aw, d))],
        out_specs=[_rows(tm, pw + aw), _rows(tm, d)],
        out_shape=[jax.ShapeDtypeStruct((t, pw + aw), BF16), jax.ShapeDtypeStruct((t, d), F32)],
        compiler_params=_params("arbitrary"),
    )(o, y_pool, x, gain, w_out)


def _mix_out_bwd(dx, o, ycat, gain, w_out):
    t, d = dx.shape
    tm = 1024
    nm = t // tm
    pw, aw = POOL_WIDTH, ATTN_WIDTH

    def body(dx_ref, o_ref, ycat_ref, g_ref, w_ref, dw_ref, dyp_ref, do_ref, dg_ref, acc):
        i = pl.program_id(0)

        @pl.when(i == 0)
        def _():
            dg_ref[...] = jnp.zeros_like(dg_ref)
            acc[...] = jnp.zeros_like(acc)

        dxb = dx_ref[...].astype(BF16)
        acc[...] += _dot_tn(ycat_ref[...], dxb)
        dyp_ref[...] = _dot_nt(dxb, w_ref[pl.ds(0, pw), :])
        dya = _dot_nt(dxb, w_ref[pl.ds(pw, aw), :])
        do, dgain = _rms_bwd(dya, o_ref[...], g_ref[...])
        do_ref[...] = do
        dg_ref[...] += dgain

        @pl.when(i == nm - 1)
        def _():
            dw_ref[...] = acc[...].astype(BF16)

    return pl.pallas_call(
        body, name="mix_out_bwd", grid=(nm,),
        in_specs=[_rows(tm, d), _rows(tm, aw), _rows(tm, pw + aw), _resident((1, aw)), _resident((pw + aw, d))],
        out_specs=[pl.BlockSpec((pw + aw, d), lambda i: (0, 0)), _rows(tm, pw), _rows(tm, aw),
                   pl.BlockSpec((1, aw), lambda i: (0, 0))],
        out_shape=[jax.ShapeDtypeStruct((pw + aw, d), BF16), jax.ShapeDtypeStruct((t, pw), F32),
                   jax.ShapeDtypeStruct((t, aw), F32), jax.ShapeDtypeStruct((1, aw), F32)],
        scratch_shapes=[pltpu.VMEM((pw + aw, d), F32)],
        compiler_params=_params("arbitrary"),
    )(dx, o, ycat, gain, w_out)


def _mix_in_bwd(dpv, dq, dk, dv, df, hm, x, dx_res, gain, w_in_t):
    t, d = x.shape
    tm = 512
    nm = t // tm
    pw, aw = POOL_WIDTH, ATTN_WIDTH

    def body(dpv_ref, dq_ref, dk_ref, dv_ref, df_ref, hm_ref, x_ref, dxr_ref, g_ref, w_ref, dw_ref, dx_ref, dxh_ref,
             dg_ref, acc):
        i = pl.program_id(0)

        @pl.when(i == 0)
        def _():
            dg_ref[...] = jnp.zeros_like(dg_ref)
            acc[...] = jnp.zeros_like(acc)

        dh = jnp.concatenate([dpv_ref[...], dq_ref[...], dk_ref[...], dv_ref[...], df_ref[...]], axis=1)
        acc[...] += _dot_tn(dh, hm_ref[...])
        dx, dgain = _rms_bwd(_dot(dh, w_ref[...]), x_ref[...], g_ref[...])
        dx = dxr_ref[...] + dx
        dx_ref[...] = dx
        dxh_ref[...] = (0.5 * dx).astype(BF16)
        dg_ref[...] += dgain

        @pl.when(i == nm - 1)
        def _():
            dw_ref[...] = acc[...].astype(BF16)

    return pl.pallas_call(
        body, name="mix_in_bwd", grid=(nm,),
        in_specs=[_rows(tm, pw), _rows(tm, aw), _rows(tm, aw), _rows(tm, aw), _rows(tm, LANES), _rows(tm, d),
                  _rows(tm, d), _rows(tm, d), _resident((1, d)), _resident((MIX_PAD, d))],
        out_specs=[pl.BlockSpec((MIX_PAD, d), lambda i: (0, 0)), _rows(tm, d), _rows(tm, d),
                   pl.BlockSpec((1, d), lambda i: (0, 0))],
        out_shape=[jax.ShapeDtypeStruct((MIX_PAD, d), BF16), jax.ShapeDtypeStruct((t, d), F32),
                   jax.ShapeDtypeStruct((t, d), BF16), jax.ShapeDtypeStruct((1, d), F32)],
        scratch_shapes=[pltpu.VMEM((MIX_PAD, d), F32)],
        compiler_params=_params("arbitrary"),
    )(dpv, dq, dk, dv, df, hm, x, dx_res, gain, w_in_t)


MESH_IDS = pl.DeviceIdType.MESH


def _me():
    return lax.axis_index("x"), lax.axis_index("y"), lax.axis_index("c")


def _peer(x, y, c, p):
    px = 1 - x if p & 4 else x
    py = 1 - y if p & 2 else y
    pc = 1 - c if p & 1 else c
    return (px, py, pc), 4 * px + 2 * py + pc


HBM_SPEC = pl.BlockSpec(memory_space=pltpu.HBM)
SEM_SPEC = pl.BlockSpec(memory_space=pltpu.SEMAPHORE)
SPLIT_COPY = pltpu.CompilerParams(has_side_effects=pltpu.SideEffectType.DATAFLOW_SIDE_EFFECTING)
PEERS = N_DEV - 1


def _hbm(a):
    return pltpu.with_memory_space_constraint(a, pltpu.HBM)


def _row_block(ref, dev, rows):
    return ref.at[pl.ds(pl.multiple_of(dev * rows, BF16_ROWS), rows)]


def _copy_ends(gather, src, land, me, peer_id):
    if gather:
        rows = src.shape[0]
        return src, _row_block(land, me, rows), _row_block(land, peer_id, rows), src, _row_block(land, me, rows)
    rows = src.shape[0] // N_DEV
    return (_row_block(src, peer_id, rows), land.at[me], land.at[peer_id], _row_block(src, me, rows), land.at[me])


def _land_shape(gather, s):
    return (N_DEV * s.shape[0], s.shape[1]) if gather else (N_DEV, s.shape[0] // N_DEV, s.shape[1])


SIBLING = 1
SAME_CORE_PEERS = (2, 4, 6)
RELAYS = len(SAME_CORE_PEERS)


def _copies_start(groups, gather, name, after=None, relayed=()):
    flat = [s for g in groups for s in g]
    n, ng = len(flat), len(groups)
    lands = [lax.empty(_land_shape(gather, s), s.dtype) for s in flat]
    n_in = 2 * n + (after is not None)

    def body(*refs):
        ins, lnd = refs[:n], refs[n:2 * n]
        sems = refs[n_in:n_in + 2 * ng]
        token = refs[-1]
        x, y, c = _me()
        me = 4 * x + 2 * y + c
        w = 0
        for gi, g in enumerate(groups):
            for k in range(len(g)):
                for p in ((SIBLING,) + SAME_CORE_PEERS if gi in relayed else range(1, N_DEV)):
                    peer, peer_id = _peer(x, y, c, p)
                    src, dst, _, _, _ = _copy_ends(gather, ins[w], lnd[w], me, peer_id)
                    pltpu.make_async_remote_copy(src, dst, sems[2 * gi].at[k * PEERS + p - 1],
                                                 sems[2 * gi + 1].at[k * PEERS + p - 1], device_id=peer,
                                                 device_id_type=MESH_IDS).start()
                w += 1
        token[...] = jnp.zeros_like(token)

    sem_shapes = []
    for g in groups:
        sem_shapes += [pltpu.SemaphoreType.DMA((len(g) * PEERS,))] * 2
    out = pl.pallas_call(
        body, name=name,
        out_shape=(*sem_shapes, *[pltpu.HBM(s.shape, s.dtype) for s in flat],
                   *[pltpu.HBM(l.shape, l.dtype) for l in lands], jax.ShapeDtypeStruct((8, LANES), F32)),
        in_specs=[HBM_SPEC] * (2 * n) + [pl.BlockSpec(memory_space=pl.ANY)] * (after is not None),
        out_specs=(*[SEM_SPEC] * (2 * ng), *[HBM_SPEC] * (2 * n), pl.BlockSpec(memory_space=pltpu.VMEM)),
        input_output_aliases={i: 2 * ng + i for i in range(2 * n)},
        compiler_params=SPLIT_COPY,
    )(*[_hbm(s) for s in flat], *[_hbm(l) for l in lands], *([after] if after is not None else []))
    sems, thru, token = out[:2 * ng], out[2 * ng:2 * ng + 2 * n], out[-1]
    res, w = [], 0
    for gi, g in enumerate(groups):
        res.append((sems[2 * gi], sems[2 * gi + 1], list(thru[w:w + len(g)]), list(thru[n + w:n + w + len(g)])))
        w += len(g)
    return res, token


def _copies_wait(started, gather, after, name):
    send, recv, srcs, lands = started
    n = len(srcs)
    after = list(after) if isinstance(after, (list, tuple)) else [after]

    own_shapes = [s.shape if gather else (s.shape[0] // N_DEV, s.shape[1]) for s in srcs]

    def body(*refs):
        ins, lnd = refs[:n], refs[n:2 * n]
        send_sems, recv_sems = refs[2 * n], refs[2 * n + 1]
        bounce, in_sems, out_sems = refs[-n - 2:-2], refs[-2], refs[-1]
        x, y, c = _me()
        me = 4 * x + 2 * y + c
        ends = [_copy_ends(gather, ins[w], lnd[w], me, me)[3:] for w in range(n)]
        loads = [pltpu.make_async_copy(ends[w][0], bounce[w], in_sems.at[w]) for w in range(n)]
        stores = [pltpu.make_async_copy(bounce[w], ends[w][1], out_sems.at[w]) for w in range(n)]
        for cp in loads:
            cp.start()
        for w in range(n):
            loads[w].wait()
            stores[w].start()
        for w in range(n):
            for p in range(1, N_DEV):
                peer, peer_id = _peer(x, y, c, p)
                src, _, arrival, _, _ = _copy_ends(gather, ins[w], lnd[w], me, peer_id)
                cp = pltpu.make_async_remote_copy(src, arrival, send_sems.at[w * PEERS + p - 1],
                                                  recv_sems.at[w * PEERS + p - 1], device_id=peer,
                                                  device_id_type=MESH_IDS)
                cp.wait_send()
                cp.wait_recv()
        for cp in stores:
            cp.wait()

    out = pl.pallas_call(
        body, name=name,
        out_shape=(*[pltpu.HBM(s.shape, s.dtype) for s in srcs], *[pltpu.HBM(l.shape, l.dtype) for l in lands]),
        in_specs=[HBM_SPEC] * (2 * n) + [SEM_SPEC, SEM_SPEC] + [pl.BlockSpec(memory_space=pl.ANY)] * len(after),
        out_specs=[HBM_SPEC] * (2 * n),
        input_output_aliases={i: i for i in range(2 * n)},
        scratch_shapes=[*[pltpu.VMEM(shape, s.dtype) for shape, s in zip(own_shapes, srcs)],
                        pltpu.SemaphoreType.DMA((n,)), pltpu.SemaphoreType.DMA((n,))],
        compiler_params=SPLIT_COPY,
    )(*srcs, *lands, send, recv, *after)
    return list(out[n:])


def _relay_to_sibling(started, name, after=None):
    send, recv, srcs, lands = started
    n = len(srcs)
    after = [] if after is None else [after]

    def body(*refs):
        ins, lnd = refs[:n], refs[n:2 * n]
        send_sems, recv_sems = refs[2 * n], refs[2 * n + 1]
        relay_send, relay_recv = refs[2 * n + 2 + len(after)], refs[2 * n + 3 + len(after)]
        x, y, c = _me()
        sibling, _ = _peer(x, y, c, SIBLING)
        for w in range(n):
            rows = ins[w].shape[0]
            for k, p in enumerate(SAME_CORE_PEERS):
                peer, peer_id = _peer(x, y, c, p)
                arrived = _row_block(lnd[w], peer_id, rows)
                first = pltpu.make_async_remote_copy(ins[w], arrived, send_sems.at[w * PEERS + p - 1],
                                                     recv_sems.at[w * PEERS + p - 1], device_id=peer,
                                                     device_id_type=MESH_IDS)
                first.wait_recv()
                pltpu.make_async_remote_copy(arrived, arrived, relay_send.at[w * RELAYS + k],
                                             relay_recv.at[w * RELAYS + k], device_id=sibling,
                                             device_id_type=MESH_IDS).start()
                first.wait_send()

    sems = pltpu.SemaphoreType.DMA((n * RELAYS,))
    out = pl.pallas_call(
        body, name=name,
        out_shape=(sems, sems, *[pltpu.HBM(s.shape, s.dtype) for s in srcs], *[pltpu.HBM(l.shape, l.dtype) for l in lands]),
        in_specs=[HBM_SPEC] * (2 * n) + [SEM_SPEC, SEM_SPEC] + [pl.BlockSpec(memory_space=pl.ANY)] * len(after),
        out_specs=(SEM_SPEC, SEM_SPEC, *[HBM_SPEC] * (2 * n)),
        input_output_aliases={i: 2 + i for i in range(2 * n)},
        compiler_params=SPLIT_COPY,
    )(*srcs, *lands, send, recv, *after)
    return send, recv, out[0], out[1], list(out[2:2 + n]), list(out[2 + n:])


def _relayed_wait(relayed, after, name):
    send, recv, relay_send, relay_recv, srcs, lands = relayed
    n = len(srcs)
    after = list(after) if isinstance(after, (list, tuple)) else [after]

    def body(*refs):
        ins, lnd = refs[:n], refs[n:2 * n]
        send_sems, recv_sems, relay_send_sems, relay_recv_sems = refs[2 * n:2 * n + 4]
        bounce, in_sems, out_sems = refs[-n - 2:-2], refs[-2], refs[-1]
        x, y, c = _me()
        me = 4 * x + 2 * y + c
        sibling, sibling_id = _peer(x, y, c, SIBLING)
        loads = [pltpu.make_async_copy(ins[w], bounce[w], in_sems.at[w]) for w in range(n)]
        stores = [pltpu.make_async_copy(bounce[w], _row_block(lnd[w], me, ins[w].shape[0]), out_sems.at[w])
                  for w in range(n)]
        for cp in loads:
            cp.start()
        for w in range(n):
            loads[w].wait()
            stores[w].start()
        for w in range(n):
            rows = ins[w].shape[0]
            direct = pltpu.make_async_remote_copy(ins[w], _row_block(lnd[w], sibling_id, rows),
                                                  send_sems.at[w * PEERS + SIBLING - 1],
                                                  recv_sems.at[w * PEERS + SIBLING - 1], device_id=sibling,
                                                  device_id_type=MESH_IDS)
            direct.wait_send()
            direct.wait_recv()
            for k, p in enumerate(SAME_CORE_PEERS):
                _, sent_id = _peer(x, y, c, p)
                _, got_id = _peer(x, y, c, p + SIBLING)
                relay = pltpu.make_async_remote_copy(_row_block(lnd[w], sent_id, rows), _row_block(lnd[w], got_id, rows),
                                                     relay_send_sems.at[w * RELAYS + k],
                                                     relay_recv_sems.at[w * RELAYS + k], device_id=sibling,
                                                     device_id_type=MESH_IDS)
                relay.wait_send()
                relay.wait_recv()
        for cp in stores:
            cp.wait()

    out = pl.pallas_call(
        body, name=name,
        out_shape=(*[pltpu.HBM(s.shape, s.dtype) for s in srcs], *[pltpu.HBM(l.shape, l.dtype) for l in lands]),
        in_specs=[HBM_SPEC] * (2 * n) + [SEM_SPEC] * 4 + [pl.BlockSpec(memory_space=pl.ANY)] * len(after),
        out_specs=[HBM_SPEC] * (2 * n),
        input_output_aliases={i: i for i in range(2 * n)},
        scratch_shapes=[*[pltpu.VMEM(s.shape, s.dtype) for s in srcs],
                        pltpu.SemaphoreType.DMA((n,)), pltpu.SemaphoreType.DMA((n,))],
        compiler_params=SPLIT_COPY,
    )(*srcs, *lands, send, recv, relay_send, relay_recv, *after)
    return list(out[n:])


def _adamw_update(w, g, m, v):
    nm = ADAM_B1 * m + (1.0 - ADAM_B1) * g
    nv = ADAM_B2 * v + (1.0 - ADAM_B2) * (g * g)
    m_hat = nm / (1.0 - ADAM_B1 ** ADAM_STEP)
    v_hat = nv / (1.0 - ADAM_B2 ** ADAM_STEP)
    return -ADAM_LR * (m_hat / (jnp.sqrt(v_hat) + ADAM_EPS) + ADAM_WD * w), nm, nv


SUM_ADAMW_COLS = 512


def _sum_adamw(parts, w, m, v, name):
    _, rows, d = parts.shape
    n = w.shape[0]
    tc = SUM_ADAMW_COLS

    def body(p_ref, w_ref, m_ref, v_ref, g_ref, d_ref, nm_ref, nv_ref):
        g = p_ref[0].astype(F32)
        for dev in range(1, N_DEV):
            g = g + p_ref[dev].astype(F32)
        g = g[:n]
        g_ref[...] = g
        d_ref[...], nm_ref[...], nv_ref[...] = _adamw_update(w_ref[...], g, m_ref[...], v_ref[...])

    spec = pl.BlockSpec((n, tc), lambda j: (0, j))
    shape = jax.ShapeDtypeStruct((n, d), F32)
    return pl.pallas_call(
        body, name=name, grid=(d // tc,),
        in_specs=[pl.BlockSpec((N_DEV, rows, tc), lambda j: (0, 0, j)), spec, spec, spec],
        out_specs=[spec] * 4, out_shape=[shape] * 4,
        compiler_params=_params("arbitrary"),
    )(parts, w, m, v)


def _pad_rows(a, rows):
    return jnp.pad(a, ((0, rows - a.shape[0]), (0, 0)))


def _row1(vec, width=D_MODEL):
    return jnp.pad(vec.reshape(1, -1), ((0, 0), (0, width - vec.shape[-1])))


COLUMN_SHARDED = ("ffn1_w_gate", "ffn1_w_up", "w_in", "ffn2_w_gate", "ffn2_w_up")
VEC_NAMES = ("ffn1_norm", "mix_norm", "ffn2_norm", "b_forget", "pool_scale", "q_norm", "k_norm", "out_norm_pool",
             "out_norm_attn")
VEC_ROWS = 16
LOSS_ROW = len(VEC_NAMES)


def _pack_vector_grads(parts, loss_part, name):
    names = [n for n in VEC_NAMES if n in parts]
    extra = [] if loss_part is None else [loss_part]

    def body(*refs):
        out_ref = refs[-1]
        out_ref[...] = jnp.zeros_like(out_ref)
        lane = lax.broadcasted_iota(jnp.int32, (1, LANES), 1)
        for n, ref in zip(names, refs):
            val = ref[...]
            if n in ("q_norm", "k_norm"):
                val = val[:, 0:LANES] + val[:, LANES:2 * LANES] + val[:, 2 * LANES:3 * LANES] + val[:, 3 * LANES:]
                val = jnp.where(lane < HEAD_DIM, val + pltpu.roll(val, HEAD_DIM, 1), 0.0)
            out_ref[pl.ds(VEC_NAMES.index(n), 1), pl.ds(0, val.shape[1])] = val
        if extra:
            out_ref[pl.ds(LOSS_ROW, 1), pl.ds(0, 1)] = refs[len(names)][...]

    vmem = pl.BlockSpec(memory_space=pltpu.VMEM)
    return pl.pallas_call(
        body, name=name, in_specs=[vmem] * (len(names) + len(extra)), out_specs=vmem,
        out_shape=jax.ShapeDtypeStruct((VEC_ROWS, D_MODEL), F32),
    )(*[parts[n] for n in names], *extra)


def _small_adamw(vec_all, pool_all, vec_params, pool_params):
    nv = len(vec_params)
    pool_rows = pool_params[0].shape[0]

    def body(*refs):
        vec_ref, pool_ref = refs[0], refs[1]
        ins = refs[2:2 + 3 * nv + 3]
        outs = refs[2 + 3 * nv + 3:-1]
        rows = refs[-1]
        total = vec_ref[pl.ds(0, VEC_ROWS), :]
        for dev in range(1, N_DEV):
            total = total + vec_ref[pl.ds(dev * VEC_ROWS, VEC_ROWS), :]
        rows[...] = total
        outs[4 * nv + 4][...] = rows[pl.ds(LOSS_ROW, 1), pl.ds(0, 1)]
        for i in range(nv):
            w_ref, m_ref, v_ref = ins[3 * i:3 * i + 3]
            g = rows[pl.ds(i, 1), pl.ds(0, w_ref.shape[1])]
            outs[4 * i][...] = g
            outs[4 * i + 1][...], outs[4 * i + 2][...], outs[4 * i + 3][...] = _adamw_update(
                w_ref[...], g, m_ref[...], v_ref[...])
        g = pool_ref[pl.ds(0, pool_rows), :].astype(F32)
        for dev in range(1, N_DEV):
            g = g + pool_ref[pl.ds(dev * pool_rows, pool_rows), :].astype(F32)
        w_ref, m_ref, v_ref = ins[3 * nv:]
        outs[4 * nv][...] = g
        outs[4 * nv + 1][...], outs[4 * nv + 2][...], outs[4 * nv + 3][...] = _adamw_update(
            w_ref[...], g, m_ref[...], v_ref[...])

    vmem = pl.BlockSpec(memory_space=pltpu.VMEM)
    flat = [a for trio in vec_params for a in trio] + list(pool_params)
    out_shape = []
    for trio in list(vec_params) + [pool_params]:
        out_shape += [jax.ShapeDtypeStruct(trio[0].shape, F32)] * 4
    out_shape.append(jax.ShapeDtypeStruct((1, 1), F32))
    return pl.pallas_call(
        body, name="adamw_small", in_specs=[vmem] * (2 + len(flat)), out_specs=[vmem] * len(out_shape),
        out_shape=out_shape, scratch_shapes=[pltpu.VMEM((VEC_ROWS, D_MODEL), F32)],
    )(vec_all, pool_all, *flat)


def kernel(x, ffn1_norm, ffn1_w_gate, ffn1_w_up, ffn1_w_down, mix_norm, w_in, b_forget, pool_w, pool_scale, q_norm, k_norm, out_norm_pool, out_norm_attn, w_out, ffn2_norm, ffn2_w_gate, ffn2_w_up, ffn2_w_down, loss_target, m_ffn1_norm, m_ffn1_w_gate, m_ffn1_w_up, m_ffn1_w_down, m_mix_norm, m_w_in, m_b_forget, m_pool_w, m_pool_scale, m_q_norm, m_k_norm, m_out_norm_pool, m_out_norm_attn, m_w_out, m_ffn2_norm, m_ffn2_w_gate, m_ffn2_w_up, m_ffn2_w_down, v_ffn1_norm, v_ffn1_w_gate, v_ffn1_w_up, v_ffn1_w_down, v_mix_norm, v_w_in, v_b_forget, v_pool_w, v_pool_scale, v_q_norm, v_k_norm, v_out_norm_pool, v_out_norm_attn, v_w_out, v_ffn2_norm, v_ffn2_w_gate, v_ffn2_w_up, v_ffn2_w_down):
    bsz, seq, d = x.shape
    t = bsz * seq
    x0 = x.reshape(t, d)
    target = loss_target.reshape(t, d)
    in_rows = -(-w_in.shape[1] // BF16_ROWS) * BF16_ROWS

    slabs = [s.astype(BF16) for s in (ffn1_w_gate.T, ffn1_w_up.T, ffn1_w_down, _pad_rows(w_in.T, in_rows), w_out,
                                       ffn2_w_gate.T, ffn2_w_up.T, ffn2_w_down)]
    first, started = _copies_start([slabs[0:2]], True, "gather_start_first", relayed=(0,))
    rest, started = _copies_start([slabs[2:3], slabs[3:4], slabs[4:5], slabs[5:8]], True, "gather_start", after=started,
                                  relayed=(0, 3))
    gathers = first + rest

    g1, gm, g2 = ffn1_norm.reshape(1, d), mix_norm.reshape(1, d), ffn2_norm.reshape(1, d)
    bf_row = _row1(b_forget, LANES)
    gq = jnp.tile(q_norm, N_HEADS).reshape(1, ATTN_WIDTH)
    gk = jnp.tile(k_norm, N_HEADS).reshape(1, ATTN_WIDTH)
    scale_row = pool_scale.reshape(1, POOL_WIDTH)
    gp, ga = out_norm_pool.reshape(1, POOL_WIDTH), out_norm_attn.reshape(1, ATTN_WIDTH)

    wg1, wu1 = _relayed_wait(_relay_to_sibling(gathers[0], "gather_relay_ffn1_up", started), started,
                             "gather_wait_ffn1_up")
    h1, sa1, sb1, s1 = _ffn_up(x0, g1, wg1, wu1, "ffn1_up")
    (wd1,) = _relayed_wait(_relay_to_sibling(gathers[1], "gather_relay_ffn1_down", h1), s1, "gather_wait_ffn1_down")
    (x1,) = _ffn_down(s1, wd1, x0, None, "ffn1_down")
    (win_g,) = _copies_wait(gathers[2], True, x1, "gather_wait_w_in")
    win_t = _repack_rows(win_g, in_rows, w_in.shape[1], N_DEV, "w_in_rows")
    hm, pv, q, k, v, f = _mix_in_fwd(x1, gm, win_t)
    pooled, mixed, y_pool = _pool_fwd(pv, pool_w, scale_row, gp, bsz, seq)
    qp, kp = _attn_prep_fwd(q, k, f, bf_row, gq, gk, bsz, seq)
    o, lse = _flash_fwd(qp, kp, v, bsz, seq)
    relayed_ffn2 = _relay_to_sibling(gathers[4], "gather_relay_ffn2", o)
    (wout,) = _copies_wait(gathers[3], True, [o, relayed_ffn2[4][0]], "gather_wait_w_out")
    ycat, x2 = _mix_out_fwd(o, y_pool, x1, ga, wout)
    wg2, wu2, wd2 = _relayed_wait(relayed_ffn2, x2, "gather_wait_ffn2")
    h2, sa2, sb2, s2 = _ffn_up(x2, g2, wg2, wu2, "ffn2_up")
    dx3, dyh2, loss_part = _ffn_down(s2, wd2, x2, target, "ffn2_down")

    da2, db2, dwg2, dwu2 = _ffn_bwd_act(dyh2, sa2, sb2, h2, wd2, dx3, "ffn2_bwd_act")
    (dwd2,) = _wgrad([s2], dyh2, da2, "ffn2_down_wgrad")
    (sent_ffn2,), tok = _copies_start([[dwg2, dwu2, dwd2]], False, "exchange_start_ffn2")
    dx2, dg2 = _ffn_bwd_dx(da2, db2, dx3, x2, g2, wg2, wu2, tok, "ffn2_bwd_dx")
    dwout, dy_pool, do, dga = _mix_out_bwd(dx2, o, ycat, ga, wout)
    (sent_out,), tok = _copies_start([[dwout]], False, "exchange_start_w_out")
    dqp, dkp, dv = _flash_bwd(qp, kp, v, o, do, lse, tok, bsz, seq)
    dq, dk, df, dgq, dgk, dbf = _attn_prep_bwd(dqp, dkp, q, k, f, bf_row, gq, gk, bsz, seq)
    dpv, dpool_w, dscale, dgp = _pool_bwd(dy_pool, mixed, pooled, pool_w, scale_row, gp, bsz, seq)
    dwin, dx1, dyh1, dgm = _mix_in_bwd(dpv, dq, dk, dv, df, hm, x1, dx2, gm, win_t)
    dwin_blocks = _repack_rows(dwin, w_in.shape[1], in_rows, N_DEV, "w_in_grad_blocks")
    (sent_in,), tok = _copies_start([[dwin_blocks]], False, "exchange_start_w_in")
    (dwd1,) = _wgrad([s1], dyh1, tok, "ffn1_down_wgrad")
    (sent_down1,), tok = _copies_start([[dwd1]], False, "exchange_start_ffn1_down", after=tok)
    da1, db1, dwg1, dwu1 = _ffn_bwd_act(dyh1, sa1, sb1, h1, wd1, tok, "ffn1_bwd_act")
    (sent_up1,), tok = _copies_start([[dwg1, dwu1]], False, "exchange_start_ffn1_up", after=tok)
    dx0, dg1 = _ffn_bwd_dx(da1, db1, dx1, x0, g1, wg1, wu1, tok, "ffn1_bwd_dx")

    pool_rows = POOL_GROUPS * POOL_GROUP_DIM
    packed = _pack_vector_grads(dict(ffn1_norm=dg1, mix_norm=dgm, ffn2_norm=dg2, b_forget=dbf, pool_scale=dscale,
                                     q_norm=dgq, k_norm=dgk, out_norm_pool=dgp, out_norm_attn=dga), loss_part,
                                "pack_vector_grads")
    pool_part = dpool_w.reshape(pool_rows, POOL_GROUP_DIM).astype(BF16)
    (sent_small,), tok = _copies_start([[packed, pool_part]], True, "small_grads_start")

    weights = dict(ffn1_norm=ffn1_norm, ffn1_w_gate=ffn1_w_gate, ffn1_w_up=ffn1_w_up, ffn1_w_down=ffn1_w_down,
                   mix_norm=mix_norm, w_in=w_in, b_forget=b_forget, pool_w=pool_w, pool_scale=pool_scale,
                   q_norm=q_norm, k_norm=k_norm, out_norm_pool=out_norm_pool, out_norm_attn=out_norm_attn,
                   w_out=w_out, ffn2_norm=ffn2_norm, ffn2_w_gate=ffn2_w_gate, ffn2_w_up=ffn2_w_up,
                   ffn2_w_down=ffn2_w_down)
    m_in = dict(ffn1_norm=m_ffn1_norm, ffn1_w_gate=m_ffn1_w_gate, ffn1_w_up=m_ffn1_w_up, ffn1_w_down=m_ffn1_w_down,
                mix_norm=m_mix_norm, w_in=m_w_in, b_forget=m_b_forget, pool_w=m_pool_w, pool_scale=m_pool_scale,
                q_norm=m_q_norm, k_norm=m_k_norm, out_norm_pool=m_out_norm_pool, out_norm_attn=m_out_norm_attn,
                w_out=m_w_out, ffn2_norm=m_ffn2_norm, ffn2_w_gate=m_ffn2_w_gate, ffn2_w_up=m_ffn2_w_up,
                ffn2_w_down=m_ffn2_w_down)
    v_in = dict(ffn1_norm=v_ffn1_norm, ffn1_w_gate=v_ffn1_w_gate, ffn1_w_up=v_ffn1_w_up, ffn1_w_down=v_ffn1_w_down,
                mix_norm=v_mix_norm, w_in=v_w_in, b_forget=v_b_forget, pool_w=v_pool_w, pool_scale=v_pool_scale,
                q_norm=v_q_norm, k_norm=v_k_norm, out_norm_pool=v_out_norm_pool, out_norm_attn=v_out_norm_attn,
                w_out=v_w_out, ffn2_norm=v_ffn2_norm, ffn2_w_gate=v_ffn2_w_gate, ffn2_w_up=v_ffn2_w_up,
                ffn2_w_down=v_ffn2_w_down)
    grads, delta, new_m, new_v = {}, {}, {}, {}
    after = [tok]
    plan = ((sent_ffn2, "ffn2", ("ffn2_w_gate", "ffn2_w_up", "ffn2_w_down")), (sent_out, "w_out", ("w_out",)),
            (sent_in, "w_in", ("w_in",)), (sent_down1, "ffn1_down", ("ffn1_w_down",)),
            (sent_up1, "ffn1_up", ("ffn1_w_gate", "ffn1_w_up")))
    for sent, tag, names in plan:
        parts = _copies_wait(sent, False, after, f"exchange_wait_{tag}")
        after = []
        for n, part in zip(names, parts):
            turn = (lambda a: a.T) if n in COLUMN_SHARDED else (lambda a: a)
            done = _sum_adamw(part, turn(weights[n]), turn(m_in[n]), turn(v_in[n]), f"adamw_{n}")
            grads[n], delta[n], new_m[n], new_v[n] = (turn(a) for a in done)
            after.append(done[3])
    vec_all, pool_all = _copies_wait(sent_small, True, after, "small_grads_wait")
    as_row = lambda a: a.reshape(1, -1)
    as_pool = lambda a: a.reshape(pool_rows, POOL_GROUP_DIM)
    small = _small_adamw(vec_all, pool_all,
                         [tuple(as_row(z[n]) for z in (weights, m_in, v_in)) for n in VEC_NAMES],
                         tuple(as_pool(z["pool_w"]) for z in (weights, m_in, v_in)))
    for i, n in enumerate(VEC_NAMES + ("pool_w",)):
        grads[n], delta[n], new_m[n], new_v[n] = (a.reshape(weights[n].shape) for a in small[4 * i:4 * i + 4])
    loss = small[-1].reshape(())

    order = ("ffn1_norm", "ffn1_w_gate", "ffn1_w_up", "ffn1_w_down", "mix_norm", "w_in", "b_forget", "pool_w",
             "pool_scale", "q_norm", "k_norm", "out_norm_pool", "out_norm_attn", "w_out", "ffn2_norm", "ffn2_w_gate",
             "ffn2_w_up", "ffn2_w_down")
    return (loss, dx0.reshape(bsz, seq, d), *[grads[n] for n in order], *[delta[n] for n in order],
            *[new_m[n] for n in order], *[new_v[n] for n in order])
```
